```python
import math
import jax
import jax.numpy as jnp
from jax import lax
import numpy as np

D_MODEL = 1024
BATCH = 8
SEQ = 4096
DEPTH = 2

N_MIXERS = 2
EPS = 1e-6
NEG_INF = -1e30

MIX_WIDTH = D_MODEL
TOK_WIDTH = 3 * MIX_WIDTH // 4
MEM_WIDTH = MIX_WIDTH // 4

DILATED_GROUPS = ((128, 1), (512, 4), (2048, 16))
ATT_HEAD_DIM = 64
ATT_HEADS = TOK_WIDTH // ATT_HEAD_DIM
HEADS_PER_GROUP = ATT_HEADS // len(DILATED_GROUPS)
BAND_BLOCK = 64
REL_BUCKETS = 32
REL_MAX_DIST = 1024

DN_HEAD_DIM = 128
DN_HEADS = TOK_WIDTH // DN_HEAD_DIM
DN_CONV = 5
DN_CHUNK = 64

MEM_LEN = 256
MEM_HEADS = 4
MEM_HEAD_DIM = MEM_WIDTH // MEM_HEADS

_FF_RAW = -(-8 * D_MODEL // 3)
D_FF = -(-_FF_RAW // 256) * 256

ATT_IN = 3 * TOK_WIDTH + MEM_WIDTH
DN_IN = 4 * TOK_WIDTH + 4 * DN_HEADS + MEM_WIDTH

kernel_name = "hybrid_dilated_attn_gated_deltanet_encoder"


def _rms_norm(x, gain):
    xf = x.astype(jnp.float32)
    y = xf * lax.rsqrt(jnp.mean(xf * xf, axis=-1, keepdims=True) + EPS) * gain.astype(jnp.float32)
    return y.astype(x.dtype)


def _l2norm(t):
    return t * lax.rsqrt(jnp.sum(t * t, axis=-1, keepdims=True) + EPS)


def _t5_bucket(rel):
    half = REL_BUCKETS // 2
    max_exact = half // 2
    n = np.abs(rel)
    large = max_exact + (np.log(np.maximum(n, 1) / max_exact) / math.log(REL_MAX_DIST / max_exact)
                         * (half - max_exact)).astype(np.int64)
    large = np.minimum(large, half - 1)
    return ((rel > 0) * half + np.where(n < max_exact, n, large)).astype(np.int32)


def _dilated_band_attention(q, k, v, bias, dil, half):
    b, s, h, dh = q.shape
    L = s // dil
    nb = -(-L // BAND_BLOCK)
    lp = nb * BAND_BLOCK
    n = b * dil

    def to_sub(t):
        return t.reshape(b, L, dil, h, dh).transpose(0, 2, 1, 3, 4).reshape(n, L, h, dh)

    def key_windows(t):
        t = jnp.pad(to_sub(t), ((0, 0), (BAND_BLOCK, lp - L + BAND_BLOCK), (0, 0), (0, 0)))
        t = t.reshape(n, nb + 2, BAND_BLOCK, h, dh)
        return jnp.concatenate([t[:, :-2], t[:, 1:-1], t[:, 2:]], axis=2)

    qs = jnp.pad(to_sub(q), ((0, 0), (0, lp - L), (0, 0), (0, 0))).reshape(n, nb, BAND_BLOCK, h, dh)
    kw = key_windows(k)
    vw = key_windows(v)

    rel = np.arange(3 * BAND_BLOCK)[None, :] - BAND_BLOCK - np.arange(BAND_BLOCK)[:, None]
    in_band = np.abs(rel) <= half
    kpos = (np.arange(nb)[:, None] - 1) * BAND_BLOCK + np.arange(3 * BAND_BLOCK)[None, :]
    valid = in_band[None] & ((kpos >= 0) & (kpos < L))[:, None, :]
    bias_band = jnp.transpose(bias[np.clip(rel + half, 0, 2 * half)], (2, 0, 1)).astype(jnp.float32)

    logits = jnp.einsum('nbqhd,nbkhd->nbhqk', qs, kw, preferred_element_type=jnp.float32)
    logits = jnp.where(valid[None, :, None], logits + bias_band[None, None], NEG_INF)
    m = jnp.max(logits, axis=-1)
    p = jnp.exp(logits - m[..., None])
    den = jnp.sum(p, axis=-1)
    o = jnp.einsum('nbhqk,nbkhd->nbqhd', p.astype(v.dtype), vw, preferred_element_type=jnp.float32)
    o = o / jnp.swapaxes(den, -1, -2)[..., None]

    def from_sub(t):
        rest = t.shape[3:]
        t = t.reshape((b, dil, lp) + rest)[:, :, :L]
        return jnp.swapaxes(t, 1, 2).reshape((b, s) + rest)

    return from_sub(o), from_sub(jnp.swapaxes(m, -1, -2)), from_sub(jnp.swapaxes(den, -1, -2))


def _dilated_mixture(q, k, v, rel_bias):
    b, s = q.shape[:2]
    outs, lses = [], []
    for gi, (window, dil) in enumerate(DILATED_GROUPS):
        half = window // (2 * dil)
        heads = slice(gi * HEADS_PER_GROUP, (gi + 1) * HEADS_PER_GROUP)
        bias = rel_bias[_t5_bucket(np.arange(-half, half + 1) * dil)][:, heads]
        o, m, den = _dilated_band_attention(q[:, :, heads], k[:, :, heads], v[:, :, heads], bias, dil, half)
        outs.append(o)
        lses.append(m + jnp.log(den))
    wts = jax.nn.softmax(jnp.stack(lses), axis=0)
    mixed = jnp.concatenate([o * w[..., None] for o, w in zip(outs, wts)], axis=2)
    return mixed.reshape(b, s, TOK_WIDTH)


def _memory_attention(q_mem, mem_n, w_kv):
    b, s, _ = q_mem.shape
    q = q_mem.reshape(b, s, MEM_HEADS, MEM_HEAD_DIM) * (MEM_HEAD_DIM ** -0.5)
    k, v = jnp.split(mem_n @ w_kv, 2, axis=-1)
    k = k.reshape(b, -1, MEM_HEADS, MEM_HEAD_DIM)
    v = v.reshape(b, -1, MEM_HEADS, MEM_HEAD_DIM)
    logits = jnp.einsum('bshd,bmhd->bhsm', q, k, preferred_element_type=jnp.float32)
    p = jax.nn.softmax(logits, axis=-1)
    o = jnp.einsum('bhsm,bmhd->bshd', p.astype(v.dtype), v)
    return o.reshape(b, s, MEM_WIDTH)


def _gated_delta_chunked(q, k, v, g, beta):
    b, s, h, dk = q.shape
    dv = v.shape[-1]
    nc = s // DN_CHUNK

    def chunks(t):
        t = t.reshape((b, nc, DN_CHUNK, h) + t.shape[3:])
        return jnp.moveaxis(t, 3, 1)

    q, k, v, g, beta = (chunks(t) for t in (q, k, v, g, beta))
    g = jnp.cumsum(g, axis=-1)
    tril = np.tril(np.ones((DN_CHUNK, DN_CHUNK), dtype=bool))
    strict = np.tril(np.ones((DN_CHUNK, DN_CHUNK), dtype=bool), -1)
    decay = jnp.exp(jnp.where(tril, g[..., :, None] - g[..., None, :], NEG_INF))
    k_beta = k * beta[..., None]
    lmat = jnp.where(strict, jnp.einsum('bhnik,bhnjk->bhnij', k_beta, k) * decay, 0.0)
    a = lmat + jnp.eye(DN_CHUNK, dtype=jnp.float32)
    rhs = jnp.concatenate([v * beta[..., None], k_beta * jnp.exp(g)[..., None]], axis=-1)
    sol = lax.linalg.triangular_solve(a, rhs, left_side=True, lower=True, unit_diagonal=True)
    u, w = sol[..., :dv], sol[..., dv:]
    intra = jnp.where(tril, jnp.einsum('bhnik,bhnjk->bhnij', q, k) * decay, 0.0)

    def step(state, inp):
        qc, kc, uc, wc, gc, ac = inp
        v_new = uc - jnp.einsum('bhck,bhkv->bhcv', wc, state)
        out = (jnp.einsum('bhck,bhkv->bhcv', qc * jnp.exp(gc)[..., None], state)
               + jnp.einsum('bhij,bhjv->bhiv', ac, v_new))
        g_last = gc[..., -1]
        state = (state * jnp.exp(g_last)[..., None, None]
                 + jnp.einsum('bhck,bhcv->bhkv', kc * jnp.exp(g_last[..., None] - gc)[..., None], v_new))
        return state, out

    xs = tuple(jnp.moveaxis(t, 2, 0) for t in (q, k, u, w, g, intra))
    state0 = jnp.zeros((b, h, dk, dv), jnp.float32)
    _, out = lax.scan(step, state0, xs)
    out = jnp.moveaxis(out, 0, 2)
    return jnp.moveaxis(out, 1, 3).reshape(b, s, h, dv)


def _attention_sublayer(h, mem_n, w_in, w_out, rel_bias, w_mem_kv):
    b, s, _ = h.shape
    q, k, v, q_mem = jnp.split(h @ w_in, [TOK_WIDTH, 2 * TOK_WIDTH, 3 * TOK_WIDTH], axis=-1)
    q = q.reshape(b, s, ATT_HEADS, ATT_HEAD_DIM) * (ATT_HEAD_DIM ** -0.5)
    k = k.reshape(b, s, ATT_HEADS, ATT_HEAD_DIM)
    v = v.reshape(b, s, ATT_HEADS, ATT_HEAD_DIM)
    mixed = _dilated_mixture(q, k, v, rel_bias).astype(h.dtype)
    mem_out = _memory_attention(q_mem, mem_n, w_mem_kv)
    return jnp.concatenate([mixed, mem_out], axis=-1) @ w_out


def _deltanet_sublayer(h, mem_n, w_in, conv_w, a_log, dt_bias, out_norm, w_out, w_mem_kv):
    b, s, _ = h.shape
    qkv, z, gate_in, q_mem = jnp.split(
        h @ w_in, [3 * TOK_WIDTH, 4 * TOK_WIDTH, 4 * TOK_WIDTH + 4 * DN_HEADS], axis=-1)
    qkv = lax.conv_general_dilated(
        qkv, conv_w[:, None, :], window_strides=(1,), padding=[(DN_CONV // 2, DN_CONV // 2)],
        dimension_numbers=('NWC', 'WIO', 'NWC'), feature_group_count=3 * TOK_WIDTH)
    q, k, v = jnp.split(jax.nn.silu(qkv).astype(jnp.float32), 3, axis=-1)
    q = _l2norm(q.reshape(b, s, DN_HEADS, DN_HEAD_DIM)) * (DN_HEAD_DIM ** -0.5)
    k = _l2norm(k.reshape(b, s, DN_HEADS, DN_HEAD_DIM))
    v = v.reshape(b, s, DN_HEADS, DN_HEAD_DIM)
    gate_in = gate_in.astype(jnp.float32).reshape(b, s, 2, 2, DN_HEADS)
    g = -jnp.exp(a_log.astype(jnp.float32)) * jax.nn.softplus(gate_in[:, :, :, 0] + dt_bias.astype(jnp.float32))
    beta = jax.nn.sigmoid(gate_in[:, :, :, 1])
    o_fwd = _gated_delta_chunked(q, k, v, g[:, :, 0], beta[:, :, 0])
    rev = lambda t: jnp.flip(t, axis=1)
    o_bwd = rev(_gated_delta_chunked(rev(q), rev(k), rev(v), rev(g[:, :, 1]), rev(beta[:, :, 1])))
    o = o_fwd + o_bwd
    zf = z.astype(jnp.float32).reshape(b, s, DN_HEADS, DN_HEAD_DIM)
    o = (o * lax.rsqrt(jnp.mean(o * o, axis=-1, keepdims=True) + EPS)
         * out_norm.astype(jnp.float32) * jax.nn.silu(zf))
    o = o.reshape(b, s, TOK_WIDTH).astype(h.dtype)
    mem_out = _memory_attention(q_mem, mem_n, w_mem_kv)
    return jnp.concatenate([o, mem_out], axis=-1) @ w_out


def _swiglu(h, w_gate_up, w_down):
    gate, up = jnp.split(h @ w_gate_up, 2, axis=-1)
    return (jax.nn.silu(gate) * up) @ w_down


def _fwd_setup_inputs(seed: int = 0) -> dict:
    key = jax.random.key(seed)
    ks = jax.random.split(key, 24)
    n_att = (DEPTH + N_MIXERS - 1) // N_MIXERS
    n_dn = DEPTH // N_MIXERS

    def nrm(k, shape, scale):
        return jax.random.normal(k, shape, jnp.float32) * scale

    def gain(k, shape):
        return 1.0 + nrm(k, shape, 0.05)

    dt = jnp.exp(jax.random.uniform(ks[8], (n_dn, 2, DN_HEADS), jnp.float32,
                                    minval=math.log(1e-3), maxval=math.log(1e-1)))
    return {
        "x": nrm(ks[0], (BATCH, SEQ, D_MODEL), 1.0),
        "mem": nrm(ks[1], (BATCH, MEM_LEN, D_MODEL), 1.0),
        "rel_bias": nrm(ks[2], (REL_BUCKETS, ATT_HEADS), 0.5),
        "att_w_in": nrm(ks[3], (n_att, D_MODEL, ATT_IN), D_MODEL ** -0.5),
        "att_w_out": nrm(ks[4], (n_att, MIX_WIDTH, D_MODEL), MIX_WIDTH ** -0.5),
        "dn_w_in": nrm(ks[5], (n_dn, D_MODEL, DN_IN), D_MODEL ** -0.5),
        "dn_conv": nrm(ks[6], (n_dn, DN_CONV, 3 * TOK_WIDTH), DN_CONV ** -0.5),
        "dn_a_log": jnp.log(jax.random.uniform(ks[7], (n_dn, 2, DN_HEADS), jnp.float32, minval=1.0, maxval=16.0)),
        "dn_dt_bias": dt + jnp.log(-jnp.expm1(-dt)),
        "dn_out_norm": gain(ks[9], (n_dn, DN_HEAD_DIM)),
        "dn_w_out": nrm(ks[10], (n_dn, MIX_WIDTH, D_MODEL), MIX_WIDTH ** -0.5),
        "mem_norm": gain(ks[11], (DEPTH, D_MODEL)),
        "mem_w_kv": nrm(ks[12], (DEPTH, D_MODEL, 2 * MEM_WIDTH), D_MODEL ** -0.5),
        "norm_mix_pre": gain(ks[13], (DEPTH, D_MODEL)),
        "norm_mix_post": gain(ks[14], (DEPTH, D_MODEL)),
        "norm_ffn_pre": gain(ks[15], (DEPTH, D_MODEL)),
        "norm_ffn_post": gain(ks[16], (DEPTH, D_MODEL)),
        "ffn_w_gate_up": nrm(ks[17], (DEPTH, D_MODEL, 2 * D_FF), D_MODEL ** -0.5),
        "ffn_w_down": nrm(ks[18], (DEPTH, D_FF, D_MODEL), D_FF ** -0.5),
    }


def _fwd_reference(x, mem, rel_bias, att_w_in, att_w_out, dn_w_in, dn_conv, dn_a_log, dn_dt_bias,
              dn_out_norm, dn_w_out, mem_norm, mem_w_kv, norm_mix_pre, norm_mix_post,
              norm_ffn_pre, norm_ffn_post, ffn_w_gate_up, ffn_w_down):
    for i in range(DEPTH):
        j = i // N_MIXERS
        h = _rms_norm(x, norm_mix_pre[i])
        mem_n = _rms_norm(mem, mem_norm[i])
        if i % N_MIXERS == 0:
            mixed = _attention_sublayer(h, mem_n, att_w_in[j], att_w_out[j], rel_bias, mem_w_kv[i])
        else:
            mixed = _deltanet_sublayer(h, mem_n, dn_w_in[j], dn_conv[j], dn_a_log[j], dn_dt_bias[j],
                                       dn_out_norm[j], dn_w_out[j], mem_w_kv[i])
        x = x + _rms_norm(mixed, norm_mix_post[i])
        h = _rms_norm(x, norm_ffn_pre[i])
        x = x + _rms_norm(_swiglu(h, ffn_w_gate_up[i], ffn_w_down[i]), norm_ffn_post[i])
    return x


import jax as _jax
import jax.numpy as _jnp

TWIN_FORMAT = 'train_step'
FWD_PARAMS = ['x', 'mem', 'rel_bias', 'att_w_in', 'att_w_out', 'dn_w_in', 'dn_conv', 'dn_a_log', 'dn_dt_bias', 'dn_out_norm', 'dn_w_out', 'mem_norm', 'mem_w_kv', 'norm_mix_pre', 'norm_mix_post', 'norm_ffn_pre', 'norm_ffn_post', 'ffn_w_gate_up', 'ffn_w_down']
TWIN_WEIGHTS = ['rel_bias', 'att_w_in', 'att_w_out', 'dn_w_in', 'dn_conv', 'dn_a_log', 'dn_dt_bias', 'dn_out_norm', 'dn_w_out', 'mem_norm', 'mem_w_kv', 'norm_mix_pre', 'norm_mix_post', 'norm_ffn_pre', 'norm_ffn_post', 'ffn_w_gate_up', 'ffn_w_down']
TWIN_DIFF_INPUT = 'x'
TWIN_INPUTS = ['x', 'mem', 'rel_bias', 'att_w_in', 'att_w_out', 'dn_w_in', 'dn_conv', 'dn_a_log', 'dn_dt_bias', 'dn_out_norm', 'dn_w_out', 'mem_norm', 'mem_w_kv', 'norm_mix_pre', 'norm_mix_post', 'norm_ffn_pre', 'norm_ffn_post', 'ffn_w_gate_up', 'ffn_w_down', 'loss_target', 'm_rel_bias', 'm_att_w_in', 'm_att_w_out', 'm_dn_w_in', 'm_dn_conv', 'm_dn_a_log', 'm_dn_dt_bias', 'm_dn_out_norm', 'm_dn_w_out', 'm_mem_norm', 'm_mem_w_kv', 'm_norm_mix_pre', 'm_norm_mix_post', 'm_norm_ffn_pre', 'm_norm_ffn_post', 'm_ffn_w_gate_up', 'm_ffn_w_down', 'v_rel_bias', 'v_att_w_in', 'v_att_w_out', 'v_dn_w_in', 'v_dn_conv', 'v_dn_a_log', 'v_dn_dt_bias', 'v_dn_out_norm', 'v_dn_w_out', 'v_mem_norm', 'v_mem_w_kv', 'v_norm_mix_pre', 'v_norm_mix_post', 'v_norm_ffn_pre', 'v_norm_ffn_post', 'v_ffn_w_gate_up', 'v_ffn_w_down']
TWIN_OUTPUTS = ['loss', 'grad_x', 'grad_rel_bias', 'grad_att_w_in', 'grad_att_w_out', 'grad_dn_w_in', 'grad_dn_conv', 'grad_dn_a_log', 'grad_dn_dt_bias', 'grad_dn_out_norm', 'grad_dn_w_out', 'grad_mem_norm', 'grad_mem_w_kv', 'grad_norm_mix_pre', 'grad_norm_mix_post', 'grad_norm_ffn_pre', 'grad_norm_ffn_post', 'grad_ffn_w_gate_up', 'grad_ffn_w_down', 'delta_rel_bias', 'delta_att_w_in', 'delta_att_w_out', 'delta_dn_w_in', 'delta_dn_conv', 'delta_dn_a_log', 'delta_dn_dt_bias', 'delta_dn_out_norm', 'delta_dn_w_out', 'delta_mem_norm', 'delta_mem_w_kv', 'delta_norm_mix_pre', 'delta_norm_mix_post', 'delta_norm_ffn_pre', 'delta_norm_ffn_post', 'delta_ffn_w_gate_up', 'delta_ffn_w_down', 'new_m_rel_bias', 'new_m_att_w_in', 'new_m_att_w_out', 'new_m_dn_w_in', 'new_m_dn_conv', 'new_m_dn_a_log', 'new_m_dn_dt_bias', 'new_m_dn_out_norm', 'new_m_dn_w_out', 'new_m_mem_norm', 'new_m_mem_w_kv', 'new_m_norm_mix_pre', 'new_m_norm_mix_post', 'new_m_norm_ffn_pre', 'new_m_norm_ffn_post', 'new_m_ffn_w_gate_up', 'new_m_ffn_w_down', 'new_v_rel_bias', 'new_v_att_w_in', 'new_v_att_w_out', 'new_v_dn_w_in', 'new_v_dn_conv', 'new_v_dn_a_log', 'new_v_dn_dt_bias', 'new_v_dn_out_norm', 'new_v_dn_w_out', 'new_v_mem_norm', 'new_v_mem_w_kv', 'new_v_norm_mix_pre', 'new_v_norm_mix_post', 'new_v_norm_ffn_pre', 'new_v_norm_ffn_post', 'new_v_ffn_w_gate_up', 'new_v_ffn_w_down']
TWIN_LEAF_KINDS = {'loss': 'loss', 'grad_x': 'grad_x', 'grad_rel_bias': 'grad_w', 'grad_att_w_in': 'grad_w', 'grad_att_w_out': 'grad_w', 'grad_dn_w_in': 'grad_w', 'grad_dn_conv': 'grad_w', 'grad_dn_a_log': 'grad_w', 'grad_dn_dt_bias': 'grad_w', 'grad_dn_out_norm': 'grad_w', 'grad_dn_w_out': 'grad_w', 'grad_mem_norm': 'grad_w', 'grad_mem_w_kv': 'grad_w', 'grad_norm_mix_pre': 'grad_w', 'grad_norm_mix_post': 'grad_w', 'grad_norm_ffn_pre': 'grad_w', 'grad_norm_ffn_post': 'grad_w', 'grad_ffn_w_gate_up': 'grad_w', 'grad_ffn_w_down': 'grad_w', 'delta_rel_bias': 'delta_w', 'delta_att_w_in': 'delta_w', 'delta_att_w_out': 'delta_w', 'delta_dn_w_in': 'delta_w', 'delta_dn_conv': 'delta_w', 'delta_dn_a_log': 'delta_w', 'delta_dn_dt_bias': 'delta_w', 'delta_dn_out_norm': 'delta_w', 'delta_dn_w_out': 'delta_w', 'delta_mem_norm': 'delta_w', 'delta_mem_w_kv': 'delta_w', 'delta_norm_mix_pre': 'delta_w', 'delta_norm_mix_post': 'delta_w', 'delta_norm_ffn_pre': 'delta_w', 'delta_norm_ffn_post': 'delta_w', 'delta_ffn_w_gate_up': 'delta_w', 'delta_ffn_w_down': 'delta_w', 'new_m_rel_bias': 'new_m', 'new_m_att_w_in': 'new_m', 'new_m_att_w_out': 'new_m', 'new_m_dn_w_in': 'new_m', 'new_m_dn_conv': 'new_m', 'new_m_dn_a_log': 'new_m', 'new_m_dn_dt_bias': 'new_m', 'new_m_dn_out_norm': 'new_m', 'new_m_dn_w_out': 'new_m', 'new_m_mem_norm': 'new_m', 'new_m_mem_w_kv': 'new_m', 'new_m_norm_mix_pre': 'new_m', 'new_m_norm_mix_post': 'new_m', 'new_m_norm_ffn_pre': 'new_m', 'new_m_norm_ffn_post': 'new_m', 'new_m_ffn_w_gate_up': 'new_m', 'new_m_ffn_w_down': 'new_m', 'new_v_rel_bias': 'new_v', 'new_v_att_w_in': 'new_v', 'new_v_att_w_out': 'new_v', 'new_v_dn_w_in': 'new_v', 'new_v_dn_conv': 'new_v', 'new_v_dn_a_log': 'new_v', 'new_v_dn_dt_bias': 'new_v', 'new_v_dn_out_norm': 'new_v', 'new_v_dn_w_out': 'new_v', 'new_v_mem_norm': 'new_v', 'new_v_mem_w_kv': 'new_v', 'new_v_norm_mix_pre': 'new_v', 'new_v_norm_mix_post': 'new_v', 'new_v_norm_ffn_pre': 'new_v', 'new_v_norm_ffn_post': 'new_v', 'new_v_ffn_w_gate_up': 'new_v', 'new_v_ffn_w_down': 'new_v'}


def _forward(args):
    return _fwd_reference(*[args[k] for k in FWD_PARAMS])


def _output_shape():
    out = _jax.eval_shape(lambda: _forward(_fwd_setup_inputs(0)))
    return out.shape, out.dtype

N_MICROBATCH = 1
ADAM_LR = 0.001
ADAM_B1 = 0.9
ADAM_B2 = 0.999
ADAM_EPS = 1e-08
ADAM_WD = 0.01
ADAM_STEP = 10
PER_EXAMPLE_BATCH_AXIS = {'x': 0, 'mem': 0, 'loss_target': 0}
SHARED_INPUTS = []
_WEIGHT_DTYPES = {'rel_bias': _jnp.float32, 'att_w_in': _jnp.float32, 'att_w_out': _jnp.float32, 'dn_w_in': _jnp.float32, 'dn_conv': _jnp.float32, 'dn_a_log': _jnp.float32, 'dn_dt_bias': _jnp.float32, 'dn_out_norm': _jnp.float32, 'dn_w_out': _jnp.float32, 'mem_norm': _jnp.float32, 'mem_w_kv': _jnp.float32, 'norm_mix_pre': _jnp.float32, 'norm_mix_post': _jnp.float32, 'norm_ffn_pre': _jnp.float32, 'norm_ffn_post': _jnp.float32, 'ffn_w_gate_up': _jnp.float32, 'ffn_w_down': _jnp.float32}
MOMENT_SCALE = {'rel_bias': 1.048849e+00, 'att_w_in': 9.226017e-01, 'att_w_out': 2.717748e+00, 'dn_w_in': 8.515512e-01, 'dn_conv': 1.016522e+00, 'dn_a_log': 1.098077e+00, 'dn_dt_bias': 1.089203e+00, 'dn_out_norm': 9.165627e+00, 'dn_w_out': 2.174744e+00, 'mem_norm': 2.277842e+00, 'mem_w_kv': 3.099130e+00, 'norm_mix_pre': 1.330409e+00, 'norm_mix_post': 3.228109e+01, 'norm_ffn_pre': 1.263444e+00, 'norm_ffn_post': 3.207971e+01, 'ffn_w_gate_up': 5.632168e-01, 'ffn_w_down': 1.103474e+00}


def _to_microbatches(a, axis):
    t = _jnp.moveaxis(a, axis, 0)
    t = t.reshape((N_MICROBATCH, t.shape[0] // N_MICROBATCH) + t.shape[1:])
    return _jnp.moveaxis(t, 1, axis + 1)


def setup_inputs(seed: int = 0) -> dict:
    inp = _fwd_setup_inputs(seed)
    key = _jax.random.fold_in(_jax.random.key(seed), 7919)
    shape, _ = _output_shape()
    out = dict(inp)
    out["loss_target"] = _jax.random.normal(_jax.random.fold_in(key, 0), shape, _jnp.float32)
    for i, name in enumerate(TWIN_WEIGHTS):
        w = inp[name].astype(_jnp.float32)
        if MOMENT_SCALE is None:
            s = _jnp.sqrt(_jnp.mean(_jnp.square(w)) + 1e-30)
        else:
            s = MOMENT_SCALE[name]
        km, kv = _jax.random.split(_jax.random.fold_in(key, i + 1))
        out[name] = w
        out["m_" + name] = s * _jax.random.normal(km, w.shape, _jnp.float32)
        out["v_" + name] = (s * s) * _jax.random.uniform(kv, w.shape, _jnp.float32, 0.5, 1.5)
    if N_MICROBATCH > 1:
        for name, axis in PER_EXAMPLE_BATCH_AXIS.items():
            out[name] = _to_microbatches(out[name], axis)
    return {'x': out['x'], 'mem': out['mem'], 'rel_bias': out['rel_bias'], 'att_w_in': out['att_w_in'], 'att_w_out': out['att_w_out'], 'dn_w_in': out['dn_w_in'], 'dn_conv': out['dn_conv'], 'dn_a_log': out['dn_a_log'], 'dn_dt_bias': out['dn_dt_bias'], 'dn_out_norm': out['dn_out_norm'], 'dn_w_out': out['dn_w_out'], 'mem_norm': out['mem_norm'], 'mem_w_kv': out['mem_w_kv'], 'norm_mix_pre': out['norm_mix_pre'], 'norm_mix_post': out['norm_mix_post'], 'norm_ffn_pre': out['norm_ffn_pre'], 'norm_ffn_post': out['norm_ffn_post'], 'ffn_w_gate_up': out['ffn_w_gate_up'], 'ffn_w_down': out['ffn_w_down'], 'loss_target': out['loss_target'], 'm_rel_bias': out['m_rel_bias'], 'm_att_w_in': out['m_att_w_in'], 'm_att_w_out': out['m_att_w_out'], 'm_dn_w_in': out['m_dn_w_in'], 'm_dn_conv': out['m_dn_conv'], 'm_dn_a_log': out['m_dn_a_log'], 'm_dn_dt_bias': out['m_dn_dt_bias'], 'm_dn_out_norm': out['m_dn_out_norm'], 'm_dn_w_out': out['m_dn_w_out'], 'm_mem_norm': out['m_mem_norm'], 'm_mem_w_kv': out['m_mem_w_kv'], 'm_norm_mix_pre': out['m_norm_mix_pre'], 'm_norm_mix_post': out['m_norm_mix_post'], 'm_norm_ffn_pre': out['m_norm_ffn_pre'], 'm_norm_ffn_post': out['m_norm_ffn_post'], 'm_ffn_w_gate_up': out['m_ffn_w_gate_up'], 'm_ffn_w_down': out['m_ffn_w_down'], 'v_rel_bias': out['v_rel_bias'], 'v_att_w_in': out['v_att_w_in'], 'v_att_w_out': out['v_att_w_out'], 'v_dn_w_in': out['v_dn_w_in'], 'v_dn_conv': out['v_dn_conv'], 'v_dn_a_log': out['v_dn_a_log'], 'v_dn_dt_bias': out['v_dn_dt_bias'], 'v_dn_out_norm': out['v_dn_out_norm'], 'v_dn_w_out': out['v_dn_w_out'], 'v_mem_norm': out['v_mem_norm'], 'v_mem_w_kv': out['v_mem_w_kv'], 'v_norm_mix_pre': out['v_norm_mix_pre'], 'v_norm_mix_post': out['v_norm_mix_post'], 'v_norm_ffn_pre': out['v_norm_ffn_pre'], 'v_norm_ffn_post': out['v_norm_ffn_post'], 'v_ffn_w_gate_up': out['v_ffn_w_gate_up'], 'v_ffn_w_down': out['v_ffn_w_down']}


def _loss(weights, diff, rest, loss_target):
    with _jax.named_scope("forward"):
        args = {**rest, TWIN_DIFF_INPUT: diff, **{k: w.astype(_WEIGHT_DTYPES[k]) for k, w in weights.items()}}
        y = _forward(args)
    with _jax.named_scope("loss_head"):
        err = _jnp.square(y.astype(_jnp.float32) - loss_target)
        return 0.5 * _jnp.sum(_jnp.mean(err, axis=-1)) if err.ndim else 0.5 * err


def _adamw(w, g, m, v):
    m = ADAM_B1 * m + (1.0 - ADAM_B1) * g
    v = ADAM_B2 * v + (1.0 - ADAM_B2) * _jnp.square(g)
    m_hat = m / (1.0 - ADAM_B1 ** ADAM_STEP)
    v_hat = v / (1.0 - ADAM_B2 ** ADAM_STEP)
    delta = -ADAM_LR * (m_hat / (_jnp.sqrt(v_hat) + ADAM_EPS) + ADAM_WD * w)
    return delta, m, v


def reference(x, mem, rel_bias, att_w_in, att_w_out, dn_w_in, dn_conv, dn_a_log, dn_dt_bias, dn_out_norm, dn_w_out, mem_norm, mem_w_kv, norm_mix_pre, norm_mix_post, norm_ffn_pre, norm_ffn_post, ffn_w_gate_up, ffn_w_down, loss_target, m_rel_bias, m_att_w_in, m_att_w_out, m_dn_w_in, m_dn_conv, m_dn_a_log, m_dn_dt_bias, m_dn_out_norm, m_dn_w_out, m_mem_norm, m_mem_w_kv, m_norm_mix_pre, m_norm_mix_post, m_norm_ffn_pre, m_norm_ffn_post, m_ffn_w_gate_up, m_ffn_w_down, v_rel_bias, v_att_w_in, v_att_w_out, v_dn_w_in, v_dn_conv, v_dn_a_log, v_dn_dt_bias, v_dn_out_norm, v_dn_w_out, v_mem_norm, v_mem_w_kv, v_norm_mix_pre, v_norm_mix_post, v_norm_ffn_pre, v_norm_ffn_post, v_ffn_w_gate_up, v_ffn_w_down):
    given = dict(x=x, mem=mem, rel_bias=rel_bias, att_w_in=att_w_in, att_w_out=att_w_out, dn_w_in=dn_w_in, dn_conv=dn_conv, dn_a_log=dn_a_log, dn_dt_bias=dn_dt_bias, dn_out_norm=dn_out_norm, dn_w_out=dn_w_out, mem_norm=mem_norm, mem_w_kv=mem_w_kv, norm_mix_pre=norm_mix_pre, norm_mix_post=norm_mix_post, norm_ffn_pre=norm_ffn_pre, norm_ffn_post=norm_ffn_post, ffn_w_gate_up=ffn_w_gate_up, ffn_w_down=ffn_w_down, loss_target=loss_target, m_rel_bias=m_rel_bias, m_att_w_in=m_att_w_in, m_att_w_out=m_att_w_out, m_dn_w_in=m_dn_w_in, m_dn_conv=m_dn_conv, m_dn_a_log=m_dn_a_log, m_dn_dt_bias=m_dn_dt_bias, m_dn_out_norm=m_dn_out_norm, m_dn_w_out=m_dn_w_out, m_mem_norm=m_mem_norm, m_mem_w_kv=m_mem_w_kv, m_norm_mix_pre=m_norm_mix_pre, m_norm_mix_post=m_norm_mix_post, m_norm_ffn_pre=m_norm_ffn_pre, m_norm_ffn_post=m_norm_ffn_post, m_ffn_w_gate_up=m_ffn_w_gate_up, m_ffn_w_down=m_ffn_w_down, v_rel_bias=v_rel_bias, v_att_w_in=v_att_w_in, v_att_w_out=v_att_w_out, v_dn_w_in=v_dn_w_in, v_dn_conv=v_dn_conv, v_dn_a_log=v_dn_a_log, v_dn_dt_bias=v_dn_dt_bias, v_dn_out_norm=v_dn_out_norm, v_dn_w_out=v_dn_w_out, v_mem_norm=v_mem_norm, v_mem_w_kv=v_mem_w_kv, v_norm_mix_pre=v_norm_mix_pre, v_norm_mix_post=v_norm_mix_post, v_norm_ffn_pre=v_norm_ffn_pre, v_norm_ffn_post=v_norm_ffn_post, v_ffn_w_gate_up=v_ffn_w_gate_up, v_ffn_w_down=v_ffn_w_down)
    weights = {n: given[n] for n in TWIN_WEIGHTS}
    shared = {n: given[n] for n in SHARED_INPUTS}
    per_example = {n: given[n] for n in ['x', 'mem']}
    grad_fn = _jax.value_and_grad(_loss, argnums=(0, 1))

    def one_microbatch(ex, loss_target):
        ex = dict(ex)
        diff = ex.pop(TWIN_DIFF_INPUT)
        return grad_fn(weights, diff, {**shared, **ex}, loss_target)

    if N_MICROBATCH == 1:
        loss, (grad_w, grad_x) = one_microbatch(per_example, given["loss_target"])
    else:
        def body(carry, xs):
            loss_sum, grad_sum = carry
            l_k, (gw_k, gx_k) = one_microbatch(xs[0], xs[1])
            with _jax.named_scope("update"):
                return (loss_sum + l_k, _jax.tree.map(_jnp.add, grad_sum, gw_k)), gx_k

        init = (_jnp.zeros((), _jnp.float32), _jax.tree.map(_jnp.zeros_like, weights))
        (loss, grad_w), grad_x = _jax.lax.scan(body, init, (per_example, given["loss_target"]))
    with _jax.named_scope("update"):
        delta_w, new_m, new_v = {}, {}, {}
        for n in TWIN_WEIGHTS:
            delta_w[n], new_m[n], new_v[n] = _adamw(weights[n], grad_w[n], given["m_" + n], given["v_" + n])
    return (loss, grad_x, *[grad_w[n] for n in TWIN_WEIGHTS], *[delta_w[n] for n in TWIN_WEIGHTS],
            *[new_m[n] for n in TWIN_WEIGHTS], *[new_v[n] for n in TWIN_WEIGHTS])
```

```python
import functools
import math

import numpy as np
import jax
import jax.numpy as jnp
from jax import lax
from jax.experimental import pallas as pl
from jax.experimental.pallas import tpu as pltpu

F32 = jnp.float32
MXU_DTYPE = jnp.bfloat16
HI = lax.Precision.HIGHEST

EPS = 1e-6
NEG_INF = -1e30
LANES = 128
SUBLANES = 8
VMEM_LIMIT = 56 * 1024 * 1024

D_MODEL = 1024
TOK_WIDTH = 768
MEM_WIDTH = 256
MEM_LEN = 256
ATT_HEAD_DIM = 64
DILATIONS = (1, 4, 16)
HALF = 64
ATT_BQ = 128
ATT_W = ATT_BQ + 2 * HALF
REL_BUCKETS = 32
REL_MAX_DIST = 1024
DN_HEADS = 6
DN_HEAD_DIM = 128
DN_CONV = 5
DN_CHUNK = 64
D_FF = 2816
ATT_IN = 2560
DN_IN = 3352
DN_IN_PAD = 3456
N_GATES = 4 * DN_HEADS

ADAM_LR = 0.001
ADAM_B1 = 0.9
ADAM_B2 = 0.999
ADAM_EPS = 1e-08
ADAM_WD = 0.01
ADAM_STEP = 10


def _tile(n, target, align):
    if n <= target:
        return n
    t = (target // align) * align
    while t >= align:
        if n % t == 0:
            return t
        t -= align
    raise ValueError(f"no tile for {n} (target {target}, align {align})")


def _params(*sem):
    return pltpu.CompilerParams(dimension_semantics=sem, vmem_limit_bytes=VMEM_LIMIT)


def _mm(a, b, *, name, ta=False, tb=False, tm=512, tn=512, tk=1024, out_dtype=F32):
    if ta:
        K, M = a.shape
    else:
        M, K = a.shape
    if tb:
        N, K2 = b.shape
    else:
        K2, N = b.shape
    assert K == K2, (a.shape, b.shape, ta, tb)
    tm = _tile(M, tm, LANES if ta else SUBLANES)
    tn = _tile(N, tn, LANES)
    tk = _tile(K, tk, LANES)
    nk = K // tk
    a_spec = pl.BlockSpec((tk, tm), lambda i, j, k: (k, i)) if ta else pl.BlockSpec((tm, tk), lambda i, j, k: (i, k))
    b_spec = pl.BlockSpec((tn, tk), lambda i, j, k: (j, k)) if tb else pl.BlockSpec((tk, tn), lambda i, j, k: (k, j))
    dims = (((0 if ta else 1,), (1 if tb else 0,)), ((), ()))

    def kern(a_ref, b_ref, o_ref, acc_ref):
        k = pl.program_id(2)

        @pl.when(k == 0)
        def _():
            acc_ref[...] = jnp.zeros_like(acc_ref)

        acc_ref[...] += lax.dot_general(a_ref[...].astype(MXU_DTYPE), b_ref[...].astype(MXU_DTYPE), dims,
                                        preferred_element_type=F32)

        @pl.when(k == nk - 1)
        def _():
            o_ref[...] = acc_ref[...].astype(o_ref.dtype)

    return pl.pallas_call(
        kern, name=name, grid=(M // tm, N // tn, nk), in_specs=[a_spec, b_spec],
        out_specs=pl.BlockSpec((tm, tn), lambda i, j, k: (i, j)),
        out_shape=jax.ShapeDtypeStruct((M, N), out_dtype),
        scratch_shapes=[pltpu.VMEM((tm, tn), F32)],
        compiler_params=_params("parallel", "parallel", "arbitrary"),
    )(a, b)


def _col(arr, width, blk):
    return (arr, width, blk)


def _rowwise(body, rows, consts, out_rows, out_acc, *, tm, name):
    n_rows = (rows[0][0] if isinstance(rows[0], tuple) else rows[0]).shape[0]
    assert n_rows % tm == 0, (n_rows, tm)
    arrs, in_specs = [], []
    for r in rows:
        arr, width, blk = r if isinstance(r, tuple) else (r, r.shape[1], 0)
        assert arr.shape[0] == n_rows
        arrs.append(arr)
        in_specs.append(pl.BlockSpec((tm, width), functools.partial(lambda i, b: (i, b), b=blk)))
    for c in consts:
        arrs.append(c)
        in_specs.append(pl.BlockSpec(c.shape, functools.partial(lambda i, n: (0,) * n, n=c.ndim)))
    n_in, n_ro = len(arrs), len(out_rows)
    out_shape = [jax.ShapeDtypeStruct((n_rows, w), dt) for w, dt in out_rows]
    out_specs = [pl.BlockSpec((tm, w), lambda i: (i, 0)) for w, _ in out_rows]
    out_shape += [jax.ShapeDtypeStruct(s, F32) for s in out_acc]
    out_specs += [pl.BlockSpec(s, lambda i: (0, 0)) for s in out_acc]

    def kern(*refs):
        ro, ao = body(*[r[...] for r in refs[:n_in]])
        outs = refs[n_in:]
        for r, v in zip(outs[:n_ro], ro, strict=True):
            r[...] = v.astype(r.dtype)
        if out_acc:
            @pl.when(pl.program_id(0) == 0)
            def _():
                for r in outs[n_ro:]:
                    r[...] = jnp.zeros_like(r)

            for r, v in zip(outs[n_ro:], ao, strict=True):
                r[...] += v

    res = pl.pallas_call(
        kern, name=name, grid=(n_rows // tm,), in_specs=in_specs, out_specs=out_specs, out_shape=out_shape,
        compiler_params=_params("arbitrary" if out_acc else "parallel"),
    )(*arrs)
    return res


def _rms(x, gain):
    return x * lax.rsqrt(jnp.mean(x * x, axis=-1, keepdims=True) + EPS) * gain


def _silu(x):
    return x * jax.nn.sigmoid(x)


def _softplus(x):
    return jnp.maximum(x, 0.0) + jnp.log(1.0 + jnp.exp(-jnp.abs(x)))


def _dot_nt(a, b, precision=None):
    return lax.dot_general(a, b, (((1,), (1,)), ((), ())), preferred_element_type=F32, precision=precision)


def _dot_tn(a, b, precision=None):
    return lax.dot_general(a, b, (((0,), (0,)), ((), ())), preferred_element_type=F32, precision=precision)


def _dot(a, b, precision=None):
    return jnp.dot(a, b, preferred_element_type=F32, precision=precision)


def _pre_norm(x, gain, *, name):
    def body(x, g):
        return (_rms(x, g),), ()
    return _rowwise(body, [x], [gain], [(x.shape[1], F32)], [], tm=_tile(x.shape[0], 512, SUBLANES), name=name)[0]


def _pre_norm_bwd(x, gain, dh, dx_other, *, name):
    def body(x, dh, dxo, g):
        _, vjp = jax.vjp(_rms, x, g)
        dx, dg = vjp(dh)
        return (dx + dxo,), (dg,)
    return _rowwise(body, [x, dh, dx_other], [gain], [(x.shape[1], F32)], [gain.shape], tm=512, name=name)


def _gain_bwd(x, gain, dh, *, name):
    def body(x, dh, g):
        _, vjp = jax.vjp(lambda g_: _rms(x, g_), g)
        return (), (vjp(dh)[0],)
    return _rowwise(body, [x, dh], [gain], [], [gain.shape], tm=_tile(x.shape[0], 512, SUBLANES), name=name)[0]


def _res_block(x_res, m, g_post, g_pre):
    x_new = x_res + _rms(m, g_post)
    return x_new, _rms(x_new, g_pre)


def _post_pre(x_res, m, g_post, g_pre, *, name):
    def body(x, m, gp, gq):
        return _res_block(x, m, gp, gq), ()
    d = x_res.shape[1]
    return _rowwise(body, [x_res, m], [g_post, g_pre], [(d, F32), (d, F32)], [], tm=512, name=name)


def _post_pre_bwd(x_res, m, g_post, g_pre, dx_new, dh, *, name):
    def body(x, m, dxn, dh, gp, gq):
        _, vjp = jax.vjp(_res_block, x, m, gp, gq)
        dx, dm, dgp, dgq = vjp((dxn, dh))
        return (dx, dm), (dgp, dgq)
    d = x_res.shape[1]
    return _rowwise(body, [x_res, m, dx_new, dh], [g_post, g_pre], [(d, F32), (d, F32)],
                    [g_post.shape, g_pre.shape], tm=256, name=name)


def _final_loss_bwd(x_res, m, g_post, target, *, name):
    d = x_res.shape[1]

    def loss_cols(x, m, g, t):
        err = x + _rms(m, g) - t
        return jnp.sum(err * err, axis=0, keepdims=True) * (0.5 / d)

    def body(x, m, t, g):
        cols, vjp = jax.vjp(lambda x_, m_, g_: loss_cols(x_, m_, g_, t), x, m, g)
        dx, dm, dg = vjp(jnp.ones_like(cols))
        return (dx, dm), (dg, cols)
    return _rowwise(body, [x_res, m, target], [g_post], [(d, F32), (d, F32)], [g_post.shape, (1, d)], tm=256, name=name)


def _swiglu_act(gu, *, name):
    def body(gate, up):
        return (_silu(gate) * up,), ()
    return _rowwise(body, [_col(gu, D_FF, 0), _col(gu, D_FF, 1)], [], [(D_FF, F32)], [], tm=256, name=name)[0]


def _swiglu_act_bwd(gu, da, *, name):
    def body(gate, up, da):
        _, vjp = jax.vjp(lambda g, u: _silu(g) * u, gate, up)
        dg, du = vjp(da)
        return (jnp.concatenate([dg, du], axis=1),), ()
    return _rowwise(body, [_col(gu, D_FF, 0), _col(gu, D_FF, 1), da], [], [(2 * D_FF, F32)], [], tm=256, name=name)[0]


def _lane_head_mask(width, head_dim, head):
    lane = lax.broadcasted_iota(jnp.int32, (1, width), 1)
    return (lane // head_dim) == head


def _mem_attn_pair(q_pair, k_pair, v_pair):
    out = jnp.zeros_like(q_pair)
    for h in range(2):
        mh = _lane_head_mask(LANES, ATT_HEAD_DIM, h)
        qh = jnp.where(mh, q_pair * (ATT_HEAD_DIM ** -0.5), 0.0)
        logits = _dot_nt(qh, k_pair)
        mx = jnp.max(logits, axis=-1, keepdims=True)
        p = jnp.exp(logits - mx)
        p = p / jnp.sum(p, axis=-1, keepdims=True)
        out = out + jnp.where(mh, _dot(p, v_pair), 0.0)
    return out


def _mem_attn(q_mem, kv):
    outs = []
    for p in range(MEM_WIDTH // LANES):
        sl = slice(p * LANES, (p + 1) * LANES)
        outs.append(_mem_attn_pair(q_mem[:, sl], kv[:, sl], kv[:, MEM_WIDTH + p * LANES: MEM_WIDTH + (p + 1) * LANES]))
    return jnp.concatenate(outs, axis=1)


def _mem_attn_bwd(q_mem, kv, do):
    dqs, dks, dvs = [], [], []
    for p in range(MEM_WIDTH // LANES):
        sl = slice(p * LANES, (p + 1) * LANES)
        sv = slice(MEM_WIDTH + p * LANES, MEM_WIDTH + (p + 1) * LANES)
        _, vjp = jax.vjp(_mem_attn_pair, q_mem[:, sl], kv[:, sl], kv[:, sv])
        dq, dk, dv = vjp(do[:, sl])
        dqs.append(dq)
        dks.append(dk)
        dvs.append(dv)
    return jnp.concatenate(dqs, axis=1), jnp.concatenate(dks + dvs, axis=1)


def _t5_bucket(rel):
    half = REL_BUCKETS // 2
    max_exact = half // 2
    n = np.abs(rel)
    large = max_exact + (np.log(np.maximum(n, 1) / max_exact) / math.log(REL_MAX_DIST / max_exact)
                         * (half - max_exact)).astype(np.int64)
    large = np.minimum(large, half - 1)
    return ((rel > 0) * half + np.where(n < max_exact, n, large)).astype(np.int32)


def _bias_tile_index(dil):
    q = np.arange(ATT_BQ)[:, None]
    kk = np.arange(ATT_W)[None, :]
    tiles = []
    for off in (-HALF, 0, HALF):
        rel = kk - q - HALF - off
        tiles.append(np.where(np.abs(rel) <= HALF, _t5_bucket(rel * dil), -1))
    return np.stack(tiles)


def _bias_tiles(rel_bias, gi):
    idx = _bias_tile_index(DILATIONS[gi])
    heads = rel_bias[:, 4 * gi: 4 * gi + 4]
    t = jnp.where((idx >= 0)[..., None], heads[np.maximum(idx, 0)], 0.0)
    return jnp.transpose(t, (0, 3, 1, 2))


def _bias_tiles_bwd(dtiles, gi):
    idx = _bias_tile_index(DILATIONS[gi])
    onehot = jnp.asarray((idx[..., None] == np.arange(REL_BUCKETS)).astype(np.float32))
    return jnp.einsum('thqk,tqkb->bh', dtiles, onehot, precision=HI)


def _att_window(i, n_sub):
    start = jnp.clip(i * ATT_BQ - HALF, 0, n_sub - ATT_W)
    off = i * ATT_BQ - HALF - start
    return pl.multiple_of(start, HALF), off


def _att_valid(off):
    q = lax.broadcasted_iota(jnp.int32, (ATT_BQ, ATT_W), 0)
    kk = lax.broadcasted_iota(jnp.int32, (ATT_BQ, ATT_W), 1)
    return jnp.abs(kk - q - HALF - off) <= HALF


def _att_tile_id(i, nq):
    return jnp.where(i == 0, 0, jnp.where(i == nq - 1, 2, 1))


def _att_fwd(qkvm, bias, gi, *, name):
    dil = DILATIONS[gi]
    s_len = qkvm.shape[0]
    n_sub = s_len // dil
    nq = n_sub // ATT_BQ
    assert n_sub % ATT_BQ == 0 and n_sub >= ATT_W
    cols = qkvm.shape[1] // LANES
    view = qkvm.reshape(n_sub, dil * qkvm.shape[1])

    def kern(q_ref, k_ref, v_ref, b_ref, o_ref, lse_ref):
        i = pl.program_id(2)
        start, off = _att_window(i, n_sub)
        valid = _att_valid(off)
        q = q_ref[...] * (ATT_HEAD_DIM ** -0.5)
        kw = k_ref[pl.ds(start, ATT_W), :]
        vw = v_ref[pl.ds(start, ATT_W), :]
        o = jnp.zeros((ATT_BQ, LANES), F32)
        lse = jnp.zeros((ATT_BQ, LANES), F32)
        for h in range(2):
            mh = _lane_head_mask(LANES, ATT_HEAD_DIM, h)
            s = _dot_nt(jnp.where(mh, q, 0.0), kw) + b_ref[h]
            s = jnp.where(valid, s, NEG_INF)
            mx = jnp.max(s, axis=-1, keepdims=True)
            p = jnp.exp(s - mx)
            den = jnp.sum(p, axis=-1, keepdims=True)
            o = jnp.where(mh, _dot(p, vw) / den, o)
            lse = jnp.where(mh, mx + jnp.log(den), lse)
        o_ref[...] = o
        lse_ref[...] = lse

    def qkv_spec(which, full):
        shape = (n_sub, LANES) if full else (ATT_BQ, LANES)
        return pl.BlockSpec(shape, lambda pr, r, i: (0 if full else i, r * cols + which * 6 + 2 * gi + pr))

    out_spec = pl.BlockSpec((ATT_BQ, LANES), lambda pr, r, i: (i, r * 2 + pr))
    o, lse = pl.pallas_call(
        kern, name=name, grid=(2, dil, nq),
        in_specs=[qkv_spec(0, False), qkv_spec(1, True), qkv_spec(2, True),
                  pl.BlockSpec((None, 2, ATT_BQ, ATT_W), lambda pr, r, i: (_att_tile_id(i, nq), pr, 0, 0))],
        out_specs=[out_spec, out_spec],
        out_shape=[jax.ShapeDtypeStruct((n_sub, dil * 2 * LANES), F32)] * 2,
        compiler_params=_params("parallel", "parallel", "arbitrary"),
    )(view, view, view, bias)
    return o.reshape(s_len, 2 * LANES), lse.reshape(s_len, 2 * LANES)


def _att_bwd(qkvm, bias, lse_tot, delta, dcat, gi, *, name):
    dil = DILATIONS[gi]
    s_len = qkvm.shape[0]
    n_sub = s_len // dil
    nq = n_sub // ATT_BQ
    cols = qkvm.shape[1] // LANES
    dcols = dcat.shape[1] // LANES
    view = qkvm.reshape(n_sub, dil * qkvm.shape[1])
    lse_v = lse_tot.reshape(n_sub, dil * 2 * LANES)
    delta_v = delta.reshape(n_sub, dil * 2 * LANES)
    dcat_v = dcat.reshape(n_sub, dil * dcat.shape[1])

    def kern(q_ref, k_ref, v_ref, b_ref, lse_ref, dl_ref, dm_ref, dq_ref, dk_ref, dv_ref, db_ref):
        r, i = pl.program_id(1), pl.program_id(2)
        start, off = _att_window(i, n_sub)
        valid = _att_valid(off)
        tile = _att_tile_id(i, nq)

        @pl.when(i == 0)
        def _():
            dk_ref[...] = jnp.zeros_like(dk_ref)
            dv_ref[...] = jnp.zeros_like(dv_ref)

        @pl.when((i == 0) & (r == 0))
        def _():
            db_ref[...] = jnp.zeros_like(db_ref)

        q = q_ref[...] * (ATT_HEAD_DIM ** -0.5)
        kw = k_ref[pl.ds(start, ATT_W), :]
        vw = v_ref[pl.ds(start, ATT_W), :]
        dm = dm_ref[...]
        lse = lse_ref[...]
        dl = dl_ref[...]
        dq = jnp.zeros((ATT_BQ, LANES), F32)
        dkw = jnp.zeros((ATT_W, LANES), F32)
        dvw = jnp.zeros((ATT_W, LANES), F32)
        for h in range(2):
            mh = _lane_head_mask(LANES, ATT_HEAD_DIM, h)
            qh = jnp.where(mh, q, 0.0)
            dmh = jnp.where(mh, dm, 0.0)
            s = _dot_nt(qh, kw) + b_ref[tile, h]
            s = jnp.where(valid, s, NEG_INF)
            lse_h = jnp.max(jnp.where(mh, lse, NEG_INF), axis=-1, keepdims=True)
            dl_h = jnp.max(jnp.where(mh, dl, NEG_INF), axis=-1, keepdims=True)
            p = jnp.exp(s - lse_h)
            ds = p * (_dot_nt(dmh, vw) - dl_h)
            dq = dq + jnp.where(mh, _dot(ds, kw), 0.0)
            dkw = dkw + _dot_tn(ds, qh)
            dvw = dvw + _dot_tn(p, dmh)
            db_ref[tile, h] += ds
        dq_ref[...] = dq * (ATT_HEAD_DIM ** -0.5)
        dk_ref[pl.ds(start, ATT_W), :] += dkw
        dv_ref[pl.ds(start, ATT_W), :] += dvw

    def qkv_spec(which, full):
        shape = (n_sub, LANES) if full else (ATT_BQ, LANES)
        return pl.BlockSpec(shape, lambda pr, r, i: (0 if full else i, r * cols + which * 6 + 2 * gi + pr))

    blk = pl.BlockSpec((ATT_BQ, LANES), lambda pr, r, i: (i, r * 2 + pr))
    full = pl.BlockSpec((n_sub, LANES), lambda pr, r, i: (0, r * 2 + pr))
    bias_spec = pl.BlockSpec((3, 2, ATT_BQ, ATT_W), lambda pr, r, i: (0, pr, 0, 0))
    sub = jax.ShapeDtypeStruct((n_sub, dil * 2 * LANES), F32)
    dq, dk, dv, db = pl.pallas_call(
        kern, name=name, grid=(2, dil, nq),
        in_specs=[qkv_spec(0, False), qkv_spec(1, True), qkv_spec(2, True), bias_spec, blk, blk,
                  pl.BlockSpec((ATT_BQ, LANES), lambda pr, r, i: (i, r * dcols + 2 * gi + pr))],
        out_specs=[blk, full, full, bias_spec],
        out_shape=[sub, sub, sub, jax.ShapeDtypeStruct(bias.shape, F32)],
        compiler_params=_params("arbitrary", "arbitrary", "arbitrary"),
    )(view, view, view, bias, lse_v, delta_v, dcat_v)
    return dq.reshape(s_len, -1), dk.reshape(s_len, -1), dv.reshape(s_len, -1), db


def _att_combine(o_g, lse_g, qkvm, kv_mem, *, name):
    def body(o0, o1, o2, l0, l1, l2, qm, kv):
        mx = jnp.maximum(jnp.maximum(l0, l1), l2)
        tot = mx + jnp.log(jnp.exp(l0 - mx) + jnp.exp(l1 - mx) + jnp.exp(l2 - mx))
        mixed = [o * jnp.exp(l - tot) for o, l in ((o0, l0), (o1, l1), (o2, l2))]
        return (jnp.concatenate(mixed + [_mem_attn(qm, kv)], axis=1), tot), ()
    return _rowwise(body, list(o_g) + list(lse_g) + [_col(qkvm, MEM_WIDTH, (3 * TOK_WIDTH) // MEM_WIDTH)], [kv_mem],
                    [(D_MODEL, F32), (MEM_WIDTH, F32)], [], tm=256, name=name)


def _head_sum_matrix():
    a = np.arange(MEM_WIDTH)
    return jnp.asarray((a[:, None] // ATT_HEAD_DIM == a[None, :] // ATT_HEAD_DIM).astype(np.float32))


def _att_bwd_prep(cat, dcat, qkvm, kv_mem, *, name):
    def body(cat, dcat, qm, kv, hs):
        prod = cat * dcat
        summed = prod[:, 0:256] + prod[:, 256:512] + prod[:, 512:768]
        delta = _dot(summed, hs, precision=HI)
        dqm, dkv = _mem_attn_bwd(qm, kv, dcat[:, TOK_WIDTH:])
        return (delta, dqm), (dkv,)
    return _rowwise(body, [cat, dcat, _col(qkvm, MEM_WIDTH, (3 * TOK_WIDTH) // MEM_WIDTH)], [kv_mem, _head_sum_matrix()],
                    [(MEM_WIDTH, F32), (MEM_WIDTH, F32)], [kv_mem.shape], tm=256, name=name)


def _dn_conv_post(s, j):
    scale = jnp.where(j < DN_HEADS, DN_HEAD_DIM ** -0.5, 1.0)
    normed = s * lax.rsqrt(jnp.sum(s * s, axis=-1, keepdims=True) + EPS) * scale
    return jnp.where(j >= 2 * DN_HEADS, s, normed)


def _shift_rows(x, sh):
    n = x.shape[0]
    row = lax.broadcasted_iota(jnp.int32, (n, 1), 0)
    rolled = pltpu.roll(x, (-sh) % n, 0)
    return jnp.where((row + sh >= 0) & (row + sh < n), rolled, 0.0)


def _dn_conv_taps(x, w_ref):
    c = x * w_ref[pl.ds(DN_CONV // 2, 1), :]
    for jj in range(DN_CONV):
        if jj != DN_CONV // 2:
            c = c + _shift_rows(x, jj - DN_CONV // 2) * w_ref[pl.ds(jj, 1), :]
    return c


def _dn_conv_fwd(proj, conv_w, *, name):
    s_len = proj.shape[0]
    width = 3 * TOK_WIDTH

    def kern(x_ref, w_ref, o_ref):
        j = pl.program_id(0)
        o_ref[...] = _dn_conv_post(_silu(_dn_conv_taps(x_ref[...], w_ref)), j)

    return pl.pallas_call(
        kern, name=name, grid=(width // LANES,),
        in_specs=[pl.BlockSpec((s_len, LANES), lambda j: (0, j)), pl.BlockSpec((DN_CONV, LANES), lambda j: (0, j))],
        out_specs=pl.BlockSpec((s_len, LANES), lambda j: (0, j)),
        out_shape=jax.ShapeDtypeStruct((s_len, width), F32),
        compiler_params=_params("parallel"),
    )(proj, conv_w)


def _dn_conv_bwd(proj, conv_w, d_post, which, *, name):
    s_len = proj.shape[0]

    def kern(x_ref, w_ref, df_ref, db_ref, dx_ref, dw_ref):
        j = pl.program_id(0) + which * DN_HEADS
        x = x_ref[...]
        c = _dn_conv_taps(x, w_ref)
        _, vjp = jax.vjp(lambda c_: _dn_conv_post(_silu(c_), j), c)
        dc = vjp(df_ref[...] + db_ref[...])[0]
        dx = dc * w_ref[pl.ds(DN_CONV // 2, 1), :]
        for jj in range(DN_CONV):
            sh = jj - DN_CONV // 2
            if sh != 0:
                dx = dx + _shift_rows(dc, -sh) * w_ref[pl.ds(jj, 1), :]
            dw_ref[pl.ds(jj, 1), :] = jnp.sum(dc * _shift_rows(x, sh), axis=0, keepdims=True)
        dx_ref[...] = dx

    return pl.pallas_call(
        kern, name=name, grid=(DN_HEADS,),
        in_specs=[pl.BlockSpec((s_len, LANES), lambda j: (0, j + which * DN_HEADS)),
                  pl.BlockSpec((DN_CONV, LANES), lambda j: (0, j + which * DN_HEADS)),
                  pl.BlockSpec((s_len, LANES), lambda j: (0, j)),
                  pl.BlockSpec((s_len, LANES), lambda j: (0, j + DN_HEADS))],
        out_specs=[pl.BlockSpec((s_len, LANES), lambda j: (0, j)), pl.BlockSpec((DN_CONV, LANES), lambda j: (0, j))],
        out_shape=[jax.ShapeDtypeStruct((s_len, TOK_WIDTH), F32), jax.ShapeDtypeStruct((DN_CONV, TOK_WIDTH), F32)],
        compiler_params=_params("parallel"),
    )(proj, conv_w, d_post, d_post)


def _gate_select_matrices():
    sel_a = np.zeros((LANES, 2 * TOK_WIDTH), np.float32)
    sel_b = np.zeros((LANES, 2 * TOK_WIDTH), np.float32)
    for d in range(2):
        for h in range(DN_HEADS):
            cols = slice((d * DN_HEADS + h) * DN_HEAD_DIM, (d * DN_HEADS + h + 1) * DN_HEAD_DIM)
            sel_a[d * 2 * DN_HEADS + h, cols] = 1.0
            sel_b[d * 2 * DN_HEADS + DN_HEADS + h, cols] = 1.0
    return jnp.asarray(sel_a), jnp.asarray(sel_b)


def _dn_gates(gate_in, a_wide, dt_wide, sel_a, sel_b):
    g = -jnp.exp(a_wide) * _softplus(_dot(gate_in, sel_a, precision=HI) + dt_wide)
    beta = jax.nn.sigmoid(_dot(gate_in, sel_b, precision=HI))
    return g, beta


def _dn_gates_fwd(proj, a_wide, dt_wide, *, name):
    sel_a, sel_b = _gate_select_matrices()

    def body(gi, a, dt, sa, sb):
        return _dn_gates(gi, a, dt, sa, sb), ()
    w = 2 * TOK_WIDTH
    return _rowwise(body, [_col(proj, LANES, DN_IN_PAD // LANES - 1)], [a_wide, dt_wide, sel_a, sel_b],
                    [(w, F32), (w, F32)], [], tm=512, name=name)


def _dn_gates_bwd(proj, a_wide, dt_wide, dg, dbeta, *, name):
    sel_a, sel_b = _gate_select_matrices()

    def body(gi, dg, db, a, dt, sa, sb):
        _, vjp = jax.vjp(lambda gi_, a_, dt_: _dn_gates(gi_, a_, dt_, sa, sb), gi, a, dt)
        dgi, da, ddt = vjp((dg, db))
        return (dgi,), (da, ddt)
    return _rowwise(body, [_col(proj, LANES, DN_IN_PAD // LANES - 1), dg, dbeta], [a_wide, dt_wide, sel_a, sel_b],
                    [(LANES, F32)], [a_wide.shape, dt_wide.shape], tm=256, name=name)


def _dn_chunk(q, k, v, g, beta, state, tri):
    c = q.shape[0]
    eye = (lax.broadcasted_iota(jnp.int32, (c, c), 0) == lax.broadcasted_iota(jnp.int32, (c, c), 1)).astype(F32)
    lane_mean = jnp.full((c, DN_HEAD_DIM), 1.0 / DN_HEAD_DIM, F32)
    gc = _dot(tri, g, precision=HI)
    g_i = _dot_nt(gc, lane_mean, precision=HI)
    g_j = _dot_nt(lane_mean, gc, precision=HI)
    decay = jnp.exp(jnp.where(tri > 0, g_i - g_j, NEG_INF))
    k_beta = k * beta
    neg_l = -(tri - eye) * (_dot_nt(k_beta, k) * decay)
    inv = eye + neg_l
    power = neg_l
    for _ in range(int(math.log2(c)) - 1):
        power = _dot(power, power, precision=HI)
        inv = inv + _dot(inv, power, precision=HI)
    e_gc = jnp.exp(gc)
    u = _dot(inv, v * beta, precision=HI)
    w = _dot(inv, k_beta * e_gc, precision=HI)
    intra = tri * (_dot_nt(q, k) * decay)
    v_new = u - _dot(w, state)
    out = _dot(q * e_gc, state) + _dot(intra, v_new)
    g_last = _dot(jnp.ones((c, c), F32), g, precision=HI)
    g_last_sq = _dot(jnp.ones((DN_HEAD_DIM, c), F32), g, precision=HI)
    state = state * jnp.exp(g_last_sq) + _dot_tn(k * jnp.exp(g_last - gc), v_new)
    return out, state


def _dn_tri():
    i = np.arange(DN_CHUNK)
    return jnp.asarray(np.stack([(i[None, :] <= i[:, None]), (i[None, :] >= i[:, None])]).astype(np.float32))


def _dn_scan_specs(nc, reverse_steps):
    def chunk(d, t):
        fwd = t + d * (nc - 1 - 2 * t)
        return (nc - 1 - fwd) if reverse_steps else fwd
    return chunk


def _dn_scan_fwd(qkv, g, beta, *, name):
    s_len = qkv.shape[0]
    nc = s_len // DN_CHUNK
    chunk = _dn_scan_specs(nc, False)

    def kern(q_ref, k_ref, v_ref, g_ref, b_ref, tri_ref, o_ref, st_ref, state):
        @pl.when(pl.program_id(2) == 0)
        def _():
            state[...] = jnp.zeros_like(state)

        st_ref[...] = state[...]
        out, new = _dn_chunk(q_ref[...], k_ref[...], v_ref[...], g_ref[...], b_ref[...], state[...], tri_ref[...])
        o_ref[...] = out
        state[...] = new

    def qkv_spec(which):
        return pl.BlockSpec((DN_CHUNK, LANES), lambda d, h, t: (chunk(d, t), which * DN_HEADS + h))

    per_dir = pl.BlockSpec((DN_CHUNK, LANES), lambda d, h, t: (chunk(d, t), d * DN_HEADS + h))
    return pl.pallas_call(
        kern, name=name, grid=(2, DN_HEADS, nc),
        in_specs=[qkv_spec(0), qkv_spec(1), qkv_spec(2), per_dir, per_dir,
                  pl.BlockSpec((None, DN_CHUNK, DN_CHUNK), lambda d, h, t: (d, 0, 0))],
        out_specs=[per_dir, pl.BlockSpec((None, DN_HEAD_DIM, DN_HEAD_DIM),
                                         lambda d, h, t: ((d * DN_HEADS + h) * nc + chunk(d, t), 0, 0))],
        out_shape=[jax.ShapeDtypeStruct((s_len, 2 * TOK_WIDTH), F32),
                   jax.ShapeDtypeStruct((2 * DN_HEADS * nc, DN_HEAD_DIM, DN_HEAD_DIM), F32)],
        scratch_shapes=[pltpu.VMEM((DN_HEAD_DIM, DN_HEAD_DIM), F32)],
        compiler_params=_params("parallel", "parallel", "arbitrary"),
    )(qkv, qkv, qkv, g, beta, _dn_tri())


def _dn_scan_bwd(qkv, g, beta, states, d_o, *, name):
    s_len = qkv.shape[0]
    nc = s_len // DN_CHUNK
    chunk = _dn_scan_specs(nc, True)

    def kern(q_ref, k_ref, v_ref, g_ref, b_ref, tri_ref, st_ref, do_ref, dq_ref, dk_ref, dv_ref, dg_ref, db_ref, d_state):
        @pl.when(pl.program_id(2) == 0)
        def _():
            d_state[...] = jnp.zeros_like(d_state)

        tri = tri_ref[...]
        _, vjp = jax.vjp(lambda q, k, v, g_, b_, s: _dn_chunk(q, k, v, g_, b_, s, tri),
                         q_ref[...], k_ref[...], v_ref[...], g_ref[...], b_ref[...], st_ref[...])
        dq, dk, dv, dg, db, ds = vjp((do_ref[...], d_state[...]))
        dq_ref[...] = dq
        dk_ref[...] = dk
        dv_ref[...] = dv
        dg_ref[...] = dg
        db_ref[...] = db
        d_state[...] = ds

    def qkv_spec(which):
        return pl.BlockSpec((DN_CHUNK, LANES), lambda d, h, t: (chunk(d, t), which * DN_HEADS + h))

    per_dir = pl.BlockSpec((DN_CHUNK, LANES), lambda d, h, t: (chunk(d, t), d * DN_HEADS + h))
    wide = jax.ShapeDtypeStruct((s_len, 2 * TOK_WIDTH), F32)
    return pl.pallas_call(
        kern, name=name, grid=(2, DN_HEADS, nc),
        in_specs=[qkv_spec(0), qkv_spec(1), qkv_spec(2), per_dir, per_dir,
                  pl.BlockSpec((None, DN_CHUNK, DN_CHUNK), lambda d, h, t: (d, 0, 0)),
                  pl.BlockSpec((None, DN_HEAD_DIM, DN_HEAD_DIM), lambda d, h, t: ((d * DN_HEADS + h) * nc + chunk(d, t), 0, 0)),
                  pl.BlockSpec((DN_CHUNK, LANES), lambda d, h, t: (chunk(d, t), h))],
        out_specs=[per_dir] * 5,
        out_shape=[wide] * 5,
        scratch_shapes=[pltpu.VMEM((DN_HEAD_DIM, DN_HEAD_DIM), F32)],
        compiler_params=_params("parallel", "parallel", "arbitrary"),
    )(qkv, qkv, qkv, g, beta, _dn_tri(), states, d_o)


def _dn_out_head(o_f, o_b, z, gain):
    o = o_f + o_b
    return o * lax.rsqrt(jnp.mean(o * o, axis=-1, keepdims=True) + EPS) * gain * _silu(z)


def _dn_out(o_dirs, proj, gain, qkv_kv_mem, *, name):
    def body(of, ob, z, qm, g, kv):
        heads = []
        for h in range(DN_HEADS):
            sl = slice(h * DN_HEAD_DIM, (h + 1) * DN_HEAD_DIM)
            heads.append(_dn_out_head(of[:, sl], ob[:, sl], z[:, sl], g))
        return (jnp.concatenate(heads + [_mem_attn(qm, kv)], axis=1),), ()
    return _rowwise(body, [_col(o_dirs, TOK_WIDTH, 0), _col(o_dirs, TOK_WIDTH, 1), _col(proj, TOK_WIDTH, 3),
                           _col(proj, MEM_WIDTH, (4 * TOK_WIDTH) // MEM_WIDTH)], [gain, qkv_kv_mem],
                    [(D_MODEL, F32)], [], tm=256, name=name)[0]


def _dn_out_bwd(o_dirs, proj, gain, kv_mem, dcat, *, name):
    def body(of, ob, z, qm, dcat, g, kv):
        dos, dzs = [], []
        dgain = jnp.zeros_like(g)
        for h in range(DN_HEADS):
            sl = slice(h * DN_HEAD_DIM, (h + 1) * DN_HEAD_DIM)
            _, vjp = jax.vjp(_dn_out_head, of[:, sl], ob[:, sl], z[:, sl], g)
            d_of, _, dz, dg = vjp(dcat[:, sl])
            dos.append(d_of)
            dzs.append(dz)
            dgain = dgain + dg
        dqm, dkv = _mem_attn_bwd(qm, kv, dcat[:, TOK_WIDTH:])
        return (jnp.concatenate(dos, axis=1), jnp.concatenate(dzs, axis=1), dqm), (dgain, dkv)
    return _rowwise(body, [_col(o_dirs, TOK_WIDTH, 0), _col(o_dirs, TOK_WIDTH, 1), _col(proj, TOK_WIDTH, 3),
                           _col(proj, MEM_WIDTH, (4 * TOK_WIDTH) // MEM_WIDTH), dcat], [gain, kv_mem],
                    [(TOK_WIDTH, F32), (TOK_WIDTH, F32), (MEM_WIDTH, F32)], [gain.shape, kv_mem.shape], tm=256, name=name)


def _wide(p):
    return jnp.repeat(p.reshape(1, 2 * DN_HEADS), DN_HEAD_DIM, axis=1)


def _wide_bwd(dp):
    return dp.reshape(2, DN_HEADS, DN_HEAD_DIM).sum(axis=-1)


def _pad_dn_w_in(w):
    gates = w[:, 4 * TOK_WIDTH: 4 * TOK_WIDTH + N_GATES]
    zeros = jnp.zeros((w.shape[0], DN_IN_PAD - DN_IN), w.dtype)
    return jnp.concatenate([w[:, :4 * TOK_WIDTH], w[:, 4 * TOK_WIDTH + N_GATES:], gates, zeros], axis=1)


def _unpad_dn_w_in(w):
    q_mem = w[:, 4 * TOK_WIDTH: 4 * TOK_WIDTH + MEM_WIDTH]
    gates = w[:, 4 * TOK_WIDTH + MEM_WIDTH: 4 * TOK_WIDTH + MEM_WIDTH + N_GATES]
    return jnp.concatenate([w[:, :4 * TOK_WIDTH], gates, q_mem], axis=1)


def _ffn_fwd(h, w_gu, w_d, tag):
    gu = _mm(h, w_gu, name=f"ffn_gu_{tag}")
    act = _swiglu_act(gu, name=f"ffn_act_{tag}")
    return gu, act, _mm(act, w_d, name=f"ffn_down_{tag}")


def _ffn_bwd(h, gu, act, w_gu, w_d, df, tag):
    d_act = _mm(df, w_d, tb=True, name=f"ffn_dact_{tag}")
    d_wd = _mm(act, df, ta=True, name=f"ffn_dwd_{tag}")
    d_gu = _swiglu_act_bwd(gu, d_act, name=f"ffn_dgu_{tag}")
    dh = _mm(d_gu, w_gu, tb=True, name=f"ffn_dh_{tag}")
    d_wgu = _mm(h, d_gu, ta=True, name=f"ffn_dwgu_{tag}")
    return dh, d_wgu, d_wd


def _local_step(x, mem, target, p):
    g = {}
    row = lambda v: v.reshape(1, -1)
    gains = {k: [row(p[k][i]) for i in range(2)] for k in
             ("mem_norm", "norm_mix_pre", "norm_mix_post", "norm_ffn_pre", "norm_ffn_post")}
    out_gain = row(p["dn_out_norm"])
    a_wide, dt_wide = _wide(p["dn_a_log"]), _wide(p["dn_dt_bias"])

    h0 = _pre_norm(x, gains["norm_mix_pre"][0], name="pre0")
    mem_n = [_pre_norm(mem, gains["mem_norm"][i], name=f"mem_norm{i}") for i in range(2)]
    kv_mem = [_mm(mem_n[i], p["mem_w_kv"][i], name=f"mem_kv{i}") for i in range(2)]
    qkvm = _mm(h0, p["att_w_in"], name="att_in")
    bias = [_bias_tiles(p["rel_bias"], gi) for gi in range(3)]
    att = [_att_fwd(qkvm, bias[gi], gi, name=f"att_fwd{gi}") for gi in range(3)]
    cat0, lse_tot = _att_combine([a[0] for a in att], [a[1] for a in att], qkvm, kv_mem[0], name="att_combine")
    mo0 = _mm(cat0, p["att_w_out"], name="att_out")
    x1, h1 = _post_pre(x, mo0, gains["norm_mix_post"][0], gains["norm_ffn_pre"][0], name="post_mix0")
    gu0, act0, f0 = _ffn_fwd(h1, p["ffn_w_gate_up"][0], p["ffn_w_down"][0], 0)
    x2, h2 = _post_pre(x1, f0, gains["norm_ffn_post"][0], gains["norm_mix_pre"][1], name="post_ffn0")

    proj = _mm(h2, p["dn_w_in"], name="dn_in")
    qkv = _dn_conv_fwd(proj, p["dn_conv"], name="dn_conv")
    gl, beta = _dn_gates_fwd(proj, a_wide, dt_wide, name="dn_gates")
    o_dirs, states = _dn_scan_fwd(qkv, gl, beta, name="dn_scan")
    cat1 = _dn_out(o_dirs, proj, out_gain, kv_mem[1], name="dn_outnorm")
    mo1 = _mm(cat1, p["dn_w_out"], name="dn_out")
    x3, h3 = _post_pre(x2, mo1, gains["norm_mix_post"][1], gains["norm_ffn_pre"][1], name="post_mix1")
    gu1, act1, f1 = _ffn_fwd(h3, p["ffn_w_gate_up"][1], p["ffn_w_down"][1], 1)

    dx3, df1, dg_ffn_post1, loss_cols = _final_loss_bwd(x3, f1, gains["norm_ffn_post"][1], target, name="loss_bwd")
    dh3, d_wgu1, d_wd1 = _ffn_bwd(h3, gu1, act1, p["ffn_w_gate_up"][1], p["ffn_w_down"][1], df1, 1)
    dx2, dmo1, dg_mix_post1, dg_ffn_pre1 = _post_pre_bwd(x2, mo1, gains["norm_mix_post"][1], gains["norm_ffn_pre"][1],
                                                         dx3, dh3, name="post_mix1_bwd")
    dcat1 = _mm(dmo1, p["dn_w_out"], tb=True, name="dn_out_dx")
    g["dn_w_out"] = _mm(cat1, dmo1, ta=True, name="dn_out_dw")
    d_o, dz, dqm1, d_out_gain, dkv1 = _dn_out_bwd(o_dirs, proj, out_gain, kv_mem[1], dcat1, name="dn_outnorm_bwd")
    dq_s, dk_s, dv_s, dgl, dbeta = _dn_scan_bwd(qkv, gl, beta, states, d_o, name="dn_scan_bwd")
    d_gate_cols, d_a_wide, d_dt_wide = _dn_gates_bwd(proj, a_wide, dt_wide, dgl, dbeta, name="dn_gates_bwd")
    d_pre, d_conv = zip(*[_dn_conv_bwd(proj, p["dn_conv"], d_s, which, name=f"dn_conv_bwd{which}")
                          for which, d_s in enumerate((dq_s, dk_s, dv_s))])
    dproj = jnp.concatenate(list(d_pre) + [dz, dqm1, d_gate_cols], axis=1)
    dh2 = _mm(dproj, p["dn_w_in"], tb=True, name="dn_in_dx")
    g["dn_w_in"] = _mm(h2, dproj, ta=True, name="dn_in_dw")
    g["dn_conv"] = jnp.concatenate(d_conv, axis=1)
    g["dn_a_log"] = _wide_bwd(d_a_wide)
    g["dn_dt_bias"] = _wide_bwd(d_dt_wide)
    g["dn_out_norm"] = d_out_gain

    dx1, df0, dg_ffn_post0, dg_mix_pre1 = _post_pre_bwd(x1, f0, gains["norm_ffn_post"][0], gains["norm_mix_pre"][1],
                                                        dx2, dh2, name="post_ffn0_bwd")
    dh1, d_wgu0, d_wd0 = _ffn_bwd(h1, gu0, act0, p["ffn_w_gate_up"][0], p["ffn_w_down"][0], df0, 0)
    dx0, dmo0, dg_mix_post0, dg_ffn_pre0 = _post_pre_bwd(x, mo0, gains["norm_mix_post"][0], gains["norm_ffn_pre"][0],
                                                         dx1, dh1, name="post_mix0_bwd")
    dcat0 = _mm(dmo0, p["att_w_out"], tb=True, name="att_out_dx")
    g["att_w_out"] = _mm(cat0, dmo0, ta=True, name="att_out_dw")
    delta, dqm0, dkv0 = _att_bwd_prep(cat0, dcat0, qkvm, kv_mem[0], name="att_bwd_prep")
    att_b = [_att_bwd(qkvm, bias[gi], lse_tot, delta, dcat0, gi, name=f"att_bwd{gi}") for gi in range(3)]
    dqkvm = jnp.concatenate([a[w] for w in range(3) for a in att_b] + [dqm0], axis=1)
    g["rel_bias"] = jnp.concatenate([_bias_tiles_bwd(att_b[gi][3], gi) for gi in range(3)], axis=1)
    dh0 = _mm(dqkvm, p["att_w_in"], tb=True, name="att_in_dx")
    g["att_w_in"] = _mm(h0, dqkvm, ta=True, name="att_in_dw")
    grad_x, dg_mix_pre0 = _pre_norm_bwd(x, gains["norm_mix_pre"][0], dh0, dx0, name="pre0_bwd")

    d_mem_kv, d_mem_norm = [], []
    for i, dkv in enumerate((dkv0, dkv1)):
        d_mem_kv.append(_mm(mem_n[i], dkv, ta=True, name=f"mem_kv_dw{i}"))
        d_mem_n = _mm(dkv, p["mem_w_kv"][i], tb=True, name=f"mem_kv_dx{i}")
        d_mem_norm.append(_gain_bwd(mem, gains["mem_norm"][i], d_mem_n, name=f"mem_norm_bwd{i}"))
    g["mem_w_kv"] = jnp.stack(d_mem_kv)
    g["mem_norm"] = jnp.concatenate(d_mem_norm, axis=0)
    g["norm_mix_pre"] = jnp.concatenate([dg_mix_pre0, dg_mix_pre1], axis=0)
    g["norm_mix_post"] = jnp.concatenate([dg_mix_post0, dg_mix_post1], axis=0)
    g["norm_ffn_pre"] = jnp.concatenate([dg_ffn_pre0, dg_ffn_pre1], axis=0)
    g["norm_ffn_post"] = jnp.concatenate([dg_ffn_post0, dg_ffn_post1], axis=0)
    g["ffn_w_gate_up"] = jnp.stack([d_wgu0, d_wgu1])
    g["ffn_w_down"] = jnp.stack([d_wd0, d_wd1])
    return loss_cols, grad_x, g


N_CHIPS = 4
N_DEV = 8
MESH = pl.DeviceIdType.MESH
BIG = (("att_w_in", (1, 1024, 640), 2), ("att_w_out", (1, 256, 1024), 1), ("dn_w_in", (1, 1024, 838), 2),
       ("dn_w_out", (1, 256, 1024), 1), ("mem_w_kv", (2, 256, 512), 1), ("ffn_w_gate_up", (2, 1024, 1408), 2),
       ("ffn_w_down", (2, 704, 1024), 1))
PACK_COLS = 1024
PACK_ELEMS = sum(math.prod(shape) for _, shape, _ in BIG)
HALF_ROWS = -(-PACK_ELEMS // (2 * PACK_COLS * 16)) * 16
PACK_ROWS = 2 * HALF_ROWS
HALF_TM = _tile(HALF_ROWS, 512, 16)


def _pack_local(blocks):
    flat = [blocks[n].reshape(-1) for n, _, _ in BIG]
    dtype = flat[0].dtype
    flat.append(jnp.zeros((PACK_ROWS * PACK_COLS - PACK_ELEMS,), dtype))
    return jnp.concatenate(flat).reshape(PACK_ROWS, PACK_COLS)


def _unpack_local(slab):
    flat = slab.reshape(-1)
    out, off = {}, 0
    for n, shape, _ in BIG:
        size = math.prod(shape)
        out[n] = flat[off: off + size].reshape(shape)
        off += size
    return out


def _full_from_gathered(gathered):
    per_chip = [_unpack_local(gathered[s]) for s in range(N_CHIPS)]
    return {n: jnp.concatenate([pc[n] for pc in per_chip], axis=axis) for n, _, axis in BIG}


def _pack_full(full):
    slabs = []
    for s in range(N_CHIPS):
        blocks = {}
        for n, shape, axis in BIG:
            blocks[n] = lax.slice_in_dim(full[n], s * shape[axis], (s + 1) * shape[axis], axis=axis)
        slabs.append(_pack_local(blocks))
    return jnp.stack(slabs)


def _mesh_pos():
    return lax.axis_index("x"), lax.axis_index("y"), lax.axis_index("c")


def _other_chips(x, y):
    return [(1 - x, y), (x, 1 - y), (1 - x, 1 - y)]


ANY = pl.BlockSpec(memory_space=pl.ANY)


def _gather_shards(slab, *, name):
    rows, cols = slab.shape
    half = rows // 2

    def body(x_ref, out_ref, send_sems, recv_sems, local_sem):
        x, y, c = _mesh_pos()
        me = 2 * x + y
        sibling = (x, y, 1 - c)
        chips = _other_chips(x, y)

        def part(chip, h):
            return out_ref.at[2 * chip[0] + chip[1], pl.ds(h * half, half), :]

        def copy(k, src, dst, to):
            return pltpu.make_async_remote_copy(src_ref=src, dst_ref=dst, send_sem=send_sems.at[k],
                                                recv_sem=recv_sems.at[k], device_id=to, device_id_type=MESH)

        mine = pltpu.make_async_copy(x_ref, out_ref.at[me], local_sem)
        mine.start()
        my_half = x_ref.at[pl.ds(c * half, half), :]
        first = [copy(j, my_half, part((x, y), c), (*chip, c)) for j, chip in enumerate(chips)]
        for cp in first:
            cp.start()
        passed = [copy(3 + j, part(chip, c), part(chip, c), sibling) for j, chip in enumerate(chips)]
        for j, chip in enumerate(chips):
            copy(j, my_half, part(chip, c), (*chip, c)).wait_recv()
            passed[j].start()
        for j, chip in enumerate(chips):
            copy(3 + j, part(chip, 1 - c), part(chip, 1 - c), sibling).wait_recv()
        for cp in first + passed:
            cp.wait_send()
        mine.wait()

    return pl.pallas_call(
        body, name=name, in_specs=[ANY], out_specs=ANY,
        out_shape=jax.ShapeDtypeStruct((N_CHIPS, rows, cols), slab.dtype),
        scratch_shapes=[pltpu.SemaphoreType.DMA((6,)), pltpu.SemaphoreType.DMA((6,)), pltpu.SemaphoreType.DMA],
    )(slab)


def _swap_other_half(slabs, *, name):
    n, rows, cols = slabs.shape
    half = rows // 2

    def body(g_ref, out_ref, send_sem, recv_sem):
        x, y, c = _mesh_pos()
        cp = pltpu.make_async_remote_copy(src_ref=g_ref.at[:, pl.ds((1 - c) * half, half), :], dst_ref=out_ref,
                                          send_sem=send_sem, recv_sem=recv_sem, device_id=(x, y, 1 - c), device_id_type=MESH)
        cp.start()
        cp.wait()

    return pl.pallas_call(
        body, name=name, in_specs=[ANY], out_specs=ANY, out_shape=jax.ShapeDtypeStruct((n, half, cols), slabs.dtype),
        scratch_shapes=[pltpu.SemaphoreType.DMA, pltpu.SemaphoreType.DMA],
    )(slabs)


def _add_own_half(slabs, received, core, *, name):
    n, rows, cols = slabs.shape
    half = rows // 2
    nb = half // HALF_TM

    def kern(c_ref, a_ref, b_ref, o_ref):
        o_ref[...] = a_ref[...] + b_ref[...]

    return pl.pallas_call(
        kern, name=name,
        grid_spec=pltpu.PrefetchScalarGridSpec(
            num_scalar_prefetch=1, grid=(n, nb),
            in_specs=[pl.BlockSpec((None, HALF_TM, cols), lambda s, i, c: (s, c[0] * nb + i, 0)),
                      pl.BlockSpec((None, HALF_TM, cols), lambda s, i, c: (s, i, 0))],
            out_specs=pl.BlockSpec((None, HALF_TM, cols), lambda s, i, c: (s, i, 0))),
        out_shape=jax.ShapeDtypeStruct((n, half, cols), F32),
        compiler_params=_params("parallel", "parallel"),
    )(core, slabs, received)


def _scatter_to_chips(sums, *, name):
    n, half, cols = sums.shape

    def body(s_ref, out_ref, send_sems, recv_sems, local_sem):
        x, y, c = _mesh_pos()
        me = 2 * x + y
        chips = _other_chips(x, y)

        def copy(k, chip):
            return pltpu.make_async_remote_copy(src_ref=s_ref.at[2 * chip[0] + chip[1]], dst_ref=out_ref.at[me],
                                                send_sem=send_sems.at[k], recv_sem=recv_sems.at[k],
                                                device_id=(*chip, c), device_id_type=MESH)

        mine = pltpu.make_async_copy(s_ref.at[me], out_ref.at[me], local_sem)
        mine.start()
        sends = [copy(j, chip) for j, chip in enumerate(chips)]
        for cp in sends:
            cp.start()
        for j, chip in enumerate(chips):
            pltpu.make_async_remote_copy(src_ref=s_ref.at[me], dst_ref=out_ref.at[2 * chip[0] + chip[1]],
                                         send_sem=send_sems.at[j], recv_sem=recv_sems.at[j],
                                         device_id=(*chip, c), device_id_type=MESH).wait_recv()
        for cp in sends:
            cp.wait_send()
        mine.wait()

    return pl.pallas_call(
        body, name=name, in_specs=[ANY], out_specs=ANY, out_shape=jax.ShapeDtypeStruct((n, half, cols), sums.dtype),
        scratch_shapes=[pltpu.SemaphoreType.DMA((3,)), pltpu.SemaphoreType.DMA((3,)), pltpu.SemaphoreType.DMA],
    )(sums)


def _sum_chips(parts, *, name):
    n, half, cols = parts.shape

    def kern(p_ref, o_ref):
        acc = p_ref[0]
        for s in range(1, n):
            acc = acc + p_ref[s]
        o_ref[...] = acc

    return pl.pallas_call(
        kern, name=name, grid=(half // HALF_TM,),
        in_specs=[pl.BlockSpec((n, HALF_TM, cols), lambda i: (0, i, 0))],
        out_specs=pl.BlockSpec((HALF_TM, cols), lambda i: (i, 0)),
        out_shape=jax.ShapeDtypeStruct((half, cols), F32),
        compiler_params=_params("parallel"),
    )(parts)


def _join_halves(mine, *, name):
    half, cols = mine.shape

    def body(m_ref, out_ref, send_sem, recv_sem, local_sem):
        x, y, c = _mesh_pos()
        own = out_ref.at[pl.ds(c * half, half), :]
        local = pltpu.make_async_copy(m_ref, own, local_sem)
        local.start()
        cp = pltpu.make_async_remote_copy(src_ref=m_ref, dst_ref=own, send_sem=send_sem, recv_sem=recv_sem,
                                          device_id=(x, y, 1 - c), device_id_type=MESH)
        cp.start()
        pltpu.make_async_remote_copy(src_ref=m_ref, dst_ref=out_ref.at[pl.ds((1 - c) * half, half), :], send_sem=send_sem,
                                     recv_sem=recv_sem, device_id=(x, y, 1 - c), device_id_type=MESH).wait_recv()
        cp.wait_send()
        local.wait()

    return pl.pallas_call(
        body, name=name, in_specs=[ANY], out_specs=ANY, out_shape=jax.ShapeDtypeStruct((2 * half, cols), mine.dtype),
        scratch_shapes=[pltpu.SemaphoreType.DMA, pltpu.SemaphoreType.DMA, pltpu.SemaphoreType.DMA],
    )(mine)


def _all_reduce_small(v, *, name):
    rows, cols = v.shape
    flips = [(dx, dy, dc) for dx in (0, 1) for dy in (0, 1) for dc in (0, 1)][1:]

    def body(v_ref, o_ref, buf, send_sems, recv_sems):
        x, y, c = _mesh_pos()

        def peer(f):
            return tuple(1 - p if fl else p for p, fl in zip((x, y, c), f))

        def index(p):
            return 4 * p[0] + 2 * p[1] + p[2]

        buf[index((x, y, c))] = v_ref[...]
        sends = []
        for k, f in enumerate(flips):
            cp = pltpu.make_async_remote_copy(src_ref=v_ref, dst_ref=buf.at[index((x, y, c))], send_sem=send_sems.at[k],
                                              recv_sem=recv_sems.at[k], device_id=peer(f), device_id_type=MESH)
            cp.start()
            sends.append(cp)
        for k, f in enumerate(flips):
            pltpu.make_async_remote_copy(src_ref=v_ref, dst_ref=buf.at[index(peer(f))], send_sem=send_sems.at[k],
                                         recv_sem=recv_sems.at[k], device_id=peer(f), device_id_type=MESH).wait_recv()
        for cp in sends:
            cp.wait_send()
        acc = buf[0]
        for d in range(1, N_DEV):
            acc = acc + buf[d]
        o_ref[...] = acc

    vmem = pl.BlockSpec(memory_space=pltpu.VMEM)
    return pl.pallas_call(
        body, name=name, in_specs=[vmem], out_specs=vmem, out_shape=jax.ShapeDtypeStruct((rows, cols), F32),
        scratch_shapes=[pltpu.VMEM((N_DEV, rows, cols), F32), pltpu.SemaphoreType.DMA((N_DEV - 1,)),
                        pltpu.SemaphoreType.DMA((N_DEV - 1,))],
    )(v)


def _reduce_scatter(slabs, core):
    received = _swap_other_half(slabs, name="rs_swap")
    sums = _add_own_half(slabs, received, core, name="rs_add")
    parts = _scatter_to_chips(sums, name="rs_scatter")
    return _join_halves(_sum_chips(parts, name="rs_sum"), name="rs_join")


def _adamw(w, g, m, v, *, name):
    def body(w, g, m, v):
        m = ADAM_B1 * m + (1.0 - ADAM_B1) * g
        v = ADAM_B2 * v + (1.0 - ADAM_B2) * (g * g)
        m_hat = m / (1.0 - ADAM_B1 ** ADAM_STEP)
        v_hat = v / (1.0 - ADAM_B2 ** ADAM_STEP)
        delta = -ADAM_LR * (m_hat / (jnp.sqrt(v_hat) + ADAM_EPS) + ADAM_WD * w)
        return (delta, m, v), ()
    rows, cols = w.shape
    return _rowwise(body, [w, g, m, v], [], [(cols, F32)] * 3, [], tm=_tile(rows, 256, SUBLANES), name=name)


def _pack_small(arrs, rows):
    flat = jnp.concatenate([a.reshape(-1) for a in arrs])
    return jnp.pad(flat, (0, rows * LANES - flat.shape[0])).reshape(rows, LANES)


def _unpack_small(packed, shapes):
    flat = packed.reshape(-1)
    out, off = [], 0
    for s in shapes:
        size = math.prod(s)
        out.append(flat[off: off + size].reshape(s))
        off += size
    return out


def _small_rows(shapes):
    return -(-sum(math.prod(s) for s in shapes) // (SUBLANES * LANES)) * SUBLANES


WEIGHTS = ("rel_bias", "att_w_in", "att_w_out", "dn_w_in", "dn_conv", "dn_a_log", "dn_dt_bias", "dn_out_norm", "dn_w_out",
           "mem_norm", "mem_w_kv", "norm_mix_pre", "norm_mix_post", "norm_ffn_pre", "norm_ffn_post", "ffn_w_gate_up",
           "ffn_w_down")
BIG_NAMES = tuple(n for n, _, _ in BIG)
SMALL_NAMES = tuple(n for n in WEIGHTS if n not in BIG_NAMES)
CONV_COLS = 3 * TOK_WIDTH
CONV_SHARD = CONV_COLS // N_CHIPS


def kernel(x, mem, rel_bias, att_w_in, att_w_out, dn_w_in, dn_conv, dn_a_log, dn_dt_bias, dn_out_norm, dn_w_out, mem_norm, mem_w_kv, norm_mix_pre, norm_mix_post, norm_ffn_pre, norm_ffn_post, ffn_w_gate_up, ffn_w_down, loss_target, m_rel_bias, m_att_w_in, m_att_w_out, m_dn_w_in, m_dn_conv, m_dn_a_log, m_dn_dt_bias, m_dn_out_norm, m_dn_w_out, m_mem_norm, m_mem_w_kv, m_norm_mix_pre, m_norm_mix_post, m_norm_ffn_pre, m_norm_ffn_post, m_ffn_w_gate_up, m_ffn_w_down, v_rel_bias, v_att_w_in, v_att_w_out, v_dn_w_in, v_dn_conv, v_dn_a_log, v_dn_dt_bias, v_dn_out_norm, v_dn_w_out, v_mem_norm, v_mem_w_kv, v_norm_mix_pre, v_norm_mix_post, v_norm_ffn_pre, v_norm_ffn_post, v_ffn_w_gate_up, v_ffn_w_down):
    w = dict(zip(WEIGHTS, (rel_bias, att_w_in, att_w_out, dn_w_in, dn_conv, dn_a_log, dn_dt_bias, dn_out_norm, dn_w_out,
                           mem_norm, mem_w_kv, norm_mix_pre, norm_mix_post, norm_ffn_pre, norm_ffn_post, ffn_w_gate_up,
                           ffn_w_down)))
    m = dict(zip(WEIGHTS, (m_rel_bias, m_att_w_in, m_att_w_out, m_dn_w_in, m_dn_conv, m_dn_a_log, m_dn_dt_bias,
                           m_dn_out_norm, m_dn_w_out, m_mem_norm, m_mem_w_kv, m_norm_mix_pre, m_norm_mix_post,
                           m_norm_ffn_pre, m_norm_ffn_post, m_ffn_w_gate_up, m_ffn_w_down)))
    v = dict(zip(WEIGHTS, (v_rel_bias, v_att_w_in, v_att_w_out, v_dn_w_in, v_dn_conv, v_dn_a_log, v_dn_dt_bias,
                           v_dn_out_norm, v_dn_w_out, v_mem_norm, v_mem_w_kv, v_norm_mix_pre, v_norm_mix_post,
                           v_norm_ffn_pre, v_norm_ffn_post, v_ffn_w_gate_up, v_ffn_w_down)))
    cx, cy, cc = _mesh_pos()
    chip = 2 * cx + cy

    gathered = _gather_shards(_pack_local({n: w[n] for n in BIG_NAMES}).astype(MXU_DTYPE), name="gather_weights")
    full = _full_from_gathered(gathered)
    conv_rows = _small_rows([(DN_CONV, CONV_COLS)])
    conv_mine = jnp.where(cc == 0, 1.0, 0.0) * w["dn_conv"][0]
    conv_placed = lax.dynamic_update_slice(jnp.zeros((DN_CONV, CONV_COLS), F32), conv_mine, (0, chip * CONV_SHARD))
    conv_full = _unpack_small(_all_reduce_small(_pack_small([conv_placed], conv_rows), name="gather_conv"),
                              [(DN_CONV, CONV_COLS)])[0]
    p = {
        "rel_bias": w["rel_bias"], "att_w_in": full["att_w_in"][0], "att_w_out": full["att_w_out"][0],
        "dn_w_in": _pad_dn_w_in(full["dn_w_in"][0]), "dn_conv": conv_full, "dn_a_log": w["dn_a_log"][0],
        "dn_dt_bias": w["dn_dt_bias"][0], "dn_out_norm": w["dn_out_norm"][0], "dn_w_out": full["dn_w_out"][0],
        "mem_norm": w["mem_norm"], "mem_w_kv": full["mem_w_kv"], "norm_mix_pre": w["norm_mix_pre"],
        "norm_mix_post": w["norm_mix_post"], "norm_ffn_pre": w["norm_ffn_pre"], "norm_ffn_post": w["norm_ffn_post"],
        "ffn_w_gate_up": full["ffn_w_gate_up"], "ffn_w_down": full["ffn_w_down"],
    }

    loss_cols, grad_x, g = _local_step(x[0], mem[0], loss_target[0], p)
    loss = lax.psum(jnp.sum(loss_cols), ("x", "y", "c"))

    g_big = {"att_w_in": g["att_w_in"][None], "att_w_out": g["att_w_out"][None], "dn_w_in": _unpad_dn_w_in(g["dn_w_in"])[None],
             "dn_w_out": g["dn_w_out"][None], "mem_w_kv": g["mem_w_kv"], "ffn_w_gate_up": g["ffn_w_gate_up"],
             "ffn_w_down": g["ffn_w_down"]}
    reduced = _reduce_scatter(_pack_full(g_big), cc.astype(jnp.int32).reshape(1))
    grads = _unpack_local(reduced)
    small_full_shapes = [(DN_CONV, CONV_COLS) if n == "dn_conv" else w[n].shape for n in SMALL_NAMES]
    small_sum = _all_reduce_small(_pack_small([g[n] for n in SMALL_NAMES], _small_rows(small_full_shapes)), name="reduce_small")
    for n, s in zip(SMALL_NAMES, _unpack_small(small_sum, small_full_shapes)):
        grads[n] = lax.dynamic_slice(s, (0, chip * CONV_SHARD), (DN_CONV, CONV_SHARD))[None] if n == "dn_conv" else s

    delta, new_m, new_v = {}, {}, {}
    for n in BIG_NAMES:
        shape = w[n].shape
        two_d = lambda a: a.reshape(-1, shape[-1])
        res = _adamw(two_d(w[n]), two_d(grads[n]), two_d(m[n]), two_d(v[n]), name=f"adamw_{n}")
        delta[n], new_m[n], new_v[n] = (r.reshape(shape) for r in res)
    small_shapes = [w[n].shape for n in SMALL_NAMES]
    rows = _small_rows(small_shapes)
    res = _adamw(*[_pack_small([d[n] for n in SMALL_NAMES], rows) for d in (w, grads, m, v)], name="adamw_small")
    for d, r in zip((delta, new_m, new_v), res):
        for n, a in zip(SMALL_NAMES, _unpack_small(r, small_shapes)):
            d[n] = a
    return (loss, grad_x[None], *[grads[n] for n in WEIGHTS], *[delta[n] for n in WEIGHTS],
            *[new_m[n] for n in WEIGHTS], *[new_v[n] for n in WEIGHTS])
```

```python
import functools
import math

import numpy as np
import jax
import jax.numpy as jnp
from jax import lax
from jax.experimental import pallas as pl
from jax.experimental.pallas import tpu as pltpu

F32 = jnp.float32
MXU_DTYPE = jnp.bfloat16
HI = lax.Precision.HIGHEST

EPS = 1e-6
NEG_INF = -1e30
LANES = 128
SUBLANES = 8
VMEM_LIMIT = 56 * 1024 * 1024

D_MODEL = 1024
TOK_WIDTH = 768
MEM_WIDTH = 256
MEM_LEN = 256
ATT_HEAD_DIM = 64
DILATIONS = (1, 4, 16)
HALF = 64
ATT_BQ = 128
ATT_W = ATT_BQ + 2 * HALF
REL_BUCKETS = 32
REL_MAX_DIST = 1024
DN_HEADS = 6
DN_HEAD_DIM = 128
DN_CONV = 5
DN_CHUNK = 128
D_FF = 2816
ATT_IN = 2560
DN_IN = 3352
DN_IN_PAD = 3456
N_GATES = 4 * DN_HEADS

ADAM_LR = 0.001
ADAM_B1 = 0.9
ADAM_B2 = 0.999
ADAM_EPS = 1e-08
ADAM_WD = 0.01
ADAM_STEP = 10


def _tile(n, target, align):
    if n <= target:
        return n
    t = (target // align) * align
    while t >= align:
        if n % t == 0:
            return t
        t -= align
    raise ValueError(f"no tile for {n} (target {target}, align {align})")


def _params(*sem):
    return pltpu.CompilerParams(dimension_semantics=sem, vmem_limit_bytes=VMEM_LIMIT)


def _mm(a, b, *, name, ta=False, tb=False, tm=512, tn=512, tk=1024, out_dtype=F32):
    if ta:
        K, M = a.shape
    else:
        M, K = a.shape
    if tb:
        N, K2 = b.shape
    else:
        K2, N = b.shape
    assert K == K2, (a.shape, b.shape, ta, tb)
    tm = _tile(M, tm, LANES if ta else SUBLANES)
    tn = _tile(N, tn, LANES)
    tk = _tile(K, tk, LANES)
    nk = K // tk
    a_spec = pl.BlockSpec((tk, tm), lambda i, j, k: (k, i)) if ta else pl.BlockSpec((tm, tk), lambda i, j, k: (i, k))
    b_spec = pl.BlockSpec((tn, tk), lambda i, j, k: (j, k)) if tb else pl.BlockSpec((tk, tn), lambda i, j, k: (k, j))
    dims = (((0 if ta else 1,), (1 if tb else 0,)), ((), ()))

    def kern(a_ref, b_ref, o_ref, acc_ref):
        k = pl.program_id(2)

        @pl.when(k == 0)
        def _():
            acc_ref[...] = jnp.zeros_like(acc_ref)

        acc_ref[...] += lax.dot_general(a_ref[...].astype(MXU_DTYPE), b_ref[...].astype(MXU_DTYPE), dims,
                                        preferred_element_type=F32)

        @pl.when(k == nk - 1)
        def _():
            o_ref[...] = acc_ref[...].astype(o_ref.dtype)

    return pl.pallas_call(
        kern, name=name, grid=(M // tm, N // tn, nk), in_specs=[a_spec, b_spec],
        out_specs=pl.BlockSpec((tm, tn), lambda i, j, k: (i, j)),
        out_shape=jax.ShapeDtypeStruct((M, N), out_dtype),
        scratch_shapes=[pltpu.VMEM((tm, tn), F32)],
        compiler_params=_params("parallel", "parallel", "arbitrary"),
    )(a, b)


def _col(arr, width, blk):
    return (arr, width, blk)


def _rowwise(body, rows, consts, out_rows, out_acc, *, tm, name):
    n_rows = (rows[0][0] if isinstance(rows[0], tuple) else rows[0]).shape[0]
    assert n_rows % tm == 0, (n_rows, tm)
    arrs, in_specs = [], []
    for r in rows:
        arr, width, blk = r if isinstance(r, tuple) else (r, r.shape[1], 0)
        assert arr.shape[0] == n_rows
        arrs.append(arr)
        in_specs.append(pl.BlockSpec((tm, width), functools.partial(lambda i, b: (i, b), b=blk)))
    for c in consts:
        arrs.append(c)
        in_specs.append(pl.BlockSpec(c.shape, functools.partial(lambda i, n: (0,) * n, n=c.ndim)))
    n_in, n_ro = len(arrs), len(out_rows)
    out_shape = [jax.ShapeDtypeStruct((n_rows, w), dt) for w, dt in out_rows]
    out_specs = [pl.BlockSpec((tm, w), lambda i: (i, 0)) for w, _ in out_rows]
    out_shape += [jax.ShapeDtypeStruct(s, F32) for s in out_acc]
    out_specs += [pl.BlockSpec(s, lambda i: (0, 0)) for s in out_acc]

    def kern(*refs):
        ro, ao = body(*[r[...] for r in refs[:n_in]])
        outs = refs[n_in:]
        for r, v in zip(outs[:n_ro], ro, strict=True):
            r[...] = v.astype(r.dtype)
        if out_acc:
            @pl.when(pl.program_id(0) == 0)
            def _():
                for r in outs[n_ro:]:
                    r[...] = jnp.zeros_like(r)

            for r, v in zip(outs[n_ro:], ao, strict=True):
                r[...] += v

    res = pl.pallas_call(
        kern, name=name, grid=(n_rows // tm,), in_specs=in_specs, out_specs=out_specs, out_shape=out_shape,
        compiler_params=_params("arbitrary" if out_acc else "parallel"),
    )(*arrs)
    return res


def _rms(x, gain):
    return x * lax.rsqrt(jnp.mean(x * x, axis=-1, keepdims=True) + EPS) * gain


def _silu(x):
    return x * jax.nn.sigmoid(x)


def _softplus(x):
    return jnp.maximum(x, 0.0) + jnp.log(1.0 + jnp.exp(-jnp.abs(x)))


def _dot_nt(a, b, precision=None):
    return lax.dot_general(a, b, (((1,), (1,)), ((), ())), preferred_element_type=F32, precision=precision)


def _dot_tn(a, b, precision=None):
    return lax.dot_general(a, b, (((0,), (0,)), ((), ())), preferred_element_type=F32, precision=precision)


def _dot(a, b, precision=None):
    return jnp.dot(a, b, preferred_element_type=F32, precision=precision)


def _pre_norm(x, gain, *, name):
    def body(x, g):
        return (_rms(x, g),), ()
    return _rowwise(body, [x], [gain], [(x.shape[1], F32)], [], tm=_tile(x.shape[0], 512, SUBLANES), name=name)[0]


def _pre_norm_bwd(x, gain, dh, dx_other, *, name):
    def body(x, dh, dxo, g):
        _, vjp = jax.vjp(_rms, x, g)
        dx, dg = vjp(dh)
        return (dx + dxo,), (dg,)
    return _rowwise(body, [x, dh, dx_other], [gain], [(x.shape[1], F32)], [gain.shape], tm=512, name=name)


def _gain_bwd(x, gain, dh, *, name):
    def body(x, dh, g):
        _, vjp = jax.vjp(lambda g_: _rms(x, g_), g)
        return (), (vjp(dh)[0],)
    return _rowwise(body, [x, dh], [gain], [], [gain.shape], tm=_tile(x.shape[0], 512, SUBLANES), name=name)[0]


def _res_block(x_res, m, g_post, g_pre):
    x_new = x_res + _rms(m, g_post)
    return x_new, _rms(x_new, g_pre)


def _post_pre(x_res, m, g_post, g_pre, *, name):
    def body(x, m, gp, gq):
        return _res_block(x, m, gp, gq), ()
    d = x_res.shape[1]
    return _rowwise(body, [x_res, m], [g_post, g_pre], [(d, F32), (d, F32)], [], tm=512, name=name)


def _post_pre_bwd(x_res, m, g_post, g_pre, dx_new, dh, *, name):
    def body(x, m, dxn, dh, gp, gq):
        _, vjp = jax.vjp(_res_block, x, m, gp, gq)
        dx, dm, dgp, dgq = vjp((dxn, dh))
        return (dx, dm), (dgp, dgq)
    d = x_res.shape[1]
    return _rowwise(body, [x_res, m, dx_new, dh], [g_post, g_pre], [(d, F32), (d, F32)],
                    [g_post.shape, g_pre.shape], tm=256, name=name)


def _final_loss_bwd(x_res, m, g_post, target, *, name):
    d = x_res.shape[1]

    def loss_cols(x, m, g, t):
        err = x + _rms(m, g) - t
        return jnp.sum(err * err, axis=0, keepdims=True) * (0.5 / d)

    def body(x, m, t, g):
        cols, vjp = jax.vjp(lambda x_, m_, g_: loss_cols(x_, m_, g_, t), x, m, g)
        dx, dm, dg = vjp(jnp.ones_like(cols))
        return (dx, dm), (dg, cols)
    return _rowwise(body, [x_res, m, target], [g_post], [(d, F32), (d, F32)], [g_post.shape, (1, d)], tm=256, name=name)


def _swiglu_act(gu, *, name):
    def body(gate, up):
        return (_silu(gate) * up,), ()
    return _rowwise(body, [_col(gu, D_FF, 0), _col(gu, D_FF, 1)], [], [(D_FF, F32)], [], tm=256, name=name)[0]


def _swiglu_act_bwd(gu, da, *, name):
    def body(gate, up, da):
        _, vjp = jax.vjp(lambda g, u: _silu(g) * u, gate, up)
        dg, du = vjp(da)
        return (jnp.concatenate([dg, du], axis=1),), ()
    return _rowwise(body, [_col(gu, D_FF, 0), _col(gu, D_FF, 1), da], [], [(2 * D_FF, F32)], [], tm=256, name=name)[0]


def _lane_head_mask(width, head_dim, head):
    lane = lax.broadcasted_iota(jnp.int32, (1, width), 1)
    return (lane // head_dim) == head


def _mem_attn_pair(q_pair, k_pair, v_pair):
    out = jnp.zeros_like(q_pair)
    for h in range(2):
        mh = _lane_head_mask(LANES, ATT_HEAD_DIM, h)
        qh = jnp.where(mh, q_pair * (ATT_HEAD_DIM ** -0.5), 0.0)
        logits = _dot_nt(qh, k_pair)
        mx = jnp.max(logits, axis=-1, keepdims=True)
        p = jnp.exp(logits - mx)
        p = p / jnp.sum(p, axis=-1, keepdims=True)
        out = out + jnp.where(mh, _dot(p, v_pair), 0.0)
    return out


def _mem_attn(q_mem, kv):
    outs = []
    for p in range(MEM_WIDTH // LANES):
        sl = slice(p * LANES, (p + 1) * LANES)
        outs.append(_mem_attn_pair(q_mem[:, sl], kv[:, sl], kv[:, MEM_WIDTH + p * LANES: MEM_WIDTH + (p + 1) * LANES]))
    return jnp.concatenate(outs, axis=1)


def _mem_attn_bwd(q_mem, kv, do):
    dqs, dks, dvs = [], [], []
    for p in range(MEM_WIDTH // LANES):
        sl = slice(p * LANES, (p + 1) * LANES)
        sv = slice(MEM_WIDTH + p * LANES, MEM_WIDTH + (p + 1) * LANES)
        _, vjp = jax.vjp(_mem_attn_pair, q_mem[:, sl], kv[:, sl], kv[:, sv])
        dq, dk, dv = vjp(do[:, sl])
        dqs.append(dq)
        dks.append(dk)
        dvs.append(dv)
    return jnp.concatenate(dqs, axis=1), jnp.concatenate(dks + dvs, axis=1)


def _t5_bucket(rel):
    half = REL_BUCKETS // 2
    max_exact = half // 2
    n = np.abs(rel)
    large = max_exact + (np.log(np.maximum(n, 1) / max_exact) / math.log(REL_MAX_DIST / max_exact)
                         * (half - max_exact)).astype(np.int64)
    large = np.minimum(large, half - 1)
    return ((rel > 0) * half + np.where(n < max_exact, n, large)).astype(np.int32)


ATT_DIAGS = ATT_BQ + ATT_W - 1


def _bias_diag_onehot(dil):
    j = np.arange(ATT_DIAGS)
    tiles = []
    for off in (-HALF, 0, HALF):
        rel = j - (ATT_BQ - 1) - HALF - off
        hot = _t5_bucket(rel * dil)[:, None] == np.arange(REL_BUCKETS)[None, :]
        tiles.append(hot & (np.abs(rel) <= HALF)[:, None])
    return np.stack(tiles).astype(np.float32)


def _toeplitz(r):
    lead = r.shape[:-1]
    a = jnp.broadcast_to(r[..., None, :], lead + (ATT_BQ, ATT_DIAGS))
    a = jnp.pad(a, [(0, 0)] * len(lead) + [(0, 0), (0, 1)])
    a = a.reshape(lead + (ATT_BQ * (ATT_DIAGS + 1),))[..., : ATT_BQ * ATT_DIAGS].reshape(lead + (ATT_BQ, ATT_DIAGS))
    return a[..., ATT_BQ - 1: ATT_BQ - 1 + ATT_W]


def _bias_tiles(rel_bias, gi):
    heads = rel_bias[:, 4 * gi: 4 * gi + 4]
    diag = jnp.einsum('tnb,bh->thn', jnp.asarray(_bias_diag_onehot(DILATIONS[gi])), heads, precision=HI)
    return _toeplitz(diag)


def _bias_tiles_bwd(rel_bias, dtiles, gi):
    return jax.vjp(lambda rb: _bias_tiles(rb, gi), rel_bias)[1](dtiles)[0]


def _att_window(i, n_sub):
    start = jnp.clip(i * ATT_BQ - HALF, 0, n_sub - ATT_W)
    off = i * ATT_BQ - HALF - start
    return pl.multiple_of(start, HALF), off


def _att_valid(off):
    q = lax.broadcasted_iota(jnp.int32, (ATT_BQ, ATT_W), 0)
    kk = lax.broadcasted_iota(jnp.int32, (ATT_BQ, ATT_W), 1)
    return jnp.abs(kk - q - HALF - off) <= HALF


def _att_tile_id(i, nq):
    return jnp.where(i == 0, 0, jnp.where(i == nq - 1, 2, 1))


def _att_fwd(qkvm, bias, gi, *, name):
    dil = DILATIONS[gi]
    s_len = qkvm.shape[0]
    n_sub = s_len // dil
    nq = n_sub // ATT_BQ
    assert n_sub % ATT_BQ == 0 and n_sub >= ATT_W
    cols = qkvm.shape[1] // LANES
    view = qkvm.reshape(n_sub, dil * qkvm.shape[1])

    def kern(q_ref, k_ref, v_ref, b_ref, o_ref, lse_ref):
        i = pl.program_id(2)
        start, off = _att_window(i, n_sub)
        valid = _att_valid(off)
        q = q_ref[...] * (ATT_HEAD_DIM ** -0.5)
        kw = k_ref[pl.ds(start, ATT_W), :]
        vw = v_ref[pl.ds(start, ATT_W), :]
        o = jnp.zeros((ATT_BQ, LANES), F32)
        lse = jnp.zeros((ATT_BQ, LANES), F32)
        for h in range(2):
            mh = _lane_head_mask(LANES, ATT_HEAD_DIM, h)
            s = _dot_nt(jnp.where(mh, q, 0.0), kw) + b_ref[h]
            s = jnp.where(valid, s, NEG_INF)
            mx = jnp.max(s, axis=-1, keepdims=True)
            p = jnp.exp(s - mx)
            den = jnp.sum(p, axis=-1, keepdims=True)
            o = jnp.where(mh, _dot(p, vw) / den, o)
            lse = jnp.where(mh, mx + jnp.log(den), lse)
        o_ref[...] = o
        lse_ref[...] = lse

    def qkv_spec(which, full):
        shape = (n_sub, LANES) if full else (ATT_BQ, LANES)
        return pl.BlockSpec(shape, lambda pr, r, i: (0 if full else i, r * cols + which * 6 + 2 * gi + pr))

    out_spec = pl.BlockSpec((ATT_BQ, LANES), lambda pr, r, i: (i, r * 2 + pr))
    o, lse = pl.pallas_call(
        kern, name=name, grid=(2, dil, nq),
        in_specs=[qkv_spec(0, False), qkv_spec(1, True), qkv_spec(2, True),
                  pl.BlockSpec((None, 2, ATT_BQ, ATT_W), lambda pr, r, i: (_att_tile_id(i, nq), pr, 0, 0))],
        out_specs=[out_spec, out_spec],
        out_shape=[jax.ShapeDtypeStruct((n_sub, dil * 2 * LANES), F32)] * 2,
        compiler_params=_params("parallel", "parallel", "arbitrary"),
    )(view, view, view, bias)
    return o.reshape(s_len, 2 * LANES), lse.reshape(s_len, 2 * LANES)


def _att_bwd(qkvm, bias, lse_tot, delta, dcat, gi, *, name):
    dil = DILATIONS[gi]
    s_len = qkvm.shape[0]
    n_sub = s_len // dil
    nq = n_sub // ATT_BQ
    cols = qkvm.shape[1] // LANES
    dcols = dcat.shape[1] // LANES
    view = qkvm.reshape(n_sub, dil * qkvm.shape[1])
    lse_v = lse_tot.reshape(n_sub, dil * 2 * LANES)
    delta_v = delta.reshape(n_sub, dil * 2 * LANES)
    dcat_v = dcat.reshape(n_sub, dil * dcat.shape[1])

    def kern(q_ref, k_ref, v_ref, b_ref, lse_ref, dl_ref, dm_ref, dq_ref, dk_ref, dv_ref, db_ref):
        r, i = pl.program_id(1), pl.program_id(2)
        start, off = _att_window(i, n_sub)
        valid = _att_valid(off)
        tile = _att_tile_id(i, nq)

        @pl.when(i == 0)
        def _():
            dk_ref[...] = jnp.zeros_like(dk_ref)
            dv_ref[...] = jnp.zeros_like(dv_ref)

        @pl.when((i == 0) & (r == 0))
        def _():
            db_ref[...] = jnp.zeros_like(db_ref)

        q = q_ref[...] * (ATT_HEAD_DIM ** -0.5)
        kw = k_ref[pl.ds(start, ATT_W), :]
        vw = v_ref[pl.ds(start, ATT_W), :]
        dm = dm_ref[...]
        lse = lse_ref[...]
        dl = dl_ref[...]
        dq = jnp.zeros((ATT_BQ, LANES), F32)
        dkw = jnp.zeros((ATT_W, LANES), F32)
        dvw = jnp.zeros((ATT_W, LANES), F32)
        for h in range(2):
            mh = _lane_head_mask(LANES, ATT_HEAD_DIM, h)
            qh = jnp.where(mh, q, 0.0)
            dmh = jnp.where(mh, dm, 0.0)
            s = _dot_nt(qh, kw) + b_ref[tile, h]
            s = jnp.where(valid, s, NEG_INF)
            lse_h = jnp.max(jnp.where(mh, lse, NEG_INF), axis=-1, keepdims=True)
            dl_h = jnp.max(jnp.where(mh, dl, NEG_INF), axis=-1, keepdims=True)
            p = jnp.exp(s - lse_h)
            ds = p * (_dot_nt(dmh, vw) - dl_h)
            dq = dq + jnp.where(mh, _dot(ds, kw), 0.0)
            dkw = dkw + _dot_tn(ds, qh)
            dvw = dvw + _dot_tn(p, dmh)
            db_ref[tile, h] += ds
        dq_ref[...] = dq * (ATT_HEAD_DIM ** -0.5)
        dk_ref[pl.ds(start, ATT_W), :] += dkw
        dv_ref[pl.ds(start, ATT_W), :] += dvw

    def qkv_spec(which, full):
        shape = (n_sub, LANES) if full else (ATT_BQ, LANES)
        return pl.BlockSpec(shape, lambda pr, r, i: (0 if full else i, r * cols + which * 6 + 2 * gi + pr))

    blk = pl.BlockSpec((ATT_BQ, LANES), lambda pr, r, i: (i, r * 2 + pr))
    full = pl.BlockSpec((n_sub, LANES), lambda pr, r, i: (0, r * 2 + pr))
    bias_spec = pl.BlockSpec((3, 2, ATT_BQ, ATT_W), lambda pr, r, i: (0, pr, 0, 0))
    sub = jax.ShapeDtypeStruct((n_sub, dil * 2 * LANES), F32)
    dq, dk, dv, db = pl.pallas_call(
        kern, name=name, grid=(2, dil, nq),
        in_specs=[qkv_spec(0, False), qkv_spec(1, True), qkv_spec(2, True), bias_spec, blk, blk,
                  pl.BlockSpec((ATT_BQ, LANES), lambda pr, r, i: (i, r * dcols + 2 * gi + pr))],
        out_specs=[blk, full, full, bias_spec],
        out_shape=[sub, sub, sub, jax.ShapeDtypeStruct(bias.shape, F32)],
        compiler_params=_params("arbitrary", "arbitrary", "arbitrary"),
    )(view, view, view, bias, lse_v, delta_v, dcat_v)
    return dq.reshape(s_len, -1), dk.reshape(s_len, -1), dv.reshape(s_len, -1), db


def _att_combine(o_g, lse_g, qkvm, kv_mem, *, name):
    def body(o0, o1, o2, l0, l1, l2, qm, kv):
        mx = jnp.maximum(jnp.maximum(l0, l1), l2)
        tot = mx + jnp.log(jnp.exp(l0 - mx) + jnp.exp(l1 - mx) + jnp.exp(l2 - mx))
        mixed = [o * jnp.exp(l - tot) for o, l in ((o0, l0), (o1, l1), (o2, l2))]
        return (jnp.concatenate(mixed + [_mem_attn(qm, kv)], axis=1), tot), ()
    return _rowwise(body, list(o_g) + list(lse_g) + [_col(qkvm, MEM_WIDTH, (3 * TOK_WIDTH) // MEM_WIDTH)], [kv_mem],
                    [(D_MODEL, F32), (MEM_WIDTH, F32)], [], tm=256, name=name)


def _head_sum_matrix():
    a = np.arange(MEM_WIDTH)
    return jnp.asarray((a[:, None] // ATT_HEAD_DIM == a[None, :] // ATT_HEAD_DIM).astype(np.float32))


def _att_bwd_prep(cat, dcat, qkvm, kv_mem, *, name):
    def body(cat, dcat, qm, kv, hs):
        prod = cat * dcat
        summed = prod[:, 0:256] + prod[:, 256:512] + prod[:, 512:768]
        delta = _dot(summed, hs, precision=HI)
        dqm, dkv = _mem_attn_bwd(qm, kv, dcat[:, TOK_WIDTH:])
        return (delta, dqm), (dkv,)
    return _rowwise(body, [cat, dcat, _col(qkvm, MEM_WIDTH, (3 * TOK_WIDTH) // MEM_WIDTH)], [kv_mem, _head_sum_matrix()],
                    [(MEM_WIDTH, F32), (MEM_WIDTH, F32)], [kv_mem.shape], tm=256, name=name)


def _dn_conv_post(s, j):
    scale = jnp.where(j < DN_HEADS, DN_HEAD_DIM ** -0.5, 1.0)
    normed = s * lax.rsqrt(jnp.sum(s * s, axis=-1, keepdims=True) + EPS) * scale
    return jnp.where(j >= 2 * DN_HEADS, s, normed)


def _shift_rows(x, sh):
    n = x.shape[0]
    row = lax.broadcasted_iota(jnp.int32, (n, 1), 0)
    rolled = pltpu.roll(x, (-sh) % n, 0)
    return jnp.where((row + sh >= 0) & (row + sh < n), rolled, 0.0)


def _dn_conv_taps(x, w_ref):
    c = x * w_ref[pl.ds(DN_CONV // 2, 1), :]
    for jj in range(DN_CONV):
        if jj != DN_CONV // 2:
            c = c + _shift_rows(x, jj - DN_CONV // 2) * w_ref[pl.ds(jj, 1), :]
    return c


def _dn_conv_fwd(proj, conv_w, *, name):
    s_len = proj.shape[0]
    width = 3 * TOK_WIDTH

    def kern(x_ref, w_ref, o_ref):
        j = pl.program_id(0)
        o_ref[...] = _dn_conv_post(_silu(_dn_conv_taps(x_ref[...], w_ref)), j)

    return pl.pallas_call(
        kern, name=name, grid=(width // LANES,),
        in_specs=[pl.BlockSpec((s_len, LANES), lambda j: (0, j)), pl.BlockSpec((DN_CONV, LANES), lambda j: (0, j))],
        out_specs=pl.BlockSpec((s_len, LANES), lambda j: (0, j)),
        out_shape=jax.ShapeDtypeStruct((s_len, width), F32),
        compiler_params=_params("parallel"),
    )(proj, conv_w)


def _dn_conv_bwd(proj, conv_w, d_fwd, d_bwd, which, *, name):
    s_len = proj.shape[0]

    def kern(x_ref, w_ref, df_ref, db_ref, dx_ref, dw_ref):
        j = pl.program_id(0) + which * DN_HEADS
        x = x_ref[...]
        c = _dn_conv_taps(x, w_ref)
        _, vjp = jax.vjp(lambda c_: _dn_conv_post(_silu(c_), j), c)
        dc = vjp(df_ref[...] + db_ref[...])[0]
        dx = dc * w_ref[pl.ds(DN_CONV // 2, 1), :]
        for jj in range(DN_CONV):
            sh = jj - DN_CONV // 2
            if sh != 0:
                dx = dx + _shift_rows(dc, -sh) * w_ref[pl.ds(jj, 1), :]
            dw_ref[pl.ds(jj, 1), :] = jnp.sum(dc * _shift_rows(x, sh), axis=0, keepdims=True)
        dx_ref[...] = dx

    return pl.pallas_call(
        kern, name=name, grid=(DN_HEADS,),
        in_specs=[pl.BlockSpec((s_len, LANES), lambda j: (0, j + which * DN_HEADS)),
                  pl.BlockSpec((DN_CONV, LANES), lambda j: (0, j + which * DN_HEADS)),
                  pl.BlockSpec((s_len, LANES), lambda j: (0, j)),
                  pl.BlockSpec((s_len, LANES), lambda j: (0, j))],
        out_specs=[pl.BlockSpec((s_len, LANES), lambda j: (0, j)), pl.BlockSpec((DN_CONV, LANES), lambda j: (0, j))],
        out_shape=[jax.ShapeDtypeStruct((s_len, TOK_WIDTH), F32), jax.ShapeDtypeStruct((DN_CONV, TOK_WIDTH), F32)],
        compiler_params=_params("parallel"),
    )(proj, conv_w, d_fwd, d_bwd)


GATE_TM = 2 * DN_CHUNK


def _gate_constants():
    sel_f = np.zeros((LANES, 2 * TOK_WIDTH), np.float32)
    sel_r = np.zeros((LANES, 2 * TOK_WIDTH), np.float32)
    sel_b = np.zeros((LANES, 2 * TOK_WIDTH), np.float32)
    for d, sel in enumerate((sel_f, sel_r)):
        for h in range(DN_HEADS):
            cols = slice((d * DN_HEADS + h) * DN_HEAD_DIM, (d * DN_HEADS + h + 1) * DN_HEAD_DIM)
            sel[d * 2 * DN_HEADS + h, cols] = 1.0
            sel_b[d * 2 * DN_HEADS + DN_HEADS + h, cols] = 1.0
    i = np.arange(GATE_TM)
    same = (i[:, None] // DN_CHUNK) == (i[None, :] // DN_CHUNK)
    cum_f = same & (i[None, :] <= i[:, None])
    cum_r = same & (i[None, :] >= i[:, None])
    return tuple(jnp.asarray(np.asarray(a, np.float32)) for a in (sel_f, sel_r, sel_b, cum_f, cum_r, same))


def _gate_params(p):
    z = jnp.zeros((DN_HEADS,), F32)
    return jnp.concatenate([p[0], z, p[1], z, jnp.zeros((LANES - N_GATES,), F32)]).reshape(1, LANES)


def _gate_params_bwd(dp):
    return jnp.stack([dp[0, 0:DN_HEADS], dp[0, 2 * DN_HEADS: 3 * DN_HEADS]])


def _dn_gates(gate_in, a_cols, dt_cols, sel_f, sel_r, sel_b, cum_f, cum_r, tot):
    g = -jnp.exp(a_cols) * _softplus(gate_in + dt_cols)
    gc = _dot(_dot(cum_f, g, precision=HI), sel_f, precision=HI) + _dot(_dot(cum_r, g, precision=HI), sel_r, precision=HI)
    g_tot = _dot(_dot(tot, g, precision=HI), sel_f + sel_r, precision=HI)
    beta = jax.nn.sigmoid(_dot(gate_in, sel_b, precision=HI))
    return gc, g_tot, beta


def _dn_gates_fwd(proj, a_cols, dt_cols, *, name):
    def body(gi, *consts):
        return _dn_gates(gi, *consts), ()
    w = 2 * TOK_WIDTH
    return _rowwise(body, [_col(proj, LANES, DN_IN_PAD // LANES - 1)], [a_cols, dt_cols, *_gate_constants()],
                    [(w, F32)] * 3, [], tm=GATE_TM, name=name)


def _dn_gates_bwd(proj, a_cols, dt_cols, d_gc, d_tot, d_beta, *, name):
    def body(gi, gcf, gcr, gtf, gtr, bf, br, a, dt, *consts):
        _, vjp = jax.vjp(lambda gi_, a_, dt_: _dn_gates(gi_, a_, dt_, *consts), gi, a, dt)
        cat = lambda f, r: jnp.concatenate([f, r], axis=1)
        dgi, da, ddt = vjp((cat(gcf, gcr), cat(gtf, gtr), cat(bf, br)))
        return (dgi,), (da, ddt)
    return _rowwise(body, [_col(proj, LANES, DN_IN_PAD // LANES - 1), *d_gc, *d_tot, *d_beta],
                    [a_cols, dt_cols, *_gate_constants()], [(LANES, F32)], [a_cols.shape, dt_cols.shape],
                    tm=GATE_TM, name=name)


INV_BASE = 8


def _block_id_equal(c, size):
    i = lax.broadcasted_iota(jnp.int32, (c, c), 0) // size
    j = lax.broadcasted_iota(jnp.int32, (c, c), 1) // size
    return (i == j).astype(F32)


def _unit_tri_inverse_impl(lmat):
    c = lmat.shape[0]
    eye = _block_id_equal(c, 1)
    same = _block_id_equal(c, INV_BASE)
    neg = -lmat * same
    inv = eye + neg
    power = neg
    for _ in range(int(math.log2(INV_BASE)) - 1):
        power = _dot(power, power)
        inv = inv + _dot(inv, power)
    size = INV_BASE
    while size < c:
        bigger = _block_id_equal(c, 2 * size)
        inv = inv - _dot(_dot(inv, lmat * (bigger - same)), inv)
        same, size = bigger, 2 * size
    resid = eye - _dot(eye + lmat, inv, precision=HI)
    return inv + _dot(inv, resid)


@jax.custom_vjp
def _unit_tri_inverse(lmat):
    return _unit_tri_inverse_impl(lmat)


def _unit_tri_inverse_fwd(lmat):
    inv = _unit_tri_inverse_impl(lmat)
    return inv, inv


def _unit_tri_inverse_bwd(inv, d_inv):
    return (-_dot_tn(inv, _dot_nt(d_inv, inv)),)


_unit_tri_inverse.defvjp(_unit_tri_inverse_fwd, _unit_tri_inverse_bwd)


def _dn_chunk(q, k, v, gc, g_tot, beta, state, tri, inverse):
    c = q.shape[0]
    assert c == DN_HEAD_DIM
    eye = _block_id_equal(c, 1)
    decay = jnp.exp(jnp.where(tri > 0, gc - gc.T, NEG_INF))
    k_beta = k * beta
    inv = inverse((tri - eye) * (_dot_nt(k_beta, k) * decay))
    e_gc = jnp.exp(gc)
    u = _dot(inv, v * beta)
    w = _dot(inv, k_beta * e_gc)
    intra = tri * (_dot_nt(q, k) * decay)
    v_new = u - _dot(w, state)
    out = _dot(q * e_gc, state) + _dot(intra, v_new)
    state = state * jnp.exp(g_tot) + _dot_tn(k * jnp.exp(g_tot - gc), v_new)
    return out, state


def _dn_tri():
    i = np.arange(DN_CHUNK)
    return jnp.asarray(np.stack([(i[None, :] <= i[:, None]), (i[None, :] >= i[:, None])]).astype(np.float32))


def _dn_row_spec(nc, col, reverse):
    return pl.BlockSpec((DN_CHUNK, TOK_WIDTH), lambda t: ((nc - 1 - t) if reverse else t, col))


def _dn_state_spec(nc, reverse):
    return pl.BlockSpec((None, DN_HEADS, DN_HEAD_DIM, DN_HEAD_DIM), lambda t: ((nc - 1 - t) if reverse else t, 0, 0, 0))


def _head_cols(h):
    return pl.ds(h * DN_HEAD_DIM, DN_HEAD_DIM)


def _dn_scan_fwd(qkv, gc, g_tot, beta, *, name):
    s_len = qkv.shape[0]
    nc = s_len // DN_CHUNK

    def kern(*refs):
        ins, (tri_ref, of_ref, or_ref, sf_ref, sr_ref, state) = refs[:12], refs[12:]

        @pl.when(pl.program_id(0) == 0)
        def _():
            state[...] = jnp.zeros_like(state)

        for d, (o_ref, st_ref) in enumerate(((of_ref, sf_ref), (or_ref, sr_ref))):
            q_ref, k_ref, v_ref, gc_ref, gt_ref, b_ref = ins[6 * d: 6 * d + 6]
            tri = tri_ref[d]
            for h in range(DN_HEADS):
                sl = _head_cols(h)
                entry = state[d * DN_HEADS + h]
                st_ref[h] = entry
                out, new = _dn_chunk(q_ref[:, sl], k_ref[:, sl], v_ref[:, sl], gc_ref[:, sl], gt_ref[:, sl], b_ref[:, sl],
                                     entry, tri, _unit_tri_inverse_impl)
                o_ref[:, sl] = out
                state[d * DN_HEADS + h] = new

    in_specs = [_dn_row_spec(nc, col, rev) for rev in (False, True) for col in (0, 1, 2, int(rev), int(rev), int(rev))]
    in_specs.append(pl.BlockSpec((2, DN_CHUNK, DN_CHUNK), lambda t: (0, 0, 0)))
    return pl.pallas_call(
        kern, name=name, grid=(nc,), in_specs=in_specs,
        out_specs=[_dn_row_spec(nc, 0, False), _dn_row_spec(nc, 0, True), _dn_state_spec(nc, False), _dn_state_spec(nc, True)],
        out_shape=[jax.ShapeDtypeStruct((s_len, TOK_WIDTH), F32)] * 2
        + [jax.ShapeDtypeStruct((nc, DN_HEADS, DN_HEAD_DIM, DN_HEAD_DIM), F32)] * 2,
        scratch_shapes=[pltpu.VMEM((2 * DN_HEADS, DN_HEAD_DIM, DN_HEAD_DIM), F32)],
        compiler_params=_params("arbitrary"),
    )(*([qkv, qkv, qkv, gc, g_tot, beta] * 2), _dn_tri())


def _dn_scan_bwd(qkv, gc, g_tot, beta, states, d_o, *, name):
    s_len = qkv.shape[0]
    nc = s_len // DN_CHUNK

    def kern(*refs):
        ins, tri_ref, outs, d_state = refs[:16], refs[16], refs[17:29], refs[29]

        @pl.when(pl.program_id(0) == 0)
        def _():
            d_state[...] = jnp.zeros_like(d_state)

        for d in range(2):
            q_ref, k_ref, v_ref, gc_ref, gt_ref, b_ref, st_ref, do_ref = ins[8 * d: 8 * d + 8]
            tri = tri_ref[d]
            for h in range(DN_HEADS):
                sl = _head_cols(h)
                _, vjp = jax.vjp(lambda q, k, v, gc_, gt_, b_, s: _dn_chunk(q, k, v, gc_, gt_, b_, s, tri, _unit_tri_inverse),
                                 q_ref[:, sl], k_ref[:, sl], v_ref[:, sl], gc_ref[:, sl], gt_ref[:, sl], b_ref[:, sl], st_ref[h])
                grads = vjp((do_ref[:, sl], d_state[d * DN_HEADS + h]))
                for o_ref, val in zip(outs[6 * d: 6 * d + 6], grads[:6], strict=True):
                    o_ref[:, sl] = val
                d_state[d * DN_HEADS + h] = grads[6]

    in_specs = []
    for rev in (True, False):
        in_specs += [_dn_row_spec(nc, col, rev) for col in (0, 1, 2, int(not rev), int(not rev), int(not rev))]
        in_specs += [_dn_state_spec(nc, rev), _dn_row_spec(nc, 0, rev)]
    in_specs.append(pl.BlockSpec((2, DN_CHUNK, DN_CHUNK), lambda t: (0, 0, 0)))
    res = pl.pallas_call(
        kern, name=name, grid=(nc,), in_specs=in_specs,
        out_specs=[_dn_row_spec(nc, 0, rev) for rev in (True, False) for _ in range(6)],
        out_shape=[jax.ShapeDtypeStruct((s_len, TOK_WIDTH), F32)] * 12,
        scratch_shapes=[pltpu.VMEM((2 * DN_HEADS, DN_HEAD_DIM, DN_HEAD_DIM), F32)],
        compiler_params=_params("arbitrary"),
    )(*[a for d in range(2) for a in (qkv, qkv, qkv, gc, g_tot, beta, states[d], d_o)], _dn_tri())
    return res[:6], res[6:]


def _dn_out_head(o_f, o_b, z, gain):
    o = o_f + o_b
    return o * lax.rsqrt(jnp.mean(o * o, axis=-1, keepdims=True) + EPS) * gain * _silu(z)


def _dn_out(o_fwd, o_rev, proj, gain, qkv_kv_mem, *, name):
    def body(of, ob, z, qm, g, kv):
        heads = []
        for h in range(DN_HEADS):
            sl = slice(h * DN_HEAD_DIM, (h + 1) * DN_HEAD_DIM)
            heads.append(_dn_out_head(of[:, sl], ob[:, sl], z[:, sl], g))
        return (jnp.concatenate(heads + [_mem_attn(qm, kv)], axis=1),), ()
    return _rowwise(body, [o_fwd, o_rev, _col(proj, TOK_WIDTH, 3),
                           _col(proj, MEM_WIDTH, (4 * TOK_WIDTH) // MEM_WIDTH)], [gain, qkv_kv_mem],
                    [(D_MODEL, F32)], [], tm=256, name=name)[0]


def _dn_out_bwd(o_fwd, o_rev, proj, gain, kv_mem, dcat, *, name):
    def body(of, ob, z, qm, dcat, g, kv):
        dos, dzs = [], []
        dgain = jnp.zeros_like(g)
        for h in range(DN_HEADS):
            sl = slice(h * DN_HEAD_DIM, (h + 1) * DN_HEAD_DIM)
            _, vjp = jax.vjp(_dn_out_head, of[:, sl], ob[:, sl], z[:, sl], g)
            d_of, _, dz, dg = vjp(dcat[:, sl])
            dos.append(d_of)
            dzs.append(dz)
            dgain = dgain + dg
        dqm, dkv = _mem_attn_bwd(qm, kv, dcat[:, TOK_WIDTH:])
        return (jnp.concatenate(dos, axis=1), jnp.concatenate(dzs, axis=1), dqm), (dgain, dkv)
    return _rowwise(body, [o_fwd, o_rev, _col(proj, TOK_WIDTH, 3),
                           _col(proj, MEM_WIDTH, (4 * TOK_WIDTH) // MEM_WIDTH), dcat], [gain, kv_mem],
                    [(TOK_WIDTH, F32), (TOK_WIDTH, F32), (MEM_WIDTH, F32)], [gain.shape, kv_mem.shape], tm=256, name=name)


def _pad_dn_w_in(w):
    gates = w[:, 4 * TOK_WIDTH: 4 * TOK_WIDTH + N_GATES]
    zeros = jnp.zeros((w.shape[0], DN_IN_PAD - DN_IN), w.dtype)
    return jnp.concatenate([w[:, :4 * TOK_WIDTH], w[:, 4 * TOK_WIDTH + N_GATES:], gates, zeros], axis=1)


def _unpad_dn_w_in(w):
    q_mem = w[:, 4 * TOK_WIDTH: 4 * TOK_WIDTH + MEM_WIDTH]
    gates = w[:, 4 * TOK_WIDTH + MEM_WIDTH: 4 * TOK_WIDTH + MEM_WIDTH + N_GATES]
    return jnp.concatenate([w[:, :4 * TOK_WIDTH], gates, q_mem], axis=1)


def _ffn_fwd(h, w_gu, w_d, tag):
    gu = _mm(h, w_gu, name=f"ffn_gu_{tag}")
    act = _swiglu_act(gu, name=f"ffn_act_{tag}")
    return gu, act, _mm(act, w_d, name=f"ffn_down_{tag}")


def _ffn_bwd(h, gu, act, w_gu, w_d, df, tag):
    d_act = _mm(df, w_d, tb=True, name=f"ffn_dact_{tag}")
    d_wd = _mm(act, df, ta=True, name=f"ffn_dwd_{tag}")
    d_gu = _swiglu_act_bwd(gu, d_act, name=f"ffn_dgu_{tag}")
    dh = _mm(d_gu, w_gu, tb=True, name=f"ffn_dh_{tag}")
    d_wgu = _mm(h, d_gu, ta=True, name=f"ffn_dwgu_{tag}")
    return dh, d_wgu, d_wd


def _local_step(x, mem, target, p):
    g = {}
    row = lambda v: v.reshape(1, -1)
    gains = {k: [row(p[k][i]) for i in range(2)] for k in
             ("mem_norm", "norm_mix_pre", "norm_mix_post", "norm_ffn_pre", "norm_ffn_post")}
    out_gain = row(p["dn_out_norm"])
    a_cols, dt_cols = _gate_params(p["dn_a_log"]), _gate_params(p["dn_dt_bias"])

    h0 = _pre_norm(x, gains["norm_mix_pre"][0], name="pre0")
    mem_n = [_pre_norm(mem, gains["mem_norm"][i], name=f"mem_norm{i}") for i in range(2)]
    kv_mem = [_mm(mem_n[i], p["mem_w_kv"][i], name=f"mem_kv{i}") for i in range(2)]
    qkvm = _mm(h0, p["att_w_in"], name="att_in")
    bias = [_bias_tiles(p["rel_bias"], gi) for gi in range(3)]
    att = [_att_fwd(qkvm, bias[gi], gi, name=f"att_fwd{gi}") for gi in range(3)]
    cat0, lse_tot = _att_combine([a[0] for a in att], [a[1] for a in att], qkvm, kv_mem[0], name="att_combine")
    mo0 = _mm(cat0, p["att_w_out"], name="att_out")
    x1, h1 = _post_pre(x, mo0, gains["norm_mix_post"][0], gains["norm_ffn_pre"][0], name="post_mix0")
    gu0, act0, f0 = _ffn_fwd(h1, p["ffn_w_gate_up"][0], p["ffn_w_down"][0], 0)
    x2, h2 = _post_pre(x1, f0, gains["norm_ffn_post"][0], gains["norm_mix_pre"][1], name="post_ffn0")

    proj = _mm(h2, p["dn_w_in"], name="dn_in")
    qkv = _dn_conv_fwd(proj, p["dn_conv"], name="dn_conv")
    gc, g_tot, beta = _dn_gates_fwd(proj, a_cols, dt_cols, name="dn_gates")
    o_fwd, o_rev, st_fwd, st_rev = _dn_scan_fwd(qkv, gc, g_tot, beta, name="dn_scan")
    cat1 = _dn_out(o_fwd, o_rev, proj, out_gain, kv_mem[1], name="dn_outnorm")
    mo1 = _mm(cat1, p["dn_w_out"], name="dn_out")
    x3, h3 = _post_pre(x2, mo1, gains["norm_mix_post"][1], gains["norm_ffn_pre"][1], name="post_mix1")
    gu1, act1, f1 = _ffn_fwd(h3, p["ffn_w_gate_up"][1], p["ffn_w_down"][1], 1)

    dx3, df1, dg_ffn_post1, loss_cols = _final_loss_bwd(x3, f1, gains["norm_ffn_post"][1], target, name="loss_bwd")
    dh3, d_wgu1, d_wd1 = _ffn_bwd(h3, gu1, act1, p["ffn_w_gate_up"][1], p["ffn_w_down"][1], df1, 1)
    dx2, dmo1, dg_mix_post1, dg_ffn_pre1 = _post_pre_bwd(x2, mo1, gains["norm_mix_post"][1], gains["norm_ffn_pre"][1],
                                                         dx3, dh3, name="post_mix1_bwd")
    dcat1 = _mm(dmo1, p["dn_w_out"], tb=True, name="dn_out_dx")
    g["dn_w_out"] = _mm(cat1, dmo1, ta=True, name="dn_out_dw")
    d_o, dz, dqm1, d_out_gain, dkv1 = _dn_out_bwd(o_fwd, o_rev, proj, out_gain, kv_mem[1], dcat1, name="dn_outnorm_bwd")
    d_f, d_r = _dn_scan_bwd(qkv, gc, g_tot, beta, (st_fwd, st_rev), d_o, name="dn_scan_bwd")
    d_gate_cols, d_a_cols, d_dt_cols = _dn_gates_bwd(proj, a_cols, dt_cols, (d_f[3], d_r[3]), (d_f[4], d_r[4]),
                                                     (d_f[5], d_r[5]), name="dn_gates_bwd")
    d_pre, d_conv = zip(*[_dn_conv_bwd(proj, p["dn_conv"], d_f[which], d_r[which], which, name=f"dn_conv_bwd{which}")
                          for which in range(3)])
    dproj = jnp.concatenate(list(d_pre) + [dz, dqm1, d_gate_cols], axis=1)
    dh2 = _mm(dproj, p["dn_w_in"], tb=True, name="dn_in_dx")
    g["dn_w_in"] = _mm(h2, dproj, ta=True, name="dn_in_dw")
    g["dn_conv"] = jnp.concatenate(d_conv, axis=1)
    g["dn_a_log"] = _gate_params_bwd(d_a_cols)
    g["dn_dt_bias"] = _gate_params_bwd(d_dt_cols)
    g["dn_out_norm"] = d_out_gain

    dx1, df0, dg_ffn_post0, dg_mix_pre1 = _post_pre_bwd(x1, f0, gains["norm_ffn_post"][0], gains["norm_mix_pre"][1],
                                                        dx2, dh2, name="post_ffn0_bwd")
    dh1, d_wgu0, d_wd0 = _ffn_bwd(h1, gu0, act0, p["ffn_w_gate_up"][0], p["ffn_w_down"][0], df0, 0)
    dx0, dmo0, dg_mix_post0, dg_ffn_pre0 = _post_pre_bwd(x, mo0, gains["norm_mix_post"][0], gains["norm_ffn_pre"][0],
                                                         dx1, dh1, name="post_mix0_bwd")
    dcat0 = _mm(dmo0, p["att_w_out"], tb=True, name="att_out_dx")
    g["att_w_out"] = _mm(cat0, dmo0, ta=True, name="att_out_dw")
    delta, dqm0, dkv0 = _att_bwd_prep(cat0, dcat0, qkvm, kv_mem[0], name="att_bwd_prep")
    att_b = [_att_bwd(qkvm, bias[gi], lse_tot, delta, dcat0, gi, name=f"att_bwd{gi}") for gi in range(3)]
    dqkvm = jnp.concatenate([a[w] for w in range(3) for a in att_b] + [dqm0], axis=1)
    g["rel_bias"] = sum(_bias_tiles_bwd(p["rel_bias"], att_b[gi][3], gi) for gi in range(3))
    dh0 = _mm(dqkvm, p["att_w_in"], tb=True, name="att_in_dx")
    g["att_w_in"] = _mm(h0, dqkvm, ta=True, name="att_in_dw")
    grad_x, dg_mix_pre0 = _pre_norm_bwd(x, gains["norm_mix_pre"][0], dh0, dx0, name="pre0_bwd")

    d_mem_kv, d_mem_norm = [], []
    for i, dkv in enumerate((dkv0, dkv1)):
        d_mem_kv.append(_mm(mem_n[i], dkv, ta=True, name=f"mem_kv_dw{i}"))
        d_mem_n = _mm(dkv, p["mem_w_kv"][i], tb=True, name=f"mem_kv_dx{i}")
        d_mem_norm.append(_gain_bwd(mem, gains["mem_norm"][i], d_mem_n, name=f"mem_norm_bwd{i}"))
    g["mem_w_kv"] = jnp.stack(d_mem_kv)
    g["mem_norm"] = jnp.concatenate(d_mem_norm, axis=0)
    g["norm_mix_pre"] = jnp.concatenate([dg_mix_pre0, dg_mix_pre1], axis=0)
    g["norm_mix_post"] = jnp.concatenate([dg_mix_post0, dg_mix_post1], axis=0)
    g["norm_ffn_pre"] = jnp.concatenate([dg_ffn_pre0, dg_ffn_pre1], axis=0)
    g["norm_ffn_post"] = jnp.concatenate([dg_ffn_post0, dg_ffn_post1], axis=0)
    g["ffn_w_gate_up"] = jnp.stack([d_wgu0, d_wgu1])
    g["ffn_w_down"] = jnp.stack([d_wd0, d_wd1])
    return loss_cols, grad_x, g


N_CHIPS = 4
N_DEV = 8
MESH = pl.DeviceIdType.MESH
BIG = (("att_w_in", (1, 1024, 640), 2), ("att_w_out", (1, 256, 1024), 1), ("dn_w_in", (1, 1024, 838), 2),
       ("dn_w_out", (1, 256, 1024), 1), ("mem_w_kv", (2, 256, 512), 1), ("ffn_w_gate_up", (2, 1024, 1408), 2),
       ("ffn_w_down", (2, 704, 1024), 1))
PACK_COLS = 1024
PACK_ELEMS = sum(math.prod(shape) for _, shape, _ in BIG)
HALF_ROWS = -(-PACK_ELEMS // (2 * PACK_COLS * 16)) * 16
PACK_ROWS = 2 * HALF_ROWS
HALF_TM = _tile(HALF_ROWS, 512, 16)


def _pack_local(blocks):
    flat = [blocks[n].reshape(-1) for n, _, _ in BIG]
    dtype = flat[0].dtype
    flat.append(jnp.zeros((PACK_ROWS * PACK_COLS - PACK_ELEMS,), dtype))
    return jnp.concatenate(flat).reshape(PACK_ROWS, PACK_COLS)


def _unpack_local(slab):
    flat = slab.reshape(-1)
    out, off = {}, 0
    for n, shape, _ in BIG:
        size = math.prod(shape)
        out[n] = flat[off: off + size].reshape(shape)
        off += size
    return out


def _full_from_gathered(gathered):
    per_chip = [_unpack_local(gathered[s]) for s in range(N_CHIPS)]
    return {n: jnp.concatenate([pc[n] for pc in per_chip], axis=axis) for n, _, axis in BIG}


def _pack_full(full):
    slabs = []
    for s in range(N_CHIPS):
        blocks = {}
        for n, shape, axis in BIG:
            blocks[n] = lax.slice_in_dim(full[n], s * shape[axis], (s + 1) * shape[axis], axis=axis)
        slabs.append(_pack_local(blocks))
    return jnp.stack(slabs)


def _mesh_pos():
    return lax.axis_index("x"), lax.axis_index("y"), lax.axis_index("c")


def _other_chips(x, y):
    return [(1 - x, y), (x, 1 - y), (1 - x, 1 - y)]


ANY = pl.BlockSpec(memory_space=pl.ANY)


def _gather_shards(slab, *, name):
    rows, cols = slab.shape
    half = rows // 2

    def body(x_ref, out_ref, send_sems, recv_sems, local_sem):
        x, y, c = _mesh_pos()
        me = 2 * x + y
        sibling = (x, y, 1 - c)
        chips = _other_chips(x, y)

        def part(chip, h):
            return out_ref.at[2 * chip[0] + chip[1], pl.ds(h * half, half), :]

        def copy(k, src, dst, to):
            return pltpu.make_async_remote_copy(src_ref=src, dst_ref=dst, send_sem=send_sems.at[k],
                                                recv_sem=recv_sems.at[k], device_id=to, device_id_type=MESH)

        mine = pltpu.make_async_copy(x_ref, out_ref.at[me], local_sem)
        mine.start()
        my_half = x_ref.at[pl.ds(c * half, half), :]
        first = [copy(j, my_half, part((x, y), c), (*chip, c)) for j, chip in enumerate(chips)]
        for cp in first:
            cp.start()
        passed = [copy(3 + j, part(chip, c), part(chip, c), sibling) for j, chip in enumerate(chips)]
        for j, chip in enumerate(chips):
            copy(j, my_half, part(chip, c), (*chip, c)).wait_recv()
            passed[j].start()
        for j, chip in enumerate(chips):
            copy(3 + j, part(chip, 1 - c), part(chip, 1 - c), sibling).wait_recv()
        for cp in first + passed:
            cp.wait_send()
        mine.wait()

    return pl.pallas_call(
        body, name=name, in_specs=[ANY], out_specs=ANY,
        out_shape=jax.ShapeDtypeStruct((N_CHIPS, rows, cols), slab.dtype),
        scratch_shapes=[pltpu.SemaphoreType.DMA((6,)), pltpu.SemaphoreType.DMA((6,)), pltpu.SemaphoreType.DMA],
    )(slab)


def _swap_other_half(slabs, *, name):
    n, rows, cols = slabs.shape
    half = rows // 2

    def body(g_ref, out_ref, send_sem, recv_sem):
        x, y, c = _mesh_pos()
        cp = pltpu.make_async_remote_copy(src_ref=g_ref.at[:, pl.ds((1 - c) * half, half), :], dst_ref=out_ref,
                                          send_sem=send_sem, recv_sem=recv_sem, device_id=(x, y, 1 - c), device_id_type=MESH)
        cp.start()
        cp.wait()

    return pl.pallas_call(
        body, name=name, in_specs=[ANY], out_specs=ANY, out_shape=jax.ShapeDtypeStruct((n, half, cols), slabs.dtype),
        scratch_shapes=[pltpu.SemaphoreType.DMA, pltpu.SemaphoreType.DMA],
    )(slabs)


def _add_own_half(slabs, received, core, *, name):
    n, rows, cols = slabs.shape
    half = rows // 2
    nb = half // HALF_TM

    def kern(c_ref, a_ref, b_ref, o_ref):
        o_ref[...] = a_ref[...] + b_ref[...]

    return pl.pallas_call(
        kern, name=name,
        grid_spec=pltpu.PrefetchScalarGridSpec(
            num_scalar_prefetch=1, grid=(n, nb),
            in_specs=[pl.BlockSpec((None, HALF_TM, cols), lambda s, i, c: (s, c[0] * nb + i, 0)),
                      pl.BlockSpec((None, HALF_TM, cols), lambda s, i, c: (s, i, 0))],
            out_specs=pl.BlockSpec((None, HALF_TM, cols), lambda s, i, c: (s, i, 0))),
        out_shape=jax.ShapeDtypeStruct((n, half, cols), F32),
        compiler_params=_params("parallel", "parallel"),
    )(core, slabs, received)


def _scatter_to_chips(sums, *, name):
    n, half, cols = sums.shape

    def body(s_ref, out_ref, send_sems, recv_sems, local_sem):
        x, y, c = _mesh_pos()
        me = 2 * x + y
        chips = _other_chips(x, y)

        def copy(k, chip):
            return pltpu.make_async_remote_copy(src_ref=s_ref.at[2 * chip[0] + chip[1]], dst_ref=out_ref.at[me],
                                                send_sem=send_sems.at[k], recv_sem=recv_sems.at[k],
                                                device_id=(*chip, c), device_id_type=MESH)

        mine = pltpu.make_async_copy(s_ref.at[me], out_ref.at[me], local_sem)
        mine.start()
        sends = [copy(j, chip) for j, chip in enumerate(chips)]
        for cp in sends:
            cp.start()
        for j, chip in enumerate(chips):
            pltpu.make_async_remote_copy(src_ref=s_ref.at[me], dst_ref=out_ref.at[2 * chip[0] + chip[1]],
                                         send_sem=send_sems.at[j], recv_sem=recv_sems.at[j],
                                         device_id=(*chip, c), device_id_type=MESH).wait_recv()
        for cp in sends:
            cp.wait_send()
        mine.wait()

    return pl.pallas_call(
        body, name=name, in_specs=[ANY], out_specs=ANY, out_shape=jax.ShapeDtypeStruct((n, half, cols), sums.dtype),
        scratch_shapes=[pltpu.SemaphoreType.DMA((3,)), pltpu.SemaphoreType.DMA((3,)), pltpu.SemaphoreType.DMA],
    )(sums)


def _sum_chips(parts, *, name):
    n, half, cols = parts.shape

    def kern(p_ref, o_ref):
        acc = p_ref[0]
        for s in range(1, n):
            acc = acc + p_ref[s]
        o_ref[...] = acc

    return pl.pallas_call(
        kern, name=name, grid=(half // HALF_TM,),
        in_specs=[pl.BlockSpec((n, HALF_TM, cols), lambda i: (0, i, 0))],
        out_specs=pl.BlockSpec((HALF_TM, cols), lambda i: (i, 0)),
        out_shape=jax.ShapeDtypeStruct((half, cols), F32),
        compiler_params=_params("parallel"),
    )(parts)


def _join_halves(mine, *, name):
    half, cols = mine.shape

    def body(m_ref, out_ref, send_sem, recv_sem, local_sem):
        x, y, c = _mesh_pos()
        own = out_ref.at[pl.ds(c * half, half), :]
        local = pltpu.make_async_copy(m_ref, own, local_sem)
        local.start()
        cp = pltpu.make_async_remote_copy(src_ref=m_ref, dst_ref=own, send_sem=send_sem, recv_sem=recv_sem,
                                          device_id=(x, y, 1 - c), device_id_type=MESH)
        cp.start()
        pltpu.make_async_remote_copy(src_ref=m_ref, dst_ref=out_ref.at[pl.ds((1 - c) * half, half), :], send_sem=send_sem,
                                     recv_sem=recv_sem, device_id=(x, y, 1 - c), device_id_type=MESH).wait_recv()
        cp.wait_send()
        local.wait()

    return pl.pallas_call(
        body, name=name, in_specs=[ANY], out_specs=ANY, out_shape=jax.ShapeDtypeStruct((2 * half, cols), mine.dtype),
        scratch_shapes=[pltpu.SemaphoreType.DMA, pltpu.SemaphoreType.DMA, pltpu.SemaphoreType.DMA],
    )(mine)


def _all_reduce_small(v, *, name):
    rows, cols = v.shape
    flips = [(dx, dy, dc) for dx in (0, 1) for dy in (0, 1) for dc in (0, 1)][1:]

    def body(v_ref, o_ref, buf, send_sems, recv_sems):
        x, y, c = _mesh_pos()

        def peer(f):
            return tuple(1 - p if fl else p for p, fl in zip((x, y, c), f))

        def index(p):
            return 4 * p[0] + 2 * p[1] + p[2]

        buf[index((x, y, c))] = v_ref[...]
        sends = []
        for k, f in enumerate(flips):
            cp = pltpu.make_async_remote_copy(src_ref=v_ref, dst_ref=buf.at[index((x, y, c))], send_sem=send_sems.at[k],
                                              recv_sem=recv_sems.at[k], device_id=peer(f), device_id_type=MESH)
            cp.start()
            sends.append(cp)
        for k, f in enumerate(flips):
            pltpu.make_async_remote_copy(src_ref=v_ref, dst_ref=buf.at[index(peer(f))], send_sem=send_sems.at[k],
                                         recv_sem=recv_sems.at[k], device_id=peer(f), device_id_type=MESH).wait_recv()
        for cp in sends:
            cp.wait_send()
        acc = buf[0]
        for d in range(1, N_DEV):
            acc = acc + buf[d]
        o_ref[...] = acc

    vmem = pl.BlockSpec(memory_space=pltpu.VMEM)
    return pl.pallas_call(
        body, name=name, in_specs=[vmem], out_specs=vmem, out_shape=jax.ShapeDtypeStruct((rows, cols), F32),
        scratch_shapes=[pltpu.VMEM((N_DEV, rows, cols), F32), pltpu.SemaphoreType.DMA((N_DEV - 1,)),
                        pltpu.SemaphoreType.DMA((N_DEV - 1,))],
    )(v)


def _reduce_scatter(slabs, core):
    received = _swap_other_half(slabs, name="rs_swap")
    sums = _add_own_half(slabs, received, core, name="rs_add")
    parts = _scatter_to_chips(sums, name="rs_scatter")
    return _join_halves(_sum_chips(parts, name="rs_sum"), name="rs_join")


def _adamw(w, g, m, v, *, name):
    def body(w, g, m, v):
        m = ADAM_B1 * m + (1.0 - ADAM_B1) * g
        v = ADAM_B2 * v + (1.0 - ADAM_B2) * (g * g)
        m_hat = m / (1.0 - ADAM_B1 ** ADAM_STEP)
        v_hat = v / (1.0 - ADAM_B2 ** ADAM_STEP)
        delta = -ADAM_LR * (m_hat / (jnp.sqrt(v_hat) + ADAM_EPS) + ADAM_WD * w)
        return (delta, m, v), ()
    rows, cols = w.shape
    return _rowwise(body, [w, g, m, v], [], [(cols, F32)] * 3, [], tm=_tile(rows, 256, SUBLANES), name=name)


def _pack_small(arrs, rows):
    flat = jnp.concatenate([a.reshape(-1) for a in arrs])
    return jnp.pad(flat, (0, rows * LANES - flat.shape[0])).reshape(rows, LANES)


def _unpack_small(packed, shapes):
    flat = packed.reshape(-1)
    out, off = [], 0
    for s in shapes:
        size = math.prod(s)
        out.append(flat[off: off + size].reshape(s))
        off += size
    return out


def _small_rows(shapes):
    return -(-sum(math.prod(s) for s in shapes) // (SUBLANES * LANES)) * SUBLANES


WEIGHTS = ("rel_bias", "att_w_in", "att_w_out", "dn_w_in", "dn_conv", "dn_a_log", "dn_dt_bias", "dn_out_norm", "dn_w_out",
           "mem_norm", "mem_w_kv", "norm_mix_pre", "norm_mix_post", "norm_ffn_pre", "norm_ffn_post", "ffn_w_gate_up",
           "ffn_w_down")
BIG_NAMES = tuple(n for n, _, _ in BIG)
SMALL_NAMES = tuple(n for n in WEIGHTS if n not in BIG_NAMES)
CONV_COLS = 3 * TOK_WIDTH
CONV_SHARD = CONV_COLS // N_CHIPS


def kernel(x, mem, rel_bias, att_w_in, att_w_out, dn_w_in, dn_conv, dn_a_log, dn_dt_bias, dn_out_norm, dn_w_out, mem_norm, mem_w_kv, norm_mix_pre, norm_mix_post, norm_ffn_pre, norm_ffn_post, ffn_w_gate_up, ffn_w_down, loss_target, m_rel_bias, m_att_w_in, m_att_w_out, m_dn_w_in, m_dn_conv, m_dn_a_log, m_dn_dt_bias, m_dn_out_norm, m_dn_w_out, m_mem_norm, m_mem_w_kv, m_norm_mix_pre, m_norm_mix_post, m_norm_ffn_pre, m_norm_ffn_post, m_ffn_w_gate_up, m_ffn_w_down, v_rel_bias, v_att_w_in, v_att_w_out, v_dn_w_in, v_dn_conv, v_dn_a_log, v_dn_dt_bias, v_dn_out_norm, v_dn_w_out, v_mem_norm, v_mem_w_kv, v_norm_mix_pre, v_norm_mix_post, v_norm_ffn_pre, v_norm_ffn_post, v_ffn_w_gate_up, v_ffn_w_down):
    w = dict(zip(WEIGHTS, (rel_bias, att_w_in, att_w_out, dn_w_in, dn_conv, dn_a_log, dn_dt_bias, dn_out_norm, dn_w_out,
                           mem_norm, mem_w_kv, norm_mix_pre, norm_mix_post, norm_ffn_pre, norm_ffn_post, ffn_w_gate_up,
                           ffn_w_down)))
    m = dict(zip(WEIGHTS, (m_rel_bias, m_att_w_in, m_att_w_out, m_dn_w_in, m_dn_conv, m_dn_a_log, m_dn_dt_bias,
                           m_dn_out_norm, m_dn_w_out, m_mem_norm, m_mem_w_kv, m_norm_mix_pre, m_norm_mix_post,
                           m_norm_ffn_pre, m_norm_ffn_post, m_ffn_w_gate_up, m_ffn_w_down)))
    v = dict(zip(WEIGHTS, (v_rel_bias, v_att_w_in, v_att_w_out, v_dn_w_in, v_dn_conv, v_dn_a_log, v_dn_dt_bias,
                           v_dn_out_norm, v_dn_w_out, v_mem_norm, v_mem_w_kv, v_norm_mix_pre, v_norm_mix_post,
                           v_norm_ffn_pre, v_norm_ffn_post, v_ffn_w_gate_up, v_ffn_w_down)))
    cx, cy, cc = _mesh_pos()
    chip = 2 * cx + cy

    gathered = _gather_shards(_pack_local({n: w[n] for n in BIG_NAMES}).astype(MXU_DTYPE), name="gather_weights")
    full = _full_from_gathered(gathered)
    conv_rows = _small_rows([(DN_CONV, CONV_COLS)])
    conv_mine = jnp.where(cc == 0, 1.0, 0.0) * w["dn_conv"][0]
    conv_placed = lax.dynamic_update_slice(jnp.zeros((DN_CONV, CONV_COLS), F32), conv_mine, (0, chip * CONV_SHARD))
    conv_full = _unpack_small(_all_reduce_small(_pack_small([conv_placed], conv_rows), name="gather_conv"),
                              [(DN_CONV, CONV_COLS)])[0]
    p = {
        "rel_bias": w["rel_bias"], "att_w_in": full["att_w_in"][0], "att_w_out": full["att_w_out"][0],
        "dn_w_in": _pad_dn_w_in(full["dn_w_in"][0]), "dn_conv": conv_full, "dn_a_log": w["dn_a_log"][0],
        "dn_dt_bias": w["dn_dt_bias"][0], "dn_out_norm": w["dn_out_norm"][0], "dn_w_out": full["dn_w_out"][0],
        "mem_norm": w["mem_norm"], "mem_w_kv": full["mem_w_kv"], "norm_mix_pre": w["norm_mix_pre"],
        "norm_mix_post": w["norm_mix_post"], "norm_ffn_pre": w["norm_ffn_pre"], "norm_ffn_post": w["norm_ffn_post"],
        "ffn_w_gate_up": full["ffn_w_gate_up"], "ffn_w_down": full["ffn_w_down"],
    }

    loss_cols, grad_x, g = _local_step(x[0], mem[0], loss_target[0], p)
    loss = lax.psum(jnp.sum(loss_cols), ("x", "y", "c"))

    g_big = {"att_w_in": g["att_w_in"][None], "att_w_out": g["att_w_out"][None], "dn_w_in": _unpad_dn_w_in(g["dn_w_in"])[None],
             "dn_w_out": g["dn_w_out"][None], "mem_w_kv": g["mem_w_kv"], "ffn_w_gate_up": g["ffn_w_gate_up"],
             "ffn_w_down": g["ffn_w_down"]}
    reduced = _reduce_scatter(_pack_full(g_big), cc.astype(jnp.int32).reshape(1))
    grads = _unpack_local(reduced)
    small_full_shapes = [(DN_CONV, CONV_COLS) if n == "dn_conv" else w[n].shape for n in SMALL_NAMES]
    small_sum = _all_reduce_small(_pack_small([g[n] for n in SMALL_NAMES], _small_rows(small_full_shapes)), name="reduce_small")
    for n, s in zip(SMALL_NAMES, _unpack_small(small_sum, small_full_shapes)):
        grads[n] = lax.dynamic_slice(s, (0, chip * CONV_SHARD), (DN_CONV, CONV_SHARD))[None] if n == "dn_conv" else s

    delta, new_m, new_v = {}, {}, {}
    for n in BIG_NAMES:
        shape = w[n].shape
        two_d = lambda a: a.reshape(-1, shape[-1])
        res = _adamw(two_d(w[n]), two_d(grads[n]), two_d(m[n]), two_d(v[n]), name=f"adamw_{n}")
        delta[n], new_m[n], new_v[n] = (r.reshape(shape) for r in res)
    small_shapes = [w[n].shape for n in SMALL_NAMES]
    rows = _small_rows(small_shapes)
    res = _adamw(*[_pack_small([d[n] for n in SMALL_NAMES], rows) for d in (w, grads, m, v)], name="adamw_small")
    for d, r in zip((delta, new_m, new_v), res):
        for n, a in zip(SMALL_NAMES, _unpack_small(r, small_shapes)):
            d[n] = a
    return (loss, grad_x[None], *[grads[n] for n in WEIGHTS], *[delta[n] for n in WEIGHTS],
            *[new_m[n] for n in WEIGHTS], *[new_v[n] for n in WEIGHTS])
```

```python
import functools
import math

import numpy as np
import jax
import jax.numpy as jnp
from jax import lax
from jax.experimental import pallas as pl
from jax.experimental.pallas import tpu as pltpu

F32 = jnp.float32
MXU_DTYPE = jnp.bfloat16
HI = lax.Precision.HIGHEST

EPS = 1e-6
NEG_INF = -1e30
LANES = 128
SUBLANES = 8
VMEM_LIMIT = 56 * 1024 * 1024

D_MODEL = 1024
TOK_WIDTH = 768
MEM_WIDTH = 256
MEM_LEN = 256
ATT_HEAD_DIM = 64
DILATIONS = (1, 4, 16)
HALF = 64
ATT_BQ = 128
ATT_W = ATT_BQ + 2 * HALF
REL_BUCKETS = 32
REL_MAX_DIST = 1024
DN_HEADS = 6
DN_HEAD_DIM = 128
DN_CONV = 5
DN_CHUNK = 128
D_FF = 2816
ATT_IN = 2560
DN_IN = 3352
DN_IN_PAD = 3456
N_GATES = 4 * DN_HEADS

ADAM_LR = 0.001
ADAM_B1 = 0.9
ADAM_B2 = 0.999
ADAM_EPS = 1e-08
ADAM_WD = 0.01
ADAM_STEP = 10


def _tile(n, target, align):
    if n <= target:
        return n
    t = (target // align) * align
    while t >= align:
        if n % t == 0:
            return t
        t -= align
    raise ValueError(f"no tile for {n} (target {target}, align {align})")


def _params(*sem):
    return pltpu.CompilerParams(dimension_semantics=sem, vmem_limit_bytes=VMEM_LIMIT)


def _mm(a, b, *, name, ta=False, tb=False, tm=1024, tn=1408, tk=1408, out_dtype=F32):
    if ta:
        K, M = a.shape
    else:
        M, K = a.shape
    if tb:
        N, K2 = b.shape
    else:
        K2, N = b.shape
    assert K == K2, (a.shape, b.shape, ta, tb)
    tm = _tile(M, tm, LANES if ta else SUBLANES)
    tn = _tile(N, tn, LANES)
    tk = _tile(K, tk, LANES)
    nk = K // tk
    a_spec = pl.BlockSpec((tk, tm), lambda i, j, k: (k, i)) if ta else pl.BlockSpec((tm, tk), lambda i, j, k: (i, k))
    b_spec = pl.BlockSpec((tn, tk), lambda i, j, k: (j, k)) if tb else pl.BlockSpec((tk, tn), lambda i, j, k: (k, j))
    dims = (((0 if ta else 1,), (1 if tb else 0,)), ((), ()))

    def kern(a_ref, b_ref, o_ref, acc_ref):
        k = pl.program_id(2)

        @pl.when(k == 0)
        def _():
            acc_ref[...] = jnp.zeros_like(acc_ref)

        acc_ref[...] += lax.dot_general(a_ref[...].astype(MXU_DTYPE), b_ref[...].astype(MXU_DTYPE), dims,
                                        preferred_element_type=F32)

        @pl.when(k == nk - 1)
        def _():
            o_ref[...] = acc_ref[...].astype(o_ref.dtype)

    return pl.pallas_call(
        kern, name=name, grid=(M // tm, N // tn, nk), in_specs=[a_spec, b_spec],
        out_specs=pl.BlockSpec((tm, tn), lambda i, j, k: (i, j)),
        out_shape=jax.ShapeDtypeStruct((M, N), out_dtype),
        scratch_shapes=[pltpu.VMEM((tm, tn), F32)],
        compiler_params=_params("parallel", "parallel", "arbitrary"),
    )(a, b)


def _col(arr, width, blk):
    return (arr, width, blk)


def _rowwise(body, rows, consts, out_rows, out_acc, *, tm, name):
    n_rows = (rows[0][0] if isinstance(rows[0], tuple) else rows[0]).shape[0]
    assert n_rows % tm == 0, (n_rows, tm)
    arrs, in_specs = [], []
    for r in rows:
        arr, width, blk = r if isinstance(r, tuple) else (r, r.shape[1], 0)
        assert arr.shape[0] == n_rows
        arrs.append(arr)
        in_specs.append(pl.BlockSpec((tm, width), functools.partial(lambda i, b: (i, b), b=blk)))
    for c in consts:
        arrs.append(c)
        in_specs.append(pl.BlockSpec(c.shape, functools.partial(lambda i, n: (0,) * n, n=c.ndim)))
    n_in, n_ro = len(arrs), len(out_rows)
    out_shape = [jax.ShapeDtypeStruct((n_rows, w), dt) for w, dt in out_rows]
    out_specs = [pl.BlockSpec((tm, w), lambda i: (i, 0)) for w, _ in out_rows]
    out_shape += [jax.ShapeDtypeStruct(s, F32) for s in out_acc]
    out_specs += [pl.BlockSpec(s, lambda i: (0, 0)) for s in out_acc]

    def kern(*refs):
        ro, ao = body(*[r[...] for r in refs[:n_in]])
        outs = refs[n_in:]
        for r, v in zip(outs[:n_ro], ro, strict=True):
            r[...] = v.astype(r.dtype)
        if out_acc:
            @pl.when(pl.program_id(0) == 0)
            def _():
                for r in outs[n_ro:]:
                    r[...] = jnp.zeros_like(r)

            for r, v in zip(outs[n_ro:], ao, strict=True):
                r[...] += v

    res = pl.pallas_call(
        kern, name=name, grid=(n_rows // tm,), in_specs=in_specs, out_specs=out_specs, out_shape=out_shape,
        compiler_params=_params("arbitrary" if out_acc else "parallel"),
    )(*arrs)
    return res


def _rms(x, gain):
    return x * lax.rsqrt(jnp.mean(x * x, axis=-1, keepdims=True) + EPS) * gain


def _silu(x):
    return x * jax.nn.sigmoid(x)


def _softplus(x):
    return jnp.maximum(x, 0.0) + jnp.log(1.0 + jnp.exp(-jnp.abs(x)))


def _dot_nt(a, b, precision=None):
    return lax.dot_general(a, b, (((1,), (1,)), ((), ())), preferred_element_type=F32, precision=precision)


def _dot_tn(a, b, precision=None):
    return lax.dot_general(a, b, (((0,), (0,)), ((), ())), preferred_element_type=F32, precision=precision)


def _dot(a, b, precision=None):
    return jnp.dot(a, b, preferred_element_type=F32, precision=precision)


def _pre_norm(x, gain, *, name):
    def body(x, g):
        return (_rms(x, g),), ()
    return _rowwise(body, [x], [gain], [(x.shape[1], MXU_DTYPE)], [], tm=_tile(x.shape[0], 512, 2 * SUBLANES), name=name)[0]


def _pre_norm_bwd(x, gain, dh, dx_other, *, name):
    def body(x, dh, dxo, g):
        _, vjp = jax.vjp(_rms, x, g)
        dx, dg = vjp(dh)
        return (dx + dxo,), (dg,)
    return _rowwise(body, [x, dh, dx_other], [gain], [(x.shape[1], F32)], [gain.shape], tm=512, name=name)


def _gain_bwd(x, gain, dh, *, name):
    def body(x, dh, g):
        _, vjp = jax.vjp(lambda g_: _rms(x, g_), g)
        return (), (vjp(dh)[0],)
    return _rowwise(body, [x, dh], [gain], [], [gain.shape], tm=_tile(x.shape[0], 512, SUBLANES), name=name)[0]


def _res_block(x_res, m, g_post, g_pre):
    x_new = x_res + _rms(m, g_post)
    return x_new, _rms(x_new, g_pre)


def _post_pre(x_res, m, g_post, g_pre, *, name):
    def body(x, m, gp, gq):
        return _res_block(x, m, gp, gq), ()
    d = x_res.shape[1]
    return _rowwise(body, [x_res, m], [g_post, g_pre], [(d, F32), (d, MXU_DTYPE)], [], tm=512, name=name)


def _post_pre_bwd(x_res, m, g_post, g_pre, dx_new, dh, *, name):
    def body(x, m, dxn, dh, gp, gq):
        _, vjp = jax.vjp(_res_block, x, m, gp, gq)
        dx, dm, dgp, dgq = vjp((dxn, dh))
        return (dx, dm), (dgp, dgq)
    d = x_res.shape[1]
    return _rowwise(body, [x_res, m, dx_new, dh], [g_post, g_pre], [(d, F32), (d, MXU_DTYPE)],
                    [g_post.shape, g_pre.shape], tm=256, name=name)


def _final_loss_bwd(x_res, m, g_post, target, *, name):
    d = x_res.shape[1]

    def loss_cols(x, m, g, t):
        err = x + _rms(m, g) - t
        return jnp.sum(err * err, axis=0, keepdims=True) * (0.5 / d)

    def body(x, m, t, g):
        cols, vjp = jax.vjp(lambda x_, m_, g_: loss_cols(x_, m_, g_, t), x, m, g)
        dx, dm, dg = vjp(jnp.ones_like(cols))
        return (dx, dm), (dg, cols)
    return _rowwise(body, [x_res, m, target], [g_post], [(d, F32), (d, MXU_DTYPE)], [g_post.shape, (1, d)], tm=256, name=name)


def _swiglu_act(gu, *, name):
    def body(gate, up):
        return (_silu(gate) * up,), ()
    return _rowwise(body, [_col(gu, D_FF, 0), _col(gu, D_FF, 1)], [], [(D_FF, MXU_DTYPE)], [], tm=256, name=name)[0]


def _swiglu_act_bwd(gu, da, *, name):
    def body(gate, up, da):
        _, vjp = jax.vjp(lambda g, u: _silu(g) * u, gate, up)
        dg, du = vjp(da)
        return (jnp.concatenate([dg, du], axis=1),), ()
    return _rowwise(body, [_col(gu, D_FF, 0), _col(gu, D_FF, 1), da], [], [(2 * D_FF, MXU_DTYPE)], [], tm=256, name=name)[0]


def _lane_head_mask(width, head_dim, head):
    lane = lax.broadcasted_iota(jnp.int32, (1, width), 1)
    return (lane // head_dim) == head


def _mem_attn_pair(q_pair, k_pair, v_pair):
    out = jnp.zeros_like(q_pair)
    for h in range(2):
        mh = _lane_head_mask(LANES, ATT_HEAD_DIM, h)
        qh = jnp.where(mh, q_pair * (ATT_HEAD_DIM ** -0.5), 0.0)
        logits = _dot_nt(qh, k_pair)
        mx = jnp.max(logits, axis=-1, keepdims=True)
        p = jnp.exp(logits - mx)
        p = p / jnp.sum(p, axis=-1, keepdims=True)
        out = out + jnp.where(mh, _dot(p, v_pair), 0.0)
    return out


def _mem_attn(q_mem, kv):
    outs = []
    for p in range(MEM_WIDTH // LANES):
        sl = slice(p * LANES, (p + 1) * LANES)
        outs.append(_mem_attn_pair(q_mem[:, sl], kv[:, sl], kv[:, MEM_WIDTH + p * LANES: MEM_WIDTH + (p + 1) * LANES]))
    return jnp.concatenate(outs, axis=1)


def _mem_attn_bwd(q_mem, kv, do):
    dqs, dks, dvs = [], [], []
    for p in range(MEM_WIDTH // LANES):
        sl = slice(p * LANES, (p + 1) * LANES)
        sv = slice(MEM_WIDTH + p * LANES, MEM_WIDTH + (p + 1) * LANES)
        _, vjp = jax.vjp(_mem_attn_pair, q_mem[:, sl], kv[:, sl], kv[:, sv])
        dq, dk, dv = vjp(do[:, sl])
        dqs.append(dq)
        dks.append(dk)
        dvs.append(dv)
    return jnp.concatenate(dqs, axis=1), jnp.concatenate(dks + dvs, axis=1)


def _t5_bucket(rel):
    half = REL_BUCKETS // 2
    max_exact = half // 2
    n = np.abs(rel)
    large = max_exact + (np.log(np.maximum(n, 1) / max_exact) / math.log(REL_MAX_DIST / max_exact)
                         * (half - max_exact)).astype(np.int64)
    large = np.minimum(large, half - 1)
    return ((rel > 0) * half + np.where(n < max_exact, n, large)).astype(np.int32)


ATT_DIAGS = ATT_BQ + ATT_W - 1


def _bias_diag_onehot(dil):
    j = np.arange(ATT_DIAGS)
    tiles = []
    for off in (-HALF, 0, HALF):
        rel = j - (ATT_BQ - 1) - HALF - off
        hot = _t5_bucket(rel * dil)[:, None] == np.arange(REL_BUCKETS)[None, :]
        tiles.append(hot & (np.abs(rel) <= HALF)[:, None])
    return np.stack(tiles).astype(np.float32)


def _toeplitz(r):
    lead = r.shape[:-1]
    a = jnp.broadcast_to(r[..., None, :], lead + (ATT_BQ, ATT_DIAGS))
    a = jnp.pad(a, [(0, 0)] * len(lead) + [(0, 0), (0, 1)])
    a = a.reshape(lead + (ATT_BQ * (ATT_DIAGS + 1),))[..., : ATT_BQ * ATT_DIAGS].reshape(lead + (ATT_BQ, ATT_DIAGS))
    return a[..., ATT_BQ - 1: ATT_BQ - 1 + ATT_W]


def _bias_tiles(rel_bias, gi):
    heads = rel_bias[:, 4 * gi: 4 * gi + 4]
    diag = jnp.einsum('tnb,bh->thn', jnp.asarray(_bias_diag_onehot(DILATIONS[gi])), heads, precision=HI)
    return _toeplitz(diag)


def _bias_tiles_bwd(rel_bias, dtiles, gi):
    return jax.vjp(lambda rb: _bias_tiles(rb, gi), rel_bias)[1](dtiles)[0]


def _att_window(i, n_sub):
    start = jnp.clip(i * ATT_BQ - HALF, 0, n_sub - ATT_W)
    off = i * ATT_BQ - HALF - start
    return pl.multiple_of(start, HALF), off


def _att_valid(off):
    q = lax.broadcasted_iota(jnp.int32, (ATT_BQ, ATT_W), 0)
    kk = lax.broadcasted_iota(jnp.int32, (ATT_BQ, ATT_W), 1)
    return jnp.abs(kk - q - HALF - off) <= HALF


def _att_tile_id(i, nq):
    return jnp.where(i == 0, 0, jnp.where(i == nq - 1, 2, 1))


def _att_fwd(qkvm, bias, gi, *, name):
    dil = DILATIONS[gi]
    s_len = qkvm.shape[0]
    n_sub = s_len // dil
    nq = n_sub // ATT_BQ
    assert n_sub % ATT_BQ == 0 and n_sub >= ATT_W
    cols = qkvm.shape[1] // LANES
    view = qkvm.reshape(n_sub, dil * qkvm.shape[1])

    def kern(q_ref, k_ref, v_ref, b_ref, o_ref, lse_ref):
        i = pl.program_id(2)
        start, off = _att_window(i, n_sub)
        valid = _att_valid(off)
        q = q_ref[...] * (ATT_HEAD_DIM ** -0.5)
        kw = k_ref[pl.ds(start, ATT_W), :]
        vw = v_ref[pl.ds(start, ATT_W), :]
        o = jnp.zeros((ATT_BQ, LANES), F32)
        lse = jnp.zeros((ATT_BQ, LANES), F32)
        for h in range(2):
            mh = _lane_head_mask(LANES, ATT_HEAD_DIM, h)
            s = _dot_nt(jnp.where(mh, q, 0.0), kw) + b_ref[h]
            s = jnp.where(valid, s, NEG_INF)
            mx = jnp.max(s, axis=-1, keepdims=True)
            p = jnp.exp(s - mx)
            den = jnp.sum(p, axis=-1, keepdims=True)
            o = jnp.where(mh, _dot(p, vw) / den, o)
            lse = jnp.where(mh, mx + jnp.log(den), lse)
        o_ref[...] = o
        lse_ref[...] = lse

    def qkv_spec(which, full):
        shape = (n_sub, LANES) if full else (ATT_BQ, LANES)
        return pl.BlockSpec(shape, lambda pr, r, i: (0 if full else i, r * cols + which * 6 + 2 * gi + pr))

    out_spec = pl.BlockSpec((ATT_BQ, LANES), lambda pr, r, i: (i, r * 2 + pr))
    o, lse = pl.pallas_call(
        kern, name=name, grid=(2, dil, nq),
        in_specs=[qkv_spec(0, False), qkv_spec(1, True), qkv_spec(2, True),
                  pl.BlockSpec((None, 2, ATT_BQ, ATT_W), lambda pr, r, i: (_att_tile_id(i, nq), pr, 0, 0))],
        out_specs=[out_spec, out_spec],
        out_shape=[jax.ShapeDtypeStruct((n_sub, dil * 2 * LANES), F32)] * 2,
        compiler_params=_params("parallel", "parallel", "arbitrary"),
    )(view, view, view, bias)
    return o.reshape(s_len, 2 * LANES), lse.reshape(s_len, 2 * LANES)


def _att_bwd(qkvm, bias, lse_tot, delta, dcat, gi, *, name):
    dil = DILATIONS[gi]
    s_len = qkvm.shape[0]
    n_sub = s_len // dil
    nq = n_sub // ATT_BQ
    cols = qkvm.shape[1] // LANES
    dcols = dcat.shape[1] // LANES
    view = qkvm.reshape(n_sub, dil * qkvm.shape[1])
    lse_v = lse_tot.reshape(n_sub, dil * 2 * LANES)
    delta_v = delta.reshape(n_sub, dil * 2 * LANES)
    dcat_v = dcat.reshape(n_sub, dil * dcat.shape[1])

    def kern(q_ref, k_ref, v_ref, b_ref, lse_ref, dl_ref, dm_ref, dq_ref, dk_ref, dv_ref, db_ref):
        r, i = pl.program_id(1), pl.program_id(2)
        start, off = _att_window(i, n_sub)
        valid = _att_valid(off)
        tile = _att_tile_id(i, nq)

        @pl.when(i == 0)
        def _():
            dk_ref[...] = jnp.zeros_like(dk_ref)
            dv_ref[...] = jnp.zeros_like(dv_ref)

        @pl.when((i == 0) & (r == 0))
        def _():
            db_ref[...] = jnp.zeros_like(db_ref)

        q = q_ref[...] * (ATT_HEAD_DIM ** -0.5)
        kw = k_ref[pl.ds(start, ATT_W), :]
        vw = v_ref[pl.ds(start, ATT_W), :]
        dm = dm_ref[...]
        lse = lse_ref[...]
        dl = dl_ref[...]
        dq = jnp.zeros((ATT_BQ, LANES), F32)
        dkw = jnp.zeros((ATT_W, LANES), F32)
        dvw = jnp.zeros((ATT_W, LANES), F32)
        for h in range(2):
            mh = _lane_head_mask(LANES, ATT_HEAD_DIM, h)
            qh = jnp.where(mh, q, 0.0)
            dmh = jnp.where(mh, dm, 0.0)
            s = _dot_nt(qh, kw) + b_ref[tile, h]
            s = jnp.where(valid, s, NEG_INF)
            lse_h = jnp.max(jnp.where(mh, lse, NEG_INF), axis=-1, keepdims=True)
            dl_h = jnp.max(jnp.where(mh, dl, NEG_INF), axis=-1, keepdims=True)
            p = jnp.exp(s - lse_h)
            ds = p * (_dot_nt(dmh, vw) - dl_h)
            dq = dq + jnp.where(mh, _dot(ds, kw), 0.0)
            dkw = dkw + _dot_tn(ds, qh)
            dvw = dvw + _dot_tn(p, dmh)
            db_ref[tile, h] += ds
        dq_ref[...] = dq * (ATT_HEAD_DIM ** -0.5)
        dk_ref[pl.ds(start, ATT_W), :] += dkw
        dv_ref[pl.ds(start, ATT_W), :] += dvw

    def qkv_spec(which, full):
        shape = (n_sub, LANES) if full else (ATT_BQ, LANES)
        return pl.BlockSpec(shape, lambda pr, r, i: (0 if full else i, r * cols + which * 6 + 2 * gi + pr))

    blk = pl.BlockSpec((ATT_BQ, LANES), lambda pr, r, i: (i, r * 2 + pr))
    full = pl.BlockSpec((n_sub, LANES), lambda pr, r, i: (0, r * 2 + pr))
    bias_spec = pl.BlockSpec((3, 2, ATT_BQ, ATT_W), lambda pr, r, i: (0, pr, 0, 0))
    sub = jax.ShapeDtypeStruct((n_sub, dil * 2 * LANES), F32)
    dq, dk, dv, db = pl.pallas_call(
        kern, name=name, grid=(2, dil, nq),
        in_specs=[qkv_spec(0, False), qkv_spec(1, True), qkv_spec(2, True), bias_spec, blk, blk,
                  pl.BlockSpec((ATT_BQ, LANES), lambda pr, r, i: (i, r * dcols + 2 * gi + pr))],
        out_specs=[blk, full, full, bias_spec],
        out_shape=[sub, sub, sub, jax.ShapeDtypeStruct(bias.shape, F32)],
        compiler_params=_params("arbitrary", "arbitrary", "arbitrary"),
    )(view, view, view, bias, lse_v, delta_v, dcat_v)
    return dq.reshape(s_len, -1), dk.reshape(s_len, -1), dv.reshape(s_len, -1), db


def _att_combine(o_g, lse_g, qkvm, kv_mem, *, name):
    def body(o0, o1, o2, l0, l1, l2, qm, kv):
        mx = jnp.maximum(jnp.maximum(l0, l1), l2)
        tot = mx + jnp.log(jnp.exp(l0 - mx) + jnp.exp(l1 - mx) + jnp.exp(l2 - mx))
        mixed = [o * jnp.exp(l - tot) for o, l in ((o0, l0), (o1, l1), (o2, l2))]
        return (jnp.concatenate(mixed + [_mem_attn(qm, kv)], axis=1), tot), ()
    return _rowwise(body, list(o_g) + list(lse_g) + [_col(qkvm, MEM_WIDTH, (3 * TOK_WIDTH) // MEM_WIDTH)], [kv_mem],
                    [(D_MODEL, F32), (MEM_WIDTH, F32)], [], tm=256, name=name)


def _head_sum_matrix():
    a = np.arange(MEM_WIDTH)
    return jnp.asarray((a[:, None] // ATT_HEAD_DIM == a[None, :] // ATT_HEAD_DIM).astype(np.float32))


def _att_bwd_prep(cat, dcat, qkvm, kv_mem, *, name):
    def body(cat, dcat, qm, kv, hs):
        prod = cat * dcat
        summed = prod[:, 0:256] + prod[:, 256:512] + prod[:, 512:768]
        delta = _dot(summed, hs, precision=HI)
        dqm, dkv = _mem_attn_bwd(qm, kv, dcat[:, TOK_WIDTH:])
        return (delta, dqm), (dkv,)
    return _rowwise(body, [cat, dcat, _col(qkvm, MEM_WIDTH, (3 * TOK_WIDTH) // MEM_WIDTH)], [kv_mem, _head_sum_matrix()],
                    [(MEM_WIDTH, F32), (MEM_WIDTH, F32)], [kv_mem.shape], tm=256, name=name)


def _dn_conv_post(s, j):
    scale = jnp.where(j < DN_HEADS, DN_HEAD_DIM ** -0.5, 1.0)
    normed = s * lax.rsqrt(jnp.sum(s * s, axis=-1, keepdims=True) + EPS) * scale
    return jnp.where(j >= 2 * DN_HEADS, s, normed)


def _shift_rows(x, sh):
    n = x.shape[0]
    row = lax.broadcasted_iota(jnp.int32, (n, 1), 0)
    rolled = pltpu.roll(x, (-sh) % n, 0)
    return jnp.where((row + sh >= 0) & (row + sh < n), rolled, 0.0)


def _dn_conv_taps(x, w_ref):
    c = x * w_ref[pl.ds(DN_CONV // 2, 1), :]
    for jj in range(DN_CONV):
        if jj != DN_CONV // 2:
            c = c + _shift_rows(x, jj - DN_CONV // 2) * w_ref[pl.ds(jj, 1), :]
    return c


def _dn_conv_fwd(proj, conv_w, *, name):
    s_len = proj.shape[0]
    width = 3 * TOK_WIDTH

    def kern(x_ref, w_ref, o_ref):
        j = pl.program_id(0)
        o_ref[...] = _dn_conv_post(_silu(_dn_conv_taps(x_ref[...], w_ref)), j)

    return pl.pallas_call(
        kern, name=name, grid=(width // LANES,),
        in_specs=[pl.BlockSpec((s_len, LANES), lambda j: (0, j)), pl.BlockSpec((DN_CONV, LANES), lambda j: (0, j))],
        out_specs=pl.BlockSpec((s_len, LANES), lambda j: (0, j)),
        out_shape=jax.ShapeDtypeStruct((s_len, width), F32),
        compiler_params=_params("parallel"),
    )(proj, conv_w)


def _dn_conv_bwd(proj, conv_w, d_fwd, d_bwd, which, *, name):
    s_len = proj.shape[0]

    def kern(x_ref, w_ref, df_ref, db_ref, dx_ref, dw_ref):
        j = pl.program_id(0) + which * DN_HEADS
        x = x_ref[...]
        c = _dn_conv_taps(x, w_ref)
        _, vjp = jax.vjp(lambda c_: _dn_conv_post(_silu(c_), j), c)
        dc = vjp(df_ref[...] + db_ref[...])[0]
        dx = dc * w_ref[pl.ds(DN_CONV // 2, 1), :]
        for jj in range(DN_CONV):
            sh = jj - DN_CONV // 2
            if sh != 0:
                dx = dx + _shift_rows(dc, -sh) * w_ref[pl.ds(jj, 1), :]
            dw_ref[pl.ds(jj, 1), :] = jnp.sum(dc * _shift_rows(x, sh), axis=0, keepdims=True)
        dx_ref[...] = dx

    return pl.pallas_call(
        kern, name=name, grid=(DN_HEADS,),
        in_specs=[pl.BlockSpec((s_len, LANES), lambda j: (0, j + which * DN_HEADS)),
                  pl.BlockSpec((DN_CONV, LANES), lambda j: (0, j + which * DN_HEADS)),
                  pl.BlockSpec((s_len, LANES), lambda j: (0, j)),
                  pl.BlockSpec((s_len, LANES), lambda j: (0, j))],
        out_specs=[pl.BlockSpec((s_len, LANES), lambda j: (0, j)), pl.BlockSpec((DN_CONV, LANES), lambda j: (0, j))],
        out_shape=[jax.ShapeDtypeStruct((s_len, TOK_WIDTH), F32), jax.ShapeDtypeStruct((DN_CONV, TOK_WIDTH), F32)],
        compiler_params=_params("parallel"),
    )(proj, conv_w, d_fwd, d_bwd)


GATE_TM = 2 * DN_CHUNK


def _gate_constants():
    sel_f = np.zeros((LANES, 2 * TOK_WIDTH), np.float32)
    sel_r = np.zeros((LANES, 2 * TOK_WIDTH), np.float32)
    sel_b = np.zeros((LANES, 2 * TOK_WIDTH), np.float32)
    for d, sel in enumerate((sel_f, sel_r)):
        for h in range(DN_HEADS):
            cols = slice((d * DN_HEADS + h) * DN_HEAD_DIM, (d * DN_HEADS + h + 1) * DN_HEAD_DIM)
            sel[d * 2 * DN_HEADS + h, cols] = 1.0
            sel_b[d * 2 * DN_HEADS + DN_HEADS + h, cols] = 1.0
    i = np.arange(GATE_TM)
    same = (i[:, None] // DN_CHUNK) == (i[None, :] // DN_CHUNK)
    cum_f = same & (i[None, :] <= i[:, None])
    cum_r = same & (i[None, :] >= i[:, None])
    return tuple(jnp.asarray(np.asarray(a, np.float32)) for a in (sel_f, sel_r, sel_b, cum_f, cum_r, same))


def _gate_params(p):
    z = jnp.zeros((DN_HEADS,), F32)
    return jnp.concatenate([p[0], z, p[1], z, jnp.zeros((LANES - N_GATES,), F32)]).reshape(1, LANES)


def _gate_params_bwd(dp):
    return jnp.stack([dp[0, 0:DN_HEADS], dp[0, 2 * DN_HEADS: 3 * DN_HEADS]])


def _dn_gates(gate_in, a_cols, dt_cols, sel_f, sel_r, sel_b, cum_f, cum_r, tot):
    g = -jnp.exp(a_cols) * _softplus(gate_in + dt_cols)
    gc = _dot(_dot(cum_f, g, precision=HI), sel_f, precision=HI) + _dot(_dot(cum_r, g, precision=HI), sel_r, precision=HI)
    g_tot = _dot(_dot(tot, g, precision=HI), sel_f + sel_r, precision=HI)
    beta = jax.nn.sigmoid(_dot(gate_in, sel_b, precision=HI))
    return gc, g_tot, beta


def _dn_gates_fwd(proj, a_cols, dt_cols, *, name):
    def body(gi, *consts):
        return _dn_gates(gi, *consts), ()
    w = 2 * TOK_WIDTH
    return _rowwise(body, [_col(proj, LANES, DN_IN_PAD // LANES - 1)], [a_cols, dt_cols, *_gate_constants()],
                    [(w, F32)] * 3, [], tm=GATE_TM, name=name)


def _dn_gates_bwd(proj, a_cols, dt_cols, d_gc, d_tot, d_beta, *, name):
    def body(gi, gcf, gcr, gtf, gtr, bf, br, a, dt, *consts):
        _, vjp = jax.vjp(lambda gi_, a_, dt_: _dn_gates(gi_, a_, dt_, *consts), gi, a, dt)
        cat = lambda f, r: jnp.concatenate([f, r], axis=1)
        dgi, da, ddt = vjp((cat(gcf, gcr), cat(gtf, gtr), cat(bf, br)))
        return (dgi,), (da, ddt)
    return _rowwise(body, [_col(proj, LANES, DN_IN_PAD // LANES - 1), *d_gc, *d_tot, *d_beta],
                    [a_cols, dt_cols, *_gate_constants()], [(LANES, F32)], [a_cols.shape, dt_cols.shape],
                    tm=GATE_TM, name=name)


INV_BASE = 8


def _block_id_equal(c, size):
    i = lax.broadcasted_iota(jnp.int32, (c, c), 0) // size
    j = lax.broadcasted_iota(jnp.int32, (c, c), 1) // size
    return (i == j).astype(F32)


def _unit_tri_inverse_impl(lmat):
    c = lmat.shape[0]
    eye = _block_id_equal(c, 1)
    same = _block_id_equal(c, INV_BASE)
    neg = -lmat * same
    inv = eye + neg
    power = neg
    for _ in range(int(math.log2(INV_BASE)) - 1):
        power = _dot(power, power)
        inv = inv + _dot(inv, power)
    size = INV_BASE
    while size < c:
        bigger = _block_id_equal(c, 2 * size)
        inv = inv - _dot(_dot(inv, lmat * (bigger - same)), inv)
        same, size = bigger, 2 * size
    resid = eye - _dot(eye + lmat, inv, precision=HI)
    return inv + _dot(inv, resid)


@jax.custom_vjp
def _unit_tri_inverse(lmat):
    return _unit_tri_inverse_impl(lmat)


def _unit_tri_inverse_fwd(lmat):
    inv = _unit_tri_inverse_impl(lmat)
    return inv, inv


def _unit_tri_inverse_bwd(inv, d_inv):
    return (-_dot_tn(inv, _dot_nt(d_inv, inv)),)


_unit_tri_inverse.defvjp(_unit_tri_inverse_fwd, _unit_tri_inverse_bwd)


def _dn_chunk(q, k, v, gc, g_tot, beta, state, tri, inverse):
    c = q.shape[0]
    assert c == DN_HEAD_DIM
    eye = _block_id_equal(c, 1)
    decay = jnp.exp(jnp.where(tri > 0, gc - gc.T, NEG_INF))
    k_beta = k * beta
    inv = inverse((tri - eye) * (_dot_nt(k_beta, k) * decay))
    e_gc = jnp.exp(gc)
    u = _dot(inv, v * beta)
    w = _dot(inv, k_beta * e_gc)
    intra = tri * (_dot_nt(q, k) * decay)
    v_new = u - _dot(w, state)
    out = _dot(q * e_gc, state) + _dot(intra, v_new)
    state = state * jnp.exp(g_tot) + _dot_tn(k * jnp.exp(g_tot - gc), v_new)
    return out, state


def _dn_tri():
    i = np.arange(DN_CHUNK)
    tri = np.stack([(i[None, :] <= i[:, None]), (i[None, :] >= i[:, None])]).astype(np.float32)
    return jnp.asarray(np.repeat(tri, DN_HEADS, axis=0))


def _stack_chains(fwd_ref, rev_ref):
    return jnp.stack([r[:, _head_cols(h)] for r in (fwd_ref, rev_ref) for h in range(DN_HEADS)])


def _unstack_chains(val, fwd_ref, rev_ref):
    for d, r in enumerate((fwd_ref, rev_ref)):
        for h in range(DN_HEADS):
            r[:, _head_cols(h)] = val[d * DN_HEADS + h]


def _dn_row_spec(nc, col, reverse):
    return pl.BlockSpec((DN_CHUNK, TOK_WIDTH), lambda t: ((nc - 1 - t) if reverse else t, col))


def _dn_state_spec(nc, reverse):
    return pl.BlockSpec((None, DN_HEADS, DN_HEAD_DIM, DN_HEAD_DIM), lambda t: ((nc - 1 - t) if reverse else t, 0, 0, 0))


def _head_cols(h):
    return pl.ds(h * DN_HEAD_DIM, DN_HEAD_DIM)


def _dn_scan_fwd(qkv, gc, g_tot, beta, *, name):
    s_len = qkv.shape[0]
    nc = s_len // DN_CHUNK

    def kern(*refs):
        ins, (tri_ref, of_ref, or_ref, sf_ref, sr_ref, state) = refs[:12], refs[12:]

        @pl.when(pl.program_id(0) == 0)
        def _():
            state[...] = jnp.zeros_like(state)

        entry = state[...]
        stacked = [_stack_chains(ins[i], ins[6 + i]) for i in range(6)]
        out, new = jax.vmap(lambda *a: _dn_chunk(*a, _unit_tri_inverse_impl))(*stacked, entry, tri_ref[...])
        sf_ref[...] = entry[:DN_HEADS]
        sr_ref[...] = entry[DN_HEADS:]
        _unstack_chains(out, of_ref, or_ref)
        state[...] = new

    in_specs = [_dn_row_spec(nc, col, rev) for rev in (False, True) for col in (0, 1, 2, int(rev), int(rev), int(rev))]
    in_specs.append(pl.BlockSpec((2 * DN_HEADS, DN_CHUNK, DN_CHUNK), lambda t: (0, 0, 0)))
    return pl.pallas_call(
        kern, name=name, grid=(nc,), in_specs=in_specs,
        out_specs=[_dn_row_spec(nc, 0, False), _dn_row_spec(nc, 0, True), _dn_state_spec(nc, False), _dn_state_spec(nc, True)],
        out_shape=[jax.ShapeDtypeStruct((s_len, TOK_WIDTH), F32)] * 2
        + [jax.ShapeDtypeStruct((nc, DN_HEADS, DN_HEAD_DIM, DN_HEAD_DIM), F32)] * 2,
        scratch_shapes=[pltpu.VMEM((2 * DN_HEADS, DN_HEAD_DIM, DN_HEAD_DIM), F32)],
        compiler_params=_params("arbitrary"),
    )(*([qkv, qkv, qkv, gc, g_tot, beta] * 2), _dn_tri())


def _dn_scan_bwd(qkv, gc, g_tot, beta, states, d_o, *, name):
    s_len = qkv.shape[0]
    nc = s_len // DN_CHUNK

    def kern(*refs):
        ins, tri_ref, outs, d_state = refs[:16], refs[16], refs[17:29], refs[29]

        @pl.when(pl.program_id(0) == 0)
        def _():
            d_state[...] = jnp.zeros_like(d_state)

        stacked = [_stack_chains(ins[i], ins[8 + i]) for i in range(6)]
        entry = jnp.concatenate([ins[6][...], ins[14][...]], axis=0)
        d_out = _stack_chains(ins[7], ins[15])
        tri = tri_ref[...]
        _, vjp = jax.vjp(lambda *a: jax.vmap(lambda *b: _dn_chunk(*b, _unit_tri_inverse))(*a, tri), *stacked, entry)
        grads = vjp((d_out, d_state[...]))
        for i in range(6):
            _unstack_chains(grads[i], outs[i], outs[6 + i])
        d_state[...] = grads[6]

    in_specs = []
    for rev in (True, False):
        in_specs += [_dn_row_spec(nc, col, rev) for col in (0, 1, 2, int(not rev), int(not rev), int(not rev))]
        in_specs += [_dn_state_spec(nc, rev), _dn_row_spec(nc, 0, rev)]
    in_specs.append(pl.BlockSpec((2 * DN_HEADS, DN_CHUNK, DN_CHUNK), lambda t: (0, 0, 0)))
    res = pl.pallas_call(
        kern, name=name, grid=(nc,), in_specs=in_specs,
        out_specs=[_dn_row_spec(nc, 0, rev) for rev in (True, False) for _ in range(6)],
        out_shape=[jax.ShapeDtypeStruct((s_len, TOK_WIDTH), F32)] * 12,
        scratch_shapes=[pltpu.VMEM((2 * DN_HEADS, DN_HEAD_DIM, DN_HEAD_DIM), F32)],
        compiler_params=_params("arbitrary"),
    )(*[a for d in range(2) for a in (qkv, qkv, qkv, gc, g_tot, beta, states[d], d_o)], _dn_tri())
    return res[:6], res[6:]


def _dn_out_head(o_f, o_b, z, gain):
    o = o_f + o_b
    return o * lax.rsqrt(jnp.mean(o * o, axis=-1, keepdims=True) + EPS) * gain * _silu(z)


def _dn_out(o_fwd, o_rev, proj, gain, qkv_kv_mem, *, name):
    def body(of, ob, z, qm, g, kv):
        heads = []
        for h in range(DN_HEADS):
            sl = slice(h * DN_HEAD_DIM, (h + 1) * DN_HEAD_DIM)
            heads.append(_dn_out_head(of[:, sl], ob[:, sl], z[:, sl], g))
        return (jnp.concatenate(heads + [_mem_attn(qm, kv)], axis=1),), ()
    return _rowwise(body, [o_fwd, o_rev, _col(proj, TOK_WIDTH, 3),
                           _col(proj, MEM_WIDTH, (4 * TOK_WIDTH) // MEM_WIDTH)], [gain, qkv_kv_mem],
                    [(D_MODEL, MXU_DTYPE)], [], tm=256, name=name)[0]


def _dn_out_bwd(o_fwd, o_rev, proj, gain, kv_mem, dcat, *, name):
    def body(of, ob, z, qm, dcat, g, kv):
        dos, dzs = [], []
        dgain = jnp.zeros_like(g)
        for h in range(DN_HEADS):
            sl = slice(h * DN_HEAD_DIM, (h + 1) * DN_HEAD_DIM)
            _, vjp = jax.vjp(_dn_out_head, of[:, sl], ob[:, sl], z[:, sl], g)
            d_of, _, dz, dg = vjp(dcat[:, sl])
            dos.append(d_of)
            dzs.append(dz)
            dgain = dgain + dg
        dqm, dkv = _mem_attn_bwd(qm, kv, dcat[:, TOK_WIDTH:])
        return (jnp.concatenate(dos, axis=1), jnp.concatenate(dzs, axis=1), dqm), (dgain, dkv)
    return _rowwise(body, [o_fwd, o_rev, _col(proj, TOK_WIDTH, 3),
                           _col(proj, MEM_WIDTH, (4 * TOK_WIDTH) // MEM_WIDTH), dcat], [gain, kv_mem],
                    [(TOK_WIDTH, F32), (TOK_WIDTH, F32), (MEM_WIDTH, F32)], [gain.shape, kv_mem.shape], tm=256, name=name)


def _pad_dn_w_in(w):
    gates = w[:, 4 * TOK_WIDTH: 4 * TOK_WIDTH + N_GATES]
    zeros = jnp.zeros((w.shape[0], DN_IN_PAD - DN_IN), w.dtype)
    return jnp.concatenate([w[:, :4 * TOK_WIDTH], w[:, 4 * TOK_WIDTH + N_GATES:], gates, zeros], axis=1)


def _unpad_dn_w_in(w):
    q_mem = w[:, 4 * TOK_WIDTH: 4 * TOK_WIDTH + MEM_WIDTH]
    gates = w[:, 4 * TOK_WIDTH + MEM_WIDTH: 4 * TOK_WIDTH + MEM_WIDTH + N_GATES]
    return jnp.concatenate([w[:, :4 * TOK_WIDTH], gates, q_mem], axis=1)


def _ffn_fwd(h, w_gu, w_d, tag):
    gu = _mm(h, w_gu, name=f"ffn_gu_{tag}")
    act = _swiglu_act(gu, name=f"ffn_act_{tag}")
    return gu, act, _mm(act, w_d, name=f"ffn_down_{tag}")


def _ffn_bwd(h, gu, act, w_gu, w_d, df, tag):
    d_act = _mm(df, w_d, tb=True, name=f"ffn_dact_{tag}")
    d_wd = _mm(act, df, ta=True, name=f"ffn_dwd_{tag}")
    d_gu = _swiglu_act_bwd(gu, d_act, name=f"ffn_dgu_{tag}")
    dh = _mm(d_gu, w_gu, tb=True, name=f"ffn_dh_{tag}")
    d_wgu = _mm(h, d_gu, ta=True, name=f"ffn_dwgu_{tag}")
    return dh, d_wgu, d_wd


def _local_step(x, mem, target, p):
    g = {}
    row = lambda v: v.reshape(1, -1)
    gains = {k: [row(p[k][i]) for i in range(2)] for k in
             ("mem_norm", "norm_mix_pre", "norm_mix_post", "norm_ffn_pre", "norm_ffn_post")}
    out_gain = row(p["dn_out_norm"])
    a_cols, dt_cols = _gate_params(p["dn_a_log"]), _gate_params(p["dn_dt_bias"])

    h0 = _pre_norm(x, gains["norm_mix_pre"][0], name="pre0")
    mem_n = [_pre_norm(mem, gains["mem_norm"][i], name=f"mem_norm{i}") for i in range(2)]
    kv_mem = [_mm(mem_n[i], p["mem_w_kv"][i], name=f"mem_kv{i}") for i in range(2)]
    qkvm = _mm(h0, p["att_w_in"], name="att_in")
    bias = [_bias_tiles(p["rel_bias"], gi) for gi in range(3)]
    att = [_att_fwd(qkvm, bias[gi], gi, name=f"att_fwd{gi}") for gi in range(3)]
    cat0, lse_tot = _att_combine([a[0] for a in att], [a[1] for a in att], qkvm, kv_mem[0], name="att_combine")
    mo0 = _mm(cat0, p["att_w_out"], name="att_out")
    x1, h1 = _post_pre(x, mo0, gains["norm_mix_post"][0], gains["norm_ffn_pre"][0], name="post_mix0")
    gu0, act0, f0 = _ffn_fwd(h1, p["ffn_w_gate_up"][0], p["ffn_w_down"][0], 0)
    x2, h2 = _post_pre(x1, f0, gains["norm_ffn_post"][0], gains["norm_mix_pre"][1], name="post_ffn0")

    proj = _mm(h2, p["dn_w_in"], name="dn_in")
    qkv = _dn_conv_fwd(proj, p["dn_conv"], name="dn_conv")
    gc, g_tot, beta = _dn_gates_fwd(proj, a_cols, dt_cols, name="dn_gates")
    o_fwd, o_rev, st_fwd, st_rev = _dn_scan_fwd(qkv, gc, g_tot, beta, name="dn_scan")
    cat1 = _dn_out(o_fwd, o_rev, proj, out_gain, kv_mem[1], name="dn_outnorm")
    mo1 = _mm(cat1, p["dn_w_out"], name="dn_out")
    x3, h3 = _post_pre(x2, mo1, gains["norm_mix_post"][1], gains["norm_ffn_pre"][1], name="post_mix1")
    gu1, act1, f1 = _ffn_fwd(h3, p["ffn_w_gate_up"][1], p["ffn_w_down"][1], 1)

    dx3, df1, dg_ffn_post1, loss_cols = _final_loss_bwd(x3, f1, gains["norm_ffn_post"][1], target, name="loss_bwd")
    dh3, d_wgu1, d_wd1 = _ffn_bwd(h3, gu1, act1, p["ffn_w_gate_up"][1], p["ffn_w_down"][1], df1, 1)
    dx2, dmo1, dg_mix_post1, dg_ffn_pre1 = _post_pre_bwd(x2, mo1, gains["norm_mix_post"][1], gains["norm_ffn_pre"][1],
                                                         dx3, dh3, name="post_mix1_bwd")
    dcat1 = _mm(dmo1, p["dn_w_out"], tb=True, name="dn_out_dx")
    g["dn_w_out"] = _mm(cat1, dmo1, ta=True, name="dn_out_dw")
    d_o, dz, dqm1, d_out_gain, dkv1 = _dn_out_bwd(o_fwd, o_rev, proj, out_gain, kv_mem[1], dcat1, name="dn_outnorm_bwd")
    d_f, d_r = _dn_scan_bwd(qkv, gc, g_tot, beta, (st_fwd, st_rev), d_o, name="dn_scan_bwd")
    d_gate_cols, d_a_cols, d_dt_cols = _dn_gates_bwd(proj, a_cols, dt_cols, (d_f[3], d_r[3]), (d_f[4], d_r[4]),
                                                     (d_f[5], d_r[5]), name="dn_gates_bwd")
    d_pre, d_conv = zip(*[_dn_conv_bwd(proj, p["dn_conv"], d_f[which], d_r[which], which, name=f"dn_conv_bwd{which}")
                          for which in range(3)])
    dproj = jnp.concatenate(list(d_pre) + [dz, dqm1, d_gate_cols], axis=1).astype(MXU_DTYPE)
    dh2 = _mm(dproj, p["dn_w_in"], tb=True, name="dn_in_dx")
    g["dn_w_in"] = _mm(h2, dproj, ta=True, name="dn_in_dw")
    g["dn_conv"] = jnp.concatenate(d_conv, axis=1)
    g["dn_a_log"] = _gate_params_bwd(d_a_cols)
    g["dn_dt_bias"] = _gate_params_bwd(d_dt_cols)
    g["dn_out_norm"] = d_out_gain

    dx1, df0, dg_ffn_post0, dg_mix_pre1 = _post_pre_bwd(x1, f0, gains["norm_ffn_post"][0], gains["norm_mix_pre"][1],
                                                        dx2, dh2, name="post_ffn0_bwd")
    dh1, d_wgu0, d_wd0 = _ffn_bwd(h1, gu0, act0, p["ffn_w_gate_up"][0], p["ffn_w_down"][0], df0, 0)
    dx0, dmo0, dg_mix_post0, dg_ffn_pre0 = _post_pre_bwd(x, mo0, gains["norm_mix_post"][0], gains["norm_ffn_pre"][0],
                                                         dx1, dh1, name="post_mix0_bwd")
    dcat0 = _mm(dmo0, p["att_w_out"], tb=True, name="att_out_dx")
    g["att_w_out"] = _mm(cat0, dmo0, ta=True, name="att_out_dw")
    delta, dqm0, dkv0 = _att_bwd_prep(cat0, dcat0, qkvm, kv_mem[0], name="att_bwd_prep")
    att_b = [_att_bwd(qkvm, bias[gi], lse_tot, delta, dcat0, gi, name=f"att_bwd{gi}") for gi in range(3)]
    dqkvm = jnp.concatenate([a[w] for w in range(3) for a in att_b] + [dqm0], axis=1).astype(MXU_DTYPE)
    g["rel_bias"] = sum(_bias_tiles_bwd(p["rel_bias"], att_b[gi][3], gi) for gi in range(3))
    dh0 = _mm(dqkvm, p["att_w_in"], tb=True, name="att_in_dx")
    g["att_w_in"] = _mm(h0, dqkvm, ta=True, name="att_in_dw")
    grad_x, dg_mix_pre0 = _pre_norm_bwd(x, gains["norm_mix_pre"][0], dh0, dx0, name="pre0_bwd")

    d_mem_kv, d_mem_norm = [], []
    for i, dkv in enumerate((dkv0, dkv1)):
        d_mem_kv.append(_mm(mem_n[i], dkv, ta=True, name=f"mem_kv_dw{i}"))
        d_mem_n = _mm(dkv, p["mem_w_kv"][i], tb=True, name=f"mem_kv_dx{i}")
        d_mem_norm.append(_gain_bwd(mem, gains["mem_norm"][i], d_mem_n, name=f"mem_norm_bwd{i}"))
    g["mem_w_kv"] = jnp.stack(d_mem_kv)
    g["mem_norm"] = jnp.concatenate(d_mem_norm, axis=0)
    g["norm_mix_pre"] = jnp.concatenate([dg_mix_pre0, dg_mix_pre1], axis=0)
    g["norm_mix_post"] = jnp.concatenate([dg_mix_post0, dg_mix_post1], axis=0)
    g["norm_ffn_pre"] = jnp.concatenate([dg_ffn_pre0, dg_ffn_pre1], axis=0)
    g["norm_ffn_post"] = jnp.concatenate([dg_ffn_post0, dg_ffn_post1], axis=0)
    g["ffn_w_gate_up"] = jnp.stack([d_wgu0, d_wgu1])
    g["ffn_w_down"] = jnp.stack([d_wd0, d_wd1])
    return loss_cols, grad_x, g


N_CHIPS = 4
N_DEV = 8
MESH = pl.DeviceIdType.MESH
BIG = (("att_w_in", (1, 1024, 640), 2), ("att_w_out", (1, 256, 1024), 1), ("dn_w_in", (1, 1024, 838), 2),
       ("dn_w_out", (1, 256, 1024), 1), ("mem_w_kv", (2, 256, 512), 1), ("ffn_w_gate_up", (2, 1024, 1408), 2),
       ("ffn_w_down", (2, 704, 1024), 1))
PACK_COLS = 1024
PACK_ELEMS = sum(math.prod(shape) for _, shape, _ in BIG)
HALF_ROWS = -(-PACK_ELEMS // (2 * PACK_COLS * 16)) * 16
PACK_ROWS = 2 * HALF_ROWS
HALF_TM = _tile(HALF_ROWS, 512, 16)


def _pack_local(blocks):
    flat = [blocks[n].reshape(-1) for n, _, _ in BIG]
    dtype = flat[0].dtype
    flat.append(jnp.zeros((PACK_ROWS * PACK_COLS - PACK_ELEMS,), dtype))
    return jnp.concatenate(flat).reshape(PACK_ROWS, PACK_COLS)


def _unpack_local(slab):
    flat = slab.reshape(-1)
    out, off = {}, 0
    for n, shape, _ in BIG:
        size = math.prod(shape)
        out[n] = flat[off: off + size].reshape(shape)
        off += size
    return out


def _full_from_gathered(gathered):
    per_chip = [_unpack_local(gathered[s]) for s in range(N_CHIPS)]
    return {n: jnp.concatenate([pc[n] for pc in per_chip], axis=axis) for n, _, axis in BIG}


def _pack_full(full):
    slabs = []
    for s in range(N_CHIPS):
        blocks = {}
        for n, shape, axis in BIG:
            blocks[n] = lax.slice_in_dim(full[n], s * shape[axis], (s + 1) * shape[axis], axis=axis)
        slabs.append(_pack_local(blocks))
    return jnp.stack(slabs)


def _mesh_pos():
    return lax.axis_index("x"), lax.axis_index("y"), lax.axis_index("c")


def _other_chips(x, y):
    return [(1 - x, y), (x, 1 - y), (1 - x, 1 - y)]


ANY = pl.BlockSpec(memory_space=pl.ANY)


def _gather_shards(slab, *, name):
    rows, cols = slab.shape
    half = rows // 2

    def body(x_ref, out_ref, send_sems, recv_sems, local_sem):
        x, y, c = _mesh_pos()
        me = 2 * x + y
        sibling = (x, y, 1 - c)
        chips = _other_chips(x, y)

        def part(chip, h):
            return out_ref.at[2 * chip[0] + chip[1], pl.ds(h * half, half), :]

        def copy(k, src, dst, to):
            return pltpu.make_async_remote_copy(src_ref=src, dst_ref=dst, send_sem=send_sems.at[k],
                                                recv_sem=recv_sems.at[k], device_id=to, device_id_type=MESH)

        mine = pltpu.make_async_copy(x_ref, out_ref.at[me], local_sem)
        mine.start()
        my_half = x_ref.at[pl.ds(c * half, half), :]
        first = [copy(j, my_half, part((x, y), c), (*chip, c)) for j, chip in enumerate(chips)]
        for cp in first:
            cp.start()
        passed = [copy(3 + j, part(chip, c), part(chip, c), sibling) for j, chip in enumerate(chips)]
        for j, chip in enumerate(chips):
            copy(j, my_half, part(chip, c), (*chip, c)).wait_recv()
            passed[j].start()
        for j, chip in enumerate(chips):
            copy(3 + j, part(chip, 1 - c), part(chip, 1 - c), sibling).wait_recv()
        for cp in first + passed:
            cp.wait_send()
        mine.wait()

    return pl.pallas_call(
        body, name=name, in_specs=[ANY], out_specs=ANY,
        out_shape=jax.ShapeDtypeStruct((N_CHIPS, rows, cols), slab.dtype),
        scratch_shapes=[pltpu.SemaphoreType.DMA((6,)), pltpu.SemaphoreType.DMA((6,)), pltpu.SemaphoreType.DMA],
    )(slab)


def _swap_other_half(slabs, *, name):
    n, rows, cols = slabs.shape
    half = rows // 2

    def body(g_ref, out_ref, send_sem, recv_sem):
        x, y, c = _mesh_pos()
        cp = pltpu.make_async_remote_copy(src_ref=g_ref.at[:, pl.ds((1 - c) * half, half), :], dst_ref=out_ref,
                                          send_sem=send_sem, recv_sem=recv_sem, device_id=(x, y, 1 - c), device_id_type=MESH)
        cp.start()
        cp.wait()

    return pl.pallas_call(
        body, name=name, in_specs=[ANY], out_specs=ANY, out_shape=jax.ShapeDtypeStruct((n, half, cols), slabs.dtype),
        scratch_shapes=[pltpu.SemaphoreType.DMA, pltpu.SemaphoreType.DMA],
    )(slabs)


def _add_own_half(slabs, received, core, *, name):
    n, rows, cols = slabs.shape
    half = rows // 2
    nb = half // HALF_TM

    def kern(c_ref, a_ref, b_ref, o_ref):
        o_ref[...] = a_ref[...] + b_ref[...]

    return pl.pallas_call(
        kern, name=name,
        grid_spec=pltpu.PrefetchScalarGridSpec(
            num_scalar_prefetch=1, grid=(n, nb),
            in_specs=[pl.BlockSpec((None, HALF_TM, cols), lambda s, i, c: (s, c[0] * nb + i, 0)),
                      pl.BlockSpec((None, HALF_TM, cols), lambda s, i, c: (s, i, 0))],
            out_specs=pl.BlockSpec((None, HALF_TM, cols), lambda s, i, c: (s, i, 0))),
        out_shape=jax.ShapeDtypeStruct((n, half, cols), F32),
        compiler_params=_params("parallel", "parallel"),
    )(core, slabs, received)


def _scatter_to_chips(sums, *, name):
    n, half, cols = sums.shape

    def body(s_ref, out_ref, send_sems, recv_sems, local_sem):
        x, y, c = _mesh_pos()
        me = 2 * x + y
        chips = _other_chips(x, y)

        def copy(k, chip):
            return pltpu.make_async_remote_copy(src_ref=s_ref.at[2 * chip[0] + chip[1]], dst_ref=out_ref.at[me],
                                                send_sem=send_sems.at[k], recv_sem=recv_sems.at[k],
                                                device_id=(*chip, c), device_id_type=MESH)

        mine = pltpu.make_async_copy(s_ref.at[me], out_ref.at[me], local_sem)
        mine.start()
        sends = [copy(j, chip) for j, chip in enumerate(chips)]
        for cp in sends:
            cp.start()
        for j, chip in enumerate(chips):
            pltpu.make_async_remote_copy(src_ref=s_ref.at[me], dst_ref=out_ref.at[2 * chip[0] + chip[1]],
                                         send_sem=send_sems.at[j], recv_sem=recv_sems.at[j],
                                         device_id=(*chip, c), device_id_type=MESH).wait_recv()
        for cp in sends:
            cp.wait_send()
        mine.wait()

    return pl.pallas_call(
        body, name=name, in_specs=[ANY], out_specs=ANY, out_shape=jax.ShapeDtypeStruct((n, half, cols), sums.dtype),
        scratch_shapes=[pltpu.SemaphoreType.DMA((3,)), pltpu.SemaphoreType.DMA((3,)), pltpu.SemaphoreType.DMA],
    )(sums)


def _sum_chips(parts, *, name):
    n, half, cols = parts.shape

    def kern(p_ref, o_ref):
        acc = p_ref[0]
        for s in range(1, n):
            acc = acc + p_ref[s]
        o_ref[...] = acc

    return pl.pallas_call(
        kern, name=name, grid=(half // HALF_TM,),
        in_specs=[pl.BlockSpec((n, HALF_TM, cols), lambda i: (0, i, 0))],
        out_specs=pl.BlockSpec((HALF_TM, cols), lambda i: (i, 0)),
        out_shape=jax.ShapeDtypeStruct((half, cols), F32),
        compiler_params=_params("parallel"),
    )(parts)


def _join_halves(mine, *, name):
    half, cols = mine.shape

    def body(m_ref, out_ref, send_sem, recv_sem, local_sem):
        x, y, c = _mesh_pos()
        own = out_ref.at[pl.ds(c * half, half), :]
        local = pltpu.make_async_copy(m_ref, own, local_sem)
        local.start()
        cp = pltpu.make_async_remote_copy(src_ref=m_ref, dst_ref=own, send_sem=send_sem, recv_sem=recv_sem,
                                          device_id=(x, y, 1 - c), device_id_type=MESH)
        cp.start()
        pltpu.make_async_remote_copy(src_ref=m_ref, dst_ref=out_ref.at[pl.ds((1 - c) * half, half), :], send_sem=send_sem,
                                     recv_sem=recv_sem, device_id=(x, y, 1 - c), device_id_type=MESH).wait_recv()
        cp.wait_send()
        local.wait()

    return pl.pallas_call(
        body, name=name, in_specs=[ANY], out_specs=ANY, out_shape=jax.ShapeDtypeStruct((2 * half, cols), mine.dtype),
        scratch_shapes=[pltpu.SemaphoreType.DMA, pltpu.SemaphoreType.DMA, pltpu.SemaphoreType.DMA],
    )(mine)


def _all_reduce_small(v, *, name):
    rows, cols = v.shape
    flips = [(dx, dy, dc) for dx in (0, 1) for dy in (0, 1) for dc in (0, 1)][1:]

    def body(v_ref, o_ref, buf, send_sems, recv_sems):
        x, y, c = _mesh_pos()

        def peer(f):
            return tuple(1 - p if fl else p for p, fl in zip((x, y, c), f))

        def index(p):
            return 4 * p[0] + 2 * p[1] + p[2]

        buf[index((x, y, c))] = v_ref[...]
        sends = []
        for k, f in enumerate(flips):
            cp = pltpu.make_async_remote_copy(src_ref=v_ref, dst_ref=buf.at[index((x, y, c))], send_sem=send_sems.at[k],
                                              recv_sem=recv_sems.at[k], device_id=peer(f), device_id_type=MESH)
            cp.start()
            sends.append(cp)
        for k, f in enumerate(flips):
            pltpu.make_async_remote_copy(src_ref=v_ref, dst_ref=buf.at[index(peer(f))], send_sem=send_sems.at[k],
                                         recv_sem=recv_sems.at[k], device_id=peer(f), device_id_type=MESH).wait_recv()
        for cp in sends:
            cp.wait_send()
        acc = buf[0]
        for d in range(1, N_DEV):
            acc = acc + buf[d]
        o_ref[...] = acc

    vmem = pl.BlockSpec(memory_space=pltpu.VMEM)
    return pl.pallas_call(
        body, name=name, in_specs=[vmem], out_specs=vmem, out_shape=jax.ShapeDtypeStruct((rows, cols), F32),
        scratch_shapes=[pltpu.VMEM((N_DEV, rows, cols), F32), pltpu.SemaphoreType.DMA((N_DEV - 1,)),
                        pltpu.SemaphoreType.DMA((N_DEV - 1,))],
    )(v)


def _reduce_scatter(slabs, core):
    received = _swap_other_half(slabs, name="rs_swap")
    sums = _add_own_half(slabs, received, core, name="rs_add")
    parts = _scatter_to_chips(sums, name="rs_scatter")
    return _join_halves(_sum_chips(parts, name="rs_sum"), name="rs_join")


def _adamw(w, g, m, v, *, name):
    def body(w, g, m, v):
        m = ADAM_B1 * m + (1.0 - ADAM_B1) * g
        v = ADAM_B2 * v + (1.0 - ADAM_B2) * (g * g)
        m_hat = m / (1.0 - ADAM_B1 ** ADAM_STEP)
        v_hat = v / (1.0 - ADAM_B2 ** ADAM_STEP)
        delta = -ADAM_LR * (m_hat / (jnp.sqrt(v_hat) + ADAM_EPS) + ADAM_WD * w)
        return (delta, m, v), ()
    rows, cols = w.shape
    return _rowwise(body, [w, g, m, v], [], [(cols, F32)] * 3, [], tm=_tile(rows, 256, SUBLANES), name=name)


def _pack_small(arrs, rows):
    flat = jnp.concatenate([a.reshape(-1) for a in arrs])
    return jnp.pad(flat, (0, rows * LANES - flat.shape[0])).reshape(rows, LANES)


def _unpack_small(packed, shapes):
    flat = packed.reshape(-1)
    out, off = [], 0
    for s in shapes:
        size = math.prod(s)
        out.append(flat[off: off + size].reshape(s))
        off += size
    return out


def _small_rows(shapes):
    return -(-sum(math.prod(s) for s in shapes) // (SUBLANES * LANES)) * SUBLANES


WEIGHTS = ("rel_bias", "att_w_in", "att_w_out", "dn_w_in", "dn_conv", "dn_a_log", "dn_dt_bias", "dn_out_norm", "dn_w_out",
           "mem_norm", "mem_w_kv", "norm_mix_pre", "norm_mix_post", "norm_ffn_pre", "norm_ffn_post", "ffn_w_gate_up",
           "ffn_w_down")
BIG_NAMES = tuple(n for n, _, _ in BIG)
SMALL_NAMES = tuple(n for n in WEIGHTS if n not in BIG_NAMES)
CONV_COLS = 3 * TOK_WIDTH
CONV_SHARD = CONV_COLS // N_CHIPS


def kernel(x, mem, rel_bias, att_w_in, att_w_out, dn_w_in, dn_conv, dn_a_log, dn_dt_bias, dn_out_norm, dn_w_out, mem_norm, mem_w_kv, norm_mix_pre, norm_mix_post, norm_ffn_pre, norm_ffn_post, ffn_w_gate_up, ffn_w_down, loss_target, m_rel_bias, m_att_w_in, m_att_w_out, m_dn_w_in, m_dn_conv, m_dn_a_log, m_dn_dt_bias, m_dn_out_norm, m_dn_w_out, m_mem_norm, m_mem_w_kv, m_norm_mix_pre, m_norm_mix_post, m_norm_ffn_pre, m_norm_ffn_post, m_ffn_w_gate_up, m_ffn_w_down, v_rel_bias, v_att_w_in, v_att_w_out, v_dn_w_in, v_dn_conv, v_dn_a_log, v_dn_dt_bias, v_dn_out_norm, v_dn_w_out, v_mem_norm, v_mem_w_kv, v_norm_mix_pre, v_norm_mix_post, v_norm_ffn_pre, v_norm_ffn_post, v_ffn_w_gate_up, v_ffn_w_down):
    w = dict(zip(WEIGHTS, (rel_bias, att_w_in, att_w_out, dn_w_in, dn_conv, dn_a_log, dn_dt_bias, dn_out_norm, dn_w_out,
                           mem_norm, mem_w_kv, norm_mix_pre, norm_mix_post, norm_ffn_pre, norm_ffn_post, ffn_w_gate_up,
                           ffn_w_down)))
    m = dict(zip(WEIGHTS, (m_rel_bias, m_att_w_in, m_att_w_out, m_dn_w_in, m_dn_conv, m_dn_a_log, m_dn_dt_bias,
                           m_dn_out_norm, m_dn_w_out, m_mem_norm, m_mem_w_kv, m_norm_mix_pre, m_norm_mix_post,
                           m_norm_ffn_pre, m_norm_ffn_post, m_ffn_w_gate_up, m_ffn_w_down)))
    v = dict(zip(WEIGHTS, (v_rel_bias, v_att_w_in, v_att_w_out, v_dn_w_in, v_dn_conv, v_dn_a_log, v_dn_dt_bias,
                           v_dn_out_norm, v_dn_w_out, v_mem_norm, v_mem_w_kv, v_norm_mix_pre, v_norm_mix_post,
                           v_norm_ffn_pre, v_norm_ffn_post, v_ffn_w_gate_up, v_ffn_w_down)))
    cx, cy, cc = _mesh_pos()
    chip = 2 * cx + cy

    gathered = _gather_shards(_pack_local({n: w[n] for n in BIG_NAMES}).astype(MXU_DTYPE), name="gather_weights")
    full = _full_from_gathered(gathered)
    conv_rows = _small_rows([(DN_CONV, CONV_COLS)])
    conv_mine = jnp.where(cc == 0, 1.0, 0.0) * w["dn_conv"][0]
    conv_placed = lax.dynamic_update_slice(jnp.zeros((DN_CONV, CONV_COLS), F32), conv_mine, (0, chip * CONV_SHARD))
    conv_full = _unpack_small(_all_reduce_small(_pack_small([conv_placed], conv_rows), name="gather_conv"),
                              [(DN_CONV, CONV_COLS)])[0]
    p = {
        "rel_bias": w["rel_bias"], "att_w_in": full["att_w_in"][0], "att_w_out": full["att_w_out"][0],
        "dn_w_in": _pad_dn_w_in(full["dn_w_in"][0]), "dn_conv": conv_full, "dn_a_log": w["dn_a_log"][0],
        "dn_dt_bias": w["dn_dt_bias"][0], "dn_out_norm": w["dn_out_norm"][0], "dn_w_out": full["dn_w_out"][0],
        "mem_norm": w["mem_norm"], "mem_w_kv": full["mem_w_kv"], "norm_mix_pre": w["norm_mix_pre"],
        "norm_mix_post": w["norm_mix_post"], "norm_ffn_pre": w["norm_ffn_pre"], "norm_ffn_post": w["norm_ffn_post"],
        "ffn_w_gate_up": full["ffn_w_gate_up"], "ffn_w_down": full["ffn_w_down"],
    }

    loss_cols, grad_x, g = _local_step(x[0], mem[0], loss_target[0], p)
    loss = lax.psum(jnp.sum(loss_cols), ("x", "y", "c"))

    g_big = {"att_w_in": g["att_w_in"][None], "att_w_out": g["att_w_out"][None], "dn_w_in": _unpad_dn_w_in(g["dn_w_in"])[None],
             "dn_w_out": g["dn_w_out"][None], "mem_w_kv": g["mem_w_kv"], "ffn_w_gate_up": g["ffn_w_gate_up"],
             "ffn_w_down": g["ffn_w_down"]}
    reduced = _reduce_scatter(_pack_full(g_big), cc.astype(jnp.int32).reshape(1))
    grads = _unpack_local(reduced)
    small_full_shapes = [(DN_CONV, CONV_COLS) if n == "dn_conv" else w[n].shape for n in SMALL_NAMES]
    small_sum = _all_reduce_small(_pack_small([g[n] for n in SMALL_NAMES], _small_rows(small_full_shapes)), name="reduce_small")
    for n, s in zip(SMALL_NAMES, _unpack_small(small_sum, small_full_shapes)):
        grads[n] = lax.dynamic_slice(s, (0, chip * CONV_SHARD), (DN_CONV, CONV_SHARD))[None] if n == "dn_conv" else s

    delta, new_m, new_v = {}, {}, {}
    for n in BIG_NAMES:
        shape = w[n].shape
        two_d = lambda a: a.reshape(-1, shape[-1])
        res = _adamw(two_d(w[n]), two_d(grads[n]), two_d(m[n]), two_d(v[n]), name=f"adamw_{n}")
        delta[n], new_m[n], new_v[n] = (r.reshape(shape) for r in res)
    small_shapes = [w[n].shape for n in SMALL_NAMES]
    rows = _small_rows(small_shapes)
    res = _adamw(*[_pack_small([d[n] for n in SMALL_NAMES], rows) for d in (w, grads, m, v)], name="adamw_small")
    for d, r in zip((delta, new_m, new_v), res):
        for n, a in zip(SMALL_NAMES, _unpack_small(r, small_shapes)):
            d[n] = a
    return (loss, grad_x[None], *[grads[n] for n in WEIGHTS], *[delta[n] for n in WEIGHTS],
            *[new_m[n] for n in WEIGHTS], *[new_v[n] for n in WEIGHTS])
```

```python
import functools
import math

import numpy as np
import jax
import jax.numpy as jnp
from jax import lax
from jax.experimental import pallas as pl
from jax.experimental.pallas import tpu as pltpu

F32 = jnp.float32
MXU_DTYPE = jnp.bfloat16
LINK_DTYPE = jnp.bfloat16
HI = lax.Precision.HIGHEST

EPS = 1e-6
NEG_INF = -1e30
LANES = 128
SUBLANES = 8
VMEM_LIMIT = 56 * 1024 * 1024

D_MODEL = 1024
TOK_WIDTH = 768
MEM_WIDTH = 256
MEM_LEN = 256
ATT_HEAD_DIM = 64
DILATIONS = (1, 4, 16)
HALF = 64
ATT_BQ = 128
ATT_W = ATT_BQ + 2 * HALF
REL_BUCKETS = 32
REL_MAX_DIST = 1024
DN_HEADS = 6
DN_HEAD_DIM = 128
DN_CONV = 5
DN_CHUNK = 128
D_FF = 2816
ATT_IN = 2560
DN_IN = 3352
DN_IN_PAD = 3456
N_GATES = 4 * DN_HEADS

ADAM_LR = 0.001
ADAM_B1 = 0.9
ADAM_B2 = 0.999
ADAM_EPS = 1e-08
ADAM_WD = 0.01
ADAM_STEP = 10


def _tile(n, target, align):
    if n <= target:
        return n
    t = (target // align) * align
    while t >= align:
        if n % t == 0:
            return t
        t -= align
    raise ValueError(f"no tile for {n} (target {target}, align {align})")


def _params(*sem):
    return pltpu.CompilerParams(dimension_semantics=sem, vmem_limit_bytes=VMEM_LIMIT)


def _mm(a, b, *, name, ta=False, tb=False, tm=1024, tn=1408, tk=1408, out_dtype=F32):
    if ta:
        K, M = a.shape
    else:
        M, K = a.shape
    if tb:
        N, K2 = b.shape
    else:
        K2, N = b.shape
    assert K == K2, (a.shape, b.shape, ta, tb)
    tm = _tile(M, tm, LANES if ta else SUBLANES)
    tn = _tile(N, tn, LANES)
    tk = _tile(K, tk, LANES)
    nk = K // tk
    a_spec = pl.BlockSpec((tk, tm), lambda i, j, k: (k, i)) if ta else pl.BlockSpec((tm, tk), lambda i, j, k: (i, k))
    b_spec = pl.BlockSpec((tn, tk), lambda i, j, k: (j, k)) if tb else pl.BlockSpec((tk, tn), lambda i, j, k: (k, j))
    dims = (((0 if ta else 1,), (1 if tb else 0,)), ((), ()))

    def kern(a_ref, b_ref, o_ref, acc_ref):
        k = pl.program_id(2)

        @pl.when(k == 0)
        def _():
            acc_ref[...] = jnp.zeros_like(acc_ref)

        acc_ref[...] += lax.dot_general(a_ref[...].astype(MXU_DTYPE), b_ref[...].astype(MXU_DTYPE), dims,
                                        preferred_element_type=F32)

        @pl.when(k == nk - 1)
        def _():
            o_ref[...] = acc_ref[...].astype(o_ref.dtype)

    return pl.pallas_call(
        kern, name=name, grid=(M // tm, N // tn, nk), in_specs=[a_spec, b_spec],
        out_specs=pl.BlockSpec((tm, tn), lambda i, j, k: (i, j)),
        out_shape=jax.ShapeDtypeStruct((M, N), out_dtype),
        scratch_shapes=[pltpu.VMEM((tm, tn), F32)],
        compiler_params=_params("parallel", "parallel", "arbitrary"),
    )(a, b)


def _col(arr, width, blk):
    return (arr, width, blk)


def _rowwise(body, rows, consts, out_rows, out_acc, *, tm, name):
    n_rows = (rows[0][0] if isinstance(rows[0], tuple) else rows[0]).shape[0]
    assert n_rows % tm == 0, (n_rows, tm)
    arrs, in_specs = [], []
    for r in rows:
        arr, width, blk = r if isinstance(r, tuple) else (r, r.shape[1], 0)
        assert arr.shape[0] == n_rows
        arrs.append(arr)
        in_specs.append(pl.BlockSpec((tm, width), functools.partial(lambda i, b: (i, b), b=blk)))
    for c in consts:
        arrs.append(c)
        in_specs.append(pl.BlockSpec(c.shape, functools.partial(lambda i, n: (0,) * n, n=c.ndim)))
    n_in, n_ro = len(arrs), len(out_rows)
    out_shape = [jax.ShapeDtypeStruct((n_rows, w), dt) for w, dt in out_rows]
    out_specs = [pl.BlockSpec((tm, w), lambda i: (i, 0)) for w, _ in out_rows]
    out_shape += [jax.ShapeDtypeStruct(s, F32) for s in out_acc]
    out_specs += [pl.BlockSpec(s, lambda i: (0, 0)) for s in out_acc]

    def kern(*refs):
        ro, ao = body(*[r[...] for r in refs[:n_in]])
        outs = refs[n_in:]
        for r, v in zip(outs[:n_ro], ro, strict=True):
            r[...] = v.astype(r.dtype)
        if out_acc:
            @pl.when(pl.program_id(0) == 0)
            def _():
                for r in outs[n_ro:]:
                    r[...] = jnp.zeros_like(r)

            for r, v in zip(outs[n_ro:], ao, strict=True):
                r[...] += v

    res = pl.pallas_call(
        kern, name=name, grid=(n_rows // tm,), in_specs=in_specs, out_specs=out_specs, out_shape=out_shape,
        compiler_params=_params("arbitrary" if out_acc else "parallel"),
    )(*arrs)
    return res


def _rms(x, gain):
    return x * lax.rsqrt(jnp.mean(x * x, axis=-1, keepdims=True) + EPS) * gain


def _silu(x):
    return x * jax.nn.sigmoid(x)


def _softplus(x):
    return jnp.maximum(x, 0.0) + jnp.log(1.0 + jnp.exp(-jnp.abs(x)))


def _dot_nt(a, b, precision=None):
    return lax.dot_general(a, b, (((1,), (1,)), ((), ())), preferred_element_type=F32, precision=precision)


def _dot_tn(a, b, precision=None):
    return lax.dot_general(a, b, (((0,), (0,)), ((), ())), preferred_element_type=F32, precision=precision)


def _dot(a, b, precision=None):
    return jnp.dot(a, b, preferred_element_type=F32, precision=precision)


def _pre_norm(x, gain, *, name):
    def body(x, g):
        return (_rms(x, g),), ()
    return _rowwise(body, [x], [gain], [(x.shape[1], MXU_DTYPE)], [], tm=_tile(x.shape[0], 512, 2 * SUBLANES), name=name)[0]


def _pre_norm_bwd(x, gain, dh, dx_other, *, name):
    def body(x, dh, dxo, g):
        _, vjp = jax.vjp(_rms, x, g)
        dx, dg = vjp(dh)
        return (dx + dxo,), (dg,)
    return _rowwise(body, [x, dh, dx_other], [gain], [(x.shape[1], F32)], [gain.shape], tm=512, name=name)


def _gain_bwd(x, gain, dh, *, name):
    def body(x, dh, g):
        _, vjp = jax.vjp(lambda g_: _rms(x, g_), g)
        return (), (vjp(dh)[0],)
    return _rowwise(body, [x, dh], [gain], [], [gain.shape], tm=_tile(x.shape[0], 512, SUBLANES), name=name)[0]


def _res_block(x_res, m, g_post, g_pre):
    x_new = x_res + _rms(m, g_post)
    return x_new, _rms(x_new, g_pre)


def _post_pre(x_res, m, g_post, g_pre, *, name):
    def body(x, m, gp, gq):
        return _res_block(x, m, gp, gq), ()
    d = x_res.shape[1]
    return _rowwise(body, [x_res, m], [g_post, g_pre], [(d, F32), (d, MXU_DTYPE)], [], tm=512, name=name)


def _post_pre_bwd(x_res, m, g_post, g_pre, dx_new, dh, *, name):
    def body(x, m, dxn, dh, gp, gq):
        _, vjp = jax.vjp(_res_block, x, m, gp, gq)
        dx, dm, dgp, dgq = vjp((dxn, dh))
        return (dx, dm), (dgp, dgq)
    d = x_res.shape[1]
    return _rowwise(body, [x_res, m, dx_new, dh], [g_post, g_pre], [(d, F32), (d, MXU_DTYPE)],
                    [g_post.shape, g_pre.shape], tm=256, name=name)


def _final_loss_bwd(x_res, m, g_post, target, *, name):
    d = x_res.shape[1]

    def loss_cols(x, m, g, t):
        err = x + _rms(m, g) - t
        return jnp.sum(err * err, axis=0, keepdims=True) * (0.5 / d)

    def body(x, m, t, g):
        cols, vjp = jax.vjp(lambda x_, m_, g_: loss_cols(x_, m_, g_, t), x, m, g)
        dx, dm, dg = vjp(jnp.ones_like(cols))
        return (dx, dm), (dg, cols)
    return _rowwise(body, [x_res, m, target], [g_post], [(d, F32), (d, MXU_DTYPE)], [g_post.shape, (1, d)], tm=256, name=name)


def _swiglu_act(gu, *, name):
    def body(gate, up):
        return (_silu(gate) * up,), ()
    return _rowwise(body, [_col(gu, D_FF, 0), _col(gu, D_FF, 1)], [], [(D_FF, MXU_DTYPE)], [], tm=256, name=name)[0]


def _swiglu_act_bwd(gu, da, *, name):
    def body(gate, up, da):
        _, vjp = jax.vjp(lambda g, u: _silu(g) * u, gate, up)
        dg, du = vjp(da)
        return (jnp.concatenate([dg, du], axis=1),), ()
    return _rowwise(body, [_col(gu, D_FF, 0), _col(gu, D_FF, 1), da], [], [(2 * D_FF, MXU_DTYPE)], [], tm=256, name=name)[0]


def _lane_head_mask(width, head_dim, head):
    lane = lax.broadcasted_iota(jnp.int32, (1, width), 1)
    return (lane // head_dim) == head


def _mem_attn_pair(q_pair, k_pair, v_pair):
    out = jnp.zeros_like(q_pair)
    for h in range(2):
        mh = _lane_head_mask(LANES, ATT_HEAD_DIM, h)
        qh = jnp.where(mh, q_pair * (ATT_HEAD_DIM ** -0.5), 0.0)
        logits = _dot_nt(qh, k_pair)
        mx = jnp.max(logits, axis=-1, keepdims=True)
        p = jnp.exp(logits - mx)
        p = p / jnp.sum(p, axis=-1, keepdims=True)
        out = out + jnp.where(mh, _dot(p, v_pair), 0.0)
    return out


def _mem_attn(q_mem, kv):
    outs = []
    for p in range(MEM_WIDTH // LANES):
        sl = slice(p * LANES, (p + 1) * LANES)
        outs.append(_mem_attn_pair(q_mem[:, sl], kv[:, sl], kv[:, MEM_WIDTH + p * LANES: MEM_WIDTH + (p + 1) * LANES]))
    return jnp.concatenate(outs, axis=1)


def _mem_attn_bwd(q_mem, kv, do):
    dqs, dks, dvs = [], [], []
    for p in range(MEM_WIDTH // LANES):
        sl = slice(p * LANES, (p + 1) * LANES)
        sv = slice(MEM_WIDTH + p * LANES, MEM_WIDTH + (p + 1) * LANES)
        _, vjp = jax.vjp(_mem_attn_pair, q_mem[:, sl], kv[:, sl], kv[:, sv])
        dq, dk, dv = vjp(do[:, sl])
        dqs.append(dq)
        dks.append(dk)
        dvs.append(dv)
    return jnp.concatenate(dqs, axis=1), jnp.concatenate(dks + dvs, axis=1)


def _t5_bucket(rel):
    half = REL_BUCKETS // 2
    max_exact = half // 2
    n = np.abs(rel)
    large = max_exact + (np.log(np.maximum(n, 1) / max_exact) / math.log(REL_MAX_DIST / max_exact)
                         * (half - max_exact)).astype(np.int64)
    large = np.minimum(large, half - 1)
    return ((rel > 0) * half + np.where(n < max_exact, n, large)).astype(np.int32)


ATT_DIAGS = ATT_BQ + ATT_W - 1


def _bias_diag_onehot(dil):
    j = np.arange(ATT_DIAGS)
    tiles = []
    for off in (-HALF, 0, HALF):
        rel = j - (ATT_BQ - 1) - HALF - off
        hot = _t5_bucket(rel * dil)[:, None] == np.arange(REL_BUCKETS)[None, :]
        tiles.append(hot & (np.abs(rel) <= HALF)[:, None])
    return np.stack(tiles).astype(np.float32)


def _toeplitz(r):
    lead = r.shape[:-1]
    a = jnp.broadcast_to(r[..., None, :], lead + (ATT_BQ, ATT_DIAGS))
    a = jnp.pad(a, [(0, 0)] * len(lead) + [(0, 0), (0, 1)])
    a = a.reshape(lead + (ATT_BQ * (ATT_DIAGS + 1),))[..., : ATT_BQ * ATT_DIAGS].reshape(lead + (ATT_BQ, ATT_DIAGS))
    return a[..., ATT_BQ - 1: ATT_BQ - 1 + ATT_W]


def _bias_tiles(rel_bias, gi):
    heads = rel_bias[:, 4 * gi: 4 * gi + 4]
    diag = jnp.einsum('tnb,bh->thn', jnp.asarray(_bias_diag_onehot(DILATIONS[gi])), heads, precision=HI)
    return _toeplitz(diag)


def _bias_tiles_bwd(rel_bias, dtiles, gi):
    return jax.vjp(lambda rb: _bias_tiles(rb, gi), rel_bias)[1](dtiles)[0]


def _att_window(i, n_sub):
    start = jnp.clip(i * ATT_BQ - HALF, 0, n_sub - ATT_W)
    off = i * ATT_BQ - HALF - start
    return pl.multiple_of(start, HALF), off


def _att_valid(off):
    q = lax.broadcasted_iota(jnp.int32, (ATT_BQ, ATT_W), 0)
    kk = lax.broadcasted_iota(jnp.int32, (ATT_BQ, ATT_W), 1)
    return jnp.abs(kk - q - HALF - off) <= HALF


def _att_tile_id(i, nq):
    return jnp.where(i == 0, 0, jnp.where(i == nq - 1, 2, 1))


def _att_fwd(qkvm, bias, gi, *, name):
    dil = DILATIONS[gi]
    s_len = qkvm.shape[0]
    n_sub = s_len // dil
    nq = n_sub // ATT_BQ
    assert n_sub % ATT_BQ == 0 and n_sub >= ATT_W
    cols = qkvm.shape[1] // LANES
    view = qkvm.reshape(n_sub, dil * qkvm.shape[1])

    def kern(q_ref, k_ref, v_ref, b_ref, o_ref, lse_ref):
        i = pl.program_id(2)
        start, off = _att_window(i, n_sub)
        valid = _att_valid(off)
        q = q_ref[...] * (ATT_HEAD_DIM ** -0.5)
        kw = k_ref[pl.ds(start, ATT_W), :]
        vw = v_ref[pl.ds(start, ATT_W), :]
        o = jnp.zeros((ATT_BQ, LANES), F32)
        lse = jnp.zeros((ATT_BQ, LANES), F32)
        for h in range(2):
            mh = _lane_head_mask(LANES, ATT_HEAD_DIM, h)
            s = _dot_nt(jnp.where(mh, q, 0.0), kw) + b_ref[h]
            s = jnp.where(valid, s, NEG_INF)
            mx = jnp.max(s, axis=-1, keepdims=True)
            p = jnp.exp(s - mx)
            den = jnp.sum(p, axis=-1, keepdims=True)
            o = jnp.where(mh, _dot(p, vw) / den, o)
            lse = jnp.where(mh, mx + jnp.log(den), lse)
        o_ref[...] = o
        lse_ref[...] = lse

    def qkv_spec(which, full):
        shape = (n_sub, LANES) if full else (ATT_BQ, LANES)
        return pl.BlockSpec(shape, lambda pr, r, i: (0 if full else i, r * cols + which * 6 + 2 * gi + pr))

    out_spec = pl.BlockSpec((ATT_BQ, LANES), lambda pr, r, i: (i, r * 2 + pr))
    o, lse = pl.pallas_call(
        kern, name=name, grid=(2, dil, nq),
        in_specs=[qkv_spec(0, False), qkv_spec(1, True), qkv_spec(2, True),
                  pl.BlockSpec((None, 2, ATT_BQ, ATT_W), lambda pr, r, i: (_att_tile_id(i, nq), pr, 0, 0))],
        out_specs=[out_spec, out_spec],
        out_shape=[jax.ShapeDtypeStruct((n_sub, dil * 2 * LANES), F32)] * 2,
        compiler_params=_params("parallel", "parallel", "arbitrary"),
    )(view, view, view, bias)
    return o.reshape(s_len, 2 * LANES), lse.reshape(s_len, 2 * LANES)


def _att_bwd(qkvm, bias, lse_tot, delta, dcat, gi, *, name):
    dil = DILATIONS[gi]
    s_len = qkvm.shape[0]
    n_sub = s_len // dil
    nq = n_sub // ATT_BQ
    cols = qkvm.shape[1] // LANES
    dcols = dcat.shape[1] // LANES
    view = qkvm.reshape(n_sub, dil * qkvm.shape[1])
    lse_v = lse_tot.reshape(n_sub, dil * 2 * LANES)
    delta_v = delta.reshape(n_sub, dil * 2 * LANES)
    dcat_v = dcat.reshape(n_sub, dil * dcat.shape[1])

    def kern(q_ref, k_ref, v_ref, b_ref, lse_ref, dl_ref, dm_ref, dq_ref, dk_ref, dv_ref, db_ref):
        r, i = pl.program_id(1), pl.program_id(2)
        start, off = _att_window(i, n_sub)
        valid = _att_valid(off)
        tile = _att_tile_id(i, nq)

        @pl.when(i == 0)
        def _():
            dk_ref[...] = jnp.zeros_like(dk_ref)
            dv_ref[...] = jnp.zeros_like(dv_ref)

        @pl.when((i == 0) & (r == 0))
        def _():
            db_ref[...] = jnp.zeros_like(db_ref)

        q = q_ref[...] * (ATT_HEAD_DIM ** -0.5)
        kw = k_ref[pl.ds(start, ATT_W), :]
        vw = v_ref[pl.ds(start, ATT_W), :]
        dm = dm_ref[...]
        lse = lse_ref[...]
        dl = dl_ref[...]
        dq = jnp.zeros((ATT_BQ, LANES), F32)
        dkw = jnp.zeros((ATT_W, LANES), F32)
        dvw = jnp.zeros((ATT_W, LANES), F32)
        for h in range(2):
            mh = _lane_head_mask(LANES, ATT_HEAD_DIM, h)
            qh = jnp.where(mh, q, 0.0)
            dmh = jnp.where(mh, dm, 0.0)
            s = _dot_nt(qh, kw) + b_ref[tile, h]
            s = jnp.where(valid, s, NEG_INF)
            lse_h = jnp.max(jnp.where(mh, lse, NEG_INF), axis=-1, keepdims=True)
            dl_h = jnp.max(jnp.where(mh, dl, NEG_INF), axis=-1, keepdims=True)
            p = jnp.exp(s - lse_h)
            ds = p * (_dot_nt(dmh, vw) - dl_h)
            dq = dq + jnp.where(mh, _dot(ds, kw), 0.0)
            dkw = dkw + _dot_tn(ds, qh)
            dvw = dvw + _dot_tn(p, dmh)
            db_ref[tile, h] += ds
        dq_ref[...] = dq * (ATT_HEAD_DIM ** -0.5)
        dk_ref[pl.ds(start, ATT_W), :] += dkw
        dv_ref[pl.ds(start, ATT_W), :] += dvw

    def qkv_spec(which, full):
        shape = (n_sub, LANES) if full else (ATT_BQ, LANES)
        return pl.BlockSpec(shape, lambda pr, r, i: (0 if full else i, r * cols + which * 6 + 2 * gi + pr))

    blk = pl.BlockSpec((ATT_BQ, LANES), lambda pr, r, i: (i, r * 2 + pr))
    full = pl.BlockSpec((n_sub, LANES), lambda pr, r, i: (0, r * 2 + pr))
    bias_spec = pl.BlockSpec((3, 2, ATT_BQ, ATT_W), lambda pr, r, i: (0, pr, 0, 0))
    sub = jax.ShapeDtypeStruct((n_sub, dil * 2 * LANES), F32)
    dq, dk, dv, db = pl.pallas_call(
        kern, name=name, grid=(2, dil, nq),
        in_specs=[qkv_spec(0, False), qkv_spec(1, True), qkv_spec(2, True), bias_spec, blk, blk,
                  pl.BlockSpec((ATT_BQ, LANES), lambda pr, r, i: (i, r * dcols + 2 * gi + pr))],
        out_specs=[blk, full, full, bias_spec],
        out_shape=[sub, sub, sub, jax.ShapeDtypeStruct(bias.shape, F32)],
        compiler_params=_params("arbitrary", "arbitrary", "arbitrary"),
    )(view, view, view, bias, lse_v, delta_v, dcat_v)
    return dq.reshape(s_len, -1), dk.reshape(s_len, -1), dv.reshape(s_len, -1), db


def _att_combine(o_g, lse_g, qkvm, kv_mem, *, name):
    def body(o0, o1, o2, l0, l1, l2, qm, kv):
        mx = jnp.maximum(jnp.maximum(l0, l1), l2)
        tot = mx + jnp.log(jnp.exp(l0 - mx) + jnp.exp(l1 - mx) + jnp.exp(l2 - mx))
        mixed = [o * jnp.exp(l - tot) for o, l in ((o0, l0), (o1, l1), (o2, l2))]
        return (jnp.concatenate(mixed + [_mem_attn(qm, kv)], axis=1), tot), ()
    return _rowwise(body, list(o_g) + list(lse_g) + [_col(qkvm, MEM_WIDTH, (3 * TOK_WIDTH) // MEM_WIDTH)], [kv_mem],
                    [(D_MODEL, F32), (MEM_WIDTH, F32)], [], tm=256, name=name)


def _head_sum_matrix():
    a = np.arange(MEM_WIDTH)
    return jnp.asarray((a[:, None] // ATT_HEAD_DIM == a[None, :] // ATT_HEAD_DIM).astype(np.float32))


def _att_bwd_prep(cat, dcat, qkvm, kv_mem, *, name):
    def body(cat, dcat, qm, kv, hs):
        prod = cat * dcat
        summed = prod[:, 0:256] + prod[:, 256:512] + prod[:, 512:768]
        delta = _dot(summed, hs, precision=HI)
        dqm, dkv = _mem_attn_bwd(qm, kv, dcat[:, TOK_WIDTH:])
        return (delta, dqm), (dkv,)
    return _rowwise(body, [cat, dcat, _col(qkvm, MEM_WIDTH, (3 * TOK_WIDTH) // MEM_WIDTH)], [kv_mem, _head_sum_matrix()],
                    [(MEM_WIDTH, F32), (MEM_WIDTH, F32)], [kv_mem.shape], tm=256, name=name)


def _dn_conv_post(s, j):
    scale = jnp.where(j < DN_HEADS, DN_HEAD_DIM ** -0.5, 1.0)
    normed = s * lax.rsqrt(jnp.sum(s * s, axis=-1, keepdims=True) + EPS) * scale
    return jnp.where(j >= 2 * DN_HEADS, s, normed)


def _shift_rows(x, sh):
    n = x.shape[0]
    row = lax.broadcasted_iota(jnp.int32, (n, 1), 0)
    rolled = pltpu.roll(x, (-sh) % n, 0)
    return jnp.where((row + sh >= 0) & (row + sh < n), rolled, 0.0)


def _dn_conv_taps(x, w_ref):
    c = x * w_ref[pl.ds(DN_CONV // 2, 1), :]
    for jj in range(DN_CONV):
        if jj != DN_CONV // 2:
            c = c + _shift_rows(x, jj - DN_CONV // 2) * w_ref[pl.ds(jj, 1), :]
    return c


def _dn_conv_fwd(proj, conv_w, *, name):
    s_len = proj.shape[0]
    width = 3 * TOK_WIDTH

    def kern(x_ref, w_ref, o_ref):
        j = pl.program_id(0)
        o_ref[...] = _dn_conv_post(_silu(_dn_conv_taps(x_ref[...], w_ref)), j)

    return pl.pallas_call(
        kern, name=name, grid=(width // LANES,),
        in_specs=[pl.BlockSpec((s_len, LANES), lambda j: (0, j)), pl.BlockSpec((DN_CONV, LANES), lambda j: (0, j))],
        out_specs=pl.BlockSpec((s_len, LANES), lambda j: (0, j)),
        out_shape=jax.ShapeDtypeStruct((s_len, width), F32),
        compiler_params=_params("parallel"),
    )(proj, conv_w)


def _dn_conv_bwd(proj, conv_w, d_fwd, d_bwd, which, *, name):
    s_len = proj.shape[0]

    def kern(x_ref, w_ref, df_ref, db_ref, dx_ref, dw_ref):
        j = pl.program_id(0) + which * DN_HEADS
        x = x_ref[...]
        c = _dn_conv_taps(x, w_ref)
        _, vjp = jax.vjp(lambda c_: _dn_conv_post(_silu(c_), j), c)
        dc = vjp(df_ref[...] + db_ref[...])[0]
        dx = dc * w_ref[pl.ds(DN_CONV // 2, 1), :]
        for jj in range(DN_CONV):
            sh = jj - DN_CONV // 2
            if sh != 0:
                dx = dx + _shift_rows(dc, -sh) * w_ref[pl.ds(jj, 1), :]
            dw_ref[pl.ds(jj, 1), :] = jnp.sum(dc * _shift_rows(x, sh), axis=0, keepdims=True)
        dx_ref[...] = dx

    return pl.pallas_call(
        kern, name=name, grid=(DN_HEADS,),
        in_specs=[pl.BlockSpec((s_len, LANES), lambda j: (0, j + which * DN_HEADS)),
                  pl.BlockSpec((DN_CONV, LANES), lambda j: (0, j + which * DN_HEADS)),
                  pl.BlockSpec((s_len, LANES), lambda j: (0, j)),
                  pl.BlockSpec((s_len, LANES), lambda j: (0, j))],
        out_specs=[pl.BlockSpec((s_len, LANES), lambda j: (0, j)), pl.BlockSpec((DN_CONV, LANES), lambda j: (0, j))],
        out_shape=[jax.ShapeDtypeStruct((s_len, TOK_WIDTH), F32), jax.ShapeDtypeStruct((DN_CONV, TOK_WIDTH), F32)],
        compiler_params=_params("parallel"),
    )(proj, conv_w, d_fwd, d_bwd)


GATE_TM = 2 * DN_CHUNK


def _gate_constants():
    sel_f = np.zeros((LANES, 2 * TOK_WIDTH), np.float32)
    sel_r = np.zeros((LANES, 2 * TOK_WIDTH), np.float32)
    sel_b = np.zeros((LANES, 2 * TOK_WIDTH), np.float32)
    for d, sel in enumerate((sel_f, sel_r)):
        for h in range(DN_HEADS):
            cols = slice((d * DN_HEADS + h) * DN_HEAD_DIM, (d * DN_HEADS + h + 1) * DN_HEAD_DIM)
            sel[d * 2 * DN_HEADS + h, cols] = 1.0
            sel_b[d * 2 * DN_HEADS + DN_HEADS + h, cols] = 1.0
    i = np.arange(GATE_TM)
    same = (i[:, None] // DN_CHUNK) == (i[None, :] // DN_CHUNK)
    cum_f = same & (i[None, :] <= i[:, None])
    cum_r = same & (i[None, :] >= i[:, None])
    return tuple(jnp.asarray(np.asarray(a, np.float32)) for a in (sel_f, sel_r, sel_b, cum_f, cum_r, same))


def _gate_params(p):
    z = jnp.zeros((DN_HEADS,), F32)
    return jnp.concatenate([p[0], z, p[1], z, jnp.zeros((LANES - N_GATES,), F32)]).reshape(1, LANES)


def _gate_params_bwd(dp):
    return jnp.stack([dp[0, 0:DN_HEADS], dp[0, 2 * DN_HEADS: 3 * DN_HEADS]])


def _dn_gates(gate_in, a_cols, dt_cols, sel_f, sel_r, sel_b, cum_f, cum_r, tot):
    g = -jnp.exp(a_cols) * _softplus(gate_in + dt_cols)
    gc = _dot(_dot(cum_f, g, precision=HI), sel_f, precision=HI) + _dot(_dot(cum_r, g, precision=HI), sel_r, precision=HI)
    g_tot = _dot(_dot(tot, g, precision=HI), sel_f + sel_r, precision=HI)
    beta = jax.nn.sigmoid(_dot(gate_in, sel_b, precision=HI))
    return gc, g_tot, beta


def _dn_gates_fwd(proj, a_cols, dt_cols, *, name):
    def body(gi, *consts):
        return _dn_gates(gi, *consts), ()
    w = 2 * TOK_WIDTH
    return _rowwise(body, [_col(proj, LANES, DN_IN_PAD // LANES - 1)], [a_cols, dt_cols, *_gate_constants()],
                    [(w, F32)] * 3, [], tm=GATE_TM, name=name)


def _dn_gates_bwd(proj, a_cols, dt_cols, d_gc, d_tot, d_beta, *, name):
    def body(gi, gcf, gcr, gtf, gtr, bf, br, a, dt, *consts):
        _, vjp = jax.vjp(lambda gi_, a_, dt_: _dn_gates(gi_, a_, dt_, *consts), gi, a, dt)
        cat = lambda f, r: jnp.concatenate([f, r], axis=1)
        dgi, da, ddt = vjp((cat(gcf, gcr), cat(gtf, gtr), cat(bf, br)))
        return (dgi,), (da, ddt)
    return _rowwise(body, [_col(proj, LANES, DN_IN_PAD // LANES - 1), *d_gc, *d_tot, *d_beta],
                    [a_cols, dt_cols, *_gate_constants()], [(LANES, F32)], [a_cols.shape, dt_cols.shape],
                    tm=GATE_TM, name=name)


INV_BASE = 8


def _block_id_equal(c, size):
    i = lax.broadcasted_iota(jnp.int32, (c, c), 0) // size
    j = lax.broadcasted_iota(jnp.int32, (c, c), 1) // size
    return (i == j).astype(F32)


def _unit_tri_inverse_impl(lmat):
    c = lmat.shape[0]
    eye = _block_id_equal(c, 1)
    same = _block_id_equal(c, INV_BASE)
    neg = -lmat * same
    inv = eye + neg
    power = neg
    for _ in range(int(math.log2(INV_BASE)) - 1):
        power = _dot(power, power)
        inv = inv + _dot(inv, power)
    size = INV_BASE
    while size < c:
        bigger = _block_id_equal(c, 2 * size)
        inv = inv - _dot(_dot(inv, lmat * (bigger - same)), inv)
        same, size = bigger, 2 * size
    resid = eye - _dot(eye + lmat, inv, precision=HI)
    return inv + _dot(inv, resid)


@jax.custom_vjp
def _unit_tri_inverse(lmat):
    return _unit_tri_inverse_impl(lmat)


def _unit_tri_inverse_fwd(lmat):
    inv = _unit_tri_inverse_impl(lmat)
    return inv, inv


def _unit_tri_inverse_bwd(inv, d_inv):
    return (-_dot_tn(inv, _dot_nt(d_inv, inv)),)


_unit_tri_inverse.defvjp(_unit_tri_inverse_fwd, _unit_tri_inverse_bwd)


def _dn_chunk(q, k, v, gc, g_tot, beta, state, tri, inverse):
    c = q.shape[0]
    assert c == DN_HEAD_DIM
    eye = _block_id_equal(c, 1)
    decay = jnp.exp(jnp.where(tri > 0, gc - gc.T, NEG_INF))
    k_beta = k * beta
    inv = inverse((tri - eye) * (_dot_nt(k_beta, k) * decay))
    e_gc = jnp.exp(gc)
    u = _dot(inv, v * beta)
    w = _dot(inv, k_beta * e_gc)
    intra = tri * (_dot_nt(q, k) * decay)
    v_new = u - _dot(w, state)
    out = _dot(q * e_gc, state) + _dot(intra, v_new)
    state = state * jnp.exp(g_tot) + _dot_tn(k * jnp.exp(g_tot - gc), v_new)
    return out, state


def _dn_tri():
    i = np.arange(DN_CHUNK)
    tri = np.stack([(i[None, :] <= i[:, None]), (i[None, :] >= i[:, None])]).astype(np.float32)
    return jnp.asarray(np.repeat(tri, DN_HEADS, axis=0))


def _stack_chains(fwd_ref, rev_ref):
    return jnp.stack([r[:, _head_cols(h)] for r in (fwd_ref, rev_ref) for h in range(DN_HEADS)])


def _unstack_chains(val, fwd_ref, rev_ref):
    for d, r in enumerate((fwd_ref, rev_ref)):
        for h in range(DN_HEADS):
            r[:, _head_cols(h)] = val[d * DN_HEADS + h]


def _dn_row_spec(nc, col, reverse):
    return pl.BlockSpec((DN_CHUNK, TOK_WIDTH), lambda t: ((nc - 1 - t) if reverse else t, col))


def _dn_state_spec(nc, reverse):
    return pl.BlockSpec((None, DN_HEADS, DN_HEAD_DIM, DN_HEAD_DIM), lambda t: ((nc - 1 - t) if reverse else t, 0, 0, 0))


def _head_cols(h):
    return pl.ds(h * DN_HEAD_DIM, DN_HEAD_DIM)


def _dn_scan_fwd(qkv, gc, g_tot, beta, *, name):
    s_len = qkv.shape[0]
    nc = s_len // DN_CHUNK

    def kern(*refs):
        ins, (tri_ref, of_ref, or_ref, sf_ref, sr_ref, state) = refs[:12], refs[12:]

        @pl.when(pl.program_id(0) == 0)
        def _():
            state[...] = jnp.zeros_like(state)

        entry = state[...]
        stacked = [_stack_chains(ins[i], ins[6 + i]) for i in range(6)]
        out, new = jax.vmap(lambda *a: _dn_chunk(*a, _unit_tri_inverse_impl))(*stacked, entry, tri_ref[...])
        sf_ref[...] = entry[:DN_HEADS]
        sr_ref[...] = entry[DN_HEADS:]
        _unstack_chains(out, of_ref, or_ref)
        state[...] = new

    in_specs = [_dn_row_spec(nc, col, rev) for rev in (False, True) for col in (0, 1, 2, int(rev), int(rev), int(rev))]
    in_specs.append(pl.BlockSpec((2 * DN_HEADS, DN_CHUNK, DN_CHUNK), lambda t: (0, 0, 0)))
    return pl.pallas_call(
        kern, name=name, grid=(nc,), in_specs=in_specs,
        out_specs=[_dn_row_spec(nc, 0, False), _dn_row_spec(nc, 0, True), _dn_state_spec(nc, False), _dn_state_spec(nc, True)],
        out_shape=[jax.ShapeDtypeStruct((s_len, TOK_WIDTH), F32)] * 2
        + [jax.ShapeDtypeStruct((nc, DN_HEADS, DN_HEAD_DIM, DN_HEAD_DIM), F32)] * 2,
        scratch_shapes=[pltpu.VMEM((2 * DN_HEADS, DN_HEAD_DIM, DN_HEAD_DIM), F32)],
        compiler_params=_params("arbitrary"),
    )(*([qkv, qkv, qkv, gc, g_tot, beta] * 2), _dn_tri())


def _dn_scan_bwd(qkv, gc, g_tot, beta, states, d_o, *, name):
    s_len = qkv.shape[0]
    nc = s_len // DN_CHUNK

    def kern(*refs):
        ins, tri_ref, outs, d_state = refs[:16], refs[16], refs[17:29], refs[29]

        @pl.when(pl.program_id(0) == 0)
        def _():
            d_state[...] = jnp.zeros_like(d_state)

        stacked = [_stack_chains(ins[i], ins[8 + i]) for i in range(6)]
        entry = jnp.concatenate([ins[6][...], ins[14][...]], axis=0)
        d_out = _stack_chains(ins[7], ins[15])
        tri = tri_ref[...]
        _, vjp = jax.vjp(lambda *a: jax.vmap(lambda *b: _dn_chunk(*b, _unit_tri_inverse))(*a, tri), *stacked, entry)
        grads = vjp((d_out, d_state[...]))
        for i in range(6):
            _unstack_chains(grads[i], outs[i], outs[6 + i])
        d_state[...] = grads[6]

    in_specs = []
    for rev in (True, False):
        in_specs += [_dn_row_spec(nc, col, rev) for col in (0, 1, 2, int(not rev), int(not rev), int(not rev))]
        in_specs += [_dn_state_spec(nc, rev), _dn_row_spec(nc, 0, rev)]
    in_specs.append(pl.BlockSpec((2 * DN_HEADS, DN_CHUNK, DN_CHUNK), lambda t: (0, 0, 0)))
    res = pl.pallas_call(
        kern, name=name, grid=(nc,), in_specs=in_specs,
        out_specs=[_dn_row_spec(nc, 0, rev) for rev in (True, False) for _ in range(6)],
        out_shape=[jax.ShapeDtypeStruct((s_len, TOK_WIDTH), F32)] * 12,
        scratch_shapes=[pltpu.VMEM((2 * DN_HEADS, DN_HEAD_DIM, DN_HEAD_DIM), F32)],
        compiler_params=_params("arbitrary"),
    )(*[a for d in range(2) for a in (qkv, qkv, qkv, gc, g_tot, beta, states[d], d_o)], _dn_tri())
    return res[:6], res[6:]


def _dn_out_head(o_f, o_b, z, gain):
    o = o_f + o_b
    return o * lax.rsqrt(jnp.mean(o * o, axis=-1, keepdims=True) + EPS) * gain * _silu(z)


def _dn_out(o_fwd, o_rev, proj, gain, qkv_kv_mem, *, name):
    def body(of, ob, z, qm, g, kv):
        heads = []
        for h in range(DN_HEADS):
            sl = slice(h * DN_HEAD_DIM, (h + 1) * DN_HEAD_DIM)
            heads.append(_dn_out_head(of[:, sl], ob[:, sl], z[:, sl], g))
        return (jnp.concatenate(heads + [_mem_attn(qm, kv)], axis=1),), ()
    return _rowwise(body, [o_fwd, o_rev, _col(proj, TOK_WIDTH, 3),
                           _col(proj, MEM_WIDTH, (4 * TOK_WIDTH) // MEM_WIDTH)], [gain, qkv_kv_mem],
                    [(D_MODEL, MXU_DTYPE)], [], tm=256, name=name)[0]


def _dn_out_bwd(o_fwd, o_rev, proj, gain, kv_mem, dcat, *, name):
    def body(of, ob, z, qm, dcat, g, kv):
        dos, dzs = [], []
        dgain = jnp.zeros_like(g)
        for h in range(DN_HEADS):
            sl = slice(h * DN_HEAD_DIM, (h + 1) * DN_HEAD_DIM)
            _, vjp = jax.vjp(_dn_out_head, of[:, sl], ob[:, sl], z[:, sl], g)
            d_of, _, dz, dg = vjp(dcat[:, sl])
            dos.append(d_of)
            dzs.append(dz)
            dgain = dgain + dg
        dqm, dkv = _mem_attn_bwd(qm, kv, dcat[:, TOK_WIDTH:])
        return (jnp.concatenate(dos, axis=1), jnp.concatenate(dzs, axis=1), dqm), (dgain, dkv)
    return _rowwise(body, [o_fwd, o_rev, _col(proj, TOK_WIDTH, 3),
                           _col(proj, MEM_WIDTH, (4 * TOK_WIDTH) // MEM_WIDTH), dcat], [gain, kv_mem],
                    [(TOK_WIDTH, F32), (TOK_WIDTH, F32), (MEM_WIDTH, F32)], [gain.shape, kv_mem.shape], tm=256, name=name)


def _pad_dn_w_in(w):
    gates = w[:, 4 * TOK_WIDTH: 4 * TOK_WIDTH + N_GATES]
    zeros = jnp.zeros((w.shape[0], DN_IN_PAD - DN_IN), w.dtype)
    return jnp.concatenate([w[:, :4 * TOK_WIDTH], w[:, 4 * TOK_WIDTH + N_GATES:], gates, zeros], axis=1)


def _unpad_dn_w_in(w):
    q_mem = w[:, 4 * TOK_WIDTH: 4 * TOK_WIDTH + MEM_WIDTH]
    gates = w[:, 4 * TOK_WIDTH + MEM_WIDTH: 4 * TOK_WIDTH + MEM_WIDTH + N_GATES]
    return jnp.concatenate([w[:, :4 * TOK_WIDTH], gates, q_mem], axis=1)


def _ffn_fwd(h, w_gu, w_d, tag):
    gu = _mm(h, w_gu, name=f"ffn_gu_{tag}")
    act = _swiglu_act(gu, name=f"ffn_act_{tag}")
    return gu, act, _mm(act, w_d, name=f"ffn_down_{tag}")


def _ffn_bwd(h, gu, act, w_gu, w_d, df, tag):
    d_act = _mm(df, w_d, tb=True, name=f"ffn_dact_{tag}")
    d_wd = _mm(act, df, ta=True, name=f"ffn_dwd_{tag}")
    d_gu = _swiglu_act_bwd(gu, d_act, name=f"ffn_dgu_{tag}")
    dh = _mm(d_gu, w_gu, tb=True, name=f"ffn_dh_{tag}")
    d_wgu = _mm(h, d_gu, ta=True, name=f"ffn_dwgu_{tag}")
    return dh, d_wgu, d_wd


def _local_step(x, mem, target, p):
    g = {}
    row = lambda v: v.reshape(1, -1)
    gains = {k: [row(p[k][i]) for i in range(2)] for k in
             ("mem_norm", "norm_mix_pre", "norm_mix_post", "norm_ffn_pre", "norm_ffn_post")}
    out_gain = row(p["dn_out_norm"])
    a_cols, dt_cols = _gate_params(p["dn_a_log"]), _gate_params(p["dn_dt_bias"])

    h0 = _pre_norm(x, gains["norm_mix_pre"][0], name="pre0")
    mem_n = [_pre_norm(mem, gains["mem_norm"][i], name=f"mem_norm{i}") for i in range(2)]
    kv_mem = [_mm(mem_n[i], p["mem_w_kv"][i], name=f"mem_kv{i}") for i in range(2)]
    qkvm = _mm(h0, p["att_w_in"], name="att_in")
    bias = [_bias_tiles(p["rel_bias"], gi) for gi in range(3)]
    att = [_att_fwd(qkvm, bias[gi], gi, name=f"att_fwd{gi}") for gi in range(3)]
    cat0, lse_tot = _att_combine([a[0] for a in att], [a[1] for a in att], qkvm, kv_mem[0], name="att_combine")
    mo0 = _mm(cat0, p["att_w_out"], name="att_out")
    x1, h1 = _post_pre(x, mo0, gains["norm_mix_post"][0], gains["norm_ffn_pre"][0], name="post_mix0")
    gu0, act0, f0 = _ffn_fwd(h1, p["ffn_w_gate_up"][0], p["ffn_w_down"][0], 0)
    x2, h2 = _post_pre(x1, f0, gains["norm_ffn_post"][0], gains["norm_mix_pre"][1], name="post_ffn0")

    proj = _mm(h2, p["dn_w_in"], name="dn_in")
    qkv = _dn_conv_fwd(proj, p["dn_conv"], name="dn_conv")
    gc, g_tot, beta = _dn_gates_fwd(proj, a_cols, dt_cols, name="dn_gates")
    o_fwd, o_rev, st_fwd, st_rev = _dn_scan_fwd(qkv, gc, g_tot, beta, name="dn_scan")
    cat1 = _dn_out(o_fwd, o_rev, proj, out_gain, kv_mem[1], name="dn_outnorm")
    mo1 = _mm(cat1, p["dn_w_out"], name="dn_out")
    x3, h3 = _post_pre(x2, mo1, gains["norm_mix_post"][1], gains["norm_ffn_pre"][1], name="post_mix1")
    gu1, act1, f1 = _ffn_fwd(h3, p["ffn_w_gate_up"][1], p["ffn_w_down"][1], 1)

    dx3, df1, dg_ffn_post1, loss_cols = _final_loss_bwd(x3, f1, gains["norm_ffn_post"][1], target, name="loss_bwd")
    dh3, d_wgu1, d_wd1 = _ffn_bwd(h3, gu1, act1, p["ffn_w_gate_up"][1], p["ffn_w_down"][1], df1, 1)
    dx2, dmo1, dg_mix_post1, dg_ffn_pre1 = _post_pre_bwd(x2, mo1, gains["norm_mix_post"][1], gains["norm_ffn_pre"][1],
                                                         dx3, dh3, name="post_mix1_bwd")
    dcat1 = _mm(dmo1, p["dn_w_out"], tb=True, name="dn_out_dx")
    g["dn_w_out"] = _mm(cat1, dmo1, ta=True, name="dn_out_dw")
    d_o, dz, dqm1, d_out_gain, dkv1 = _dn_out_bwd(o_fwd, o_rev, proj, out_gain, kv_mem[1], dcat1, name="dn_outnorm_bwd")
    d_f, d_r = _dn_scan_bwd(qkv, gc, g_tot, beta, (st_fwd, st_rev), d_o, name="dn_scan_bwd")
    d_gate_cols, d_a_cols, d_dt_cols = _dn_gates_bwd(proj, a_cols, dt_cols, (d_f[3], d_r[3]), (d_f[4], d_r[4]),
                                                     (d_f[5], d_r[5]), name="dn_gates_bwd")
    d_pre, d_conv = zip(*[_dn_conv_bwd(proj, p["dn_conv"], d_f[which], d_r[which], which, name=f"dn_conv_bwd{which}")
                          for which in range(3)])
    dproj = jnp.concatenate(list(d_pre) + [dz, dqm1, d_gate_cols], axis=1).astype(MXU_DTYPE)
    dh2 = _mm(dproj, p["dn_w_in"], tb=True, name="dn_in_dx")
    g["dn_w_in"] = _mm(h2, dproj, ta=True, name="dn_in_dw")
    g["dn_conv"] = jnp.concatenate(d_conv, axis=1)
    g["dn_a_log"] = _gate_params_bwd(d_a_cols)
    g["dn_dt_bias"] = _gate_params_bwd(d_dt_cols)
    g["dn_out_norm"] = d_out_gain

    dx1, df0, dg_ffn_post0, dg_mix_pre1 = _post_pre_bwd(x1, f0, gains["norm_ffn_post"][0], gains["norm_mix_pre"][1],
                                                        dx2, dh2, name="post_ffn0_bwd")
    dh1, d_wgu0, d_wd0 = _ffn_bwd(h1, gu0, act0, p["ffn_w_gate_up"][0], p["ffn_w_down"][0], df0, 0)
    dx0, dmo0, dg_mix_post0, dg_ffn_pre0 = _post_pre_bwd(x, mo0, gains["norm_mix_post"][0], gains["norm_ffn_pre"][0],
                                                         dx1, dh1, name="post_mix0_bwd")
    dcat0 = _mm(dmo0, p["att_w_out"], tb=True, name="att_out_dx")
    g["att_w_out"] = _mm(cat0, dmo0, ta=True, name="att_out_dw")
    delta, dqm0, dkv0 = _att_bwd_prep(cat0, dcat0, qkvm, kv_mem[0], name="att_bwd_prep")
    att_b = [_att_bwd(qkvm, bias[gi], lse_tot, delta, dcat0, gi, name=f"att_bwd{gi}") for gi in range(3)]
    dqkvm = jnp.concatenate([a[w] for w in range(3) for a in att_b] + [dqm0], axis=1).astype(MXU_DTYPE)
    g["rel_bias"] = sum(_bias_tiles_bwd(p["rel_bias"], att_b[gi][3], gi) for gi in range(3))
    dh0 = _mm(dqkvm, p["att_w_in"], tb=True, name="att_in_dx")
    g["att_w_in"] = _mm(h0, dqkvm, ta=True, name="att_in_dw")
    grad_x, dg_mix_pre0 = _pre_norm_bwd(x, gains["norm_mix_pre"][0], dh0, dx0, name="pre0_bwd")

    d_mem_kv, d_mem_norm = [], []
    for i, dkv in enumerate((dkv0, dkv1)):
        d_mem_kv.append(_mm(mem_n[i], dkv, ta=True, name=f"mem_kv_dw{i}"))
        d_mem_n = _mm(dkv, p["mem_w_kv"][i], tb=True, name=f"mem_kv_dx{i}")
        d_mem_norm.append(_gain_bwd(mem, gains["mem_norm"][i], d_mem_n, name=f"mem_norm_bwd{i}"))
    g["mem_w_kv"] = jnp.stack(d_mem_kv)
    g["mem_norm"] = jnp.concatenate(d_mem_norm, axis=0)
    g["norm_mix_pre"] = jnp.concatenate([dg_mix_pre0, dg_mix_pre1], axis=0)
    g["norm_mix_post"] = jnp.concatenate([dg_mix_post0, dg_mix_post1], axis=0)
    g["norm_ffn_pre"] = jnp.concatenate([dg_ffn_pre0, dg_ffn_pre1], axis=0)
    g["norm_ffn_post"] = jnp.concatenate([dg_ffn_post0, dg_ffn_post1], axis=0)
    g["ffn_w_gate_up"] = jnp.stack([d_wgu0, d_wgu1])
    g["ffn_w_down"] = jnp.stack([d_wd0, d_wd1])
    return loss_cols, grad_x, g


N_CHIPS = 4
N_DEV = 8
MESH = pl.DeviceIdType.MESH
BIG = (("att_w_in", (1, 1024, 640), 2), ("att_w_out", (1, 256, 1024), 1), ("dn_w_in", (1, 1024, 838), 2),
       ("dn_w_out", (1, 256, 1024), 1), ("mem_w_kv", (2, 256, 512), 1), ("ffn_w_gate_up", (2, 1024, 1408), 2),
       ("ffn_w_down", (2, 704, 1024), 1))
PACK_COLS = 1024
PACK_ELEMS = sum(math.prod(shape) for _, shape, _ in BIG)
HALF_ROWS = -(-PACK_ELEMS // (2 * PACK_COLS * 16)) * 16
PACK_ROWS = 2 * HALF_ROWS
HALF_TM = _tile(HALF_ROWS, 512, 16)


def _pack_local(blocks):
    flat = [blocks[n].reshape(-1) for n, _, _ in BIG]
    dtype = flat[0].dtype
    flat.append(jnp.zeros((PACK_ROWS * PACK_COLS - PACK_ELEMS,), dtype))
    return jnp.concatenate(flat).reshape(PACK_ROWS, PACK_COLS)


def _unpack_local(slab):
    flat = slab.reshape(-1)
    out, off = {}, 0
    for n, shape, _ in BIG:
        size = math.prod(shape)
        out[n] = flat[off: off + size].reshape(shape)
        off += size
    return out


def _full_from_gathered(gathered):
    per_chip = [_unpack_local(gathered[s]) for s in range(N_CHIPS)]
    return {n: jnp.concatenate([pc[n] for pc in per_chip], axis=axis) for n, _, axis in BIG}


def _pack_full(full):
    slabs = []
    for s in range(N_CHIPS):
        blocks = {}
        for n, shape, axis in BIG:
            blocks[n] = lax.slice_in_dim(full[n], s * shape[axis], (s + 1) * shape[axis], axis=axis)
        slabs.append(_pack_local(blocks))
    return jnp.stack(slabs)


def _mesh_pos():
    return lax.axis_index("x"), lax.axis_index("y"), lax.axis_index("c")


def _other_chips(x, y):
    return [(1 - x, y), (x, 1 - y), (1 - x, 1 - y)]


ANY = pl.BlockSpec(memory_space=pl.ANY)


def _gather_shards(slab, *, name):
    rows, cols = slab.shape
    half = rows // 2

    def body(x_ref, out_ref, send_sems, recv_sems):
        x, y, c = _mesh_pos()
        me = 2 * x + y
        sibling = (x, y, 1 - c)
        chips = _other_chips(x, y)

        def part(chip, h):
            return out_ref.at[2 * chip[0] + chip[1], pl.ds(h * half, half), :]

        def copy(k, src, dst, to):
            return pltpu.make_async_remote_copy(src_ref=src, dst_ref=dst, send_sem=send_sems.at[k],
                                                recv_sem=recv_sems.at[k], device_id=to, device_id_type=MESH)

        my_half = x_ref.at[pl.ds(c * half, half), :]
        first = [copy(j, my_half, part((x, y), c), (*chip, c)) for j, chip in enumerate(chips)]
        for cp in first:
            cp.start()
        passed = [copy(3 + j, part(chip, c), part(chip, c), sibling) for j, chip in enumerate(chips)]
        for j, chip in enumerate(chips):
            copy(j, my_half, part(chip, c), (*chip, c)).wait_recv()
            passed[j].start()
        for j, chip in enumerate(chips):
            copy(3 + j, part(chip, 1 - c), part(chip, 1 - c), sibling).wait_recv()
        for cp in first + passed:
            cp.wait_send()

    return pl.pallas_call(
        body, name=name, in_specs=[ANY], out_specs=ANY,
        out_shape=jax.ShapeDtypeStruct((N_CHIPS, rows, cols), slab.dtype),
        scratch_shapes=[pltpu.SemaphoreType.DMA((6,)), pltpu.SemaphoreType.DMA((6,))],
    )(slab)


def _swap_other_half(slabs, *, name):
    n, rows, cols = slabs.shape
    half = rows // 2

    def body(g_ref, out_ref, send_sem, recv_sem):
        x, y, c = _mesh_pos()
        cp = pltpu.make_async_remote_copy(src_ref=g_ref.at[:, pl.ds((1 - c) * half, half), :], dst_ref=out_ref,
                                          send_sem=send_sem, recv_sem=recv_sem, device_id=(x, y, 1 - c), device_id_type=MESH)
        cp.start()
        cp.wait()

    return pl.pallas_call(
        body, name=name, in_specs=[ANY], out_specs=ANY, out_shape=jax.ShapeDtypeStruct((n, half, cols), slabs.dtype),
        scratch_shapes=[pltpu.SemaphoreType.DMA, pltpu.SemaphoreType.DMA],
    )(slabs)


def _add_own_half(slabs, received, core, *, name):
    n, rows, cols = slabs.shape
    half = rows // 2
    nb = half // HALF_TM

    def kern(c_ref, a_ref, b_ref, o_ref):
        o_ref[...] = (a_ref[...] + b_ref[...]).astype(o_ref.dtype)

    return pl.pallas_call(
        kern, name=name,
        grid_spec=pltpu.PrefetchScalarGridSpec(
            num_scalar_prefetch=1, grid=(n, nb),
            in_specs=[pl.BlockSpec((None, HALF_TM, cols), lambda s, i, c: (s, c[0] * nb + i, 0)),
                      pl.BlockSpec((None, HALF_TM, cols), lambda s, i, c: (s, i, 0))],
            out_specs=pl.BlockSpec((None, HALF_TM, cols), lambda s, i, c: (s, i, 0))),
        out_shape=jax.ShapeDtypeStruct((n, half, cols), LINK_DTYPE),
        compiler_params=_params("parallel", "parallel"),
    )(core, slabs, received)


def _scatter_to_chips(sums, *, name):
    n, half, cols = sums.shape

    def body(s_ref, out_ref, send_sems, recv_sems):
        x, y, c = _mesh_pos()
        me = 2 * x + y
        chips = _other_chips(x, y)

        def copy(k, chip):
            return pltpu.make_async_remote_copy(src_ref=s_ref.at[2 * chip[0] + chip[1]], dst_ref=out_ref.at[me],
                                                send_sem=send_sems.at[k], recv_sem=recv_sems.at[k],
                                                device_id=(*chip, c), device_id_type=MESH)

        sends = [copy(j, chip) for j, chip in enumerate(chips)]
        for cp in sends:
            cp.start()
        for j, chip in enumerate(chips):
            pltpu.make_async_remote_copy(src_ref=s_ref.at[me], dst_ref=out_ref.at[2 * chip[0] + chip[1]],
                                         send_sem=send_sems.at[j], recv_sem=recv_sems.at[j],
                                         device_id=(*chip, c), device_id_type=MESH).wait_recv()
        for cp in sends:
            cp.wait_send()

    return pl.pallas_call(
        body, name=name, in_specs=[ANY], out_specs=ANY, out_shape=jax.ShapeDtypeStruct((n, half, cols), sums.dtype),
        scratch_shapes=[pltpu.SemaphoreType.DMA((3,)), pltpu.SemaphoreType.DMA((3,))],
    )(sums)


def _sum_chips(parts, *, name):
    n, half, cols = parts.shape

    def kern(p_ref, o_ref):
        acc = p_ref[0].astype(F32)
        for s in range(1, n):
            acc = acc + p_ref[s].astype(F32)
        o_ref[...] = acc

    return pl.pallas_call(
        kern, name=name, grid=(half // HALF_TM,),
        in_specs=[pl.BlockSpec((n, HALF_TM, cols), lambda i: (0, i, 0))],
        out_specs=pl.BlockSpec((HALF_TM, cols), lambda i: (i, 0)),
        out_shape=jax.ShapeDtypeStruct((half, cols), F32),
        compiler_params=_params("parallel"),
    )(parts)


def _join_halves(mine, *, name):
    half, cols = mine.shape

    def body(m_ref, out_ref, send_sem, recv_sem):
        x, y, c = _mesh_pos()
        cp = pltpu.make_async_remote_copy(src_ref=m_ref, dst_ref=out_ref, send_sem=send_sem, recv_sem=recv_sem,
                                          device_id=(x, y, 1 - c), device_id_type=MESH)
        cp.start()
        cp.wait()

    return pl.pallas_call(
        body, name=name, in_specs=[ANY], out_specs=ANY, out_shape=jax.ShapeDtypeStruct((half, cols), mine.dtype),
        scratch_shapes=[pltpu.SemaphoreType.DMA, pltpu.SemaphoreType.DMA],
    )(mine)


def _all_reduce_small(v, *, name):
    rows, cols = v.shape
    flips = [(dx, dy, dc) for dx in (0, 1) for dy in (0, 1) for dc in (0, 1)][1:]

    def body(v_ref, o_ref, buf, send_sems, recv_sems):
        x, y, c = _mesh_pos()

        def peer(f):
            return tuple(1 - p if fl else p for p, fl in zip((x, y, c), f))

        def index(p):
            return 4 * p[0] + 2 * p[1] + p[2]

        buf[index((x, y, c))] = v_ref[...]
        sends = []
        for k, f in enumerate(flips):
            cp = pltpu.make_async_remote_copy(src_ref=v_ref, dst_ref=buf.at[index((x, y, c))], send_sem=send_sems.at[k],
                                              recv_sem=recv_sems.at[k], device_id=peer(f), device_id_type=MESH)
            cp.start()
            sends.append(cp)
        for k, f in enumerate(flips):
            pltpu.make_async_remote_copy(src_ref=v_ref, dst_ref=buf.at[index(peer(f))], send_sem=send_sems.at[k],
                                         recv_sem=recv_sems.at[k], device_id=peer(f), device_id_type=MESH).wait_recv()
        for cp in sends:
            cp.wait_send()
        acc = buf[0]
        for d in range(1, N_DEV):
            acc = acc + buf[d]
        o_ref[...] = acc

    vmem = pl.BlockSpec(memory_space=pltpu.VMEM)
    return pl.pallas_call(
        body, name=name, in_specs=[vmem], out_specs=vmem, out_shape=jax.ShapeDtypeStruct((rows, cols), F32),
        scratch_shapes=[pltpu.VMEM((N_DEV, rows, cols), F32), pltpu.SemaphoreType.DMA((N_DEV - 1,)),
                        pltpu.SemaphoreType.DMA((N_DEV - 1,))],
    )(v)


def _reduce_scatter(slabs, core):
    x, y, c = _mesh_pos()
    chip = 2 * x + y
    received = _swap_other_half(slabs, name="rs_swap")
    sums = _add_own_half(slabs, received, core, name="rs_add")
    parts = _scatter_to_chips(sums, name="rs_scatter")
    parts = lax.dynamic_update_slice(parts, lax.dynamic_slice_in_dim(sums, chip, 1, axis=0), (chip, 0, 0))
    mine = _sum_chips(parts, name="rs_sum")
    other = _join_halves(mine, name="rs_join")
    return jnp.concatenate([jnp.where(c == 0, mine, other), jnp.where(c == 0, other, mine)], axis=0)


def _adamw(w, g, m, v, *, name):
    def body(w, g, m, v):
        m = ADAM_B1 * m + (1.0 - ADAM_B1) * g
        v = ADAM_B2 * v + (1.0 - ADAM_B2) * (g * g)
        m_hat = m / (1.0 - ADAM_B1 ** ADAM_STEP)
        v_hat = v / (1.0 - ADAM_B2 ** ADAM_STEP)
        delta = -ADAM_LR * (m_hat / (jnp.sqrt(v_hat) + ADAM_EPS) + ADAM_WD * w)
        return (delta, m, v), ()
    rows, cols = w.shape
    return _rowwise(body, [w, g, m, v], [], [(cols, F32)] * 3, [], tm=_tile(rows, 256, SUBLANES), name=name)


def _pack_small(arrs, rows):
    flat = jnp.concatenate([a.reshape(-1) for a in arrs])
    return jnp.pad(flat, (0, rows * LANES - flat.shape[0])).reshape(rows, LANES)


def _unpack_small(packed, shapes):
    flat = packed.reshape(-1)
    out, off = [], 0
    for s in shapes:
        size = math.prod(s)
        out.append(flat[off: off + size].reshape(s))
        off += size
    return out


def _small_rows(shapes):
    return -(-sum(math.prod(s) for s in shapes) // (SUBLANES * LANES)) * SUBLANES


WEIGHTS = ("rel_bias", "att_w_in", "att_w_out", "dn_w_in", "dn_conv", "dn_a_log", "dn_dt_bias", "dn_out_norm", "dn_w_out",
           "mem_norm", "mem_w_kv", "norm_mix_pre", "norm_mix_post", "norm_ffn_pre", "norm_ffn_post", "ffn_w_gate_up",
           "ffn_w_down")
BIG_NAMES = tuple(n for n, _, _ in BIG)
SMALL_NAMES = tuple(n for n in WEIGHTS if n not in BIG_NAMES)
CONV_COLS = 3 * TOK_WIDTH
CONV_SHARD = CONV_COLS // N_CHIPS


def kernel(x, mem, rel_bias, att_w_in, att_w_out, dn_w_in, dn_conv, dn_a_log, dn_dt_bias, dn_out_norm, dn_w_out, mem_norm, mem_w_kv, norm_mix_pre, norm_mix_post, norm_ffn_pre, norm_ffn_post, ffn_w_gate_up, ffn_w_down, loss_target, m_rel_bias, m_att_w_in, m_att_w_out, m_dn_w_in, m_dn_conv, m_dn_a_log, m_dn_dt_bias, m_dn_out_norm, m_dn_w_out, m_mem_norm, m_mem_w_kv, m_norm_mix_pre, m_norm_mix_post, m_norm_ffn_pre, m_norm_ffn_post, m_ffn_w_gate_up, m_ffn_w_down, v_rel_bias, v_att_w_in, v_att_w_out, v_dn_w_in, v_dn_conv, v_dn_a_log, v_dn_dt_bias, v_dn_out_norm, v_dn_w_out, v_mem_norm, v_mem_w_kv, v_norm_mix_pre, v_norm_mix_post, v_norm_ffn_pre, v_norm_ffn_post, v_ffn_w_gate_up, v_ffn_w_down):
    w = dict(zip(WEIGHTS, (rel_bias, att_w_in, att_w_out, dn_w_in, dn_conv, dn_a_log, dn_dt_bias, dn_out_norm, dn_w_out,
                           mem_norm, mem_w_kv, norm_mix_pre, norm_mix_post, norm_ffn_pre, norm_ffn_post, ffn_w_gate_up,
                           ffn_w_down)))
    m = dict(zip(WEIGHTS, (m_rel_bias, m_att_w_in, m_att_w_out, m_dn_w_in, m_dn_conv, m_dn_a_log, m_dn_dt_bias,
                           m_dn_out_norm, m_dn_w_out, m_mem_norm, m_mem_w_kv, m_norm_mix_pre, m_norm_mix_post,
                           m_norm_ffn_pre, m_norm_ffn_post, m_ffn_w_gate_up, m_ffn_w_down)))
    v = dict(zip(WEIGHTS, (v_rel_bias, v_att_w_in, v_att_w_out, v_dn_w_in, v_dn_conv, v_dn_a_log, v_dn_dt_bias,
                           v_dn_out_norm, v_dn_w_out, v_mem_norm, v_mem_w_kv, v_norm_mix_pre, v_norm_mix_post,
                           v_norm_ffn_pre, v_norm_ffn_post, v_ffn_w_gate_up, v_ffn_w_down)))
    cx, cy, cc = _mesh_pos()
    chip = 2 * cx + cy

    slab = lax.optimization_barrier(_pack_local({n: w[n] for n in BIG_NAMES}).astype(MXU_DTYPE))
    gathered = lax.dynamic_update_slice(_gather_shards(slab, name="gather_weights"), slab[None], (chip, 0, 0))
    full = _full_from_gathered(gathered)
    conv_rows = _small_rows([(DN_CONV, CONV_COLS)])
    conv_mine = jnp.where(cc == 0, 1.0, 0.0) * w["dn_conv"][0]
    conv_placed = lax.dynamic_update_slice(jnp.zeros((DN_CONV, CONV_COLS), F32), conv_mine, (0, chip * CONV_SHARD))
    conv_full = _unpack_small(_all_reduce_small(_pack_small([conv_placed], conv_rows), name="gather_conv"),
                              [(DN_CONV, CONV_COLS)])[0]
    p = {
        "rel_bias": w["rel_bias"], "att_w_in": full["att_w_in"][0], "att_w_out": full["att_w_out"][0],
        "dn_w_in": _pad_dn_w_in(full["dn_w_in"][0]), "dn_conv": conv_full, "dn_a_log": w["dn_a_log"][0],
        "dn_dt_bias": w["dn_dt_bias"][0], "dn_out_norm": w["dn_out_norm"][0], "dn_w_out": full["dn_w_out"][0],
        "mem_norm": w["mem_norm"], "mem_w_kv": full["mem_w_kv"], "norm_mix_pre": w["norm_mix_pre"],
        "norm_mix_post": w["norm_mix_post"], "norm_ffn_pre": w["norm_ffn_pre"], "norm_ffn_post": w["norm_ffn_post"],
        "ffn_w_gate_up": full["ffn_w_gate_up"], "ffn_w_down": full["ffn_w_down"],
    }

    loss_cols, grad_x, g = _local_step(x[0], mem[0], loss_target[0], p)
    loss = lax.psum(jnp.sum(loss_cols), ("x", "y", "c"))

    g_big = {"att_w_in": g["att_w_in"][None], "att_w_out": g["att_w_out"][None], "dn_w_in": _unpad_dn_w_in(g["dn_w_in"])[None],
             "dn_w_out": g["dn_w_out"][None], "mem_w_kv": g["mem_w_kv"], "ffn_w_gate_up": g["ffn_w_gate_up"],
             "ffn_w_down": g["ffn_w_down"]}
    reduced = _reduce_scatter(_pack_full(g_big), cc.astype(jnp.int32).reshape(1))
    grads = _unpack_local(reduced)
    small_full_shapes = [(DN_CONV, CONV_COLS) if n == "dn_conv" else w[n].shape for n in SMALL_NAMES]
    small_sum = _all_reduce_small(_pack_small([g[n] for n in SMALL_NAMES], _small_rows(small_full_shapes)), name="reduce_small")
    for n, s in zip(SMALL_NAMES, _unpack_small(small_sum, small_full_shapes)):
        grads[n] = lax.dynamic_slice(s, (0, chip * CONV_SHARD), (DN_CONV, CONV_SHARD))[None] if n == "dn_conv" else s

    delta, new_m, new_v = {}, {}, {}
    for n in BIG_NAMES:
        shape = w[n].shape
        two_d = lambda a: a.reshape(-1, shape[-1])
        res = _adamw(two_d(w[n]), two_d(grads[n]), two_d(m[n]), two_d(v[n]), name=f"adamw_{n}")
        delta[n], new_m[n], new_v[n] = (r.reshape(shape) for r in res)
    small_shapes = [w[n].shape for n in SMALL_NAMES]
    rows = _small_rows(small_shapes)
    res = _adamw(*[_pack_small([d[n] for n in SMALL_NAMES], rows) for d in (w, grads, m, v)], name="adamw_small")
    for d, r in zip((delta, new_m, new_v), res):
        for n, a in zip(SMALL_NAMES, _unpack_small(r, small_shapes)):
            d[n] = a
    return (loss, grad_x[None], *[grads[n] for n in WEIGHTS], *[delta[n] for n in WEIGHTS],
            *[new_m[n] for n in WEIGHTS], *[new_v[n] for n in WEIGHTS])
```

```python
import functools
import math

import numpy as np
import jax
import jax.numpy as jnp
from jax import lax
from jax.experimental import pallas as pl
from jax.experimental.pallas import tpu as pltpu

F32 = jnp.float32
MXU_DTYPE = jnp.bfloat16
LINK_DTYPE = jnp.bfloat16
HI = lax.Precision.HIGHEST

EPS = 1e-6
NEG_INF = -1e30
LANES = 128
SUBLANES = 8
VMEM_LIMIT = 56 * 1024 * 1024

D_MODEL = 1024
TOK_WIDTH = 768
MEM_WIDTH = 256
MEM_LEN = 256
ATT_HEAD_DIM = 64
DILATIONS = (1, 4, 16)
HALF = 64
ATT_BQ = 128
ATT_W = ATT_BQ + 2 * HALF
REL_BUCKETS = 32
REL_MAX_DIST = 1024
DN_HEADS = 6
DN_HEAD_DIM = 128
DN_CONV = 5
DN_CHUNK = 128
D_FF = 2816
ATT_IN = 2560
DN_IN = 3352
DN_IN_PAD = 3456
N_GATES = 4 * DN_HEADS

ADAM_LR = 0.001
ADAM_B1 = 0.9
ADAM_B2 = 0.999
ADAM_EPS = 1e-08
ADAM_WD = 0.01
ADAM_STEP = 10


def _tile(n, target, align):
    if n <= target:
        return n
    t = (target // align) * align
    while t >= align:
        if n % t == 0:
            return t
        t -= align
    raise ValueError(f"no tile for {n} (target {target}, align {align})")


def _params(*sem):
    return pltpu.CompilerParams(dimension_semantics=sem, vmem_limit_bytes=VMEM_LIMIT)


def _mm(a, b, *, name, ta=False, tb=False, tm=1024, tn=1408, tk=1408, out_dtype=F32):
    if ta:
        K, M = a.shape
    else:
        M, K = a.shape
    if tb:
        N, K2 = b.shape
    else:
        K2, N = b.shape
    assert K == K2, (a.shape, b.shape, ta, tb)
    tm = _tile(M, tm, LANES if ta else SUBLANES)
    tn = _tile(N, tn, LANES)
    tk = _tile(K, tk, LANES)
    nk = K // tk
    a_spec = pl.BlockSpec((tk, tm), lambda i, j, k: (k, i)) if ta else pl.BlockSpec((tm, tk), lambda i, j, k: (i, k))
    b_spec = pl.BlockSpec((tn, tk), lambda i, j, k: (j, k)) if tb else pl.BlockSpec((tk, tn), lambda i, j, k: (k, j))
    dims = (((0 if ta else 1,), (1 if tb else 0,)), ((), ()))

    def kern(a_ref, b_ref, o_ref, acc_ref):
        k = pl.program_id(2)

        @pl.when(k == 0)
        def _():
            acc_ref[...] = jnp.zeros_like(acc_ref)

        acc_ref[...] += lax.dot_general(a_ref[...].astype(MXU_DTYPE), b_ref[...].astype(MXU_DTYPE), dims,
                                        preferred_element_type=F32)

        @pl.when(k == nk - 1)
        def _():
            o_ref[...] = acc_ref[...].astype(o_ref.dtype)

    return pl.pallas_call(
        kern, name=name, grid=(M // tm, N // tn, nk), in_specs=[a_spec, b_spec],
        out_specs=pl.BlockSpec((tm, tn), lambda i, j, k: (i, j)),
        out_shape=jax.ShapeDtypeStruct((M, N), out_dtype),
        scratch_shapes=[pltpu.VMEM((tm, tn), F32)],
        compiler_params=_params("parallel", "parallel", "arbitrary"),
    )(a, b)


def _col(arr, width, blk):
    return (arr, width, blk)


def _rowwise(body, rows, consts, out_rows, out_acc, *, tm, name):
    n_rows = (rows[0][0] if isinstance(rows[0], tuple) else rows[0]).shape[0]
    assert n_rows % tm == 0, (n_rows, tm)
    arrs, in_specs = [], []
    for r in rows:
        arr, width, blk = r if isinstance(r, tuple) else (r, r.shape[1], 0)
        assert arr.shape[0] == n_rows
        arrs.append(arr)
        in_specs.append(pl.BlockSpec((tm, width), functools.partial(lambda i, b: (i, b), b=blk)))
    for c in consts:
        arrs.append(c)
        in_specs.append(pl.BlockSpec(c.shape, functools.partial(lambda i, n: (0,) * n, n=c.ndim)))
    n_in, n_ro = len(arrs), len(out_rows)
    out_shape = [jax.ShapeDtypeStruct((n_rows, w), dt) for w, dt in out_rows]
    out_specs = [pl.BlockSpec((tm, w), lambda i: (i, 0)) for w, _ in out_rows]
    out_shape += [jax.ShapeDtypeStruct(s, F32) for s in out_acc]
    out_specs += [pl.BlockSpec(s, lambda i: (0, 0)) for s in out_acc]

    def kern(*refs):
        ro, ao = body(*[r[...] for r in refs[:n_in]])
        outs = refs[n_in:]
        for r, v in zip(outs[:n_ro], ro, strict=True):
            r[...] = v.astype(r.dtype)
        if out_acc:
            @pl.when(pl.program_id(0) == 0)
            def _():
                for r in outs[n_ro:]:
                    r[...] = jnp.zeros_like(r)

            for r, v in zip(outs[n_ro:], ao, strict=True):
                r[...] += v

    res = pl.pallas_call(
        kern, name=name, grid=(n_rows // tm,), in_specs=in_specs, out_specs=out_specs, out_shape=out_shape,
        compiler_params=_params("arbitrary" if out_acc else "parallel"),
    )(*arrs)
    return res


def _rms(x, gain):
    return x * lax.rsqrt(jnp.mean(x * x, axis=-1, keepdims=True) + EPS) * gain


def _silu(x):
    return x * jax.nn.sigmoid(x)


def _softplus(x):
    return jnp.maximum(x, 0.0) + jnp.log(1.0 + jnp.exp(-jnp.abs(x)))


def _dot_nt(a, b, precision=None):
    return lax.dot_general(a, b, (((1,), (1,)), ((), ())), preferred_element_type=F32, precision=precision)


def _dot_tn(a, b, precision=None):
    return lax.dot_general(a, b, (((0,), (0,)), ((), ())), preferred_element_type=F32, precision=precision)


def _dot(a, b, precision=None):
    return jnp.dot(a, b, preferred_element_type=F32, precision=precision)


def _pre_norm(x, gain, *, name):
    def body(x, g):
        return (_rms(x, g),), ()
    return _rowwise(body, [x], [gain], [(x.shape[1], MXU_DTYPE)], [], tm=_tile(x.shape[0], 512, 2 * SUBLANES), name=name)[0]


def _pre_norm_bwd(x, gain, dh, dx_other, *, name):
    def body(x, dh, dxo, g):
        _, vjp = jax.vjp(_rms, x, g)
        dx, dg = vjp(dh)
        return (dx + dxo,), (dg,)
    return _rowwise(body, [x, dh, dx_other], [gain], [(x.shape[1], F32)], [gain.shape], tm=512, name=name)


def _gain_bwd(x, gain, dh, *, name):
    def body(x, dh, g):
        _, vjp = jax.vjp(lambda g_: _rms(x, g_), g)
        return (), (vjp(dh)[0],)
    return _rowwise(body, [x, dh], [gain], [], [gain.shape], tm=_tile(x.shape[0], 512, SUBLANES), name=name)[0]


def _res_block(x_res, m, g_post, g_pre):
    x_new = x_res + _rms(m, g_post)
    return x_new, _rms(x_new, g_pre)


def _post_pre(x_res, m, g_post, g_pre, *, name):
    def body(x, m, gp, gq):
        return _res_block(x, m, gp, gq), ()
    d = x_res.shape[1]
    return _rowwise(body, [x_res, m], [g_post, g_pre], [(d, F32), (d, MXU_DTYPE)], [], tm=512, name=name)


def _post_pre_bwd(x_res, m, g_post, g_pre, dx_new, dh, *, name):
    def body(x, m, dxn, dh, gp, gq):
        _, vjp = jax.vjp(_res_block, x, m, gp, gq)
        dx, dm, dgp, dgq = vjp((dxn, dh))
        return (dx, dm), (dgp, dgq)
    d = x_res.shape[1]
    return _rowwise(body, [x_res, m, dx_new, dh], [g_post, g_pre], [(d, F32), (d, MXU_DTYPE)],
                    [g_post.shape, g_pre.shape], tm=256, name=name)


def _final_loss_bwd(x_res, m, g_post, target, *, name):
    d = x_res.shape[1]

    def loss_cols(x, m, g, t):
        err = x + _rms(m, g) - t
        return jnp.sum(err * err, axis=0, keepdims=True) * (0.5 / d)

    def body(x, m, t, g):
        cols, vjp = jax.vjp(lambda x_, m_, g_: loss_cols(x_, m_, g_, t), x, m, g)
        dx, dm, dg = vjp(jnp.ones_like(cols))
        return (dx, dm), (dg, cols)
    return _rowwise(body, [x_res, m, target], [g_post], [(d, F32), (d, MXU_DTYPE)], [g_post.shape, (1, d)], tm=256, name=name)


def _swiglu_act(gu, *, name):
    def body(gate, up):
        return (_silu(gate) * up,), ()
    return _rowwise(body, [_col(gu, D_FF, 0), _col(gu, D_FF, 1)], [], [(D_FF, MXU_DTYPE)], [], tm=256, name=name)[0]


def _swiglu_act_bwd(gu, da, *, name):
    def body(gate, up, da):
        _, vjp = jax.vjp(lambda g, u: _silu(g) * u, gate, up)
        dg, du = vjp(da)
        return (jnp.concatenate([dg, du], axis=1),), ()
    return _rowwise(body, [_col(gu, D_FF, 0), _col(gu, D_FF, 1), da], [], [(2 * D_FF, MXU_DTYPE)], [], tm=256, name=name)[0]


def _lane_head_mask(width, head_dim, head):
    lane = lax.broadcasted_iota(jnp.int32, (1, width), 1)
    return (lane // head_dim) == head


def _mem_attn_pair(q_pair, k_pair, v_pair):
    out = jnp.zeros_like(q_pair)
    for h in range(2):
        mh = _lane_head_mask(LANES, ATT_HEAD_DIM, h)
        qh = jnp.where(mh, q_pair * (ATT_HEAD_DIM ** -0.5), 0.0)
        logits = _dot_nt(qh, k_pair)
        mx = jnp.max(logits, axis=-1, keepdims=True)
        p = jnp.exp(logits - mx)
        p = p / jnp.sum(p, axis=-1, keepdims=True)
        out = out + jnp.where(mh, _dot(p, v_pair), 0.0)
    return out


def _mem_attn(q_mem, kv):
    outs = []
    for p in range(MEM_WIDTH // LANES):
        sl = slice(p * LANES, (p + 1) * LANES)
        outs.append(_mem_attn_pair(q_mem[:, sl], kv[:, sl], kv[:, MEM_WIDTH + p * LANES: MEM_WIDTH + (p + 1) * LANES]))
    return jnp.concatenate(outs, axis=1)


def _mem_attn_bwd(q_mem, kv, do):
    dqs, dks, dvs = [], [], []
    for p in range(MEM_WIDTH // LANES):
        sl = slice(p * LANES, (p + 1) * LANES)
        sv = slice(MEM_WIDTH + p * LANES, MEM_WIDTH + (p + 1) * LANES)
        _, vjp = jax.vjp(_mem_attn_pair, q_mem[:, sl], kv[:, sl], kv[:, sv])
        dq, dk, dv = vjp(do[:, sl])
        dqs.append(dq)
        dks.append(dk)
        dvs.append(dv)
    return jnp.concatenate(dqs, axis=1), jnp.concatenate(dks + dvs, axis=1)


def _t5_bucket(rel):
    half = REL_BUCKETS // 2
    max_exact = half // 2
    n = np.abs(rel)
    large = max_exact + (np.log(np.maximum(n, 1) / max_exact) / math.log(REL_MAX_DIST / max_exact)
                         * (half - max_exact)).astype(np.int64)
    large = np.minimum(large, half - 1)
    return ((rel > 0) * half + np.where(n < max_exact, n, large)).astype(np.int32)


ATT_DIAGS = ATT_BQ + ATT_W - 1


def _bias_diag_onehot(dil):
    j = np.arange(ATT_DIAGS)
    tiles = []
    for off in (-HALF, 0, HALF):
        rel = j - (ATT_BQ - 1) - HALF - off
        hot = _t5_bucket(rel * dil)[:, None] == np.arange(REL_BUCKETS)[None, :]
        tiles.append(hot & (np.abs(rel) <= HALF)[:, None])
    return np.stack(tiles).astype(np.float32)


def _toeplitz(r):
    lead = r.shape[:-1]
    a = jnp.broadcast_to(r[..., None, :], lead + (ATT_BQ, ATT_DIAGS))
    a = jnp.pad(a, [(0, 0)] * len(lead) + [(0, 0), (0, 1)])
    a = a.reshape(lead + (ATT_BQ * (ATT_DIAGS + 1),))[..., : ATT_BQ * ATT_DIAGS].reshape(lead + (ATT_BQ, ATT_DIAGS))
    return a[..., ATT_BQ - 1: ATT_BQ - 1 + ATT_W]


def _bias_tiles(rel_bias, gi):
    heads = rel_bias[:, 4 * gi: 4 * gi + 4]
    diag = jnp.einsum('tnb,bh->thn', jnp.asarray(_bias_diag_onehot(DILATIONS[gi])), heads, precision=HI)
    return _toeplitz(diag)


def _bias_tiles_bwd(rel_bias, dtiles, gi):
    return jax.vjp(lambda rb: _bias_tiles(rb, gi), rel_bias)[1](dtiles)[0]


def _att_window(i, n_sub):
    start = jnp.clip(i * ATT_BQ - HALF, 0, n_sub - ATT_W)
    off = i * ATT_BQ - HALF - start
    return pl.multiple_of(start, HALF), off


def _att_valid(off):
    q = lax.broadcasted_iota(jnp.int32, (ATT_BQ, ATT_W), 0)
    kk = lax.broadcasted_iota(jnp.int32, (ATT_BQ, ATT_W), 1)
    return jnp.abs(kk - q - HALF - off) <= HALF


def _att_tile_id(i, nq):
    return jnp.where(i == 0, 0, jnp.where(i == nq - 1, 2, 1))


def _att_fwd(qkvm, bias, gi, *, name):
    dil = DILATIONS[gi]
    s_len = qkvm.shape[0]
    n_sub = s_len // dil
    nq = n_sub // ATT_BQ
    assert n_sub % ATT_BQ == 0 and n_sub >= ATT_W
    cols = qkvm.shape[1] // LANES
    view = qkvm.reshape(n_sub, dil * qkvm.shape[1])

    def kern(q_ref, k_ref, v_ref, b_ref, o_ref, lse_ref):
        i = pl.program_id(2)
        start, off = _att_window(i, n_sub)
        valid = _att_valid(off)
        q = q_ref[...] * (ATT_HEAD_DIM ** -0.5)
        kw = k_ref[pl.ds(start, ATT_W), :]
        vw = v_ref[pl.ds(start, ATT_W), :]
        o = jnp.zeros((ATT_BQ, LANES), F32)
        lse = jnp.zeros((ATT_BQ, LANES), F32)
        for h in range(2):
            mh = _lane_head_mask(LANES, ATT_HEAD_DIM, h)
            s = _dot_nt(jnp.where(mh, q, 0.0), kw) + b_ref[h]
            s = jnp.where(valid, s, NEG_INF)
            mx = jnp.max(s, axis=-1, keepdims=True)
            p = jnp.exp(s - mx)
            den = jnp.sum(p, axis=-1, keepdims=True)
            o = jnp.where(mh, _dot(p, vw) / den, o)
            lse = jnp.where(mh, mx + jnp.log(den), lse)
        o_ref[...] = o
        lse_ref[...] = lse

    def qkv_spec(which, full):
        shape = (n_sub, LANES) if full else (ATT_BQ, LANES)
        return pl.BlockSpec(shape, lambda pr, r, i: (0 if full else i, r * cols + which * 6 + 2 * gi + pr))

    out_spec = pl.BlockSpec((ATT_BQ, LANES), lambda pr, r, i: (i, r * 2 + pr))
    o, lse = pl.pallas_call(
        kern, name=name, grid=(2, dil, nq),
        in_specs=[qkv_spec(0, False), qkv_spec(1, True), qkv_spec(2, True),
                  pl.BlockSpec((None, 2, ATT_BQ, ATT_W), lambda pr, r, i: (_att_tile_id(i, nq), pr, 0, 0))],
        out_specs=[out_spec, out_spec],
        out_shape=[jax.ShapeDtypeStruct((n_sub, dil * 2 * LANES), F32)] * 2,
        compiler_params=_params("parallel", "parallel", "arbitrary"),
    )(view, view, view, bias)
    return o.reshape(s_len, 2 * LANES), lse.reshape(s_len, 2 * LANES)


def _att_bwd(qkvm, bias, lse_tot, delta, dcat, gi, *, name):
    dil = DILATIONS[gi]
    s_len = qkvm.shape[0]
    n_sub = s_len // dil
    nq = n_sub // ATT_BQ
    cols = qkvm.shape[1] // LANES
    dcols = dcat.shape[1] // LANES
    view = qkvm.reshape(n_sub, dil * qkvm.shape[1])
    lse_v = lse_tot.reshape(n_sub, dil * 2 * LANES)
    delta_v = delta.reshape(n_sub, dil * 2 * LANES)
    dcat_v = dcat.reshape(n_sub, dil * dcat.shape[1])

    def kern(q_ref, k_ref, v_ref, b_ref, lse_ref, dl_ref, dm_ref, dq_ref, dk_ref, dv_ref, db_ref):
        r, i = pl.program_id(1), pl.program_id(2)
        start, off = _att_window(i, n_sub)
        valid = _att_valid(off)
        tile = _att_tile_id(i, nq)

        @pl.when(i == 0)
        def _():
            dk_ref[...] = jnp.zeros_like(dk_ref)
            dv_ref[...] = jnp.zeros_like(dv_ref)

        @pl.when((i == 0) & (r == 0))
        def _():
            db_ref[...] = jnp.zeros_like(db_ref)

        q = q_ref[...] * (ATT_HEAD_DIM ** -0.5)
        kw = k_ref[pl.ds(start, ATT_W), :]
        vw = v_ref[pl.ds(start, ATT_W), :]
        dm = dm_ref[...]
        lse = lse_ref[...]
        dl = dl_ref[...]
        dq = jnp.zeros((ATT_BQ, LANES), F32)
        dkw = jnp.zeros((ATT_W, LANES), F32)
        dvw = jnp.zeros((ATT_W, LANES), F32)
        for h in range(2):
            mh = _lane_head_mask(LANES, ATT_HEAD_DIM, h)
            qh = jnp.where(mh, q, 0.0)
            dmh = jnp.where(mh, dm, 0.0)
            s = _dot_nt(qh, kw) + b_ref[tile, h]
            s = jnp.where(valid, s, NEG_INF)
            lse_h = jnp.max(jnp.where(mh, lse, NEG_INF), axis=-1, keepdims=True)
            dl_h = jnp.max(jnp.where(mh, dl, NEG_INF), axis=-1, keepdims=True)
            p = jnp.exp(s - lse_h)
            ds = p * (_dot_nt(dmh, vw) - dl_h)
            dq = dq + jnp.where(mh, _dot(ds, kw), 0.0)
            dkw = dkw + _dot_tn(ds, qh)
            dvw = dvw + _dot_tn(p, dmh)
            db_ref[tile, h] += ds
        dq_ref[...] = dq * (ATT_HEAD_DIM ** -0.5)
        dk_ref[pl.ds(start, ATT_W), :] += dkw
        dv_ref[pl.ds(start, ATT_W), :] += dvw

    def qkv_spec(which, full):
        shape = (n_sub, LANES) if full else (ATT_BQ, LANES)
        return pl.BlockSpec(shape, lambda pr, r, i: (0 if full else i, r * cols + which * 6 + 2 * gi + pr))

    blk = pl.BlockSpec((ATT_BQ, LANES), lambda pr, r, i: (i, r * 2 + pr))
    full = pl.BlockSpec((n_sub, LANES), lambda pr, r, i: (0, r * 2 + pr))
    bias_spec = pl.BlockSpec((3, 2, ATT_BQ, ATT_W), lambda pr, r, i: (0, pr, 0, 0))
    sub = jax.ShapeDtypeStruct((n_sub, dil * 2 * LANES), F32)
    dq, dk, dv, db = pl.pallas_call(
        kern, name=name, grid=(2, dil, nq),
        in_specs=[qkv_spec(0, False), qkv_spec(1, True), qkv_spec(2, True), bias_spec, blk, blk,
                  pl.BlockSpec((ATT_BQ, LANES), lambda pr, r, i: (i, r * dcols + 2 * gi + pr))],
        out_specs=[blk, full, full, bias_spec],
        out_shape=[sub, sub, sub, jax.ShapeDtypeStruct(bias.shape, F32)],
        compiler_params=_params("arbitrary", "arbitrary", "arbitrary"),
    )(view, view, view, bias, lse_v, delta_v, dcat_v)
    return dq.reshape(s_len, -1), dk.reshape(s_len, -1), dv.reshape(s_len, -1), db


def _att_combine(o_g, lse_g, qkvm, kv_mem, *, name):
    def body(o0, o1, o2, l0, l1, l2, qm, kv):
        mx = jnp.maximum(jnp.maximum(l0, l1), l2)
        tot = mx + jnp.log(jnp.exp(l0 - mx) + jnp.exp(l1 - mx) + jnp.exp(l2 - mx))
        mixed = [o * jnp.exp(l - tot) for o, l in ((o0, l0), (o1, l1), (o2, l2))]
        return (jnp.concatenate(mixed + [_mem_attn(qm, kv)], axis=1), tot), ()
    return _rowwise(body, list(o_g) + list(lse_g) + [_col(qkvm, MEM_WIDTH, (3 * TOK_WIDTH) // MEM_WIDTH)], [kv_mem],
                    [(D_MODEL, F32), (MEM_WIDTH, F32)], [], tm=256, name=name)


def _head_sum_matrix():
    a = np.arange(MEM_WIDTH)
    return jnp.asarray((a[:, None] // ATT_HEAD_DIM == a[None, :] // ATT_HEAD_DIM).astype(np.float32))


def _att_bwd_prep(cat, dcat, qkvm, kv_mem, *, name):
    def body(cat, dcat, qm, kv, hs):
        prod = cat * dcat
        summed = prod[:, 0:256] + prod[:, 256:512] + prod[:, 512:768]
        delta = _dot(summed, hs, precision=HI)
        dqm, dkv = _mem_attn_bwd(qm, kv, dcat[:, TOK_WIDTH:])
        return (delta, dqm), (dkv,)
    return _rowwise(body, [cat, dcat, _col(qkvm, MEM_WIDTH, (3 * TOK_WIDTH) // MEM_WIDTH)], [kv_mem, _head_sum_matrix()],
                    [(MEM_WIDTH, F32), (MEM_WIDTH, F32)], [kv_mem.shape], tm=256, name=name)


def _dn_conv_post(s, j):
    scale = jnp.where(j < DN_HEADS, DN_HEAD_DIM ** -0.5, 1.0)
    normed = s * lax.rsqrt(jnp.sum(s * s, axis=-1, keepdims=True) + EPS) * scale
    return jnp.where(j >= 2 * DN_HEADS, s, normed)


def _shift_rows(x, sh):
    n = x.shape[0]
    row = lax.broadcasted_iota(jnp.int32, (n, 1), 0)
    rolled = pltpu.roll(x, (-sh) % n, 0)
    return jnp.where((row + sh >= 0) & (row + sh < n), rolled, 0.0)


def _dn_conv_taps(x, w_ref):
    c = x * w_ref[pl.ds(DN_CONV // 2, 1), :]
    for jj in range(DN_CONV):
        if jj != DN_CONV // 2:
            c = c + _shift_rows(x, jj - DN_CONV // 2) * w_ref[pl.ds(jj, 1), :]
    return c


def _dn_conv_fwd(proj, conv_w, *, name):
    s_len = proj.shape[0]
    width = 3 * TOK_WIDTH

    def kern(x_ref, w_ref, o_ref):
        j = pl.program_id(0)
        o_ref[...] = _dn_conv_post(_silu(_dn_conv_taps(x_ref[...], w_ref)), j)

    return pl.pallas_call(
        kern, name=name, grid=(width // LANES,),
        in_specs=[pl.BlockSpec((s_len, LANES), lambda j: (0, j)), pl.BlockSpec((DN_CONV, LANES), lambda j: (0, j))],
        out_specs=pl.BlockSpec((s_len, LANES), lambda j: (0, j)),
        out_shape=jax.ShapeDtypeStruct((s_len, width), F32),
        compiler_params=_params("parallel"),
    )(proj, conv_w)


def _dn_conv_bwd(proj, conv_w, d_fwd, d_bwd, which, *, name):
    s_len = proj.shape[0]

    def kern(x_ref, w_ref, df_ref, db_ref, dx_ref, dw_ref):
        j = pl.program_id(0) + which * DN_HEADS
        x = x_ref[...]
        c = _dn_conv_taps(x, w_ref)
        _, vjp = jax.vjp(lambda c_: _dn_conv_post(_silu(c_), j), c)
        dc = vjp(df_ref[...] + db_ref[...])[0]
        dx = dc * w_ref[pl.ds(DN_CONV // 2, 1), :]
        for jj in range(DN_CONV):
            sh = jj - DN_CONV // 2
            if sh != 0:
                dx = dx + _shift_rows(dc, -sh) * w_ref[pl.ds(jj, 1), :]
            dw_ref[pl.ds(jj, 1), :] = jnp.sum(dc * _shift_rows(x, sh), axis=0, keepdims=True)
        dx_ref[...] = dx

    return pl.pallas_call(
        kern, name=name, grid=(DN_HEADS,),
        in_specs=[pl.BlockSpec((s_len, LANES), lambda j: (0, j + which * DN_HEADS)),
                  pl.BlockSpec((DN_CONV, LANES), lambda j: (0, j + which * DN_HEADS)),
                  pl.BlockSpec((s_len, LANES), lambda j: (0, j)),
                  pl.BlockSpec((s_len, LANES), lambda j: (0, j))],
        out_specs=[pl.BlockSpec((s_len, LANES), lambda j: (0, j)), pl.BlockSpec((DN_CONV, LANES), lambda j: (0, j))],
        out_shape=[jax.ShapeDtypeStruct((s_len, TOK_WIDTH), F32), jax.ShapeDtypeStruct((DN_CONV, TOK_WIDTH), F32)],
        compiler_params=_params("parallel"),
    )(proj, conv_w, d_fwd, d_bwd)


GATE_TM = 2 * DN_CHUNK


FWD_GATE_LANES = 2 * DN_HEADS


def _gate_constants():
    i = np.arange(GATE_TM)
    same = (i[:, None] // DN_CHUNK) == (i[None, :] // DN_CHUNK)
    cum_f = same & (i[None, :] <= i[:, None])
    cum_r = same & (i[None, :] >= i[:, None])
    return tuple(jnp.asarray(np.asarray(a, np.float32)) for a in (cum_f, cum_r, same))


def _gate_params(p):
    z = jnp.zeros((DN_HEADS,), F32)
    return jnp.concatenate([p[0], z, p[1], z, jnp.zeros((LANES - N_GATES,), F32)]).reshape(1, LANES)


def _gate_params_bwd(dp):
    return jnp.stack([dp[0, 0:DN_HEADS], dp[0, 2 * DN_HEADS: 3 * DN_HEADS]])


def _dn_gates(gate_in, a_cols, dt_cols, cum_f, cum_r, tot):
    g = -jnp.exp(a_cols) * _softplus(gate_in + dt_cols)
    fwd_lane = lax.broadcasted_iota(jnp.int32, (1, LANES), 1) < FWD_GATE_LANES
    gc = jnp.where(fwd_lane, _dot(cum_f, g, precision=HI), _dot(cum_r, g, precision=HI))
    return gc, _dot(tot, g, precision=HI), jax.nn.sigmoid(gate_in)


def _dn_gates_fwd(proj, a_cols, dt_cols, *, name):
    def body(gi, *consts):
        return _dn_gates(gi, *consts), ()
    return _rowwise(body, [_col(proj, LANES, DN_IN_PAD // LANES - 1)], [a_cols, dt_cols, *_gate_constants()],
                    [(LANES, F32)] * 3, [], tm=GATE_TM, name=name)


def _dn_gates_bwd(proj, a_cols, dt_cols, d_gates, *, name):
    def body(gi, gcf, gtf, bf, gcr, gtr, br, a, dt, *consts):
        _, vjp = jax.vjp(lambda gi_, a_, dt_: _dn_gates(gi_, a_, dt_, *consts), gi, a, dt)
        dgi, da, ddt = vjp((gcf + gcr, gtf + gtr, bf + br))
        return (dgi,), (da, ddt)
    return _rowwise(body, [_col(proj, LANES, DN_IN_PAD // LANES - 1), *d_gates[0], *d_gates[1]],
                    [a_cols, dt_cols, *_gate_constants()], [(LANES, F32)], [a_cols.shape, dt_cols.shape],
                    tm=GATE_TM, name=name)


INV_BASE = 8


def _block_id_equal(c, size):
    i = lax.broadcasted_iota(jnp.int32, (c, c), 0) // size
    j = lax.broadcasted_iota(jnp.int32, (c, c), 1) // size
    return (i == j).astype(F32)


def _unit_tri_inverse_impl(lmat):
    c = lmat.shape[0]
    eye = _block_id_equal(c, 1)
    same = _block_id_equal(c, INV_BASE)
    neg = -lmat * same
    inv = eye + neg
    power = neg
    for _ in range(int(math.log2(INV_BASE)) - 1):
        power = _dot(power, power)
        inv = inv + _dot(inv, power)
    size = INV_BASE
    while size < c:
        bigger = _block_id_equal(c, 2 * size)
        inv = inv - _dot(_dot(inv, lmat * (bigger - same)), inv)
        same, size = bigger, 2 * size
    resid = eye - _dot(eye + lmat, inv, precision=HI)
    return inv + _dot(inv, resid)


@jax.custom_vjp
def _unit_tri_inverse(lmat):
    return _unit_tri_inverse_impl(lmat)


def _unit_tri_inverse_fwd(lmat):
    inv = _unit_tri_inverse_impl(lmat)
    return inv, inv


def _unit_tri_inverse_bwd(inv, d_inv):
    return (-_dot_tn(inv, _dot_nt(d_inv, inv)),)


_unit_tri_inverse.defvjp(_unit_tri_inverse_fwd, _unit_tri_inverse_bwd)


def _dn_chunk(q, k, v, gates_t, gc_row, tot_row, beta_row, state, tri, inverse):
    c = q.shape[0]
    assert c == DN_HEAD_DIM
    eye = _block_id_equal(c, 1)

    def along_rows(x, pick):
        return jnp.broadcast_to(jnp.sum(x * pick, axis=0, keepdims=True), (c, c))

    gc_j = along_rows(gates_t[0], gc_row)
    gc = gc_j.T
    g_tot = along_rows(gates_t[1], tot_row)
    beta = along_rows(gates_t[2], beta_row).T
    decay = jnp.exp(jnp.where(tri > 0, gc - gc_j, NEG_INF))
    k_beta = k * beta
    inv = inverse((tri - eye) * (_dot_nt(k_beta, k) * decay))
    e_gc = jnp.exp(gc)
    u = _dot(inv, v * beta)
    w = _dot(inv, k_beta * e_gc)
    intra = tri * (_dot_nt(q, k) * decay)
    v_new = u - _dot(w, state)
    out = _dot(q * e_gc, state) + _dot(intra, v_new)
    state = state * jnp.exp(g_tot) + _dot_tn(k * jnp.exp(g_tot - gc), v_new)
    return out, state


def _dn_tri():
    i = np.arange(DN_CHUNK)
    tri = np.stack([(i[None, :] <= i[:, None]), (i[None, :] >= i[:, None])]).astype(np.float32)
    return jnp.asarray(np.repeat(tri, DN_HEADS, axis=0))


def _dn_gate_picks():
    picks = np.zeros((3, 2 * DN_HEADS, 2 * DN_CHUNK, 1), np.float32)
    for d in range(2):
        for h in range(DN_HEADS):
            alpha = d * DN_CHUNK + d * 2 * DN_HEADS + h
            picks[0, d * DN_HEADS + h, alpha] = 1.0
            picks[1, d * DN_HEADS + h, alpha] = 1.0
            picks[2, d * DN_HEADS + h, alpha + DN_HEADS] = 1.0
    return jnp.asarray(picks)


def _stack_chains(fwd_ref, rev_ref):
    return jnp.stack([r[:, _head_cols(h)] for r in (fwd_ref, rev_ref) for h in range(DN_HEADS)])


def _unstack_chains(val, fwd_ref, rev_ref):
    for d, r in enumerate((fwd_ref, rev_ref)):
        for h in range(DN_HEADS):
            r[:, _head_cols(h)] = val[d * DN_HEADS + h]


def _gates_transposed(fwd_refs, rev_refs):
    return jnp.stack([jnp.concatenate([f[...].T, r[...].T], axis=0) for f, r in zip(fwd_refs, rev_refs, strict=True)])


def _dn_row_spec(nc, col, reverse, width=TOK_WIDTH):
    return pl.BlockSpec((DN_CHUNK, width), lambda t: ((nc - 1 - t) if reverse else t, col))


def _dn_state_spec(nc, reverse):
    return pl.BlockSpec((None, DN_HEADS, DN_HEAD_DIM, DN_HEAD_DIM), lambda t: ((nc - 1 - t) if reverse else t, 0, 0, 0))


def _head_cols(h):
    return pl.ds(h * DN_HEAD_DIM, DN_HEAD_DIM)


def _const_spec(arr):
    return pl.BlockSpec(arr.shape, functools.partial(lambda t, n: (0,) * n, n=arr.ndim))


def _dn_chains(inverse):
    return jax.vmap(lambda q, k, v, gates_t, *rest: _dn_chunk(q, k, v, gates_t, *rest, inverse),
                    in_axes=(0, 0, 0, None, 0, 0, 0, 0, 0))


def _dn_scan_fwd(qkv, gates, *, name):
    s_len = qkv.shape[0]
    nc = s_len // DN_CHUNK
    tri, picks = _dn_tri(), _dn_gate_picks()

    def kern(*refs):
        ins, (tri_ref, pick_ref, of_ref, or_ref, sf_ref, sr_ref, state) = refs[:12], refs[12:]

        @pl.when(pl.program_id(0) == 0)
        def _():
            state[...] = jnp.zeros_like(state)

        entry = state[...]
        qkv_c = [_stack_chains(ins[i], ins[6 + i]) for i in range(3)]
        gates_t = _gates_transposed(ins[3:6], ins[9:12])
        out, new = _dn_chains(_unit_tri_inverse_impl)(*qkv_c, gates_t, pick_ref[0], pick_ref[1], pick_ref[2], entry, tri_ref[...])
        sf_ref[...] = entry[:DN_HEADS]
        sr_ref[...] = entry[DN_HEADS:]
        _unstack_chains(out, of_ref, or_ref)
        state[...] = new

    in_specs = []
    for rev in (False, True):
        in_specs += [_dn_row_spec(nc, col, rev) for col in (0, 1, 2)] + [_dn_row_spec(nc, 0, rev, LANES)] * 3
    in_specs += [_const_spec(tri), _const_spec(picks)]
    return pl.pallas_call(
        kern, name=name, grid=(nc,), in_specs=in_specs,
        out_specs=[_dn_row_spec(nc, 0, False), _dn_row_spec(nc, 0, True), _dn_state_spec(nc, False), _dn_state_spec(nc, True)],
        out_shape=[jax.ShapeDtypeStruct((s_len, TOK_WIDTH), F32)] * 2
        + [jax.ShapeDtypeStruct((nc, DN_HEADS, DN_HEAD_DIM, DN_HEAD_DIM), F32)] * 2,
        scratch_shapes=[pltpu.VMEM((2 * DN_HEADS, DN_HEAD_DIM, DN_HEAD_DIM), F32)],
        compiler_params=_params("arbitrary"),
    )(*([qkv, qkv, qkv, *gates] * 2), tri, picks)


def _dn_scan_bwd(qkv, gates, states, d_o, *, name):
    s_len = qkv.shape[0]
    nc = s_len // DN_CHUNK
    tri, picks = _dn_tri(), _dn_gate_picks()

    def kern(*refs):
        ins, tri_ref, pick_ref, outs, d_state = refs[:16], refs[16], refs[17], refs[18:30], refs[30]

        @pl.when(pl.program_id(0) == 0)
        def _():
            d_state[...] = jnp.zeros_like(d_state)

        qkv_c = [_stack_chains(ins[i], ins[8 + i]) for i in range(3)]
        gates_t = _gates_transposed(ins[3:6], ins[11:14])
        entry = jnp.concatenate([ins[6][...], ins[14][...]], axis=0)
        d_out = _stack_chains(ins[7], ins[15])
        tri_v, picks_v = tri_ref[...], pick_ref[...]
        _, vjp = jax.vjp(lambda q, k, v, g, s: _dn_chains(_unit_tri_inverse)(q, k, v, g, picks_v[0], picks_v[1], picks_v[2], s, tri_v),
                         *qkv_c, gates_t, entry)
        dq, dk, dv, d_gates_t, d_entry = vjp((d_out, d_state[...]))
        for i, val in enumerate((dq, dk, dv)):
            _unstack_chains(val, outs[i], outs[6 + i])
        for i in range(3):
            outs[3 + i][...] = d_gates_t[i, :DN_CHUNK].T
            outs[9 + i][...] = d_gates_t[i, DN_CHUNK:].T
        d_state[...] = d_entry

    in_specs, out_specs, out_shape = [], [], []
    for rev in (True, False):
        in_specs += [_dn_row_spec(nc, col, rev) for col in (0, 1, 2)] + [_dn_row_spec(nc, 0, rev, LANES)] * 3
        in_specs += [_dn_state_spec(nc, rev), _dn_row_spec(nc, 0, rev)]
        out_specs += [_dn_row_spec(nc, 0, rev)] * 3 + [_dn_row_spec(nc, 0, rev, LANES)] * 3
        out_shape += [jax.ShapeDtypeStruct((s_len, TOK_WIDTH), F32)] * 3 + [jax.ShapeDtypeStruct((s_len, LANES), F32)] * 3
    in_specs += [_const_spec(tri), _const_spec(picks)]
    res = pl.pallas_call(
        kern, name=name, grid=(nc,), in_specs=in_specs, out_specs=out_specs, out_shape=out_shape,
        scratch_shapes=[pltpu.VMEM((2 * DN_HEADS, DN_HEAD_DIM, DN_HEAD_DIM), F32)],
        compiler_params=_params("arbitrary"),
    )(*[a for d in range(2) for a in (qkv, qkv, qkv, *gates, states[d], d_o)], tri, picks)
    return (res[0:3], res[3:6]), (res[6:9], res[9:12])


def _dn_out_head(o_f, o_b, z, gain):
    o = o_f + o_b
    return o * lax.rsqrt(jnp.mean(o * o, axis=-1, keepdims=True) + EPS) * gain * _silu(z)


def _dn_out(o_fwd, o_rev, proj, gain, qkv_kv_mem, *, name):
    def body(of, ob, z, qm, g, kv):
        heads = []
        for h in range(DN_HEADS):
            sl = slice(h * DN_HEAD_DIM, (h + 1) * DN_HEAD_DIM)
            heads.append(_dn_out_head(of[:, sl], ob[:, sl], z[:, sl], g))
        return (jnp.concatenate(heads + [_mem_attn(qm, kv)], axis=1),), ()
    return _rowwise(body, [o_fwd, o_rev, _col(proj, TOK_WIDTH, 3),
                           _col(proj, MEM_WIDTH, (4 * TOK_WIDTH) // MEM_WIDTH)], [gain, qkv_kv_mem],
                    [(D_MODEL, MXU_DTYPE)], [], tm=256, name=name)[0]


def _dn_out_bwd(o_fwd, o_rev, proj, gain, kv_mem, dcat, *, name):
    def body(of, ob, z, qm, dcat, g, kv):
        dos, dzs = [], []
        dgain = jnp.zeros_like(g)
        for h in range(DN_HEADS):
            sl = slice(h * DN_HEAD_DIM, (h + 1) * DN_HEAD_DIM)
            _, vjp = jax.vjp(_dn_out_head, of[:, sl], ob[:, sl], z[:, sl], g)
            d_of, _, dz, dg = vjp(dcat[:, sl])
            dos.append(d_of)
            dzs.append(dz)
            dgain = dgain + dg
        dqm, dkv = _mem_attn_bwd(qm, kv, dcat[:, TOK_WIDTH:])
        return (jnp.concatenate(dos, axis=1), jnp.concatenate(dzs, axis=1), dqm), (dgain, dkv)
    return _rowwise(body, [o_fwd, o_rev, _col(proj, TOK_WIDTH, 3),
                           _col(proj, MEM_WIDTH, (4 * TOK_WIDTH) // MEM_WIDTH), dcat], [gain, kv_mem],
                    [(TOK_WIDTH, F32), (TOK_WIDTH, F32), (MEM_WIDTH, F32)], [gain.shape, kv_mem.shape], tm=256, name=name)


def _pad_dn_w_in(w):
    gates = w[:, 4 * TOK_WIDTH: 4 * TOK_WIDTH + N_GATES]
    zeros = jnp.zeros((w.shape[0], DN_IN_PAD - DN_IN), w.dtype)
    return jnp.concatenate([w[:, :4 * TOK_WIDTH], w[:, 4 * TOK_WIDTH + N_GATES:], gates, zeros], axis=1)


def _unpad_dn_w_in(w):
    q_mem = w[:, 4 * TOK_WIDTH: 4 * TOK_WIDTH + MEM_WIDTH]
    gates = w[:, 4 * TOK_WIDTH + MEM_WIDTH: 4 * TOK_WIDTH + MEM_WIDTH + N_GATES]
    return jnp.concatenate([w[:, :4 * TOK_WIDTH], gates, q_mem], axis=1)


def _ffn_fwd(h, w_gu, w_d, tag):
    gu = _mm(h, w_gu, name=f"ffn_gu_{tag}")
    act = _swiglu_act(gu, name=f"ffn_act_{tag}")
    return gu, act, _mm(act, w_d, name=f"ffn_down_{tag}")


def _ffn_bwd(h, gu, act, w_gu, w_d, df, tag):
    d_act = _mm(df, w_d, tb=True, name=f"ffn_dact_{tag}")
    d_wd = _mm(act, df, ta=True, name=f"ffn_dwd_{tag}")
    d_gu = _swiglu_act_bwd(gu, d_act, name=f"ffn_dgu_{tag}")
    dh = _mm(d_gu, w_gu, tb=True, name=f"ffn_dh_{tag}")
    d_wgu = _mm(h, d_gu, ta=True, name=f"ffn_dwgu_{tag}")
    return dh, d_wgu, d_wd


def _local_step(x, mem, target, p):
    g = {}
    row = lambda v: v.reshape(1, -1)
    gains = {k: [row(p[k][i]) for i in range(2)] for k in
             ("mem_norm", "norm_mix_pre", "norm_mix_post", "norm_ffn_pre", "norm_ffn_post")}
    out_gain = row(p["dn_out_norm"])
    a_cols, dt_cols = _gate_params(p["dn_a_log"]), _gate_params(p["dn_dt_bias"])

    h0 = _pre_norm(x, gains["norm_mix_pre"][0], name="pre0")
    mem_n = [_pre_norm(mem, gains["mem_norm"][i], name=f"mem_norm{i}") for i in range(2)]
    kv_mem = [_mm(mem_n[i], p["mem_w_kv"][i], name=f"mem_kv{i}") for i in range(2)]
    qkvm = _mm(h0, p["att_w_in"], name="att_in")
    bias = [_bias_tiles(p["rel_bias"], gi) for gi in range(3)]
    att = [_att_fwd(qkvm, bias[gi], gi, name=f"att_fwd{gi}") for gi in range(3)]
    cat0, lse_tot = _att_combine([a[0] for a in att], [a[1] for a in att], qkvm, kv_mem[0], name="att_combine")
    mo0 = _mm(cat0, p["att_w_out"], name="att_out")
    x1, h1 = _post_pre(x, mo0, gains["norm_mix_post"][0], gains["norm_ffn_pre"][0], name="post_mix0")
    gu0, act0, f0 = _ffn_fwd(h1, p["ffn_w_gate_up"][0], p["ffn_w_down"][0], 0)
    x2, h2 = _post_pre(x1, f0, gains["norm_ffn_post"][0], gains["norm_mix_pre"][1], name="post_ffn0")

    proj = _mm(h2, p["dn_w_in"], name="dn_in")
    qkv = _dn_conv_fwd(proj, p["dn_conv"], name="dn_conv")
    gates = _dn_gates_fwd(proj, a_cols, dt_cols, name="dn_gates")
    o_fwd, o_rev, st_fwd, st_rev = _dn_scan_fwd(qkv, gates, name="dn_scan")
    cat1 = _dn_out(o_fwd, o_rev, proj, out_gain, kv_mem[1], name="dn_outnorm")
    mo1 = _mm(cat1, p["dn_w_out"], name="dn_out")
    x3, h3 = _post_pre(x2, mo1, gains["norm_mix_post"][1], gains["norm_ffn_pre"][1], name="post_mix1")
    gu1, act1, f1 = _ffn_fwd(h3, p["ffn_w_gate_up"][1], p["ffn_w_down"][1], 1)

    dx3, df1, dg_ffn_post1, loss_cols = _final_loss_bwd(x3, f1, gains["norm_ffn_post"][1], target, name="loss_bwd")
    dh3, d_wgu1, d_wd1 = _ffn_bwd(h3, gu1, act1, p["ffn_w_gate_up"][1], p["ffn_w_down"][1], df1, 1)
    dx2, dmo1, dg_mix_post1, dg_ffn_pre1 = _post_pre_bwd(x2, mo1, gains["norm_mix_post"][1], gains["norm_ffn_pre"][1],
                                                         dx3, dh3, name="post_mix1_bwd")
    dcat1 = _mm(dmo1, p["dn_w_out"], tb=True, name="dn_out_dx")
    g["dn_w_out"] = _mm(cat1, dmo1, ta=True, name="dn_out_dw")
    d_o, dz, dqm1, d_out_gain, dkv1 = _dn_out_bwd(o_fwd, o_rev, proj, out_gain, kv_mem[1], dcat1, name="dn_outnorm_bwd")
    (d_f, dg_f), (d_r, dg_r) = _dn_scan_bwd(qkv, gates, (st_fwd, st_rev), d_o, name="dn_scan_bwd")
    d_gate_cols, d_a_cols, d_dt_cols = _dn_gates_bwd(proj, a_cols, dt_cols, (dg_f, dg_r), name="dn_gates_bwd")
    d_pre, d_conv = zip(*[_dn_conv_bwd(proj, p["dn_conv"], d_f[which], d_r[which], which, name=f"dn_conv_bwd{which}")
                          for which in range(3)])
    dproj = jnp.concatenate(list(d_pre) + [dz, dqm1, d_gate_cols], axis=1).astype(MXU_DTYPE)
    dh2 = _mm(dproj, p["dn_w_in"], tb=True, name="dn_in_dx")
    g["dn_w_in"] = _mm(h2, dproj, ta=True, name="dn_in_dw")
    g["dn_conv"] = jnp.concatenate(d_conv, axis=1)
    g["dn_a_log"] = _gate_params_bwd(d_a_cols)
    g["dn_dt_bias"] = _gate_params_bwd(d_dt_cols)
    g["dn_out_norm"] = d_out_gain

    dx1, df0, dg_ffn_post0, dg_mix_pre1 = _post_pre_bwd(x1, f0, gains["norm_ffn_post"][0], gains["norm_mix_pre"][1],
                                                        dx2, dh2, name="post_ffn0_bwd")
    dh1, d_wgu0, d_wd0 = _ffn_bwd(h1, gu0, act0, p["ffn_w_gate_up"][0], p["ffn_w_down"][0], df0, 0)
    dx0, dmo0, dg_mix_post0, dg_ffn_pre0 = _post_pre_bwd(x, mo0, gains["norm_mix_post"][0], gains["norm_ffn_pre"][0],
                                                         dx1, dh1, name="post_mix0_bwd")
    dcat0 = _mm(dmo0, p["att_w_out"], tb=True, name="att_out_dx")
    g["att_w_out"] = _mm(cat0, dmo0, ta=True, name="att_out_dw")
    delta, dqm0, dkv0 = _att_bwd_prep(cat0, dcat0, qkvm, kv_mem[0], name="att_bwd_prep")
    att_b = [_att_bwd(qkvm, bias[gi], lse_tot, delta, dcat0, gi, name=f"att_bwd{gi}") for gi in range(3)]
    dqkvm = jnp.concatenate([a[w] for w in range(3) for a in att_b] + [dqm0], axis=1).astype(MXU_DTYPE)
    g["rel_bias"] = sum(_bias_tiles_bwd(p["rel_bias"], att_b[gi][3], gi) for gi in range(3))
    dh0 = _mm(dqkvm, p["att_w_in"], tb=True, name="att_in_dx")
    g["att_w_in"] = _mm(h0, dqkvm, ta=True, name="att_in_dw")
    grad_x, dg_mix_pre0 = _pre_norm_bwd(x, gains["norm_mix_pre"][0], dh0, dx0, name="pre0_bwd")

    d_mem_kv, d_mem_norm = [], []
    for i, dkv in enumerate((dkv0, dkv1)):
        d_mem_kv.append(_mm(mem_n[i], dkv, ta=True, name=f"mem_kv_dw{i}"))
        d_mem_n = _mm(dkv, p["mem_w_kv"][i], tb=True, name=f"mem_kv_dx{i}")
        d_mem_norm.append(_gain_bwd(mem, gains["mem_norm"][i], d_mem_n, name=f"mem_norm_bwd{i}"))
    g["mem_w_kv"] = d_mem_kv
    g["mem_norm"] = jnp.concatenate(d_mem_norm, axis=0)
    g["norm_mix_pre"] = jnp.concatenate([dg_mix_pre0, dg_mix_pre1], axis=0)
    g["norm_mix_post"] = jnp.concatenate([dg_mix_post0, dg_mix_post1], axis=0)
    g["norm_ffn_pre"] = jnp.concatenate([dg_ffn_pre0, dg_ffn_pre1], axis=0)
    g["norm_ffn_post"] = jnp.concatenate([dg_ffn_post0, dg_ffn_post1], axis=0)
    g["ffn_w_gate_up"] = [d_wgu0, d_wgu1]
    g["ffn_w_down"] = [d_wd0, d_wd1]
    return loss_cols, grad_x, g


N_CHIPS = 4
N_DEV = 8
MESH = pl.DeviceIdType.MESH
BIG = (("att_w_in", (1, 1024, 640), 2), ("att_w_out", (1, 256, 1024), 1), ("dn_w_in", (1, 1024, 838), 2),
       ("dn_w_out", (1, 256, 1024), 1), ("mem_w_kv", (2, 256, 512), 1), ("ffn_w_gate_up", (2, 1024, 1408), 2),
       ("ffn_w_down", (2, 704, 1024), 1))
PACK_COLS = 1024
PACK_ELEMS = sum(math.prod(shape) for _, shape, _ in BIG)
HALF_ROWS = -(-PACK_ELEMS // (2 * PACK_COLS * 16)) * 16
PACK_ROWS = 2 * HALF_ROWS
HALF_TM = _tile(HALF_ROWS, 512, 16)


def _pack_local(blocks):
    flat = [layer.reshape(-1) for n, _, _ in BIG for layer in (blocks[n] if isinstance(blocks[n], list) else [blocks[n]])]
    dtype = flat[0].dtype
    flat.append(jnp.zeros((PACK_ROWS * PACK_COLS - PACK_ELEMS,), dtype))
    return jnp.concatenate(flat).reshape(PACK_ROWS, PACK_COLS)


def _unpack_local(slab):
    flat = slab.reshape(-1)
    out, off = {}, 0
    for n, shape, _ in BIG:
        size = math.prod(shape)
        out[n] = flat[off: off + size].reshape(shape)
        off += size
    return out


def _full_from_gathered(gathered):
    per_chip = [_unpack_local(gathered[s]) for s in range(N_CHIPS)]
    return {n: jnp.concatenate([pc[n] for pc in per_chip], axis=axis) for n, _, axis in BIG}


def _pack_full(full):
    slabs = []
    for s in range(N_CHIPS):
        blocks = {}
        for n, shape, axis in BIG:
            blocks[n] = [lax.slice_in_dim(layer, s * shape[axis], (s + 1) * shape[axis], axis=axis - 1) for layer in full[n]]
        slabs.append(_pack_local(blocks))
    return jnp.stack(slabs)


def _mesh_pos():
    return lax.axis_index("x"), lax.axis_index("y"), lax.axis_index("c")


def _other_chips(x, y):
    return [(1 - x, y), (x, 1 - y), (1 - x, 1 - y)]


ANY = pl.BlockSpec(memory_space=pl.ANY)


def _gather_shards(slab, *, name):
    rows, cols = slab.shape
    half = rows // 2

    def body(x_ref, out_ref, send_sems, recv_sems):
        x, y, c = _mesh_pos()
        me = 2 * x + y
        sibling = (x, y, 1 - c)
        chips = _other_chips(x, y)

        def part(chip, h):
            return out_ref.at[2 * chip[0] + chip[1], pl.ds(h * half, half), :]

        def copy(k, src, dst, to):
            return pltpu.make_async_remote_copy(src_ref=src, dst_ref=dst, send_sem=send_sems.at[k],
                                                recv_sem=recv_sems.at[k], device_id=to, device_id_type=MESH)

        my_half = x_ref.at[pl.ds(c * half, half), :]
        first = [copy(j, my_half, part((x, y), c), (*chip, c)) for j, chip in enumerate(chips)]
        for cp in first:
            cp.start()
        passed = [copy(3 + j, part(chip, c), part(chip, c), sibling) for j, chip in enumerate(chips)]
        for j, chip in enumerate(chips):
            copy(j, my_half, part(chip, c), (*chip, c)).wait_recv()
            passed[j].start()
        for j, chip in enumerate(chips):
            copy(3 + j, part(chip, 1 - c), part(chip, 1 - c), sibling).wait_recv()
        for cp in first + passed:
            cp.wait_send()

    return pl.pallas_call(
        body, name=name, in_specs=[ANY], out_specs=ANY,
        out_shape=jax.ShapeDtypeStruct((N_CHIPS, rows, cols), slab.dtype),
        scratch_shapes=[pltpu.SemaphoreType.DMA((6,)), pltpu.SemaphoreType.DMA((6,))],
    )(slab)


def _swap_other_half(slabs, *, name):
    n, rows, cols = slabs.shape
    half = rows // 2

    def body(g_ref, out_ref, send_sem, recv_sem):
        x, y, c = _mesh_pos()
        cp = pltpu.make_async_remote_copy(src_ref=g_ref.at[:, pl.ds((1 - c) * half, half), :], dst_ref=out_ref,
                                          send_sem=send_sem, recv_sem=recv_sem, device_id=(x, y, 1 - c), device_id_type=MESH)
        cp.start()
        cp.wait()

    return pl.pallas_call(
        body, name=name, in_specs=[ANY], out_specs=ANY, out_shape=jax.ShapeDtypeStruct((n, half, cols), slabs.dtype),
        scratch_shapes=[pltpu.SemaphoreType.DMA, pltpu.SemaphoreType.DMA],
    )(slabs)


def _add_own_half(slabs, received, core, *, name):
    n, rows, cols = slabs.shape
    half = rows // 2
    nb = half // HALF_TM

    def kern(c_ref, a_ref, b_ref, o_ref):
        o_ref[...] = (a_ref[...] + b_ref[...]).astype(o_ref.dtype)

    return pl.pallas_call(
        kern, name=name,
        grid_spec=pltpu.PrefetchScalarGridSpec(
            num_scalar_prefetch=1, grid=(n, nb),
            in_specs=[pl.BlockSpec((None, HALF_TM, cols), lambda s, i, c: (s, c[0] * nb + i, 0)),
                      pl.BlockSpec((None, HALF_TM, cols), lambda s, i, c: (s, i, 0))],
            out_specs=pl.BlockSpec((None, HALF_TM, cols), lambda s, i, c: (s, i, 0))),
        out_shape=jax.ShapeDtypeStruct((n, half, cols), LINK_DTYPE),
        compiler_params=_params("parallel", "parallel"),
    )(core, slabs, received)


def _scatter_to_chips(sums, *, name):
    n, half, cols = sums.shape

    def body(s_ref, out_ref, send_sems, recv_sems):
        x, y, c = _mesh_pos()
        me = 2 * x + y
        chips = _other_chips(x, y)

        def copy(k, chip):
            return pltpu.make_async_remote_copy(src_ref=s_ref.at[2 * chip[0] + chip[1]], dst_ref=out_ref.at[me],
                                                send_sem=send_sems.at[k], recv_sem=recv_sems.at[k],
                                                device_id=(*chip, c), device_id_type=MESH)

        sends = [copy(j, chip) for j, chip in enumerate(chips)]
        for cp in sends:
            cp.start()
        for j, chip in enumerate(chips):
            pltpu.make_async_remote_copy(src_ref=s_ref.at[me], dst_ref=out_ref.at[2 * chip[0] + chip[1]],
                                         send_sem=send_sems.at[j], recv_sem=recv_sems.at[j],
                                         device_id=(*chip, c), device_id_type=MESH).wait_recv()
        for cp in sends:
            cp.wait_send()

    return pl.pallas_call(
        body, name=name, in_specs=[ANY], out_specs=ANY, out_shape=jax.ShapeDtypeStruct((n, half, cols), sums.dtype),
        scratch_shapes=[pltpu.SemaphoreType.DMA((3,)), pltpu.SemaphoreType.DMA((3,))],
    )(sums)


def _sum_chips(parts, *, name):
    n, half, cols = parts.shape

    def kern(p_ref, o_ref):
        acc = p_ref[0].astype(F32)
        for s in range(1, n):
            acc = acc + p_ref[s].astype(F32)
        o_ref[...] = acc

    return pl.pallas_call(
        kern, name=name, grid=(half // HALF_TM,),
        in_specs=[pl.BlockSpec((n, HALF_TM, cols), lambda i: (0, i, 0))],
        out_specs=pl.BlockSpec((HALF_TM, cols), lambda i: (i, 0)),
        out_shape=jax.ShapeDtypeStruct((half, cols), F32),
        compiler_params=_params("parallel"),
    )(parts)


def _join_halves(mine, *, name):
    half, cols = mine.shape

    def body(m_ref, out_ref, send_sem, recv_sem):
        x, y, c = _mesh_pos()
        cp = pltpu.make_async_remote_copy(src_ref=m_ref, dst_ref=out_ref, send_sem=send_sem, recv_sem=recv_sem,
                                          device_id=(x, y, 1 - c), device_id_type=MESH)
        cp.start()
        cp.wait()

    return pl.pallas_call(
        body, name=name, in_specs=[ANY], out_specs=ANY, out_shape=jax.ShapeDtypeStruct((half, cols), mine.dtype),
        scratch_shapes=[pltpu.SemaphoreType.DMA, pltpu.SemaphoreType.DMA],
    )(mine)


def _all_reduce_small(v, *, name):
    rows, cols = v.shape
    flips = [(dx, dy, dc) for dx in (0, 1) for dy in (0, 1) for dc in (0, 1)][1:]

    def body(v_ref, o_ref, buf, send_sems, recv_sems):
        x, y, c = _mesh_pos()

        def peer(f):
            return tuple(1 - p if fl else p for p, fl in zip((x, y, c), f))

        def index(p):
            return 4 * p[0] + 2 * p[1] + p[2]

        buf[index((x, y, c))] = v_ref[...]
        sends = []
        for k, f in enumerate(flips):
            cp = pltpu.make_async_remote_copy(src_ref=v_ref, dst_ref=buf.at[index((x, y, c))], send_sem=send_sems.at[k],
                                              recv_sem=recv_sems.at[k], device_id=peer(f), device_id_type=MESH)
            cp.start()
            sends.append(cp)
        for k, f in enumerate(flips):
            pltpu.make_async_remote_copy(src_ref=v_ref, dst_ref=buf.at[index(peer(f))], send_sem=send_sems.at[k],
                                         recv_sem=recv_sems.at[k], device_id=peer(f), device_id_type=MESH).wait_recv()
        for cp in sends:
            cp.wait_send()
        acc = buf[0]
        for d in range(1, N_DEV):
            acc = acc + buf[d]
        o_ref[...] = acc

    vmem = pl.BlockSpec(memory_space=pltpu.VMEM)
    return pl.pallas_call(
        body, name=name, in_specs=[vmem], out_specs=vmem, out_shape=jax.ShapeDtypeStruct((rows, cols), F32),
        scratch_shapes=[pltpu.VMEM((N_DEV, rows, cols), F32), pltpu.SemaphoreType.DMA((N_DEV - 1,)),
                        pltpu.SemaphoreType.DMA((N_DEV - 1,))],
    )(v)


def _reduce_scatter(slabs, core):
    x, y, c = _mesh_pos()
    chip = 2 * x + y
    received = _swap_other_half(slabs, name="rs_swap")
    sums = _add_own_half(slabs, received, core, name="rs_add")
    parts = _scatter_to_chips(sums, name="rs_scatter")
    parts = lax.dynamic_update_slice(parts, lax.dynamic_slice_in_dim(sums, chip, 1, axis=0), (chip, 0, 0))
    mine = _sum_chips(parts, name="rs_sum")
    other = _join_halves(mine, name="rs_join")
    return jnp.concatenate([jnp.where(c == 0, mine, other), jnp.where(c == 0, other, mine)], axis=0)


def _adamw(w, g, m, v, *, name):
    def body(w, g, m, v):
        m = ADAM_B1 * m + (1.0 - ADAM_B1) * g
        v = ADAM_B2 * v + (1.0 - ADAM_B2) * (g * g)
        m_hat = m / (1.0 - ADAM_B1 ** ADAM_STEP)
        v_hat = v / (1.0 - ADAM_B2 ** ADAM_STEP)
        delta = -ADAM_LR * (m_hat / (jnp.sqrt(v_hat) + ADAM_EPS) + ADAM_WD * w)
        return (delta, m, v), ()
    rows, cols = w.shape
    return _rowwise(body, [w, g, m, v], [], [(cols, F32)] * 3, [], tm=_tile(rows, 256, SUBLANES), name=name)


def _pack_small(arrs, rows):
    flat = jnp.concatenate([a.reshape(-1) for a in arrs])
    return jnp.pad(flat, (0, rows * LANES - flat.shape[0])).reshape(rows, LANES)


def _unpack_small(packed, shapes):
    flat = packed.reshape(-1)
    out, off = [], 0
    for s in shapes:
        size = math.prod(s)
        out.append(flat[off: off + size].reshape(s))
        off += size
    return out


def _small_rows(shapes):
    return -(-sum(math.prod(s) for s in shapes) // (SUBLANES * LANES)) * SUBLANES


WEIGHTS = ("rel_bias", "att_w_in", "att_w_out", "dn_w_in", "dn_conv", "dn_a_log", "dn_dt_bias", "dn_out_norm", "dn_w_out",
           "mem_norm", "mem_w_kv", "norm_mix_pre", "norm_mix_post", "norm_ffn_pre", "norm_ffn_post", "ffn_w_gate_up",
           "ffn_w_down")
BIG_NAMES = tuple(n for n, _, _ in BIG)
SMALL_NAMES = tuple(n for n in WEIGHTS if n not in BIG_NAMES)
CONV_COLS = 3 * TOK_WIDTH
CONV_SHARD = CONV_COLS // N_CHIPS


def kernel(x, mem, rel_bias, att_w_in, att_w_out, dn_w_in, dn_conv, dn_a_log, dn_dt_bias, dn_out_norm, dn_w_out, mem_norm, mem_w_kv, norm_mix_pre, norm_mix_post, norm_ffn_pre, norm_ffn_post, ffn_w_gate_up, ffn_w_down, loss_target, m_rel_bias, m_att_w_in, m_att_w_out, m_dn_w_in, m_dn_conv, m_dn_a_log, m_dn_dt_bias, m_dn_out_norm, m_dn_w_out, m_mem_norm, m_mem_w_kv, m_norm_mix_pre, m_norm_mix_post, m_norm_ffn_pre, m_norm_ffn_post, m_ffn_w_gate_up, m_ffn_w_down, v_rel_bias, v_att_w_in, v_att_w_out, v_dn_w_in, v_dn_conv, v_dn_a_log, v_dn_dt_bias, v_dn_out_norm, v_dn_w_out, v_mem_norm, v_mem_w_kv, v_norm_mix_pre, v_norm_mix_post, v_norm_ffn_pre, v_norm_ffn_post, v_ffn_w_gate_up, v_ffn_w_down):
    w = dict(zip(WEIGHTS, (rel_bias, att_w_in, att_w_out, dn_w_in, dn_conv, dn_a_log, dn_dt_bias, dn_out_norm, dn_w_out,
                           mem_norm, mem_w_kv, norm_mix_pre, norm_mix_post, norm_ffn_pre, norm_ffn_post, ffn_w_gate_up,
                           ffn_w_down)))
    m = dict(zip(WEIGHTS, (m_rel_bias, m_att_w_in, m_att_w_out, m_dn_w_in, m_dn_conv, m_dn_a_log, m_dn_dt_bias,
                           m_dn_out_norm, m_dn_w_out, m_mem_norm, m_mem_w_kv, m_norm_mix_pre, m_norm_mix_post,
                           m_norm_ffn_pre, m_norm_ffn_post, m_ffn_w_gate_up, m_ffn_w_down)))
    v = dict(zip(WEIGHTS, (v_rel_bias, v_att_w_in, v_att_w_out, v_dn_w_in, v_dn_conv, v_dn_a_log, v_dn_dt_bias,
                           v_dn_out_norm, v_dn_w_out, v_mem_norm, v_mem_w_kv, v_norm_mix_pre, v_norm_mix_post,
                           v_norm_ffn_pre, v_norm_ffn_post, v_ffn_w_gate_up, v_ffn_w_down)))
    cx, cy, cc = _mesh_pos()
    chip = 2 * cx + cy

    slab = lax.optimization_barrier(_pack_local({n: w[n] for n in BIG_NAMES}).astype(MXU_DTYPE))
    gathered = lax.dynamic_update_slice(_gather_shards(slab, name="gather_weights"), slab[None], (chip, 0, 0))
    full = _full_from_gathered(gathered)
    conv_rows = _small_rows([(DN_CONV, CONV_COLS)])
    conv_mine = jnp.where(cc == 0, 1.0, 0.0) * w["dn_conv"][0]
    conv_placed = lax.dynamic_update_slice(jnp.zeros((DN_CONV, CONV_COLS), F32), conv_mine, (0, chip * CONV_SHARD))
    conv_full = _unpack_small(_all_reduce_small(_pack_small([conv_placed], conv_rows), name="gather_conv"),
                              [(DN_CONV, CONV_COLS)])[0]
    p = {
        "rel_bias": w["rel_bias"], "att_w_in": full["att_w_in"][0], "att_w_out": full["att_w_out"][0],
        "dn_w_in": _pad_dn_w_in(full["dn_w_in"][0]), "dn_conv": conv_full, "dn_a_log": w["dn_a_log"][0],
        "dn_dt_bias": w["dn_dt_bias"][0], "dn_out_norm": w["dn_out_norm"][0], "dn_w_out": full["dn_w_out"][0],
        "mem_norm": w["mem_norm"], "mem_w_kv": full["mem_w_kv"], "norm_mix_pre": w["norm_mix_pre"],
        "norm_mix_post": w["norm_mix_post"], "norm_ffn_pre": w["norm_ffn_pre"], "norm_ffn_post": w["norm_ffn_post"],
        "ffn_w_gate_up": full["ffn_w_gate_up"], "ffn_w_down": full["ffn_w_down"],
    }

    loss_cols, grad_x, g = _local_step(x[0], mem[0], loss_target[0], p)
    loss = lax.psum(jnp.sum(loss_cols), ("x", "y", "c"))

    g_big = {"att_w_in": [g["att_w_in"]], "att_w_out": [g["att_w_out"]], "dn_w_in": [_unpad_dn_w_in(g["dn_w_in"])],
             "dn_w_out": [g["dn_w_out"]], "mem_w_kv": g["mem_w_kv"], "ffn_w_gate_up": g["ffn_w_gate_up"],
             "ffn_w_down": g["ffn_w_down"]}
    reduced = _reduce_scatter(_pack_full(g_big), cc.astype(jnp.int32).reshape(1))
    grads = _unpack_local(reduced)
    small_full_shapes = [(DN_CONV, CONV_COLS) if n == "dn_conv" else w[n].shape for n in SMALL_NAMES]
    small_sum = _all_reduce_small(_pack_small([g[n] for n in SMALL_NAMES], _small_rows(small_full_shapes)), name="reduce_small")
    for n, s in zip(SMALL_NAMES, _unpack_small(small_sum, small_full_shapes)):
        grads[n] = lax.dynamic_slice(s, (0, chip * CONV_SHARD), (DN_CONV, CONV_SHARD))[None] if n == "dn_conv" else s

    delta, new_m, new_v = {}, {}, {}
    for n in BIG_NAMES:
        shape = w[n].shape
        two_d = lambda a: a.reshape(-1, shape[-1])
        res = _adamw(two_d(w[n]), two_d(grads[n]), two_d(m[n]), two_d(v[n]), name=f"adamw_{n}")
        delta[n], new_m[n], new_v[n] = (r.reshape(shape) for r in res)
    small_shapes = [w[n].shape for n in SMALL_NAMES]
    rows = _small_rows(small_shapes)
    res = _adamw(*[_pack_small([d[n] for n in SMALL_NAMES], rows) for d in (w, grads, m, v)], name="adamw_small")
    for d, r in zip((delta, new_m, new_v), res):
        for n, a in zip(SMALL_NAMES, _unpack_small(r, small_shapes)):
            d[n] = a
    return (loss, grad_x[None], *[grads[n] for n in WEIGHTS], *[delta[n] for n in WEIGHTS],
            *[new_m[n] for n in WEIGHTS], *[new_v[n] for n in WEIGHTS])
```

```python
import functools
import math

import numpy as np
import jax
import jax.numpy as jnp
from jax import lax
from jax.experimental import pallas as pl
from jax.experimental.pallas import tpu as pltpu

F32 = jnp.float32
MXU_DTYPE = jnp.bfloat16
LINK_DTYPE = jnp.bfloat16
HI = lax.Precision.HIGHEST

EPS = 1e-6
NEG_INF = -1e30
LANES = 128
SUBLANES = 8
VMEM_LIMIT = 56 * 1024 * 1024

D_MODEL = 1024
TOK_WIDTH = 768
MEM_WIDTH = 256
MEM_LEN = 256
ATT_HEAD_DIM = 64
DILATIONS = (1, 4, 16)
HALF = 64
ATT_BQ = 128
ATT_W = ATT_BQ + 2 * HALF
REL_BUCKETS = 32
REL_MAX_DIST = 1024
DN_HEADS = 6
DN_HEAD_DIM = 128
DN_CONV = 5
DN_CHUNK = 128
D_FF = 2816
ATT_IN = 2560
DN_IN = 3352
DN_IN_PAD = 3456
N_GATES = 4 * DN_HEADS

ADAM_LR = 0.001
ADAM_B1 = 0.9
ADAM_B2 = 0.999
ADAM_EPS = 1e-08
ADAM_WD = 0.01
ADAM_STEP = 10


def _tile(n, target, align):
    if n <= target:
        return n
    t = (target // align) * align
    while t >= align:
        if n % t == 0:
            return t
        t -= align
    raise ValueError(f"no tile for {n} (target {target}, align {align})")


def _params(*sem):
    return pltpu.CompilerParams(dimension_semantics=sem, vmem_limit_bytes=VMEM_LIMIT)


def _mm(a, b, *, name, ta=False, tb=False, out_shards=None, tm=1024, tn=1408, tk=1408, out_dtype=F32):
    if ta:
        K, M = a.shape
    else:
        M, K = a.shape
    sharded_b = b.ndim == 3
    if sharded_b:
        n_sh, b_rows, b_cols = b.shape
        N, K2 = (b_rows, n_sh * b_cols) if tb else (n_sh * b_cols, b_rows)
    else:
        N, K2 = b.shape if tb else b.shape[::-1]
    assert K == K2, (a.shape, b.shape, ta, tb)
    tm = _tile(M, tm, LANES if ta else SUBLANES)
    tn = N // out_shards if out_shards else (b_cols if sharded_b and not tb else _tile(N, tn, LANES))
    tk = b_cols if sharded_b and tb else _tile(K, tk, LANES)
    nk = K // tk
    a_spec = pl.BlockSpec((tk, tm), lambda i, j, k: (k, i)) if ta else pl.BlockSpec((tm, tk), lambda i, j, k: (i, k))
    if sharded_b:
        b_spec = (pl.BlockSpec((None, tn, tk), lambda i, j, k: (k, j, 0)) if tb
                  else pl.BlockSpec((None, tk, tn), lambda i, j, k: (j, k, 0)))
    else:
        b_spec = pl.BlockSpec((tn, tk), lambda i, j, k: (j, k)) if tb else pl.BlockSpec((tk, tn), lambda i, j, k: (k, j))
    if out_shards:
        out_spec = pl.BlockSpec((None, tm, tn), lambda i, j, k: (j, i, 0))
        out_shape = jax.ShapeDtypeStruct((out_shards, M, tn), out_dtype)
    else:
        out_spec = pl.BlockSpec((tm, tn), lambda i, j, k: (i, j))
        out_shape = jax.ShapeDtypeStruct((M, N), out_dtype)
    dims = (((0 if ta else 1,), (1 if tb else 0,)), ((), ()))

    def kern(a_ref, b_ref, o_ref, acc_ref):
        k = pl.program_id(2)

        @pl.when(k == 0)
        def _():
            acc_ref[...] = jnp.zeros_like(acc_ref)

        acc_ref[...] += lax.dot_general(a_ref[...].astype(MXU_DTYPE), b_ref[...].astype(MXU_DTYPE), dims,
                                        preferred_element_type=F32)

        @pl.when(k == nk - 1)
        def _():
            o_ref[...] = acc_ref[...].astype(o_ref.dtype)

    return pl.pallas_call(
        kern, name=name, grid=(M // tm, N // tn, nk), in_specs=[a_spec, b_spec],
        out_specs=out_spec, out_shape=out_shape,
        scratch_shapes=[pltpu.VMEM((tm, tn), F32)],
        compiler_params=_params("parallel", "parallel", "arbitrary"),
    )(a, b)


def _shards_of(w):
    return w.shape[0] if w.ndim == 3 else None


def _col(arr, width, blk):
    return (arr, width, blk)


def _rowwise(body, rows, consts, out_rows, out_acc, *, tm, name):
    n_rows = (rows[0][0] if isinstance(rows[0], tuple) else rows[0]).shape[0]
    assert n_rows % tm == 0, (n_rows, tm)
    arrs, in_specs = [], []
    for r in rows:
        arr, width, blk = r if isinstance(r, tuple) else (r, r.shape[1], 0)
        assert arr.shape[0] == n_rows
        arrs.append(arr)
        in_specs.append(pl.BlockSpec((tm, width), functools.partial(lambda i, b: (i, b), b=blk)))
    for c in consts:
        arrs.append(c)
        in_specs.append(pl.BlockSpec(c.shape, functools.partial(lambda i, n: (0,) * n, n=c.ndim)))
    n_in, n_ro = len(arrs), len(out_rows)
    out_shape = [jax.ShapeDtypeStruct((n_rows, w), dt) for w, dt in out_rows]
    out_specs = [pl.BlockSpec((tm, w), lambda i: (i, 0)) for w, _ in out_rows]
    out_shape += [jax.ShapeDtypeStruct(s, F32) for s in out_acc]
    out_specs += [pl.BlockSpec(s, lambda i: (0, 0)) for s in out_acc]

    def kern(*refs):
        ro, ao = body(*[r[...] for r in refs[:n_in]])
        outs = refs[n_in:]
        for r, v in zip(outs[:n_ro], ro, strict=True):
            r[...] = v.astype(r.dtype)
        if out_acc:
            @pl.when(pl.program_id(0) == 0)
            def _():
                for r in outs[n_ro:]:
                    r[...] = jnp.zeros_like(r)

            for r, v in zip(outs[n_ro:], ao, strict=True):
                r[...] += v

    res = pl.pallas_call(
        kern, name=name, grid=(n_rows // tm,), in_specs=in_specs, out_specs=out_specs, out_shape=out_shape,
        compiler_params=_params("arbitrary" if out_acc else "parallel"),
    )(*arrs)
    return res


def _rms(x, gain):
    return x * lax.rsqrt(jnp.mean(x * x, axis=-1, keepdims=True) + EPS) * gain


def _silu(x):
    return x * jax.nn.sigmoid(x)


def _softplus(x):
    return jnp.maximum(x, 0.0) + jnp.log(1.0 + jnp.exp(-jnp.abs(x)))


def _dot_nt(a, b, precision=None):
    return lax.dot_general(a, b, (((1,), (1,)), ((), ())), preferred_element_type=F32, precision=precision)


def _dot_tn(a, b, precision=None):
    return lax.dot_general(a, b, (((0,), (0,)), ((), ())), preferred_element_type=F32, precision=precision)


def _dot(a, b, precision=None):
    return jnp.dot(a, b, preferred_element_type=F32, precision=precision)


def _pre_norm(x, gain, *, name):
    def body(x, g):
        return (_rms(x, g),), ()
    return _rowwise(body, [x], [gain], [(x.shape[1], MXU_DTYPE)], [], tm=_tile(x.shape[0], 512, 2 * SUBLANES), name=name)[0]


def _pre_norm_bwd(x, gain, dh, dx_other, *, name):
    def body(x, dh, dxo, g):
        _, vjp = jax.vjp(_rms, x, g)
        dx, dg = vjp(dh)
        return (dx + dxo,), (dg,)
    return _rowwise(body, [x, dh, dx_other], [gain], [(x.shape[1], F32)], [gain.shape], tm=512, name=name)


def _gain_bwd(x, gain, dh, *, name):
    def body(x, dh, g):
        _, vjp = jax.vjp(lambda g_: _rms(x, g_), g)
        return (), (vjp(dh)[0],)
    return _rowwise(body, [x, dh], [gain], [], [gain.shape], tm=_tile(x.shape[0], 512, SUBLANES), name=name)[0]


def _res_block(x_res, m, g_post, g_pre):
    x_new = x_res + _rms(m, g_post)
    return x_new, _rms(x_new, g_pre)


def _post_pre(x_res, m, g_post, g_pre, *, name):
    def body(x, m, gp, gq):
        return _res_block(x, m, gp, gq), ()
    d = x_res.shape[1]
    return _rowwise(body, [x_res, m], [g_post, g_pre], [(d, F32), (d, MXU_DTYPE)], [], tm=512, name=name)


def _post_pre_bwd(x_res, m, g_post, g_pre, dx_new, dh, *, name):
    def body(x, m, dxn, dh, gp, gq):
        _, vjp = jax.vjp(_res_block, x, m, gp, gq)
        dx, dm, dgp, dgq = vjp((dxn, dh))
        return (dx, dm), (dgp, dgq)
    d = x_res.shape[1]
    return _rowwise(body, [x_res, m, dx_new, dh], [g_post, g_pre], [(d, F32), (d, MXU_DTYPE)],
                    [g_post.shape, g_pre.shape], tm=256, name=name)


def _final_loss_bwd(x_res, m, g_post, target, *, name):
    d = x_res.shape[1]

    def loss_cols(x, m, g, t):
        err = x + _rms(m, g) - t
        return jnp.sum(err * err, axis=0, keepdims=True) * (0.5 / d)

    def body(x, m, t, g):
        cols, vjp = jax.vjp(lambda x_, m_, g_: loss_cols(x_, m_, g_, t), x, m, g)
        dx, dm, dg = vjp(jnp.ones_like(cols))
        return (dx, dm), (dg, cols)
    return _rowwise(body, [x_res, m, target], [g_post], [(d, F32), (d, MXU_DTYPE)], [g_post.shape, (1, d)], tm=256, name=name)


def _swiglu_act(gu, *, name):
    def body(gate, up):
        return (_silu(gate) * up,), ()
    return _rowwise(body, [_col(gu, D_FF, 0), _col(gu, D_FF, 1)], [], [(D_FF, MXU_DTYPE)], [], tm=256, name=name)[0]


def _swiglu_act_bwd(gu, da, *, name):
    def body(gate, up, da):
        _, vjp = jax.vjp(lambda g, u: _silu(g) * u, gate, up)
        dg, du = vjp(da)
        return (jnp.concatenate([dg, du], axis=1),), ()
    return _rowwise(body, [_col(gu, D_FF, 0), _col(gu, D_FF, 1), da], [], [(2 * D_FF, MXU_DTYPE)], [], tm=256, name=name)[0]


def _lane_head_mask(width, head_dim, head):
    lane = lax.broadcasted_iota(jnp.int32, (1, width), 1)
    return (lane // head_dim) == head


def _mem_attn_pair(q_pair, k_pair, v_pair):
    out = jnp.zeros_like(q_pair)
    for h in range(2):
        mh = _lane_head_mask(LANES, ATT_HEAD_DIM, h)
        qh = jnp.where(mh, q_pair * (ATT_HEAD_DIM ** -0.5), 0.0)
        logits = _dot_nt(qh, k_pair)
        mx = jnp.max(logits, axis=-1, keepdims=True)
        p = jnp.exp(logits - mx)
        p = p / jnp.sum(p, axis=-1, keepdims=True)
        out = out + jnp.where(mh, _dot(p, v_pair), 0.0)
    return out


def _mem_attn(q_mem, kv):
    outs = []
    for p in range(MEM_WIDTH // LANES):
        sl = slice(p * LANES, (p + 1) * LANES)
        outs.append(_mem_attn_pair(q_mem[:, sl], kv[:, sl], kv[:, MEM_WIDTH + p * LANES: MEM_WIDTH + (p + 1) * LANES]))
    return jnp.concatenate(outs, axis=1)


def _mem_attn_bwd(q_mem, kv, do):
    dqs, dks, dvs = [], [], []
    for p in range(MEM_WIDTH // LANES):
        sl = slice(p * LANES, (p + 1) * LANES)
        sv = slice(MEM_WIDTH + p * LANES, MEM_WIDTH + (p + 1) * LANES)
        _, vjp = jax.vjp(_mem_attn_pair, q_mem[:, sl], kv[:, sl], kv[:, sv])
        dq, dk, dv = vjp(do[:, sl])
        dqs.append(dq)
        dks.append(dk)
        dvs.append(dv)
    return jnp.concatenate(dqs, axis=1), jnp.concatenate(dks + dvs, axis=1)


def _t5_bucket(rel):
    half = REL_BUCKETS // 2
    max_exact = half // 2
    n = np.abs(rel)
    large = max_exact + (np.log(np.maximum(n, 1) / max_exact) / math.log(REL_MAX_DIST / max_exact)
                         * (half - max_exact)).astype(np.int64)
    large = np.minimum(large, half - 1)
    return ((rel > 0) * half + np.where(n < max_exact, n, large)).astype(np.int32)


ATT_DIAGS = ATT_BQ + ATT_W - 1


def _bias_diag_onehot(dil):
    j = np.arange(ATT_DIAGS)
    tiles = []
    for off in (-HALF, 0, HALF):
        rel = j - (ATT_BQ - 1) - HALF - off
        hot = _t5_bucket(rel * dil)[:, None] == np.arange(REL_BUCKETS)[None, :]
        tiles.append(hot & (np.abs(rel) <= HALF)[:, None])
    return np.stack(tiles).astype(np.float32)


def _toeplitz(r):
    lead = r.shape[:-1]
    a = jnp.broadcast_to(r[..., None, :], lead + (ATT_BQ, ATT_DIAGS))
    a = jnp.pad(a, [(0, 0)] * len(lead) + [(0, 0), (0, 1)])
    a = a.reshape(lead + (ATT_BQ * (ATT_DIAGS + 1),))[..., : ATT_BQ * ATT_DIAGS].reshape(lead + (ATT_BQ, ATT_DIAGS))
    return a[..., ATT_BQ - 1: ATT_BQ - 1 + ATT_W]


def _bias_tiles(rel_bias, gi):
    heads = rel_bias[:, 4 * gi: 4 * gi + 4]
    diag = jnp.einsum('tnb,bh->thn', jnp.asarray(_bias_diag_onehot(DILATIONS[gi])), heads, precision=HI)
    return _toeplitz(diag)


def _bias_tiles_bwd(rel_bias, dtiles, gi):
    return jax.vjp(lambda rb: _bias_tiles(rb, gi), rel_bias)[1](dtiles)[0]


def _att_window(i, n_sub):
    start = jnp.clip(i * ATT_BQ - HALF, 0, n_sub - ATT_W)
    off = i * ATT_BQ - HALF - start
    return pl.multiple_of(start, HALF), off


def _att_valid(off):
    q = lax.broadcasted_iota(jnp.int32, (ATT_BQ, ATT_W), 0)
    kk = lax.broadcasted_iota(jnp.int32, (ATT_BQ, ATT_W), 1)
    return jnp.abs(kk - q - HALF - off) <= HALF


def _att_tile_id(i, nq):
    return jnp.where(i == 0, 0, jnp.where(i == nq - 1, 2, 1))


def _att_fwd(qkvm, bias, gi, *, name):
    dil = DILATIONS[gi]
    s_len = qkvm.shape[0]
    n_sub = s_len // dil
    nq = n_sub // ATT_BQ
    assert n_sub % ATT_BQ == 0 and n_sub >= ATT_W
    cols = qkvm.shape[1] // LANES
    view = qkvm.reshape(n_sub, dil * qkvm.shape[1])

    def kern(q_ref, k_ref, v_ref, b_ref, o_ref, lse_ref):
        i = pl.program_id(2)
        start, off = _att_window(i, n_sub)
        valid = _att_valid(off)
        q = q_ref[...] * (ATT_HEAD_DIM ** -0.5)
        kw = k_ref[pl.ds(start, ATT_W), :]
        vw = v_ref[pl.ds(start, ATT_W), :]
        o = jnp.zeros((ATT_BQ, LANES), F32)
        lse = jnp.zeros((ATT_BQ, LANES), F32)
        for h in range(2):
            mh = _lane_head_mask(LANES, ATT_HEAD_DIM, h)
            s = _dot_nt(jnp.where(mh, q, 0.0), kw) + b_ref[h]
            s = jnp.where(valid, s, NEG_INF)
            mx = jnp.max(s, axis=-1, keepdims=True)
            p = jnp.exp(s - mx)
            den = jnp.sum(p, axis=-1, keepdims=True)
            o = jnp.where(mh, _dot(p, vw) / den, o)
            lse = jnp.where(mh, mx + jnp.log(den), lse)
        o_ref[...] = o
        lse_ref[...] = lse

    def qkv_spec(which, full):
        shape = (n_sub, LANES) if full else (ATT_BQ, LANES)
        return pl.BlockSpec(shape, lambda pr, r, i: (0 if full else i, r * cols + which * 6 + 2 * gi + pr))

    out_spec = pl.BlockSpec((ATT_BQ, LANES), lambda pr, r, i: (i, r * 2 + pr))
    o, lse = pl.pallas_call(
        kern, name=name, grid=(2, dil, nq),
        in_specs=[qkv_spec(0, False), qkv_spec(1, True), qkv_spec(2, True),
                  pl.BlockSpec((None, 2, ATT_BQ, ATT_W), lambda pr, r, i: (_att_tile_id(i, nq), pr, 0, 0))],
        out_specs=[out_spec, out_spec],
        out_shape=[jax.ShapeDtypeStruct((n_sub, dil * 2 * LANES), F32)] * 2,
        compiler_params=_params("parallel", "parallel", "arbitrary"),
    )(view, view, view, bias)
    return o.reshape(s_len, 2 * LANES), lse.reshape(s_len, 2 * LANES)


def _att_bwd(qkvm, bias, lse_tot, delta, dcat, gi, *, name):
    dil = DILATIONS[gi]
    s_len = qkvm.shape[0]
    n_sub = s_len // dil
    nq = n_sub // ATT_BQ
    cols = qkvm.shape[1] // LANES
    dcols = dcat.shape[1] // LANES
    view = qkvm.reshape(n_sub, dil * qkvm.shape[1])
    lse_v = lse_tot.reshape(n_sub, dil * 2 * LANES)
    delta_v = delta.reshape(n_sub, dil * 2 * LANES)
    dcat_v = dcat.reshape(n_sub, dil * dcat.shape[1])

    def kern(q_ref, k_ref, v_ref, b_ref, lse_ref, dl_ref, dm_ref, dq_ref, dk_ref, dv_ref, db_ref):
        r, i = pl.program_id(1), pl.program_id(2)
        start, off = _att_window(i, n_sub)
        valid = _att_valid(off)
        tile = _att_tile_id(i, nq)

        @pl.when(i == 0)
        def _():
            dk_ref[...] = jnp.zeros_like(dk_ref)
            dv_ref[...] = jnp.zeros_like(dv_ref)

        @pl.when((i == 0) & (r == 0))
        def _():
            db_ref[...] = jnp.zeros_like(db_ref)

        q = q_ref[...] * (ATT_HEAD_DIM ** -0.5)
        kw = k_ref[pl.ds(start, ATT_W), :]
        vw = v_ref[pl.ds(start, ATT_W), :]
        dm = dm_ref[...]
        lse = lse_ref[...]
        dl = dl_ref[...]
        dq = jnp.zeros((ATT_BQ, LANES), F32)
        dkw = jnp.zeros((ATT_W, LANES), F32)
        dvw = jnp.zeros((ATT_W, LANES), F32)
        for h in range(2):
            mh = _lane_head_mask(LANES, ATT_HEAD_DIM, h)
            qh = jnp.where(mh, q, 0.0)
            dmh = jnp.where(mh, dm, 0.0)
            s = _dot_nt(qh, kw) + b_ref[tile, h]
            s = jnp.where(valid, s, NEG_INF)
            lse_h = jnp.max(jnp.where(mh, lse, NEG_INF), axis=-1, keepdims=True)
            dl_h = jnp.max(jnp.where(mh, dl, NEG_INF), axis=-1, keepdims=True)
            p = jnp.exp(s - lse_h)
            ds = p * (_dot_nt(dmh, vw) - dl_h)
            dq = dq + jnp.where(mh, _dot(ds, kw), 0.0)
            dkw = dkw + _dot_tn(ds, qh)
            dvw = dvw + _dot_tn(p, dmh)
            db_ref[tile, h] += ds
        dq_ref[...] = dq * (ATT_HEAD_DIM ** -0.5)
        dk_ref[pl.ds(start, ATT_W), :] += dkw
        dv_ref[pl.ds(start, ATT_W), :] += dvw

    def qkv_spec(which, full):
        shape = (n_sub, LANES) if full else (ATT_BQ, LANES)
        return pl.BlockSpec(shape, lambda pr, r, i: (0 if full else i, r * cols + which * 6 + 2 * gi + pr))

    blk = pl.BlockSpec((ATT_BQ, LANES), lambda pr, r, i: (i, r * 2 + pr))
    full = pl.BlockSpec((n_sub, LANES), lambda pr, r, i: (0, r * 2 + pr))
    bias_spec = pl.BlockSpec((3, 2, ATT_BQ, ATT_W), lambda pr, r, i: (0, pr, 0, 0))
    sub = jax.ShapeDtypeStruct((n_sub, dil * 2 * LANES), F32)
    dq, dk, dv, db = pl.pallas_call(
        kern, name=name, grid=(2, dil, nq),
        in_specs=[qkv_spec(0, False), qkv_spec(1, True), qkv_spec(2, True), bias_spec, blk, blk,
                  pl.BlockSpec((ATT_BQ, LANES), lambda pr, r, i: (i, r * dcols + 2 * gi + pr))],
        out_specs=[blk, full, full, bias_spec],
        out_shape=[sub, sub, sub, jax.ShapeDtypeStruct(bias.shape, F32)],
        compiler_params=_params("arbitrary", "arbitrary", "arbitrary"),
    )(view, view, view, bias, lse_v, delta_v, dcat_v)
    return dq.reshape(s_len, -1), dk.reshape(s_len, -1), dv.reshape(s_len, -1), db


def _att_combine(o_g, lse_g, qkvm, kv_mem, *, name):
    def body(o0, o1, o2, l0, l1, l2, qm, kv):
        mx = jnp.maximum(jnp.maximum(l0, l1), l2)
        tot = mx + jnp.log(jnp.exp(l0 - mx) + jnp.exp(l1 - mx) + jnp.exp(l2 - mx))
        mixed = [o * jnp.exp(l - tot) for o, l in ((o0, l0), (o1, l1), (o2, l2))]
        return (jnp.concatenate(mixed + [_mem_attn(qm, kv)], axis=1), tot), ()
    return _rowwise(body, list(o_g) + list(lse_g) + [_col(qkvm, MEM_WIDTH, (3 * TOK_WIDTH) // MEM_WIDTH)], [kv_mem],
                    [(D_MODEL, F32), (MEM_WIDTH, F32)], [], tm=256, name=name)


def _head_sum_matrix():
    a = np.arange(MEM_WIDTH)
    return jnp.asarray((a[:, None] // ATT_HEAD_DIM == a[None, :] // ATT_HEAD_DIM).astype(np.float32))


def _att_bwd_prep(cat, dcat, qkvm, kv_mem, *, name):
    def body(cat, dcat, qm, kv, hs):
        prod = cat * dcat
        summed = prod[:, 0:256] + prod[:, 256:512] + prod[:, 512:768]
        delta = _dot(summed, hs, precision=HI)
        dqm, dkv = _mem_attn_bwd(qm, kv, dcat[:, TOK_WIDTH:])
        return (delta, dqm), (dkv,)
    return _rowwise(body, [cat, dcat, _col(qkvm, MEM_WIDTH, (3 * TOK_WIDTH) // MEM_WIDTH)], [kv_mem, _head_sum_matrix()],
                    [(MEM_WIDTH, F32), (MEM_WIDTH, F32)], [kv_mem.shape], tm=256, name=name)


def _dn_conv_post(s, j):
    scale = jnp.where(j < DN_HEADS, DN_HEAD_DIM ** -0.5, 1.0)
    normed = s * lax.rsqrt(jnp.sum(s * s, axis=-1, keepdims=True) + EPS) * scale
    return jnp.where(j >= 2 * DN_HEADS, s, normed)


def _shift_rows(x, sh):
    n = x.shape[0]
    row = lax.broadcasted_iota(jnp.int32, (n, 1), 0)
    rolled = pltpu.roll(x, (-sh) % n, 0)
    return jnp.where((row + sh >= 0) & (row + sh < n), rolled, 0.0)


def _dn_conv_taps(x, w_ref):
    c = x * w_ref[pl.ds(DN_CONV // 2, 1), :]
    for jj in range(DN_CONV):
        if jj != DN_CONV // 2:
            c = c + _shift_rows(x, jj - DN_CONV // 2) * w_ref[pl.ds(jj, 1), :]
    return c


def _dn_conv_fwd(proj, conv_w, *, name):
    s_len = proj.shape[0]
    width = 3 * TOK_WIDTH

    def kern(x_ref, w_ref, o_ref):
        j = pl.program_id(0)
        o_ref[...] = _dn_conv_post(_silu(_dn_conv_taps(x_ref[...], w_ref)), j)

    return pl.pallas_call(
        kern, name=name, grid=(width // LANES,),
        in_specs=[pl.BlockSpec((s_len, LANES), lambda j: (0, j)), pl.BlockSpec((DN_CONV, LANES), lambda j: (0, j))],
        out_specs=pl.BlockSpec((s_len, LANES), lambda j: (0, j)),
        out_shape=jax.ShapeDtypeStruct((s_len, width), F32),
        compiler_params=_params("parallel"),
    )(proj, conv_w)


def _dn_conv_bwd(proj, conv_w, d_fwd, d_bwd, which, *, name):
    s_len = proj.shape[0]

    def kern(x_ref, w_ref, df_ref, db_ref, dx_ref, dw_ref):
        j = pl.program_id(0) + which * DN_HEADS
        x = x_ref[...]
        c = _dn_conv_taps(x, w_ref)
        _, vjp = jax.vjp(lambda c_: _dn_conv_post(_silu(c_), j), c)
        dc = vjp(df_ref[...] + db_ref[...])[0]
        dx = dc * w_ref[pl.ds(DN_CONV // 2, 1), :]
        for jj in range(DN_CONV):
            sh = jj - DN_CONV // 2
            if sh != 0:
                dx = dx + _shift_rows(dc, -sh) * w_ref[pl.ds(jj, 1), :]
            dw_ref[pl.ds(jj, 1), :] = jnp.sum(dc * _shift_rows(x, sh), axis=0, keepdims=True)
        dx_ref[...] = dx

    return pl.pallas_call(
        kern, name=name, grid=(DN_HEADS,),
        in_specs=[pl.BlockSpec((s_len, LANES), lambda j: (0, j + which * DN_HEADS)),
                  pl.BlockSpec((DN_CONV, LANES), lambda j: (0, j + which * DN_HEADS)),
                  pl.BlockSpec((s_len, LANES), lambda j: (0, j)),
                  pl.BlockSpec((s_len, LANES), lambda j: (0, j))],
        out_specs=[pl.BlockSpec((s_len, LANES), lambda j: (0, j)), pl.BlockSpec((DN_CONV, LANES), lambda j: (0, j))],
        out_shape=[jax.ShapeDtypeStruct((s_len, TOK_WIDTH), F32), jax.ShapeDtypeStruct((DN_CONV, TOK_WIDTH), F32)],
        compiler_params=_params("parallel"),
    )(proj, conv_w, d_fwd, d_bwd)


GATE_TM = 2 * DN_CHUNK


FWD_GATE_LANES = 2 * DN_HEADS


def _gate_constants():
    i = np.arange(GATE_TM)
    same = (i[:, None] // DN_CHUNK) == (i[None, :] // DN_CHUNK)
    cum_f = same & (i[None, :] <= i[:, None])
    cum_r = same & (i[None, :] >= i[:, None])
    return tuple(jnp.asarray(np.asarray(a, np.float32)) for a in (cum_f, cum_r, same))


def _gate_params(p):
    z = jnp.zeros((DN_HEADS,), F32)
    return jnp.concatenate([p[0], z, p[1], z, jnp.zeros((LANES - N_GATES,), F32)]).reshape(1, LANES)


def _gate_params_bwd(dp):
    return jnp.stack([dp[0, 0:DN_HEADS], dp[0, 2 * DN_HEADS: 3 * DN_HEADS]])


def _dn_gates(gate_in, a_cols, dt_cols, cum_f, cum_r, tot):
    g = -jnp.exp(a_cols) * _softplus(gate_in + dt_cols)
    fwd_lane = lax.broadcasted_iota(jnp.int32, (1, LANES), 1) < FWD_GATE_LANES
    gc = jnp.where(fwd_lane, _dot(cum_f, g, precision=HI), _dot(cum_r, g, precision=HI))
    return gc, _dot(tot, g, precision=HI), jax.nn.sigmoid(gate_in)


def _dn_gates_fwd(proj, a_cols, dt_cols, *, name):
    def body(gi, *consts):
        return _dn_gates(gi, *consts), ()
    return _rowwise(body, [_col(proj, LANES, DN_IN_PAD // LANES - 1)], [a_cols, dt_cols, *_gate_constants()],
                    [(LANES, F32)] * 3, [], tm=GATE_TM, name=name)


def _dn_gates_bwd(proj, a_cols, dt_cols, d_gates, *, name):
    def body(gi, gcf, gtf, bf, gcr, gtr, br, a, dt, *consts):
        _, vjp = jax.vjp(lambda gi_, a_, dt_: _dn_gates(gi_, a_, dt_, *consts), gi, a, dt)
        dgi, da, ddt = vjp((gcf + gcr, gtf + gtr, bf + br))
        return (dgi,), (da, ddt)
    return _rowwise(body, [_col(proj, LANES, DN_IN_PAD // LANES - 1), *d_gates[0], *d_gates[1]],
                    [a_cols, dt_cols, *_gate_constants()], [(LANES, F32)], [a_cols.shape, dt_cols.shape],
                    tm=GATE_TM, name=name)


INV_BASE = 8


def _block_id_equal(c, size):
    i = lax.broadcasted_iota(jnp.int32, (c, c), 0) // size
    j = lax.broadcasted_iota(jnp.int32, (c, c), 1) // size
    return (i == j).astype(F32)


def _unit_tri_inverse_impl(lmat):
    c = lmat.shape[0]
    eye = _block_id_equal(c, 1)
    same = _block_id_equal(c, INV_BASE)
    neg = -lmat * same
    inv = eye + neg
    power = neg
    for _ in range(int(math.log2(INV_BASE)) - 1):
        power = _dot(power, power)
        inv = inv + _dot(inv, power)
    size = INV_BASE
    while size < c:
        bigger = _block_id_equal(c, 2 * size)
        inv = inv - _dot(_dot(inv, lmat * (bigger - same)), inv)
        same, size = bigger, 2 * size
    resid = eye - _dot(eye + lmat, inv, precision=HI)
    return inv + _dot(inv, resid)


@jax.custom_vjp
def _unit_tri_inverse(lmat):
    return _unit_tri_inverse_impl(lmat)


def _unit_tri_inverse_fwd(lmat):
    inv = _unit_tri_inverse_impl(lmat)
    return inv, inv


def _unit_tri_inverse_bwd(inv, d_inv):
    return (-_dot_tn(inv, _dot_nt(d_inv, inv)),)


_unit_tri_inverse.defvjp(_unit_tri_inverse_fwd, _unit_tri_inverse_bwd)


def _dn_chunk(q, k, v, gates_t, gc_row, tot_row, beta_row, state, tri, inverse):
    c = q.shape[0]
    assert c == DN_HEAD_DIM
    eye = _block_id_equal(c, 1)

    def along_rows(x, pick):
        return jnp.broadcast_to(jnp.sum(x * pick, axis=0, keepdims=True), (c, c))

    gc_j = along_rows(gates_t[0], gc_row)
    gc = gc_j.T
    g_tot = along_rows(gates_t[1], tot_row)
    beta = along_rows(gates_t[2], beta_row).T
    decay = jnp.exp(jnp.where(tri > 0, gc - gc_j, NEG_INF))
    k_beta = k * beta
    inv = inverse((tri - eye) * (_dot_nt(k_beta, k) * decay))
    e_gc = jnp.exp(gc)
    u = _dot(inv, v * beta)
    w = _dot(inv, k_beta * e_gc)
    intra = tri * (_dot_nt(q, k) * decay)
    v_new = u - _dot(w, state)
    out = _dot(q * e_gc, state) + _dot(intra, v_new)
    state = state * jnp.exp(g_tot) + _dot_tn(k * jnp.exp(g_tot - gc), v_new)
    return out, state


def _dn_tri():
    i = np.arange(DN_CHUNK)
    tri = np.stack([(i[None, :] <= i[:, None]), (i[None, :] >= i[:, None])]).astype(np.float32)
    return jnp.asarray(np.repeat(tri, DN_HEADS, axis=0))


def _dn_gate_picks():
    picks = np.zeros((3, 2 * DN_HEADS, 2 * DN_CHUNK, 1), np.float32)
    for d in range(2):
        for h in range(DN_HEADS):
            alpha = d * DN_CHUNK + d * 2 * DN_HEADS + h
            picks[0, d * DN_HEADS + h, alpha] = 1.0
            picks[1, d * DN_HEADS + h, alpha] = 1.0
            picks[2, d * DN_HEADS + h, alpha + DN_HEADS] = 1.0
    return jnp.asarray(picks)


def _stack_chains(fwd_ref, rev_ref):
    return jnp.stack([r[:, _head_cols(h)] for r in (fwd_ref, rev_ref) for h in range(DN_HEADS)])


def _unstack_chains(val, fwd_ref, rev_ref):
    for d, r in enumerate((fwd_ref, rev_ref)):
        for h in range(DN_HEADS):
            r[:, _head_cols(h)] = val[d * DN_HEADS + h]


def _gates_transposed(fwd_refs, rev_refs):
    return jnp.stack([jnp.concatenate([f[...].T, r[...].T], axis=0) for f, r in zip(fwd_refs, rev_refs, strict=True)])


def _dn_row_spec(nc, col, reverse, width=TOK_WIDTH):
    return pl.BlockSpec((DN_CHUNK, width), lambda t: ((nc - 1 - t) if reverse else t, col))


def _dn_state_spec(nc, reverse):
    return pl.BlockSpec((None, DN_HEADS, DN_HEAD_DIM, DN_HEAD_DIM), lambda t: ((nc - 1 - t) if reverse else t, 0, 0, 0))


def _head_cols(h):
    return pl.ds(h * DN_HEAD_DIM, DN_HEAD_DIM)


def _const_spec(arr):
    return pl.BlockSpec(arr.shape, functools.partial(lambda t, n: (0,) * n, n=arr.ndim))


def _dn_chains(inverse):
    return jax.vmap(lambda q, k, v, gates_t, *rest: _dn_chunk(q, k, v, gates_t, *rest, inverse),
                    in_axes=(0, 0, 0, None, 0, 0, 0, 0, 0))


def _dn_scan_fwd(qkv, gates, *, name):
    s_len = qkv.shape[0]
    nc = s_len // DN_CHUNK
    tri, picks = _dn_tri(), _dn_gate_picks()

    def kern(*refs):
        ins, (tri_ref, pick_ref, of_ref, or_ref, sf_ref, sr_ref, state) = refs[:12], refs[12:]

        @pl.when(pl.program_id(0) == 0)
        def _():
            state[...] = jnp.zeros_like(state)

        entry = state[...]
        qkv_c = [_stack_chains(ins[i], ins[6 + i]) for i in range(3)]
        gates_t = _gates_transposed(ins[3:6], ins[9:12])
        out, new = _dn_chains(_unit_tri_inverse_impl)(*qkv_c, gates_t, pick_ref[0], pick_ref[1], pick_ref[2], entry, tri_ref[...])
        sf_ref[...] = entry[:DN_HEADS]
        sr_ref[...] = entry[DN_HEADS:]
        _unstack_chains(out, of_ref, or_ref)
        state[...] = new

    in_specs = []
    for rev in (False, True):
        in_specs += [_dn_row_spec(nc, col, rev) for col in (0, 1, 2)] + [_dn_row_spec(nc, 0, rev, LANES)] * 3
    in_specs += [_const_spec(tri), _const_spec(picks)]
    return pl.pallas_call(
        kern, name=name, grid=(nc,), in_specs=in_specs,
        out_specs=[_dn_row_spec(nc, 0, False), _dn_row_spec(nc, 0, True), _dn_state_spec(nc, False), _dn_state_spec(nc, True)],
        out_shape=[jax.ShapeDtypeStruct((s_len, TOK_WIDTH), F32)] * 2
        + [jax.ShapeDtypeStruct((nc, DN_HEADS, DN_HEAD_DIM, DN_HEAD_DIM), F32)] * 2,
        scratch_shapes=[pltpu.VMEM((2 * DN_HEADS, DN_HEAD_DIM, DN_HEAD_DIM), F32)],
        compiler_params=_params("arbitrary"),
    )(*([qkv, qkv, qkv, *gates] * 2), tri, picks)


def _dn_scan_bwd(qkv, gates, states, d_o, *, name):
    s_len = qkv.shape[0]
    nc = s_len // DN_CHUNK
    tri, picks = _dn_tri(), _dn_gate_picks()

    def kern(*refs):
        ins, tri_ref, pick_ref, outs, d_state = refs[:16], refs[16], refs[17], refs[18:30], refs[30]

        @pl.when(pl.program_id(0) == 0)
        def _():
            d_state[...] = jnp.zeros_like(d_state)

        qkv_c = [_stack_chains(ins[i], ins[8 + i]) for i in range(3)]
        gates_t = _gates_transposed(ins[3:6], ins[11:14])
        entry = jnp.concatenate([ins[6][...], ins[14][...]], axis=0)
        d_out = _stack_chains(ins[7], ins[15])
        tri_v, picks_v = tri_ref[...], pick_ref[...]
        _, vjp = jax.vjp(lambda q, k, v, g, s: _dn_chains(_unit_tri_inverse)(q, k, v, g, picks_v[0], picks_v[1], picks_v[2], s, tri_v),
                         *qkv_c, gates_t, entry)
        dq, dk, dv, d_gates_t, d_entry = vjp((d_out, d_state[...]))
        for i, val in enumerate((dq, dk, dv)):
            _unstack_chains(val, outs[i], outs[6 + i])
        for i in range(3):
            outs[3 + i][...] = d_gates_t[i, :DN_CHUNK].T
            outs[9 + i][...] = d_gates_t[i, DN_CHUNK:].T
        d_state[...] = d_entry

    in_specs, out_specs, out_shape = [], [], []
    for rev in (True, False):
        in_specs += [_dn_row_spec(nc, col, rev) for col in (0, 1, 2)] + [_dn_row_spec(nc, 0, rev, LANES)] * 3
        in_specs += [_dn_state_spec(nc, rev), _dn_row_spec(nc, 0, rev)]
        out_specs += [_dn_row_spec(nc, 0, rev)] * 3 + [_dn_row_spec(nc, 0, rev, LANES)] * 3
        out_shape += [jax.ShapeDtypeStruct((s_len, TOK_WIDTH), F32)] * 3 + [jax.ShapeDtypeStruct((s_len, LANES), F32)] * 3
    in_specs += [_const_spec(tri), _const_spec(picks)]
    res = pl.pallas_call(
        kern, name=name, grid=(nc,), in_specs=in_specs, out_specs=out_specs, out_shape=out_shape,
        scratch_shapes=[pltpu.VMEM((2 * DN_HEADS, DN_HEAD_DIM, DN_HEAD_DIM), F32)],
        compiler_params=_params("arbitrary"),
    )(*[a for d in range(2) for a in (qkv, qkv, qkv, *gates, states[d], d_o)], tri, picks)
    return (res[0:3], res[3:6]), (res[6:9], res[9:12])


def _dn_out_head(o_f, o_b, z, gain):
    o = o_f + o_b
    return o * lax.rsqrt(jnp.mean(o * o, axis=-1, keepdims=True) + EPS) * gain * _silu(z)


def _dn_out(o_fwd, o_rev, proj, gain, qkv_kv_mem, *, name):
    def body(of, ob, z, qm, g, kv):
        heads = []
        for h in range(DN_HEADS):
            sl = slice(h * DN_HEAD_DIM, (h + 1) * DN_HEAD_DIM)
            heads.append(_dn_out_head(of[:, sl], ob[:, sl], z[:, sl], g))
        return (jnp.concatenate(heads + [_mem_attn(qm, kv)], axis=1),), ()
    return _rowwise(body, [o_fwd, o_rev, _col(proj, TOK_WIDTH, 3),
                           _col(proj, MEM_WIDTH, (4 * TOK_WIDTH) // MEM_WIDTH)], [gain, qkv_kv_mem],
                    [(D_MODEL, MXU_DTYPE)], [], tm=256, name=name)[0]


def _dn_out_bwd(o_fwd, o_rev, proj, gain, kv_mem, dcat, *, name):
    def body(of, ob, z, qm, dcat, g, kv):
        dos, dzs = [], []
        dgain = jnp.zeros_like(g)
        for h in range(DN_HEADS):
            sl = slice(h * DN_HEAD_DIM, (h + 1) * DN_HEAD_DIM)
            _, vjp = jax.vjp(_dn_out_head, of[:, sl], ob[:, sl], z[:, sl], g)
            d_of, _, dz, dg = vjp(dcat[:, sl])
            dos.append(d_of)
            dzs.append(dz)
            dgain = dgain + dg
        dqm, dkv = _mem_attn_bwd(qm, kv, dcat[:, TOK_WIDTH:])
        return (jnp.concatenate(dos, axis=1), jnp.concatenate(dzs, axis=1), dqm), (dgain, dkv)
    return _rowwise(body, [o_fwd, o_rev, _col(proj, TOK_WIDTH, 3),
                           _col(proj, MEM_WIDTH, (4 * TOK_WIDTH) // MEM_WIDTH), dcat], [gain, kv_mem],
                    [(TOK_WIDTH, F32), (TOK_WIDTH, F32), (MEM_WIDTH, F32)], [gain.shape, kv_mem.shape], tm=256, name=name)


def _pad_dn_w_in(w):
    gates = w[:, 4 * TOK_WIDTH: 4 * TOK_WIDTH + N_GATES]
    zeros = jnp.zeros((w.shape[0], DN_IN_PAD - DN_IN), w.dtype)
    return jnp.concatenate([w[:, :4 * TOK_WIDTH], w[:, 4 * TOK_WIDTH + N_GATES:], gates, zeros], axis=1)


def _unpad_dn_w_in(w):
    q_mem = w[:, 4 * TOK_WIDTH: 4 * TOK_WIDTH + MEM_WIDTH]
    gates = w[:, 4 * TOK_WIDTH + MEM_WIDTH: 4 * TOK_WIDTH + MEM_WIDTH + N_GATES]
    return jnp.concatenate([w[:, :4 * TOK_WIDTH], gates, q_mem], axis=1)


def _ffn_fwd(h, w_gu, w_d, tag):
    gu = _mm(h, w_gu, name=f"ffn_gu_{tag}")
    act = _swiglu_act(gu, name=f"ffn_act_{tag}")
    return gu, act, _mm(act, w_d, name=f"ffn_down_{tag}")


def _ffn_bwd(h, gu, act, w_gu, w_d, df, tag):
    d_act = _mm(df, w_d, tb=True, name=f"ffn_dact_{tag}")
    d_wd = _mm(act, df, ta=True, name=f"ffn_dwd_{tag}")
    d_gu = _swiglu_act_bwd(gu, d_act, name=f"ffn_dgu_{tag}")
    dh = _mm(d_gu, w_gu, tb=True, name=f"ffn_dh_{tag}")
    d_wgu = _mm(h, d_gu, ta=True, out_shards=_shards_of(w_gu), name=f"ffn_dwgu_{tag}")
    return dh, d_wgu, d_wd


def _local_step(x, mem, target, p):
    g = {}
    row = lambda v: v.reshape(1, -1)
    gains = {k: [row(p[k][i]) for i in range(2)] for k in
             ("mem_norm", "norm_mix_pre", "norm_mix_post", "norm_ffn_pre", "norm_ffn_post")}
    out_gain = row(p["dn_out_norm"])
    a_cols, dt_cols = _gate_params(p["dn_a_log"]), _gate_params(p["dn_dt_bias"])

    h0 = _pre_norm(x, gains["norm_mix_pre"][0], name="pre0")
    mem_n = [_pre_norm(mem, gains["mem_norm"][i], name=f"mem_norm{i}") for i in range(2)]
    kv_mem = [_mm(mem_n[i], p["mem_w_kv"][i], name=f"mem_kv{i}") for i in range(2)]
    qkvm = _mm(h0, p["att_w_in"], name="att_in")
    bias = [_bias_tiles(p["rel_bias"], gi) for gi in range(3)]
    att = [_att_fwd(qkvm, bias[gi], gi, name=f"att_fwd{gi}") for gi in range(3)]
    cat0, lse_tot = _att_combine([a[0] for a in att], [a[1] for a in att], qkvm, kv_mem[0], name="att_combine")
    mo0 = _mm(cat0, p["att_w_out"], name="att_out")
    x1, h1 = _post_pre(x, mo0, gains["norm_mix_post"][0], gains["norm_ffn_pre"][0], name="post_mix0")
    gu0, act0, f0 = _ffn_fwd(h1, p["ffn_w_gate_up"][0], p["ffn_w_down"][0], 0)
    x2, h2 = _post_pre(x1, f0, gains["norm_ffn_post"][0], gains["norm_mix_pre"][1], name="post_ffn0")

    proj = _mm(h2, p["dn_w_in"], name="dn_in")
    qkv = _dn_conv_fwd(proj, p["dn_conv"], name="dn_conv")
    gates = _dn_gates_fwd(proj, a_cols, dt_cols, name="dn_gates")
    o_fwd, o_rev, st_fwd, st_rev = _dn_scan_fwd(qkv, gates, name="dn_scan")
    cat1 = _dn_out(o_fwd, o_rev, proj, out_gain, kv_mem[1], name="dn_outnorm")
    mo1 = _mm(cat1, p["dn_w_out"], name="dn_out")
    x3, h3 = _post_pre(x2, mo1, gains["norm_mix_post"][1], gains["norm_ffn_pre"][1], name="post_mix1")
    gu1, act1, f1 = _ffn_fwd(h3, p["ffn_w_gate_up"][1], p["ffn_w_down"][1], 1)

    dx3, df1, dg_ffn_post1, loss_cols = _final_loss_bwd(x3, f1, gains["norm_ffn_post"][1], target, name="loss_bwd")
    dh3, d_wgu1, d_wd1 = _ffn_bwd(h3, gu1, act1, p["ffn_w_gate_up"][1], p["ffn_w_down"][1], df1, 1)
    dx2, dmo1, dg_mix_post1, dg_ffn_pre1 = _post_pre_bwd(x2, mo1, gains["norm_mix_post"][1], gains["norm_ffn_pre"][1],
                                                         dx3, dh3, name="post_mix1_bwd")
    dcat1 = _mm(dmo1, p["dn_w_out"], tb=True, name="dn_out_dx")
    g["dn_w_out"] = _mm(cat1, dmo1, ta=True, name="dn_out_dw")
    d_o, dz, dqm1, d_out_gain, dkv1 = _dn_out_bwd(o_fwd, o_rev, proj, out_gain, kv_mem[1], dcat1, name="dn_outnorm_bwd")
    (d_f, dg_f), (d_r, dg_r) = _dn_scan_bwd(qkv, gates, (st_fwd, st_rev), d_o, name="dn_scan_bwd")
    d_gate_cols, d_a_cols, d_dt_cols = _dn_gates_bwd(proj, a_cols, dt_cols, (dg_f, dg_r), name="dn_gates_bwd")
    d_pre, d_conv = zip(*[_dn_conv_bwd(proj, p["dn_conv"], d_f[which], d_r[which], which, name=f"dn_conv_bwd{which}")
                          for which in range(3)])
    dproj = jnp.concatenate(list(d_pre) + [dz, dqm1, d_gate_cols], axis=1).astype(MXU_DTYPE)
    dh2 = _mm(dproj, p["dn_w_in"], tb=True, name="dn_in_dx")
    g["dn_w_in"] = _mm(h2, dproj, ta=True, name="dn_in_dw")
    g["dn_conv"] = jnp.concatenate(d_conv, axis=1)
    g["dn_a_log"] = _gate_params_bwd(d_a_cols)
    g["dn_dt_bias"] = _gate_params_bwd(d_dt_cols)
    g["dn_out_norm"] = d_out_gain

    dx1, df0, dg_ffn_post0, dg_mix_pre1 = _post_pre_bwd(x1, f0, gains["norm_ffn_post"][0], gains["norm_mix_pre"][1],
                                                        dx2, dh2, name="post_ffn0_bwd")
    dh1, d_wgu0, d_wd0 = _ffn_bwd(h1, gu0, act0, p["ffn_w_gate_up"][0], p["ffn_w_down"][0], df0, 0)
    dx0, dmo0, dg_mix_post0, dg_ffn_pre0 = _post_pre_bwd(x, mo0, gains["norm_mix_post"][0], gains["norm_ffn_pre"][0],
                                                         dx1, dh1, name="post_mix0_bwd")
    dcat0 = _mm(dmo0, p["att_w_out"], tb=True, name="att_out_dx")
    g["att_w_out"] = _mm(cat0, dmo0, ta=True, name="att_out_dw")
    delta, dqm0, dkv0 = _att_bwd_prep(cat0, dcat0, qkvm, kv_mem[0], name="att_bwd_prep")
    att_b = [_att_bwd(qkvm, bias[gi], lse_tot, delta, dcat0, gi, name=f"att_bwd{gi}") for gi in range(3)]
    dqkvm = jnp.concatenate([a[w] for w in range(3) for a in att_b] + [dqm0], axis=1).astype(MXU_DTYPE)
    g["rel_bias"] = sum(_bias_tiles_bwd(p["rel_bias"], att_b[gi][3], gi) for gi in range(3))
    dh0 = _mm(dqkvm, p["att_w_in"], tb=True, name="att_in_dx")
    g["att_w_in"] = _mm(h0, dqkvm, ta=True, out_shards=_shards_of(p["att_w_in"]), name="att_in_dw")
    grad_x, dg_mix_pre0 = _pre_norm_bwd(x, gains["norm_mix_pre"][0], dh0, dx0, name="pre0_bwd")

    d_mem_kv, d_mem_norm = [], []
    for i, dkv in enumerate((dkv0, dkv1)):
        d_mem_kv.append(_mm(mem_n[i], dkv, ta=True, name=f"mem_kv_dw{i}"))
        d_mem_n = _mm(dkv, p["mem_w_kv"][i], tb=True, name=f"mem_kv_dx{i}")
        d_mem_norm.append(_gain_bwd(mem, gains["mem_norm"][i], d_mem_n, name=f"mem_norm_bwd{i}"))
    g["mem_w_kv"] = d_mem_kv
    g["mem_norm"] = jnp.concatenate(d_mem_norm, axis=0)
    g["norm_mix_pre"] = jnp.concatenate([dg_mix_pre0, dg_mix_pre1], axis=0)
    g["norm_mix_post"] = jnp.concatenate([dg_mix_post0, dg_mix_post1], axis=0)
    g["norm_ffn_pre"] = jnp.concatenate([dg_ffn_pre0, dg_ffn_pre1], axis=0)
    g["norm_ffn_post"] = jnp.concatenate([dg_ffn_post0, dg_ffn_post1], axis=0)
    g["ffn_w_gate_up"] = [d_wgu0, d_wgu1]
    g["ffn_w_down"] = [d_wd0, d_wd1]
    return loss_cols, grad_x, g


N_CHIPS = 4
N_DEV = 8
MESH = pl.DeviceIdType.MESH
BIG = (("att_w_in", (1, 1024, 640), 2), ("att_w_out", (1, 256, 1024), 1), ("dn_w_in", (1, 1024, 838), 2),
       ("dn_w_out", (1, 256, 1024), 1), ("mem_w_kv", (2, 256, 512), 1), ("ffn_w_gate_up", (2, 1024, 1408), 2),
       ("ffn_w_down", (2, 704, 1024), 1))


def _mesh_pos():
    return lax.axis_index("x"), lax.axis_index("y"), lax.axis_index("c")


def _other_chips(x, y):
    return [(1 - x, y), (x, 1 - y), (1 - x, 1 - y)]


ANY = pl.BlockSpec(memory_space=pl.ANY)


def _all_reduce_small(v, *, name):
    rows, cols = v.shape
    flips = [(dx, dy, dc) for dx in (0, 1) for dy in (0, 1) for dc in (0, 1)][1:]

    def body(v_ref, o_ref, buf, send_sems, recv_sems):
        x, y, c = _mesh_pos()

        def peer(f):
            return tuple(1 - p if fl else p for p, fl in zip((x, y, c), f))

        def index(p):
            return 4 * p[0] + 2 * p[1] + p[2]

        buf[index((x, y, c))] = v_ref[...]
        sends = []
        for k, f in enumerate(flips):
            cp = pltpu.make_async_remote_copy(src_ref=v_ref, dst_ref=buf.at[index((x, y, c))], send_sem=send_sems.at[k],
                                              recv_sem=recv_sems.at[k], device_id=peer(f), device_id_type=MESH)
            cp.start()
            sends.append(cp)
        for k, f in enumerate(flips):
            pltpu.make_async_remote_copy(src_ref=v_ref, dst_ref=buf.at[index(peer(f))], send_sem=send_sems.at[k],
                                         recv_sem=recv_sems.at[k], device_id=peer(f), device_id_type=MESH).wait_recv()
        for cp in sends:
            cp.wait_send()
        acc = buf[0]
        for d in range(1, N_DEV):
            acc = acc + buf[d]
        o_ref[...] = acc

    vmem = pl.BlockSpec(memory_space=pltpu.VMEM)
    return pl.pallas_call(
        body, name=name, in_specs=[vmem], out_specs=vmem, out_shape=jax.ShapeDtypeStruct((rows, cols), F32),
        scratch_shapes=[pltpu.VMEM((N_DEV, rows, cols), F32), pltpu.SemaphoreType.DMA((N_DEV - 1,)),
                        pltpu.SemaphoreType.DMA((N_DEV - 1,))],
    )(v)


def _adamw(w, g, m, v, *, name):
    def body(w, g, m, v):
        m = ADAM_B1 * m + (1.0 - ADAM_B1) * g
        v = ADAM_B2 * v + (1.0 - ADAM_B2) * (g * g)
        m_hat = m / (1.0 - ADAM_B1 ** ADAM_STEP)
        v_hat = v / (1.0 - ADAM_B2 ** ADAM_STEP)
        delta = -ADAM_LR * (m_hat / (jnp.sqrt(v_hat) + ADAM_EPS) + ADAM_WD * w)
        return (delta, m, v), ()
    rows, cols = w.shape
    return _rowwise(body, [w, g, m, v], [], [(cols, F32)] * 3, [], tm=_tile(rows, 256, SUBLANES), name=name)


def _pack_small(arrs, rows):
    flat = jnp.concatenate([a.reshape(-1) for a in arrs])
    return jnp.pad(flat, (0, rows * LANES - flat.shape[0])).reshape(rows, LANES)


def _unpack_small(packed, shapes):
    flat = packed.reshape(-1)
    out, off = [], 0
    for s in shapes:
        size = math.prod(s)
        out.append(flat[off: off + size].reshape(s))
        off += size
    return out


def _small_rows(shapes):
    return -(-sum(math.prod(s) for s in shapes) // (SUBLANES * LANES)) * SUBLANES


def _sem_pairs(n):
    return [pltpu.SemaphoreType.DMA((n,)), pltpu.SemaphoreType.DMA((n,))]


def _gather_blocks(blocks, *, name):
    n = len(blocks)

    def body(*refs):
        x_refs, out_refs, (send_sems, recv_sems) = refs[:n], refs[n: 2 * n], refs[2 * n:]
        x, y, c = _mesh_pos()
        sibling = (x, y, 1 - c)
        chips = _other_chips(x, y)

        def copy(k, src, dst, to):
            return pltpu.make_async_remote_copy(src_ref=src, dst_ref=dst, send_sem=send_sems.at[k],
                                                recv_sem=recv_sems.at[k], device_id=to, device_id_type=MESH)

        def part(b, chip, h):
            half = blocks[b].shape[0] // 2
            return out_refs[b].at[2 * chip[0] + chip[1], pl.ds(h * half, half), :]

        def my_half(b):
            half = blocks[b].shape[0] // 2
            return x_refs[b].at[pl.ds(c * half, half), :]

        first = [copy(6 * b + j, my_half(b), part(b, (x, y), c), (*chip, c)) for b in range(n) for j, chip in enumerate(chips)]
        for cp in first:
            cp.start()
        passed = []
        for b in range(n):
            for j, chip in enumerate(chips):
                copy(6 * b + j, my_half(b), part(b, chip, c), (*chip, c)).wait_recv()
                cp = copy(6 * b + 3 + j, part(b, chip, c), part(b, chip, c), sibling)
                cp.start()
                passed.append(cp)
        for b in range(n):
            for j, chip in enumerate(chips):
                copy(6 * b + 3 + j, part(b, chip, 1 - c), part(b, chip, 1 - c), sibling).wait_recv()
        for cp in first + passed:
            cp.wait_send()

    return pl.pallas_call(
        body, name=name, in_specs=[ANY] * n, out_specs=[ANY] * n,
        out_shape=[jax.ShapeDtypeStruct((N_CHIPS, *a.shape), a.dtype) for a in blocks],
        scratch_shapes=_sem_pairs(6 * n),
    )(*blocks)


def _swap_halves(blocks, own_rows, *, name):
    n = len(blocks)

    def body(*refs):
        in_refs, out_refs, (send_sems, recv_sems) = refs[:n], refs[n: 2 * n], refs[2 * n:]
        x, y, c = _mesh_pos()
        copies = [pltpu.make_async_remote_copy(src_ref=own_rows(in_refs[b], c), dst_ref=out_refs[b], send_sem=send_sems.at[b],
                                               recv_sem=recv_sems.at[b], device_id=(x, y, 1 - c), device_id_type=MESH)
                  for b in range(n)]
        for cp in copies:
            cp.start()
        for cp in copies:
            cp.wait()

    def sent_shape(a):
        return jax.eval_shape(lambda r: own_rows(r, 0), a)

    return pl.pallas_call(
        body, name=name, in_specs=[ANY] * n, out_specs=[ANY] * n,
        out_shape=[jax.ShapeDtypeStruct(sent_shape(a).shape, a.dtype) for a in blocks],
        scratch_shapes=_sem_pairs(n),
    )(*blocks)


def _other_half_rows(ref, c):
    half = ref.shape[1] // 2
    return ref[:, (1 - c) * half: (2 - c) * half, :] if isinstance(c, int) else ref.at[:, pl.ds((1 - c) * half, half), :]


def _whole(ref, c):
    return ref


def _add_own_half_block(block, received, core, *, name):
    n, rows, cols = block.shape
    half = rows // 2
    tm = _tile(half, 512, 2 * SUBLANES)
    nb = half // tm

    def kern(c_ref, a_ref, b_ref, o_ref):
        o_ref[...] = (a_ref[...] + b_ref[...]).astype(o_ref.dtype)

    return pl.pallas_call(
        kern, name=name,
        grid_spec=pltpu.PrefetchScalarGridSpec(
            num_scalar_prefetch=1, grid=(n, nb),
            in_specs=[pl.BlockSpec((None, tm, cols), lambda s, i, c: (s, c[0] * nb + i, 0)),
                      pl.BlockSpec((None, tm, cols), lambda s, i, c: (s, i, 0))],
            out_specs=pl.BlockSpec((None, tm, cols), lambda s, i, c: (s, i, 0))),
        out_shape=jax.ShapeDtypeStruct((n, half, cols), LINK_DTYPE),
        compiler_params=_params("parallel", "parallel"),
    )(core, block, received)


def _scatter_blocks(sums, *, name):
    n = len(sums)

    def body(*refs):
        s_refs, out_refs, (send_sems, recv_sems) = refs[:n], refs[n: 2 * n], refs[2 * n:]
        x, y, c = _mesh_pos()
        me = 2 * x + y
        chips = _other_chips(x, y)

        def copy(b, j, chip, src_slot, dst_slot):
            return pltpu.make_async_remote_copy(src_ref=s_refs[b].at[src_slot], dst_ref=out_refs[b].at[dst_slot],
                                                send_sem=send_sems.at[3 * b + j], recv_sem=recv_sems.at[3 * b + j],
                                                device_id=(*chip, c), device_id_type=MESH)

        sends = [copy(b, j, chip, 2 * chip[0] + chip[1], me) for b in range(n) for j, chip in enumerate(chips)]
        for cp in sends:
            cp.start()
        for b in range(n):
            for j, chip in enumerate(chips):
                copy(b, j, chip, me, 2 * chip[0] + chip[1]).wait_recv()
        for cp in sends:
            cp.wait_send()

    return pl.pallas_call(
        body, name=name, in_specs=[ANY] * n, out_specs=[ANY] * n,
        out_shape=[jax.ShapeDtypeStruct(a.shape, a.dtype) for a in sums], scratch_shapes=_sem_pairs(3 * n),
    )(*sums)


def _sum_chips_block(parts, *, name):
    n, half, cols = parts.shape
    tm = _tile(half, 512, 2 * SUBLANES)

    def kern(p_ref, o_ref):
        acc = p_ref[0].astype(F32)
        for s in range(1, n):
            acc = acc + p_ref[s].astype(F32)
        o_ref[...] = acc

    return pl.pallas_call(
        kern, name=name, grid=(half // tm,),
        in_specs=[pl.BlockSpec((n, tm, cols), lambda i: (0, i, 0))],
        out_specs=pl.BlockSpec((tm, cols), lambda i: (i, 0)),
        out_shape=jax.ShapeDtypeStruct((half, cols), F32),
        compiler_params=_params("parallel"),
    )(parts)


def _reduce_scatter_blocks(blocks, names):
    x, y, c = _mesh_pos()
    chip = 2 * x + y
    core = c.astype(jnp.int32).reshape(1)
    received = _swap_halves(blocks, _other_half_rows, name="rs_swap")
    sums = [_add_own_half_block(b, r, core, name=f"rs_add_{nm}") for b, r, nm in zip(blocks, received, names, strict=True)]
    parts = _scatter_blocks(sums, name="rs_scatter")
    parts = [lax.dynamic_update_slice(p, lax.dynamic_slice_in_dim(s, chip, 1, axis=0), (chip, 0, 0))
             for p, s in zip(parts, sums, strict=True)]
    mine = [_sum_chips_block(p, name=f"rs_sum_{nm}") for p, nm in zip(parts, names, strict=True)]
    other = _swap_halves(mine, _whole, name="rs_join")
    return [jnp.concatenate([jnp.where(c == 0, a, b), jnp.where(c == 0, b, a)], axis=0) for a, b in zip(mine, other, strict=True)]


WEIGHTS = ("rel_bias", "att_w_in", "att_w_out", "dn_w_in", "dn_conv", "dn_a_log", "dn_dt_bias", "dn_out_norm", "dn_w_out",
           "mem_norm", "mem_w_kv", "norm_mix_pre", "norm_mix_post", "norm_ffn_pre", "norm_ffn_post", "ffn_w_gate_up",
           "ffn_w_down")
BIG_NAMES = tuple(n for n, _, _ in BIG)
SMALL_NAMES = tuple(n for n in WEIGHTS if n not in BIG_NAMES)
CONV_COLS = 3 * TOK_WIDTH
CONV_SHARD = CONV_COLS // N_CHIPS
BLOCKS = tuple((n, layer) for n, shape, _ in BIG for layer in range(shape[0]))
COLUMN_SHARDED = {n: axis == 2 for n, _, axis in BIG}


def kernel(x, mem, rel_bias, att_w_in, att_w_out, dn_w_in, dn_conv, dn_a_log, dn_dt_bias, dn_out_norm, dn_w_out, mem_norm, mem_w_kv, norm_mix_pre, norm_mix_post, norm_ffn_pre, norm_ffn_post, ffn_w_gate_up, ffn_w_down, loss_target, m_rel_bias, m_att_w_in, m_att_w_out, m_dn_w_in, m_dn_conv, m_dn_a_log, m_dn_dt_bias, m_dn_out_norm, m_dn_w_out, m_mem_norm, m_mem_w_kv, m_norm_mix_pre, m_norm_mix_post, m_norm_ffn_pre, m_norm_ffn_post, m_ffn_w_gate_up, m_ffn_w_down, v_rel_bias, v_att_w_in, v_att_w_out, v_dn_w_in, v_dn_conv, v_dn_a_log, v_dn_dt_bias, v_dn_out_norm, v_dn_w_out, v_mem_norm, v_mem_w_kv, v_norm_mix_pre, v_norm_mix_post, v_norm_ffn_pre, v_norm_ffn_post, v_ffn_w_gate_up, v_ffn_w_down):
    w = dict(zip(WEIGHTS, (rel_bias, att_w_in, att_w_out, dn_w_in, dn_conv, dn_a_log, dn_dt_bias, dn_out_norm, dn_w_out,
                           mem_norm, mem_w_kv, norm_mix_pre, norm_mix_post, norm_ffn_pre, norm_ffn_post, ffn_w_gate_up,
                           ffn_w_down)))
    m = dict(zip(WEIGHTS, (m_rel_bias, m_att_w_in, m_att_w_out, m_dn_w_in, m_dn_conv, m_dn_a_log, m_dn_dt_bias,
                           m_dn_out_norm, m_dn_w_out, m_mem_norm, m_mem_w_kv, m_norm_mix_pre, m_norm_mix_post,
                           m_norm_ffn_pre, m_norm_ffn_post, m_ffn_w_gate_up, m_ffn_w_down)))
    v = dict(zip(WEIGHTS, (v_rel_bias, v_att_w_in, v_att_w_out, v_dn_w_in, v_dn_conv, v_dn_a_log, v_dn_dt_bias,
                           v_dn_out_norm, v_dn_w_out, v_mem_norm, v_mem_w_kv, v_norm_mix_pre, v_norm_mix_post,
                           v_norm_ffn_pre, v_norm_ffn_post, v_ffn_w_gate_up, v_ffn_w_down)))
    cx, cy, cc = _mesh_pos()
    chip = 2 * cx + cy

    local = lax.optimization_barrier([w[n][layer].astype(MXU_DTYPE) for n, layer in BLOCKS])
    gathered = [lax.dynamic_update_slice(got, mine[None], (chip, 0, 0))
                for got, mine in zip(_gather_blocks(local, name="gather_weights"), local, strict=True)]
    full = {}
    for (n, _), got in zip(BLOCKS, gathered, strict=True):
        full.setdefault(n, []).append(got if COLUMN_SHARDED[n] else got.reshape(-1, got.shape[-1]))
    dn_w_in = jnp.concatenate([full["dn_w_in"][0][s] for s in range(N_CHIPS)], axis=1)
    conv_rows = _small_rows([(DN_CONV, CONV_COLS)])
    conv_mine = jnp.where(cc == 0, 1.0, 0.0) * w["dn_conv"][0]
    conv_placed = lax.dynamic_update_slice(jnp.zeros((DN_CONV, CONV_COLS), F32), conv_mine, (0, chip * CONV_SHARD))
    conv_full = _unpack_small(_all_reduce_small(_pack_small([conv_placed], conv_rows), name="gather_conv"),
                              [(DN_CONV, CONV_COLS)])[0]
    p = {
        "rel_bias": w["rel_bias"], "att_w_in": full["att_w_in"][0], "att_w_out": full["att_w_out"][0],
        "dn_w_in": _pad_dn_w_in(dn_w_in), "dn_conv": conv_full, "dn_a_log": w["dn_a_log"][0],
        "dn_dt_bias": w["dn_dt_bias"][0], "dn_out_norm": w["dn_out_norm"][0], "dn_w_out": full["dn_w_out"][0],
        "mem_norm": w["mem_norm"], "mem_w_kv": full["mem_w_kv"], "norm_mix_pre": w["norm_mix_pre"],
        "norm_mix_post": w["norm_mix_post"], "norm_ffn_pre": w["norm_ffn_pre"], "norm_ffn_post": w["norm_ffn_post"],
        "ffn_w_gate_up": full["ffn_w_gate_up"], "ffn_w_down": full["ffn_w_down"],
    }

    loss_cols, grad_x, g = _local_step(x[0], mem[0], loss_target[0], p)
    loss = lax.psum(jnp.sum(loss_cols), ("x", "y", "c"))

    g_layers = {"att_w_in": [g["att_w_in"]], "att_w_out": [g["att_w_out"]], "dn_w_in": [_unpad_dn_w_in(g["dn_w_in"])],
                "dn_w_out": [g["dn_w_out"]], "mem_w_kv": g["mem_w_kv"], "ffn_w_gate_up": g["ffn_w_gate_up"],
                "ffn_w_down": g["ffn_w_down"]}

    def chip_blocks(n, a):
        if a.ndim == 3:
            return a
        if COLUMN_SHARDED[n]:
            return a.reshape(a.shape[0], N_CHIPS, -1).transpose(1, 0, 2)
        return a.reshape(N_CHIPS, -1, a.shape[-1])

    reduced = _reduce_scatter_blocks([chip_blocks(n, g_layers[n][layer]) for n, layer in BLOCKS],
                                     [f"{n}{layer}" for n, layer in BLOCKS])
    grads = {n: jnp.concatenate([r for (bn, _), r in zip(BLOCKS, reduced, strict=True) if bn == n], axis=0).reshape(shape)
             for n, shape, _ in BIG}
    small_full_shapes = [(DN_CONV, CONV_COLS) if n == "dn_conv" else w[n].shape for n in SMALL_NAMES]
    small_sum = _all_reduce_small(_pack_small([g[n] for n in SMALL_NAMES], _small_rows(small_full_shapes)), name="reduce_small")
    for n, s in zip(SMALL_NAMES, _unpack_small(small_sum, small_full_shapes)):
        grads[n] = lax.dynamic_slice(s, (0, chip * CONV_SHARD), (DN_CONV, CONV_SHARD))[None] if n == "dn_conv" else s

    delta, new_m, new_v = {}, {}, {}
    for n in BIG_NAMES:
        shape = w[n].shape
        two_d = lambda a: a.reshape(-1, shape[-1])
        res = _adamw(two_d(w[n]), two_d(grads[n]), two_d(m[n]), two_d(v[n]), name=f"adamw_{n}")
        delta[n], new_m[n], new_v[n] = (r.reshape(shape) for r in res)
    small_shapes = [w[n].shape for n in SMALL_NAMES]
    rows = _small_rows(small_shapes)
    res = _adamw(*[_pack_small([d[n] for n in SMALL_NAMES], rows) for d in (w, grads, m, v)], name="adamw_small")
    for d, r in zip((delta, new_m, new_v), res):
        for n, a in zip(SMALL_NAMES, _unpack_small(r, small_shapes)):
            d[n] = a
    return (loss, grad_x[None], *[grads[n] for n in WEIGHTS], *[delta[n] for n in WEIGHTS],
            *[new_m[n] for n in WEIGHTS], *[new_v[n] for n in WEIGHTS])
```

```python
import functools
import math

import numpy as np
import jax
import jax.numpy as jnp
from jax import lax
from jax.experimental import pallas as pl
from jax.experimental.pallas import tpu as pltpu

F32 = jnp.float32
MXU_DTYPE = jnp.bfloat16
LINK_DTYPE = jnp.bfloat16
HI = lax.Precision.HIGHEST

EPS = 1e-6
NEG_INF = -1e30
LANES = 128
SUBLANES = 8
VMEM_LIMIT = 56 * 1024 * 1024

D_MODEL = 1024
TOK_WIDTH = 768
MEM_WIDTH = 256
MEM_LEN = 256
ATT_HEAD_DIM = 64
DILATIONS = (1, 4, 16)
HALF = 64
ATT_BQ = 128
ATT_W = ATT_BQ + 2 * HALF
REL_BUCKETS = 32
REL_MAX_DIST = 1024
DN_HEADS = 6
DN_HEAD_DIM = 128
DN_CONV = 5
DN_CHUNK = 128
D_FF = 2816
ATT_IN = 2560
DN_IN = 3352
DN_IN_PAD = 3456
N_GATES = 4 * DN_HEADS

ADAM_LR = 0.001
ADAM_B1 = 0.9
ADAM_B2 = 0.999
ADAM_EPS = 1e-08
ADAM_WD = 0.01
ADAM_STEP = 10


def _tile(n, target, align):
    if n <= target:
        return n
    t = (target // align) * align
    while t >= align:
        if n % t == 0:
            return t
        t -= align
    raise ValueError(f"no tile for {n} (target {target}, align {align})")


def _params(*sem):
    return pltpu.CompilerParams(dimension_semantics=sem, vmem_limit_bytes=VMEM_LIMIT)


def _mm(a, b, *, name, ta=False, tb=False, out_shards=None, tm=1024, tn=1408, tk=1408, out_dtype=F32):
    if ta:
        K, M = a.shape
    else:
        M, K = a.shape
    sharded_b = b.ndim == 3
    if sharded_b:
        n_sh, b_rows, b_cols = b.shape
        N, K2 = (b_rows, n_sh * b_cols) if tb else (n_sh * b_cols, b_rows)
    else:
        N, K2 = b.shape if tb else b.shape[::-1]
    assert K == K2, (a.shape, b.shape, ta, tb)
    tm = _tile(M, tm, LANES if ta else SUBLANES)
    tn = N // out_shards if out_shards else (b_cols if sharded_b and not tb else _tile(N, tn, LANES))
    tk = b_cols if sharded_b and tb else _tile(K, tk, LANES)
    nk = K // tk
    a_spec = pl.BlockSpec((tk, tm), lambda i, j, k: (k, i)) if ta else pl.BlockSpec((tm, tk), lambda i, j, k: (i, k))
    if sharded_b:
        b_spec = (pl.BlockSpec((None, tn, tk), lambda i, j, k: (k, j, 0)) if tb
                  else pl.BlockSpec((None, tk, tn), lambda i, j, k: (j, k, 0)))
    else:
        b_spec = pl.BlockSpec((tn, tk), lambda i, j, k: (j, k)) if tb else pl.BlockSpec((tk, tn), lambda i, j, k: (k, j))
    if out_shards:
        out_spec = pl.BlockSpec((None, tm, tn), lambda i, j, k: (j, i, 0))
        out_shape = jax.ShapeDtypeStruct((out_shards, M, tn), out_dtype)
    else:
        out_spec = pl.BlockSpec((tm, tn), lambda i, j, k: (i, j))
        out_shape = jax.ShapeDtypeStruct((M, N), out_dtype)
    dims = (((0 if ta else 1,), (1 if tb else 0,)), ((), ()))

    def kern(a_ref, b_ref, o_ref, acc_ref):
        k = pl.program_id(2)

        @pl.when(k == 0)
        def _():
            acc_ref[...] = jnp.zeros_like(acc_ref)

        acc_ref[...] += lax.dot_general(a_ref[...].astype(MXU_DTYPE), b_ref[...].astype(MXU_DTYPE), dims,
                                        preferred_element_type=F32)

        @pl.when(k == nk - 1)
        def _():
            o_ref[...] = acc_ref[...].astype(o_ref.dtype)

    return pl.pallas_call(
        kern, name=name, grid=(M // tm, N // tn, nk), in_specs=[a_spec, b_spec],
        out_specs=out_spec, out_shape=out_shape,
        scratch_shapes=[pltpu.VMEM((tm, tn), F32)],
        compiler_params=_params("parallel", "parallel", "arbitrary"),
    )(a, b)


def _shards_of(w):
    return w.shape[0] if w.ndim == 3 else None


def _col(arr, width, blk):
    return (arr, width, blk)


def _rowwise(body, rows, consts, out_rows, out_acc, *, tm, name):
    n_rows = (rows[0][0] if isinstance(rows[0], tuple) else rows[0]).shape[0]
    assert n_rows % tm == 0, (n_rows, tm)
    arrs, in_specs = [], []
    for r in rows:
        arr, width, blk = r if isinstance(r, tuple) else (r, r.shape[1], 0)
        assert arr.shape[0] == n_rows
        arrs.append(arr)
        in_specs.append(pl.BlockSpec((tm, width), functools.partial(lambda i, b: (i, b), b=blk)))
    for c in consts:
        arrs.append(c)
        in_specs.append(pl.BlockSpec(c.shape, functools.partial(lambda i, n: (0,) * n, n=c.ndim)))
    n_in, n_ro = len(arrs), len(out_rows)
    out_shape = [jax.ShapeDtypeStruct((n_rows, w), dt) for w, dt in out_rows]
    out_specs = [pl.BlockSpec((tm, w), lambda i: (i, 0)) for w, _ in out_rows]
    out_shape += [jax.ShapeDtypeStruct(s, F32) for s in out_acc]
    out_specs += [pl.BlockSpec(s, lambda i: (0, 0)) for s in out_acc]

    def kern(*refs):
        ro, ao = body(*[r[...] for r in refs[:n_in]])
        outs = refs[n_in:]
        for r, v in zip(outs[:n_ro], ro, strict=True):
            r[...] = v.astype(r.dtype)
        if out_acc:
            @pl.when(pl.program_id(0) == 0)
            def _():
                for r in outs[n_ro:]:
                    r[...] = jnp.zeros_like(r)

            for r, v in zip(outs[n_ro:], ao, strict=True):
                r[...] += v

    res = pl.pallas_call(
        kern, name=name, grid=(n_rows // tm,), in_specs=in_specs, out_specs=out_specs, out_shape=out_shape,
        compiler_params=_params("arbitrary" if out_acc else "parallel"),
    )(*arrs)
    return res


def _rms(x, gain):
    return x * lax.rsqrt(jnp.mean(x * x, axis=-1, keepdims=True) + EPS) * gain


def _silu(x):
    return x * jax.nn.sigmoid(x)


def _softplus(x):
    return jnp.maximum(x, 0.0) + jnp.log(1.0 + jnp.exp(-jnp.abs(x)))


def _dot_nt(a, b, precision=None):
    return lax.dot_general(a, b, (((1,), (1,)), ((), ())), preferred_element_type=F32, precision=precision)


def _dot_tn(a, b, precision=None):
    return lax.dot_general(a, b, (((0,), (0,)), ((), ())), preferred_element_type=F32, precision=precision)


def _dot(a, b, precision=None):
    return jnp.dot(a, b, preferred_element_type=F32, precision=precision)


def _pre_norm(x, gain, *, name):
    def body(x, g):
        return (_rms(x, g),), ()
    return _rowwise(body, [x], [gain], [(x.shape[1], MXU_DTYPE)], [], tm=_tile(x.shape[0], 512, 2 * SUBLANES), name=name)[0]


def _pre_norm_bwd(x, gain, dh, dx_other, *, name):
    def body(x, dh, dxo, g):
        _, vjp = jax.vjp(_rms, x, g)
        dx, dg = vjp(dh)
        return (dx + dxo,), (dg,)
    return _rowwise(body, [x, dh, dx_other], [gain], [(x.shape[1], F32)], [gain.shape], tm=512, name=name)


def _gain_bwd(x, gain, dh, *, name):
    def body(x, dh, g):
        _, vjp = jax.vjp(lambda g_: _rms(x, g_), g)
        return (), (vjp(dh)[0],)
    return _rowwise(body, [x, dh], [gain], [], [gain.shape], tm=_tile(x.shape[0], 512, SUBLANES), name=name)[0]


def _res_block(x_res, m, g_post, g_pre):
    x_new = x_res + _rms(m, g_post)
    return x_new, _rms(x_new, g_pre)


def _post_pre(x_res, m, g_post, g_pre, *, name):
    def body(x, m, gp, gq):
        return _res_block(x, m, gp, gq), ()
    d = x_res.shape[1]
    return _rowwise(body, [x_res, m], [g_post, g_pre], [(d, F32), (d, MXU_DTYPE)], [], tm=512, name=name)


def _post_pre_bwd(x_res, m, g_post, g_pre, dx_new, dh, *, name):
    def body(x, m, dxn, dh, gp, gq):
        _, vjp = jax.vjp(_res_block, x, m, gp, gq)
        dx, dm, dgp, dgq = vjp((dxn, dh))
        return (dx, dm), (dgp, dgq)
    d = x_res.shape[1]
    return _rowwise(body, [x_res, m, dx_new, dh], [g_post, g_pre], [(d, F32), (d, MXU_DTYPE)],
                    [g_post.shape, g_pre.shape], tm=256, name=name)


def _final_loss_bwd(x_res, m, g_post, target, *, name):
    d = x_res.shape[1]

    def loss_cols(x, m, g, t):
        err = x + _rms(m, g) - t
        return jnp.sum(err * err, axis=0, keepdims=True) * (0.5 / d)

    def body(x, m, t, g):
        cols, vjp = jax.vjp(lambda x_, m_, g_: loss_cols(x_, m_, g_, t), x, m, g)
        dx, dm, dg = vjp(jnp.ones_like(cols))
        return (dx, dm), (dg, cols)
    return _rowwise(body, [x_res, m, target], [g_post], [(d, F32), (d, MXU_DTYPE)], [g_post.shape, (1, d)], tm=256, name=name)


def _swiglu_act(gu, *, name):
    def body(gate, up):
        return (_silu(gate) * up,), ()
    return _rowwise(body, [_col(gu, D_FF, 0), _col(gu, D_FF, 1)], [], [(D_FF, MXU_DTYPE)], [], tm=256, name=name)[0]


def _swiglu_act_bwd(gu, da, *, name):
    def body(gate, up, da):
        _, vjp = jax.vjp(lambda g, u: _silu(g) * u, gate, up)
        dg, du = vjp(da)
        return (jnp.concatenate([dg, du], axis=1),), ()
    return _rowwise(body, [_col(gu, D_FF, 0), _col(gu, D_FF, 1), da], [], [(2 * D_FF, MXU_DTYPE)], [], tm=256, name=name)[0]


def _lane_head_mask(width, head_dim, head):
    lane = lax.broadcasted_iota(jnp.int32, (1, width), 1)
    return (lane // head_dim) == head


def _mem_attn_pair(q_pair, k_pair, v_pair):
    out = jnp.zeros_like(q_pair)
    for h in range(2):
        mh = _lane_head_mask(LANES, ATT_HEAD_DIM, h)
        qh = jnp.where(mh, q_pair * (ATT_HEAD_DIM ** -0.5), 0.0)
        logits = _dot_nt(qh, k_pair)
        mx = jnp.max(logits, axis=-1, keepdims=True)
        p = jnp.exp(logits - mx)
        p = p / jnp.sum(p, axis=-1, keepdims=True)
        out = out + jnp.where(mh, _dot(p, v_pair), 0.0)
    return out


def _mem_attn(q_mem, kv):
    outs = []
    for p in range(MEM_WIDTH // LANES):
        sl = slice(p * LANES, (p + 1) * LANES)
        outs.append(_mem_attn_pair(q_mem[:, sl], kv[:, sl], kv[:, MEM_WIDTH + p * LANES: MEM_WIDTH + (p + 1) * LANES]))
    return jnp.concatenate(outs, axis=1)


def _mem_attn_bwd(q_mem, kv, do):
    dqs, dks, dvs = [], [], []
    for p in range(MEM_WIDTH // LANES):
        sl = slice(p * LANES, (p + 1) * LANES)
        sv = slice(MEM_WIDTH + p * LANES, MEM_WIDTH + (p + 1) * LANES)
        _, vjp = jax.vjp(_mem_attn_pair, q_mem[:, sl], kv[:, sl], kv[:, sv])
        dq, dk, dv = vjp(do[:, sl])
        dqs.append(dq)
        dks.append(dk)
        dvs.append(dv)
    return jnp.concatenate(dqs, axis=1), jnp.concatenate(dks + dvs, axis=1)


def _t5_bucket(rel):
    half = REL_BUCKETS // 2
    max_exact = half // 2
    n = np.abs(rel)
    large = max_exact + (np.log(np.maximum(n, 1) / max_exact) / math.log(REL_MAX_DIST / max_exact)
                         * (half - max_exact)).astype(np.int64)
    large = np.minimum(large, half - 1)
    return ((rel > 0) * half + np.where(n < max_exact, n, large)).astype(np.int32)


ATT_DIAGS = ATT_BQ + ATT_W - 1


def _bias_diag_onehot(dil):
    j = np.arange(ATT_DIAGS)
    tiles = []
    for off in (-HALF, 0, HALF):
        rel = j - (ATT_BQ - 1) - HALF - off
        hot = _t5_bucket(rel * dil)[:, None] == np.arange(REL_BUCKETS)[None, :]
        tiles.append(hot & (np.abs(rel) <= HALF)[:, None])
    return np.stack(tiles).astype(np.float32)


def _toeplitz(r):
    lead = r.shape[:-1]
    a = jnp.broadcast_to(r[..., None, :], lead + (ATT_BQ, ATT_DIAGS))
    a = jnp.pad(a, [(0, 0)] * len(lead) + [(0, 0), (0, 1)])
    a = a.reshape(lead + (ATT_BQ * (ATT_DIAGS + 1),))[..., : ATT_BQ * ATT_DIAGS].reshape(lead + (ATT_BQ, ATT_DIAGS))
    return a[..., ATT_BQ - 1: ATT_BQ - 1 + ATT_W]


def _bias_tiles(rel_bias, gi):
    heads = rel_bias[:, 4 * gi: 4 * gi + 4]
    diag = jnp.einsum('tnb,bh->thn', jnp.asarray(_bias_diag_onehot(DILATIONS[gi])), heads, precision=HI)
    return _toeplitz(diag)


def _bias_tiles_bwd(rel_bias, dtiles, gi):
    return jax.vjp(lambda rb: _bias_tiles(rb, gi), rel_bias)[1](dtiles)[0]


def _att_window(i, n_sub):
    start = jnp.clip(i * ATT_BQ - HALF, 0, n_sub - ATT_W)
    off = i * ATT_BQ - HALF - start
    return pl.multiple_of(start, HALF), off


def _att_valid(off):
    q = lax.broadcasted_iota(jnp.int32, (ATT_BQ, ATT_W), 0)
    kk = lax.broadcasted_iota(jnp.int32, (ATT_BQ, ATT_W), 1)
    return jnp.abs(kk - q - HALF - off) <= HALF


def _att_tile_id(i, nq):
    return jnp.where(i == 0, 0, jnp.where(i == nq - 1, 2, 1))


def _att_fwd(qkvm, bias, gi, *, name):
    dil = DILATIONS[gi]
    s_len = qkvm.shape[0]
    n_sub = s_len // dil
    nq = n_sub // ATT_BQ
    assert n_sub % ATT_BQ == 0 and n_sub >= ATT_W
    cols = qkvm.shape[1] // LANES
    view = qkvm.reshape(n_sub, dil * qkvm.shape[1])

    def kern(q_ref, k_ref, v_ref, b_ref, o_ref, lse_ref):
        i = pl.program_id(2)
        start, off = _att_window(i, n_sub)
        valid = _att_valid(off)
        q = q_ref[...] * (ATT_HEAD_DIM ** -0.5)
        kw = k_ref[pl.ds(start, ATT_W), :]
        vw = v_ref[pl.ds(start, ATT_W), :]
        o = jnp.zeros((ATT_BQ, LANES), F32)
        lse = jnp.zeros((ATT_BQ, LANES), F32)
        for h in range(2):
            mh = _lane_head_mask(LANES, ATT_HEAD_DIM, h)
            s = _dot_nt(jnp.where(mh, q, 0.0), kw) + b_ref[h]
            s = jnp.where(valid, s, NEG_INF)
            mx = jnp.max(s, axis=-1, keepdims=True)
            p = jnp.exp(s - mx)
            den = jnp.sum(p, axis=-1, keepdims=True)
            o = jnp.where(mh, _dot(p, vw) / den, o)
            lse = jnp.where(mh, mx + jnp.log(den), lse)
        o_ref[...] = o
        lse_ref[...] = lse

    def qkv_spec(which, full):
        shape = (n_sub, LANES) if full else (ATT_BQ, LANES)
        return pl.BlockSpec(shape, lambda pr, r, i: (0 if full else i, r * cols + which * 6 + 2 * gi + pr))

    out_spec = pl.BlockSpec((ATT_BQ, LANES), lambda pr, r, i: (i, r * 2 + pr))
    o, lse = pl.pallas_call(
        kern, name=name, grid=(2, dil, nq),
        in_specs=[qkv_spec(0, False), qkv_spec(1, True), qkv_spec(2, True),
                  pl.BlockSpec((None, 2, ATT_BQ, ATT_W), lambda pr, r, i: (_att_tile_id(i, nq), pr, 0, 0))],
        out_specs=[out_spec, out_spec],
        out_shape=[jax.ShapeDtypeStruct((n_sub, dil * 2 * LANES), F32)] * 2,
        compiler_params=_params("parallel", "parallel", "arbitrary"),
    )(view, view, view, bias)
    return o.reshape(s_len, 2 * LANES), lse.reshape(s_len, 2 * LANES)


def _att_bwd(qkvm, bias, lse_tot, delta, dcat, gi, *, name):
    dil = DILATIONS[gi]
    s_len = qkvm.shape[0]
    n_sub = s_len // dil
    nq = n_sub // ATT_BQ
    cols = qkvm.shape[1] // LANES
    dcols = dcat.shape[1] // LANES
    view = qkvm.reshape(n_sub, dil * qkvm.shape[1])
    lse_v = lse_tot.reshape(n_sub, dil * 2 * LANES)
    delta_v = delta.reshape(n_sub, dil * 2 * LANES)
    dcat_v = dcat.reshape(n_sub, dil * dcat.shape[1])

    def kern(q_ref, k_ref, v_ref, b_ref, lse_ref, dl_ref, dm_ref, dq_ref, dk_ref, dv_ref, db_ref):
        r, i = pl.program_id(1), pl.program_id(2)
        start, off = _att_window(i, n_sub)
        valid = _att_valid(off)
        tile = _att_tile_id(i, nq)

        @pl.when(i == 0)
        def _():
            dk_ref[...] = jnp.zeros_like(dk_ref)
            dv_ref[...] = jnp.zeros_like(dv_ref)

        @pl.when((i == 0) & (r == 0))
        def _():
            db_ref[...] = jnp.zeros_like(db_ref)

        q = q_ref[...] * (ATT_HEAD_DIM ** -0.5)
        kw = k_ref[pl.ds(start, ATT_W), :]
        vw = v_ref[pl.ds(start, ATT_W), :]
        dm = dm_ref[...]
        lse = lse_ref[...]
        dl = dl_ref[...]
        dq = jnp.zeros((ATT_BQ, LANES), F32)
        dkw = jnp.zeros((ATT_W, LANES), F32)
        dvw = jnp.zeros((ATT_W, LANES), F32)
        for h in range(2):
            mh = _lane_head_mask(LANES, ATT_HEAD_DIM, h)
            qh = jnp.where(mh, q, 0.0)
            dmh = jnp.where(mh, dm, 0.0)
            s = _dot_nt(qh, kw) + b_ref[tile, h]
            s = jnp.where(valid, s, NEG_INF)
            lse_h = jnp.max(jnp.where(mh, lse, NEG_INF), axis=-1, keepdims=True)
            dl_h = jnp.max(jnp.where(mh, dl, NEG_INF), axis=-1, keepdims=True)
            p = jnp.exp(s - lse_h)
            ds = p * (_dot_nt(dmh, vw) - dl_h)
            dq = dq + jnp.where(mh, _dot(ds, kw), 0.0)
            dkw = dkw + _dot_tn(ds, qh)
            dvw = dvw + _dot_tn(p, dmh)
            db_ref[tile, h] += ds
        dq_ref[...] = dq * (ATT_HEAD_DIM ** -0.5)
        dk_ref[pl.ds(start, ATT_W), :] += dkw
        dv_ref[pl.ds(start, ATT_W), :] += dvw

    def qkv_spec(which, full):
        shape = (n_sub, LANES) if full else (ATT_BQ, LANES)
        return pl.BlockSpec(shape, lambda pr, r, i: (0 if full else i, r * cols + which * 6 + 2 * gi + pr))

    blk = pl.BlockSpec((ATT_BQ, LANES), lambda pr, r, i: (i, r * 2 + pr))
    full = pl.BlockSpec((n_sub, LANES), lambda pr, r, i: (0, r * 2 + pr))
    bias_spec = pl.BlockSpec((3, 2, ATT_BQ, ATT_W), lambda pr, r, i: (0, pr, 0, 0))
    sub = jax.ShapeDtypeStruct((n_sub, dil * 2 * LANES), F32)
    dq, dk, dv, db = pl.pallas_call(
        kern, name=name, grid=(2, dil, nq),
        in_specs=[qkv_spec(0, False), qkv_spec(1, True), qkv_spec(2, True), bias_spec, blk, blk,
                  pl.BlockSpec((ATT_BQ, LANES), lambda pr, r, i: (i, r * dcols + 2 * gi + pr))],
        out_specs=[blk, full, full, bias_spec],
        out_shape=[sub, sub, sub, jax.ShapeDtypeStruct(bias.shape, F32)],
        compiler_params=_params("arbitrary", "arbitrary", "arbitrary"),
    )(view, view, view, bias, lse_v, delta_v, dcat_v)
    return dq.reshape(s_len, -1), dk.reshape(s_len, -1), dv.reshape(s_len, -1), db


def _att_combine(o_g, lse_g, qkvm, kv_mem, *, name):
    def body(o0, o1, o2, l0, l1, l2, qm, kv):
        mx = jnp.maximum(jnp.maximum(l0, l1), l2)
        tot = mx + jnp.log(jnp.exp(l0 - mx) + jnp.exp(l1 - mx) + jnp.exp(l2 - mx))
        mixed = [o * jnp.exp(l - tot) for o, l in ((o0, l0), (o1, l1), (o2, l2))]
        return (jnp.concatenate(mixed + [_mem_attn(qm, kv)], axis=1), tot), ()
    return _rowwise(body, list(o_g) + list(lse_g) + [_col(qkvm, MEM_WIDTH, (3 * TOK_WIDTH) // MEM_WIDTH)], [kv_mem],
                    [(D_MODEL, F32), (MEM_WIDTH, F32)], [], tm=256, name=name)


def _head_sum_matrix():
    a = np.arange(MEM_WIDTH)
    return jnp.asarray((a[:, None] // ATT_HEAD_DIM == a[None, :] // ATT_HEAD_DIM).astype(np.float32))


def _att_bwd_prep(cat, dcat, qkvm, kv_mem, *, name):
    def body(cat, dcat, qm, kv, hs):
        prod = cat * dcat
        summed = prod[:, 0:256] + prod[:, 256:512] + prod[:, 512:768]
        delta = _dot(summed, hs, precision=HI)
        dqm, dkv = _mem_attn_bwd(qm, kv, dcat[:, TOK_WIDTH:])
        return (delta, dqm), (dkv,)
    return _rowwise(body, [cat, dcat, _col(qkvm, MEM_WIDTH, (3 * TOK_WIDTH) // MEM_WIDTH)], [kv_mem, _head_sum_matrix()],
                    [(MEM_WIDTH, F32), (MEM_WIDTH, F32)], [kv_mem.shape], tm=256, name=name)


def _dn_conv_post(s, j):
    scale = jnp.where(j < DN_HEADS, DN_HEAD_DIM ** -0.5, 1.0)
    normed = s * lax.rsqrt(jnp.sum(s * s, axis=-1, keepdims=True) + EPS) * scale
    return jnp.where(j >= 2 * DN_HEADS, s, normed)


def _shift_rows(x, sh):
    n = x.shape[0]
    row = lax.broadcasted_iota(jnp.int32, (n, 1), 0)
    rolled = pltpu.roll(x, (-sh) % n, 0)
    return jnp.where((row + sh >= 0) & (row + sh < n), rolled, 0.0)


def _dn_conv_taps(x, w_ref):
    c = x * w_ref[pl.ds(DN_CONV // 2, 1), :]
    for jj in range(DN_CONV):
        if jj != DN_CONV // 2:
            c = c + _shift_rows(x, jj - DN_CONV // 2) * w_ref[pl.ds(jj, 1), :]
    return c


def _dn_conv_fwd(proj, conv_w, *, name):
    s_len = proj.shape[0]
    width = 3 * TOK_WIDTH

    def kern(x_ref, w_ref, o_ref):
        j = pl.program_id(0)
        o_ref[...] = _dn_conv_post(_silu(_dn_conv_taps(x_ref[...], w_ref)), j)

    return pl.pallas_call(
        kern, name=name, grid=(width // LANES,),
        in_specs=[pl.BlockSpec((s_len, LANES), lambda j: (0, j)), pl.BlockSpec((DN_CONV, LANES), lambda j: (0, j))],
        out_specs=pl.BlockSpec((s_len, LANES), lambda j: (0, j)),
        out_shape=jax.ShapeDtypeStruct((s_len, width), F32),
        compiler_params=_params("parallel"),
    )(proj, conv_w)


def _dn_conv_bwd(proj, conv_w, d_fwd, d_bwd, which, *, name):
    s_len = proj.shape[0]

    def kern(x_ref, w_ref, df_ref, db_ref, dx_ref, dw_ref):
        j = pl.program_id(0) + which * DN_HEADS
        x = x_ref[...]
        c = _dn_conv_taps(x, w_ref)
        _, vjp = jax.vjp(lambda c_: _dn_conv_post(_silu(c_), j), c)
        dc = vjp(df_ref[...] + db_ref[...])[0]
        dx = dc * w_ref[pl.ds(DN_CONV // 2, 1), :]
        for jj in range(DN_CONV):
            sh = jj - DN_CONV // 2
            if sh != 0:
                dx = dx + _shift_rows(dc, -sh) * w_ref[pl.ds(jj, 1), :]
            dw_ref[pl.ds(jj, 1), :] = jnp.sum(dc * _shift_rows(x, sh), axis=0, keepdims=True)
        dx_ref[...] = dx

    return pl.pallas_call(
        kern, name=name, grid=(DN_HEADS,),
        in_specs=[pl.BlockSpec((s_len, LANES), lambda j: (0, j + which * DN_HEADS)),
                  pl.BlockSpec((DN_CONV, LANES), lambda j: (0, j + which * DN_HEADS)),
                  pl.BlockSpec((s_len, LANES), lambda j: (0, j)),
                  pl.BlockSpec((s_len, LANES), lambda j: (0, j))],
        out_specs=[pl.BlockSpec((s_len, LANES), lambda j: (0, j)), pl.BlockSpec((DN_CONV, LANES), lambda j: (0, j))],
        out_shape=[jax.ShapeDtypeStruct((s_len, TOK_WIDTH), F32), jax.ShapeDtypeStruct((DN_CONV, TOK_WIDTH), F32)],
        compiler_params=_params("parallel"),
    )(proj, conv_w, d_fwd, d_bwd)


GATE_TM = 2 * DN_CHUNK


FWD_GATE_LANES = 2 * DN_HEADS


def _gate_constants():
    i = np.arange(GATE_TM)
    same = (i[:, None] // DN_CHUNK) == (i[None, :] // DN_CHUNK)
    cum_f = same & (i[None, :] <= i[:, None])
    cum_r = same & (i[None, :] >= i[:, None])
    return tuple(jnp.asarray(np.asarray(a, np.float32)) for a in (cum_f, cum_r, same))


def _gate_params(p):
    z = jnp.zeros((DN_HEADS,), F32)
    return jnp.concatenate([p[0], z, p[1], z, jnp.zeros((LANES - N_GATES,), F32)]).reshape(1, LANES)


def _gate_params_bwd(dp):
    return jnp.stack([dp[0, 0:DN_HEADS], dp[0, 2 * DN_HEADS: 3 * DN_HEADS]])


def _dn_gates(gate_in, a_cols, dt_cols, cum_f, cum_r, tot):
    g = -jnp.exp(a_cols) * _softplus(gate_in + dt_cols)
    fwd_lane = lax.broadcasted_iota(jnp.int32, (1, LANES), 1) < FWD_GATE_LANES
    gc = jnp.where(fwd_lane, _dot(cum_f, g, precision=HI), _dot(cum_r, g, precision=HI))
    return gc, _dot(tot, g, precision=HI), jax.nn.sigmoid(gate_in)


def _dn_gates_fwd(proj, a_cols, dt_cols, *, name):
    def body(gi, *consts):
        return _dn_gates(gi, *consts), ()
    return _rowwise(body, [_col(proj, LANES, DN_IN_PAD // LANES - 1)], [a_cols, dt_cols, *_gate_constants()],
                    [(LANES, F32)] * 3, [], tm=GATE_TM, name=name)


def _dn_gates_bwd(proj, a_cols, dt_cols, d_gates, *, name):
    def body(gi, gcf, gtf, bf, gcr, gtr, br, a, dt, *consts):
        _, vjp = jax.vjp(lambda gi_, a_, dt_: _dn_gates(gi_, a_, dt_, *consts), gi, a, dt)
        dgi, da, ddt = vjp((gcf + gcr, gtf + gtr, bf + br))
        return (dgi,), (da, ddt)
    return _rowwise(body, [_col(proj, LANES, DN_IN_PAD // LANES - 1), *d_gates[0], *d_gates[1]],
                    [a_cols, dt_cols, *_gate_constants()], [(LANES, F32)], [a_cols.shape, dt_cols.shape],
                    tm=GATE_TM, name=name)


INV_BASE = 8


def _block_id_equal(c, size):
    i = lax.broadcasted_iota(jnp.int32, (c, c), 0) // size
    j = lax.broadcasted_iota(jnp.int32, (c, c), 1) // size
    return (i == j).astype(F32)


def _unit_tri_inverse_impl(lmat):
    c = lmat.shape[0]
    eye = _block_id_equal(c, 1)
    same = _block_id_equal(c, INV_BASE)
    neg = -lmat * same
    inv = eye + neg
    power = neg
    for _ in range(int(math.log2(INV_BASE)) - 1):
        power = _dot(power, power)
        inv = inv + _dot(inv, power)
    size = INV_BASE
    while size < c:
        bigger = _block_id_equal(c, 2 * size)
        inv = inv - _dot(_dot(inv, lmat * (bigger - same)), inv)
        same, size = bigger, 2 * size
    resid = eye - _dot(eye + lmat, inv, precision=HI)
    return inv + _dot(inv, resid)


@jax.custom_vjp
def _unit_tri_inverse(lmat):
    return _unit_tri_inverse_impl(lmat)


def _unit_tri_inverse_fwd(lmat):
    inv = _unit_tri_inverse_impl(lmat)
    return inv, inv


def _unit_tri_inverse_bwd(inv, d_inv):
    return (-_dot_tn(inv, _dot_nt(d_inv, inv)),)


_unit_tri_inverse.defvjp(_unit_tri_inverse_fwd, _unit_tri_inverse_bwd)


def _dn_chunk(q, k, v, gates_t, gc_row, tot_row, beta_row, state, tri, inverse):
    c = q.shape[0]
    assert c == DN_HEAD_DIM
    eye = _block_id_equal(c, 1)

    def along_rows(x, pick):
        return jnp.broadcast_to(jnp.sum(x * pick, axis=0, keepdims=True), (c, c))

    gc_j = along_rows(gates_t[0], gc_row)
    gc = gc_j.T
    g_tot = along_rows(gates_t[1], tot_row)
    beta = along_rows(gates_t[2], beta_row).T
    decay = jnp.exp(jnp.where(tri > 0, gc - gc_j, NEG_INF))
    k_beta = k * beta
    inv = inverse((tri - eye) * (_dot_nt(k_beta, k) * decay))
    e_gc = jnp.exp(gc)
    u = _dot(inv, v * beta)
    w = _dot(inv, k_beta * e_gc)
    intra = tri * (_dot_nt(q, k) * decay)
    v_new = u - _dot(w, state)
    out = _dot(q * e_gc, state) + _dot(intra, v_new)
    state = state * jnp.exp(g_tot) + _dot_tn(k * jnp.exp(g_tot - gc), v_new)
    return out, state


def _dn_tri():
    i = np.arange(DN_CHUNK)
    tri = np.stack([(i[None, :] <= i[:, None]), (i[None, :] >= i[:, None])]).astype(np.float32)
    return jnp.asarray(np.repeat(tri, DN_HEADS, axis=0))


def _dn_gate_picks():
    picks = np.zeros((3, 2 * DN_HEADS, 2 * DN_CHUNK, 1), np.float32)
    for d in range(2):
        for h in range(DN_HEADS):
            alpha = d * DN_CHUNK + d * 2 * DN_HEADS + h
            picks[0, d * DN_HEADS + h, alpha] = 1.0
            picks[1, d * DN_HEADS + h, alpha] = 1.0
            picks[2, d * DN_HEADS + h, alpha + DN_HEADS] = 1.0
    return jnp.asarray(picks)


def _stack_chains(fwd_ref, rev_ref):
    return jnp.stack([r[:, _head_cols(h)] for r in (fwd_ref, rev_ref) for h in range(DN_HEADS)])


def _unstack_chains(val, fwd_ref, rev_ref):
    for d, r in enumerate((fwd_ref, rev_ref)):
        for h in range(DN_HEADS):
            r[:, _head_cols(h)] = val[d * DN_HEADS + h]


def _gates_transposed(fwd_refs, rev_refs):
    return jnp.stack([jnp.concatenate([f[...].T, r[...].T], axis=0) for f, r in zip(fwd_refs, rev_refs, strict=True)])


def _dn_row_spec(nc, col, reverse, width=TOK_WIDTH):
    return pl.BlockSpec((DN_CHUNK, width), lambda t: ((nc - 1 - t) if reverse else t, col))


def _dn_state_spec(nc, reverse):
    return pl.BlockSpec((None, DN_HEADS, DN_HEAD_DIM, DN_HEAD_DIM), lambda t: ((nc - 1 - t) if reverse else t, 0, 0, 0))


def _head_cols(h):
    return pl.ds(h * DN_HEAD_DIM, DN_HEAD_DIM)


def _const_spec(arr):
    return pl.BlockSpec(arr.shape, functools.partial(lambda t, n: (0,) * n, n=arr.ndim))


def _dn_chains(inverse):
    return jax.vmap(lambda q, k, v, gates_t, *rest: _dn_chunk(q, k, v, gates_t, *rest, inverse),
                    in_axes=(0, 0, 0, None, 0, 0, 0, 0, 0))


def _dn_scan_fwd(qkv, gates, *, name):
    s_len = qkv.shape[0]
    nc = s_len // DN_CHUNK
    tri, picks = _dn_tri(), _dn_gate_picks()

    def kern(*refs):
        ins, (tri_ref, pick_ref, of_ref, or_ref, sf_ref, sr_ref, state) = refs[:12], refs[12:]

        @pl.when(pl.program_id(0) == 0)
        def _():
            state[...] = jnp.zeros_like(state)

        entry = state[...]
        qkv_c = [_stack_chains(ins[i], ins[6 + i]) for i in range(3)]
        gates_t = _gates_transposed(ins[3:6], ins[9:12])
        out, new = _dn_chains(_unit_tri_inverse_impl)(*qkv_c, gates_t, pick_ref[0], pick_ref[1], pick_ref[2], entry, tri_ref[...])
        sf_ref[...] = entry[:DN_HEADS]
        sr_ref[...] = entry[DN_HEADS:]
        _unstack_chains(out, of_ref, or_ref)
        state[...] = new

    in_specs = []
    for rev in (False, True):
        in_specs += [_dn_row_spec(nc, col, rev) for col in (0, 1, 2)] + [_dn_row_spec(nc, 0, rev, LANES)] * 3
    in_specs += [_const_spec(tri), _const_spec(picks)]
    return pl.pallas_call(
        kern, name=name, grid=(nc,), in_specs=in_specs,
        out_specs=[_dn_row_spec(nc, 0, False), _dn_row_spec(nc, 0, True), _dn_state_spec(nc, False), _dn_state_spec(nc, True)],
        out_shape=[jax.ShapeDtypeStruct((s_len, TOK_WIDTH), F32)] * 2
        + [jax.ShapeDtypeStruct((nc, DN_HEADS, DN_HEAD_DIM, DN_HEAD_DIM), F32)] * 2,
        scratch_shapes=[pltpu.VMEM((2 * DN_HEADS, DN_HEAD_DIM, DN_HEAD_DIM), F32)],
        compiler_params=_params("arbitrary"),
    )(*([qkv, qkv, qkv, *gates] * 2), tri, picks)


def _dn_scan_bwd(qkv, gates, states, d_o, *, name):
    s_len = qkv.shape[0]
    nc = s_len // DN_CHUNK
    tri, picks = _dn_tri(), _dn_gate_picks()

    def kern(*refs):
        ins, tri_ref, pick_ref, outs, d_state = refs[:16], refs[16], refs[17], refs[18:30], refs[30]

        @pl.when(pl.program_id(0) == 0)
        def _():
            d_state[...] = jnp.zeros_like(d_state)

        qkv_c = [_stack_chains(ins[i], ins[8 + i]) for i in range(3)]
        gates_t = _gates_transposed(ins[3:6], ins[11:14])
        entry = jnp.concatenate([ins[6][...], ins[14][...]], axis=0)
        d_out = _stack_chains(ins[7], ins[15])
        tri_v, picks_v = tri_ref[...], pick_ref[...]
        _, vjp = jax.vjp(lambda q, k, v, g, s: _dn_chains(_unit_tri_inverse)(q, k, v, g, picks_v[0], picks_v[1], picks_v[2], s, tri_v),
                         *qkv_c, gates_t, entry)
        dq, dk, dv, d_gates_t, d_entry = vjp((d_out, d_state[...]))
        for i, val in enumerate((dq, dk, dv)):
            _unstack_chains(val, outs[i], outs[6 + i])
        for i in range(3):
            outs[3 + i][...] = d_gates_t[i, :DN_CHUNK].T
            outs[9 + i][...] = d_gates_t[i, DN_CHUNK:].T
        d_state[...] = d_entry

    in_specs, out_specs, out_shape = [], [], []
    for rev in (True, False):
        in_specs += [_dn_row_spec(nc, col, rev) for col in (0, 1, 2)] + [_dn_row_spec(nc, 0, rev, LANES)] * 3
        in_specs += [_dn_state_spec(nc, rev), _dn_row_spec(nc, 0, rev)]
        out_specs += [_dn_row_spec(nc, 0, rev)] * 3 + [_dn_row_spec(nc, 0, rev, LANES)] * 3
        out_shape += [jax.ShapeDtypeStruct((s_len, TOK_WIDTH), F32)] * 3 + [jax.ShapeDtypeStruct((s_len, LANES), F32)] * 3
    in_specs += [_const_spec(tri), _const_spec(picks)]
    res = pl.pallas_call(
        kern, name=name, grid=(nc,), in_specs=in_specs, out_specs=out_specs, out_shape=out_shape,
        scratch_shapes=[pltpu.VMEM((2 * DN_HEADS, DN_HEAD_DIM, DN_HEAD_DIM), F32)],
        compiler_params=_params("arbitrary"),
    )(*[a for d in range(2) for a in (qkv, qkv, qkv, *gates, states[d], d_o)], tri, picks)
    return (res[0:3], res[3:6]), (res[6:9], res[9:12])


def _dn_out_head(o_f, o_b, z, gain):
    o = o_f + o_b
    return o * lax.rsqrt(jnp.mean(o * o, axis=-1, keepdims=True) + EPS) * gain * _silu(z)


def _dn_out(o_fwd, o_rev, proj, gain, qkv_kv_mem, *, name):
    def body(of, ob, z, qm, g, kv):
        heads = []
        for h in range(DN_HEADS):
            sl = slice(h * DN_HEAD_DIM, (h + 1) * DN_HEAD_DIM)
            heads.append(_dn_out_head(of[:, sl], ob[:, sl], z[:, sl], g))
        return (jnp.concatenate(heads + [_mem_attn(qm, kv)], axis=1),), ()
    return _rowwise(body, [o_fwd, o_rev, _col(proj, TOK_WIDTH, 3),
                           _col(proj, MEM_WIDTH, (4 * TOK_WIDTH) // MEM_WIDTH)], [gain, qkv_kv_mem],
                    [(D_MODEL, MXU_DTYPE)], [], tm=256, name=name)[0]


def _dn_out_bwd(o_fwd, o_rev, proj, gain, kv_mem, dcat, *, name):
    def body(of, ob, z, qm, dcat, g, kv):
        dos, dzs = [], []
        dgain = jnp.zeros_like(g)
        for h in range(DN_HEADS):
            sl = slice(h * DN_HEAD_DIM, (h + 1) * DN_HEAD_DIM)
            _, vjp = jax.vjp(_dn_out_head, of[:, sl], ob[:, sl], z[:, sl], g)
            d_of, _, dz, dg = vjp(dcat[:, sl])
            dos.append(d_of)
            dzs.append(dz)
            dgain = dgain + dg
        dqm, dkv = _mem_attn_bwd(qm, kv, dcat[:, TOK_WIDTH:])
        return (jnp.concatenate(dos, axis=1), jnp.concatenate(dzs, axis=1), dqm), (dgain, dkv)
    return _rowwise(body, [o_fwd, o_rev, _col(proj, TOK_WIDTH, 3),
                           _col(proj, MEM_WIDTH, (4 * TOK_WIDTH) // MEM_WIDTH), dcat], [gain, kv_mem],
                    [(TOK_WIDTH, F32), (TOK_WIDTH, F32), (MEM_WIDTH, F32)], [gain.shape, kv_mem.shape], tm=256, name=name)


def _pad_dn_w_in(w):
    gates = w[:, 4 * TOK_WIDTH: 4 * TOK_WIDTH + N_GATES]
    zeros = jnp.zeros((w.shape[0], DN_IN_PAD - DN_IN), w.dtype)
    return jnp.concatenate([w[:, :4 * TOK_WIDTH], w[:, 4 * TOK_WIDTH + N_GATES:], gates, zeros], axis=1)


def _unpad_dn_w_in(w):
    q_mem = w[:, 4 * TOK_WIDTH: 4 * TOK_WIDTH + MEM_WIDTH]
    gates = w[:, 4 * TOK_WIDTH + MEM_WIDTH: 4 * TOK_WIDTH + MEM_WIDTH + N_GATES]
    return jnp.concatenate([w[:, :4 * TOK_WIDTH], gates, q_mem], axis=1)


def _ffn_fwd(h, w_gu, w_d, tag):
    gu = _mm(h, w_gu, name=f"ffn_gu_{tag}")
    act = _swiglu_act(gu, name=f"ffn_act_{tag}")
    return gu, act, _mm(act, w_d, name=f"ffn_down_{tag}")


def _ffn_bwd(h, gu, act, w_gu, w_d, df, tag):
    d_act = _mm(df, w_d, tb=True, name=f"ffn_dact_{tag}")
    d_wd = _mm(act, df, ta=True, name=f"ffn_dwd_{tag}")
    d_gu = _swiglu_act_bwd(gu, d_act, name=f"ffn_dgu_{tag}")
    dh = _mm(d_gu, w_gu, tb=True, name=f"ffn_dh_{tag}")
    d_wgu = _mm(h, d_gu, ta=True, out_shards=_shards_of(w_gu), name=f"ffn_dwgu_{tag}")
    return dh, d_wgu, d_wd


def _local_step(x, mem, target, p):
    g = {}
    row = lambda v: v.reshape(1, -1)
    gains = {k: [row(p[k][i]) for i in range(2)] for k in
             ("mem_norm", "norm_mix_pre", "norm_mix_post", "norm_ffn_pre", "norm_ffn_post")}
    out_gain = row(p["dn_out_norm"])
    a_cols, dt_cols = _gate_params(p["dn_a_log"]), _gate_params(p["dn_dt_bias"])

    h0 = _pre_norm(x, gains["norm_mix_pre"][0], name="pre0")
    mem_n = [_pre_norm(mem, gains["mem_norm"][i], name=f"mem_norm{i}") for i in range(2)]
    kv_mem = [_mm(mem_n[i], p["mem_w_kv"][i], name=f"mem_kv{i}") for i in range(2)]
    qkvm = _mm(h0, p["att_w_in"], name="att_in")
    bias = [_bias_tiles(p["rel_bias"], gi) for gi in range(3)]
    att = [_att_fwd(qkvm, bias[gi], gi, name=f"att_fwd{gi}") for gi in range(3)]
    cat0, lse_tot = _att_combine([a[0] for a in att], [a[1] for a in att], qkvm, kv_mem[0], name="att_combine")
    mo0 = _mm(cat0, p["att_w_out"], name="att_out")
    x1, h1 = _post_pre(x, mo0, gains["norm_mix_post"][0], gains["norm_ffn_pre"][0], name="post_mix0")
    w_gu0, w_d0 = p["ffn_weights"](0, h1)
    gu0, act0, f0 = _ffn_fwd(h1, w_gu0, w_d0, 0)
    x2, h2 = _post_pre(x1, f0, gains["norm_ffn_post"][0], gains["norm_mix_pre"][1], name="post_ffn0")

    dn_w_in, dn_w_out = p["dn_weights"](h2)
    proj = _mm(h2, dn_w_in, name="dn_in")
    qkv = _dn_conv_fwd(proj, p["dn_conv"], name="dn_conv")
    gates = _dn_gates_fwd(proj, a_cols, dt_cols, name="dn_gates")
    o_fwd, o_rev, st_fwd, st_rev = _dn_scan_fwd(qkv, gates, name="dn_scan")
    cat1 = _dn_out(o_fwd, o_rev, proj, out_gain, kv_mem[1], name="dn_outnorm")
    mo1 = _mm(cat1, dn_w_out, name="dn_out")
    x3, h3 = _post_pre(x2, mo1, gains["norm_mix_post"][1], gains["norm_ffn_pre"][1], name="post_mix1")
    w_gu1, w_d1 = p["ffn_weights"](1, h3)
    gu1, act1, f1 = _ffn_fwd(h3, w_gu1, w_d1, 1)

    dx3, df1, dg_ffn_post1, loss_cols = _final_loss_bwd(x3, f1, gains["norm_ffn_post"][1], target, name="loss_bwd")
    dh3, d_wgu1, d_wd1 = _ffn_bwd(h3, gu1, act1, w_gu1, w_d1, df1, 1)
    dx2, dmo1, dg_mix_post1, dg_ffn_pre1 = _post_pre_bwd(x2, mo1, gains["norm_mix_post"][1], gains["norm_ffn_pre"][1],
                                                         dx3, dh3, name="post_mix1_bwd")
    dcat1 = _mm(dmo1, dn_w_out, tb=True, name="dn_out_dx")
    g["dn_w_out"] = _mm(cat1, dmo1, ta=True, name="dn_out_dw")
    d_o, dz, dqm1, d_out_gain, dkv1 = _dn_out_bwd(o_fwd, o_rev, proj, out_gain, kv_mem[1], dcat1, name="dn_outnorm_bwd")
    (d_f, dg_f), (d_r, dg_r) = _dn_scan_bwd(qkv, gates, (st_fwd, st_rev), d_o, name="dn_scan_bwd")
    d_gate_cols, d_a_cols, d_dt_cols = _dn_gates_bwd(proj, a_cols, dt_cols, (dg_f, dg_r), name="dn_gates_bwd")
    d_pre, d_conv = zip(*[_dn_conv_bwd(proj, p["dn_conv"], d_f[which], d_r[which], which, name=f"dn_conv_bwd{which}")
                          for which in range(3)])
    dproj = jnp.concatenate(list(d_pre) + [dz, dqm1, d_gate_cols], axis=1).astype(MXU_DTYPE)
    dh2 = _mm(dproj, dn_w_in, tb=True, name="dn_in_dx")
    g["dn_w_in"] = _mm(h2, dproj, ta=True, name="dn_in_dw")
    g["dn_conv"] = jnp.concatenate(d_conv, axis=1)
    g["dn_a_log"] = _gate_params_bwd(d_a_cols)
    g["dn_dt_bias"] = _gate_params_bwd(d_dt_cols)
    g["dn_out_norm"] = d_out_gain

    dx1, df0, dg_ffn_post0, dg_mix_pre1 = _post_pre_bwd(x1, f0, gains["norm_ffn_post"][0], gains["norm_mix_pre"][1],
                                                        dx2, dh2, name="post_ffn0_bwd")
    dh1, d_wgu0, d_wd0 = _ffn_bwd(h1, gu0, act0, w_gu0, w_d0, df0, 0)
    dx0, dmo0, dg_mix_post0, dg_ffn_pre0 = _post_pre_bwd(x, mo0, gains["norm_mix_post"][0], gains["norm_ffn_pre"][0],
                                                         dx1, dh1, name="post_mix0_bwd")
    dcat0 = _mm(dmo0, p["att_w_out"], tb=True, name="att_out_dx")
    g["att_w_out"] = _mm(cat0, dmo0, ta=True, name="att_out_dw")
    delta, dqm0, dkv0 = _att_bwd_prep(cat0, dcat0, qkvm, kv_mem[0], name="att_bwd_prep")
    att_b = [_att_bwd(qkvm, bias[gi], lse_tot, delta, dcat0, gi, name=f"att_bwd{gi}") for gi in range(3)]
    dqkvm = jnp.concatenate([a[w] for w in range(3) for a in att_b] + [dqm0], axis=1).astype(MXU_DTYPE)
    g["rel_bias"] = sum(_bias_tiles_bwd(p["rel_bias"], att_b[gi][3], gi) for gi in range(3))
    dh0 = _mm(dqkvm, p["att_w_in"], tb=True, name="att_in_dx")
    g["att_w_in"] = _mm(h0, dqkvm, ta=True, out_shards=_shards_of(p["att_w_in"]), name="att_in_dw")
    grad_x, dg_mix_pre0 = _pre_norm_bwd(x, gains["norm_mix_pre"][0], dh0, dx0, name="pre0_bwd")

    d_mem_kv, d_mem_norm = [], []
    for i, dkv in enumerate((dkv0, dkv1)):
        d_mem_kv.append(_mm(mem_n[i], dkv, ta=True, name=f"mem_kv_dw{i}"))
        d_mem_n = _mm(dkv, p["mem_w_kv"][i], tb=True, name=f"mem_kv_dx{i}")
        d_mem_norm.append(_gain_bwd(mem, gains["mem_norm"][i], d_mem_n, name=f"mem_norm_bwd{i}"))
    g["mem_w_kv"] = d_mem_kv
    g["mem_norm"] = jnp.concatenate(d_mem_norm, axis=0)
    g["norm_mix_pre"] = jnp.concatenate([dg_mix_pre0, dg_mix_pre1], axis=0)
    g["norm_mix_post"] = jnp.concatenate([dg_mix_post0, dg_mix_post1], axis=0)
    g["norm_ffn_pre"] = jnp.concatenate([dg_ffn_pre0, dg_ffn_pre1], axis=0)
    g["norm_ffn_post"] = jnp.concatenate([dg_ffn_post0, dg_ffn_post1], axis=0)
    g["ffn_w_gate_up"] = [d_wgu0, d_wgu1]
    g["ffn_w_down"] = [d_wd0, d_wd1]
    return loss_cols, grad_x, g


N_CHIPS = 4
N_DEV = 8
MESH = pl.DeviceIdType.MESH
BIG = (("att_w_in", (1, 1024, 640), 2), ("att_w_out", (1, 256, 1024), 1), ("dn_w_in", (1, 1024, 838), 2),
       ("dn_w_out", (1, 256, 1024), 1), ("mem_w_kv", (2, 256, 512), 1), ("ffn_w_gate_up", (2, 1024, 1408), 2),
       ("ffn_w_down", (2, 704, 1024), 1))


def _mesh_pos():
    return lax.axis_index("x"), lax.axis_index("y"), lax.axis_index("c")


def _other_chips(x, y):
    return [(1 - x, y), (x, 1 - y), (1 - x, 1 - y)]


ANY = pl.BlockSpec(memory_space=pl.ANY)


def _all_reduce_small(v, *, name):
    rows, cols = v.shape
    flips = [(dx, dy, dc) for dx in (0, 1) for dy in (0, 1) for dc in (0, 1)][1:]

    def body(v_ref, o_ref, buf, send_sems, recv_sems):
        x, y, c = _mesh_pos()

        def peer(f):
            return tuple(1 - p if fl else p for p, fl in zip((x, y, c), f))

        def index(p):
            return 4 * p[0] + 2 * p[1] + p[2]

        buf[index((x, y, c))] = v_ref[...]
        sends = []
        for k, f in enumerate(flips):
            cp = pltpu.make_async_remote_copy(src_ref=v_ref, dst_ref=buf.at[index((x, y, c))], send_sem=send_sems.at[k],
                                              recv_sem=recv_sems.at[k], device_id=peer(f), device_id_type=MESH)
            cp.start()
            sends.append(cp)
        for k, f in enumerate(flips):
            pltpu.make_async_remote_copy(src_ref=v_ref, dst_ref=buf.at[index(peer(f))], send_sem=send_sems.at[k],
                                         recv_sem=recv_sems.at[k], device_id=peer(f), device_id_type=MESH).wait_recv()
        for cp in sends:
            cp.wait_send()
        acc = buf[0]
        for d in range(1, N_DEV):
            acc = acc + buf[d]
        o_ref[...] = acc

    vmem = pl.BlockSpec(memory_space=pltpu.VMEM)
    return pl.pallas_call(
        body, name=name, in_specs=[vmem], out_specs=vmem, out_shape=jax.ShapeDtypeStruct((rows, cols), F32),
        scratch_shapes=[pltpu.VMEM((N_DEV, rows, cols), F32), pltpu.SemaphoreType.DMA((N_DEV - 1,)),
                        pltpu.SemaphoreType.DMA((N_DEV - 1,))],
    )(v)


def _adamw(w, g, m, v, *, name):
    def body(w, g, m, v):
        m = ADAM_B1 * m + (1.0 - ADAM_B1) * g
        v = ADAM_B2 * v + (1.0 - ADAM_B2) * (g * g)
        m_hat = m / (1.0 - ADAM_B1 ** ADAM_STEP)
        v_hat = v / (1.0 - ADAM_B2 ** ADAM_STEP)
        delta = -ADAM_LR * (m_hat / (jnp.sqrt(v_hat) + ADAM_EPS) + ADAM_WD * w)
        return (delta, m, v), ()
    rows, cols = w.shape
    return _rowwise(body, [w, g, m, v], [], [(cols, F32)] * 3, [], tm=_tile(rows, 256, SUBLANES), name=name)


def _pack_small(arrs, rows):
    flat = jnp.concatenate([a.reshape(-1) for a in arrs])
    return jnp.pad(flat, (0, rows * LANES - flat.shape[0])).reshape(rows, LANES)


def _unpack_small(packed, shapes):
    flat = packed.reshape(-1)
    out, off = [], 0
    for s in shapes:
        size = math.prod(s)
        out.append(flat[off: off + size].reshape(s))
        off += size
    return out


def _small_rows(shapes):
    return -(-sum(math.prod(s) for s in shapes) // (SUBLANES * LANES)) * SUBLANES


def _sem_pairs(n):
    return [pltpu.SemaphoreType.DMA((n,)), pltpu.SemaphoreType.DMA((n,))]


def _gather_blocks(blocks, *, name):
    n = len(blocks)

    def body(*refs):
        x_refs, out_refs, (send_sems, recv_sems) = refs[:n], refs[n: 2 * n], refs[2 * n:]
        x, y, c = _mesh_pos()
        sibling = (x, y, 1 - c)
        chips = _other_chips(x, y)

        def copy(k, src, dst, to):
            return pltpu.make_async_remote_copy(src_ref=src, dst_ref=dst, send_sem=send_sems.at[k],
                                                recv_sem=recv_sems.at[k], device_id=to, device_id_type=MESH)

        def part(b, chip, h):
            half = blocks[b].shape[0] // 2
            return out_refs[b].at[2 * chip[0] + chip[1], pl.ds(h * half, half), :]

        def my_half(b):
            half = blocks[b].shape[0] // 2
            return x_refs[b].at[pl.ds(c * half, half), :]

        first = [copy(6 * b + j, my_half(b), part(b, (x, y), c), (*chip, c)) for b in range(n) for j, chip in enumerate(chips)]
        for cp in first:
            cp.start()
        passed = []
        for b in range(n):
            for j, chip in enumerate(chips):
                copy(6 * b + j, my_half(b), part(b, chip, c), (*chip, c)).wait_recv()
                cp = copy(6 * b + 3 + j, part(b, chip, c), part(b, chip, c), sibling)
                cp.start()
                passed.append(cp)
        for b in range(n):
            for j, chip in enumerate(chips):
                copy(6 * b + 3 + j, part(b, chip, 1 - c), part(b, chip, 1 - c), sibling).wait_recv()
        for cp in first + passed:
            cp.wait_send()

    return pl.pallas_call(
        body, name=name, in_specs=[ANY] * n, out_specs=[ANY] * n,
        out_shape=[jax.ShapeDtypeStruct((N_CHIPS, *a.shape), a.dtype) for a in blocks],
        scratch_shapes=_sem_pairs(6 * n),
    )(*blocks)


HBM = pl.BlockSpec(memory_space=pltpu.HBM)
SEM = pl.BlockSpec(memory_space=pltpu.SEMAPHORE)
DATAFLOW = pltpu.SideEffectType.DATAFLOW_SIDE_EFFECTING


def _gather_start(blocks, *, name):
    n = len(blocks)
    lands = [lax.empty((N_CHIPS, *a.shape), a.dtype) for a in blocks]

    def body(*refs):
        x_refs, land_refs, send_sems, recv_sems, token = refs[:n], refs[n: 2 * n], refs[2 * n], refs[2 * n + 1], refs[-1]
        x, y, c = _mesh_pos()
        for b in range(n):
            for j, chip in enumerate(_other_chips(x, y)):
                pltpu.make_async_remote_copy(src_ref=x_refs[b], dst_ref=land_refs[b].at[2 * x + y], send_sem=send_sems.at[3 * b + j],
                                             recv_sem=recv_sems.at[3 * b + j], device_id=(*chip, c), device_id_type=MESH).start()
        token[...] = jnp.zeros_like(token)

    operands = [pltpu.with_memory_space_constraint(a, pltpu.HBM) for a in blocks + lands]
    res = pl.pallas_call(
        body, name=name, in_specs=[HBM] * (2 * n),
        out_shape=(pltpu.SemaphoreType.DMA((3 * n,)), pltpu.SemaphoreType.DMA((3 * n,)),
                   *[pltpu.HBM(a.shape, a.dtype) for a in operands], jax.ShapeDtypeStruct((SUBLANES, LANES), F32)),
        out_specs=(SEM, SEM, *[HBM] * (2 * n), pl.BlockSpec(memory_space=pltpu.VMEM)),
        input_output_aliases={i: 2 + i for i in range(2 * n)},
        compiler_params=pltpu.CompilerParams(has_side_effects=DATAFLOW),
    )(*operands)
    return res[0], res[1], list(res[2: 2 + n]), list(res[2 + n: 2 + 2 * n]), res[-1]


def _gather_wait(started, after, *, name):
    send_sems, recv_sems, blocks, lands, _ = started
    n = len(blocks)

    def body(*refs):
        x_refs, land_refs, send_sems, recv_sems = refs[:n], refs[n: 2 * n], refs[2 * n], refs[2 * n + 1]
        x, y, c = _mesh_pos()
        for b in range(n):
            for j, chip in enumerate(_other_chips(x, y)):
                cp = pltpu.make_async_remote_copy(src_ref=x_refs[b], dst_ref=land_refs[b].at[2 * chip[0] + chip[1]],
                                                  send_sem=send_sems.at[3 * b + j], recv_sem=recv_sems.at[3 * b + j],
                                                  device_id=(*chip, c), device_id_type=MESH)
                cp.wait_send()
                cp.wait_recv()

    res = pl.pallas_call(
        body, name=name, in_specs=(*[HBM] * (2 * n), SEM, SEM, ANY),
        out_shape=tuple(pltpu.HBM(a.shape, a.dtype) for a in blocks + lands), out_specs=tuple([HBM] * (2 * n)),
        input_output_aliases={i: i for i in range(2 * n)},
        compiler_params=pltpu.CompilerParams(has_side_effects=DATAFLOW),
    )(*blocks, *lands, send_sems, recv_sems, after)
    return list(res[n:])


def _swap_halves(blocks, own_rows, *, name):
    n = len(blocks)

    def body(*refs):
        in_refs, out_refs, (send_sems, recv_sems) = refs[:n], refs[n: 2 * n], refs[2 * n:]
        x, y, c = _mesh_pos()
        copies = [pltpu.make_async_remote_copy(src_ref=own_rows(in_refs[b], c), dst_ref=out_refs[b], send_sem=send_sems.at[b],
                                               recv_sem=recv_sems.at[b], device_id=(x, y, 1 - c), device_id_type=MESH)
                  for b in range(n)]
        for cp in copies:
            cp.start()
        for cp in copies:
            cp.wait()

    def sent_shape(a):
        return jax.eval_shape(lambda r: own_rows(r, 0), a)

    return pl.pallas_call(
        body, name=name, in_specs=[ANY] * n, out_specs=[ANY] * n,
        out_shape=[jax.ShapeDtypeStruct(sent_shape(a).shape, a.dtype) for a in blocks],
        scratch_shapes=_sem_pairs(n),
    )(*blocks)


def _other_half_rows(ref, c):
    half = ref.shape[1] // 2
    return ref[:, (1 - c) * half: (2 - c) * half, :] if isinstance(c, int) else ref.at[:, pl.ds((1 - c) * half, half), :]


def _whole(ref, c):
    return ref


def _add_own_half_block(block, received, core, *, name):
    n, rows, cols = block.shape
    half = rows // 2
    tm = _tile(half, 512, 2 * SUBLANES)
    nb = half // tm

    def kern(c_ref, a_ref, b_ref, o_ref):
        o_ref[...] = (a_ref[...] + b_ref[...]).astype(o_ref.dtype)

    return pl.pallas_call(
        kern, name=name,
        grid_spec=pltpu.PrefetchScalarGridSpec(
            num_scalar_prefetch=1, grid=(n, nb),
            in_specs=[pl.BlockSpec((None, tm, cols), lambda s, i, c: (s, c[0] * nb + i, 0)),
                      pl.BlockSpec((None, tm, cols), lambda s, i, c: (s, i, 0))],
            out_specs=pl.BlockSpec((None, tm, cols), lambda s, i, c: (s, i, 0))),
        out_shape=jax.ShapeDtypeStruct((n, half, cols), LINK_DTYPE),
        compiler_params=_params("parallel", "parallel"),
    )(core, block, received)


def _scatter_blocks(sums, *, name):
    n = len(sums)

    def body(*refs):
        s_refs, out_refs, (send_sems, recv_sems) = refs[:n], refs[n: 2 * n], refs[2 * n:]
        x, y, c = _mesh_pos()
        me = 2 * x + y
        chips = _other_chips(x, y)

        def copy(b, j, chip, src_slot, dst_slot):
            return pltpu.make_async_remote_copy(src_ref=s_refs[b].at[src_slot], dst_ref=out_refs[b].at[dst_slot],
                                                send_sem=send_sems.at[3 * b + j], recv_sem=recv_sems.at[3 * b + j],
                                                device_id=(*chip, c), device_id_type=MESH)

        sends = [copy(b, j, chip, 2 * chip[0] + chip[1], me) for b in range(n) for j, chip in enumerate(chips)]
        for cp in sends:
            cp.start()
        for b in range(n):
            for j, chip in enumerate(chips):
                copy(b, j, chip, me, 2 * chip[0] + chip[1]).wait_recv()
        for cp in sends:
            cp.wait_send()

    return pl.pallas_call(
        body, name=name, in_specs=[ANY] * n, out_specs=[ANY] * n,
        out_shape=[jax.ShapeDtypeStruct(a.shape, a.dtype) for a in sums], scratch_shapes=_sem_pairs(3 * n),
    )(*sums)


def _sum_chips_block(parts, *, name):
    n, half, cols = parts.shape
    tm = _tile(half, 512, 2 * SUBLANES)

    def kern(p_ref, o_ref):
        acc = p_ref[0].astype(F32)
        for s in range(1, n):
            acc = acc + p_ref[s].astype(F32)
        o_ref[...] = acc

    return pl.pallas_call(
        kern, name=name, grid=(half // tm,),
        in_specs=[pl.BlockSpec((n, tm, cols), lambda i: (0, i, 0))],
        out_specs=pl.BlockSpec((tm, cols), lambda i: (i, 0)),
        out_shape=jax.ShapeDtypeStruct((half, cols), F32),
        compiler_params=_params("parallel"),
    )(parts)


def _reduce_scatter_blocks(blocks, names):
    x, y, c = _mesh_pos()
    chip = 2 * x + y
    core = c.astype(jnp.int32).reshape(1)
    received = _swap_halves(blocks, _other_half_rows, name="rs_swap")
    sums = [_add_own_half_block(b, r, core, name=f"rs_add_{nm}") for b, r, nm in zip(blocks, received, names, strict=True)]
    parts = _scatter_blocks(sums, name="rs_scatter")
    parts = [lax.dynamic_update_slice(p, lax.dynamic_slice_in_dim(s, chip, 1, axis=0), (chip, 0, 0))
             for p, s in zip(parts, sums, strict=True)]
    mine = [_sum_chips_block(p, name=f"rs_sum_{nm}") for p, nm in zip(parts, names, strict=True)]
    other = _swap_halves(mine, _whole, name="rs_join")
    return [jnp.concatenate([jnp.where(c == 0, a, b), jnp.where(c == 0, b, a)], axis=0) for a, b in zip(mine, other, strict=True)]


WEIGHTS = ("rel_bias", "att_w_in", "att_w_out", "dn_w_in", "dn_conv", "dn_a_log", "dn_dt_bias", "dn_out_norm", "dn_w_out",
           "mem_norm", "mem_w_kv", "norm_mix_pre", "norm_mix_post", "norm_ffn_pre", "norm_ffn_post", "ffn_w_gate_up",
           "ffn_w_down")
BIG_NAMES = tuple(n for n, _, _ in BIG)
SMALL_NAMES = tuple(n for n in WEIGHTS if n not in BIG_NAMES)
CONV_COLS = 3 * TOK_WIDTH
CONV_SHARD = CONV_COLS // N_CHIPS
BLOCKS = tuple((n, layer) for n, shape, _ in BIG for layer in range(shape[0]))
COLUMN_SHARDED = {n: axis == 2 for n, _, axis in BIG}


def kernel(x, mem, rel_bias, att_w_in, att_w_out, dn_w_in, dn_conv, dn_a_log, dn_dt_bias, dn_out_norm, dn_w_out, mem_norm, mem_w_kv, norm_mix_pre, norm_mix_post, norm_ffn_pre, norm_ffn_post, ffn_w_gate_up, ffn_w_down, loss_target, m_rel_bias, m_att_w_in, m_att_w_out, m_dn_w_in, m_dn_conv, m_dn_a_log, m_dn_dt_bias, m_dn_out_norm, m_dn_w_out, m_mem_norm, m_mem_w_kv, m_norm_mix_pre, m_norm_mix_post, m_norm_ffn_pre, m_norm_ffn_post, m_ffn_w_gate_up, m_ffn_w_down, v_rel_bias, v_att_w_in, v_att_w_out, v_dn_w_in, v_dn_conv, v_dn_a_log, v_dn_dt_bias, v_dn_out_norm, v_dn_w_out, v_mem_norm, v_mem_w_kv, v_norm_mix_pre, v_norm_mix_post, v_norm_ffn_pre, v_norm_ffn_post, v_ffn_w_gate_up, v_ffn_w_down):
    w = dict(zip(WEIGHTS, (rel_bias, att_w_in, att_w_out, dn_w_in, dn_conv, dn_a_log, dn_dt_bias, dn_out_norm, dn_w_out,
                           mem_norm, mem_w_kv, norm_mix_pre, norm_mix_post, norm_ffn_pre, norm_ffn_post, ffn_w_gate_up,
                           ffn_w_down)))
    m = dict(zip(WEIGHTS, (m_rel_bias, m_att_w_in, m_att_w_out, m_dn_w_in, m_dn_conv, m_dn_a_log, m_dn_dt_bias,
                           m_dn_out_norm, m_dn_w_out, m_mem_norm, m_mem_w_kv, m_norm_mix_pre, m_norm_mix_post,
                           m_norm_ffn_pre, m_norm_ffn_post, m_ffn_w_gate_up, m_ffn_w_down)))
    v = dict(zip(WEIGHTS, (v_rel_bias, v_att_w_in, v_att_w_out, v_dn_w_in, v_dn_conv, v_dn_a_log, v_dn_dt_bias,
                           v_dn_out_norm, v_dn_w_out, v_mem_norm, v_mem_w_kv, v_norm_mix_pre, v_norm_mix_post,
                           v_norm_ffn_pre, v_norm_ffn_post, v_ffn_w_gate_up, v_ffn_w_down)))
    cx, cy, cc = _mesh_pos()
    chip = 2 * cx + cy

    local = dict(zip(BLOCKS, lax.optimization_barrier([w[n][layer].astype(MXU_DTYPE) for n, layer in BLOCKS]), strict=True))

    def usable(block, got):
        got = lax.dynamic_update_slice(got, local[block][None], (chip, 0, 0))
        return got if COLUMN_SHARDED[block[0]] else got.reshape(-1, got.shape[-1])

    late = {"ffn0": [("ffn_w_gate_up", 0), ("ffn_w_down", 0)], "dn": [("dn_w_in", 0), ("dn_w_out", 0)],
            "ffn1": [("ffn_w_gate_up", 1), ("ffn_w_down", 1)]}
    started = {k: _gather_start([local[b] for b in blks], name=f"gather_start_{k}") for k, blks in late.items()}
    started_token = sum(s[4][0, 0] for s in started.values())

    def late_weights(key, after):
        lands = _gather_wait(started[key], after, name=f"gather_wait_{key}")
        return [usable(b, got) for b, got in zip(late[key], lands, strict=True)]

    def dn_weights(after):
        w_in, w_out = late_weights("dn", after)
        return _pad_dn_w_in(jnp.concatenate([w_in[s] for s in range(N_CHIPS)], axis=1)), w_out

    first = [b for b in BLOCKS if all(b not in blks for blks in late.values())]
    full = {}
    for b, got in zip(first, _gather_blocks([local[b] for b in first], name="gather_weights"), strict=True):
        full.setdefault(b[0], []).append(usable(b, got))
    conv_rows = _small_rows([(DN_CONV, CONV_COLS)])
    conv_mine = jnp.where(cc == 0, 1.0, 0.0) * w["dn_conv"][0]
    conv_placed = lax.dynamic_update_slice(jnp.zeros((DN_CONV, CONV_COLS), F32), conv_mine, (0, chip * CONV_SHARD))
    conv_full = _unpack_small(_all_reduce_small(_pack_small([conv_placed], conv_rows), name="gather_conv"),
                              [(DN_CONV, CONV_COLS)])[0]
    p = {
        "rel_bias": w["rel_bias"], "att_w_in": full["att_w_in"][0], "att_w_out": full["att_w_out"][0],
        "dn_conv": conv_full, "dn_a_log": w["dn_a_log"][0], "dn_dt_bias": w["dn_dt_bias"][0],
        "dn_out_norm": w["dn_out_norm"][0], "mem_norm": w["mem_norm"], "mem_w_kv": full["mem_w_kv"],
        "norm_mix_pre": w["norm_mix_pre"] + started_token,
        "norm_mix_post": w["norm_mix_post"], "norm_ffn_pre": w["norm_ffn_pre"], "norm_ffn_post": w["norm_ffn_post"],
        "ffn_weights": lambda layer, after: late_weights(f"ffn{layer}", after), "dn_weights": dn_weights,
    }

    loss_cols, grad_x, g = _local_step(x[0], mem[0], loss_target[0], p)
    loss = lax.psum(jnp.sum(loss_cols), ("x", "y", "c"))

    g_layers = {"att_w_in": [g["att_w_in"]], "att_w_out": [g["att_w_out"]], "dn_w_in": [_unpad_dn_w_in(g["dn_w_in"])],
                "dn_w_out": [g["dn_w_out"]], "mem_w_kv": g["mem_w_kv"], "ffn_w_gate_up": g["ffn_w_gate_up"],
                "ffn_w_down": g["ffn_w_down"]}

    def chip_blocks(n, a):
        if a.ndim == 3:
            return a
        if COLUMN_SHARDED[n]:
            return a.reshape(a.shape[0], N_CHIPS, -1).transpose(1, 0, 2)
        return a.reshape(N_CHIPS, -1, a.shape[-1])

    reduced = _reduce_scatter_blocks([chip_blocks(n, g_layers[n][layer]) for n, layer in BLOCKS],
                                     [f"{n}{layer}" for n, layer in BLOCKS])
    grads = {n: jnp.concatenate([r for (bn, _), r in zip(BLOCKS, reduced, strict=True) if bn == n], axis=0).reshape(shape)
             for n, shape, _ in BIG}
    small_full_shapes = [(DN_CONV, CONV_COLS) if n == "dn_conv" else w[n].shape for n in SMALL_NAMES]
    small_sum = _all_reduce_small(_pack_small([g[n] for n in SMALL_NAMES], _small_rows(small_full_shapes)), name="reduce_small")
    for n, s in zip(SMALL_NAMES, _unpack_small(small_sum, small_full_shapes)):
        grads[n] = lax.dynamic_slice(s, (0, chip * CONV_SHARD), (DN_CONV, CONV_SHARD))[None] if n == "dn_conv" else s

    delta, new_m, new_v = {}, {}, {}
    for n in BIG_NAMES:
        shape = w[n].shape
        two_d = lambda a: a.reshape(-1, shape[-1])
        res = _adamw(two_d(w[n]), two_d(grads[n]), two_d(m[n]), two_d(v[n]), name=f"adamw_{n}")
        delta[n], new_m[n], new_v[n] = (r.reshape(shape) for r in res)
    small_shapes = [w[n].shape for n in SMALL_NAMES]
    rows = _small_rows(small_shapes)
    res = _adamw(*[_pack_small([d[n] for n in SMALL_NAMES], rows) for d in (w, grads, m, v)], name="adamw_small")
    for d, r in zip((delta, new_m, new_v), res):
        for n, a in zip(SMALL_NAMES, _unpack_small(r, small_shapes)):
            d[n] = a
    return (loss, grad_x[None], *[grads[n] for n in WEIGHTS], *[delta[n] for n in WEIGHTS],
            *[new_m[n] for n in WEIGHTS], *[new_v[n] for n in WEIGHTS])
```

```python
import functools
import math

import numpy as np
import jax
import jax.numpy as jnp
from jax import lax
from jax.experimental import pallas as pl
from jax.experimental.pallas import tpu as pltpu

F32 = jnp.float32
MXU_DTYPE = jnp.bfloat16
LINK_DTYPE = jnp.bfloat16
HI = lax.Precision.HIGHEST

EPS = 1e-6
NEG_INF = -1e30
LANES = 128
SUBLANES = 8
VMEM_LIMIT = 56 * 1024 * 1024

D_MODEL = 1024
TOK_WIDTH = 768
MEM_WIDTH = 256
MEM_LEN = 256
ATT_HEAD_DIM = 64
DILATIONS = (1, 4, 16)
HALF = 64
ATT_BQ = 128
ATT_W = ATT_BQ + 2 * HALF
REL_BUCKETS = 32
REL_MAX_DIST = 1024
DN_HEADS = 6
DN_HEAD_DIM = 128
DN_CONV = 5
DN_CHUNK = 128
D_FF = 2816
ATT_IN = 2560
DN_IN = 3352
DN_IN_PAD = 3456
N_GATES = 4 * DN_HEADS

ADAM_LR = 0.001
ADAM_B1 = 0.9
ADAM_B2 = 0.999
ADAM_EPS = 1e-08
ADAM_WD = 0.01
ADAM_STEP = 10


def _tile(n, target, align):
    if n <= target:
        return n
    t = (target // align) * align
    while t >= align:
        if n % t == 0:
            return t
        t -= align
    raise ValueError(f"no tile for {n} (target {target}, align {align})")


def _params(*sem):
    return pltpu.CompilerParams(dimension_semantics=sem, vmem_limit_bytes=VMEM_LIMIT)


def _mm(a, b, *, name, ta=False, tb=False, out_shards=None, tm=1024, tn=1408, tk=1408, out_dtype=F32):
    if ta:
        K, M = a.shape
    else:
        M, K = a.shape
    sharded_b = b.ndim == 3
    if sharded_b:
        n_sh, b_rows, b_cols = b.shape
        N, K2 = (b_rows, n_sh * b_cols) if tb else (n_sh * b_cols, b_rows)
    else:
        N, K2 = b.shape if tb else b.shape[::-1]
    assert K == K2, (a.shape, b.shape, ta, tb)
    tm = _tile(M, tm, LANES if ta else SUBLANES)
    tn = N // out_shards if out_shards else (b_cols if sharded_b and not tb else _tile(N, tn, LANES))
    tk = b_cols if sharded_b and tb else _tile(K, tk, LANES)
    nk = K // tk
    a_spec = pl.BlockSpec((tk, tm), lambda i, j, k: (k, i)) if ta else pl.BlockSpec((tm, tk), lambda i, j, k: (i, k))
    if sharded_b:
        b_spec = (pl.BlockSpec((None, tn, tk), lambda i, j, k: (k, j, 0)) if tb
                  else pl.BlockSpec((None, tk, tn), lambda i, j, k: (j, k, 0)))
    else:
        b_spec = pl.BlockSpec((tn, tk), lambda i, j, k: (j, k)) if tb else pl.BlockSpec((tk, tn), lambda i, j, k: (k, j))
    if out_shards:
        out_spec = pl.BlockSpec((None, tm, tn), lambda i, j, k: (j, i, 0))
        out_shape = jax.ShapeDtypeStruct((out_shards, M, tn), out_dtype)
    else:
        out_spec = pl.BlockSpec((tm, tn), lambda i, j, k: (i, j))
        out_shape = jax.ShapeDtypeStruct((M, N), out_dtype)
    dims = (((0 if ta else 1,), (1 if tb else 0,)), ((), ()))

    def kern(a_ref, b_ref, o_ref, acc_ref):
        k = pl.program_id(2)

        @pl.when(k == 0)
        def _():
            acc_ref[...] = jnp.zeros_like(acc_ref)

        acc_ref[...] += lax.dot_general(a_ref[...].astype(MXU_DTYPE), b_ref[...].astype(MXU_DTYPE), dims,
                                        preferred_element_type=F32)

        @pl.when(k == nk - 1)
        def _():
            o_ref[...] = acc_ref[...].astype(o_ref.dtype)

    return pl.pallas_call(
        kern, name=name, grid=(M // tm, N // tn, nk), in_specs=[a_spec, b_spec],
        out_specs=out_spec, out_shape=out_shape,
        scratch_shapes=[pltpu.VMEM((tm, tn), F32)],
        compiler_params=_params("parallel", "parallel", "arbitrary"),
    )(a, b)


def _shards_of(w):
    return w.shape[0] if w.ndim == 3 else None


def _col(arr, width, blk):
    return (arr, width, blk)


def _rowwise(body, rows, consts, out_rows, out_acc, *, tm, name):
    n_rows = (rows[0][0] if isinstance(rows[0], tuple) else rows[0]).shape[0]
    assert n_rows % tm == 0, (n_rows, tm)
    arrs, in_specs = [], []
    for r in rows:
        arr, width, blk = r if isinstance(r, tuple) else (r, r.shape[1], 0)
        assert arr.shape[0] == n_rows
        arrs.append(arr)
        in_specs.append(pl.BlockSpec((tm, width), functools.partial(lambda i, b: (i, b), b=blk)))
    for c in consts:
        arrs.append(c)
        in_specs.append(pl.BlockSpec(c.shape, functools.partial(lambda i, n: (0,) * n, n=c.ndim)))
    n_in, n_ro = len(arrs), len(out_rows)
    out_shape = [jax.ShapeDtypeStruct((n_rows, w), dt) for w, dt in out_rows]
    out_specs = [pl.BlockSpec((tm, w), lambda i: (i, 0)) for w, _ in out_rows]
    out_shape += [jax.ShapeDtypeStruct(s, F32) for s in out_acc]
    out_specs += [pl.BlockSpec(s, lambda i: (0, 0)) for s in out_acc]

    def kern(*refs):
        ro, ao = body(*[r[...] for r in refs[:n_in]])
        outs = refs[n_in:]
        for r, v in zip(outs[:n_ro], ro, strict=True):
            r[...] = v.astype(r.dtype)
        if out_acc:
            @pl.when(pl.program_id(0) == 0)
            def _():
                for r in outs[n_ro:]:
                    r[...] = jnp.zeros_like(r)

            for r, v in zip(outs[n_ro:], ao, strict=True):
                r[...] += v

    res = pl.pallas_call(
        kern, name=name, grid=(n_rows // tm,), in_specs=in_specs, out_specs=out_specs, out_shape=out_shape,
        compiler_params=_params("arbitrary" if out_acc else "parallel"),
    )(*arrs)
    return res


def _rms(x, gain):
    return x * lax.rsqrt(jnp.mean(x * x, axis=-1, keepdims=True) + EPS) * gain


def _silu(x):
    return x * jax.nn.sigmoid(x)


def _softplus(x):
    return jnp.maximum(x, 0.0) + jnp.log(1.0 + jnp.exp(-jnp.abs(x)))


def _dot_nt(a, b, precision=None):
    return lax.dot_general(a, b, (((1,), (1,)), ((), ())), preferred_element_type=F32, precision=precision)


def _dot_tn(a, b, precision=None):
    return lax.dot_general(a, b, (((0,), (0,)), ((), ())), preferred_element_type=F32, precision=precision)


def _dot(a, b, precision=None):
    return jnp.dot(a, b, preferred_element_type=F32, precision=precision)


def _pre_norm(x, gain, *, name):
    def body(x, g):
        return (_rms(x, g),), ()
    return _rowwise(body, [x], [gain], [(x.shape[1], MXU_DTYPE)], [], tm=_tile(x.shape[0], 512, 2 * SUBLANES), name=name)[0]


def _pre_norm_bwd(x, gain, dh, dx_other, *, name):
    def body(x, dh, dxo, g):
        _, vjp = jax.vjp(_rms, x, g)
        dx, dg = vjp(dh)
        return (dx + dxo,), (dg,)
    return _rowwise(body, [x, dh, dx_other], [gain], [(x.shape[1], F32)], [gain.shape], tm=512, name=name)


def _gain_bwd(x, gain, dh, *, name):
    def body(x, dh, g):
        _, vjp = jax.vjp(lambda g_: _rms(x, g_), g)
        return (), (vjp(dh)[0],)
    return _rowwise(body, [x, dh], [gain], [], [gain.shape], tm=_tile(x.shape[0], 512, SUBLANES), name=name)[0]


def _res_block(x_res, m, g_post, g_pre):
    x_new = x_res + _rms(m, g_post)
    return x_new, _rms(x_new, g_pre)


def _post_pre(x_res, m, g_post, g_pre, *, name):
    def body(x, m, gp, gq):
        return _res_block(x, m, gp, gq), ()
    d = x_res.shape[1]
    return _rowwise(body, [x_res, m], [g_post, g_pre], [(d, F32), (d, MXU_DTYPE)], [], tm=512, name=name)


def _post_pre_bwd(x_res, m, g_post, g_pre, dx_new, dh, *, name):
    def body(x, m, dxn, dh, gp, gq):
        _, vjp = jax.vjp(_res_block, x, m, gp, gq)
        dx, dm, dgp, dgq = vjp((dxn, dh))
        return (dx, dm), (dgp, dgq)
    d = x_res.shape[1]
    return _rowwise(body, [x_res, m, dx_new, dh], [g_post, g_pre], [(d, F32), (d, MXU_DTYPE)],
                    [g_post.shape, g_pre.shape], tm=256, name=name)


def _final_loss_bwd(x_res, m, g_post, target, *, name):
    d = x_res.shape[1]

    def loss_cols(x, m, g, t):
        err = x + _rms(m, g) - t
        return jnp.sum(err * err, axis=0, keepdims=True) * (0.5 / d)

    def body(x, m, t, g):
        cols, vjp = jax.vjp(lambda x_, m_, g_: loss_cols(x_, m_, g_, t), x, m, g)
        dx, dm, dg = vjp(jnp.ones_like(cols))
        return (dx, dm), (dg, cols)
    return _rowwise(body, [x_res, m, target], [g_post], [(d, F32), (d, MXU_DTYPE)], [g_post.shape, (1, d)], tm=256, name=name)


def _swiglu_act(gu, *, name):
    def body(gate, up):
        return (_silu(gate) * up,), ()
    return _rowwise(body, [_col(gu, D_FF, 0), _col(gu, D_FF, 1)], [], [(D_FF, MXU_DTYPE)], [], tm=256, name=name)[0]


def _swiglu_act_bwd(gu, da, *, name):
    def body(gate, up, da):
        _, vjp = jax.vjp(lambda g, u: _silu(g) * u, gate, up)
        dg, du = vjp(da)
        return (jnp.concatenate([dg, du], axis=1),), ()
    return _rowwise(body, [_col(gu, D_FF, 0), _col(gu, D_FF, 1), da], [], [(2 * D_FF, MXU_DTYPE)], [], tm=256, name=name)[0]


def _lane_head_mask(width, head_dim, head):
    lane = lax.broadcasted_iota(jnp.int32, (1, width), 1)
    return (lane // head_dim) == head


def _mem_attn_pair(q_pair, k_pair, v_pair):
    out = jnp.zeros_like(q_pair)
    for h in range(2):
        mh = _lane_head_mask(LANES, ATT_HEAD_DIM, h)
        qh = jnp.where(mh, q_pair * (ATT_HEAD_DIM ** -0.5), 0.0)
        logits = _dot_nt(qh, k_pair)
        mx = jnp.max(logits, axis=-1, keepdims=True)
        p = jnp.exp(logits - mx)
        p = p / jnp.sum(p, axis=-1, keepdims=True)
        out = out + jnp.where(mh, _dot(p, v_pair), 0.0)
    return out


def _mem_attn(q_mem, kv):
    outs = []
    for p in range(MEM_WIDTH // LANES):
        sl = slice(p * LANES, (p + 1) * LANES)
        outs.append(_mem_attn_pair(q_mem[:, sl], kv[:, sl], kv[:, MEM_WIDTH + p * LANES: MEM_WIDTH + (p + 1) * LANES]))
    return jnp.concatenate(outs, axis=1)


def _mem_attn_bwd(q_mem, kv, do):
    dqs, dks, dvs = [], [], []
    for p in range(MEM_WIDTH // LANES):
        sl = slice(p * LANES, (p + 1) * LANES)
        sv = slice(MEM_WIDTH + p * LANES, MEM_WIDTH + (p + 1) * LANES)
        _, vjp = jax.vjp(_mem_attn_pair, q_mem[:, sl], kv[:, sl], kv[:, sv])
        dq, dk, dv = vjp(do[:, sl])
        dqs.append(dq)
        dks.append(dk)
        dvs.append(dv)
    return jnp.concatenate(dqs, axis=1), jnp.concatenate(dks + dvs, axis=1)


def _t5_bucket(rel):
    half = REL_BUCKETS // 2
    max_exact = half // 2
    n = np.abs(rel)
    large = max_exact + (np.log(np.maximum(n, 1) / max_exact) / math.log(REL_MAX_DIST / max_exact)
                         * (half - max_exact)).astype(np.int64)
    large = np.minimum(large, half - 1)
    return ((rel > 0) * half + np.where(n < max_exact, n, large)).astype(np.int32)


ATT_DIAGS = ATT_BQ + ATT_W - 1


def _bias_diag_onehot(dil):
    j = np.arange(ATT_DIAGS)
    tiles = []
    for off in (-HALF, 0, HALF):
        rel = j - (ATT_BQ - 1) - HALF - off
        hot = _t5_bucket(rel * dil)[:, None] == np.arange(REL_BUCKETS)[None, :]
        tiles.append(hot & (np.abs(rel) <= HALF)[:, None])
    return np.stack(tiles).astype(np.float32)


def _toeplitz(r):
    lead = r.shape[:-1]
    a = jnp.broadcast_to(r[..., None, :], lead + (ATT_BQ, ATT_DIAGS))
    a = jnp.pad(a, [(0, 0)] * len(lead) + [(0, 0), (0, 1)])
    a = a.reshape(lead + (ATT_BQ * (ATT_DIAGS + 1),))[..., : ATT_BQ * ATT_DIAGS].reshape(lead + (ATT_BQ, ATT_DIAGS))
    return a[..., ATT_BQ - 1: ATT_BQ - 1 + ATT_W]


def _bias_tiles(rel_bias, gi):
    heads = rel_bias[:, 4 * gi: 4 * gi + 4]
    diag = jnp.einsum('tnb,bh->thn', jnp.asarray(_bias_diag_onehot(DILATIONS[gi])), heads, precision=HI)
    return _toeplitz(diag)


def _bias_tiles_bwd(rel_bias, dtiles, gi):
    return jax.vjp(lambda rb: _bias_tiles(rb, gi), rel_bias)[1](dtiles)[0]


def _att_window(i, n_sub):
    start = jnp.clip(i * ATT_BQ - HALF, 0, n_sub - ATT_W)
    off = i * ATT_BQ - HALF - start
    return pl.multiple_of(start, HALF), off


def _att_valid(off):
    q = lax.broadcasted_iota(jnp.int32, (ATT_BQ, ATT_W), 0)
    kk = lax.broadcasted_iota(jnp.int32, (ATT_BQ, ATT_W), 1)
    return jnp.abs(kk - q - HALF - off) <= HALF


def _att_tile_id(i, nq):
    return jnp.where(i == 0, 0, jnp.where(i == nq - 1, 2, 1))


def _att_fwd(qkvm, bias, gi, *, name):
    dil = DILATIONS[gi]
    s_len = qkvm.shape[0]
    n_sub = s_len // dil
    nq = n_sub // ATT_BQ
    assert n_sub % ATT_BQ == 0 and n_sub >= ATT_W
    cols = qkvm.shape[1] // LANES
    view = qkvm.reshape(n_sub, dil * qkvm.shape[1])

    def kern(q_ref, k_ref, v_ref, b_ref, o_ref, lse_ref):
        i = pl.program_id(2)
        start, off = _att_window(i, n_sub)
        valid = _att_valid(off)
        q = q_ref[...] * (ATT_HEAD_DIM ** -0.5)
        kw = k_ref[pl.ds(start, ATT_W), :]
        vw = v_ref[pl.ds(start, ATT_W), :]
        o = jnp.zeros((ATT_BQ, LANES), F32)
        lse = jnp.zeros((ATT_BQ, LANES), F32)
        for h in range(2):
            mh = _lane_head_mask(LANES, ATT_HEAD_DIM, h)
            s = _dot_nt(jnp.where(mh, q, 0.0), kw) + b_ref[h]
            s = jnp.where(valid, s, NEG_INF)
            mx = jnp.max(s, axis=-1, keepdims=True)
            p = jnp.exp(s - mx)
            den = jnp.sum(p, axis=-1, keepdims=True)
            o = jnp.where(mh, _dot(p, vw) / den, o)
            lse = jnp.where(mh, mx + jnp.log(den), lse)
        o_ref[...] = o
        lse_ref[...] = lse

    def qkv_spec(which, full):
        shape = (n_sub, LANES) if full else (ATT_BQ, LANES)
        return pl.BlockSpec(shape, lambda pr, r, i: (0 if full else i, r * cols + which * 6 + 2 * gi + pr))

    out_spec = pl.BlockSpec((ATT_BQ, LANES), lambda pr, r, i: (i, r * 2 + pr))
    o, lse = pl.pallas_call(
        kern, name=name, grid=(2, dil, nq),
        in_specs=[qkv_spec(0, False), qkv_spec(1, True), qkv_spec(2, True),
                  pl.BlockSpec((None, 2, ATT_BQ, ATT_W), lambda pr, r, i: (_att_tile_id(i, nq), pr, 0, 0))],
        out_specs=[out_spec, out_spec],
        out_shape=[jax.ShapeDtypeStruct((n_sub, dil * 2 * LANES), F32)] * 2,
        compiler_params=_params("parallel", "parallel", "arbitrary"),
    )(view, view, view, bias)
    return o.reshape(s_len, 2 * LANES), lse.reshape(s_len, 2 * LANES)


def _att_bwd(qkvm, bias, lse_tot, delta, dcat, gi, *, name):
    dil = DILATIONS[gi]
    s_len = qkvm.shape[0]
    n_sub = s_len // dil
    nq = n_sub // ATT_BQ
    cols = qkvm.shape[1] // LANES
    dcols = dcat.shape[1] // LANES
    view = qkvm.reshape(n_sub, dil * qkvm.shape[1])
    lse_v = lse_tot.reshape(n_sub, dil * 2 * LANES)
    delta_v = delta.reshape(n_sub, dil * 2 * LANES)
    dcat_v = dcat.reshape(n_sub, dil * dcat.shape[1])

    def kern(q_ref, k_ref, v_ref, b_ref, lse_ref, dl_ref, dm_ref, dq_ref, dk_ref, dv_ref, db_ref):
        r, i = pl.program_id(1), pl.program_id(2)
        start, off = _att_window(i, n_sub)
        valid = _att_valid(off)
        tile = _att_tile_id(i, nq)

        @pl.when(i == 0)
        def _():
            dk_ref[...] = jnp.zeros_like(dk_ref)
            dv_ref[...] = jnp.zeros_like(dv_ref)

        @pl.when((i == 0) & (r == 0))
        def _():
            db_ref[...] = jnp.zeros_like(db_ref)

        q = q_ref[...] * (ATT_HEAD_DIM ** -0.5)
        kw = k_ref[pl.ds(start, ATT_W), :]
        vw = v_ref[pl.ds(start, ATT_W), :]
        dm = dm_ref[...]
        lse = lse_ref[...]
        dl = dl_ref[...]
        dq = jnp.zeros((ATT_BQ, LANES), F32)
        dkw = jnp.zeros((ATT_W, LANES), F32)
        dvw = jnp.zeros((ATT_W, LANES), F32)
        for h in range(2):
            mh = _lane_head_mask(LANES, ATT_HEAD_DIM, h)
            qh = jnp.where(mh, q, 0.0)
            dmh = jnp.where(mh, dm, 0.0)
            s = _dot_nt(qh, kw) + b_ref[tile, h]
            s = jnp.where(valid, s, NEG_INF)
            lse_h = jnp.max(jnp.where(mh, lse, NEG_INF), axis=-1, keepdims=True)
            dl_h = jnp.max(jnp.where(mh, dl, NEG_INF), axis=-1, keepdims=True)
            p = jnp.exp(s - lse_h)
            ds = p * (_dot_nt(dmh, vw) - dl_h)
            dq = dq + jnp.where(mh, _dot(ds, kw), 0.0)
            dkw = dkw + _dot_tn(ds, qh)
            dvw = dvw + _dot_tn(p, dmh)
            db_ref[tile, h] += ds
        dq_ref[...] = dq * (ATT_HEAD_DIM ** -0.5)
        dk_ref[pl.ds(start, ATT_W), :] += dkw
        dv_ref[pl.ds(start, ATT_W), :] += dvw

    def qkv_spec(which, full):
        shape = (n_sub, LANES) if full else (ATT_BQ, LANES)
        return pl.BlockSpec(shape, lambda pr, r, i: (0 if full else i, r * cols + which * 6 + 2 * gi + pr))

    blk = pl.BlockSpec((ATT_BQ, LANES), lambda pr, r, i: (i, r * 2 + pr))
    full = pl.BlockSpec((n_sub, LANES), lambda pr, r, i: (0, r * 2 + pr))
    bias_spec = pl.BlockSpec((3, 2, ATT_BQ, ATT_W), lambda pr, r, i: (0, pr, 0, 0))
    sub = jax.ShapeDtypeStruct((n_sub, dil * 2 * LANES), F32)
    dq, dk, dv, db = pl.pallas_call(
        kern, name=name, grid=(2, dil, nq),
        in_specs=[qkv_spec(0, False), qkv_spec(1, True), qkv_spec(2, True), bias_spec, blk, blk,
                  pl.BlockSpec((ATT_BQ, LANES), lambda pr, r, i: (i, r * dcols + 2 * gi + pr))],
        out_specs=[blk, full, full, bias_spec],
        out_shape=[sub, sub, sub, jax.ShapeDtypeStruct(bias.shape, F32)],
        compiler_params=_params("arbitrary", "arbitrary", "arbitrary"),
    )(view, view, view, bias, lse_v, delta_v, dcat_v)
    return dq.reshape(s_len, -1), dk.reshape(s_len, -1), dv.reshape(s_len, -1), db


def _att_combine(o_g, lse_g, qkvm, kv_mem, *, name):
    def body(o0, o1, o2, l0, l1, l2, qm, kv):
        mx = jnp.maximum(jnp.maximum(l0, l1), l2)
        tot = mx + jnp.log(jnp.exp(l0 - mx) + jnp.exp(l1 - mx) + jnp.exp(l2 - mx))
        mixed = [o * jnp.exp(l - tot) for o, l in ((o0, l0), (o1, l1), (o2, l2))]
        return (jnp.concatenate(mixed + [_mem_attn(qm, kv)], axis=1), tot), ()
    return _rowwise(body, list(o_g) + list(lse_g) + [_col(qkvm, MEM_WIDTH, (3 * TOK_WIDTH) // MEM_WIDTH)], [kv_mem],
                    [(D_MODEL, F32), (MEM_WIDTH, F32)], [], tm=256, name=name)


def _head_sum_matrix():
    a = np.arange(MEM_WIDTH)
    return jnp.asarray((a[:, None] // ATT_HEAD_DIM == a[None, :] // ATT_HEAD_DIM).astype(np.float32))


def _att_bwd_prep(cat, dcat, qkvm, kv_mem, *, name):
    def body(cat, dcat, qm, kv, hs):
        prod = cat * dcat
        summed = prod[:, 0:256] + prod[:, 256:512] + prod[:, 512:768]
        delta = _dot(summed, hs, precision=HI)
        dqm, dkv = _mem_attn_bwd(qm, kv, dcat[:, TOK_WIDTH:])
        return (delta, dqm), (dkv,)
    return _rowwise(body, [cat, dcat, _col(qkvm, MEM_WIDTH, (3 * TOK_WIDTH) // MEM_WIDTH)], [kv_mem, _head_sum_matrix()],
                    [(MEM_WIDTH, F32), (MEM_WIDTH, F32)], [kv_mem.shape], tm=256, name=name)


def _dn_conv_post(s, j):
    scale = jnp.where(j < DN_HEADS, DN_HEAD_DIM ** -0.5, 1.0)
    normed = s * lax.rsqrt(jnp.sum(s * s, axis=-1, keepdims=True) + EPS) * scale
    return jnp.where(j >= 2 * DN_HEADS, s, normed)


def _shift_rows(x, sh):
    n = x.shape[0]
    row = lax.broadcasted_iota(jnp.int32, (n, 1), 0)
    rolled = pltpu.roll(x, (-sh) % n, 0)
    return jnp.where((row + sh >= 0) & (row + sh < n), rolled, 0.0)


def _dn_conv_taps(x, w_ref):
    c = x * w_ref[pl.ds(DN_CONV // 2, 1), :]
    for jj in range(DN_CONV):
        if jj != DN_CONV // 2:
            c = c + _shift_rows(x, jj - DN_CONV // 2) * w_ref[pl.ds(jj, 1), :]
    return c


def _dn_conv_fwd(proj, conv_w, *, name):
    s_len = proj.shape[0]
    width = 3 * TOK_WIDTH

    def kern(x_ref, w_ref, o_ref):
        j = pl.program_id(0)
        o_ref[...] = _dn_conv_post(_silu(_dn_conv_taps(x_ref[...], w_ref)), j)

    return pl.pallas_call(
        kern, name=name, grid=(width // LANES,),
        in_specs=[pl.BlockSpec((s_len, LANES), lambda j: (0, j)), pl.BlockSpec((DN_CONV, LANES), lambda j: (0, j))],
        out_specs=pl.BlockSpec((s_len, LANES), lambda j: (0, j)),
        out_shape=jax.ShapeDtypeStruct((s_len, width), F32),
        compiler_params=_params("parallel"),
    )(proj, conv_w)


def _dn_conv_bwd(proj, conv_w, d_fwd, d_bwd, which, *, name):
    s_len = proj.shape[0]

    def kern(x_ref, w_ref, df_ref, db_ref, dx_ref, dw_ref):
        j = pl.program_id(0) + which * DN_HEADS
        x = x_ref[...]
        c = _dn_conv_taps(x, w_ref)
        _, vjp = jax.vjp(lambda c_: _dn_conv_post(_silu(c_), j), c)
        dc = vjp(df_ref[...] + db_ref[...])[0]
        dx = dc * w_ref[pl.ds(DN_CONV // 2, 1), :]
        for jj in range(DN_CONV):
            sh = jj - DN_CONV // 2
            if sh != 0:
                dx = dx + _shift_rows(dc, -sh) * w_ref[pl.ds(jj, 1), :]
            dw_ref[pl.ds(jj, 1), :] = jnp.sum(dc * _shift_rows(x, sh), axis=0, keepdims=True)
        dx_ref[...] = dx

    return pl.pallas_call(
        kern, name=name, grid=(DN_HEADS,),
        in_specs=[pl.BlockSpec((s_len, LANES), lambda j: (0, j + which * DN_HEADS)),
                  pl.BlockSpec((DN_CONV, LANES), lambda j: (0, j + which * DN_HEADS)),
                  pl.BlockSpec((s_len, LANES), lambda j: (0, j)),
                  pl.BlockSpec((s_len, LANES), lambda j: (0, j))],
        out_specs=[pl.BlockSpec((s_len, LANES), lambda j: (0, j)), pl.BlockSpec((DN_CONV, LANES), lambda j: (0, j))],
        out_shape=[jax.ShapeDtypeStruct((s_len, TOK_WIDTH), F32), jax.ShapeDtypeStruct((DN_CONV, TOK_WIDTH), F32)],
        compiler_params=_params("parallel"),
    )(proj, conv_w, d_fwd, d_bwd)


GATE_TM = 2 * DN_CHUNK


FWD_GATE_LANES = 2 * DN_HEADS


def _gate_constants():
    i = np.arange(GATE_TM)
    same = (i[:, None] // DN_CHUNK) == (i[None, :] // DN_CHUNK)
    cum_f = same & (i[None, :] <= i[:, None])
    cum_r = same & (i[None, :] >= i[:, None])
    return tuple(jnp.asarray(np.asarray(a, np.float32)) for a in (cum_f, cum_r, same))


def _gate_params(p):
    z = jnp.zeros((DN_HEADS,), F32)
    return jnp.concatenate([p[0], z, p[1], z, jnp.zeros((LANES - N_GATES,), F32)]).reshape(1, LANES)


def _gate_params_bwd(dp):
    return jnp.stack([dp[0, 0:DN_HEADS], dp[0, 2 * DN_HEADS: 3 * DN_HEADS]])


def _dn_gates(gate_in, a_cols, dt_cols, cum_f, cum_r, tot):
    g = -jnp.exp(a_cols) * _softplus(gate_in + dt_cols)
    fwd_lane = lax.broadcasted_iota(jnp.int32, (1, LANES), 1) < FWD_GATE_LANES
    gc = jnp.where(fwd_lane, _dot(cum_f, g, precision=HI), _dot(cum_r, g, precision=HI))
    return gc, _dot(tot, g, precision=HI), jax.nn.sigmoid(gate_in)


def _dn_gates_fwd(proj, a_cols, dt_cols, *, name):
    def body(gi, *consts):
        return _dn_gates(gi, *consts), ()
    return _rowwise(body, [_col(proj, LANES, DN_IN_PAD // LANES - 1)], [a_cols, dt_cols, *_gate_constants()],
                    [(LANES, F32)] * 3, [], tm=GATE_TM, name=name)


def _dn_gates_bwd(proj, a_cols, dt_cols, d_gates, *, name):
    def body(gi, gcf, gtf, bf, gcr, gtr, br, a, dt, *consts):
        _, vjp = jax.vjp(lambda gi_, a_, dt_: _dn_gates(gi_, a_, dt_, *consts), gi, a, dt)
        dgi, da, ddt = vjp((gcf + gcr, gtf + gtr, bf + br))
        return (dgi,), (da, ddt)
    return _rowwise(body, [_col(proj, LANES, DN_IN_PAD // LANES - 1), *d_gates[0], *d_gates[1]],
                    [a_cols, dt_cols, *_gate_constants()], [(LANES, F32)], [a_cols.shape, dt_cols.shape],
                    tm=GATE_TM, name=name)


INV_BASE = 8


def _block_id_equal(c, size):
    i = lax.broadcasted_iota(jnp.int32, (c, c), 0) // size
    j = lax.broadcasted_iota(jnp.int32, (c, c), 1) // size
    return (i == j).astype(F32)


def _unit_tri_inverse_impl(lmat):
    c = lmat.shape[0]
    eye = _block_id_equal(c, 1)
    same = _block_id_equal(c, INV_BASE)
    neg = -lmat * same
    inv = eye + neg
    power = neg
    for _ in range(int(math.log2(INV_BASE)) - 1):
        power = _dot(power, power)
        inv = inv + _dot(inv, power)
    size = INV_BASE
    while size < c:
        bigger = _block_id_equal(c, 2 * size)
        inv = inv - _dot(_dot(inv, lmat * (bigger - same)), inv)
        same, size = bigger, 2 * size
    resid = eye - _dot(eye + lmat, inv, precision=HI)
    return inv + _dot(inv, resid)


@jax.custom_vjp
def _unit_tri_inverse(lmat):
    return _unit_tri_inverse_impl(lmat)


def _unit_tri_inverse_fwd(lmat):
    inv = _unit_tri_inverse_impl(lmat)
    return inv, inv


def _unit_tri_inverse_bwd(inv, d_inv):
    return (-_dot_tn(inv, _dot_nt(d_inv, inv)),)


_unit_tri_inverse.defvjp(_unit_tri_inverse_fwd, _unit_tri_inverse_bwd)


def _dn_chunk(q, k, v, gates_t, gc_row, tot_row, beta_row, state, tri, inverse):
    c = q.shape[0]
    assert c == DN_HEAD_DIM
    eye = _block_id_equal(c, 1)

    def along_rows(x, pick):
        return jnp.broadcast_to(jnp.sum(x * pick, axis=0, keepdims=True), (c, c))

    gc_j = along_rows(gates_t[0], gc_row)
    gc = gc_j.T
    g_tot = along_rows(gates_t[1], tot_row)
    beta = along_rows(gates_t[2], beta_row).T
    decay = jnp.exp(jnp.where(tri > 0, gc - gc_j, NEG_INF))
    k_beta = k * beta
    inv = inverse((tri - eye) * (_dot_nt(k_beta, k) * decay))
    e_gc = jnp.exp(gc)
    u = _dot(inv, v * beta)
    w = _dot(inv, k_beta * e_gc)
    intra = tri * (_dot_nt(q, k) * decay)
    v_new = u - _dot(w, state)
    out = _dot(q * e_gc, state) + _dot(intra, v_new)
    state = state * jnp.exp(g_tot) + _dot_tn(k * jnp.exp(g_tot - gc), v_new)
    return out, state


def _dn_tri():
    i = np.arange(DN_CHUNK)
    tri = np.stack([(i[None, :] <= i[:, None]), (i[None, :] >= i[:, None])]).astype(np.float32)
    return jnp.asarray(np.repeat(tri, DN_HEADS, axis=0))


def _dn_gate_picks():
    picks = np.zeros((3, 2 * DN_HEADS, 2 * DN_CHUNK, 1), np.float32)
    for d in range(2):
        for h in range(DN_HEADS):
            alpha = d * DN_CHUNK + d * 2 * DN_HEADS + h
            picks[0, d * DN_HEADS + h, alpha] = 1.0
            picks[1, d * DN_HEADS + h, alpha] = 1.0
            picks[2, d * DN_HEADS + h, alpha + DN_HEADS] = 1.0
    return jnp.asarray(picks)


def _stack_chains(fwd_ref, rev_ref):
    return jnp.stack([r[:, _head_cols(h)] for r in (fwd_ref, rev_ref) for h in range(DN_HEADS)])


def _unstack_chains(val, fwd_ref, rev_ref):
    for d, r in enumerate((fwd_ref, rev_ref)):
        for h in range(DN_HEADS):
            r[:, _head_cols(h)] = val[d * DN_HEADS + h]


def _gates_transposed(fwd_refs, rev_refs):
    return jnp.stack([jnp.concatenate([f[...].T, r[...].T], axis=0) for f, r in zip(fwd_refs, rev_refs, strict=True)])


def _dn_row_spec(nc, col, reverse, width=TOK_WIDTH):
    return pl.BlockSpec((DN_CHUNK, width), lambda t: ((nc - 1 - t) if reverse else t, col))


def _dn_state_spec(nc, reverse):
    return pl.BlockSpec((None, DN_HEADS, DN_HEAD_DIM, DN_HEAD_DIM), lambda t: ((nc - 1 - t) if reverse else t, 0, 0, 0))


def _head_cols(h):
    return pl.ds(h * DN_HEAD_DIM, DN_HEAD_DIM)


def _const_spec(arr):
    return pl.BlockSpec(arr.shape, functools.partial(lambda t, n: (0,) * n, n=arr.ndim))


def _dn_chains(inverse):
    return jax.vmap(lambda q, k, v, gates_t, *rest: _dn_chunk(q, k, v, gates_t, *rest, inverse),
                    in_axes=(0, 0, 0, None, 0, 0, 0, 0, 0))


def _dn_scan_fwd(qkv, gates, *, name):
    s_len = qkv.shape[0]
    nc = s_len // DN_CHUNK
    tri, picks = _dn_tri(), _dn_gate_picks()

    def kern(*refs):
        ins, (tri_ref, pick_ref, of_ref, or_ref, sf_ref, sr_ref, state) = refs[:12], refs[12:]

        @pl.when(pl.program_id(0) == 0)
        def _():
            state[...] = jnp.zeros_like(state)

        entry = state[...]
        qkv_c = [_stack_chains(ins[i], ins[6 + i]) for i in range(3)]
        gates_t = _gates_transposed(ins[3:6], ins[9:12])
        out, new = _dn_chains(_unit_tri_inverse_impl)(*qkv_c, gates_t, pick_ref[0], pick_ref[1], pick_ref[2], entry, tri_ref[...])
        sf_ref[...] = entry[:DN_HEADS]
        sr_ref[...] = entry[DN_HEADS:]
        _unstack_chains(out, of_ref, or_ref)
        state[...] = new

    in_specs = []
    for rev in (False, True):
        in_specs += [_dn_row_spec(nc, col, rev) for col in (0, 1, 2)] + [_dn_row_spec(nc, 0, rev, LANES)] * 3
    in_specs += [_const_spec(tri), _const_spec(picks)]
    return pl.pallas_call(
        kern, name=name, grid=(nc,), in_specs=in_specs,
        out_specs=[_dn_row_spec(nc, 0, False), _dn_row_spec(nc, 0, True), _dn_state_spec(nc, False), _dn_state_spec(nc, True)],
        out_shape=[jax.ShapeDtypeStruct((s_len, TOK_WIDTH), F32)] * 2
        + [jax.ShapeDtypeStruct((nc, DN_HEADS, DN_HEAD_DIM, DN_HEAD_DIM), F32)] * 2,
        scratch_shapes=[pltpu.VMEM((2 * DN_HEADS, DN_HEAD_DIM, DN_HEAD_DIM), F32)],
        compiler_params=_params("arbitrary"),
    )(*([qkv, qkv, qkv, *gates] * 2), tri, picks)


def _dn_scan_bwd(qkv, gates, states, d_o, *, name):
    s_len = qkv.shape[0]
    nc = s_len // DN_CHUNK
    tri, picks = _dn_tri(), _dn_gate_picks()

    def kern(*refs):
        ins, tri_ref, pick_ref, outs, d_state = refs[:16], refs[16], refs[17], refs[18:30], refs[30]

        @pl.when(pl.program_id(0) == 0)
        def _():
            d_state[...] = jnp.zeros_like(d_state)

        qkv_c = [_stack_chains(ins[i], ins[8 + i]) for i in range(3)]
        gates_t = _gates_transposed(ins[3:6], ins[11:14])
        entry = jnp.concatenate([ins[6][...], ins[14][...]], axis=0)
        d_out = _stack_chains(ins[7], ins[15])
        tri_v, picks_v = tri_ref[...], pick_ref[...]
        _, vjp = jax.vjp(lambda q, k, v, g, s: _dn_chains(_unit_tri_inverse)(q, k, v, g, picks_v[0], picks_v[1], picks_v[2], s, tri_v),
                         *qkv_c, gates_t, entry)
        dq, dk, dv, d_gates_t, d_entry = vjp((d_out, d_state[...]))
        for i, val in enumerate((dq, dk, dv)):
            _unstack_chains(val, outs[i], outs[6 + i])
        for i in range(3):
            outs[3 + i][...] = d_gates_t[i, :DN_CHUNK].T
            outs[9 + i][...] = d_gates_t[i, DN_CHUNK:].T
        d_state[...] = d_entry

    in_specs, out_specs, out_shape = [], [], []
    for rev in (True, False):
        in_specs += [_dn_row_spec(nc, col, rev) for col in (0, 1, 2)] + [_dn_row_spec(nc, 0, rev, LANES)] * 3
        in_specs += [_dn_state_spec(nc, rev), _dn_row_spec(nc, 0, rev)]
        out_specs += [_dn_row_spec(nc, 0, rev)] * 3 + [_dn_row_spec(nc, 0, rev, LANES)] * 3
        out_shape += [jax.ShapeDtypeStruct((s_len, TOK_WIDTH), F32)] * 3 + [jax.ShapeDtypeStruct((s_len, LANES), F32)] * 3
    in_specs += [_const_spec(tri), _const_spec(picks)]
    res = pl.pallas_call(
        kern, name=name, grid=(nc,), in_specs=in_specs, out_specs=out_specs, out_shape=out_shape,
        scratch_shapes=[pltpu.VMEM((2 * DN_HEADS, DN_HEAD_DIM, DN_HEAD_DIM), F32)],
        compiler_params=_params("arbitrary"),
    )(*[a for d in range(2) for a in (qkv, qkv, qkv, *gates, states[d], d_o)], tri, picks)
    return (res[0:3], res[3:6]), (res[6:9], res[9:12])


def _dn_out_head(o_f, o_b, z, gain):
    o = o_f + o_b
    return o * lax.rsqrt(jnp.mean(o * o, axis=-1, keepdims=True) + EPS) * gain * _silu(z)


def _dn_out(o_fwd, o_rev, proj, gain, qkv_kv_mem, *, name):
    def body(of, ob, z, qm, g, kv):
        heads = []
        for h in range(DN_HEADS):
            sl = slice(h * DN_HEAD_DIM, (h + 1) * DN_HEAD_DIM)
            heads.append(_dn_out_head(of[:, sl], ob[:, sl], z[:, sl], g))
        return (jnp.concatenate(heads + [_mem_attn(qm, kv)], axis=1),), ()
    return _rowwise(body, [o_fwd, o_rev, _col(proj, TOK_WIDTH, 3),
                           _col(proj, MEM_WIDTH, (4 * TOK_WIDTH) // MEM_WIDTH)], [gain, qkv_kv_mem],
                    [(D_MODEL, MXU_DTYPE)], [], tm=256, name=name)[0]


def _dn_out_bwd(o_fwd, o_rev, proj, gain, kv_mem, dcat, *, name):
    def body(of, ob, z, qm, dcat, g, kv):
        dos, dzs = [], []
        dgain = jnp.zeros_like(g)
        for h in range(DN_HEADS):
            sl = slice(h * DN_HEAD_DIM, (h + 1) * DN_HEAD_DIM)
            _, vjp = jax.vjp(_dn_out_head, of[:, sl], ob[:, sl], z[:, sl], g)
            d_of, _, dz, dg = vjp(dcat[:, sl])
            dos.append(d_of)
            dzs.append(dz)
            dgain = dgain + dg
        dqm, dkv = _mem_attn_bwd(qm, kv, dcat[:, TOK_WIDTH:])
        return (jnp.concatenate(dos, axis=1), jnp.concatenate(dzs, axis=1), dqm), (dgain, dkv)
    return _rowwise(body, [o_fwd, o_rev, _col(proj, TOK_WIDTH, 3),
                           _col(proj, MEM_WIDTH, (4 * TOK_WIDTH) // MEM_WIDTH), dcat], [gain, kv_mem],
                    [(TOK_WIDTH, F32), (TOK_WIDTH, F32), (MEM_WIDTH, F32)], [gain.shape, kv_mem.shape], tm=256, name=name)


def _pad_dn_w_in(w):
    gates = w[:, 4 * TOK_WIDTH: 4 * TOK_WIDTH + N_GATES]
    zeros = jnp.zeros((w.shape[0], DN_IN_PAD - DN_IN), w.dtype)
    return jnp.concatenate([w[:, :4 * TOK_WIDTH], w[:, 4 * TOK_WIDTH + N_GATES:], gates, zeros], axis=1)


def _unpad_dn_w_in(w):
    q_mem = w[:, 4 * TOK_WIDTH: 4 * TOK_WIDTH + MEM_WIDTH]
    gates = w[:, 4 * TOK_WIDTH + MEM_WIDTH: 4 * TOK_WIDTH + MEM_WIDTH + N_GATES]
    return jnp.concatenate([w[:, :4 * TOK_WIDTH], gates, q_mem], axis=1)


def _ffn_fwd(h, w_gu, w_d, tag):
    gu = _mm(h, w_gu, name=f"ffn_gu_{tag}")
    act = _swiglu_act(gu, name=f"ffn_act_{tag}")
    return gu, act, _mm(act, w_d, name=f"ffn_down_{tag}")


def _ffn_bwd(h, gu, act, w_gu, w_d, df, tag):
    d_act = _mm(df, w_d, tb=True, name=f"ffn_dact_{tag}")
    d_wd = _mm(act, df, ta=True, name=f"ffn_dwd_{tag}")
    d_gu = _swiglu_act_bwd(gu, d_act, name=f"ffn_dgu_{tag}")
    dh = _mm(d_gu, w_gu, tb=True, name=f"ffn_dh_{tag}")
    d_wgu = _mm(h, d_gu, ta=True, out_shards=_shards_of(w_gu), name=f"ffn_dwgu_{tag}")
    return dh, d_wgu, d_wd


def _local_step(x, mem, target, p):
    g = {}
    row = lambda v: v.reshape(1, -1)
    gains = {k: [row(p[k][i]) for i in range(2)] for k in
             ("mem_norm", "norm_mix_pre", "norm_mix_post", "norm_ffn_pre", "norm_ffn_post")}
    out_gain = row(p["dn_out_norm"])
    a_cols, dt_cols = _gate_params(p["dn_a_log"]), _gate_params(p["dn_dt_bias"])

    h0 = _pre_norm(x, gains["norm_mix_pre"][0], name="pre0")
    mem_n = [_pre_norm(mem, gains["mem_norm"][i], name=f"mem_norm{i}") for i in range(2)]
    kv_mem = [_mm(mem_n[i], p["mem_w_kv"][i], name=f"mem_kv{i}") for i in range(2)]
    qkvm = _mm(h0, p["att_w_in"], name="att_in")
    bias = [_bias_tiles(p["rel_bias"], gi) for gi in range(3)]
    att = [_att_fwd(qkvm, bias[gi], gi, name=f"att_fwd{gi}") for gi in range(3)]
    cat0, lse_tot = _att_combine([a[0] for a in att], [a[1] for a in att], qkvm, kv_mem[0], name="att_combine")
    mo0 = _mm(cat0, p["att_w_out"], name="att_out")
    x1, h1 = _post_pre(x, mo0, gains["norm_mix_post"][0], gains["norm_ffn_pre"][0], name="post_mix0")
    w_gu0, w_d0 = p["ffn_weights"](0, h1)
    gu0, act0, f0 = _ffn_fwd(h1, w_gu0, w_d0, 0)
    x2, h2 = _post_pre(x1, f0, gains["norm_ffn_post"][0], gains["norm_mix_pre"][1], name="post_ffn0")

    dn_w_in, dn_w_out = p["dn_weights"](h2)
    proj = _mm(h2, dn_w_in, name="dn_in")
    qkv = _dn_conv_fwd(proj, p["dn_conv"], name="dn_conv")
    gates = _dn_gates_fwd(proj, a_cols, dt_cols, name="dn_gates")
    o_fwd, o_rev, st_fwd, st_rev = _dn_scan_fwd(qkv, gates, name="dn_scan")
    cat1 = _dn_out(o_fwd, o_rev, proj, out_gain, kv_mem[1], name="dn_outnorm")
    mo1 = _mm(cat1, dn_w_out, name="dn_out")
    x3, h3 = _post_pre(x2, mo1, gains["norm_mix_post"][1], gains["norm_ffn_pre"][1], name="post_mix1")
    w_gu1, w_d1 = p["ffn_weights"](1, h3)
    gu1, act1, f1 = _ffn_fwd(h3, w_gu1, w_d1, 1)

    dx3, df1, dg_ffn_post1, loss_cols = _final_loss_bwd(x3, f1, gains["norm_ffn_post"][1], target, name="loss_bwd")
    dh3, d_wgu1, d_wd1 = _ffn_bwd(h3, gu1, act1, w_gu1, w_d1, df1, 1)
    dx2, dmo1, dg_mix_post1, dg_ffn_pre1 = _post_pre_bwd(x2, mo1, gains["norm_mix_post"][1], gains["norm_ffn_pre"][1],
                                                         dx3, dh3, name="post_mix1_bwd")
    dcat1 = _mm(dmo1, dn_w_out, tb=True, name="dn_out_dx")
    g["dn_w_out"] = _mm(cat1, dmo1, ta=True, name="dn_out_dw")
    d_o, dz, dqm1, d_out_gain, dkv1 = _dn_out_bwd(o_fwd, o_rev, proj, out_gain, kv_mem[1], dcat1, name="dn_outnorm_bwd")
    (d_f, dg_f), (d_r, dg_r) = _dn_scan_bwd(qkv, gates, (st_fwd, st_rev), d_o, name="dn_scan_bwd")
    d_gate_cols, d_a_cols, d_dt_cols = _dn_gates_bwd(proj, a_cols, dt_cols, (dg_f, dg_r), name="dn_gates_bwd")
    d_pre, d_conv = zip(*[_dn_conv_bwd(proj, p["dn_conv"], d_f[which], d_r[which], which, name=f"dn_conv_bwd{which}")
                          for which in range(3)])
    dproj = jnp.concatenate(list(d_pre) + [dz, dqm1, d_gate_cols], axis=1).astype(MXU_DTYPE)
    dh2 = _mm(dproj, dn_w_in, tb=True, name="dn_in_dx")
    g["dn_w_in"] = _mm(h2, dproj, ta=True, name="dn_in_dw")
    g["dn_conv"] = jnp.concatenate(d_conv, axis=1)
    g["dn_a_log"] = _gate_params_bwd(d_a_cols)
    g["dn_dt_bias"] = _gate_params_bwd(d_dt_cols)
    g["dn_out_norm"] = d_out_gain

    dx1, df0, dg_ffn_post0, dg_mix_pre1 = _post_pre_bwd(x1, f0, gains["norm_ffn_post"][0], gains["norm_mix_pre"][1],
                                                        dx2, dh2, name="post_ffn0_bwd")
    dh1, d_wgu0, d_wd0 = _ffn_bwd(h1, gu0, act0, w_gu0, w_d0, df0, 0)
    dx0, dmo0, dg_mix_post0, dg_ffn_pre0 = _post_pre_bwd(x, mo0, gains["norm_mix_post"][0], gains["norm_ffn_pre"][0],
                                                         dx1, dh1, name="post_mix0_bwd")
    dcat0 = _mm(dmo0, p["att_w_out"], tb=True, name="att_out_dx")
    g["att_w_out"] = _mm(cat0, dmo0, ta=True, name="att_out_dw")
    delta, dqm0, dkv0 = _att_bwd_prep(cat0, dcat0, qkvm, kv_mem[0], name="att_bwd_prep")
    att_b = [_att_bwd(qkvm, bias[gi], lse_tot, delta, dcat0, gi, name=f"att_bwd{gi}") for gi in range(3)]
    dqkvm = jnp.concatenate([a[w] for w in range(3) for a in att_b] + [dqm0], axis=1).astype(MXU_DTYPE)
    g["rel_bias"] = sum(_bias_tiles_bwd(p["rel_bias"], att_b[gi][3], gi) for gi in range(3))
    dh0 = _mm(dqkvm, p["att_w_in"], tb=True, name="att_in_dx")
    g["att_w_in"] = _mm(h0, dqkvm, ta=True, out_shards=_shards_of(p["att_w_in"]), name="att_in_dw")
    grad_x, dg_mix_pre0 = _pre_norm_bwd(x, gains["norm_mix_pre"][0], dh0, dx0, name="pre0_bwd")

    d_mem_kv, d_mem_norm = [], []
    for i, dkv in enumerate((dkv0, dkv1)):
        d_mem_kv.append(_mm(mem_n[i], dkv, ta=True, name=f"mem_kv_dw{i}"))
        d_mem_n = _mm(dkv, p["mem_w_kv"][i], tb=True, name=f"mem_kv_dx{i}")
        d_mem_norm.append(_gain_bwd(mem, gains["mem_norm"][i], d_mem_n, name=f"mem_norm_bwd{i}"))
    g["mem_w_kv"] = d_mem_kv
    g["mem_norm"] = jnp.concatenate(d_mem_norm, axis=0)
    g["norm_mix_pre"] = jnp.concatenate([dg_mix_pre0, dg_mix_pre1], axis=0)
    g["norm_mix_post"] = jnp.concatenate([dg_mix_post0, dg_mix_post1], axis=0)
    g["norm_ffn_pre"] = jnp.concatenate([dg_ffn_pre0, dg_ffn_pre1], axis=0)
    g["norm_ffn_post"] = jnp.concatenate([dg_ffn_post0, dg_ffn_post1], axis=0)
    g["ffn_w_gate_up"] = [d_wgu0, d_wgu1]
    g["ffn_w_down"] = [d_wd0, d_wd1]
    return loss_cols, grad_x, g


N_CHIPS = 4
N_DEV = 8
MESH = pl.DeviceIdType.MESH
BIG = (("att_w_in", (1, 1024, 640), 2), ("att_w_out", (1, 256, 1024), 1), ("dn_w_in", (1, 1024, 838), 2),
       ("dn_w_out", (1, 256, 1024), 1), ("mem_w_kv", (2, 256, 512), 1), ("ffn_w_gate_up", (2, 1024, 1408), 2),
       ("ffn_w_down", (2, 704, 1024), 1))


def _mesh_pos():
    return lax.axis_index("x"), lax.axis_index("y"), lax.axis_index("c")


def _other_chips(x, y):
    return [(1 - x, y), (x, 1 - y), (1 - x, 1 - y)]


ANY = pl.BlockSpec(memory_space=pl.ANY)


def _all_reduce_small(v, *, name):
    rows, cols = v.shape
    flips = [(dx, dy, dc) for dx in (0, 1) for dy in (0, 1) for dc in (0, 1)][1:]

    def body(v_ref, o_ref, buf, send_sems, recv_sems):
        x, y, c = _mesh_pos()

        def peer(f):
            return tuple(1 - p if fl else p for p, fl in zip((x, y, c), f))

        def index(p):
            return 4 * p[0] + 2 * p[1] + p[2]

        buf[index((x, y, c))] = v_ref[...]
        sends = []
        for k, f in enumerate(flips):
            cp = pltpu.make_async_remote_copy(src_ref=v_ref, dst_ref=buf.at[index((x, y, c))], send_sem=send_sems.at[k],
                                              recv_sem=recv_sems.at[k], device_id=peer(f), device_id_type=MESH)
            cp.start()
            sends.append(cp)
        for k, f in enumerate(flips):
            pltpu.make_async_remote_copy(src_ref=v_ref, dst_ref=buf.at[index(peer(f))], send_sem=send_sems.at[k],
                                         recv_sem=recv_sems.at[k], device_id=peer(f), device_id_type=MESH).wait_recv()
        for cp in sends:
            cp.wait_send()
        acc = buf[0]
        for d in range(1, N_DEV):
            acc = acc + buf[d]
        o_ref[...] = acc

    vmem = pl.BlockSpec(memory_space=pltpu.VMEM)
    return pl.pallas_call(
        body, name=name, in_specs=[vmem], out_specs=vmem, out_shape=jax.ShapeDtypeStruct((rows, cols), F32),
        scratch_shapes=[pltpu.VMEM((N_DEV, rows, cols), F32), pltpu.SemaphoreType.DMA((N_DEV - 1,)),
                        pltpu.SemaphoreType.DMA((N_DEV - 1,))],
    )(v)


def _adamw(w, g, m, v, *, name):
    def body(w, g, m, v):
        m = ADAM_B1 * m + (1.0 - ADAM_B1) * g
        v = ADAM_B2 * v + (1.0 - ADAM_B2) * (g * g)
        m_hat = m / (1.0 - ADAM_B1 ** ADAM_STEP)
        v_hat = v / (1.0 - ADAM_B2 ** ADAM_STEP)
        delta = -ADAM_LR * (m_hat / (jnp.sqrt(v_hat) + ADAM_EPS) + ADAM_WD * w)
        return (delta, m, v), ()
    rows, cols = w.shape
    return _rowwise(body, [w, g, m, v], [], [(cols, F32)] * 3, [], tm=_tile(rows, 256, SUBLANES), name=name)


def _pack_small(arrs, rows):
    flat = jnp.concatenate([a.reshape(-1) for a in arrs])
    return jnp.pad(flat, (0, rows * LANES - flat.shape[0])).reshape(rows, LANES)


def _unpack_small(packed, shapes):
    flat = packed.reshape(-1)
    out, off = [], 0
    for s in shapes:
        size = math.prod(s)
        out.append(flat[off: off + size].reshape(s))
        off += size
    return out


def _small_rows(shapes):
    return -(-sum(math.prod(s) for s in shapes) // (SUBLANES * LANES)) * SUBLANES


def _sem_pairs(n):
    return [pltpu.SemaphoreType.DMA((n,)), pltpu.SemaphoreType.DMA((n,))]


def _gather_blocks(blocks, *, name):
    n = len(blocks)

    def body(*refs):
        x_refs, out_refs, (send_sems, recv_sems) = refs[:n], refs[n: 2 * n], refs[2 * n:]
        x, y, c = _mesh_pos()
        sibling = (x, y, 1 - c)
        chips = _other_chips(x, y)

        def copy(k, src, dst, to):
            return pltpu.make_async_remote_copy(src_ref=src, dst_ref=dst, send_sem=send_sems.at[k],
                                                recv_sem=recv_sems.at[k], device_id=to, device_id_type=MESH)

        def part(b, chip, h):
            half = blocks[b].shape[0] // 2
            return out_refs[b].at[2 * chip[0] + chip[1], pl.ds(h * half, half), :]

        def my_half(b):
            half = blocks[b].shape[0] // 2
            return x_refs[b].at[pl.ds(c * half, half), :]

        first = [copy(6 * b + j, my_half(b), part(b, (x, y), c), (*chip, c)) for b in range(n) for j, chip in enumerate(chips)]
        for cp in first:
            cp.start()
        passed = []
        for b in range(n):
            for j, chip in enumerate(chips):
                copy(6 * b + j, my_half(b), part(b, chip, c), (*chip, c)).wait_recv()
                cp = copy(6 * b + 3 + j, part(b, chip, c), part(b, chip, c), sibling)
                cp.start()
                passed.append(cp)
        for b in range(n):
            for j, chip in enumerate(chips):
                copy(6 * b + 3 + j, part(b, chip, 1 - c), part(b, chip, 1 - c), sibling).wait_recv()
        for cp in first + passed:
            cp.wait_send()

    return pl.pallas_call(
        body, name=name, in_specs=[ANY] * n, out_specs=[ANY] * n,
        out_shape=[jax.ShapeDtypeStruct((N_CHIPS, *a.shape), a.dtype) for a in blocks],
        scratch_shapes=_sem_pairs(6 * n),
    )(*blocks)


HBM = pl.BlockSpec(memory_space=pltpu.HBM)
SEM = pl.BlockSpec(memory_space=pltpu.SEMAPHORE)
DATAFLOW = pltpu.SideEffectType.DATAFLOW_SIDE_EFFECTING


def _gather_start(blocks, *, name):
    n = len(blocks)
    lands = [lax.empty((N_CHIPS, *a.shape), a.dtype) for a in blocks]

    def body(*refs):
        x_refs, land_refs, send_sems, recv_sems, token = refs[:n], refs[n: 2 * n], refs[2 * n], refs[2 * n + 1], refs[-1]
        x, y, c = _mesh_pos()
        for b in range(n):
            for j, chip in enumerate(_other_chips(x, y)):
                pltpu.make_async_remote_copy(src_ref=x_refs[b], dst_ref=land_refs[b].at[2 * x + y], send_sem=send_sems.at[3 * b + j],
                                             recv_sem=recv_sems.at[3 * b + j], device_id=(*chip, c), device_id_type=MESH).start()
        token[...] = jnp.zeros_like(token)

    operands = [pltpu.with_memory_space_constraint(a, pltpu.HBM) for a in blocks + lands]
    res = pl.pallas_call(
        body, name=name, in_specs=[HBM] * (2 * n),
        out_shape=(pltpu.SemaphoreType.DMA((3 * n,)), pltpu.SemaphoreType.DMA((3 * n,)),
                   *[pltpu.HBM(a.shape, a.dtype) for a in operands], jax.ShapeDtypeStruct((SUBLANES, LANES), F32)),
        out_specs=(SEM, SEM, *[HBM] * (2 * n), pl.BlockSpec(memory_space=pltpu.VMEM)),
        input_output_aliases={i: 2 + i for i in range(2 * n)},
        compiler_params=pltpu.CompilerParams(has_side_effects=DATAFLOW),
    )(*operands)
    return res[0], res[1], list(res[2: 2 + n]), list(res[2 + n: 2 + 2 * n]), res[-1]


def _gather_wait(started, after, *, name):
    send_sems, recv_sems, blocks, lands, _ = started
    n = len(blocks)

    def body(*refs):
        x_refs, land_refs, send_sems, recv_sems = refs[:n], refs[n: 2 * n], refs[2 * n], refs[2 * n + 1]
        x, y, c = _mesh_pos()
        for b in range(n):
            for j, chip in enumerate(_other_chips(x, y)):
                cp = pltpu.make_async_remote_copy(src_ref=x_refs[b], dst_ref=land_refs[b].at[2 * chip[0] + chip[1]],
                                                  send_sem=send_sems.at[3 * b + j], recv_sem=recv_sems.at[3 * b + j],
                                                  device_id=(*chip, c), device_id_type=MESH)
                cp.wait_send()
                cp.wait_recv()

    res = pl.pallas_call(
        body, name=name, in_specs=(*[HBM] * (2 * n), SEM, SEM, ANY),
        out_shape=tuple(pltpu.HBM(a.shape, a.dtype) for a in blocks + lands), out_specs=tuple([HBM] * (2 * n)),
        input_output_aliases={i: i for i in range(2 * n)},
        compiler_params=pltpu.CompilerParams(has_side_effects=DATAFLOW),
    )(*blocks, *lands, send_sems, recv_sems, after)
    return list(res[n:])


def _swap_halves(blocks, own_rows, *, name):
    n = len(blocks)

    def body(*refs):
        in_refs, out_refs, (send_sems, recv_sems) = refs[:n], refs[n: 2 * n], refs[2 * n:]
        x, y, c = _mesh_pos()
        copies = [pltpu.make_async_remote_copy(src_ref=own_rows(in_refs[b], c), dst_ref=out_refs[b], send_sem=send_sems.at[b],
                                               recv_sem=recv_sems.at[b], device_id=(x, y, 1 - c), device_id_type=MESH)
                  for b in range(n)]
        for cp in copies:
            cp.start()
        for cp in copies:
            cp.wait()

    def sent_shape(a):
        return jax.eval_shape(lambda r: own_rows(r, 0), a)

    return pl.pallas_call(
        body, name=name, in_specs=[ANY] * n, out_specs=[ANY] * n,
        out_shape=[jax.ShapeDtypeStruct(sent_shape(a).shape, a.dtype) for a in blocks],
        scratch_shapes=_sem_pairs(n),
    )(*blocks)


def _other_half_rows(ref, c):
    half = ref.shape[1] // 2
    return ref[:, (1 - c) * half: (2 - c) * half, :] if isinstance(c, int) else ref.at[:, pl.ds((1 - c) * half, half), :]


def _whole(ref, c):
    return ref


def _add_own_half_block(block, received, core, *, name):
    n, rows, cols = block.shape
    half = rows // 2
    tm = _tile(half, 512, 2 * SUBLANES)
    nb = half // tm

    def kern(c_ref, a_ref, b_ref, o_ref):
        o_ref[...] = (a_ref[...] + b_ref[...]).astype(o_ref.dtype)

    return pl.pallas_call(
        kern, name=name,
        grid_spec=pltpu.PrefetchScalarGridSpec(
            num_scalar_prefetch=1, grid=(n, nb),
            in_specs=[pl.BlockSpec((None, tm, cols), lambda s, i, c: (s, c[0] * nb + i, 0)),
                      pl.BlockSpec((None, tm, cols), lambda s, i, c: (s, i, 0))],
            out_specs=pl.BlockSpec((None, tm, cols), lambda s, i, c: (s, i, 0))),
        out_shape=jax.ShapeDtypeStruct((n, half, cols), LINK_DTYPE),
        compiler_params=_params("parallel", "parallel"),
    )(core, block, received)


def _scatter_blocks(sums, *, name):
    n = len(sums)

    def body(*refs):
        s_refs, out_refs, (send_sems, recv_sems) = refs[:n], refs[n: 2 * n], refs[2 * n:]
        x, y, c = _mesh_pos()
        me = 2 * x + y
        chips = _other_chips(x, y)

        def copy(b, j, chip, src_slot, dst_slot):
            return pltpu.make_async_remote_copy(src_ref=s_refs[b].at[src_slot], dst_ref=out_refs[b].at[dst_slot],
                                                send_sem=send_sems.at[3 * b + j], recv_sem=recv_sems.at[3 * b + j],
                                                device_id=(*chip, c), device_id_type=MESH)

        sends = [copy(b, j, chip, 2 * chip[0] + chip[1], me) for b in range(n) for j, chip in enumerate(chips)]
        for cp in sends:
            cp.start()
        for b in range(n):
            for j, chip in enumerate(chips):
                copy(b, j, chip, me, 2 * chip[0] + chip[1]).wait_recv()
        for cp in sends:
            cp.wait_send()

    return pl.pallas_call(
        body, name=name, in_specs=[ANY] * n, out_specs=[ANY] * n,
        out_shape=[jax.ShapeDtypeStruct(a.shape, a.dtype) for a in sums], scratch_shapes=_sem_pairs(3 * n),
    )(*sums)


def _sum_chips_block(parts, *, name):
    n, half, cols = parts.shape
    tm = _tile(half, 512, 2 * SUBLANES)

    def kern(p_ref, o_ref):
        acc = p_ref[0].astype(F32)
        for s in range(1, n):
            acc = acc + p_ref[s].astype(F32)
        o_ref[...] = acc

    return pl.pallas_call(
        kern, name=name, grid=(half // tm,),
        in_specs=[pl.BlockSpec((n, tm, cols), lambda i: (0, i, 0))],
        out_specs=pl.BlockSpec((tm, cols), lambda i: (i, 0)),
        out_shape=jax.ShapeDtypeStruct((half, cols), F32),
        compiler_params=_params("parallel"),
    )(parts)


def _reduce_scatter_blocks(blocks, names):
    x, y, c = _mesh_pos()
    chip = 2 * x + y
    core = c.astype(jnp.int32).reshape(1)
    received = _swap_halves(blocks, _other_half_rows, name="rs_swap")
    sums = [_add_own_half_block(b, r, core, name=f"rs_add_{nm}") for b, r, nm in zip(blocks, received, names, strict=True)]
    parts = _scatter_blocks(sums, name="rs_scatter")
    parts = [lax.dynamic_update_slice(p, lax.dynamic_slice_in_dim(s, chip, 1, axis=0), (chip, 0, 0))
             for p, s in zip(parts, sums, strict=True)]
    mine = [_sum_chips_block(p, name=f"rs_sum_{nm}") for p, nm in zip(parts, names, strict=True)]
    other = _swap_halves(mine, _whole, name="rs_join")
    return [jnp.concatenate([jnp.where(c == 0, a, b), jnp.where(c == 0, b, a)], axis=0) for a, b in zip(mine, other, strict=True)]


WEIGHTS = ("rel_bias", "att_w_in", "att_w_out", "dn_w_in", "dn_conv", "dn_a_log", "dn_dt_bias", "dn_out_norm", "dn_w_out",
           "mem_norm", "mem_w_kv", "norm_mix_pre", "norm_mix_post", "norm_ffn_pre", "norm_ffn_post", "ffn_w_gate_up",
           "ffn_w_down")
BIG_NAMES = tuple(n for n, _, _ in BIG)
SMALL_NAMES = tuple(n for n in WEIGHTS if n not in BIG_NAMES)
CONV_COLS = 3 * TOK_WIDTH
CONV_SHARD = CONV_COLS // N_CHIPS
BLOCKS = tuple((n, layer) for n, shape, _ in BIG for layer in range(shape[0]))
COLUMN_SHARDED = {n: axis == 2 for n, _, axis in BIG}


def kernel(x, mem, rel_bias, att_w_in, att_w_out, dn_w_in, dn_conv, dn_a_log, dn_dt_bias, dn_out_norm, dn_w_out, mem_norm, mem_w_kv, norm_mix_pre, norm_mix_post, norm_ffn_pre, norm_ffn_post, ffn_w_gate_up, ffn_w_down, loss_target, m_rel_bias, m_att_w_in, m_att_w_out, m_dn_w_in, m_dn_conv, m_dn_a_log, m_dn_dt_bias, m_dn_out_norm, m_dn_w_out, m_mem_norm, m_mem_w_kv, m_norm_mix_pre, m_norm_mix_post, m_norm_ffn_pre, m_norm_ffn_post, m_ffn_w_gate_up, m_ffn_w_down, v_rel_bias, v_att_w_in, v_att_w_out, v_dn_w_in, v_dn_conv, v_dn_a_log, v_dn_dt_bias, v_dn_out_norm, v_dn_w_out, v_mem_norm, v_mem_w_kv, v_norm_mix_pre, v_norm_mix_post, v_norm_ffn_pre, v_norm_ffn_post, v_ffn_w_gate_up, v_ffn_w_down):
    w = dict(zip(WEIGHTS, (rel_bias, att_w_in, att_w_out, dn_w_in, dn_conv, dn_a_log, dn_dt_bias, dn_out_norm, dn_w_out,
                           mem_norm, mem_w_kv, norm_mix_pre, norm_mix_post, norm_ffn_pre, norm_ffn_post, ffn_w_gate_up,
                           ffn_w_down)))
    m = dict(zip(WEIGHTS, (m_rel_bias, m_att_w_in, m_att_w_out, m_dn_w_in, m_dn_conv, m_dn_a_log, m_dn_dt_bias,
                           m_dn_out_norm, m_dn_w_out, m_mem_norm, m_mem_w_kv, m_norm_mix_pre, m_norm_mix_post,
                           m_norm_ffn_pre, m_norm_ffn_post, m_ffn_w_gate_up, m_ffn_w_down)))
    v = dict(zip(WEIGHTS, (v_rel_bias, v_att_w_in, v_att_w_out, v_dn_w_in, v_dn_conv, v_dn_a_log, v_dn_dt_bias,
                           v_dn_out_norm, v_dn_w_out, v_mem_norm, v_mem_w_kv, v_norm_mix_pre, v_norm_mix_post,
                           v_norm_ffn_pre, v_norm_ffn_post, v_ffn_w_gate_up, v_ffn_w_down)))
    cx, cy, cc = _mesh_pos()
    chip = 2 * cx + cy

    local = dict(zip(BLOCKS, lax.optimization_barrier([w[n][layer].astype(MXU_DTYPE) for n, layer in BLOCKS]), strict=True))

    def usable(block, got):
        got = lax.dynamic_update_slice(got, local[block][None], (chip, 0, 0))
        return got if COLUMN_SHARDED[block[0]] else got.reshape(-1, got.shape[-1])

    late = {"ffn0": [("ffn_w_gate_up", 0), ("ffn_w_down", 0)], "dn": [("dn_w_in", 0), ("dn_w_out", 0)],
            "ffn1": [("ffn_w_gate_up", 1), ("ffn_w_down", 1)]}
    first = [b for b in BLOCKS if all(b not in blks for blks in late.values())]
    first_got = _gather_blocks([local[b] for b in first], name="gather_weights")
    late_local, _ = lax.optimization_barrier(({k: [local[b] for b in blks] for k, blks in late.items()}, first_got[0]))
    started = {k: _gather_start(late_local[k], name=f"gather_start_{k}") for k in late}
    started_token = sum(s[4][0, 0] for s in started.values())

    def late_weights(key, after):
        lands = _gather_wait(started[key], after, name=f"gather_wait_{key}")
        return [usable(b, got) for b, got in zip(late[key], lands, strict=True)]

    def dn_weights(after):
        w_in, w_out = late_weights("dn", after)
        return _pad_dn_w_in(jnp.concatenate([w_in[s] for s in range(N_CHIPS)], axis=1)), w_out

    full = {}
    for b, got in zip(first, first_got, strict=True):
        full.setdefault(b[0], []).append(usable(b, got))
    conv_rows = _small_rows([(DN_CONV, CONV_COLS)])
    conv_mine = jnp.where(cc == 0, 1.0, 0.0) * w["dn_conv"][0]
    conv_placed = lax.dynamic_update_slice(jnp.zeros((DN_CONV, CONV_COLS), F32), conv_mine, (0, chip * CONV_SHARD))
    conv_full = _unpack_small(_all_reduce_small(_pack_small([conv_placed], conv_rows), name="gather_conv"),
                              [(DN_CONV, CONV_COLS)])[0]
    p = {
        "rel_bias": w["rel_bias"], "att_w_in": full["att_w_in"][0], "att_w_out": full["att_w_out"][0],
        "dn_conv": conv_full, "dn_a_log": w["dn_a_log"][0], "dn_dt_bias": w["dn_dt_bias"][0],
        "dn_out_norm": w["dn_out_norm"][0], "mem_norm": w["mem_norm"], "mem_w_kv": full["mem_w_kv"],
        "norm_mix_pre": w["norm_mix_pre"] + started_token,
        "norm_mix_post": w["norm_mix_post"], "norm_ffn_pre": w["norm_ffn_pre"], "norm_ffn_post": w["norm_ffn_post"],
        "ffn_weights": lambda layer, after: late_weights(f"ffn{layer}", after), "dn_weights": dn_weights,
    }

    loss_cols, grad_x, g = _local_step(x[0], mem[0], loss_target[0], p)
    loss = lax.psum(jnp.sum(loss_cols), ("x", "y", "c"))

    g_layers = {"att_w_in": [g["att_w_in"]], "att_w_out": [g["att_w_out"]], "dn_w_in": [_unpad_dn_w_in(g["dn_w_in"])],
                "dn_w_out": [g["dn_w_out"]], "mem_w_kv": g["mem_w_kv"], "ffn_w_gate_up": g["ffn_w_gate_up"],
                "ffn_w_down": g["ffn_w_down"]}

    def chip_blocks(n, a):
        if a.ndim == 3:
            return a
        if COLUMN_SHARDED[n]:
            return a.reshape(a.shape[0], N_CHIPS, -1).transpose(1, 0, 2)
        return a.reshape(N_CHIPS, -1, a.shape[-1])

    reduced = _reduce_scatter_blocks([chip_blocks(n, g_layers[n][layer]) for n, layer in BLOCKS],
                                     [f"{n}{layer}" for n, layer in BLOCKS])
    grads = {n: jnp.concatenate([r for (bn, _), r in zip(BLOCKS, reduced, strict=True) if bn == n], axis=0).reshape(shape)
             for n, shape, _ in BIG}
    small_full_shapes = [(DN_CONV, CONV_COLS) if n == "dn_conv" else w[n].shape for n in SMALL_NAMES]
    small_sum = _all_reduce_small(_pack_small([g[n] for n in SMALL_NAMES], _small_rows(small_full_shapes)), name="reduce_small")
    for n, s in zip(SMALL_NAMES, _unpack_small(small_sum, small_full_shapes)):
        grads[n] = lax.dynamic_slice(s, (0, chip * CONV_SHARD), (DN_CONV, CONV_SHARD))[None] if n == "dn_conv" else s

    delta, new_m, new_v = {}, {}, {}
    for n in BIG_NAMES:
        shape = w[n].shape
        two_d = lambda a: a.reshape(-1, shape[-1])
        res = _adamw(two_d(w[n]), two_d(grads[n]), two_d(m[n]), two_d(v[n]), name=f"adamw_{n}")
        delta[n], new_m[n], new_v[n] = (r.reshape(shape) for r in res)
    small_shapes = [w[n].shape for n in SMALL_NAMES]
    rows = _small_rows(small_shapes)
    res = _adamw(*[_pack_small([d[n] for n in SMALL_NAMES], rows) for d in (w, grads, m, v)], name="adamw_small")
    for d, r in zip((delta, new_m, new_v), res):
        for n, a in zip(SMALL_NAMES, _unpack_small(r, small_shapes)):
            d[n] = a
    return (loss, grad_x[None], *[grads[n] for n in WEIGHTS], *[delta[n] for n in WEIGHTS],
            *[new_m[n] for n in WEIGHTS], *[new_v[n] for n in WEIGHTS])
```

```python
import functools
import math

import numpy as np
import jax
import jax.numpy as jnp
from jax import lax
from jax.experimental import pallas as pl
from jax.experimental.pallas import tpu as pltpu

F32 = jnp.float32
MXU_DTYPE = jnp.bfloat16
LINK_DTYPE = jnp.bfloat16
HI = lax.Precision.HIGHEST

EPS = 1e-6
NEG_INF = -1e30
LANES = 128
SUBLANES = 8
VMEM_LIMIT = 56 * 1024 * 1024

D_MODEL = 1024
TOK_WIDTH = 768
MEM_WIDTH = 256
MEM_LEN = 256
ATT_HEAD_DIM = 64
DILATIONS = (1, 4, 16)
HALF = 64
ATT_BQ = 128
ATT_W = ATT_BQ + 2 * HALF
REL_BUCKETS = 32
REL_MAX_DIST = 1024
DN_HEADS = 6
DN_HEAD_DIM = 128
DN_CONV = 5
DN_CHUNK = 128
D_FF = 2816
ATT_IN = 2560
DN_IN = 3352
DN_IN_PAD = 3456
N_GATES = 4 * DN_HEADS

ADAM_LR = 0.001
ADAM_B1 = 0.9
ADAM_B2 = 0.999
ADAM_EPS = 1e-08
ADAM_WD = 0.01
ADAM_STEP = 10


def _tile(n, target, align):
    if n <= target:
        return n
    t = (target // align) * align
    while t >= align:
        if n % t == 0:
            return t
        t -= align
    raise ValueError(f"no tile for {n} (target {target}, align {align})")


def _params(*sem):
    return pltpu.CompilerParams(dimension_semantics=sem, vmem_limit_bytes=VMEM_LIMIT)


def _mm(a, b, *, name, ta=False, tb=False, out_shards=None, tm=1024, tn=1408, tk=1408, out_dtype=F32):
    if ta:
        K, M = a.shape
    else:
        M, K = a.shape
    sharded_b = b.ndim == 3
    if sharded_b:
        n_sh, b_rows, b_cols = b.shape
        N, K2 = (b_rows, n_sh * b_cols) if tb else (n_sh * b_cols, b_rows)
    else:
        N, K2 = b.shape if tb else b.shape[::-1]
    assert K == K2, (a.shape, b.shape, ta, tb)
    tm = _tile(M, tm, LANES if ta else SUBLANES)
    tn = N // out_shards if out_shards else (b_cols if sharded_b and not tb else _tile(N, tn, LANES))
    tk = b_cols if sharded_b and tb else _tile(K, tk, LANES)
    nk = K // tk
    a_spec = pl.BlockSpec((tk, tm), lambda i, j, k: (k, i)) if ta else pl.BlockSpec((tm, tk), lambda i, j, k: (i, k))
    if sharded_b:
        b_spec = (pl.BlockSpec((None, tn, tk), lambda i, j, k: (k, j, 0)) if tb
                  else pl.BlockSpec((None, tk, tn), lambda i, j, k: (j, k, 0)))
    else:
        b_spec = pl.BlockSpec((tn, tk), lambda i, j, k: (j, k)) if tb else pl.BlockSpec((tk, tn), lambda i, j, k: (k, j))
    if out_shards:
        out_spec = pl.BlockSpec((None, tm, tn), lambda i, j, k: (j, i, 0))
        out_shape = jax.ShapeDtypeStruct((out_shards, M, tn), out_dtype)
    else:
        out_spec = pl.BlockSpec((tm, tn), lambda i, j, k: (i, j))
        out_shape = jax.ShapeDtypeStruct((M, N), out_dtype)
    dims = (((0 if ta else 1,), (1 if tb else 0,)), ((), ()))

    def kern(a_ref, b_ref, o_ref, acc_ref):
        k = pl.program_id(2)

        @pl.when(k == 0)
        def _():
            acc_ref[...] = jnp.zeros_like(acc_ref)

        acc_ref[...] += lax.dot_general(a_ref[...].astype(MXU_DTYPE), b_ref[...].astype(MXU_DTYPE), dims,
                                        preferred_element_type=F32)

        @pl.when(k == nk - 1)
        def _():
            o_ref[...] = acc_ref[...].astype(o_ref.dtype)

    return pl.pallas_call(
        kern, name=name, grid=(M // tm, N // tn, nk), in_specs=[a_spec, b_spec],
        out_specs=out_spec, out_shape=out_shape,
        scratch_shapes=[pltpu.VMEM((tm, tn), F32)],
        compiler_params=_params("parallel", "parallel", "arbitrary"),
    )(a, b)


def _shards_of(w):
    return w.shape[0] if w.ndim == 3 else None


def _col(arr, width, blk):
    return (arr, width, blk)


def _rowwise(body, rows, consts, out_rows, out_acc, *, tm, name):
    n_rows = (rows[0][0] if isinstance(rows[0], tuple) else rows[0]).shape[0]
    assert n_rows % tm == 0, (n_rows, tm)
    arrs, in_specs = [], []
    for r in rows:
        arr, width, blk = r if isinstance(r, tuple) else (r, r.shape[1], 0)
        assert arr.shape[0] == n_rows
        arrs.append(arr)
        in_specs.append(pl.BlockSpec((tm, width), functools.partial(lambda i, b: (i, b), b=blk)))
    for c in consts:
        arrs.append(c)
        in_specs.append(pl.BlockSpec(c.shape, functools.partial(lambda i, n: (0,) * n, n=c.ndim)))
    n_in, n_ro = len(arrs), len(out_rows)
    out_shape = [jax.ShapeDtypeStruct((n_rows, w), dt) for w, dt in out_rows]
    out_specs = [pl.BlockSpec((tm, w), lambda i: (i, 0)) for w, _ in out_rows]
    out_shape += [jax.ShapeDtypeStruct(s, F32) for s in out_acc]
    out_specs += [pl.BlockSpec(s, lambda i: (0, 0)) for s in out_acc]

    def kern(*refs):
        ro, ao = body(*[r[...] for r in refs[:n_in]])
        outs = refs[n_in:]
        for r, v in zip(outs[:n_ro], ro, strict=True):
            r[...] = v.astype(r.dtype)
        if out_acc:
            @pl.when(pl.program_id(0) == 0)
            def _():
                for r in outs[n_ro:]:
                    r[...] = jnp.zeros_like(r)

            for r, v in zip(outs[n_ro:], ao, strict=True):
                r[...] += v

    res = pl.pallas_call(
        kern, name=name, grid=(n_rows // tm,), in_specs=in_specs, out_specs=out_specs, out_shape=out_shape,
        compiler_params=_params("arbitrary" if out_acc else "parallel"),
    )(*arrs)
    return res


def _rms(x, gain):
    return x * lax.rsqrt(jnp.mean(x * x, axis=-1, keepdims=True) + EPS) * gain


def _silu(x):
    return x * jax.nn.sigmoid(x)


def _softplus(x):
    return jnp.maximum(x, 0.0) + jnp.log(1.0 + jnp.exp(-jnp.abs(x)))


def _dot_nt(a, b, precision=None):
    return lax.dot_general(a, b, (((1,), (1,)), ((), ())), preferred_element_type=F32, precision=precision)


def _dot_tn(a, b, precision=None):
    return lax.dot_general(a, b, (((0,), (0,)), ((), ())), preferred_element_type=F32, precision=precision)


def _dot(a, b, precision=None):
    return jnp.dot(a, b, preferred_element_type=F32, precision=precision)


def _pre_norm(x, gain, *, name):
    def body(x, g):
        return (_rms(x, g),), ()
    return _rowwise(body, [x], [gain], [(x.shape[1], MXU_DTYPE)], [], tm=_tile(x.shape[0], 512, 2 * SUBLANES), name=name)[0]


def _pre_norm_bwd(x, gain, dh, dx_other, *, name):
    def body(x, dh, dxo, g):
        _, vjp = jax.vjp(_rms, x, g)
        dx, dg = vjp(dh)
        return (dx + dxo,), (dg,)
    return _rowwise(body, [x, dh, dx_other], [gain], [(x.shape[1], F32)], [gain.shape], tm=512, name=name)


def _gain_bwd(x, gain, dh, *, name):
    def body(x, dh, g):
        _, vjp = jax.vjp(lambda g_: _rms(x, g_), g)
        return (), (vjp(dh)[0],)
    return _rowwise(body, [x, dh], [gain], [], [gain.shape], tm=_tile(x.shape[0], 512, SUBLANES), name=name)[0]


def _res_block(x_res, m, g_post, g_pre):
    x_new = x_res + _rms(m, g_post)
    return x_new, _rms(x_new, g_pre)


def _post_pre(x_res, m, g_post, g_pre, *, name):
    def body(x, m, gp, gq):
        return _res_block(x, m, gp, gq), ()
    d = x_res.shape[1]
    return _rowwise(body, [x_res, m], [g_post, g_pre], [(d, F32), (d, MXU_DTYPE)], [], tm=512, name=name)


def _post_pre_bwd(x_res, m, g_post, g_pre, dx_new, dh, *, name):
    def body(x, m, dxn, dh, gp, gq):
        _, vjp = jax.vjp(_res_block, x, m, gp, gq)
        dx, dm, dgp, dgq = vjp((dxn, dh))
        return (dx, dm), (dgp, dgq)
    d = x_res.shape[1]
    return _rowwise(body, [x_res, m, dx_new, dh], [g_post, g_pre], [(d, F32), (d, MXU_DTYPE)],
                    [g_post.shape, g_pre.shape], tm=256, name=name)


def _final_loss_bwd(x_res, m, g_post, target, *, name):
    d = x_res.shape[1]

    def loss_cols(x, m, g, t):
        err = x + _rms(m, g) - t
        return jnp.sum(err * err, axis=0, keepdims=True) * (0.5 / d)

    def body(x, m, t, g):
        cols, vjp = jax.vjp(lambda x_, m_, g_: loss_cols(x_, m_, g_, t), x, m, g)
        dx, dm, dg = vjp(jnp.ones_like(cols))
        return (dx, dm), (dg, cols)
    return _rowwise(body, [x_res, m, target], [g_post], [(d, F32), (d, MXU_DTYPE)], [g_post.shape, (1, d)], tm=256, name=name)


def _swiglu_act(gu, *, name):
    def body(gate, up):
        return (_silu(gate) * up,), ()
    return _rowwise(body, [_col(gu, D_FF, 0), _col(gu, D_FF, 1)], [], [(D_FF, MXU_DTYPE)], [], tm=256, name=name)[0]


def _swiglu_act_bwd(gu, da, *, name):
    def body(gate, up, da):
        _, vjp = jax.vjp(lambda g, u: _silu(g) * u, gate, up)
        dg, du = vjp(da)
        return (jnp.concatenate([dg, du], axis=1),), ()
    return _rowwise(body, [_col(gu, D_FF, 0), _col(gu, D_FF, 1), da], [], [(2 * D_FF, MXU_DTYPE)], [], tm=256, name=name)[0]


def _lane_head_mask(width, head_dim, head):
    lane = lax.broadcasted_iota(jnp.int32, (1, width), 1)
    return (lane // head_dim) == head


def _mem_attn_pair(q_pair, k_pair, v_pair):
    out = jnp.zeros_like(q_pair)
    for h in range(2):
        mh = _lane_head_mask(LANES, ATT_HEAD_DIM, h)
        qh = jnp.where(mh, q_pair * (ATT_HEAD_DIM ** -0.5), 0.0)
        logits = _dot_nt(qh, k_pair)
        mx = jnp.max(logits, axis=-1, keepdims=True)
        p = jnp.exp(logits - mx)
        p = p / jnp.sum(p, axis=-1, keepdims=True)
        out = out + jnp.where(mh, _dot(p, v_pair), 0.0)
    return out


def _mem_attn(q_mem, kv):
    outs = []
    for p in range(MEM_WIDTH // LANES):
        sl = slice(p * LANES, (p + 1) * LANES)
        outs.append(_mem_attn_pair(q_mem[:, sl], kv[:, sl], kv[:, MEM_WIDTH + p * LANES: MEM_WIDTH + (p + 1) * LANES]))
    return jnp.concatenate(outs, axis=1)


def _mem_attn_bwd(q_mem, kv, do):
    dqs, dks, dvs = [], [], []
    for p in range(MEM_WIDTH // LANES):
        sl = slice(p * LANES, (p + 1) * LANES)
        sv = slice(MEM_WIDTH + p * LANES, MEM_WIDTH + (p + 1) * LANES)
        _, vjp = jax.vjp(_mem_attn_pair, q_mem[:, sl], kv[:, sl], kv[:, sv])
        dq, dk, dv = vjp(do[:, sl])
        dqs.append(dq)
        dks.append(dk)
        dvs.append(dv)
    return jnp.concatenate(dqs, axis=1), jnp.concatenate(dks + dvs, axis=1)


def _t5_bucket(rel):
    half = REL_BUCKETS // 2
    max_exact = half // 2
    n = np.abs(rel)
    large = max_exact + (np.log(np.maximum(n, 1) / max_exact) / math.log(REL_MAX_DIST / max_exact)
                         * (half - max_exact)).astype(np.int64)
    large = np.minimum(large, half - 1)
    return ((rel > 0) * half + np.where(n < max_exact, n, large)).astype(np.int32)


ATT_DIAGS = ATT_BQ + ATT_W - 1


def _bias_diag_onehot(dil):
    j = np.arange(ATT_DIAGS)
    tiles = []
    for off in (-HALF, 0, HALF):
        rel = j - (ATT_BQ - 1) - HALF - off
        hot = _t5_bucket(rel * dil)[:, None] == np.arange(REL_BUCKETS)[None, :]
        tiles.append(hot & (np.abs(rel) <= HALF)[:, None])
    return np.stack(tiles).astype(np.float32)


def _toeplitz(r):
    lead = r.shape[:-1]
    a = jnp.broadcast_to(r[..., None, :], lead + (ATT_BQ, ATT_DIAGS))
    a = jnp.pad(a, [(0, 0)] * len(lead) + [(0, 0), (0, 1)])
    a = a.reshape(lead + (ATT_BQ * (ATT_DIAGS + 1),))[..., : ATT_BQ * ATT_DIAGS].reshape(lead + (ATT_BQ, ATT_DIAGS))
    return a[..., ATT_BQ - 1: ATT_BQ - 1 + ATT_W]


def _bias_tiles(rel_bias, gi):
    heads = rel_bias[:, 4 * gi: 4 * gi + 4]
    diag = jnp.einsum('tnb,bh->thn', jnp.asarray(_bias_diag_onehot(DILATIONS[gi])), heads, precision=HI)
    return _toeplitz(diag)


def _bias_tiles_bwd(rel_bias, dtiles, gi):
    return jax.vjp(lambda rb: _bias_tiles(rb, gi), rel_bias)[1](dtiles)[0]


def _att_window(i, n_sub):
    start = jnp.clip(i * ATT_BQ - HALF, 0, n_sub - ATT_W)
    off = i * ATT_BQ - HALF - start
    return pl.multiple_of(start, HALF), off


def _att_valid(off):
    q = lax.broadcasted_iota(jnp.int32, (ATT_BQ, ATT_W), 0)
    kk = lax.broadcasted_iota(jnp.int32, (ATT_BQ, ATT_W), 1)
    return jnp.abs(kk - q - HALF - off) <= HALF


def _att_tile_id(i, nq):
    return jnp.where(i == 0, 0, jnp.where(i == nq - 1, 2, 1))


def _att_fwd(qkvm, bias, gi, *, name):
    dil = DILATIONS[gi]
    s_len = qkvm.shape[0]
    n_sub = s_len // dil
    nq = n_sub // ATT_BQ
    assert n_sub % ATT_BQ == 0 and n_sub >= ATT_W
    cols = qkvm.shape[1] // LANES
    view = qkvm.reshape(n_sub, dil * qkvm.shape[1])

    def kern(q_ref, k_ref, v_ref, b_ref, o_ref, lse_ref):
        i = pl.program_id(2)
        start, off = _att_window(i, n_sub)
        valid = _att_valid(off)
        q = q_ref[...] * (ATT_HEAD_DIM ** -0.5)
        kw = k_ref[pl.ds(start, ATT_W), :]
        vw = v_ref[pl.ds(start, ATT_W), :]
        o = jnp.zeros((ATT_BQ, LANES), F32)
        lse = jnp.zeros((ATT_BQ, LANES), F32)
        for h in range(2):
            mh = _lane_head_mask(LANES, ATT_HEAD_DIM, h)
            s = _dot_nt(jnp.where(mh, q, 0.0), kw) + b_ref[h]
            s = jnp.where(valid, s, NEG_INF)
            mx = jnp.max(s, axis=-1, keepdims=True)
            p = jnp.exp(s - mx)
            den = jnp.sum(p, axis=-1, keepdims=True)
            o = jnp.where(mh, _dot(p, vw) / den, o)
            lse = jnp.where(mh, mx + jnp.log(den), lse)
        o_ref[...] = o
        lse_ref[...] = lse

    def qkv_spec(which, full):
        shape = (n_sub, LANES) if full else (ATT_BQ, LANES)
        return pl.BlockSpec(shape, lambda pr, r, i: (0 if full else i, r * cols + which * 6 + 2 * gi + pr))

    out_spec = pl.BlockSpec((ATT_BQ, LANES), lambda pr, r, i: (i, r * 2 + pr))
    o, lse = pl.pallas_call(
        kern, name=name, grid=(2, dil, nq),
        in_specs=[qkv_spec(0, False), qkv_spec(1, True), qkv_spec(2, True),
                  pl.BlockSpec((None, 2, ATT_BQ, ATT_W), lambda pr, r, i: (_att_tile_id(i, nq), pr, 0, 0))],
        out_specs=[out_spec, out_spec],
        out_shape=[jax.ShapeDtypeStruct((n_sub, dil * 2 * LANES), F32)] * 2,
        compiler_params=_params("parallel", "parallel", "arbitrary"),
    )(view, view, view, bias)
    return o.reshape(s_len, 2 * LANES), lse.reshape(s_len, 2 * LANES)


def _att_bwd(qkvm, bias, lse_tot, delta, dcat, gi, *, name):
    dil = DILATIONS[gi]
    s_len = qkvm.shape[0]
    n_sub = s_len // dil
    nq = n_sub // ATT_BQ
    cols = qkvm.shape[1] // LANES
    dcols = dcat.shape[1] // LANES
    view = qkvm.reshape(n_sub, dil * qkvm.shape[1])
    lse_v = lse_tot.reshape(n_sub, dil * 2 * LANES)
    delta_v = delta.reshape(n_sub, dil * 2 * LANES)
    dcat_v = dcat.reshape(n_sub, dil * dcat.shape[1])

    def kern(q_ref, k_ref, v_ref, b_ref, lse_ref, dl_ref, dm_ref, dq_ref, dk_ref, dv_ref, db_ref):
        r, i = pl.program_id(1), pl.program_id(2)
        start, off = _att_window(i, n_sub)
        valid = _att_valid(off)
        tile = _att_tile_id(i, nq)

        @pl.when(i == 0)
        def _():
            dk_ref[...] = jnp.zeros_like(dk_ref)
            dv_ref[...] = jnp.zeros_like(dv_ref)

        @pl.when((i == 0) & (r == 0))
        def _():
            db_ref[...] = jnp.zeros_like(db_ref)

        q = q_ref[...] * (ATT_HEAD_DIM ** -0.5)
        kw = k_ref[pl.ds(start, ATT_W), :]
        vw = v_ref[pl.ds(start, ATT_W), :]
        dm = dm_ref[...]
        lse = lse_ref[...]
        dl = dl_ref[...]
        dq = jnp.zeros((ATT_BQ, LANES), F32)
        dkw = jnp.zeros((ATT_W, LANES), F32)
        dvw = jnp.zeros((ATT_W, LANES), F32)
        for h in range(2):
            mh = _lane_head_mask(LANES, ATT_HEAD_DIM, h)
            qh = jnp.where(mh, q, 0.0)
            dmh = jnp.where(mh, dm, 0.0)
            s = _dot_nt(qh, kw) + b_ref[tile, h]
            s = jnp.where(valid, s, NEG_INF)
            lse_h = jnp.max(jnp.where(mh, lse, NEG_INF), axis=-1, keepdims=True)
            dl_h = jnp.max(jnp.where(mh, dl, NEG_INF), axis=-1, keepdims=True)
            p = jnp.exp(s - lse_h)
            ds = p * (_dot_nt(dmh, vw) - dl_h)
            dq = dq + jnp.where(mh, _dot(ds, kw), 0.0)
            dkw = dkw + _dot_tn(ds, qh)
            dvw = dvw + _dot_tn(p, dmh)
            db_ref[tile, h] += ds
        dq_ref[...] = dq * (ATT_HEAD_DIM ** -0.5)
        dk_ref[pl.ds(start, ATT_W), :] += dkw
        dv_ref[pl.ds(start, ATT_W), :] += dvw

    def qkv_spec(which, full):
        shape = (n_sub, LANES) if full else (ATT_BQ, LANES)
        return pl.BlockSpec(shape, lambda pr, r, i: (0 if full else i, r * cols + which * 6 + 2 * gi + pr))

    blk = pl.BlockSpec((ATT_BQ, LANES), lambda pr, r, i: (i, r * 2 + pr))
    full = pl.BlockSpec((n_sub, LANES), lambda pr, r, i: (0, r * 2 + pr))
    bias_spec = pl.BlockSpec((3, 2, ATT_BQ, ATT_W), lambda pr, r, i: (0, pr, 0, 0))
    sub = jax.ShapeDtypeStruct((n_sub, dil * 2 * LANES), F32)
    dq, dk, dv, db = pl.pallas_call(
        kern, name=name, grid=(2, dil, nq),
        in_specs=[qkv_spec(0, False), qkv_spec(1, True), qkv_spec(2, True), bias_spec, blk, blk,
                  pl.BlockSpec((ATT_BQ, LANES), lambda pr, r, i: (i, r * dcols + 2 * gi + pr))],
        out_specs=[blk, full, full, bias_spec],
        out_shape=[sub, sub, sub, jax.ShapeDtypeStruct(bias.shape, F32)],
        compiler_params=_params("arbitrary", "arbitrary", "arbitrary"),
    )(view, view, view, bias, lse_v, delta_v, dcat_v)
    return dq.reshape(s_len, -1), dk.reshape(s_len, -1), dv.reshape(s_len, -1), db


def _att_combine(o_g, lse_g, qkvm, kv_mem, *, name):
    def body(o0, o1, o2, l0, l1, l2, qm, kv):
        mx = jnp.maximum(jnp.maximum(l0, l1), l2)
        tot = mx + jnp.log(jnp.exp(l0 - mx) + jnp.exp(l1 - mx) + jnp.exp(l2 - mx))
        mixed = [o * jnp.exp(l - tot) for o, l in ((o0, l0), (o1, l1), (o2, l2))]
        return (jnp.concatenate(mixed + [_mem_attn(qm, kv)], axis=1), tot), ()
    return _rowwise(body, list(o_g) + list(lse_g) + [_col(qkvm, MEM_WIDTH, (3 * TOK_WIDTH) // MEM_WIDTH)], [kv_mem],
                    [(D_MODEL, F32), (MEM_WIDTH, F32)], [], tm=256, name=name)


def _head_sum_matrix():
    a = np.arange(MEM_WIDTH)
    return jnp.asarray((a[:, None] // ATT_HEAD_DIM == a[None, :] // ATT_HEAD_DIM).astype(np.float32))


def _att_bwd_prep(cat, dcat, qkvm, kv_mem, *, name):
    def body(cat, dcat, qm, kv, hs):
        prod = cat * dcat
        summed = prod[:, 0:256] + prod[:, 256:512] + prod[:, 512:768]
        delta = _dot(summed, hs, precision=HI)
        dqm, dkv = _mem_attn_bwd(qm, kv, dcat[:, TOK_WIDTH:])
        return (delta, dqm), (dkv,)
    return _rowwise(body, [cat, dcat, _col(qkvm, MEM_WIDTH, (3 * TOK_WIDTH) // MEM_WIDTH)], [kv_mem, _head_sum_matrix()],
                    [(MEM_WIDTH, F32), (MEM_WIDTH, F32)], [kv_mem.shape], tm=256, name=name)


def _dn_conv_post(s, j):
    scale = jnp.where(j < DN_HEADS, DN_HEAD_DIM ** -0.5, 1.0)
    normed = s * lax.rsqrt(jnp.sum(s * s, axis=-1, keepdims=True) + EPS) * scale
    return jnp.where(j >= 2 * DN_HEADS, s, normed)


def _shift_rows(x, sh):
    n = x.shape[0]
    row = lax.broadcasted_iota(jnp.int32, (n, 1), 0)
    rolled = pltpu.roll(x, (-sh) % n, 0)
    return jnp.where((row + sh >= 0) & (row + sh < n), rolled, 0.0)


def _dn_conv_taps(x, w_ref):
    c = x * w_ref[pl.ds(DN_CONV // 2, 1), :]
    for jj in range(DN_CONV):
        if jj != DN_CONV // 2:
            c = c + _shift_rows(x, jj - DN_CONV // 2) * w_ref[pl.ds(jj, 1), :]
    return c


def _dn_conv_fwd(proj, conv_w, *, name):
    s_len = proj.shape[0]
    width = 3 * TOK_WIDTH

    def kern(x_ref, w_ref, o_ref):
        j = pl.program_id(0)
        o_ref[...] = _dn_conv_post(_silu(_dn_conv_taps(x_ref[...], w_ref)), j)

    return pl.pallas_call(
        kern, name=name, grid=(width // LANES,),
        in_specs=[pl.BlockSpec((s_len, LANES), lambda j: (0, j)), pl.BlockSpec((DN_CONV, LANES), lambda j: (0, j))],
        out_specs=pl.BlockSpec((s_len, LANES), lambda j: (0, j)),
        out_shape=jax.ShapeDtypeStruct((s_len, width), F32),
        compiler_params=_params("parallel"),
    )(proj, conv_w)


def _dn_conv_bwd(proj, conv_w, d_fwd, d_bwd, which, *, name):
    s_len = proj.shape[0]

    def kern(x_ref, w_ref, df_ref, db_ref, dx_ref, dw_ref):
        j = pl.program_id(0) + which * DN_HEADS
        x = x_ref[...]
        c = _dn_conv_taps(x, w_ref)
        _, vjp = jax.vjp(lambda c_: _dn_conv_post(_silu(c_), j), c)
        dc = vjp(df_ref[...] + db_ref[...])[0]
        dx = dc * w_ref[pl.ds(DN_CONV // 2, 1), :]
        for jj in range(DN_CONV):
            sh = jj - DN_CONV // 2
            if sh != 0:
                dx = dx + _shift_rows(dc, -sh) * w_ref[pl.ds(jj, 1), :]
            dw_ref[pl.ds(jj, 1), :] = jnp.sum(dc * _shift_rows(x, sh), axis=0, keepdims=True)
        dx_ref[...] = dx

    return pl.pallas_call(
        kern, name=name, grid=(DN_HEADS,),
        in_specs=[pl.BlockSpec((s_len, LANES), lambda j: (0, j + which * DN_HEADS)),
                  pl.BlockSpec((DN_CONV, LANES), lambda j: (0, j + which * DN_HEADS)),
                  pl.BlockSpec((s_len, LANES), lambda j: (0, j)),
                  pl.BlockSpec((s_len, LANES), lambda j: (0, j))],
        out_specs=[pl.BlockSpec((s_len, LANES), lambda j: (0, j)), pl.BlockSpec((DN_CONV, LANES), lambda j: (0, j))],
        out_shape=[jax.ShapeDtypeStruct((s_len, TOK_WIDTH), F32), jax.ShapeDtypeStruct((DN_CONV, TOK_WIDTH), F32)],
        compiler_params=_params("parallel"),
    )(proj, conv_w, d_fwd, d_bwd)


GATE_TM = 2 * DN_CHUNK


FWD_GATE_LANES = 2 * DN_HEADS


def _gate_constants():
    i = np.arange(GATE_TM)
    same = (i[:, None] // DN_CHUNK) == (i[None, :] // DN_CHUNK)
    cum_f = same & (i[None, :] <= i[:, None])
    cum_r = same & (i[None, :] >= i[:, None])
    return tuple(jnp.asarray(np.asarray(a, np.float32)) for a in (cum_f, cum_r, same))


def _gate_params(p):
    z = jnp.zeros((DN_HEADS,), F32)
    return jnp.concatenate([p[0], z, p[1], z, jnp.zeros((LANES - N_GATES,), F32)]).reshape(1, LANES)


def _gate_params_bwd(dp):
    return jnp.stack([dp[0, 0:DN_HEADS], dp[0, 2 * DN_HEADS: 3 * DN_HEADS]])


def _dn_gates(gate_in, a_cols, dt_cols, cum_f, cum_r, tot):
    g = -jnp.exp(a_cols) * _softplus(gate_in + dt_cols)
    fwd_lane = lax.broadcasted_iota(jnp.int32, (1, LANES), 1) < FWD_GATE_LANES
    gc = jnp.where(fwd_lane, _dot(cum_f, g, precision=HI), _dot(cum_r, g, precision=HI))
    return gc, _dot(tot, g, precision=HI), jax.nn.sigmoid(gate_in)


def _dn_gates_fwd(proj, a_cols, dt_cols, *, name):
    def body(gi, *consts):
        return _dn_gates(gi, *consts), ()
    return _rowwise(body, [_col(proj, LANES, DN_IN_PAD // LANES - 1)], [a_cols, dt_cols, *_gate_constants()],
                    [(LANES, F32)] * 3, [], tm=GATE_TM, name=name)


def _dn_gates_bwd(proj, a_cols, dt_cols, d_gates, *, name):
    def body(gi, gcf, gtf, bf, gcr, gtr, br, a, dt, *consts):
        _, vjp = jax.vjp(lambda gi_, a_, dt_: _dn_gates(gi_, a_, dt_, *consts), gi, a, dt)
        dgi, da, ddt = vjp((gcf + gcr, gtf + gtr, bf + br))
        return (dgi,), (da, ddt)
    return _rowwise(body, [_col(proj, LANES, DN_IN_PAD // LANES - 1), *d_gates[0], *d_gates[1]],
                    [a_cols, dt_cols, *_gate_constants()], [(LANES, F32)], [a_cols.shape, dt_cols.shape],
                    tm=GATE_TM, name=name)


INV_BASE = 8


def _block_id_equal(c, size):
    i = lax.broadcasted_iota(jnp.int32, (c, c), 0) // size
    j = lax.broadcasted_iota(jnp.int32, (c, c), 1) // size
    return (i == j).astype(F32)


def _unit_tri_inverse_impl(lmat):
    c = lmat.shape[0]
    eye = _block_id_equal(c, 1)
    same = _block_id_equal(c, INV_BASE)
    neg = -lmat * same
    inv = eye + neg
    power = neg
    for _ in range(int(math.log2(INV_BASE)) - 1):
        power = _dot(power, power)
        inv = inv + _dot(inv, power)
    size = INV_BASE
    while size < c:
        bigger = _block_id_equal(c, 2 * size)
        inv = inv - _dot(_dot(inv, lmat * (bigger - same)), inv)
        same, size = bigger, 2 * size
    resid = eye - _dot(eye + lmat, inv, precision=HI)
    return inv + _dot(inv, resid)


@jax.custom_vjp
def _unit_tri_inverse(lmat):
    return _unit_tri_inverse_impl(lmat)


def _unit_tri_inverse_fwd(lmat):
    inv = _unit_tri_inverse_impl(lmat)
    return inv, inv


def _unit_tri_inverse_bwd(inv, d_inv):
    return (-_dot_tn(inv, _dot_nt(d_inv, inv)),)


_unit_tri_inverse.defvjp(_unit_tri_inverse_fwd, _unit_tri_inverse_bwd)


def _dn_chunk(q, k, v, gates_t, gc_row, tot_row, beta_row, state, tri, inverse):
    c = q.shape[0]
    assert c == DN_HEAD_DIM
    eye = _block_id_equal(c, 1)

    def along_rows(x, pick):
        return jnp.broadcast_to(jnp.sum(x * pick, axis=0, keepdims=True), (c, c))

    gc_j = along_rows(gates_t[0], gc_row)
    gc = gc_j.T
    g_tot = along_rows(gates_t[1], tot_row)
    beta = along_rows(gates_t[2], beta_row).T
    decay = jnp.exp(jnp.where(tri > 0, gc - gc_j, NEG_INF))
    k_beta = k * beta
    inv = inverse((tri - eye) * (_dot_nt(k_beta, k) * decay))
    e_gc = jnp.exp(gc)
    u = _dot(inv, v * beta)
    w = _dot(inv, k_beta * e_gc)
    intra = tri * (_dot_nt(q, k) * decay)
    v_new = u - _dot(w, state)
    out = _dot(q * e_gc, state) + _dot(intra, v_new)
    state = state * jnp.exp(g_tot) + _dot_tn(k * jnp.exp(g_tot - gc), v_new)
    return out, state


def _dn_tri():
    i = np.arange(DN_CHUNK)
    tri = np.stack([(i[None, :] <= i[:, None]), (i[None, :] >= i[:, None])]).astype(np.float32)
    return jnp.asarray(np.repeat(tri, DN_HEADS, axis=0))


def _dn_gate_picks():
    picks = np.zeros((3, 2 * DN_HEADS, 2 * DN_CHUNK, 1), np.float32)
    for d in range(2):
        for h in range(DN_HEADS):
            alpha = d * DN_CHUNK + d * 2 * DN_HEADS + h
            picks[0, d * DN_HEADS + h, alpha] = 1.0
            picks[1, d * DN_HEADS + h, alpha] = 1.0
            picks[2, d * DN_HEADS + h, alpha + DN_HEADS] = 1.0
    return jnp.asarray(picks)


def _stack_chains(fwd_ref, rev_ref):
    return jnp.stack([r[:, _head_cols(h)] for r in (fwd_ref, rev_ref) for h in range(DN_HEADS)])


def _unstack_chains(val, fwd_ref, rev_ref):
    for d, r in enumerate((fwd_ref, rev_ref)):
        for h in range(DN_HEADS):
            r[:, _head_cols(h)] = val[d * DN_HEADS + h]


def _gates_transposed(fwd_refs, rev_refs):
    return jnp.stack([jnp.concatenate([f[...].T, r[...].T], axis=0) for f, r in zip(fwd_refs, rev_refs, strict=True)])


def _dn_row_spec(nc, col, reverse, width=TOK_WIDTH):
    return pl.BlockSpec((DN_CHUNK, width), lambda t: ((nc - 1 - t) if reverse else t, col))


def _dn_state_spec(nc, reverse):
    return pl.BlockSpec((None, DN_HEADS, DN_HEAD_DIM, DN_HEAD_DIM), lambda t: ((nc - 1 - t) if reverse else t, 0, 0, 0))


def _head_cols(h):
    return pl.ds(h * DN_HEAD_DIM, DN_HEAD_DIM)


def _const_spec(arr):
    return pl.BlockSpec(arr.shape, functools.partial(lambda t, n: (0,) * n, n=arr.ndim))


def _dn_chains(inverse):
    return jax.vmap(lambda q, k, v, gates_t, *rest: _dn_chunk(q, k, v, gates_t, *rest, inverse),
                    in_axes=(0, 0, 0, None, 0, 0, 0, 0, 0))


def _dn_scan_fwd(qkv, gates, *, name):
    s_len = qkv.shape[0]
    nc = s_len // DN_CHUNK
    tri, picks = _dn_tri(), _dn_gate_picks()

    def kern(*refs):
        ins, (tri_ref, pick_ref, of_ref, or_ref, sf_ref, sr_ref, state) = refs[:12], refs[12:]

        @pl.when(pl.program_id(0) == 0)
        def _():
            state[...] = jnp.zeros_like(state)

        entry = state[...]
        qkv_c = [_stack_chains(ins[i], ins[6 + i]) for i in range(3)]
        gates_t = _gates_transposed(ins[3:6], ins[9:12])
        out, new = _dn_chains(_unit_tri_inverse_impl)(*qkv_c, gates_t, pick_ref[0], pick_ref[1], pick_ref[2], entry, tri_ref[...])
        sf_ref[...] = entry[:DN_HEADS]
        sr_ref[...] = entry[DN_HEADS:]
        _unstack_chains(out, of_ref, or_ref)
        state[...] = new

    in_specs = []
    for rev in (False, True):
        in_specs += [_dn_row_spec(nc, col, rev) for col in (0, 1, 2)] + [_dn_row_spec(nc, 0, rev, LANES)] * 3
    in_specs += [_const_spec(tri), _const_spec(picks)]
    return pl.pallas_call(
        kern, name=name, grid=(nc,), in_specs=in_specs,
        out_specs=[_dn_row_spec(nc, 0, False), _dn_row_spec(nc, 0, True), _dn_state_spec(nc, False), _dn_state_spec(nc, True)],
        out_shape=[jax.ShapeDtypeStruct((s_len, TOK_WIDTH), F32)] * 2
        + [jax.ShapeDtypeStruct((nc, DN_HEADS, DN_HEAD_DIM, DN_HEAD_DIM), F32)] * 2,
        scratch_shapes=[pltpu.VMEM((2 * DN_HEADS, DN_HEAD_DIM, DN_HEAD_DIM), F32)],
        compiler_params=_params("arbitrary"),
    )(*([qkv, qkv, qkv, *gates] * 2), tri, picks)


def _dn_scan_bwd(qkv, gates, states, d_o, *, name):
    s_len = qkv.shape[0]
    nc = s_len // DN_CHUNK
    tri, picks = _dn_tri(), _dn_gate_picks()

    def kern(*refs):
        ins, tri_ref, pick_ref, outs, d_state = refs[:16], refs[16], refs[17], refs[18:30], refs[30]

        @pl.when(pl.program_id(0) == 0)
        def _():
            d_state[...] = jnp.zeros_like(d_state)

        qkv_c = [_stack_chains(ins[i], ins[8 + i]) for i in range(3)]
        gates_t = _gates_transposed(ins[3:6], ins[11:14])
        entry = jnp.concatenate([ins[6][...], ins[14][...]], axis=0)
        d_out = _stack_chains(ins[7], ins[15])
        tri_v, picks_v = tri_ref[...], pick_ref[...]
        _, vjp = jax.vjp(lambda q, k, v, g, s: _dn_chains(_unit_tri_inverse)(q, k, v, g, picks_v[0], picks_v[1], picks_v[2], s, tri_v),
                         *qkv_c, gates_t, entry)
        dq, dk, dv, d_gates_t, d_entry = vjp((d_out, d_state[...]))
        for i, val in enumerate((dq, dk, dv)):
            _unstack_chains(val, outs[i], outs[6 + i])
        for i in range(3):
            outs[3 + i][...] = d_gates_t[i, :DN_CHUNK].T
            outs[9 + i][...] = d_gates_t[i, DN_CHUNK:].T
        d_state[...] = d_entry

    in_specs, out_specs, out_shape = [], [], []
    for rev in (True, False):
        in_specs += [_dn_row_spec(nc, col, rev) for col in (0, 1, 2)] + [_dn_row_spec(nc, 0, rev, LANES)] * 3
        in_specs += [_dn_state_spec(nc, rev), _dn_row_spec(nc, 0, rev)]
        out_specs += [_dn_row_spec(nc, 0, rev)] * 3 + [_dn_row_spec(nc, 0, rev, LANES)] * 3
        out_shape += [jax.ShapeDtypeStruct((s_len, TOK_WIDTH), F32)] * 3 + [jax.ShapeDtypeStruct((s_len, LANES), F32)] * 3
    in_specs += [_const_spec(tri), _const_spec(picks)]
    res = pl.pallas_call(
        kern, name=name, grid=(nc,), in_specs=in_specs, out_specs=out_specs, out_shape=out_shape,
        scratch_shapes=[pltpu.VMEM((2 * DN_HEADS, DN_HEAD_DIM, DN_HEAD_DIM), F32)],
        compiler_params=_params("arbitrary"),
    )(*[a for d in range(2) for a in (qkv, qkv, qkv, *gates, states[d], d_o)], tri, picks)
    return (res[0:3], res[3:6]), (res[6:9], res[9:12])


def _dn_out_head(o_f, o_b, z, gain):
    o = o_f + o_b
    return o * lax.rsqrt(jnp.mean(o * o, axis=-1, keepdims=True) + EPS) * gain * _silu(z)


def _dn_out(o_fwd, o_rev, proj, gain, qkv_kv_mem, *, name):
    def body(of, ob, z, qm, g, kv):
        heads = []
        for h in range(DN_HEADS):
            sl = slice(h * DN_HEAD_DIM, (h + 1) * DN_HEAD_DIM)
            heads.append(_dn_out_head(of[:, sl], ob[:, sl], z[:, sl], g))
        return (jnp.concatenate(heads + [_mem_attn(qm, kv)], axis=1),), ()
    return _rowwise(body, [o_fwd, o_rev, _col(proj, TOK_WIDTH, 3),
                           _col(proj, MEM_WIDTH, (4 * TOK_WIDTH) // MEM_WIDTH)], [gain, qkv_kv_mem],
                    [(D_MODEL, MXU_DTYPE)], [], tm=256, name=name)[0]


def _dn_out_bwd(o_fwd, o_rev, proj, gain, kv_mem, dcat, *, name):
    def body(of, ob, z, qm, dcat, g, kv):
        dos, dzs = [], []
        dgain = jnp.zeros_like(g)
        for h in range(DN_HEADS):
            sl = slice(h * DN_HEAD_DIM, (h + 1) * DN_HEAD_DIM)
            _, vjp = jax.vjp(_dn_out_head, of[:, sl], ob[:, sl], z[:, sl], g)
            d_of, _, dz, dg = vjp(dcat[:, sl])
            dos.append(d_of)
            dzs.append(dz)
            dgain = dgain + dg
        dqm, dkv = _mem_attn_bwd(qm, kv, dcat[:, TOK_WIDTH:])
        return (jnp.concatenate(dos, axis=1), jnp.concatenate(dzs, axis=1), dqm), (dgain, dkv)
    return _rowwise(body, [o_fwd, o_rev, _col(proj, TOK_WIDTH, 3),
                           _col(proj, MEM_WIDTH, (4 * TOK_WIDTH) // MEM_WIDTH), dcat], [gain, kv_mem],
                    [(TOK_WIDTH, F32), (TOK_WIDTH, F32), (MEM_WIDTH, F32)], [gain.shape, kv_mem.shape], tm=256, name=name)


def _pad_dn_w_in(w):
    gates = w[:, 4 * TOK_WIDTH: 4 * TOK_WIDTH + N_GATES]
    zeros = jnp.zeros((w.shape[0], DN_IN_PAD - DN_IN), w.dtype)
    return jnp.concatenate([w[:, :4 * TOK_WIDTH], w[:, 4 * TOK_WIDTH + N_GATES:], gates, zeros], axis=1)


def _unpad_dn_w_in(w):
    q_mem = w[:, 4 * TOK_WIDTH: 4 * TOK_WIDTH + MEM_WIDTH]
    gates = w[:, 4 * TOK_WIDTH + MEM_WIDTH: 4 * TOK_WIDTH + MEM_WIDTH + N_GATES]
    return jnp.concatenate([w[:, :4 * TOK_WIDTH], gates, q_mem], axis=1)


def _ffn_fwd(h, w_gu, w_d, tag):
    gu = _mm(h, w_gu, name=f"ffn_gu_{tag}")
    act = _swiglu_act(gu, name=f"ffn_act_{tag}")
    return gu, act, _mm(act, w_d, name=f"ffn_down_{tag}")


def _ffn_bwd(h, gu, act, w_gu, w_d, df, tag):
    d_act = _mm(df, w_d, tb=True, name=f"ffn_dact_{tag}")
    d_wd = _mm(act, df, ta=True, name=f"ffn_dwd_{tag}")
    d_gu = _swiglu_act_bwd(gu, d_act, name=f"ffn_dgu_{tag}")
    dh = _mm(d_gu, w_gu, tb=True, name=f"ffn_dh_{tag}")
    d_wgu = _mm(h, d_gu, ta=True, out_shards=_shards_of(w_gu), name=f"ffn_dwgu_{tag}")
    return dh, d_wgu, d_wd


def _local_step(x, mem, target, p):
    g = {}
    row = lambda v: v.reshape(1, -1)
    gains = {k: [row(p[k][i]) for i in range(2)] for k in
             ("mem_norm", "norm_mix_pre", "norm_mix_post", "norm_ffn_pre", "norm_ffn_post")}
    out_gain = row(p["dn_out_norm"])
    a_cols, dt_cols = _gate_params(p["dn_a_log"]), _gate_params(p["dn_dt_bias"])

    h0 = _pre_norm(x, gains["norm_mix_pre"][0], name="pre0")
    mem_n = [_pre_norm(mem, gains["mem_norm"][i], name=f"mem_norm{i}") for i in range(2)]
    kv_mem = [_mm(mem_n[i], p["mem_w_kv"][i], name=f"mem_kv{i}") for i in range(2)]
    qkvm = _mm(h0, p["att_w_in"], name="att_in")
    bias = [_bias_tiles(p["rel_bias"], gi) for gi in range(3)]
    att = [_att_fwd(qkvm, bias[gi], gi, name=f"att_fwd{gi}") for gi in range(3)]
    cat0, lse_tot = _att_combine([a[0] for a in att], [a[1] for a in att], qkvm, kv_mem[0], name="att_combine")
    mo0 = _mm(cat0, p["att_w_out"], name="att_out")
    x1, h1 = _post_pre(x, mo0, gains["norm_mix_post"][0], gains["norm_ffn_pre"][0], name="post_mix0")
    w_gu0, w_d0 = p["ffn_weights"](0, h1)
    gu0, act0, f0 = _ffn_fwd(h1, w_gu0, w_d0, 0)
    x2, h2 = _post_pre(x1, f0, gains["norm_ffn_post"][0], gains["norm_mix_pre"][1], name="post_ffn0")

    dn_w_in, dn_w_out = p["dn_weights"](h2)
    proj = _mm(h2, dn_w_in, name="dn_in")
    qkv = _dn_conv_fwd(proj, p["dn_conv"], name="dn_conv")
    gates = _dn_gates_fwd(proj, a_cols, dt_cols, name="dn_gates")
    o_fwd, o_rev, st_fwd, st_rev = _dn_scan_fwd(qkv, gates, name="dn_scan")
    cat1 = _dn_out(o_fwd, o_rev, proj, out_gain, kv_mem[1], name="dn_outnorm")
    mo1 = _mm(cat1, dn_w_out, name="dn_out")
    x3, h3 = _post_pre(x2, mo1, gains["norm_mix_post"][1], gains["norm_ffn_pre"][1], name="post_mix1")
    w_gu1, w_d1 = p["ffn_weights"](1, h3)
    gu1, act1, f1 = _ffn_fwd(h3, w_gu1, w_d1, 1)

    dx3, df1, dg_ffn_post1, loss_cols = _final_loss_bwd(x3, f1, gains["norm_ffn_post"][1], target, name="loss_bwd")
    dh3, d_wgu1, d_wd1 = _ffn_bwd(h3, gu1, act1, w_gu1, w_d1, df1, 1)
    sent = p["grads_ready"]("ffn1", {("ffn_w_gate_up", 1): d_wgu1, ("ffn_w_down", 1): d_wd1})
    dx2, dmo1, dg_mix_post1, dg_ffn_pre1 = _post_pre_bwd(x2, mo1, gains["norm_mix_post"][1] + sent, gains["norm_ffn_pre"][1],
                                                         dx3, dh3, name="post_mix1_bwd")
    dcat1 = _mm(dmo1, dn_w_out, tb=True, name="dn_out_dx")
    g["dn_w_out"] = _mm(cat1, dmo1, ta=True, name="dn_out_dw")
    d_o, dz, dqm1, d_out_gain, dkv1 = _dn_out_bwd(o_fwd, o_rev, proj, out_gain, kv_mem[1], dcat1, name="dn_outnorm_bwd")
    (d_f, dg_f), (d_r, dg_r) = _dn_scan_bwd(qkv, gates, (st_fwd, st_rev), d_o, name="dn_scan_bwd")
    d_gate_cols, d_a_cols, d_dt_cols = _dn_gates_bwd(proj, a_cols, dt_cols, (dg_f, dg_r), name="dn_gates_bwd")
    d_pre, d_conv = zip(*[_dn_conv_bwd(proj, p["dn_conv"], d_f[which], d_r[which], which, name=f"dn_conv_bwd{which}")
                          for which in range(3)])
    dproj = jnp.concatenate(list(d_pre) + [dz, dqm1, d_gate_cols], axis=1).astype(MXU_DTYPE)
    dh2 = _mm(dproj, dn_w_in, tb=True, name="dn_in_dx")
    g["dn_w_in"] = _mm(h2, dproj, ta=True, name="dn_in_dw")
    g["dn_conv"] = jnp.concatenate(d_conv, axis=1)
    g["dn_a_log"] = _gate_params_bwd(d_a_cols)
    g["dn_dt_bias"] = _gate_params_bwd(d_dt_cols)
    g["dn_out_norm"] = d_out_gain

    sent = p["grads_ready"]("dn", {("dn_w_in", 0): g["dn_w_in"], ("dn_w_out", 0): g["dn_w_out"]})
    dx1, df0, dg_ffn_post0, dg_mix_pre1 = _post_pre_bwd(x1, f0, gains["norm_ffn_post"][0] + sent, gains["norm_mix_pre"][1],
                                                        dx2, dh2, name="post_ffn0_bwd")
    dh1, d_wgu0, d_wd0 = _ffn_bwd(h1, gu0, act0, w_gu0, w_d0, df0, 0)
    sent = p["grads_ready"]("ffn0", {("ffn_w_gate_up", 0): d_wgu0, ("ffn_w_down", 0): d_wd0})
    dx0, dmo0, dg_mix_post0, dg_ffn_pre0 = _post_pre_bwd(x, mo0, gains["norm_mix_post"][0] + sent, gains["norm_ffn_pre"][0],
                                                         dx1, dh1, name="post_mix0_bwd")
    dcat0 = _mm(dmo0, p["att_w_out"], tb=True, name="att_out_dx")
    g["att_w_out"] = _mm(cat0, dmo0, ta=True, name="att_out_dw")
    delta, dqm0, dkv0 = _att_bwd_prep(cat0, dcat0, qkvm, kv_mem[0], name="att_bwd_prep")
    att_b = [_att_bwd(qkvm, bias[gi], lse_tot, delta, dcat0, gi, name=f"att_bwd{gi}") for gi in range(3)]
    dqkvm = jnp.concatenate([a[w] for w in range(3) for a in att_b] + [dqm0], axis=1).astype(MXU_DTYPE)
    g["rel_bias"] = sum(_bias_tiles_bwd(p["rel_bias"], att_b[gi][3], gi) for gi in range(3))
    dh0 = _mm(dqkvm, p["att_w_in"], tb=True, name="att_in_dx")
    g["att_w_in"] = _mm(h0, dqkvm, ta=True, out_shards=_shards_of(p["att_w_in"]), name="att_in_dw")
    grad_x, dg_mix_pre0 = _pre_norm_bwd(x, gains["norm_mix_pre"][0], dh0, dx0, name="pre0_bwd")

    d_mem_kv, d_mem_norm = [], []
    for i, dkv in enumerate((dkv0, dkv1)):
        d_mem_kv.append(_mm(mem_n[i], dkv, ta=True, name=f"mem_kv_dw{i}"))
        d_mem_n = _mm(dkv, p["mem_w_kv"][i], tb=True, name=f"mem_kv_dx{i}")
        d_mem_norm.append(_gain_bwd(mem, gains["mem_norm"][i], d_mem_n, name=f"mem_norm_bwd{i}"))
    g["mem_w_kv"] = d_mem_kv
    g["mem_norm"] = jnp.concatenate(d_mem_norm, axis=0)
    g["norm_mix_pre"] = jnp.concatenate([dg_mix_pre0, dg_mix_pre1], axis=0)
    g["norm_mix_post"] = jnp.concatenate([dg_mix_post0, dg_mix_post1], axis=0)
    g["norm_ffn_pre"] = jnp.concatenate([dg_ffn_pre0, dg_ffn_pre1], axis=0)
    g["norm_ffn_post"] = jnp.concatenate([dg_ffn_post0, dg_ffn_post1], axis=0)
    g["ffn_w_gate_up"] = [d_wgu0, d_wgu1]
    g["ffn_w_down"] = [d_wd0, d_wd1]
    return loss_cols, grad_x, g


N_CHIPS = 4
N_DEV = 8
MESH = pl.DeviceIdType.MESH
BIG = (("att_w_in", (1, 1024, 640), 2), ("att_w_out", (1, 256, 1024), 1), ("dn_w_in", (1, 1024, 838), 2),
       ("dn_w_out", (1, 256, 1024), 1), ("mem_w_kv", (2, 256, 512), 1), ("ffn_w_gate_up", (2, 1024, 1408), 2),
       ("ffn_w_down", (2, 704, 1024), 1))


def _mesh_pos():
    return lax.axis_index("x"), lax.axis_index("y"), lax.axis_index("c")


def _other_chips(x, y):
    return [(1 - x, y), (x, 1 - y), (1 - x, 1 - y)]


ANY = pl.BlockSpec(memory_space=pl.ANY)


def _all_reduce_small(v, *, name):
    rows, cols = v.shape
    flips = [(dx, dy, dc) for dx in (0, 1) for dy in (0, 1) for dc in (0, 1)][1:]

    def body(v_ref, o_ref, buf, send_sems, recv_sems):
        x, y, c = _mesh_pos()

        def peer(f):
            return tuple(1 - p if fl else p for p, fl in zip((x, y, c), f))

        def index(p):
            return 4 * p[0] + 2 * p[1] + p[2]

        buf[index((x, y, c))] = v_ref[...]
        sends = []
        for k, f in enumerate(flips):
            cp = pltpu.make_async_remote_copy(src_ref=v_ref, dst_ref=buf.at[index((x, y, c))], send_sem=send_sems.at[k],
                                              recv_sem=recv_sems.at[k], device_id=peer(f), device_id_type=MESH)
            cp.start()
            sends.append(cp)
        for k, f in enumerate(flips):
            pltpu.make_async_remote_copy(src_ref=v_ref, dst_ref=buf.at[index(peer(f))], send_sem=send_sems.at[k],
                                         recv_sem=recv_sems.at[k], device_id=peer(f), device_id_type=MESH).wait_recv()
        for cp in sends:
            cp.wait_send()
        acc = buf[0]
        for d in range(1, N_DEV):
            acc = acc + buf[d]
        o_ref[...] = acc

    vmem = pl.BlockSpec(memory_space=pltpu.VMEM)
    return pl.pallas_call(
        body, name=name, in_specs=[vmem], out_specs=vmem, out_shape=jax.ShapeDtypeStruct((rows, cols), F32),
        scratch_shapes=[pltpu.VMEM((N_DEV, rows, cols), F32), pltpu.SemaphoreType.DMA((N_DEV - 1,)),
                        pltpu.SemaphoreType.DMA((N_DEV - 1,))],
    )(v)


def _adamw(w, g, m, v, *, name):
    def body(w, g, m, v):
        m = ADAM_B1 * m + (1.0 - ADAM_B1) * g
        v = ADAM_B2 * v + (1.0 - ADAM_B2) * (g * g)
        m_hat = m / (1.0 - ADAM_B1 ** ADAM_STEP)
        v_hat = v / (1.0 - ADAM_B2 ** ADAM_STEP)
        delta = -ADAM_LR * (m_hat / (jnp.sqrt(v_hat) + ADAM_EPS) + ADAM_WD * w)
        return (delta, m, v), ()
    rows, cols = w.shape
    return _rowwise(body, [w, g, m, v], [], [(cols, F32)] * 3, [], tm=_tile(rows, 256, SUBLANES), name=name)


def _pack_small(arrs, rows):
    flat = jnp.concatenate([a.reshape(-1) for a in arrs])
    return jnp.pad(flat, (0, rows * LANES - flat.shape[0])).reshape(rows, LANES)


def _unpack_small(packed, shapes):
    flat = packed.reshape(-1)
    out, off = [], 0
    for s in shapes:
        size = math.prod(s)
        out.append(flat[off: off + size].reshape(s))
        off += size
    return out


def _small_rows(shapes):
    return -(-sum(math.prod(s) for s in shapes) // (SUBLANES * LANES)) * SUBLANES


def _sem_pairs(n):
    return [pltpu.SemaphoreType.DMA((n,)), pltpu.SemaphoreType.DMA((n,))]


def _gather_blocks(blocks, *, name):
    n = len(blocks)

    def body(*refs):
        x_refs, out_refs, (send_sems, recv_sems) = refs[:n], refs[n: 2 * n], refs[2 * n:]
        x, y, c = _mesh_pos()
        sibling = (x, y, 1 - c)
        chips = _other_chips(x, y)

        def copy(k, src, dst, to):
            return pltpu.make_async_remote_copy(src_ref=src, dst_ref=dst, send_sem=send_sems.at[k],
                                                recv_sem=recv_sems.at[k], device_id=to, device_id_type=MESH)

        def part(b, chip, h):
            half = blocks[b].shape[0] // 2
            return out_refs[b].at[2 * chip[0] + chip[1], pl.ds(h * half, half), :]

        def my_half(b):
            half = blocks[b].shape[0] // 2
            return x_refs[b].at[pl.ds(c * half, half), :]

        first = [copy(6 * b + j, my_half(b), part(b, (x, y), c), (*chip, c)) for b in range(n) for j, chip in enumerate(chips)]
        for cp in first:
            cp.start()
        passed = []
        for b in range(n):
            for j, chip in enumerate(chips):
                copy(6 * b + j, my_half(b), part(b, chip, c), (*chip, c)).wait_recv()
                cp = copy(6 * b + 3 + j, part(b, chip, c), part(b, chip, c), sibling)
                cp.start()
                passed.append(cp)
        for b in range(n):
            for j, chip in enumerate(chips):
                copy(6 * b + 3 + j, part(b, chip, 1 - c), part(b, chip, 1 - c), sibling).wait_recv()
        for cp in first + passed:
            cp.wait_send()

    return pl.pallas_call(
        body, name=name, in_specs=[ANY] * n, out_specs=[ANY] * n,
        out_shape=[jax.ShapeDtypeStruct((N_CHIPS, *a.shape), a.dtype) for a in blocks],
        scratch_shapes=_sem_pairs(6 * n),
    )(*blocks)


HBM = pl.BlockSpec(memory_space=pltpu.HBM)
SEM = pl.BlockSpec(memory_space=pltpu.SEMAPHORE)
DATAFLOW = pltpu.SideEffectType.DATAFLOW_SIDE_EFFECTING


def _gather_start(blocks, *, name):
    n = len(blocks)
    lands = [lax.empty((N_CHIPS, *a.shape), a.dtype) for a in blocks]

    def body(*refs):
        x_refs, land_refs, send_sems, recv_sems, token = refs[:n], refs[n: 2 * n], refs[2 * n], refs[2 * n + 1], refs[-1]
        x, y, c = _mesh_pos()
        for b in range(n):
            for j, chip in enumerate(_other_chips(x, y)):
                pltpu.make_async_remote_copy(src_ref=x_refs[b], dst_ref=land_refs[b].at[2 * x + y], send_sem=send_sems.at[3 * b + j],
                                             recv_sem=recv_sems.at[3 * b + j], device_id=(*chip, c), device_id_type=MESH).start()
        token[...] = jnp.zeros_like(token)

    operands = [pltpu.with_memory_space_constraint(a, pltpu.HBM) for a in blocks + lands]
    res = pl.pallas_call(
        body, name=name, in_specs=[HBM] * (2 * n),
        out_shape=(pltpu.SemaphoreType.DMA((3 * n,)), pltpu.SemaphoreType.DMA((3 * n,)),
                   *[pltpu.HBM(a.shape, a.dtype) for a in operands], jax.ShapeDtypeStruct((SUBLANES, LANES), F32)),
        out_specs=(SEM, SEM, *[HBM] * (2 * n), pl.BlockSpec(memory_space=pltpu.VMEM)),
        input_output_aliases={i: 2 + i for i in range(2 * n)},
        compiler_params=pltpu.CompilerParams(has_side_effects=DATAFLOW),
    )(*operands)
    return res[0], res[1], list(res[2: 2 + n]), list(res[2 + n: 2 + 2 * n]), res[-1]


def _gather_wait(started, after, *, name):
    send_sems, recv_sems, blocks, lands, _ = started
    n = len(blocks)

    def body(*refs):
        x_refs, land_refs, send_sems, recv_sems = refs[:n], refs[n: 2 * n], refs[2 * n], refs[2 * n + 1]
        x, y, c = _mesh_pos()
        for b in range(n):
            for j, chip in enumerate(_other_chips(x, y)):
                cp = pltpu.make_async_remote_copy(src_ref=x_refs[b], dst_ref=land_refs[b].at[2 * chip[0] + chip[1]],
                                                  send_sem=send_sems.at[3 * b + j], recv_sem=recv_sems.at[3 * b + j],
                                                  device_id=(*chip, c), device_id_type=MESH)
                cp.wait_send()
                cp.wait_recv()

    res = pl.pallas_call(
        body, name=name, in_specs=(*[HBM] * (2 * n), SEM, SEM, ANY),
        out_shape=tuple(pltpu.HBM(a.shape, a.dtype) for a in blocks + lands), out_specs=tuple([HBM] * (2 * n)),
        input_output_aliases={i: i for i in range(2 * n)},
        compiler_params=pltpu.CompilerParams(has_side_effects=DATAFLOW),
    )(*blocks, *lands, send_sems, recv_sems, after)
    return list(res[n:])


def _swap_halves(blocks, own_rows, *, name):
    n = len(blocks)

    def body(*refs):
        in_refs, out_refs, (send_sems, recv_sems) = refs[:n], refs[n: 2 * n], refs[2 * n:]
        x, y, c = _mesh_pos()
        copies = [pltpu.make_async_remote_copy(src_ref=own_rows(in_refs[b], c), dst_ref=out_refs[b], send_sem=send_sems.at[b],
                                               recv_sem=recv_sems.at[b], device_id=(x, y, 1 - c), device_id_type=MESH)
                  for b in range(n)]
        for cp in copies:
            cp.start()
        for cp in copies:
            cp.wait()

    def sent_shape(a):
        return jax.eval_shape(lambda r: own_rows(r, 0), a)

    return pl.pallas_call(
        body, name=name, in_specs=[ANY] * n, out_specs=[ANY] * n,
        out_shape=[jax.ShapeDtypeStruct(sent_shape(a).shape, a.dtype) for a in blocks],
        scratch_shapes=_sem_pairs(n),
    )(*blocks)


def _other_half_rows(ref, c):
    half = ref.shape[1] // 2
    return ref[:, (1 - c) * half: (2 - c) * half, :] if isinstance(c, int) else ref.at[:, pl.ds((1 - c) * half, half), :]


def _whole(ref, c):
    return ref


def _add_own_half_block(block, received, core, *, name):
    n, rows, cols = block.shape
    half = rows // 2
    tm = _tile(half, 512, 2 * SUBLANES)
    nb = half // tm

    def kern(c_ref, a_ref, b_ref, o_ref):
        o_ref[...] = (a_ref[...] + b_ref[...]).astype(o_ref.dtype)

    return pl.pallas_call(
        kern, name=name,
        grid_spec=pltpu.PrefetchScalarGridSpec(
            num_scalar_prefetch=1, grid=(n, nb),
            in_specs=[pl.BlockSpec((None, tm, cols), lambda s, i, c: (s, c[0] * nb + i, 0)),
                      pl.BlockSpec((None, tm, cols), lambda s, i, c: (s, i, 0))],
            out_specs=pl.BlockSpec((None, tm, cols), lambda s, i, c: (s, i, 0))),
        out_shape=jax.ShapeDtypeStruct((n, half, cols), LINK_DTYPE),
        compiler_params=_params("parallel", "parallel"),
    )(core, block, received)


def _sum_chips_block(parts, *, name):
    n, half, cols = parts.shape
    tm = _tile(half, 512, 2 * SUBLANES)

    def kern(p_ref, o_ref):
        acc = p_ref[0].astype(F32)
        for s in range(1, n):
            acc = acc + p_ref[s].astype(F32)
        o_ref[...] = acc

    return pl.pallas_call(
        kern, name=name, grid=(half // tm,),
        in_specs=[pl.BlockSpec((n, tm, cols), lambda i: (0, i, 0))],
        out_specs=pl.BlockSpec((tm, cols), lambda i: (i, 0)),
        out_shape=jax.ShapeDtypeStruct((half, cols), F32),
        compiler_params=_params("parallel"),
    )(parts)


def _scatter_start(sums, *, name):
    n = len(sums)
    lands = [lax.empty(a.shape, a.dtype) for a in sums]

    def body(*refs):
        s_refs, land_refs, send_sems, recv_sems, token = refs[:n], refs[n: 2 * n], refs[2 * n], refs[2 * n + 1], refs[-1]
        x, y, c = _mesh_pos()
        for b in range(n):
            for j, chip in enumerate(_other_chips(x, y)):
                pltpu.make_async_remote_copy(src_ref=s_refs[b].at[2 * chip[0] + chip[1]], dst_ref=land_refs[b].at[2 * x + y],
                                             send_sem=send_sems.at[3 * b + j], recv_sem=recv_sems.at[3 * b + j],
                                             device_id=(*chip, c), device_id_type=MESH).start()
        token[...] = jnp.zeros_like(token)

    operands = [pltpu.with_memory_space_constraint(a, pltpu.HBM) for a in sums + lands]
    res = pl.pallas_call(
        body, name=name, in_specs=[HBM] * (2 * n),
        out_shape=(pltpu.SemaphoreType.DMA((3 * n,)), pltpu.SemaphoreType.DMA((3 * n,)),
                   *[pltpu.HBM(a.shape, a.dtype) for a in operands], jax.ShapeDtypeStruct((SUBLANES, LANES), F32)),
        out_specs=(SEM, SEM, *[HBM] * (2 * n), pl.BlockSpec(memory_space=pltpu.VMEM)),
        input_output_aliases={i: 2 + i for i in range(2 * n)},
        compiler_params=pltpu.CompilerParams(has_side_effects=DATAFLOW),
    )(*operands)
    return res[0], res[1], list(res[2: 2 + n]), list(res[2 + n: 2 + 2 * n]), res[-1]


def _scatter_wait(started, after, *, name):
    send_sems, recv_sems, sums, lands, _ = started
    n = len(sums)

    def body(*refs):
        s_refs, land_refs, send_sems, recv_sems = refs[:n], refs[n: 2 * n], refs[2 * n], refs[2 * n + 1]
        x, y, c = _mesh_pos()
        for b in range(n):
            for j, chip in enumerate(_other_chips(x, y)):
                cp = pltpu.make_async_remote_copy(src_ref=s_refs[b].at[2 * x + y], dst_ref=land_refs[b].at[2 * chip[0] + chip[1]],
                                                  send_sem=send_sems.at[3 * b + j], recv_sem=recv_sems.at[3 * b + j],
                                                  device_id=(*chip, c), device_id_type=MESH)
                cp.wait_send()
                cp.wait_recv()

    res = pl.pallas_call(
        body, name=name, in_specs=(*[HBM] * (2 * n), SEM, SEM, ANY),
        out_shape=tuple(pltpu.HBM(a.shape, a.dtype) for a in sums + lands), out_specs=tuple([HBM] * (2 * n)),
        input_output_aliases={i: i for i in range(2 * n)},
        compiler_params=pltpu.CompilerParams(has_side_effects=DATAFLOW),
    )(*sums, *lands, send_sems, recv_sems, after)
    return list(res[:n]), list(res[n:])


def _reduce_begin(blocks, names, key):
    core = lax.axis_index("c").astype(jnp.int32).reshape(1)
    received = _swap_halves(blocks, _other_half_rows, name=f"rs_swap_{key}")
    sums = [_add_own_half_block(b, r, core, name=f"rs_add_{nm}") for b, r, nm in zip(blocks, received, names, strict=True)]
    return _scatter_start(sums, name=f"rs_scatter_start_{key}")


def _reduce_finish(begun, names, after):
    x, y, c = _mesh_pos()
    chip = 2 * x + y
    mine = {}
    for key, started in begun.items():
        sums, lands = _scatter_wait(started, after, name=f"rs_scatter_wait_{key}")
        parts = [lax.dynamic_update_slice(p, lax.dynamic_slice_in_dim(s, chip, 1, axis=0), (chip, 0, 0))
                 for p, s in zip(lands, sums, strict=True)]
        mine[key] = [_sum_chips_block(p, name=f"rs_sum_{nm}") for p, nm in zip(parts, names[key], strict=True)]
    flat = [a for key in begun for a in mine[key]]
    other = iter(_swap_halves(flat, _whole, name="rs_join"))
    return {key: [jnp.concatenate([jnp.where(c == 0, a, b), jnp.where(c == 0, b, a)], axis=0)
                  for a, b in ((a, next(other)) for a in mine[key])] for key in begun}


WEIGHTS = ("rel_bias", "att_w_in", "att_w_out", "dn_w_in", "dn_conv", "dn_a_log", "dn_dt_bias", "dn_out_norm", "dn_w_out",
           "mem_norm", "mem_w_kv", "norm_mix_pre", "norm_mix_post", "norm_ffn_pre", "norm_ffn_post", "ffn_w_gate_up",
           "ffn_w_down")
BIG_NAMES = tuple(n for n, _, _ in BIG)
SMALL_NAMES = tuple(n for n in WEIGHTS if n not in BIG_NAMES)
CONV_COLS = 3 * TOK_WIDTH
CONV_SHARD = CONV_COLS // N_CHIPS
BLOCKS = tuple((n, layer) for n, shape, _ in BIG for layer in range(shape[0]))
COLUMN_SHARDED = {n: axis == 2 for n, _, axis in BIG}


def kernel(x, mem, rel_bias, att_w_in, att_w_out, dn_w_in, dn_conv, dn_a_log, dn_dt_bias, dn_out_norm, dn_w_out, mem_norm, mem_w_kv, norm_mix_pre, norm_mix_post, norm_ffn_pre, norm_ffn_post, ffn_w_gate_up, ffn_w_down, loss_target, m_rel_bias, m_att_w_in, m_att_w_out, m_dn_w_in, m_dn_conv, m_dn_a_log, m_dn_dt_bias, m_dn_out_norm, m_dn_w_out, m_mem_norm, m_mem_w_kv, m_norm_mix_pre, m_norm_mix_post, m_norm_ffn_pre, m_norm_ffn_post, m_ffn_w_gate_up, m_ffn_w_down, v_rel_bias, v_att_w_in, v_att_w_out, v_dn_w_in, v_dn_conv, v_dn_a_log, v_dn_dt_bias, v_dn_out_norm, v_dn_w_out, v_mem_norm, v_mem_w_kv, v_norm_mix_pre, v_norm_mix_post, v_norm_ffn_pre, v_norm_ffn_post, v_ffn_w_gate_up, v_ffn_w_down):
    w = dict(zip(WEIGHTS, (rel_bias, att_w_in, att_w_out, dn_w_in, dn_conv, dn_a_log, dn_dt_bias, dn_out_norm, dn_w_out,
                           mem_norm, mem_w_kv, norm_mix_pre, norm_mix_post, norm_ffn_pre, norm_ffn_post, ffn_w_gate_up,
                           ffn_w_down)))
    m = dict(zip(WEIGHTS, (m_rel_bias, m_att_w_in, m_att_w_out, m_dn_w_in, m_dn_conv, m_dn_a_log, m_dn_dt_bias,
                           m_dn_out_norm, m_dn_w_out, m_mem_norm, m_mem_w_kv, m_norm_mix_pre, m_norm_mix_post,
                           m_norm_ffn_pre, m_norm_ffn_post, m_ffn_w_gate_up, m_ffn_w_down)))
    v = dict(zip(WEIGHTS, (v_rel_bias, v_att_w_in, v_att_w_out, v_dn_w_in, v_dn_conv, v_dn_a_log, v_dn_dt_bias,
                           v_dn_out_norm, v_dn_w_out, v_mem_norm, v_mem_w_kv, v_norm_mix_pre, v_norm_mix_post,
                           v_norm_ffn_pre, v_norm_ffn_post, v_ffn_w_gate_up, v_ffn_w_down)))
    cx, cy, cc = _mesh_pos()
    chip = 2 * cx + cy

    local = dict(zip(BLOCKS, lax.optimization_barrier([w[n][layer].astype(MXU_DTYPE) for n, layer in BLOCKS]), strict=True))

    def usable(block, got):
        got = lax.dynamic_update_slice(got, local[block][None], (chip, 0, 0))
        return got if COLUMN_SHARDED[block[0]] else got.reshape(-1, got.shape[-1])

    late = {"ffn0": [("ffn_w_gate_up", 0), ("ffn_w_down", 0)], "dn": [("dn_w_in", 0), ("dn_w_out", 0)],
            "ffn1": [("ffn_w_gate_up", 1), ("ffn_w_down", 1)]}
    first = [b for b in BLOCKS if all(b not in blks for blks in late.values())]
    first_got = _gather_blocks([local[b] for b in first], name="gather_weights")
    late_local, _ = lax.optimization_barrier(({k: [local[b] for b in blks] for k, blks in late.items()}, first_got[0]))
    started = {k: _gather_start(late_local[k], name=f"gather_start_{k}") for k in late}
    started_token = sum(s[4][0, 0] for s in started.values())

    def late_weights(key, after):
        lands = _gather_wait(started[key], after, name=f"gather_wait_{key}")
        return [usable(b, got) for b, got in zip(late[key], lands, strict=True)]

    def dn_weights(after):
        w_in, w_out = late_weights("dn", after)
        return _pad_dn_w_in(jnp.concatenate([w_in[s] for s in range(N_CHIPS)], axis=1)), w_out

    full = {}
    for b, got in zip(first, first_got, strict=True):
        full.setdefault(b[0], []).append(usable(b, got))
    conv_rows = _small_rows([(DN_CONV, CONV_COLS)])
    conv_mine = jnp.where(cc == 0, 1.0, 0.0) * w["dn_conv"][0]
    conv_placed = lax.dynamic_update_slice(jnp.zeros((DN_CONV, CONV_COLS), F32), conv_mine, (0, chip * CONV_SHARD))
    conv_full = _unpack_small(_all_reduce_small(_pack_small([conv_placed], conv_rows), name="gather_conv"),
                              [(DN_CONV, CONV_COLS)])[0]
    p = {
        "rel_bias": w["rel_bias"], "att_w_in": full["att_w_in"][0], "att_w_out": full["att_w_out"][0],
        "dn_conv": conv_full, "dn_a_log": w["dn_a_log"][0], "dn_dt_bias": w["dn_dt_bias"][0],
        "dn_out_norm": w["dn_out_norm"][0], "mem_norm": w["mem_norm"], "mem_w_kv": full["mem_w_kv"],
        "norm_mix_pre": w["norm_mix_pre"] + started_token,
        "norm_mix_post": w["norm_mix_post"], "norm_ffn_pre": w["norm_ffn_pre"], "norm_ffn_post": w["norm_ffn_post"],
        "ffn_weights": lambda layer, after: late_weights(f"ffn{layer}", after), "dn_weights": dn_weights,
    }

    def chip_blocks(n, a):
        if n == "dn_w_in":
            a = _unpad_dn_w_in(a)
        if a.ndim == 3:
            return a
        if COLUMN_SHARDED[n]:
            return a.reshape(a.shape[0], N_CHIPS, -1).transpose(1, 0, 2)
        return a.reshape(N_CHIPS, -1, a.shape[-1])

    begun, begun_blocks = {}, {}

    def grads_ready(key, layer_grads):
        begun_blocks[key] = list(layer_grads)
        begun[key] = _reduce_begin([chip_blocks(n, a) for (n, _), a in layer_grads.items()],
                                   [f"{n}{layer}" for n, layer in layer_grads], key)
        return begun[key][4][0, 0]

    p["grads_ready"] = grads_ready
    loss_cols, grad_x, g = _local_step(x[0], mem[0], loss_target[0], p)
    loss = lax.psum(jnp.sum(loss_cols), ("x", "y", "c"))
    grads_ready("att", {("att_w_in", 0): g["att_w_in"], ("att_w_out", 0): g["att_w_out"],
                        ("mem_w_kv", 0): g["mem_w_kv"][0], ("mem_w_kv", 1): g["mem_w_kv"][1]})
    finished = _reduce_finish(begun, {k: [f"{n}{layer}" for n, layer in blks] for k, blks in begun_blocks.items()}, grad_x)
    reduced = {b: r for k in begun for b, r in zip(begun_blocks[k], finished[k], strict=True)}
    grads = {n: jnp.concatenate([reduced[b] for b in BLOCKS if b[0] == n], axis=0).reshape(shape) for n, shape, _ in BIG}
    small_full_shapes = [(DN_CONV, CONV_COLS) if n == "dn_conv" else w[n].shape for n in SMALL_NAMES]
    small_sum = _all_reduce_small(_pack_small([g[n] for n in SMALL_NAMES], _small_rows(small_full_shapes)), name="reduce_small")
    for n, s in zip(SMALL_NAMES, _unpack_small(small_sum, small_full_shapes)):
        grads[n] = lax.dynamic_slice(s, (0, chip * CONV_SHARD), (DN_CONV, CONV_SHARD))[None] if n == "dn_conv" else s

    delta, new_m, new_v = {}, {}, {}
    for n in BIG_NAMES:
        shape = w[n].shape
        two_d = lambda a: a.reshape(-1, shape[-1])
        res = _adamw(two_d(w[n]), two_d(grads[n]), two_d(m[n]), two_d(v[n]), name=f"adamw_{n}")
        delta[n], new_m[n], new_v[n] = (r.reshape(shape) for r in res)
    small_shapes = [w[n].shape for n in SMALL_NAMES]
    rows = _small_rows(small_shapes)
    res = _adamw(*[_pack_small([d[n] for n in SMALL_NAMES], rows) for d in (w, grads, m, v)], name="adamw_small")
    for d, r in zip((delta, new_m, new_v), res):
        for n, a in zip(SMALL_NAMES, _unpack_small(r, small_shapes)):
            d[n] = a
    return (loss, grad_x[None], *[grads[n] for n in WEIGHTS], *[delta[n] for n in WEIGHTS],
            *[new_m[n] for n in WEIGHTS], *[new_v[n] for n in WEIGHTS])
```

```python
import functools
import math

import numpy as np
import jax
import jax.numpy as jnp
from jax import lax
from jax.experimental import pallas as pl
from jax.experimental.pallas import tpu as pltpu

F32 = jnp.float32
MXU_DTYPE = jnp.bfloat16
LINK_DTYPE = jnp.bfloat16
HI = lax.Precision.HIGHEST

EPS = 1e-6
NEG_INF = -1e30
LANES = 128
SUBLANES = 8
VMEM_LIMIT = 56 * 1024 * 1024

D_MODEL = 1024
TOK_WIDTH = 768
MEM_WIDTH = 256
MEM_LEN = 256
ATT_HEAD_DIM = 64
DILATIONS = (1, 4, 16)
HALF = 64
ATT_BQ = 128
ATT_W = ATT_BQ + 2 * HALF
REL_BUCKETS = 32
REL_MAX_DIST = 1024
DN_HEADS = 6
DN_HEAD_DIM = 128
DN_CONV = 5
DN_CHUNK = 128
D_FF = 2816
ATT_IN = 2560
DN_IN = 3352
DN_IN_PAD = 3456
N_GATES = 4 * DN_HEADS

ADAM_LR = 0.001
ADAM_B1 = 0.9
ADAM_B2 = 0.999
ADAM_EPS = 1e-08
ADAM_WD = 0.01
ADAM_STEP = 10


def _tile(n, target, align):
    if n <= target:
        return n
    t = (target // align) * align
    while t >= align:
        if n % t == 0:
            return t
        t -= align
    raise ValueError(f"no tile for {n} (target {target}, align {align})")


def _params(*sem):
    return pltpu.CompilerParams(dimension_semantics=sem, vmem_limit_bytes=VMEM_LIMIT)


def _mm(a, b, *, name, ta=False, tb=False, out_shards=None, tm=1024, tn=1408, tk=1408, out_dtype=F32):
    if ta:
        K, M = a.shape
    else:
        M, K = a.shape
    sharded_b = b.ndim == 3
    if sharded_b:
        n_sh, b_rows, b_cols = b.shape
        N, K2 = (b_rows, n_sh * b_cols) if tb else (n_sh * b_cols, b_rows)
    else:
        N, K2 = b.shape if tb else b.shape[::-1]
    assert K == K2, (a.shape, b.shape, ta, tb)
    tm = _tile(M, tm, LANES if ta else SUBLANES)
    tn = N // out_shards if out_shards else (b_cols if sharded_b and not tb else _tile(N, tn, LANES))
    tk = b_cols if sharded_b and tb else _tile(K, tk, LANES)
    nk = K // tk
    a_spec = pl.BlockSpec((tk, tm), lambda i, j, k: (k, i)) if ta else pl.BlockSpec((tm, tk), lambda i, j, k: (i, k))
    if sharded_b:
        b_spec = (pl.BlockSpec((None, tn, tk), lambda i, j, k: (k, j, 0)) if tb
                  else pl.BlockSpec((None, tk, tn), lambda i, j, k: (j, k, 0)))
    else:
        b_spec = pl.BlockSpec((tn, tk), lambda i, j, k: (j, k)) if tb else pl.BlockSpec((tk, tn), lambda i, j, k: (k, j))
    if out_shards:
        out_spec = pl.BlockSpec((None, tm, tn), lambda i, j, k: (j, i, 0))
        out_shape = jax.ShapeDtypeStruct((out_shards, M, tn), out_dtype)
    else:
        out_spec = pl.BlockSpec((tm, tn), lambda i, j, k: (i, j))
        out_shape = jax.ShapeDtypeStruct((M, N), out_dtype)
    dims = (((0 if ta else 1,), (1 if tb else 0,)), ((), ()))

    def kern(a_ref, b_ref, o_ref, acc_ref):
        k = pl.program_id(2)

        @pl.when(k == 0)
        def _():
            acc_ref[...] = jnp.zeros_like(acc_ref)

        acc_ref[...] += lax.dot_general(a_ref[...].astype(MXU_DTYPE), b_ref[...].astype(MXU_DTYPE), dims,
                                        preferred_element_type=F32)

        @pl.when(k == nk - 1)
        def _():
            o_ref[...] = acc_ref[...].astype(o_ref.dtype)

    return pl.pallas_call(
        kern, name=name, grid=(M // tm, N // tn, nk), in_specs=[a_spec, b_spec],
        out_specs=out_spec, out_shape=out_shape,
        scratch_shapes=[pltpu.VMEM((tm, tn), F32)],
        compiler_params=_params("parallel", "parallel", "arbitrary"),
    )(a, b)


def _shards_of(w):
    return w.shape[0] if w.ndim == 3 else None


def _col(arr, width, blk):
    return (arr, width, blk)


def _rowwise(body, rows, consts, out_rows, out_acc, *, tm, name):
    n_rows = (rows[0][0] if isinstance(rows[0], tuple) else rows[0]).shape[0]
    assert n_rows % tm == 0, (n_rows, tm)
    arrs, in_specs = [], []
    for r in rows:
        arr, width, blk = r if isinstance(r, tuple) else (r, r.shape[1], 0)
        assert arr.shape[0] == n_rows
        arrs.append(arr)
        in_specs.append(pl.BlockSpec((tm, width), functools.partial(lambda i, b: (i, b), b=blk)))
    for c in consts:
        arrs.append(c)
        in_specs.append(pl.BlockSpec(c.shape, functools.partial(lambda i, n: (0,) * n, n=c.ndim)))
    n_in, n_ro = len(arrs), len(out_rows)
    out_shape = [jax.ShapeDtypeStruct((n_rows, w), dt) for w, dt in out_rows]
    out_specs = [pl.BlockSpec((tm, w), lambda i: (i, 0)) for w, _ in out_rows]
    out_shape += [jax.ShapeDtypeStruct(s, F32) for s in out_acc]
    out_specs += [pl.BlockSpec(s, lambda i: (0, 0)) for s in out_acc]

    def kern(*refs):
        ro, ao = body(*[r[...] for r in refs[:n_in]])
        outs = refs[n_in:]
        for r, v in zip(outs[:n_ro], ro, strict=True):
            r[...] = v.astype(r.dtype)
        if out_acc:
            @pl.when(pl.program_id(0) == 0)
            def _():
                for r in outs[n_ro:]:
                    r[...] = jnp.zeros_like(r)

            for r, v in zip(outs[n_ro:], ao, strict=True):
                r[...] += v

    res = pl.pallas_call(
        kern, name=name, grid=(n_rows // tm,), in_specs=in_specs, out_specs=out_specs, out_shape=out_shape,
        compiler_params=_params("arbitrary" if out_acc else "parallel"),
    )(*arrs)
    return res


def _rms(x, gain):
    return x * lax.rsqrt(jnp.mean(x * x, axis=-1, keepdims=True) + EPS) * gain


def _silu(x):
    return x * jax.nn.sigmoid(x)


def _softplus(x):
    return jnp.maximum(x, 0.0) + jnp.log(1.0 + jnp.exp(-jnp.abs(x)))


def _dot_nt(a, b, precision=None):
    return lax.dot_general(a, b, (((1,), (1,)), ((), ())), preferred_element_type=F32, precision=precision)


def _dot_tn(a, b, precision=None):
    return lax.dot_general(a, b, (((0,), (0,)), ((), ())), preferred_element_type=F32, precision=precision)


def _dot(a, b, precision=None):
    return jnp.dot(a, b, preferred_element_type=F32, precision=precision)


def _pre_norm(x, gain, *, name):
    def body(x, g):
        return (_rms(x, g),), ()
    return _rowwise(body, [x], [gain], [(x.shape[1], MXU_DTYPE)], [], tm=_tile(x.shape[0], 512, 2 * SUBLANES), name=name)[0]


def _pre_norm_bwd(x, gain, dh, dx_other, *, name):
    def body(x, dh, dxo, g):
        _, vjp = jax.vjp(_rms, x, g)
        dx, dg = vjp(dh)
        return (dx + dxo,), (dg,)
    return _rowwise(body, [x, dh, dx_other], [gain], [(x.shape[1], F32)], [gain.shape], tm=512, name=name)


def _gain_bwd(x, gain, dh, *, name):
    def body(x, dh, g):
        _, vjp = jax.vjp(lambda g_: _rms(x, g_), g)
        return (), (vjp(dh)[0],)
    return _rowwise(body, [x, dh], [gain], [], [gain.shape], tm=_tile(x.shape[0], 512, SUBLANES), name=name)[0]


def _res_block(x_res, m, g_post, g_pre):
    x_new = x_res + _rms(m, g_post)
    return x_new, _rms(x_new, g_pre)


def _post_pre(x_res, m, g_post, g_pre, *, name):
    def body(x, m, gp, gq):
        return _res_block(x, m, gp, gq), ()
    d = x_res.shape[1]
    return _rowwise(body, [x_res, m], [g_post, g_pre], [(d, F32), (d, MXU_DTYPE)], [], tm=512, name=name)


def _post_pre_bwd(x_res, m, g_post, g_pre, dx_new, dh, *, name):
    def body(x, m, dxn, dh, gp, gq):
        _, vjp = jax.vjp(_res_block, x, m, gp, gq)
        dx, dm, dgp, dgq = vjp((dxn, dh))
        return (dx, dm), (dgp, dgq)
    d = x_res.shape[1]
    return _rowwise(body, [x_res, m, dx_new, dh], [g_post, g_pre], [(d, F32), (d, MXU_DTYPE)],
                    [g_post.shape, g_pre.shape], tm=256, name=name)


def _final_loss_bwd(x_res, m, g_post, target, *, name):
    d = x_res.shape[1]

    def loss_cols(x, m, g, t):
        err = x + _rms(m, g) - t
        return jnp.sum(err * err, axis=0, keepdims=True) * (0.5 / d)

    def body(x, m, t, g):
        cols, vjp = jax.vjp(lambda x_, m_, g_: loss_cols(x_, m_, g_, t), x, m, g)
        dx, dm, dg = vjp(jnp.ones_like(cols))
        return (dx, dm), (dg, cols)
    return _rowwise(body, [x_res, m, target], [g_post], [(d, F32), (d, MXU_DTYPE)], [g_post.shape, (1, d)], tm=256, name=name)


def _swiglu_act(gu, *, name):
    def body(gate, up):
        return (_silu(gate.astype(F32)) * up.astype(F32),), ()
    return _rowwise(body, [_col(gu, D_FF, 0), _col(gu, D_FF, 1)], [], [(D_FF, MXU_DTYPE)], [], tm=256, name=name)[0]


def _swiglu_act_bwd(gu, da, *, name):
    def body(gate, up, da):
        _, vjp = jax.vjp(lambda g, u: _silu(g) * u, gate.astype(F32), up.astype(F32))
        dg, du = vjp(da.astype(F32))
        return (jnp.concatenate([dg, du], axis=1),), ()
    return _rowwise(body, [_col(gu, D_FF, 0), _col(gu, D_FF, 1), da], [], [(2 * D_FF, MXU_DTYPE)], [], tm=256, name=name)[0]


def _lane_head_mask(width, head_dim, head):
    lane = lax.broadcasted_iota(jnp.int32, (1, width), 1)
    return (lane // head_dim) == head


def _mem_attn_pair(q_pair, k_pair, v_pair):
    out = jnp.zeros_like(q_pair)
    for h in range(2):
        mh = _lane_head_mask(LANES, ATT_HEAD_DIM, h)
        qh = jnp.where(mh, q_pair * (ATT_HEAD_DIM ** -0.5), 0.0)
        logits = _dot_nt(qh, k_pair)
        mx = jnp.max(logits, axis=-1, keepdims=True)
        p = jnp.exp(logits - mx)
        p = p / jnp.sum(p, axis=-1, keepdims=True)
        out = out + jnp.where(mh, _dot(p, v_pair), 0.0)
    return out


def _mem_attn(q_mem, kv):
    outs = []
    for p in range(MEM_WIDTH // LANES):
        sl = slice(p * LANES, (p + 1) * LANES)
        outs.append(_mem_attn_pair(q_mem[:, sl], kv[:, sl], kv[:, MEM_WIDTH + p * LANES: MEM_WIDTH + (p + 1) * LANES]))
    return jnp.concatenate(outs, axis=1)


def _mem_attn_bwd(q_mem, kv, do):
    dqs, dks, dvs = [], [], []
    for p in range(MEM_WIDTH // LANES):
        sl = slice(p * LANES, (p + 1) * LANES)
        sv = slice(MEM_WIDTH + p * LANES, MEM_WIDTH + (p + 1) * LANES)
        _, vjp = jax.vjp(_mem_attn_pair, q_mem[:, sl], kv[:, sl], kv[:, sv])
        dq, dk, dv = vjp(do[:, sl])
        dqs.append(dq)
        dks.append(dk)
        dvs.append(dv)
    return jnp.concatenate(dqs, axis=1), jnp.concatenate(dks + dvs, axis=1)


def _t5_bucket(rel):
    half = REL_BUCKETS // 2
    max_exact = half // 2
    n = np.abs(rel)
    large = max_exact + (np.log(np.maximum(n, 1) / max_exact) / math.log(REL_MAX_DIST / max_exact)
                         * (half - max_exact)).astype(np.int64)
    large = np.minimum(large, half - 1)
    return ((rel > 0) * half + np.where(n < max_exact, n, large)).astype(np.int32)


ATT_DIAGS = ATT_BQ + ATT_W - 1


def _bias_diag_onehot(dil):
    j = np.arange(ATT_DIAGS)
    tiles = []
    for off in (-HALF, 0, HALF):
        rel = j - (ATT_BQ - 1) - HALF - off
        hot = _t5_bucket(rel * dil)[:, None] == np.arange(REL_BUCKETS)[None, :]
        tiles.append(hot & (np.abs(rel) <= HALF)[:, None])
    return np.stack(tiles).astype(np.float32)


def _toeplitz(r):
    lead = r.shape[:-1]
    a = jnp.broadcast_to(r[..., None, :], lead + (ATT_BQ, ATT_DIAGS))
    a = jnp.pad(a, [(0, 0)] * len(lead) + [(0, 0), (0, 1)])
    a = a.reshape(lead + (ATT_BQ * (ATT_DIAGS + 1),))[..., : ATT_BQ * ATT_DIAGS].reshape(lead + (ATT_BQ, ATT_DIAGS))
    return a[..., ATT_BQ - 1: ATT_BQ - 1 + ATT_W]


def _bias_tiles(rel_bias, gi):
    heads = rel_bias[:, 4 * gi: 4 * gi + 4]
    diag = jnp.einsum('tnb,bh->thn', jnp.asarray(_bias_diag_onehot(DILATIONS[gi])), heads, precision=HI)
    return _toeplitz(diag)


def _bias_tiles_bwd(rel_bias, dtiles, gi):
    return jax.vjp(lambda rb: _bias_tiles(rb, gi), rel_bias)[1](dtiles)[0]


def _att_window(i, n_sub):
    start = jnp.clip(i * ATT_BQ - HALF, 0, n_sub - ATT_W)
    off = i * ATT_BQ - HALF - start
    return pl.multiple_of(start, HALF), off


def _att_valid(off):
    q = lax.broadcasted_iota(jnp.int32, (ATT_BQ, ATT_W), 0)
    kk = lax.broadcasted_iota(jnp.int32, (ATT_BQ, ATT_W), 1)
    return jnp.abs(kk - q - HALF - off) <= HALF


def _att_tile_id(i, nq):
    return jnp.where(i == 0, 0, jnp.where(i == nq - 1, 2, 1))


def _att_fwd(qkvm, bias, gi, *, name):
    dil = DILATIONS[gi]
    s_len = qkvm.shape[0]
    n_sub = s_len // dil
    nq = n_sub // ATT_BQ
    assert n_sub % ATT_BQ == 0 and n_sub >= ATT_W
    cols = qkvm.shape[1] // LANES
    view = qkvm.reshape(n_sub, dil * qkvm.shape[1])

    def kern(q_ref, k_ref, v_ref, b_ref, o_ref, lse_ref):
        i = pl.program_id(2)
        start, off = _att_window(i, n_sub)
        valid = _att_valid(off)
        q = q_ref[...].astype(F32) * (ATT_HEAD_DIM ** -0.5)
        kw = k_ref[pl.ds(start, ATT_W), :].astype(F32)
        vw = v_ref[pl.ds(start, ATT_W), :].astype(F32)
        o = jnp.zeros((ATT_BQ, LANES), F32)
        lse = jnp.zeros((ATT_BQ, LANES), F32)
        for h in range(2):
            mh = _lane_head_mask(LANES, ATT_HEAD_DIM, h)
            s = _dot_nt(jnp.where(mh, q, 0.0), kw) + b_ref[h]
            s = jnp.where(valid, s, NEG_INF)
            mx = jnp.max(s, axis=-1, keepdims=True)
            p = jnp.exp(s - mx)
            den = jnp.sum(p, axis=-1, keepdims=True)
            o = jnp.where(mh, _dot(p, vw) / den, o)
            lse = jnp.where(mh, mx + jnp.log(den), lse)
        o_ref[...] = o
        lse_ref[...] = lse

    def qkv_spec(which, full):
        shape = (n_sub, LANES) if full else (ATT_BQ, LANES)
        return pl.BlockSpec(shape, lambda pr, r, i: (0 if full else i, r * cols + which * 6 + 2 * gi + pr))

    out_spec = pl.BlockSpec((ATT_BQ, LANES), lambda pr, r, i: (i, r * 2 + pr))
    o, lse = pl.pallas_call(
        kern, name=name, grid=(2, dil, nq),
        in_specs=[qkv_spec(0, False), qkv_spec(1, True), qkv_spec(2, True),
                  pl.BlockSpec((None, 2, ATT_BQ, ATT_W), lambda pr, r, i: (_att_tile_id(i, nq), pr, 0, 0))],
        out_specs=[out_spec, out_spec],
        out_shape=[jax.ShapeDtypeStruct((n_sub, dil * 2 * LANES), F32)] * 2,
        compiler_params=_params("parallel", "parallel", "arbitrary"),
    )(view, view, view, bias)
    return o.reshape(s_len, 2 * LANES), lse.reshape(s_len, 2 * LANES)


def _att_bwd(qkvm, bias, lse_tot, delta, dcat, gi, *, name):
    dil = DILATIONS[gi]
    s_len = qkvm.shape[0]
    n_sub = s_len // dil
    nq = n_sub // ATT_BQ
    cols = qkvm.shape[1] // LANES
    dcols = dcat.shape[1] // LANES
    view = qkvm.reshape(n_sub, dil * qkvm.shape[1])
    lse_v = lse_tot.reshape(n_sub, dil * 2 * LANES)
    delta_v = delta.reshape(n_sub, dil * 2 * LANES)
    dcat_v = dcat.reshape(n_sub, dil * dcat.shape[1])

    def kern(q_ref, k_ref, v_ref, b_ref, lse_ref, dl_ref, dm_ref, dq_ref, dk_ref, dv_ref, db_ref):
        r, i = pl.program_id(1), pl.program_id(2)
        start, off = _att_window(i, n_sub)
        valid = _att_valid(off)
        tile = _att_tile_id(i, nq)

        @pl.when(i == 0)
        def _():
            dk_ref[...] = jnp.zeros_like(dk_ref)
            dv_ref[...] = jnp.zeros_like(dv_ref)

        @pl.when((i == 0) & (r == 0))
        def _():
            db_ref[...] = jnp.zeros_like(db_ref)

        q = q_ref[...].astype(F32) * (ATT_HEAD_DIM ** -0.5)
        kw = k_ref[pl.ds(start, ATT_W), :].astype(F32)
        vw = v_ref[pl.ds(start, ATT_W), :].astype(F32)
        dm = dm_ref[...]
        lse = lse_ref[...]
        dl = dl_ref[...]
        dq = jnp.zeros((ATT_BQ, LANES), F32)
        dkw = jnp.zeros((ATT_W, LANES), F32)
        dvw = jnp.zeros((ATT_W, LANES), F32)
        for h in range(2):
            mh = _lane_head_mask(LANES, ATT_HEAD_DIM, h)
            qh = jnp.where(mh, q, 0.0)
            dmh = jnp.where(mh, dm, 0.0)
            s = _dot_nt(qh, kw) + b_ref[tile, h]
            s = jnp.where(valid, s, NEG_INF)
            lse_h = jnp.max(jnp.where(mh, lse, NEG_INF), axis=-1, keepdims=True)
            dl_h = jnp.max(jnp.where(mh, dl, NEG_INF), axis=-1, keepdims=True)
            p = jnp.exp(s - lse_h)
            ds = p * (_dot_nt(dmh, vw) - dl_h)
            dq = dq + jnp.where(mh, _dot(ds, kw), 0.0)
            dkw = dkw + _dot_tn(ds, qh)
            dvw = dvw + _dot_tn(p, dmh)
            db_ref[tile, h] += ds
        dq_ref[...] = dq * (ATT_HEAD_DIM ** -0.5)
        dk_ref[pl.ds(start, ATT_W), :] += dkw
        dv_ref[pl.ds(start, ATT_W), :] += dvw

    def qkv_spec(which, full):
        shape = (n_sub, LANES) if full else (ATT_BQ, LANES)
        return pl.BlockSpec(shape, lambda pr, r, i: (0 if full else i, r * cols + which * 6 + 2 * gi + pr))

    blk = pl.BlockSpec((ATT_BQ, LANES), lambda pr, r, i: (i, r * 2 + pr))
    full = pl.BlockSpec((n_sub, LANES), lambda pr, r, i: (0, r * 2 + pr))
    bias_spec = pl.BlockSpec((3, 2, ATT_BQ, ATT_W), lambda pr, r, i: (0, pr, 0, 0))
    sub = jax.ShapeDtypeStruct((n_sub, dil * 2 * LANES), F32)
    dq, dk, dv, db = pl.pallas_call(
        kern, name=name, grid=(2, dil, nq),
        in_specs=[qkv_spec(0, False), qkv_spec(1, True), qkv_spec(2, True), bias_spec, blk, blk,
                  pl.BlockSpec((ATT_BQ, LANES), lambda pr, r, i: (i, r * dcols + 2 * gi + pr))],
        out_specs=[blk, full, full, bias_spec],
        out_shape=[sub, sub, sub, jax.ShapeDtypeStruct(bias.shape, F32)],
        compiler_params=_params("arbitrary", "arbitrary", "arbitrary"),
    )(view, view, view, bias, lse_v, delta_v, dcat_v)
    return dq.reshape(s_len, -1), dk.reshape(s_len, -1), dv.reshape(s_len, -1), db


def _att_combine(o_g, lse_g, qkvm, kv_mem, *, name):
    def body(o0, o1, o2, l0, l1, l2, qm, kv):
        mx = jnp.maximum(jnp.maximum(l0, l1), l2)
        tot = mx + jnp.log(jnp.exp(l0 - mx) + jnp.exp(l1 - mx) + jnp.exp(l2 - mx))
        mixed = [o * jnp.exp(l - tot) for o, l in ((o0, l0), (o1, l1), (o2, l2))]
        return (jnp.concatenate(mixed + [_mem_attn(qm.astype(F32), kv)], axis=1), tot), ()
    return _rowwise(body, list(o_g) + list(lse_g) + [_col(qkvm, MEM_WIDTH, (3 * TOK_WIDTH) // MEM_WIDTH)], [kv_mem],
                    [(D_MODEL, F32), (MEM_WIDTH, F32)], [], tm=256, name=name)


def _head_sum_matrix():
    a = np.arange(MEM_WIDTH)
    return jnp.asarray((a[:, None] // ATT_HEAD_DIM == a[None, :] // ATT_HEAD_DIM).astype(np.float32))


def _att_bwd_prep(cat, dcat, qkvm, kv_mem, *, name):
    def body(cat, dcat, qm, kv, hs):
        prod = cat * dcat
        summed = prod[:, 0:256] + prod[:, 256:512] + prod[:, 512:768]
        delta = _dot(summed, hs, precision=HI)
        dqm, dkv = _mem_attn_bwd(qm.astype(F32), kv, dcat[:, TOK_WIDTH:])
        return (delta, dqm), (dkv,)
    return _rowwise(body, [cat, dcat, _col(qkvm, MEM_WIDTH, (3 * TOK_WIDTH) // MEM_WIDTH)], [kv_mem, _head_sum_matrix()],
                    [(MEM_WIDTH, F32), (MEM_WIDTH, F32)], [kv_mem.shape], tm=256, name=name)


def _dn_conv_post(s, j):
    scale = jnp.where(j < DN_HEADS, DN_HEAD_DIM ** -0.5, 1.0)
    normed = s * lax.rsqrt(jnp.sum(s * s, axis=-1, keepdims=True) + EPS) * scale
    return jnp.where(j >= 2 * DN_HEADS, s, normed)


def _shift_rows(x, sh):
    n = x.shape[0]
    row = lax.broadcasted_iota(jnp.int32, (n, 1), 0)
    rolled = pltpu.roll(x, (-sh) % n, 0)
    return jnp.where((row + sh >= 0) & (row + sh < n), rolled, 0.0)


def _dn_conv_taps(x, w_ref):
    c = x * w_ref[pl.ds(DN_CONV // 2, 1), :]
    for jj in range(DN_CONV):
        if jj != DN_CONV // 2:
            c = c + _shift_rows(x, jj - DN_CONV // 2) * w_ref[pl.ds(jj, 1), :]
    return c


def _dn_conv_fwd(proj, conv_w, *, name):
    s_len = proj.shape[0]
    width = 3 * TOK_WIDTH

    def kern(x_ref, w_ref, o_ref):
        j = pl.program_id(0)
        o_ref[...] = _dn_conv_post(_silu(_dn_conv_taps(x_ref[...], w_ref)), j)

    return pl.pallas_call(
        kern, name=name, grid=(width // LANES,),
        in_specs=[pl.BlockSpec((s_len, LANES), lambda j: (0, j)), pl.BlockSpec((DN_CONV, LANES), lambda j: (0, j))],
        out_specs=pl.BlockSpec((s_len, LANES), lambda j: (0, j)),
        out_shape=jax.ShapeDtypeStruct((s_len, width), F32),
        compiler_params=_params("parallel"),
    )(proj, conv_w)


def _dn_conv_bwd(proj, conv_w, d_fwd, d_bwd, which, *, name):
    s_len = proj.shape[0]

    def kern(x_ref, w_ref, df_ref, db_ref, dx_ref, dw_ref):
        j = pl.program_id(0) + which * DN_HEADS
        x = x_ref[...]
        c = _dn_conv_taps(x, w_ref)
        _, vjp = jax.vjp(lambda c_: _dn_conv_post(_silu(c_), j), c)
        dc = vjp(df_ref[...] + db_ref[...])[0]
        dx = dc * w_ref[pl.ds(DN_CONV // 2, 1), :]
        for jj in range(DN_CONV):
            sh = jj - DN_CONV // 2
            if sh != 0:
                dx = dx + _shift_rows(dc, -sh) * w_ref[pl.ds(jj, 1), :]
            dw_ref[pl.ds(jj, 1), :] = jnp.sum(dc * _shift_rows(x, sh), axis=0, keepdims=True)
        dx_ref[...] = dx

    return pl.pallas_call(
        kern, name=name, grid=(DN_HEADS,),
        in_specs=[pl.BlockSpec((s_len, LANES), lambda j: (0, j + which * DN_HEADS)),
                  pl.BlockSpec((DN_CONV, LANES), lambda j: (0, j + which * DN_HEADS)),
                  pl.BlockSpec((s_len, LANES), lambda j: (0, j)),
                  pl.BlockSpec((s_len, LANES), lambda j: (0, j))],
        out_specs=[pl.BlockSpec((s_len, LANES), lambda j: (0, j)), pl.BlockSpec((DN_CONV, LANES), lambda j: (0, j))],
        out_shape=[jax.ShapeDtypeStruct((s_len, TOK_WIDTH), F32), jax.ShapeDtypeStruct((DN_CONV, TOK_WIDTH), F32)],
        compiler_params=_params("parallel"),
    )(proj, conv_w, d_fwd, d_bwd)


GATE_TM = 2 * DN_CHUNK


FWD_GATE_LANES = 2 * DN_HEADS


def _gate_constants():
    i = np.arange(GATE_TM)
    same = (i[:, None] // DN_CHUNK) == (i[None, :] // DN_CHUNK)
    cum_f = same & (i[None, :] <= i[:, None])
    cum_r = same & (i[None, :] >= i[:, None])
    return tuple(jnp.asarray(np.asarray(a, np.float32)) for a in (cum_f, cum_r, same))


def _gate_params(p):
    z = jnp.zeros((DN_HEADS,), F32)
    return jnp.concatenate([p[0], z, p[1], z, jnp.zeros((LANES - N_GATES,), F32)]).reshape(1, LANES)


def _gate_params_bwd(dp):
    return jnp.stack([dp[0, 0:DN_HEADS], dp[0, 2 * DN_HEADS: 3 * DN_HEADS]])


def _dn_gates(gate_in, a_cols, dt_cols, cum_f, cum_r, tot):
    g = -jnp.exp(a_cols) * _softplus(gate_in + dt_cols)
    fwd_lane = lax.broadcasted_iota(jnp.int32, (1, LANES), 1) < FWD_GATE_LANES
    gc = jnp.where(fwd_lane, _dot(cum_f, g, precision=HI), _dot(cum_r, g, precision=HI))
    return gc, _dot(tot, g, precision=HI), jax.nn.sigmoid(gate_in)


def _dn_gates_fwd(proj, a_cols, dt_cols, *, name):
    def body(gi, *consts):
        return _dn_gates(gi, *consts), ()
    return _rowwise(body, [_col(proj, LANES, DN_IN_PAD // LANES - 1)], [a_cols, dt_cols, *_gate_constants()],
                    [(LANES, F32)] * 3, [], tm=GATE_TM, name=name)


def _dn_gates_bwd(proj, a_cols, dt_cols, d_gates, *, name):
    def body(gi, gcf, gtf, bf, gcr, gtr, br, a, dt, *consts):
        _, vjp = jax.vjp(lambda gi_, a_, dt_: _dn_gates(gi_, a_, dt_, *consts), gi, a, dt)
        dgi, da, ddt = vjp((gcf + gcr, gtf + gtr, bf + br))
        return (dgi,), (da, ddt)
    return _rowwise(body, [_col(proj, LANES, DN_IN_PAD // LANES - 1), *d_gates[0], *d_gates[1]],
                    [a_cols, dt_cols, *_gate_constants()], [(LANES, F32)], [a_cols.shape, dt_cols.shape],
                    tm=GATE_TM, name=name)


INV_BASE = 8


def _block_id_equal(c, size):
    i = lax.broadcasted_iota(jnp.int32, (c, c), 0) // size
    j = lax.broadcasted_iota(jnp.int32, (c, c), 1) // size
    return (i == j).astype(F32)


def _unit_tri_inverse_impl(lmat):
    c = lmat.shape[0]
    eye = _block_id_equal(c, 1)
    same = _block_id_equal(c, INV_BASE)
    neg = -lmat * same
    inv = eye + neg
    power = neg
    for _ in range(int(math.log2(INV_BASE)) - 1):
        power = _dot(power, power)
        inv = inv + _dot(inv, power)
    size = INV_BASE
    while size < c:
        bigger = _block_id_equal(c, 2 * size)
        inv = inv - _dot(_dot(inv, lmat * (bigger - same)), inv)
        same, size = bigger, 2 * size
    resid = eye - _dot(eye + lmat, inv, precision=HI)
    return inv + _dot(inv, resid)


@jax.custom_vjp
def _unit_tri_inverse(lmat):
    return _unit_tri_inverse_impl(lmat)


def _unit_tri_inverse_fwd(lmat):
    inv = _unit_tri_inverse_impl(lmat)
    return inv, inv


def _unit_tri_inverse_bwd(inv, d_inv):
    return (-_dot_tn(inv, _dot_nt(d_inv, inv)),)


_unit_tri_inverse.defvjp(_unit_tri_inverse_fwd, _unit_tri_inverse_bwd)


def _dn_chunk(q, k, v, gates_t, gc_row, tot_row, beta_row, state, tri, inverse):
    c = q.shape[0]
    assert c == DN_HEAD_DIM
    eye = _block_id_equal(c, 1)

    def along_rows(x, pick):
        return jnp.broadcast_to(jnp.sum(x * pick, axis=0, keepdims=True), (c, c))

    gc_j = along_rows(gates_t[0], gc_row)
    gc = gc_j.T
    g_tot = along_rows(gates_t[1], tot_row)
    beta = along_rows(gates_t[2], beta_row).T
    decay = jnp.exp(jnp.where(tri > 0, gc - gc_j, NEG_INF))
    k_beta = k * beta
    inv = inverse((tri - eye) * (_dot_nt(k_beta, k) * decay))
    e_gc = jnp.exp(gc)
    u = _dot(inv, v * beta)
    w = _dot(inv, k_beta * e_gc)
    intra = tri * (_dot_nt(q, k) * decay)
    v_new = u - _dot(w, state)
    out = _dot(q * e_gc, state) + _dot(intra, v_new)
    state = state * jnp.exp(g_tot) + _dot_tn(k * jnp.exp(g_tot - gc), v_new)
    return out, state


def _dn_tri():
    i = np.arange(DN_CHUNK)
    tri = np.stack([(i[None, :] <= i[:, None]), (i[None, :] >= i[:, None])]).astype(np.float32)
    return jnp.asarray(np.repeat(tri, DN_HEADS, axis=0))


def _dn_gate_picks():
    picks = np.zeros((3, 2 * DN_HEADS, 2 * DN_CHUNK, 1), np.float32)
    for d in range(2):
        for h in range(DN_HEADS):
            alpha = d * DN_CHUNK + d * 2 * DN_HEADS + h
            picks[0, d * DN_HEADS + h, alpha] = 1.0
            picks[1, d * DN_HEADS + h, alpha] = 1.0
            picks[2, d * DN_HEADS + h, alpha + DN_HEADS] = 1.0
    return jnp.asarray(picks)


def _stack_chains(fwd_ref, rev_ref):
    return jnp.stack([r[:, _head_cols(h)] for r in (fwd_ref, rev_ref) for h in range(DN_HEADS)])


def _unstack_chains(val, fwd_ref, rev_ref):
    for d, r in enumerate((fwd_ref, rev_ref)):
        for h in range(DN_HEADS):
            r[:, _head_cols(h)] = val[d * DN_HEADS + h]


def _gates_transposed(fwd_refs, rev_refs):
    return jnp.stack([jnp.concatenate([f[...].T, r[...].T], axis=0) for f, r in zip(fwd_refs, rev_refs, strict=True)])


def _dn_row_spec(nc, col, reverse, width=TOK_WIDTH):
    return pl.BlockSpec((DN_CHUNK, width), lambda t: ((nc - 1 - t) if reverse else t, col))


def _dn_state_spec(nc, reverse):
    return pl.BlockSpec((None, DN_HEADS, DN_HEAD_DIM, DN_HEAD_DIM), lambda t: ((nc - 1 - t) if reverse else t, 0, 0, 0))


def _head_cols(h):
    return pl.ds(h * DN_HEAD_DIM, DN_HEAD_DIM)


def _const_spec(arr):
    return pl.BlockSpec(arr.shape, functools.partial(lambda t, n: (0,) * n, n=arr.ndim))


def _dn_chains(inverse):
    return jax.vmap(lambda q, k, v, gates_t, *rest: _dn_chunk(q, k, v, gates_t, *rest, inverse),
                    in_axes=(0, 0, 0, None, 0, 0, 0, 0, 0))


def _dn_scan_fwd(qkv, gates, *, name):
    s_len = qkv.shape[0]
    nc = s_len // DN_CHUNK
    tri, picks = _dn_tri(), _dn_gate_picks()

    def kern(*refs):
        ins, (tri_ref, pick_ref, of_ref, or_ref, sf_ref, sr_ref, state) = refs[:12], refs[12:]

        @pl.when(pl.program_id(0) == 0)
        def _():
            state[...] = jnp.zeros_like(state)

        entry = state[...]
        qkv_c = [_stack_chains(ins[i], ins[6 + i]) for i in range(3)]
        gates_t = _gates_transposed(ins[3:6], ins[9:12])
        out, new = _dn_chains(_unit_tri_inverse_impl)(*qkv_c, gates_t, pick_ref[0], pick_ref[1], pick_ref[2], entry, tri_ref[...])
        sf_ref[...] = entry[:DN_HEADS]
        sr_ref[...] = entry[DN_HEADS:]
        _unstack_chains(out, of_ref, or_ref)
        state[...] = new

    in_specs = []
    for rev in (False, True):
        in_specs += [_dn_row_spec(nc, col, rev) for col in (0, 1, 2)] + [_dn_row_spec(nc, 0, rev, LANES)] * 3
    in_specs += [_const_spec(tri), _const_spec(picks)]
    return pl.pallas_call(
        kern, name=name, grid=(nc,), in_specs=in_specs,
        out_specs=[_dn_row_spec(nc, 0, False), _dn_row_spec(nc, 0, True), _dn_state_spec(nc, False), _dn_state_spec(nc, True)],
        out_shape=[jax.ShapeDtypeStruct((s_len, TOK_WIDTH), F32)] * 2
        + [jax.ShapeDtypeStruct((nc, DN_HEADS, DN_HEAD_DIM, DN_HEAD_DIM), F32)] * 2,
        scratch_shapes=[pltpu.VMEM((2 * DN_HEADS, DN_HEAD_DIM, DN_HEAD_DIM), F32)],
        compiler_params=_params("arbitrary"),
    )(*([qkv, qkv, qkv, *gates] * 2), tri, picks)


def _dn_scan_bwd(qkv, gates, states, d_o, *, name):
    s_len = qkv.shape[0]
    nc = s_len // DN_CHUNK
    tri, picks = _dn_tri(), _dn_gate_picks()

    def kern(*refs):
        ins, tri_ref, pick_ref, outs, d_state = refs[:16], refs[16], refs[17], refs[18:30], refs[30]

        @pl.when(pl.program_id(0) == 0)
        def _():
            d_state[...] = jnp.zeros_like(d_state)

        qkv_c = [_stack_chains(ins[i], ins[8 + i]) for i in range(3)]
        gates_t = _gates_transposed(ins[3:6], ins[11:14])
        entry = jnp.concatenate([ins[6][...], ins[14][...]], axis=0)
        d_out = _stack_chains(ins[7], ins[15])
        tri_v, picks_v = tri_ref[...], pick_ref[...]
        _, vjp = jax.vjp(lambda q, k, v, g, s: _dn_chains(_unit_tri_inverse)(q, k, v, g, picks_v[0], picks_v[1], picks_v[2], s, tri_v),
                         *qkv_c, gates_t, entry)
        dq, dk, dv, d_gates_t, d_entry = vjp((d_out, d_state[...]))
        for i, val in enumerate((dq, dk, dv)):
            _unstack_chains(val, outs[i], outs[6 + i])
        for i in range(3):
            outs[3 + i][...] = d_gates_t[i, :DN_CHUNK].T
            outs[9 + i][...] = d_gates_t[i, DN_CHUNK:].T
        d_state[...] = d_entry

    in_specs, out_specs, out_shape = [], [], []
    for rev in (True, False):
        in_specs += [_dn_row_spec(nc, col, rev) for col in (0, 1, 2)] + [_dn_row_spec(nc, 0, rev, LANES)] * 3
        in_specs += [_dn_state_spec(nc, rev), _dn_row_spec(nc, 0, rev)]
        out_specs += [_dn_row_spec(nc, 0, rev)] * 3 + [_dn_row_spec(nc, 0, rev, LANES)] * 3
        out_shape += [jax.ShapeDtypeStruct((s_len, TOK_WIDTH), F32)] * 3 + [jax.ShapeDtypeStruct((s_len, LANES), F32)] * 3
    in_specs += [_const_spec(tri), _const_spec(picks)]
    res = pl.pallas_call(
        kern, name=name, grid=(nc,), in_specs=in_specs, out_specs=out_specs, out_shape=out_shape,
        scratch_shapes=[pltpu.VMEM((2 * DN_HEADS, DN_HEAD_DIM, DN_HEAD_DIM), F32)],
        compiler_params=_params("arbitrary"),
    )(*[a for d in range(2) for a in (qkv, qkv, qkv, *gates, states[d], d_o)], tri, picks)
    return (res[0:3], res[3:6]), (res[6:9], res[9:12])


def _dn_out_head(o_f, o_b, z, gain):
    o = o_f + o_b
    return o * lax.rsqrt(jnp.mean(o * o, axis=-1, keepdims=True) + EPS) * gain * _silu(z)


def _dn_out(o_fwd, o_rev, proj, gain, qkv_kv_mem, *, name):
    def body(of, ob, z, qm, g, kv):
        heads = []
        for h in range(DN_HEADS):
            sl = slice(h * DN_HEAD_DIM, (h + 1) * DN_HEAD_DIM)
            heads.append(_dn_out_head(of[:, sl], ob[:, sl], z[:, sl], g))
        return (jnp.concatenate(heads + [_mem_attn(qm, kv)], axis=1),), ()
    return _rowwise(body, [o_fwd, o_rev, _col(proj, TOK_WIDTH, 3),
                           _col(proj, MEM_WIDTH, (4 * TOK_WIDTH) // MEM_WIDTH)], [gain, qkv_kv_mem],
                    [(D_MODEL, MXU_DTYPE)], [], tm=256, name=name)[0]


def _dn_out_bwd(o_fwd, o_rev, proj, gain, kv_mem, dcat, *, name):
    def body(of, ob, z, qm, dcat, g, kv):
        dos, dzs = [], []
        dgain = jnp.zeros_like(g)
        for h in range(DN_HEADS):
            sl = slice(h * DN_HEAD_DIM, (h + 1) * DN_HEAD_DIM)
            _, vjp = jax.vjp(_dn_out_head, of[:, sl], ob[:, sl], z[:, sl], g)
            d_of, _, dz, dg = vjp(dcat[:, sl])
            dos.append(d_of)
            dzs.append(dz)
            dgain = dgain + dg
        dqm, dkv = _mem_attn_bwd(qm, kv, dcat[:, TOK_WIDTH:])
        return (jnp.concatenate(dos, axis=1), jnp.concatenate(dzs, axis=1), dqm), (dgain, dkv)
    return _rowwise(body, [o_fwd, o_rev, _col(proj, TOK_WIDTH, 3),
                           _col(proj, MEM_WIDTH, (4 * TOK_WIDTH) // MEM_WIDTH), dcat], [gain, kv_mem],
                    [(TOK_WIDTH, F32), (TOK_WIDTH, F32), (MEM_WIDTH, F32)], [gain.shape, kv_mem.shape], tm=256, name=name)


def _pad_dn_w_in(w):
    gates = w[:, 4 * TOK_WIDTH: 4 * TOK_WIDTH + N_GATES]
    zeros = jnp.zeros((w.shape[0], DN_IN_PAD - DN_IN), w.dtype)
    return jnp.concatenate([w[:, :4 * TOK_WIDTH], w[:, 4 * TOK_WIDTH + N_GATES:], gates, zeros], axis=1)


def _unpad_dn_w_in(w):
    q_mem = w[:, 4 * TOK_WIDTH: 4 * TOK_WIDTH + MEM_WIDTH]
    gates = w[:, 4 * TOK_WIDTH + MEM_WIDTH: 4 * TOK_WIDTH + MEM_WIDTH + N_GATES]
    return jnp.concatenate([w[:, :4 * TOK_WIDTH], gates, q_mem], axis=1)


def _ffn_fwd(h, w_gu, w_d, tag):
    gu = _mm(h, w_gu, out_dtype=MXU_DTYPE, name=f"ffn_gu_{tag}")
    act = _swiglu_act(gu, name=f"ffn_act_{tag}")
    return gu, act, _mm(act, w_d, name=f"ffn_down_{tag}")


def _ffn_bwd(h, gu, act, w_gu, w_d, df, tag):
    d_act = _mm(df, w_d, tb=True, out_dtype=MXU_DTYPE, name=f"ffn_dact_{tag}")
    d_wd = _mm(act, df, ta=True, name=f"ffn_dwd_{tag}")
    d_gu = _swiglu_act_bwd(gu, d_act, name=f"ffn_dgu_{tag}")
    dh = _mm(d_gu, w_gu, tb=True, name=f"ffn_dh_{tag}")
    d_wgu = _mm(h, d_gu, ta=True, out_shards=_shards_of(w_gu), name=f"ffn_dwgu_{tag}")
    return dh, d_wgu, d_wd


def _local_step(x, mem, target, p):
    g = {}
    row = lambda v: v.reshape(1, -1)
    gains = {k: [row(p[k][i]) for i in range(2)] for k in
             ("mem_norm", "norm_mix_pre", "norm_mix_post", "norm_ffn_pre", "norm_ffn_post")}
    out_gain = row(p["dn_out_norm"])
    a_cols, dt_cols = _gate_params(p["dn_a_log"]), _gate_params(p["dn_dt_bias"])

    h0 = _pre_norm(x, gains["norm_mix_pre"][0], name="pre0")
    mem_n = [_pre_norm(mem, gains["mem_norm"][i], name=f"mem_norm{i}") for i in range(2)]
    kv_mem = [_mm(mem_n[i], p["mem_w_kv"][i], name=f"mem_kv{i}") for i in range(2)]
    qkvm = _mm(h0, p["att_w_in"], out_dtype=MXU_DTYPE, name="att_in")
    bias = [_bias_tiles(p["rel_bias"], gi) for gi in range(3)]
    att = [_att_fwd(qkvm, bias[gi], gi, name=f"att_fwd{gi}") for gi in range(3)]
    cat0, lse_tot = _att_combine([a[0] for a in att], [a[1] for a in att], qkvm, kv_mem[0], name="att_combine")
    mo0 = _mm(cat0, p["att_w_out"], name="att_out")
    x1, h1 = _post_pre(x, mo0, gains["norm_mix_post"][0], gains["norm_ffn_pre"][0], name="post_mix0")
    w_gu0, w_d0 = p["ffn_weights"](0, h1)
    gu0, act0, f0 = _ffn_fwd(h1, w_gu0, w_d0, 0)
    x2, h2 = _post_pre(x1, f0, gains["norm_ffn_post"][0], gains["norm_mix_pre"][1], name="post_ffn0")

    dn_w_in, dn_w_out = p["dn_weights"](h2)
    proj = _mm(h2, dn_w_in, name="dn_in")
    qkv = _dn_conv_fwd(proj, p["dn_conv"], name="dn_conv")
    gates = _dn_gates_fwd(proj, a_cols, dt_cols, name="dn_gates")
    o_fwd, o_rev, st_fwd, st_rev = _dn_scan_fwd(qkv, gates, name="dn_scan")
    cat1 = _dn_out(o_fwd, o_rev, proj, out_gain, kv_mem[1], name="dn_outnorm")
    mo1 = _mm(cat1, dn_w_out, name="dn_out")
    x3, h3 = _post_pre(x2, mo1, gains["norm_mix_post"][1], gains["norm_ffn_pre"][1], name="post_mix1")
    w_gu1, w_d1 = p["ffn_weights"](1, h3)
    gu1, act1, f1 = _ffn_fwd(h3, w_gu1, w_d1, 1)

    dx3, df1, dg_ffn_post1, loss_cols = _final_loss_bwd(x3, f1, gains["norm_ffn_post"][1], target, name="loss_bwd")
    dh3, d_wgu1, d_wd1 = _ffn_bwd(h3, gu1, act1, w_gu1, w_d1, df1, 1)
    sent = p["grads_ready"]("ffn1", {("ffn_w_gate_up", 1): d_wgu1, ("ffn_w_down", 1): d_wd1})
    dx2, dmo1, dg_mix_post1, dg_ffn_pre1 = _post_pre_bwd(x2, mo1, gains["norm_mix_post"][1] + sent, gains["norm_ffn_pre"][1],
                                                         dx3, dh3, name="post_mix1_bwd")
    dcat1 = _mm(dmo1, dn_w_out, tb=True, name="dn_out_dx")
    g["dn_w_out"] = _mm(cat1, dmo1, ta=True, name="dn_out_dw")
    d_o, dz, dqm1, d_out_gain, dkv1 = _dn_out_bwd(o_fwd, o_rev, proj, out_gain, kv_mem[1], dcat1, name="dn_outnorm_bwd")
    (d_f, dg_f), (d_r, dg_r) = _dn_scan_bwd(qkv, gates, (st_fwd, st_rev), d_o, name="dn_scan_bwd")
    d_gate_cols, d_a_cols, d_dt_cols = _dn_gates_bwd(proj, a_cols, dt_cols, (dg_f, dg_r), name="dn_gates_bwd")
    d_pre, d_conv = zip(*[_dn_conv_bwd(proj, p["dn_conv"], d_f[which], d_r[which], which, name=f"dn_conv_bwd{which}")
                          for which in range(3)])
    dproj = jnp.concatenate(list(d_pre) + [dz, dqm1, d_gate_cols], axis=1).astype(MXU_DTYPE)
    dh2 = _mm(dproj, dn_w_in, tb=True, name="dn_in_dx")
    g["dn_w_in"] = _mm(h2, dproj, ta=True, name="dn_in_dw")
    g["dn_conv"] = jnp.concatenate(d_conv, axis=1)
    g["dn_a_log"] = _gate_params_bwd(d_a_cols)
    g["dn_dt_bias"] = _gate_params_bwd(d_dt_cols)
    g["dn_out_norm"] = d_out_gain

    sent = p["grads_ready"]("dn", {("dn_w_in", 0): g["dn_w_in"], ("dn_w_out", 0): g["dn_w_out"]})
    dx1, df0, dg_ffn_post0, dg_mix_pre1 = _post_pre_bwd(x1, f0, gains["norm_ffn_post"][0] + sent, gains["norm_mix_pre"][1],
                                                        dx2, dh2, name="post_ffn0_bwd")
    dh1, d_wgu0, d_wd0 = _ffn_bwd(h1, gu0, act0, w_gu0, w_d0, df0, 0)
    sent = p["grads_ready"]("ffn0", {("ffn_w_gate_up", 0): d_wgu0, ("ffn_w_down", 0): d_wd0})
    dx0, dmo0, dg_mix_post0, dg_ffn_pre0 = _post_pre_bwd(x, mo0, gains["norm_mix_post"][0] + sent, gains["norm_ffn_pre"][0],
                                                         dx1, dh1, name="post_mix0_bwd")
    dcat0 = _mm(dmo0, p["att_w_out"], tb=True, name="att_out_dx")
    g["att_w_out"] = _mm(cat0, dmo0, ta=True, name="att_out_dw")
    delta, dqm0, dkv0 = _att_bwd_prep(cat0, dcat0, qkvm, kv_mem[0], name="att_bwd_prep")
    att_b = [_att_bwd(qkvm, bias[gi], lse_tot, delta, dcat0, gi, name=f"att_bwd{gi}") for gi in range(3)]
    dqkvm = jnp.concatenate([a[w] for w in range(3) for a in att_b] + [dqm0], axis=1).astype(MXU_DTYPE)
    g["rel_bias"] = sum(_bias_tiles_bwd(p["rel_bias"], att_b[gi][3], gi) for gi in range(3))
    dh0 = _mm(dqkvm, p["att_w_in"], tb=True, name="att_in_dx")
    g["att_w_in"] = _mm(h0, dqkvm, ta=True, out_shards=_shards_of(p["att_w_in"]), name="att_in_dw")
    grad_x, dg_mix_pre0 = _pre_norm_bwd(x, gains["norm_mix_pre"][0], dh0, dx0, name="pre0_bwd")

    d_mem_kv, d_mem_norm = [], []
    for i, dkv in enumerate((dkv0, dkv1)):
        d_mem_kv.append(_mm(mem_n[i], dkv, ta=True, name=f"mem_kv_dw{i}"))
        d_mem_n = _mm(dkv, p["mem_w_kv"][i], tb=True, name=f"mem_kv_dx{i}")
        d_mem_norm.append(_gain_bwd(mem, gains["mem_norm"][i], d_mem_n, name=f"mem_norm_bwd{i}"))
    g["mem_w_kv"] = d_mem_kv
    g["mem_norm"] = jnp.concatenate(d_mem_norm, axis=0)
    g["norm_mix_pre"] = jnp.concatenate([dg_mix_pre0, dg_mix_pre1], axis=0)
    g["norm_mix_post"] = jnp.concatenate([dg_mix_post0, dg_mix_post1], axis=0)
    g["norm_ffn_pre"] = jnp.concatenate([dg_ffn_pre0, dg_ffn_pre1], axis=0)
    g["norm_ffn_post"] = jnp.concatenate([dg_ffn_post0, dg_ffn_post1], axis=0)
    g["ffn_w_gate_up"] = [d_wgu0, d_wgu1]
    g["ffn_w_down"] = [d_wd0, d_wd1]
    return loss_cols, grad_x, g


N_CHIPS = 4
N_DEV = 8
MESH = pl.DeviceIdType.MESH
BIG = (("att_w_in", (1, 1024, 640), 2), ("att_w_out", (1, 256, 1024), 1), ("dn_w_in", (1, 1024, 838), 2),
       ("dn_w_out", (1, 256, 1024), 1), ("mem_w_kv", (2, 256, 512), 1), ("ffn_w_gate_up", (2, 1024, 1408), 2),
       ("ffn_w_down", (2, 704, 1024), 1))


def _mesh_pos():
    return lax.axis_index("x"), lax.axis_index("y"), lax.axis_index("c")


def _other_chips(x, y):
    return [(1 - x, y), (x, 1 - y), (1 - x, 1 - y)]


ANY = pl.BlockSpec(memory_space=pl.ANY)


def _all_reduce_small(v, *, name):
    rows, cols = v.shape
    flips = [(dx, dy, dc) for dx in (0, 1) for dy in (0, 1) for dc in (0, 1)][1:]

    def body(v_ref, o_ref, buf, send_sems, recv_sems):
        x, y, c = _mesh_pos()

        def peer(f):
            return tuple(1 - p if fl else p for p, fl in zip((x, y, c), f))

        def index(p):
            return 4 * p[0] + 2 * p[1] + p[2]

        buf[index((x, y, c))] = v_ref[...]
        sends = []
        for k, f in enumerate(flips):
            cp = pltpu.make_async_remote_copy(src_ref=v_ref, dst_ref=buf.at[index((x, y, c))], send_sem=send_sems.at[k],
                                              recv_sem=recv_sems.at[k], device_id=peer(f), device_id_type=MESH)
            cp.start()
            sends.append(cp)
        for k, f in enumerate(flips):
            pltpu.make_async_remote_copy(src_ref=v_ref, dst_ref=buf.at[index(peer(f))], send_sem=send_sems.at[k],
                                         recv_sem=recv_sems.at[k], device_id=peer(f), device_id_type=MESH).wait_recv()
        for cp in sends:
            cp.wait_send()
        acc = buf[0]
        for d in range(1, N_DEV):
            acc = acc + buf[d]
        o_ref[...] = acc

    vmem = pl.BlockSpec(memory_space=pltpu.VMEM)
    return pl.pallas_call(
        body, name=name, in_specs=[vmem], out_specs=vmem, out_shape=jax.ShapeDtypeStruct((rows, cols), F32),
        scratch_shapes=[pltpu.VMEM((N_DEV, rows, cols), F32), pltpu.SemaphoreType.DMA((N_DEV - 1,)),
                        pltpu.SemaphoreType.DMA((N_DEV - 1,))],
    )(v)


def _adamw(w, g, m, v, *, name):
    def body(w, g, m, v):
        m = ADAM_B1 * m + (1.0 - ADAM_B1) * g
        v = ADAM_B2 * v + (1.0 - ADAM_B2) * (g * g)
        m_hat = m / (1.0 - ADAM_B1 ** ADAM_STEP)
        v_hat = v / (1.0 - ADAM_B2 ** ADAM_STEP)
        delta = -ADAM_LR * (m_hat / (jnp.sqrt(v_hat) + ADAM_EPS) + ADAM_WD * w)
        return (delta, m, v), ()
    rows, cols = w.shape
    return _rowwise(body, [w, g, m, v], [], [(cols, F32)] * 3, [], tm=_tile(rows, 256, SUBLANES), name=name)


def _pack_small(arrs, rows):
    flat = jnp.concatenate([a.reshape(-1) for a in arrs])
    return jnp.pad(flat, (0, rows * LANES - flat.shape[0])).reshape(rows, LANES)


def _unpack_small(packed, shapes):
    flat = packed.reshape(-1)
    out, off = [], 0
    for s in shapes:
        size = math.prod(s)
        out.append(flat[off: off + size].reshape(s))
        off += size
    return out


def _small_rows(shapes):
    return -(-sum(math.prod(s) for s in shapes) // (SUBLANES * LANES)) * SUBLANES


def _sem_pairs(n):
    return [pltpu.SemaphoreType.DMA((n,)), pltpu.SemaphoreType.DMA((n,))]


def _gather_blocks(blocks, *, name):
    n = len(blocks)

    def body(*refs):
        x_refs, out_refs, (send_sems, recv_sems) = refs[:n], refs[n: 2 * n], refs[2 * n:]
        x, y, c = _mesh_pos()
        sibling = (x, y, 1 - c)
        chips = _other_chips(x, y)

        def copy(k, src, dst, to):
            return pltpu.make_async_remote_copy(src_ref=src, dst_ref=dst, send_sem=send_sems.at[k],
                                                recv_sem=recv_sems.at[k], device_id=to, device_id_type=MESH)

        def part(b, chip, h):
            half = blocks[b].shape[0] // 2
            return out_refs[b].at[2 * chip[0] + chip[1], pl.ds(h * half, half), :]

        def my_half(b):
            half = blocks[b].shape[0] // 2
            return x_refs[b].at[pl.ds(c * half, half), :]

        first = [copy(6 * b + j, my_half(b), part(b, (x, y), c), (*chip, c)) for b in range(n) for j, chip in enumerate(chips)]
        for cp in first:
            cp.start()
        passed = []
        for b in range(n):
            for j, chip in enumerate(chips):
                copy(6 * b + j, my_half(b), part(b, chip, c), (*chip, c)).wait_recv()
                cp = copy(6 * b + 3 + j, part(b, chip, c), part(b, chip, c), sibling)
                cp.start()
                passed.append(cp)
        for b in range(n):
            for j, chip in enumerate(chips):
                copy(6 * b + 3 + j, part(b, chip, 1 - c), part(b, chip, 1 - c), sibling).wait_recv()
        for cp in first + passed:
            cp.wait_send()

    return pl.pallas_call(
        body, name=name, in_specs=[ANY] * n, out_specs=[ANY] * n,
        out_shape=[jax.ShapeDtypeStruct((N_CHIPS, *a.shape), a.dtype) for a in blocks],
        scratch_shapes=_sem_pairs(6 * n),
    )(*blocks)


HBM = pl.BlockSpec(memory_space=pltpu.HBM)
SEM = pl.BlockSpec(memory_space=pltpu.SEMAPHORE)
DATAFLOW = pltpu.SideEffectType.DATAFLOW_SIDE_EFFECTING


def _gather_start(blocks, *, name):
    n = len(blocks)
    lands = [lax.empty((N_CHIPS, *a.shape), a.dtype) for a in blocks]

    def body(*refs):
        x_refs, land_refs, send_sems, recv_sems, token = refs[:n], refs[n: 2 * n], refs[2 * n], refs[2 * n + 1], refs[-1]
        x, y, c = _mesh_pos()
        for b in range(n):
            for j, chip in enumerate(_other_chips(x, y)):
                pltpu.make_async_remote_copy(src_ref=x_refs[b], dst_ref=land_refs[b].at[2 * x + y], send_sem=send_sems.at[3 * b + j],
                                             recv_sem=recv_sems.at[3 * b + j], device_id=(*chip, c), device_id_type=MESH).start()
        token[...] = jnp.zeros_like(token)

    operands = [pltpu.with_memory_space_constraint(a, pltpu.HBM) for a in blocks + lands]
    res = pl.pallas_call(
        body, name=name, in_specs=[HBM] * (2 * n),
        out_shape=(pltpu.SemaphoreType.DMA((3 * n,)), pltpu.SemaphoreType.DMA((3 * n,)),
                   *[pltpu.HBM(a.shape, a.dtype) for a in operands], jax.ShapeDtypeStruct((SUBLANES, LANES), F32)),
        out_specs=(SEM, SEM, *[HBM] * (2 * n), pl.BlockSpec(memory_space=pltpu.VMEM)),
        input_output_aliases={i: 2 + i for i in range(2 * n)},
        compiler_params=pltpu.CompilerParams(has_side_effects=DATAFLOW),
    )(*operands)
    return res[0], res[1], list(res[2: 2 + n]), list(res[2 + n: 2 + 2 * n]), res[-1]


def _gather_wait(started, after, *, name):
    send_sems, recv_sems, blocks, lands, _ = started
    n = len(blocks)

    def body(*refs):
        x_refs, land_refs, send_sems, recv_sems = refs[:n], refs[n: 2 * n], refs[2 * n], refs[2 * n + 1]
        x, y, c = _mesh_pos()
        for b in range(n):
            for j, chip in enumerate(_other_chips(x, y)):
                cp = pltpu.make_async_remote_copy(src_ref=x_refs[b], dst_ref=land_refs[b].at[2 * chip[0] + chip[1]],
                                                  send_sem=send_sems.at[3 * b + j], recv_sem=recv_sems.at[3 * b + j],
                                                  device_id=(*chip, c), device_id_type=MESH)
                cp.wait_send()
                cp.wait_recv()

    res = pl.pallas_call(
        body, name=name, in_specs=(*[HBM] * (2 * n), SEM, SEM, ANY),
        out_shape=tuple(pltpu.HBM(a.shape, a.dtype) for a in blocks + lands), out_specs=tuple([HBM] * (2 * n)),
        input_output_aliases={i: i for i in range(2 * n)},
        compiler_params=pltpu.CompilerParams(has_side_effects=DATAFLOW),
    )(*blocks, *lands, send_sems, recv_sems, after)
    return list(res[n:])


def _swap_halves(blocks, own_rows, *, name):
    n = len(blocks)

    def body(*refs):
        in_refs, out_refs, (send_sems, recv_sems) = refs[:n], refs[n: 2 * n], refs[2 * n:]
        x, y, c = _mesh_pos()
        copies = [pltpu.make_async_remote_copy(src_ref=own_rows(in_refs[b], c), dst_ref=out_refs[b], send_sem=send_sems.at[b],
                                               recv_sem=recv_sems.at[b], device_id=(x, y, 1 - c), device_id_type=MESH)
                  for b in range(n)]
        for cp in copies:
            cp.start()
        for cp in copies:
            cp.wait()

    def sent_shape(a):
        return jax.eval_shape(lambda r: own_rows(r, 0), a)

    return pl.pallas_call(
        body, name=name, in_specs=[ANY] * n, out_specs=[ANY] * n,
        out_shape=[jax.ShapeDtypeStruct(sent_shape(a).shape, a.dtype) for a in blocks],
        scratch_shapes=_sem_pairs(n),
    )(*blocks)


def _other_half_rows(ref, c):
    half = ref.shape[1] // 2
    return ref[:, (1 - c) * half: (2 - c) * half, :] if isinstance(c, int) else ref.at[:, pl.ds((1 - c) * half, half), :]


def _whole(ref, c):
    return ref


def _add_own_half_block(block, received, core, *, name):
    n, rows, cols = block.shape
    half = rows // 2
    tm = _tile(half, 512, 2 * SUBLANES)
    nb = half // tm

    def kern(c_ref, a_ref, b_ref, o_ref):
        o_ref[...] = (a_ref[...] + b_ref[...]).astype(o_ref.dtype)

    return pl.pallas_call(
        kern, name=name,
        grid_spec=pltpu.PrefetchScalarGridSpec(
            num_scalar_prefetch=1, grid=(n, nb),
            in_specs=[pl.BlockSpec((None, tm, cols), lambda s, i, c: (s, c[0] * nb + i, 0)),
                      pl.BlockSpec((None, tm, cols), lambda s, i, c: (s, i, 0))],
            out_specs=pl.BlockSpec((None, tm, cols), lambda s, i, c: (s, i, 0))),
        out_shape=jax.ShapeDtypeStruct((n, half, cols), LINK_DTYPE),
        compiler_params=_params("parallel", "parallel"),
    )(core, block, received)


def _sum_chips_block(parts, *, name):
    n, half, cols = parts.shape
    tm = _tile(half, 512, 2 * SUBLANES)

    def kern(p_ref, o_ref):
        acc = p_ref[0].astype(F32)
        for s in range(1, n):
            acc = acc + p_ref[s].astype(F32)
        o_ref[...] = acc

    return pl.pallas_call(
        kern, name=name, grid=(half // tm,),
        in_specs=[pl.BlockSpec((n, tm, cols), lambda i: (0, i, 0))],
        out_specs=pl.BlockSpec((tm, cols), lambda i: (i, 0)),
        out_shape=jax.ShapeDtypeStruct((half, cols), F32),
        compiler_params=_params("parallel"),
    )(parts)


def _scatter_start(sums, *, name):
    n = len(sums)
    lands = [lax.empty(a.shape, a.dtype) for a in sums]

    def body(*refs):
        s_refs, land_refs, send_sems, recv_sems, token = refs[:n], refs[n: 2 * n], refs[2 * n], refs[2 * n + 1], refs[-1]
        x, y, c = _mesh_pos()
        for b in range(n):
            for j, chip in enumerate(_other_chips(x, y)):
                pltpu.make_async_remote_copy(src_ref=s_refs[b].at[2 * chip[0] + chip[1]], dst_ref=land_refs[b].at[2 * x + y],
                                             send_sem=send_sems.at[3 * b + j], recv_sem=recv_sems.at[3 * b + j],
                                             device_id=(*chip, c), device_id_type=MESH).start()
        token[...] = jnp.zeros_like(token)

    operands = [pltpu.with_memory_space_constraint(a, pltpu.HBM) for a in sums + lands]
    res = pl.pallas_call(
        body, name=name, in_specs=[HBM] * (2 * n),
        out_shape=(pltpu.SemaphoreType.DMA((3 * n,)), pltpu.SemaphoreType.DMA((3 * n,)),
                   *[pltpu.HBM(a.shape, a.dtype) for a in operands], jax.ShapeDtypeStruct((SUBLANES, LANES), F32)),
        out_specs=(SEM, SEM, *[HBM] * (2 * n), pl.BlockSpec(memory_space=pltpu.VMEM)),
        input_output_aliases={i: 2 + i for i in range(2 * n)},
        compiler_params=pltpu.CompilerParams(has_side_effects=DATAFLOW),
    )(*operands)
    return res[0], res[1], list(res[2: 2 + n]), list(res[2 + n: 2 + 2 * n]), res[-1]


def _scatter_wait(started, after, *, name):
    send_sems, recv_sems, sums, lands, _ = started
    n = len(sums)

    def body(*refs):
        s_refs, land_refs, send_sems, recv_sems = refs[:n], refs[n: 2 * n], refs[2 * n], refs[2 * n + 1]
        x, y, c = _mesh_pos()
        for b in range(n):
            for j, chip in enumerate(_other_chips(x, y)):
                cp = pltpu.make_async_remote_copy(src_ref=s_refs[b].at[2 * x + y], dst_ref=land_refs[b].at[2 * chip[0] + chip[1]],
                                                  send_sem=send_sems.at[3 * b + j], recv_sem=recv_sems.at[3 * b + j],
                                                  device_id=(*chip, c), device_id_type=MESH)
                cp.wait_send()
                cp.wait_recv()

    res = pl.pallas_call(
        body, name=name, in_specs=(*[HBM] * (2 * n), SEM, SEM, ANY),
        out_shape=tuple(pltpu.HBM(a.shape, a.dtype) for a in sums + lands), out_specs=tuple([HBM] * (2 * n)),
        input_output_aliases={i: i for i in range(2 * n)},
        compiler_params=pltpu.CompilerParams(has_side_effects=DATAFLOW),
    )(*sums, *lands, send_sems, recv_sems, after)
    return list(res[:n]), list(res[n:])


def _reduce_begin(blocks, names, key):
    core = lax.axis_index("c").astype(jnp.int32).reshape(1)
    received = _swap_halves(blocks, _other_half_rows, name=f"rs_swap_{key}")
    sums = [_add_own_half_block(b, r, core, name=f"rs_add_{nm}") for b, r, nm in zip(blocks, received, names, strict=True)]
    return _scatter_start(sums, name=f"rs_scatter_start_{key}")


def _reduce_finish(begun, names, after):
    x, y, c = _mesh_pos()
    chip = 2 * x + y
    mine = {}
    for key, started in begun.items():
        sums, lands = _scatter_wait(started, after, name=f"rs_scatter_wait_{key}")
        parts = [lax.dynamic_update_slice(p, lax.dynamic_slice_in_dim(s, chip, 1, axis=0), (chip, 0, 0))
                 for p, s in zip(lands, sums, strict=True)]
        mine[key] = [_sum_chips_block(p, name=f"rs_sum_{nm}") for p, nm in zip(parts, names[key], strict=True)]
    flat = [a for key in begun for a in mine[key]]
    other = iter(_swap_halves(flat, _whole, name="rs_join"))
    return {key: [jnp.concatenate([jnp.where(c == 0, a, b), jnp.where(c == 0, b, a)], axis=0)
                  for a, b in ((a, next(other)) for a in mine[key])] for key in begun}


WEIGHTS = ("rel_bias", "att_w_in", "att_w_out", "dn_w_in", "dn_conv", "dn_a_log", "dn_dt_bias", "dn_out_norm", "dn_w_out",
           "mem_norm", "mem_w_kv", "norm_mix_pre", "norm_mix_post", "norm_ffn_pre", "norm_ffn_post", "ffn_w_gate_up",
           "ffn_w_down")
BIG_NAMES = tuple(n for n, _, _ in BIG)
SMALL_NAMES = tuple(n for n in WEIGHTS if n not in BIG_NAMES)
CONV_COLS = 3 * TOK_WIDTH
CONV_SHARD = CONV_COLS // N_CHIPS
BLOCKS = tuple((n, layer) for n, shape, _ in BIG for layer in range(shape[0]))
COLUMN_SHARDED = {n: axis == 2 for n, _, axis in BIG}


def kernel(x, mem, rel_bias, att_w_in, att_w_out, dn_w_in, dn_conv, dn_a_log, dn_dt_bias, dn_out_norm, dn_w_out, mem_norm, mem_w_kv, norm_mix_pre, norm_mix_post, norm_ffn_pre, norm_ffn_post, ffn_w_gate_up, ffn_w_down, loss_target, m_rel_bias, m_att_w_in, m_att_w_out, m_dn_w_in, m_dn_conv, m_dn_a_log, m_dn_dt_bias, m_dn_out_norm, m_dn_w_out, m_mem_norm, m_mem_w_kv, m_norm_mix_pre, m_norm_mix_post, m_norm_ffn_pre, m_norm_ffn_post, m_ffn_w_gate_up, m_ffn_w_down, v_rel_bias, v_att_w_in, v_att_w_out, v_dn_w_in, v_dn_conv, v_dn_a_log, v_dn_dt_bias, v_dn_out_norm, v_dn_w_out, v_mem_norm, v_mem_w_kv, v_norm_mix_pre, v_norm_mix_post, v_norm_ffn_pre, v_norm_ffn_post, v_ffn_w_gate_up, v_ffn_w_down):
    w = dict(zip(WEIGHTS, (rel_bias, att_w_in, att_w_out, dn_w_in, dn_conv, dn_a_log, dn_dt_bias, dn_out_norm, dn_w_out,
                           mem_norm, mem_w_kv, norm_mix_pre, norm_mix_post, norm_ffn_pre, norm_ffn_post, ffn_w_gate_up,
                           ffn_w_down)))
    m = dict(zip(WEIGHTS, (m_rel_bias, m_att_w_in, m_att_w_out, m_dn_w_in, m_dn_conv, m_dn_a_log, m_dn_dt_bias,
                           m_dn_out_norm, m_dn_w_out, m_mem_norm, m_mem_w_kv, m_norm_mix_pre, m_norm_mix_post,
                           m_norm_ffn_pre, m_norm_ffn_post, m_ffn_w_gate_up, m_ffn_w_down)))
    v = dict(zip(WEIGHTS, (v_rel_bias, v_att_w_in, v_att_w_out, v_dn_w_in, v_dn_conv, v_dn_a_log, v_dn_dt_bias,
                           v_dn_out_norm, v_dn_w_out, v_mem_norm, v_mem_w_kv, v_norm_mix_pre, v_norm_mix_post,
                           v_norm_ffn_pre, v_norm_ffn_post, v_ffn_w_gate_up, v_ffn_w_down)))
    cx, cy, cc = _mesh_pos()
    chip = 2 * cx + cy

    local = dict(zip(BLOCKS, lax.optimization_barrier([w[n][layer].astype(MXU_DTYPE) for n, layer in BLOCKS]), strict=True))

    def usable(block, got):
        got = lax.dynamic_update_slice(got, local[block][None], (chip, 0, 0))
        return got if COLUMN_SHARDED[block[0]] else got.reshape(-1, got.shape[-1])

    late = {"ffn0": [("ffn_w_gate_up", 0), ("ffn_w_down", 0)], "dn": [("dn_w_in", 0), ("dn_w_out", 0)],
            "ffn1": [("ffn_w_gate_up", 1), ("ffn_w_down", 1)]}
    first = [b for b in BLOCKS if all(b not in blks for blks in late.values())]
    first_got = _gather_blocks([local[b] for b in first], name="gather_weights")
    late_local, _ = lax.optimization_barrier(({k: [local[b] for b in blks] for k, blks in late.items()}, first_got[0]))
    started = {k: _gather_start(late_local[k], name=f"gather_start_{k}") for k in late}
    started_token = sum(s[4][0, 0] for s in started.values())

    def late_weights(key, after):
        lands = _gather_wait(started[key], after, name=f"gather_wait_{key}")
        return [usable(b, got) for b, got in zip(late[key], lands, strict=True)]

    def dn_weights(after):
        w_in, w_out = late_weights("dn", after)
        return _pad_dn_w_in(jnp.concatenate([w_in[s] for s in range(N_CHIPS)], axis=1)), w_out

    full = {}
    for b, got in zip(first, first_got, strict=True):
        full.setdefault(b[0], []).append(usable(b, got))
    conv_rows = _small_rows([(DN_CONV, CONV_COLS)])
    conv_mine = jnp.where(cc == 0, 1.0, 0.0) * w["dn_conv"][0]
    conv_placed = lax.dynamic_update_slice(jnp.zeros((DN_CONV, CONV_COLS), F32), conv_mine, (0, chip * CONV_SHARD))
    conv_full = _unpack_small(_all_reduce_small(_pack_small([conv_placed], conv_rows), name="gather_conv"),
                              [(DN_CONV, CONV_COLS)])[0]
    p = {
        "rel_bias": w["rel_bias"], "att_w_in": full["att_w_in"][0], "att_w_out": full["att_w_out"][0],
        "dn_conv": conv_full, "dn_a_log": w["dn_a_log"][0], "dn_dt_bias": w["dn_dt_bias"][0],
        "dn_out_norm": w["dn_out_norm"][0], "mem_norm": w["mem_norm"], "mem_w_kv": full["mem_w_kv"],
        "norm_mix_pre": w["norm_mix_pre"] + started_token,
        "norm_mix_post": w["norm_mix_post"], "norm_ffn_pre": w["norm_ffn_pre"], "norm_ffn_post": w["norm_ffn_post"],
        "ffn_weights": lambda layer, after: late_weights(f"ffn{layer}", after), "dn_weights": dn_weights,
    }

    def chip_blocks(n, a):
        if n == "dn_w_in":
            a = _unpad_dn_w_in(a)
        if a.ndim == 3:
            return a
        if COLUMN_SHARDED[n]:
            return a.reshape(a.shape[0], N_CHIPS, -1).transpose(1, 0, 2)
        return a.reshape(N_CHIPS, -1, a.shape[-1])

    begun, begun_blocks = {}, {}

    def grads_ready(key, layer_grads):
        begun_blocks[key] = list(layer_grads)
        begun[key] = _reduce_begin([chip_blocks(n, a) for (n, _), a in layer_grads.items()],
                                   [f"{n}{layer}" for n, layer in layer_grads], key)
        return begun[key][4][0, 0]

    p["grads_ready"] = grads_ready
    loss_cols, grad_x, g = _local_step(x[0], mem[0], loss_target[0], p)
    loss = lax.psum(jnp.sum(loss_cols), ("x", "y", "c"))
    grads_ready("att", {("att_w_in", 0): g["att_w_in"], ("att_w_out", 0): g["att_w_out"],
                        ("mem_w_kv", 0): g["mem_w_kv"][0], ("mem_w_kv", 1): g["mem_w_kv"][1]})
    finished = _reduce_finish(begun, {k: [f"{n}{layer}" for n, layer in blks] for k, blks in begun_blocks.items()}, grad_x)
    reduced = {b: r for k in begun for b, r in zip(begun_blocks[k], finished[k], strict=True)}
    grads = {n: jnp.concatenate([reduced[b] for b in BLOCKS if b[0] == n], axis=0).reshape(shape) for n, shape, _ in BIG}
    small_full_shapes = [(DN_CONV, CONV_COLS) if n == "dn_conv" else w[n].shape for n in SMALL_NAMES]
    small_sum = _all_reduce_small(_pack_small([g[n] for n in SMALL_NAMES], _small_rows(small_full_shapes)), name="reduce_small")
    for n, s in zip(SMALL_NAMES, _unpack_small(small_sum, small_full_shapes)):
        grads[n] = lax.dynamic_slice(s, (0, chip * CONV_SHARD), (DN_CONV, CONV_SHARD))[None] if n == "dn_conv" else s

    delta, new_m, new_v = {}, {}, {}
    for n in BIG_NAMES:
        shape = w[n].shape
        two_d = lambda a: a.reshape(-1, shape[-1])
        res = _adamw(two_d(w[n]), two_d(grads[n]), two_d(m[n]), two_d(v[n]), name=f"adamw_{n}")
        delta[n], new_m[n], new_v[n] = (r.reshape(shape) for r in res)
    small_shapes = [w[n].shape for n in SMALL_NAMES]
    rows = _small_rows(small_shapes)
    res = _adamw(*[_pack_small([d[n] for n in SMALL_NAMES], rows) for d in (w, grads, m, v)], name="adamw_small")
    for d, r in zip((delta, new_m, new_v), res):
        for n, a in zip(SMALL_NAMES, _unpack_small(r, small_shapes)):
            d[n] = a
    return (loss, grad_x[None], *[grads[n] for n in WEIGHTS], *[delta[n] for n in WEIGHTS],
            *[new_m[n] for n in WEIGHTS], *[new_v[n] for n in WEIGHTS])
```

```python
import functools
import math

import numpy as np
import jax
import jax.numpy as jnp
from jax import lax
from jax.experimental import pallas as pl
from jax.experimental.pallas import tpu as pltpu

F32 = jnp.float32
MXU_DTYPE = jnp.bfloat16
LINK_DTYPE = jnp.bfloat16
HI = lax.Precision.HIGHEST

EPS = 1e-6
NEG_INF = -1e30
LANES = 128
SUBLANES = 8
VMEM_LIMIT = 56 * 1024 * 1024

D_MODEL = 1024
TOK_WIDTH = 768
MEM_WIDTH = 256
MEM_LEN = 256
ATT_HEAD_DIM = 64
DILATIONS = (1, 4, 16)
HALF = 64
ATT_BQ = 128
ATT_W = ATT_BQ + 2 * HALF
REL_BUCKETS = 32
REL_MAX_DIST = 1024
DN_HEADS = 6
DN_HEAD_DIM = 128
DN_CONV = 5
DN_CHUNK = 128
D_FF = 2816
ATT_IN = 2560
DN_IN = 3352
DN_IN_PAD = 3456
N_GATES = 4 * DN_HEADS

ADAM_LR = 0.001
ADAM_B1 = 0.9
ADAM_B2 = 0.999
ADAM_EPS = 1e-08
ADAM_WD = 0.01
ADAM_STEP = 10


def _tile(n, target, align):
    if n <= target:
        return n
    t = (target // align) * align
    while t >= align:
        if n % t == 0:
            return t
        t -= align
    raise ValueError(f"no tile for {n} (target {target}, align {align})")


def _params(*sem):
    return pltpu.CompilerParams(dimension_semantics=sem, vmem_limit_bytes=VMEM_LIMIT)


def _mm(a, b, *, name, ta=False, tb=False, out_shards=None, tm=1408, tn=1408, tk=1408, out_dtype=F32):
    if ta:
        K, M = a.shape
    else:
        M, K = a.shape
    sharded_b = b.ndim == 3
    if sharded_b:
        n_sh, b_rows, b_cols = b.shape
        N, K2 = (b_rows, n_sh * b_cols) if tb else (n_sh * b_cols, b_rows)
    else:
        N, K2 = b.shape if tb else b.shape[::-1]
    assert K == K2, (a.shape, b.shape, ta, tb)
    tm = _tile(M, tm, LANES if ta else SUBLANES)
    tn = N // out_shards if out_shards else (b_cols if sharded_b and not tb else _tile(N, tn, LANES))
    tk = b_cols if sharded_b and tb else _tile(K, tk, LANES)
    nk = K // tk
    a_spec = pl.BlockSpec((tk, tm), lambda i, j, k: (k, i)) if ta else pl.BlockSpec((tm, tk), lambda i, j, k: (i, k))
    if sharded_b:
        b_spec = (pl.BlockSpec((None, tn, tk), lambda i, j, k: (k, j, 0)) if tb
                  else pl.BlockSpec((None, tk, tn), lambda i, j, k: (j, k, 0)))
    else:
        b_spec = pl.BlockSpec((tn, tk), lambda i, j, k: (j, k)) if tb else pl.BlockSpec((tk, tn), lambda i, j, k: (k, j))
    if out_shards:
        out_spec = pl.BlockSpec((None, tm, tn), lambda i, j, k: (j, i, 0))
        out_shape = jax.ShapeDtypeStruct((out_shards, M, tn), out_dtype)
    else:
        out_spec = pl.BlockSpec((tm, tn), lambda i, j, k: (i, j))
        out_shape = jax.ShapeDtypeStruct((M, N), out_dtype)
    dims = (((0 if ta else 1,), (1 if tb else 0,)), ((), ()))

    def kern(a_ref, b_ref, o_ref, acc_ref):
        k = pl.program_id(2)

        @pl.when(k == 0)
        def _():
            acc_ref[...] = jnp.zeros_like(acc_ref)

        acc_ref[...] += lax.dot_general(a_ref[...].astype(MXU_DTYPE), b_ref[...].astype(MXU_DTYPE), dims,
                                        preferred_element_type=F32)

        @pl.when(k == nk - 1)
        def _():
            o_ref[...] = acc_ref[...].astype(o_ref.dtype)

    return pl.pallas_call(
        kern, name=name, grid=(M // tm, N // tn, nk), in_specs=[a_spec, b_spec],
        out_specs=out_spec, out_shape=out_shape,
        scratch_shapes=[pltpu.VMEM((tm, tn), F32)],
        compiler_params=_params("parallel", "parallel", "arbitrary"),
    )(a, b)


def _shards_of(w):
    return w.shape[0] if w.ndim == 3 else None


def _col(arr, width, blk):
    return (arr, width, blk)


def _rowwise(body, rows, consts, out_rows, out_acc, *, tm, name):
    n_rows = (rows[0][0] if isinstance(rows[0], tuple) else rows[0]).shape[0]
    assert n_rows % tm == 0, (n_rows, tm)
    arrs, in_specs = [], []
    for r in rows:
        arr, width, blk = r if isinstance(r, tuple) else (r, r.shape[1], 0)
        assert arr.shape[0] == n_rows
        arrs.append(arr)
        in_specs.append(pl.BlockSpec((tm, width), functools.partial(lambda i, b: (i, b), b=blk)))
    for c in consts:
        arrs.append(c)
        in_specs.append(pl.BlockSpec(c.shape, functools.partial(lambda i, n: (0,) * n, n=c.ndim)))
    n_in, n_ro = len(arrs), len(out_rows)
    out_shape = [jax.ShapeDtypeStruct((n_rows, w), dt) for w, dt in out_rows]
    out_specs = [pl.BlockSpec((tm, w), lambda i: (i, 0)) for w, _ in out_rows]
    out_shape += [jax.ShapeDtypeStruct(s, F32) for s in out_acc]
    out_specs += [pl.BlockSpec(s, lambda i: (0, 0)) for s in out_acc]

    def kern(*refs):
        ro, ao = body(*[r[...] for r in refs[:n_in]])
        outs = refs[n_in:]
        for r, v in zip(outs[:n_ro], ro, strict=True):
            r[...] = v.astype(r.dtype)
        if out_acc:
            @pl.when(pl.program_id(0) == 0)
            def _():
                for r in outs[n_ro:]:
                    r[...] = jnp.zeros_like(r)

            for r, v in zip(outs[n_ro:], ao, strict=True):
                r[...] += v

    res = pl.pallas_call(
        kern, name=name, grid=(n_rows // tm,), in_specs=in_specs, out_specs=out_specs, out_shape=out_shape,
        compiler_params=_params("arbitrary" if out_acc else "parallel"),
    )(*arrs)
    return res


def _rms(x, gain):
    return x * lax.rsqrt(jnp.mean(x * x, axis=-1, keepdims=True) + EPS) * gain


def _silu(x):
    return x * jax.nn.sigmoid(x)


def _softplus(x):
    return jnp.maximum(x, 0.0) + jnp.log(1.0 + jnp.exp(-jnp.abs(x)))


def _dot_nt(a, b, precision=None):
    return lax.dot_general(a, b, (((1,), (1,)), ((), ())), preferred_element_type=F32, precision=precision)


def _dot_tn(a, b, precision=None):
    return lax.dot_general(a, b, (((0,), (0,)), ((), ())), preferred_element_type=F32, precision=precision)


def _dot(a, b, precision=None):
    return jnp.dot(a, b, preferred_element_type=F32, precision=precision)


def _pre_norm(x, gain, *, name):
    def body(x, g):
        return (_rms(x, g),), ()
    return _rowwise(body, [x], [gain], [(x.shape[1], MXU_DTYPE)], [], tm=_tile(x.shape[0], 512, 2 * SUBLANES), name=name)[0]


def _pre_norm_bwd(x, gain, dh, dx_other, *, name):
    def body(x, dh, dxo, g):
        _, vjp = jax.vjp(_rms, x, g)
        dx, dg = vjp(dh)
        return (dx + dxo,), (dg,)
    return _rowwise(body, [x, dh, dx_other], [gain], [(x.shape[1], F32)], [gain.shape], tm=512, name=name)


def _gain_bwd(x, gain, dh, *, name):
    def body(x, dh, g):
        _, vjp = jax.vjp(lambda g_: _rms(x, g_), g)
        return (), (vjp(dh)[0],)
    return _rowwise(body, [x, dh], [gain], [], [gain.shape], tm=_tile(x.shape[0], 512, SUBLANES), name=name)[0]


def _res_block(x_res, m, g_post, g_pre):
    x_new = x_res + _rms(m, g_post)
    return x_new, _rms(x_new, g_pre)


def _post_pre(x_res, m, g_post, g_pre, *, name):
    def body(x, m, gp, gq):
        return _res_block(x, m, gp, gq), ()
    d = x_res.shape[1]
    return _rowwise(body, [x_res, m], [g_post, g_pre], [(d, F32), (d, MXU_DTYPE)], [], tm=512, name=name)


def _post_pre_bwd(x_res, m, g_post, g_pre, dx_new, dh, *, name):
    def body(x, m, dxn, dh, gp, gq):
        _, vjp = jax.vjp(_res_block, x, m, gp, gq)
        dx, dm, dgp, dgq = vjp((dxn, dh))
        return (dx, dm), (dgp, dgq)
    d = x_res.shape[1]
    return _rowwise(body, [x_res, m, dx_new, dh], [g_post, g_pre], [(d, F32), (d, MXU_DTYPE)],
                    [g_post.shape, g_pre.shape], tm=256, name=name)


def _final_loss_bwd(x_res, m, g_post, target, *, name):
    d = x_res.shape[1]

    def loss_cols(x, m, g, t):
        err = x + _rms(m, g) - t
        return jnp.sum(err * err, axis=0, keepdims=True) * (0.5 / d)

    def body(x, m, t, g):
        cols, vjp = jax.vjp(lambda x_, m_, g_: loss_cols(x_, m_, g_, t), x, m, g)
        dx, dm, dg = vjp(jnp.ones_like(cols))
        return (dx, dm), (dg, cols)
    return _rowwise(body, [x_res, m, target], [g_post], [(d, F32), (d, MXU_DTYPE)], [g_post.shape, (1, d)], tm=256, name=name)


def _swiglu_act(gu, *, name):
    def body(gate, up):
        return (_silu(gate.astype(F32)) * up.astype(F32),), ()
    return _rowwise(body, [_col(gu, D_FF, 0), _col(gu, D_FF, 1)], [], [(D_FF, MXU_DTYPE)], [], tm=256, name=name)[0]


def _swiglu_act_bwd(gu, da, *, name):
    def body(gate, up, da):
        _, vjp = jax.vjp(lambda g, u: _silu(g) * u, gate.astype(F32), up.astype(F32))
        dg, du = vjp(da.astype(F32))
        return (jnp.concatenate([dg, du], axis=1),), ()
    return _rowwise(body, [_col(gu, D_FF, 0), _col(gu, D_FF, 1), da], [], [(2 * D_FF, MXU_DTYPE)], [], tm=256, name=name)[0]


def _lane_head_mask(width, head_dim, head):
    lane = lax.broadcasted_iota(jnp.int32, (1, width), 1)
    return (lane // head_dim) == head


def _mem_attn_pair(q_pair, k_pair, v_pair):
    out = jnp.zeros_like(q_pair)
    for h in range(2):
        mh = _lane_head_mask(LANES, ATT_HEAD_DIM, h)
        qh = jnp.where(mh, q_pair * (ATT_HEAD_DIM ** -0.5), 0.0)
        logits = _dot_nt(qh, k_pair)
        mx = jnp.max(logits, axis=-1, keepdims=True)
        p = jnp.exp(logits - mx)
        p = p / jnp.sum(p, axis=-1, keepdims=True)
        out = out + jnp.where(mh, _dot(p, v_pair), 0.0)
    return out


def _mem_attn(q_mem, kv):
    outs = []
    for p in range(MEM_WIDTH // LANES):
        sl = slice(p * LANES, (p + 1) * LANES)
        outs.append(_mem_attn_pair(q_mem[:, sl], kv[:, sl], kv[:, MEM_WIDTH + p * LANES: MEM_WIDTH + (p + 1) * LANES]))
    return jnp.concatenate(outs, axis=1)


def _mem_attn_bwd(q_mem, kv, do):
    dqs, dks, dvs = [], [], []
    for p in range(MEM_WIDTH // LANES):
        sl = slice(p * LANES, (p + 1) * LANES)
        sv = slice(MEM_WIDTH + p * LANES, MEM_WIDTH + (p + 1) * LANES)
        _, vjp = jax.vjp(_mem_attn_pair, q_mem[:, sl], kv[:, sl], kv[:, sv])
        dq, dk, dv = vjp(do[:, sl])
        dqs.append(dq)
        dks.append(dk)
        dvs.append(dv)
    return jnp.concatenate(dqs, axis=1), jnp.concatenate(dks + dvs, axis=1)


def _t5_bucket(rel):
    half = REL_BUCKETS // 2
    max_exact = half // 2
    n = np.abs(rel)
    large = max_exact + (np.log(np.maximum(n, 1) / max_exact) / math.log(REL_MAX_DIST / max_exact)
                         * (half - max_exact)).astype(np.int64)
    large = np.minimum(large, half - 1)
    return ((rel > 0) * half + np.where(n < max_exact, n, large)).astype(np.int32)


ATT_DIAGS = ATT_BQ + ATT_W - 1


def _bias_diag_onehot(dil):
    j = np.arange(ATT_DIAGS)
    tiles = []
    for off in (-HALF, 0, HALF):
        rel = j - (ATT_BQ - 1) - HALF - off
        hot = _t5_bucket(rel * dil)[:, None] == np.arange(REL_BUCKETS)[None, :]
        tiles.append(hot & (np.abs(rel) <= HALF)[:, None])
    return np.stack(tiles).astype(np.float32)


def _toeplitz(r):
    lead = r.shape[:-1]
    a = jnp.broadcast_to(r[..., None, :], lead + (ATT_BQ, ATT_DIAGS))
    a = jnp.pad(a, [(0, 0)] * len(lead) + [(0, 0), (0, 1)])
    a = a.reshape(lead + (ATT_BQ * (ATT_DIAGS + 1),))[..., : ATT_BQ * ATT_DIAGS].reshape(lead + (ATT_BQ, ATT_DIAGS))
    return a[..., ATT_BQ - 1: ATT_BQ - 1 + ATT_W]


def _bias_tiles(rel_bias, gi):
    heads = rel_bias[:, 4 * gi: 4 * gi + 4]
    diag = jnp.einsum('tnb,bh->thn', jnp.asarray(_bias_diag_onehot(DILATIONS[gi])), heads, precision=HI)
    return _toeplitz(diag)


def _bias_tiles_bwd(rel_bias, dtiles, gi):
    return jax.vjp(lambda rb: _bias_tiles(rb, gi), rel_bias)[1](dtiles)[0]


def _att_window(i, n_sub):
    start = jnp.clip(i * ATT_BQ - HALF, 0, n_sub - ATT_W)
    off = i * ATT_BQ - HALF - start
    return pl.multiple_of(start, HALF), off


def _att_valid(off):
    q = lax.broadcasted_iota(jnp.int32, (ATT_BQ, ATT_W), 0)
    kk = lax.broadcasted_iota(jnp.int32, (ATT_BQ, ATT_W), 1)
    return jnp.abs(kk - q - HALF - off) <= HALF


def _att_tile_id(i, nq):
    return jnp.where(i == 0, 0, jnp.where(i == nq - 1, 2, 1))


ATT_GROUP_HEADS = 4


def _heads(x, mask=False):
    out = []
    for p in range(2):
        pair = x[:, p * LANES: (p + 1) * LANES]
        for h in range(2):
            out.append(jnp.where(_lane_head_mask(LANES, ATT_HEAD_DIM, h), pair, 0.0) if mask else pair)
    return jnp.stack(out)


def _join_heads(x):
    first = _lane_head_mask(LANES, ATT_HEAD_DIM, 0)
    return jnp.concatenate([jnp.where(first, x[2 * p], x[2 * p + 1]) for p in range(2)], axis=1)


def _head_scalar(x):
    out = []
    for p in range(2):
        pair = x[:, p * LANES: (p + 1) * LANES]
        for h in range(2):
            out.append(jnp.max(jnp.where(_lane_head_mask(LANES, ATT_HEAD_DIM, h), pair, NEG_INF), axis=-1, keepdims=True))
    return jnp.stack(out)


def _bdot(a, b):
    return jnp.einsum('hqk,hkd->hqd', a, b, preferred_element_type=F32)


def _bdot_nt(a, b):
    return jnp.einsum('hqd,hkd->hqk', a, b, preferred_element_type=F32)


def _bdot_tn(a, b):
    return jnp.einsum('hqk,hqd->hkd', a, b, preferred_element_type=F32)


def _att_fwd(qkvm, bias, gi, *, name):
    dil = DILATIONS[gi]
    s_len = qkvm.shape[0]
    n_sub = s_len // dil
    nq = n_sub // ATT_BQ
    assert n_sub % ATT_BQ == 0 and n_sub >= ATT_W
    cols = qkvm.shape[1] // (2 * LANES)
    view = qkvm.reshape(n_sub, dil * qkvm.shape[1])

    def kern(q_ref, k_ref, v_ref, b_ref, o_ref, lse_ref):
        start, off = _att_window(pl.program_id(1), n_sub)
        valid = _att_valid(off)
        q4 = _heads(q_ref[...].astype(F32) * (ATT_HEAD_DIM ** -0.5), mask=True)
        k4 = _heads(k_ref[pl.ds(start, ATT_W), :].astype(F32))
        v4 = _heads(v_ref[pl.ds(start, ATT_W), :].astype(F32))
        s = jnp.where(valid, _bdot_nt(q4, k4) + b_ref[...], NEG_INF)
        mx = jnp.max(s, axis=-1, keepdims=True)
        p = jnp.exp(s - mx)
        den = jnp.sum(p, axis=-1, keepdims=True)
        o_ref[...] = _join_heads(_bdot(p, v4) / den)
        lse_ref[...] = _join_heads(jnp.broadcast_to(mx + jnp.log(den), (ATT_GROUP_HEADS, ATT_BQ, LANES)))

    def qkv_spec(which, full):
        shape = (n_sub, 2 * LANES) if full else (ATT_BQ, 2 * LANES)
        return pl.BlockSpec(shape, lambda r, i: (0 if full else i, r * cols + which * 3 + gi))

    out_spec = pl.BlockSpec((ATT_BQ, 2 * LANES), lambda r, i: (i, r))
    o, lse = pl.pallas_call(
        kern, name=name, grid=(dil, nq),
        in_specs=[qkv_spec(0, False), qkv_spec(1, True), qkv_spec(2, True),
                  pl.BlockSpec((None, ATT_GROUP_HEADS, ATT_BQ, ATT_W), lambda r, i: (_att_tile_id(i, nq), 0, 0, 0))],
        out_specs=[out_spec, out_spec],
        out_shape=[jax.ShapeDtypeStruct((n_sub, dil * 2 * LANES), F32)] * 2,
        compiler_params=_params("parallel", "arbitrary"),
    )(view, view, view, bias)
    return o.reshape(s_len, 2 * LANES), lse.reshape(s_len, 2 * LANES)


def _att_bwd(qkvm, bias, lse_tot, delta, dcat, gi, *, name):
    dil = DILATIONS[gi]
    s_len = qkvm.shape[0]
    n_sub = s_len // dil
    nq = n_sub // ATT_BQ
    cols = qkvm.shape[1] // (2 * LANES)
    dcols = dcat.shape[1] // (2 * LANES)
    view = qkvm.reshape(n_sub, dil * qkvm.shape[1])
    lse_v = lse_tot.reshape(n_sub, dil * 2 * LANES)
    delta_v = delta.reshape(n_sub, dil * 2 * LANES)
    dcat_v = dcat.reshape(n_sub, dil * dcat.shape[1])

    def kern(q_ref, k_ref, v_ref, b_ref, lse_ref, dl_ref, dm_ref, dq_ref, dk_ref, dv_ref, db_ref):
        r, i = pl.program_id(0), pl.program_id(1)
        start, off = _att_window(i, n_sub)
        valid = _att_valid(off)
        tile = _att_tile_id(i, nq)

        @pl.when(i == 0)
        def _():
            dk_ref[...] = jnp.zeros_like(dk_ref)
            dv_ref[...] = jnp.zeros_like(dv_ref)

        @pl.when((i == 0) & (r == 0))
        def _():
            db_ref[...] = jnp.zeros_like(db_ref)

        q4 = _heads(q_ref[...].astype(F32) * (ATT_HEAD_DIM ** -0.5), mask=True)
        k4 = _heads(k_ref[pl.ds(start, ATT_W), :].astype(F32))
        v4 = _heads(v_ref[pl.ds(start, ATT_W), :].astype(F32))
        dm4 = _heads(dm_ref[...], mask=True)
        s = jnp.where(valid, _bdot_nt(q4, k4) + b_ref[tile], NEG_INF)
        p = jnp.exp(s - _head_scalar(lse_ref[...]))
        ds = p * (_bdot_nt(dm4, v4) - _head_scalar(dl_ref[...]))
        dq_ref[...] = _join_heads(_bdot(ds, k4)) * (ATT_HEAD_DIM ** -0.5)
        dk4 = _bdot_tn(ds, q4)
        dv4 = _bdot_tn(p, dm4)
        dk_ref[pl.ds(start, ATT_W), :] += jnp.concatenate([dk4[0] + dk4[1], dk4[2] + dk4[3]], axis=1)
        dv_ref[pl.ds(start, ATT_W), :] += jnp.concatenate([dv4[0] + dv4[1], dv4[2] + dv4[3]], axis=1)
        db_ref[tile] += ds

    def qkv_spec(which, full):
        shape = (n_sub, 2 * LANES) if full else (ATT_BQ, 2 * LANES)
        return pl.BlockSpec(shape, lambda r, i: (0 if full else i, r * cols + which * 3 + gi))

    blk = pl.BlockSpec((ATT_BQ, 2 * LANES), lambda r, i: (i, r))
    full = pl.BlockSpec((n_sub, 2 * LANES), lambda r, i: (0, r))
    bias_spec = pl.BlockSpec(bias.shape, lambda r, i: (0, 0, 0, 0))
    sub = jax.ShapeDtypeStruct((n_sub, dil * 2 * LANES), F32)
    dq, dk, dv, db = pl.pallas_call(
        kern, name=name, grid=(dil, nq),
        in_specs=[qkv_spec(0, False), qkv_spec(1, True), qkv_spec(2, True), bias_spec, blk, blk,
                  pl.BlockSpec((ATT_BQ, 2 * LANES), lambda r, i: (i, r * dcols + gi))],
        out_specs=[blk, full, full, bias_spec],
        out_shape=[sub, sub, sub, jax.ShapeDtypeStruct(bias.shape, F32)],
        compiler_params=_params("arbitrary", "arbitrary"),
    )(view, view, view, bias, lse_v, delta_v, dcat_v)
    return dq.reshape(s_len, -1), dk.reshape(s_len, -1), dv.reshape(s_len, -1), db


def _att_combine(o_g, lse_g, qkvm, kv_mem, *, name):
    def body(o0, o1, o2, l0, l1, l2, qm, kv):
        mx = jnp.maximum(jnp.maximum(l0, l1), l2)
        tot = mx + jnp.log(jnp.exp(l0 - mx) + jnp.exp(l1 - mx) + jnp.exp(l2 - mx))
        mixed = [o * jnp.exp(l - tot) for o, l in ((o0, l0), (o1, l1), (o2, l2))]
        return (jnp.concatenate(mixed + [_mem_attn(qm.astype(F32), kv)], axis=1), tot), ()
    return _rowwise(body, list(o_g) + list(lse_g) + [_col(qkvm, MEM_WIDTH, (3 * TOK_WIDTH) // MEM_WIDTH)], [kv_mem],
                    [(D_MODEL, F32), (MEM_WIDTH, F32)], [], tm=256, name=name)


def _head_sum_matrix():
    a = np.arange(MEM_WIDTH)
    return jnp.asarray((a[:, None] // ATT_HEAD_DIM == a[None, :] // ATT_HEAD_DIM).astype(np.float32))


def _att_bwd_prep(cat, dcat, qkvm, kv_mem, *, name):
    def body(cat, dcat, qm, kv, hs):
        prod = cat * dcat
        summed = prod[:, 0:256] + prod[:, 256:512] + prod[:, 512:768]
        delta = _dot(summed, hs, precision=HI)
        dqm, dkv = _mem_attn_bwd(qm.astype(F32), kv, dcat[:, TOK_WIDTH:])
        return (delta, dqm), (dkv,)
    return _rowwise(body, [cat, dcat, _col(qkvm, MEM_WIDTH, (3 * TOK_WIDTH) // MEM_WIDTH)], [kv_mem, _head_sum_matrix()],
                    [(MEM_WIDTH, F32), (MEM_WIDTH, F32)], [kv_mem.shape], tm=256, name=name)


def _dn_conv_post(s, j):
    scale = jnp.where(j < DN_HEADS, DN_HEAD_DIM ** -0.5, 1.0)
    normed = s * lax.rsqrt(jnp.sum(s * s, axis=-1, keepdims=True) + EPS) * scale
    return jnp.where(j >= 2 * DN_HEADS, s, normed)


def _shift_rows(x, sh):
    n = x.shape[0]
    row = lax.broadcasted_iota(jnp.int32, (n, 1), 0)
    rolled = pltpu.roll(x, (-sh) % n, 0)
    return jnp.where((row + sh >= 0) & (row + sh < n), rolled, 0.0)


def _dn_conv_taps(x, w_ref):
    c = x * w_ref[pl.ds(DN_CONV // 2, 1), :]
    for jj in range(DN_CONV):
        if jj != DN_CONV // 2:
            c = c + _shift_rows(x, jj - DN_CONV // 2) * w_ref[pl.ds(jj, 1), :]
    return c


def _dn_conv_fwd(proj, conv_w, *, name):
    s_len = proj.shape[0]
    width = 3 * TOK_WIDTH

    def kern(x_ref, w_ref, o_ref):
        j = pl.program_id(0)
        o_ref[...] = _dn_conv_post(_silu(_dn_conv_taps(x_ref[...], w_ref)), j)

    return pl.pallas_call(
        kern, name=name, grid=(width // LANES,),
        in_specs=[pl.BlockSpec((s_len, LANES), lambda j: (0, j)), pl.BlockSpec((DN_CONV, LANES), lambda j: (0, j))],
        out_specs=pl.BlockSpec((s_len, LANES), lambda j: (0, j)),
        out_shape=jax.ShapeDtypeStruct((s_len, width), F32),
        compiler_params=_params("parallel"),
    )(proj, conv_w)


def _dn_conv_bwd(proj, conv_w, d_fwd, d_bwd, which, *, name):
    s_len = proj.shape[0]

    def kern(x_ref, w_ref, df_ref, db_ref, dx_ref, dw_ref):
        j = pl.program_id(0) + which * DN_HEADS
        x = x_ref[...]
        c = _dn_conv_taps(x, w_ref)
        _, vjp = jax.vjp(lambda c_: _dn_conv_post(_silu(c_), j), c)
        dc = vjp(df_ref[...] + db_ref[...])[0]
        dx = dc * w_ref[pl.ds(DN_CONV // 2, 1), :]
        for jj in range(DN_CONV):
            sh = jj - DN_CONV // 2
            if sh != 0:
                dx = dx + _shift_rows(dc, -sh) * w_ref[pl.ds(jj, 1), :]
            dw_ref[pl.ds(jj, 1), :] = jnp.sum(dc * _shift_rows(x, sh), axis=0, keepdims=True)
        dx_ref[...] = dx

    return pl.pallas_call(
        kern, name=name, grid=(DN_HEADS,),
        in_specs=[pl.BlockSpec((s_len, LANES), lambda j: (0, j + which * DN_HEADS)),
                  pl.BlockSpec((DN_CONV, LANES), lambda j: (0, j + which * DN_HEADS)),
                  pl.BlockSpec((s_len, LANES), lambda j: (0, j)),
                  pl.BlockSpec((s_len, LANES), lambda j: (0, j))],
        out_specs=[pl.BlockSpec((s_len, LANES), lambda j: (0, j)), pl.BlockSpec((DN_CONV, LANES), lambda j: (0, j))],
        out_shape=[jax.ShapeDtypeStruct((s_len, TOK_WIDTH), F32), jax.ShapeDtypeStruct((DN_CONV, TOK_WIDTH), F32)],
        compiler_params=_params("parallel"),
    )(proj, conv_w, d_fwd, d_bwd)


GATE_TM = 2 * DN_CHUNK


FWD_GATE_LANES = 2 * DN_HEADS


def _gate_constants():
    i = np.arange(GATE_TM)
    same = (i[:, None] // DN_CHUNK) == (i[None, :] // DN_CHUNK)
    cum_f = same & (i[None, :] <= i[:, None])
    cum_r = same & (i[None, :] >= i[:, None])
    return tuple(jnp.asarray(np.asarray(a, np.float32)) for a in (cum_f, cum_r, same))


def _gate_params(p):
    z = jnp.zeros((DN_HEADS,), F32)
    return jnp.concatenate([p[0], z, p[1], z, jnp.zeros((LANES - N_GATES,), F32)]).reshape(1, LANES)


def _gate_params_bwd(dp):
    return jnp.stack([dp[0, 0:DN_HEADS], dp[0, 2 * DN_HEADS: 3 * DN_HEADS]])


def _dn_gates(gate_in, a_cols, dt_cols, cum_f, cum_r, tot):
    g = -jnp.exp(a_cols) * _softplus(gate_in + dt_cols)
    fwd_lane = lax.broadcasted_iota(jnp.int32, (1, LANES), 1) < FWD_GATE_LANES
    gc = jnp.where(fwd_lane, _dot(cum_f, g, precision=HI), _dot(cum_r, g, precision=HI))
    return gc, _dot(tot, g, precision=HI), jax.nn.sigmoid(gate_in)


def _dn_gates_fwd(proj, a_cols, dt_cols, *, name):
    def body(gi, *consts):
        return _dn_gates(gi, *consts), ()
    return _rowwise(body, [_col(proj, LANES, DN_IN_PAD // LANES - 1)], [a_cols, dt_cols, *_gate_constants()],
                    [(LANES, F32)] * 3, [], tm=GATE_TM, name=name)


def _dn_gates_bwd(proj, a_cols, dt_cols, d_gates, *, name):
    def body(gi, gcf, gtf, bf, gcr, gtr, br, a, dt, *consts):
        _, vjp = jax.vjp(lambda gi_, a_, dt_: _dn_gates(gi_, a_, dt_, *consts), gi, a, dt)
        dgi, da, ddt = vjp((gcf + gcr, gtf + gtr, bf + br))
        return (dgi,), (da, ddt)
    return _rowwise(body, [_col(proj, LANES, DN_IN_PAD // LANES - 1), *d_gates[0], *d_gates[1]],
                    [a_cols, dt_cols, *_gate_constants()], [(LANES, F32)], [a_cols.shape, dt_cols.shape],
                    tm=GATE_TM, name=name)


INV_BASE = 8


def _block_id_equal(c, size):
    i = lax.broadcasted_iota(jnp.int32, (c, c), 0) // size
    j = lax.broadcasted_iota(jnp.int32, (c, c), 1) // size
    return (i == j).astype(F32)


def _unit_tri_inverse_impl(lmat):
    c = lmat.shape[0]
    eye = _block_id_equal(c, 1)
    same = _block_id_equal(c, INV_BASE)
    neg = -lmat * same
    inv = eye + neg
    power = neg
    for _ in range(int(math.log2(INV_BASE)) - 1):
        power = _dot(power, power)
        inv = inv + _dot(inv, power)
    size = INV_BASE
    while size < c:
        bigger = _block_id_equal(c, 2 * size)
        inv = inv - _dot(_dot(inv, lmat * (bigger - same)), inv)
        same, size = bigger, 2 * size
    resid = eye - _dot(eye + lmat, inv, precision=HI)
    return inv + _dot(inv, resid)


@jax.custom_vjp
def _unit_tri_inverse(lmat):
    return _unit_tri_inverse_impl(lmat)


def _unit_tri_inverse_fwd(lmat):
    inv = _unit_tri_inverse_impl(lmat)
    return inv, inv


def _unit_tri_inverse_bwd(inv, d_inv):
    return (-_dot_tn(inv, _dot_nt(d_inv, inv)),)


_unit_tri_inverse.defvjp(_unit_tri_inverse_fwd, _unit_tri_inverse_bwd)


def _dn_chunk(q, k, v, gates_t, gc_row, tot_row, beta_row, state, tri, inverse):
    c = q.shape[0]
    assert c == DN_HEAD_DIM
    eye = _block_id_equal(c, 1)

    def along_rows(x, pick):
        return jnp.broadcast_to(jnp.sum(x * pick, axis=0, keepdims=True), (c, c))

    gc_j = along_rows(gates_t[0], gc_row)
    gc = gc_j.T
    g_tot = along_rows(gates_t[1], tot_row)
    beta = along_rows(gates_t[2], beta_row).T
    decay = jnp.exp(jnp.where(tri > 0, gc - gc_j, NEG_INF))
    k_beta = k * beta
    inv = inverse((tri - eye) * (_dot_nt(k_beta, k) * decay))
    e_gc = jnp.exp(gc)
    u = _dot(inv, v * beta)
    w = _dot(inv, k_beta * e_gc)
    intra = tri * (_dot_nt(q, k) * decay)
    v_new = u - _dot(w, state)
    out = _dot(q * e_gc, state) + _dot(intra, v_new)
    state = state * jnp.exp(g_tot) + _dot_tn(k * jnp.exp(g_tot - gc), v_new)
    return out, state


def _dn_tri():
    i = np.arange(DN_CHUNK)
    tri = np.stack([(i[None, :] <= i[:, None]), (i[None, :] >= i[:, None])]).astype(np.float32)
    return jnp.asarray(np.repeat(tri, DN_HEADS, axis=0))


def _dn_gate_picks():
    picks = np.zeros((3, 2 * DN_HEADS, 2 * DN_CHUNK, 1), np.float32)
    for d in range(2):
        for h in range(DN_HEADS):
            alpha = d * DN_CHUNK + d * 2 * DN_HEADS + h
            picks[0, d * DN_HEADS + h, alpha] = 1.0
            picks[1, d * DN_HEADS + h, alpha] = 1.0
            picks[2, d * DN_HEADS + h, alpha + DN_HEADS] = 1.0
    return jnp.asarray(picks)


def _stack_chains(fwd_ref, rev_ref):
    return jnp.stack([r[:, _head_cols(h)] for r in (fwd_ref, rev_ref) for h in range(DN_HEADS)])


def _unstack_chains(val, fwd_ref, rev_ref):
    for d, r in enumerate((fwd_ref, rev_ref)):
        for h in range(DN_HEADS):
            r[:, _head_cols(h)] = val[d * DN_HEADS + h]


def _gates_transposed(fwd_refs, rev_refs):
    return jnp.stack([jnp.concatenate([f[...].T, r[...].T], axis=0) for f, r in zip(fwd_refs, rev_refs, strict=True)])


def _dn_row_spec(nc, col, reverse, width=TOK_WIDTH):
    return pl.BlockSpec((DN_CHUNK, width), lambda t: ((nc - 1 - t) if reverse else t, col))


def _dn_state_spec(nc, reverse):
    return pl.BlockSpec((None, DN_HEADS, DN_HEAD_DIM, DN_HEAD_DIM), lambda t: ((nc - 1 - t) if reverse else t, 0, 0, 0))


def _head_cols(h):
    return pl.ds(h * DN_HEAD_DIM, DN_HEAD_DIM)


def _const_spec(arr):
    return pl.BlockSpec(arr.shape, functools.partial(lambda t, n: (0,) * n, n=arr.ndim))


def _dn_chains(inverse):
    return jax.vmap(lambda q, k, v, gates_t, *rest: _dn_chunk(q, k, v, gates_t, *rest, inverse),
                    in_axes=(0, 0, 0, None, 0, 0, 0, 0, 0))


def _dn_scan_fwd(qkv, gates, *, name):
    s_len = qkv.shape[0]
    nc = s_len // DN_CHUNK
    tri, picks = _dn_tri(), _dn_gate_picks()

    def kern(*refs):
        ins, (tri_ref, pick_ref, of_ref, or_ref, sf_ref, sr_ref, state) = refs[:12], refs[12:]

        @pl.when(pl.program_id(0) == 0)
        def _():
            state[...] = jnp.zeros_like(state)

        entry = state[...]
        qkv_c = [_stack_chains(ins[i], ins[6 + i]) for i in range(3)]
        gates_t = _gates_transposed(ins[3:6], ins[9:12])
        out, new = _dn_chains(_unit_tri_inverse_impl)(*qkv_c, gates_t, pick_ref[0], pick_ref[1], pick_ref[2], entry, tri_ref[...])
        sf_ref[...] = entry[:DN_HEADS]
        sr_ref[...] = entry[DN_HEADS:]
        _unstack_chains(out, of_ref, or_ref)
        state[...] = new

    in_specs = []
    for rev in (False, True):
        in_specs += [_dn_row_spec(nc, col, rev) for col in (0, 1, 2)] + [_dn_row_spec(nc, 0, rev, LANES)] * 3
    in_specs += [_const_spec(tri), _const_spec(picks)]
    return pl.pallas_call(
        kern, name=name, grid=(nc,), in_specs=in_specs,
        out_specs=[_dn_row_spec(nc, 0, False), _dn_row_spec(nc, 0, True), _dn_state_spec(nc, False), _dn_state_spec(nc, True)],
        out_shape=[jax.ShapeDtypeStruct((s_len, TOK_WIDTH), F32)] * 2
        + [jax.ShapeDtypeStruct((nc, DN_HEADS, DN_HEAD_DIM, DN_HEAD_DIM), F32)] * 2,
        scratch_shapes=[pltpu.VMEM((2 * DN_HEADS, DN_HEAD_DIM, DN_HEAD_DIM), F32)],
        compiler_params=_params("arbitrary"),
    )(*([qkv, qkv, qkv, *gates] * 2), tri, picks)


def _dn_scan_bwd(qkv, gates, states, d_o, *, name):
    s_len = qkv.shape[0]
    nc = s_len // DN_CHUNK
    tri, picks = _dn_tri(), _dn_gate_picks()

    def kern(*refs):
        ins, tri_ref, pick_ref, outs, d_state = refs[:16], refs[16], refs[17], refs[18:30], refs[30]

        @pl.when(pl.program_id(0) == 0)
        def _():
            d_state[...] = jnp.zeros_like(d_state)

        qkv_c = [_stack_chains(ins[i], ins[8 + i]) for i in range(3)]
        gates_t = _gates_transposed(ins[3:6], ins[11:14])
        entry = jnp.concatenate([ins[6][...], ins[14][...]], axis=0)
        d_out = _stack_chains(ins[7], ins[15])
        tri_v, picks_v = tri_ref[...], pick_ref[...]
        _, vjp = jax.vjp(lambda q, k, v, g, s: _dn_chains(_unit_tri_inverse)(q, k, v, g, picks_v[0], picks_v[1], picks_v[2], s, tri_v),
                         *qkv_c, gates_t, entry)
        dq, dk, dv, d_gates_t, d_entry = vjp((d_out, d_state[...]))
        for i, val in enumerate((dq, dk, dv)):
            _unstack_chains(val, outs[i], outs[6 + i])
        for i in range(3):
            outs[3 + i][...] = d_gates_t[i, :DN_CHUNK].T
            outs[9 + i][...] = d_gates_t[i, DN_CHUNK:].T
        d_state[...] = d_entry

    in_specs, out_specs, out_shape = [], [], []
    for rev in (True, False):
        in_specs += [_dn_row_spec(nc, col, rev) for col in (0, 1, 2)] + [_dn_row_spec(nc, 0, rev, LANES)] * 3
        in_specs += [_dn_state_spec(nc, rev), _dn_row_spec(nc, 0, rev)]
        out_specs += [_dn_row_spec(nc, 0, rev)] * 3 + [_dn_row_spec(nc, 0, rev, LANES)] * 3
        out_shape += [jax.ShapeDtypeStruct((s_len, TOK_WIDTH), F32)] * 3 + [jax.ShapeDtypeStruct((s_len, LANES), F32)] * 3
    in_specs += [_const_spec(tri), _const_spec(picks)]
    res = pl.pallas_call(
        kern, name=name, grid=(nc,), in_specs=in_specs, out_specs=out_specs, out_shape=out_shape,
        scratch_shapes=[pltpu.VMEM((2 * DN_HEADS, DN_HEAD_DIM, DN_HEAD_DIM), F32)],
        compiler_params=_params("arbitrary"),
    )(*[a for d in range(2) for a in (qkv, qkv, qkv, *gates, states[d], d_o)], tri, picks)
    return (res[0:3], res[3:6]), (res[6:9], res[9:12])


def _dn_out_head(o_f, o_b, z, gain):
    o = o_f + o_b
    return o * lax.rsqrt(jnp.mean(o * o, axis=-1, keepdims=True) + EPS) * gain * _silu(z)


def _dn_out(o_fwd, o_rev, proj, gain, qkv_kv_mem, *, name):
    def body(of, ob, z, qm, g, kv):
        heads = []
        for h in range(DN_HEADS):
            sl = slice(h * DN_HEAD_DIM, (h + 1) * DN_HEAD_DIM)
            heads.append(_dn_out_head(of[:, sl], ob[:, sl], z[:, sl], g))
        return (jnp.concatenate(heads + [_mem_attn(qm, kv)], axis=1),), ()
    return _rowwise(body, [o_fwd, o_rev, _col(proj, TOK_WIDTH, 3),
                           _col(proj, MEM_WIDTH, (4 * TOK_WIDTH) // MEM_WIDTH)], [gain, qkv_kv_mem],
                    [(D_MODEL, MXU_DTYPE)], [], tm=256, name=name)[0]


def _dn_out_bwd(o_fwd, o_rev, proj, gain, kv_mem, dcat, *, name):
    def body(of, ob, z, qm, dcat, g, kv):
        dos, dzs = [], []
        dgain = jnp.zeros_like(g)
        for h in range(DN_HEADS):
            sl = slice(h * DN_HEAD_DIM, (h + 1) * DN_HEAD_DIM)
            _, vjp = jax.vjp(_dn_out_head, of[:, sl], ob[:, sl], z[:, sl], g)
            d_of, _, dz, dg = vjp(dcat[:, sl])
            dos.append(d_of)
            dzs.append(dz)
            dgain = dgain + dg
        dqm, dkv = _mem_attn_bwd(qm, kv, dcat[:, TOK_WIDTH:])
        return (jnp.concatenate(dos, axis=1), jnp.concatenate(dzs, axis=1), dqm), (dgain, dkv)
    return _rowwise(body, [o_fwd, o_rev, _col(proj, TOK_WIDTH, 3),
                           _col(proj, MEM_WIDTH, (4 * TOK_WIDTH) // MEM_WIDTH), dcat], [gain, kv_mem],
                    [(TOK_WIDTH, F32), (TOK_WIDTH, F32), (MEM_WIDTH, F32)], [gain.shape, kv_mem.shape], tm=256, name=name)


def _pad_dn_w_in(w):
    gates = w[:, 4 * TOK_WIDTH: 4 * TOK_WIDTH + N_GATES]
    zeros = jnp.zeros((w.shape[0], DN_IN_PAD - DN_IN), w.dtype)
    return jnp.concatenate([w[:, :4 * TOK_WIDTH], w[:, 4 * TOK_WIDTH + N_GATES:], gates, zeros], axis=1)


def _unpad_dn_w_in(w):
    q_mem = w[:, 4 * TOK_WIDTH: 4 * TOK_WIDTH + MEM_WIDTH]
    gates = w[:, 4 * TOK_WIDTH + MEM_WIDTH: 4 * TOK_WIDTH + MEM_WIDTH + N_GATES]
    return jnp.concatenate([w[:, :4 * TOK_WIDTH], gates, q_mem], axis=1)


def _ffn_fwd(h, w_gu, w_d, tag):
    gu = _mm(h, w_gu, out_dtype=MXU_DTYPE, name=f"ffn_gu_{tag}")
    act = _swiglu_act(gu, name=f"ffn_act_{tag}")
    return gu, act, _mm(act, w_d, name=f"ffn_down_{tag}")


def _ffn_bwd(h, gu, act, w_gu, w_d, df, tag):
    d_act = _mm(df, w_d, tb=True, out_dtype=MXU_DTYPE, name=f"ffn_dact_{tag}")
    d_wd = _mm(act, df, ta=True, name=f"ffn_dwd_{tag}")
    d_gu = _swiglu_act_bwd(gu, d_act, name=f"ffn_dgu_{tag}")
    dh = _mm(d_gu, w_gu, tb=True, name=f"ffn_dh_{tag}")
    d_wgu = _mm(h, d_gu, ta=True, out_shards=_shards_of(w_gu), name=f"ffn_dwgu_{tag}")
    return dh, d_wgu, d_wd


def _local_step(x, mem, target, p):
    g = {}
    row = lambda v: v.reshape(1, -1)
    gains = {k: [row(p[k][i]) for i in range(2)] for k in
             ("mem_norm", "norm_mix_pre", "norm_mix_post", "norm_ffn_pre", "norm_ffn_post")}
    out_gain = row(p["dn_out_norm"])
    a_cols, dt_cols = _gate_params(p["dn_a_log"]), _gate_params(p["dn_dt_bias"])

    h0 = _pre_norm(x, gains["norm_mix_pre"][0], name="pre0")
    mem_n = [_pre_norm(mem, gains["mem_norm"][i], name=f"mem_norm{i}") for i in range(2)]
    kv_mem = [_mm(mem_n[i], p["mem_w_kv"][i], name=f"mem_kv{i}") for i in range(2)]
    qkvm = _mm(h0, p["att_w_in"], out_dtype=MXU_DTYPE, name="att_in")
    bias = [_bias_tiles(p["rel_bias"], gi) for gi in range(3)]
    att = [_att_fwd(qkvm, bias[gi], gi, name=f"att_fwd{gi}") for gi in range(3)]
    cat0, lse_tot = _att_combine([a[0] for a in att], [a[1] for a in att], qkvm, kv_mem[0], name="att_combine")
    mo0 = _mm(cat0, p["att_w_out"], name="att_out")
    x1, h1 = _post_pre(x, mo0, gains["norm_mix_post"][0], gains["norm_ffn_pre"][0], name="post_mix0")
    w_gu0, w_d0 = p["ffn_weights"](0, h1)
    gu0, act0, f0 = _ffn_fwd(h1, w_gu0, w_d0, 0)
    x2, h2 = _post_pre(x1, f0, gains["norm_ffn_post"][0], gains["norm_mix_pre"][1], name="post_ffn0")

    dn_w_in, dn_w_out = p["dn_weights"](h2)
    proj = _mm(h2, dn_w_in, name="dn_in")
    qkv = _dn_conv_fwd(proj, p["dn_conv"], name="dn_conv")
    gates = _dn_gates_fwd(proj, a_cols, dt_cols, name="dn_gates")
    o_fwd, o_rev, st_fwd, st_rev = _dn_scan_fwd(qkv, gates, name="dn_scan")
    cat1 = _dn_out(o_fwd, o_rev, proj, out_gain, kv_mem[1], name="dn_outnorm")
    mo1 = _mm(cat1, dn_w_out, name="dn_out")
    x3, h3 = _post_pre(x2, mo1, gains["norm_mix_post"][1], gains["norm_ffn_pre"][1], name="post_mix1")
    w_gu1, w_d1 = p["ffn_weights"](1, h3)
    gu1, act1, f1 = _ffn_fwd(h3, w_gu1, w_d1, 1)

    dx3, df1, dg_ffn_post1, loss_cols = _final_loss_bwd(x3, f1, gains["norm_ffn_post"][1], target, name="loss_bwd")
    dh3, d_wgu1, d_wd1 = _ffn_bwd(h3, gu1, act1, w_gu1, w_d1, df1, 1)
    sent = p["grads_ready"]("ffn1", {("ffn_w_gate_up", 1): d_wgu1, ("ffn_w_down", 1): d_wd1})
    dx2, dmo1, dg_mix_post1, dg_ffn_pre1 = _post_pre_bwd(x2, mo1, gains["norm_mix_post"][1] + sent, gains["norm_ffn_pre"][1],
                                                         dx3, dh3, name="post_mix1_bwd")
    dcat1 = _mm(dmo1, dn_w_out, tb=True, name="dn_out_dx")
    g["dn_w_out"] = _mm(cat1, dmo1, ta=True, name="dn_out_dw")
    d_o, dz, dqm1, d_out_gain, dkv1 = _dn_out_bwd(o_fwd, o_rev, proj, out_gain, kv_mem[1], dcat1, name="dn_outnorm_bwd")
    (d_f, dg_f), (d_r, dg_r) = _dn_scan_bwd(qkv, gates, (st_fwd, st_rev), d_o, name="dn_scan_bwd")
    d_gate_cols, d_a_cols, d_dt_cols = _dn_gates_bwd(proj, a_cols, dt_cols, (dg_f, dg_r), name="dn_gates_bwd")
    d_pre, d_conv = zip(*[_dn_conv_bwd(proj, p["dn_conv"], d_f[which], d_r[which], which, name=f"dn_conv_bwd{which}")
                          for which in range(3)])
    dproj = jnp.concatenate(list(d_pre) + [dz, dqm1, d_gate_cols], axis=1).astype(MXU_DTYPE)
    dh2 = _mm(dproj, dn_w_in, tb=True, name="dn_in_dx")
    g["dn_w_in"] = _mm(h2, dproj, ta=True, name="dn_in_dw")
    g["dn_conv"] = jnp.concatenate(d_conv, axis=1)
    g["dn_a_log"] = _gate_params_bwd(d_a_cols)
    g["dn_dt_bias"] = _gate_params_bwd(d_dt_cols)
    g["dn_out_norm"] = d_out_gain

    sent = p["grads_ready"]("dn", {("dn_w_in", 0): g["dn_w_in"], ("dn_w_out", 0): g["dn_w_out"]})
    dx1, df0, dg_ffn_post0, dg_mix_pre1 = _post_pre_bwd(x1, f0, gains["norm_ffn_post"][0] + sent, gains["norm_mix_pre"][1],
                                                        dx2, dh2, name="post_ffn0_bwd")
    dh1, d_wgu0, d_wd0 = _ffn_bwd(h1, gu0, act0, w_gu0, w_d0, df0, 0)
    sent = p["grads_ready"]("ffn0", {("ffn_w_gate_up", 0): d_wgu0, ("ffn_w_down", 0): d_wd0})
    dx0, dmo0, dg_mix_post0, dg_ffn_pre0 = _post_pre_bwd(x, mo0, gains["norm_mix_post"][0] + sent, gains["norm_ffn_pre"][0],
                                                         dx1, dh1, name="post_mix0_bwd")
    dcat0 = _mm(dmo0, p["att_w_out"], tb=True, name="att_out_dx")
    g["att_w_out"] = _mm(cat0, dmo0, ta=True, name="att_out_dw")
    delta, dqm0, dkv0 = _att_bwd_prep(cat0, dcat0, qkvm, kv_mem[0], name="att_bwd_prep")
    att_b = [_att_bwd(qkvm, bias[gi], lse_tot, delta, dcat0, gi, name=f"att_bwd{gi}") for gi in range(3)]
    dqkvm = jnp.concatenate([a[w] for w in range(3) for a in att_b] + [dqm0], axis=1).astype(MXU_DTYPE)
    g["rel_bias"] = sum(_bias_tiles_bwd(p["rel_bias"], att_b[gi][3], gi) for gi in range(3))
    dh0 = _mm(dqkvm, p["att_w_in"], tb=True, name="att_in_dx")
    g["att_w_in"] = _mm(h0, dqkvm, ta=True, out_shards=_shards_of(p["att_w_in"]), name="att_in_dw")
    grad_x, dg_mix_pre0 = _pre_norm_bwd(x, gains["norm_mix_pre"][0], dh0, dx0, name="pre0_bwd")

    d_mem_kv, d_mem_norm = [], []
    for i, dkv in enumerate((dkv0, dkv1)):
        d_mem_kv.append(_mm(mem_n[i], dkv, ta=True, name=f"mem_kv_dw{i}"))
        d_mem_n = _mm(dkv, p["mem_w_kv"][i], tb=True, name=f"mem_kv_dx{i}")
        d_mem_norm.append(_gain_bwd(mem, gains["mem_norm"][i], d_mem_n, name=f"mem_norm_bwd{i}"))
    g["mem_w_kv"] = d_mem_kv
    g["mem_norm"] = jnp.concatenate(d_mem_norm, axis=0)
    g["norm_mix_pre"] = jnp.concatenate([dg_mix_pre0, dg_mix_pre1], axis=0)
    g["norm_mix_post"] = jnp.concatenate([dg_mix_post0, dg_mix_post1], axis=0)
    g["norm_ffn_pre"] = jnp.concatenate([dg_ffn_pre0, dg_ffn_pre1], axis=0)
    g["norm_ffn_post"] = jnp.concatenate([dg_ffn_post0, dg_ffn_post1], axis=0)
    g["ffn_w_gate_up"] = [d_wgu0, d_wgu1]
    g["ffn_w_down"] = [d_wd0, d_wd1]
    return loss_cols, grad_x, g


N_CHIPS = 4
N_DEV = 8
MESH = pl.DeviceIdType.MESH
BIG = (("att_w_in", (1, 1024, 640), 2), ("att_w_out", (1, 256, 1024), 1), ("dn_w_in", (1, 1024, 838), 2),
       ("dn_w_out", (1, 256, 1024), 1), ("mem_w_kv", (2, 256, 512), 1), ("ffn_w_gate_up", (2, 1024, 1408), 2),
       ("ffn_w_down", (2, 704, 1024), 1))


def _mesh_pos():
    return lax.axis_index("x"), lax.axis_index("y"), lax.axis_index("c")


def _other_chips(x, y):
    return [(1 - x, y), (x, 1 - y), (1 - x, 1 - y)]


ANY = pl.BlockSpec(memory_space=pl.ANY)


def _all_reduce_small(v, *, name):
    rows, cols = v.shape
    flips = [(dx, dy, dc) for dx in (0, 1) for dy in (0, 1) for dc in (0, 1)][1:]

    def body(v_ref, o_ref, buf, send_sems, recv_sems):
        x, y, c = _mesh_pos()

        def peer(f):
            return tuple(1 - p if fl else p for p, fl in zip((x, y, c), f))

        def index(p):
            return 4 * p[0] + 2 * p[1] + p[2]

        buf[index((x, y, c))] = v_ref[...]
        sends = []
        for k, f in enumerate(flips):
            cp = pltpu.make_async_remote_copy(src_ref=v_ref, dst_ref=buf.at[index((x, y, c))], send_sem=send_sems.at[k],
                                              recv_sem=recv_sems.at[k], device_id=peer(f), device_id_type=MESH)
            cp.start()
            sends.append(cp)
        for k, f in enumerate(flips):
            pltpu.make_async_remote_copy(src_ref=v_ref, dst_ref=buf.at[index(peer(f))], send_sem=send_sems.at[k],
                                         recv_sem=recv_sems.at[k], device_id=peer(f), device_id_type=MESH).wait_recv()
        for cp in sends:
            cp.wait_send()
        acc = buf[0]
        for d in range(1, N_DEV):
            acc = acc + buf[d]
        o_ref[...] = acc

    vmem = pl.BlockSpec(memory_space=pltpu.VMEM)
    return pl.pallas_call(
        body, name=name, in_specs=[vmem], out_specs=vmem, out_shape=jax.ShapeDtypeStruct((rows, cols), F32),
        scratch_shapes=[pltpu.VMEM((N_DEV, rows, cols), F32), pltpu.SemaphoreType.DMA((N_DEV - 1,)),
                        pltpu.SemaphoreType.DMA((N_DEV - 1,))],
    )(v)


def _adamw(w, g, m, v, *, name):
    def body(w, g, m, v):
        m = ADAM_B1 * m + (1.0 - ADAM_B1) * g
        v = ADAM_B2 * v + (1.0 - ADAM_B2) * (g * g)
        m_hat = m / (1.0 - ADAM_B1 ** ADAM_STEP)
        v_hat = v / (1.0 - ADAM_B2 ** ADAM_STEP)
        delta = -ADAM_LR * (m_hat / (jnp.sqrt(v_hat) + ADAM_EPS) + ADAM_WD * w)
        return (delta, m, v), ()
    rows, cols = w.shape
    return _rowwise(body, [w, g, m, v], [], [(cols, F32)] * 3, [], tm=_tile(rows, 256, SUBLANES), name=name)


def _pack_small(arrs, rows):
    flat = jnp.concatenate([a.reshape(-1) for a in arrs])
    return jnp.pad(flat, (0, rows * LANES - flat.shape[0])).reshape(rows, LANES)


def _unpack_small(packed, shapes):
    flat = packed.reshape(-1)
    out, off = [], 0
    for s in shapes:
        size = math.prod(s)
        out.append(flat[off: off + size].reshape(s))
        off += size
    return out


def _small_rows(shapes):
    return -(-sum(math.prod(s) for s in shapes) // (SUBLANES * LANES)) * SUBLANES


def _sem_pairs(n):
    return [pltpu.SemaphoreType.DMA((n,)), pltpu.SemaphoreType.DMA((n,))]


def _gather_blocks(blocks, *, name):
    n = len(blocks)

    def body(*refs):
        x_refs, out_refs, (send_sems, recv_sems) = refs[:n], refs[n: 2 * n], refs[2 * n:]
        x, y, c = _mesh_pos()
        sibling = (x, y, 1 - c)
        chips = _other_chips(x, y)

        def copy(k, src, dst, to):
            return pltpu.make_async_remote_copy(src_ref=src, dst_ref=dst, send_sem=send_sems.at[k],
                                                recv_sem=recv_sems.at[k], device_id=to, device_id_type=MESH)

        def part(b, chip, h):
            half = blocks[b].shape[0] // 2
            return out_refs[b].at[2 * chip[0] + chip[1], pl.ds(h * half, half), :]

        def my_half(b):
            half = blocks[b].shape[0] // 2
            return x_refs[b].at[pl.ds(c * half, half), :]

        first = [copy(6 * b + j, my_half(b), part(b, (x, y), c), (*chip, c)) for b in range(n) for j, chip in enumerate(chips)]
        for cp in first:
            cp.start()
        passed = []
        for b in range(n):
            for j, chip in enumerate(chips):
                copy(6 * b + j, my_half(b), part(b, chip, c), (*chip, c)).wait_recv()
                cp = copy(6 * b + 3 + j, part(b, chip, c), part(b, chip, c), sibling)
                cp.start()
                passed.append(cp)
        for b in range(n):
            for j, chip in enumerate(chips):
                copy(6 * b + 3 + j, part(b, chip, 1 - c), part(b, chip, 1 - c), sibling).wait_recv()
        for cp in first + passed:
            cp.wait_send()

    return pl.pallas_call(
        body, name=name, in_specs=[ANY] * n, out_specs=[ANY] * n,
        out_shape=[jax.ShapeDtypeStruct((N_CHIPS, *a.shape), a.dtype) for a in blocks],
        scratch_shapes=_sem_pairs(6 * n),
    )(*blocks)


HBM = pl.BlockSpec(memory_space=pltpu.HBM)
SEM = pl.BlockSpec(memory_space=pltpu.SEMAPHORE)
DATAFLOW = pltpu.SideEffectType.DATAFLOW_SIDE_EFFECTING


def _gather_start(blocks, *, name):
    n = len(blocks)
    lands = [lax.empty((N_CHIPS, *a.shape), a.dtype) for a in blocks]

    def body(*refs):
        x_refs, land_refs, send_sems, recv_sems, token = refs[:n], refs[n: 2 * n], refs[2 * n], refs[2 * n + 1], refs[-1]
        x, y, c = _mesh_pos()
        for b in range(n):
            for j, chip in enumerate(_other_chips(x, y)):
                pltpu.make_async_remote_copy(src_ref=x_refs[b], dst_ref=land_refs[b].at[2 * x + y], send_sem=send_sems.at[3 * b + j],
                                             recv_sem=recv_sems.at[3 * b + j], device_id=(*chip, c), device_id_type=MESH).start()
        token[...] = jnp.zeros_like(token)

    operands = [pltpu.with_memory_space_constraint(a, pltpu.HBM) for a in blocks + lands]
    res = pl.pallas_call(
        body, name=name, in_specs=[HBM] * (2 * n),
        out_shape=(pltpu.SemaphoreType.DMA((3 * n,)), pltpu.SemaphoreType.DMA((3 * n,)),
                   *[pltpu.HBM(a.shape, a.dtype) for a in operands], jax.ShapeDtypeStruct((SUBLANES, LANES), F32)),
        out_specs=(SEM, SEM, *[HBM] * (2 * n), pl.BlockSpec(memory_space=pltpu.VMEM)),
        input_output_aliases={i: 2 + i for i in range(2 * n)},
        compiler_params=pltpu.CompilerParams(has_side_effects=DATAFLOW),
    )(*operands)
    return res[0], res[1], list(res[2: 2 + n]), list(res[2 + n: 2 + 2 * n]), res[-1]


def _gather_wait(started, after, *, name):
    send_sems, recv_sems, blocks, lands, _ = started
    n = len(blocks)

    def body(*refs):
        x_refs, land_refs, send_sems, recv_sems = refs[:n], refs[n: 2 * n], refs[2 * n], refs[2 * n + 1]
        x, y, c = _mesh_pos()
        for b in range(n):
            for j, chip in enumerate(_other_chips(x, y)):
                cp = pltpu.make_async_remote_copy(src_ref=x_refs[b], dst_ref=land_refs[b].at[2 * chip[0] + chip[1]],
                                                  send_sem=send_sems.at[3 * b + j], recv_sem=recv_sems.at[3 * b + j],
                                                  device_id=(*chip, c), device_id_type=MESH)
                cp.wait_send()
                cp.wait_recv()

    res = pl.pallas_call(
        body, name=name, in_specs=(*[HBM] * (2 * n), SEM, SEM, ANY),
        out_shape=tuple(pltpu.HBM(a.shape, a.dtype) for a in blocks + lands), out_specs=tuple([HBM] * (2 * n)),
        input_output_aliases={i: i for i in range(2 * n)},
        compiler_params=pltpu.CompilerParams(has_side_effects=DATAFLOW),
    )(*blocks, *lands, send_sems, recv_sems, after)
    return list(res[n:])


def _swap_halves(blocks, own_rows, *, name):
    n = len(blocks)

    def body(*refs):
        in_refs, out_refs, (send_sems, recv_sems) = refs[:n], refs[n: 2 * n], refs[2 * n:]
        x, y, c = _mesh_pos()
        copies = [pltpu.make_async_remote_copy(src_ref=own_rows(in_refs[b], c), dst_ref=out_refs[b], send_sem=send_sems.at[b],
                                               recv_sem=recv_sems.at[b], device_id=(x, y, 1 - c), device_id_type=MESH)
                  for b in range(n)]
        for cp in copies:
            cp.start()
        for cp in copies:
            cp.wait()

    def sent_shape(a):
        return jax.eval_shape(lambda r: own_rows(r, 0), a)

    return pl.pallas_call(
        body, name=name, in_specs=[ANY] * n, out_specs=[ANY] * n,
        out_shape=[jax.ShapeDtypeStruct(sent_shape(a).shape, a.dtype) for a in blocks],
        scratch_shapes=_sem_pairs(n),
    )(*blocks)


def _other_half_rows(ref, c):
    half = ref.shape[1] // 2
    return ref[:, (1 - c) * half: (2 - c) * half, :] if isinstance(c, int) else ref.at[:, pl.ds((1 - c) * half, half), :]


def _whole(ref, c):
    return ref


def _add_own_half_block(block, received, core, *, name):
    n, rows, cols = block.shape
    half = rows // 2
    tm = _tile(half, 512, 2 * SUBLANES)
    nb = half // tm

    def kern(c_ref, a_ref, b_ref, o_ref):
        o_ref[...] = (a_ref[...] + b_ref[...]).astype(o_ref.dtype)

    return pl.pallas_call(
        kern, name=name,
        grid_spec=pltpu.PrefetchScalarGridSpec(
            num_scalar_prefetch=1, grid=(n, nb),
            in_specs=[pl.BlockSpec((None, tm, cols), lambda s, i, c: (s, c[0] * nb + i, 0)),
                      pl.BlockSpec((None, tm, cols), lambda s, i, c: (s, i, 0))],
            out_specs=pl.BlockSpec((None, tm, cols), lambda s, i, c: (s, i, 0))),
        out_shape=jax.ShapeDtypeStruct((n, half, cols), LINK_DTYPE),
        compiler_params=_params("parallel", "parallel"),
    )(core, block, received)


def _sum_chips_block(parts, *, name):
    n, half, cols = parts.shape
    tm = _tile(half, 512, 2 * SUBLANES)

    def kern(p_ref, o_ref):
        acc = p_ref[0].astype(F32)
        for s in range(1, n):
            acc = acc + p_ref[s].astype(F32)
        o_ref[...] = acc

    return pl.pallas_call(
        kern, name=name, grid=(half // tm,),
        in_specs=[pl.BlockSpec((n, tm, cols), lambda i: (0, i, 0))],
        out_specs=pl.BlockSpec((tm, cols), lambda i: (i, 0)),
        out_shape=jax.ShapeDtypeStruct((half, cols), F32),
        compiler_params=_params("parallel"),
    )(parts)


def _scatter_start(sums, *, name):
    n = len(sums)
    lands = [lax.empty(a.shape, a.dtype) for a in sums]

    def body(*refs):
        s_refs, land_refs, send_sems, recv_sems, token = refs[:n], refs[n: 2 * n], refs[2 * n], refs[2 * n + 1], refs[-1]
        x, y, c = _mesh_pos()
        for b in range(n):
            for j, chip in enumerate(_other_chips(x, y)):
                pltpu.make_async_remote_copy(src_ref=s_refs[b].at[2 * chip[0] + chip[1]], dst_ref=land_refs[b].at[2 * x + y],
                                             send_sem=send_sems.at[3 * b + j], recv_sem=recv_sems.at[3 * b + j],
                                             device_id=(*chip, c), device_id_type=MESH).start()
        token[...] = jnp.zeros_like(token)

    operands = [pltpu.with_memory_space_constraint(a, pltpu.HBM) for a in sums + lands]
    res = pl.pallas_call(
        body, name=name, in_specs=[HBM] * (2 * n),
        out_shape=(pltpu.SemaphoreType.DMA((3 * n,)), pltpu.SemaphoreType.DMA((3 * n,)),
                   *[pltpu.HBM(a.shape, a.dtype) for a in operands], jax.ShapeDtypeStruct((SUBLANES, LANES), F32)),
        out_specs=(SEM, SEM, *[HBM] * (2 * n), pl.BlockSpec(memory_space=pltpu.VMEM)),
        input_output_aliases={i: 2 + i for i in range(2 * n)},
        compiler_params=pltpu.CompilerParams(has_side_effects=DATAFLOW),
    )(*operands)
    return res[0], res[1], list(res[2: 2 + n]), list(res[2 + n: 2 + 2 * n]), res[-1]


def _scatter_wait(started, after, *, name):
    send_sems, recv_sems, sums, lands, _ = started
    n = len(sums)

    def body(*refs):
        s_refs, land_refs, send_sems, recv_sems = refs[:n], refs[n: 2 * n], refs[2 * n], refs[2 * n + 1]
        x, y, c = _mesh_pos()
        for b in range(n):
            for j, chip in enumerate(_other_chips(x, y)):
                cp = pltpu.make_async_remote_copy(src_ref=s_refs[b].at[2 * x + y], dst_ref=land_refs[b].at[2 * chip[0] + chip[1]],
                                                  send_sem=send_sems.at[3 * b + j], recv_sem=recv_sems.at[3 * b + j],
                                                  device_id=(*chip, c), device_id_type=MESH)
                cp.wait_send()
                cp.wait_recv()

    res = pl.pallas_call(
        body, name=name, in_specs=(*[HBM] * (2 * n), SEM, SEM, ANY),
        out_shape=tuple(pltpu.HBM(a.shape, a.dtype) for a in sums + lands), out_specs=tuple([HBM] * (2 * n)),
        input_output_aliases={i: i for i in range(2 * n)},
        compiler_params=pltpu.CompilerParams(has_side_effects=DATAFLOW),
    )(*sums, *lands, send_sems, recv_sems, after)
    return list(res[:n]), list(res[n:])


def _reduce_begin(blocks, names, key):
    core = lax.axis_index("c").astype(jnp.int32).reshape(1)
    received = _swap_halves(blocks, _other_half_rows, name=f"rs_swap_{key}")
    sums = [_add_own_half_block(b, r, core, name=f"rs_add_{nm}") for b, r, nm in zip(blocks, received, names, strict=True)]
    return _scatter_start(sums, name=f"rs_scatter_start_{key}")


def _reduce_finish(begun, names, after):
    x, y, c = _mesh_pos()
    chip = 2 * x + y
    mine = {}
    for key, started in begun.items():
        sums, lands = _scatter_wait(started, after, name=f"rs_scatter_wait_{key}")
        parts = [lax.dynamic_update_slice(p, lax.dynamic_slice_in_dim(s, chip, 1, axis=0), (chip, 0, 0))
                 for p, s in zip(lands, sums, strict=True)]
        mine[key] = [_sum_chips_block(p, name=f"rs_sum_{nm}") for p, nm in zip(parts, names[key], strict=True)]
    flat = [a for key in begun for a in mine[key]]
    other = iter(_swap_halves(flat, _whole, name="rs_join"))
    return {key: [jnp.concatenate([jnp.where(c == 0, a, b), jnp.where(c == 0, b, a)], axis=0)
                  for a, b in ((a, next(other)) for a in mine[key])] for key in begun}


WEIGHTS = ("rel_bias", "att_w_in", "att_w_out", "dn_w_in", "dn_conv", "dn_a_log", "dn_dt_bias", "dn_out_norm", "dn_w_out",
           "mem_norm", "mem_w_kv", "norm_mix_pre", "norm_mix_post", "norm_ffn_pre", "norm_ffn_post", "ffn_w_gate_up",
           "ffn_w_down")
BIG_NAMES = tuple(n for n, _, _ in BIG)
SMALL_NAMES = tuple(n for n in WEIGHTS if n not in BIG_NAMES)
CONV_COLS = 3 * TOK_WIDTH
CONV_SHARD = CONV_COLS // N_CHIPS
BLOCKS = tuple((n, layer) for n, shape, _ in BIG for layer in range(shape[0]))
COLUMN_SHARDED = {n: axis == 2 for n, _, axis in BIG}


def kernel(x, mem, rel_bias, att_w_in, att_w_out, dn_w_in, dn_conv, dn_a_log, dn_dt_bias, dn_out_norm, dn_w_out, mem_norm, mem_w_kv, norm_mix_pre, norm_mix_post, norm_ffn_pre, norm_ffn_post, ffn_w_gate_up, ffn_w_down, loss_target, m_rel_bias, m_att_w_in, m_att_w_out, m_dn_w_in, m_dn_conv, m_dn_a_log, m_dn_dt_bias, m_dn_out_norm, m_dn_w_out, m_mem_norm, m_mem_w_kv, m_norm_mix_pre, m_norm_mix_post, m_norm_ffn_pre, m_norm_ffn_post, m_ffn_w_gate_up, m_ffn_w_down, v_rel_bias, v_att_w_in, v_att_w_out, v_dn_w_in, v_dn_conv, v_dn_a_log, v_dn_dt_bias, v_dn_out_norm, v_dn_w_out, v_mem_norm, v_mem_w_kv, v_norm_mix_pre, v_norm_mix_post, v_norm_ffn_pre, v_norm_ffn_post, v_ffn_w_gate_up, v_ffn_w_down):
    w = dict(zip(WEIGHTS, (rel_bias, att_w_in, att_w_out, dn_w_in, dn_conv, dn_a_log, dn_dt_bias, dn_out_norm, dn_w_out,
                           mem_norm, mem_w_kv, norm_mix_pre, norm_mix_post, norm_ffn_pre, norm_ffn_post, ffn_w_gate_up,
                           ffn_w_down)))
    m = dict(zip(WEIGHTS, (m_rel_bias, m_att_w_in, m_att_w_out, m_dn_w_in, m_dn_conv, m_dn_a_log, m_dn_dt_bias,
                           m_dn_out_norm, m_dn_w_out, m_mem_norm, m_mem_w_kv, m_norm_mix_pre, m_norm_mix_post,
                           m_norm_ffn_pre, m_norm_ffn_post, m_ffn_w_gate_up, m_ffn_w_down)))
    v = dict(zip(WEIGHTS, (v_rel_bias, v_att_w_in, v_att_w_out, v_dn_w_in, v_dn_conv, v_dn_a_log, v_dn_dt_bias,
                           v_dn_out_norm, v_dn_w_out, v_mem_norm, v_mem_w_kv, v_norm_mix_pre, v_norm_mix_post,
                           v_norm_ffn_pre, v_norm_ffn_post, v_ffn_w_gate_up, v_ffn_w_down)))
    cx, cy, cc = _mesh_pos()
    chip = 2 * cx + cy

    local = dict(zip(BLOCKS, lax.optimization_barrier([w[n][layer].astype(MXU_DTYPE) for n, layer in BLOCKS]), strict=True))

    def usable(block, got):
        got = lax.dynamic_update_slice(got, local[block][None], (chip, 0, 0))
        return got if COLUMN_SHARDED[block[0]] else got.reshape(-1, got.shape[-1])

    late = {"ffn0": [("ffn_w_gate_up", 0), ("ffn_w_down", 0)], "dn": [("dn_w_in", 0), ("dn_w_out", 0)],
            "ffn1": [("ffn_w_gate_up", 1), ("ffn_w_down", 1)]}
    first = [b for b in BLOCKS if all(b not in blks for blks in late.values())]
    first_got = _gather_blocks([local[b] for b in first], name="gather_weights")
    late_local, _ = lax.optimization_barrier(({k: [local[b] for b in blks] for k, blks in late.items()}, first_got[0]))
    started = {k: _gather_start(late_local[k], name=f"gather_start_{k}") for k in late}
    started_token = sum(s[4][0, 0] for s in started.values())

    def late_weights(key, after):
        lands = _gather_wait(started[key], after, name=f"gather_wait_{key}")
        return [usable(b, got) for b, got in zip(late[key], lands, strict=True)]

    def dn_weights(after):
        w_in, w_out = late_weights("dn", after)
        return _pad_dn_w_in(jnp.concatenate([w_in[s] for s in range(N_CHIPS)], axis=1)), w_out

    full = {}
    for b, got in zip(first, first_got, strict=True):
        full.setdefault(b[0], []).append(usable(b, got))
    conv_rows = _small_rows([(DN_CONV, CONV_COLS)])
    conv_mine = jnp.where(cc == 0, 1.0, 0.0) * w["dn_conv"][0]
    conv_placed = lax.dynamic_update_slice(jnp.zeros((DN_CONV, CONV_COLS), F32), conv_mine, (0, chip * CONV_SHARD))
    conv_full = _unpack_small(_all_reduce_small(_pack_small([conv_placed], conv_rows), name="gather_conv"),
                              [(DN_CONV, CONV_COLS)])[0]
    p = {
        "rel_bias": w["rel_bias"], "att_w_in": full["att_w_in"][0], "att_w_out": full["att_w_out"][0],
        "dn_conv": conv_full, "dn_a_log": w["dn_a_log"][0], "dn_dt_bias": w["dn_dt_bias"][0],
        "dn_out_norm": w["dn_out_norm"][0], "mem_norm": w["mem_norm"], "mem_w_kv": full["mem_w_kv"],
        "norm_mix_pre": w["norm_mix_pre"] + started_token,
        "norm_mix_post": w["norm_mix_post"], "norm_ffn_pre": w["norm_ffn_pre"], "norm_ffn_post": w["norm_ffn_post"],
        "ffn_weights": lambda layer, after: late_weights(f"ffn{layer}", after), "dn_weights": dn_weights,
    }

    def chip_blocks(n, a):
        if n == "dn_w_in":
            a = _unpad_dn_w_in(a)
        if a.ndim == 3:
            return a
        if COLUMN_SHARDED[n]:
            return a.reshape(a.shape[0], N_CHIPS, -1).transpose(1, 0, 2)
        return a.reshape(N_CHIPS, -1, a.shape[-1])

    begun, begun_blocks = {}, {}

    def grads_ready(key, layer_grads):
        begun_blocks[key] = list(layer_grads)
        begun[key] = _reduce_begin([chip_blocks(n, a) for (n, _), a in layer_grads.items()],
                                   [f"{n}{layer}" for n, layer in layer_grads], key)
        return begun[key][4][0, 0]

    p["grads_ready"] = grads_ready
    loss_cols, grad_x, g = _local_step(x[0], mem[0], loss_target[0], p)
    loss = lax.psum(jnp.sum(loss_cols), ("x", "y", "c"))
    grads_ready("att", {("att_w_in", 0): g["att_w_in"], ("att_w_out", 0): g["att_w_out"],
                        ("mem_w_kv", 0): g["mem_w_kv"][0], ("mem_w_kv", 1): g["mem_w_kv"][1]})
    finished = _reduce_finish(begun, {k: [f"{n}{layer}" for n, layer in blks] for k, blks in begun_blocks.items()}, grad_x)
    reduced = {b: r for k in begun for b, r in zip(begun_blocks[k], finished[k], strict=True)}
    grads = {n: jnp.concatenate([reduced[b] for b in BLOCKS if b[0] == n], axis=0).reshape(shape) for n, shape, _ in BIG}
    small_full_shapes = [(DN_CONV, CONV_COLS) if n == "dn_conv" else w[n].shape for n in SMALL_NAMES]
    small_sum = _all_reduce_small(_pack_small([g[n] for n in SMALL_NAMES], _small_rows(small_full_shapes)), name="reduce_small")
    for n, s in zip(SMALL_NAMES, _unpack_small(small_sum, small_full_shapes)):
        grads[n] = lax.dynamic_slice(s, (0, chip * CONV_SHARD), (DN_CONV, CONV_SHARD))[None] if n == "dn_conv" else s

    delta, new_m, new_v = {}, {}, {}
    for n in BIG_NAMES:
        shape = w[n].shape
        two_d = lambda a: a.reshape(-1, shape[-1])
        res = _adamw(two_d(w[n]), two_d(grads[n]), two_d(m[n]), two_d(v[n]), name=f"adamw_{n}")
        delta[n], new_m[n], new_v[n] = (r.reshape(shape) for r in res)
    small_shapes = [w[n].shape for n in SMALL_NAMES]
    rows = _small_rows(small_shapes)
    res = _adamw(*[_pack_small([d[n] for n in SMALL_NAMES], rows) for d in (w, grads, m, v)], name="adamw_small")
    for d, r in zip((delta, new_m, new_v), res):
        for n, a in zip(SMALL_NAMES, _unpack_small(r, small_shapes)):
            d[n] = a
    return (loss, grad_x[None], *[grads[n] for n in WEIGHTS], *[delta[n] for n in WEIGHTS],
            *[new_m[n] for n in WEIGHTS], *[new_v[n] for n in WEIGHTS])
```

```python
import functools
import math

import numpy as np
import jax
import jax.numpy as jnp
from jax import lax
from jax.experimental import pallas as pl
from jax.experimental.pallas import tpu as pltpu

F32 = jnp.float32
MXU_DTYPE = jnp.bfloat16
LINK_DTYPE = jnp.bfloat16
HI = lax.Precision.HIGHEST

EPS = 1e-6
NEG_INF = -1e30
LANES = 128
SUBLANES = 8
VMEM_LIMIT = 56 * 1024 * 1024

D_MODEL = 1024
TOK_WIDTH = 768
MEM_WIDTH = 256
MEM_LEN = 256
ATT_HEAD_DIM = 64
DILATIONS = (1, 4, 16)
HALF = 64
ATT_BQ = 128
ATT_W = ATT_BQ + 2 * HALF
REL_BUCKETS = 32
REL_MAX_DIST = 1024
DN_HEADS = 6
DN_HEAD_DIM = 128
DN_CONV = 5
DN_CHUNK = 128
D_FF = 2816
ATT_IN = 2560
DN_IN = 3352
DN_IN_PAD = 3456
N_GATES = 4 * DN_HEADS

ADAM_LR = 0.001
ADAM_B1 = 0.9
ADAM_B2 = 0.999
ADAM_EPS = 1e-08
ADAM_WD = 0.01
ADAM_STEP = 10


def _tile(n, target, align):
    if n <= target:
        return n
    t = (target // align) * align
    while t >= align:
        if n % t == 0:
            return t
        t -= align
    raise ValueError(f"no tile for {n} (target {target}, align {align})")


def _params(*sem):
    return pltpu.CompilerParams(dimension_semantics=sem, vmem_limit_bytes=VMEM_LIMIT)


def _mm(a, b, *, name, ta=False, tb=False, out_shards=None, tm=1408, tn=1408, tk=1408, out_dtype=F32):
    if ta:
        K, M = a.shape
    else:
        M, K = a.shape
    sharded_b = b.ndim == 3
    if sharded_b:
        n_sh, b_rows, b_cols = b.shape
        N, K2 = (b_rows, n_sh * b_cols) if tb else (n_sh * b_cols, b_rows)
    else:
        N, K2 = b.shape if tb else b.shape[::-1]
    assert K == K2, (a.shape, b.shape, ta, tb)
    tm = _tile(M, tm, LANES if ta else SUBLANES)
    tn = N // out_shards if out_shards else (b_cols if sharded_b and not tb else _tile(N, tn, LANES))
    tk = b_cols if sharded_b and tb else _tile(K, tk, LANES)
    nk = K // tk
    a_spec = pl.BlockSpec((tk, tm), lambda i, j, k: (k, i)) if ta else pl.BlockSpec((tm, tk), lambda i, j, k: (i, k))
    if sharded_b:
        b_spec = (pl.BlockSpec((None, tn, tk), lambda i, j, k: (k, j, 0)) if tb
                  else pl.BlockSpec((None, tk, tn), lambda i, j, k: (j, k, 0)))
    else:
        b_spec = pl.BlockSpec((tn, tk), lambda i, j, k: (j, k)) if tb else pl.BlockSpec((tk, tn), lambda i, j, k: (k, j))
    if out_shards:
        out_spec = pl.BlockSpec((None, tm, tn), lambda i, j, k: (j, i, 0))
        out_shape = jax.ShapeDtypeStruct((out_shards, M, tn), out_dtype)
    else:
        out_spec = pl.BlockSpec((tm, tn), lambda i, j, k: (i, j))
        out_shape = jax.ShapeDtypeStruct((M, N), out_dtype)
    dims = (((0 if ta else 1,), (1 if tb else 0,)), ((), ()))

    def kern(a_ref, b_ref, o_ref, acc_ref):
        k = pl.program_id(2)

        @pl.when(k == 0)
        def _():
            acc_ref[...] = jnp.zeros_like(acc_ref)

        acc_ref[...] += lax.dot_general(a_ref[...].astype(MXU_DTYPE), b_ref[...].astype(MXU_DTYPE), dims,
                                        preferred_element_type=F32)

        @pl.when(k == nk - 1)
        def _():
            o_ref[...] = acc_ref[...].astype(o_ref.dtype)

    return pl.pallas_call(
        kern, name=name, grid=(M // tm, N // tn, nk), in_specs=[a_spec, b_spec],
        out_specs=out_spec, out_shape=out_shape,
        scratch_shapes=[pltpu.VMEM((tm, tn), F32)],
        compiler_params=_params("parallel", "parallel", "arbitrary"),
    )(a, b)


def _shards_of(w):
    return w.shape[0] if w.ndim == 3 else None


def _col(arr, width, blk):
    return (arr, width, blk)


def _rowwise(body, rows, consts, out_rows, out_acc, *, tm, name):
    n_rows = (rows[0][0] if isinstance(rows[0], tuple) else rows[0]).shape[0]
    assert n_rows % tm == 0, (n_rows, tm)
    arrs, in_specs = [], []
    for r in rows:
        arr, width, blk = r if isinstance(r, tuple) else (r, r.shape[1], 0)
        assert arr.shape[0] == n_rows
        arrs.append(arr)
        in_specs.append(pl.BlockSpec((tm, width), functools.partial(lambda i, b: (i, b), b=blk)))
    for c in consts:
        arrs.append(c)
        in_specs.append(pl.BlockSpec(c.shape, functools.partial(lambda i, n: (0,) * n, n=c.ndim)))
    n_in, n_ro = len(arrs), len(out_rows)
    out_shape = [jax.ShapeDtypeStruct((n_rows, w), dt) for w, dt in out_rows]
    out_specs = [pl.BlockSpec((tm, w), lambda i: (i, 0)) for w, _ in out_rows]
    out_shape += [jax.ShapeDtypeStruct(s, F32) for s in out_acc]
    out_specs += [pl.BlockSpec(s, lambda i: (0, 0)) for s in out_acc]

    def kern(*refs):
        ro, ao = body(*[r[...] for r in refs[:n_in]])
        outs = refs[n_in:]
        for r, v in zip(outs[:n_ro], ro, strict=True):
            r[...] = v.astype(r.dtype)
        if out_acc:
            @pl.when(pl.program_id(0) == 0)
            def _():
                for r in outs[n_ro:]:
                    r[...] = jnp.zeros_like(r)

            for r, v in zip(outs[n_ro:], ao, strict=True):
                r[...] += v

    res = pl.pallas_call(
        kern, name=name, grid=(n_rows // tm,), in_specs=in_specs, out_specs=out_specs, out_shape=out_shape,
        compiler_params=_params("arbitrary" if out_acc else "parallel"),
    )(*arrs)
    return res


def _rms(x, gain):
    return x * lax.rsqrt(jnp.mean(x * x, axis=-1, keepdims=True) + EPS) * gain


def _silu(x):
    return x * jax.nn.sigmoid(x)


def _softplus(x):
    return jnp.maximum(x, 0.0) + jnp.log(1.0 + jnp.exp(-jnp.abs(x)))


def _dot_nt(a, b, precision=None):
    return lax.dot_general(a, b, (((1,), (1,)), ((), ())), preferred_element_type=F32, precision=precision)


def _dot_tn(a, b, precision=None):
    return lax.dot_general(a, b, (((0,), (0,)), ((), ())), preferred_element_type=F32, precision=precision)


def _dot(a, b, precision=None):
    return jnp.dot(a, b, preferred_element_type=F32, precision=precision)


def _pre_norm(x, gain, *, name):
    def body(x, g):
        return (_rms(x, g),), ()
    return _rowwise(body, [x], [gain], [(x.shape[1], MXU_DTYPE)], [], tm=_tile(x.shape[0], 512, 2 * SUBLANES), name=name)[0]


def _pre_norm_bwd(x, gain, dh, dx_other, *, name):
    def body(x, dh, dxo, g):
        _, vjp = jax.vjp(_rms, x, g)
        dx, dg = vjp(dh)
        return (dx + dxo,), (dg,)
    return _rowwise(body, [x, dh, dx_other], [gain], [(x.shape[1], F32)], [gain.shape], tm=512, name=name)


def _gain_bwd(x, gain, dh, *, name):
    def body(x, dh, g):
        _, vjp = jax.vjp(lambda g_: _rms(x, g_), g)
        return (), (vjp(dh)[0],)
    return _rowwise(body, [x, dh], [gain], [], [gain.shape], tm=_tile(x.shape[0], 512, SUBLANES), name=name)[0]


def _res_block(x_res, m, g_post, g_pre):
    x_new = x_res + _rms(m, g_post)
    return x_new, _rms(x_new, g_pre)


def _post_pre(x_res, m, g_post, g_pre, *, name):
    def body(x, m, gp, gq):
        return _res_block(x, m, gp, gq), ()
    d = x_res.shape[1]
    return _rowwise(body, [x_res, m], [g_post, g_pre], [(d, F32), (d, MXU_DTYPE)], [], tm=512, name=name)


def _post_pre_bwd(x_res, m, g_post, g_pre, dx_new, dh, *, name):
    def body(x, m, dxn, dh, gp, gq):
        _, vjp = jax.vjp(_res_block, x, m, gp, gq)
        dx, dm, dgp, dgq = vjp((dxn, dh))
        return (dx, dm), (dgp, dgq)
    d = x_res.shape[1]
    return _rowwise(body, [x_res, m, dx_new, dh], [g_post, g_pre], [(d, F32), (d, MXU_DTYPE)],
                    [g_post.shape, g_pre.shape], tm=256, name=name)


def _final_loss_bwd(x_res, m, g_post, target, *, name):
    d = x_res.shape[1]

    def loss_cols(x, m, g, t):
        err = x + _rms(m, g) - t
        return jnp.sum(err * err, axis=0, keepdims=True) * (0.5 / d)

    def body(x, m, t, g):
        cols, vjp = jax.vjp(lambda x_, m_, g_: loss_cols(x_, m_, g_, t), x, m, g)
        dx, dm, dg = vjp(jnp.ones_like(cols))
        return (dx, dm), (dg, cols)
    return _rowwise(body, [x_res, m, target], [g_post], [(d, F32), (d, MXU_DTYPE)], [g_post.shape, (1, d)], tm=256, name=name)


def _swiglu_act(gu, *, name):
    def body(gate, up):
        return (_silu(gate.astype(F32)) * up.astype(F32),), ()
    return _rowwise(body, [_col(gu, D_FF, 0), _col(gu, D_FF, 1)], [], [(D_FF, MXU_DTYPE)], [], tm=256, name=name)[0]


def _swiglu_act_bwd(gu, da, *, name):
    def body(gate, up, da):
        _, vjp = jax.vjp(lambda g, u: _silu(g) * u, gate.astype(F32), up.astype(F32))
        dg, du = vjp(da.astype(F32))
        return (jnp.concatenate([dg, du], axis=1),), ()
    return _rowwise(body, [_col(gu, D_FF, 0), _col(gu, D_FF, 1), da], [], [(2 * D_FF, MXU_DTYPE)], [], tm=256, name=name)[0]


def _lane_head_mask(width, head_dim, head):
    lane = lax.broadcasted_iota(jnp.int32, (1, width), 1)
    return (lane // head_dim) == head


def _mem_attn_heads(q4, k4, v4):
    logits = _bdot_nt(q4, k4)
    p = jnp.exp(logits - jnp.max(logits, axis=-1, keepdims=True))
    return _bdot(p / jnp.sum(p, axis=-1, keepdims=True), v4)


def _mem_heads(q_mem, kv):
    return _heads(q_mem * (ATT_HEAD_DIM ** -0.5), mask=True), _heads(kv[:, :MEM_WIDTH]), _heads(kv[:, MEM_WIDTH:])


def _mem_attn(q_mem, kv):
    return _join_heads(_mem_attn_heads(*_mem_heads(q_mem, kv)))


def _mem_attn_bwd(q_mem, kv, do):
    _, vjp = jax.vjp(_mem_attn_heads, *_mem_heads(q_mem, kv))
    dq4, dk4, dv4 = vjp(_heads(do, mask=True))
    return (_join_heads(dq4) * (ATT_HEAD_DIM ** -0.5),
            jnp.concatenate([dk4[0] + dk4[1], dk4[2] + dk4[3], dv4[0] + dv4[1], dv4[2] + dv4[3]], axis=1))


def _t5_bucket(rel):
    half = REL_BUCKETS // 2
    max_exact = half // 2
    n = np.abs(rel)
    large = max_exact + (np.log(np.maximum(n, 1) / max_exact) / math.log(REL_MAX_DIST / max_exact)
                         * (half - max_exact)).astype(np.int64)
    large = np.minimum(large, half - 1)
    return ((rel > 0) * half + np.where(n < max_exact, n, large)).astype(np.int32)


ATT_DIAGS = ATT_BQ + ATT_W - 1


def _bias_diag_onehot(dil):
    j = np.arange(ATT_DIAGS)
    tiles = []
    for off in (-HALF, 0, HALF):
        rel = j - (ATT_BQ - 1) - HALF - off
        hot = _t5_bucket(rel * dil)[:, None] == np.arange(REL_BUCKETS)[None, :]
        tiles.append(hot & (np.abs(rel) <= HALF)[:, None])
    return np.stack(tiles).astype(np.float32)


def _toeplitz(r):
    lead = r.shape[:-1]
    a = jnp.broadcast_to(r[..., None, :], lead + (ATT_BQ, ATT_DIAGS))
    a = jnp.pad(a, [(0, 0)] * len(lead) + [(0, 0), (0, 1)])
    a = a.reshape(lead + (ATT_BQ * (ATT_DIAGS + 1),))[..., : ATT_BQ * ATT_DIAGS].reshape(lead + (ATT_BQ, ATT_DIAGS))
    return a[..., ATT_BQ - 1: ATT_BQ - 1 + ATT_W]


def _bias_tiles(rel_bias, gi):
    heads = rel_bias[:, 4 * gi: 4 * gi + 4]
    diag = jnp.einsum('tnb,bh->thn', jnp.asarray(_bias_diag_onehot(DILATIONS[gi])), heads, precision=HI)
    return _toeplitz(diag)


def _bias_tiles_bwd(rel_bias, dtiles, gi):
    return jax.vjp(lambda rb: _bias_tiles(rb, gi), rel_bias)[1](dtiles)[0]


def _att_window(i, n_sub):
    start = jnp.clip(i * ATT_BQ - HALF, 0, n_sub - ATT_W)
    off = i * ATT_BQ - HALF - start
    return pl.multiple_of(start, HALF), off


def _att_valid(off):
    q = lax.broadcasted_iota(jnp.int32, (ATT_BQ, ATT_W), 0)
    kk = lax.broadcasted_iota(jnp.int32, (ATT_BQ, ATT_W), 1)
    return jnp.abs(kk - q - HALF - off) <= HALF


def _att_tile_id(i, nq):
    return jnp.where(i == 0, 0, jnp.where(i == nq - 1, 2, 1))


ATT_GROUP_HEADS = 4


def _heads(x, mask=False):
    out = []
    for p in range(2):
        pair = x[:, p * LANES: (p + 1) * LANES]
        for h in range(2):
            out.append(jnp.where(_lane_head_mask(LANES, ATT_HEAD_DIM, h), pair, 0.0) if mask else pair)
    return jnp.stack(out)


def _join_heads(x):
    first = _lane_head_mask(LANES, ATT_HEAD_DIM, 0)
    return jnp.concatenate([jnp.where(first, x[2 * p], x[2 * p + 1]) for p in range(2)], axis=1)


def _head_scalar(x):
    out = []
    for p in range(2):
        pair = x[:, p * LANES: (p + 1) * LANES]
        for h in range(2):
            out.append(jnp.max(jnp.where(_lane_head_mask(LANES, ATT_HEAD_DIM, h), pair, NEG_INF), axis=-1, keepdims=True))
    return jnp.stack(out)


def _bdot(a, b):
    return jnp.einsum('hqk,hkd->hqd', a, b, preferred_element_type=F32)


def _bdot_nt(a, b):
    return jnp.einsum('hqd,hkd->hqk', a, b, preferred_element_type=F32)


def _bdot_tn(a, b):
    return jnp.einsum('hqk,hqd->hkd', a, b, preferred_element_type=F32)


def _att_fwd(qkvm, bias, gi, *, name):
    dil = DILATIONS[gi]
    s_len = qkvm.shape[0]
    n_sub = s_len // dil
    nq = n_sub // ATT_BQ
    assert n_sub % ATT_BQ == 0 and n_sub >= ATT_W
    cols = qkvm.shape[1] // (2 * LANES)
    view = qkvm.reshape(n_sub, dil * qkvm.shape[1])

    def kern(q_ref, k_ref, v_ref, b_ref, o_ref, lse_ref):
        start, off = _att_window(pl.program_id(1), n_sub)
        valid = _att_valid(off)
        q4 = _heads(q_ref[...].astype(F32) * (ATT_HEAD_DIM ** -0.5), mask=True)
        k4 = _heads(k_ref[pl.ds(start, ATT_W), :].astype(F32))
        v4 = _heads(v_ref[pl.ds(start, ATT_W), :].astype(F32))
        s = jnp.where(valid, _bdot_nt(q4, k4) + b_ref[...], NEG_INF)
        mx = jnp.max(s, axis=-1, keepdims=True)
        p = jnp.exp(s - mx)
        den = jnp.sum(p, axis=-1, keepdims=True)
        o_ref[...] = _join_heads(_bdot(p, v4) / den)
        lse_ref[...] = _join_heads(jnp.broadcast_to(mx + jnp.log(den), (ATT_GROUP_HEADS, ATT_BQ, LANES)))

    def qkv_spec(which, full):
        shape = (n_sub, 2 * LANES) if full else (ATT_BQ, 2 * LANES)
        return pl.BlockSpec(shape, lambda r, i: (0 if full else i, r * cols + which * 3 + gi))

    out_spec = pl.BlockSpec((ATT_BQ, 2 * LANES), lambda r, i: (i, r))
    o, lse = pl.pallas_call(
        kern, name=name, grid=(dil, nq),
        in_specs=[qkv_spec(0, False), qkv_spec(1, True), qkv_spec(2, True),
                  pl.BlockSpec((None, ATT_GROUP_HEADS, ATT_BQ, ATT_W), lambda r, i: (_att_tile_id(i, nq), 0, 0, 0))],
        out_specs=[out_spec, out_spec],
        out_shape=[jax.ShapeDtypeStruct((n_sub, dil * 2 * LANES), F32)] * 2,
        compiler_params=_params("parallel", "arbitrary"),
    )(view, view, view, bias)
    return o.reshape(s_len, 2 * LANES), lse.reshape(s_len, 2 * LANES)


def _att_bwd(qkvm, bias, lse_tot, delta, dcat, gi, *, name):
    dil = DILATIONS[gi]
    s_len = qkvm.shape[0]
    n_sub = s_len // dil
    nq = n_sub // ATT_BQ
    cols = qkvm.shape[1] // (2 * LANES)
    dcols = dcat.shape[1] // (2 * LANES)
    view = qkvm.reshape(n_sub, dil * qkvm.shape[1])
    lse_v = lse_tot.reshape(n_sub, dil * 2 * LANES)
    delta_v = delta.reshape(n_sub, dil * 2 * LANES)
    dcat_v = dcat.reshape(n_sub, dil * dcat.shape[1])

    def kern(q_ref, k_ref, v_ref, b_ref, lse_ref, dl_ref, dm_ref, dq_ref, dk_ref, dv_ref, db_ref):
        r, i = pl.program_id(0), pl.program_id(1)
        start, off = _att_window(i, n_sub)
        valid = _att_valid(off)
        tile = _att_tile_id(i, nq)

        @pl.when(i == 0)
        def _():
            dk_ref[...] = jnp.zeros_like(dk_ref)
            dv_ref[...] = jnp.zeros_like(dv_ref)

        @pl.when((i == 0) & (r == 0))
        def _():
            db_ref[...] = jnp.zeros_like(db_ref)

        q4 = _heads(q_ref[...].astype(F32) * (ATT_HEAD_DIM ** -0.5), mask=True)
        k4 = _heads(k_ref[pl.ds(start, ATT_W), :].astype(F32))
        v4 = _heads(v_ref[pl.ds(start, ATT_W), :].astype(F32))
        dm4 = _heads(dm_ref[...], mask=True)
        s = jnp.where(valid, _bdot_nt(q4, k4) + b_ref[tile], NEG_INF)
        p = jnp.exp(s - _head_scalar(lse_ref[...]))
        ds = p * (_bdot_nt(dm4, v4) - _head_scalar(dl_ref[...]))
        dq_ref[...] = _join_heads(_bdot(ds, k4)) * (ATT_HEAD_DIM ** -0.5)
        dk4 = _bdot_tn(ds, q4)
        dv4 = _bdot_tn(p, dm4)
        dk_ref[pl.ds(start, ATT_W), :] += jnp.concatenate([dk4[0] + dk4[1], dk4[2] + dk4[3]], axis=1)
        dv_ref[pl.ds(start, ATT_W), :] += jnp.concatenate([dv4[0] + dv4[1], dv4[2] + dv4[3]], axis=1)
        db_ref[tile] += ds

    def qkv_spec(which, full):
        shape = (n_sub, 2 * LANES) if full else (ATT_BQ, 2 * LANES)
        return pl.BlockSpec(shape, lambda r, i: (0 if full else i, r * cols + which * 3 + gi))

    blk = pl.BlockSpec((ATT_BQ, 2 * LANES), lambda r, i: (i, r))
    full = pl.BlockSpec((n_sub, 2 * LANES), lambda r, i: (0, r))
    bias_spec = pl.BlockSpec(bias.shape, lambda r, i: (0, 0, 0, 0))
    sub = jax.ShapeDtypeStruct((n_sub, dil * 2 * LANES), F32)
    dq, dk, dv, db = pl.pallas_call(
        kern, name=name, grid=(dil, nq),
        in_specs=[qkv_spec(0, False), qkv_spec(1, True), qkv_spec(2, True), bias_spec, blk, blk,
                  pl.BlockSpec((ATT_BQ, 2 * LANES), lambda r, i: (i, r * dcols + gi))],
        out_specs=[blk, full, full, bias_spec],
        out_shape=[sub, sub, sub, jax.ShapeDtypeStruct(bias.shape, F32)],
        compiler_params=_params("arbitrary", "arbitrary"),
    )(view, view, view, bias, lse_v, delta_v, dcat_v)
    return dq.reshape(s_len, -1), dk.reshape(s_len, -1), dv.reshape(s_len, -1), db


def _att_combine(o_g, lse_g, qkvm, kv_mem, *, name):
    def body(o0, o1, o2, l0, l1, l2, qm, kv):
        mx = jnp.maximum(jnp.maximum(l0, l1), l2)
        tot = mx + jnp.log(jnp.exp(l0 - mx) + jnp.exp(l1 - mx) + jnp.exp(l2 - mx))
        mixed = [o * jnp.exp(l - tot) for o, l in ((o0, l0), (o1, l1), (o2, l2))]
        return (jnp.concatenate(mixed + [_mem_attn(qm.astype(F32), kv)], axis=1), tot), ()
    return _rowwise(body, list(o_g) + list(lse_g) + [_col(qkvm, MEM_WIDTH, (3 * TOK_WIDTH) // MEM_WIDTH)], [kv_mem],
                    [(D_MODEL, F32), (MEM_WIDTH, F32)], [], tm=256, name=name)


def _head_sum_matrix():
    a = np.arange(MEM_WIDTH)
    return jnp.asarray((a[:, None] // ATT_HEAD_DIM == a[None, :] // ATT_HEAD_DIM).astype(np.float32))


def _att_bwd_prep(cat, dcat, qkvm, kv_mem, *, name):
    def body(cat, dcat, qm, kv, hs):
        prod = cat * dcat
        summed = prod[:, 0:256] + prod[:, 256:512] + prod[:, 512:768]
        delta = _dot(summed, hs, precision=HI)
        dqm, dkv = _mem_attn_bwd(qm.astype(F32), kv, dcat[:, TOK_WIDTH:])
        return (delta, dqm), (dkv,)
    return _rowwise(body, [cat, dcat, _col(qkvm, MEM_WIDTH, (3 * TOK_WIDTH) // MEM_WIDTH)], [kv_mem, _head_sum_matrix()],
                    [(MEM_WIDTH, F32), (MEM_WIDTH, F32)], [kv_mem.shape], tm=256, name=name)


def _dn_conv_post(s, j):
    scale = jnp.where(j < DN_HEADS, DN_HEAD_DIM ** -0.5, 1.0)
    normed = s * lax.rsqrt(jnp.sum(s * s, axis=-1, keepdims=True) + EPS) * scale
    return jnp.where(j >= 2 * DN_HEADS, s, normed)


def _shift_rows(x, sh):
    n = x.shape[0]
    row = lax.broadcasted_iota(jnp.int32, (n, 1), 0)
    rolled = pltpu.roll(x, (-sh) % n, 0)
    return jnp.where((row + sh >= 0) & (row + sh < n), rolled, 0.0)


def _dn_conv_taps(x, w_ref):
    c = x * w_ref[pl.ds(DN_CONV // 2, 1), :]
    for jj in range(DN_CONV):
        if jj != DN_CONV // 2:
            c = c + _shift_rows(x, jj - DN_CONV // 2) * w_ref[pl.ds(jj, 1), :]
    return c


def _dn_conv_fwd(proj, conv_w, *, name):
    s_len = proj.shape[0]
    width = 3 * TOK_WIDTH

    def kern(x_ref, w_ref, o_ref):
        j = pl.program_id(0)
        o_ref[...] = _dn_conv_post(_silu(_dn_conv_taps(x_ref[...], w_ref)), j)

    return pl.pallas_call(
        kern, name=name, grid=(width // LANES,),
        in_specs=[pl.BlockSpec((s_len, LANES), lambda j: (0, j)), pl.BlockSpec((DN_CONV, LANES), lambda j: (0, j))],
        out_specs=pl.BlockSpec((s_len, LANES), lambda j: (0, j)),
        out_shape=jax.ShapeDtypeStruct((s_len, width), F32),
        compiler_params=_params("parallel"),
    )(proj, conv_w)


def _dn_conv_bwd(proj, conv_w, d_fwd, d_bwd, which, *, name):
    s_len = proj.shape[0]

    def kern(x_ref, w_ref, df_ref, db_ref, dx_ref, dw_ref):
        j = pl.program_id(0) + which * DN_HEADS
        x = x_ref[...]
        c = _dn_conv_taps(x, w_ref)
        _, vjp = jax.vjp(lambda c_: _dn_conv_post(_silu(c_), j), c)
        dc = vjp(df_ref[...] + db_ref[...])[0]
        dx = dc * w_ref[pl.ds(DN_CONV // 2, 1), :]
        for jj in range(DN_CONV):
            sh = jj - DN_CONV // 2
            if sh != 0:
                dx = dx + _shift_rows(dc, -sh) * w_ref[pl.ds(jj, 1), :]
            dw_ref[pl.ds(jj, 1), :] = jnp.sum(dc * _shift_rows(x, sh), axis=0, keepdims=True)
        dx_ref[...] = dx

    return pl.pallas_call(
        kern, name=name, grid=(DN_HEADS,),
        in_specs=[pl.BlockSpec((s_len, LANES), lambda j: (0, j + which * DN_HEADS)),
                  pl.BlockSpec((DN_CONV, LANES), lambda j: (0, j + which * DN_HEADS)),
                  pl.BlockSpec((s_len, LANES), lambda j: (0, j)),
                  pl.BlockSpec((s_len, LANES), lambda j: (0, j))],
        out_specs=[pl.BlockSpec((s_len, LANES), lambda j: (0, j)), pl.BlockSpec((DN_CONV, LANES), lambda j: (0, j))],
        out_shape=[jax.ShapeDtypeStruct((s_len, TOK_WIDTH), F32), jax.ShapeDtypeStruct((DN_CONV, TOK_WIDTH), F32)],
        compiler_params=_params("parallel"),
    )(proj, conv_w, d_fwd, d_bwd)


GATE_TM = 2 * DN_CHUNK


FWD_GATE_LANES = 2 * DN_HEADS


def _gate_constants():
    i = np.arange(GATE_TM)
    same = (i[:, None] // DN_CHUNK) == (i[None, :] // DN_CHUNK)
    cum_f = same & (i[None, :] <= i[:, None])
    cum_r = same & (i[None, :] >= i[:, None])
    return tuple(jnp.asarray(np.asarray(a, np.float32)) for a in (cum_f, cum_r, same))


def _gate_params(p):
    z = jnp.zeros((DN_HEADS,), F32)
    return jnp.concatenate([p[0], z, p[1], z, jnp.zeros((LANES - N_GATES,), F32)]).reshape(1, LANES)


def _gate_params_bwd(dp):
    return jnp.stack([dp[0, 0:DN_HEADS], dp[0, 2 * DN_HEADS: 3 * DN_HEADS]])


def _dn_gates(gate_in, a_cols, dt_cols, cum_f, cum_r, tot):
    g = -jnp.exp(a_cols) * _softplus(gate_in + dt_cols)
    fwd_lane = lax.broadcasted_iota(jnp.int32, (1, LANES), 1) < FWD_GATE_LANES
    gc = jnp.where(fwd_lane, _dot(cum_f, g, precision=HI), _dot(cum_r, g, precision=HI))
    return gc, _dot(tot, g, precision=HI), jax.nn.sigmoid(gate_in)


def _dn_gates_fwd(proj, a_cols, dt_cols, *, name):
    def body(gi, *consts):
        return _dn_gates(gi, *consts), ()
    return _rowwise(body, [_col(proj, LANES, DN_IN_PAD // LANES - 1)], [a_cols, dt_cols, *_gate_constants()],
                    [(LANES, F32)] * 3, [], tm=GATE_TM, name=name)


def _dn_gates_bwd(proj, a_cols, dt_cols, d_gates, *, name):
    def body(gi, gcf, gtf, bf, gcr, gtr, br, a, dt, *consts):
        _, vjp = jax.vjp(lambda gi_, a_, dt_: _dn_gates(gi_, a_, dt_, *consts), gi, a, dt)
        dgi, da, ddt = vjp((gcf + gcr, gtf + gtr, bf + br))
        return (dgi,), (da, ddt)
    return _rowwise(body, [_col(proj, LANES, DN_IN_PAD // LANES - 1), *d_gates[0], *d_gates[1]],
                    [a_cols, dt_cols, *_gate_constants()], [(LANES, F32)], [a_cols.shape, dt_cols.shape],
                    tm=GATE_TM, name=name)


INV_BASE = 8


def _block_id_equal(c, size):
    i = lax.broadcasted_iota(jnp.int32, (c, c), 0) // size
    j = lax.broadcasted_iota(jnp.int32, (c, c), 1) // size
    return (i == j).astype(F32)


def _unit_tri_inverse_impl(lmat):
    c = lmat.shape[0]
    eye = _block_id_equal(c, 1)
    same = _block_id_equal(c, INV_BASE)
    neg = -lmat * same
    inv = eye + neg
    power = neg
    for _ in range(int(math.log2(INV_BASE)) - 1):
        power = _dot(power, power)
        inv = inv + _dot(inv, power)
    size = INV_BASE
    while size < c:
        bigger = _block_id_equal(c, 2 * size)
        inv = inv - _dot(_dot(inv, lmat * (bigger - same)), inv)
        same, size = bigger, 2 * size
    resid = eye - _dot(eye + lmat, inv, precision=HI)
    return inv + _dot(inv, resid)


@jax.custom_vjp
def _unit_tri_inverse(lmat):
    return _unit_tri_inverse_impl(lmat)


def _unit_tri_inverse_fwd(lmat):
    inv = _unit_tri_inverse_impl(lmat)
    return inv, inv


def _unit_tri_inverse_bwd(inv, d_inv):
    return (-_dot_tn(inv, _dot_nt(d_inv, inv)),)


_unit_tri_inverse.defvjp(_unit_tri_inverse_fwd, _unit_tri_inverse_bwd)


def _dn_chunk(q, k, v, gates_t, gc_row, tot_row, beta_row, state, tri, inverse):
    c = q.shape[0]
    assert c == DN_HEAD_DIM
    eye = _block_id_equal(c, 1)

    def along_rows(x, pick):
        return jnp.broadcast_to(jnp.sum(x * pick, axis=0, keepdims=True), (c, c))

    gc_j = along_rows(gates_t[0], gc_row)
    gc = gc_j.T
    g_tot = along_rows(gates_t[1], tot_row)
    beta = along_rows(gates_t[2], beta_row).T
    decay = jnp.exp(jnp.where(tri > 0, gc - gc_j, NEG_INF))
    k_beta = k * beta
    inv = inverse((tri - eye) * (_dot_nt(k_beta, k) * decay))
    e_gc = jnp.exp(gc)
    u = _dot(inv, v * beta)
    w = _dot(inv, k_beta * e_gc)
    intra = tri * (_dot_nt(q, k) * decay)
    v_new = u - _dot(w, state)
    out = _dot(q * e_gc, state) + _dot(intra, v_new)
    state = state * jnp.exp(g_tot) + _dot_tn(k * jnp.exp(g_tot - gc), v_new)
    return out, state


def _dn_tri():
    i = np.arange(DN_CHUNK)
    tri = np.stack([(i[None, :] <= i[:, None]), (i[None, :] >= i[:, None])]).astype(np.float32)
    return jnp.asarray(np.repeat(tri, DN_HEADS, axis=0))


def _dn_gate_picks():
    picks = np.zeros((3, 2 * DN_HEADS, 2 * DN_CHUNK, 1), np.float32)
    for d in range(2):
        for h in range(DN_HEADS):
            alpha = d * DN_CHUNK + d * 2 * DN_HEADS + h
            picks[0, d * DN_HEADS + h, alpha] = 1.0
            picks[1, d * DN_HEADS + h, alpha] = 1.0
            picks[2, d * DN_HEADS + h, alpha + DN_HEADS] = 1.0
    return jnp.asarray(picks)


def _stack_chains(fwd_ref, rev_ref):
    return jnp.stack([r[:, _head_cols(h)] for r in (fwd_ref, rev_ref) for h in range(DN_HEADS)])


def _unstack_chains(val, fwd_ref, rev_ref):
    for d, r in enumerate((fwd_ref, rev_ref)):
        for h in range(DN_HEADS):
            r[:, _head_cols(h)] = val[d * DN_HEADS + h]


def _gates_transposed(fwd_refs, rev_refs):
    return jnp.stack([jnp.concatenate([f[...].T, r[...].T], axis=0) for f, r in zip(fwd_refs, rev_refs, strict=True)])


def _dn_row_spec(nc, col, reverse, width=TOK_WIDTH):
    return pl.BlockSpec((DN_CHUNK, width), lambda t: ((nc - 1 - t) if reverse else t, col))


def _dn_state_spec(nc, reverse):
    return pl.BlockSpec((None, DN_HEADS, DN_HEAD_DIM, DN_HEAD_DIM), lambda t: ((nc - 1 - t) if reverse else t, 0, 0, 0))


def _head_cols(h):
    return pl.ds(h * DN_HEAD_DIM, DN_HEAD_DIM)


def _const_spec(arr):
    return pl.BlockSpec(arr.shape, functools.partial(lambda t, n: (0,) * n, n=arr.ndim))


def _dn_chains(inverse):
    return jax.vmap(lambda q, k, v, gates_t, *rest: _dn_chunk(q, k, v, gates_t, *rest, inverse),
                    in_axes=(0, 0, 0, None, 0, 0, 0, 0, 0))


def _dn_scan_fwd(qkv, gates, *, name):
    s_len = qkv.shape[0]
    nc = s_len // DN_CHUNK
    tri, picks = _dn_tri(), _dn_gate_picks()

    def kern(*refs):
        ins, (tri_ref, pick_ref, of_ref, or_ref, sf_ref, sr_ref, state) = refs[:12], refs[12:]

        @pl.when(pl.program_id(0) == 0)
        def _():
            state[...] = jnp.zeros_like(state)

        entry = state[...]
        qkv_c = [_stack_chains(ins[i], ins[6 + i]) for i in range(3)]
        gates_t = _gates_transposed(ins[3:6], ins[9:12])
        out, new = _dn_chains(_unit_tri_inverse_impl)(*qkv_c, gates_t, pick_ref[0], pick_ref[1], pick_ref[2], entry, tri_ref[...])
        sf_ref[...] = entry[:DN_HEADS]
        sr_ref[...] = entry[DN_HEADS:]
        _unstack_chains(out, of_ref, or_ref)
        state[...] = new

    in_specs = []
    for rev in (False, True):
        in_specs += [_dn_row_spec(nc, col, rev) for col in (0, 1, 2)] + [_dn_row_spec(nc, 0, rev, LANES)] * 3
    in_specs += [_const_spec(tri), _const_spec(picks)]
    return pl.pallas_call(
        kern, name=name, grid=(nc,), in_specs=in_specs,
        out_specs=[_dn_row_spec(nc, 0, False), _dn_row_spec(nc, 0, True), _dn_state_spec(nc, False), _dn_state_spec(nc, True)],
        out_shape=[jax.ShapeDtypeStruct((s_len, TOK_WIDTH), F32)] * 2
        + [jax.ShapeDtypeStruct((nc, DN_HEADS, DN_HEAD_DIM, DN_HEAD_DIM), F32)] * 2,
        scratch_shapes=[pltpu.VMEM((2 * DN_HEADS, DN_HEAD_DIM, DN_HEAD_DIM), F32)],
        compiler_params=_params("arbitrary"),
    )(*([qkv, qkv, qkv, *gates] * 2), tri, picks)


def _dn_scan_bwd(qkv, gates, states, d_o, *, name):
    s_len = qkv.shape[0]
    nc = s_len // DN_CHUNK
    tri, picks = _dn_tri(), _dn_gate_picks()

    def kern(*refs):
        ins, tri_ref, pick_ref, outs, d_state = refs[:16], refs[16], refs[17], refs[18:30], refs[30]

        @pl.when(pl.program_id(0) == 0)
        def _():
            d_state[...] = jnp.zeros_like(d_state)

        qkv_c = [_stack_chains(ins[i], ins[8 + i]) for i in range(3)]
        gates_t = _gates_transposed(ins[3:6], ins[11:14])
        entry = jnp.concatenate([ins[6][...], ins[14][...]], axis=0)
        d_out = _stack_chains(ins[7], ins[15])
        tri_v, picks_v = tri_ref[...], pick_ref[...]
        _, vjp = jax.vjp(lambda q, k, v, g, s: _dn_chains(_unit_tri_inverse)(q, k, v, g, picks_v[0], picks_v[1], picks_v[2], s, tri_v),
                         *qkv_c, gates_t, entry)
        dq, dk, dv, d_gates_t, d_entry = vjp((d_out, d_state[...]))
        for i, val in enumerate((dq, dk, dv)):
            _unstack_chains(val, outs[i], outs[6 + i])
        for i in range(3):
            outs[3 + i][...] = d_gates_t[i, :DN_CHUNK].T
            outs[9 + i][...] = d_gates_t[i, DN_CHUNK:].T
        d_state[...] = d_entry

    in_specs, out_specs, out_shape = [], [], []
    for rev in (True, False):
        in_specs += [_dn_row_spec(nc, col, rev) for col in (0, 1, 2)] + [_dn_row_spec(nc, 0, rev, LANES)] * 3
        in_specs += [_dn_state_spec(nc, rev), _dn_row_spec(nc, 0, rev)]
        out_specs += [_dn_row_spec(nc, 0, rev)] * 3 + [_dn_row_spec(nc, 0, rev, LANES)] * 3
        out_shape += [jax.ShapeDtypeStruct((s_len, TOK_WIDTH), F32)] * 3 + [jax.ShapeDtypeStruct((s_len, LANES), F32)] * 3
    in_specs += [_const_spec(tri), _const_spec(picks)]
    res = pl.pallas_call(
        kern, name=name, grid=(nc,), in_specs=in_specs, out_specs=out_specs, out_shape=out_shape,
        scratch_shapes=[pltpu.VMEM((2 * DN_HEADS, DN_HEAD_DIM, DN_HEAD_DIM), F32)],
        compiler_params=_params("arbitrary"),
    )(*[a for d in range(2) for a in (qkv, qkv, qkv, *gates, states[d], d_o)], tri, picks)
    return (res[0:3], res[3:6]), (res[6:9], res[9:12])


def _dn_out_head(o_f, o_b, z, gain):
    o = o_f + o_b
    return o * lax.rsqrt(jnp.mean(o * o, axis=-1, keepdims=True) + EPS) * gain * _silu(z)


def _dn_out(o_fwd, o_rev, proj, gain, qkv_kv_mem, *, name):
    def body(of, ob, z, qm, g, kv):
        heads = []
        for h in range(DN_HEADS):
            sl = slice(h * DN_HEAD_DIM, (h + 1) * DN_HEAD_DIM)
            heads.append(_dn_out_head(of[:, sl], ob[:, sl], z[:, sl], g))
        return (jnp.concatenate(heads + [_mem_attn(qm, kv)], axis=1),), ()
    return _rowwise(body, [o_fwd, o_rev, _col(proj, TOK_WIDTH, 3),
                           _col(proj, MEM_WIDTH, (4 * TOK_WIDTH) // MEM_WIDTH)], [gain, qkv_kv_mem],
                    [(D_MODEL, MXU_DTYPE)], [], tm=256, name=name)[0]


def _dn_out_bwd(o_fwd, o_rev, proj, gain, kv_mem, dcat, *, name):
    def body(of, ob, z, qm, dcat, g, kv):
        dos, dzs = [], []
        dgain = jnp.zeros_like(g)
        for h in range(DN_HEADS):
            sl = slice(h * DN_HEAD_DIM, (h + 1) * DN_HEAD_DIM)
            _, vjp = jax.vjp(_dn_out_head, of[:, sl], ob[:, sl], z[:, sl], g)
            d_of, _, dz, dg = vjp(dcat[:, sl])
            dos.append(d_of)
            dzs.append(dz)
            dgain = dgain + dg
        dqm, dkv = _mem_attn_bwd(qm, kv, dcat[:, TOK_WIDTH:])
        return (jnp.concatenate(dos, axis=1), jnp.concatenate(dzs, axis=1), dqm), (dgain, dkv)
    return _rowwise(body, [o_fwd, o_rev, _col(proj, TOK_WIDTH, 3),
                           _col(proj, MEM_WIDTH, (4 * TOK_WIDTH) // MEM_WIDTH), dcat], [gain, kv_mem],
                    [(TOK_WIDTH, F32), (TOK_WIDTH, F32), (MEM_WIDTH, F32)], [gain.shape, kv_mem.shape], tm=256, name=name)


def _pad_dn_w_in(w):
    gates = w[:, 4 * TOK_WIDTH: 4 * TOK_WIDTH + N_GATES]
    zeros = jnp.zeros((w.shape[0], DN_IN_PAD - DN_IN), w.dtype)
    return jnp.concatenate([w[:, :4 * TOK_WIDTH], w[:, 4 * TOK_WIDTH + N_GATES:], gates, zeros], axis=1)


def _unpad_dn_w_in(w):
    q_mem = w[:, 4 * TOK_WIDTH: 4 * TOK_WIDTH + MEM_WIDTH]
    gates = w[:, 4 * TOK_WIDTH + MEM_WIDTH: 4 * TOK_WIDTH + MEM_WIDTH + N_GATES]
    return jnp.concatenate([w[:, :4 * TOK_WIDTH], gates, q_mem], axis=1)


def _ffn_fwd(h, w_gu, w_d, tag):
    gu = _mm(h, w_gu, out_dtype=MXU_DTYPE, name=f"ffn_gu_{tag}")
    act = _swiglu_act(gu, name=f"ffn_act_{tag}")
    return gu, act, _mm(act, w_d, name=f"ffn_down_{tag}")


def _ffn_bwd(h, gu, act, w_gu, w_d, df, tag):
    d_act = _mm(df, w_d, tb=True, out_dtype=MXU_DTYPE, name=f"ffn_dact_{tag}")
    d_wd = _mm(act, df, ta=True, name=f"ffn_dwd_{tag}")
    d_gu = _swiglu_act_bwd(gu, d_act, name=f"ffn_dgu_{tag}")
    dh = _mm(d_gu, w_gu, tb=True, name=f"ffn_dh_{tag}")
    d_wgu = _mm(h, d_gu, ta=True, out_shards=_shards_of(w_gu), name=f"ffn_dwgu_{tag}")
    return dh, d_wgu, d_wd


def _local_step(x, mem, target, p):
    g = {}
    row = lambda v: v.reshape(1, -1)
    gains = {k: [row(p[k][i]) for i in range(2)] for k in
             ("mem_norm", "norm_mix_pre", "norm_mix_post", "norm_ffn_pre", "norm_ffn_post")}
    out_gain = row(p["dn_out_norm"])
    a_cols, dt_cols = _gate_params(p["dn_a_log"]), _gate_params(p["dn_dt_bias"])

    h0 = _pre_norm(x, gains["norm_mix_pre"][0], name="pre0")
    mem_n = [_pre_norm(mem, gains["mem_norm"][i], name=f"mem_norm{i}") for i in range(2)]
    kv_mem = [_mm(mem_n[i], p["mem_w_kv"][i], name=f"mem_kv{i}") for i in range(2)]
    qkvm = _mm(h0, p["att_w_in"], out_dtype=MXU_DTYPE, name="att_in")
    bias = [_bias_tiles(p["rel_bias"], gi) for gi in range(3)]
    att = [_att_fwd(qkvm, bias[gi], gi, name=f"att_fwd{gi}") for gi in range(3)]
    cat0, lse_tot = _att_combine([a[0] for a in att], [a[1] for a in att], qkvm, kv_mem[0], name="att_combine")
    mo0 = _mm(cat0, p["att_w_out"], name="att_out")
    x1, h1 = _post_pre(x, mo0, gains["norm_mix_post"][0], gains["norm_ffn_pre"][0], name="post_mix0")
    w_gu0, w_d0 = p["ffn_weights"](0, h1)
    gu0, act0, f0 = _ffn_fwd(h1, w_gu0, w_d0, 0)
    x2, h2 = _post_pre(x1, f0, gains["norm_ffn_post"][0], gains["norm_mix_pre"][1], name="post_ffn0")

    dn_w_in, dn_w_out = p["dn_weights"](h2)
    proj = _mm(h2, dn_w_in, name="dn_in")
    qkv = _dn_conv_fwd(proj, p["dn_conv"], name="dn_conv")
    gates = _dn_gates_fwd(proj, a_cols, dt_cols, name="dn_gates")
    o_fwd, o_rev, st_fwd, st_rev = _dn_scan_fwd(qkv, gates, name="dn_scan")
    cat1 = _dn_out(o_fwd, o_rev, proj, out_gain, kv_mem[1], name="dn_outnorm")
    mo1 = _mm(cat1, dn_w_out, name="dn_out")
    x3, h3 = _post_pre(x2, mo1, gains["norm_mix_post"][1], gains["norm_ffn_pre"][1], name="post_mix1")
    w_gu1, w_d1 = p["ffn_weights"](1, h3)
    gu1, act1, f1 = _ffn_fwd(h3, w_gu1, w_d1, 1)

    dx3, df1, dg_ffn_post1, loss_cols = _final_loss_bwd(x3, f1, gains["norm_ffn_post"][1], target, name="loss_bwd")
    dh3, d_wgu1, d_wd1 = _ffn_bwd(h3, gu1, act1, w_gu1, w_d1, df1, 1)
    sent = p["grads_ready"]("ffn1", {("ffn_w_gate_up", 1): d_wgu1, ("ffn_w_down", 1): d_wd1})
    dx2, dmo1, dg_mix_post1, dg_ffn_pre1 = _post_pre_bwd(x2, mo1, gains["norm_mix_post"][1] + sent, gains["norm_ffn_pre"][1],
                                                         dx3, dh3, name="post_mix1_bwd")
    dcat1 = _mm(dmo1, dn_w_out, tb=True, name="dn_out_dx")
    g["dn_w_out"] = _mm(cat1, dmo1, ta=True, name="dn_out_dw")
    d_o, dz, dqm1, d_out_gain, dkv1 = _dn_out_bwd(o_fwd, o_rev, proj, out_gain, kv_mem[1], dcat1, name="dn_outnorm_bwd")
    (d_f, dg_f), (d_r, dg_r) = _dn_scan_bwd(qkv, gates, (st_fwd, st_rev), d_o, name="dn_scan_bwd")
    d_gate_cols, d_a_cols, d_dt_cols = _dn_gates_bwd(proj, a_cols, dt_cols, (dg_f, dg_r), name="dn_gates_bwd")
    d_pre, d_conv = zip(*[_dn_conv_bwd(proj, p["dn_conv"], d_f[which], d_r[which], which, name=f"dn_conv_bwd{which}")
                          for which in range(3)])
    dproj = jnp.concatenate(list(d_pre) + [dz, dqm1, d_gate_cols], axis=1).astype(MXU_DTYPE)
    dh2 = _mm(dproj, dn_w_in, tb=True, name="dn_in_dx")
    g["dn_w_in"] = _mm(h2, dproj, ta=True, name="dn_in_dw")
    g["dn_conv"] = jnp.concatenate(d_conv, axis=1)
    g["dn_a_log"] = _gate_params_bwd(d_a_cols)
    g["dn_dt_bias"] = _gate_params_bwd(d_dt_cols)
    g["dn_out_norm"] = d_out_gain

    d_mem_kv1 = _mm(mem_n[1], dkv1, ta=True, name="mem_kv_dw1")
    sent = p["grads_ready"]("dn", {("dn_w_in", 0): g["dn_w_in"], ("dn_w_out", 0): g["dn_w_out"], ("mem_w_kv", 1): d_mem_kv1})
    dx1, df0, dg_ffn_post0, dg_mix_pre1 = _post_pre_bwd(x1, f0, gains["norm_ffn_post"][0] + sent, gains["norm_mix_pre"][1],
                                                        dx2, dh2, name="post_ffn0_bwd")
    dh1, d_wgu0, d_wd0 = _ffn_bwd(h1, gu0, act0, w_gu0, w_d0, df0, 0)
    sent = p["grads_ready"]("ffn0", {("ffn_w_gate_up", 0): d_wgu0, ("ffn_w_down", 0): d_wd0})
    dx0, dmo0, dg_mix_post0, dg_ffn_pre0 = _post_pre_bwd(x, mo0, gains["norm_mix_post"][0] + sent, gains["norm_ffn_pre"][0],
                                                         dx1, dh1, name="post_mix0_bwd")
    dcat0 = _mm(dmo0, p["att_w_out"], tb=True, name="att_out_dx")
    g["att_w_out"] = _mm(cat0, dmo0, ta=True, name="att_out_dw")
    delta, dqm0, dkv0 = _att_bwd_prep(cat0, dcat0, qkvm, kv_mem[0], name="att_bwd_prep")
    d_mem_kv0 = _mm(mem_n[0], dkv0, ta=True, name="mem_kv_dw0")
    sent = p["grads_ready"]("att_out", {("att_w_out", 0): g["att_w_out"], ("mem_w_kv", 0): d_mem_kv0})
    att_b = [_att_bwd(qkvm, bias[gi] + sent, lse_tot, delta, dcat0, gi, name=f"att_bwd{gi}") for gi in range(3)]
    dqkvm = jnp.concatenate([a[w] for w in range(3) for a in att_b] + [dqm0], axis=1).astype(MXU_DTYPE)
    g["rel_bias"] = sum(_bias_tiles_bwd(p["rel_bias"], att_b[gi][3], gi) for gi in range(3))
    dh0 = _mm(dqkvm, p["att_w_in"], tb=True, name="att_in_dx")
    g["att_w_in"] = _mm(h0, dqkvm, ta=True, out_shards=_shards_of(p["att_w_in"]), name="att_in_dw")
    grad_x, dg_mix_pre0 = _pre_norm_bwd(x, gains["norm_mix_pre"][0], dh0, dx0, name="pre0_bwd")

    d_mem_norm = []
    for i, dkv in enumerate((dkv0, dkv1)):
        d_mem_n = _mm(dkv, p["mem_w_kv"][i], tb=True, name=f"mem_kv_dx{i}")
        d_mem_norm.append(_gain_bwd(mem, gains["mem_norm"][i], d_mem_n, name=f"mem_norm_bwd{i}"))
    g["mem_w_kv"] = [d_mem_kv0, d_mem_kv1]
    g["mem_norm"] = jnp.concatenate(d_mem_norm, axis=0)
    g["norm_mix_pre"] = jnp.concatenate([dg_mix_pre0, dg_mix_pre1], axis=0)
    g["norm_mix_post"] = jnp.concatenate([dg_mix_post0, dg_mix_post1], axis=0)
    g["norm_ffn_pre"] = jnp.concatenate([dg_ffn_pre0, dg_ffn_pre1], axis=0)
    g["norm_ffn_post"] = jnp.concatenate([dg_ffn_post0, dg_ffn_post1], axis=0)
    g["ffn_w_gate_up"] = [d_wgu0, d_wgu1]
    g["ffn_w_down"] = [d_wd0, d_wd1]
    return loss_cols, grad_x, g


N_CHIPS = 4
N_DEV = 8
MESH = pl.DeviceIdType.MESH
BIG = (("att_w_in", (1, 1024, 640), 2), ("att_w_out", (1, 256, 1024), 1), ("dn_w_in", (1, 1024, 838), 2),
       ("dn_w_out", (1, 256, 1024), 1), ("mem_w_kv", (2, 256, 512), 1), ("ffn_w_gate_up", (2, 1024, 1408), 2),
       ("ffn_w_down", (2, 704, 1024), 1))


def _mesh_pos():
    return lax.axis_index("x"), lax.axis_index("y"), lax.axis_index("c")


def _other_chips(x, y):
    return [(1 - x, y), (x, 1 - y), (1 - x, 1 - y)]


ANY = pl.BlockSpec(memory_space=pl.ANY)


def _all_reduce_small(v, *, name):
    rows, cols = v.shape
    flips = [(dx, dy, dc) for dx in (0, 1) for dy in (0, 1) for dc in (0, 1)][1:]

    def body(v_ref, o_ref, buf, send_sems, recv_sems):
        x, y, c = _mesh_pos()

        def peer(f):
            return tuple(1 - p if fl else p for p, fl in zip((x, y, c), f))

        def index(p):
            return 4 * p[0] + 2 * p[1] + p[2]

        buf[index((x, y, c))] = v_ref[...]
        sends = []
        for k, f in enumerate(flips):
            cp = pltpu.make_async_remote_copy(src_ref=v_ref, dst_ref=buf.at[index((x, y, c))], send_sem=send_sems.at[k],
                                              recv_sem=recv_sems.at[k], device_id=peer(f), device_id_type=MESH)
            cp.start()
            sends.append(cp)
        for k, f in enumerate(flips):
            pltpu.make_async_remote_copy(src_ref=v_ref, dst_ref=buf.at[index(peer(f))], send_sem=send_sems.at[k],
                                         recv_sem=recv_sems.at[k], device_id=peer(f), device_id_type=MESH).wait_recv()
        for cp in sends:
            cp.wait_send()
        acc = buf[0]
        for d in range(1, N_DEV):
            acc = acc + buf[d]
        o_ref[...] = acc

    vmem = pl.BlockSpec(memory_space=pltpu.VMEM)
    return pl.pallas_call(
        body, name=name, in_specs=[vmem], out_specs=vmem, out_shape=jax.ShapeDtypeStruct((rows, cols), F32),
        scratch_shapes=[pltpu.VMEM((N_DEV, rows, cols), F32), pltpu.SemaphoreType.DMA((N_DEV - 1,)),
                        pltpu.SemaphoreType.DMA((N_DEV - 1,))],
    )(v)


def _adamw(w, g, m, v, *, name):
    def body(w, g, m, v):
        m = ADAM_B1 * m + (1.0 - ADAM_B1) * g
        v = ADAM_B2 * v + (1.0 - ADAM_B2) * (g * g)
        m_hat = m / (1.0 - ADAM_B1 ** ADAM_STEP)
        v_hat = v / (1.0 - ADAM_B2 ** ADAM_STEP)
        delta = -ADAM_LR * (m_hat / (jnp.sqrt(v_hat) + ADAM_EPS) + ADAM_WD * w)
        return (delta, m, v), ()
    rows, cols = w.shape
    if rows % SUBLANES == 0:
        return _rowwise(body, [w, g, m, v], [], [(cols, F32)] * 3, [], tm=_tile(rows, 256, SUBLANES), name=name)

    def kern(*refs):
        outs, _ = body(*[r[...] for r in refs[:4]])
        for r, val in zip(refs[4:], outs, strict=True):
            r[...] = val

    spec = pl.BlockSpec((rows, _tile(cols, 256, LANES)), lambda j: (0, j))
    return pl.pallas_call(
        kern, name=name, grid=(cols // spec.block_shape[1],), in_specs=[spec] * 4, out_specs=[spec] * 3,
        out_shape=[jax.ShapeDtypeStruct((rows, cols), F32)] * 3, compiler_params=_params("parallel"),
    )(w, g, m, v)


def _pack_small(arrs, rows):
    flat = jnp.concatenate([a.reshape(-1) for a in arrs])
    return jnp.pad(flat, (0, rows * LANES - flat.shape[0])).reshape(rows, LANES)


def _unpack_small(packed, shapes):
    flat = packed.reshape(-1)
    out, off = [], 0
    for s in shapes:
        size = math.prod(s)
        out.append(flat[off: off + size].reshape(s))
        off += size
    return out


def _small_rows(shapes):
    return -(-sum(math.prod(s) for s in shapes) // (SUBLANES * LANES)) * SUBLANES


def _sem_pairs(n):
    return [pltpu.SemaphoreType.DMA((n,)), pltpu.SemaphoreType.DMA((n,))]


def _gather_blocks(blocks, *, name):
    n = len(blocks)

    def body(*refs):
        x_refs, out_refs, (send_sems, recv_sems) = refs[:n], refs[n: 2 * n], refs[2 * n:]
        x, y, c = _mesh_pos()
        sibling = (x, y, 1 - c)
        chips = _other_chips(x, y)

        def copy(k, src, dst, to):
            return pltpu.make_async_remote_copy(src_ref=src, dst_ref=dst, send_sem=send_sems.at[k],
                                                recv_sem=recv_sems.at[k], device_id=to, device_id_type=MESH)

        def part(b, chip, h):
            half = blocks[b].shape[0] // 2
            return out_refs[b].at[2 * chip[0] + chip[1], pl.ds(h * half, half), :]

        def my_half(b):
            half = blocks[b].shape[0] // 2
            return x_refs[b].at[pl.ds(c * half, half), :]

        first = [copy(6 * b + j, my_half(b), part(b, (x, y), c), (*chip, c)) for b in range(n) for j, chip in enumerate(chips)]
        for cp in first:
            cp.start()
        passed = []
        for b in range(n):
            for j, chip in enumerate(chips):
                copy(6 * b + j, my_half(b), part(b, chip, c), (*chip, c)).wait_recv()
                cp = copy(6 * b + 3 + j, part(b, chip, c), part(b, chip, c), sibling)
                cp.start()
                passed.append(cp)
        for b in range(n):
            for j, chip in enumerate(chips):
                copy(6 * b + 3 + j, part(b, chip, 1 - c), part(b, chip, 1 - c), sibling).wait_recv()
        for cp in first + passed:
            cp.wait_send()

    return pl.pallas_call(
        body, name=name, in_specs=[ANY] * n, out_specs=[ANY] * n,
        out_shape=[jax.ShapeDtypeStruct((N_CHIPS, *a.shape), a.dtype) for a in blocks],
        scratch_shapes=_sem_pairs(6 * n),
    )(*blocks)


HBM = pl.BlockSpec(memory_space=pltpu.HBM)
SEM = pl.BlockSpec(memory_space=pltpu.SEMAPHORE)
DATAFLOW = pltpu.SideEffectType.DATAFLOW_SIDE_EFFECTING


def _gather_start(blocks, *, name):
    n = len(blocks)
    lands = [lax.empty((N_CHIPS, *a.shape), a.dtype) for a in blocks]

    def body(*refs):
        x_refs, land_refs, send_sems, recv_sems, token = refs[:n], refs[n: 2 * n], refs[2 * n], refs[2 * n + 1], refs[-1]
        x, y, c = _mesh_pos()
        for b in range(n):
            for j, chip in enumerate(_other_chips(x, y)):
                pltpu.make_async_remote_copy(src_ref=x_refs[b], dst_ref=land_refs[b].at[2 * x + y], send_sem=send_sems.at[3 * b + j],
                                             recv_sem=recv_sems.at[3 * b + j], device_id=(*chip, c), device_id_type=MESH).start()
        token[...] = jnp.zeros_like(token)

    operands = [pltpu.with_memory_space_constraint(a, pltpu.HBM) for a in blocks + lands]
    res = pl.pallas_call(
        body, name=name, in_specs=[HBM] * (2 * n),
        out_shape=(pltpu.SemaphoreType.DMA((3 * n,)), pltpu.SemaphoreType.DMA((3 * n,)),
                   *[pltpu.HBM(a.shape, a.dtype) for a in operands], jax.ShapeDtypeStruct((SUBLANES, LANES), F32)),
        out_specs=(SEM, SEM, *[HBM] * (2 * n), pl.BlockSpec(memory_space=pltpu.VMEM)),
        input_output_aliases={i: 2 + i for i in range(2 * n)},
        compiler_params=pltpu.CompilerParams(has_side_effects=DATAFLOW),
    )(*operands)
    return res[0], res[1], list(res[2: 2 + n]), list(res[2 + n: 2 + 2 * n]), res[-1]


def _gather_wait(started, after, *, name):
    send_sems, recv_sems, blocks, lands, _ = started
    n = len(blocks)

    def body(*refs):
        x_refs, land_refs, send_sems, recv_sems = refs[:n], refs[n: 2 * n], refs[2 * n], refs[2 * n + 1]
        x, y, c = _mesh_pos()
        for b in range(n):
            for j, chip in enumerate(_other_chips(x, y)):
                cp = pltpu.make_async_remote_copy(src_ref=x_refs[b], dst_ref=land_refs[b].at[2 * chip[0] + chip[1]],
                                                  send_sem=send_sems.at[3 * b + j], recv_sem=recv_sems.at[3 * b + j],
                                                  device_id=(*chip, c), device_id_type=MESH)
                cp.wait_send()
                cp.wait_recv()

    res = pl.pallas_call(
        body, name=name, in_specs=(*[HBM] * (2 * n), SEM, SEM, ANY),
        out_shape=tuple(pltpu.HBM(a.shape, a.dtype) for a in blocks + lands), out_specs=tuple([HBM] * (2 * n)),
        input_output_aliases={i: i for i in range(2 * n)},
        compiler_params=pltpu.CompilerParams(has_side_effects=DATAFLOW),
    )(*blocks, *lands, send_sems, recv_sems, after)
    return list(res[n:])


def _swap_halves(blocks, own_rows, *, name):
    n = len(blocks)

    def body(*refs):
        in_refs, out_refs, (send_sems, recv_sems) = refs[:n], refs[n: 2 * n], refs[2 * n:]
        x, y, c = _mesh_pos()
        copies = [pltpu.make_async_remote_copy(src_ref=own_rows(in_refs[b], c), dst_ref=out_refs[b], send_sem=send_sems.at[b],
                                               recv_sem=recv_sems.at[b], device_id=(x, y, 1 - c), device_id_type=MESH)
                  for b in range(n)]
        for cp in copies:
            cp.start()
        for cp in copies:
            cp.wait()

    def sent_shape(a):
        return jax.eval_shape(lambda r: own_rows(r, 0), a)

    return pl.pallas_call(
        body, name=name, in_specs=[ANY] * n, out_specs=[ANY] * n,
        out_shape=[jax.ShapeDtypeStruct(sent_shape(a).shape, a.dtype) for a in blocks],
        scratch_shapes=_sem_pairs(n),
    )(*blocks)


def _other_half_rows(ref, c):
    half = ref.shape[1] // 2
    return ref[:, (1 - c) * half: (2 - c) * half, :] if isinstance(c, int) else ref.at[:, pl.ds((1 - c) * half, half), :]


def _whole(ref, c):
    return ref


def _add_own_half_block(block, received, core, *, name):
    n, rows, cols = block.shape
    half = rows // 2
    tm = _tile(half, 512, 2 * SUBLANES)
    nb = half // tm

    def kern(c_ref, a_ref, b_ref, o_ref):
        o_ref[...] = (a_ref[...] + b_ref[...]).astype(o_ref.dtype)

    return pl.pallas_call(
        kern, name=name,
        grid_spec=pltpu.PrefetchScalarGridSpec(
            num_scalar_prefetch=1, grid=(n, nb),
            in_specs=[pl.BlockSpec((None, tm, cols), lambda s, i, c: (s, c[0] * nb + i, 0)),
                      pl.BlockSpec((None, tm, cols), lambda s, i, c: (s, i, 0))],
            out_specs=pl.BlockSpec((None, tm, cols), lambda s, i, c: (s, i, 0))),
        out_shape=jax.ShapeDtypeStruct((n, half, cols), LINK_DTYPE),
        compiler_params=_params("parallel", "parallel"),
    )(core, block, received)


def _sum_chips_block(parts, *, name):
    n, half, cols = parts.shape
    tm = _tile(half, 512, 2 * SUBLANES)

    def kern(p_ref, o_ref):
        acc = p_ref[0].astype(F32)
        for s in range(1, n):
            acc = acc + p_ref[s].astype(F32)
        o_ref[...] = acc

    return pl.pallas_call(
        kern, name=name, grid=(half // tm,),
        in_specs=[pl.BlockSpec((n, tm, cols), lambda i: (0, i, 0))],
        out_specs=pl.BlockSpec((tm, cols), lambda i: (i, 0)),
        out_shape=jax.ShapeDtypeStruct((half, cols), F32),
        compiler_params=_params("parallel"),
    )(parts)


def _scatter_start(sums, *, name):
    n = len(sums)
    lands = [lax.empty(a.shape, a.dtype) for a in sums]

    def body(*refs):
        s_refs, land_refs, send_sems, recv_sems, token = refs[:n], refs[n: 2 * n], refs[2 * n], refs[2 * n + 1], refs[-1]
        x, y, c = _mesh_pos()
        for b in range(n):
            for j, chip in enumerate(_other_chips(x, y)):
                pltpu.make_async_remote_copy(src_ref=s_refs[b].at[2 * chip[0] + chip[1]], dst_ref=land_refs[b].at[2 * x + y],
                                             send_sem=send_sems.at[3 * b + j], recv_sem=recv_sems.at[3 * b + j],
                                             device_id=(*chip, c), device_id_type=MESH).start()
        token[...] = jnp.zeros_like(token)

    operands = [pltpu.with_memory_space_constraint(a, pltpu.HBM) for a in sums + lands]
    res = pl.pallas_call(
        body, name=name, in_specs=[HBM] * (2 * n),
        out_shape=(pltpu.SemaphoreType.DMA((3 * n,)), pltpu.SemaphoreType.DMA((3 * n,)),
                   *[pltpu.HBM(a.shape, a.dtype) for a in operands], jax.ShapeDtypeStruct((SUBLANES, LANES), F32)),
        out_specs=(SEM, SEM, *[HBM] * (2 * n), pl.BlockSpec(memory_space=pltpu.VMEM)),
        input_output_aliases={i: 2 + i for i in range(2 * n)},
        compiler_params=pltpu.CompilerParams(has_side_effects=DATAFLOW),
    )(*operands)
    return res[0], res[1], list(res[2: 2 + n]), list(res[2 + n: 2 + 2 * n]), res[-1]


def _scatter_wait(started, after, *, name):
    send_sems, recv_sems, sums, lands, _ = started
    n = len(sums)

    def body(*refs):
        s_refs, land_refs, send_sems, recv_sems = refs[:n], refs[n: 2 * n], refs[2 * n], refs[2 * n + 1]
        x, y, c = _mesh_pos()
        for b in range(n):
            for j, chip in enumerate(_other_chips(x, y)):
                cp = pltpu.make_async_remote_copy(src_ref=s_refs[b].at[2 * x + y], dst_ref=land_refs[b].at[2 * chip[0] + chip[1]],
                                                  send_sem=send_sems.at[3 * b + j], recv_sem=recv_sems.at[3 * b + j],
                                                  device_id=(*chip, c), device_id_type=MESH)
                cp.wait_send()
                cp.wait_recv()

    res = pl.pallas_call(
        body, name=name, in_specs=(*[HBM] * (2 * n), SEM, SEM, ANY),
        out_shape=tuple(pltpu.HBM(a.shape, a.dtype) for a in sums + lands), out_specs=tuple([HBM] * (2 * n)),
        input_output_aliases={i: i for i in range(2 * n)},
        compiler_params=pltpu.CompilerParams(has_side_effects=DATAFLOW),
    )(*sums, *lands, send_sems, recv_sems, after)
    return list(res[:n]), list(res[n:])


def _reduce_begin(blocks, names, key):
    core = lax.axis_index("c").astype(jnp.int32).reshape(1)
    received = _swap_halves(blocks, _other_half_rows, name=f"rs_swap_{key}")
    sums = [_add_own_half_block(b, r, core, name=f"rs_add_{nm}") for b, r, nm in zip(blocks, received, names, strict=True)]
    return _scatter_start(sums, name=f"rs_scatter_start_{key}")


def _reduce_finish(begun, names, after):
    x, y, c = _mesh_pos()
    chip = 2 * x + y
    mine = {}
    for key, started in begun.items():
        sums, lands = _scatter_wait(started, after, name=f"rs_scatter_wait_{key}")
        parts = [lax.dynamic_update_slice(p, lax.dynamic_slice_in_dim(s, chip, 1, axis=0), (chip, 0, 0))
                 for p, s in zip(lands, sums, strict=True)]
        mine[key] = [_sum_chips_block(p, name=f"rs_sum_{nm}") for p, nm in zip(parts, names[key], strict=True)]
    flat = [a for key in begun for a in mine[key]]
    other = iter(_swap_halves(flat, _whole, name="rs_join"))
    return {key: [jnp.concatenate([jnp.where(c == 0, a, b), jnp.where(c == 0, b, a)], axis=0)
                  for a, b in ((a, next(other)) for a in mine[key])] for key in begun}


WEIGHTS = ("rel_bias", "att_w_in", "att_w_out", "dn_w_in", "dn_conv", "dn_a_log", "dn_dt_bias", "dn_out_norm", "dn_w_out",
           "mem_norm", "mem_w_kv", "norm_mix_pre", "norm_mix_post", "norm_ffn_pre", "norm_ffn_post", "ffn_w_gate_up",
           "ffn_w_down")
BIG_NAMES = tuple(n for n, _, _ in BIG)
SMALL_NAMES = tuple(n for n in WEIGHTS if n not in BIG_NAMES)
CONV_COLS = 3 * TOK_WIDTH
CONV_SHARD = CONV_COLS // N_CHIPS
BLOCKS = tuple((n, layer) for n, shape, _ in BIG for layer in range(shape[0]))
COLUMN_SHARDED = {n: axis == 2 for n, _, axis in BIG}


def kernel(x, mem, rel_bias, att_w_in, att_w_out, dn_w_in, dn_conv, dn_a_log, dn_dt_bias, dn_out_norm, dn_w_out, mem_norm, mem_w_kv, norm_mix_pre, norm_mix_post, norm_ffn_pre, norm_ffn_post, ffn_w_gate_up, ffn_w_down, loss_target, m_rel_bias, m_att_w_in, m_att_w_out, m_dn_w_in, m_dn_conv, m_dn_a_log, m_dn_dt_bias, m_dn_out_norm, m_dn_w_out, m_mem_norm, m_mem_w_kv, m_norm_mix_pre, m_norm_mix_post, m_norm_ffn_pre, m_norm_ffn_post, m_ffn_w_gate_up, m_ffn_w_down, v_rel_bias, v_att_w_in, v_att_w_out, v_dn_w_in, v_dn_conv, v_dn_a_log, v_dn_dt_bias, v_dn_out_norm, v_dn_w_out, v_mem_norm, v_mem_w_kv, v_norm_mix_pre, v_norm_mix_post, v_norm_ffn_pre, v_norm_ffn_post, v_ffn_w_gate_up, v_ffn_w_down):
    w = dict(zip(WEIGHTS, (rel_bias, att_w_in, att_w_out, dn_w_in, dn_conv, dn_a_log, dn_dt_bias, dn_out_norm, dn_w_out,
                           mem_norm, mem_w_kv, norm_mix_pre, norm_mix_post, norm_ffn_pre, norm_ffn_post, ffn_w_gate_up,
                           ffn_w_down)))
    m = dict(zip(WEIGHTS, (m_rel_bias, m_att_w_in, m_att_w_out, m_dn_w_in, m_dn_conv, m_dn_a_log, m_dn_dt_bias,
                           m_dn_out_norm, m_dn_w_out, m_mem_norm, m_mem_w_kv, m_norm_mix_pre, m_norm_mix_post,
                           m_norm_ffn_pre, m_norm_ffn_post, m_ffn_w_gate_up, m_ffn_w_down)))
    v = dict(zip(WEIGHTS, (v_rel_bias, v_att_w_in, v_att_w_out, v_dn_w_in, v_dn_conv, v_dn_a_log, v_dn_dt_bias,
                           v_dn_out_norm, v_dn_w_out, v_mem_norm, v_mem_w_kv, v_norm_mix_pre, v_norm_mix_post,
                           v_norm_ffn_pre, v_norm_ffn_post, v_ffn_w_gate_up, v_ffn_w_down)))
    cx, cy, cc = _mesh_pos()
    chip = 2 * cx + cy

    local = dict(zip(BLOCKS, lax.optimization_barrier(
        [(w[n][layer].T if n == "dn_w_in" else w[n][layer]).astype(MXU_DTYPE) for n, layer in BLOCKS]), strict=True))

    def usable(block, got):
        got = lax.dynamic_update_slice(got, local[block][None], (chip, 0, 0))
        return got if COLUMN_SHARDED[block[0]] else got.reshape(-1, got.shape[-1])

    late = {"ffn0": [("ffn_w_gate_up", 0), ("ffn_w_down", 0)], "dn": [("dn_w_in", 0), ("dn_w_out", 0)],
            "ffn1": [("ffn_w_gate_up", 1), ("ffn_w_down", 1)]}
    first = [b for b in BLOCKS if all(b not in blks for blks in late.values())]
    first_got = _gather_blocks([local[b] for b in first], name="gather_weights")
    late_local, _ = lax.optimization_barrier(({k: [local[b] for b in blks] for k, blks in late.items()}, first_got[0]))
    started = {k: _gather_start(late_local[k], name=f"gather_start_{k}") for k in late}
    started_token = sum(s[4][0, 0] for s in started.values())

    def late_weights(key, after):
        lands = _gather_wait(started[key], after, name=f"gather_wait_{key}")
        return [usable(b, got) for b, got in zip(late[key], lands, strict=True)]

    def dn_weights(after):
        w_in, w_out = late_weights("dn", after)
        return _pad_dn_w_in(jnp.concatenate([w_in[s].T for s in range(N_CHIPS)], axis=1)), w_out

    full = {}
    for b, got in zip(first, first_got, strict=True):
        full.setdefault(b[0], []).append(usable(b, got))
    conv_rows = _small_rows([(DN_CONV, CONV_COLS)])
    conv_mine = jnp.where(cc == 0, 1.0, 0.0) * w["dn_conv"][0]
    conv_placed = lax.dynamic_update_slice(jnp.zeros((DN_CONV, CONV_COLS), F32), conv_mine, (0, chip * CONV_SHARD))
    conv_full = _unpack_small(_all_reduce_small(_pack_small([conv_placed], conv_rows), name="gather_conv"),
                              [(DN_CONV, CONV_COLS)])[0]
    p = {
        "rel_bias": w["rel_bias"], "att_w_in": full["att_w_in"][0], "att_w_out": full["att_w_out"][0],
        "dn_conv": conv_full, "dn_a_log": w["dn_a_log"][0], "dn_dt_bias": w["dn_dt_bias"][0],
        "dn_out_norm": w["dn_out_norm"][0], "mem_norm": w["mem_norm"], "mem_w_kv": full["mem_w_kv"],
        "norm_mix_pre": w["norm_mix_pre"] + started_token,
        "norm_mix_post": w["norm_mix_post"], "norm_ffn_pre": w["norm_ffn_pre"], "norm_ffn_post": w["norm_ffn_post"],
        "ffn_weights": lambda layer, after: late_weights(f"ffn{layer}", after), "dn_weights": dn_weights,
    }

    def chip_blocks(n, a):
        if n == "dn_w_in":
            a = _unpad_dn_w_in(a)
        if a.ndim == 3:
            return a
        if COLUMN_SHARDED[n]:
            return a.reshape(a.shape[0], N_CHIPS, -1).transpose(1, 0, 2)
        return a.reshape(N_CHIPS, -1, a.shape[-1])

    begun, begun_blocks = {}, {}

    def grads_ready(key, layer_grads):
        begun_blocks[key] = list(layer_grads)
        begun[key] = _reduce_begin([chip_blocks(n, a) for (n, _), a in layer_grads.items()],
                                   [f"{n}{layer}" for n, layer in layer_grads], key)
        return begun[key][4][0, 0]

    p["grads_ready"] = grads_ready
    loss_cols, grad_x, g = _local_step(x[0], mem[0], loss_target[0], p)
    loss = lax.psum(jnp.sum(loss_cols), ("x", "y", "c"))
    grads_ready("att_in", {("att_w_in", 0): g["att_w_in"]})
    finished = _reduce_finish(begun, {k: [f"{n}{layer}" for n, layer in blks] for k, blks in begun_blocks.items()}, grad_x)
    reduced = {b: r for k in begun for b, r in zip(begun_blocks[k], finished[k], strict=True)}
    grads = {n: jnp.concatenate([reduced[b] for b in BLOCKS if b[0] == n], axis=0).reshape(shape) for n, shape, _ in BIG}
    small_full_shapes = [(DN_CONV, CONV_COLS) if n == "dn_conv" else w[n].shape for n in SMALL_NAMES]
    small_sum = _all_reduce_small(_pack_small([g[n] for n in SMALL_NAMES], _small_rows(small_full_shapes)), name="reduce_small")
    for n, s in zip(SMALL_NAMES, _unpack_small(small_sum, small_full_shapes)):
        grads[n] = lax.dynamic_slice(s, (0, chip * CONV_SHARD), (DN_CONV, CONV_SHARD))[None] if n == "dn_conv" else s

    delta, new_m, new_v = {}, {}, {}
    for n in BIG_NAMES:
        shape = w[n].shape
        two_d = (lambda a: a[0].T) if n == "dn_w_in" else (lambda a: a.reshape(-1, shape[-1]))
        back = (lambda a: a.T[None]) if n == "dn_w_in" else (lambda a: a.reshape(shape))
        g_2d = two_d(grads[n])
        res = _adamw(two_d(w[n]), g_2d, two_d(m[n]), two_d(v[n]), name=f"adamw_{n}")
        grads[n], delta[n], new_m[n], new_v[n] = (back(r) for r in (g_2d, *res))
    small_shapes = [w[n].shape for n in SMALL_NAMES]
    rows = _small_rows(small_shapes)
    res = _adamw(*[_pack_small([d[n] for n in SMALL_NAMES], rows) for d in (w, grads, m, v)], name="adamw_small")
    for d, r in zip((delta, new_m, new_v), res):
        for n, a in zip(SMALL_NAMES, _unpack_small(r, small_shapes)):
            d[n] = a
    return (loss, grad_x[None], *[grads[n] for n in WEIGHTS], *[delta[n] for n in WEIGHTS],
            *[new_m[n] for n in WEIGHTS], *[new_v[n] for n in WEIGHTS])
```

```python
import functools
import math

import numpy as np
import jax
import jax.numpy as jnp
from jax import lax
from jax.experimental import pallas as pl
from jax.experimental.pallas import tpu as pltpu

F32 = jnp.float32
MXU_DTYPE = jnp.bfloat16
LINK_DTYPE = jnp.bfloat16
HI = lax.Precision.HIGHEST

EPS = 1e-6
NEG_INF = -1e30
LANES = 128
SUBLANES = 8
VMEM_LIMIT = 56 * 1024 * 1024

D_MODEL = 1024
TOK_WIDTH = 768
MEM_WIDTH = 256
MEM_LEN = 256
ATT_HEAD_DIM = 64
DILATIONS = (1, 4, 16)
HALF = 64
ATT_BQ = 128
ATT_W = ATT_BQ + 2 * HALF
REL_BUCKETS = 32
REL_MAX_DIST = 1024
DN_HEADS = 6
DN_HEAD_DIM = 128
DN_CONV = 5
DN_CHUNK = 128
D_FF = 2816
ATT_IN = 2560
DN_IN = 3352
DN_IN_PAD = 3456
N_GATES = 4 * DN_HEADS

ADAM_LR = 0.001
ADAM_B1 = 0.9
ADAM_B2 = 0.999
ADAM_EPS = 1e-08
ADAM_WD = 0.01
ADAM_STEP = 10


def _tile(n, target, align):
    if n <= target:
        return n
    t = (target // align) * align
    while t >= align:
        if n % t == 0:
            return t
        t -= align
    raise ValueError(f"no tile for {n} (target {target}, align {align})")


def _params(*sem):
    return pltpu.CompilerParams(dimension_semantics=sem, vmem_limit_bytes=VMEM_LIMIT)


def _mm(a, b, *, name, ta=False, tb=False, out_shards=None, tm=1408, tn=1408, tk=1408, out_dtype=F32):
    if ta:
        K, M = a.shape
    else:
        M, K = a.shape
    sharded_b = b.ndim == 3
    if sharded_b:
        n_sh, b_rows, b_cols = b.shape
        N, K2 = (b_rows, n_sh * b_cols) if tb else (n_sh * b_cols, b_rows)
    else:
        N, K2 = b.shape if tb else b.shape[::-1]
    assert K == K2, (a.shape, b.shape, ta, tb)
    tm = _tile(M, tm, LANES if ta else SUBLANES)
    tn = N // out_shards if out_shards else (b_cols if sharded_b and not tb else _tile(N, tn, LANES))
    tk = b_cols if sharded_b and tb else _tile(K, tk, LANES)
    nk = K // tk
    a_spec = pl.BlockSpec((tk, tm), lambda i, j, k: (k, i)) if ta else pl.BlockSpec((tm, tk), lambda i, j, k: (i, k))
    if sharded_b:
        b_spec = (pl.BlockSpec((None, tn, tk), lambda i, j, k: (k, j, 0)) if tb
                  else pl.BlockSpec((None, tk, tn), lambda i, j, k: (j, k, 0)))
    else:
        b_spec = pl.BlockSpec((tn, tk), lambda i, j, k: (j, k)) if tb else pl.BlockSpec((tk, tn), lambda i, j, k: (k, j))
    if out_shards:
        out_spec = pl.BlockSpec((None, tm, tn), lambda i, j, k: (j, i, 0))
        out_shape = jax.ShapeDtypeStruct((out_shards, M, tn), out_dtype)
    else:
        out_spec = pl.BlockSpec((tm, tn), lambda i, j, k: (i, j))
        out_shape = jax.ShapeDtypeStruct((M, N), out_dtype)
    dims = (((0 if ta else 1,), (1 if tb else 0,)), ((), ()))

    def kern(a_ref, b_ref, o_ref, acc_ref):
        k = pl.program_id(2)

        @pl.when(k == 0)
        def _():
            acc_ref[...] = jnp.zeros_like(acc_ref)

        acc_ref[...] += lax.dot_general(a_ref[...].astype(MXU_DTYPE), b_ref[...].astype(MXU_DTYPE), dims,
                                        preferred_element_type=F32)

        @pl.when(k == nk - 1)
        def _():
            o_ref[...] = acc_ref[...].astype(o_ref.dtype)

    return pl.pallas_call(
        kern, name=name, grid=(M // tm, N // tn, nk), in_specs=[a_spec, b_spec],
        out_specs=out_spec, out_shape=out_shape,
        scratch_shapes=[pltpu.VMEM((tm, tn), F32)],
        compiler_params=_params("parallel", "parallel", "arbitrary"),
    )(a, b)


def _shards_of(w):
    return w.shape[0] if w.ndim == 3 else None


def _col(arr, width, blk):
    return (arr, width, blk)


def _rowwise(body, rows, consts, out_rows, out_acc, *, tm, name):
    n_rows = (rows[0][0] if isinstance(rows[0], tuple) else rows[0]).shape[0]
    assert n_rows % tm == 0, (n_rows, tm)
    arrs, in_specs = [], []
    for r in rows:
        arr, width, blk = r if isinstance(r, tuple) else (r, r.shape[1], 0)
        assert arr.shape[0] == n_rows
        arrs.append(arr)
        in_specs.append(pl.BlockSpec((tm, width), functools.partial(lambda i, b: (i, b), b=blk)))
    for c in consts:
        arrs.append(c)
        in_specs.append(pl.BlockSpec(c.shape, functools.partial(lambda i, n: (0,) * n, n=c.ndim)))
    n_in, n_ro = len(arrs), len(out_rows)
    out_shape = [jax.ShapeDtypeStruct((n_rows, w), dt) for w, dt in out_rows]
    out_specs = [pl.BlockSpec((tm, w), lambda i: (i, 0)) for w, _ in out_rows]
    out_shape += [jax.ShapeDtypeStruct(s, F32) for s in out_acc]
    out_specs += [pl.BlockSpec(s, lambda i: (0, 0)) for s in out_acc]

    def kern(*refs):
        ro, ao = body(*[r[...] for r in refs[:n_in]])
        outs = refs[n_in:]
        for r, v in zip(outs[:n_ro], ro, strict=True):
            r[...] = v.astype(r.dtype)
        if out_acc:
            @pl.when(pl.program_id(0) == 0)
            def _():
                for r in outs[n_ro:]:
                    r[...] = jnp.zeros_like(r)

            for r, v in zip(outs[n_ro:], ao, strict=True):
                r[...] += v

    res = pl.pallas_call(
        kern, name=name, grid=(n_rows // tm,), in_specs=in_specs, out_specs=out_specs, out_shape=out_shape,
        compiler_params=_params("arbitrary" if out_acc else "parallel"),
    )(*arrs)
    return res


def _rms(x, gain):
    return x * lax.rsqrt(jnp.mean(x * x, axis=-1, keepdims=True) + EPS) * gain


def _silu(x):
    return x * jax.nn.sigmoid(x)


def _softplus(x):
    return jnp.maximum(x, 0.0) + jnp.log(1.0 + jnp.exp(-jnp.abs(x)))


def _dot_nt(a, b, precision=None):
    return lax.dot_general(a, b, (((1,), (1,)), ((), ())), preferred_element_type=F32, precision=precision)


def _dot_tn(a, b, precision=None):
    return lax.dot_general(a, b, (((0,), (0,)), ((), ())), preferred_element_type=F32, precision=precision)


def _dot(a, b, precision=None):
    return jnp.dot(a, b, preferred_element_type=F32, precision=precision)


def _pre_norm(x, gain, *, name):
    def body(x, g):
        return (_rms(x, g),), ()
    return _rowwise(body, [x], [gain], [(x.shape[1], MXU_DTYPE)], [], tm=_tile(x.shape[0], 512, 2 * SUBLANES), name=name)[0]


def _pre_norm_bwd(x, gain, dh, dx_other, *, name):
    def body(x, dh, dxo, g):
        _, vjp = jax.vjp(_rms, x, g)
        dx, dg = vjp(dh)
        return (dx + dxo,), (dg,)
    return _rowwise(body, [x, dh, dx_other], [gain], [(x.shape[1], F32)], [gain.shape], tm=512, name=name)


def _gain_bwd(x, gain, dh, *, name):
    def body(x, dh, g):
        _, vjp = jax.vjp(lambda g_: _rms(x, g_), g)
        return (), (vjp(dh)[0],)
    return _rowwise(body, [x, dh], [gain], [], [gain.shape], tm=_tile(x.shape[0], 512, SUBLANES), name=name)[0]


def _res_block(x_res, m, g_post, g_pre):
    x_new = x_res + _rms(m, g_post)
    return x_new, _rms(x_new, g_pre)


def _post_pre(x_res, m, g_post, g_pre, *, name):
    def body(x, m, gp, gq):
        return _res_block(x, m, gp, gq), ()
    d = x_res.shape[1]
    return _rowwise(body, [x_res, m], [g_post, g_pre], [(d, F32), (d, MXU_DTYPE)], [], tm=512, name=name)


def _post_pre_bwd(x_res, m, g_post, g_pre, dx_new, dh, *, name):
    def body(x, m, dxn, dh, gp, gq):
        _, vjp = jax.vjp(_res_block, x, m, gp, gq)
        dx, dm, dgp, dgq = vjp((dxn, dh))
        return (dx, dm), (dgp, dgq)
    d = x_res.shape[1]
    return _rowwise(body, [x_res, m, dx_new, dh], [g_post, g_pre], [(d, F32), (d, MXU_DTYPE)],
                    [g_post.shape, g_pre.shape], tm=256, name=name)


def _final_loss_bwd(x_res, m, g_post, target, *, name):
    d = x_res.shape[1]

    def loss_cols(x, m, g, t):
        err = x + _rms(m, g) - t
        return jnp.sum(err * err, axis=0, keepdims=True) * (0.5 / d)

    def body(x, m, t, g):
        cols, vjp = jax.vjp(lambda x_, m_, g_: loss_cols(x_, m_, g_, t), x, m, g)
        dx, dm, dg = vjp(jnp.ones_like(cols))
        return (dx, dm), (dg, cols)
    return _rowwise(body, [x_res, m, target], [g_post], [(d, F32), (d, MXU_DTYPE)], [g_post.shape, (1, d)], tm=256, name=name)


def _swiglu_act(gu, *, name):
    def body(gate, up):
        return (_silu(gate.astype(F32)) * up.astype(F32),), ()
    return _rowwise(body, [_col(gu, D_FF, 0), _col(gu, D_FF, 1)], [], [(D_FF, MXU_DTYPE)], [], tm=256, name=name)[0]


def _swiglu_act_bwd(gu, da, *, name):
    def body(gate, up, da):
        _, vjp = jax.vjp(lambda g, u: _silu(g) * u, gate.astype(F32), up.astype(F32))
        dg, du = vjp(da.astype(F32))
        return (jnp.concatenate([dg, du], axis=1),), ()
    return _rowwise(body, [_col(gu, D_FF, 0), _col(gu, D_FF, 1), da], [], [(2 * D_FF, MXU_DTYPE)], [], tm=256, name=name)[0]


def _lane_head_mask(width, head_dim, head):
    lane = lax.broadcasted_iota(jnp.int32, (1, width), 1)
    return (lane // head_dim) == head


def _mem_attn_heads(q4, k4, v4):
    logits = _bdot_nt(q4, k4)
    p = jnp.exp(logits - jnp.max(logits, axis=-1, keepdims=True))
    return _bdot(p / jnp.sum(p, axis=-1, keepdims=True), v4)


def _mem_heads(q_mem, kv):
    return _heads(q_mem * (ATT_HEAD_DIM ** -0.5), mask=True), _heads(kv[:, :MEM_WIDTH]), _heads(kv[:, MEM_WIDTH:])


def _mem_attn(q_mem, kv):
    return _join_heads(_mem_attn_heads(*_mem_heads(q_mem, kv)))


def _mem_attn_bwd(q_mem, kv, do):
    _, vjp = jax.vjp(_mem_attn_heads, *_mem_heads(q_mem, kv))
    dq4, dk4, dv4 = vjp(_heads(do, mask=True))
    return (_join_heads(dq4) * (ATT_HEAD_DIM ** -0.5),
            jnp.concatenate([dk4[0] + dk4[1], dk4[2] + dk4[3], dv4[0] + dv4[1], dv4[2] + dv4[3]], axis=1))


def _t5_bucket(rel):
    half = REL_BUCKETS // 2
    max_exact = half // 2
    n = np.abs(rel)
    large = max_exact + (np.log(np.maximum(n, 1) / max_exact) / math.log(REL_MAX_DIST / max_exact)
                         * (half - max_exact)).astype(np.int64)
    large = np.minimum(large, half - 1)
    return ((rel > 0) * half + np.where(n < max_exact, n, large)).astype(np.int32)


ATT_DIAGS = ATT_BQ + ATT_W - 1


def _bias_diag_onehot(dil):
    j = np.arange(ATT_DIAGS)
    tiles = []
    for off in (-HALF, 0, HALF):
        rel = j - (ATT_BQ - 1) - HALF - off
        hot = _t5_bucket(rel * dil)[:, None] == np.arange(REL_BUCKETS)[None, :]
        tiles.append(hot & (np.abs(rel) <= HALF)[:, None])
    return np.stack(tiles).astype(np.float32)


def _toeplitz(r):
    lead = r.shape[:-1]
    a = jnp.broadcast_to(r[..., None, :], lead + (ATT_BQ, ATT_DIAGS))
    a = jnp.pad(a, [(0, 0)] * len(lead) + [(0, 0), (0, 1)])
    a = a.reshape(lead + (ATT_BQ * (ATT_DIAGS + 1),))[..., : ATT_BQ * ATT_DIAGS].reshape(lead + (ATT_BQ, ATT_DIAGS))
    return a[..., ATT_BQ - 1: ATT_BQ - 1 + ATT_W]


def _bias_tiles(rel_bias, gi):
    heads = rel_bias[:, 4 * gi: 4 * gi + 4]
    diag = jnp.einsum('tnb,bh->thn', jnp.asarray(_bias_diag_onehot(DILATIONS[gi])), heads, precision=HI)
    return _toeplitz(diag)


def _bias_tiles_bwd(rel_bias, dtiles, gi):
    return jax.vjp(lambda rb: _bias_tiles(rb, gi), rel_bias)[1](dtiles)[0]


def _att_window(i, n_sub):
    start = jnp.clip(i * ATT_BQ - HALF, 0, n_sub - ATT_W)
    off = i * ATT_BQ - HALF - start
    return pl.multiple_of(start, HALF), off


def _att_valid(off):
    q = lax.broadcasted_iota(jnp.int32, (ATT_BQ, ATT_W), 0)
    kk = lax.broadcasted_iota(jnp.int32, (ATT_BQ, ATT_W), 1)
    return jnp.abs(kk - q - HALF - off) <= HALF


def _att_tile_id(i, nq):
    return jnp.where(i == 0, 0, jnp.where(i == nq - 1, 2, 1))


ATT_GROUP_HEADS = 4


def _heads(x, mask=False):
    out = []
    for p in range(2):
        pair = x[:, p * LANES: (p + 1) * LANES]
        for h in range(2):
            out.append(jnp.where(_lane_head_mask(LANES, ATT_HEAD_DIM, h), pair, 0.0) if mask else pair)
    return jnp.stack(out)


def _join_heads(x):
    first = _lane_head_mask(LANES, ATT_HEAD_DIM, 0)
    return jnp.concatenate([jnp.where(first, x[2 * p], x[2 * p + 1]) for p in range(2)], axis=1)


def _head_scalar(x):
    out = []
    for p in range(2):
        pair = x[:, p * LANES: (p + 1) * LANES]
        for h in range(2):
            out.append(jnp.max(jnp.where(_lane_head_mask(LANES, ATT_HEAD_DIM, h), pair, NEG_INF), axis=-1, keepdims=True))
    return jnp.stack(out)


def _bdot(a, b):
    return jnp.einsum('hqk,hkd->hqd', a, b, preferred_element_type=F32)


def _bdot_nt(a, b):
    return jnp.einsum('hqd,hkd->hqk', a, b, preferred_element_type=F32)


def _bdot_tn(a, b):
    return jnp.einsum('hqk,hqd->hkd', a, b, preferred_element_type=F32)


def _att_fwd(qkvm, bias, gi, *, name):
    dil = DILATIONS[gi]
    s_len = qkvm.shape[0]
    n_sub = s_len // dil
    nq = n_sub // ATT_BQ
    assert n_sub % ATT_BQ == 0 and n_sub >= ATT_W
    cols = qkvm.shape[1] // (2 * LANES)
    view = qkvm.reshape(n_sub, dil * qkvm.shape[1])

    def kern(q_ref, k_ref, v_ref, b_ref, o_ref, lse_ref):
        start, off = _att_window(pl.program_id(1), n_sub)
        valid = _att_valid(off)
        q4 = _heads(q_ref[...].astype(F32) * (ATT_HEAD_DIM ** -0.5), mask=True)
        k4 = _heads(k_ref[pl.ds(start, ATT_W), :].astype(F32))
        v4 = _heads(v_ref[pl.ds(start, ATT_W), :].astype(F32))
        s = jnp.where(valid, _bdot_nt(q4, k4) + b_ref[...], NEG_INF)
        mx = jnp.max(s, axis=-1, keepdims=True)
        p = jnp.exp(s - mx)
        den = jnp.sum(p, axis=-1, keepdims=True)
        o_ref[...] = _join_heads(_bdot(p, v4) / den)
        lse_ref[...] = _join_heads(jnp.broadcast_to(mx + jnp.log(den), (ATT_GROUP_HEADS, ATT_BQ, LANES)))

    def qkv_spec(which, full):
        shape = (n_sub, 2 * LANES) if full else (ATT_BQ, 2 * LANES)
        return pl.BlockSpec(shape, lambda r, i: (0 if full else i, r * cols + which * 3 + gi))

    out_spec = pl.BlockSpec((ATT_BQ, 2 * LANES), lambda r, i: (i, r))
    o, lse = pl.pallas_call(
        kern, name=name, grid=(dil, nq),
        in_specs=[qkv_spec(0, False), qkv_spec(1, True), qkv_spec(2, True),
                  pl.BlockSpec((None, ATT_GROUP_HEADS, ATT_BQ, ATT_W), lambda r, i: (_att_tile_id(i, nq), 0, 0, 0))],
        out_specs=[out_spec, out_spec],
        out_shape=[jax.ShapeDtypeStruct((n_sub, dil * 2 * LANES), F32)] * 2,
        compiler_params=_params("parallel", "arbitrary"),
    )(view, view, view, bias)
    return o.reshape(s_len, 2 * LANES), lse.reshape(s_len, 2 * LANES)


def _att_bwd(qkvm, bias, lse_tot, delta, dcat, gi, *, name):
    dil = DILATIONS[gi]
    s_len = qkvm.shape[0]
    n_sub = s_len // dil
    nq = n_sub // ATT_BQ
    cols = qkvm.shape[1] // (2 * LANES)
    dcols = dcat.shape[1] // (2 * LANES)
    view = qkvm.reshape(n_sub, dil * qkvm.shape[1])
    lse_v = lse_tot.reshape(n_sub, dil * 2 * LANES)
    delta_v = delta.reshape(n_sub, dil * 2 * LANES)
    dcat_v = dcat.reshape(n_sub, dil * dcat.shape[1])

    def kern(q_ref, k_ref, v_ref, b_ref, lse_ref, dl_ref, dm_ref, dq_ref, dk_ref, dv_ref, db_ref):
        r, i = pl.program_id(0), pl.program_id(1)
        start, off = _att_window(i, n_sub)
        valid = _att_valid(off)
        tile = _att_tile_id(i, nq)

        @pl.when(i == 0)
        def _():
            dk_ref[...] = jnp.zeros_like(dk_ref)
            dv_ref[...] = jnp.zeros_like(dv_ref)

        @pl.when((i == 0) & (r == 0))
        def _():
            db_ref[...] = jnp.zeros_like(db_ref)

        q4 = _heads(q_ref[...].astype(F32) * (ATT_HEAD_DIM ** -0.5), mask=True)
        k4 = _heads(k_ref[pl.ds(start, ATT_W), :].astype(F32))
        v4 = _heads(v_ref[pl.ds(start, ATT_W), :].astype(F32))
        dm4 = _heads(dm_ref[...], mask=True)
        s = jnp.where(valid, _bdot_nt(q4, k4) + b_ref[tile], NEG_INF)
        p = jnp.exp(s - _head_scalar(lse_ref[...]))
        ds = p * (_bdot_nt(dm4, v4) - _head_scalar(dl_ref[...]))
        dq_ref[...] = _join_heads(_bdot(ds, k4)) * (ATT_HEAD_DIM ** -0.5)
        dk4 = _bdot_tn(ds, q4)
        dv4 = _bdot_tn(p, dm4)
        dk_ref[pl.ds(start, ATT_W), :] += jnp.concatenate([dk4[0] + dk4[1], dk4[2] + dk4[3]], axis=1)
        dv_ref[pl.ds(start, ATT_W), :] += jnp.concatenate([dv4[0] + dv4[1], dv4[2] + dv4[3]], axis=1)
        db_ref[tile] += ds

    def qkv_spec(which, full):
        shape = (n_sub, 2 * LANES) if full else (ATT_BQ, 2 * LANES)
        return pl.BlockSpec(shape, lambda r, i: (0 if full else i, r * cols + which * 3 + gi))

    blk = pl.BlockSpec((ATT_BQ, 2 * LANES), lambda r, i: (i, r))
    full = pl.BlockSpec((n_sub, 2 * LANES), lambda r, i: (0, r))
    bias_spec = pl.BlockSpec(bias.shape, lambda r, i: (0, 0, 0, 0))
    sub = jax.ShapeDtypeStruct((n_sub, dil * 2 * LANES), F32)
    dq, dk, dv, db = pl.pallas_call(
        kern, name=name, grid=(dil, nq),
        in_specs=[qkv_spec(0, False), qkv_spec(1, True), qkv_spec(2, True), bias_spec, blk, blk,
                  pl.BlockSpec((ATT_BQ, 2 * LANES), lambda r, i: (i, r * dcols + gi))],
        out_specs=[blk, full, full, bias_spec],
        out_shape=[sub, sub, sub, jax.ShapeDtypeStruct(bias.shape, F32)],
        compiler_params=_params("arbitrary", "arbitrary"),
    )(view, view, view, bias, lse_v, delta_v, dcat_v)
    return dq.reshape(s_len, -1), dk.reshape(s_len, -1), dv.reshape(s_len, -1), db


def _att_combine(o_g, lse_g, qkvm, kv_mem, *, name):
    def body(o0, o1, o2, l0, l1, l2, qm, kv):
        mx = jnp.maximum(jnp.maximum(l0, l1), l2)
        tot = mx + jnp.log(jnp.exp(l0 - mx) + jnp.exp(l1 - mx) + jnp.exp(l2 - mx))
        mixed = [o * jnp.exp(l - tot) for o, l in ((o0, l0), (o1, l1), (o2, l2))]
        return (jnp.concatenate(mixed + [_mem_attn(qm.astype(F32), kv)], axis=1), tot), ()
    return _rowwise(body, list(o_g) + list(lse_g) + [_col(qkvm, MEM_WIDTH, (3 * TOK_WIDTH) // MEM_WIDTH)], [kv_mem],
                    [(D_MODEL, F32), (MEM_WIDTH, F32)], [], tm=256, name=name)


def _head_sum_matrix():
    a = np.arange(MEM_WIDTH)
    return jnp.asarray((a[:, None] // ATT_HEAD_DIM == a[None, :] // ATT_HEAD_DIM).astype(np.float32))


def _att_bwd_prep(cat, dcat, qkvm, kv_mem, *, name):
    def body(cat, dcat, qm, kv, hs):
        prod = cat * dcat
        summed = prod[:, 0:256] + prod[:, 256:512] + prod[:, 512:768]
        delta = _dot(summed, hs, precision=HI)
        dqm, dkv = _mem_attn_bwd(qm.astype(F32), kv, dcat[:, TOK_WIDTH:])
        return (delta, dqm), (dkv,)
    return _rowwise(body, [cat, dcat, _col(qkvm, MEM_WIDTH, (3 * TOK_WIDTH) // MEM_WIDTH)], [kv_mem, _head_sum_matrix()],
                    [(MEM_WIDTH, F32), (MEM_WIDTH, F32)], [kv_mem.shape], tm=256, name=name)


def _dn_conv_post(s, j):
    scale = jnp.where(j < DN_HEADS, DN_HEAD_DIM ** -0.5, 1.0)
    normed = s * lax.rsqrt(jnp.sum(s * s, axis=-1, keepdims=True) + EPS) * scale
    return jnp.where(j >= 2 * DN_HEADS, s, normed)


def _shift_rows(x, sh):
    n = x.shape[0]
    row = lax.broadcasted_iota(jnp.int32, (n, 1), 0)
    rolled = pltpu.roll(x, (-sh) % n, 0)
    return jnp.where((row + sh >= 0) & (row + sh < n), rolled, 0.0)


def _dn_conv_taps(x, w_ref):
    c = x * w_ref[pl.ds(DN_CONV // 2, 1), :]
    for jj in range(DN_CONV):
        if jj != DN_CONV // 2:
            c = c + _shift_rows(x, jj - DN_CONV // 2) * w_ref[pl.ds(jj, 1), :]
    return c


def _dn_conv_fwd(proj, conv_w, *, name):
    s_len = proj.shape[0]
    width = 3 * TOK_WIDTH

    def kern(x_ref, w_ref, o_ref):
        j = pl.program_id(0)
        o_ref[...] = _dn_conv_post(_silu(_dn_conv_taps(x_ref[...], w_ref)), j)

    return pl.pallas_call(
        kern, name=name, grid=(width // LANES,),
        in_specs=[pl.BlockSpec((s_len, LANES), lambda j: (0, j)), pl.BlockSpec((DN_CONV, LANES), lambda j: (0, j))],
        out_specs=pl.BlockSpec((s_len, LANES), lambda j: (0, j)),
        out_shape=jax.ShapeDtypeStruct((s_len, width), F32),
        compiler_params=_params("parallel"),
    )(proj, conv_w)


def _dn_conv_bwd(proj, conv_w, d_fwd, d_bwd, which, *, name):
    s_len = proj.shape[0]

    def kern(x_ref, w_ref, df_ref, db_ref, dx_ref, dw_ref):
        j = pl.program_id(0) + which * DN_HEADS
        x = x_ref[...]
        c = _dn_conv_taps(x, w_ref)
        _, vjp = jax.vjp(lambda c_: _dn_conv_post(_silu(c_), j), c)
        dc = vjp(df_ref[...] + db_ref[...])[0]
        dx = dc * w_ref[pl.ds(DN_CONV // 2, 1), :]
        for jj in range(DN_CONV):
            sh = jj - DN_CONV // 2
            if sh != 0:
                dx = dx + _shift_rows(dc, -sh) * w_ref[pl.ds(jj, 1), :]
            dw_ref[pl.ds(jj, 1), :] = jnp.sum(dc * _shift_rows(x, sh), axis=0, keepdims=True)
        dx_ref[...] = dx

    return pl.pallas_call(
        kern, name=name, grid=(DN_HEADS,),
        in_specs=[pl.BlockSpec((s_len, LANES), lambda j: (0, j + which * DN_HEADS)),
                  pl.BlockSpec((DN_CONV, LANES), lambda j: (0, j + which * DN_HEADS)),
                  pl.BlockSpec((s_len, LANES), lambda j: (0, j)),
                  pl.BlockSpec((s_len, LANES), lambda j: (0, j))],
        out_specs=[pl.BlockSpec((s_len, LANES), lambda j: (0, j)), pl.BlockSpec((DN_CONV, LANES), lambda j: (0, j))],
        out_shape=[jax.ShapeDtypeStruct((s_len, TOK_WIDTH), F32), jax.ShapeDtypeStruct((DN_CONV, TOK_WIDTH), F32)],
        compiler_params=_params("parallel"),
    )(proj, conv_w, d_fwd, d_bwd)


GATE_TM = 2 * DN_CHUNK


FWD_GATE_LANES = 2 * DN_HEADS


def _gate_constants():
    i = np.arange(GATE_TM)
    same = (i[:, None] // DN_CHUNK) == (i[None, :] // DN_CHUNK)
    cum_f = same & (i[None, :] <= i[:, None])
    cum_r = same & (i[None, :] >= i[:, None])
    return tuple(jnp.asarray(np.asarray(a, np.float32)) for a in (cum_f, cum_r, same))


def _gate_params(p):
    z = jnp.zeros((DN_HEADS,), F32)
    return jnp.concatenate([p[0], z, p[1], z, jnp.zeros((LANES - N_GATES,), F32)]).reshape(1, LANES)


def _gate_params_bwd(dp):
    return jnp.stack([dp[0, 0:DN_HEADS], dp[0, 2 * DN_HEADS: 3 * DN_HEADS]])


def _dn_gates(gate_in, a_cols, dt_cols, cum_f, cum_r, tot):
    g = -jnp.exp(a_cols) * _softplus(gate_in + dt_cols)
    fwd_lane = lax.broadcasted_iota(jnp.int32, (1, LANES), 1) < FWD_GATE_LANES
    gc = jnp.where(fwd_lane, _dot(cum_f, g, precision=HI), _dot(cum_r, g, precision=HI))
    return gc, _dot(tot, g, precision=HI), jax.nn.sigmoid(gate_in)


def _dn_gates_fwd(proj, a_cols, dt_cols, *, name):
    def body(gi, *consts):
        return _dn_gates(gi, *consts), ()
    return _rowwise(body, [_col(proj, LANES, DN_IN_PAD // LANES - 1)], [a_cols, dt_cols, *_gate_constants()],
                    [(LANES, F32)] * 3, [], tm=GATE_TM, name=name)


def _dn_gates_bwd(proj, a_cols, dt_cols, d_gates, *, name):
    def body(gi, gcf, gtf, bf, gcr, gtr, br, a, dt, *consts):
        _, vjp = jax.vjp(lambda gi_, a_, dt_: _dn_gates(gi_, a_, dt_, *consts), gi, a, dt)
        dgi, da, ddt = vjp((gcf + gcr, gtf + gtr, bf + br))
        return (dgi,), (da, ddt)
    return _rowwise(body, [_col(proj, LANES, DN_IN_PAD // LANES - 1), *d_gates[0], *d_gates[1]],
                    [a_cols, dt_cols, *_gate_constants()], [(LANES, F32)], [a_cols.shape, dt_cols.shape],
                    tm=GATE_TM, name=name)


INV_BASE = 8


def _block_id_equal(c, size):
    i = lax.broadcasted_iota(jnp.int32, (c, c), 0) // size
    j = lax.broadcasted_iota(jnp.int32, (c, c), 1) // size
    return (i == j).astype(F32)


def _unit_tri_inverse_impl(lmat):
    c = lmat.shape[0]
    eye = _block_id_equal(c, 1)
    same = _block_id_equal(c, INV_BASE)
    neg = -lmat * same
    inv = eye + neg
    power = neg
    for _ in range(int(math.log2(INV_BASE)) - 1):
        power = _dot(power, power)
        inv = inv + _dot(inv, power)
    size = INV_BASE
    while size < c:
        bigger = _block_id_equal(c, 2 * size)
        inv = inv - _dot(_dot(inv, lmat * (bigger - same)), inv)
        same, size = bigger, 2 * size
    resid = eye - _dot(eye + lmat, inv, precision=HI)
    return inv + _dot(inv, resid)


@jax.custom_vjp
def _unit_tri_inverse(lmat):
    return _unit_tri_inverse_impl(lmat)


def _unit_tri_inverse_fwd(lmat):
    inv = _unit_tri_inverse_impl(lmat)
    return inv, inv


def _unit_tri_inverse_bwd(inv, d_inv):
    return (-_dot_tn(inv, _dot_nt(d_inv, inv)),)


_unit_tri_inverse.defvjp(_unit_tri_inverse_fwd, _unit_tri_inverse_bwd)


def _dn_chunk(q, k, v, gates_t, gc_row, tot_row, beta_row, state, tri, inverse):
    c = q.shape[0]
    assert c == DN_HEAD_DIM
    eye = _block_id_equal(c, 1)

    def along_rows(x, pick):
        return jnp.broadcast_to(jnp.sum(x * pick, axis=0, keepdims=True), (c, c))

    gc_j = along_rows(gates_t[0], gc_row)
    gc = gc_j.T
    g_tot = along_rows(gates_t[1], tot_row)
    beta = along_rows(gates_t[2], beta_row).T
    decay = jnp.exp(jnp.where(tri > 0, gc - gc_j, NEG_INF))
    k_beta = k * beta
    inv = inverse((tri - eye) * (_dot_nt(k_beta, k) * decay))
    e_gc = jnp.exp(gc)
    u = _dot(inv, v * beta)
    w = _dot(inv, k_beta * e_gc)
    intra = tri * (_dot_nt(q, k) * decay)
    v_new = u - _dot(w, state)
    out = _dot(q * e_gc, state) + _dot(intra, v_new)
    state = state * jnp.exp(g_tot) + _dot_tn(k * jnp.exp(g_tot - gc), v_new)
    return out, state


def _dn_tri():
    i = np.arange(DN_CHUNK)
    tri = np.stack([(i[None, :] <= i[:, None]), (i[None, :] >= i[:, None])]).astype(np.float32)
    return jnp.asarray(np.repeat(tri, DN_HEADS, axis=0))


def _dn_gate_picks():
    picks = np.zeros((3, 2 * DN_HEADS, 2 * DN_CHUNK, 1), np.float32)
    for d in range(2):
        for h in range(DN_HEADS):
            alpha = d * DN_CHUNK + d * 2 * DN_HEADS + h
            picks[0, d * DN_HEADS + h, alpha] = 1.0
            picks[1, d * DN_HEADS + h, alpha] = 1.0
            picks[2, d * DN_HEADS + h, alpha + DN_HEADS] = 1.0
    return jnp.asarray(picks)


def _stack_chains(fwd_ref, rev_ref):
    return jnp.stack([r[:, _head_cols(h)] for r in (fwd_ref, rev_ref) for h in range(DN_HEADS)])


def _unstack_chains(val, fwd_ref, rev_ref):
    for d, r in enumerate((fwd_ref, rev_ref)):
        for h in range(DN_HEADS):
            r[:, _head_cols(h)] = val[d * DN_HEADS + h]


def _gates_transposed(fwd_refs, rev_refs):
    return jnp.stack([jnp.concatenate([f[...].T, r[...].T], axis=0) for f, r in zip(fwd_refs, rev_refs, strict=True)])


def _dn_row_spec(nc, col, reverse, width=TOK_WIDTH):
    return pl.BlockSpec((DN_CHUNK, width), lambda t: ((nc - 1 - t) if reverse else t, col))


def _dn_state_spec(nc, reverse):
    return pl.BlockSpec((None, DN_HEADS, DN_HEAD_DIM, DN_HEAD_DIM), lambda t: ((nc - 1 - t) if reverse else t, 0, 0, 0))


def _head_cols(h):
    return pl.ds(h * DN_HEAD_DIM, DN_HEAD_DIM)


def _const_spec(arr):
    return pl.BlockSpec(arr.shape, functools.partial(lambda t, n: (0,) * n, n=arr.ndim))


def _dn_chains(inverse):
    return jax.vmap(lambda q, k, v, gates_t, *rest: _dn_chunk(q, k, v, gates_t, *rest, inverse),
                    in_axes=(0, 0, 0, None, 0, 0, 0, 0, 0))


def _dn_scan_fwd(qkv, gates, *, name):
    s_len = qkv.shape[0]
    nc = s_len // DN_CHUNK
    tri, picks = _dn_tri(), _dn_gate_picks()

    def kern(*refs):
        ins, (tri_ref, pick_ref, of_ref, or_ref, sf_ref, sr_ref, state) = refs[:12], refs[12:]

        @pl.when(pl.program_id(0) == 0)
        def _():
            state[...] = jnp.zeros_like(state)

        entry = state[...]
        qkv_c = [_stack_chains(ins[i], ins[6 + i]) for i in range(3)]
        gates_t = _gates_transposed(ins[3:6], ins[9:12])
        out, new = _dn_chains(_unit_tri_inverse_impl)(*qkv_c, gates_t, pick_ref[0], pick_ref[1], pick_ref[2], entry, tri_ref[...])
        sf_ref[...] = entry[:DN_HEADS]
        sr_ref[...] = entry[DN_HEADS:]
        _unstack_chains(out, of_ref, or_ref)
        state[...] = new

    in_specs = []
    for rev in (False, True):
        in_specs += [_dn_row_spec(nc, col, rev) for col in (0, 1, 2)] + [_dn_row_spec(nc, 0, rev, LANES)] * 3
    in_specs += [_const_spec(tri), _const_spec(picks)]
    return pl.pallas_call(
        kern, name=name, grid=(nc,), in_specs=in_specs,
        out_specs=[_dn_row_spec(nc, 0, False), _dn_row_spec(nc, 0, True), _dn_state_spec(nc, False), _dn_state_spec(nc, True)],
        out_shape=[jax.ShapeDtypeStruct((s_len, TOK_WIDTH), F32)] * 2
        + [jax.ShapeDtypeStruct((nc, DN_HEADS, DN_HEAD_DIM, DN_HEAD_DIM), F32)] * 2,
        scratch_shapes=[pltpu.VMEM((2 * DN_HEADS, DN_HEAD_DIM, DN_HEAD_DIM), F32)],
        compiler_params=_params("arbitrary"),
    )(*([qkv, qkv, qkv, *gates] * 2), tri, picks)


def _dn_scan_bwd(qkv, gates, states, d_o, *, name):
    s_len = qkv.shape[0]
    nc = s_len // DN_CHUNK
    tri, picks = _dn_tri(), _dn_gate_picks()

    def kern(*refs):
        ins, tri_ref, pick_ref, outs, d_state = refs[:16], refs[16], refs[17], refs[18:30], refs[30]

        @pl.when(pl.program_id(0) == 0)
        def _():
            d_state[...] = jnp.zeros_like(d_state)

        qkv_c = [_stack_chains(ins[i], ins[8 + i]) for i in range(3)]
        gates_t = _gates_transposed(ins[3:6], ins[11:14])
        entry = jnp.concatenate([ins[6][...], ins[14][...]], axis=0)
        d_out = _stack_chains(ins[7], ins[15])
        tri_v, picks_v = tri_ref[...], pick_ref[...]
        _, vjp = jax.vjp(lambda q, k, v, g, s: _dn_chains(_unit_tri_inverse)(q, k, v, g, picks_v[0], picks_v[1], picks_v[2], s, tri_v),
                         *qkv_c, gates_t, entry)
        dq, dk, dv, d_gates_t, d_entry = vjp((d_out, d_state[...]))
        for i, val in enumerate((dq, dk, dv)):
            _unstack_chains(val, outs[i], outs[6 + i])
        for i in range(3):
            outs[3 + i][...] = d_gates_t[i, :DN_CHUNK].T
            outs[9 + i][...] = d_gates_t[i, DN_CHUNK:].T
        d_state[...] = d_entry

    in_specs, out_specs, out_shape = [], [], []
    for rev in (True, False):
        in_specs += [_dn_row_spec(nc, col, rev) for col in (0, 1, 2)] + [_dn_row_spec(nc, 0, rev, LANES)] * 3
        in_specs += [_dn_state_spec(nc, rev), _dn_row_spec(nc, 0, rev)]
        out_specs += [_dn_row_spec(nc, 0, rev)] * 3 + [_dn_row_spec(nc, 0, rev, LANES)] * 3
        out_shape += [jax.ShapeDtypeStruct((s_len, TOK_WIDTH), F32)] * 3 + [jax.ShapeDtypeStruct((s_len, LANES), F32)] * 3
    in_specs += [_const_spec(tri), _const_spec(picks)]
    res = pl.pallas_call(
        kern, name=name, grid=(nc,), in_specs=in_specs, out_specs=out_specs, out_shape=out_shape,
        scratch_shapes=[pltpu.VMEM((2 * DN_HEADS, DN_HEAD_DIM, DN_HEAD_DIM), F32)],
        compiler_params=_params("arbitrary"),
    )(*[a for d in range(2) for a in (qkv, qkv, qkv, *gates, states[d], d_o)], tri, picks)
    return (res[0:3], res[3:6]), (res[6:9], res[9:12])


def _dn_out_head(o_f, o_b, z, gain):
    o = o_f + o_b
    return o * lax.rsqrt(jnp.mean(o * o, axis=-1, keepdims=True) + EPS) * gain * _silu(z)


def _dn_out(o_fwd, o_rev, proj, gain, qkv_kv_mem, *, name):
    def body(of, ob, z, qm, g, kv):
        heads = []
        for h in range(DN_HEADS):
            sl = slice(h * DN_HEAD_DIM, (h + 1) * DN_HEAD_DIM)
            heads.append(_dn_out_head(of[:, sl], ob[:, sl], z[:, sl], g))
        return (jnp.concatenate(heads + [_mem_attn(qm, kv)], axis=1),), ()
    return _rowwise(body, [o_fwd, o_rev, _col(proj, TOK_WIDTH, 3),
                           _col(proj, MEM_WIDTH, (4 * TOK_WIDTH) // MEM_WIDTH)], [gain, qkv_kv_mem],
                    [(D_MODEL, MXU_DTYPE)], [], tm=256, name=name)[0]


def _dn_out_bwd(o_fwd, o_rev, proj, gain, kv_mem, dcat, *, name):
    def body(of, ob, z, qm, dcat, g, kv):
        dos, dzs = [], []
        dgain = jnp.zeros_like(g)
        for h in range(DN_HEADS):
            sl = slice(h * DN_HEAD_DIM, (h + 1) * DN_HEAD_DIM)
            _, vjp = jax.vjp(_dn_out_head, of[:, sl], ob[:, sl], z[:, sl], g)
            d_of, _, dz, dg = vjp(dcat[:, sl])
            dos.append(d_of)
            dzs.append(dz)
            dgain = dgain + dg
        dqm, dkv = _mem_attn_bwd(qm, kv, dcat[:, TOK_WIDTH:])
        return (jnp.concatenate(dos, axis=1), jnp.concatenate(dzs, axis=1), dqm), (dgain, dkv)
    return _rowwise(body, [o_fwd, o_rev, _col(proj, TOK_WIDTH, 3),
                           _col(proj, MEM_WIDTH, (4 * TOK_WIDTH) // MEM_WIDTH), dcat], [gain, kv_mem],
                    [(TOK_WIDTH, F32), (TOK_WIDTH, F32), (MEM_WIDTH, F32)], [gain.shape, kv_mem.shape], tm=256, name=name)


def _pad_dn_w_in(w):
    gates = w[:, 4 * TOK_WIDTH: 4 * TOK_WIDTH + N_GATES]
    zeros = jnp.zeros((w.shape[0], DN_IN_PAD - DN_IN), w.dtype)
    return jnp.concatenate([w[:, :4 * TOK_WIDTH], w[:, 4 * TOK_WIDTH + N_GATES:], gates, zeros], axis=1)


def _unpad_dn_w_in(w):
    q_mem = w[:, 4 * TOK_WIDTH: 4 * TOK_WIDTH + MEM_WIDTH]
    gates = w[:, 4 * TOK_WIDTH + MEM_WIDTH: 4 * TOK_WIDTH + MEM_WIDTH + N_GATES]
    return jnp.concatenate([w[:, :4 * TOK_WIDTH], gates, q_mem], axis=1)


def _ffn_fwd(h, w_gu, w_d, tag):
    gu = _mm(h, w_gu, out_dtype=MXU_DTYPE, name=f"ffn_gu_{tag}")
    act = _swiglu_act(gu, name=f"ffn_act_{tag}")
    return gu, act, _mm(act, w_d, name=f"ffn_down_{tag}")


def _ffn_bwd(h, gu, act, w_gu, w_d, df, tag):
    d_act = _mm(df, w_d, tb=True, out_dtype=MXU_DTYPE, name=f"ffn_dact_{tag}")
    d_wd = _mm(act, df, ta=True, out_dtype=LINK_DTYPE, name=f"ffn_dwd_{tag}")
    d_gu = _swiglu_act_bwd(gu, d_act, name=f"ffn_dgu_{tag}")
    dh = _mm(d_gu, w_gu, tb=True, name=f"ffn_dh_{tag}")
    d_wgu = _mm(h, d_gu, ta=True, out_shards=_shards_of(w_gu), out_dtype=LINK_DTYPE, name=f"ffn_dwgu_{tag}")
    return dh, d_wgu, d_wd


def _local_step(x, mem, target, p):
    g = {}
    row = lambda v: v.reshape(1, -1)
    gains = {k: [row(p[k][i]) for i in range(2)] for k in
             ("mem_norm", "norm_mix_pre", "norm_mix_post", "norm_ffn_pre", "norm_ffn_post")}
    out_gain = row(p["dn_out_norm"])
    a_cols, dt_cols = _gate_params(p["dn_a_log"]), _gate_params(p["dn_dt_bias"])

    h0 = _pre_norm(x, gains["norm_mix_pre"][0], name="pre0")
    mem_n = [_pre_norm(mem, gains["mem_norm"][i], name=f"mem_norm{i}") for i in range(2)]
    kv_mem = [_mm(mem_n[i], p["mem_w_kv"][i], name=f"mem_kv{i}") for i in range(2)]
    qkvm = _mm(h0, p["att_w_in"], out_dtype=MXU_DTYPE, name="att_in")
    bias = [_bias_tiles(p["rel_bias"], gi) for gi in range(3)]
    att = [_att_fwd(qkvm, bias[gi], gi, name=f"att_fwd{gi}") for gi in range(3)]
    cat0, lse_tot = _att_combine([a[0] for a in att], [a[1] for a in att], qkvm, kv_mem[0], name="att_combine")
    mo0 = _mm(cat0, p["att_w_out"], name="att_out")
    x1, h1 = _post_pre(x, mo0, gains["norm_mix_post"][0], gains["norm_ffn_pre"][0], name="post_mix0")
    w_gu0, w_d0 = p["ffn_weights"](0, h1)
    gu0, act0, f0 = _ffn_fwd(h1, w_gu0, w_d0, 0)
    x2, h2 = _post_pre(x1, f0, gains["norm_ffn_post"][0], gains["norm_mix_pre"][1], name="post_ffn0")

    dn_w_in, dn_w_out = p["dn_weights"](h2)
    proj = _mm(h2, dn_w_in, name="dn_in")
    qkv = _dn_conv_fwd(proj, p["dn_conv"], name="dn_conv")
    gates = _dn_gates_fwd(proj, a_cols, dt_cols, name="dn_gates")
    o_fwd, o_rev, st_fwd, st_rev = _dn_scan_fwd(qkv, gates, name="dn_scan")
    cat1 = _dn_out(o_fwd, o_rev, proj, out_gain, kv_mem[1], name="dn_outnorm")
    mo1 = _mm(cat1, dn_w_out, name="dn_out")
    x3, h3 = _post_pre(x2, mo1, gains["norm_mix_post"][1], gains["norm_ffn_pre"][1], name="post_mix1")
    w_gu1, w_d1 = p["ffn_weights"](1, h3)
    gu1, act1, f1 = _ffn_fwd(h3, w_gu1, w_d1, 1)

    dx3, df1, dg_ffn_post1, loss_cols = _final_loss_bwd(x3, f1, gains["norm_ffn_post"][1], target, name="loss_bwd")
    dh3, d_wgu1, d_wd1 = _ffn_bwd(h3, gu1, act1, w_gu1, w_d1, df1, 1)
    sent = p["grads_ready"]("ffn1", {("ffn_w_gate_up", 1): d_wgu1, ("ffn_w_down", 1): d_wd1})
    dx2, dmo1, dg_mix_post1, dg_ffn_pre1 = _post_pre_bwd(x2, mo1, gains["norm_mix_post"][1] + sent, gains["norm_ffn_pre"][1],
                                                         dx3, dh3, name="post_mix1_bwd")
    dcat1 = _mm(dmo1, dn_w_out, tb=True, name="dn_out_dx")
    g["dn_w_out"] = _mm(cat1, dmo1, ta=True, out_dtype=LINK_DTYPE, name="dn_out_dw")
    d_o, dz, dqm1, d_out_gain, dkv1 = _dn_out_bwd(o_fwd, o_rev, proj, out_gain, kv_mem[1], dcat1, name="dn_outnorm_bwd")
    (d_f, dg_f), (d_r, dg_r) = _dn_scan_bwd(qkv, gates, (st_fwd, st_rev), d_o, name="dn_scan_bwd")
    d_gate_cols, d_a_cols, d_dt_cols = _dn_gates_bwd(proj, a_cols, dt_cols, (dg_f, dg_r), name="dn_gates_bwd")
    d_pre, d_conv = zip(*[_dn_conv_bwd(proj, p["dn_conv"], d_f[which], d_r[which], which, name=f"dn_conv_bwd{which}")
                          for which in range(3)])
    dproj = jnp.concatenate(list(d_pre) + [dz, dqm1, d_gate_cols], axis=1).astype(MXU_DTYPE)
    dh2 = _mm(dproj, dn_w_in, tb=True, name="dn_in_dx")
    g["dn_w_in"] = _mm(h2, dproj, ta=True, out_dtype=LINK_DTYPE, name="dn_in_dw")
    g["dn_conv"] = jnp.concatenate(d_conv, axis=1)
    g["dn_a_log"] = _gate_params_bwd(d_a_cols)
    g["dn_dt_bias"] = _gate_params_bwd(d_dt_cols)
    g["dn_out_norm"] = d_out_gain

    d_mem_kv1 = _mm(mem_n[1], dkv1, ta=True, out_dtype=LINK_DTYPE, name="mem_kv_dw1")
    sent = p["grads_ready"]("dn", {("dn_w_in", 0): g["dn_w_in"], ("dn_w_out", 0): g["dn_w_out"], ("mem_w_kv", 1): d_mem_kv1})
    dx1, df0, dg_ffn_post0, dg_mix_pre1 = _post_pre_bwd(x1, f0, gains["norm_ffn_post"][0] + sent, gains["norm_mix_pre"][1],
                                                        dx2, dh2, name="post_ffn0_bwd")
    dh1, d_wgu0, d_wd0 = _ffn_bwd(h1, gu0, act0, w_gu0, w_d0, df0, 0)
    sent = p["grads_ready"]("ffn0", {("ffn_w_gate_up", 0): d_wgu0, ("ffn_w_down", 0): d_wd0})
    dx0, dmo0, dg_mix_post0, dg_ffn_pre0 = _post_pre_bwd(x, mo0, gains["norm_mix_post"][0] + sent, gains["norm_ffn_pre"][0],
                                                         dx1, dh1, name="post_mix0_bwd")
    dcat0 = _mm(dmo0, p["att_w_out"], tb=True, name="att_out_dx")
    g["att_w_out"] = _mm(cat0, dmo0, ta=True, out_dtype=LINK_DTYPE, name="att_out_dw")
    delta, dqm0, dkv0 = _att_bwd_prep(cat0, dcat0, qkvm, kv_mem[0], name="att_bwd_prep")
    d_mem_kv0 = _mm(mem_n[0], dkv0, ta=True, out_dtype=LINK_DTYPE, name="mem_kv_dw0")
    sent = p["grads_ready"]("att_out", {("att_w_out", 0): g["att_w_out"], ("mem_w_kv", 0): d_mem_kv0})
    att_b = [_att_bwd(qkvm, bias[gi] + sent, lse_tot, delta, dcat0, gi, name=f"att_bwd{gi}") for gi in range(3)]
    dqkvm = jnp.concatenate([a[w] for w in range(3) for a in att_b] + [dqm0], axis=1).astype(MXU_DTYPE)
    g["rel_bias"] = sum(_bias_tiles_bwd(p["rel_bias"], att_b[gi][3], gi) for gi in range(3))
    dh0 = _mm(dqkvm, p["att_w_in"], tb=True, name="att_in_dx")
    g["att_w_in"] = _mm(h0, dqkvm, ta=True, out_shards=_shards_of(p["att_w_in"]), out_dtype=LINK_DTYPE, name="att_in_dw")
    grad_x, dg_mix_pre0 = _pre_norm_bwd(x, gains["norm_mix_pre"][0], dh0, dx0, name="pre0_bwd")

    d_mem_norm = []
    for i, dkv in enumerate((dkv0, dkv1)):
        d_mem_n = _mm(dkv, p["mem_w_kv"][i], tb=True, name=f"mem_kv_dx{i}")
        d_mem_norm.append(_gain_bwd(mem, gains["mem_norm"][i], d_mem_n, name=f"mem_norm_bwd{i}"))
    g["mem_w_kv"] = [d_mem_kv0, d_mem_kv1]
    g["mem_norm"] = jnp.concatenate(d_mem_norm, axis=0)
    g["norm_mix_pre"] = jnp.concatenate([dg_mix_pre0, dg_mix_pre1], axis=0)
    g["norm_mix_post"] = jnp.concatenate([dg_mix_post0, dg_mix_post1], axis=0)
    g["norm_ffn_pre"] = jnp.concatenate([dg_ffn_pre0, dg_ffn_pre1], axis=0)
    g["norm_ffn_post"] = jnp.concatenate([dg_ffn_post0, dg_ffn_post1], axis=0)
    g["ffn_w_gate_up"] = [d_wgu0, d_wgu1]
    g["ffn_w_down"] = [d_wd0, d_wd1]
    return loss_cols, grad_x, g


N_CHIPS = 4
N_DEV = 8
MESH = pl.DeviceIdType.MESH
BIG = (("att_w_in", (1, 1024, 640), 2), ("att_w_out", (1, 256, 1024), 1), ("dn_w_in", (1, 1024, 838), 2),
       ("dn_w_out", (1, 256, 1024), 1), ("mem_w_kv", (2, 256, 512), 1), ("ffn_w_gate_up", (2, 1024, 1408), 2),
       ("ffn_w_down", (2, 704, 1024), 1))


def _mesh_pos():
    return lax.axis_index("x"), lax.axis_index("y"), lax.axis_index("c")


def _other_chips(x, y):
    return [(1 - x, y), (x, 1 - y), (1 - x, 1 - y)]


ANY = pl.BlockSpec(memory_space=pl.ANY)


def _all_reduce_small(v, *, name):
    rows, cols = v.shape
    flips = [(dx, dy, dc) for dx in (0, 1) for dy in (0, 1) for dc in (0, 1)][1:]

    def body(v_ref, o_ref, buf, send_sems, recv_sems):
        x, y, c = _mesh_pos()

        def peer(f):
            return tuple(1 - p if fl else p for p, fl in zip((x, y, c), f))

        def index(p):
            return 4 * p[0] + 2 * p[1] + p[2]

        buf[index((x, y, c))] = v_ref[...]
        sends = []
        for k, f in enumerate(flips):
            cp = pltpu.make_async_remote_copy(src_ref=v_ref, dst_ref=buf.at[index((x, y, c))], send_sem=send_sems.at[k],
                                              recv_sem=recv_sems.at[k], device_id=peer(f), device_id_type=MESH)
            cp.start()
            sends.append(cp)
        for k, f in enumerate(flips):
            pltpu.make_async_remote_copy(src_ref=v_ref, dst_ref=buf.at[index(peer(f))], send_sem=send_sems.at[k],
                                         recv_sem=recv_sems.at[k], device_id=peer(f), device_id_type=MESH).wait_recv()
        for cp in sends:
            cp.wait_send()
        acc = buf[0]
        for d in range(1, N_DEV):
            acc = acc + buf[d]
        o_ref[...] = acc

    vmem = pl.BlockSpec(memory_space=pltpu.VMEM)
    return pl.pallas_call(
        body, name=name, in_specs=[vmem], out_specs=vmem, out_shape=jax.ShapeDtypeStruct((rows, cols), F32),
        scratch_shapes=[pltpu.VMEM((N_DEV, rows, cols), F32), pltpu.SemaphoreType.DMA((N_DEV - 1,)),
                        pltpu.SemaphoreType.DMA((N_DEV - 1,))],
    )(v)


def _adamw(w, g, m, v, *, name):
    def body(w, g, m, v):
        m = ADAM_B1 * m + (1.0 - ADAM_B1) * g
        v = ADAM_B2 * v + (1.0 - ADAM_B2) * (g * g)
        m_hat = m / (1.0 - ADAM_B1 ** ADAM_STEP)
        v_hat = v / (1.0 - ADAM_B2 ** ADAM_STEP)
        delta = -ADAM_LR * (m_hat / (jnp.sqrt(v_hat) + ADAM_EPS) + ADAM_WD * w)
        return (delta, m, v), ()
    rows, cols = w.shape
    if rows % SUBLANES == 0:
        return _rowwise(body, [w, g, m, v], [], [(cols, F32)] * 3, [], tm=_tile(rows, 256, SUBLANES), name=name)

    def kern(*refs):
        outs, _ = body(*[r[...] for r in refs[:4]])
        for r, val in zip(refs[4:], outs, strict=True):
            r[...] = val

    spec = pl.BlockSpec((rows, _tile(cols, 256, LANES)), lambda j: (0, j))
    return pl.pallas_call(
        kern, name=name, grid=(cols // spec.block_shape[1],), in_specs=[spec] * 4, out_specs=[spec] * 3,
        out_shape=[jax.ShapeDtypeStruct((rows, cols), F32)] * 3, compiler_params=_params("parallel"),
    )(w, g, m, v)


def _pack_small(arrs, rows):
    flat = jnp.concatenate([a.reshape(-1) for a in arrs])
    return jnp.pad(flat, (0, rows * LANES - flat.shape[0])).reshape(rows, LANES)


def _unpack_small(packed, shapes):
    flat = packed.reshape(-1)
    out, off = [], 0
    for s in shapes:
        size = math.prod(s)
        out.append(flat[off: off + size].reshape(s))
        off += size
    return out


def _small_rows(shapes):
    return -(-sum(math.prod(s) for s in shapes) // (SUBLANES * LANES)) * SUBLANES


def _sem_pairs(n):
    return [pltpu.SemaphoreType.DMA((n,)), pltpu.SemaphoreType.DMA((n,))]


def _gather_blocks(blocks, *, name):
    n = len(blocks)

    def body(*refs):
        x_refs, out_refs, (send_sems, recv_sems) = refs[:n], refs[n: 2 * n], refs[2 * n:]
        x, y, c = _mesh_pos()
        sibling = (x, y, 1 - c)
        chips = _other_chips(x, y)

        def copy(k, src, dst, to):
            return pltpu.make_async_remote_copy(src_ref=src, dst_ref=dst, send_sem=send_sems.at[k],
                                                recv_sem=recv_sems.at[k], device_id=to, device_id_type=MESH)

        def part(b, chip, h):
            half = blocks[b].shape[0] // 2
            return out_refs[b].at[2 * chip[0] + chip[1], pl.ds(h * half, half), :]

        def my_half(b):
            half = blocks[b].shape[0] // 2
            return x_refs[b].at[pl.ds(c * half, half), :]

        first = [copy(6 * b + j, my_half(b), part(b, (x, y), c), (*chip, c)) for b in range(n) for j, chip in enumerate(chips)]
        for cp in first:
            cp.start()
        passed = []
        for b in range(n):
            for j, chip in enumerate(chips):
                copy(6 * b + j, my_half(b), part(b, chip, c), (*chip, c)).wait_recv()
                cp = copy(6 * b + 3 + j, part(b, chip, c), part(b, chip, c), sibling)
                cp.start()
                passed.append(cp)
        for b in range(n):
            for j, chip in enumerate(chips):
                copy(6 * b + 3 + j, part(b, chip, 1 - c), part(b, chip, 1 - c), sibling).wait_recv()
        for cp in first + passed:
            cp.wait_send()

    return pl.pallas_call(
        body, name=name, in_specs=[ANY] * n, out_specs=[ANY] * n,
        out_shape=[jax.ShapeDtypeStruct((N_CHIPS, *a.shape), a.dtype) for a in blocks],
        scratch_shapes=_sem_pairs(6 * n),
    )(*blocks)


HBM = pl.BlockSpec(memory_space=pltpu.HBM)
SEM = pl.BlockSpec(memory_space=pltpu.SEMAPHORE)
DATAFLOW = pltpu.SideEffectType.DATAFLOW_SIDE_EFFECTING


def _gather_start(blocks, *, name):
    n = len(blocks)
    lands = [lax.empty((N_CHIPS, *a.shape), a.dtype) for a in blocks]

    def body(*refs):
        x_refs, land_refs, send_sems, recv_sems, token = refs[:n], refs[n: 2 * n], refs[2 * n], refs[2 * n + 1], refs[-1]
        x, y, c = _mesh_pos()
        for b in range(n):
            for j, chip in enumerate(_other_chips(x, y)):
                pltpu.make_async_remote_copy(src_ref=x_refs[b], dst_ref=land_refs[b].at[2 * x + y], send_sem=send_sems.at[3 * b + j],
                                             recv_sem=recv_sems.at[3 * b + j], device_id=(*chip, c), device_id_type=MESH).start()
        token[...] = jnp.zeros_like(token)

    operands = [pltpu.with_memory_space_constraint(a, pltpu.HBM) for a in blocks + lands]
    res = pl.pallas_call(
        body, name=name, in_specs=[HBM] * (2 * n),
        out_shape=(pltpu.SemaphoreType.DMA((3 * n,)), pltpu.SemaphoreType.DMA((3 * n,)),
                   *[pltpu.HBM(a.shape, a.dtype) for a in operands], jax.ShapeDtypeStruct((SUBLANES, LANES), F32)),
        out_specs=(SEM, SEM, *[HBM] * (2 * n), pl.BlockSpec(memory_space=pltpu.VMEM)),
        input_output_aliases={i: 2 + i for i in range(2 * n)},
        compiler_params=pltpu.CompilerParams(has_side_effects=DATAFLOW),
    )(*operands)
    return res[0], res[1], list(res[2: 2 + n]), list(res[2 + n: 2 + 2 * n]), res[-1]


def _gather_wait(started, after, *, name):
    send_sems, recv_sems, blocks, lands, _ = started
    n = len(blocks)

    def body(*refs):
        x_refs, land_refs, send_sems, recv_sems = refs[:n], refs[n: 2 * n], refs[2 * n], refs[2 * n + 1]
        x, y, c = _mesh_pos()
        for b in range(n):
            for j, chip in enumerate(_other_chips(x, y)):
                cp = pltpu.make_async_remote_copy(src_ref=x_refs[b], dst_ref=land_refs[b].at[2 * chip[0] + chip[1]],
                                                  send_sem=send_sems.at[3 * b + j], recv_sem=recv_sems.at[3 * b + j],
                                                  device_id=(*chip, c), device_id_type=MESH)
                cp.wait_send()
                cp.wait_recv()

    res = pl.pallas_call(
        body, name=name, in_specs=(*[HBM] * (2 * n), SEM, SEM, ANY),
        out_shape=tuple(pltpu.HBM(a.shape, a.dtype) for a in blocks + lands), out_specs=tuple([HBM] * (2 * n)),
        input_output_aliases={i: i for i in range(2 * n)},
        compiler_params=pltpu.CompilerParams(has_side_effects=DATAFLOW),
    )(*blocks, *lands, send_sems, recv_sems, after)
    return list(res[n:])


def _swap_halves(blocks, own_rows, *, name):
    n = len(blocks)

    def body(*refs):
        in_refs, out_refs, (send_sems, recv_sems) = refs[:n], refs[n: 2 * n], refs[2 * n:]
        x, y, c = _mesh_pos()
        copies = [pltpu.make_async_remote_copy(src_ref=own_rows(in_refs[b], c), dst_ref=out_refs[b], send_sem=send_sems.at[b],
                                               recv_sem=recv_sems.at[b], device_id=(x, y, 1 - c), device_id_type=MESH)
                  for b in range(n)]
        for cp in copies:
            cp.start()
        for cp in copies:
            cp.wait()

    def sent_shape(a):
        return jax.eval_shape(lambda r: own_rows(r, 0), a)

    return pl.pallas_call(
        body, name=name, in_specs=[ANY] * n, out_specs=[ANY] * n,
        out_shape=[jax.ShapeDtypeStruct(sent_shape(a).shape, a.dtype) for a in blocks],
        scratch_shapes=_sem_pairs(n),
    )(*blocks)


def _whole(ref, c):
    return ref


def _sum_chips_block(parts, *, name):
    n, half, cols = parts.shape
    tm = _tile(half, 512, 2 * SUBLANES)

    def kern(p_ref, o_ref):
        acc = p_ref[0].astype(F32)
        for s in range(1, n):
            acc = acc + p_ref[s].astype(F32)
        o_ref[...] = acc

    return pl.pallas_call(
        kern, name=name, grid=(half // tm,),
        in_specs=[pl.BlockSpec((n, tm, cols), lambda i: (0, i, 0))],
        out_specs=pl.BlockSpec((tm, cols), lambda i: (i, 0)),
        out_shape=jax.ShapeDtypeStruct((half, cols), F32),
        compiler_params=_params("parallel"),
    )(parts)


PEER_FLIPS = [(dx, dy, dc) for dx in (0, 1) for dy in (0, 1) for dc in (0, 1)][1:]


def _flipped(pos, flip):
    return tuple(1 - p if f else p for p, f in zip(pos, flip))


def _device_index(pos):
    return 4 * pos[0] + 2 * pos[1] + pos[2]


def _scatter_start(blocks, *, name):
    n = len(blocks)
    lands = [lax.empty((N_DEV, a.shape[1] // 2, a.shape[2]), a.dtype) for a in blocks]

    def body(*refs):
        g_refs, land_refs, send_sems, recv_sems, token = refs[:n], refs[n: 2 * n], refs[2 * n], refs[2 * n + 1], refs[-1]
        pos = _mesh_pos()
        for b in range(n):
            half = blocks[b].shape[1] // 2
            for k, flip in enumerate(PEER_FLIPS):
                peer = _flipped(pos, flip)
                pltpu.make_async_remote_copy(src_ref=g_refs[b].at[2 * peer[0] + peer[1], pl.ds(peer[2] * half, half), :],
                                             dst_ref=land_refs[b].at[_device_index(pos)],
                                             send_sem=send_sems.at[7 * b + k], recv_sem=recv_sems.at[7 * b + k],
                                             device_id=peer, device_id_type=MESH).start()
        token[...] = jnp.zeros_like(token)

    operands = [pltpu.with_memory_space_constraint(a, pltpu.HBM) for a in blocks + lands]
    res = pl.pallas_call(
        body, name=name, in_specs=[HBM] * (2 * n),
        out_shape=(pltpu.SemaphoreType.DMA((7 * n,)), pltpu.SemaphoreType.DMA((7 * n,)),
                   *[pltpu.HBM(a.shape, a.dtype) for a in operands], jax.ShapeDtypeStruct((SUBLANES, LANES), F32)),
        out_specs=(SEM, SEM, *[HBM] * (2 * n), pl.BlockSpec(memory_space=pltpu.VMEM)),
        input_output_aliases={i: 2 + i for i in range(2 * n)},
        compiler_params=pltpu.CompilerParams(has_side_effects=DATAFLOW),
    )(*operands)
    return res[0], res[1], list(res[2: 2 + n]), list(res[2 + n: 2 + 2 * n]), res[-1]


def _scatter_wait(started, after, *, name):
    send_sems, recv_sems, blocks, lands, _ = started
    n = len(blocks)

    def body(*refs):
        g_refs, land_refs, send_sems, recv_sems = refs[:n], refs[n: 2 * n], refs[2 * n], refs[2 * n + 1]
        pos = _mesh_pos()
        for b in range(n):
            half = blocks[b].shape[1] // 2
            for k, flip in enumerate(PEER_FLIPS):
                peer = _flipped(pos, flip)
                cp = pltpu.make_async_remote_copy(src_ref=g_refs[b].at[0, pl.ds(0, half), :],
                                                  dst_ref=land_refs[b].at[_device_index(peer)],
                                                  send_sem=send_sems.at[7 * b + k], recv_sem=recv_sems.at[7 * b + k],
                                                  device_id=peer, device_id_type=MESH)
                cp.wait_send()
                cp.wait_recv()

    res = pl.pallas_call(
        body, name=name, in_specs=(*[HBM] * (2 * n), SEM, SEM, ANY),
        out_shape=tuple(pltpu.HBM(a.shape, a.dtype) for a in blocks + lands), out_specs=tuple([HBM] * (2 * n)),
        input_output_aliases={i: i for i in range(2 * n)},
        compiler_params=pltpu.CompilerParams(has_side_effects=DATAFLOW),
    )(*blocks, *lands, send_sems, recv_sems, after)
    return list(res[:n]), list(res[n:])


def _reduce_finish(begun, names, after):
    x, y, c = _mesh_pos()
    mine = {}
    for key, started in begun.items():
        blocks, lands = _scatter_wait(started, after, name=f"rs_scatter_wait_{key}")
        parts = []
        for blk, land in zip(blocks, lands, strict=True):
            half = blk.shape[1] // 2
            own = lax.dynamic_slice(blk, (2 * x + y, c * half, 0), (1, half, blk.shape[2]))
            parts.append(lax.dynamic_update_slice(land, own, (_device_index((x, y, c)), 0, 0)))
        mine[key] = [_sum_chips_block(p, name=f"rs_sum_{nm}") for p, nm in zip(parts, names[key], strict=True)]
    flat = [a for key in begun for a in mine[key]]
    other = iter(_swap_halves(flat, _whole, name="rs_join"))
    return {key: [jnp.concatenate([jnp.where(c == 0, a, b), jnp.where(c == 0, b, a)], axis=0)
                  for a, b in ((a, next(other)) for a in mine[key])] for key in begun}


WEIGHTS = ("rel_bias", "att_w_in", "att_w_out", "dn_w_in", "dn_conv", "dn_a_log", "dn_dt_bias", "dn_out_norm", "dn_w_out",
           "mem_norm", "mem_w_kv", "norm_mix_pre", "norm_mix_post", "norm_ffn_pre", "norm_ffn_post", "ffn_w_gate_up",
           "ffn_w_down")
BIG_NAMES = tuple(n for n, _, _ in BIG)
SMALL_NAMES = tuple(n for n in WEIGHTS if n not in BIG_NAMES)
CONV_COLS = 3 * TOK_WIDTH
CONV_SHARD = CONV_COLS // N_CHIPS
BLOCKS = tuple((n, layer) for n, shape, _ in BIG for layer in range(shape[0]))
COLUMN_SHARDED = {n: axis == 2 for n, _, axis in BIG}


def kernel(x, mem, rel_bias, att_w_in, att_w_out, dn_w_in, dn_conv, dn_a_log, dn_dt_bias, dn_out_norm, dn_w_out, mem_norm, mem_w_kv, norm_mix_pre, norm_mix_post, norm_ffn_pre, norm_ffn_post, ffn_w_gate_up, ffn_w_down, loss_target, m_rel_bias, m_att_w_in, m_att_w_out, m_dn_w_in, m_dn_conv, m_dn_a_log, m_dn_dt_bias, m_dn_out_norm, m_dn_w_out, m_mem_norm, m_mem_w_kv, m_norm_mix_pre, m_norm_mix_post, m_norm_ffn_pre, m_norm_ffn_post, m_ffn_w_gate_up, m_ffn_w_down, v_rel_bias, v_att_w_in, v_att_w_out, v_dn_w_in, v_dn_conv, v_dn_a_log, v_dn_dt_bias, v_dn_out_norm, v_dn_w_out, v_mem_norm, v_mem_w_kv, v_norm_mix_pre, v_norm_mix_post, v_norm_ffn_pre, v_norm_ffn_post, v_ffn_w_gate_up, v_ffn_w_down):
    w = dict(zip(WEIGHTS, (rel_bias, att_w_in, att_w_out, dn_w_in, dn_conv, dn_a_log, dn_dt_bias, dn_out_norm, dn_w_out,
                           mem_norm, mem_w_kv, norm_mix_pre, norm_mix_post, norm_ffn_pre, norm_ffn_post, ffn_w_gate_up,
                           ffn_w_down)))
    m = dict(zip(WEIGHTS, (m_rel_bias, m_att_w_in, m_att_w_out, m_dn_w_in, m_dn_conv, m_dn_a_log, m_dn_dt_bias,
                           m_dn_out_norm, m_dn_w_out, m_mem_norm, m_mem_w_kv, m_norm_mix_pre, m_norm_mix_post,
                           m_norm_ffn_pre, m_norm_ffn_post, m_ffn_w_gate_up, m_ffn_w_down)))
    v = dict(zip(WEIGHTS, (v_rel_bias, v_att_w_in, v_att_w_out, v_dn_w_in, v_dn_conv, v_dn_a_log, v_dn_dt_bias,
                           v_dn_out_norm, v_dn_w_out, v_mem_norm, v_mem_w_kv, v_norm_mix_pre, v_norm_mix_post,
                           v_norm_ffn_pre, v_norm_ffn_post, v_ffn_w_gate_up, v_ffn_w_down)))
    cx, cy, cc = _mesh_pos()
    chip = 2 * cx + cy

    local = dict(zip(BLOCKS, lax.optimization_barrier(
        [(w[n][layer].T if n == "dn_w_in" else w[n][layer]).astype(MXU_DTYPE) for n, layer in BLOCKS]), strict=True))

    def usable(block, got):
        got = lax.dynamic_update_slice(got, local[block][None], (chip, 0, 0))
        return got if COLUMN_SHARDED[block[0]] else got.reshape(-1, got.shape[-1])

    late = {"ffn0": [("ffn_w_gate_up", 0), ("ffn_w_down", 0)], "dn": [("dn_w_in", 0), ("dn_w_out", 0)],
            "ffn1": [("ffn_w_gate_up", 1), ("ffn_w_down", 1)]}
    first = [b for b in BLOCKS if all(b not in blks for blks in late.values())]
    first_got = _gather_blocks([local[b] for b in first], name="gather_weights")
    late_local, _ = lax.optimization_barrier(({k: [local[b] for b in blks] for k, blks in late.items()}, first_got[0]))
    started = {k: _gather_start(late_local[k], name=f"gather_start_{k}") for k in late}
    started_token = sum(s[4][0, 0] for s in started.values())

    def late_weights(key, after):
        lands = _gather_wait(started[key], after, name=f"gather_wait_{key}")
        return [usable(b, got) for b, got in zip(late[key], lands, strict=True)]

    def dn_weights(after):
        w_in, w_out = late_weights("dn", after)
        return _pad_dn_w_in(jnp.concatenate([w_in[s].T for s in range(N_CHIPS)], axis=1)), w_out

    full = {}
    for b, got in zip(first, first_got, strict=True):
        full.setdefault(b[0], []).append(usable(b, got))
    conv_rows = _small_rows([(DN_CONV, CONV_COLS)])
    conv_mine = jnp.where(cc == 0, 1.0, 0.0) * w["dn_conv"][0]
    conv_placed = lax.dynamic_update_slice(jnp.zeros((DN_CONV, CONV_COLS), F32), conv_mine, (0, chip * CONV_SHARD))
    conv_full = _unpack_small(_all_reduce_small(_pack_small([conv_placed], conv_rows), name="gather_conv"),
                              [(DN_CONV, CONV_COLS)])[0]
    p = {
        "rel_bias": w["rel_bias"], "att_w_in": full["att_w_in"][0], "att_w_out": full["att_w_out"][0],
        "dn_conv": conv_full, "dn_a_log": w["dn_a_log"][0], "dn_dt_bias": w["dn_dt_bias"][0],
        "dn_out_norm": w["dn_out_norm"][0], "mem_norm": w["mem_norm"], "mem_w_kv": full["mem_w_kv"],
        "norm_mix_pre": w["norm_mix_pre"] + started_token,
        "norm_mix_post": w["norm_mix_post"], "norm_ffn_pre": w["norm_ffn_pre"], "norm_ffn_post": w["norm_ffn_post"],
        "ffn_weights": lambda layer, after: late_weights(f"ffn{layer}", after), "dn_weights": dn_weights,
    }

    def chip_blocks(n, a):
        if n == "dn_w_in":
            a = _unpad_dn_w_in(a)
        if a.ndim == 3:
            return a
        if COLUMN_SHARDED[n]:
            return a.reshape(a.shape[0], N_CHIPS, -1).transpose(1, 0, 2)
        return a.reshape(N_CHIPS, -1, a.shape[-1])

    begun, begun_blocks = {}, {}

    def grads_ready(key, layer_grads):
        begun_blocks[key] = list(layer_grads)
        begun[key] = _scatter_start([chip_blocks(n, a) for (n, _), a in layer_grads.items()], name=f"rs_scatter_start_{key}")
        return begun[key][4][0, 0]

    p["grads_ready"] = grads_ready
    loss_cols, grad_x, g = _local_step(x[0], mem[0], loss_target[0], p)
    loss = lax.psum(jnp.sum(loss_cols), ("x", "y", "c"))
    grads_ready("att_in", {("att_w_in", 0): g["att_w_in"]})
    finished = _reduce_finish(begun, {k: [f"{n}{layer}" for n, layer in blks] for k, blks in begun_blocks.items()}, grad_x)
    reduced = {b: r for k in begun for b, r in zip(begun_blocks[k], finished[k], strict=True)}
    grads = {n: jnp.concatenate([reduced[b] for b in BLOCKS if b[0] == n], axis=0).reshape(shape) for n, shape, _ in BIG}
    small_full_shapes = [(DN_CONV, CONV_COLS) if n == "dn_conv" else w[n].shape for n in SMALL_NAMES]
    small_sum = _all_reduce_small(_pack_small([g[n] for n in SMALL_NAMES], _small_rows(small_full_shapes)), name="reduce_small")
    for n, s in zip(SMALL_NAMES, _unpack_small(small_sum, small_full_shapes)):
        grads[n] = lax.dynamic_slice(s, (0, chip * CONV_SHARD), (DN_CONV, CONV_SHARD))[None] if n == "dn_conv" else s

    delta, new_m, new_v = {}, {}, {}
    for n in BIG_NAMES:
        shape = w[n].shape
        two_d = (lambda a: a[0].T) if n == "dn_w_in" else (lambda a: a.reshape(-1, shape[-1]))
        back = (lambda a: a.T[None]) if n == "dn_w_in" else (lambda a: a.reshape(shape))
        g_2d = two_d(grads[n])
        res = _adamw(two_d(w[n]), g_2d, two_d(m[n]), two_d(v[n]), name=f"adamw_{n}")
        grads[n], delta[n], new_m[n], new_v[n] = (back(r) for r in (g_2d, *res))
    small_shapes = [w[n].shape for n in SMALL_NAMES]
    rows = _small_rows(small_shapes)
    res = _adamw(*[_pack_small([d[n] for n in SMALL_NAMES], rows) for d in (w, grads, m, v)], name="adamw_small")
    for d, r in zip((delta, new_m, new_v), res):
        for n, a in zip(SMALL_NAMES, _unpack_small(r, small_shapes)):
            d[n] = a
    return (loss, grad_x[None], *[grads[n] for n in WEIGHTS], *[delta[n] for n in WEIGHTS],
            *[new_m[n] for n in WEIGHTS], *[new_v[n] for n in WEIGHTS])
```

```python
import functools
import math

import numpy as np
import jax
import jax.numpy as jnp
from jax import lax
from jax.experimental import pallas as pl
from jax.experimental.pallas import tpu as pltpu

F32 = jnp.float32
MXU_DTYPE = jnp.bfloat16
LINK_DTYPE = jnp.bfloat16
HI = lax.Precision.HIGHEST

EPS = 1e-6
NEG_INF = -1e30
LANES = 128
SUBLANES = 8
VMEM_LIMIT = 56 * 1024 * 1024

D_MODEL = 1024
TOK_WIDTH = 768
MEM_WIDTH = 256
MEM_LEN = 256
ATT_HEAD_DIM = 64
DILATIONS = (1, 4, 16)
HALF = 64
ATT_BQ = 128
ATT_W = ATT_BQ + 2 * HALF
REL_BUCKETS = 32
REL_MAX_DIST = 1024
DN_HEADS = 6
DN_HEAD_DIM = 128
DN_CONV = 5
DN_CHUNK = 128
D_FF = 2816
ATT_IN = 2560
DN_IN = 3352
DN_IN_PAD = 3456
N_GATES = 4 * DN_HEADS

ADAM_LR = 0.001
ADAM_B1 = 0.9
ADAM_B2 = 0.999
ADAM_EPS = 1e-08
ADAM_WD = 0.01
ADAM_STEP = 10


def _tile(n, target, align):
    if n <= target:
        return n
    t = (target // align) * align
    while t >= align:
        if n % t == 0:
            return t
        t -= align
    raise ValueError(f"no tile for {n} (target {target}, align {align})")


def _params(*sem):
    return pltpu.CompilerParams(dimension_semantics=sem, vmem_limit_bytes=VMEM_LIMIT)


def _mm(a, b, *, name, ta=False, tb=False, out_shards=None, tm=1408, tn=1408, tk=1408, out_dtype=F32):
    if ta:
        K, M = a.shape
    else:
        M, K = a.shape
    sharded_b = b.ndim == 3
    if sharded_b:
        n_sh, b_rows, b_cols = b.shape
        N, K2 = (b_rows, n_sh * b_cols) if tb else (n_sh * b_cols, b_rows)
    else:
        N, K2 = b.shape if tb else b.shape[::-1]
    assert K == K2, (a.shape, b.shape, ta, tb)
    tm = _tile(M, tm, LANES if ta else SUBLANES)
    tn = N // out_shards if out_shards else (b_cols if sharded_b and not tb else _tile(N, tn, LANES))
    tk = b_cols if sharded_b and tb else _tile(K, tk, LANES)
    nk = K // tk
    a_spec = pl.BlockSpec((tk, tm), lambda i, j, k: (k, i)) if ta else pl.BlockSpec((tm, tk), lambda i, j, k: (i, k))
    if sharded_b:
        b_spec = (pl.BlockSpec((None, tn, tk), lambda i, j, k: (k, j, 0)) if tb
                  else pl.BlockSpec((None, tk, tn), lambda i, j, k: (j, k, 0)))
    else:
        b_spec = pl.BlockSpec((tn, tk), lambda i, j, k: (j, k)) if tb else pl.BlockSpec((tk, tn), lambda i, j, k: (k, j))
    if out_shards:
        out_spec = pl.BlockSpec((None, tm, tn), lambda i, j, k: (j, i, 0))
        out_shape = jax.ShapeDtypeStruct((out_shards, M, tn), out_dtype)
    else:
        out_spec = pl.BlockSpec((tm, tn), lambda i, j, k: (i, j))
        out_shape = jax.ShapeDtypeStruct((M, N), out_dtype)
    dims = (((0 if ta else 1,), (1 if tb else 0,)), ((), ()))

    def kern(a_ref, b_ref, o_ref, acc_ref):
        k = pl.program_id(2)

        @pl.when(k == 0)
        def _():
            acc_ref[...] = jnp.zeros_like(acc_ref)

        acc_ref[...] += lax.dot_general(a_ref[...].astype(MXU_DTYPE), b_ref[...].astype(MXU_DTYPE), dims,
                                        preferred_element_type=F32)

        @pl.when(k == nk - 1)
        def _():
            o_ref[...] = acc_ref[...].astype(o_ref.dtype)

    return pl.pallas_call(
        kern, name=name, grid=(M // tm, N // tn, nk), in_specs=[a_spec, b_spec],
        out_specs=out_spec, out_shape=out_shape,
        scratch_shapes=[pltpu.VMEM((tm, tn), F32)],
        compiler_params=_params("parallel", "parallel", "arbitrary"),
    )(a, b)


def _shards_of(w):
    return w.shape[0] if w.ndim == 3 else None


def _col(arr, width, blk):
    return (arr, width, blk)


def _rowwise(body, rows, consts, out_rows, out_acc, *, tm, name):
    n_rows = (rows[0][0] if isinstance(rows[0], tuple) else rows[0]).shape[0]
    assert n_rows % tm == 0, (n_rows, tm)
    arrs, in_specs = [], []
    for r in rows:
        arr, width, blk = r if isinstance(r, tuple) else (r, r.shape[1], 0)
        assert arr.shape[0] == n_rows
        arrs.append(arr)
        in_specs.append(pl.BlockSpec((tm, width), functools.partial(lambda i, b: (i, b), b=blk)))
    for c in consts:
        arrs.append(c)
        in_specs.append(pl.BlockSpec(c.shape, functools.partial(lambda i, n: (0,) * n, n=c.ndim)))
    n_in, n_ro = len(arrs), len(out_rows)
    out_shape = [jax.ShapeDtypeStruct((n_rows, w), dt) for w, dt in out_rows]
    out_specs = [pl.BlockSpec((tm, w), lambda i: (i, 0)) for w, _ in out_rows]
    out_shape += [jax.ShapeDtypeStruct(s, F32) for s in out_acc]
    out_specs += [pl.BlockSpec(s, lambda i: (0, 0)) for s in out_acc]

    def kern(*refs):
        ro, ao = body(*[r[...] for r in refs[:n_in]])
        outs = refs[n_in:]
        for r, v in zip(outs[:n_ro], ro, strict=True):
            r[...] = v.astype(r.dtype)
        if out_acc:
            @pl.when(pl.program_id(0) == 0)
            def _():
                for r in outs[n_ro:]:
                    r[...] = jnp.zeros_like(r)

            for r, v in zip(outs[n_ro:], ao, strict=True):
                r[...] += v

    res = pl.pallas_call(
        kern, name=name, grid=(n_rows // tm,), in_specs=in_specs, out_specs=out_specs, out_shape=out_shape,
        compiler_params=_params("arbitrary" if out_acc else "parallel"),
    )(*arrs)
    return res


def _rms(x, gain):
    return x * lax.rsqrt(jnp.mean(x * x, axis=-1, keepdims=True) + EPS) * gain


def _silu(x):
    return x * jax.nn.sigmoid(x)


def _softplus(x):
    return jnp.maximum(x, 0.0) + jnp.log(1.0 + jnp.exp(-jnp.abs(x)))


def _dot_nt(a, b, precision=None):
    return lax.dot_general(a, b, (((1,), (1,)), ((), ())), preferred_element_type=F32, precision=precision)


def _dot_tn(a, b, precision=None):
    return lax.dot_general(a, b, (((0,), (0,)), ((), ())), preferred_element_type=F32, precision=precision)


def _dot(a, b, precision=None):
    return jnp.dot(a, b, preferred_element_type=F32, precision=precision)


def _pre_norm(x, gain, *, name):
    def body(x, g):
        return (_rms(x, g),), ()
    return _rowwise(body, [x], [gain], [(x.shape[1], MXU_DTYPE)], [], tm=_tile(x.shape[0], 512, 2 * SUBLANES), name=name)[0]


def _pre_norm_bwd(x, gain, dh, dx_other, *, name):
    def body(x, dh, dxo, g):
        _, vjp = jax.vjp(_rms, x, g)
        dx, dg = vjp(dh)
        return (dx + dxo,), (dg,)
    return _rowwise(body, [x, dh, dx_other], [gain], [(x.shape[1], F32)], [gain.shape], tm=512, name=name)


def _gain_bwd(x, gain, dh, *, name):
    def body(x, dh, g):
        _, vjp = jax.vjp(lambda g_: _rms(x, g_), g)
        return (), (vjp(dh)[0],)
    return _rowwise(body, [x, dh], [gain], [], [gain.shape], tm=_tile(x.shape[0], 512, SUBLANES), name=name)[0]


def _res_block(x_res, m, g_post, g_pre):
    x_new = x_res + _rms(m, g_post)
    return x_new, _rms(x_new, g_pre)


def _post_pre(x_res, m, g_post, g_pre, *, name):
    def body(x, m, gp, gq):
        return _res_block(x, m, gp, gq), ()
    d = x_res.shape[1]
    return _rowwise(body, [x_res, m], [g_post, g_pre], [(d, F32), (d, MXU_DTYPE)], [], tm=512, name=name)


def _post_pre_bwd(x_res, m, g_post, g_pre, dx_new, dh, *, name):
    def body(x, m, dxn, dh, gp, gq):
        _, vjp = jax.vjp(_res_block, x, m, gp, gq)
        dx, dm, dgp, dgq = vjp((dxn, dh))
        return (dx, dm), (dgp, dgq)
    d = x_res.shape[1]
    return _rowwise(body, [x_res, m, dx_new, dh], [g_post, g_pre], [(d, F32), (d, MXU_DTYPE)],
                    [g_post.shape, g_pre.shape], tm=256, name=name)


def _final_loss_bwd(x_res, m, g_post, target, *, name):
    d = x_res.shape[1]

    def loss_cols(x, m, g, t):
        err = x + _rms(m, g) - t
        return jnp.sum(err * err, axis=0, keepdims=True) * (0.5 / d)

    def body(x, m, t, g):
        cols, vjp = jax.vjp(lambda x_, m_, g_: loss_cols(x_, m_, g_, t), x, m, g)
        dx, dm, dg = vjp(jnp.ones_like(cols))
        return (dx, dm), (dg, cols)
    return _rowwise(body, [x_res, m, target], [g_post], [(d, F32), (d, MXU_DTYPE)], [g_post.shape, (1, d)], tm=256, name=name)


def _swiglu_act(gu, *, name):
    def body(gate, up):
        return (_silu(gate.astype(F32)) * up.astype(F32),), ()
    return _rowwise(body, [_col(gu, D_FF, 0), _col(gu, D_FF, 1)], [], [(D_FF, MXU_DTYPE)], [], tm=256, name=name)[0]


def _swiglu_act_bwd(gu, da, *, name):
    def body(gate, up, da):
        _, vjp = jax.vjp(lambda g, u: _silu(g) * u, gate.astype(F32), up.astype(F32))
        dg, du = vjp(da.astype(F32))
        return (jnp.concatenate([dg, du], axis=1),), ()
    return _rowwise(body, [_col(gu, D_FF, 0), _col(gu, D_FF, 1), da], [], [(2 * D_FF, MXU_DTYPE)], [], tm=256, name=name)[0]


def _lane_head_mask(width, head_dim, head):
    lane = lax.broadcasted_iota(jnp.int32, (1, width), 1)
    return (lane // head_dim) == head


def _mem_attn_heads(q4, k4, v4):
    logits = _bdot_nt(q4, k4)
    p = jnp.exp(logits - jnp.max(logits, axis=-1, keepdims=True))
    return _bdot(p / jnp.sum(p, axis=-1, keepdims=True), v4)


def _mem_heads(q_mem, kv):
    return _heads(q_mem * (ATT_HEAD_DIM ** -0.5), mask=True), _heads(kv[:, :MEM_WIDTH]), _heads(kv[:, MEM_WIDTH:])


def _mem_attn(q_mem, kv):
    return _join_heads(_mem_attn_heads(*_mem_heads(q_mem, kv)))


def _mem_attn_bwd(q_mem, kv, do):
    _, vjp = jax.vjp(_mem_attn_heads, *_mem_heads(q_mem, kv))
    dq4, dk4, dv4 = vjp(_heads(do, mask=True))
    return (_join_heads(dq4) * (ATT_HEAD_DIM ** -0.5),
            jnp.concatenate([dk4[0] + dk4[1], dk4[2] + dk4[3], dv4[0] + dv4[1], dv4[2] + dv4[3]], axis=1))


def _t5_bucket(rel):
    half = REL_BUCKETS // 2
    max_exact = half // 2
    n = np.abs(rel)
    large = max_exact + (np.log(np.maximum(n, 1) / max_exact) / math.log(REL_MAX_DIST / max_exact)
                         * (half - max_exact)).astype(np.int64)
    large = np.minimum(large, half - 1)
    return ((rel > 0) * half + np.where(n < max_exact, n, large)).astype(np.int32)


ATT_DIAGS = ATT_BQ + ATT_W - 1


def _bias_diag_onehot(dil):
    j = np.arange(ATT_DIAGS)
    tiles = []
    for off in (-HALF, 0, HALF):
        rel = j - (ATT_BQ - 1) - HALF - off
        hot = _t5_bucket(rel * dil)[:, None] == np.arange(REL_BUCKETS)[None, :]
        tiles.append(hot & (np.abs(rel) <= HALF)[:, None])
    return np.stack(tiles).astype(np.float32)


def _toeplitz(r):
    lead = r.shape[:-1]
    a = jnp.broadcast_to(r[..., None, :], lead + (ATT_BQ, ATT_DIAGS))
    a = jnp.pad(a, [(0, 0)] * len(lead) + [(0, 0), (0, 1)])
    a = a.reshape(lead + (ATT_BQ * (ATT_DIAGS + 1),))[..., : ATT_BQ * ATT_DIAGS].reshape(lead + (ATT_BQ, ATT_DIAGS))
    return a[..., ATT_BQ - 1: ATT_BQ - 1 + ATT_W]


def _bias_tiles(rel_bias, gi):
    heads = rel_bias[:, 4 * gi: 4 * gi + 4]
    diag = jnp.einsum('tnb,bh->thn', jnp.asarray(_bias_diag_onehot(DILATIONS[gi])), heads, precision=HI)
    return _toeplitz(diag)


def _bias_tiles_bwd(rel_bias, dtiles, gi):
    return jax.vjp(lambda rb: _bias_tiles(rb, gi), rel_bias)[1](dtiles)[0]


def _att_window(i, n_sub):
    start = jnp.clip(i * ATT_BQ - HALF, 0, n_sub - ATT_W)
    off = i * ATT_BQ - HALF - start
    return pl.multiple_of(start, HALF), off


def _att_valid(off):
    q = lax.broadcasted_iota(jnp.int32, (ATT_BQ, ATT_W), 0)
    kk = lax.broadcasted_iota(jnp.int32, (ATT_BQ, ATT_W), 1)
    return jnp.abs(kk - q - HALF - off) <= HALF


def _att_tile_id(i, nq):
    return jnp.where(i == 0, 0, jnp.where(i == nq - 1, 2, 1))


ATT_GROUP_HEADS = 4


def _heads(x, mask=False):
    out = []
    for p in range(2):
        pair = x[:, p * LANES: (p + 1) * LANES]
        for h in range(2):
            out.append(jnp.where(_lane_head_mask(LANES, ATT_HEAD_DIM, h), pair, 0.0) if mask else pair)
    return jnp.stack(out)


def _join_heads(x):
    first = _lane_head_mask(LANES, ATT_HEAD_DIM, 0)
    return jnp.concatenate([jnp.where(first, x[2 * p], x[2 * p + 1]) for p in range(2)], axis=1)


def _head_scalar(x):
    out = []
    for p in range(2):
        pair = x[:, p * LANES: (p + 1) * LANES]
        for h in range(2):
            out.append(jnp.max(jnp.where(_lane_head_mask(LANES, ATT_HEAD_DIM, h), pair, NEG_INF), axis=-1, keepdims=True))
    return jnp.stack(out)


def _bdot(a, b):
    return jnp.einsum('hqk,hkd->hqd', a, b, preferred_element_type=F32)


def _bdot_nt(a, b):
    return jnp.einsum('hqd,hkd->hqk', a, b, preferred_element_type=F32)


def _bdot_tn(a, b):
    return jnp.einsum('hqk,hqd->hkd', a, b, preferred_element_type=F32)


def _att_fwd(qkvm, bias, gi, *, name):
    dil = DILATIONS[gi]
    s_len = qkvm.shape[0]
    n_sub = s_len // dil
    nq = n_sub // ATT_BQ
    assert n_sub % ATT_BQ == 0 and n_sub >= ATT_W
    cols = qkvm.shape[1] // (2 * LANES)
    view = qkvm.reshape(n_sub, dil * qkvm.shape[1])

    def kern(q_ref, k_ref, v_ref, b_ref, o_ref, lse_ref):
        start, off = _att_window(pl.program_id(1), n_sub)
        valid = _att_valid(off)
        q4 = _heads(q_ref[...].astype(F32) * (ATT_HEAD_DIM ** -0.5), mask=True)
        k4 = _heads(k_ref[pl.ds(start, ATT_W), :].astype(F32))
        v4 = _heads(v_ref[pl.ds(start, ATT_W), :].astype(F32))
        s = jnp.where(valid, _bdot_nt(q4, k4) + b_ref[...], NEG_INF)
        mx = jnp.max(s, axis=-1, keepdims=True)
        p = jnp.exp(s - mx)
        den = jnp.sum(p, axis=-1, keepdims=True)
        o_ref[...] = _join_heads(_bdot(p, v4) / den)
        lse_ref[...] = _join_heads(jnp.broadcast_to(mx + jnp.log(den), (ATT_GROUP_HEADS, ATT_BQ, LANES)))

    def qkv_spec(which, full):
        shape = (n_sub, 2 * LANES) if full else (ATT_BQ, 2 * LANES)
        return pl.BlockSpec(shape, lambda r, i: (0 if full else i, r * cols + which * 3 + gi))

    out_spec = pl.BlockSpec((ATT_BQ, 2 * LANES), lambda r, i: (i, r))
    o, lse = pl.pallas_call(
        kern, name=name, grid=(dil, nq),
        in_specs=[qkv_spec(0, False), qkv_spec(1, True), qkv_spec(2, True),
                  pl.BlockSpec((None, ATT_GROUP_HEADS, ATT_BQ, ATT_W), lambda r, i: (_att_tile_id(i, nq), 0, 0, 0))],
        out_specs=[out_spec, out_spec],
        out_shape=[jax.ShapeDtypeStruct((n_sub, dil * 2 * LANES), F32)] * 2,
        compiler_params=_params("parallel", "arbitrary"),
    )(view, view, view, bias)
    return o.reshape(s_len, 2 * LANES), lse.reshape(s_len, 2 * LANES)


def _att_bwd(qkvm, bias, lse_tot, delta, dcat, gi, *, name):
    dil = DILATIONS[gi]
    s_len = qkvm.shape[0]
    n_sub = s_len // dil
    nq = n_sub // ATT_BQ
    cols = qkvm.shape[1] // (2 * LANES)
    dcols = dcat.shape[1] // (2 * LANES)
    view = qkvm.reshape(n_sub, dil * qkvm.shape[1])
    lse_v = lse_tot.reshape(n_sub, dil * 2 * LANES)
    delta_v = delta.reshape(n_sub, dil * 2 * LANES)
    dcat_v = dcat.reshape(n_sub, dil * dcat.shape[1])

    def kern(q_ref, k_ref, v_ref, b_ref, lse_ref, dl_ref, dm_ref, dq_ref, dk_ref, dv_ref, db_ref):
        r, i = pl.program_id(0), pl.program_id(1)
        start, off = _att_window(i, n_sub)
        valid = _att_valid(off)
        tile = _att_tile_id(i, nq)

        @pl.when(i == 0)
        def _():
            dk_ref[...] = jnp.zeros_like(dk_ref)
            dv_ref[...] = jnp.zeros_like(dv_ref)

        @pl.when((i == 0) & (r == 0))
        def _():
            db_ref[...] = jnp.zeros_like(db_ref)

        q4 = _heads(q_ref[...].astype(F32) * (ATT_HEAD_DIM ** -0.5), mask=True)
        k4 = _heads(k_ref[pl.ds(start, ATT_W), :].astype(F32))
        v4 = _heads(v_ref[pl.ds(start, ATT_W), :].astype(F32))
        dm4 = _heads(dm_ref[...], mask=True)
        s = jnp.where(valid, _bdot_nt(q4, k4) + b_ref[tile], NEG_INF)
        p = jnp.exp(s - _head_scalar(lse_ref[...]))
        ds = p * (_bdot_nt(dm4, v4) - _head_scalar(dl_ref[...]))
        dq_ref[...] = _join_heads(_bdot(ds, k4)) * (ATT_HEAD_DIM ** -0.5)
        dk4 = _bdot_tn(ds, q4)
        dv4 = _bdot_tn(p, dm4)
        dk_ref[pl.ds(start, ATT_W), :] += jnp.concatenate([dk4[0] + dk4[1], dk4[2] + dk4[3]], axis=1)
        dv_ref[pl.ds(start, ATT_W), :] += jnp.concatenate([dv4[0] + dv4[1], dv4[2] + dv4[3]], axis=1)
        db_ref[tile] += ds

    def qkv_spec(which, full):
        shape = (n_sub, 2 * LANES) if full else (ATT_BQ, 2 * LANES)
        return pl.BlockSpec(shape, lambda r, i: (0 if full else i, r * cols + which * 3 + gi))

    blk = pl.BlockSpec((ATT_BQ, 2 * LANES), lambda r, i: (i, r))
    full = pl.BlockSpec((n_sub, 2 * LANES), lambda r, i: (0, r))
    bias_spec = pl.BlockSpec(bias.shape, lambda r, i: (0, 0, 0, 0))
    sub = jax.ShapeDtypeStruct((n_sub, dil * 2 * LANES), F32)
    dq, dk, dv, db = pl.pallas_call(
        kern, name=name, grid=(dil, nq),
        in_specs=[qkv_spec(0, False), qkv_spec(1, True), qkv_spec(2, True), bias_spec, blk, blk,
                  pl.BlockSpec((ATT_BQ, 2 * LANES), lambda r, i: (i, r * dcols + gi))],
        out_specs=[blk, full, full, bias_spec],
        out_shape=[sub, sub, sub, jax.ShapeDtypeStruct(bias.shape, F32)],
        compiler_params=_params("arbitrary", "arbitrary"),
    )(view, view, view, bias, lse_v, delta_v, dcat_v)
    return dq.reshape(s_len, -1), dk.reshape(s_len, -1), dv.reshape(s_len, -1), db


def _att_combine(o_g, lse_g, qkvm, kv_mem, *, name):
    def body(o0, o1, o2, l0, l1, l2, qm, kv):
        mx = jnp.maximum(jnp.maximum(l0, l1), l2)
        tot = mx + jnp.log(jnp.exp(l0 - mx) + jnp.exp(l1 - mx) + jnp.exp(l2 - mx))
        mixed = [o * jnp.exp(l - tot) for o, l in ((o0, l0), (o1, l1), (o2, l2))]
        return (jnp.concatenate(mixed + [_mem_attn(qm.astype(F32), kv)], axis=1), tot), ()
    return _rowwise(body, list(o_g) + list(lse_g) + [_col(qkvm, MEM_WIDTH, (3 * TOK_WIDTH) // MEM_WIDTH)], [kv_mem],
                    [(D_MODEL, F32), (MEM_WIDTH, F32)], [], tm=256, name=name)


def _head_sum_matrix():
    a = np.arange(MEM_WIDTH)
    return jnp.asarray((a[:, None] // ATT_HEAD_DIM == a[None, :] // ATT_HEAD_DIM).astype(np.float32))


def _att_bwd_prep(cat, dcat, qkvm, kv_mem, *, name):
    def body(cat, dcat, qm, kv, hs):
        prod = cat * dcat
        summed = prod[:, 0:256] + prod[:, 256:512] + prod[:, 512:768]
        delta = _dot(summed, hs, precision=HI)
        dqm, dkv = _mem_attn_bwd(qm.astype(F32), kv, dcat[:, TOK_WIDTH:])
        return (delta, dqm), (dkv,)
    return _rowwise(body, [cat, dcat, _col(qkvm, MEM_WIDTH, (3 * TOK_WIDTH) // MEM_WIDTH)], [kv_mem, _head_sum_matrix()],
                    [(MEM_WIDTH, F32), (MEM_WIDTH, F32)], [kv_mem.shape], tm=256, name=name)


def _dn_conv_post(s, j):
    scale = jnp.where(j < DN_HEADS, DN_HEAD_DIM ** -0.5, 1.0)
    normed = s * lax.rsqrt(jnp.sum(s * s, axis=-1, keepdims=True) + EPS) * scale
    return jnp.where(j >= 2 * DN_HEADS, s, normed)


def _shift_rows(x, sh):
    n = x.shape[0]
    row = lax.broadcasted_iota(jnp.int32, (n, 1), 0)
    rolled = pltpu.roll(x, (-sh) % n, 0)
    return jnp.where((row + sh >= 0) & (row + sh < n), rolled, 0.0)


def _dn_conv_taps(x, w_ref):
    c = x * w_ref[pl.ds(DN_CONV // 2, 1), :]
    for jj in range(DN_CONV):
        if jj != DN_CONV // 2:
            c = c + _shift_rows(x, jj - DN_CONV // 2) * w_ref[pl.ds(jj, 1), :]
    return c


def _dn_conv_fwd(proj, conv_w, *, name):
    s_len = proj.shape[0]
    width = 3 * TOK_WIDTH

    def kern(x_ref, w_ref, o_ref):
        j = pl.program_id(0)
        o_ref[...] = _dn_conv_post(_silu(_dn_conv_taps(x_ref[...], w_ref)), j)

    return pl.pallas_call(
        kern, name=name, grid=(width // LANES,),
        in_specs=[pl.BlockSpec((s_len, LANES), lambda j: (0, j)), pl.BlockSpec((DN_CONV, LANES), lambda j: (0, j))],
        out_specs=pl.BlockSpec((s_len, LANES), lambda j: (0, j)),
        out_shape=jax.ShapeDtypeStruct((s_len, width), F32),
        compiler_params=_params("parallel"),
    )(proj, conv_w)


def _dn_conv_bwd(proj, conv_w, d_fwd, d_bwd, which, *, name):
    s_len = proj.shape[0]

    def kern(x_ref, w_ref, df_ref, db_ref, dx_ref, dw_ref):
        j = pl.program_id(0) + which * DN_HEADS
        x = x_ref[...]
        c = _dn_conv_taps(x, w_ref)
        _, vjp = jax.vjp(lambda c_: _dn_conv_post(_silu(c_), j), c)
        dc = vjp(df_ref[...] + db_ref[...])[0]
        dx = dc * w_ref[pl.ds(DN_CONV // 2, 1), :]
        for jj in range(DN_CONV):
            sh = jj - DN_CONV // 2
            if sh != 0:
                dx = dx + _shift_rows(dc, -sh) * w_ref[pl.ds(jj, 1), :]
            dw_ref[pl.ds(jj, 1), :] = jnp.sum(dc * _shift_rows(x, sh), axis=0, keepdims=True)
        dx_ref[...] = dx

    return pl.pallas_call(
        kern, name=name, grid=(DN_HEADS,),
        in_specs=[pl.BlockSpec((s_len, LANES), lambda j: (0, j + which * DN_HEADS)),
                  pl.BlockSpec((DN_CONV, LANES), lambda j: (0, j + which * DN_HEADS)),
                  pl.BlockSpec((s_len, LANES), lambda j: (0, j)),
                  pl.BlockSpec((s_len, LANES), lambda j: (0, j))],
        out_specs=[pl.BlockSpec((s_len, LANES), lambda j: (0, j)), pl.BlockSpec((DN_CONV, LANES), lambda j: (0, j))],
        out_shape=[jax.ShapeDtypeStruct((s_len, TOK_WIDTH), F32), jax.ShapeDtypeStruct((DN_CONV, TOK_WIDTH), F32)],
        compiler_params=_params("parallel"),
    )(proj, conv_w, d_fwd, d_bwd)


GATE_TM = 2 * DN_CHUNK


FWD_GATE_LANES = 2 * DN_HEADS


def _gate_constants():
    i = np.arange(GATE_TM)
    same = (i[:, None] // DN_CHUNK) == (i[None, :] // DN_CHUNK)
    cum_f = same & (i[None, :] <= i[:, None])
    cum_r = same & (i[None, :] >= i[:, None])
    return tuple(jnp.asarray(np.asarray(a, np.float32)) for a in (cum_f, cum_r, same))


def _gate_params(p):
    z = jnp.zeros((DN_HEADS,), F32)
    return jnp.concatenate([p[0], z, p[1], z, jnp.zeros((LANES - N_GATES,), F32)]).reshape(1, LANES)


def _gate_params_bwd(dp):
    return jnp.stack([dp[0, 0:DN_HEADS], dp[0, 2 * DN_HEADS: 3 * DN_HEADS]])


def _dn_gates(gate_in, a_cols, dt_cols, cum_f, cum_r, tot):
    g = -jnp.exp(a_cols) * _softplus(gate_in + dt_cols)
    fwd_lane = lax.broadcasted_iota(jnp.int32, (1, LANES), 1) < FWD_GATE_LANES
    gc = jnp.where(fwd_lane, _dot(cum_f, g, precision=HI), _dot(cum_r, g, precision=HI))
    return gc, _dot(tot, g, precision=HI), jax.nn.sigmoid(gate_in)


def _dn_gates_fwd(proj, a_cols, dt_cols, *, name):
    def body(gi, *consts):
        return _dn_gates(gi, *consts), ()
    return _rowwise(body, [_col(proj, LANES, DN_IN_PAD // LANES - 1)], [a_cols, dt_cols, *_gate_constants()],
                    [(LANES, F32)] * 3, [], tm=GATE_TM, name=name)


def _dn_gates_bwd(proj, a_cols, dt_cols, d_gates, *, name):
    def body(gi, gcf, gtf, bf, gcr, gtr, br, a, dt, *consts):
        _, vjp = jax.vjp(lambda gi_, a_, dt_: _dn_gates(gi_, a_, dt_, *consts), gi, a, dt)
        dgi, da, ddt = vjp((gcf + gcr, gtf + gtr, bf + br))
        return (dgi,), (da, ddt)
    return _rowwise(body, [_col(proj, LANES, DN_IN_PAD // LANES - 1), *d_gates[0], *d_gates[1]],
                    [a_cols, dt_cols, *_gate_constants()], [(LANES, F32)], [a_cols.shape, dt_cols.shape],
                    tm=GATE_TM, name=name)


INV_BASE = 8


def _block_id_equal(c, size):
    i = lax.broadcasted_iota(jnp.int32, (c, c), 0) // size
    j = lax.broadcasted_iota(jnp.int32, (c, c), 1) // size
    return (i == j).astype(F32)


def _unit_tri_inverse_impl(lmat):
    c = lmat.shape[0]
    eye = _block_id_equal(c, 1)
    same = _block_id_equal(c, INV_BASE)
    neg = -lmat * same
    inv = eye + neg
    power = neg
    for _ in range(int(math.log2(INV_BASE)) - 1):
        power = _dot(power, power)
        inv = inv + _dot(inv, power)
    size = INV_BASE
    while size < c:
        bigger = _block_id_equal(c, 2 * size)
        inv = inv - _dot(_dot(inv, lmat * (bigger - same)), inv)
        same, size = bigger, 2 * size
    resid = eye - _dot(eye + lmat, inv, precision=HI)
    return inv + _dot(inv, resid)


@jax.custom_vjp
def _unit_tri_inverse(lmat):
    return _unit_tri_inverse_impl(lmat)


def _unit_tri_inverse_fwd(lmat):
    inv = _unit_tri_inverse_impl(lmat)
    return inv, inv


def _unit_tri_inverse_bwd(inv, d_inv):
    return (-_dot_tn(inv, _dot_nt(d_inv, inv)),)


_unit_tri_inverse.defvjp(_unit_tri_inverse_fwd, _unit_tri_inverse_bwd)


def _dn_chunk(q, k, v, gates_t, gc_row, tot_row, beta_row, state, tri, inverse):
    c = q.shape[0]
    assert c == DN_HEAD_DIM
    eye = _block_id_equal(c, 1)

    def along_rows(x, pick):
        return jnp.broadcast_to(jnp.sum(x * pick, axis=0, keepdims=True), (c, c))

    gc_j = along_rows(gates_t[0], gc_row)
    gc = gc_j.T
    g_tot = along_rows(gates_t[1], tot_row)
    beta = along_rows(gates_t[2], beta_row).T
    decay = jnp.exp(jnp.where(tri > 0, gc - gc_j, NEG_INF))
    k_beta = k * beta
    inv = inverse((tri - eye) * (_dot_nt(k_beta, k) * decay))
    e_gc = jnp.exp(gc)
    u = _dot(inv, v * beta)
    w = _dot(inv, k_beta * e_gc)
    intra = tri * (_dot_nt(q, k) * decay)
    v_new = u - _dot(w, state)
    out = _dot(q * e_gc, state) + _dot(intra, v_new)
    state = state * jnp.exp(g_tot) + _dot_tn(k * jnp.exp(g_tot - gc), v_new)
    return out, state


def _dn_tri():
    i = np.arange(DN_CHUNK)
    tri = np.stack([(i[None, :] <= i[:, None]), (i[None, :] >= i[:, None])]).astype(np.float32)
    return jnp.asarray(np.repeat(tri, DN_HEADS, axis=0))


def _dn_gate_picks():
    picks = np.zeros((3, 2 * DN_HEADS, 2 * DN_CHUNK, 1), np.float32)
    for d in range(2):
        for h in range(DN_HEADS):
            alpha = d * DN_CHUNK + d * 2 * DN_HEADS + h
            picks[0, d * DN_HEADS + h, alpha] = 1.0
            picks[1, d * DN_HEADS + h, alpha] = 1.0
            picks[2, d * DN_HEADS + h, alpha + DN_HEADS] = 1.0
    return jnp.asarray(picks)


def _stack_chains(fwd_ref, rev_ref):
    return jnp.stack([r[:, _head_cols(h)] for r in (fwd_ref, rev_ref) for h in range(DN_HEADS)])


def _unstack_chains(val, fwd_ref, rev_ref):
    for d, r in enumerate((fwd_ref, rev_ref)):
        for h in range(DN_HEADS):
            r[:, _head_cols(h)] = val[d * DN_HEADS + h]


def _gates_transposed(fwd_refs, rev_refs):
    return jnp.stack([jnp.concatenate([f[...].T, r[...].T], axis=0) for f, r in zip(fwd_refs, rev_refs, strict=True)])


def _dn_row_spec(nc, col, reverse, width=TOK_WIDTH):
    return pl.BlockSpec((DN_CHUNK, width), lambda t: ((nc - 1 - t) if reverse else t, col))


def _dn_state_spec(nc, reverse):
    return pl.BlockSpec((None, DN_HEADS, DN_HEAD_DIM, DN_HEAD_DIM), lambda t: ((nc - 1 - t) if reverse else t, 0, 0, 0))


def _head_cols(h):
    return pl.ds(h * DN_HEAD_DIM, DN_HEAD_DIM)


def _const_spec(arr):
    return pl.BlockSpec(arr.shape, functools.partial(lambda t, n: (0,) * n, n=arr.ndim))


def _dn_chains(inverse):
    return jax.vmap(lambda q, k, v, gates_t, *rest: _dn_chunk(q, k, v, gates_t, *rest, inverse),
                    in_axes=(0, 0, 0, None, 0, 0, 0, 0, 0))


def _dn_scan_fwd(qkv, gates, *, name):
    s_len = qkv.shape[0]
    nc = s_len // DN_CHUNK
    tri, picks = _dn_tri(), _dn_gate_picks()

    def kern(*refs):
        ins, (tri_ref, pick_ref, of_ref, or_ref, sf_ref, sr_ref, state) = refs[:12], refs[12:]

        @pl.when(pl.program_id(0) == 0)
        def _():
            state[...] = jnp.zeros_like(state)

        entry = state[...]
        qkv_c = [_stack_chains(ins[i], ins[6 + i]) for i in range(3)]
        gates_t = _gates_transposed(ins[3:6], ins[9:12])
        out, new = _dn_chains(_unit_tri_inverse_impl)(*qkv_c, gates_t, pick_ref[0], pick_ref[1], pick_ref[2], entry, tri_ref[...])
        sf_ref[...] = entry[:DN_HEADS]
        sr_ref[...] = entry[DN_HEADS:]
        _unstack_chains(out, of_ref, or_ref)
        state[...] = new

    in_specs = []
    for rev in (False, True):
        in_specs += [_dn_row_spec(nc, col, rev) for col in (0, 1, 2)] + [_dn_row_spec(nc, 0, rev, LANES)] * 3
    in_specs += [_const_spec(tri), _const_spec(picks)]
    return pl.pallas_call(
        kern, name=name, grid=(nc,), in_specs=in_specs,
        out_specs=[_dn_row_spec(nc, 0, False), _dn_row_spec(nc, 0, True), _dn_state_spec(nc, False), _dn_state_spec(nc, True)],
        out_shape=[jax.ShapeDtypeStruct((s_len, TOK_WIDTH), F32)] * 2
        + [jax.ShapeDtypeStruct((nc, DN_HEADS, DN_HEAD_DIM, DN_HEAD_DIM), F32)] * 2,
        scratch_shapes=[pltpu.VMEM((2 * DN_HEADS, DN_HEAD_DIM, DN_HEAD_DIM), F32)],
        compiler_params=_params("arbitrary"),
    )(*([qkv, qkv, qkv, *gates] * 2), tri, picks)


def _dn_scan_bwd(qkv, gates, states, d_o, *, name):
    s_len = qkv.shape[0]
    nc = s_len // DN_CHUNK
    tri, picks = _dn_tri(), _dn_gate_picks()

    def kern(*refs):
        ins, tri_ref, pick_ref, outs, d_state = refs[:16], refs[16], refs[17], refs[18:30], refs[30]

        @pl.when(pl.program_id(0) == 0)
        def _():
            d_state[...] = jnp.zeros_like(d_state)

        qkv_c = [_stack_chains(ins[i], ins[8 + i]) for i in range(3)]
        gates_t = _gates_transposed(ins[3:6], ins[11:14])
        entry = jnp.concatenate([ins[6][...], ins[14][...]], axis=0)
        d_out = _stack_chains(ins[7], ins[15])
        tri_v, picks_v = tri_ref[...], pick_ref[...]
        _, vjp = jax.vjp(lambda q, k, v, g, s: _dn_chains(_unit_tri_inverse)(q, k, v, g, picks_v[0], picks_v[1], picks_v[2], s, tri_v),
                         *qkv_c, gates_t, entry)
        dq, dk, dv, d_gates_t, d_entry = vjp((d_out, d_state[...]))
        for i, val in enumerate((dq, dk, dv)):
            _unstack_chains(val, outs[i], outs[6 + i])
        for i in range(3):
            outs[3 + i][...] = d_gates_t[i, :DN_CHUNK].T
            outs[9 + i][...] = d_gates_t[i, DN_CHUNK:].T
        d_state[...] = d_entry

    in_specs, out_specs, out_shape = [], [], []
    for rev in (True, False):
        in_specs += [_dn_row_spec(nc, col, rev) for col in (0, 1, 2)] + [_dn_row_spec(nc, 0, rev, LANES)] * 3
        in_specs += [_dn_state_spec(nc, rev), _dn_row_spec(nc, 0, rev)]
        out_specs += [_dn_row_spec(nc, 0, rev)] * 3 + [_dn_row_spec(nc, 0, rev, LANES)] * 3
        out_shape += [jax.ShapeDtypeStruct((s_len, TOK_WIDTH), F32)] * 3 + [jax.ShapeDtypeStruct((s_len, LANES), F32)] * 3
    in_specs += [_const_spec(tri), _const_spec(picks)]
    res = pl.pallas_call(
        kern, name=name, grid=(nc,), in_specs=in_specs, out_specs=out_specs, out_shape=out_shape,
        scratch_shapes=[pltpu.VMEM((2 * DN_HEADS, DN_HEAD_DIM, DN_HEAD_DIM), F32)],
        compiler_params=_params("arbitrary"),
    )(*[a for d in range(2) for a in (qkv, qkv, qkv, *gates, states[d], d_o)], tri, picks)
    return (res[0:3], res[3:6]), (res[6:9], res[9:12])


def _dn_out_head(o_f, o_b, z, gain):
    o = o_f + o_b
    return o * lax.rsqrt(jnp.mean(o * o, axis=-1, keepdims=True) + EPS) * gain * _silu(z)


def _dn_out(o_fwd, o_rev, proj, gain, qkv_kv_mem, *, name):
    def body(of, ob, z, qm, g, kv):
        heads = []
        for h in range(DN_HEADS):
            sl = slice(h * DN_HEAD_DIM, (h + 1) * DN_HEAD_DIM)
            heads.append(_dn_out_head(of[:, sl], ob[:, sl], z[:, sl], g))
        return (jnp.concatenate(heads + [_mem_attn(qm, kv)], axis=1),), ()
    return _rowwise(body, [o_fwd, o_rev, _col(proj, TOK_WIDTH, 3),
                           _col(proj, MEM_WIDTH, (4 * TOK_WIDTH) // MEM_WIDTH)], [gain, qkv_kv_mem],
                    [(D_MODEL, MXU_DTYPE)], [], tm=256, name=name)[0]


def _dn_out_bwd(o_fwd, o_rev, proj, gain, kv_mem, dcat, *, name):
    def body(of, ob, z, qm, dcat, g, kv):
        dos, dzs = [], []
        dgain = jnp.zeros_like(g)
        for h in range(DN_HEADS):
            sl = slice(h * DN_HEAD_DIM, (h + 1) * DN_HEAD_DIM)
            _, vjp = jax.vjp(_dn_out_head, of[:, sl], ob[:, sl], z[:, sl], g)
            d_of, _, dz, dg = vjp(dcat[:, sl])
            dos.append(d_of)
            dzs.append(dz)
            dgain = dgain + dg
        dqm, dkv = _mem_attn_bwd(qm, kv, dcat[:, TOK_WIDTH:])
        return (jnp.concatenate(dos, axis=1), jnp.concatenate(dzs, axis=1), dqm), (dgain, dkv)
    return _rowwise(body, [o_fwd, o_rev, _col(proj, TOK_WIDTH, 3),
                           _col(proj, MEM_WIDTH, (4 * TOK_WIDTH) // MEM_WIDTH), dcat], [gain, kv_mem],
                    [(TOK_WIDTH, F32), (TOK_WIDTH, F32), (MEM_WIDTH, F32)], [gain.shape, kv_mem.shape], tm=256, name=name)


def _pad_dn_w_in(w):
    gates = w[:, 4 * TOK_WIDTH: 4 * TOK_WIDTH + N_GATES]
    zeros = jnp.zeros((w.shape[0], DN_IN_PAD - DN_IN), w.dtype)
    return jnp.concatenate([w[:, :4 * TOK_WIDTH], w[:, 4 * TOK_WIDTH + N_GATES:], gates, zeros], axis=1)


def _unpad_dn_w_in(w):
    q_mem = w[:, 4 * TOK_WIDTH: 4 * TOK_WIDTH + MEM_WIDTH]
    gates = w[:, 4 * TOK_WIDTH + MEM_WIDTH: 4 * TOK_WIDTH + MEM_WIDTH + N_GATES]
    return jnp.concatenate([w[:, :4 * TOK_WIDTH], gates, q_mem], axis=1)


def _ffn_fwd(h, w_gu, w_d, tag):
    gu = _mm(h, w_gu, out_dtype=MXU_DTYPE, name=f"ffn_gu_{tag}")
    act = _swiglu_act(gu, name=f"ffn_act_{tag}")
    return gu, act, _mm(act, w_d, name=f"ffn_down_{tag}")


def _ffn_bwd(h, gu, act, w_gu, w_d, df, tag):
    d_act = _mm(df, w_d, tb=True, out_dtype=MXU_DTYPE, name=f"ffn_dact_{tag}")
    d_wd = _mm(act, df, ta=True, out_dtype=LINK_DTYPE, name=f"ffn_dwd_{tag}")
    d_gu = _swiglu_act_bwd(gu, d_act, name=f"ffn_dgu_{tag}")
    dh = _mm(d_gu, w_gu, tb=True, name=f"ffn_dh_{tag}")
    d_wgu = _mm(h, d_gu, ta=True, out_shards=_shards_of(w_gu), out_dtype=LINK_DTYPE, name=f"ffn_dwgu_{tag}")
    return dh, d_wgu, d_wd


def _local_step(x, mem, target, p):
    g = {}
    row = lambda v: v.reshape(1, -1)
    gains = {k: [row(p[k][i]) for i in range(2)] for k in
             ("mem_norm", "norm_mix_pre", "norm_mix_post", "norm_ffn_pre", "norm_ffn_post")}
    out_gain = row(p["dn_out_norm"])
    a_cols, dt_cols = _gate_params(p["dn_a_log"]), _gate_params(p["dn_dt_bias"])

    h0 = _pre_norm(x, gains["norm_mix_pre"][0], name="pre0")
    mem_n = [_pre_norm(mem, gains["mem_norm"][i], name=f"mem_norm{i}") for i in range(2)]
    qkvm = _mm(h0, p["att_w_in"], out_dtype=MXU_DTYPE, name="att_in")
    bias = [_bias_tiles(p["rel_bias"], gi) for gi in range(3)]
    att = [_att_fwd(qkvm, bias[gi], gi, name=f"att_fwd{gi}") for gi in range(3)]
    att_w_out, *mem_w_kv = p["att_rest_weights"](att[2][0])
    kv_mem = [_mm(mem_n[i], mem_w_kv[i], name=f"mem_kv{i}") for i in range(2)]
    cat0, lse_tot = _att_combine([a[0] for a in att], [a[1] for a in att], qkvm, kv_mem[0], name="att_combine")
    mo0 = _mm(cat0, att_w_out, name="att_out")
    x1, h1 = _post_pre(x, mo0, gains["norm_mix_post"][0], gains["norm_ffn_pre"][0], name="post_mix0")
    w_gu0, w_d0 = p["ffn_weights"](0, h1)
    gu0, act0, f0 = _ffn_fwd(h1, w_gu0, w_d0, 0)
    x2, h2 = _post_pre(x1, f0, gains["norm_ffn_post"][0], gains["norm_mix_pre"][1], name="post_ffn0")

    dn_w_in, dn_w_out = p["dn_weights"](h2)
    proj = _mm(h2, dn_w_in, name="dn_in")
    qkv = _dn_conv_fwd(proj, p["dn_conv"], name="dn_conv")
    gates = _dn_gates_fwd(proj, a_cols, dt_cols, name="dn_gates")
    o_fwd, o_rev, st_fwd, st_rev = _dn_scan_fwd(qkv, gates, name="dn_scan")
    cat1 = _dn_out(o_fwd, o_rev, proj, out_gain, kv_mem[1], name="dn_outnorm")
    mo1 = _mm(cat1, dn_w_out, name="dn_out")
    x3, h3 = _post_pre(x2, mo1, gains["norm_mix_post"][1], gains["norm_ffn_pre"][1], name="post_mix1")
    w_gu1, w_d1 = p["ffn_weights"](1, h3)
    gu1, act1, f1 = _ffn_fwd(h3, w_gu1, w_d1, 1)

    dx3, df1, dg_ffn_post1, loss_cols = _final_loss_bwd(x3, f1, gains["norm_ffn_post"][1], target, name="loss_bwd")
    dh3, d_wgu1, d_wd1 = _ffn_bwd(h3, gu1, act1, w_gu1, w_d1, df1, 1)
    sent = p["grads_ready"]("ffn1", {("ffn_w_gate_up", 1): d_wgu1, ("ffn_w_down", 1): d_wd1})
    dx2, dmo1, dg_mix_post1, dg_ffn_pre1 = _post_pre_bwd(x2, mo1, gains["norm_mix_post"][1] + sent, gains["norm_ffn_pre"][1],
                                                         dx3, dh3, name="post_mix1_bwd")
    dcat1 = _mm(dmo1, dn_w_out, tb=True, name="dn_out_dx")
    g["dn_w_out"] = _mm(cat1, dmo1, ta=True, out_dtype=LINK_DTYPE, name="dn_out_dw")
    d_o, dz, dqm1, d_out_gain, dkv1 = _dn_out_bwd(o_fwd, o_rev, proj, out_gain, kv_mem[1], dcat1, name="dn_outnorm_bwd")
    (d_f, dg_f), (d_r, dg_r) = _dn_scan_bwd(qkv, gates, (st_fwd, st_rev), d_o, name="dn_scan_bwd")
    d_gate_cols, d_a_cols, d_dt_cols = _dn_gates_bwd(proj, a_cols, dt_cols, (dg_f, dg_r), name="dn_gates_bwd")
    d_pre, d_conv = zip(*[_dn_conv_bwd(proj, p["dn_conv"], d_f[which], d_r[which], which, name=f"dn_conv_bwd{which}")
                          for which in range(3)])
    dproj = jnp.concatenate(list(d_pre) + [dz, dqm1, d_gate_cols], axis=1).astype(MXU_DTYPE)
    dh2 = _mm(dproj, dn_w_in, tb=True, name="dn_in_dx")
    g["dn_w_in"] = _mm(h2, dproj, ta=True, out_dtype=LINK_DTYPE, name="dn_in_dw")
    g["dn_conv"] = jnp.concatenate(d_conv, axis=1)
    g["dn_a_log"] = _gate_params_bwd(d_a_cols)
    g["dn_dt_bias"] = _gate_params_bwd(d_dt_cols)
    g["dn_out_norm"] = d_out_gain

    d_mem_kv1 = _mm(mem_n[1], dkv1, ta=True, out_dtype=LINK_DTYPE, name="mem_kv_dw1")
    sent = p["grads_ready"]("dn", {("dn_w_in", 0): g["dn_w_in"], ("dn_w_out", 0): g["dn_w_out"], ("mem_w_kv", 1): d_mem_kv1})
    dx1, df0, dg_ffn_post0, dg_mix_pre1 = _post_pre_bwd(x1, f0, gains["norm_ffn_post"][0] + sent, gains["norm_mix_pre"][1],
                                                        dx2, dh2, name="post_ffn0_bwd")
    dh1, d_wgu0, d_wd0 = _ffn_bwd(h1, gu0, act0, w_gu0, w_d0, df0, 0)
    sent = p["grads_ready"]("ffn0", {("ffn_w_gate_up", 0): d_wgu0, ("ffn_w_down", 0): d_wd0})
    dx0, dmo0, dg_mix_post0, dg_ffn_pre0 = _post_pre_bwd(x, mo0, gains["norm_mix_post"][0] + sent, gains["norm_ffn_pre"][0],
                                                         dx1, dh1, name="post_mix0_bwd")
    dcat0 = _mm(dmo0, att_w_out, tb=True, name="att_out_dx")
    g["att_w_out"] = _mm(cat0, dmo0, ta=True, out_dtype=LINK_DTYPE, name="att_out_dw")
    delta, dqm0, dkv0 = _att_bwd_prep(cat0, dcat0, qkvm, kv_mem[0], name="att_bwd_prep")
    d_mem_kv0 = _mm(mem_n[0], dkv0, ta=True, out_dtype=LINK_DTYPE, name="mem_kv_dw0")
    sent = p["grads_ready"]("att_out", {("att_w_out", 0): g["att_w_out"], ("mem_w_kv", 0): d_mem_kv0})
    att_b = [_att_bwd(qkvm, bias[gi] + sent, lse_tot, delta, dcat0, gi, name=f"att_bwd{gi}") for gi in range(3)]
    dqkvm = jnp.concatenate([a[w] for w in range(3) for a in att_b] + [dqm0], axis=1).astype(MXU_DTYPE)
    g["rel_bias"] = sum(_bias_tiles_bwd(p["rel_bias"], att_b[gi][3], gi) for gi in range(3))
    g["att_w_in"] = _mm(h0, dqkvm, ta=True, out_shards=_shards_of(p["att_w_in"]), out_dtype=LINK_DTYPE, name="att_in_dw")
    sent = p["grads_ready"]("att_in", {("att_w_in", 0): g["att_w_in"]})
    dh0 = _mm(dqkvm, p["att_w_in"], tb=True, name="att_in_dx")
    grad_x, dg_mix_pre0 = _pre_norm_bwd(x, gains["norm_mix_pre"][0] + sent, dh0, dx0, name="pre0_bwd")

    d_mem_norm = []
    for i, dkv in enumerate((dkv0, dkv1)):
        d_mem_n = _mm(dkv, mem_w_kv[i], tb=True, name=f"mem_kv_dx{i}")
        d_mem_norm.append(_gain_bwd(mem, gains["mem_norm"][i], d_mem_n, name=f"mem_norm_bwd{i}"))
    g["mem_w_kv"] = [d_mem_kv0, d_mem_kv1]
    g["mem_norm"] = jnp.concatenate(d_mem_norm, axis=0)
    g["norm_mix_pre"] = jnp.concatenate([dg_mix_pre0, dg_mix_pre1], axis=0)
    g["norm_mix_post"] = jnp.concatenate([dg_mix_post0, dg_mix_post1], axis=0)
    g["norm_ffn_pre"] = jnp.concatenate([dg_ffn_pre0, dg_ffn_pre1], axis=0)
    g["norm_ffn_post"] = jnp.concatenate([dg_ffn_post0, dg_ffn_post1], axis=0)
    g["ffn_w_gate_up"] = [d_wgu0, d_wgu1]
    g["ffn_w_down"] = [d_wd0, d_wd1]
    return loss_cols, grad_x, g


N_CHIPS = 4
N_DEV = 8
MESH = pl.DeviceIdType.MESH
BIG = (("att_w_in", (1, 1024, 640), 2), ("att_w_out", (1, 256, 1024), 1), ("dn_w_in", (1, 1024, 838), 2),
       ("dn_w_out", (1, 256, 1024), 1), ("mem_w_kv", (2, 256, 512), 1), ("ffn_w_gate_up", (2, 1024, 1408), 2),
       ("ffn_w_down", (2, 704, 1024), 1))


def _mesh_pos():
    return lax.axis_index("x"), lax.axis_index("y"), lax.axis_index("c")


def _other_chips(x, y):
    return [(1 - x, y), (x, 1 - y), (1 - x, 1 - y)]


ANY = pl.BlockSpec(memory_space=pl.ANY)


def _all_reduce_small(v, *, name):
    rows, cols = v.shape
    flips = [(dx, dy, dc) for dx in (0, 1) for dy in (0, 1) for dc in (0, 1)][1:]

    def body(v_ref, o_ref, buf, send_sems, recv_sems):
        x, y, c = _mesh_pos()

        def peer(f):
            return tuple(1 - p if fl else p for p, fl in zip((x, y, c), f))

        def index(p):
            return 4 * p[0] + 2 * p[1] + p[2]

        buf[index((x, y, c))] = v_ref[...]
        sends = []
        for k, f in enumerate(flips):
            cp = pltpu.make_async_remote_copy(src_ref=v_ref, dst_ref=buf.at[index((x, y, c))], send_sem=send_sems.at[k],
                                              recv_sem=recv_sems.at[k], device_id=peer(f), device_id_type=MESH)
            cp.start()
            sends.append(cp)
        for k, f in enumerate(flips):
            pltpu.make_async_remote_copy(src_ref=v_ref, dst_ref=buf.at[index(peer(f))], send_sem=send_sems.at[k],
                                         recv_sem=recv_sems.at[k], device_id=peer(f), device_id_type=MESH).wait_recv()
        for cp in sends:
            cp.wait_send()
        acc = buf[0]
        for d in range(1, N_DEV):
            acc = acc + buf[d]
        o_ref[...] = acc

    vmem = pl.BlockSpec(memory_space=pltpu.VMEM)
    return pl.pallas_call(
        body, name=name, in_specs=[vmem], out_specs=vmem, out_shape=jax.ShapeDtypeStruct((rows, cols), F32),
        scratch_shapes=[pltpu.VMEM((N_DEV, rows, cols), F32), pltpu.SemaphoreType.DMA((N_DEV - 1,)),
                        pltpu.SemaphoreType.DMA((N_DEV - 1,))],
    )(v)


def _adamw(w, g, m, v, *, name):
    def body(w, g, m, v):
        m = ADAM_B1 * m + (1.0 - ADAM_B1) * g
        v = ADAM_B2 * v + (1.0 - ADAM_B2) * (g * g)
        m_hat = m / (1.0 - ADAM_B1 ** ADAM_STEP)
        v_hat = v / (1.0 - ADAM_B2 ** ADAM_STEP)
        delta = -ADAM_LR * (m_hat / (jnp.sqrt(v_hat) + ADAM_EPS) + ADAM_WD * w)
        return (delta, m, v), ()
    rows, cols = w.shape
    if rows % SUBLANES == 0:
        return _rowwise(body, [w, g, m, v], [], [(cols, F32)] * 3, [], tm=_tile(rows, 256, SUBLANES), name=name)

    def kern(*refs):
        outs, _ = body(*[r[...] for r in refs[:4]])
        for r, val in zip(refs[4:], outs, strict=True):
            r[...] = val

    spec = pl.BlockSpec((rows, _tile(cols, 256, LANES)), lambda j: (0, j))
    return pl.pallas_call(
        kern, name=name, grid=(cols // spec.block_shape[1],), in_specs=[spec] * 4, out_specs=[spec] * 3,
        out_shape=[jax.ShapeDtypeStruct((rows, cols), F32)] * 3, compiler_params=_params("parallel"),
    )(w, g, m, v)


def _pack_small(arrs, rows):
    flat = jnp.concatenate([a.reshape(-1) for a in arrs])
    return jnp.pad(flat, (0, rows * LANES - flat.shape[0])).reshape(rows, LANES)


def _unpack_small(packed, shapes):
    flat = packed.reshape(-1)
    out, off = [], 0
    for s in shapes:
        size = math.prod(s)
        out.append(flat[off: off + size].reshape(s))
        off += size
    return out


def _small_rows(shapes):
    return -(-sum(math.prod(s) for s in shapes) // (SUBLANES * LANES)) * SUBLANES


def _sem_pairs(n):
    return [pltpu.SemaphoreType.DMA((n,)), pltpu.SemaphoreType.DMA((n,))]


def _gather_blocks(blocks, *, name):
    n = len(blocks)

    def body(*refs):
        x_refs, out_refs, (send_sems, recv_sems) = refs[:n], refs[n: 2 * n], refs[2 * n:]
        x, y, c = _mesh_pos()
        sibling = (x, y, 1 - c)
        chips = _other_chips(x, y)

        def copy(k, src, dst, to):
            return pltpu.make_async_remote_copy(src_ref=src, dst_ref=dst, send_sem=send_sems.at[k],
                                                recv_sem=recv_sems.at[k], device_id=to, device_id_type=MESH)

        def part(b, chip, h):
            half = blocks[b].shape[0] // 2
            return out_refs[b].at[2 * chip[0] + chip[1], pl.ds(h * half, half), :]

        def my_half(b):
            half = blocks[b].shape[0] // 2
            return x_refs[b].at[pl.ds(c * half, half), :]

        first = [copy(6 * b + j, my_half(b), part(b, (x, y), c), (*chip, c)) for b in range(n) for j, chip in enumerate(chips)]
        for cp in first:
            cp.start()
        passed = []
        for b in range(n):
            for j, chip in enumerate(chips):
                copy(6 * b + j, my_half(b), part(b, chip, c), (*chip, c)).wait_recv()
                cp = copy(6 * b + 3 + j, part(b, chip, c), part(b, chip, c), sibling)
                cp.start()
                passed.append(cp)
        for b in range(n):
            for j, chip in enumerate(chips):
                copy(6 * b + 3 + j, part(b, chip, 1 - c), part(b, chip, 1 - c), sibling).wait_recv()
        for cp in first + passed:
            cp.wait_send()

    return pl.pallas_call(
        body, name=name, in_specs=[ANY] * n, out_specs=[ANY] * n,
        out_shape=[jax.ShapeDtypeStruct((N_CHIPS, *a.shape), a.dtype) for a in blocks],
        scratch_shapes=_sem_pairs(6 * n),
    )(*blocks)


HBM = pl.BlockSpec(memory_space=pltpu.HBM)
SEM = pl.BlockSpec(memory_space=pltpu.SEMAPHORE)
DATAFLOW = pltpu.SideEffectType.DATAFLOW_SIDE_EFFECTING


def _gather_start(blocks, *, name):
    n = len(blocks)
    lands = [lax.empty((N_CHIPS, *a.shape), a.dtype) for a in blocks]

    def body(*refs):
        x_refs, land_refs, send_sems, recv_sems, token = refs[:n], refs[n: 2 * n], refs[2 * n], refs[2 * n + 1], refs[-1]
        x, y, c = _mesh_pos()
        for b in range(n):
            for j, chip in enumerate(_other_chips(x, y)):
                pltpu.make_async_remote_copy(src_ref=x_refs[b], dst_ref=land_refs[b].at[2 * x + y], send_sem=send_sems.at[3 * b + j],
                                             recv_sem=recv_sems.at[3 * b + j], device_id=(*chip, c), device_id_type=MESH).start()
        token[...] = jnp.zeros_like(token)

    operands = [pltpu.with_memory_space_constraint(a, pltpu.HBM) for a in blocks + lands]
    res = pl.pallas_call(
        body, name=name, in_specs=[HBM] * (2 * n),
        out_shape=(pltpu.SemaphoreType.DMA((3 * n,)), pltpu.SemaphoreType.DMA((3 * n,)),
                   *[pltpu.HBM(a.shape, a.dtype) for a in operands], jax.ShapeDtypeStruct((SUBLANES, LANES), F32)),
        out_specs=(SEM, SEM, *[HBM] * (2 * n), pl.BlockSpec(memory_space=pltpu.VMEM)),
        input_output_aliases={i: 2 + i for i in range(2 * n)},
        compiler_params=pltpu.CompilerParams(has_side_effects=DATAFLOW),
    )(*operands)
    return res[0], res[1], list(res[2: 2 + n]), list(res[2 + n: 2 + 2 * n]), res[-1]


def _gather_wait(started, after, *, name):
    send_sems, recv_sems, blocks, lands, _ = started
    n = len(blocks)

    def body(*refs):
        x_refs, land_refs, send_sems, recv_sems = refs[:n], refs[n: 2 * n], refs[2 * n], refs[2 * n + 1]
        x, y, c = _mesh_pos()
        for b in range(n):
            for j, chip in enumerate(_other_chips(x, y)):
                cp = pltpu.make_async_remote_copy(src_ref=x_refs[b], dst_ref=land_refs[b].at[2 * chip[0] + chip[1]],
                                                  send_sem=send_sems.at[3 * b + j], recv_sem=recv_sems.at[3 * b + j],
                                                  device_id=(*chip, c), device_id_type=MESH)
                cp.wait_send()
                cp.wait_recv()

    res = pl.pallas_call(
        body, name=name, in_specs=(*[HBM] * (2 * n), SEM, SEM, ANY),
        out_shape=tuple(pltpu.HBM(a.shape, a.dtype) for a in blocks + lands), out_specs=tuple([HBM] * (2 * n)),
        input_output_aliases={i: i for i in range(2 * n)},
        compiler_params=pltpu.CompilerParams(has_side_effects=DATAFLOW),
    )(*blocks, *lands, send_sems, recv_sems, after)
    return list(res[n:])


def _swap_halves(blocks, own_rows, *, name):
    n = len(blocks)

    def body(*refs):
        in_refs, out_refs, (send_sems, recv_sems) = refs[:n], refs[n: 2 * n], refs[2 * n:]
        x, y, c = _mesh_pos()
        copies = [pltpu.make_async_remote_copy(src_ref=own_rows(in_refs[b], c), dst_ref=out_refs[b], send_sem=send_sems.at[b],
                                               recv_sem=recv_sems.at[b], device_id=(x, y, 1 - c), device_id_type=MESH)
                  for b in range(n)]
        for cp in copies:
            cp.start()
        for cp in copies:
            cp.wait()

    def sent_shape(a):
        return jax.eval_shape(lambda r: own_rows(r, 0), a)

    return pl.pallas_call(
        body, name=name, in_specs=[ANY] * n, out_specs=[ANY] * n,
        out_shape=[jax.ShapeDtypeStruct(sent_shape(a).shape, a.dtype) for a in blocks],
        scratch_shapes=_sem_pairs(n),
    )(*blocks)


def _whole(ref, c):
    return ref


def _sum_chips_block(parts, *, name):
    n, half, cols = parts.shape
    tm = _tile(half, 512, 2 * SUBLANES)

    def kern(p_ref, o_ref):
        acc = p_ref[0].astype(F32)
        for s in range(1, n):
            acc = acc + p_ref[s].astype(F32)
        o_ref[...] = acc

    return pl.pallas_call(
        kern, name=name, grid=(half // tm,),
        in_specs=[pl.BlockSpec((n, tm, cols), lambda i: (0, i, 0))],
        out_specs=pl.BlockSpec((tm, cols), lambda i: (i, 0)),
        out_shape=jax.ShapeDtypeStruct((half, cols), F32),
        compiler_params=_params("parallel"),
    )(parts)


PEER_FLIPS = [(dx, dy, dc) for dx in (0, 1) for dy in (0, 1) for dc in (0, 1)][1:]


def _flipped(pos, flip):
    return tuple(1 - p if f else p for p, f in zip(pos, flip))


def _device_index(pos):
    return 4 * pos[0] + 2 * pos[1] + pos[2]


def _scatter_start(blocks, *, name):
    n = len(blocks)
    lands = [lax.empty((N_DEV, a.shape[1] // 2, a.shape[2]), a.dtype) for a in blocks]

    def body(*refs):
        g_refs, land_refs, send_sems, recv_sems, token = refs[:n], refs[n: 2 * n], refs[2 * n], refs[2 * n + 1], refs[-1]
        pos = _mesh_pos()
        for b in range(n):
            half = blocks[b].shape[1] // 2
            for k, flip in enumerate(PEER_FLIPS):
                peer = _flipped(pos, flip)
                pltpu.make_async_remote_copy(src_ref=g_refs[b].at[2 * peer[0] + peer[1], pl.ds(peer[2] * half, half), :],
                                             dst_ref=land_refs[b].at[_device_index(pos)],
                                             send_sem=send_sems.at[7 * b + k], recv_sem=recv_sems.at[7 * b + k],
                                             device_id=peer, device_id_type=MESH).start()
        token[...] = jnp.zeros_like(token)

    operands = [pltpu.with_memory_space_constraint(a, pltpu.HBM) for a in blocks + lands]
    res = pl.pallas_call(
        body, name=name, in_specs=[HBM] * (2 * n),
        out_shape=(pltpu.SemaphoreType.DMA((7 * n,)), pltpu.SemaphoreType.DMA((7 * n,)),
                   *[pltpu.HBM(a.shape, a.dtype) for a in operands], jax.ShapeDtypeStruct((SUBLANES, LANES), F32)),
        out_specs=(SEM, SEM, *[HBM] * (2 * n), pl.BlockSpec(memory_space=pltpu.VMEM)),
        input_output_aliases={i: 2 + i for i in range(2 * n)},
        compiler_params=pltpu.CompilerParams(has_side_effects=DATAFLOW),
    )(*operands)
    return res[0], res[1], list(res[2: 2 + n]), list(res[2 + n: 2 + 2 * n]), res[-1]


def _scatter_wait(started, after, *, name):
    send_sems, recv_sems, blocks, lands, _ = started
    n = len(blocks)

    def body(*refs):
        g_refs, land_refs, send_sems, recv_sems = refs[:n], refs[n: 2 * n], refs[2 * n], refs[2 * n + 1]
        pos = _mesh_pos()
        for b in range(n):
            half = blocks[b].shape[1] // 2
            for k, flip in enumerate(PEER_FLIPS):
                peer = _flipped(pos, flip)
                cp = pltpu.make_async_remote_copy(src_ref=g_refs[b].at[0, pl.ds(0, half), :],
                                                  dst_ref=land_refs[b].at[_device_index(peer)],
                                                  send_sem=send_sems.at[7 * b + k], recv_sem=recv_sems.at[7 * b + k],
                                                  device_id=peer, device_id_type=MESH)
                cp.wait_send()
                cp.wait_recv()

    res = pl.pallas_call(
        body, name=name, in_specs=(*[HBM] * (2 * n), SEM, SEM, ANY),
        out_shape=tuple(pltpu.HBM(a.shape, a.dtype) for a in blocks + lands), out_specs=tuple([HBM] * (2 * n)),
        input_output_aliases={i: i for i in range(2 * n)},
        compiler_params=pltpu.CompilerParams(has_side_effects=DATAFLOW),
    )(*blocks, *lands, send_sems, recv_sems, after)
    return list(res[:n]), list(res[n:])


def _reduce_finish(begun, names, after):
    x, y, c = _mesh_pos()
    mine = {}
    for key, started in begun.items():
        blocks, lands = _scatter_wait(started, after, name=f"rs_scatter_wait_{key}")
        parts = []
        for blk, land in zip(blocks, lands, strict=True):
            half = blk.shape[1] // 2
            own = lax.dynamic_slice(blk, (2 * x + y, c * half, 0), (1, half, blk.shape[2]))
            parts.append(lax.dynamic_update_slice(land, own, (_device_index((x, y, c)), 0, 0)))
        mine[key] = [_sum_chips_block(p, name=f"rs_sum_{nm}") for p, nm in zip(parts, names[key], strict=True)]
    flat = [a for key in begun for a in mine[key]]
    other = iter(_swap_halves(flat, _whole, name="rs_join"))
    return {key: [jnp.concatenate([jnp.where(c == 0, a, b), jnp.where(c == 0, b, a)], axis=0)
                  for a, b in ((a, next(other)) for a in mine[key])] for key in begun}


WEIGHTS = ("rel_bias", "att_w_in", "att_w_out", "dn_w_in", "dn_conv", "dn_a_log", "dn_dt_bias", "dn_out_norm", "dn_w_out",
           "mem_norm", "mem_w_kv", "norm_mix_pre", "norm_mix_post", "norm_ffn_pre", "norm_ffn_post", "ffn_w_gate_up",
           "ffn_w_down")
BIG_NAMES = tuple(n for n, _, _ in BIG)
SMALL_NAMES = tuple(n for n in WEIGHTS if n not in BIG_NAMES)
CONV_COLS = 3 * TOK_WIDTH
CONV_SHARD = CONV_COLS // N_CHIPS
BLOCKS = tuple((n, layer) for n, shape, _ in BIG for layer in range(shape[0]))
COLUMN_SHARDED = {n: axis == 2 for n, _, axis in BIG}


def kernel(x, mem, rel_bias, att_w_in, att_w_out, dn_w_in, dn_conv, dn_a_log, dn_dt_bias, dn_out_norm, dn_w_out, mem_norm, mem_w_kv, norm_mix_pre, norm_mix_post, norm_ffn_pre, norm_ffn_post, ffn_w_gate_up, ffn_w_down, loss_target, m_rel_bias, m_att_w_in, m_att_w_out, m_dn_w_in, m_dn_conv, m_dn_a_log, m_dn_dt_bias, m_dn_out_norm, m_dn_w_out, m_mem_norm, m_mem_w_kv, m_norm_mix_pre, m_norm_mix_post, m_norm_ffn_pre, m_norm_ffn_post, m_ffn_w_gate_up, m_ffn_w_down, v_rel_bias, v_att_w_in, v_att_w_out, v_dn_w_in, v_dn_conv, v_dn_a_log, v_dn_dt_bias, v_dn_out_norm, v_dn_w_out, v_mem_norm, v_mem_w_kv, v_norm_mix_pre, v_norm_mix_post, v_norm_ffn_pre, v_norm_ffn_post, v_ffn_w_gate_up, v_ffn_w_down):
    w = dict(zip(WEIGHTS, (rel_bias, att_w_in, att_w_out, dn_w_in, dn_conv, dn_a_log, dn_dt_bias, dn_out_norm, dn_w_out,
                           mem_norm, mem_w_kv, norm_mix_pre, norm_mix_post, norm_ffn_pre, norm_ffn_post, ffn_w_gate_up,
                           ffn_w_down)))
    m = dict(zip(WEIGHTS, (m_rel_bias, m_att_w_in, m_att_w_out, m_dn_w_in, m_dn_conv, m_dn_a_log, m_dn_dt_bias,
                           m_dn_out_norm, m_dn_w_out, m_mem_norm, m_mem_w_kv, m_norm_mix_pre, m_norm_mix_post,
                           m_norm_ffn_pre, m_norm_ffn_post, m_ffn_w_gate_up, m_ffn_w_down)))
    v = dict(zip(WEIGHTS, (v_rel_bias, v_att_w_in, v_att_w_out, v_dn_w_in, v_dn_conv, v_dn_a_log, v_dn_dt_bias,
                           v_dn_out_norm, v_dn_w_out, v_mem_norm, v_mem_w_kv, v_norm_mix_pre, v_norm_mix_post,
                           v_norm_ffn_pre, v_norm_ffn_post, v_ffn_w_gate_up, v_ffn_w_down)))
    cx, cy, cc = _mesh_pos()
    chip = 2 * cx + cy

    local = dict(zip(BLOCKS, lax.optimization_barrier(
        [(w[n][layer].T if n == "dn_w_in" else w[n][layer]).astype(MXU_DTYPE) for n, layer in BLOCKS]), strict=True))

    def usable(block, got):
        got = lax.dynamic_update_slice(got, local[block][None], (chip, 0, 0))
        return got if COLUMN_SHARDED[block[0]] else got.reshape(-1, got.shape[-1])

    late = {"att_rest": [("att_w_out", 0), ("mem_w_kv", 0), ("mem_w_kv", 1)],
            "ffn0": [("ffn_w_gate_up", 0), ("ffn_w_down", 0)], "dn": [("dn_w_in", 0), ("dn_w_out", 0)],
            "ffn1": [("ffn_w_gate_up", 1), ("ffn_w_down", 1)]}
    first = [b for b in BLOCKS if all(b not in blks for blks in late.values())]
    first_got = _gather_blocks([local[b] for b in first], name="gather_weights")
    late_local, _ = lax.optimization_barrier(({k: [local[b] for b in blks] for k, blks in late.items()}, first_got[0]))
    started = {k: _gather_start(late_local[k], name=f"gather_start_{k}") for k in late}
    started_token = sum(s[4][0, 0] for s in started.values())

    def late_weights(key, after):
        lands = _gather_wait(started[key], after, name=f"gather_wait_{key}")
        return [usable(b, got) for b, got in zip(late[key], lands, strict=True)]

    def dn_weights(after):
        w_in, w_out = late_weights("dn", after)
        return _pad_dn_w_in(jnp.concatenate([w_in[s].T for s in range(N_CHIPS)], axis=1)), w_out

    full = {}
    for b, got in zip(first, first_got, strict=True):
        full.setdefault(b[0], []).append(usable(b, got))
    conv_rows = _small_rows([(DN_CONV, CONV_COLS)])
    conv_mine = jnp.where(cc == 0, 1.0, 0.0) * w["dn_conv"][0]
    conv_placed = lax.dynamic_update_slice(jnp.zeros((DN_CONV, CONV_COLS), F32), conv_mine, (0, chip * CONV_SHARD))
    conv_full = _unpack_small(_all_reduce_small(_pack_small([conv_placed], conv_rows), name="gather_conv"),
                              [(DN_CONV, CONV_COLS)])[0]
    p = {
        "rel_bias": w["rel_bias"], "att_w_in": full["att_w_in"][0], "att_rest_weights": lambda after: late_weights("att_rest", after),
        "dn_conv": conv_full, "dn_a_log": w["dn_a_log"][0], "dn_dt_bias": w["dn_dt_bias"][0],
        "dn_out_norm": w["dn_out_norm"][0], "mem_norm": w["mem_norm"],
        "norm_mix_pre": w["norm_mix_pre"] + started_token,
        "norm_mix_post": w["norm_mix_post"], "norm_ffn_pre": w["norm_ffn_pre"], "norm_ffn_post": w["norm_ffn_post"],
        "ffn_weights": lambda layer, after: late_weights(f"ffn{layer}", after), "dn_weights": dn_weights,
    }

    def chip_blocks(n, a):
        if n == "dn_w_in":
            a = _unpad_dn_w_in(a)
        if a.ndim == 3:
            return a
        if COLUMN_SHARDED[n]:
            return a.reshape(a.shape[0], N_CHIPS, -1).transpose(1, 0, 2)
        return a.reshape(N_CHIPS, -1, a.shape[-1])

    begun, begun_blocks = {}, {}

    def grads_ready(key, layer_grads):
        begun_blocks[key] = list(layer_grads)
        begun[key] = _scatter_start([chip_blocks(n, a) for (n, _), a in layer_grads.items()], name=f"rs_scatter_start_{key}")
        return begun[key][4][0, 0]

    p["grads_ready"] = grads_ready
    loss_cols, grad_x, g = _local_step(x[0], mem[0], loss_target[0], p)
    loss = lax.psum(jnp.sum(loss_cols), ("x", "y", "c"))
    finished = _reduce_finish(begun, {k: [f"{n}{layer}" for n, layer in blks] for k, blks in begun_blocks.items()}, grad_x)
    reduced = {b: r for k in begun for b, r in zip(begun_blocks[k], finished[k], strict=True)}
    grads = {n: jnp.concatenate([reduced[b] for b in BLOCKS if b[0] == n], axis=0).reshape(shape) for n, shape, _ in BIG}
    small_full_shapes = [(DN_CONV, CONV_COLS) if n == "dn_conv" else w[n].shape for n in SMALL_NAMES]
    small_sum = _all_reduce_small(_pack_small([g[n] for n in SMALL_NAMES], _small_rows(small_full_shapes)), name="reduce_small")
    for n, s in zip(SMALL_NAMES, _unpack_small(small_sum, small_full_shapes)):
        grads[n] = lax.dynamic_slice(s, (0, chip * CONV_SHARD), (DN_CONV, CONV_SHARD))[None] if n == "dn_conv" else s

    delta, new_m, new_v = {}, {}, {}
    for n in BIG_NAMES:
        shape = w[n].shape
        two_d = (lambda a: a[0].T) if n == "dn_w_in" else (lambda a: a.reshape(-1, shape[-1]))
        back = (lambda a: a.T[None]) if n == "dn_w_in" else (lambda a: a.reshape(shape))
        g_2d = two_d(grads[n])
        res = _adamw(two_d(w[n]), g_2d, two_d(m[n]), two_d(v[n]), name=f"adamw_{n}")
        grads[n], delta[n], new_m[n], new_v[n] = (back(r) for r in (g_2d, *res))
    small_shapes = [w[n].shape for n in SMALL_NAMES]
    rows = _small_rows(small_shapes)
    res = _adamw(*[_pack_small([d[n] for n in SMALL_NAMES], rows) for d in (w, grads, m, v)], name="adamw_small")
    for d, r in zip((delta, new_m, new_v), res):
        for n, a in zip(SMALL_NAMES, _unpack_small(r, small_shapes)):
            d[n] = a
    return (loss, grad_x[None], *[grads[n] for n in WEIGHTS], *[delta[n] for n in WEIGHTS],
            *[new_m[n] for n in WEIGHTS], *[new_v[n] for n in WEIGHTS])
```

```python
import functools
import math

import numpy as np
import jax
import jax.numpy as jnp
from jax import lax
from jax.experimental import pallas as pl
from jax.experimental.pallas import tpu as pltpu

F32 = jnp.float32
MXU_DTYPE = jnp.bfloat16
LINK_DTYPE = jnp.bfloat16
HI = lax.Precision.HIGHEST

EPS = 1e-6
NEG_INF = -1e30
LANES = 128
SUBLANES = 8
VMEM_LIMIT = 56 * 1024 * 1024
MM_WHOLE_K_BUDGET = 44 * 1024 * 1024

D_MODEL = 1024
TOK_WIDTH = 768
MEM_WIDTH = 256
MEM_LEN = 256
ATT_HEAD_DIM = 64
DILATIONS = (1, 4, 16)
HALF = 64
ATT_BQ = 128
ATT_W = ATT_BQ + 2 * HALF
REL_BUCKETS = 32
REL_MAX_DIST = 1024
DN_HEADS = 6
DN_HEAD_DIM = 128
DN_CONV = 5
DN_CHUNK = 128
D_FF = 2816
ATT_IN = 2560
DN_IN = 3352
DN_IN_PAD = 3456
N_GATES = 4 * DN_HEADS

ADAM_LR = 0.001
ADAM_B1 = 0.9
ADAM_B2 = 0.999
ADAM_EPS = 1e-08
ADAM_WD = 0.01
ADAM_STEP = 10


def _tile(n, target, align):
    if n <= target:
        return n
    t = (target // align) * align
    while t >= align:
        if n % t == 0:
            return t
        t -= align
    raise ValueError(f"no tile for {n} (target {target}, align {align})")


def _params(*sem):
    return pltpu.CompilerParams(dimension_semantics=sem, vmem_limit_bytes=VMEM_LIMIT)


def _mm(a, b, *, name, ta=False, tb=False, out_shards=None, tm=1408, tn=1408, tk=1408, out_dtype=F32):
    if ta:
        K, M = a.shape
    else:
        M, K = a.shape
    sharded_b = b.ndim == 3
    if sharded_b:
        n_sh, b_rows, b_cols = b.shape
        N, K2 = (b_rows, n_sh * b_cols) if tb else (n_sh * b_cols, b_rows)
    else:
        N, K2 = b.shape if tb else b.shape[::-1]
    assert K == K2, (a.shape, b.shape, ta, tb)
    tm = _tile(M, tm, LANES if ta else SUBLANES)
    tn = N // out_shards if out_shards else (b_cols if sharded_b and not tb else _tile(N, tn, LANES))
    tk = b_cols if sharded_b and tb else _tile(K, tk, LANES)

    def vmem_bytes(tk_):
        return 2 * (tm * tk_ * a.dtype.itemsize + tk_ * tn * b.dtype.itemsize + tm * tn * jnp.dtype(out_dtype).itemsize)

    if not (sharded_b and tb) and vmem_bytes(K) <= MM_WHOLE_K_BUDGET:
        tk = K
    nk = K // tk
    a_spec = pl.BlockSpec((tk, tm), lambda i, j, k: (k, i)) if ta else pl.BlockSpec((tm, tk), lambda i, j, k: (i, k))
    if sharded_b:
        b_spec = (pl.BlockSpec((None, tn, tk), lambda i, j, k: (k, j, 0)) if tb
                  else pl.BlockSpec((None, tk, tn), lambda i, j, k: (j, k, 0)))
    else:
        b_spec = pl.BlockSpec((tn, tk), lambda i, j, k: (j, k)) if tb else pl.BlockSpec((tk, tn), lambda i, j, k: (k, j))
    if out_shards:
        out_spec = pl.BlockSpec((None, tm, tn), lambda i, j, k: (j, i, 0))
        out_shape = jax.ShapeDtypeStruct((out_shards, M, tn), out_dtype)
    else:
        out_spec = pl.BlockSpec((tm, tn), lambda i, j, k: (i, j))
        out_shape = jax.ShapeDtypeStruct((M, N), out_dtype)
    dims = (((0 if ta else 1,), (1 if tb else 0,)), ((), ()))

    def product(a_ref, b_ref):
        return lax.dot_general(a_ref[...].astype(MXU_DTYPE), b_ref[...].astype(MXU_DTYPE), dims, preferred_element_type=F32)

    def kern_whole(a_ref, b_ref, o_ref):
        o_ref[...] = product(a_ref, b_ref).astype(o_ref.dtype)

    def kern_steps(a_ref, b_ref, o_ref, acc_ref):
        k = pl.program_id(2)

        @pl.when(k == 0)
        def _():
            acc_ref[...] = jnp.zeros_like(acc_ref)

        acc_ref[...] += product(a_ref, b_ref)

        @pl.when(k == nk - 1)
        def _():
            o_ref[...] = acc_ref[...].astype(o_ref.dtype)

    return pl.pallas_call(
        kern_whole if nk == 1 else kern_steps, name=name, grid=(M // tm, N // tn, nk), in_specs=[a_spec, b_spec],
        out_specs=out_spec, out_shape=out_shape,
        scratch_shapes=[] if nk == 1 else [pltpu.VMEM((tm, tn), F32)],
        compiler_params=_params("parallel", "parallel", "arbitrary"),
    )(a, b)


def _shards_of(w):
    return w.shape[0] if w.ndim == 3 else None


def _col(arr, width, blk):
    return (arr, width, blk)


def _rowwise(body, rows, consts, out_rows, out_acc, *, tm, name):
    n_rows = (rows[0][0] if isinstance(rows[0], tuple) else rows[0]).shape[0]
    assert n_rows % tm == 0, (n_rows, tm)
    arrs, in_specs = [], []
    for r in rows:
        arr, width, blk = r if isinstance(r, tuple) else (r, r.shape[1], 0)
        assert arr.shape[0] == n_rows
        arrs.append(arr)
        in_specs.append(pl.BlockSpec((tm, width), functools.partial(lambda i, b: (i, b), b=blk)))
    for c in consts:
        arrs.append(c)
        in_specs.append(pl.BlockSpec(c.shape, functools.partial(lambda i, n: (0,) * n, n=c.ndim)))
    n_in, n_ro = len(arrs), len(out_rows)
    out_shape = [jax.ShapeDtypeStruct((n_rows, w), dt) for w, dt in out_rows]
    out_specs = [pl.BlockSpec((tm, w), lambda i: (i, 0)) for w, _ in out_rows]
    out_shape += [jax.ShapeDtypeStruct(s, F32) for s in out_acc]
    out_specs += [pl.BlockSpec(s, lambda i: (0, 0)) for s in out_acc]

    def kern(*refs):
        ro, ao = body(*[r[...] for r in refs[:n_in]])
        outs = refs[n_in:]
        for r, v in zip(outs[:n_ro], ro, strict=True):
            r[...] = v.astype(r.dtype)
        if out_acc:
            @pl.when(pl.program_id(0) == 0)
            def _():
                for r in outs[n_ro:]:
                    r[...] = jnp.zeros_like(r)

            for r, v in zip(outs[n_ro:], ao, strict=True):
                r[...] += v

    res = pl.pallas_call(
        kern, name=name, grid=(n_rows // tm,), in_specs=in_specs, out_specs=out_specs, out_shape=out_shape,
        compiler_params=_params("arbitrary" if out_acc else "parallel"),
    )(*arrs)
    return res


def _rms(x, gain):
    return x * lax.rsqrt(jnp.mean(x * x, axis=-1, keepdims=True) + EPS) * gain


def _silu(x):
    return x * jax.nn.sigmoid(x)


def _softplus(x):
    return jnp.maximum(x, 0.0) + jnp.log(1.0 + jnp.exp(-jnp.abs(x)))


def _dot_nt(a, b, precision=None):
    return lax.dot_general(a, b, (((1,), (1,)), ((), ())), preferred_element_type=F32, precision=precision)


def _dot_tn(a, b, precision=None):
    return lax.dot_general(a, b, (((0,), (0,)), ((), ())), preferred_element_type=F32, precision=precision)


def _dot(a, b, precision=None):
    return jnp.dot(a, b, preferred_element_type=F32, precision=precision)


def _pre_norm(x, gain, *, name):
    def body(x, g):
        return (_rms(x, g),), ()
    return _rowwise(body, [x], [gain], [(x.shape[1], MXU_DTYPE)], [], tm=_tile(x.shape[0], 512, 2 * SUBLANES), name=name)[0]


def _pre_norm_bwd(x, gain, dh, dx_other, *, name):
    def body(x, dh, dxo, g):
        _, vjp = jax.vjp(_rms, x, g)
        dx, dg = vjp(dh)
        return (dx + dxo,), (dg,)
    return _rowwise(body, [x, dh, dx_other], [gain], [(x.shape[1], F32)], [gain.shape], tm=512, name=name)


def _gain_bwd(x, gain, dh, *, name):
    def body(x, dh, g):
        _, vjp = jax.vjp(lambda g_: _rms(x, g_), g)
        return (), (vjp(dh)[0],)
    return _rowwise(body, [x, dh], [gain], [], [gain.shape], tm=_tile(x.shape[0], 512, SUBLANES), name=name)[0]


def _res_block(x_res, m, g_post, g_pre):
    x_new = x_res + _rms(m, g_post)
    return x_new, _rms(x_new, g_pre)


def _post_pre(x_res, m, g_post, g_pre, *, name):
    def body(x, m, gp, gq):
        return _res_block(x, m, gp, gq), ()
    d = x_res.shape[1]
    return _rowwise(body, [x_res, m], [g_post, g_pre], [(d, F32), (d, MXU_DTYPE)], [], tm=512, name=name)


def _post_pre_bwd(x_res, m, g_post, g_pre, dx_new, dh, *, name):
    def body(x, m, dxn, dh, gp, gq):
        _, vjp = jax.vjp(_res_block, x, m, gp, gq)
        dx, dm, dgp, dgq = vjp((dxn, dh))
        return (dx, dm), (dgp, dgq)
    d = x_res.shape[1]
    return _rowwise(body, [x_res, m, dx_new, dh], [g_post, g_pre], [(d, F32), (d, MXU_DTYPE)],
                    [g_post.shape, g_pre.shape], tm=256, name=name)


def _final_loss_bwd(x_res, m, g_post, target, *, name):
    d = x_res.shape[1]

    def loss_cols(x, m, g, t):
        err = x + _rms(m, g) - t
        return jnp.sum(err * err, axis=0, keepdims=True) * (0.5 / d)

    def body(x, m, t, g):
        cols, vjp = jax.vjp(lambda x_, m_, g_: loss_cols(x_, m_, g_, t), x, m, g)
        dx, dm, dg = vjp(jnp.ones_like(cols))
        return (dx, dm), (dg, cols)
    return _rowwise(body, [x_res, m, target], [g_post], [(d, F32), (d, MXU_DTYPE)], [g_post.shape, (1, d)], tm=256, name=name)


def _swiglu_act(gu, *, name):
    def body(gate, up):
        return (_silu(gate.astype(F32)) * up.astype(F32),), ()
    return _rowwise(body, [_col(gu, D_FF, 0), _col(gu, D_FF, 1)], [], [(D_FF, MXU_DTYPE)], [], tm=256, name=name)[0]


def _swiglu_act_bwd(gu, da, *, name):
    def body(gate, up, da):
        _, vjp = jax.vjp(lambda g, u: _silu(g) * u, gate.astype(F32), up.astype(F32))
        dg, du = vjp(da.astype(F32))
        return (jnp.concatenate([dg, du], axis=1),), ()
    return _rowwise(body, [_col(gu, D_FF, 0), _col(gu, D_FF, 1), da], [], [(2 * D_FF, MXU_DTYPE)], [], tm=256, name=name)[0]


def _lane_head_mask(width, head_dim, head):
    lane = lax.broadcasted_iota(jnp.int32, (1, width), 1)
    return (lane // head_dim) == head


def _mem_attn_heads(q4, k4, v4):
    logits = _bdot_nt(q4, k4)
    p = jnp.exp(logits - jnp.max(logits, axis=-1, keepdims=True))
    return _bdot(p / jnp.sum(p, axis=-1, keepdims=True), v4)


def _mem_heads(q_mem, kv):
    return _heads(q_mem * (ATT_HEAD_DIM ** -0.5), mask=True), _heads(kv[:, :MEM_WIDTH]), _heads(kv[:, MEM_WIDTH:])


def _mem_attn(q_mem, kv):
    return _join_heads(_mem_attn_heads(*_mem_heads(q_mem, kv)))


def _mem_attn_bwd(q_mem, kv, do):
    _, vjp = jax.vjp(_mem_attn_heads, *_mem_heads(q_mem, kv))
    dq4, dk4, dv4 = vjp(_heads(do, mask=True))
    return (_join_heads(dq4) * (ATT_HEAD_DIM ** -0.5),
            jnp.concatenate([dk4[0] + dk4[1], dk4[2] + dk4[3], dv4[0] + dv4[1], dv4[2] + dv4[3]], axis=1))


def _t5_bucket(rel):
    half = REL_BUCKETS // 2
    max_exact = half // 2
    n = np.abs(rel)
    large = max_exact + (np.log(np.maximum(n, 1) / max_exact) / math.log(REL_MAX_DIST / max_exact)
                         * (half - max_exact)).astype(np.int64)
    large = np.minimum(large, half - 1)
    return ((rel > 0) * half + np.where(n < max_exact, n, large)).astype(np.int32)


ATT_DIAGS = ATT_BQ + ATT_W - 1


def _bias_diag_onehot(dil):
    j = np.arange(ATT_DIAGS)
    tiles = []
    for off in (-HALF, 0, HALF):
        rel = j - (ATT_BQ - 1) - HALF - off
        hot = _t5_bucket(rel * dil)[:, None] == np.arange(REL_BUCKETS)[None, :]
        tiles.append(hot & (np.abs(rel) <= HALF)[:, None])
    return np.stack(tiles).astype(np.float32)


def _toeplitz(r):
    lead = r.shape[:-1]
    a = jnp.broadcast_to(r[..., None, :], lead + (ATT_BQ, ATT_DIAGS))
    a = jnp.pad(a, [(0, 0)] * len(lead) + [(0, 0), (0, 1)])
    a = a.reshape(lead + (ATT_BQ * (ATT_DIAGS + 1),))[..., : ATT_BQ * ATT_DIAGS].reshape(lead + (ATT_BQ, ATT_DIAGS))
    return a[..., ATT_BQ - 1: ATT_BQ - 1 + ATT_W]


def _bias_tiles(rel_bias, gi):
    heads = rel_bias[:, 4 * gi: 4 * gi + 4]
    diag = jnp.einsum('tnb,bh->thn', jnp.asarray(_bias_diag_onehot(DILATIONS[gi])), heads, precision=HI)
    return _toeplitz(diag)


def _bias_tiles_bwd(rel_bias, dtiles, gi):
    return jax.vjp(lambda rb: _bias_tiles(rb, gi), rel_bias)[1](dtiles)[0]


def _att_window(i, n_sub):
    start = jnp.clip(i * ATT_BQ - HALF, 0, n_sub - ATT_W)
    off = i * ATT_BQ - HALF - start
    return pl.multiple_of(start, HALF), off


def _att_valid(off):
    q = lax.broadcasted_iota(jnp.int32, (ATT_BQ, ATT_W), 0)
    kk = lax.broadcasted_iota(jnp.int32, (ATT_BQ, ATT_W), 1)
    return jnp.abs(kk - q - HALF - off) <= HALF


def _att_tile_id(i, nq):
    return jnp.where(i == 0, 0, jnp.where(i == nq - 1, 2, 1))


ATT_GROUP_HEADS = 4


def _heads(x, mask=False):
    out = []
    for p in range(2):
        pair = x[:, p * LANES: (p + 1) * LANES]
        for h in range(2):
            out.append(jnp.where(_lane_head_mask(LANES, ATT_HEAD_DIM, h), pair, 0.0) if mask else pair)
    return jnp.stack(out)


def _join_heads(x):
    first = _lane_head_mask(LANES, ATT_HEAD_DIM, 0)
    return jnp.concatenate([jnp.where(first, x[2 * p], x[2 * p + 1]) for p in range(2)], axis=1)


def _head_scalar(x):
    out = []
    for p in range(2):
        pair = x[:, p * LANES: (p + 1) * LANES]
        for h in range(2):
            out.append(jnp.max(jnp.where(_lane_head_mask(LANES, ATT_HEAD_DIM, h), pair, NEG_INF), axis=-1, keepdims=True))
    return jnp.stack(out)


def _bdot(a, b):
    return jnp.einsum('hqk,hkd->hqd', a, b, preferred_element_type=F32)


def _bdot_nt(a, b):
    return jnp.einsum('hqd,hkd->hqk', a, b, preferred_element_type=F32)


def _bdot_tn(a, b):
    return jnp.einsum('hqk,hqd->hkd', a, b, preferred_element_type=F32)


def _att_fwd(qkvm, bias, gi, *, name):
    dil = DILATIONS[gi]
    s_len = qkvm.shape[0]
    n_sub = s_len // dil
    nq = n_sub // ATT_BQ
    assert n_sub % ATT_BQ == 0 and n_sub >= ATT_W
    cols = qkvm.shape[1] // (2 * LANES)
    view = qkvm.reshape(n_sub, dil * qkvm.shape[1])

    def kern(q_ref, k_ref, v_ref, b_ref, o_ref, lse_ref):
        start, off = _att_window(pl.program_id(1), n_sub)
        valid = _att_valid(off)
        q4 = _heads(q_ref[...].astype(F32) * (ATT_HEAD_DIM ** -0.5), mask=True)
        k4 = _heads(k_ref[pl.ds(start, ATT_W), :].astype(F32))
        v4 = _heads(v_ref[pl.ds(start, ATT_W), :].astype(F32))
        s = jnp.where(valid, _bdot_nt(q4, k4) + b_ref[...], NEG_INF)
        mx = jnp.max(s, axis=-1, keepdims=True)
        p = jnp.exp(s - mx)
        den = jnp.sum(p, axis=-1, keepdims=True)
        o_ref[...] = _join_heads(_bdot(p, v4) / den)
        lse_ref[...] = _join_heads(jnp.broadcast_to(mx + jnp.log(den), (ATT_GROUP_HEADS, ATT_BQ, LANES)))

    def qkv_spec(which, full):
        shape = (n_sub, 2 * LANES) if full else (ATT_BQ, 2 * LANES)
        return pl.BlockSpec(shape, lambda r, i: (0 if full else i, r * cols + which * 3 + gi))

    out_spec = pl.BlockSpec((ATT_BQ, 2 * LANES), lambda r, i: (i, r))
    o, lse = pl.pallas_call(
        kern, name=name, grid=(dil, nq),
        in_specs=[qkv_spec(0, False), qkv_spec(1, True), qkv_spec(2, True),
                  pl.BlockSpec((None, ATT_GROUP_HEADS, ATT_BQ, ATT_W), lambda r, i: (_att_tile_id(i, nq), 0, 0, 0))],
        out_specs=[out_spec, out_spec],
        out_shape=[jax.ShapeDtypeStruct((n_sub, dil * 2 * LANES), F32)] * 2,
        compiler_params=_params("parallel", "arbitrary"),
    )(view, view, view, bias)
    return o.reshape(s_len, 2 * LANES), lse.reshape(s_len, 2 * LANES)


def _att_bwd(qkvm, bias, lse_tot, delta, dcat, gi, *, name):
    dil = DILATIONS[gi]
    s_len = qkvm.shape[0]
    n_sub = s_len // dil
    nq = n_sub // ATT_BQ
    cols = qkvm.shape[1] // (2 * LANES)
    dcols = dcat.shape[1] // (2 * LANES)
    view = qkvm.reshape(n_sub, dil * qkvm.shape[1])
    lse_v = lse_tot.reshape(n_sub, dil * 2 * LANES)
    delta_v = delta.reshape(n_sub, dil * 2 * LANES)
    dcat_v = dcat.reshape(n_sub, dil * dcat.shape[1])

    def kern(q_ref, k_ref, v_ref, b_ref, lse_ref, dl_ref, dm_ref, dq_ref, dk_ref, dv_ref, db_ref):
        r, i = pl.program_id(0), pl.program_id(1)
        start, off = _att_window(i, n_sub)
        valid = _att_valid(off)
        tile = _att_tile_id(i, nq)

        @pl.when(i == 0)
        def _():
            dk_ref[...] = jnp.zeros_like(dk_ref)
            dv_ref[...] = jnp.zeros_like(dv_ref)

        @pl.when((i == 0) & (r == 0))
        def _():
            db_ref[...] = jnp.zeros_like(db_ref)

        q4 = _heads(q_ref[...].astype(F32) * (ATT_HEAD_DIM ** -0.5), mask=True)
        k4 = _heads(k_ref[pl.ds(start, ATT_W), :].astype(F32))
        v4 = _heads(v_ref[pl.ds(start, ATT_W), :].astype(F32))
        dm4 = _heads(dm_ref[...], mask=True)
        s = jnp.where(valid, _bdot_nt(q4, k4) + b_ref[tile], NEG_INF)
        p = jnp.exp(s - _head_scalar(lse_ref[...]))
        ds = p * (_bdot_nt(dm4, v4) - _head_scalar(dl_ref[...]))
        dq_ref[...] = _join_heads(_bdot(ds, k4)) * (ATT_HEAD_DIM ** -0.5)
        dk4 = _bdot_tn(ds, q4)
        dv4 = _bdot_tn(p, dm4)
        dk_ref[pl.ds(start, ATT_W), :] += jnp.concatenate([dk4[0] + dk4[1], dk4[2] + dk4[3]], axis=1)
        dv_ref[pl.ds(start, ATT_W), :] += jnp.concatenate([dv4[0] + dv4[1], dv4[2] + dv4[3]], axis=1)
        db_ref[tile] += ds

    def qkv_spec(which, full):
        shape = (n_sub, 2 * LANES) if full else (ATT_BQ, 2 * LANES)
        return pl.BlockSpec(shape, lambda r, i: (0 if full else i, r * cols + which * 3 + gi))

    blk = pl.BlockSpec((ATT_BQ, 2 * LANES), lambda r, i: (i, r))
    full = pl.BlockSpec((n_sub, 2 * LANES), lambda r, i: (0, r))
    bias_spec = pl.BlockSpec(bias.shape, lambda r, i: (0, 0, 0, 0))
    sub = jax.ShapeDtypeStruct((n_sub, dil * 2 * LANES), F32)
    dq, dk, dv, db = pl.pallas_call(
        kern, name=name, grid=(dil, nq),
        in_specs=[qkv_spec(0, False), qkv_spec(1, True), qkv_spec(2, True), bias_spec, blk, blk,
                  pl.BlockSpec((ATT_BQ, 2 * LANES), lambda r, i: (i, r * dcols + gi))],
        out_specs=[blk, full, full, bias_spec],
        out_shape=[sub, sub, sub, jax.ShapeDtypeStruct(bias.shape, F32)],
        compiler_params=_params("arbitrary", "arbitrary"),
    )(view, view, view, bias, lse_v, delta_v, dcat_v)
    return dq.reshape(s_len, -1), dk.reshape(s_len, -1), dv.reshape(s_len, -1), db


def _att_combine(o_g, lse_g, qkvm, kv_mem, *, name):
    def body(o0, o1, o2, l0, l1, l2, qm, kv):
        mx = jnp.maximum(jnp.maximum(l0, l1), l2)
        tot = mx + jnp.log(jnp.exp(l0 - mx) + jnp.exp(l1 - mx) + jnp.exp(l2 - mx))
        mixed = [o * jnp.exp(l - tot) for o, l in ((o0, l0), (o1, l1), (o2, l2))]
        return (jnp.concatenate(mixed + [_mem_attn(qm.astype(F32), kv)], axis=1), tot), ()
    return _rowwise(body, list(o_g) + list(lse_g) + [_col(qkvm, MEM_WIDTH, (3 * TOK_WIDTH) // MEM_WIDTH)], [kv_mem],
                    [(D_MODEL, F32), (MEM_WIDTH, F32)], [], tm=256, name=name)


def _head_sum_matrix():
    a = np.arange(MEM_WIDTH)
    return jnp.asarray((a[:, None] // ATT_HEAD_DIM == a[None, :] // ATT_HEAD_DIM).astype(np.float32))


def _att_bwd_prep(cat, dcat, qkvm, kv_mem, *, name):
    def body(cat, dcat, qm, kv, hs):
        prod = cat * dcat
        summed = prod[:, 0:256] + prod[:, 256:512] + prod[:, 512:768]
        delta = _dot(summed, hs, precision=HI)
        dqm, dkv = _mem_attn_bwd(qm.astype(F32), kv, dcat[:, TOK_WIDTH:])
        return (delta, dqm), (dkv,)
    return _rowwise(body, [cat, dcat, _col(qkvm, MEM_WIDTH, (3 * TOK_WIDTH) // MEM_WIDTH)], [kv_mem, _head_sum_matrix()],
                    [(MEM_WIDTH, F32), (MEM_WIDTH, F32)], [kv_mem.shape], tm=256, name=name)


def _dn_conv_post(s, j):
    scale = jnp.where(j < DN_HEADS, DN_HEAD_DIM ** -0.5, 1.0)
    normed = s * lax.rsqrt(jnp.sum(s * s, axis=-1, keepdims=True) + EPS) * scale
    return jnp.where(j >= 2 * DN_HEADS, s, normed)


def _shift_rows(x, sh):
    n = x.shape[0]
    row = lax.broadcasted_iota(jnp.int32, (n, 1), 0)
    rolled = pltpu.roll(x, (-sh) % n, 0)
    return jnp.where((row + sh >= 0) & (row + sh < n), rolled, 0.0)


def _dn_conv_taps(x, w_ref):
    c = x * w_ref[pl.ds(DN_CONV // 2, 1), :]
    for jj in range(DN_CONV):
        if jj != DN_CONV // 2:
            c = c + _shift_rows(x, jj - DN_CONV // 2) * w_ref[pl.ds(jj, 1), :]
    return c


def _dn_conv_fwd(proj, conv_w, *, name):
    s_len = proj.shape[0]
    width = 3 * TOK_WIDTH

    def kern(x_ref, w_ref, o_ref):
        j = pl.program_id(0)
        o_ref[...] = _dn_conv_post(_silu(_dn_conv_taps(x_ref[...], w_ref)), j)

    return pl.pallas_call(
        kern, name=name, grid=(width // LANES,),
        in_specs=[pl.BlockSpec((s_len, LANES), lambda j: (0, j)), pl.BlockSpec((DN_CONV, LANES), lambda j: (0, j))],
        out_specs=pl.BlockSpec((s_len, LANES), lambda j: (0, j)),
        out_shape=jax.ShapeDtypeStruct((s_len, width), F32),
        compiler_params=_params("parallel"),
    )(proj, conv_w)


def _dn_conv_bwd(proj, conv_w, d_fwd, d_bwd, which, *, name):
    s_len = proj.shape[0]

    def kern(x_ref, w_ref, df_ref, db_ref, dx_ref, dw_ref):
        j = pl.program_id(0) + which * DN_HEADS
        x = x_ref[...]
        c = _dn_conv_taps(x, w_ref)
        _, vjp = jax.vjp(lambda c_: _dn_conv_post(_silu(c_), j), c)
        dc = vjp(df_ref[...] + db_ref[...])[0]
        dx = dc * w_ref[pl.ds(DN_CONV // 2, 1), :]
        for jj in range(DN_CONV):
            sh = jj - DN_CONV // 2
            if sh != 0:
                dx = dx + _shift_rows(dc, -sh) * w_ref[pl.ds(jj, 1), :]
            dw_ref[pl.ds(jj, 1), :] = jnp.sum(dc * _shift_rows(x, sh), axis=0, keepdims=True)
        dx_ref[...] = dx

    return pl.pallas_call(
        kern, name=name, grid=(DN_HEADS,),
        in_specs=[pl.BlockSpec((s_len, LANES), lambda j: (0, j + which * DN_HEADS)),
                  pl.BlockSpec((DN_CONV, LANES), lambda j: (0, j + which * DN_HEADS)),
                  pl.BlockSpec((s_len, LANES), lambda j: (0, j)),
                  pl.BlockSpec((s_len, LANES), lambda j: (0, j))],
        out_specs=[pl.BlockSpec((s_len, LANES), lambda j: (0, j)), pl.BlockSpec((DN_CONV, LANES), lambda j: (0, j))],
        out_shape=[jax.ShapeDtypeStruct((s_len, TOK_WIDTH), F32), jax.ShapeDtypeStruct((DN_CONV, TOK_WIDTH), F32)],
        compiler_params=_params("parallel"),
    )(proj, conv_w, d_fwd, d_bwd)


GATE_TM = 2 * DN_CHUNK


FWD_GATE_LANES = 2 * DN_HEADS


def _gate_constants():
    i = np.arange(GATE_TM)
    same = (i[:, None] // DN_CHUNK) == (i[None, :] // DN_CHUNK)
    cum_f = same & (i[None, :] <= i[:, None])
    cum_r = same & (i[None, :] >= i[:, None])
    return tuple(jnp.asarray(np.asarray(a, np.float32)) for a in (cum_f, cum_r, same))


def _gate_params(p):
    z = jnp.zeros((DN_HEADS,), F32)
    return jnp.concatenate([p[0], z, p[1], z, jnp.zeros((LANES - N_GATES,), F32)]).reshape(1, LANES)


def _gate_params_bwd(dp):
    return jnp.stack([dp[0, 0:DN_HEADS], dp[0, 2 * DN_HEADS: 3 * DN_HEADS]])


def _dn_gates(gate_in, a_cols, dt_cols, cum_f, cum_r, tot):
    g = -jnp.exp(a_cols) * _softplus(gate_in + dt_cols)
    fwd_lane = lax.broadcasted_iota(jnp.int32, (1, LANES), 1) < FWD_GATE_LANES
    gc = jnp.where(fwd_lane, _dot(cum_f, g, precision=HI), _dot(cum_r, g, precision=HI))
    return gc, _dot(tot, g, precision=HI), jax.nn.sigmoid(gate_in)


def _dn_gates_fwd(proj, a_cols, dt_cols, *, name):
    def body(gi, *consts):
        return _dn_gates(gi, *consts), ()
    return _rowwise(body, [_col(proj, LANES, DN_IN_PAD // LANES - 1)], [a_cols, dt_cols, *_gate_constants()],
                    [(LANES, F32)] * 3, [], tm=GATE_TM, name=name)


def _dn_gates_bwd(proj, a_cols, dt_cols, d_gates, *, name):
    def body(gi, gcf, gtf, bf, gcr, gtr, br, a, dt, *consts):
        _, vjp = jax.vjp(lambda gi_, a_, dt_: _dn_gates(gi_, a_, dt_, *consts), gi, a, dt)
        dgi, da, ddt = vjp((gcf + gcr, gtf + gtr, bf + br))
        return (dgi,), (da, ddt)
    return _rowwise(body, [_col(proj, LANES, DN_IN_PAD // LANES - 1), *d_gates[0], *d_gates[1]],
                    [a_cols, dt_cols, *_gate_constants()], [(LANES, F32)], [a_cols.shape, dt_cols.shape],
                    tm=GATE_TM, name=name)


INV_BASE = 8


def _block_id_equal(c, size):
    i = lax.broadcasted_iota(jnp.int32, (c, c), 0) // size
    j = lax.broadcasted_iota(jnp.int32, (c, c), 1) // size
    return (i == j).astype(F32)


def _unit_tri_inverse_impl(lmat):
    c = lmat.shape[0]
    eye = _block_id_equal(c, 1)
    same = _block_id_equal(c, INV_BASE)
    neg = -lmat * same
    inv = eye + neg
    power = neg
    for _ in range(int(math.log2(INV_BASE)) - 1):
        power = _dot(power, power)
        inv = inv + _dot(inv, power)
    size = INV_BASE
    while size < c:
        bigger = _block_id_equal(c, 2 * size)
        inv = inv - _dot(_dot(inv, lmat * (bigger - same)), inv)
        same, size = bigger, 2 * size
    resid = eye - _dot(eye + lmat, inv, precision=HI)
    return inv + _dot(inv, resid)


@jax.custom_vjp
def _unit_tri_inverse(lmat):
    return _unit_tri_inverse_impl(lmat)


def _unit_tri_inverse_fwd(lmat):
    inv = _unit_tri_inverse_impl(lmat)
    return inv, inv


def _unit_tri_inverse_bwd(inv, d_inv):
    return (-_dot_tn(inv, _dot_nt(d_inv, inv)),)


_unit_tri_inverse.defvjp(_unit_tri_inverse_fwd, _unit_tri_inverse_bwd)


def _dn_chunk(q, k, v, gates_t, gc_row, tot_row, beta_row, state, tri, inverse):
    c = q.shape[0]
    assert c == DN_HEAD_DIM
    eye = _block_id_equal(c, 1)

    def along_rows(x, pick):
        return jnp.broadcast_to(jnp.sum(x * pick, axis=0, keepdims=True), (c, c))

    gc_j = along_rows(gates_t[0], gc_row)
    gc = gc_j.T
    g_tot = along_rows(gates_t[1], tot_row)
    beta = along_rows(gates_t[2], beta_row).T
    decay = jnp.exp(jnp.where(tri > 0, gc - gc_j, NEG_INF))
    k_beta = k * beta
    inv = inverse((tri - eye) * (_dot_nt(k_beta, k) * decay))
    e_gc = jnp.exp(gc)
    u = _dot(inv, v * beta)
    w = _dot(inv, k_beta * e_gc)
    intra = tri * (_dot_nt(q, k) * decay)
    v_new = u - _dot(w, state)
    out = _dot(q * e_gc, state) + _dot(intra, v_new)
    state = state * jnp.exp(g_tot) + _dot_tn(k * jnp.exp(g_tot - gc), v_new)
    return out, state


def _dn_tri():
    i = np.arange(DN_CHUNK)
    tri = np.stack([(i[None, :] <= i[:, None]), (i[None, :] >= i[:, None])]).astype(np.float32)
    return jnp.asarray(np.repeat(tri, DN_HEADS, axis=0))


def _dn_gate_picks():
    picks = np.zeros((3, 2 * DN_HEADS, 2 * DN_CHUNK, 1), np.float32)
    for d in range(2):
        for h in range(DN_HEADS):
            alpha = d * DN_CHUNK + d * 2 * DN_HEADS + h
            picks[0, d * DN_HEADS + h, alpha] = 1.0
            picks[1, d * DN_HEADS + h, alpha] = 1.0
            picks[2, d * DN_HEADS + h, alpha + DN_HEADS] = 1.0
    return jnp.asarray(picks)


def _stack_chains(fwd_ref, rev_ref):
    return jnp.stack([r[:, _head_cols(h)] for r in (fwd_ref, rev_ref) for h in range(DN_HEADS)])


def _unstack_chains(val, fwd_ref, rev_ref):
    for d, r in enumerate((fwd_ref, rev_ref)):
        for h in range(DN_HEADS):
            r[:, _head_cols(h)] = val[d * DN_HEADS + h]


def _gates_transposed(fwd_refs, rev_refs):
    return jnp.stack([jnp.concatenate([f[...].T, r[...].T], axis=0) for f, r in zip(fwd_refs, rev_refs, strict=True)])


def _dn_row_spec(nc, col, reverse, width=TOK_WIDTH):
    return pl.BlockSpec((DN_CHUNK, width), lambda t: ((nc - 1 - t) if reverse else t, col))


def _dn_state_spec(nc, reverse):
    return pl.BlockSpec((None, DN_HEADS, DN_HEAD_DIM, DN_HEAD_DIM), lambda t: ((nc - 1 - t) if reverse else t, 0, 0, 0))


def _head_cols(h):
    return pl.ds(h * DN_HEAD_DIM, DN_HEAD_DIM)


def _const_spec(arr):
    return pl.BlockSpec(arr.shape, functools.partial(lambda t, n: (0,) * n, n=arr.ndim))


def _dn_chains(inverse):
    return jax.vmap(lambda q, k, v, gates_t, *rest: _dn_chunk(q, k, v, gates_t, *rest, inverse),
                    in_axes=(0, 0, 0, None, 0, 0, 0, 0, 0))


def _dn_scan_fwd(qkv, gates, *, name):
    s_len = qkv.shape[0]
    nc = s_len // DN_CHUNK
    tri, picks = _dn_tri(), _dn_gate_picks()

    def kern(*refs):
        ins, (tri_ref, pick_ref, of_ref, or_ref, sf_ref, sr_ref, state) = refs[:12], refs[12:]

        @pl.when(pl.program_id(0) == 0)
        def _():
            state[...] = jnp.zeros_like(state)

        entry = state[...]
        qkv_c = [_stack_chains(ins[i], ins[6 + i]) for i in range(3)]
        gates_t = _gates_transposed(ins[3:6], ins[9:12])
        out, new = _dn_chains(_unit_tri_inverse_impl)(*qkv_c, gates_t, pick_ref[0], pick_ref[1], pick_ref[2], entry, tri_ref[...])
        sf_ref[...] = entry[:DN_HEADS]
        sr_ref[...] = entry[DN_HEADS:]
        _unstack_chains(out, of_ref, or_ref)
        state[...] = new

    in_specs = []
    for rev in (False, True):
        in_specs += [_dn_row_spec(nc, col, rev) for col in (0, 1, 2)] + [_dn_row_spec(nc, 0, rev, LANES)] * 3
    in_specs += [_const_spec(tri), _const_spec(picks)]
    return pl.pallas_call(
        kern, name=name, grid=(nc,), in_specs=in_specs,
        out_specs=[_dn_row_spec(nc, 0, False), _dn_row_spec(nc, 0, True), _dn_state_spec(nc, False), _dn_state_spec(nc, True)],
        out_shape=[jax.ShapeDtypeStruct((s_len, TOK_WIDTH), F32)] * 2
        + [jax.ShapeDtypeStruct((nc, DN_HEADS, DN_HEAD_DIM, DN_HEAD_DIM), F32)] * 2,
        scratch_shapes=[pltpu.VMEM((2 * DN_HEADS, DN_HEAD_DIM, DN_HEAD_DIM), F32)],
        compiler_params=_params("arbitrary"),
    )(*([qkv, qkv, qkv, *gates] * 2), tri, picks)


def _dn_scan_bwd(qkv, gates, states, d_o, *, name):
    s_len = qkv.shape[0]
    nc = s_len // DN_CHUNK
    tri, picks = _dn_tri(), _dn_gate_picks()

    def kern(*refs):
        ins, tri_ref, pick_ref, outs, d_state = refs[:16], refs[16], refs[17], refs[18:30], refs[30]

        @pl.when(pl.program_id(0) == 0)
        def _():
            d_state[...] = jnp.zeros_like(d_state)

        qkv_c = [_stack_chains(ins[i], ins[8 + i]) for i in range(3)]
        gates_t = _gates_transposed(ins[3:6], ins[11:14])
        entry = jnp.concatenate([ins[6][...], ins[14][...]], axis=0)
        d_out = _stack_chains(ins[7], ins[15])
        tri_v, picks_v = tri_ref[...], pick_ref[...]
        _, vjp = jax.vjp(lambda q, k, v, g, s: _dn_chains(_unit_tri_inverse)(q, k, v, g, picks_v[0], picks_v[1], picks_v[2], s, tri_v),
                         *qkv_c, gates_t, entry)
        dq, dk, dv, d_gates_t, d_entry = vjp((d_out, d_state[...]))
        for i, val in enumerate((dq, dk, dv)):
            _unstack_chains(val, outs[i], outs[6 + i])
        for i in range(3):
            outs[3 + i][...] = d_gates_t[i, :DN_CHUNK].T
            outs[9 + i][...] = d_gates_t[i, DN_CHUNK:].T
        d_state[...] = d_entry

    in_specs, out_specs, out_shape = [], [], []
    for rev in (True, False):
        in_specs += [_dn_row_spec(nc, col, rev) for col in (0, 1, 2)] + [_dn_row_spec(nc, 0, rev, LANES)] * 3
        in_specs += [_dn_state_spec(nc, rev), _dn_row_spec(nc, 0, rev)]
        out_specs += [_dn_row_spec(nc, 0, rev)] * 3 + [_dn_row_spec(nc, 0, rev, LANES)] * 3
        out_shape += [jax.ShapeDtypeStruct((s_len, TOK_WIDTH), F32)] * 3 + [jax.ShapeDtypeStruct((s_len, LANES), F32)] * 3
    in_specs += [_const_spec(tri), _const_spec(picks)]
    res = pl.pallas_call(
        kern, name=name, grid=(nc,), in_specs=in_specs, out_specs=out_specs, out_shape=out_shape,
        scratch_shapes=[pltpu.VMEM((2 * DN_HEADS, DN_HEAD_DIM, DN_HEAD_DIM), F32)],
        compiler_params=_params("arbitrary"),
    )(*[a for d in range(2) for a in (qkv, qkv, qkv, *gates, states[d], d_o)], tri, picks)
    return (res[0:3], res[3:6]), (res[6:9], res[9:12])


def _dn_out_head(o_f, o_b, z, gain):
    o = o_f + o_b
    return o * lax.rsqrt(jnp.mean(o * o, axis=-1, keepdims=True) + EPS) * gain * _silu(z)


def _dn_out(o_fwd, o_rev, proj, gain, qkv_kv_mem, *, name):
    def body(of, ob, z, qm, g, kv):
        heads = []
        for h in range(DN_HEADS):
            sl = slice(h * DN_HEAD_DIM, (h + 1) * DN_HEAD_DIM)
            heads.append(_dn_out_head(of[:, sl], ob[:, sl], z[:, sl], g))
        return (jnp.concatenate(heads + [_mem_attn(qm, kv)], axis=1),), ()
    return _rowwise(body, [o_fwd, o_rev, _col(proj, TOK_WIDTH, 3),
                           _col(proj, MEM_WIDTH, (4 * TOK_WIDTH) // MEM_WIDTH)], [gain, qkv_kv_mem],
                    [(D_MODEL, MXU_DTYPE)], [], tm=256, name=name)[0]


def _dn_out_bwd(o_fwd, o_rev, proj, gain, kv_mem, dcat, *, name):
    def body(of, ob, z, qm, dcat, g, kv):
        dos, dzs = [], []
        dgain = jnp.zeros_like(g)
        for h in range(DN_HEADS):
            sl = slice(h * DN_HEAD_DIM, (h + 1) * DN_HEAD_DIM)
            _, vjp = jax.vjp(_dn_out_head, of[:, sl], ob[:, sl], z[:, sl], g)
            d_of, _, dz, dg = vjp(dcat[:, sl])
            dos.append(d_of)
            dzs.append(dz)
            dgain = dgain + dg
        dqm, dkv = _mem_attn_bwd(qm, kv, dcat[:, TOK_WIDTH:])
        return (jnp.concatenate(dos, axis=1), jnp.concatenate(dzs, axis=1), dqm), (dgain, dkv)
    return _rowwise(body, [o_fwd, o_rev, _col(proj, TOK_WIDTH, 3),
                           _col(proj, MEM_WIDTH, (4 * TOK_WIDTH) // MEM_WIDTH), dcat], [gain, kv_mem],
                    [(TOK_WIDTH, F32), (TOK_WIDTH, F32), (MEM_WIDTH, F32)], [gain.shape, kv_mem.shape], tm=256, name=name)


def _pad_dn_w_in(w):
    gates = w[:, 4 * TOK_WIDTH: 4 * TOK_WIDTH + N_GATES]
    zeros = jnp.zeros((w.shape[0], DN_IN_PAD - DN_IN), w.dtype)
    return jnp.concatenate([w[:, :4 * TOK_WIDTH], w[:, 4 * TOK_WIDTH + N_GATES:], gates, zeros], axis=1)


def _unpad_dn_w_in(w):
    q_mem = w[:, 4 * TOK_WIDTH: 4 * TOK_WIDTH + MEM_WIDTH]
    gates = w[:, 4 * TOK_WIDTH + MEM_WIDTH: 4 * TOK_WIDTH + MEM_WIDTH + N_GATES]
    return jnp.concatenate([w[:, :4 * TOK_WIDTH], gates, q_mem], axis=1)


def _ffn_fwd(h, w_gu, w_d, tag):
    gu = _mm(h, w_gu, out_dtype=MXU_DTYPE, name=f"ffn_gu_{tag}")
    act = _swiglu_act(gu, name=f"ffn_act_{tag}")
    return gu, act, _mm(act, w_d, name=f"ffn_down_{tag}")


def _ffn_bwd(h, gu, act, w_gu, w_d, df, tag):
    d_act = _mm(df, w_d, tb=True, out_dtype=MXU_DTYPE, name=f"ffn_dact_{tag}")
    d_wd = _mm(act, df, ta=True, out_dtype=LINK_DTYPE, name=f"ffn_dwd_{tag}")
    d_gu = _swiglu_act_bwd(gu, d_act, name=f"ffn_dgu_{tag}")
    dh = _mm(d_gu, w_gu, tb=True, name=f"ffn_dh_{tag}")
    d_wgu = _mm(h, d_gu, ta=True, out_shards=_shards_of(w_gu), out_dtype=LINK_DTYPE, name=f"ffn_dwgu_{tag}")
    return dh, d_wgu, d_wd


def _local_step(x, mem, target, p):
    g = {}
    row = lambda v: v.reshape(1, -1)
    gains = {k: [row(p[k][i]) for i in range(2)] for k in
             ("mem_norm", "norm_mix_pre", "norm_mix_post", "norm_ffn_pre", "norm_ffn_post")}
    out_gain = row(p["dn_out_norm"])
    a_cols, dt_cols = _gate_params(p["dn_a_log"]), _gate_params(p["dn_dt_bias"])

    h0 = _pre_norm(x, gains["norm_mix_pre"][0], name="pre0")
    mem_n = [_pre_norm(mem, gains["mem_norm"][i], name=f"mem_norm{i}") for i in range(2)]
    qkvm = _mm(h0, p["att_w_in"], out_dtype=MXU_DTYPE, name="att_in")
    bias = [_bias_tiles(p["rel_bias"], gi) for gi in range(3)]
    att = [_att_fwd(qkvm, bias[gi], gi, name=f"att_fwd{gi}") for gi in range(3)]
    att_w_out, *mem_w_kv = p["att_rest_weights"](att[2][0])
    kv_mem = [_mm(mem_n[i], mem_w_kv[i], name=f"mem_kv{i}") for i in range(2)]
    cat0, lse_tot = _att_combine([a[0] for a in att], [a[1] for a in att], qkvm, kv_mem[0], name="att_combine")
    mo0 = _mm(cat0, att_w_out, name="att_out")
    x1, h1 = _post_pre(x, mo0, gains["norm_mix_post"][0], gains["norm_ffn_pre"][0], name="post_mix0")
    w_gu0, w_d0 = p["ffn_weights"](0, h1)
    gu0, act0, f0 = _ffn_fwd(h1, w_gu0, w_d0, 0)
    x2, h2 = _post_pre(x1, f0, gains["norm_ffn_post"][0], gains["norm_mix_pre"][1], name="post_ffn0")

    dn_w_in, dn_w_out = p["dn_weights"](h2)
    proj = _mm(h2, dn_w_in, name="dn_in")
    qkv = _dn_conv_fwd(proj, p["dn_conv"], name="dn_conv")
    gates = _dn_gates_fwd(proj, a_cols, dt_cols, name="dn_gates")
    o_fwd, o_rev, st_fwd, st_rev = _dn_scan_fwd(qkv, gates, name="dn_scan")
    cat1 = _dn_out(o_fwd, o_rev, proj, out_gain, kv_mem[1], name="dn_outnorm")
    mo1 = _mm(cat1, dn_w_out, name="dn_out")
    x3, h3 = _post_pre(x2, mo1, gains["norm_mix_post"][1], gains["norm_ffn_pre"][1], name="post_mix1")
    w_gu1, w_d1 = p["ffn_weights"](1, h3)
    gu1, act1, f1 = _ffn_fwd(h3, w_gu1, w_d1, 1)

    dx3, df1, dg_ffn_post1, loss_cols = _final_loss_bwd(x3, f1, gains["norm_ffn_post"][1], target, name="loss_bwd")
    dh3, d_wgu1, d_wd1 = _ffn_bwd(h3, gu1, act1, w_gu1, w_d1, df1, 1)
    sent = p["grads_ready"]("ffn1", {("ffn_w_gate_up", 1): d_wgu1, ("ffn_w_down", 1): d_wd1})
    dx2, dmo1, dg_mix_post1, dg_ffn_pre1 = _post_pre_bwd(x2, mo1, gains["norm_mix_post"][1] + sent, gains["norm_ffn_pre"][1],
                                                         dx3, dh3, name="post_mix1_bwd")
    dcat1 = _mm(dmo1, dn_w_out, tb=True, name="dn_out_dx")
    g["dn_w_out"] = _mm(cat1, dmo1, ta=True, out_dtype=LINK_DTYPE, name="dn_out_dw")
    d_o, dz, dqm1, d_out_gain, dkv1 = _dn_out_bwd(o_fwd, o_rev, proj, out_gain, kv_mem[1], dcat1, name="dn_outnorm_bwd")
    (d_f, dg_f), (d_r, dg_r) = _dn_scan_bwd(qkv, gates, (st_fwd, st_rev), d_o, name="dn_scan_bwd")
    d_gate_cols, d_a_cols, d_dt_cols = _dn_gates_bwd(proj, a_cols, dt_cols, (dg_f, dg_r), name="dn_gates_bwd")
    d_pre, d_conv = zip(*[_dn_conv_bwd(proj, p["dn_conv"], d_f[which], d_r[which], which, name=f"dn_conv_bwd{which}")
                          for which in range(3)])
    dproj = jnp.concatenate(list(d_pre) + [dz, dqm1, d_gate_cols], axis=1).astype(MXU_DTYPE)
    dh2 = _mm(dproj, dn_w_in, tb=True, name="dn_in_dx")
    g["dn_w_in"] = _mm(h2, dproj, ta=True, out_dtype=LINK_DTYPE, name="dn_in_dw")
    g["dn_conv"] = jnp.concatenate(d_conv, axis=1)
    g["dn_a_log"] = _gate_params_bwd(d_a_cols)
    g["dn_dt_bias"] = _gate_params_bwd(d_dt_cols)
    g["dn_out_norm"] = d_out_gain

    d_mem_kv1 = _mm(mem_n[1], dkv1, ta=True, out_dtype=LINK_DTYPE, name="mem_kv_dw1")
    sent = p["grads_ready"]("dn", {("dn_w_in", 0): g["dn_w_in"], ("dn_w_out", 0): g["dn_w_out"], ("mem_w_kv", 1): d_mem_kv1})
    dx1, df0, dg_ffn_post0, dg_mix_pre1 = _post_pre_bwd(x1, f0, gains["norm_ffn_post"][0] + sent, gains["norm_mix_pre"][1],
                                                        dx2, dh2, name="post_ffn0_bwd")
    dh1, d_wgu0, d_wd0 = _ffn_bwd(h1, gu0, act0, w_gu0, w_d0, df0, 0)
    sent = p["grads_ready"]("ffn0", {("ffn_w_gate_up", 0): d_wgu0, ("ffn_w_down", 0): d_wd0})
    dx0, dmo0, dg_mix_post0, dg_ffn_pre0 = _post_pre_bwd(x, mo0, gains["norm_mix_post"][0] + sent, gains["norm_ffn_pre"][0],
                                                         dx1, dh1, name="post_mix0_bwd")
    dcat0 = _mm(dmo0, att_w_out, tb=True, name="att_out_dx")
    g["att_w_out"] = _mm(cat0, dmo0, ta=True, out_dtype=LINK_DTYPE, name="att_out_dw")
    delta, dqm0, dkv0 = _att_bwd_prep(cat0, dcat0, qkvm, kv_mem[0], name="att_bwd_prep")
    d_mem_kv0 = _mm(mem_n[0], dkv0, ta=True, out_dtype=LINK_DTYPE, name="mem_kv_dw0")
    sent = p["grads_ready"]("att_out", {("att_w_out", 0): g["att_w_out"], ("mem_w_kv", 0): d_mem_kv0})
    att_b = [_att_bwd(qkvm, bias[gi] + sent, lse_tot, delta, dcat0, gi, name=f"att_bwd{gi}") for gi in range(3)]
    dqkvm = jnp.concatenate([a[w] for w in range(3) for a in att_b] + [dqm0], axis=1).astype(MXU_DTYPE)
    g["rel_bias"] = sum(_bias_tiles_bwd(p["rel_bias"], att_b[gi][3], gi) for gi in range(3))
    g["att_w_in"] = _mm(h0, dqkvm, ta=True, out_shards=_shards_of(p["att_w_in"]), out_dtype=LINK_DTYPE, name="att_in_dw")
    sent = p["grads_ready"]("att_in", {("att_w_in", 0): g["att_w_in"]})
    dh0 = _mm(dqkvm, p["att_w_in"], tb=True, name="att_in_dx")
    grad_x, dg_mix_pre0 = _pre_norm_bwd(x, gains["norm_mix_pre"][0] + sent, dh0, dx0, name="pre0_bwd")

    d_mem_norm = []
    for i, dkv in enumerate((dkv0, dkv1)):
        d_mem_n = _mm(dkv, mem_w_kv[i], tb=True, name=f"mem_kv_dx{i}")
        d_mem_norm.append(_gain_bwd(mem, gains["mem_norm"][i], d_mem_n, name=f"mem_norm_bwd{i}"))
    g["mem_w_kv"] = [d_mem_kv0, d_mem_kv1]
    g["mem_norm"] = jnp.concatenate(d_mem_norm, axis=0)
    g["norm_mix_pre"] = jnp.concatenate([dg_mix_pre0, dg_mix_pre1], axis=0)
    g["norm_mix_post"] = jnp.concatenate([dg_mix_post0, dg_mix_post1], axis=0)
    g["norm_ffn_pre"] = jnp.concatenate([dg_ffn_pre0, dg_ffn_pre1], axis=0)
    g["norm_ffn_post"] = jnp.concatenate([dg_ffn_post0, dg_ffn_post1], axis=0)
    g["ffn_w_gate_up"] = [d_wgu0, d_wgu1]
    g["ffn_w_down"] = [d_wd0, d_wd1]
    return loss_cols, grad_x, g


N_CHIPS = 4
N_DEV = 8
MESH = pl.DeviceIdType.MESH
BIG = (("att_w_in", (1, 1024, 640), 2), ("att_w_out", (1, 256, 1024), 1), ("dn_w_in", (1, 1024, 838), 2),
       ("dn_w_out", (1, 256, 1024), 1), ("mem_w_kv", (2, 256, 512), 1), ("ffn_w_gate_up", (2, 1024, 1408), 2),
       ("ffn_w_down", (2, 704, 1024), 1))


def _mesh_pos():
    return lax.axis_index("x"), lax.axis_index("y"), lax.axis_index("c")


def _other_chips(x, y):
    return [(1 - x, y), (x, 1 - y), (1 - x, 1 - y)]


ANY = pl.BlockSpec(memory_space=pl.ANY)


def _all_reduce_small(v, *, name):
    rows, cols = v.shape
    flips = [(dx, dy, dc) for dx in (0, 1) for dy in (0, 1) for dc in (0, 1)][1:]

    def body(v_ref, o_ref, buf, send_sems, recv_sems):
        x, y, c = _mesh_pos()

        def peer(f):
            return tuple(1 - p if fl else p for p, fl in zip((x, y, c), f))

        def index(p):
            return 4 * p[0] + 2 * p[1] + p[2]

        buf[index((x, y, c))] = v_ref[...]
        sends = []
        for k, f in enumerate(flips):
            cp = pltpu.make_async_remote_copy(src_ref=v_ref, dst_ref=buf.at[index((x, y, c))], send_sem=send_sems.at[k],
                                              recv_sem=recv_sems.at[k], device_id=peer(f), device_id_type=MESH)
            cp.start()
            sends.append(cp)
        for k, f in enumerate(flips):
            pltpu.make_async_remote_copy(src_ref=v_ref, dst_ref=buf.at[index(peer(f))], send_sem=send_sems.at[k],
                                         recv_sem=recv_sems.at[k], device_id=peer(f), device_id_type=MESH).wait_recv()
        for cp in sends:
            cp.wait_send()
        acc = buf[0]
        for d in range(1, N_DEV):
            acc = acc + buf[d]
        o_ref[...] = acc

    vmem = pl.BlockSpec(memory_space=pltpu.VMEM)
    return pl.pallas_call(
        body, name=name, in_specs=[vmem], out_specs=vmem, out_shape=jax.ShapeDtypeStruct((rows, cols), F32),
        scratch_shapes=[pltpu.VMEM((N_DEV, rows, cols), F32), pltpu.SemaphoreType.DMA((N_DEV - 1,)),
                        pltpu.SemaphoreType.DMA((N_DEV - 1,))],
    )(v)


def _adamw(w, g, m, v, *, name):
    def body(w, g, m, v):
        m = ADAM_B1 * m + (1.0 - ADAM_B1) * g
        v = ADAM_B2 * v + (1.0 - ADAM_B2) * (g * g)
        m_hat = m / (1.0 - ADAM_B1 ** ADAM_STEP)
        v_hat = v / (1.0 - ADAM_B2 ** ADAM_STEP)
        delta = -ADAM_LR * (m_hat / (jnp.sqrt(v_hat) + ADAM_EPS) + ADAM_WD * w)
        return (delta, m, v), ()
    rows, cols = w.shape
    if rows % SUBLANES == 0:
        return _rowwise(body, [w, g, m, v], [], [(cols, F32)] * 3, [], tm=_tile(rows, 256, SUBLANES), name=name)

    def kern(*refs):
        outs, _ = body(*[r[...] for r in refs[:4]])
        for r, val in zip(refs[4:], outs, strict=True):
            r[...] = val

    spec = pl.BlockSpec((rows, _tile(cols, 256, LANES)), lambda j: (0, j))
    return pl.pallas_call(
        kern, name=name, grid=(cols // spec.block_shape[1],), in_specs=[spec] * 4, out_specs=[spec] * 3,
        out_shape=[jax.ShapeDtypeStruct((rows, cols), F32)] * 3, compiler_params=_params("parallel"),
    )(w, g, m, v)


def _pack_small(arrs, rows):
    flat = jnp.concatenate([a.reshape(-1) for a in arrs])
    return jnp.pad(flat, (0, rows * LANES - flat.shape[0])).reshape(rows, LANES)


def _unpack_small(packed, shapes):
    flat = packed.reshape(-1)
    out, off = [], 0
    for s in shapes:
        size = math.prod(s)
        out.append(flat[off: off + size].reshape(s))
        off += size
    return out


def _small_rows(shapes):
    return -(-sum(math.prod(s) for s in shapes) // (SUBLANES * LANES)) * SUBLANES


def _sem_pairs(n):
    return [pltpu.SemaphoreType.DMA((n,)), pltpu.SemaphoreType.DMA((n,))]


def _gather_blocks(blocks, *, name):
    n = len(blocks)

    def body(*refs):
        x_refs, out_refs, (send_sems, recv_sems) = refs[:n], refs[n: 2 * n], refs[2 * n:]
        x, y, c = _mesh_pos()
        sibling = (x, y, 1 - c)
        chips = _other_chips(x, y)

        def copy(k, src, dst, to):
            return pltpu.make_async_remote_copy(src_ref=src, dst_ref=dst, send_sem=send_sems.at[k],
                                                recv_sem=recv_sems.at[k], device_id=to, device_id_type=MESH)

        def part(b, chip, h):
            half = blocks[b].shape[0] // 2
            return out_refs[b].at[2 * chip[0] + chip[1], pl.ds(h * half, half), :]

        def my_half(b):
            half = blocks[b].shape[0] // 2
            return x_refs[b].at[pl.ds(c * half, half), :]

        first = [copy(6 * b + j, my_half(b), part(b, (x, y), c), (*chip, c)) for b in range(n) for j, chip in enumerate(chips)]
        for cp in first:
            cp.start()
        passed = []
        for b in range(n):
            for j, chip in enumerate(chips):
                copy(6 * b + j, my_half(b), part(b, chip, c), (*chip, c)).wait_recv()
                cp = copy(6 * b + 3 + j, part(b, chip, c), part(b, chip, c), sibling)
                cp.start()
                passed.append(cp)
        for b in range(n):
            for j, chip in enumerate(chips):
                copy(6 * b + 3 + j, part(b, chip, 1 - c), part(b, chip, 1 - c), sibling).wait_recv()
        for cp in first + passed:
            cp.wait_send()

    return pl.pallas_call(
        body, name=name, in_specs=[ANY] * n, out_specs=[ANY] * n,
        out_shape=[jax.ShapeDtypeStruct((N_CHIPS, *a.shape), a.dtype) for a in blocks],
        scratch_shapes=_sem_pairs(6 * n),
    )(*blocks)


HBM = pl.BlockSpec(memory_space=pltpu.HBM)
SEM = pl.BlockSpec(memory_space=pltpu.SEMAPHORE)
DATAFLOW = pltpu.SideEffectType.DATAFLOW_SIDE_EFFECTING


def _gather_start(blocks, *, name):
    n = len(blocks)
    lands = [lax.empty((N_CHIPS, *a.shape), a.dtype) for a in blocks]

    def body(*refs):
        x_refs, land_refs, send_sems, recv_sems, token = refs[:n], refs[n: 2 * n], refs[2 * n], refs[2 * n + 1], refs[-1]
        x, y, c = _mesh_pos()
        for b in range(n):
            for j, chip in enumerate(_other_chips(x, y)):
                pltpu.make_async_remote_copy(src_ref=x_refs[b], dst_ref=land_refs[b].at[2 * x + y], send_sem=send_sems.at[3 * b + j],
                                             recv_sem=recv_sems.at[3 * b + j], device_id=(*chip, c), device_id_type=MESH).start()
        token[...] = jnp.zeros_like(token)

    operands = [pltpu.with_memory_space_constraint(a, pltpu.HBM) for a in blocks + lands]
    res = pl.pallas_call(
        body, name=name, in_specs=[HBM] * (2 * n),
        out_shape=(pltpu.SemaphoreType.DMA((3 * n,)), pltpu.SemaphoreType.DMA((3 * n,)),
                   *[pltpu.HBM(a.shape, a.dtype) for a in operands], jax.ShapeDtypeStruct((SUBLANES, LANES), F32)),
        out_specs=(SEM, SEM, *[HBM] * (2 * n), pl.BlockSpec(memory_space=pltpu.VMEM)),
        input_output_aliases={i: 2 + i for i in range(2 * n)},
        compiler_params=pltpu.CompilerParams(has_side_effects=DATAFLOW),
    )(*operands)
    return res[0], res[1], list(res[2: 2 + n]), list(res[2 + n: 2 + 2 * n]), res[-1]


def _gather_wait(started, after, *, name):
    send_sems, recv_sems, blocks, lands, _ = started
    n = len(blocks)

    def body(*refs):
        x_refs, land_refs, send_sems, recv_sems = refs[:n], refs[n: 2 * n], refs[2 * n], refs[2 * n + 1]
        x, y, c = _mesh_pos()
        for b in range(n):
            for j, chip in enumerate(_other_chips(x, y)):
                cp = pltpu.make_async_remote_copy(src_ref=x_refs[b], dst_ref=land_refs[b].at[2 * chip[0] + chip[1]],
                                                  send_sem=send_sems.at[3 * b + j], recv_sem=recv_sems.at[3 * b + j],
                                                  device_id=(*chip, c), device_id_type=MESH)
                cp.wait_send()
                cp.wait_recv()

    res = pl.pallas_call(
        body, name=name, in_specs=(*[HBM] * (2 * n), SEM, SEM, ANY),
        out_shape=tuple(pltpu.HBM(a.shape, a.dtype) for a in blocks + lands), out_specs=tuple([HBM] * (2 * n)),
        input_output_aliases={i: i for i in range(2 * n)},
        compiler_params=pltpu.CompilerParams(has_side_effects=DATAFLOW),
    )(*blocks, *lands, send_sems, recv_sems, after)
    return list(res[n:])


def _swap_halves(blocks, own_rows, *, name):
    n = len(blocks)

    def body(*refs):
        in_refs, out_refs, (send_sems, recv_sems) = refs[:n], refs[n: 2 * n], refs[2 * n:]
        x, y, c = _mesh_pos()
        copies = [pltpu.make_async_remote_copy(src_ref=own_rows(in_refs[b], c), dst_ref=out_refs[b], send_sem=send_sems.at[b],
                                               recv_sem=recv_sems.at[b], device_id=(x, y, 1 - c), device_id_type=MESH)
                  for b in range(n)]
        for cp in copies:
            cp.start()
        for cp in copies:
            cp.wait()

    def sent_shape(a):
        return jax.eval_shape(lambda r: own_rows(r, 0), a)

    return pl.pallas_call(
        body, name=name, in_specs=[ANY] * n, out_specs=[ANY] * n,
        out_shape=[jax.ShapeDtypeStruct(sent_shape(a).shape, a.dtype) for a in blocks],
        scratch_shapes=_sem_pairs(n),
    )(*blocks)


def _whole(ref, c):
    return ref


def _sum_chips_block(parts, *, name):
    n, half, cols = parts.shape
    tm = _tile(half, 512, 2 * SUBLANES)

    def kern(p_ref, o_ref):
        acc = p_ref[0].astype(F32)
        for s in range(1, n):
            acc = acc + p_ref[s].astype(F32)
        o_ref[...] = acc

    return pl.pallas_call(
        kern, name=name, grid=(half // tm,),
        in_specs=[pl.BlockSpec((n, tm, cols), lambda i: (0, i, 0))],
        out_specs=pl.BlockSpec((tm, cols), lambda i: (i, 0)),
        out_shape=jax.ShapeDtypeStruct((half, cols), F32),
        compiler_params=_params("parallel"),
    )(parts)


PEER_FLIPS = [(dx, dy, dc) for dx in (0, 1) for dy in (0, 1) for dc in (0, 1)][1:]


def _flipped(pos, flip):
    return tuple(1 - p if f else p for p, f in zip(pos, flip))


def _device_index(pos):
    return 4 * pos[0] + 2 * pos[1] + pos[2]


def _scatter_start(blocks, *, name):
    n = len(blocks)
    lands = [lax.empty((N_DEV, a.shape[1] // 2, a.shape[2]), a.dtype) for a in blocks]

    def body(*refs):
        g_refs, land_refs, send_sems, recv_sems, token = refs[:n], refs[n: 2 * n], refs[2 * n], refs[2 * n + 1], refs[-1]
        pos = _mesh_pos()
        for b in range(n):
            half = blocks[b].shape[1] // 2
            for k, flip in enumerate(PEER_FLIPS):
                peer = _flipped(pos, flip)
                pltpu.make_async_remote_copy(src_ref=g_refs[b].at[2 * peer[0] + peer[1], pl.ds(peer[2] * half, half), :],
                                             dst_ref=land_refs[b].at[_device_index(pos)],
                                             send_sem=send_sems.at[7 * b + k], recv_sem=recv_sems.at[7 * b + k],
                                             device_id=peer, device_id_type=MESH).start()
        token[...] = jnp.zeros_like(token)

    operands = [pltpu.with_memory_space_constraint(a, pltpu.HBM) for a in blocks + lands]
    res = pl.pallas_call(
        body, name=name, in_specs=[HBM] * (2 * n),
        out_shape=(pltpu.SemaphoreType.DMA((7 * n,)), pltpu.SemaphoreType.DMA((7 * n,)),
                   *[pltpu.HBM(a.shape, a.dtype) for a in operands], jax.ShapeDtypeStruct((SUBLANES, LANES), F32)),
        out_specs=(SEM, SEM, *[HBM] * (2 * n), pl.BlockSpec(memory_space=pltpu.VMEM)),
        input_output_aliases={i: 2 + i for i in range(2 * n)},
        compiler_params=pltpu.CompilerParams(has_side_effects=DATAFLOW),
    )(*operands)
    return res[0], res[1], list(res[2: 2 + n]), list(res[2 + n: 2 + 2 * n]), res[-1]


def _scatter_wait(started, after, *, name):
    send_sems, recv_sems, blocks, lands, _ = started
    n = len(blocks)

    def body(*refs):
        g_refs, land_refs, send_sems, recv_sems = refs[:n], refs[n: 2 * n], refs[2 * n], refs[2 * n + 1]
        pos = _mesh_pos()
        for b in range(n):
            half = blocks[b].shape[1] // 2
            for k, flip in enumerate(PEER_FLIPS):
                peer = _flipped(pos, flip)
                cp = pltpu.make_async_remote_copy(src_ref=g_refs[b].at[0, pl.ds(0, half), :],
                                                  dst_ref=land_refs[b].at[_device_index(peer)],
                                                  send_sem=send_sems.at[7 * b + k], recv_sem=recv_sems.at[7 * b + k],
                                                  device_id=peer, device_id_type=MESH)
                cp.wait_send()
                cp.wait_recv()

    res = pl.pallas_call(
        body, name=name, in_specs=(*[HBM] * (2 * n), SEM, SEM, ANY),
        out_shape=tuple(pltpu.HBM(a.shape, a.dtype) for a in blocks + lands), out_specs=tuple([HBM] * (2 * n)),
        input_output_aliases={i: i for i in range(2 * n)},
        compiler_params=pltpu.CompilerParams(has_side_effects=DATAFLOW),
    )(*blocks, *lands, send_sems, recv_sems, after)
    return list(res[:n]), list(res[n:])


def _reduce_finish(begun, names, after):
    x, y, c = _mesh_pos()
    mine = {}
    for key, started in begun.items():
        blocks, lands = _scatter_wait(started, after, name=f"rs_scatter_wait_{key}")
        parts = []
        for blk, land in zip(blocks, lands, strict=True):
            half = blk.shape[1] // 2
            own = lax.dynamic_slice(blk, (2 * x + y, c * half, 0), (1, half, blk.shape[2]))
            parts.append(lax.dynamic_update_slice(land, own, (_device_index((x, y, c)), 0, 0)))
        mine[key] = [_sum_chips_block(p, name=f"rs_sum_{nm}") for p, nm in zip(parts, names[key], strict=True)]
    flat = [a for key in begun for a in mine[key]]
    other = iter(_swap_halves(flat, _whole, name="rs_join"))
    return {key: [jnp.concatenate([jnp.where(c == 0, a, b), jnp.where(c == 0, b, a)], axis=0)
                  for a, b in ((a, next(other)) for a in mine[key])] for key in begun}


WEIGHTS = ("rel_bias", "att_w_in", "att_w_out", "dn_w_in", "dn_conv", "dn_a_log", "dn_dt_bias", "dn_out_norm", "dn_w_out",
           "mem_norm", "mem_w_kv", "norm_mix_pre", "norm_mix_post", "norm_ffn_pre", "norm_ffn_post", "ffn_w_gate_up",
           "ffn_w_down")
BIG_NAMES = tuple(n for n, _, _ in BIG)
SMALL_NAMES = tuple(n for n in WEIGHTS if n not in BIG_NAMES)
CONV_COLS = 3 * TOK_WIDTH
CONV_SHARD = CONV_COLS // N_CHIPS
BLOCKS = tuple((n, layer) for n, shape, _ in BIG for layer in range(shape[0]))
COLUMN_SHARDED = {n: axis == 2 for n, _, axis in BIG}


def kernel(x, mem, rel_bias, att_w_in, att_w_out, dn_w_in, dn_conv, dn_a_log, dn_dt_bias, dn_out_norm, dn_w_out, mem_norm, mem_w_kv, norm_mix_pre, norm_mix_post, norm_ffn_pre, norm_ffn_post, ffn_w_gate_up, ffn_w_down, loss_target, m_rel_bias, m_att_w_in, m_att_w_out, m_dn_w_in, m_dn_conv, m_dn_a_log, m_dn_dt_bias, m_dn_out_norm, m_dn_w_out, m_mem_norm, m_mem_w_kv, m_norm_mix_pre, m_norm_mix_post, m_norm_ffn_pre, m_norm_ffn_post, m_ffn_w_gate_up, m_ffn_w_down, v_rel_bias, v_att_w_in, v_att_w_out, v_dn_w_in, v_dn_conv, v_dn_a_log, v_dn_dt_bias, v_dn_out_norm, v_dn_w_out, v_mem_norm, v_mem_w_kv, v_norm_mix_pre, v_norm_mix_post, v_norm_ffn_pre, v_norm_ffn_post, v_ffn_w_gate_up, v_ffn_w_down):
    w = dict(zip(WEIGHTS, (rel_bias, att_w_in, att_w_out, dn_w_in, dn_conv, dn_a_log, dn_dt_bias, dn_out_norm, dn_w_out,
                           mem_norm, mem_w_kv, norm_mix_pre, norm_mix_post, norm_ffn_pre, norm_ffn_post, ffn_w_gate_up,
                           ffn_w_down)))
    m = dict(zip(WEIGHTS, (m_rel_bias, m_att_w_in, m_att_w_out, m_dn_w_in, m_dn_conv, m_dn_a_log, m_dn_dt_bias,
                           m_dn_out_norm, m_dn_w_out, m_mem_norm, m_mem_w_kv, m_norm_mix_pre, m_norm_mix_post,
                           m_norm_ffn_pre, m_norm_ffn_post, m_ffn_w_gate_up, m_ffn_w_down)))
    v = dict(zip(WEIGHTS, (v_rel_bias, v_att_w_in, v_att_w_out, v_dn_w_in, v_dn_conv, v_dn_a_log, v_dn_dt_bias,
                           v_dn_out_norm, v_dn_w_out, v_mem_norm, v_mem_w_kv, v_norm_mix_pre, v_norm_mix_post,
                           v_norm_ffn_pre, v_norm_ffn_post, v_ffn_w_gate_up, v_ffn_w_down)))
    cx, cy, cc = _mesh_pos()
    chip = 2 * cx + cy

    local = dict(zip(BLOCKS, lax.optimization_barrier(
        [(w[n][layer].T if n == "dn_w_in" else w[n][layer]).astype(MXU_DTYPE) for n, layer in BLOCKS]), strict=True))

    def usable(block, got):
        got = lax.dynamic_update_slice(got, local[block][None], (chip, 0, 0))
        return got if COLUMN_SHARDED[block[0]] else got.reshape(-1, got.shape[-1])

    late = {"att_rest": [("att_w_out", 0), ("mem_w_kv", 0), ("mem_w_kv", 1)],
            "ffn0": [("ffn_w_gate_up", 0), ("ffn_w_down", 0)], "dn": [("dn_w_in", 0), ("dn_w_out", 0)],
            "ffn1": [("ffn_w_gate_up", 1), ("ffn_w_down", 1)]}
    first = [b for b in BLOCKS if all(b not in blks for blks in late.values())]
    first_got = _gather_blocks([local[b] for b in first], name="gather_weights")
    late_local, _ = lax.optimization_barrier(({k: [local[b] for b in blks] for k, blks in late.items()}, first_got[0]))
    started = {k: _gather_start(late_local[k], name=f"gather_start_{k}") for k in late}
    started_token = sum(s[4][0, 0] for s in started.values())

    def late_weights(key, after):
        lands = _gather_wait(started[key], after, name=f"gather_wait_{key}")
        return [usable(b, got) for b, got in zip(late[key], lands, strict=True)]

    def dn_weights(after):
        w_in, w_out = late_weights("dn", after)
        return _pad_dn_w_in(jnp.concatenate([w_in[s].T for s in range(N_CHIPS)], axis=1)), w_out

    full = {}
    for b, got in zip(first, first_got, strict=True):
        full.setdefault(b[0], []).append(usable(b, got))
    conv_rows = _small_rows([(DN_CONV, CONV_COLS)])
    conv_mine = jnp.where(cc == 0, 1.0, 0.0) * w["dn_conv"][0]
    conv_placed = lax.dynamic_update_slice(jnp.zeros((DN_CONV, CONV_COLS), F32), conv_mine, (0, chip * CONV_SHARD))
    conv_full = _unpack_small(_all_reduce_small(_pack_small([conv_placed], conv_rows), name="gather_conv"),
                              [(DN_CONV, CONV_COLS)])[0]
    p = {
        "rel_bias": w["rel_bias"], "att_w_in": full["att_w_in"][0], "att_rest_weights": lambda after: late_weights("att_rest", after),
        "dn_conv": conv_full, "dn_a_log": w["dn_a_log"][0], "dn_dt_bias": w["dn_dt_bias"][0],
        "dn_out_norm": w["dn_out_norm"][0], "mem_norm": w["mem_norm"],
        "norm_mix_pre": w["norm_mix_pre"] + started_token,
        "norm_mix_post": w["norm_mix_post"], "norm_ffn_pre": w["norm_ffn_pre"], "norm_ffn_post": w["norm_ffn_post"],
        "ffn_weights": lambda layer, after: late_weights(f"ffn{layer}", after), "dn_weights": dn_weights,
    }

    def chip_blocks(n, a):
        if n == "dn_w_in":
            a = _unpad_dn_w_in(a)
        if a.ndim == 3:
            return a
        if COLUMN_SHARDED[n]:
            return a.reshape(a.shape[0], N_CHIPS, -1).transpose(1, 0, 2)
        return a.reshape(N_CHIPS, -1, a.shape[-1])

    begun, begun_blocks = {}, {}

    def grads_ready(key, layer_grads):
        begun_blocks[key] = list(layer_grads)
        begun[key] = _scatter_start([chip_blocks(n, a) for (n, _), a in layer_grads.items()], name=f"rs_scatter_start_{key}")
        return begun[key][4][0, 0]

    p["grads_ready"] = grads_ready
    loss_cols, grad_x, g = _local_step(x[0], mem[0], loss_target[0], p)
    loss = lax.psum(jnp.sum(loss_cols), ("x", "y", "c"))
    finished = _reduce_finish(begun, {k: [f"{n}{layer}" for n, layer in blks] for k, blks in begun_blocks.items()}, grad_x)
    reduced = {b: r for k in begun for b, r in zip(begun_blocks[k], finished[k], strict=True)}
    grads = {n: jnp.concatenate([reduced[b] for b in BLOCKS if b[0] == n], axis=0).reshape(shape) for n, shape, _ in BIG}
    small_full_shapes = [(DN_CONV, CONV_COLS) if n == "dn_conv" else w[n].shape for n in SMALL_NAMES]
    small_sum = _all_reduce_small(_pack_small([g[n] for n in SMALL_NAMES], _small_rows(small_full_shapes)), name="reduce_small")
    for n, s in zip(SMALL_NAMES, _unpack_small(small_sum, small_full_shapes)):
        grads[n] = lax.dynamic_slice(s, (0, chip * CONV_SHARD), (DN_CONV, CONV_SHARD))[None] if n == "dn_conv" else s

    delta, new_m, new_v = {}, {}, {}
    for n in BIG_NAMES:
        shape = w[n].shape
        two_d = (lambda a: a[0].T) if n == "dn_w_in" else (lambda a: a.reshape(-1, shape[-1]))
        back = (lambda a: a.T[None]) if n == "dn_w_in" else (lambda a: a.reshape(shape))
        g_2d = two_d(grads[n])
        res = _adamw(two_d(w[n]), g_2d, two_d(m[n]), two_d(v[n]), name=f"adamw_{n}")
        grads[n], delta[n], new_m[n], new_v[n] = (back(r) for r in (g_2d, *res))
    small_shapes = [w[n].shape for n in SMALL_NAMES]
    rows = _small_rows(small_shapes)
    res = _adamw(*[_pack_small([d[n] for n in SMALL_NAMES], rows) for d in (w, grads, m, v)], name="adamw_small")
    for d, r in zip((delta, new_m, new_v), res):
        for n, a in zip(SMALL_NAMES, _unpack_small(r, small_shapes)):
            d[n] = a
    return (loss, grad_x[None], *[grads[n] for n in WEIGHTS], *[delta[n] for n in WEIGHTS],
            *[new_m[n] for n in WEIGHTS], *[new_v[n] for n in WEIGHTS])
```

```python
import functools
import math

import numpy as np
import jax
import jax.numpy as jnp
from jax import lax
from jax.experimental import pallas as pl
from jax.experimental.pallas import tpu as pltpu

F32 = jnp.float32
MXU_DTYPE = jnp.bfloat16
LINK_DTYPE = jnp.bfloat16
HI = lax.Precision.HIGHEST

EPS = 1e-6
NEG_INF = -1e30
LANES = 128
SUBLANES = 8
VMEM_LIMIT = 56 * 1024 * 1024
MM_WHOLE_K_BUDGET = 44 * 1024 * 1024

D_MODEL = 1024
TOK_WIDTH = 768
MEM_WIDTH = 256
MEM_LEN = 256
ATT_HEAD_DIM = 64
DILATIONS = (1, 4, 16)
HALF = 64
ATT_BQ = 128
ATT_W = ATT_BQ + 2 * HALF
REL_BUCKETS = 32
REL_MAX_DIST = 1024
DN_HEADS = 6
DN_HEAD_DIM = 128
DN_CONV = 5
DN_CHUNK = 128
D_FF = 2816
DN_IN = 3352
DN_IN_PAD = 3456
N_GATES = 4 * DN_HEADS

ADAM_LR = 0.001
ADAM_B1 = 0.9
ADAM_B2 = 0.999
ADAM_EPS = 1e-08
ADAM_WD = 0.01
ADAM_STEP = 10


def _tile(n, target, align):
    if n <= target:
        return n
    t = (target // align) * align
    while t >= align:
        if n % t == 0:
            return t
        t -= align
    raise ValueError(f"no tile for {n} (target {target}, align {align})")


def _params(*sem):
    return pltpu.CompilerParams(dimension_semantics=sem, vmem_limit_bytes=VMEM_LIMIT)


def _mm(a, b, *, name, ta=False, tb=False, out_shards=None, tm=1408, tn=1408, tk=1408, out_dtype=F32):
    if ta:
        K, M = a.shape
    else:
        M, K = a.shape
    sharded_b = b.ndim == 3
    if sharded_b:
        n_sh, b_rows, b_cols = b.shape
        N, K2 = (b_rows, n_sh * b_cols) if tb else (n_sh * b_cols, b_rows)
    else:
        N, K2 = b.shape if tb else b.shape[::-1]
    assert K == K2, (a.shape, b.shape, ta, tb)
    tm = _tile(M, tm, LANES if ta else SUBLANES)
    tn = N // out_shards if out_shards else (b_cols if sharded_b and not tb else _tile(N, tn, LANES))
    tk = b_cols if sharded_b and tb else _tile(K, tk, LANES)

    def vmem_bytes(tm_, tk_):
        return 2 * (tm_ * tk_ * a.dtype.itemsize + tk_ * tn * b.dtype.itemsize + tm_ * tn * jnp.dtype(out_dtype).itemsize)

    for rows in (tm, tm // 2):
        if not (sharded_b and tb) and M % rows == 0 and rows % LANES == 0 and vmem_bytes(rows, K) <= MM_WHOLE_K_BUDGET:
            tm, tk = rows, K
            break
    nk = K // tk
    a_spec = pl.BlockSpec((tk, tm), lambda i, j, k: (k, i)) if ta else pl.BlockSpec((tm, tk), lambda i, j, k: (i, k))
    if sharded_b:
        b_spec = (pl.BlockSpec((None, tn, tk), lambda i, j, k: (k, j, 0)) if tb
                  else pl.BlockSpec((None, tk, tn), lambda i, j, k: (j, k, 0)))
    else:
        b_spec = pl.BlockSpec((tn, tk), lambda i, j, k: (j, k)) if tb else pl.BlockSpec((tk, tn), lambda i, j, k: (k, j))
    if out_shards:
        out_spec = pl.BlockSpec((None, tm, tn), lambda i, j, k: (j, i, 0))
        out_shape = jax.ShapeDtypeStruct((out_shards, M, tn), out_dtype)
    else:
        out_spec = pl.BlockSpec((tm, tn), lambda i, j, k: (i, j))
        out_shape = jax.ShapeDtypeStruct((M, N), out_dtype)
    dims = (((0 if ta else 1,), (1 if tb else 0,)), ((), ()))

    def product(a_ref, b_ref):
        return lax.dot_general(a_ref[...].astype(MXU_DTYPE), b_ref[...].astype(MXU_DTYPE), dims, preferred_element_type=F32)

    def kern_whole(a_ref, b_ref, o_ref):
        o_ref[...] = product(a_ref, b_ref).astype(o_ref.dtype)

    def kern_steps(a_ref, b_ref, o_ref, acc_ref):
        k = pl.program_id(2)

        @pl.when(k == 0)
        def _():
            acc_ref[...] = jnp.zeros_like(acc_ref)

        acc_ref[...] += product(a_ref, b_ref)

        @pl.when(k == nk - 1)
        def _():
            o_ref[...] = acc_ref[...].astype(o_ref.dtype)

    return pl.pallas_call(
        kern_whole if nk == 1 else kern_steps, name=name, grid=(M // tm, N // tn, nk), in_specs=[a_spec, b_spec],
        out_specs=out_spec, out_shape=out_shape,
        scratch_shapes=[] if nk == 1 else [pltpu.VMEM((tm, tn), F32)],
        compiler_params=_params("parallel", "parallel", "arbitrary"),
    )(a, b)


def _shards_of(w):
    return w.shape[0] if w.ndim == 3 else None


def _col(arr, width, blk):
    return (arr, width, blk)


def _rowwise(body, rows, consts, out_rows, out_acc, *, tm, name):
    n_rows = (rows[0][0] if isinstance(rows[0], tuple) else rows[0]).shape[0]
    assert n_rows % tm == 0, (n_rows, tm)
    arrs, in_specs = [], []
    for r in rows:
        arr, width, blk = r if isinstance(r, tuple) else (r, r.shape[1], 0)
        assert arr.shape[0] == n_rows
        arrs.append(arr)
        in_specs.append(pl.BlockSpec((tm, width), functools.partial(lambda i, b: (i, b), b=blk)))
    for c in consts:
        arrs.append(c)
        in_specs.append(pl.BlockSpec(c.shape, functools.partial(lambda i, n: (0,) * n, n=c.ndim)))
    n_in, n_ro = len(arrs), len(out_rows)
    out_shape = [jax.ShapeDtypeStruct((n_rows, w), dt) for w, dt in out_rows]
    out_specs = [pl.BlockSpec((tm, w), lambda i: (i, 0)) for w, _ in out_rows]
    out_shape += [jax.ShapeDtypeStruct(s, F32) for s in out_acc]
    out_specs += [pl.BlockSpec(s, lambda i: (0, 0)) for s in out_acc]

    def kern(*refs):
        ro, ao = body(*[r[...] for r in refs[:n_in]])
        outs = refs[n_in:]
        for r, v in zip(outs[:n_ro], ro, strict=True):
            r[...] = v.astype(r.dtype)
        if out_acc:
            @pl.when(pl.program_id(0) == 0)
            def _():
                for r in outs[n_ro:]:
                    r[...] = jnp.zeros_like(r)

            for r, v in zip(outs[n_ro:], ao, strict=True):
                r[...] += v

    res = pl.pallas_call(
        kern, name=name, grid=(n_rows // tm,), in_specs=in_specs, out_specs=out_specs, out_shape=out_shape,
        compiler_params=_params("arbitrary" if out_acc else "parallel"),
    )(*arrs)
    return res


def _rms(x, gain):
    return x * lax.rsqrt(jnp.mean(x * x, axis=-1, keepdims=True) + EPS) * gain


def _silu(x):
    return x * jax.nn.sigmoid(x)


def _softplus(x):
    return jnp.maximum(x, 0.0) + jnp.log(1.0 + jnp.exp(-jnp.abs(x)))


def _dot_nt(a, b, precision=None):
    return lax.dot_general(a, b, (((1,), (1,)), ((), ())), preferred_element_type=F32, precision=precision)


def _dot_tn(a, b, precision=None):
    return lax.dot_general(a, b, (((0,), (0,)), ((), ())), preferred_element_type=F32, precision=precision)


def _dot(a, b, precision=None):
    return jnp.dot(a, b, preferred_element_type=F32, precision=precision)


def _pre_norm(x, gain, *, name):
    def body(x, g):
        return (_rms(x, g),), ()
    return _rowwise(body, [x], [gain], [(x.shape[1], MXU_DTYPE)], [], tm=_tile(x.shape[0], 512, 2 * SUBLANES), name=name)[0]


def _pre_norm_bwd(x, gain, dh, dx_other, *, name):
    def body(x, dh, dxo, g):
        _, vjp = jax.vjp(_rms, x, g)
        dx, dg = vjp(dh)
        return (dx + dxo,), (dg,)
    return _rowwise(body, [x, dh, dx_other], [gain], [(x.shape[1], F32)], [gain.shape], tm=512, name=name)


def _gain_bwd(x, gain, dh, *, name):
    def body(x, dh, g):
        _, vjp = jax.vjp(lambda g_: _rms(x, g_), g)
        return (), (vjp(dh)[0],)
    return _rowwise(body, [x, dh], [gain], [], [gain.shape], tm=_tile(x.shape[0], 512, SUBLANES), name=name)[0]


def _res_block(x_res, m, g_post, g_pre):
    x_new = x_res + _rms(m, g_post)
    return x_new, _rms(x_new, g_pre)


def _post_pre(x_res, m, g_post, g_pre, *, name):
    def body(x, m, gp, gq):
        return _res_block(x, m, gp, gq), ()
    d = x_res.shape[1]
    return _rowwise(body, [x_res, m], [g_post, g_pre], [(d, F32), (d, MXU_DTYPE)], [], tm=512, name=name)


def _post_pre_bwd(x_res, m, g_post, g_pre, dx_new, dh, *, name):
    def body(x, m, dxn, dh, gp, gq):
        _, vjp = jax.vjp(_res_block, x, m, gp, gq)
        dx, dm, dgp, dgq = vjp((dxn, dh))
        return (dx, dm), (dgp, dgq)
    d = x_res.shape[1]
    return _rowwise(body, [x_res, m, dx_new, dh], [g_post, g_pre], [(d, F32), (d, MXU_DTYPE)],
                    [g_post.shape, g_pre.shape], tm=256, name=name)


def _final_loss_bwd(x_res, m, g_post, target, *, name):
    d = x_res.shape[1]

    def loss_cols(x, m, g, t):
        err = x + _rms(m, g) - t
        return jnp.sum(err * err, axis=0, keepdims=True) * (0.5 / d)

    def body(x, m, t, g):
        cols, vjp = jax.vjp(lambda x_, m_, g_: loss_cols(x_, m_, g_, t), x, m, g)
        dx, dm, dg = vjp(jnp.ones_like(cols))
        return (dx, dm), (dg, cols)
    return _rowwise(body, [x_res, m, target], [g_post], [(d, F32), (d, MXU_DTYPE)], [g_post.shape, (1, d)], tm=256, name=name)


def _swiglu_act(gu, *, name):
    def body(gate, up):
        return (_silu(gate.astype(F32)) * up.astype(F32),), ()
    return _rowwise(body, [_col(gu, D_FF, 0), _col(gu, D_FF, 1)], [], [(D_FF, MXU_DTYPE)], [], tm=256, name=name)[0]


def _swiglu_act_bwd(gu, da, *, name):
    def body(gate, up, da):
        _, vjp = jax.vjp(lambda g, u: _silu(g) * u, gate.astype(F32), up.astype(F32))
        dg, du = vjp(da.astype(F32))
        return (jnp.concatenate([dg, du], axis=1),), ()
    return _rowwise(body, [_col(gu, D_FF, 0), _col(gu, D_FF, 1), da], [], [(2 * D_FF, MXU_DTYPE)], [], tm=256, name=name)[0]


def _lane_head_mask(width, head_dim, head):
    lane = lax.broadcasted_iota(jnp.int32, (1, width), 1)
    return (lane // head_dim) == head


def _mem_attn_heads(q4, k4, v4):
    logits = _bdot_nt(q4, k4)
    p = jnp.exp(logits - jnp.max(logits, axis=-1, keepdims=True))
    return _bdot(p / jnp.sum(p, axis=-1, keepdims=True), v4)


def _mem_heads(q_mem, kv):
    return _heads(q_mem * (ATT_HEAD_DIM ** -0.5), mask=True), _heads(kv[:, :MEM_WIDTH]), _heads(kv[:, MEM_WIDTH:])


def _mem_attn(q_mem, kv):
    return _join_heads(_mem_attn_heads(*_mem_heads(q_mem, kv)))


def _mem_attn_bwd(q_mem, kv, do):
    _, vjp = jax.vjp(_mem_attn_heads, *_mem_heads(q_mem, kv))
    dq4, dk4, dv4 = vjp(_heads(do, mask=True))
    return (_join_heads(dq4) * (ATT_HEAD_DIM ** -0.5),
            jnp.concatenate([dk4[0] + dk4[1], dk4[2] + dk4[3], dv4[0] + dv4[1], dv4[2] + dv4[3]], axis=1))


def _t5_bucket(rel):
    half = REL_BUCKETS // 2
    max_exact = half // 2
    n = np.abs(rel)
    large = max_exact + (np.log(np.maximum(n, 1) / max_exact) / math.log(REL_MAX_DIST / max_exact)
                         * (half - max_exact)).astype(np.int64)
    large = np.minimum(large, half - 1)
    return ((rel > 0) * half + np.where(n < max_exact, n, large)).astype(np.int32)


ATT_DIAGS = ATT_BQ + ATT_W - 1


def _bias_diag_onehot(dil):
    j = np.arange(ATT_DIAGS)
    tiles = []
    for off in (-HALF, 0, HALF):
        rel = j - (ATT_BQ - 1) - HALF - off
        hot = _t5_bucket(rel * dil)[:, None] == np.arange(REL_BUCKETS)[None, :]
        tiles.append(hot & (np.abs(rel) <= HALF)[:, None])
    return np.stack(tiles).astype(np.float32)


def _toeplitz(r):
    lead = r.shape[:-1]
    a = jnp.broadcast_to(r[..., None, :], lead + (ATT_BQ, ATT_DIAGS))
    a = jnp.pad(a, [(0, 0)] * len(lead) + [(0, 0), (0, 1)])
    a = a.reshape(lead + (ATT_BQ * (ATT_DIAGS + 1),))[..., : ATT_BQ * ATT_DIAGS].reshape(lead + (ATT_BQ, ATT_DIAGS))
    return a[..., ATT_BQ - 1: ATT_BQ - 1 + ATT_W]


def _bias_tiles(rel_bias, gi):
    heads = rel_bias[:, 4 * gi: 4 * gi + 4]
    diag = jnp.einsum('tnb,bh->thn', jnp.asarray(_bias_diag_onehot(DILATIONS[gi])), heads, precision=HI)
    return _toeplitz(diag)


def _bias_tiles_bwd(rel_bias, dtiles, gi):
    return jax.vjp(lambda rb: _bias_tiles(rb, gi), rel_bias)[1](dtiles)[0]


def _att_window(i, n_sub):
    start = jnp.clip(i * ATT_BQ - HALF, 0, n_sub - ATT_W)
    off = i * ATT_BQ - HALF - start
    return pl.multiple_of(start, HALF), off


def _att_valid(off):
    q = lax.broadcasted_iota(jnp.int32, (ATT_BQ, ATT_W), 0)
    kk = lax.broadcasted_iota(jnp.int32, (ATT_BQ, ATT_W), 1)
    return jnp.abs(kk - q - HALF - off) <= HALF


def _att_tile_id(i, nq):
    return jnp.where(i == 0, 0, jnp.where(i == nq - 1, 2, 1))


ATT_GROUP_HEADS = 4


def _heads(x, mask=False):
    out = []
    for p in range(2):
        pair = x[:, p * LANES: (p + 1) * LANES]
        for h in range(2):
            out.append(jnp.where(_lane_head_mask(LANES, ATT_HEAD_DIM, h), pair, 0.0) if mask else pair)
    return jnp.stack(out)


def _join_heads(x):
    first = _lane_head_mask(LANES, ATT_HEAD_DIM, 0)
    return jnp.concatenate([jnp.where(first, x[2 * p], x[2 * p + 1]) for p in range(2)], axis=1)


def _head_scalar(x):
    out = []
    for p in range(2):
        pair = x[:, p * LANES: (p + 1) * LANES]
        for h in range(2):
            out.append(jnp.max(jnp.where(_lane_head_mask(LANES, ATT_HEAD_DIM, h), pair, NEG_INF), axis=-1, keepdims=True))
    return jnp.stack(out)


def _bdot(a, b):
    return jnp.einsum('hqk,hkd->hqd', a, b, preferred_element_type=F32)


def _bdot_nt(a, b):
    return jnp.einsum('hqd,hkd->hqk', a, b, preferred_element_type=F32)


def _bdot_tn(a, b):
    return jnp.einsum('hqk,hqd->hkd', a, b, preferred_element_type=F32)


def _att_fwd(qkvm, bias, gi, *, name):
    dil = DILATIONS[gi]
    s_len = qkvm.shape[0]
    n_sub = s_len // dil
    nq = n_sub // ATT_BQ
    assert n_sub % ATT_BQ == 0 and n_sub >= ATT_W
    cols = qkvm.shape[1] // (2 * LANES)
    view = qkvm.reshape(n_sub, dil * qkvm.shape[1])

    def kern(q_ref, k_ref, v_ref, b_ref, o_ref, lse_ref):
        start, off = _att_window(pl.program_id(1), n_sub)
        valid = _att_valid(off)
        q4 = _heads(q_ref[...].astype(F32) * (ATT_HEAD_DIM ** -0.5), mask=True)
        k4 = _heads(k_ref[pl.ds(start, ATT_W), :].astype(F32))
        v4 = _heads(v_ref[pl.ds(start, ATT_W), :].astype(F32))
        s = jnp.where(valid, _bdot_nt(q4, k4) + b_ref[...], NEG_INF)
        mx = jnp.max(s, axis=-1, keepdims=True)
        p = jnp.exp(s - mx)
        den = jnp.sum(p, axis=-1, keepdims=True)
        o_ref[...] = _join_heads(_bdot(p, v4) / den)
        lse_ref[...] = _join_heads(jnp.broadcast_to(mx + jnp.log(den), (ATT_GROUP_HEADS, ATT_BQ, LANES)))

    def qkv_spec(which, full):
        shape = (n_sub, 2 * LANES) if full else (ATT_BQ, 2 * LANES)
        return pl.BlockSpec(shape, lambda r, i: (0 if full else i, r * cols + which * 3 + gi))

    out_spec = pl.BlockSpec((ATT_BQ, 2 * LANES), lambda r, i: (i, r))
    o, lse = pl.pallas_call(
        kern, name=name, grid=(dil, nq),
        in_specs=[qkv_spec(0, False), qkv_spec(1, True), qkv_spec(2, True),
                  pl.BlockSpec((None, ATT_GROUP_HEADS, ATT_BQ, ATT_W), lambda r, i: (_att_tile_id(i, nq), 0, 0, 0))],
        out_specs=[out_spec, out_spec],
        out_shape=[jax.ShapeDtypeStruct((n_sub, dil * 2 * LANES), F32)] * 2,
        compiler_params=_params("parallel", "arbitrary"),
    )(view, view, view, bias)
    return o.reshape(s_len, 2 * LANES), lse.reshape(s_len, 2 * LANES)


def _att_bwd(qkvm, bias, lse_tot, delta, dcat, gi, *, name):
    dil = DILATIONS[gi]
    s_len = qkvm.shape[0]
    n_sub = s_len // dil
    nq = n_sub // ATT_BQ
    cols = qkvm.shape[1] // (2 * LANES)
    dcols = dcat.shape[1] // (2 * LANES)
    view = qkvm.reshape(n_sub, dil * qkvm.shape[1])
    lse_v = lse_tot.reshape(n_sub, dil * 2 * LANES)
    delta_v = delta.reshape(n_sub, dil * 2 * LANES)
    dcat_v = dcat.reshape(n_sub, dil * dcat.shape[1])

    def kern(q_ref, k_ref, v_ref, b_ref, lse_ref, dl_ref, dm_ref, dq_ref, dk_ref, dv_ref, db_ref):
        r, i = pl.program_id(0), pl.program_id(1)
        start, off = _att_window(i, n_sub)
        valid = _att_valid(off)
        tile = _att_tile_id(i, nq)

        @pl.when(i == 0)
        def _():
            dk_ref[...] = jnp.zeros_like(dk_ref)
            dv_ref[...] = jnp.zeros_like(dv_ref)

        @pl.when((i == 0) & (r == 0))
        def _():
            db_ref[...] = jnp.zeros_like(db_ref)

        q4 = _heads(q_ref[...].astype(F32) * (ATT_HEAD_DIM ** -0.5), mask=True)
        k4 = _heads(k_ref[pl.ds(start, ATT_W), :].astype(F32))
        v4 = _heads(v_ref[pl.ds(start, ATT_W), :].astype(F32))
        dm4 = _heads(dm_ref[...], mask=True)
        s = jnp.where(valid, _bdot_nt(q4, k4) + b_ref[tile], NEG_INF)
        p = jnp.exp(s - _head_scalar(lse_ref[...]))
        ds = p * (_bdot_nt(dm4, v4) - _head_scalar(dl_ref[...]))
        dq_ref[...] = _join_heads(_bdot(ds, k4)) * (ATT_HEAD_DIM ** -0.5)
        dk4 = _bdot_tn(ds, q4)
        dv4 = _bdot_tn(p, dm4)
        dk_ref[pl.ds(start, ATT_W), :] += jnp.concatenate([dk4[0] + dk4[1], dk4[2] + dk4[3]], axis=1)
        dv_ref[pl.ds(start, ATT_W), :] += jnp.concatenate([dv4[0] + dv4[1], dv4[2] + dv4[3]], axis=1)
        db_ref[tile] += ds

    def qkv_spec(which, full):
        shape = (n_sub, 2 * LANES) if full else (ATT_BQ, 2 * LANES)
        return pl.BlockSpec(shape, lambda r, i: (0 if full else i, r * cols + which * 3 + gi))

    blk = pl.BlockSpec((ATT_BQ, 2 * LANES), lambda r, i: (i, r))
    full = pl.BlockSpec((n_sub, 2 * LANES), lambda r, i: (0, r))
    bias_spec = pl.BlockSpec(bias.shape, lambda r, i: (0, 0, 0, 0))
    sub = jax.ShapeDtypeStruct((n_sub, dil * 2 * LANES), F32)
    dq, dk, dv, db = pl.pallas_call(
        kern, name=name, grid=(dil, nq),
        in_specs=[qkv_spec(0, False), qkv_spec(1, True), qkv_spec(2, True), bias_spec, blk, blk,
                  pl.BlockSpec((ATT_BQ, 2 * LANES), lambda r, i: (i, r * dcols + gi))],
        out_specs=[blk, full, full, bias_spec],
        out_shape=[sub, sub, sub, jax.ShapeDtypeStruct(bias.shape, F32)],
        compiler_params=_params("arbitrary", "arbitrary"),
    )(view, view, view, bias, lse_v, delta_v, dcat_v)
    return dq.reshape(s_len, -1), dk.reshape(s_len, -1), dv.reshape(s_len, -1), db


def _att_combine(o_g, lse_g, qkvm, kv_mem, *, name):
    def body(o0, o1, o2, l0, l1, l2, qm, kv):
        mx = jnp.maximum(jnp.maximum(l0, l1), l2)
        tot = mx + jnp.log(jnp.exp(l0 - mx) + jnp.exp(l1 - mx) + jnp.exp(l2 - mx))
        mixed = [o * jnp.exp(l - tot) for o, l in ((o0, l0), (o1, l1), (o2, l2))]
        return (jnp.concatenate(mixed + [_mem_attn(qm.astype(F32), kv)], axis=1), tot), ()
    return _rowwise(body, list(o_g) + list(lse_g) + [_col(qkvm, MEM_WIDTH, (3 * TOK_WIDTH) // MEM_WIDTH)], [kv_mem],
                    [(D_MODEL, F32), (MEM_WIDTH, F32)], [], tm=256, name=name)


def _head_sum_matrix():
    a = np.arange(MEM_WIDTH)
    return jnp.asarray((a[:, None] // ATT_HEAD_DIM == a[None, :] // ATT_HEAD_DIM).astype(np.float32))


def _att_bwd_prep(cat, dcat, qkvm, kv_mem, *, name):
    def body(cat, dcat, qm, kv, hs):
        prod = cat * dcat
        summed = prod[:, 0:256] + prod[:, 256:512] + prod[:, 512:768]
        delta = _dot(summed, hs, precision=HI)
        dqm, dkv = _mem_attn_bwd(qm.astype(F32), kv, dcat[:, TOK_WIDTH:])
        return (delta, dqm), (dkv,)
    return _rowwise(body, [cat, dcat, _col(qkvm, MEM_WIDTH, (3 * TOK_WIDTH) // MEM_WIDTH)], [kv_mem, _head_sum_matrix()],
                    [(MEM_WIDTH, F32), (MEM_WIDTH, F32)], [kv_mem.shape], tm=256, name=name)


def _dn_conv_post(s, j):
    scale = jnp.where(j < DN_HEADS, DN_HEAD_DIM ** -0.5, 1.0)
    normed = s * lax.rsqrt(jnp.sum(s * s, axis=-1, keepdims=True) + EPS) * scale
    return jnp.where(j >= 2 * DN_HEADS, s, normed)


def _shift_rows(x, sh):
    n = x.shape[0]
    row = lax.broadcasted_iota(jnp.int32, (n, 1), 0)
    rolled = pltpu.roll(x, (-sh) % n, 0)
    return jnp.where((row + sh >= 0) & (row + sh < n), rolled, 0.0)


def _dn_conv_taps(x, w_ref):
    c = x * w_ref[pl.ds(DN_CONV // 2, 1), :]
    for jj in range(DN_CONV):
        if jj != DN_CONV // 2:
            c = c + _shift_rows(x, jj - DN_CONV // 2) * w_ref[pl.ds(jj, 1), :]
    return c


def _dn_conv_fwd(proj, conv_w, *, name):
    s_len = proj.shape[0]
    width = 3 * TOK_WIDTH

    def kern(x_ref, w_ref, o_ref):
        j = pl.program_id(0)
        o_ref[...] = _dn_conv_post(_silu(_dn_conv_taps(x_ref[...], w_ref)), j)

    return pl.pallas_call(
        kern, name=name, grid=(width // LANES,),
        in_specs=[pl.BlockSpec((s_len, LANES), lambda j: (0, j)), pl.BlockSpec((DN_CONV, LANES), lambda j: (0, j))],
        out_specs=pl.BlockSpec((s_len, LANES), lambda j: (0, j)),
        out_shape=jax.ShapeDtypeStruct((s_len, width), F32),
        compiler_params=_params("parallel"),
    )(proj, conv_w)


def _dn_conv_bwd(proj, conv_w, d_fwd, d_bwd, which, *, name):
    s_len = proj.shape[0]

    def kern(x_ref, w_ref, df_ref, db_ref, dx_ref, dw_ref):
        j = pl.program_id(0) + which * DN_HEADS
        x = x_ref[...]
        c = _dn_conv_taps(x, w_ref)
        _, vjp = jax.vjp(lambda c_: _dn_conv_post(_silu(c_), j), c)
        dc = vjp(df_ref[...] + db_ref[...])[0]
        dx = dc * w_ref[pl.ds(DN_CONV // 2, 1), :]
        for jj in range(DN_CONV):
            sh = jj - DN_CONV // 2
            if sh != 0:
                dx = dx + _shift_rows(dc, -sh) * w_ref[pl.ds(jj, 1), :]
            dw_ref[pl.ds(jj, 1), :] = jnp.sum(dc * _shift_rows(x, sh), axis=0, keepdims=True)
        dx_ref[...] = dx

    return pl.pallas_call(
        kern, name=name, grid=(DN_HEADS,),
        in_specs=[pl.BlockSpec((s_len, LANES), lambda j: (0, j + which * DN_HEADS)),
                  pl.BlockSpec((DN_CONV, LANES), lambda j: (0, j + which * DN_HEADS)),
                  pl.BlockSpec((s_len, LANES), lambda j: (0, j)),
                  pl.BlockSpec((s_len, LANES), lambda j: (0, j))],
        out_specs=[pl.BlockSpec((s_len, LANES), lambda j: (0, j)), pl.BlockSpec((DN_CONV, LANES), lambda j: (0, j))],
        out_shape=[jax.ShapeDtypeStruct((s_len, TOK_WIDTH), F32), jax.ShapeDtypeStruct((DN_CONV, TOK_WIDTH), F32)],
        compiler_params=_params("parallel"),
    )(proj, conv_w, d_fwd, d_bwd)


GATE_TM = 2 * DN_CHUNK


FWD_GATE_LANES = 2 * DN_HEADS


def _gate_constants():
    i = np.arange(GATE_TM)
    same = (i[:, None] // DN_CHUNK) == (i[None, :] // DN_CHUNK)
    cum_f = same & (i[None, :] <= i[:, None])
    cum_r = same & (i[None, :] >= i[:, None])
    return tuple(jnp.asarray(np.asarray(a, np.float32)) for a in (cum_f, cum_r, same))


def _gate_params(p):
    z = jnp.zeros((DN_HEADS,), F32)
    return jnp.concatenate([p[0], z, p[1], z, jnp.zeros((LANES - N_GATES,), F32)]).reshape(1, LANES)


def _gate_params_bwd(dp):
    return jnp.stack([dp[0, 0:DN_HEADS], dp[0, 2 * DN_HEADS: 3 * DN_HEADS]])


def _dn_gates(gate_in, a_cols, dt_cols, cum_f, cum_r, tot):
    g = -jnp.exp(a_cols) * _softplus(gate_in + dt_cols)
    fwd_lane = lax.broadcasted_iota(jnp.int32, (1, LANES), 1) < FWD_GATE_LANES
    gc = jnp.where(fwd_lane, _dot(cum_f, g, precision=HI), _dot(cum_r, g, precision=HI))
    return gc, _dot(tot, g, precision=HI), jax.nn.sigmoid(gate_in)


def _dn_gates_fwd(proj, a_cols, dt_cols, *, name):
    def body(gi, *consts):
        return _dn_gates(gi, *consts), ()
    return _rowwise(body, [_col(proj, LANES, DN_IN_PAD // LANES - 1)], [a_cols, dt_cols, *_gate_constants()],
                    [(LANES, F32)] * 3, [], tm=GATE_TM, name=name)


def _dn_gates_bwd(proj, a_cols, dt_cols, d_gates, *, name):
    def body(gi, gcf, gtf, bf, gcr, gtr, br, a, dt, *consts):
        _, vjp = jax.vjp(lambda gi_, a_, dt_: _dn_gates(gi_, a_, dt_, *consts), gi, a, dt)
        dgi, da, ddt = vjp((gcf + gcr, gtf + gtr, bf + br))
        return (dgi,), (da, ddt)
    return _rowwise(body, [_col(proj, LANES, DN_IN_PAD // LANES - 1), *d_gates[0], *d_gates[1]],
                    [a_cols, dt_cols, *_gate_constants()], [(LANES, F32)], [a_cols.shape, dt_cols.shape],
                    tm=GATE_TM, name=name)


INV_BASE = 8


def _block_id_equal(c, size):
    i = lax.broadcasted_iota(jnp.int32, (c, c), 0) // size
    j = lax.broadcasted_iota(jnp.int32, (c, c), 1) // size
    return (i == j).astype(F32)


def _unit_tri_inverse_impl(lmat):
    c = lmat.shape[0]
    eye = _block_id_equal(c, 1)
    same = _block_id_equal(c, INV_BASE)
    neg = -lmat * same
    inv = eye + neg
    power = neg
    for _ in range(int(math.log2(INV_BASE)) - 1):
        power = _dot(power, power)
        inv = inv + _dot(inv, power)
    size = INV_BASE
    while size < c:
        bigger = _block_id_equal(c, 2 * size)
        inv = inv - _dot(_dot(inv, lmat * (bigger - same)), inv)
        same, size = bigger, 2 * size
    resid = eye - _dot(eye + lmat, inv, precision=HI)
    return inv + _dot(inv, resid)


@jax.custom_vjp
def _unit_tri_inverse(lmat):
    return _unit_tri_inverse_impl(lmat)


def _unit_tri_inverse_fwd(lmat):
    inv = _unit_tri_inverse_impl(lmat)
    return inv, inv


def _unit_tri_inverse_bwd(inv, d_inv):
    return (-_dot_tn(inv, _dot_nt(d_inv, inv)),)


_unit_tri_inverse.defvjp(_unit_tri_inverse_fwd, _unit_tri_inverse_bwd)


def _dn_chunk(q, k, v, gates_t, gc_row, tot_row, beta_row, state, tri, inverse):
    c = q.shape[0]
    assert c == DN_HEAD_DIM
    eye = _block_id_equal(c, 1)

    def along_rows(x, pick):
        return jnp.broadcast_to(jnp.sum(x * pick, axis=0, keepdims=True), (c, c))

    gc_j = along_rows(gates_t[0], gc_row)
    gc = gc_j.T
    g_tot = along_rows(gates_t[1], tot_row)
    beta = along_rows(gates_t[2], beta_row).T
    decay = jnp.exp(jnp.where(tri > 0, gc - gc_j, NEG_INF))
    k_beta = k * beta
    inv = inverse((tri - eye) * (_dot_nt(k_beta, k) * decay))
    e_gc = jnp.exp(gc)
    u = _dot(inv, v * beta)
    w = _dot(inv, k_beta * e_gc)
    intra = tri * (_dot_nt(q, k) * decay)
    v_new = u - _dot(w, state)
    out = _dot(q * e_gc, state) + _dot(intra, v_new)
    state = state * jnp.exp(g_tot) + _dot_tn(k * jnp.exp(g_tot - gc), v_new)
    return out, state


def _dn_tri():
    i = np.arange(DN_CHUNK)
    tri = np.stack([(i[None, :] <= i[:, None]), (i[None, :] >= i[:, None])]).astype(np.float32)
    return jnp.asarray(np.repeat(tri, DN_HEADS, axis=0))


def _dn_gate_picks():
    picks = np.zeros((3, 2 * DN_HEADS, 2 * DN_CHUNK, 1), np.float32)
    for d in range(2):
        for h in range(DN_HEADS):
            alpha = d * DN_CHUNK + d * 2 * DN_HEADS + h
            picks[0, d * DN_HEADS + h, alpha] = 1.0
            picks[1, d * DN_HEADS + h, alpha] = 1.0
            picks[2, d * DN_HEADS + h, alpha + DN_HEADS] = 1.0
    return jnp.asarray(picks)


def _stack_chains(fwd_ref, rev_ref):
    return jnp.stack([r[:, _head_cols(h)] for r in (fwd_ref, rev_ref) for h in range(DN_HEADS)])


def _unstack_chains(val, fwd_ref, rev_ref):
    for d, r in enumerate((fwd_ref, rev_ref)):
        for h in range(DN_HEADS):
            r[:, _head_cols(h)] = val[d * DN_HEADS + h]


def _gates_transposed(fwd_refs, rev_refs):
    return jnp.stack([jnp.concatenate([f[...].T, r[...].T], axis=0) for f, r in zip(fwd_refs, rev_refs, strict=True)])


def _dn_row_spec(nc, col, reverse, width=TOK_WIDTH):
    return pl.BlockSpec((DN_CHUNK, width), lambda t: ((nc - 1 - t) if reverse else t, col))


def _dn_state_spec(nc, reverse):
    return pl.BlockSpec((None, DN_HEADS, DN_HEAD_DIM, DN_HEAD_DIM), lambda t: ((nc - 1 - t) if reverse else t, 0, 0, 0))


def _head_cols(h):
    return pl.ds(h * DN_HEAD_DIM, DN_HEAD_DIM)


def _const_spec(arr):
    return pl.BlockSpec(arr.shape, functools.partial(lambda t, n: (0,) * n, n=arr.ndim))


def _dn_chains(inverse):
    return jax.vmap(lambda q, k, v, gates_t, *rest: _dn_chunk(q, k, v, gates_t, *rest, inverse),
                    in_axes=(0, 0, 0, None, 0, 0, 0, 0, 0))


def _dn_scan_fwd(qkv, gates, *, name):
    s_len = qkv.shape[0]
    nc = s_len // DN_CHUNK
    tri, picks = _dn_tri(), _dn_gate_picks()

    def kern(*refs):
        ins, (tri_ref, pick_ref, of_ref, or_ref, sf_ref, sr_ref, state) = refs[:12], refs[12:]

        @pl.when(pl.program_id(0) == 0)
        def _():
            state[...] = jnp.zeros_like(state)

        entry = state[...]
        qkv_c = [_stack_chains(ins[i], ins[6 + i]) for i in range(3)]
        gates_t = _gates_transposed(ins[3:6], ins[9:12])
        out, new = _dn_chains(_unit_tri_inverse_impl)(*qkv_c, gates_t, pick_ref[0], pick_ref[1], pick_ref[2], entry, tri_ref[...])
        sf_ref[...] = entry[:DN_HEADS]
        sr_ref[...] = entry[DN_HEADS:]
        _unstack_chains(out, of_ref, or_ref)
        state[...] = new

    in_specs = []
    for rev in (False, True):
        in_specs += [_dn_row_spec(nc, col, rev) for col in (0, 1, 2)] + [_dn_row_spec(nc, 0, rev, LANES)] * 3
    in_specs += [_const_spec(tri), _const_spec(picks)]
    return pl.pallas_call(
        kern, name=name, grid=(nc,), in_specs=in_specs,
        out_specs=[_dn_row_spec(nc, 0, False), _dn_row_spec(nc, 0, True), _dn_state_spec(nc, False), _dn_state_spec(nc, True)],
        out_shape=[jax.ShapeDtypeStruct((s_len, TOK_WIDTH), F32)] * 2
        + [jax.ShapeDtypeStruct((nc, DN_HEADS, DN_HEAD_DIM, DN_HEAD_DIM), F32)] * 2,
        scratch_shapes=[pltpu.VMEM((2 * DN_HEADS, DN_HEAD_DIM, DN_HEAD_DIM), F32)],
        compiler_params=_params("arbitrary"),
    )(*([qkv, qkv, qkv, *gates] * 2), tri, picks)


def _dn_scan_bwd(qkv, gates, states, d_o, *, name):
    s_len = qkv.shape[0]
    nc = s_len // DN_CHUNK
    tri, picks = _dn_tri(), _dn_gate_picks()

    def kern(*refs):
        ins, tri_ref, pick_ref, outs, d_state = refs[:16], refs[16], refs[17], refs[18:30], refs[30]

        @pl.when(pl.program_id(0) == 0)
        def _():
            d_state[...] = jnp.zeros_like(d_state)

        qkv_c = [_stack_chains(ins[i], ins[8 + i]) for i in range(3)]
        gates_t = _gates_transposed(ins[3:6], ins[11:14])
        entry = jnp.concatenate([ins[6][...], ins[14][...]], axis=0)
        d_out = _stack_chains(ins[7], ins[15])
        tri_v, picks_v = tri_ref[...], pick_ref[...]
        _, vjp = jax.vjp(lambda q, k, v, g, s: _dn_chains(_unit_tri_inverse)(q, k, v, g, picks_v[0], picks_v[1], picks_v[2], s, tri_v),
                         *qkv_c, gates_t, entry)
        dq, dk, dv, d_gates_t, d_entry = vjp((d_out, d_state[...]))
        for i, val in enumerate((dq, dk, dv)):
            _unstack_chains(val, outs[i], outs[6 + i])
        for i in range(3):
            outs[3 + i][...] = d_gates_t[i, :DN_CHUNK].T
            outs[9 + i][...] = d_gates_t[i, DN_CHUNK:].T
        d_state[...] = d_entry

    in_specs, out_specs, out_shape = [], [], []
    for rev in (True, False):
        in_specs += [_dn_row_spec(nc, col, rev) for col in (0, 1, 2)] + [_dn_row_spec(nc, 0, rev, LANES)] * 3
        in_specs += [_dn_state_spec(nc, rev), _dn_row_spec(nc, 0, rev)]
        out_specs += [_dn_row_spec(nc, 0, rev)] * 3 + [_dn_row_spec(nc, 0, rev, LANES)] * 3
        out_shape += [jax.ShapeDtypeStruct((s_len, TOK_WIDTH), F32)] * 3 + [jax.ShapeDtypeStruct((s_len, LANES), F32)] * 3
    in_specs += [_const_spec(tri), _const_spec(picks)]
    res = pl.pallas_call(
        kern, name=name, grid=(nc,), in_specs=in_specs, out_specs=out_specs, out_shape=out_shape,
        scratch_shapes=[pltpu.VMEM((2 * DN_HEADS, DN_HEAD_DIM, DN_HEAD_DIM), F32)],
        compiler_params=_params("arbitrary"),
    )(*[a for d in range(2) for a in (qkv, qkv, qkv, *gates, states[d], d_o)], tri, picks)
    return (res[0:3], res[3:6]), (res[6:9], res[9:12])


def _dn_out_head(o_f, o_b, z, gain):
    o = o_f + o_b
    return o * lax.rsqrt(jnp.mean(o * o, axis=-1, keepdims=True) + EPS) * gain * _silu(z)


def _dn_out(o_fwd, o_rev, proj, gain, qkv_kv_mem, *, name):
    def body(of, ob, z, qm, g, kv):
        heads = []
        for h in range(DN_HEADS):
            sl = slice(h * DN_HEAD_DIM, (h + 1) * DN_HEAD_DIM)
            heads.append(_dn_out_head(of[:, sl], ob[:, sl], z[:, sl], g))
        return (jnp.concatenate(heads + [_mem_attn(qm, kv)], axis=1),), ()
    return _rowwise(body, [o_fwd, o_rev, _col(proj, TOK_WIDTH, 3),
                           _col(proj, MEM_WIDTH, (4 * TOK_WIDTH) // MEM_WIDTH)], [gain, qkv_kv_mem],
                    [(D_MODEL, MXU_DTYPE)], [], tm=256, name=name)[0]


def _dn_out_bwd(o_fwd, o_rev, proj, gain, kv_mem, dcat, *, name):
    def body(of, ob, z, qm, dcat, g, kv):
        dos, dzs = [], []
        dgain = jnp.zeros_like(g)
        for h in range(DN_HEADS):
            sl = slice(h * DN_HEAD_DIM, (h + 1) * DN_HEAD_DIM)
            _, vjp = jax.vjp(_dn_out_head, of[:, sl], ob[:, sl], z[:, sl], g)
            d_of, _, dz, dg = vjp(dcat[:, sl])
            dos.append(d_of)
            dzs.append(dz)
            dgain = dgain + dg
        dqm, dkv = _mem_attn_bwd(qm, kv, dcat[:, TOK_WIDTH:])
        return (jnp.concatenate(dos, axis=1), jnp.concatenate(dzs, axis=1), dqm), (dgain, dkv)
    return _rowwise(body, [o_fwd, o_rev, _col(proj, TOK_WIDTH, 3),
                           _col(proj, MEM_WIDTH, (4 * TOK_WIDTH) // MEM_WIDTH), dcat], [gain, kv_mem],
                    [(TOK_WIDTH, F32), (TOK_WIDTH, F32), (MEM_WIDTH, F32)], [gain.shape, kv_mem.shape], tm=256, name=name)


def _pad_dn_w_in(w):
    gates = w[:, 4 * TOK_WIDTH: 4 * TOK_WIDTH + N_GATES]
    zeros = jnp.zeros((w.shape[0], DN_IN_PAD - DN_IN), w.dtype)
    return jnp.concatenate([w[:, :4 * TOK_WIDTH], w[:, 4 * TOK_WIDTH + N_GATES:], gates, zeros], axis=1)


def _unpad_dn_w_in(w):
    q_mem = w[:, 4 * TOK_WIDTH: 4 * TOK_WIDTH + MEM_WIDTH]
    gates = w[:, 4 * TOK_WIDTH + MEM_WIDTH: 4 * TOK_WIDTH + MEM_WIDTH + N_GATES]
    return jnp.concatenate([w[:, :4 * TOK_WIDTH], gates, q_mem], axis=1)


def _ffn_fwd(h, w_gu, w_d, tag):
    gu = _mm(h, w_gu, out_dtype=MXU_DTYPE, name=f"ffn_gu_{tag}")
    act = _swiglu_act(gu, name=f"ffn_act_{tag}")
    return gu, act, _mm(act, w_d, name=f"ffn_down_{tag}")


def _ffn_bwd(h, gu, act, w_gu, w_d, df, tag):
    d_act = _mm(df, w_d, tb=True, out_dtype=MXU_DTYPE, name=f"ffn_dact_{tag}")
    d_wd = _mm(act, df, ta=True, out_dtype=LINK_DTYPE, name=f"ffn_dwd_{tag}")
    d_gu = _swiglu_act_bwd(gu, d_act, name=f"ffn_dgu_{tag}")
    dh = _mm(d_gu, w_gu, tb=True, name=f"ffn_dh_{tag}")
    d_wgu = _mm(h, d_gu, ta=True, out_shards=_shards_of(w_gu), out_dtype=LINK_DTYPE, name=f"ffn_dwgu_{tag}")
    return dh, d_wgu, d_wd


def _local_step(x, mem, target, p):
    g = {}
    row = lambda v: v.reshape(1, -1)
    gains = {k: [row(p[k][i]) for i in range(2)] for k in
             ("mem_norm", "norm_mix_pre", "norm_mix_post", "norm_ffn_pre", "norm_ffn_post")}
    out_gain = row(p["dn_out_norm"])
    a_cols, dt_cols = _gate_params(p["dn_a_log"]), _gate_params(p["dn_dt_bias"])

    h0 = _pre_norm(x, gains["norm_mix_pre"][0], name="pre0")
    mem_n = [_pre_norm(mem, gains["mem_norm"][i], name=f"mem_norm{i}") for i in range(2)]
    qkvm = _mm(h0, p["att_w_in"], out_dtype=MXU_DTYPE, name="att_in")
    bias = [_bias_tiles(p["rel_bias"], gi) for gi in range(3)]
    att = [_att_fwd(qkvm, bias[gi], gi, name=f"att_fwd{gi}") for gi in range(3)]
    att_w_out, *mem_w_kv = p["att_rest_weights"](att[2][0])
    kv_mem = [_mm(mem_n[i], mem_w_kv[i], name=f"mem_kv{i}") for i in range(2)]
    cat0, lse_tot = _att_combine([a[0] for a in att], [a[1] for a in att], qkvm, kv_mem[0], name="att_combine")
    mo0 = _mm(cat0, att_w_out, name="att_out")
    x1, h1 = _post_pre(x, mo0, gains["norm_mix_post"][0], gains["norm_ffn_pre"][0], name="post_mix0")
    w_gu0, w_d0 = p["ffn_weights"](0, h1)
    gu0, act0, f0 = _ffn_fwd(h1, w_gu0, w_d0, 0)
    x2, h2 = _post_pre(x1, f0, gains["norm_ffn_post"][0], gains["norm_mix_pre"][1], name="post_ffn0")

    dn_w_in, dn_w_out = p["dn_weights"](h2)
    proj = _mm(h2, dn_w_in, name="dn_in")
    qkv = _dn_conv_fwd(proj, p["dn_conv"], name="dn_conv")
    gates = _dn_gates_fwd(proj, a_cols, dt_cols, name="dn_gates")
    o_fwd, o_rev, st_fwd, st_rev = _dn_scan_fwd(qkv, gates, name="dn_scan")
    cat1 = _dn_out(o_fwd, o_rev, proj, out_gain, kv_mem[1], name="dn_outnorm")
    mo1 = _mm(cat1, dn_w_out, name="dn_out")
    x3, h3 = _post_pre(x2, mo1, gains["norm_mix_post"][1], gains["norm_ffn_pre"][1], name="post_mix1")
    w_gu1, w_d1 = p["ffn_weights"](1, h3)
    gu1, act1, f1 = _ffn_fwd(h3, w_gu1, w_d1, 1)

    dx3, df1, dg_ffn_post1, loss_cols = _final_loss_bwd(x3, f1, gains["norm_ffn_post"][1], target, name="loss_bwd")
    dh3, d_wgu1, d_wd1 = _ffn_bwd(h3, gu1, act1, w_gu1, w_d1, df1, 1)
    sent = p["grads_ready"]("ffn1", {("ffn_w_gate_up", 1): d_wgu1, ("ffn_w_down", 1): d_wd1})
    dx2, dmo1, dg_mix_post1, dg_ffn_pre1 = _post_pre_bwd(x2, mo1, gains["norm_mix_post"][1] + sent, gains["norm_ffn_pre"][1],
                                                         dx3, dh3, name="post_mix1_bwd")
    dcat1 = _mm(dmo1, dn_w_out, tb=True, name="dn_out_dx")
    g["dn_w_out"] = _mm(cat1, dmo1, ta=True, out_dtype=LINK_DTYPE, name="dn_out_dw")
    d_o, dz, dqm1, d_out_gain, dkv1 = _dn_out_bwd(o_fwd, o_rev, proj, out_gain, kv_mem[1], dcat1, name="dn_outnorm_bwd")
    (d_f, dg_f), (d_r, dg_r) = _dn_scan_bwd(qkv, gates, (st_fwd, st_rev), d_o, name="dn_scan_bwd")
    d_gate_cols, d_a_cols, d_dt_cols = _dn_gates_bwd(proj, a_cols, dt_cols, (dg_f, dg_r), name="dn_gates_bwd")
    d_pre, d_conv = zip(*[_dn_conv_bwd(proj, p["dn_conv"], d_f[which], d_r[which], which, name=f"dn_conv_bwd{which}")
                          for which in range(3)])
    dproj = jnp.concatenate(list(d_pre) + [dz, dqm1, d_gate_cols], axis=1).astype(MXU_DTYPE)
    dh2 = _mm(dproj, dn_w_in, tb=True, name="dn_in_dx")
    g["dn_w_in"] = _mm(h2, dproj, ta=True, out_dtype=LINK_DTYPE, name="dn_in_dw")
    g["dn_conv"] = jnp.concatenate(d_conv, axis=1)
    g["dn_a_log"] = _gate_params_bwd(d_a_cols)
    g["dn_dt_bias"] = _gate_params_bwd(d_dt_cols)
    g["dn_out_norm"] = d_out_gain

    d_mem_kv1 = _mm(mem_n[1], dkv1, ta=True, out_dtype=LINK_DTYPE, name="mem_kv_dw1")
    sent = p["grads_ready"]("dn", {("dn_w_in", 0): g["dn_w_in"], ("dn_w_out", 0): g["dn_w_out"], ("mem_w_kv", 1): d_mem_kv1})
    dx1, df0, dg_ffn_post0, dg_mix_pre1 = _post_pre_bwd(x1, f0, gains["norm_ffn_post"][0] + sent, gains["norm_mix_pre"][1],
                                                        dx2, dh2, name="post_ffn0_bwd")
    dh1, d_wgu0, d_wd0 = _ffn_bwd(h1, gu0, act0, w_gu0, w_d0, df0, 0)
    sent = p["grads_ready"]("ffn0", {("ffn_w_gate_up", 0): d_wgu0, ("ffn_w_down", 0): d_wd0})
    dx0, dmo0, dg_mix_post0, dg_ffn_pre0 = _post_pre_bwd(x, mo0, gains["norm_mix_post"][0] + sent, gains["norm_ffn_pre"][0],
                                                         dx1, dh1, name="post_mix0_bwd")
    dcat0 = _mm(dmo0, att_w_out, tb=True, name="att_out_dx")
    g["att_w_out"] = _mm(cat0, dmo0, ta=True, out_dtype=LINK_DTYPE, name="att_out_dw")
    delta, dqm0, dkv0 = _att_bwd_prep(cat0, dcat0, qkvm, kv_mem[0], name="att_bwd_prep")
    d_mem_kv0 = _mm(mem_n[0], dkv0, ta=True, out_dtype=LINK_DTYPE, name="mem_kv_dw0")
    sent = p["grads_ready"]("att_out", {("att_w_out", 0): g["att_w_out"], ("mem_w_kv", 0): d_mem_kv0})
    att_b = [_att_bwd(qkvm, bias[gi] + sent, lse_tot, delta, dcat0, gi, name=f"att_bwd{gi}") for gi in range(3)]
    dqkvm = jnp.concatenate([a[w] for w in range(3) for a in att_b] + [dqm0], axis=1).astype(MXU_DTYPE)
    g["rel_bias"] = sum(_bias_tiles_bwd(p["rel_bias"], att_b[gi][3], gi) for gi in range(3))
    g["att_w_in"] = _mm(h0, dqkvm, ta=True, out_shards=_shards_of(p["att_w_in"]), out_dtype=LINK_DTYPE, name="att_in_dw")
    sent = p["grads_ready"]("att_in", {("att_w_in", 0): g["att_w_in"]})
    dh0 = _mm(dqkvm, p["att_w_in"], tb=True, name="att_in_dx")
    grad_x, dg_mix_pre0 = _pre_norm_bwd(x, gains["norm_mix_pre"][0] + sent, dh0, dx0, name="pre0_bwd")

    d_mem_norm = []
    for i, dkv in enumerate((dkv0, dkv1)):
        d_mem_n = _mm(dkv, mem_w_kv[i], tb=True, name=f"mem_kv_dx{i}")
        d_mem_norm.append(_gain_bwd(mem, gains["mem_norm"][i], d_mem_n, name=f"mem_norm_bwd{i}"))
    g["mem_w_kv"] = [d_mem_kv0, d_mem_kv1]
    g["mem_norm"] = jnp.concatenate(d_mem_norm, axis=0)
    g["norm_mix_pre"] = jnp.concatenate([dg_mix_pre0, dg_mix_pre1], axis=0)
    g["norm_mix_post"] = jnp.concatenate([dg_mix_post0, dg_mix_post1], axis=0)
    g["norm_ffn_pre"] = jnp.concatenate([dg_ffn_pre0, dg_ffn_pre1], axis=0)
    g["norm_ffn_post"] = jnp.concatenate([dg_ffn_post0, dg_ffn_post1], axis=0)
    g["ffn_w_gate_up"] = [d_wgu0, d_wgu1]
    g["ffn_w_down"] = [d_wd0, d_wd1]
    return loss_cols, grad_x, g


N_CHIPS = 4
N_DEV = 8
MESH = pl.DeviceIdType.MESH
BIG = (("att_w_in", (1, 1024, 640), 2), ("att_w_out", (1, 256, 1024), 1), ("dn_w_in", (1, 1024, 838), 2),
       ("dn_w_out", (1, 256, 1024), 1), ("mem_w_kv", (2, 256, 512), 1), ("ffn_w_gate_up", (2, 1024, 1408), 2),
       ("ffn_w_down", (2, 704, 1024), 1))


def _mesh_pos():
    return lax.axis_index("x"), lax.axis_index("y"), lax.axis_index("c")


def _other_chips(x, y):
    return [(1 - x, y), (x, 1 - y), (1 - x, 1 - y)]


ANY = pl.BlockSpec(memory_space=pl.ANY)


def _all_reduce_small(v, *, name):
    rows, cols = v.shape
    flips = [(dx, dy, dc) for dx in (0, 1) for dy in (0, 1) for dc in (0, 1)][1:]

    def body(v_ref, o_ref, buf, send_sems, recv_sems):
        x, y, c = _mesh_pos()

        def peer(f):
            return tuple(1 - p if fl else p for p, fl in zip((x, y, c), f))

        def index(p):
            return 4 * p[0] + 2 * p[1] + p[2]

        buf[index((x, y, c))] = v_ref[...]
        sends = []
        for k, f in enumerate(flips):
            cp = pltpu.make_async_remote_copy(src_ref=v_ref, dst_ref=buf.at[index((x, y, c))], send_sem=send_sems.at[k],
                                              recv_sem=recv_sems.at[k], device_id=peer(f), device_id_type=MESH)
            cp.start()
            sends.append(cp)
        for k, f in enumerate(flips):
            pltpu.make_async_remote_copy(src_ref=v_ref, dst_ref=buf.at[index(peer(f))], send_sem=send_sems.at[k],
                                         recv_sem=recv_sems.at[k], device_id=peer(f), device_id_type=MESH).wait_recv()
        for cp in sends:
            cp.wait_send()
        acc = buf[0]
        for d in range(1, N_DEV):
            acc = acc + buf[d]
        o_ref[...] = acc

    vmem = pl.BlockSpec(memory_space=pltpu.VMEM)
    return pl.pallas_call(
        body, name=name, in_specs=[vmem], out_specs=vmem, out_shape=jax.ShapeDtypeStruct((rows, cols), F32),
        scratch_shapes=[pltpu.VMEM((N_DEV, rows, cols), F32), pltpu.SemaphoreType.DMA((N_DEV - 1,)),
                        pltpu.SemaphoreType.DMA((N_DEV - 1,))],
    )(v)


def _adamw(w, g, m, v, *, name):
    def body(w, g, m, v):
        m = ADAM_B1 * m + (1.0 - ADAM_B1) * g
        v = ADAM_B2 * v + (1.0 - ADAM_B2) * (g * g)
        m_hat = m / (1.0 - ADAM_B1 ** ADAM_STEP)
        v_hat = v / (1.0 - ADAM_B2 ** ADAM_STEP)
        delta = -ADAM_LR * (m_hat / (jnp.sqrt(v_hat) + ADAM_EPS) + ADAM_WD * w)
        return (delta, m, v), ()
    rows, cols = w.shape
    if rows % SUBLANES == 0:
        return _rowwise(body, [w, g, m, v], [], [(cols, F32)] * 3, [], tm=_tile(rows, 256, SUBLANES), name=name)

    def kern(*refs):
        outs, _ = body(*[r[...] for r in refs[:4]])
        for r, val in zip(refs[4:], outs, strict=True):
            r[...] = val

    spec = pl.BlockSpec((rows, _tile(cols, 256, LANES)), lambda j: (0, j))
    return pl.pallas_call(
        kern, name=name, grid=(cols // spec.block_shape[1],), in_specs=[spec] * 4, out_specs=[spec] * 3,
        out_shape=[jax.ShapeDtypeStruct((rows, cols), F32)] * 3, compiler_params=_params("parallel"),
    )(w, g, m, v)


def _pack_small(arrs, rows):
    flat = jnp.concatenate([a.reshape(-1) for a in arrs])
    return jnp.pad(flat, (0, rows * LANES - flat.shape[0])).reshape(rows, LANES)


def _unpack_small(packed, shapes):
    flat = packed.reshape(-1)
    out, off = [], 0
    for s in shapes:
        size = math.prod(s)
        out.append(flat[off: off + size].reshape(s))
        off += size
    return out


def _small_rows(shapes):
    return -(-sum(math.prod(s) for s in shapes) // (SUBLANES * LANES)) * SUBLANES


def _sem_pairs(n):
    return [pltpu.SemaphoreType.DMA((n,)), pltpu.SemaphoreType.DMA((n,))]


def _gather_blocks(blocks, *, name):
    n = len(blocks)

    def body(*refs):
        x_refs, out_refs, (send_sems, recv_sems) = refs[:n], refs[n: 2 * n], refs[2 * n:]
        x, y, c = _mesh_pos()
        sibling = (x, y, 1 - c)
        chips = _other_chips(x, y)

        def copy(k, src, dst, to):
            return pltpu.make_async_remote_copy(src_ref=src, dst_ref=dst, send_sem=send_sems.at[k],
                                                recv_sem=recv_sems.at[k], device_id=to, device_id_type=MESH)

        def part(b, chip, h):
            half = blocks[b].shape[0] // 2
            return out_refs[b].at[2 * chip[0] + chip[1], pl.ds(h * half, half), :]

        def my_half(b):
            half = blocks[b].shape[0] // 2
            return x_refs[b].at[pl.ds(c * half, half), :]

        first = [copy(6 * b + j, my_half(b), part(b, (x, y), c), (*chip, c)) for b in range(n) for j, chip in enumerate(chips)]
        for cp in first:
            cp.start()
        passed = []
        for b in range(n):
            for j, chip in enumerate(chips):
                copy(6 * b + j, my_half(b), part(b, chip, c), (*chip, c)).wait_recv()
                cp = copy(6 * b + 3 + j, part(b, chip, c), part(b, chip, c), sibling)
                cp.start()
                passed.append(cp)
        for b in range(n):
            for j, chip in enumerate(chips):
                copy(6 * b + 3 + j, part(b, chip, 1 - c), part(b, chip, 1 - c), sibling).wait_recv()
        for cp in first + passed:
            cp.wait_send()

    return pl.pallas_call(
        body, name=name, in_specs=[ANY] * n, out_specs=[ANY] * n,
        out_shape=[jax.ShapeDtypeStruct((N_CHIPS, *a.shape), a.dtype) for a in blocks],
        scratch_shapes=_sem_pairs(6 * n),
    )(*blocks)


HBM = pl.BlockSpec(memory_space=pltpu.HBM)
SEM = pl.BlockSpec(memory_space=pltpu.SEMAPHORE)
DATAFLOW = pltpu.SideEffectType.DATAFLOW_SIDE_EFFECTING


def _gather_start(blocks, *, name):
    n = len(blocks)
    lands = [lax.empty((N_CHIPS, *a.shape), a.dtype) for a in blocks]

    def body(*refs):
        x_refs, land_refs, send_sems, recv_sems, token = refs[:n], refs[n: 2 * n], refs[2 * n], refs[2 * n + 1], refs[-1]
        x, y, c = _mesh_pos()
        for b in range(n):
            for j, chip in enumerate(_other_chips(x, y)):
                pltpu.make_async_remote_copy(src_ref=x_refs[b], dst_ref=land_refs[b].at[2 * x + y], send_sem=send_sems.at[3 * b + j],
                                             recv_sem=recv_sems.at[3 * b + j], device_id=(*chip, c), device_id_type=MESH).start()
        token[...] = jnp.zeros_like(token)

    operands = [pltpu.with_memory_space_constraint(a, pltpu.HBM) for a in blocks + lands]
    res = pl.pallas_call(
        body, name=name, in_specs=[HBM] * (2 * n),
        out_shape=(pltpu.SemaphoreType.DMA((3 * n,)), pltpu.SemaphoreType.DMA((3 * n,)),
                   *[pltpu.HBM(a.shape, a.dtype) for a in operands], jax.ShapeDtypeStruct((SUBLANES, LANES), F32)),
        out_specs=(SEM, SEM, *[HBM] * (2 * n), pl.BlockSpec(memory_space=pltpu.VMEM)),
        input_output_aliases={i: 2 + i for i in range(2 * n)},
        compiler_params=pltpu.CompilerParams(has_side_effects=DATAFLOW),
    )(*operands)
    return res[0], res[1], list(res[2: 2 + n]), list(res[2 + n: 2 + 2 * n]), res[-1]


def _gather_wait(started, after, *, name):
    send_sems, recv_sems, blocks, lands, _ = started
    n = len(blocks)

    def body(*refs):
        x_refs, land_refs, send_sems, recv_sems = refs[:n], refs[n: 2 * n], refs[2 * n], refs[2 * n + 1]
        x, y, c = _mesh_pos()
        for b in range(n):
            for j, chip in enumerate(_other_chips(x, y)):
                cp = pltpu.make_async_remote_copy(src_ref=x_refs[b], dst_ref=land_refs[b].at[2 * chip[0] + chip[1]],
                                                  send_sem=send_sems.at[3 * b + j], recv_sem=recv_sems.at[3 * b + j],
                                                  device_id=(*chip, c), device_id_type=MESH)
                cp.wait_send()
                cp.wait_recv()

    res = pl.pallas_call(
        body, name=name, in_specs=(*[HBM] * (2 * n), SEM, SEM, ANY),
        out_shape=tuple(pltpu.HBM(a.shape, a.dtype) for a in blocks + lands), out_specs=tuple([HBM] * (2 * n)),
        input_output_aliases={i: i for i in range(2 * n)},
        compiler_params=pltpu.CompilerParams(has_side_effects=DATAFLOW),
    )(*blocks, *lands, send_sems, recv_sems, after)
    return list(res[n:])


def _swap_with_sibling(blocks, *, name):
    n = len(blocks)

    def body(*refs):
        in_refs, out_refs, (send_sems, recv_sems) = refs[:n], refs[n: 2 * n], refs[2 * n:]
        x, y, c = _mesh_pos()
        copies = [pltpu.make_async_remote_copy(src_ref=in_refs[b], dst_ref=out_refs[b], send_sem=send_sems.at[b],
                                               recv_sem=recv_sems.at[b], device_id=(x, y, 1 - c), device_id_type=MESH)
                  for b in range(n)]
        for cp in copies:
            cp.start()
        for cp in copies:
            cp.wait()

    return pl.pallas_call(
        body, name=name, in_specs=[ANY] * n, out_specs=[ANY] * n,
        out_shape=[jax.ShapeDtypeStruct(a.shape, a.dtype) for a in blocks], scratch_shapes=_sem_pairs(n),
    )(*blocks)


def _sum_chips_block(parts, *, name):
    n, half, cols = parts.shape
    tm = _tile(half, 512, 2 * SUBLANES)

    def kern(p_ref, o_ref):
        acc = p_ref[0].astype(F32)
        for s in range(1, n):
            acc = acc + p_ref[s].astype(F32)
        o_ref[...] = acc

    return pl.pallas_call(
        kern, name=name, grid=(half // tm,),
        in_specs=[pl.BlockSpec((n, tm, cols), lambda i: (0, i, 0))],
        out_specs=pl.BlockSpec((tm, cols), lambda i: (i, 0)),
        out_shape=jax.ShapeDtypeStruct((half, cols), F32),
        compiler_params=_params("parallel"),
    )(parts)


PEER_FLIPS = [(dx, dy, dc) for dx in (0, 1) for dy in (0, 1) for dc in (0, 1)][1:]


def _flipped(pos, flip):
    return tuple(1 - p if f else p for p, f in zip(pos, flip))


def _device_index(pos):
    return 4 * pos[0] + 2 * pos[1] + pos[2]


def _scatter_start(blocks, *, name):
    n = len(blocks)
    lands = [lax.empty((N_DEV, a.shape[1] // 2, a.shape[2]), a.dtype) for a in blocks]

    def body(*refs):
        g_refs, land_refs, send_sems, recv_sems, token = refs[:n], refs[n: 2 * n], refs[2 * n], refs[2 * n + 1], refs[-1]
        pos = _mesh_pos()
        for b in range(n):
            half = blocks[b].shape[1] // 2
            for k, flip in enumerate(PEER_FLIPS):
                peer = _flipped(pos, flip)
                pltpu.make_async_remote_copy(src_ref=g_refs[b].at[2 * peer[0] + peer[1], pl.ds(peer[2] * half, half), :],
                                             dst_ref=land_refs[b].at[_device_index(pos)],
                                             send_sem=send_sems.at[7 * b + k], recv_sem=recv_sems.at[7 * b + k],
                                             device_id=peer, device_id_type=MESH).start()
        token[...] = jnp.zeros_like(token)

    operands = [pltpu.with_memory_space_constraint(a, pltpu.HBM) for a in blocks + lands]
    res = pl.pallas_call(
        body, name=name, in_specs=[HBM] * (2 * n),
        out_shape=(pltpu.SemaphoreType.DMA((7 * n,)), pltpu.SemaphoreType.DMA((7 * n,)),
                   *[pltpu.HBM(a.shape, a.dtype) for a in operands], jax.ShapeDtypeStruct((SUBLANES, LANES), F32)),
        out_specs=(SEM, SEM, *[HBM] * (2 * n), pl.BlockSpec(memory_space=pltpu.VMEM)),
        input_output_aliases={i: 2 + i for i in range(2 * n)},
        compiler_params=pltpu.CompilerParams(has_side_effects=DATAFLOW),
    )(*operands)
    return res[0], res[1], list(res[2: 2 + n]), list(res[2 + n: 2 + 2 * n]), res[-1]


def _scatter_wait(started, after, *, name):
    send_sems, recv_sems, blocks, lands, _ = started
    n = len(blocks)

    def body(*refs):
        g_refs, land_refs, send_sems, recv_sems = refs[:n], refs[n: 2 * n], refs[2 * n], refs[2 * n + 1]
        pos = _mesh_pos()
        for b in range(n):
            half = blocks[b].shape[1] // 2
            for k, flip in enumerate(PEER_FLIPS):
                peer = _flipped(pos, flip)
                cp = pltpu.make_async_remote_copy(src_ref=g_refs[b].at[0, pl.ds(0, half), :],
                                                  dst_ref=land_refs[b].at[_device_index(peer)],
                                                  send_sem=send_sems.at[7 * b + k], recv_sem=recv_sems.at[7 * b + k],
                                                  device_id=peer, device_id_type=MESH)
                cp.wait_send()
                cp.wait_recv()

    res = pl.pallas_call(
        body, name=name, in_specs=(*[HBM] * (2 * n), SEM, SEM, ANY),
        out_shape=tuple(pltpu.HBM(a.shape, a.dtype) for a in blocks + lands), out_specs=tuple([HBM] * (2 * n)),
        input_output_aliases={i: i for i in range(2 * n)},
        compiler_params=pltpu.CompilerParams(has_side_effects=DATAFLOW),
    )(*blocks, *lands, send_sems, recv_sems, after)
    return list(res[:n]), list(res[n:])


def _reduce_finish(begun, names, after):
    x, y, c = _mesh_pos()
    mine = {}
    for key, started in begun.items():
        blocks, lands = _scatter_wait(started, after, name=f"rs_scatter_wait_{key}")
        parts = []
        for blk, land in zip(blocks, lands, strict=True):
            half = blk.shape[1] // 2
            own = lax.dynamic_slice(blk, (2 * x + y, c * half, 0), (1, half, blk.shape[2]))
            parts.append(lax.dynamic_update_slice(land, own, (_device_index((x, y, c)), 0, 0)))
        mine[key] = [_sum_chips_block(p, name=f"rs_sum_{nm}") for p, nm in zip(parts, names[key], strict=True)]
    flat = [a for key in begun for a in mine[key]]
    other = iter(_swap_with_sibling(flat, name="rs_join"))
    return {key: [jnp.concatenate([jnp.where(c == 0, a, b), jnp.where(c == 0, b, a)], axis=0)
                  for a, b in ((a, next(other)) for a in mine[key])] for key in begun}


WEIGHTS = ("rel_bias", "att_w_in", "att_w_out", "dn_w_in", "dn_conv", "dn_a_log", "dn_dt_bias", "dn_out_norm", "dn_w_out",
           "mem_norm", "mem_w_kv", "norm_mix_pre", "norm_mix_post", "norm_ffn_pre", "norm_ffn_post", "ffn_w_gate_up",
           "ffn_w_down")
BIG_NAMES = tuple(n for n, _, _ in BIG)
SMALL_NAMES = tuple(n for n in WEIGHTS if n not in BIG_NAMES)
CONV_COLS = 3 * TOK_WIDTH
CONV_SHARD = CONV_COLS // N_CHIPS
BLOCKS = tuple((n, layer) for n, shape, _ in BIG for layer in range(shape[0]))
COLUMN_SHARDED = {n: axis == 2 for n, _, axis in BIG}


def kernel(x, mem, rel_bias, att_w_in, att_w_out, dn_w_in, dn_conv, dn_a_log, dn_dt_bias, dn_out_norm, dn_w_out, mem_norm, mem_w_kv, norm_mix_pre, norm_mix_post, norm_ffn_pre, norm_ffn_post, ffn_w_gate_up, ffn_w_down, loss_target, m_rel_bias, m_att_w_in, m_att_w_out, m_dn_w_in, m_dn_conv, m_dn_a_log, m_dn_dt_bias, m_dn_out_norm, m_dn_w_out, m_mem_norm, m_mem_w_kv, m_norm_mix_pre, m_norm_mix_post, m_norm_ffn_pre, m_norm_ffn_post, m_ffn_w_gate_up, m_ffn_w_down, v_rel_bias, v_att_w_in, v_att_w_out, v_dn_w_in, v_dn_conv, v_dn_a_log, v_dn_dt_bias, v_dn_out_norm, v_dn_w_out, v_mem_norm, v_mem_w_kv, v_norm_mix_pre, v_norm_mix_post, v_norm_ffn_pre, v_norm_ffn_post, v_ffn_w_gate_up, v_ffn_w_down):
    w = dict(zip(WEIGHTS, (rel_bias, att_w_in, att_w_out, dn_w_in, dn_conv, dn_a_log, dn_dt_bias, dn_out_norm, dn_w_out,
                           mem_norm, mem_w_kv, norm_mix_pre, norm_mix_post, norm_ffn_pre, norm_ffn_post, ffn_w_gate_up,
                           ffn_w_down)))
    m = dict(zip(WEIGHTS, (m_rel_bias, m_att_w_in, m_att_w_out, m_dn_w_in, m_dn_conv, m_dn_a_log, m_dn_dt_bias,
                           m_dn_out_norm, m_dn_w_out, m_mem_norm, m_mem_w_kv, m_norm_mix_pre, m_norm_mix_post,
                           m_norm_ffn_pre, m_norm_ffn_post, m_ffn_w_gate_up, m_ffn_w_down)))
    v = dict(zip(WEIGHTS, (v_rel_bias, v_att_w_in, v_att_w_out, v_dn_w_in, v_dn_conv, v_dn_a_log, v_dn_dt_bias,
                           v_dn_out_norm, v_dn_w_out, v_mem_norm, v_mem_w_kv, v_norm_mix_pre, v_norm_mix_post,
                           v_norm_ffn_pre, v_norm_ffn_post, v_ffn_w_gate_up, v_ffn_w_down)))
    cx, cy, cc = _mesh_pos()
    chip = 2 * cx + cy

    local = dict(zip(BLOCKS, lax.optimization_barrier(
        [(w[n][layer].T if n == "dn_w_in" else w[n][layer]).astype(MXU_DTYPE) for n, layer in BLOCKS]), strict=True))

    def usable(block, got):
        got = lax.dynamic_update_slice(got, local[block][None], (chip, 0, 0))
        return got if COLUMN_SHARDED[block[0]] else got.reshape(-1, got.shape[-1])

    late = {"att_rest": [("att_w_out", 0), ("mem_w_kv", 0), ("mem_w_kv", 1)],
            "ffn0": [("ffn_w_gate_up", 0), ("ffn_w_down", 0)], "dn": [("dn_w_in", 0), ("dn_w_out", 0)],
            "ffn1": [("ffn_w_gate_up", 1), ("ffn_w_down", 1)]}
    first = [b for b in BLOCKS if all(b not in blks for blks in late.values())]
    first_got = _gather_blocks([local[b] for b in first], name="gather_weights")
    late_local, _ = lax.optimization_barrier(({k: [local[b] for b in blks] for k, blks in late.items()}, first_got[0]))
    started = {k: _gather_start(late_local[k], name=f"gather_start_{k}") for k in late}
    started_token = sum(s[4][0, 0] for s in started.values())

    def late_weights(key, after):
        lands = _gather_wait(started[key], after, name=f"gather_wait_{key}")
        return [usable(b, got) for b, got in zip(late[key], lands, strict=True)]

    def dn_weights(after):
        w_in, w_out = late_weights("dn", after)
        return _pad_dn_w_in(jnp.concatenate([w_in[s].T for s in range(N_CHIPS)], axis=1)), w_out

    full = {}
    for b, got in zip(first, first_got, strict=True):
        full.setdefault(b[0], []).append(usable(b, got))
    conv_rows = _small_rows([(DN_CONV, CONV_COLS)])
    conv_mine = jnp.where(cc == 0, 1.0, 0.0) * w["dn_conv"][0]
    conv_placed = lax.dynamic_update_slice(jnp.zeros((DN_CONV, CONV_COLS), F32), conv_mine, (0, chip * CONV_SHARD))
    conv_full = _unpack_small(_all_reduce_small(_pack_small([conv_placed], conv_rows), name="gather_conv"),
                              [(DN_CONV, CONV_COLS)])[0]
    p = {
        "rel_bias": w["rel_bias"], "att_w_in": full["att_w_in"][0], "att_rest_weights": lambda after: late_weights("att_rest", after),
        "dn_conv": conv_full, "dn_a_log": w["dn_a_log"][0], "dn_dt_bias": w["dn_dt_bias"][0],
        "dn_out_norm": w["dn_out_norm"][0], "mem_norm": w["mem_norm"],
        "norm_mix_pre": w["norm_mix_pre"] + started_token,
        "norm_mix_post": w["norm_mix_post"], "norm_ffn_pre": w["norm_ffn_pre"], "norm_ffn_post": w["norm_ffn_post"],
        "ffn_weights": lambda layer, after: late_weights(f"ffn{layer}", after), "dn_weights": dn_weights,
    }

    def chip_blocks(n, a):
        if n == "dn_w_in":
            a = _unpad_dn_w_in(a)
        if a.ndim == 3:
            return a
        if COLUMN_SHARDED[n]:
            return a.reshape(a.shape[0], N_CHIPS, -1).transpose(1, 0, 2)
        return a.reshape(N_CHIPS, -1, a.shape[-1])

    begun, begun_blocks = {}, {}

    def grads_ready(key, layer_grads):
        begun_blocks[key] = list(layer_grads)
        begun[key] = _scatter_start([chip_blocks(n, a) for (n, _), a in layer_grads.items()], name=f"rs_scatter_start_{key}")
        return begun[key][4][0, 0]

    p["grads_ready"] = grads_ready
    loss_cols, grad_x, g = _local_step(x[0], mem[0], loss_target[0], p)
    loss = lax.psum(jnp.sum(loss_cols), ("x", "y", "c"))
    finished = _reduce_finish(begun, {k: [f"{n}{layer}" for n, layer in blks] for k, blks in begun_blocks.items()}, grad_x)
    reduced = {b: r for k in begun for b, r in zip(begun_blocks[k], finished[k], strict=True)}
    grads = {n: jnp.concatenate([reduced[b] for b in BLOCKS if b[0] == n], axis=0).reshape(shape) for n, shape, _ in BIG}
    small_full_shapes = [(DN_CONV, CONV_COLS) if n == "dn_conv" else w[n].shape for n in SMALL_NAMES]
    small_sum = _all_reduce_small(_pack_small([g[n] for n in SMALL_NAMES], _small_rows(small_full_shapes)), name="reduce_small")
    for n, s in zip(SMALL_NAMES, _unpack_small(small_sum, small_full_shapes)):
        grads[n] = lax.dynamic_slice(s, (0, chip * CONV_SHARD), (DN_CONV, CONV_SHARD))[None] if n == "dn_conv" else s

    delta, new_m, new_v = {}, {}, {}
    for n in BIG_NAMES:
        shape = w[n].shape
        two_d = (lambda a: a[0].T) if n == "dn_w_in" else (lambda a: a.reshape(-1, shape[-1]))
        back = (lambda a: a.T[None]) if n == "dn_w_in" else (lambda a: a.reshape(shape))
        g_2d = two_d(grads[n])
        res = _adamw(two_d(w[n]), g_2d, two_d(m[n]), two_d(v[n]), name=f"adamw_{n}")
        grads[n], delta[n], new_m[n], new_v[n] = (back(r) for r in (g_2d, *res))
    small_shapes = [w[n].shape for n in SMALL_NAMES]
    rows = _small_rows(small_shapes)
    res = _adamw(*[_pack_small([d[n] for n in SMALL_NAMES], rows) for d in (w, grads, m, v)], name="adamw_small")
    for d, r in zip((delta, new_m, new_v), res):
        for n, a in zip(SMALL_NAMES, _unpack_small(r, small_shapes)):
            d[n] = a
    return (loss, grad_x[None], *[grads[n] for n in WEIGHTS], *[delta[n] for n in WEIGHTS],
            *[new_m[n] for n in WEIGHTS], *[new_v[n] for n in WEIGHTS])
```

```python
import functools
import math

import numpy as np
import jax
import jax.numpy as jnp
from jax import lax
from jax.experimental import pallas as pl
from jax.experimental.pallas import tpu as pltpu

F32 = jnp.float32
MXU_DTYPE = jnp.bfloat16
LINK_DTYPE = jnp.bfloat16
HI = lax.Precision.HIGHEST

EPS = 1e-6
NEG_INF = -1e30
LANES = 128
SUBLANES = 8
VMEM_LIMIT = 56 * 1024 * 1024
MM_WHOLE_K_BUDGET = 44 * 1024 * 1024

D_MODEL = 1024
TOK_WIDTH = 768
MEM_WIDTH = 256
MEM_LEN = 256
ATT_HEAD_DIM = 64
DILATIONS = (1, 4, 16)
HALF = 64
ATT_BQ = 128
ATT_W = ATT_BQ + 2 * HALF
REL_BUCKETS = 32
REL_MAX_DIST = 1024
DN_HEADS = 6
DN_HEAD_DIM = 128
DN_CONV = 5
DN_CHUNK = 128
D_FF = 2816
DN_IN = 3352
DN_IN_PAD = 3456
N_GATES = 4 * DN_HEADS

ADAM_LR = 0.001
ADAM_B1 = 0.9
ADAM_B2 = 0.999
ADAM_EPS = 1e-08
ADAM_WD = 0.01
ADAM_STEP = 10


def _tile(n, target, align):
    if n <= target:
        return n
    t = (target // align) * align
    while t >= align:
        if n % t == 0:
            return t
        t -= align
    raise ValueError(f"no tile for {n} (target {target}, align {align})")


def _params(*sem):
    return pltpu.CompilerParams(dimension_semantics=sem, vmem_limit_bytes=VMEM_LIMIT)


def _mm(a, b, *, name, ta=False, tb=False, out_shards=None, tm=1408, tn=1408, tk=1408, out_dtype=F32):
    if ta:
        K, M = a.shape
    else:
        M, K = a.shape
    sharded_b = b.ndim == 3
    if sharded_b:
        n_sh, b_rows, b_cols = b.shape
        N, K2 = (b_rows, n_sh * b_cols) if tb else (n_sh * b_cols, b_rows)
    else:
        N, K2 = b.shape if tb else b.shape[::-1]
    assert K == K2, (a.shape, b.shape, ta, tb)
    tm = _tile(M, tm, LANES if ta else SUBLANES)
    tn = N // out_shards if out_shards else (b_cols if sharded_b and not tb else _tile(N, tn, LANES))
    tk = b_cols if sharded_b and tb else _tile(K, tk, LANES)

    def vmem_bytes(tm_, tk_):
        return 2 * (tm_ * tk_ * a.dtype.itemsize + tk_ * tn * b.dtype.itemsize + tm_ * tn * jnp.dtype(out_dtype).itemsize)

    for rows in (tm, tm // 2):
        if not (sharded_b and tb) and M % rows == 0 and rows % LANES == 0 and vmem_bytes(rows, K) <= MM_WHOLE_K_BUDGET:
            tm, tk = rows, K
            break
    nk = K // tk
    a_spec = pl.BlockSpec((tk, tm), lambda i, j, k: (k, i)) if ta else pl.BlockSpec((tm, tk), lambda i, j, k: (i, k))
    if sharded_b:
        b_spec = (pl.BlockSpec((None, tn, tk), lambda i, j, k: (k, j, 0)) if tb
                  else pl.BlockSpec((None, tk, tn), lambda i, j, k: (j, k, 0)))
    else:
        b_spec = pl.BlockSpec((tn, tk), lambda i, j, k: (j, k)) if tb else pl.BlockSpec((tk, tn), lambda i, j, k: (k, j))
    if out_shards:
        out_spec = pl.BlockSpec((None, tm, tn), lambda i, j, k: (j, i, 0))
        out_shape = jax.ShapeDtypeStruct((out_shards, M, tn), out_dtype)
    else:
        out_spec = pl.BlockSpec((tm, tn), lambda i, j, k: (i, j))
        out_shape = jax.ShapeDtypeStruct((M, N), out_dtype)
    dims = (((0 if ta else 1,), (1 if tb else 0,)), ((), ()))

    def product(a_ref, b_ref):
        return lax.dot_general(a_ref[...].astype(MXU_DTYPE), b_ref[...].astype(MXU_DTYPE), dims, preferred_element_type=F32)

    def kern_whole(a_ref, b_ref, o_ref):
        o_ref[...] = product(a_ref, b_ref).astype(o_ref.dtype)

    def kern_steps(a_ref, b_ref, o_ref, acc_ref):
        k = pl.program_id(2)

        @pl.when(k == 0)
        def _():
            acc_ref[...] = jnp.zeros_like(acc_ref)

        acc_ref[...] += product(a_ref, b_ref)

        @pl.when(k == nk - 1)
        def _():
            o_ref[...] = acc_ref[...].astype(o_ref.dtype)

    return pl.pallas_call(
        kern_whole if nk == 1 else kern_steps, name=name, grid=(M // tm, N // tn, nk), in_specs=[a_spec, b_spec],
        out_specs=out_spec, out_shape=out_shape,
        scratch_shapes=[] if nk == 1 else [pltpu.VMEM((tm, tn), F32)],
        compiler_params=_params("parallel", "parallel", "arbitrary"),
    )(a, b)


def _shards_of(w):
    return w.shape[0] if w.ndim == 3 else None


def _col(arr, width, blk):
    return (arr, width, blk)


def _rowwise(body, rows, consts, out_rows, out_acc, *, tm, name):
    n_rows = (rows[0][0] if isinstance(rows[0], tuple) else rows[0]).shape[0]
    assert n_rows % tm == 0, (n_rows, tm)
    arrs, in_specs = [], []
    for r in rows:
        arr, width, blk = r if isinstance(r, tuple) else (r, r.shape[1], 0)
        assert arr.shape[0] == n_rows
        arrs.append(arr)
        in_specs.append(pl.BlockSpec((tm, width), functools.partial(lambda i, b: (i, b), b=blk)))
    for c in consts:
        arrs.append(c)
        in_specs.append(pl.BlockSpec(c.shape, functools.partial(lambda i, n: (0,) * n, n=c.ndim)))
    n_in, n_ro = len(arrs), len(out_rows)
    out_shape = [jax.ShapeDtypeStruct((n_rows, w), dt) for w, dt in out_rows]
    out_specs = [pl.BlockSpec((tm, w), lambda i: (i, 0)) for w, _ in out_rows]
    out_shape += [jax.ShapeDtypeStruct(s, F32) for s in out_acc]
    out_specs += [pl.BlockSpec(s, lambda i: (0, 0)) for s in out_acc]

    def kern(*refs):
        ro, ao = body(*[r[...] for r in refs[:n_in]])
        outs = refs[n_in:]
        for r, v in zip(outs[:n_ro], ro, strict=True):
            r[...] = v.astype(r.dtype)
        if out_acc:
            @pl.when(pl.program_id(0) == 0)
            def _():
                for r in outs[n_ro:]:
                    r[...] = jnp.zeros_like(r)

            for r, v in zip(outs[n_ro:], ao, strict=True):
                r[...] += v

    res = pl.pallas_call(
        kern, name=name, grid=(n_rows // tm,), in_specs=in_specs, out_specs=out_specs, out_shape=out_shape,
        compiler_params=_params("arbitrary" if out_acc else "parallel"),
    )(*arrs)
    return res


def _rms(x, gain):
    return x * lax.rsqrt(jnp.mean(x * x, axis=-1, keepdims=True) + EPS) * gain


def _silu(x):
    return x * jax.nn.sigmoid(x)


def _softplus(x):
    return jnp.maximum(x, 0.0) + jnp.log(1.0 + jnp.exp(-jnp.abs(x)))


def _dot_nt(a, b, precision=None):
    return lax.dot_general(a, b, (((1,), (1,)), ((), ())), preferred_element_type=F32, precision=precision)


def _dot_tn(a, b, precision=None):
    return lax.dot_general(a, b, (((0,), (0,)), ((), ())), preferred_element_type=F32, precision=precision)


def _dot(a, b, precision=None):
    return jnp.dot(a, b, preferred_element_type=F32, precision=precision)


def _pre_norm(x, gain, *, name):
    def body(x, g):
        return (_rms(x, g),), ()
    return _rowwise(body, [x], [gain], [(x.shape[1], MXU_DTYPE)], [], tm=_tile(x.shape[0], 512, 2 * SUBLANES), name=name)[0]


def _pre_norm_bwd(x, gain, dh, dx_other, *, name):
    def body(x, dh, dxo, g):
        _, vjp = jax.vjp(_rms, x, g)
        dx, dg = vjp(dh)
        return (dx + dxo,), (dg,)
    return _rowwise(body, [x, dh, dx_other], [gain], [(x.shape[1], F32)], [gain.shape], tm=512, name=name)


def _gain_bwd(x, gain, dh, *, name):
    def body(x, dh, g):
        _, vjp = jax.vjp(lambda g_: _rms(x, g_), g)
        return (), (vjp(dh)[0],)
    return _rowwise(body, [x, dh], [gain], [], [gain.shape], tm=_tile(x.shape[0], 512, SUBLANES), name=name)[0]


def _res_block(x_res, m, g_post, g_pre):
    x_new = x_res + _rms(m, g_post)
    return x_new, _rms(x_new, g_pre)


def _post_pre(x_res, m, g_post, g_pre, *, name):
    def body(x, m, gp, gq):
        return _res_block(x, m, gp, gq), ()
    d = x_res.shape[1]
    return _rowwise(body, [x_res, m], [g_post, g_pre], [(d, F32), (d, MXU_DTYPE)], [], tm=512, name=name)


def _post_pre_bwd(x_res, m, g_post, g_pre, dx_new, dh, *, name):
    def body(x, m, dxn, dh, gp, gq):
        _, vjp = jax.vjp(_res_block, x, m, gp, gq)
        dx, dm, dgp, dgq = vjp((dxn, dh))
        return (dx, dm), (dgp, dgq)
    d = x_res.shape[1]
    return _rowwise(body, [x_res, m, dx_new, dh], [g_post, g_pre], [(d, F32), (d, MXU_DTYPE)],
                    [g_post.shape, g_pre.shape], tm=256, name=name)


def _final_loss_bwd(x_res, m, g_post, target, *, name):
    d = x_res.shape[1]

    def loss_cols(x, m, g, t):
        err = x + _rms(m, g) - t
        return jnp.sum(err * err, axis=0, keepdims=True) * (0.5 / d)

    def body(x, m, t, g):
        cols, vjp = jax.vjp(lambda x_, m_, g_: loss_cols(x_, m_, g_, t), x, m, g)
        dx, dm, dg = vjp(jnp.ones_like(cols))
        return (dx, dm), (dg, cols)
    return _rowwise(body, [x_res, m, target], [g_post], [(d, F32), (d, MXU_DTYPE)], [g_post.shape, (1, d)], tm=256, name=name)


def _swiglu_act(gu, *, name):
    def body(gate, up):
        return (_silu(gate.astype(F32)) * up.astype(F32),), ()
    return _rowwise(body, [_col(gu, D_FF, 0), _col(gu, D_FF, 1)], [], [(D_FF, MXU_DTYPE)], [], tm=256, name=name)[0]


def _swiglu_act_bwd(gu, da, *, name):
    def body(gate, up, da):
        _, vjp = jax.vjp(lambda g, u: _silu(g) * u, gate.astype(F32), up.astype(F32))
        dg, du = vjp(da.astype(F32))
        return (jnp.concatenate([dg, du], axis=1),), ()
    return _rowwise(body, [_col(gu, D_FF, 0), _col(gu, D_FF, 1), da], [], [(2 * D_FF, MXU_DTYPE)], [], tm=256, name=name)[0]


def _lane_head_mask(width, head_dim, head):
    lane = lax.broadcasted_iota(jnp.int32, (1, width), 1)
    return (lane // head_dim) == head


def _mem_attn_heads(q4, k4, v4):
    logits = _bdot_nt(q4, k4)
    p = jnp.exp(logits - jnp.max(logits, axis=-1, keepdims=True))
    return _bdot(p / jnp.sum(p, axis=-1, keepdims=True), v4)


def _mem_heads(q_mem, kv):
    return _heads(q_mem * (ATT_HEAD_DIM ** -0.5), mask=True), _heads(kv[:, :MEM_WIDTH]), _heads(kv[:, MEM_WIDTH:])


def _mem_attn(q_mem, kv):
    return _join_heads(_mem_attn_heads(*_mem_heads(q_mem, kv)))


def _mem_attn_bwd(q_mem, kv, do):
    _, vjp = jax.vjp(_mem_attn_heads, *_mem_heads(q_mem, kv))
    dq4, dk4, dv4 = vjp(_heads(do, mask=True))
    return (_join_heads(dq4) * (ATT_HEAD_DIM ** -0.5),
            jnp.concatenate([dk4[0] + dk4[1], dk4[2] + dk4[3], dv4[0] + dv4[1], dv4[2] + dv4[3]], axis=1))


def _t5_bucket(rel):
    half = REL_BUCKETS // 2
    max_exact = half // 2
    n = np.abs(rel)
    large = max_exact + (np.log(np.maximum(n, 1) / max_exact) / math.log(REL_MAX_DIST / max_exact)
                         * (half - max_exact)).astype(np.int64)
    large = np.minimum(large, half - 1)
    return ((rel > 0) * half + np.where(n < max_exact, n, large)).astype(np.int32)


ATT_DIAGS = ATT_BQ + ATT_W - 1


def _bias_diag_onehot(dil):
    j = np.arange(ATT_DIAGS)
    tiles = []
    for off in (-HALF, 0, HALF):
        rel = j - (ATT_BQ - 1) - HALF - off
        hot = _t5_bucket(rel * dil)[:, None] == np.arange(REL_BUCKETS)[None, :]
        tiles.append(hot & (np.abs(rel) <= HALF)[:, None])
    return np.stack(tiles).astype(np.float32)


def _toeplitz(r):
    lead = r.shape[:-1]
    a = jnp.broadcast_to(r[..., None, :], lead + (ATT_BQ, ATT_DIAGS))
    a = jnp.pad(a, [(0, 0)] * len(lead) + [(0, 0), (0, 1)])
    a = a.reshape(lead + (ATT_BQ * (ATT_DIAGS + 1),))[..., : ATT_BQ * ATT_DIAGS].reshape(lead + (ATT_BQ, ATT_DIAGS))
    return a[..., ATT_BQ - 1: ATT_BQ - 1 + ATT_W]


def _bias_tiles(rel_bias, gi):
    heads = rel_bias[:, 4 * gi: 4 * gi + 4]
    diag = jnp.einsum('tnb,bh->thn', jnp.asarray(_bias_diag_onehot(DILATIONS[gi])), heads, precision=HI)
    return _toeplitz(diag)


def _bias_tiles_bwd(rel_bias, dtiles, gi):
    return jax.vjp(lambda rb: _bias_tiles(rb, gi), rel_bias)[1](dtiles)[0]


def _att_window(i, n_sub):
    start = jnp.clip(i * ATT_BQ - HALF, 0, n_sub - ATT_W)
    off = i * ATT_BQ - HALF - start
    return pl.multiple_of(start, HALF), off


def _att_valid(off):
    q = lax.broadcasted_iota(jnp.int32, (ATT_BQ, ATT_W), 0)
    kk = lax.broadcasted_iota(jnp.int32, (ATT_BQ, ATT_W), 1)
    return jnp.abs(kk - q - HALF - off) <= HALF


def _att_tile_id(i, nq):
    return jnp.where(i == 0, 0, jnp.where(i == nq - 1, 2, 1))


ATT_GROUP_HEADS = 4


def _heads(x, mask=False):
    out = []
    for p in range(2):
        pair = x[:, p * LANES: (p + 1) * LANES]
        for h in range(2):
            out.append(jnp.where(_lane_head_mask(LANES, ATT_HEAD_DIM, h), pair, 0.0) if mask else pair)
    return jnp.stack(out)


def _join_heads(x):
    first = _lane_head_mask(LANES, ATT_HEAD_DIM, 0)
    return jnp.concatenate([jnp.where(first, x[2 * p], x[2 * p + 1]) for p in range(2)], axis=1)


def _head_scalar(x):
    out = []
    for p in range(2):
        pair = x[:, p * LANES: (p + 1) * LANES]
        for h in range(2):
            out.append(jnp.max(jnp.where(_lane_head_mask(LANES, ATT_HEAD_DIM, h), pair, NEG_INF), axis=-1, keepdims=True))
    return jnp.stack(out)


def _bdot(a, b):
    return jnp.einsum('hqk,hkd->hqd', a, b, preferred_element_type=F32)


def _bdot_nt(a, b):
    return jnp.einsum('hqd,hkd->hqk', a, b, preferred_element_type=F32)


def _bdot_tn(a, b):
    return jnp.einsum('hqk,hqd->hkd', a, b, preferred_element_type=F32)


def _residue_view(arr, col_blocks, dil):
    if dil == 1:
        return arr, lambda r, j: col_blocks[j]
    width = 2 * LANES
    picked = jnp.concatenate([arr[:, b * width: (b + 1) * width] for b in col_blocks], axis=1)
    n = len(col_blocks)
    return picked.reshape(arr.shape[0] // dil, dil * n * width), lambda r, j: r * n + j


def _att_fwd(qkvm, bias, gi, *, name):
    dil = DILATIONS[gi]
    s_len = qkvm.shape[0]
    n_sub = s_len // dil
    nq = n_sub // ATT_BQ
    assert n_sub % ATT_BQ == 0 and n_sub >= ATT_W
    view, at = _residue_view(qkvm, [gi, 3 + gi, 6 + gi], dil)

    def kern(q_ref, k_ref, v_ref, b_ref, o_ref, lse_ref):
        start, off = _att_window(pl.program_id(1), n_sub)
        valid = _att_valid(off)
        q4 = _heads(q_ref[...].astype(F32) * (ATT_HEAD_DIM ** -0.5), mask=True)
        k4 = _heads(k_ref[pl.ds(start, ATT_W), :].astype(F32))
        v4 = _heads(v_ref[pl.ds(start, ATT_W), :].astype(F32))
        s = jnp.where(valid, _bdot_nt(q4, k4) + b_ref[...], NEG_INF)
        mx = jnp.max(s, axis=-1, keepdims=True)
        p = jnp.exp(s - mx)
        den = jnp.sum(p, axis=-1, keepdims=True)
        o_ref[...] = _join_heads(_bdot(p, v4) / den)
        lse_ref[...] = _join_heads(jnp.broadcast_to(mx + jnp.log(den), (ATT_GROUP_HEADS, ATT_BQ, LANES)))

    def qkv_spec(which, full):
        shape = (n_sub, 2 * LANES) if full else (ATT_BQ, 2 * LANES)
        return pl.BlockSpec(shape, lambda r, i: (0 if full else i, at(r, which)))

    out_spec = pl.BlockSpec((ATT_BQ, 2 * LANES), lambda r, i: (i, r))
    o, lse = pl.pallas_call(
        kern, name=name, grid=(dil, nq),
        in_specs=[qkv_spec(0, False), qkv_spec(1, True), qkv_spec(2, True),
                  pl.BlockSpec((None, ATT_GROUP_HEADS, ATT_BQ, ATT_W), lambda r, i: (_att_tile_id(i, nq), 0, 0, 0))],
        out_specs=[out_spec, out_spec],
        out_shape=[jax.ShapeDtypeStruct((n_sub, dil * 2 * LANES), F32)] * 2,
        compiler_params=_params("parallel", "arbitrary"),
    )(view, view, view, bias)
    return o.reshape(s_len, 2 * LANES), lse.reshape(s_len, 2 * LANES)


def _att_bwd(qkvm, bias, lse_tot, delta, dcat, gi, *, name):
    dil = DILATIONS[gi]
    s_len = qkvm.shape[0]
    n_sub = s_len // dil
    nq = n_sub // ATT_BQ
    view, at = _residue_view(qkvm, [gi, 3 + gi, 6 + gi], dil)
    lse_v = lse_tot.reshape(n_sub, dil * 2 * LANES)
    delta_v = delta.reshape(n_sub, dil * 2 * LANES)
    dcat_v, dcat_at = _residue_view(dcat, [gi], dil)

    def kern(q_ref, k_ref, v_ref, b_ref, lse_ref, dl_ref, dm_ref, dq_ref, dk_ref, dv_ref, db_ref):
        r, i = pl.program_id(0), pl.program_id(1)
        start, off = _att_window(i, n_sub)
        valid = _att_valid(off)
        tile = _att_tile_id(i, nq)

        @pl.when(i == 0)
        def _():
            dk_ref[...] = jnp.zeros_like(dk_ref)
            dv_ref[...] = jnp.zeros_like(dv_ref)

        @pl.when((i == 0) & (r == 0))
        def _():
            db_ref[...] = jnp.zeros_like(db_ref)

        q4 = _heads(q_ref[...].astype(F32) * (ATT_HEAD_DIM ** -0.5), mask=True)
        k4 = _heads(k_ref[pl.ds(start, ATT_W), :].astype(F32))
        v4 = _heads(v_ref[pl.ds(start, ATT_W), :].astype(F32))
        dm4 = _heads(dm_ref[...], mask=True)
        s = jnp.where(valid, _bdot_nt(q4, k4) + b_ref[tile], NEG_INF)
        p = jnp.exp(s - _head_scalar(lse_ref[...]))
        ds = p * (_bdot_nt(dm4, v4) - _head_scalar(dl_ref[...]))
        dq_ref[...] = _join_heads(_bdot(ds, k4)) * (ATT_HEAD_DIM ** -0.5)
        dk4 = _bdot_tn(ds, q4)
        dv4 = _bdot_tn(p, dm4)
        dk_ref[pl.ds(start, ATT_W), :] += jnp.concatenate([dk4[0] + dk4[1], dk4[2] + dk4[3]], axis=1)
        dv_ref[pl.ds(start, ATT_W), :] += jnp.concatenate([dv4[0] + dv4[1], dv4[2] + dv4[3]], axis=1)
        db_ref[tile] += ds

    def qkv_spec(which, full):
        shape = (n_sub, 2 * LANES) if full else (ATT_BQ, 2 * LANES)
        return pl.BlockSpec(shape, lambda r, i: (0 if full else i, at(r, which)))

    blk = pl.BlockSpec((ATT_BQ, 2 * LANES), lambda r, i: (i, r))
    full = pl.BlockSpec((n_sub, 2 * LANES), lambda r, i: (0, r))
    bias_spec = pl.BlockSpec(bias.shape, lambda r, i: (0, 0, 0, 0))
    sub = jax.ShapeDtypeStruct((n_sub, dil * 2 * LANES), F32)
    dq, dk, dv, db = pl.pallas_call(
        kern, name=name, grid=(dil, nq),
        in_specs=[qkv_spec(0, False), qkv_spec(1, True), qkv_spec(2, True), bias_spec, blk, blk,
                  pl.BlockSpec((ATT_BQ, 2 * LANES), lambda r, i: (i, dcat_at(r, 0)))],
        out_specs=[blk, full, full, bias_spec],
        out_shape=[sub, sub, sub, jax.ShapeDtypeStruct(bias.shape, F32)],
        compiler_params=_params("arbitrary", "arbitrary"),
    )(view, view, view, bias, lse_v, delta_v, dcat_v)
    return dq.reshape(s_len, -1), dk.reshape(s_len, -1), dv.reshape(s_len, -1), db


def _att_combine(o_g, lse_g, qkvm, kv_mem, *, name):
    def body(o0, o1, o2, l0, l1, l2, qm, kv):
        mx = jnp.maximum(jnp.maximum(l0, l1), l2)
        tot = mx + jnp.log(jnp.exp(l0 - mx) + jnp.exp(l1 - mx) + jnp.exp(l2 - mx))
        mixed = [o * jnp.exp(l - tot) for o, l in ((o0, l0), (o1, l1), (o2, l2))]
        return (jnp.concatenate(mixed + [_mem_attn(qm.astype(F32), kv)], axis=1), tot), ()
    return _rowwise(body, list(o_g) + list(lse_g) + [_col(qkvm, MEM_WIDTH, (3 * TOK_WIDTH) // MEM_WIDTH)], [kv_mem],
                    [(D_MODEL, F32), (MEM_WIDTH, F32)], [], tm=256, name=name)


def _head_sum_matrix():
    a = np.arange(MEM_WIDTH)
    return jnp.asarray((a[:, None] // ATT_HEAD_DIM == a[None, :] // ATT_HEAD_DIM).astype(np.float32))


def _att_bwd_prep(cat, dcat, qkvm, kv_mem, *, name):
    def body(cat, dcat, qm, kv, hs):
        prod = cat * dcat
        summed = prod[:, 0:256] + prod[:, 256:512] + prod[:, 512:768]
        delta = _dot(summed, hs, precision=HI)
        dqm, dkv = _mem_attn_bwd(qm.astype(F32), kv, dcat[:, TOK_WIDTH:])
        return (delta, dqm), (dkv,)
    return _rowwise(body, [cat, dcat, _col(qkvm, MEM_WIDTH, (3 * TOK_WIDTH) // MEM_WIDTH)], [kv_mem, _head_sum_matrix()],
                    [(MEM_WIDTH, F32), (MEM_WIDTH, F32)], [kv_mem.shape], tm=256, name=name)


def _dn_conv_post(s, j):
    scale = jnp.where(j < DN_HEADS, DN_HEAD_DIM ** -0.5, 1.0)
    normed = s * lax.rsqrt(jnp.sum(s * s, axis=-1, keepdims=True) + EPS) * scale
    return jnp.where(j >= 2 * DN_HEADS, s, normed)


def _shift_rows(x, sh):
    n = x.shape[0]
    row = lax.broadcasted_iota(jnp.int32, (n, 1), 0)
    rolled = pltpu.roll(x, (-sh) % n, 0)
    return jnp.where((row + sh >= 0) & (row + sh < n), rolled, 0.0)


def _dn_conv_taps(x, w_ref):
    c = x * w_ref[pl.ds(DN_CONV // 2, 1), :]
    for jj in range(DN_CONV):
        if jj != DN_CONV // 2:
            c = c + _shift_rows(x, jj - DN_CONV // 2) * w_ref[pl.ds(jj, 1), :]
    return c


def _dn_conv_fwd(proj, conv_w, *, name):
    s_len = proj.shape[0]
    width = 3 * TOK_WIDTH

    def kern(x_ref, w_ref, o_ref):
        j = pl.program_id(0)
        o_ref[...] = _dn_conv_post(_silu(_dn_conv_taps(x_ref[...], w_ref)), j)

    return pl.pallas_call(
        kern, name=name, grid=(width // LANES,),
        in_specs=[pl.BlockSpec((s_len, LANES), lambda j: (0, j)), pl.BlockSpec((DN_CONV, LANES), lambda j: (0, j))],
        out_specs=pl.BlockSpec((s_len, LANES), lambda j: (0, j)),
        out_shape=jax.ShapeDtypeStruct((s_len, width), F32),
        compiler_params=_params("parallel"),
    )(proj, conv_w)


def _dn_conv_bwd(proj, conv_w, d_fwd, d_bwd, which, *, name):
    s_len = proj.shape[0]

    def kern(x_ref, w_ref, df_ref, db_ref, dx_ref, dw_ref):
        j = pl.program_id(0) + which * DN_HEADS
        x = x_ref[...]
        c = _dn_conv_taps(x, w_ref)
        _, vjp = jax.vjp(lambda c_: _dn_conv_post(_silu(c_), j), c)
        dc = vjp(df_ref[...] + db_ref[...])[0]
        dx = dc * w_ref[pl.ds(DN_CONV // 2, 1), :]
        for jj in range(DN_CONV):
            sh = jj - DN_CONV // 2
            if sh != 0:
                dx = dx + _shift_rows(dc, -sh) * w_ref[pl.ds(jj, 1), :]
            dw_ref[pl.ds(jj, 1), :] = jnp.sum(dc * _shift_rows(x, sh), axis=0, keepdims=True)
        dx_ref[...] = dx

    return pl.pallas_call(
        kern, name=name, grid=(DN_HEADS,),
        in_specs=[pl.BlockSpec((s_len, LANES), lambda j: (0, j + which * DN_HEADS)),
                  pl.BlockSpec((DN_CONV, LANES), lambda j: (0, j + which * DN_HEADS)),
                  pl.BlockSpec((s_len, LANES), lambda j: (0, j)),
                  pl.BlockSpec((s_len, LANES), lambda j: (0, j))],
        out_specs=[pl.BlockSpec((s_len, LANES), lambda j: (0, j)), pl.BlockSpec((DN_CONV, LANES), lambda j: (0, j))],
        out_shape=[jax.ShapeDtypeStruct((s_len, TOK_WIDTH), F32), jax.ShapeDtypeStruct((DN_CONV, TOK_WIDTH), F32)],
        compiler_params=_params("parallel"),
    )(proj, conv_w, d_fwd, d_bwd)


GATE_TM = 2 * DN_CHUNK


FWD_GATE_LANES = 2 * DN_HEADS


def _gate_constants():
    i = np.arange(GATE_TM)
    same = (i[:, None] // DN_CHUNK) == (i[None, :] // DN_CHUNK)
    cum_f = same & (i[None, :] <= i[:, None])
    cum_r = same & (i[None, :] >= i[:, None])
    return tuple(jnp.asarray(np.asarray(a, np.float32)) for a in (cum_f, cum_r, same))


def _gate_params(p):
    z = jnp.zeros((DN_HEADS,), F32)
    return jnp.concatenate([p[0], z, p[1], z, jnp.zeros((LANES - N_GATES,), F32)]).reshape(1, LANES)


def _gate_params_bwd(dp):
    return jnp.stack([dp[0, 0:DN_HEADS], dp[0, 2 * DN_HEADS: 3 * DN_HEADS]])


def _dn_gates(gate_in, a_cols, dt_cols, cum_f, cum_r, tot):
    g = -jnp.exp(a_cols) * _softplus(gate_in + dt_cols)
    fwd_lane = lax.broadcasted_iota(jnp.int32, (1, LANES), 1) < FWD_GATE_LANES
    gc = jnp.where(fwd_lane, _dot(cum_f, g, precision=HI), _dot(cum_r, g, precision=HI))
    return gc, _dot(tot, g, precision=HI), jax.nn.sigmoid(gate_in)


def _dn_gates_fwd(proj, a_cols, dt_cols, *, name):
    def body(gi, *consts):
        return _dn_gates(gi, *consts), ()
    return _rowwise(body, [_col(proj, LANES, DN_IN_PAD // LANES - 1)], [a_cols, dt_cols, *_gate_constants()],
                    [(LANES, F32)] * 3, [], tm=GATE_TM, name=name)


def _dn_gates_bwd(proj, a_cols, dt_cols, d_gates, *, name):
    def body(gi, gcf, gtf, bf, gcr, gtr, br, a, dt, *consts):
        _, vjp = jax.vjp(lambda gi_, a_, dt_: _dn_gates(gi_, a_, dt_, *consts), gi, a, dt)
        dgi, da, ddt = vjp((gcf + gcr, gtf + gtr, bf + br))
        return (dgi,), (da, ddt)
    return _rowwise(body, [_col(proj, LANES, DN_IN_PAD // LANES - 1), *d_gates[0], *d_gates[1]],
                    [a_cols, dt_cols, *_gate_constants()], [(LANES, F32)], [a_cols.shape, dt_cols.shape],
                    tm=GATE_TM, name=name)


INV_BASE = 8


def _block_id_equal(c, size):
    i = lax.broadcasted_iota(jnp.int32, (c, c), 0) // size
    j = lax.broadcasted_iota(jnp.int32, (c, c), 1) // size
    return (i == j).astype(F32)


def _unit_tri_inverse_impl(lmat):
    c = lmat.shape[0]
    eye = _block_id_equal(c, 1)
    same = _block_id_equal(c, INV_BASE)
    neg = -lmat * same
    inv = eye + neg
    power = neg
    for _ in range(int(math.log2(INV_BASE)) - 1):
        power = _dot(power, power)
        inv = inv + _dot(inv, power)
    size = INV_BASE
    while size < c:
        bigger = _block_id_equal(c, 2 * size)
        inv = inv - _dot(_dot(inv, lmat * (bigger - same)), inv)
        same, size = bigger, 2 * size
    resid = eye - _dot(eye + lmat, inv, precision=HI)
    return inv + _dot(inv, resid)


@jax.custom_vjp
def _unit_tri_inverse(lmat):
    return _unit_tri_inverse_impl(lmat)


def _unit_tri_inverse_fwd(lmat):
    inv = _unit_tri_inverse_impl(lmat)
    return inv, inv


def _unit_tri_inverse_bwd(inv, d_inv):
    return (-_dot_tn(inv, _dot_nt(d_inv, inv)),)


_unit_tri_inverse.defvjp(_unit_tri_inverse_fwd, _unit_tri_inverse_bwd)


def _dn_chunk(q, k, v, gates_t, gc_row, tot_row, beta_row, state, tri, inverse):
    c = q.shape[0]
    assert c == DN_HEAD_DIM
    eye = _block_id_equal(c, 1)

    def along_rows(x, pick):
        return jnp.broadcast_to(jnp.sum(x * pick, axis=0, keepdims=True), (c, c))

    gc_j = along_rows(gates_t[0], gc_row)
    gc = gc_j.T
    g_tot = along_rows(gates_t[1], tot_row)
    beta = along_rows(gates_t[2], beta_row).T
    decay = jnp.exp(jnp.where(tri > 0, gc - gc_j, NEG_INF))
    k_beta = k * beta
    inv = inverse((tri - eye) * (_dot_nt(k_beta, k) * decay))
    e_gc = jnp.exp(gc)
    u = _dot(inv, v * beta)
    w = _dot(inv, k_beta * e_gc)
    intra = tri * (_dot_nt(q, k) * decay)
    v_new = u - _dot(w, state)
    out = _dot(q * e_gc, state) + _dot(intra, v_new)
    state = state * jnp.exp(g_tot) + _dot_tn(k * jnp.exp(g_tot - gc), v_new)
    return out, state


def _dn_tri():
    i = np.arange(DN_CHUNK)
    tri = np.stack([(i[None, :] <= i[:, None]), (i[None, :] >= i[:, None])]).astype(np.float32)
    return jnp.asarray(np.repeat(tri, DN_HEADS, axis=0))


def _dn_gate_picks():
    picks = np.zeros((3, 2 * DN_HEADS, 2 * DN_CHUNK, 1), np.float32)
    for d in range(2):
        for h in range(DN_HEADS):
            alpha = d * DN_CHUNK + d * 2 * DN_HEADS + h
            picks[0, d * DN_HEADS + h, alpha] = 1.0
            picks[1, d * DN_HEADS + h, alpha] = 1.0
            picks[2, d * DN_HEADS + h, alpha + DN_HEADS] = 1.0
    return jnp.asarray(picks)


def _stack_chains(fwd_ref, rev_ref):
    return jnp.stack([r[:, _head_cols(h)] for r in (fwd_ref, rev_ref) for h in range(DN_HEADS)])


def _unstack_chains(val, fwd_ref, rev_ref):
    for d, r in enumerate((fwd_ref, rev_ref)):
        for h in range(DN_HEADS):
            r[:, _head_cols(h)] = val[d * DN_HEADS + h]


def _gates_transposed(fwd_refs, rev_refs):
    return jnp.stack([jnp.concatenate([f[...].T, r[...].T], axis=0) for f, r in zip(fwd_refs, rev_refs, strict=True)])


def _dn_row_spec(nc, col, reverse, width=TOK_WIDTH):
    return pl.BlockSpec((DN_CHUNK, width), lambda t: ((nc - 1 - t) if reverse else t, col))


def _dn_state_spec(nc, reverse):
    return pl.BlockSpec((None, DN_HEADS, DN_HEAD_DIM, DN_HEAD_DIM), lambda t: ((nc - 1 - t) if reverse else t, 0, 0, 0))


def _head_cols(h):
    return pl.ds(h * DN_HEAD_DIM, DN_HEAD_DIM)


def _const_spec(arr):
    return pl.BlockSpec(arr.shape, functools.partial(lambda t, n: (0,) * n, n=arr.ndim))


def _dn_chains(inverse):
    return jax.vmap(lambda q, k, v, gates_t, *rest: _dn_chunk(q, k, v, gates_t, *rest, inverse),
                    in_axes=(0, 0, 0, None, 0, 0, 0, 0, 0))


def _dn_scan_fwd(qkv, gates, *, name):
    s_len = qkv.shape[0]
    nc = s_len // DN_CHUNK
    tri, picks = _dn_tri(), _dn_gate_picks()

    def kern(*refs):
        ins, (tri_ref, pick_ref, of_ref, or_ref, sf_ref, sr_ref, state) = refs[:12], refs[12:]

        @pl.when(pl.program_id(0) == 0)
        def _():
            state[...] = jnp.zeros_like(state)

        entry = state[...]
        qkv_c = [_stack_chains(ins[i], ins[6 + i]) for i in range(3)]
        gates_t = _gates_transposed(ins[3:6], ins[9:12])
        out, new = _dn_chains(_unit_tri_inverse_impl)(*qkv_c, gates_t, pick_ref[0], pick_ref[1], pick_ref[2], entry, tri_ref[...])
        sf_ref[...] = entry[:DN_HEADS]
        sr_ref[...] = entry[DN_HEADS:]
        _unstack_chains(out, of_ref, or_ref)
        state[...] = new

    in_specs = []
    for rev in (False, True):
        in_specs += [_dn_row_spec(nc, col, rev) for col in (0, 1, 2)] + [_dn_row_spec(nc, 0, rev, LANES)] * 3
    in_specs += [_const_spec(tri), _const_spec(picks)]
    return pl.pallas_call(
        kern, name=name, grid=(nc,), in_specs=in_specs,
        out_specs=[_dn_row_spec(nc, 0, False), _dn_row_spec(nc, 0, True), _dn_state_spec(nc, False), _dn_state_spec(nc, True)],
        out_shape=[jax.ShapeDtypeStruct((s_len, TOK_WIDTH), F32)] * 2
        + [jax.ShapeDtypeStruct((nc, DN_HEADS, DN_HEAD_DIM, DN_HEAD_DIM), F32)] * 2,
        scratch_shapes=[pltpu.VMEM((2 * DN_HEADS, DN_HEAD_DIM, DN_HEAD_DIM), F32)],
        compiler_params=_params("arbitrary"),
    )(*([qkv, qkv, qkv, *gates] * 2), tri, picks)


def _dn_scan_bwd(qkv, gates, states, d_o, *, name):
    s_len = qkv.shape[0]
    nc = s_len // DN_CHUNK
    tri, picks = _dn_tri(), _dn_gate_picks()

    def kern(*refs):
        ins, tri_ref, pick_ref, outs, d_state = refs[:16], refs[16], refs[17], refs[18:30], refs[30]

        @pl.when(pl.program_id(0) == 0)
        def _():
            d_state[...] = jnp.zeros_like(d_state)

        qkv_c = [_stack_chains(ins[i], ins[8 + i]) for i in range(3)]
        gates_t = _gates_transposed(ins[3:6], ins[11:14])
        entry = jnp.concatenate([ins[6][...], ins[14][...]], axis=0)
        d_out = _stack_chains(ins[7], ins[15])
        tri_v, picks_v = tri_ref[...], pick_ref[...]
        _, vjp = jax.vjp(lambda q, k, v, g, s: _dn_chains(_unit_tri_inverse)(q, k, v, g, picks_v[0], picks_v[1], picks_v[2], s, tri_v),
                         *qkv_c, gates_t, entry)
        dq, dk, dv, d_gates_t, d_entry = vjp((d_out, d_state[...]))
        for i, val in enumerate((dq, dk, dv)):
            _unstack_chains(val, outs[i], outs[6 + i])
        for i in range(3):
            outs[3 + i][...] = d_gates_t[i, :DN_CHUNK].T
            outs[9 + i][...] = d_gates_t[i, DN_CHUNK:].T
        d_state[...] = d_entry

    in_specs, out_specs, out_shape = [], [], []
    for rev in (True, False):
        in_specs += [_dn_row_spec(nc, col, rev) for col in (0, 1, 2)] + [_dn_row_spec(nc, 0, rev, LANES)] * 3
        in_specs += [_dn_state_spec(nc, rev), _dn_row_spec(nc, 0, rev)]
        out_specs += [_dn_row_spec(nc, 0, rev)] * 3 + [_dn_row_spec(nc, 0, rev, LANES)] * 3
        out_shape += [jax.ShapeDtypeStruct((s_len, TOK_WIDTH), F32)] * 3 + [jax.ShapeDtypeStruct((s_len, LANES), F32)] * 3
    in_specs += [_const_spec(tri), _const_spec(picks)]
    res = pl.pallas_call(
        kern, name=name, grid=(nc,), in_specs=in_specs, out_specs=out_specs, out_shape=out_shape,
        scratch_shapes=[pltpu.VMEM((2 * DN_HEADS, DN_HEAD_DIM, DN_HEAD_DIM), F32)],
        compiler_params=_params("arbitrary"),
    )(*[a for d in range(2) for a in (qkv, qkv, qkv, *gates, states[d], d_o)], tri, picks)
    return (res[0:3], res[3:6]), (res[6:9], res[9:12])


def _dn_out_head(o_f, o_b, z, gain):
    o = o_f + o_b
    return o * lax.rsqrt(jnp.mean(o * o, axis=-1, keepdims=True) + EPS) * gain * _silu(z)


def _dn_out(o_fwd, o_rev, proj, gain, qkv_kv_mem, *, name):
    def body(of, ob, z, qm, g, kv):
        heads = []
        for h in range(DN_HEADS):
            sl = slice(h * DN_HEAD_DIM, (h + 1) * DN_HEAD_DIM)
            heads.append(_dn_out_head(of[:, sl], ob[:, sl], z[:, sl], g))
        return (jnp.concatenate(heads + [_mem_attn(qm, kv)], axis=1),), ()
    return _rowwise(body, [o_fwd, o_rev, _col(proj, TOK_WIDTH, 3),
                           _col(proj, MEM_WIDTH, (4 * TOK_WIDTH) // MEM_WIDTH)], [gain, qkv_kv_mem],
                    [(D_MODEL, MXU_DTYPE)], [], tm=256, name=name)[0]


def _dn_out_bwd(o_fwd, o_rev, proj, gain, kv_mem, dcat, *, name):
    def body(of, ob, z, qm, dcat, g, kv):
        dos, dzs = [], []
        dgain = jnp.zeros_like(g)
        for h in range(DN_HEADS):
            sl = slice(h * DN_HEAD_DIM, (h + 1) * DN_HEAD_DIM)
            _, vjp = jax.vjp(_dn_out_head, of[:, sl], ob[:, sl], z[:, sl], g)
            d_of, _, dz, dg = vjp(dcat[:, sl])
            dos.append(d_of)
            dzs.append(dz)
            dgain = dgain + dg
        dqm, dkv = _mem_attn_bwd(qm, kv, dcat[:, TOK_WIDTH:])
        return (jnp.concatenate(dos, axis=1), jnp.concatenate(dzs, axis=1), dqm), (dgain, dkv)
    return _rowwise(body, [o_fwd, o_rev, _col(proj, TOK_WIDTH, 3),
                           _col(proj, MEM_WIDTH, (4 * TOK_WIDTH) // MEM_WIDTH), dcat], [gain, kv_mem],
                    [(TOK_WIDTH, F32), (TOK_WIDTH, F32), (MEM_WIDTH, F32)], [gain.shape, kv_mem.shape], tm=256, name=name)


def _pad_dn_w_in(w):
    gates = w[:, 4 * TOK_WIDTH: 4 * TOK_WIDTH + N_GATES]
    zeros = jnp.zeros((w.shape[0], DN_IN_PAD - DN_IN), w.dtype)
    return jnp.concatenate([w[:, :4 * TOK_WIDTH], w[:, 4 * TOK_WIDTH + N_GATES:], gates, zeros], axis=1)


def _unpad_dn_w_in(w):
    q_mem = w[:, 4 * TOK_WIDTH: 4 * TOK_WIDTH + MEM_WIDTH]
    gates = w[:, 4 * TOK_WIDTH + MEM_WIDTH: 4 * TOK_WIDTH + MEM_WIDTH + N_GATES]
    return jnp.concatenate([w[:, :4 * TOK_WIDTH], gates, q_mem], axis=1)


def _ffn_fwd(h, w_gu, w_d, tag):
    gu = _mm(h, w_gu, out_dtype=MXU_DTYPE, name=f"ffn_gu_{tag}")
    act = _swiglu_act(gu, name=f"ffn_act_{tag}")
    return gu, act, _mm(act, w_d, name=f"ffn_down_{tag}")


def _ffn_bwd(h, gu, act, w_gu, w_d, df, tag):
    d_act = _mm(df, w_d, tb=True, out_dtype=MXU_DTYPE, name=f"ffn_dact_{tag}")
    d_wd = _mm(act, df, ta=True, out_dtype=LINK_DTYPE, name=f"ffn_dwd_{tag}")
    d_gu = _swiglu_act_bwd(gu, d_act, name=f"ffn_dgu_{tag}")
    dh = _mm(d_gu, w_gu, tb=True, name=f"ffn_dh_{tag}")
    d_wgu = _mm(h, d_gu, ta=True, out_shards=_shards_of(w_gu), out_dtype=LINK_DTYPE, name=f"ffn_dwgu_{tag}")
    return dh, d_wgu, d_wd


def _local_step(x, mem, target, p):
    g = {}
    row = lambda v: v.reshape(1, -1)
    gains = {k: [row(p[k][i]) for i in range(2)] for k in
             ("mem_norm", "norm_mix_pre", "norm_mix_post", "norm_ffn_pre", "norm_ffn_post")}
    out_gain = row(p["dn_out_norm"])
    a_cols, dt_cols = _gate_params(p["dn_a_log"]), _gate_params(p["dn_dt_bias"])

    h0 = _pre_norm(x, gains["norm_mix_pre"][0], name="pre0")
    mem_n = [_pre_norm(mem, gains["mem_norm"][i], name=f"mem_norm{i}") for i in range(2)]
    qkvm = _mm(h0, p["att_w_in"], out_dtype=MXU_DTYPE, name="att_in")
    bias = [_bias_tiles(p["rel_bias"], gi) for gi in range(3)]
    att = [_att_fwd(qkvm, bias[gi], gi, name=f"att_fwd{gi}") for gi in range(3)]
    att_w_out, *mem_w_kv = p["att_rest_weights"](att[2][0])
    kv_mem = [_mm(mem_n[i], mem_w_kv[i], name=f"mem_kv{i}") for i in range(2)]
    cat0, lse_tot = _att_combine([a[0] for a in att], [a[1] for a in att], qkvm, kv_mem[0], name="att_combine")
    mo0 = _mm(cat0, att_w_out, name="att_out")
    x1, h1 = _post_pre(x, mo0, gains["norm_mix_post"][0], gains["norm_ffn_pre"][0], name="post_mix0")
    w_gu0, w_d0 = p["ffn_weights"](0, h1)
    gu0, act0, f0 = _ffn_fwd(h1, w_gu0, w_d0, 0)
    x2, h2 = _post_pre(x1, f0, gains["norm_ffn_post"][0], gains["norm_mix_pre"][1], name="post_ffn0")

    dn_w_in, dn_w_out = p["dn_weights"](h2)
    proj = _mm(h2, dn_w_in, name="dn_in")
    qkv = _dn_conv_fwd(proj, p["dn_conv"], name="dn_conv")
    gates = _dn_gates_fwd(proj, a_cols, dt_cols, name="dn_gates")
    o_fwd, o_rev, st_fwd, st_rev = _dn_scan_fwd(qkv, gates, name="dn_scan")
    cat1 = _dn_out(o_fwd, o_rev, proj, out_gain, kv_mem[1], name="dn_outnorm")
    mo1 = _mm(cat1, dn_w_out, name="dn_out")
    x3, h3 = _post_pre(x2, mo1, gains["norm_mix_post"][1], gains["norm_ffn_pre"][1], name="post_mix1")
    w_gu1, w_d1 = p["ffn_weights"](1, h3)
    gu1, act1, f1 = _ffn_fwd(h3, w_gu1, w_d1, 1)

    dx3, df1, dg_ffn_post1, loss_cols = _final_loss_bwd(x3, f1, gains["norm_ffn_post"][1], target, name="loss_bwd")
    dh3, d_wgu1, d_wd1 = _ffn_bwd(h3, gu1, act1, w_gu1, w_d1, df1, 1)
    sent = p["grads_ready"]("ffn1", {("ffn_w_gate_up", 1): d_wgu1, ("ffn_w_down", 1): d_wd1})
    dx2, dmo1, dg_mix_post1, dg_ffn_pre1 = _post_pre_bwd(x2, mo1, gains["norm_mix_post"][1] + sent, gains["norm_ffn_pre"][1],
                                                         dx3, dh3, name="post_mix1_bwd")
    dcat1 = _mm(dmo1, dn_w_out, tb=True, name="dn_out_dx")
    g["dn_w_out"] = _mm(cat1, dmo1, ta=True, out_dtype=LINK_DTYPE, name="dn_out_dw")
    d_o, dz, dqm1, d_out_gain, dkv1 = _dn_out_bwd(o_fwd, o_rev, proj, out_gain, kv_mem[1], dcat1, name="dn_outnorm_bwd")
    (d_f, dg_f), (d_r, dg_r) = _dn_scan_bwd(qkv, gates, (st_fwd, st_rev), d_o, name="dn_scan_bwd")
    d_gate_cols, d_a_cols, d_dt_cols = _dn_gates_bwd(proj, a_cols, dt_cols, (dg_f, dg_r), name="dn_gates_bwd")
    d_pre, d_conv = zip(*[_dn_conv_bwd(proj, p["dn_conv"], d_f[which], d_r[which], which, name=f"dn_conv_bwd{which}")
                          for which in range(3)])
    dproj = jnp.concatenate(list(d_pre) + [dz, dqm1, d_gate_cols], axis=1).astype(MXU_DTYPE)
    dh2 = _mm(dproj, dn_w_in, tb=True, name="dn_in_dx")
    g["dn_w_in"] = _mm(h2, dproj, ta=True, out_dtype=LINK_DTYPE, name="dn_in_dw")
    g["dn_conv"] = jnp.concatenate(d_conv, axis=1)
    g["dn_a_log"] = _gate_params_bwd(d_a_cols)
    g["dn_dt_bias"] = _gate_params_bwd(d_dt_cols)
    g["dn_out_norm"] = d_out_gain

    d_mem_kv1 = _mm(mem_n[1], dkv1, ta=True, out_dtype=LINK_DTYPE, name="mem_kv_dw1")
    sent = p["grads_ready"]("dn", {("dn_w_in", 0): g["dn_w_in"], ("dn_w_out", 0): g["dn_w_out"], ("mem_w_kv", 1): d_mem_kv1})
    dx1, df0, dg_ffn_post0, dg_mix_pre1 = _post_pre_bwd(x1, f0, gains["norm_ffn_post"][0] + sent, gains["norm_mix_pre"][1],
                                                        dx2, dh2, name="post_ffn0_bwd")
    dh1, d_wgu0, d_wd0 = _ffn_bwd(h1, gu0, act0, w_gu0, w_d0, df0, 0)
    sent = p["grads_ready"]("ffn0", {("ffn_w_gate_up", 0): d_wgu0, ("ffn_w_down", 0): d_wd0})
    dx0, dmo0, dg_mix_post0, dg_ffn_pre0 = _post_pre_bwd(x, mo0, gains["norm_mix_post"][0] + sent, gains["norm_ffn_pre"][0],
                                                         dx1, dh1, name="post_mix0_bwd")
    dcat0 = _mm(dmo0, att_w_out, tb=True, name="att_out_dx")
    g["att_w_out"] = _mm(cat0, dmo0, ta=True, out_dtype=LINK_DTYPE, name="att_out_dw")
    delta, dqm0, dkv0 = _att_bwd_prep(cat0, dcat0, qkvm, kv_mem[0], name="att_bwd_prep")
    d_mem_kv0 = _mm(mem_n[0], dkv0, ta=True, out_dtype=LINK_DTYPE, name="mem_kv_dw0")
    sent = p["grads_ready"]("att_out", {("att_w_out", 0): g["att_w_out"], ("mem_w_kv", 0): d_mem_kv0})
    att_b = [_att_bwd(qkvm, bias[gi] + sent, lse_tot, delta, dcat0, gi, name=f"att_bwd{gi}") for gi in range(3)]
    dqkvm = jnp.concatenate([a[w] for w in range(3) for a in att_b] + [dqm0], axis=1).astype(MXU_DTYPE)
    g["rel_bias"] = sum(_bias_tiles_bwd(p["rel_bias"], att_b[gi][3], gi) for gi in range(3))
    g["att_w_in"] = _mm(h0, dqkvm, ta=True, out_shards=_shards_of(p["att_w_in"]), out_dtype=LINK_DTYPE, name="att_in_dw")
    sent = p["grads_ready"]("att_in", {("att_w_in", 0): g["att_w_in"]})
    dh0 = _mm(dqkvm, p["att_w_in"], tb=True, name="att_in_dx")
    grad_x, dg_mix_pre0 = _pre_norm_bwd(x, gains["norm_mix_pre"][0] + sent, dh0, dx0, name="pre0_bwd")

    d_mem_norm = []
    for i, dkv in enumerate((dkv0, dkv1)):
        d_mem_n = _mm(dkv, mem_w_kv[i], tb=True, name=f"mem_kv_dx{i}")
        d_mem_norm.append(_gain_bwd(mem, gains["mem_norm"][i], d_mem_n, name=f"mem_norm_bwd{i}"))
    g["mem_w_kv"] = [d_mem_kv0, d_mem_kv1]
    g["mem_norm"] = jnp.concatenate(d_mem_norm, axis=0)
    g["norm_mix_pre"] = jnp.concatenate([dg_mix_pre0, dg_mix_pre1], axis=0)
    g["norm_mix_post"] = jnp.concatenate([dg_mix_post0, dg_mix_post1], axis=0)
    g["norm_ffn_pre"] = jnp.concatenate([dg_ffn_pre0, dg_ffn_pre1], axis=0)
    g["norm_ffn_post"] = jnp.concatenate([dg_ffn_post0, dg_ffn_post1], axis=0)
    g["ffn_w_gate_up"] = [d_wgu0, d_wgu1]
    g["ffn_w_down"] = [d_wd0, d_wd1]
    return loss_cols, grad_x, g


N_CHIPS = 4
N_DEV = 8
MESH = pl.DeviceIdType.MESH
BIG = (("att_w_in", (1, 1024, 640), 2), ("att_w_out", (1, 256, 1024), 1), ("dn_w_in", (1, 1024, 838), 2),
       ("dn_w_out", (1, 256, 1024), 1), ("mem_w_kv", (2, 256, 512), 1), ("ffn_w_gate_up", (2, 1024, 1408), 2),
       ("ffn_w_down", (2, 704, 1024), 1))


def _mesh_pos():
    return lax.axis_index("x"), lax.axis_index("y"), lax.axis_index("c")


def _other_chips(x, y):
    return [(1 - x, y), (x, 1 - y), (1 - x, 1 - y)]


ANY = pl.BlockSpec(memory_space=pl.ANY)


def _all_reduce_small(v, *, name):
    rows, cols = v.shape
    flips = [(dx, dy, dc) for dx in (0, 1) for dy in (0, 1) for dc in (0, 1)][1:]

    def body(v_ref, o_ref, buf, send_sems, recv_sems):
        x, y, c = _mesh_pos()

        def peer(f):
            return tuple(1 - p if fl else p for p, fl in zip((x, y, c), f))

        def index(p):
            return 4 * p[0] + 2 * p[1] + p[2]

        buf[index((x, y, c))] = v_ref[...]
        sends = []
        for k, f in enumerate(flips):
            cp = pltpu.make_async_remote_copy(src_ref=v_ref, dst_ref=buf.at[index((x, y, c))], send_sem=send_sems.at[k],
                                              recv_sem=recv_sems.at[k], device_id=peer(f), device_id_type=MESH)
            cp.start()
            sends.append(cp)
        for k, f in enumerate(flips):
            pltpu.make_async_remote_copy(src_ref=v_ref, dst_ref=buf.at[index(peer(f))], send_sem=send_sems.at[k],
                                         recv_sem=recv_sems.at[k], device_id=peer(f), device_id_type=MESH).wait_recv()
        for cp in sends:
            cp.wait_send()
        acc = buf[0]
        for d in range(1, N_DEV):
            acc = acc + buf[d]
        o_ref[...] = acc

    vmem = pl.BlockSpec(memory_space=pltpu.VMEM)
    return pl.pallas_call(
        body, name=name, in_specs=[vmem], out_specs=vmem, out_shape=jax.ShapeDtypeStruct((rows, cols), F32),
        scratch_shapes=[pltpu.VMEM((N_DEV, rows, cols), F32), pltpu.SemaphoreType.DMA((N_DEV - 1,)),
                        pltpu.SemaphoreType.DMA((N_DEV - 1,))],
    )(v)


def _adamw(w, g, m, v, *, name):
    def body(w, g, m, v):
        m = ADAM_B1 * m + (1.0 - ADAM_B1) * g
        v = ADAM_B2 * v + (1.0 - ADAM_B2) * (g * g)
        m_hat = m / (1.0 - ADAM_B1 ** ADAM_STEP)
        v_hat = v / (1.0 - ADAM_B2 ** ADAM_STEP)
        delta = -ADAM_LR * (m_hat / (jnp.sqrt(v_hat) + ADAM_EPS) + ADAM_WD * w)
        return (delta, m, v), ()
    rows, cols = w.shape
    if rows % SUBLANES == 0:
        return _rowwise(body, [w, g, m, v], [], [(cols, F32)] * 3, [], tm=_tile(rows, 256, SUBLANES), name=name)

    def kern(*refs):
        outs, _ = body(*[r[...] for r in refs[:4]])
        for r, val in zip(refs[4:], outs, strict=True):
            r[...] = val

    spec = pl.BlockSpec((rows, _tile(cols, 256, LANES)), lambda j: (0, j))
    return pl.pallas_call(
        kern, name=name, grid=(cols // spec.block_shape[1],), in_specs=[spec] * 4, out_specs=[spec] * 3,
        out_shape=[jax.ShapeDtypeStruct((rows, cols), F32)] * 3, compiler_params=_params("parallel"),
    )(w, g, m, v)


def _pack_small(arrs, rows):
    flat = jnp.concatenate([a.reshape(-1) for a in arrs])
    return jnp.pad(flat, (0, rows * LANES - flat.shape[0])).reshape(rows, LANES)


def _unpack_small(packed, shapes):
    flat = packed.reshape(-1)
    out, off = [], 0
    for s in shapes:
        size = math.prod(s)
        out.append(flat[off: off + size].reshape(s))
        off += size
    return out


def _small_rows(shapes):
    return -(-sum(math.prod(s) for s in shapes) // (SUBLANES * LANES)) * SUBLANES


def _sem_pairs(n):
    return [pltpu.SemaphoreType.DMA((n,)), pltpu.SemaphoreType.DMA((n,))]


def _gather_blocks(blocks, *, name):
    n = len(blocks)

    def body(*refs):
        x_refs, out_refs, (send_sems, recv_sems) = refs[:n], refs[n: 2 * n], refs[2 * n:]
        x, y, c = _mesh_pos()
        sibling = (x, y, 1 - c)
        chips = _other_chips(x, y)

        def copy(k, src, dst, to):
            return pltpu.make_async_remote_copy(src_ref=src, dst_ref=dst, send_sem=send_sems.at[k],
                                                recv_sem=recv_sems.at[k], device_id=to, device_id_type=MESH)

        def part(b, chip, h):
            half = blocks[b].shape[0] // 2
            return out_refs[b].at[2 * chip[0] + chip[1], pl.ds(h * half, half), :]

        def my_half(b):
            half = blocks[b].shape[0] // 2
            return x_refs[b].at[pl.ds(c * half, half), :]

        first = [copy(6 * b + j, my_half(b), part(b, (x, y), c), (*chip, c)) for b in range(n) for j, chip in enumerate(chips)]
        for cp in first:
            cp.start()
        passed = []
        for b in range(n):
            for j, chip in enumerate(chips):
                copy(6 * b + j, my_half(b), part(b, chip, c), (*chip, c)).wait_recv()
                cp = copy(6 * b + 3 + j, part(b, chip, c), part(b, chip, c), sibling)
                cp.start()
                passed.append(cp)
        for b in range(n):
            for j, chip in enumerate(chips):
                copy(6 * b + 3 + j, part(b, chip, 1 - c), part(b, chip, 1 - c), sibling).wait_recv()
        for cp in first + passed:
            cp.wait_send()

    return pl.pallas_call(
        body, name=name, in_specs=[ANY] * n, out_specs=[ANY] * n,
        out_shape=[jax.ShapeDtypeStruct((N_CHIPS, *a.shape), a.dtype) for a in blocks],
        scratch_shapes=_sem_pairs(6 * n),
    )(*blocks)


HBM = pl.BlockSpec(memory_space=pltpu.HBM)
SEM = pl.BlockSpec(memory_space=pltpu.SEMAPHORE)
DATAFLOW = pltpu.SideEffectType.DATAFLOW_SIDE_EFFECTING


def _gather_start(blocks, *, name):
    n = len(blocks)
    lands = [lax.empty((N_CHIPS, *a.shape), a.dtype) for a in blocks]

    def body(*refs):
        x_refs, land_refs, send_sems, recv_sems, token = refs[:n], refs[n: 2 * n], refs[2 * n], refs[2 * n + 1], refs[-1]
        x, y, c = _mesh_pos()
        for b in range(n):
            for j, chip in enumerate(_other_chips(x, y)):
                pltpu.make_async_remote_copy(src_ref=x_refs[b], dst_ref=land_refs[b].at[2 * x + y], send_sem=send_sems.at[3 * b + j],
                                             recv_sem=recv_sems.at[3 * b + j], device_id=(*chip, c), device_id_type=MESH).start()
        token[...] = jnp.zeros_like(token)

    operands = [pltpu.with_memory_space_constraint(a, pltpu.HBM) for a in blocks + lands]
    res = pl.pallas_call(
        body, name=name, in_specs=[HBM] * (2 * n),
        out_shape=(pltpu.SemaphoreType.DMA((3 * n,)), pltpu.SemaphoreType.DMA((3 * n,)),
                   *[pltpu.HBM(a.shape, a.dtype) for a in operands], jax.ShapeDtypeStruct((SUBLANES, LANES), F32)),
        out_specs=(SEM, SEM, *[HBM] * (2 * n), pl.BlockSpec(memory_space=pltpu.VMEM)),
        input_output_aliases={i: 2 + i for i in range(2 * n)},
        compiler_params=pltpu.CompilerParams(has_side_effects=DATAFLOW),
    )(*operands)
    return res[0], res[1], list(res[2: 2 + n]), list(res[2 + n: 2 + 2 * n]), res[-1]


def _gather_wait(started, after, *, name):
    send_sems, recv_sems, blocks, lands, _ = started
    n = len(blocks)

    def body(*refs):
        x_refs, land_refs, send_sems, recv_sems = refs[:n], refs[n: 2 * n], refs[2 * n], refs[2 * n + 1]
        x, y, c = _mesh_pos()
        for b in range(n):
            for j, chip in enumerate(_other_chips(x, y)):
                cp = pltpu.make_async_remote_copy(src_ref=x_refs[b], dst_ref=land_refs[b].at[2 * chip[0] + chip[1]],
                                                  send_sem=send_sems.at[3 * b + j], recv_sem=recv_sems.at[3 * b + j],
                                                  device_id=(*chip, c), device_id_type=MESH)
                cp.wait_send()
                cp.wait_recv()

    res = pl.pallas_call(
        body, name=name, in_specs=(*[HBM] * (2 * n), SEM, SEM, ANY),
        out_shape=tuple(pltpu.HBM(a.shape, a.dtype) for a in blocks + lands), out_specs=tuple([HBM] * (2 * n)),
        input_output_aliases={i: i for i in range(2 * n)},
        compiler_params=pltpu.CompilerParams(has_side_effects=DATAFLOW),
    )(*blocks, *lands, send_sems, recv_sems, after)
    return list(res[n:])


def _swap_with_sibling(blocks, *, name):
    n = len(blocks)

    def body(*refs):
        in_refs, out_refs, (send_sems, recv_sems) = refs[:n], refs[n: 2 * n], refs[2 * n:]
        x, y, c = _mesh_pos()
        copies = [pltpu.make_async_remote_copy(src_ref=in_refs[b], dst_ref=out_refs[b], send_sem=send_sems.at[b],
                                               recv_sem=recv_sems.at[b], device_id=(x, y, 1 - c), device_id_type=MESH)
                  for b in range(n)]
        for cp in copies:
            cp.start()
        for cp in copies:
            cp.wait()

    return pl.pallas_call(
        body, name=name, in_specs=[ANY] * n, out_specs=[ANY] * n,
        out_shape=[jax.ShapeDtypeStruct(a.shape, a.dtype) for a in blocks], scratch_shapes=_sem_pairs(n),
    )(*blocks)


def _sum_chips_block(parts, *, name):
    n, half, cols = parts.shape
    tm = _tile(half, 512, 2 * SUBLANES)

    def kern(p_ref, o_ref):
        acc = p_ref[0].astype(F32)
        for s in range(1, n):
            acc = acc + p_ref[s].astype(F32)
        o_ref[...] = acc

    return pl.pallas_call(
        kern, name=name, grid=(half // tm,),
        in_specs=[pl.BlockSpec((n, tm, cols), lambda i: (0, i, 0))],
        out_specs=pl.BlockSpec((tm, cols), lambda i: (i, 0)),
        out_shape=jax.ShapeDtypeStruct((half, cols), F32),
        compiler_params=_params("parallel"),
    )(parts)


PEER_FLIPS = [(dx, dy, dc) for dx in (0, 1) for dy in (0, 1) for dc in (0, 1)][1:]


def _flipped(pos, flip):
    return tuple(1 - p if f else p for p, f in zip(pos, flip))


def _device_index(pos):
    return 4 * pos[0] + 2 * pos[1] + pos[2]


def _scatter_start(blocks, *, name):
    n = len(blocks)
    lands = [lax.empty((N_DEV, a.shape[1] // 2, a.shape[2]), a.dtype) for a in blocks]

    def body(*refs):
        g_refs, land_refs, send_sems, recv_sems, token = refs[:n], refs[n: 2 * n], refs[2 * n], refs[2 * n + 1], refs[-1]
        pos = _mesh_pos()
        for b in range(n):
            half = blocks[b].shape[1] // 2
            for k, flip in enumerate(PEER_FLIPS):
                peer = _flipped(pos, flip)
                pltpu.make_async_remote_copy(src_ref=g_refs[b].at[2 * peer[0] + peer[1], pl.ds(peer[2] * half, half), :],
                                             dst_ref=land_refs[b].at[_device_index(pos)],
                                             send_sem=send_sems.at[7 * b + k], recv_sem=recv_sems.at[7 * b + k],
                                             device_id=peer, device_id_type=MESH).start()
        token[...] = jnp.zeros_like(token)

    operands = [pltpu.with_memory_space_constraint(a, pltpu.HBM) for a in blocks + lands]
    res = pl.pallas_call(
        body, name=name, in_specs=[HBM] * (2 * n),
        out_shape=(pltpu.SemaphoreType.DMA((7 * n,)), pltpu.SemaphoreType.DMA((7 * n,)),
                   *[pltpu.HBM(a.shape, a.dtype) for a in operands], jax.ShapeDtypeStruct((SUBLANES, LANES), F32)),
        out_specs=(SEM, SEM, *[HBM] * (2 * n), pl.BlockSpec(memory_space=pltpu.VMEM)),
        input_output_aliases={i: 2 + i for i in range(2 * n)},
        compiler_params=pltpu.CompilerParams(has_side_effects=DATAFLOW),
    )(*operands)
    return res[0], res[1], list(res[2: 2 + n]), list(res[2 + n: 2 + 2 * n]), res[-1]


def _scatter_wait(started, after, *, name):
    send_sems, recv_sems, blocks, lands, _ = started
    n = len(blocks)

    def body(*refs):
        g_refs, land_refs, send_sems, recv_sems = refs[:n], refs[n: 2 * n], refs[2 * n], refs[2 * n + 1]
        pos = _mesh_pos()
        for b in range(n):
            half = blocks[b].shape[1] // 2
            for k, flip in enumerate(PEER_FLIPS):
                peer = _flipped(pos, flip)
                cp = pltpu.make_async_remote_copy(src_ref=g_refs[b].at[0, pl.ds(0, half), :],
                                                  dst_ref=land_refs[b].at[_device_index(peer)],
                                                  send_sem=send_sems.at[7 * b + k], recv_sem=recv_sems.at[7 * b + k],
                                                  device_id=peer, device_id_type=MESH)
                cp.wait_send()
                cp.wait_recv()

    res = pl.pallas_call(
        body, name=name, in_specs=(*[HBM] * (2 * n), SEM, SEM, ANY),
        out_shape=tuple(pltpu.HBM(a.shape, a.dtype) for a in blocks + lands), out_specs=tuple([HBM] * (2 * n)),
        input_output_aliases={i: i for i in range(2 * n)},
        compiler_params=pltpu.CompilerParams(has_side_effects=DATAFLOW),
    )(*blocks, *lands, send_sems, recv_sems, after)
    return list(res[:n]), list(res[n:])


def _reduce_finish(begun, names, after):
    x, y, c = _mesh_pos()
    mine = {}
    for key, started in begun.items():
        blocks, lands = _scatter_wait(started, after, name=f"rs_scatter_wait_{key}")
        parts = []
        for blk, land in zip(blocks, lands, strict=True):
            half = blk.shape[1] // 2
            own = lax.dynamic_slice(blk, (2 * x + y, c * half, 0), (1, half, blk.shape[2]))
            parts.append(lax.dynamic_update_slice(land, own, (_device_index((x, y, c)), 0, 0)))
        mine[key] = [_sum_chips_block(p, name=f"rs_sum_{nm}") for p, nm in zip(parts, names[key], strict=True)]
    flat = [a for key in begun for a in mine[key]]
    other = iter(_swap_with_sibling(flat, name="rs_join"))
    return {key: [jnp.concatenate([jnp.where(c == 0, a, b), jnp.where(c == 0, b, a)], axis=0)
                  for a, b in ((a, next(other)) for a in mine[key])] for key in begun}


WEIGHTS = ("rel_bias", "att_w_in", "att_w_out", "dn_w_in", "dn_conv", "dn_a_log", "dn_dt_bias", "dn_out_norm", "dn_w_out",
           "mem_norm", "mem_w_kv", "norm_mix_pre", "norm_mix_post", "norm_ffn_pre", "norm_ffn_post", "ffn_w_gate_up",
           "ffn_w_down")
BIG_NAMES = tuple(n for n, _, _ in BIG)
SMALL_NAMES = tuple(n for n in WEIGHTS if n not in BIG_NAMES)
CONV_COLS = 3 * TOK_WIDTH
CONV_SHARD = CONV_COLS // N_CHIPS
BLOCKS = tuple((n, layer) for n, shape, _ in BIG for layer in range(shape[0]))
COLUMN_SHARDED = {n: axis == 2 for n, _, axis in BIG}


def kernel(x, mem, rel_bias, att_w_in, att_w_out, dn_w_in, dn_conv, dn_a_log, dn_dt_bias, dn_out_norm, dn_w_out, mem_norm, mem_w_kv, norm_mix_pre, norm_mix_post, norm_ffn_pre, norm_ffn_post, ffn_w_gate_up, ffn_w_down, loss_target, m_rel_bias, m_att_w_in, m_att_w_out, m_dn_w_in, m_dn_conv, m_dn_a_log, m_dn_dt_bias, m_dn_out_norm, m_dn_w_out, m_mem_norm, m_mem_w_kv, m_norm_mix_pre, m_norm_mix_post, m_norm_ffn_pre, m_norm_ffn_post, m_ffn_w_gate_up, m_ffn_w_down, v_rel_bias, v_att_w_in, v_att_w_out, v_dn_w_in, v_dn_conv, v_dn_a_log, v_dn_dt_bias, v_dn_out_norm, v_dn_w_out, v_mem_norm, v_mem_w_kv, v_norm_mix_pre, v_norm_mix_post, v_norm_ffn_pre, v_norm_ffn_post, v_ffn_w_gate_up, v_ffn_w_down):
    w = dict(zip(WEIGHTS, (rel_bias, att_w_in, att_w_out, dn_w_in, dn_conv, dn_a_log, dn_dt_bias, dn_out_norm, dn_w_out,
                           mem_norm, mem_w_kv, norm_mix_pre, norm_mix_post, norm_ffn_pre, norm_ffn_post, ffn_w_gate_up,
                           ffn_w_down)))
    m = dict(zip(WEIGHTS, (m_rel_bias, m_att_w_in, m_att_w_out, m_dn_w_in, m_dn_conv, m_dn_a_log, m_dn_dt_bias,
                           m_dn_out_norm, m_dn_w_out, m_mem_norm, m_mem_w_kv, m_norm_mix_pre, m_norm_mix_post,
                           m_norm_ffn_pre, m_norm_ffn_post, m_ffn_w_gate_up, m_ffn_w_down)))
    v = dict(zip(WEIGHTS, (v_rel_bias, v_att_w_in, v_att_w_out, v_dn_w_in, v_dn_conv, v_dn_a_log, v_dn_dt_bias,
                           v_dn_out_norm, v_dn_w_out, v_mem_norm, v_mem_w_kv, v_norm_mix_pre, v_norm_mix_post,
                           v_norm_ffn_pre, v_norm_ffn_post, v_ffn_w_gate_up, v_ffn_w_down)))
    cx, cy, cc = _mesh_pos()
    chip = 2 * cx + cy

    local = dict(zip(BLOCKS, lax.optimization_barrier(
        [(w[n][layer].T if n == "dn_w_in" else w[n][layer]).astype(MXU_DTYPE) for n, layer in BLOCKS]), strict=True))

    def usable(block, got):
        got = lax.dynamic_update_slice(got, local[block][None], (chip, 0, 0))
        return got if COLUMN_SHARDED[block[0]] else got.reshape(-1, got.shape[-1])

    late = {"att_rest": [("att_w_out", 0), ("mem_w_kv", 0), ("mem_w_kv", 1)],
            "ffn0": [("ffn_w_gate_up", 0), ("ffn_w_down", 0)], "dn": [("dn_w_in", 0), ("dn_w_out", 0)],
            "ffn1": [("ffn_w_gate_up", 1), ("ffn_w_down", 1)]}
    first = [b for b in BLOCKS if all(b not in blks for blks in late.values())]
    first_got = _gather_blocks([local[b] for b in first], name="gather_weights")
    late_local, _ = lax.optimization_barrier(({k: [local[b] for b in blks] for k, blks in late.items()}, first_got[0]))
    started = {k: _gather_start(late_local[k], name=f"gather_start_{k}") for k in late}
    started_token = sum(s[4][0, 0] for s in started.values())

    def late_weights(key, after):
        lands = _gather_wait(started[key], after, name=f"gather_wait_{key}")
        return [usable(b, got) for b, got in zip(late[key], lands, strict=True)]

    def dn_weights(after):
        w_in, w_out = late_weights("dn", after)
        return _pad_dn_w_in(jnp.concatenate([w_in[s].T for s in range(N_CHIPS)], axis=1)), w_out

    full = {}
    for b, got in zip(first, first_got, strict=True):
        full.setdefault(b[0], []).append(usable(b, got))
    conv_rows = _small_rows([(DN_CONV, CONV_COLS)])
    conv_mine = jnp.where(cc == 0, 1.0, 0.0) * w["dn_conv"][0]
    conv_placed = lax.dynamic_update_slice(jnp.zeros((DN_CONV, CONV_COLS), F32), conv_mine, (0, chip * CONV_SHARD))
    conv_full = _unpack_small(_all_reduce_small(_pack_small([conv_placed], conv_rows), name="gather_conv"),
                              [(DN_CONV, CONV_COLS)])[0]
    p = {
        "rel_bias": w["rel_bias"], "att_w_in": full["att_w_in"][0], "att_rest_weights": lambda after: late_weights("att_rest", after),
        "dn_conv": conv_full, "dn_a_log": w["dn_a_log"][0], "dn_dt_bias": w["dn_dt_bias"][0],
        "dn_out_norm": w["dn_out_norm"][0], "mem_norm": w["mem_norm"],
        "norm_mix_pre": w["norm_mix_pre"] + started_token,
        "norm_mix_post": w["norm_mix_post"], "norm_ffn_pre": w["norm_ffn_pre"], "norm_ffn_post": w["norm_ffn_post"],
        "ffn_weights": lambda layer, after: late_weights(f"ffn{layer}", after), "dn_weights": dn_weights,
    }

    def chip_blocks(n, a):
        if n == "dn_w_in":
            a = _unpad_dn_w_in(a)
        if a.ndim == 3:
            return a
        if COLUMN_SHARDED[n]:
            return a.reshape(a.shape[0], N_CHIPS, -1).transpose(1, 0, 2)
        return a.reshape(N_CHIPS, -1, a.shape[-1])

    begun, begun_blocks = {}, {}

    def grads_ready(key, layer_grads):
        begun_blocks[key] = list(layer_grads)
        begun[key] = _scatter_start([chip_blocks(n, a) for (n, _), a in layer_grads.items()], name=f"rs_scatter_start_{key}")
        return begun[key][4][0, 0]

    p["grads_ready"] = grads_ready
    loss_cols, grad_x, g = _local_step(x[0], mem[0], loss_target[0], p)
    loss = lax.psum(jnp.sum(loss_cols), ("x", "y", "c"))
    finished = _reduce_finish(begun, {k: [f"{n}{layer}" for n, layer in blks] for k, blks in begun_blocks.items()}, grad_x)
    reduced = {b: r for k in begun for b, r in zip(begun_blocks[k], finished[k], strict=True)}
    grads = {n: jnp.concatenate([reduced[b] for b in BLOCKS if b[0] == n], axis=0).reshape(shape) for n, shape, _ in BIG}
    small_full_shapes = [(DN_CONV, CONV_COLS) if n == "dn_conv" else w[n].shape for n in SMALL_NAMES]
    small_sum = _all_reduce_small(_pack_small([g[n] for n in SMALL_NAMES], _small_rows(small_full_shapes)), name="reduce_small")
    for n, s in zip(SMALL_NAMES, _unpack_small(small_sum, small_full_shapes)):
        grads[n] = lax.dynamic_slice(s, (0, chip * CONV_SHARD), (DN_CONV, CONV_SHARD))[None] if n == "dn_conv" else s

    delta, new_m, new_v = {}, {}, {}
    for n in BIG_NAMES:
        shape = w[n].shape
        two_d = (lambda a: a[0].T) if n == "dn_w_in" else (lambda a: a.reshape(-1, shape[-1]))
        back = (lambda a: a.T[None]) if n == "dn_w_in" else (lambda a: a.reshape(shape))
        g_2d = two_d(grads[n])
        res = _adamw(two_d(w[n]), g_2d, two_d(m[n]), two_d(v[n]), name=f"adamw_{n}")
        grads[n], delta[n], new_m[n], new_v[n] = (back(r) for r in (g_2d, *res))
    small_shapes = [w[n].shape for n in SMALL_NAMES]
    rows = _small_rows(small_shapes)
    res = _adamw(*[_pack_small([d[n] for n in SMALL_NAMES], rows) for d in (w, grads, m, v)], name="adamw_small")
    for d, r in zip((delta, new_m, new_v), res):
        for n, a in zip(SMALL_NAMES, _unpack_small(r, small_shapes)):
            d[n] = a
    return (loss, grad_x[None], *[grads[n] for n in WEIGHTS], *[delta[n] for n in WEIGHTS],
            *[new_m[n] for n in WEIGHTS], *[new_v[n] for n in WEIGHTS])
```

```python
import functools
import math

import numpy as np
import jax
import jax.numpy as jnp
from jax import lax
from jax.experimental import pallas as pl
from jax.experimental.pallas import tpu as pltpu

F32 = jnp.float32
MXU_DTYPE = jnp.bfloat16
LINK_DTYPE = jnp.bfloat16
HI = lax.Precision.HIGHEST

EPS = 1e-6
NEG_INF = -1e30
LANES = 128
SUBLANES = 8
VMEM_LIMIT = 56 * 1024 * 1024
MM_WHOLE_K_BUDGET = 44 * 1024 * 1024

D_MODEL = 1024
TOK_WIDTH = 768
MEM_WIDTH = 256
MEM_LEN = 256
ATT_HEAD_DIM = 64
DILATIONS = (1, 4, 16)
HALF = 64
ATT_BQ = 128
ATT_W = ATT_BQ + 2 * HALF
REL_BUCKETS = 32
REL_MAX_DIST = 1024
DN_HEADS = 6
DN_HEAD_DIM = 128
DN_CONV = 5
DN_CHUNK = 128
D_FF = 2816
DN_IN = 3352
DN_IN_PAD = 3456
N_GATES = 4 * DN_HEADS

ADAM_LR = 0.001
ADAM_B1 = 0.9
ADAM_B2 = 0.999
ADAM_EPS = 1e-08
ADAM_WD = 0.01
ADAM_STEP = 10


def _tile(n, target, align):
    if n <= target:
        return n
    t = (target // align) * align
    while t >= align:
        if n % t == 0:
            return t
        t -= align
    raise ValueError(f"no tile for {n} (target {target}, align {align})")


def _params(*sem):
    return pltpu.CompilerParams(dimension_semantics=sem, vmem_limit_bytes=VMEM_LIMIT)


def _mm(a, b, *, name, ta=False, tb=False, out_shards=None, tm=1408, tn=1408, tk=1408, out_dtype=F32):
    if ta:
        K, M = a.shape
    else:
        M, K = a.shape
    sharded_b = b.ndim == 3
    if sharded_b:
        n_sh, b_rows, b_cols = b.shape
        N, K2 = (b_rows, n_sh * b_cols) if tb else (n_sh * b_cols, b_rows)
    else:
        N, K2 = b.shape if tb else b.shape[::-1]
    assert K == K2, (a.shape, b.shape, ta, tb)
    tm = _tile(M, tm, LANES if ta else SUBLANES)
    tn = N // out_shards if out_shards else (b_cols if sharded_b and not tb else _tile(N, tn, LANES))
    tk = b_cols if sharded_b and tb else _tile(K, tk, LANES)

    def vmem_bytes(tm_, tk_):
        return 2 * (tm_ * tk_ * a.dtype.itemsize + tk_ * tn * b.dtype.itemsize + tm_ * tn * jnp.dtype(out_dtype).itemsize)

    for rows in (tm, tm // 2):
        if not (sharded_b and tb) and M % rows == 0 and rows % LANES == 0 and vmem_bytes(rows, K) <= MM_WHOLE_K_BUDGET:
            tm, tk = rows, K
            break
    nk = K // tk
    a_spec = pl.BlockSpec((tk, tm), lambda i, j, k: (k, i)) if ta else pl.BlockSpec((tm, tk), lambda i, j, k: (i, k))
    if sharded_b:
        b_spec = (pl.BlockSpec((None, tn, tk), lambda i, j, k: (k, j, 0)) if tb
                  else pl.BlockSpec((None, tk, tn), lambda i, j, k: (j, k, 0)))
    else:
        b_spec = pl.BlockSpec((tn, tk), lambda i, j, k: (j, k)) if tb else pl.BlockSpec((tk, tn), lambda i, j, k: (k, j))
    if out_shards:
        out_spec = pl.BlockSpec((None, tm, tn), lambda i, j, k: (j, i, 0))
        out_shape = jax.ShapeDtypeStruct((out_shards, M, tn), out_dtype)
    else:
        out_spec = pl.BlockSpec((tm, tn), lambda i, j, k: (i, j))
        out_shape = jax.ShapeDtypeStruct((M, N), out_dtype)
    dims = (((0 if ta else 1,), (1 if tb else 0,)), ((), ()))

    def product(a_ref, b_ref):
        return lax.dot_general(a_ref[...].astype(MXU_DTYPE), b_ref[...].astype(MXU_DTYPE), dims, preferred_element_type=F32)

    def kern_whole(a_ref, b_ref, o_ref):
        o_ref[...] = product(a_ref, b_ref).astype(o_ref.dtype)

    def kern_steps(a_ref, b_ref, o_ref, acc_ref):
        k = pl.program_id(2)

        @pl.when(k == 0)
        def _():
            acc_ref[...] = jnp.zeros_like(acc_ref)

        acc_ref[...] += product(a_ref, b_ref)

        @pl.when(k == nk - 1)
        def _():
            o_ref[...] = acc_ref[...].astype(o_ref.dtype)

    return pl.pallas_call(
        kern_whole if nk == 1 else kern_steps, name=name, grid=(M // tm, N // tn, nk), in_specs=[a_spec, b_spec],
        out_specs=out_spec, out_shape=out_shape,
        scratch_shapes=[] if nk == 1 else [pltpu.VMEM((tm, tn), F32)],
        compiler_params=_params("parallel", "parallel", "arbitrary"),
    )(a, b)


def _shards_of(w):
    return w.shape[0] if w.ndim == 3 else None


def _col(arr, width, blk):
    return (arr, width, blk)


def _rowwise(body, rows, consts, out_rows, out_acc, *, tm, name):
    n_rows = (rows[0][0] if isinstance(rows[0], tuple) else rows[0]).shape[0]
    assert n_rows % tm == 0, (n_rows, tm)
    arrs, in_specs = [], []
    for r in rows:
        arr, width, blk = r if isinstance(r, tuple) else (r, r.shape[1], 0)
        assert arr.shape[0] == n_rows
        arrs.append(arr)
        in_specs.append(pl.BlockSpec((tm, width), functools.partial(lambda i, b: (i, b), b=blk)))
    for c in consts:
        arrs.append(c)
        in_specs.append(pl.BlockSpec(c.shape, functools.partial(lambda i, n: (0,) * n, n=c.ndim)))
    n_in, n_ro = len(arrs), len(out_rows)
    out_shape = [jax.ShapeDtypeStruct((n_rows, w), dt) for w, dt in out_rows]
    out_specs = [pl.BlockSpec((tm, w), lambda i: (i, 0)) for w, _ in out_rows]
    out_shape += [jax.ShapeDtypeStruct(s, F32) for s in out_acc]
    out_specs += [pl.BlockSpec(s, lambda i: (0, 0)) for s in out_acc]

    def kern(*refs):
        ro, ao = body(*[r[...] for r in refs[:n_in]])
        outs = refs[n_in:]
        for r, v in zip(outs[:n_ro], ro, strict=True):
            r[...] = v.astype(r.dtype)
        if out_acc:
            @pl.when(pl.program_id(0) == 0)
            def _():
                for r in outs[n_ro:]:
                    r[...] = jnp.zeros_like(r)

            for r, v in zip(outs[n_ro:], ao, strict=True):
                r[...] += v

    res = pl.pallas_call(
        kern, name=name, grid=(n_rows // tm,), in_specs=in_specs, out_specs=out_specs, out_shape=out_shape,
        compiler_params=_params("arbitrary" if out_acc else "parallel"),
    )(*arrs)
    return res


def _rms(x, gain):
    return x * lax.rsqrt(jnp.mean(x * x, axis=-1, keepdims=True) + EPS) * gain


def _silu(x):
    return x * jax.nn.sigmoid(x)


def _softplus(x):
    return jnp.maximum(x, 0.0) + jnp.log(1.0 + jnp.exp(-jnp.abs(x)))


def _dot_nt(a, b, precision=None):
    return lax.dot_general(a, b, (((1,), (1,)), ((), ())), preferred_element_type=F32, precision=precision)


def _dot_tn(a, b, precision=None):
    return lax.dot_general(a, b, (((0,), (0,)), ((), ())), preferred_element_type=F32, precision=precision)


def _dot(a, b, precision=None):
    return jnp.dot(a, b, preferred_element_type=F32, precision=precision)


def _pre_norm(x, gain, *, name):
    def body(x, g):
        return (_rms(x, g),), ()
    return _rowwise(body, [x], [gain], [(x.shape[1], MXU_DTYPE)], [], tm=_tile(x.shape[0], 512, 2 * SUBLANES), name=name)[0]


def _pre_norm_bwd(x, gain, dh, dx_other, *, name):
    def body(x, dh, dxo, g):
        _, vjp = jax.vjp(_rms, x, g)
        dx, dg = vjp(dh)
        return (dx + dxo,), (dg,)
    return _rowwise(body, [x, dh, dx_other], [gain], [(x.shape[1], F32)], [gain.shape], tm=512, name=name)


def _gain_bwd(x, gain, dh, *, name):
    def body(x, dh, g):
        _, vjp = jax.vjp(lambda g_: _rms(x, g_), g)
        return (), (vjp(dh)[0],)
    return _rowwise(body, [x, dh], [gain], [], [gain.shape], tm=_tile(x.shape[0], 512, SUBLANES), name=name)[0]


def _res_block(x_res, m, g_post, g_pre):
    x_new = x_res + _rms(m, g_post)
    return x_new, _rms(x_new, g_pre)


def _post_pre(x_res, m, g_post, g_pre, *, name):
    def body(x, m, gp, gq):
        return _res_block(x, m, gp, gq), ()
    d = x_res.shape[1]
    return _rowwise(body, [x_res, m], [g_post, g_pre], [(d, F32), (d, MXU_DTYPE)], [], tm=512, name=name)


def _post_pre_bwd(x_res, m, g_post, g_pre, dx_new, dh, *, name):
    def body(x, m, dxn, dh, gp, gq):
        _, vjp = jax.vjp(_res_block, x, m, gp, gq)
        dx, dm, dgp, dgq = vjp((dxn, dh))
        return (dx, dm), (dgp, dgq)
    d = x_res.shape[1]
    return _rowwise(body, [x_res, m, dx_new, dh], [g_post, g_pre], [(d, F32), (d, MXU_DTYPE)],
                    [g_post.shape, g_pre.shape], tm=256, name=name)


def _final_loss_bwd(x_res, m, g_post, target, *, name):
    d = x_res.shape[1]

    def loss_cols(x, m, g, t):
        err = x + _rms(m, g) - t
        return jnp.sum(err * err, axis=0, keepdims=True) * (0.5 / d)

    def body(x, m, t, g):
        cols, vjp = jax.vjp(lambda x_, m_, g_: loss_cols(x_, m_, g_, t), x, m, g)
        dx, dm, dg = vjp(jnp.ones_like(cols))
        return (dx, dm), (dg, cols)
    return _rowwise(body, [x_res, m, target], [g_post], [(d, F32), (d, MXU_DTYPE)], [g_post.shape, (1, d)], tm=256, name=name)


def _swiglu_act(gu, *, name):
    def body(gate, up):
        return (_silu(gate.astype(F32)) * up.astype(F32),), ()
    return _rowwise(body, [_col(gu, D_FF, 0), _col(gu, D_FF, 1)], [], [(D_FF, MXU_DTYPE)], [], tm=256, name=name)[0]


def _swiglu_act_bwd(gu, da, *, name):
    def body(gate, up, da):
        _, vjp = jax.vjp(lambda g, u: _silu(g) * u, gate.astype(F32), up.astype(F32))
        dg, du = vjp(da.astype(F32))
        return (jnp.concatenate([dg, du], axis=1),), ()
    return _rowwise(body, [_col(gu, D_FF, 0), _col(gu, D_FF, 1), da], [], [(2 * D_FF, MXU_DTYPE)], [], tm=256, name=name)[0]


def _lane_head_mask(width, head_dim, head):
    lane = lax.broadcasted_iota(jnp.int32, (1, width), 1)
    return (lane // head_dim) == head


def _mem_attn_heads(q4, k4, v4):
    logits = _bdot_nt(q4, k4)
    p = jnp.exp(logits - jnp.max(logits, axis=-1, keepdims=True))
    return _bdot(p / jnp.sum(p, axis=-1, keepdims=True), v4)


def _mem_heads(q_mem, kv):
    return _heads(q_mem * (ATT_HEAD_DIM ** -0.5), mask=True), _heads(kv[:, :MEM_WIDTH]), _heads(kv[:, MEM_WIDTH:])


def _mem_attn(q_mem, kv):
    return _join_heads(_mem_attn_heads(*_mem_heads(q_mem, kv)))


def _mem_attn_bwd(q_mem, kv, do):
    _, vjp = jax.vjp(_mem_attn_heads, *_mem_heads(q_mem, kv))
    dq4, dk4, dv4 = vjp(_heads(do, mask=True))
    return (_join_heads(dq4) * (ATT_HEAD_DIM ** -0.5),
            jnp.concatenate([dk4[0] + dk4[1], dk4[2] + dk4[3], dv4[0] + dv4[1], dv4[2] + dv4[3]], axis=1))


def _t5_bucket(rel):
    half = REL_BUCKETS // 2
    max_exact = half // 2
    n = np.abs(rel)
    large = max_exact + (np.log(np.maximum(n, 1) / max_exact) / math.log(REL_MAX_DIST / max_exact)
                         * (half - max_exact)).astype(np.int64)
    large = np.minimum(large, half - 1)
    return ((rel > 0) * half + np.where(n < max_exact, n, large)).astype(np.int32)


ATT_DIAGS = ATT_BQ + ATT_W - 1


def _bias_diag_onehot(dil):
    j = np.arange(ATT_DIAGS)
    tiles = []
    for off in (-HALF, 0, HALF):
        rel = j - (ATT_BQ - 1) - HALF - off
        hot = _t5_bucket(rel * dil)[:, None] == np.arange(REL_BUCKETS)[None, :]
        tiles.append(hot & (np.abs(rel) <= HALF)[:, None])
    return np.stack(tiles).astype(np.float32)


def _toeplitz(r):
    lead = r.shape[:-1]
    a = jnp.broadcast_to(r[..., None, :], lead + (ATT_BQ, ATT_DIAGS))
    a = jnp.pad(a, [(0, 0)] * len(lead) + [(0, 0), (0, 1)])
    a = a.reshape(lead + (ATT_BQ * (ATT_DIAGS + 1),))[..., : ATT_BQ * ATT_DIAGS].reshape(lead + (ATT_BQ, ATT_DIAGS))
    return a[..., ATT_BQ - 1: ATT_BQ - 1 + ATT_W]


def _bias_tiles(rel_bias, gi):
    heads = rel_bias[:, 4 * gi: 4 * gi + 4]
    diag = jnp.einsum('tnb,bh->thn', jnp.asarray(_bias_diag_onehot(DILATIONS[gi])), heads, precision=HI)
    return _toeplitz(diag)


def _bias_tiles_bwd(rel_bias, dtiles, gi):
    return jax.vjp(lambda rb: _bias_tiles(rb, gi), rel_bias)[1](dtiles)[0]


def _att_window(i, n_sub):
    start = jnp.clip(i * ATT_BQ - HALF, 0, n_sub - ATT_W)
    off = i * ATT_BQ - HALF - start
    return pl.multiple_of(start, HALF), off


def _att_valid(off):
    q = lax.broadcasted_iota(jnp.int32, (ATT_BQ, ATT_W), 0)
    kk = lax.broadcasted_iota(jnp.int32, (ATT_BQ, ATT_W), 1)
    return jnp.abs(kk - q - HALF - off) <= HALF


def _att_tile_id(i, nq):
    return jnp.where(i == 0, 0, jnp.where(i == nq - 1, 2, 1))


ATT_GROUP_HEADS = 4


def _heads(x, mask=False):
    out = []
    for p in range(2):
        pair = x[:, p * LANES: (p + 1) * LANES]
        for h in range(2):
            out.append(jnp.where(_lane_head_mask(LANES, ATT_HEAD_DIM, h), pair, 0.0) if mask else pair)
    return jnp.stack(out)


def _join_heads(x):
    first = _lane_head_mask(LANES, ATT_HEAD_DIM, 0)
    return jnp.concatenate([jnp.where(first, x[2 * p], x[2 * p + 1]) for p in range(2)], axis=1)


def _head_scalar(x):
    out = []
    for p in range(2):
        pair = x[:, p * LANES: (p + 1) * LANES]
        for h in range(2):
            out.append(jnp.max(jnp.where(_lane_head_mask(LANES, ATT_HEAD_DIM, h), pair, NEG_INF), axis=-1, keepdims=True))
    return jnp.stack(out)


def _bdot(a, b):
    return jnp.einsum('hqk,hkd->hqd', a, b, preferred_element_type=F32)


def _bdot_nt(a, b):
    return jnp.einsum('hqd,hkd->hqk', a, b, preferred_element_type=F32)


def _bdot_tn(a, b):
    return jnp.einsum('hqk,hqd->hkd', a, b, preferred_element_type=F32)


def _residue_view(arr, col_blocks, dil):
    if dil == 1:
        return arr, lambda r, j: col_blocks[j]
    width = 2 * LANES
    picked = jnp.concatenate([arr[:, b * width: (b + 1) * width] for b in col_blocks], axis=1)
    n = len(col_blocks)
    return picked.reshape(arr.shape[0] // dil, dil * n * width), lambda r, j: r * n + j


def _att_fwd(qkvm, bias, gi, *, name):
    dil = DILATIONS[gi]
    s_len = qkvm.shape[0]
    n_sub = s_len // dil
    nq = n_sub // ATT_BQ
    assert n_sub % ATT_BQ == 0 and n_sub >= ATT_W
    view, at = _residue_view(qkvm, [gi, 3 + gi, 6 + gi], dil)

    def kern(q_ref, k_ref, v_ref, b_ref, o_ref, lse_ref):
        start, off = _att_window(pl.program_id(1), n_sub)
        valid = _att_valid(off)
        q4 = _heads(q_ref[...].astype(F32) * (ATT_HEAD_DIM ** -0.5), mask=True)
        k4 = _heads(k_ref[pl.ds(start, ATT_W), :].astype(F32))
        v4 = _heads(v_ref[pl.ds(start, ATT_W), :].astype(F32))
        s = jnp.where(valid, _bdot_nt(q4, k4) + b_ref[...], NEG_INF)
        mx = jnp.max(s, axis=-1, keepdims=True)
        p = jnp.exp(s - mx)
        den = jnp.sum(p, axis=-1, keepdims=True)
        o_ref[...] = _join_heads(_bdot(p, v4) / den)
        lse_ref[...] = _join_heads(jnp.broadcast_to(mx + jnp.log(den), (ATT_GROUP_HEADS, ATT_BQ, LANES)))

    def qkv_spec(which, full):
        shape = (n_sub, 2 * LANES) if full else (ATT_BQ, 2 * LANES)
        return pl.BlockSpec(shape, lambda r, i: (0 if full else i, at(r, which)))

    out_spec = pl.BlockSpec((ATT_BQ, 2 * LANES), lambda r, i: (i, r))
    o, lse = pl.pallas_call(
        kern, name=name, grid=(dil, nq),
        in_specs=[qkv_spec(0, False), qkv_spec(1, True), qkv_spec(2, True),
                  pl.BlockSpec((None, ATT_GROUP_HEADS, ATT_BQ, ATT_W), lambda r, i: (_att_tile_id(i, nq), 0, 0, 0))],
        out_specs=[out_spec, out_spec],
        out_shape=[jax.ShapeDtypeStruct((n_sub, dil * 2 * LANES), F32)] * 2,
        compiler_params=_params("parallel", "arbitrary"),
    )(view, view, view, bias)
    return o.reshape(s_len, 2 * LANES), lse.reshape(s_len, 2 * LANES)


def _att_bwd(qkvm, bias, lse_tot, delta, dcat, gi, *, name):
    dil = DILATIONS[gi]
    s_len = qkvm.shape[0]
    n_sub = s_len // dil
    nq = n_sub // ATT_BQ
    view, at = _residue_view(qkvm, [gi, 3 + gi, 6 + gi], dil)
    lse_v = lse_tot.reshape(n_sub, dil * 2 * LANES)
    delta_v = delta.reshape(n_sub, dil * 2 * LANES)
    dcat_v, dcat_at = _residue_view(dcat, [gi], dil)

    def kern(q_ref, k_ref, v_ref, b_ref, lse_ref, dl_ref, dm_ref, dq_ref, dk_ref, dv_ref, db_ref):
        r, i = pl.program_id(0), pl.program_id(1)
        start, off = _att_window(i, n_sub)
        valid = _att_valid(off)
        tile = _att_tile_id(i, nq)

        @pl.when(i == 0)
        def _():
            dk_ref[...] = jnp.zeros_like(dk_ref)
            dv_ref[...] = jnp.zeros_like(dv_ref)

        @pl.when((i == 0) & (r == 0))
        def _():
            db_ref[...] = jnp.zeros_like(db_ref)

        q4 = _heads(q_ref[...].astype(F32) * (ATT_HEAD_DIM ** -0.5), mask=True)
        k4 = _heads(k_ref[pl.ds(start, ATT_W), :].astype(F32))
        v4 = _heads(v_ref[pl.ds(start, ATT_W), :].astype(F32))
        dm4 = _heads(dm_ref[...], mask=True)
        s = jnp.where(valid, _bdot_nt(q4, k4) + b_ref[tile], NEG_INF)
        p = jnp.exp(s - _head_scalar(lse_ref[...]))
        ds = p * (_bdot_nt(dm4, v4) - _head_scalar(dl_ref[...]))
        dq_ref[...] = _join_heads(_bdot(ds, k4)) * (ATT_HEAD_DIM ** -0.5)
        dk4 = _bdot_tn(ds, q4)
        dv4 = _bdot_tn(p, dm4)
        dk_ref[pl.ds(start, ATT_W), :] += jnp.concatenate([dk4[0] + dk4[1], dk4[2] + dk4[3]], axis=1)
        dv_ref[pl.ds(start, ATT_W), :] += jnp.concatenate([dv4[0] + dv4[1], dv4[2] + dv4[3]], axis=1)
        db_ref[tile] += ds

    def qkv_spec(which, full):
        shape = (n_sub, 2 * LANES) if full else (ATT_BQ, 2 * LANES)
        return pl.BlockSpec(shape, lambda r, i: (0 if full else i, at(r, which)))

    blk = pl.BlockSpec((ATT_BQ, 2 * LANES), lambda r, i: (i, r))
    full = pl.BlockSpec((n_sub, 2 * LANES), lambda r, i: (0, r))
    bias_spec = pl.BlockSpec(bias.shape, lambda r, i: (0, 0, 0, 0))
    sub = jax.ShapeDtypeStruct((n_sub, dil * 2 * LANES), F32)
    dq, dk, dv, db = pl.pallas_call(
        kern, name=name, grid=(dil, nq),
        in_specs=[qkv_spec(0, False), qkv_spec(1, True), qkv_spec(2, True), bias_spec, blk, blk,
                  pl.BlockSpec((ATT_BQ, 2 * LANES), lambda r, i: (i, dcat_at(r, 0)))],
        out_specs=[blk, full, full, bias_spec],
        out_shape=[sub, sub, sub, jax.ShapeDtypeStruct(bias.shape, F32)],
        compiler_params=_params("arbitrary", "arbitrary"),
    )(view, view, view, bias, lse_v, delta_v, dcat_v)
    return dq.reshape(s_len, -1), dk.reshape(s_len, -1), dv.reshape(s_len, -1), db


def _att_combine(o_g, lse_g, qkvm, kv_mem, *, name):
    def body(o0, o1, o2, l0, l1, l2, qm, kv):
        mx = jnp.maximum(jnp.maximum(l0, l1), l2)
        tot = mx + jnp.log(jnp.exp(l0 - mx) + jnp.exp(l1 - mx) + jnp.exp(l2 - mx))
        mixed = [o * jnp.exp(l - tot) for o, l in ((o0, l0), (o1, l1), (o2, l2))]
        return (jnp.concatenate(mixed + [_mem_attn(qm.astype(F32), kv)], axis=1), tot), ()
    return _rowwise(body, list(o_g) + list(lse_g) + [_col(qkvm, MEM_WIDTH, (3 * TOK_WIDTH) // MEM_WIDTH)], [kv_mem],
                    [(D_MODEL, F32), (MEM_WIDTH, F32)], [], tm=256, name=name)


def _head_sum_matrix():
    a = np.arange(MEM_WIDTH)
    return jnp.asarray((a[:, None] // ATT_HEAD_DIM == a[None, :] // ATT_HEAD_DIM).astype(np.float32))


def _att_bwd_prep(cat, dcat, qkvm, kv_mem, *, name):
    def body(cat, dcat, qm, kv, hs):
        prod = cat * dcat
        summed = prod[:, 0:256] + prod[:, 256:512] + prod[:, 512:768]
        delta = _dot(summed, hs, precision=HI)
        dqm, dkv = _mem_attn_bwd(qm.astype(F32), kv, dcat[:, TOK_WIDTH:])
        return (delta, dqm), (dkv,)
    return _rowwise(body, [cat, dcat, _col(qkvm, MEM_WIDTH, (3 * TOK_WIDTH) // MEM_WIDTH)], [kv_mem, _head_sum_matrix()],
                    [(MEM_WIDTH, F32), (MEM_WIDTH, F32)], [kv_mem.shape], tm=256, name=name)


def _dn_conv_post(s, j):
    scale = jnp.where(j < DN_HEADS, DN_HEAD_DIM ** -0.5, 1.0)
    normed = s * lax.rsqrt(jnp.sum(s * s, axis=-1, keepdims=True) + EPS) * scale
    return jnp.where(j >= 2 * DN_HEADS, s, normed)


def _shift_rows(x, sh):
    n = x.shape[0]
    row = lax.broadcasted_iota(jnp.int32, (n, 1), 0)
    rolled = pltpu.roll(x, (-sh) % n, 0)
    return jnp.where((row + sh >= 0) & (row + sh < n), rolled, 0.0)


def _dn_conv_taps(x, w_ref):
    c = x * w_ref[pl.ds(DN_CONV // 2, 1), :]
    for jj in range(DN_CONV):
        if jj != DN_CONV // 2:
            c = c + _shift_rows(x, jj - DN_CONV // 2) * w_ref[pl.ds(jj, 1), :]
    return c


def _dn_conv_fwd(proj, conv_w, *, name):
    s_len = proj.shape[0]
    width = 3 * TOK_WIDTH

    def kern(x_ref, w_ref, o_ref):
        j = pl.program_id(0)
        o_ref[...] = _dn_conv_post(_silu(_dn_conv_taps(x_ref[...], w_ref)), j)

    return pl.pallas_call(
        kern, name=name, grid=(width // LANES,),
        in_specs=[pl.BlockSpec((s_len, LANES), lambda j: (0, j)), pl.BlockSpec((DN_CONV, LANES), lambda j: (0, j))],
        out_specs=pl.BlockSpec((s_len, LANES), lambda j: (0, j)),
        out_shape=jax.ShapeDtypeStruct((s_len, width), F32),
        compiler_params=_params("parallel"),
    )(proj, conv_w)


def _dn_conv_bwd(proj, conv_w, d_fwd, d_bwd, which, *, name):
    s_len = proj.shape[0]

    def kern(x_ref, w_ref, df_ref, db_ref, dx_ref, dw_ref):
        j = pl.program_id(0) + which * DN_HEADS
        x = x_ref[...]
        c = _dn_conv_taps(x, w_ref)
        _, vjp = jax.vjp(lambda c_: _dn_conv_post(_silu(c_), j), c)
        dc = vjp(df_ref[...] + db_ref[...])[0]
        dx = dc * w_ref[pl.ds(DN_CONV // 2, 1), :]
        for jj in range(DN_CONV):
            sh = jj - DN_CONV // 2
            if sh != 0:
                dx = dx + _shift_rows(dc, -sh) * w_ref[pl.ds(jj, 1), :]
            dw_ref[pl.ds(jj, 1), :] = jnp.sum(dc * _shift_rows(x, sh), axis=0, keepdims=True)
        dx_ref[...] = dx

    return pl.pallas_call(
        kern, name=name, grid=(DN_HEADS,),
        in_specs=[pl.BlockSpec((s_len, LANES), lambda j: (0, j + which * DN_HEADS)),
                  pl.BlockSpec((DN_CONV, LANES), lambda j: (0, j + which * DN_HEADS)),
                  pl.BlockSpec((s_len, LANES), lambda j: (0, j)),
                  pl.BlockSpec((s_len, LANES), lambda j: (0, j))],
        out_specs=[pl.BlockSpec((s_len, LANES), lambda j: (0, j)), pl.BlockSpec((DN_CONV, LANES), lambda j: (0, j))],
        out_shape=[jax.ShapeDtypeStruct((s_len, TOK_WIDTH), F32), jax.ShapeDtypeStruct((DN_CONV, TOK_WIDTH), F32)],
        compiler_params=_params("parallel"),
    )(proj, conv_w, d_fwd, d_bwd)


GATE_TM = 2 * DN_CHUNK


FWD_GATE_LANES = 2 * DN_HEADS


def _gate_constants():
    i = np.arange(GATE_TM)
    same = (i[:, None] // DN_CHUNK) == (i[None, :] // DN_CHUNK)
    cum_f = same & (i[None, :] <= i[:, None])
    cum_r = same & (i[None, :] >= i[:, None])
    return tuple(jnp.asarray(np.asarray(a, np.float32)) for a in (cum_f, cum_r, same))


def _gate_params(p):
    z = jnp.zeros((DN_HEADS,), F32)
    return jnp.concatenate([p[0], z, p[1], z, jnp.zeros((LANES - N_GATES,), F32)]).reshape(1, LANES)


def _gate_params_bwd(dp):
    return jnp.stack([dp[0, 0:DN_HEADS], dp[0, 2 * DN_HEADS: 3 * DN_HEADS]])


def _dn_gates(gate_in, a_cols, dt_cols, cum_f, cum_r, tot):
    g = -jnp.exp(a_cols) * _softplus(gate_in + dt_cols)
    fwd_lane = lax.broadcasted_iota(jnp.int32, (1, LANES), 1) < FWD_GATE_LANES
    gc = jnp.where(fwd_lane, _dot(cum_f, g, precision=HI), _dot(cum_r, g, precision=HI))
    return gc, _dot(tot, g, precision=HI), jax.nn.sigmoid(gate_in)


def _dn_gates_fwd(proj, a_cols, dt_cols, *, name):
    def body(gi, *consts):
        return _dn_gates(gi, *consts), ()
    return _rowwise(body, [_col(proj, LANES, DN_IN_PAD // LANES - 1)], [a_cols, dt_cols, *_gate_constants()],
                    [(LANES, F32)] * 3, [], tm=GATE_TM, name=name)


def _dn_gates_bwd(proj, a_cols, dt_cols, d_gates, *, name):
    def body(gi, gcf, gtf, bf, gcr, gtr, br, a, dt, *consts):
        _, vjp = jax.vjp(lambda gi_, a_, dt_: _dn_gates(gi_, a_, dt_, *consts), gi, a, dt)
        dgi, da, ddt = vjp((gcf + gcr, gtf + gtr, bf + br))
        return (dgi,), (da, ddt)
    return _rowwise(body, [_col(proj, LANES, DN_IN_PAD // LANES - 1), *d_gates[0], *d_gates[1]],
                    [a_cols, dt_cols, *_gate_constants()], [(LANES, F32)], [a_cols.shape, dt_cols.shape],
                    tm=GATE_TM, name=name)


INV_BASE = 8


def _block_id_equal(c, size):
    i = lax.broadcasted_iota(jnp.int32, (c, c), 0) // size
    j = lax.broadcasted_iota(jnp.int32, (c, c), 1) // size
    return (i == j).astype(F32)


def _unit_tri_inverse_impl(lmat):
    c = lmat.shape[0]
    eye = _block_id_equal(c, 1)
    same = _block_id_equal(c, INV_BASE)
    neg = -lmat * same
    inv = eye + neg
    power = neg
    for _ in range(int(math.log2(INV_BASE)) - 1):
        power = _dot(power, power)
        inv = inv + _dot(inv, power)
    size = INV_BASE
    while size < c:
        bigger = _block_id_equal(c, 2 * size)
        inv = inv - _dot(_dot(inv, lmat * (bigger - same)), inv)
        same, size = bigger, 2 * size
    resid = eye - _dot(eye + lmat, inv, precision=HI)
    return inv + _dot(inv, resid)


@jax.custom_vjp
def _unit_tri_inverse(lmat):
    return _unit_tri_inverse_impl(lmat)


def _unit_tri_inverse_fwd(lmat):
    inv = _unit_tri_inverse_impl(lmat)
    return inv, inv


def _unit_tri_inverse_bwd(inv, d_inv):
    return (-_dot_tn(inv, _dot_nt(d_inv, inv)),)


_unit_tri_inverse.defvjp(_unit_tri_inverse_fwd, _unit_tri_inverse_bwd)


def _dn_chunk(q, k, v, gates_t, gc_row, tot_row, beta_row, state, tri, inverse):
    c = q.shape[0]
    assert c == DN_HEAD_DIM
    eye = _block_id_equal(c, 1)

    def along_rows(x, pick):
        return jnp.broadcast_to(jnp.sum(x * pick, axis=0, keepdims=True), (c, c))

    gc_j = along_rows(gates_t[0], gc_row)
    gc = gc_j.T
    g_tot = along_rows(gates_t[1], tot_row)
    beta = along_rows(gates_t[2], beta_row).T
    decay = jnp.exp(jnp.where(tri > 0, gc - gc_j, NEG_INF))
    k_beta = k * beta
    inv = inverse((tri - eye) * (_dot_nt(k_beta, k) * decay))
    e_gc = jnp.exp(gc)
    u = _dot(inv, v * beta)
    w = _dot(inv, k_beta * e_gc)
    intra = tri * (_dot_nt(q, k) * decay)
    v_new = u - _dot(w, state)
    out = _dot(q * e_gc, state) + _dot(intra, v_new)
    state = state * jnp.exp(g_tot) + _dot_tn(k * jnp.exp(g_tot - gc), v_new)
    return out, state


def _dn_tri():
    i = np.arange(DN_CHUNK)
    tri = np.stack([(i[None, :] <= i[:, None]), (i[None, :] >= i[:, None])]).astype(np.float32)
    return jnp.asarray(np.repeat(tri, DN_HEADS, axis=0))


def _dn_gate_picks():
    picks = np.zeros((3, 2 * DN_HEADS, 2 * DN_CHUNK, 1), np.float32)
    for d in range(2):
        for h in range(DN_HEADS):
            alpha = d * DN_CHUNK + d * 2 * DN_HEADS + h
            picks[0, d * DN_HEADS + h, alpha] = 1.0
            picks[1, d * DN_HEADS + h, alpha] = 1.0
            picks[2, d * DN_HEADS + h, alpha + DN_HEADS] = 1.0
    return jnp.asarray(picks)


def _stack_chains(fwd_ref, rev_ref):
    return jnp.stack([r[:, _head_cols(h)] for r in (fwd_ref, rev_ref) for h in range(DN_HEADS)])


def _unstack_chains(val, fwd_ref, rev_ref):
    for d, r in enumerate((fwd_ref, rev_ref)):
        for h in range(DN_HEADS):
            r[:, _head_cols(h)] = val[d * DN_HEADS + h]


def _gates_transposed(fwd_refs, rev_refs):
    return jnp.stack([jnp.concatenate([f[...].T, r[...].T], axis=0) for f, r in zip(fwd_refs, rev_refs, strict=True)])


def _dn_row_spec(nc, col, reverse, width=TOK_WIDTH):
    return pl.BlockSpec((DN_CHUNK, width), lambda t: ((nc - 1 - t) if reverse else t, col))


def _dn_state_spec(nc, reverse):
    return pl.BlockSpec((None, DN_HEADS, DN_HEAD_DIM, DN_HEAD_DIM), lambda t: ((nc - 1 - t) if reverse else t, 0, 0, 0))


def _head_cols(h):
    return pl.ds(h * DN_HEAD_DIM, DN_HEAD_DIM)


def _const_spec(arr):
    return pl.BlockSpec(arr.shape, functools.partial(lambda t, n: (0,) * n, n=arr.ndim))


def _dn_chains(inverse):
    return jax.vmap(lambda q, k, v, gates_t, *rest: _dn_chunk(q, k, v, gates_t, *rest, inverse),
                    in_axes=(0, 0, 0, None, 0, 0, 0, 0, 0))


def _dn_scan_fwd(qkv, gates, *, name):
    s_len = qkv.shape[0]
    nc = s_len // DN_CHUNK
    tri, picks = _dn_tri(), _dn_gate_picks()

    def kern(*refs):
        ins, (tri_ref, pick_ref, of_ref, or_ref, sf_ref, sr_ref, state) = refs[:12], refs[12:]

        @pl.when(pl.program_id(0) == 0)
        def _():
            state[...] = jnp.zeros_like(state)

        entry = state[...]
        qkv_c = [_stack_chains(ins[i], ins[6 + i]) for i in range(3)]
        gates_t = _gates_transposed(ins[3:6], ins[9:12])
        out, new = _dn_chains(_unit_tri_inverse_impl)(*qkv_c, gates_t, pick_ref[0], pick_ref[1], pick_ref[2], entry, tri_ref[...])
        sf_ref[...] = entry[:DN_HEADS]
        sr_ref[...] = entry[DN_HEADS:]
        _unstack_chains(out, of_ref, or_ref)
        state[...] = new

    in_specs = []
    for rev in (False, True):
        in_specs += [_dn_row_spec(nc, col, rev) for col in (0, 1, 2)] + [_dn_row_spec(nc, 0, rev, LANES)] * 3
    in_specs += [_const_spec(tri), _const_spec(picks)]
    return pl.pallas_call(
        kern, name=name, grid=(nc,), in_specs=in_specs,
        out_specs=[_dn_row_spec(nc, 0, False), _dn_row_spec(nc, 0, True), _dn_state_spec(nc, False), _dn_state_spec(nc, True)],
        out_shape=[jax.ShapeDtypeStruct((s_len, TOK_WIDTH), F32)] * 2
        + [jax.ShapeDtypeStruct((nc, DN_HEADS, DN_HEAD_DIM, DN_HEAD_DIM), F32)] * 2,
        scratch_shapes=[pltpu.VMEM((2 * DN_HEADS, DN_HEAD_DIM, DN_HEAD_DIM), F32)],
        compiler_params=_params("arbitrary"),
    )(*([qkv, qkv, qkv, *gates] * 2), tri, picks)


def _dn_scan_bwd(qkv, gates, states, d_o, *, name):
    s_len = qkv.shape[0]
    nc = s_len // DN_CHUNK
    tri, picks = _dn_tri(), _dn_gate_picks()

    def kern(*refs):
        ins, tri_ref, pick_ref, outs, d_state = refs[:16], refs[16], refs[17], refs[18:30], refs[30]

        @pl.when(pl.program_id(0) == 0)
        def _():
            d_state[...] = jnp.zeros_like(d_state)

        qkv_c = [_stack_chains(ins[i], ins[8 + i]) for i in range(3)]
        gates_t = _gates_transposed(ins[3:6], ins[11:14])
        entry = jnp.concatenate([ins[6][...], ins[14][...]], axis=0)
        d_out = _stack_chains(ins[7], ins[15])
        tri_v, picks_v = tri_ref[...], pick_ref[...]
        _, vjp = jax.vjp(lambda q, k, v, g, s: _dn_chains(_unit_tri_inverse)(q, k, v, g, picks_v[0], picks_v[1], picks_v[2], s, tri_v),
                         *qkv_c, gates_t, entry)
        dq, dk, dv, d_gates_t, d_entry = vjp((d_out, d_state[...]))
        for i, val in enumerate((dq, dk, dv)):
            _unstack_chains(val, outs[i], outs[6 + i])
        for i in range(3):
            outs[3 + i][...] = d_gates_t[i, :DN_CHUNK].T
            outs[9 + i][...] = d_gates_t[i, DN_CHUNK:].T
        d_state[...] = d_entry

    in_specs, out_specs, out_shape = [], [], []
    for rev in (True, False):
        in_specs += [_dn_row_spec(nc, col, rev) for col in (0, 1, 2)] + [_dn_row_spec(nc, 0, rev, LANES)] * 3
        in_specs += [_dn_state_spec(nc, rev), _dn_row_spec(nc, 0, rev)]
        out_specs += [_dn_row_spec(nc, 0, rev)] * 3 + [_dn_row_spec(nc, 0, rev, LANES)] * 3
        out_shape += [jax.ShapeDtypeStruct((s_len, TOK_WIDTH), F32)] * 3 + [jax.ShapeDtypeStruct((s_len, LANES), F32)] * 3
    in_specs += [_const_spec(tri), _const_spec(picks)]
    res = pl.pallas_call(
        kern, name=name, grid=(nc,), in_specs=in_specs, out_specs=out_specs, out_shape=out_shape,
        scratch_shapes=[pltpu.VMEM((2 * DN_HEADS, DN_HEAD_DIM, DN_HEAD_DIM), F32)],
        compiler_params=_params("arbitrary"),
    )(*[a for d in range(2) for a in (qkv, qkv, qkv, *gates, states[d], d_o)], tri, picks)
    return (res[0:3], res[3:6]), (res[6:9], res[9:12])


def _dn_out_head(o_f, o_b, z, gain):
    o = o_f + o_b
    return o * lax.rsqrt(jnp.mean(o * o, axis=-1, keepdims=True) + EPS) * gain * _silu(z)


def _dn_out(o_fwd, o_rev, proj, gain, qkv_kv_mem, *, name):
    def body(of, ob, z, qm, g, kv):
        heads = []
        for h in range(DN_HEADS):
            sl = slice(h * DN_HEAD_DIM, (h + 1) * DN_HEAD_DIM)
            heads.append(_dn_out_head(of[:, sl], ob[:, sl], z[:, sl], g))
        return (jnp.concatenate(heads + [_mem_attn(qm, kv)], axis=1),), ()
    return _rowwise(body, [o_fwd, o_rev, _col(proj, TOK_WIDTH, 3),
                           _col(proj, MEM_WIDTH, (4 * TOK_WIDTH) // MEM_WIDTH)], [gain, qkv_kv_mem],
                    [(D_MODEL, MXU_DTYPE)], [], tm=256, name=name)[0]


def _dn_out_bwd(o_fwd, o_rev, proj, gain, kv_mem, dcat, *, name):
    def body(of, ob, z, qm, dcat, g, kv):
        dos, dzs = [], []
        dgain = jnp.zeros_like(g)
        for h in range(DN_HEADS):
            sl = slice(h * DN_HEAD_DIM, (h + 1) * DN_HEAD_DIM)
            _, vjp = jax.vjp(_dn_out_head, of[:, sl], ob[:, sl], z[:, sl], g)
            d_of, _, dz, dg = vjp(dcat[:, sl])
            dos.append(d_of)
            dzs.append(dz)
            dgain = dgain + dg
        dqm, dkv = _mem_attn_bwd(qm, kv, dcat[:, TOK_WIDTH:])
        return (jnp.concatenate(dos, axis=1), jnp.concatenate(dzs, axis=1), dqm), (dgain, dkv)
    return _rowwise(body, [o_fwd, o_rev, _col(proj, TOK_WIDTH, 3),
                           _col(proj, MEM_WIDTH, (4 * TOK_WIDTH) // MEM_WIDTH), dcat], [gain, kv_mem],
                    [(TOK_WIDTH, F32), (TOK_WIDTH, F32), (MEM_WIDTH, F32)], [gain.shape, kv_mem.shape], tm=256, name=name)


def _pad_dn_w_in(w):
    gates = w[:, 4 * TOK_WIDTH: 4 * TOK_WIDTH + N_GATES]
    zeros = jnp.zeros((w.shape[0], DN_IN_PAD - DN_IN), w.dtype)
    return jnp.concatenate([w[:, :4 * TOK_WIDTH], w[:, 4 * TOK_WIDTH + N_GATES:], gates, zeros], axis=1)


def _unpad_dn_w_in(w):
    q_mem = w[:, 4 * TOK_WIDTH: 4 * TOK_WIDTH + MEM_WIDTH]
    gates = w[:, 4 * TOK_WIDTH + MEM_WIDTH: 4 * TOK_WIDTH + MEM_WIDTH + N_GATES]
    return jnp.concatenate([w[:, :4 * TOK_WIDTH], gates, q_mem], axis=1)


def _ffn_fwd(h, w_gu, w_d, tag):
    gu = _mm(h, w_gu, out_dtype=MXU_DTYPE, name=f"ffn_gu_{tag}")
    act = _swiglu_act(gu, name=f"ffn_act_{tag}")
    return gu, act, _mm(act, w_d, name=f"ffn_down_{tag}")


def _ffn_bwd(h, gu, act, w_gu, w_d, df, tag):
    d_act = _mm(df, w_d, tb=True, out_dtype=MXU_DTYPE, name=f"ffn_dact_{tag}")
    d_wd = _mm(act, df, ta=True, out_dtype=LINK_DTYPE, name=f"ffn_dwd_{tag}")
    d_gu = _swiglu_act_bwd(gu, d_act, name=f"ffn_dgu_{tag}")
    dh = _mm(d_gu, w_gu, tb=True, name=f"ffn_dh_{tag}")
    d_wgu = _mm(h, d_gu, ta=True, out_shards=_shards_of(w_gu), out_dtype=LINK_DTYPE, name=f"ffn_dwgu_{tag}")
    return dh, d_wgu, d_wd


def _local_step(x, mem, target, p):
    g = {}
    row = lambda v: v.reshape(1, -1)
    gains = {k: [row(p[k][i]) for i in range(2)] for k in
             ("mem_norm", "norm_mix_pre", "norm_mix_post", "norm_ffn_pre", "norm_ffn_post")}
    out_gain = row(p["dn_out_norm"])
    a_cols, dt_cols = _gate_params(p["dn_a_log"]), _gate_params(p["dn_dt_bias"])

    h0 = _pre_norm(x, gains["norm_mix_pre"][0], name="pre0")
    mem_n = [_pre_norm(mem, gains["mem_norm"][i], name=f"mem_norm{i}") for i in range(2)]
    qkvm = _mm(h0, p["att_w_in"], out_dtype=MXU_DTYPE, name="att_in")
    bias = [_bias_tiles(p["rel_bias"], gi) for gi in range(3)]
    att = [_att_fwd(qkvm, bias[gi], gi, name=f"att_fwd{gi}") for gi in range(3)]
    att_w_out, *mem_w_kv = p["att_rest_weights"](att[2][0])
    kv_mem = [_mm(mem_n[i], mem_w_kv[i], name=f"mem_kv{i}") for i in range(2)]
    cat0, lse_tot = _att_combine([a[0] for a in att], [a[1] for a in att], qkvm, kv_mem[0], name="att_combine")
    mo0 = _mm(cat0, att_w_out, name="att_out")
    x1, h1 = _post_pre(x, mo0, gains["norm_mix_post"][0], gains["norm_ffn_pre"][0], name="post_mix0")
    w_gu0, w_d0 = p["ffn_weights"](0, h1)
    gu0, act0, f0 = _ffn_fwd(h1, w_gu0, w_d0, 0)
    x2, h2 = _post_pre(x1, f0, gains["norm_ffn_post"][0], gains["norm_mix_pre"][1], name="post_ffn0")

    dn_w_in, dn_w_out = p["dn_weights"](h2)
    proj = _mm(h2, dn_w_in, name="dn_in")
    qkv = _dn_conv_fwd(proj, p["dn_conv"], name="dn_conv")
    gates = _dn_gates_fwd(proj, a_cols, dt_cols, name="dn_gates")
    o_fwd, o_rev, st_fwd, st_rev = _dn_scan_fwd(qkv, gates, name="dn_scan")
    cat1 = _dn_out(o_fwd, o_rev, proj, out_gain, kv_mem[1], name="dn_outnorm")
    mo1 = _mm(cat1, dn_w_out, name="dn_out")
    x3, h3 = _post_pre(x2, mo1, gains["norm_mix_post"][1], gains["norm_ffn_pre"][1], name="post_mix1")
    w_gu1, w_d1 = p["ffn_weights"](1, h3)
    gu1, act1, f1 = _ffn_fwd(h3, w_gu1, w_d1, 1)

    dx3, df1, dg_ffn_post1, loss_cols = _final_loss_bwd(x3, f1, gains["norm_ffn_post"][1], target, name="loss_bwd")
    dh3, d_wgu1, d_wd1 = _ffn_bwd(h3, gu1, act1, w_gu1, w_d1, df1, 1)
    sent = p["grads_ready"]("ffn1", {("ffn_w_gate_up", 1): d_wgu1, ("ffn_w_down", 1): d_wd1})
    dx2, dmo1, dg_mix_post1, dg_ffn_pre1 = _post_pre_bwd(x2, mo1, gains["norm_mix_post"][1] + sent, gains["norm_ffn_pre"][1],
                                                         dx3, dh3, name="post_mix1_bwd")
    dcat1 = _mm(dmo1, dn_w_out, tb=True, name="dn_out_dx")
    g["dn_w_out"] = _mm(cat1, dmo1, ta=True, out_dtype=LINK_DTYPE, name="dn_out_dw")
    d_o, dz, dqm1, d_out_gain, dkv1 = _dn_out_bwd(o_fwd, o_rev, proj, out_gain, kv_mem[1], dcat1, name="dn_outnorm_bwd")
    (d_f, dg_f), (d_r, dg_r) = _dn_scan_bwd(qkv, gates, (st_fwd, st_rev), d_o, name="dn_scan_bwd")
    d_gate_cols, d_a_cols, d_dt_cols = _dn_gates_bwd(proj, a_cols, dt_cols, (dg_f, dg_r), name="dn_gates_bwd")
    d_pre, d_conv = zip(*[_dn_conv_bwd(proj, p["dn_conv"], d_f[which], d_r[which], which, name=f"dn_conv_bwd{which}")
                          for which in range(3)])
    dproj = jnp.concatenate(list(d_pre) + [dz, dqm1, d_gate_cols], axis=1).astype(MXU_DTYPE)
    dh2 = _mm(dproj, dn_w_in, tb=True, name="dn_in_dx")
    g["dn_w_in"] = _mm(h2, dproj, ta=True, out_dtype=LINK_DTYPE, name="dn_in_dw")
    g["dn_conv"] = jnp.concatenate(d_conv, axis=1)
    g["dn_a_log"] = _gate_params_bwd(d_a_cols)
    g["dn_dt_bias"] = _gate_params_bwd(d_dt_cols)
    g["dn_out_norm"] = d_out_gain

    d_mem_kv1 = _mm(mem_n[1], dkv1, ta=True, out_dtype=LINK_DTYPE, name="mem_kv_dw1")
    sent = p["grads_ready"]("dn", {("dn_w_in", 0): g["dn_w_in"], ("dn_w_out", 0): g["dn_w_out"], ("mem_w_kv", 1): d_mem_kv1})
    dx1, df0, dg_ffn_post0, dg_mix_pre1 = _post_pre_bwd(x1, f0, gains["norm_ffn_post"][0] + sent, gains["norm_mix_pre"][1],
                                                        dx2, dh2, name="post_ffn0_bwd")
    dh1, d_wgu0, d_wd0 = _ffn_bwd(h1, gu0, act0, w_gu0, w_d0, df0, 0)
    sent = p["grads_ready"]("ffn0", {("ffn_w_gate_up", 0): d_wgu0, ("ffn_w_down", 0): d_wd0})
    dx0, dmo0, dg_mix_post0, dg_ffn_pre0 = _post_pre_bwd(x, mo0, gains["norm_mix_post"][0] + sent, gains["norm_ffn_pre"][0],
                                                         dx1, dh1, name="post_mix0_bwd")
    dcat0 = _mm(dmo0, att_w_out, tb=True, name="att_out_dx")
    g["att_w_out"] = _mm(cat0, dmo0, ta=True, out_dtype=LINK_DTYPE, name="att_out_dw")
    delta, dqm0, dkv0 = _att_bwd_prep(cat0, dcat0, qkvm, kv_mem[0], name="att_bwd_prep")
    d_mem_kv0 = _mm(mem_n[0], dkv0, ta=True, out_dtype=LINK_DTYPE, name="mem_kv_dw0")
    sent = p["grads_ready"]("att_out", {("att_w_out", 0): g["att_w_out"], ("mem_w_kv", 0): d_mem_kv0})
    att_b = [_att_bwd(qkvm, bias[gi] + sent, lse_tot, delta, dcat0, gi, name=f"att_bwd{gi}") for gi in range(3)]
    dqkvm = jnp.concatenate([a[w] for w in range(3) for a in att_b] + [dqm0], axis=1).astype(MXU_DTYPE)
    g["rel_bias"] = sum(_bias_tiles_bwd(p["rel_bias"], att_b[gi][3], gi) for gi in range(3))
    g["att_w_in"] = _mm(h0, dqkvm, ta=True, out_shards=_shards_of(p["att_w_in"]), out_dtype=LINK_DTYPE, name="att_in_dw")
    sent = p["grads_ready"]("att_in", {("att_w_in", 0): g["att_w_in"]})
    dh0 = _mm(dqkvm, p["att_w_in"], tb=True, name="att_in_dx")
    grad_x, dg_mix_pre0 = _pre_norm_bwd(x, gains["norm_mix_pre"][0] + sent, dh0, dx0, name="pre0_bwd")

    d_mem_norm = []
    for i, dkv in enumerate((dkv0, dkv1)):
        d_mem_n = _mm(dkv, mem_w_kv[i], tb=True, name=f"mem_kv_dx{i}")
        d_mem_norm.append(_gain_bwd(mem, gains["mem_norm"][i], d_mem_n, name=f"mem_norm_bwd{i}"))
    g["mem_w_kv"] = [d_mem_kv0, d_mem_kv1]
    g["mem_norm"] = jnp.concatenate(d_mem_norm, axis=0)
    g["norm_mix_pre"] = jnp.concatenate([dg_mix_pre0, dg_mix_pre1], axis=0)
    g["norm_mix_post"] = jnp.concatenate([dg_mix_post0, dg_mix_post1], axis=0)
    g["norm_ffn_pre"] = jnp.concatenate([dg_ffn_pre0, dg_ffn_pre1], axis=0)
    g["norm_ffn_post"] = jnp.concatenate([dg_ffn_post0, dg_ffn_post1], axis=0)
    g["ffn_w_gate_up"] = [d_wgu0, d_wgu1]
    g["ffn_w_down"] = [d_wd0, d_wd1]
    return loss_cols, grad_x, g


N_CHIPS = 4
N_DEV = 8
MESH = pl.DeviceIdType.MESH
BIG = (("att_w_in", (1, 1024, 640), 2), ("att_w_out", (1, 256, 1024), 1), ("dn_w_in", (1, 1024, 838), 2),
       ("dn_w_out", (1, 256, 1024), 1), ("mem_w_kv", (2, 256, 512), 1), ("ffn_w_gate_up", (2, 1024, 1408), 2),
       ("ffn_w_down", (2, 704, 1024), 1))


def _mesh_pos():
    return lax.axis_index("x"), lax.axis_index("y"), lax.axis_index("c")


def _other_chips(x, y):
    return [(1 - x, y), (x, 1 - y), (1 - x, 1 - y)]


ANY = pl.BlockSpec(memory_space=pl.ANY)


def _all_reduce_small(v, *, name):
    rows, cols = v.shape
    flips = [(dx, dy, dc) for dx in (0, 1) for dy in (0, 1) for dc in (0, 1)][1:]

    def body(v_ref, o_ref, buf, send_sems, recv_sems):
        x, y, c = _mesh_pos()

        def peer(f):
            return tuple(1 - p if fl else p for p, fl in zip((x, y, c), f))

        def index(p):
            return 4 * p[0] + 2 * p[1] + p[2]

        buf[index((x, y, c))] = v_ref[...]
        sends = []
        for k, f in enumerate(flips):
            cp = pltpu.make_async_remote_copy(src_ref=v_ref, dst_ref=buf.at[index((x, y, c))], send_sem=send_sems.at[k],
                                              recv_sem=recv_sems.at[k], device_id=peer(f), device_id_type=MESH)
            cp.start()
            sends.append(cp)
        for k, f in enumerate(flips):
            pltpu.make_async_remote_copy(src_ref=v_ref, dst_ref=buf.at[index(peer(f))], send_sem=send_sems.at[k],
                                         recv_sem=recv_sems.at[k], device_id=peer(f), device_id_type=MESH).wait_recv()
        for cp in sends:
            cp.wait_send()
        acc = buf[0]
        for d in range(1, N_DEV):
            acc = acc + buf[d]
        o_ref[...] = acc

    vmem = pl.BlockSpec(memory_space=pltpu.VMEM)
    return pl.pallas_call(
        body, name=name, in_specs=[vmem], out_specs=vmem, out_shape=jax.ShapeDtypeStruct((rows, cols), F32),
        scratch_shapes=[pltpu.VMEM((N_DEV, rows, cols), F32), pltpu.SemaphoreType.DMA((N_DEV - 1,)),
                        pltpu.SemaphoreType.DMA((N_DEV - 1,))],
    )(v)


def _adamw_update(w, g, m, v):
    m = ADAM_B1 * m + (1.0 - ADAM_B1) * g
    v = ADAM_B2 * v + (1.0 - ADAM_B2) * (g * g)
    m_hat = m / (1.0 - ADAM_B1 ** ADAM_STEP)
    v_hat = v / (1.0 - ADAM_B2 ** ADAM_STEP)
    return -ADAM_LR * (m_hat / (jnp.sqrt(v_hat) + ADAM_EPS) + ADAM_WD * w), m, v


def _adamw_many(ws, gs, ms, vs, *, name):
    n = len(ws)

    def kern(*refs):
        for i in range(n):
            outs = _adamw_update(*[refs[k * n + i][...] for k in range(4)])
            for k, val in enumerate(outs):
                refs[(4 + k) * n + i][...] = val

    vmem = pl.BlockSpec(memory_space=pltpu.VMEM)
    res = pl.pallas_call(
        kern, name=name, in_specs=[vmem] * (4 * n), out_specs=[vmem] * (3 * n),
        out_shape=[jax.ShapeDtypeStruct(a.shape, F32) for _ in range(3) for a in ws],
    )(*ws, *gs, *ms, *vs)
    return res[:n], res[n: 2 * n], res[2 * n:]


def _adamw(w, g, m, v, *, name):
    def body(w, g, m, v):
        return _adamw_update(w, g, m, v), ()
    rows, cols = w.shape
    if rows % SUBLANES == 0:
        return _rowwise(body, [w, g, m, v], [], [(cols, F32)] * 3, [], tm=_tile(rows, 256, SUBLANES), name=name)

    def kern(*refs):
        outs, _ = body(*[r[...] for r in refs[:4]])
        for r, val in zip(refs[4:], outs, strict=True):
            r[...] = val

    spec = pl.BlockSpec((rows, _tile(cols, 256, LANES)), lambda j: (0, j))
    return pl.pallas_call(
        kern, name=name, grid=(cols // spec.block_shape[1],), in_specs=[spec] * 4, out_specs=[spec] * 3,
        out_shape=[jax.ShapeDtypeStruct((rows, cols), F32)] * 3, compiler_params=_params("parallel"),
    )(w, g, m, v)


def _pack_small(arrs, rows):
    flat = jnp.concatenate([a.reshape(-1) for a in arrs])
    return jnp.pad(flat, (0, rows * LANES - flat.shape[0])).reshape(rows, LANES)


def _unpack_small(packed, shapes):
    flat = packed.reshape(-1)
    out, off = [], 0
    for s in shapes:
        size = math.prod(s)
        out.append(flat[off: off + size].reshape(s))
        off += size
    return out


def _small_rows(shapes):
    return -(-sum(math.prod(s) for s in shapes) // (SUBLANES * LANES)) * SUBLANES


def _sem_pairs(n):
    return [pltpu.SemaphoreType.DMA((n,)), pltpu.SemaphoreType.DMA((n,))]


def _gather_blocks(blocks, *, name):
    n = len(blocks)

    def body(*refs):
        x_refs, out_refs, (send_sems, recv_sems) = refs[:n], refs[n: 2 * n], refs[2 * n:]
        x, y, c = _mesh_pos()
        sibling = (x, y, 1 - c)
        chips = _other_chips(x, y)

        def copy(k, src, dst, to):
            return pltpu.make_async_remote_copy(src_ref=src, dst_ref=dst, send_sem=send_sems.at[k],
                                                recv_sem=recv_sems.at[k], device_id=to, device_id_type=MESH)

        def part(b, chip, h):
            half = blocks[b].shape[0] // 2
            return out_refs[b].at[2 * chip[0] + chip[1], pl.ds(h * half, half), :]

        def my_half(b):
            half = blocks[b].shape[0] // 2
            return x_refs[b].at[pl.ds(c * half, half), :]

        first = [copy(6 * b + j, my_half(b), part(b, (x, y), c), (*chip, c)) for b in range(n) for j, chip in enumerate(chips)]
        for cp in first:
            cp.start()
        passed = []
        for b in range(n):
            for j, chip in enumerate(chips):
                copy(6 * b + j, my_half(b), part(b, chip, c), (*chip, c)).wait_recv()
                cp = copy(6 * b + 3 + j, part(b, chip, c), part(b, chip, c), sibling)
                cp.start()
                passed.append(cp)
        for b in range(n):
            for j, chip in enumerate(chips):
                copy(6 * b + 3 + j, part(b, chip, 1 - c), part(b, chip, 1 - c), sibling).wait_recv()
        for cp in first + passed:
            cp.wait_send()

    return pl.pallas_call(
        body, name=name, in_specs=[ANY] * n, out_specs=[ANY] * n,
        out_shape=[jax.ShapeDtypeStruct((N_CHIPS, *a.shape), a.dtype) for a in blocks],
        scratch_shapes=_sem_pairs(6 * n),
    )(*blocks)


HBM = pl.BlockSpec(memory_space=pltpu.HBM)
SEM = pl.BlockSpec(memory_space=pltpu.SEMAPHORE)
DATAFLOW = pltpu.SideEffectType.DATAFLOW_SIDE_EFFECTING


def _gather_start(blocks, *, name):
    n = len(blocks)
    lands = [lax.empty((N_CHIPS, *a.shape), a.dtype) for a in blocks]

    def body(*refs):
        x_refs, land_refs, send_sems, recv_sems, token = refs[:n], refs[n: 2 * n], refs[2 * n], refs[2 * n + 1], refs[-1]
        x, y, c = _mesh_pos()
        for b in range(n):
            for j, chip in enumerate(_other_chips(x, y)):
                pltpu.make_async_remote_copy(src_ref=x_refs[b], dst_ref=land_refs[b].at[2 * x + y], send_sem=send_sems.at[3 * b + j],
                                             recv_sem=recv_sems.at[3 * b + j], device_id=(*chip, c), device_id_type=MESH).start()
        token[...] = jnp.zeros_like(token)

    operands = [pltpu.with_memory_space_constraint(a, pltpu.HBM) for a in blocks + lands]
    res = pl.pallas_call(
        body, name=name, in_specs=[HBM] * (2 * n),
        out_shape=(pltpu.SemaphoreType.DMA((3 * n,)), pltpu.SemaphoreType.DMA((3 * n,)),
                   *[pltpu.HBM(a.shape, a.dtype) for a in operands], jax.ShapeDtypeStruct((SUBLANES, LANES), F32)),
        out_specs=(SEM, SEM, *[HBM] * (2 * n), pl.BlockSpec(memory_space=pltpu.VMEM)),
        input_output_aliases={i: 2 + i for i in range(2 * n)},
        compiler_params=pltpu.CompilerParams(has_side_effects=DATAFLOW),
    )(*operands)
    return res[0], res[1], list(res[2: 2 + n]), list(res[2 + n: 2 + 2 * n]), res[-1]


def _gather_wait(started, after, *, name):
    send_sems, recv_sems, blocks, lands, _ = started
    n = len(blocks)

    def body(*refs):
        x_refs, land_refs, send_sems, recv_sems = refs[:n], refs[n: 2 * n], refs[2 * n], refs[2 * n + 1]
        x, y, c = _mesh_pos()
        for b in range(n):
            for j, chip in enumerate(_other_chips(x, y)):
                cp = pltpu.make_async_remote_copy(src_ref=x_refs[b], dst_ref=land_refs[b].at[2 * chip[0] + chip[1]],
                                                  send_sem=send_sems.at[3 * b + j], recv_sem=recv_sems.at[3 * b + j],
                                                  device_id=(*chip, c), device_id_type=MESH)
                cp.wait_send()
                cp.wait_recv()

    res = pl.pallas_call(
        body, name=name, in_specs=(*[HBM] * (2 * n), SEM, SEM, ANY),
        out_shape=tuple(pltpu.HBM(a.shape, a.dtype) for a in blocks + lands), out_specs=tuple([HBM] * (2 * n)),
        input_output_aliases={i: i for i in range(2 * n)},
        compiler_params=pltpu.CompilerParams(has_side_effects=DATAFLOW),
    )(*blocks, *lands, send_sems, recv_sems, after)
    return list(res[n:])


def _swap_with_sibling(blocks, *, name):
    n = len(blocks)

    def body(*refs):
        in_refs, out_refs, (send_sems, recv_sems) = refs[:n], refs[n: 2 * n], refs[2 * n:]
        x, y, c = _mesh_pos()
        copies = [pltpu.make_async_remote_copy(src_ref=in_refs[b], dst_ref=out_refs[b], send_sem=send_sems.at[b],
                                               recv_sem=recv_sems.at[b], device_id=(x, y, 1 - c), device_id_type=MESH)
                  for b in range(n)]
        for cp in copies:
            cp.start()
        for cp in copies:
            cp.wait()

    return pl.pallas_call(
        body, name=name, in_specs=[ANY] * n, out_specs=[ANY] * n,
        out_shape=[jax.ShapeDtypeStruct(a.shape, a.dtype) for a in blocks], scratch_shapes=_sem_pairs(n),
    )(*blocks)


def _sum_chips_block(parts, *, name):
    n, half, cols = parts.shape
    tm = _tile(half, 512, 2 * SUBLANES)

    def kern(p_ref, o_ref):
        acc = p_ref[0].astype(F32)
        for s in range(1, n):
            acc = acc + p_ref[s].astype(F32)
        o_ref[...] = acc

    return pl.pallas_call(
        kern, name=name, grid=(half // tm,),
        in_specs=[pl.BlockSpec((n, tm, cols), lambda i: (0, i, 0))],
        out_specs=pl.BlockSpec((tm, cols), lambda i: (i, 0)),
        out_shape=jax.ShapeDtypeStruct((half, cols), F32),
        compiler_params=_params("parallel"),
    )(parts)


PEER_FLIPS = [(dx, dy, dc) for dx in (0, 1) for dy in (0, 1) for dc in (0, 1)][1:]


def _flipped(pos, flip):
    return tuple(1 - p if f else p for p, f in zip(pos, flip))


def _device_index(pos):
    return 4 * pos[0] + 2 * pos[1] + pos[2]


def _scatter_start(blocks, *, name):
    n = len(blocks)
    lands = [lax.empty((N_DEV, a.shape[1] // 2, a.shape[2]), a.dtype) for a in blocks]

    def body(*refs):
        g_refs, land_refs, send_sems, recv_sems, token = refs[:n], refs[n: 2 * n], refs[2 * n], refs[2 * n + 1], refs[-1]
        pos = _mesh_pos()
        for b in range(n):
            half = blocks[b].shape[1] // 2
            for k, flip in enumerate(PEER_FLIPS):
                peer = _flipped(pos, flip)
                pltpu.make_async_remote_copy(src_ref=g_refs[b].at[2 * peer[0] + peer[1], pl.ds(peer[2] * half, half), :],
                                             dst_ref=land_refs[b].at[_device_index(pos)],
                                             send_sem=send_sems.at[7 * b + k], recv_sem=recv_sems.at[7 * b + k],
                                             device_id=peer, device_id_type=MESH).start()
        token[...] = jnp.zeros_like(token)

    operands = [pltpu.with_memory_space_constraint(a, pltpu.HBM) for a in blocks + lands]
    res = pl.pallas_call(
        body, name=name, in_specs=[HBM] * (2 * n),
        out_shape=(pltpu.SemaphoreType.DMA((7 * n,)), pltpu.SemaphoreType.DMA((7 * n,)),
                   *[pltpu.HBM(a.shape, a.dtype) for a in operands], jax.ShapeDtypeStruct((SUBLANES, LANES), F32)),
        out_specs=(SEM, SEM, *[HBM] * (2 * n), pl.BlockSpec(memory_space=pltpu.VMEM)),
        input_output_aliases={i: 2 + i for i in range(2 * n)},
        compiler_params=pltpu.CompilerParams(has_side_effects=DATAFLOW),
    )(*operands)
    return res[0], res[1], list(res[2: 2 + n]), list(res[2 + n: 2 + 2 * n]), res[-1]


def _scatter_wait(started, after, *, name):
    send_sems, recv_sems, blocks, lands, _ = started
    n = len(blocks)

    def body(*refs):
        g_refs, land_refs, send_sems, recv_sems = refs[:n], refs[n: 2 * n], refs[2 * n], refs[2 * n + 1]
        pos = _mesh_pos()
        for b in range(n):
            half = blocks[b].shape[1] // 2
            for k, flip in enumerate(PEER_FLIPS):
                peer = _flipped(pos, flip)
                cp = pltpu.make_async_remote_copy(src_ref=g_refs[b].at[0, pl.ds(0, half), :],
                                                  dst_ref=land_refs[b].at[_device_index(peer)],
                                                  send_sem=send_sems.at[7 * b + k], recv_sem=recv_sems.at[7 * b + k],
                                                  device_id=peer, device_id_type=MESH)
                cp.wait_send()
                cp.wait_recv()

    res = pl.pallas_call(
        body, name=name, in_specs=(*[HBM] * (2 * n), SEM, SEM, ANY),
        out_shape=tuple(pltpu.HBM(a.shape, a.dtype) for a in blocks + lands), out_specs=tuple([HBM] * (2 * n)),
        input_output_aliases={i: i for i in range(2 * n)},
        compiler_params=pltpu.CompilerParams(has_side_effects=DATAFLOW),
    )(*blocks, *lands, send_sems, recv_sems, after)
    return list(res[:n]), list(res[n:])


def _reduce_finish(begun, names, after):
    x, y, c = _mesh_pos()
    mine = {}
    for key, started in begun.items():
        blocks, lands = _scatter_wait(started, after, name=f"rs_scatter_wait_{key}")
        parts = []
        for blk, land in zip(blocks, lands, strict=True):
            half = blk.shape[1] // 2
            own = lax.dynamic_slice(blk, (2 * x + y, c * half, 0), (1, half, blk.shape[2]))
            parts.append(lax.dynamic_update_slice(land, own, (_device_index((x, y, c)), 0, 0)))
        mine[key] = [_sum_chips_block(p, name=f"rs_sum_{nm}") for p, nm in zip(parts, names[key], strict=True)]
    flat = [a for key in begun for a in mine[key]]
    other = iter(_swap_with_sibling(flat, name="rs_join"))
    return {key: [jnp.concatenate([jnp.where(c == 0, a, b), jnp.where(c == 0, b, a)], axis=0)
                  for a, b in ((a, next(other)) for a in mine[key])] for key in begun}


WEIGHTS = ("rel_bias", "att_w_in", "att_w_out", "dn_w_in", "dn_conv", "dn_a_log", "dn_dt_bias", "dn_out_norm", "dn_w_out",
           "mem_norm", "mem_w_kv", "norm_mix_pre", "norm_mix_post", "norm_ffn_pre", "norm_ffn_post", "ffn_w_gate_up",
           "ffn_w_down")
BIG_NAMES = tuple(n for n, _, _ in BIG)
SMALL_NAMES = tuple(n for n in WEIGHTS if n not in BIG_NAMES)
CONV_COLS = 3 * TOK_WIDTH
CONV_SHARD = CONV_COLS // N_CHIPS
BLOCKS = tuple((n, layer) for n, shape, _ in BIG for layer in range(shape[0]))
COLUMN_SHARDED = {n: axis == 2 for n, _, axis in BIG}


def kernel(x, mem, rel_bias, att_w_in, att_w_out, dn_w_in, dn_conv, dn_a_log, dn_dt_bias, dn_out_norm, dn_w_out, mem_norm, mem_w_kv, norm_mix_pre, norm_mix_post, norm_ffn_pre, norm_ffn_post, ffn_w_gate_up, ffn_w_down, loss_target, m_rel_bias, m_att_w_in, m_att_w_out, m_dn_w_in, m_dn_conv, m_dn_a_log, m_dn_dt_bias, m_dn_out_norm, m_dn_w_out, m_mem_norm, m_mem_w_kv, m_norm_mix_pre, m_norm_mix_post, m_norm_ffn_pre, m_norm_ffn_post, m_ffn_w_gate_up, m_ffn_w_down, v_rel_bias, v_att_w_in, v_att_w_out, v_dn_w_in, v_dn_conv, v_dn_a_log, v_dn_dt_bias, v_dn_out_norm, v_dn_w_out, v_mem_norm, v_mem_w_kv, v_norm_mix_pre, v_norm_mix_post, v_norm_ffn_pre, v_norm_ffn_post, v_ffn_w_gate_up, v_ffn_w_down):
    w = dict(zip(WEIGHTS, (rel_bias, att_w_in, att_w_out, dn_w_in, dn_conv, dn_a_log, dn_dt_bias, dn_out_norm, dn_w_out,
                           mem_norm, mem_w_kv, norm_mix_pre, norm_mix_post, norm_ffn_pre, norm_ffn_post, ffn_w_gate_up,
                           ffn_w_down)))
    m = dict(zip(WEIGHTS, (m_rel_bias, m_att_w_in, m_att_w_out, m_dn_w_in, m_dn_conv, m_dn_a_log, m_dn_dt_bias,
                           m_dn_out_norm, m_dn_w_out, m_mem_norm, m_mem_w_kv, m_norm_mix_pre, m_norm_mix_post,
                           m_norm_ffn_pre, m_norm_ffn_post, m_ffn_w_gate_up, m_ffn_w_down)))
    v = dict(zip(WEIGHTS, (v_rel_bias, v_att_w_in, v_att_w_out, v_dn_w_in, v_dn_conv, v_dn_a_log, v_dn_dt_bias,
                           v_dn_out_norm, v_dn_w_out, v_mem_norm, v_mem_w_kv, v_norm_mix_pre, v_norm_mix_post,
                           v_norm_ffn_pre, v_norm_ffn_post, v_ffn_w_gate_up, v_ffn_w_down)))
    cx, cy, cc = _mesh_pos()
    chip = 2 * cx + cy

    local = dict(zip(BLOCKS, lax.optimization_barrier(
        [(w[n][layer].T if n == "dn_w_in" else w[n][layer]).astype(MXU_DTYPE) for n, layer in BLOCKS]), strict=True))

    def usable(block, got):
        got = lax.dynamic_update_slice(got, local[block][None], (chip, 0, 0))
        return got if COLUMN_SHARDED[block[0]] else got.reshape(-1, got.shape[-1])

    late = {"att_rest": [("att_w_out", 0), ("mem_w_kv", 0), ("mem_w_kv", 1)],
            "ffn0": [("ffn_w_gate_up", 0), ("ffn_w_down", 0)], "dn": [("dn_w_in", 0), ("dn_w_out", 0)],
            "ffn1": [("ffn_w_gate_up", 1), ("ffn_w_down", 1)]}
    first = [b for b in BLOCKS if all(b not in blks for blks in late.values())]
    first_got = _gather_blocks([local[b] for b in first], name="gather_weights")
    late_local, _ = lax.optimization_barrier(({k: [local[b] for b in blks] for k, blks in late.items()}, first_got[0]))
    started = {k: _gather_start(late_local[k], name=f"gather_start_{k}") for k in late}
    started_token = sum(s[4][0, 0] for s in started.values())

    def late_weights(key, after):
        lands = _gather_wait(started[key], after, name=f"gather_wait_{key}")
        return [usable(b, got) for b, got in zip(late[key], lands, strict=True)]

    def dn_weights(after):
        w_in, w_out = late_weights("dn", after)
        return _pad_dn_w_in(jnp.concatenate([w_in[s].T for s in range(N_CHIPS)], axis=1)), w_out

    full = {}
    for b, got in zip(first, first_got, strict=True):
        full.setdefault(b[0], []).append(usable(b, got))
    conv_rows = _small_rows([(DN_CONV, CONV_COLS)])
    conv_mine = jnp.where(cc == 0, 1.0, 0.0) * w["dn_conv"][0]
    conv_placed = lax.dynamic_update_slice(jnp.zeros((DN_CONV, CONV_COLS), F32), conv_mine, (0, chip * CONV_SHARD))
    conv_full = _unpack_small(_all_reduce_small(_pack_small([conv_placed], conv_rows), name="gather_conv"),
                              [(DN_CONV, CONV_COLS)])[0]
    p = {
        "rel_bias": w["rel_bias"], "att_w_in": full["att_w_in"][0], "att_rest_weights": lambda after: late_weights("att_rest", after),
        "dn_conv": conv_full, "dn_a_log": w["dn_a_log"][0], "dn_dt_bias": w["dn_dt_bias"][0],
        "dn_out_norm": w["dn_out_norm"][0], "mem_norm": w["mem_norm"],
        "norm_mix_pre": w["norm_mix_pre"] + started_token,
        "norm_mix_post": w["norm_mix_post"], "norm_ffn_pre": w["norm_ffn_pre"], "norm_ffn_post": w["norm_ffn_post"],
        "ffn_weights": lambda layer, after: late_weights(f"ffn{layer}", after), "dn_weights": dn_weights,
    }

    def chip_blocks(n, a):
        if n == "dn_w_in":
            a = _unpad_dn_w_in(a)
        if a.ndim == 3:
            return a
        if COLUMN_SHARDED[n]:
            return a.reshape(a.shape[0], N_CHIPS, -1).transpose(1, 0, 2)
        return a.reshape(N_CHIPS, -1, a.shape[-1])

    begun, begun_blocks = {}, {}

    def grads_ready(key, layer_grads):
        begun_blocks[key] = list(layer_grads)
        begun[key] = _scatter_start([chip_blocks(n, a) for (n, _), a in layer_grads.items()], name=f"rs_scatter_start_{key}")
        return begun[key][4][0, 0]

    p["grads_ready"] = grads_ready
    loss_cols, grad_x, g = _local_step(x[0], mem[0], loss_target[0], p)
    finished = _reduce_finish(begun, {k: [f"{n}{layer}" for n, layer in blks] for k, blks in begun_blocks.items()}, grad_x)
    reduced = {b: r for k in begun for b, r in zip(begun_blocks[k], finished[k], strict=True)}
    grads = {n: jnp.concatenate([reduced[b] for b in BLOCKS if b[0] == n], axis=0).reshape(shape) for n, shape, _ in BIG}
    small_full_shapes = [(DN_CONV, CONV_COLS) if n == "dn_conv" else w[n].shape for n in SMALL_NAMES] + [(1,)]
    small_sum = _all_reduce_small(_pack_small([g[n] for n in SMALL_NAMES] + [jnp.sum(loss_cols).reshape(1)],
                                              _small_rows(small_full_shapes)), name="reduce_small")
    *small_grads, loss = _unpack_small(small_sum, small_full_shapes)
    loss = loss[0]
    for n, s in zip(SMALL_NAMES, small_grads, strict=True):
        grads[n] = lax.dynamic_slice(s, (0, chip * CONV_SHARD), (DN_CONV, CONV_SHARD))[None] if n == "dn_conv" else s

    delta, new_m, new_v = {}, {}, {}
    for n in BIG_NAMES:
        shape = w[n].shape
        two_d = (lambda a: a[0].T) if n == "dn_w_in" else (lambda a: a.reshape(-1, shape[-1]))
        back = (lambda a: a.T[None]) if n == "dn_w_in" else (lambda a: a.reshape(shape))
        g_2d = two_d(grads[n])
        res = _adamw(two_d(w[n]), g_2d, two_d(m[n]), two_d(v[n]), name=f"adamw_{n}")
        grads[n], delta[n], new_m[n], new_v[n] = (back(r) for r in (g_2d, *res))
    res = _adamw_many(*[[d[n].reshape(-1, d[n].shape[-1]) for n in SMALL_NAMES] for d in (w, grads, m, v)], name="adamw_small")
    for d, outs in zip((delta, new_m, new_v), res):
        for n, a in zip(SMALL_NAMES, outs):
            d[n] = a.reshape(w[n].shape)
    return (loss, grad_x[None], *[grads[n] for n in WEIGHTS], *[delta[n] for n in WEIGHTS],
            *[new_m[n] for n in WEIGHTS], *[new_v[n] for n in WEIGHTS])
```

```python
import functools
import math

import numpy as np
import jax
import jax.numpy as jnp
from jax import lax
from jax.experimental import pallas as pl
from jax.experimental.pallas import tpu as pltpu

F32 = jnp.float32
MXU_DTYPE = jnp.bfloat16
LINK_DTYPE = jnp.bfloat16
HI = lax.Precision.HIGHEST

EPS = 1e-6
NEG_INF = -1e30
LANES = 128
SUBLANES = 8
VMEM_LIMIT = 56 * 1024 * 1024
MM_WHOLE_K_BUDGET = 44 * 1024 * 1024

D_MODEL = 1024
TOK_WIDTH = 768
MEM_WIDTH = 256
MEM_LEN = 256
ATT_HEAD_DIM = 64
DILATIONS = (1, 4, 16)
HALF = 64
ATT_BQ = 128
ATT_W = ATT_BQ + 2 * HALF
REL_BUCKETS = 32
REL_MAX_DIST = 1024
DN_HEADS = 6
DN_HEAD_DIM = 128
DN_CONV = 5
DN_CHUNK = 128
D_FF = 2816
DN_IN = 3352
DN_IN_PAD = 3456
N_GATES = 4 * DN_HEADS

ADAM_LR = 0.001
ADAM_B1 = 0.9
ADAM_B2 = 0.999
ADAM_EPS = 1e-08
ADAM_WD = 0.01
ADAM_STEP = 10


def _tile(n, target, align):
    if n <= target:
        return n
    t = (target // align) * align
    while t >= align:
        if n % t == 0:
            return t
        t -= align
    raise ValueError(f"no tile for {n} (target {target}, align {align})")


def _params(*sem):
    return pltpu.CompilerParams(dimension_semantics=sem, vmem_limit_bytes=VMEM_LIMIT)


def _mm(a, b, *, name, ta=False, tb=False, out_shards=None, tm=1408, tn=1408, tk=1408, out_dtype=F32):
    if ta:
        K, M = a.shape
    else:
        M, K = a.shape
    sharded_b = b.ndim == 3
    if sharded_b:
        n_sh, b_rows, b_cols = b.shape
        N, K2 = (b_rows, n_sh * b_cols) if tb else (n_sh * b_cols, b_rows)
    else:
        N, K2 = b.shape if tb else b.shape[::-1]
    assert K == K2, (a.shape, b.shape, ta, tb)
    tm = _tile(M, tm, LANES if ta else SUBLANES)
    tn = N // out_shards if out_shards else (b_cols if sharded_b and not tb else _tile(N, tn, LANES))
    tk = b_cols if sharded_b and tb else _tile(K, tk, LANES)

    def vmem_bytes(tm_, tk_):
        return 2 * (tm_ * tk_ * a.dtype.itemsize + tk_ * tn * b.dtype.itemsize + tm_ * tn * jnp.dtype(out_dtype).itemsize)

    for rows in (tm, tm // 2):
        if not (sharded_b and tb) and M % rows == 0 and rows % LANES == 0 and vmem_bytes(rows, K) <= MM_WHOLE_K_BUDGET:
            tm, tk = rows, K
            break
    nk = K // tk
    a_spec = pl.BlockSpec((tk, tm), lambda i, j, k: (k, i)) if ta else pl.BlockSpec((tm, tk), lambda i, j, k: (i, k))
    if sharded_b:
        b_spec = (pl.BlockSpec((None, tn, tk), lambda i, j, k: (k, j, 0)) if tb
                  else pl.BlockSpec((None, tk, tn), lambda i, j, k: (j, k, 0)))
    else:
        b_spec = pl.BlockSpec((tn, tk), lambda i, j, k: (j, k)) if tb else pl.BlockSpec((tk, tn), lambda i, j, k: (k, j))
    if out_shards:
        out_spec = pl.BlockSpec((None, tm, tn), lambda i, j, k: (j, i, 0))
        out_shape = jax.ShapeDtypeStruct((out_shards, M, tn), out_dtype)
    else:
        out_spec = pl.BlockSpec((tm, tn), lambda i, j, k: (i, j))
        out_shape = jax.ShapeDtypeStruct((M, N), out_dtype)
    dims = (((0 if ta else 1,), (1 if tb else 0,)), ((), ()))

    def product(a_ref, b_ref):
        return lax.dot_general(a_ref[...].astype(MXU_DTYPE), b_ref[...].astype(MXU_DTYPE), dims, preferred_element_type=F32)

    def kern_whole(a_ref, b_ref, o_ref):
        o_ref[...] = product(a_ref, b_ref).astype(o_ref.dtype)

    def kern_steps(a_ref, b_ref, o_ref, acc_ref):
        k = pl.program_id(2)

        @pl.when(k == 0)
        def _():
            acc_ref[...] = jnp.zeros_like(acc_ref)

        acc_ref[...] += product(a_ref, b_ref)

        @pl.when(k == nk - 1)
        def _():
            o_ref[...] = acc_ref[...].astype(o_ref.dtype)

    return pl.pallas_call(
        kern_whole if nk == 1 else kern_steps, name=name, grid=(M // tm, N // tn, nk), in_specs=[a_spec, b_spec],
        out_specs=out_spec, out_shape=out_shape,
        scratch_shapes=[] if nk == 1 else [pltpu.VMEM((tm, tn), F32)],
        compiler_params=_params("parallel", "parallel", "arbitrary"),
    )(a, b)


def _shards_of(w):
    return w.shape[0] if w.ndim == 3 else None


def _col(arr, width, blk):
    return (arr, width, blk)


def _rowwise(body, rows, consts, out_rows, out_acc, *, tm, name):
    n_rows = (rows[0][0] if isinstance(rows[0], tuple) else rows[0]).shape[0]
    assert n_rows % tm == 0, (n_rows, tm)
    arrs, in_specs = [], []
    for r in rows:
        arr, width, blk = r if isinstance(r, tuple) else (r, r.shape[1], 0)
        assert arr.shape[0] == n_rows
        arrs.append(arr)
        in_specs.append(pl.BlockSpec((tm, width), functools.partial(lambda i, b: (i, b), b=blk)))
    for c in consts:
        arrs.append(c)
        in_specs.append(pl.BlockSpec(c.shape, functools.partial(lambda i, n: (0,) * n, n=c.ndim)))
    n_in, n_ro = len(arrs), len(out_rows)
    out_shape = [jax.ShapeDtypeStruct((n_rows, w), dt) for w, dt in out_rows]
    out_specs = [pl.BlockSpec((tm, w), lambda i: (i, 0)) for w, _ in out_rows]
    out_shape += [jax.ShapeDtypeStruct(s, F32) for s in out_acc]
    out_specs += [pl.BlockSpec(s, lambda i: (0, 0)) for s in out_acc]

    def kern(*refs):
        ro, ao = body(*[r[...] for r in refs[:n_in]])
        outs = refs[n_in:]
        for r, v in zip(outs[:n_ro], ro, strict=True):
            r[...] = v.astype(r.dtype)
        if out_acc:
            @pl.when(pl.program_id(0) == 0)
            def _():
                for r in outs[n_ro:]:
                    r[...] = jnp.zeros_like(r)

            for r, v in zip(outs[n_ro:], ao, strict=True):
                r[...] += v

    res = pl.pallas_call(
        kern, name=name, grid=(n_rows // tm,), in_specs=in_specs, out_specs=out_specs, out_shape=out_shape,
        compiler_params=_params("arbitrary" if out_acc else "parallel"),
    )(*arrs)
    return res


def _rms(x, gain):
    return x * lax.rsqrt(jnp.mean(x * x, axis=-1, keepdims=True) + EPS) * gain


def _silu(x):
    return x * jax.nn.sigmoid(x)


def _softplus(x):
    return jnp.maximum(x, 0.0) + jnp.log(1.0 + jnp.exp(-jnp.abs(x)))


def _dot_nt(a, b, precision=None):
    return lax.dot_general(a, b, (((1,), (1,)), ((), ())), preferred_element_type=F32, precision=precision)


def _dot_tn(a, b, precision=None):
    return lax.dot_general(a, b, (((0,), (0,)), ((), ())), preferred_element_type=F32, precision=precision)


def _dot(a, b, precision=None):
    return jnp.dot(a, b, preferred_element_type=F32, precision=precision)


def _pre_norm(x, gain, *, name):
    def body(x, g):
        return (_rms(x, g),), ()
    return _rowwise(body, [x], [gain], [(x.shape[1], MXU_DTYPE)], [], tm=_tile(x.shape[0], 512, 2 * SUBLANES), name=name)[0]


def _pre_norm_bwd(x, gain, dh, dx_other, *, name):
    def body(x, dh, dxo, g):
        _, vjp = jax.vjp(_rms, x, g)
        dx, dg = vjp(dh)
        return (dx + dxo,), (dg,)
    return _rowwise(body, [x, dh, dx_other], [gain], [(x.shape[1], F32)], [gain.shape], tm=512, name=name)


def _gain_bwd(x, gain, dh, *, name):
    def body(x, dh, g):
        _, vjp = jax.vjp(lambda g_: _rms(x, g_), g)
        return (), (vjp(dh)[0],)
    return _rowwise(body, [x, dh], [gain], [], [gain.shape], tm=_tile(x.shape[0], 512, SUBLANES), name=name)[0]


def _res_block(x_res, m, g_post, g_pre):
    x_new = x_res + _rms(m, g_post)
    return x_new, _rms(x_new, g_pre)


def _post_pre(x_res, m, g_post, g_pre, *, name):
    def body(x, m, gp, gq):
        return _res_block(x, m, gp, gq), ()
    d = x_res.shape[1]
    return _rowwise(body, [x_res, m], [g_post, g_pre], [(d, F32), (d, MXU_DTYPE)], [], tm=512, name=name)


def _post_pre_bwd(x_res, m, g_post, g_pre, dx_new, dh, *, name):
    def body(x, m, dxn, dh, gp, gq):
        _, vjp = jax.vjp(_res_block, x, m, gp, gq)
        dx, dm, dgp, dgq = vjp((dxn, dh))
        return (dx, dm), (dgp, dgq)
    d = x_res.shape[1]
    return _rowwise(body, [x_res, m, dx_new, dh], [g_post, g_pre], [(d, F32), (d, MXU_DTYPE)],
                    [g_post.shape, g_pre.shape], tm=512, name=name)


def _final_loss_bwd(x_res, m, g_post, target, *, name):
    d = x_res.shape[1]

    def loss_cols(x, m, g, t):
        err = x + _rms(m, g) - t
        return jnp.sum(err * err, axis=0, keepdims=True) * (0.5 / d)

    def body(x, m, t, g):
        cols, vjp = jax.vjp(lambda x_, m_, g_: loss_cols(x_, m_, g_, t), x, m, g)
        dx, dm, dg = vjp(jnp.ones_like(cols))
        return (dx, dm), (dg, cols)
    return _rowwise(body, [x_res, m, target], [g_post], [(d, F32), (d, MXU_DTYPE)], [g_post.shape, (1, d)], tm=256, name=name)


def _swiglu_act(gu, *, name):
    def body(gate, up):
        return (_silu(gate.astype(F32)) * up.astype(F32),), ()
    return _rowwise(body, [_col(gu, D_FF, 0), _col(gu, D_FF, 1)], [], [(D_FF, MXU_DTYPE)], [], tm=256, name=name)[0]


def _swiglu_act_bwd(gu, da, *, name):
    def body(gate, up, da):
        _, vjp = jax.vjp(lambda g, u: _silu(g) * u, gate.astype(F32), up.astype(F32))
        dg, du = vjp(da.astype(F32))
        return (jnp.concatenate([dg, du], axis=1),), ()
    return _rowwise(body, [_col(gu, D_FF, 0), _col(gu, D_FF, 1), da], [], [(2 * D_FF, MXU_DTYPE)], [], tm=256, name=name)[0]


def _lane_head_mask(width, head_dim, head):
    lane = lax.broadcasted_iota(jnp.int32, (1, width), 1)
    return (lane // head_dim) == head


def _mem_attn_heads(q4, k4, v4):
    logits = _bdot_nt(q4, k4)
    p = jnp.exp(logits - jnp.max(logits, axis=-1, keepdims=True))
    return _bdot(p / jnp.sum(p, axis=-1, keepdims=True), v4)


def _mem_heads(q_mem, kv):
    return _heads(q_mem * (ATT_HEAD_DIM ** -0.5), mask=True), _heads(kv[:, :MEM_WIDTH]), _heads(kv[:, MEM_WIDTH:])


def _mem_attn(q_mem, kv):
    return _join_heads(_mem_attn_heads(*_mem_heads(q_mem, kv)))


def _mem_attn_bwd(q_mem, kv, do):
    _, vjp = jax.vjp(_mem_attn_heads, *_mem_heads(q_mem, kv))
    dq4, dk4, dv4 = vjp(_heads(do, mask=True))
    return (_join_heads(dq4) * (ATT_HEAD_DIM ** -0.5),
            jnp.concatenate([dk4[0] + dk4[1], dk4[2] + dk4[3], dv4[0] + dv4[1], dv4[2] + dv4[3]], axis=1))


def _t5_bucket(rel):
    half = REL_BUCKETS // 2
    max_exact = half // 2
    n = np.abs(rel)
    large = max_exact + (np.log(np.maximum(n, 1) / max_exact) / math.log(REL_MAX_DIST / max_exact)
                         * (half - max_exact)).astype(np.int64)
    large = np.minimum(large, half - 1)
    return ((rel > 0) * half + np.where(n < max_exact, n, large)).astype(np.int32)


ATT_DIAGS = ATT_BQ + ATT_W - 1


def _bias_diag_onehot(dil):
    j = np.arange(ATT_DIAGS)
    tiles = []
    for off in (-HALF, 0, HALF):
        rel = j - (ATT_BQ - 1) - HALF - off
        hot = _t5_bucket(rel * dil)[:, None] == np.arange(REL_BUCKETS)[None, :]
        tiles.append(hot & (np.abs(rel) <= HALF)[:, None])
    return np.stack(tiles).astype(np.float32)


def _toeplitz(r):
    lead = r.shape[:-1]
    a = jnp.broadcast_to(r[..., None, :], lead + (ATT_BQ, ATT_DIAGS))
    a = jnp.pad(a, [(0, 0)] * len(lead) + [(0, 0), (0, 1)])
    a = a.reshape(lead + (ATT_BQ * (ATT_DIAGS + 1),))[..., : ATT_BQ * ATT_DIAGS].reshape(lead + (ATT_BQ, ATT_DIAGS))
    return a[..., ATT_BQ - 1: ATT_BQ - 1 + ATT_W]


def _bias_tiles(rel_bias):
    onehot = jnp.asarray(np.stack([_bias_diag_onehot(dil) for dil in DILATIONS]))
    heads = rel_bias.reshape(REL_BUCKETS, len(DILATIONS), ATT_GROUP_HEADS)
    return _toeplitz(jnp.einsum('gtnb,bgh->gthn', onehot, heads, precision=HI))


def _bias_tiles_bwd(rel_bias, dtiles):
    return jax.vjp(_bias_tiles, rel_bias)[1](dtiles)[0]


def _att_window(i, n_sub):
    start = jnp.clip(i * ATT_BQ - HALF, 0, n_sub - ATT_W)
    off = i * ATT_BQ - HALF - start
    return pl.multiple_of(start, HALF), off


def _att_valid(off):
    q = lax.broadcasted_iota(jnp.int32, (ATT_BQ, ATT_W), 0)
    kk = lax.broadcasted_iota(jnp.int32, (ATT_BQ, ATT_W), 1)
    return jnp.abs(kk - q - HALF - off) <= HALF


def _att_tile_id(i, nq):
    return jnp.where(i == 0, 0, jnp.where(i == nq - 1, 2, 1))


ATT_GROUP_HEADS = 4


def _heads(x, mask=False):
    out = []
    for p in range(2):
        pair = x[:, p * LANES: (p + 1) * LANES]
        for h in range(2):
            out.append(jnp.where(_lane_head_mask(LANES, ATT_HEAD_DIM, h), pair, 0.0) if mask else pair)
    return jnp.stack(out)


def _join_heads(x):
    first = _lane_head_mask(LANES, ATT_HEAD_DIM, 0)
    return jnp.concatenate([jnp.where(first, x[2 * p], x[2 * p + 1]) for p in range(2)], axis=1)


def _head_scalar(x):
    out = []
    for p in range(2):
        pair = x[:, p * LANES: (p + 1) * LANES]
        for h in range(2):
            out.append(jnp.max(jnp.where(_lane_head_mask(LANES, ATT_HEAD_DIM, h), pair, NEG_INF), axis=-1, keepdims=True))
    return jnp.stack(out)


def _bdot(a, b):
    return jnp.einsum('hqk,hkd->hqd', a, b, preferred_element_type=F32)


def _bdot_nt(a, b):
    return jnp.einsum('hqd,hkd->hqk', a, b, preferred_element_type=F32)


def _bdot_tn(a, b):
    return jnp.einsum('hqk,hqd->hkd', a, b, preferred_element_type=F32)


def _residue_view(arr, col_blocks, dil):
    if dil == 1:
        return arr, lambda r, j: col_blocks[j]
    width = 2 * LANES
    picked = jnp.concatenate([arr[:, b * width: (b + 1) * width] for b in col_blocks], axis=1)
    n = len(col_blocks)
    return picked.reshape(arr.shape[0] // dil, dil * n * width), lambda r, j: r * n + j


def _att_fwd(qkvm, bias, gi, *, name):
    dil = DILATIONS[gi]
    s_len = qkvm.shape[0]
    n_sub = s_len // dil
    nq = n_sub // ATT_BQ
    assert n_sub % ATT_BQ == 0 and n_sub >= ATT_W
    view, at = _residue_view(qkvm, [gi, 3 + gi, 6 + gi], dil)

    def kern(q_ref, k_ref, v_ref, b_ref, o_ref, lse_ref):
        start, off = _att_window(pl.program_id(1), n_sub)
        valid = _att_valid(off)
        q4 = _heads(q_ref[...].astype(F32) * (ATT_HEAD_DIM ** -0.5), mask=True)
        k4 = _heads(k_ref[pl.ds(start, ATT_W), :].astype(F32))
        v4 = _heads(v_ref[pl.ds(start, ATT_W), :].astype(F32))
        s = jnp.where(valid, _bdot_nt(q4, k4) + b_ref[...], NEG_INF)
        mx = jnp.max(s, axis=-1, keepdims=True)
        p = jnp.exp(s - mx)
        den = jnp.sum(p, axis=-1, keepdims=True)
        o_ref[...] = _join_heads(_bdot(p, v4) / den)
        lse_ref[...] = _join_heads(jnp.broadcast_to(mx + jnp.log(den), (ATT_GROUP_HEADS, ATT_BQ, LANES)))

    def qkv_spec(which, full):
        shape = (n_sub, 2 * LANES) if full else (ATT_BQ, 2 * LANES)
        return pl.BlockSpec(shape, lambda r, i: (0 if full else i, at(r, which)))

    out_spec = pl.BlockSpec((ATT_BQ, 2 * LANES), lambda r, i: (i, r))
    o, lse = pl.pallas_call(
        kern, name=name, grid=(dil, nq),
        in_specs=[qkv_spec(0, False), qkv_spec(1, True), qkv_spec(2, True),
                  pl.BlockSpec((None, ATT_GROUP_HEADS, ATT_BQ, ATT_W), lambda r, i: (_att_tile_id(i, nq), 0, 0, 0))],
        out_specs=[out_spec, out_spec],
        out_shape=[jax.ShapeDtypeStruct((n_sub, dil * 2 * LANES), F32)] * 2,
        compiler_params=_params("parallel", "arbitrary"),
    )(view, view, view, bias)
    return o.reshape(s_len, 2 * LANES), lse.reshape(s_len, 2 * LANES)


def _att_bwd(qkvm, bias, lse_tot, delta, dcat, gi, *, name):
    dil = DILATIONS[gi]
    s_len = qkvm.shape[0]
    n_sub = s_len // dil
    nq = n_sub // ATT_BQ
    view, at = _residue_view(qkvm, [gi, 3 + gi, 6 + gi], dil)
    lse_v = lse_tot.reshape(n_sub, dil * 2 * LANES)
    delta_v = delta.reshape(n_sub, dil * 2 * LANES)
    dcat_v, dcat_at = _residue_view(dcat, [gi], dil)

    def kern(q_ref, k_ref, v_ref, b_ref, lse_ref, dl_ref, dm_ref, dq_ref, dk_ref, dv_ref, db_ref):
        r, i = pl.program_id(0), pl.program_id(1)
        start, off = _att_window(i, n_sub)
        valid = _att_valid(off)
        tile = _att_tile_id(i, nq)

        @pl.when(i == 0)
        def _():
            dk_ref[...] = jnp.zeros_like(dk_ref)
            dv_ref[...] = jnp.zeros_like(dv_ref)

        @pl.when((i == 0) & (r == 0))
        def _():
            db_ref[...] = jnp.zeros_like(db_ref)

        q4 = _heads(q_ref[...].astype(F32) * (ATT_HEAD_DIM ** -0.5), mask=True)
        k4 = _heads(k_ref[pl.ds(start, ATT_W), :].astype(F32))
        v4 = _heads(v_ref[pl.ds(start, ATT_W), :].astype(F32))
        dm4 = _heads(dm_ref[...], mask=True)
        s = jnp.where(valid, _bdot_nt(q4, k4) + b_ref[tile], NEG_INF)
        p = jnp.exp(s - _head_scalar(lse_ref[...]))
        ds = p * (_bdot_nt(dm4, v4) - _head_scalar(dl_ref[...]))
        dq_ref[...] = _join_heads(_bdot(ds, k4)) * (ATT_HEAD_DIM ** -0.5)
        dk4 = _bdot_tn(ds, q4)
        dv4 = _bdot_tn(p, dm4)
        dk_ref[pl.ds(start, ATT_W), :] += jnp.concatenate([dk4[0] + dk4[1], dk4[2] + dk4[3]], axis=1)
        dv_ref[pl.ds(start, ATT_W), :] += jnp.concatenate([dv4[0] + dv4[1], dv4[2] + dv4[3]], axis=1)
        db_ref[tile] += ds

    def qkv_spec(which, full):
        shape = (n_sub, 2 * LANES) if full else (ATT_BQ, 2 * LANES)
        return pl.BlockSpec(shape, lambda r, i: (0 if full else i, at(r, which)))

    blk = pl.BlockSpec((ATT_BQ, 2 * LANES), lambda r, i: (i, r))
    full = pl.BlockSpec((n_sub, 2 * LANES), lambda r, i: (0, r))
    bias_spec = pl.BlockSpec(bias.shape, lambda r, i: (0, 0, 0, 0))
    sub = jax.ShapeDtypeStruct((n_sub, dil * 2 * LANES), F32)
    dq, dk, dv, db = pl.pallas_call(
        kern, name=name, grid=(dil, nq),
        in_specs=[qkv_spec(0, False), qkv_spec(1, True), qkv_spec(2, True), bias_spec, blk, blk,
                  pl.BlockSpec((ATT_BQ, 2 * LANES), lambda r, i: (i, dcat_at(r, 0)))],
        out_specs=[blk, full, full, bias_spec],
        out_shape=[sub, sub, sub, jax.ShapeDtypeStruct(bias.shape, F32)],
        compiler_params=_params("arbitrary", "arbitrary"),
    )(view, view, view, bias, lse_v, delta_v, dcat_v)
    return dq.reshape(s_len, -1), dk.reshape(s_len, -1), dv.reshape(s_len, -1), db


def _att_combine(o_g, lse_g, qkvm, kv_mem, *, name):
    def body(o0, o1, o2, l0, l1, l2, qm, kv):
        mx = jnp.maximum(jnp.maximum(l0, l1), l2)
        tot = mx + jnp.log(jnp.exp(l0 - mx) + jnp.exp(l1 - mx) + jnp.exp(l2 - mx))
        mixed = [o * jnp.exp(l - tot) for o, l in ((o0, l0), (o1, l1), (o2, l2))]
        return (jnp.concatenate(mixed + [_mem_attn(qm.astype(F32), kv)], axis=1), tot), ()
    return _rowwise(body, list(o_g) + list(lse_g) + [_col(qkvm, MEM_WIDTH, (3 * TOK_WIDTH) // MEM_WIDTH)], [kv_mem],
                    [(D_MODEL, F32), (MEM_WIDTH, F32)], [], tm=256, name=name)


def _head_sum_matrix():
    a = np.arange(MEM_WIDTH)
    return jnp.asarray((a[:, None] // ATT_HEAD_DIM == a[None, :] // ATT_HEAD_DIM).astype(np.float32))


def _att_bwd_prep(cat, dcat, qkvm, kv_mem, *, name):
    def body(cat, dcat, qm, kv, hs):
        prod = cat * dcat
        summed = prod[:, 0:256] + prod[:, 256:512] + prod[:, 512:768]
        delta = _dot(summed, hs, precision=HI)
        dqm, dkv = _mem_attn_bwd(qm.astype(F32), kv, dcat[:, TOK_WIDTH:])
        return (delta, dqm), (dkv,)
    return _rowwise(body, [cat, dcat, _col(qkvm, MEM_WIDTH, (3 * TOK_WIDTH) // MEM_WIDTH)], [kv_mem, _head_sum_matrix()],
                    [(MEM_WIDTH, F32), (MEM_WIDTH, F32)], [kv_mem.shape], tm=256, name=name)


def _dn_conv_post(s, j):
    scale = jnp.where(j < DN_HEADS, DN_HEAD_DIM ** -0.5, 1.0)
    normed = s * lax.rsqrt(jnp.sum(s * s, axis=-1, keepdims=True) + EPS) * scale
    return jnp.where(j >= 2 * DN_HEADS, s, normed)


def _shift_rows(x, sh):
    n = x.shape[0]
    row = lax.broadcasted_iota(jnp.int32, (n, 1), 0)
    rolled = pltpu.roll(x, (-sh) % n, 0)
    return jnp.where((row + sh >= 0) & (row + sh < n), rolled, 0.0)


def _dn_conv_taps(x, w_ref):
    c = x * w_ref[pl.ds(DN_CONV // 2, 1), :]
    for jj in range(DN_CONV):
        if jj != DN_CONV // 2:
            c = c + _shift_rows(x, jj - DN_CONV // 2) * w_ref[pl.ds(jj, 1), :]
    return c


def _dn_conv_fwd(proj, conv_w, *, name):
    s_len = proj.shape[0]
    width = 3 * TOK_WIDTH

    def kern(x_ref, w_ref, o_ref):
        j = pl.program_id(0)
        o_ref[...] = _dn_conv_post(_silu(_dn_conv_taps(x_ref[...], w_ref)), j)

    return pl.pallas_call(
        kern, name=name, grid=(width // LANES,),
        in_specs=[pl.BlockSpec((s_len, LANES), lambda j: (0, j)), pl.BlockSpec((DN_CONV, LANES), lambda j: (0, j))],
        out_specs=pl.BlockSpec((s_len, LANES), lambda j: (0, j)),
        out_shape=jax.ShapeDtypeStruct((s_len, width), F32),
        compiler_params=_params("parallel"),
    )(proj, conv_w)


def _dn_conv_bwd(proj, conv_w, d_fwd, d_bwd, which, *, name):
    s_len = proj.shape[0]

    def kern(x_ref, w_ref, df_ref, db_ref, dx_ref, dw_ref):
        j = pl.program_id(0) + which * DN_HEADS
        x = x_ref[...]
        c = _dn_conv_taps(x, w_ref)
        _, vjp = jax.vjp(lambda c_: _dn_conv_post(_silu(c_), j), c)
        dc = vjp(df_ref[...] + db_ref[...])[0]
        dx = dc * w_ref[pl.ds(DN_CONV // 2, 1), :]
        for jj in range(DN_CONV):
            sh = jj - DN_CONV // 2
            if sh != 0:
                dx = dx + _shift_rows(dc, -sh) * w_ref[pl.ds(jj, 1), :]
            dw_ref[pl.ds(jj, 1), :] = jnp.sum(dc * _shift_rows(x, sh), axis=0, keepdims=True)
        dx_ref[...] = dx

    return pl.pallas_call(
        kern, name=name, grid=(DN_HEADS,),
        in_specs=[pl.BlockSpec((s_len, LANES), lambda j: (0, j + which * DN_HEADS)),
                  pl.BlockSpec((DN_CONV, LANES), lambda j: (0, j + which * DN_HEADS)),
                  pl.BlockSpec((s_len, LANES), lambda j: (0, j)),
                  pl.BlockSpec((s_len, LANES), lambda j: (0, j))],
        out_specs=[pl.BlockSpec((s_len, LANES), lambda j: (0, j)), pl.BlockSpec((DN_CONV, LANES), lambda j: (0, j))],
        out_shape=[jax.ShapeDtypeStruct((s_len, TOK_WIDTH), F32), jax.ShapeDtypeStruct((DN_CONV, TOK_WIDTH), F32)],
        compiler_params=_params("parallel"),
    )(proj, conv_w, d_fwd, d_bwd)


GATE_TM = 2 * DN_CHUNK


FWD_GATE_LANES = 2 * DN_HEADS


def _gate_constants():
    i = np.arange(GATE_TM)
    same = (i[:, None] // DN_CHUNK) == (i[None, :] // DN_CHUNK)
    cum_f = same & (i[None, :] <= i[:, None])
    cum_r = same & (i[None, :] >= i[:, None])
    return tuple(jnp.asarray(np.asarray(a, np.float32)) for a in (cum_f, cum_r, same))


def _gate_params(p):
    z = jnp.zeros((DN_HEADS,), F32)
    return jnp.concatenate([p[0], z, p[1], z, jnp.zeros((LANES - N_GATES,), F32)]).reshape(1, LANES)


def _gate_params_bwd(dp):
    return jnp.stack([dp[0, 0:DN_HEADS], dp[0, 2 * DN_HEADS: 3 * DN_HEADS]])


def _dn_gates(gate_in, a_cols, dt_cols, cum_f, cum_r, tot):
    g = -jnp.exp(a_cols) * _softplus(gate_in + dt_cols)
    fwd_lane = lax.broadcasted_iota(jnp.int32, (1, LANES), 1) < FWD_GATE_LANES
    gc = jnp.where(fwd_lane, _dot(cum_f, g, precision=HI), _dot(cum_r, g, precision=HI))
    return gc, _dot(tot, g, precision=HI), jax.nn.sigmoid(gate_in)


def _dn_gates_fwd(proj, a_cols, dt_cols, *, name):
    def body(gi, *consts):
        return _dn_gates(gi, *consts), ()
    return _rowwise(body, [_col(proj, LANES, DN_IN_PAD // LANES - 1)], [a_cols, dt_cols, *_gate_constants()],
                    [(LANES, F32)] * 3, [], tm=GATE_TM, name=name)


def _dn_gates_bwd(proj, a_cols, dt_cols, d_gates, *, name):
    def body(gi, gcf, gtf, bf, gcr, gtr, br, a, dt, *consts):
        _, vjp = jax.vjp(lambda gi_, a_, dt_: _dn_gates(gi_, a_, dt_, *consts), gi, a, dt)
        dgi, da, ddt = vjp((gcf + gcr, gtf + gtr, bf + br))
        return (dgi,), (da, ddt)
    return _rowwise(body, [_col(proj, LANES, DN_IN_PAD // LANES - 1), *d_gates[0], *d_gates[1]],
                    [a_cols, dt_cols, *_gate_constants()], [(LANES, F32)], [a_cols.shape, dt_cols.shape],
                    tm=GATE_TM, name=name)


INV_BASE = 8


def _block_id_equal(c, size):
    i = lax.broadcasted_iota(jnp.int32, (c, c), 0) // size
    j = lax.broadcasted_iota(jnp.int32, (c, c), 1) // size
    return (i == j).astype(F32)


def _unit_tri_inverse_impl(lmat):
    c = lmat.shape[0]
    eye = _block_id_equal(c, 1)
    same = _block_id_equal(c, INV_BASE)
    neg = -lmat * same
    inv = eye + neg
    power = neg
    for _ in range(int(math.log2(INV_BASE)) - 1):
        power = _dot(power, power)
        inv = inv + _dot(inv, power)
    size = INV_BASE
    while size < c:
        bigger = _block_id_equal(c, 2 * size)
        inv = inv - _dot(_dot(inv, lmat * (bigger - same)), inv)
        same, size = bigger, 2 * size
    resid = eye - _dot(eye + lmat, inv, precision=HI)
    return inv + _dot(inv, resid)


@jax.custom_vjp
def _unit_tri_inverse(lmat):
    return _unit_tri_inverse_impl(lmat)


def _unit_tri_inverse_fwd(lmat):
    inv = _unit_tri_inverse_impl(lmat)
    return inv, inv


def _unit_tri_inverse_bwd(inv, d_inv):
    return (-_dot_tn(inv, _dot_nt(d_inv, inv)),)


_unit_tri_inverse.defvjp(_unit_tri_inverse_fwd, _unit_tri_inverse_bwd)


def _dn_chunk(q, k, v, gates_t, gc_row, tot_row, beta_row, state, tri, inverse):
    c = q.shape[0]
    assert c == DN_HEAD_DIM
    eye = _block_id_equal(c, 1)

    def along_rows(x, pick):
        return jnp.broadcast_to(jnp.sum(x * pick, axis=0, keepdims=True), (c, c))

    gc_j = along_rows(gates_t[0], gc_row)
    gc = gc_j.T
    g_tot = along_rows(gates_t[1], tot_row)
    beta = along_rows(gates_t[2], beta_row).T
    decay = jnp.exp(jnp.where(tri > 0, gc - gc_j, NEG_INF))
    k_beta = k * beta
    inv = inverse((tri - eye) * (_dot_nt(k_beta, k) * decay))
    e_gc = jnp.exp(gc)
    u = _dot(inv, v * beta)
    w = _dot(inv, k_beta * e_gc)
    intra = tri * (_dot_nt(q, k) * decay)
    v_new = u - _dot(w, state)
    out = _dot(q * e_gc, state) + _dot(intra, v_new)
    state = state * jnp.exp(g_tot) + _dot_tn(k * jnp.exp(g_tot - gc), v_new)
    return out, state


def _dn_tri():
    i = np.arange(DN_CHUNK)
    tri = np.stack([(i[None, :] <= i[:, None]), (i[None, :] >= i[:, None])]).astype(np.float32)
    return jnp.asarray(np.repeat(tri, DN_HEADS, axis=0))


def _dn_gate_picks():
    picks = np.zeros((3, 2 * DN_HEADS, 2 * DN_CHUNK, 1), np.float32)
    for d in range(2):
        for h in range(DN_HEADS):
            alpha = d * DN_CHUNK + d * 2 * DN_HEADS + h
            picks[0, d * DN_HEADS + h, alpha] = 1.0
            picks[1, d * DN_HEADS + h, alpha] = 1.0
            picks[2, d * DN_HEADS + h, alpha + DN_HEADS] = 1.0
    return jnp.asarray(picks)


def _stack_chains(fwd_ref, rev_ref):
    return jnp.stack([r[:, _head_cols(h)] for r in (fwd_ref, rev_ref) for h in range(DN_HEADS)])


def _unstack_chains(val, fwd_ref, rev_ref):
    for d, r in enumerate((fwd_ref, rev_ref)):
        for h in range(DN_HEADS):
            r[:, _head_cols(h)] = val[d * DN_HEADS + h]


def _gates_transposed(fwd_refs, rev_refs):
    return jnp.stack([jnp.concatenate([f[...].T, r[...].T], axis=0) for f, r in zip(fwd_refs, rev_refs, strict=True)])


def _dn_row_spec(nc, col, reverse, width=TOK_WIDTH):
    return pl.BlockSpec((DN_CHUNK, width), lambda t: ((nc - 1 - t) if reverse else t, col))


def _dn_state_spec(nc, reverse):
    return pl.BlockSpec((None, DN_HEADS, DN_HEAD_DIM, DN_HEAD_DIM), lambda t: ((nc - 1 - t) if reverse else t, 0, 0, 0))


def _head_cols(h):
    return pl.ds(h * DN_HEAD_DIM, DN_HEAD_DIM)


def _const_spec(arr):
    return pl.BlockSpec(arr.shape, functools.partial(lambda t, n: (0,) * n, n=arr.ndim))


def _dn_chains(inverse):
    return jax.vmap(lambda q, k, v, gates_t, *rest: _dn_chunk(q, k, v, gates_t, *rest, inverse),
                    in_axes=(0, 0, 0, None, 0, 0, 0, 0, 0))


def _dn_scan_fwd(qkv, gates, *, name):
    s_len = qkv.shape[0]
    nc = s_len // DN_CHUNK
    tri, picks = _dn_tri(), _dn_gate_picks()

    def kern(*refs):
        ins, (tri_ref, pick_ref, of_ref, or_ref, sf_ref, sr_ref, state) = refs[:12], refs[12:]

        @pl.when(pl.program_id(0) == 0)
        def _():
            state[...] = jnp.zeros_like(state)

        entry = state[...]
        qkv_c = [_stack_chains(ins[i], ins[6 + i]) for i in range(3)]
        gates_t = _gates_transposed(ins[3:6], ins[9:12])
        out, new = _dn_chains(_unit_tri_inverse_impl)(*qkv_c, gates_t, pick_ref[0], pick_ref[1], pick_ref[2], entry, tri_ref[...])
        sf_ref[...] = entry[:DN_HEADS]
        sr_ref[...] = entry[DN_HEADS:]
        _unstack_chains(out, of_ref, or_ref)
        state[...] = new

    in_specs = []
    for rev in (False, True):
        in_specs += [_dn_row_spec(nc, col, rev) for col in (0, 1, 2)] + [_dn_row_spec(nc, 0, rev, LANES)] * 3
    in_specs += [_const_spec(tri), _const_spec(picks)]
    return pl.pallas_call(
        kern, name=name, grid=(nc,), in_specs=in_specs,
        out_specs=[_dn_row_spec(nc, 0, False), _dn_row_spec(nc, 0, True), _dn_state_spec(nc, False), _dn_state_spec(nc, True)],
        out_shape=[jax.ShapeDtypeStruct((s_len, TOK_WIDTH), F32)] * 2
        + [jax.ShapeDtypeStruct((nc, DN_HEADS, DN_HEAD_DIM, DN_HEAD_DIM), F32)] * 2,
        scratch_shapes=[pltpu.VMEM((2 * DN_HEADS, DN_HEAD_DIM, DN_HEAD_DIM), F32)],
        compiler_params=_params("arbitrary"),
    )(*([qkv, qkv, qkv, *gates] * 2), tri, picks)


def _dn_scan_bwd(qkv, gates, states, d_o, *, name):
    s_len = qkv.shape[0]
    nc = s_len // DN_CHUNK
    tri, picks = _dn_tri(), _dn_gate_picks()

    def kern(*refs):
        ins, tri_ref, pick_ref, outs, d_state = refs[:16], refs[16], refs[17], refs[18:30], refs[30]

        @pl.when(pl.program_id(0) == 0)
        def _():
            d_state[...] = jnp.zeros_like(d_state)

        qkv_c = [_stack_chains(ins[i], ins[8 + i]) for i in range(3)]
        gates_t = _gates_transposed(ins[3:6], ins[11:14])
        entry = jnp.concatenate([ins[6][...], ins[14][...]], axis=0)
        d_out = _stack_chains(ins[7], ins[15])
        tri_v, picks_v = tri_ref[...], pick_ref[...]
        _, vjp = jax.vjp(lambda q, k, v, g, s: _dn_chains(_unit_tri_inverse)(q, k, v, g, picks_v[0], picks_v[1], picks_v[2], s, tri_v),
                         *qkv_c, gates_t, entry)
        dq, dk, dv, d_gates_t, d_entry = vjp((d_out, d_state[...]))
        for i, val in enumerate((dq, dk, dv)):
            _unstack_chains(val, outs[i], outs[6 + i])
        for i in range(3):
            outs[3 + i][...] = d_gates_t[i, :DN_CHUNK].T
            outs[9 + i][...] = d_gates_t[i, DN_CHUNK:].T
        d_state[...] = d_entry

    in_specs, out_specs, out_shape = [], [], []
    for rev in (True, False):
        in_specs += [_dn_row_spec(nc, col, rev) for col in (0, 1, 2)] + [_dn_row_spec(nc, 0, rev, LANES)] * 3
        in_specs += [_dn_state_spec(nc, rev), _dn_row_spec(nc, 0, rev)]
        out_specs += [_dn_row_spec(nc, 0, rev)] * 3 + [_dn_row_spec(nc, 0, rev, LANES)] * 3
        out_shape += [jax.ShapeDtypeStruct((s_len, TOK_WIDTH), F32)] * 3 + [jax.ShapeDtypeStruct((s_len, LANES), F32)] * 3
    in_specs += [_const_spec(tri), _const_spec(picks)]
    res = pl.pallas_call(
        kern, name=name, grid=(nc,), in_specs=in_specs, out_specs=out_specs, out_shape=out_shape,
        scratch_shapes=[pltpu.VMEM((2 * DN_HEADS, DN_HEAD_DIM, DN_HEAD_DIM), F32)],
        compiler_params=_params("arbitrary"),
    )(*[a for d in range(2) for a in (qkv, qkv, qkv, *gates, states[d], d_o)], tri, picks)
    return (res[0:3], res[3:6]), (res[6:9], res[9:12])


def _dn_out_head(o_f, o_b, z, gain):
    o = o_f + o_b
    return o * lax.rsqrt(jnp.mean(o * o, axis=-1, keepdims=True) + EPS) * gain * _silu(z)


def _dn_out(o_fwd, o_rev, proj, gain, qkv_kv_mem, *, name):
    def body(of, ob, z, qm, g, kv):
        heads = []
        for h in range(DN_HEADS):
            sl = slice(h * DN_HEAD_DIM, (h + 1) * DN_HEAD_DIM)
            heads.append(_dn_out_head(of[:, sl], ob[:, sl], z[:, sl], g))
        return (jnp.concatenate(heads + [_mem_attn(qm, kv)], axis=1),), ()
    return _rowwise(body, [o_fwd, o_rev, _col(proj, TOK_WIDTH, 3),
                           _col(proj, MEM_WIDTH, (4 * TOK_WIDTH) // MEM_WIDTH)], [gain, qkv_kv_mem],
                    [(D_MODEL, MXU_DTYPE)], [], tm=256, name=name)[0]


def _dn_out_bwd(o_fwd, o_rev, proj, gain, kv_mem, dcat, *, name):
    def body(of, ob, z, qm, dcat, g, kv):
        dos, dzs = [], []
        dgain = jnp.zeros_like(g)
        for h in range(DN_HEADS):
            sl = slice(h * DN_HEAD_DIM, (h + 1) * DN_HEAD_DIM)
            _, vjp = jax.vjp(_dn_out_head, of[:, sl], ob[:, sl], z[:, sl], g)
            d_of, _, dz, dg = vjp(dcat[:, sl])
            dos.append(d_of)
            dzs.append(dz)
            dgain = dgain + dg
        dqm, dkv = _mem_attn_bwd(qm, kv, dcat[:, TOK_WIDTH:])
        return (jnp.concatenate(dos, axis=1), jnp.concatenate(dzs, axis=1), dqm), (dgain, dkv)
    return _rowwise(body, [o_fwd, o_rev, _col(proj, TOK_WIDTH, 3),
                           _col(proj, MEM_WIDTH, (4 * TOK_WIDTH) // MEM_WIDTH), dcat], [gain, kv_mem],
                    [(TOK_WIDTH, F32), (TOK_WIDTH, F32), (MEM_WIDTH, F32)], [gain.shape, kv_mem.shape], tm=256, name=name)


def _pad_dn_w_in(w):
    gates = w[:, 4 * TOK_WIDTH: 4 * TOK_WIDTH + N_GATES]
    zeros = jnp.zeros((w.shape[0], DN_IN_PAD - DN_IN), w.dtype)
    return jnp.concatenate([w[:, :4 * TOK_WIDTH], w[:, 4 * TOK_WIDTH + N_GATES:], gates, zeros], axis=1)


def _unpad_dn_w_in(w):
    q_mem = w[:, 4 * TOK_WIDTH: 4 * TOK_WIDTH + MEM_WIDTH]
    gates = w[:, 4 * TOK_WIDTH + MEM_WIDTH: 4 * TOK_WIDTH + MEM_WIDTH + N_GATES]
    return jnp.concatenate([w[:, :4 * TOK_WIDTH], gates, q_mem], axis=1)


def _ffn_fwd(h, w_gu, w_d, tag):
    gu = _mm(h, w_gu, out_dtype=MXU_DTYPE, name=f"ffn_gu_{tag}")
    act = _swiglu_act(gu, name=f"ffn_act_{tag}")
    return gu, act, _mm(act, w_d, name=f"ffn_down_{tag}")


def _ffn_bwd(h, gu, act, w_gu, w_d, df, tag):
    d_act = _mm(df, w_d, tb=True, out_dtype=MXU_DTYPE, name=f"ffn_dact_{tag}")
    d_wd = _mm(act, df, ta=True, out_dtype=LINK_DTYPE, name=f"ffn_dwd_{tag}")
    d_gu = _swiglu_act_bwd(gu, d_act, name=f"ffn_dgu_{tag}")
    dh = _mm(d_gu, w_gu, tb=True, name=f"ffn_dh_{tag}")
    d_wgu = _mm(h, d_gu, ta=True, out_shards=_shards_of(w_gu), out_dtype=LINK_DTYPE, name=f"ffn_dwgu_{tag}")
    return dh, d_wgu, d_wd


def _local_step(x, mem, target, p):
    g = {}
    row = lambda v: v.reshape(1, -1)
    gains = {k: [row(p[k][i]) for i in range(2)] for k in
             ("mem_norm", "norm_mix_pre", "norm_mix_post", "norm_ffn_pre", "norm_ffn_post")}
    out_gain = row(p["dn_out_norm"])
    a_cols, dt_cols = _gate_params(p["dn_a_log"]), _gate_params(p["dn_dt_bias"])

    h0 = _pre_norm(x, gains["norm_mix_pre"][0], name="pre0")
    mem_n = [_pre_norm(mem, gains["mem_norm"][i], name=f"mem_norm{i}") for i in range(2)]
    qkvm = _mm(h0, p["att_w_in"], out_dtype=MXU_DTYPE, name="att_in")
    bias = _bias_tiles(p["rel_bias"])
    att = [_att_fwd(qkvm, bias[gi], gi, name=f"att_fwd{gi}") for gi in range(3)]
    att_w_out, *mem_w_kv = p["att_rest_weights"](att[2][0])
    kv_mem = [_mm(mem_n[i], mem_w_kv[i], name=f"mem_kv{i}") for i in range(2)]
    cat0, lse_tot = _att_combine([a[0] for a in att], [a[1] for a in att], qkvm, kv_mem[0], name="att_combine")
    mo0 = _mm(cat0, att_w_out, name="att_out")
    x1, h1 = _post_pre(x, mo0, gains["norm_mix_post"][0], gains["norm_ffn_pre"][0], name="post_mix0")
    w_gu0, w_d0 = p["ffn_weights"](0, h1)
    gu0, act0, f0 = _ffn_fwd(h1, w_gu0, w_d0, 0)
    x2, h2 = _post_pre(x1, f0, gains["norm_ffn_post"][0], gains["norm_mix_pre"][1], name="post_ffn0")

    dn_w_in, dn_w_out = p["dn_weights"](h2)
    proj = _mm(h2, dn_w_in, name="dn_in")
    qkv = _dn_conv_fwd(proj, p["dn_conv"], name="dn_conv")
    gates = _dn_gates_fwd(proj, a_cols, dt_cols, name="dn_gates")
    o_fwd, o_rev, st_fwd, st_rev = _dn_scan_fwd(qkv, gates, name="dn_scan")
    cat1 = _dn_out(o_fwd, o_rev, proj, out_gain, kv_mem[1], name="dn_outnorm")
    mo1 = _mm(cat1, dn_w_out, name="dn_out")
    x3, h3 = _post_pre(x2, mo1, gains["norm_mix_post"][1], gains["norm_ffn_pre"][1], name="post_mix1")
    w_gu1, w_d1 = p["ffn_weights"](1, h3)
    gu1, act1, f1 = _ffn_fwd(h3, w_gu1, w_d1, 1)

    dx3, df1, dg_ffn_post1, loss_cols = _final_loss_bwd(x3, f1, gains["norm_ffn_post"][1], target, name="loss_bwd")
    dh3, d_wgu1, d_wd1 = _ffn_bwd(h3, gu1, act1, w_gu1, w_d1, df1, 1)
    sent = p["grads_ready"]("ffn1", {("ffn_w_gate_up", 1): d_wgu1, ("ffn_w_down", 1): d_wd1})
    dx2, dmo1, dg_mix_post1, dg_ffn_pre1 = _post_pre_bwd(x2, mo1, gains["norm_mix_post"][1] + sent, gains["norm_ffn_pre"][1],
                                                         dx3, dh3, name="post_mix1_bwd")
    dcat1 = _mm(dmo1, dn_w_out, tb=True, name="dn_out_dx")
    g["dn_w_out"] = _mm(cat1, dmo1, ta=True, out_dtype=LINK_DTYPE, name="dn_out_dw")
    d_o, dz, dqm1, d_out_gain, dkv1 = _dn_out_bwd(o_fwd, o_rev, proj, out_gain, kv_mem[1], dcat1, name="dn_outnorm_bwd")
    (d_f, dg_f), (d_r, dg_r) = _dn_scan_bwd(qkv, gates, (st_fwd, st_rev), d_o, name="dn_scan_bwd")
    d_gate_cols, d_a_cols, d_dt_cols = _dn_gates_bwd(proj, a_cols, dt_cols, (dg_f, dg_r), name="dn_gates_bwd")
    d_pre, d_conv = zip(*[_dn_conv_bwd(proj, p["dn_conv"], d_f[which], d_r[which], which, name=f"dn_conv_bwd{which}")
                          for which in range(3)])
    dproj = jnp.concatenate(list(d_pre) + [dz, dqm1, d_gate_cols], axis=1).astype(MXU_DTYPE)
    dh2 = _mm(dproj, dn_w_in, tb=True, name="dn_in_dx")
    g["dn_w_in"] = _mm(h2, dproj, ta=True, out_dtype=LINK_DTYPE, name="dn_in_dw")
    g["dn_conv"] = jnp.concatenate(d_conv, axis=1)
    g["dn_a_log"] = _gate_params_bwd(d_a_cols)
    g["dn_dt_bias"] = _gate_params_bwd(d_dt_cols)
    g["dn_out_norm"] = d_out_gain

    d_mem_kv1 = _mm(mem_n[1], dkv1, ta=True, out_dtype=LINK_DTYPE, name="mem_kv_dw1")
    sent = p["grads_ready"]("dn", {("dn_w_in", 0): g["dn_w_in"], ("dn_w_out", 0): g["dn_w_out"], ("mem_w_kv", 1): d_mem_kv1})
    dx1, df0, dg_ffn_post0, dg_mix_pre1 = _post_pre_bwd(x1, f0, gains["norm_ffn_post"][0] + sent, gains["norm_mix_pre"][1],
                                                        dx2, dh2, name="post_ffn0_bwd")
    dh1, d_wgu0, d_wd0 = _ffn_bwd(h1, gu0, act0, w_gu0, w_d0, df0, 0)
    sent = p["grads_ready"]("ffn0", {("ffn_w_gate_up", 0): d_wgu0, ("ffn_w_down", 0): d_wd0})
    dx0, dmo0, dg_mix_post0, dg_ffn_pre0 = _post_pre_bwd(x, mo0, gains["norm_mix_post"][0] + sent, gains["norm_ffn_pre"][0],
                                                         dx1, dh1, name="post_mix0_bwd")
    dcat0 = _mm(dmo0, att_w_out, tb=True, name="att_out_dx")
    g["att_w_out"] = _mm(cat0, dmo0, ta=True, out_dtype=LINK_DTYPE, name="att_out_dw")
    delta, dqm0, dkv0 = _att_bwd_prep(cat0, dcat0, qkvm, kv_mem[0], name="att_bwd_prep")
    d_mem_kv0 = _mm(mem_n[0], dkv0, ta=True, out_dtype=LINK_DTYPE, name="mem_kv_dw0")
    sent = p["grads_ready"]("att_out", {("att_w_out", 0): g["att_w_out"], ("mem_w_kv", 0): d_mem_kv0})
    att_b = [_att_bwd(qkvm, bias[gi] + sent, lse_tot, delta, dcat0, gi, name=f"att_bwd{gi}") for gi in range(3)]
    dqkvm = jnp.concatenate([a[w] for w in range(3) for a in att_b] + [dqm0], axis=1).astype(MXU_DTYPE)
    g["rel_bias"] = _bias_tiles_bwd(p["rel_bias"], jnp.stack([att_b[gi][3] for gi in range(3)]))
    g["att_w_in"] = _mm(h0, dqkvm, ta=True, out_shards=_shards_of(p["att_w_in"]), out_dtype=LINK_DTYPE, name="att_in_dw")
    sent = p["grads_ready"]("att_in", {("att_w_in", 0): g["att_w_in"]})
    dh0 = _mm(dqkvm, p["att_w_in"], tb=True, name="att_in_dx")
    grad_x, dg_mix_pre0 = _pre_norm_bwd(x, gains["norm_mix_pre"][0] + sent, dh0, dx0, name="pre0_bwd")

    d_mem_norm = []
    for i, dkv in enumerate((dkv0, dkv1)):
        d_mem_n = _mm(dkv, mem_w_kv[i], tb=True, name=f"mem_kv_dx{i}")
        d_mem_norm.append(_gain_bwd(mem, gains["mem_norm"][i], d_mem_n, name=f"mem_norm_bwd{i}"))
    g["mem_w_kv"] = [d_mem_kv0, d_mem_kv1]
    g["mem_norm"] = jnp.concatenate(d_mem_norm, axis=0)
    g["norm_mix_pre"] = jnp.concatenate([dg_mix_pre0, dg_mix_pre1], axis=0)
    g["norm_mix_post"] = jnp.concatenate([dg_mix_post0, dg_mix_post1], axis=0)
    g["norm_ffn_pre"] = jnp.concatenate([dg_ffn_pre0, dg_ffn_pre1], axis=0)
    g["norm_ffn_post"] = jnp.concatenate([dg_ffn_post0, dg_ffn_post1], axis=0)
    g["ffn_w_gate_up"] = [d_wgu0, d_wgu1]
    g["ffn_w_down"] = [d_wd0, d_wd1]
    return loss_cols, grad_x, g


N_CHIPS = 4
N_DEV = 8
MESH = pl.DeviceIdType.MESH
BIG = (("att_w_in", (1, 1024, 640), 2), ("att_w_out", (1, 256, 1024), 1), ("dn_w_in", (1, 1024, 838), 2),
       ("dn_w_out", (1, 256, 1024), 1), ("mem_w_kv", (2, 256, 512), 1), ("ffn_w_gate_up", (2, 1024, 1408), 2),
       ("ffn_w_down", (2, 704, 1024), 1))


def _mesh_pos():
    return lax.axis_index("x"), lax.axis_index("y"), lax.axis_index("c")


def _other_chips(x, y):
    return [(1 - x, y), (x, 1 - y), (1 - x, 1 - y)]


ANY = pl.BlockSpec(memory_space=pl.ANY)


def _all_reduce_small(v, *, name):
    rows, cols = v.shape
    flips = [(dx, dy, dc) for dx in (0, 1) for dy in (0, 1) for dc in (0, 1)][1:]

    def body(v_ref, o_ref, buf, send_sems, recv_sems):
        x, y, c = _mesh_pos()

        def peer(f):
            return tuple(1 - p if fl else p for p, fl in zip((x, y, c), f))

        def index(p):
            return 4 * p[0] + 2 * p[1] + p[2]

        buf[index((x, y, c))] = v_ref[...]
        sends = []
        for k, f in enumerate(flips):
            cp = pltpu.make_async_remote_copy(src_ref=v_ref, dst_ref=buf.at[index((x, y, c))], send_sem=send_sems.at[k],
                                              recv_sem=recv_sems.at[k], device_id=peer(f), device_id_type=MESH)
            cp.start()
            sends.append(cp)
        for k, f in enumerate(flips):
            pltpu.make_async_remote_copy(src_ref=v_ref, dst_ref=buf.at[index(peer(f))], send_sem=send_sems.at[k],
                                         recv_sem=recv_sems.at[k], device_id=peer(f), device_id_type=MESH).wait_recv()
        for cp in sends:
            cp.wait_send()
        acc = buf[0]
        for d in range(1, N_DEV):
            acc = acc + buf[d]
        o_ref[...] = acc

    vmem = pl.BlockSpec(memory_space=pltpu.VMEM)
    return pl.pallas_call(
        body, name=name, in_specs=[vmem], out_specs=vmem, out_shape=jax.ShapeDtypeStruct((rows, cols), F32),
        scratch_shapes=[pltpu.VMEM((N_DEV, rows, cols), F32), pltpu.SemaphoreType.DMA((N_DEV - 1,)),
                        pltpu.SemaphoreType.DMA((N_DEV - 1,))],
    )(v)


def _adamw_update(w, g, m, v):
    m = ADAM_B1 * m + (1.0 - ADAM_B1) * g
    v = ADAM_B2 * v + (1.0 - ADAM_B2) * (g * g)
    m_hat = m / (1.0 - ADAM_B1 ** ADAM_STEP)
    v_hat = v / (1.0 - ADAM_B2 ** ADAM_STEP)
    return -ADAM_LR * (m_hat / (jnp.sqrt(v_hat) + ADAM_EPS) + ADAM_WD * w), m, v


def _adamw_many(ws, gs, ms, vs, *, name):
    n = len(ws)

    def kern(*refs):
        for i in range(n):
            outs = _adamw_update(*[refs[k * n + i][...] for k in range(4)])
            for k, val in enumerate(outs):
                refs[(4 + k) * n + i][...] = val

    vmem = pl.BlockSpec(memory_space=pltpu.VMEM)
    res = pl.pallas_call(
        kern, name=name, in_specs=[vmem] * (4 * n), out_specs=[vmem] * (3 * n),
        out_shape=[jax.ShapeDtypeStruct(a.shape, F32) for _ in range(3) for a in ws],
    )(*ws, *gs, *ms, *vs)
    return res[:n], res[n: 2 * n], res[2 * n:]


def _adamw(w, g, m, v, *, name):
    def body(w, g, m, v):
        return _adamw_update(w, g, m, v), ()
    rows, cols = w.shape
    if rows % SUBLANES == 0:
        return _rowwise(body, [w, g, m, v], [], [(cols, F32)] * 3, [], tm=_tile(rows, 256, SUBLANES), name=name)

    def kern(*refs):
        outs, _ = body(*[r[...] for r in refs[:4]])
        for r, val in zip(refs[4:], outs, strict=True):
            r[...] = val

    spec = pl.BlockSpec((rows, _tile(cols, 256, LANES)), lambda j: (0, j))
    return pl.pallas_call(
        kern, name=name, grid=(cols // spec.block_shape[1],), in_specs=[spec] * 4, out_specs=[spec] * 3,
        out_shape=[jax.ShapeDtypeStruct((rows, cols), F32)] * 3, compiler_params=_params("parallel"),
    )(w, g, m, v)


def _pack_small(arrs, rows):
    flat = jnp.concatenate([a.reshape(-1) for a in arrs])
    return jnp.pad(flat, (0, rows * LANES - flat.shape[0])).reshape(rows, LANES)


def _unpack_small(packed, shapes):
    flat = packed.reshape(-1)
    out, off = [], 0
    for s in shapes:
        size = math.prod(s)
        out.append(flat[off: off + size].reshape(s))
        off += size
    return out


def _small_rows(shapes):
    return -(-sum(math.prod(s) for s in shapes) // (SUBLANES * LANES)) * SUBLANES


def _sem_pairs(n):
    return [pltpu.SemaphoreType.DMA((n,)), pltpu.SemaphoreType.DMA((n,))]


def _gather_blocks(blocks, *, name):
    n = len(blocks)

    def body(*refs):
        x_refs, out_refs, (send_sems, recv_sems) = refs[:n], refs[n: 2 * n], refs[2 * n:]
        x, y, c = _mesh_pos()
        sibling = (x, y, 1 - c)
        chips = _other_chips(x, y)

        def copy(k, src, dst, to):
            return pltpu.make_async_remote_copy(src_ref=src, dst_ref=dst, send_sem=send_sems.at[k],
                                                recv_sem=recv_sems.at[k], device_id=to, device_id_type=MESH)

        def part(b, chip, h):
            half = blocks[b].shape[0] // 2
            return out_refs[b].at[2 * chip[0] + chip[1], pl.ds(h * half, half), :]

        def my_half(b):
            half = blocks[b].shape[0] // 2
            return x_refs[b].at[pl.ds(c * half, half), :]

        first = [copy(6 * b + j, my_half(b), part(b, (x, y), c), (*chip, c)) for b in range(n) for j, chip in enumerate(chips)]
        for cp in first:
            cp.start()
        passed = []
        for b in range(n):
            for j, chip in enumerate(chips):
                copy(6 * b + j, my_half(b), part(b, chip, c), (*chip, c)).wait_recv()
                cp = copy(6 * b + 3 + j, part(b, chip, c), part(b, chip, c), sibling)
                cp.start()
                passed.append(cp)
        for b in range(n):
            for j, chip in enumerate(chips):
                copy(6 * b + 3 + j, part(b, chip, 1 - c), part(b, chip, 1 - c), sibling).wait_recv()
        for cp in first + passed:
            cp.wait_send()

    return pl.pallas_call(
        body, name=name, in_specs=[ANY] * n, out_specs=[ANY] * n,
        out_shape=[jax.ShapeDtypeStruct((N_CHIPS, *a.shape), a.dtype) for a in blocks],
        scratch_shapes=_sem_pairs(6 * n),
    )(*blocks)


HBM = pl.BlockSpec(memory_space=pltpu.HBM)
SEM = pl.BlockSpec(memory_space=pltpu.SEMAPHORE)
DATAFLOW = pltpu.SideEffectType.DATAFLOW_SIDE_EFFECTING


def _gather_start(blocks, *, name):
    n = len(blocks)
    lands = [lax.empty((N_CHIPS, *a.shape), a.dtype) for a in blocks]

    def body(*refs):
        x_refs, land_refs, send_sems, recv_sems, token = refs[:n], refs[n: 2 * n], refs[2 * n], refs[2 * n + 1], refs[-1]
        x, y, c = _mesh_pos()
        for b in range(n):
            for j, chip in enumerate(_other_chips(x, y)):
                pltpu.make_async_remote_copy(src_ref=x_refs[b], dst_ref=land_refs[b].at[2 * x + y], send_sem=send_sems.at[3 * b + j],
                                             recv_sem=recv_sems.at[3 * b + j], device_id=(*chip, c), device_id_type=MESH).start()
        token[...] = jnp.zeros_like(token)

    operands = [pltpu.with_memory_space_constraint(a, pltpu.HBM) for a in blocks + lands]
    res = pl.pallas_call(
        body, name=name, in_specs=[HBM] * (2 * n),
        out_shape=(pltpu.SemaphoreType.DMA((3 * n,)), pltpu.SemaphoreType.DMA((3 * n,)),
                   *[pltpu.HBM(a.shape, a.dtype) for a in operands], jax.ShapeDtypeStruct((SUBLANES, LANES), F32)),
        out_specs=(SEM, SEM, *[HBM] * (2 * n), pl.BlockSpec(memory_space=pltpu.VMEM)),
        input_output_aliases={i: 2 + i for i in range(2 * n)},
        compiler_params=pltpu.CompilerParams(has_side_effects=DATAFLOW),
    )(*operands)
    return res[0], res[1], list(res[2: 2 + n]), list(res[2 + n: 2 + 2 * n]), res[-1]


def _gather_wait(started, after, *, name):
    send_sems, recv_sems, blocks, lands, _ = started
    n = len(blocks)

    def body(*refs):
        x_refs, land_refs, send_sems, recv_sems = refs[:n], refs[n: 2 * n], refs[2 * n], refs[2 * n + 1]
        x, y, c = _mesh_pos()
        for b in range(n):
            for j, chip in enumerate(_other_chips(x, y)):
                cp = pltpu.make_async_remote_copy(src_ref=x_refs[b], dst_ref=land_refs[b].at[2 * chip[0] + chip[1]],
                                                  send_sem=send_sems.at[3 * b + j], recv_sem=recv_sems.at[3 * b + j],
                                                  device_id=(*chip, c), device_id_type=MESH)
                cp.wait_send()
                cp.wait_recv()

    res = pl.pallas_call(
        body, name=name, in_specs=(*[HBM] * (2 * n), SEM, SEM, ANY),
        out_shape=tuple(pltpu.HBM(a.shape, a.dtype) for a in blocks + lands), out_specs=tuple([HBM] * (2 * n)),
        input_output_aliases={i: i for i in range(2 * n)},
        compiler_params=pltpu.CompilerParams(has_side_effects=DATAFLOW),
    )(*blocks, *lands, send_sems, recv_sems, after)
    return list(res[n:])


def _swap_with_sibling(blocks, *, name):
    n = len(blocks)

    def body(*refs):
        in_refs, out_refs, (send_sems, recv_sems) = refs[:n], refs[n: 2 * n], refs[2 * n:]
        x, y, c = _mesh_pos()
        copies = [pltpu.make_async_remote_copy(src_ref=in_refs[b], dst_ref=out_refs[b], send_sem=send_sems.at[b],
                                               recv_sem=recv_sems.at[b], device_id=(x, y, 1 - c), device_id_type=MESH)
                  for b in range(n)]
        for cp in copies:
            cp.start()
        for cp in copies:
            cp.wait()

    return pl.pallas_call(
        body, name=name, in_specs=[ANY] * n, out_specs=[ANY] * n,
        out_shape=[jax.ShapeDtypeStruct(a.shape, a.dtype) for a in blocks], scratch_shapes=_sem_pairs(n),
    )(*blocks)


def _sum_chips_block(parts, *, name):
    n, half, cols = parts.shape
    tm = _tile(half, 512, 2 * SUBLANES)

    def kern(p_ref, o_ref):
        acc = p_ref[0].astype(F32)
        for s in range(1, n):
            acc = acc + p_ref[s].astype(F32)
        o_ref[...] = acc

    return pl.pallas_call(
        kern, name=name, grid=(half // tm,),
        in_specs=[pl.BlockSpec((n, tm, cols), lambda i: (0, i, 0))],
        out_specs=pl.BlockSpec((tm, cols), lambda i: (i, 0)),
        out_shape=jax.ShapeDtypeStruct((half, cols), F32),
        compiler_params=_params("parallel"),
    )(parts)


PEER_FLIPS = [(dx, dy, dc) for dx in (0, 1) for dy in (0, 1) for dc in (0, 1)][1:]


def _flipped(pos, flip):
    return tuple(1 - p if f else p for p, f in zip(pos, flip))


def _device_index(pos):
    return 4 * pos[0] + 2 * pos[1] + pos[2]


def _scatter_start(blocks, *, name):
    n = len(blocks)
    lands = [lax.empty((N_DEV, a.shape[1] // 2, a.shape[2]), a.dtype) for a in blocks]

    def body(*refs):
        g_refs, land_refs, send_sems, recv_sems, token = refs[:n], refs[n: 2 * n], refs[2 * n], refs[2 * n + 1], refs[-1]
        pos = _mesh_pos()
        for b in range(n):
            half = blocks[b].shape[1] // 2
            for k, flip in enumerate(PEER_FLIPS):
                peer = _flipped(pos, flip)
                pltpu.make_async_remote_copy(src_ref=g_refs[b].at[2 * peer[0] + peer[1], pl.ds(peer[2] * half, half), :],
                                             dst_ref=land_refs[b].at[_device_index(pos)],
                                             send_sem=send_sems.at[7 * b + k], recv_sem=recv_sems.at[7 * b + k],
                                             device_id=peer, device_id_type=MESH).start()
        token[...] = jnp.zeros_like(token)

    operands = [pltpu.with_memory_space_constraint(a, pltpu.HBM) for a in blocks + lands]
    res = pl.pallas_call(
        body, name=name, in_specs=[HBM] * (2 * n),
        out_shape=(pltpu.SemaphoreType.DMA((7 * n,)), pltpu.SemaphoreType.DMA((7 * n,)),
                   *[pltpu.HBM(a.shape, a.dtype) for a in operands], jax.ShapeDtypeStruct((SUBLANES, LANES), F32)),
        out_specs=(SEM, SEM, *[HBM] * (2 * n), pl.BlockSpec(memory_space=pltpu.VMEM)),
        input_output_aliases={i: 2 + i for i in range(2 * n)},
        compiler_params=pltpu.CompilerParams(has_side_effects=DATAFLOW),
    )(*operands)
    return res[0], res[1], list(res[2: 2 + n]), list(res[2 + n: 2 + 2 * n]), res[-1]


def _scatter_wait(started, after, *, name):
    send_sems, recv_sems, blocks, lands, _ = started
    n = len(blocks)

    def body(*refs):
        g_refs, land_refs, send_sems, recv_sems = refs[:n], refs[n: 2 * n], refs[2 * n], refs[2 * n + 1]
        pos = _mesh_pos()
        for b in range(n):
            half = blocks[b].shape[1] // 2
            for k, flip in enumerate(PEER_FLIPS):
                peer = _flipped(pos, flip)
                cp = pltpu.make_async_remote_copy(src_ref=g_refs[b].at[0, pl.ds(0, half), :],
                                                  dst_ref=land_refs[b].at[_device_index(peer)],
                                                  send_sem=send_sems.at[7 * b + k], recv_sem=recv_sems.at[7 * b + k],
                                                  device_id=peer, device_id_type=MESH)
                cp.wait_send()
                cp.wait_recv()

    res = pl.pallas_call(
        body, name=name, in_specs=(*[HBM] * (2 * n), SEM, SEM, ANY),
        out_shape=tuple(pltpu.HBM(a.shape, a.dtype) for a in blocks + lands), out_specs=tuple([HBM] * (2 * n)),
        input_output_aliases={i: i for i in range(2 * n)},
        compiler_params=pltpu.CompilerParams(has_side_effects=DATAFLOW),
    )(*blocks, *lands, send_sems, recv_sems, after)
    return list(res[:n]), list(res[n:])


def _reduce_finish(begun, names, after):
    x, y, c = _mesh_pos()
    mine = {}
    for key, started in begun.items():
        blocks, lands = _scatter_wait(started, after, name=f"rs_scatter_wait_{key}")
        parts = []
        for blk, land in zip(blocks, lands, strict=True):
            half = blk.shape[1] // 2
            own = lax.dynamic_slice(blk, (2 * x + y, c * half, 0), (1, half, blk.shape[2]))
            parts.append(lax.dynamic_update_slice(land, own, (_device_index((x, y, c)), 0, 0)))
        mine[key] = [_sum_chips_block(p, name=f"rs_sum_{nm}") for p, nm in zip(parts, names[key], strict=True)]
    flat = [a for key in begun for a in mine[key]]
    other = iter(_swap_with_sibling(flat, name="rs_join"))
    return {key: [jnp.concatenate([jnp.where(c == 0, a, b), jnp.where(c == 0, b, a)], axis=0)
                  for a, b in ((a, next(other)) for a in mine[key])] for key in begun}


WEIGHTS = ("rel_bias", "att_w_in", "att_w_out", "dn_w_in", "dn_conv", "dn_a_log", "dn_dt_bias", "dn_out_norm", "dn_w_out",
           "mem_norm", "mem_w_kv", "norm_mix_pre", "norm_mix_post", "norm_ffn_pre", "norm_ffn_post", "ffn_w_gate_up",
           "ffn_w_down")
BIG_NAMES = tuple(n for n, _, _ in BIG)
SMALL_NAMES = tuple(n for n in WEIGHTS if n not in BIG_NAMES)
CONV_COLS = 3 * TOK_WIDTH
CONV_SHARD = CONV_COLS // N_CHIPS
BLOCKS = tuple((n, layer) for n, shape, _ in BIG for layer in range(shape[0]))
COLUMN_SHARDED = {n: axis == 2 for n, _, axis in BIG}


def kernel(x, mem, rel_bias, att_w_in, att_w_out, dn_w_in, dn_conv, dn_a_log, dn_dt_bias, dn_out_norm, dn_w_out, mem_norm, mem_w_kv, norm_mix_pre, norm_mix_post, norm_ffn_pre, norm_ffn_post, ffn_w_gate_up, ffn_w_down, loss_target, m_rel_bias, m_att_w_in, m_att_w_out, m_dn_w_in, m_dn_conv, m_dn_a_log, m_dn_dt_bias, m_dn_out_norm, m_dn_w_out, m_mem_norm, m_mem_w_kv, m_norm_mix_pre, m_norm_mix_post, m_norm_ffn_pre, m_norm_ffn_post, m_ffn_w_gate_up, m_ffn_w_down, v_rel_bias, v_att_w_in, v_att_w_out, v_dn_w_in, v_dn_conv, v_dn_a_log, v_dn_dt_bias, v_dn_out_norm, v_dn_w_out, v_mem_norm, v_mem_w_kv, v_norm_mix_pre, v_norm_mix_post, v_norm_ffn_pre, v_norm_ffn_post, v_ffn_w_gate_up, v_ffn_w_down):
    w = dict(zip(WEIGHTS, (rel_bias, att_w_in, att_w_out, dn_w_in, dn_conv, dn_a_log, dn_dt_bias, dn_out_norm, dn_w_out,
                           mem_norm, mem_w_kv, norm_mix_pre, norm_mix_post, norm_ffn_pre, norm_ffn_post, ffn_w_gate_up,
                           ffn_w_down)))
    m = dict(zip(WEIGHTS, (m_rel_bias, m_att_w_in, m_att_w_out, m_dn_w_in, m_dn_conv, m_dn_a_log, m_dn_dt_bias,
                           m_dn_out_norm, m_dn_w_out, m_mem_norm, m_mem_w_kv, m_norm_mix_pre, m_norm_mix_post,
                           m_norm_ffn_pre, m_norm_ffn_post, m_ffn_w_gate_up, m_ffn_w_down)))
    v = dict(zip(WEIGHTS, (v_rel_bias, v_att_w_in, v_att_w_out, v_dn_w_in, v_dn_conv, v_dn_a_log, v_dn_dt_bias,
                           v_dn_out_norm, v_dn_w_out, v_mem_norm, v_mem_w_kv, v_norm_mix_pre, v_norm_mix_post,
                           v_norm_ffn_pre, v_norm_ffn_post, v_ffn_w_gate_up, v_ffn_w_down)))
    cx, cy, cc = _mesh_pos()
    chip = 2 * cx + cy

    local = dict(zip(BLOCKS, lax.optimization_barrier(
        [(w[n][layer].T if n == "dn_w_in" else w[n][layer]).astype(MXU_DTYPE) for n, layer in BLOCKS]), strict=True))

    def usable(block, got):
        got = lax.dynamic_update_slice(got, local[block][None], (chip, 0, 0))
        return got if COLUMN_SHARDED[block[0]] else got.reshape(-1, got.shape[-1])

    late = {"att_rest": [("att_w_out", 0), ("mem_w_kv", 0), ("mem_w_kv", 1)],
            "ffn0": [("ffn_w_gate_up", 0), ("ffn_w_down", 0)], "dn": [("dn_w_in", 0), ("dn_w_out", 0)],
            "ffn1": [("ffn_w_gate_up", 1), ("ffn_w_down", 1)]}
    first = [b for b in BLOCKS if all(b not in blks for blks in late.values())]
    first_got = _gather_blocks([local[b] for b in first], name="gather_weights")
    late_local, _ = lax.optimization_barrier(({k: [local[b] for b in blks] for k, blks in late.items()}, first_got[0]))
    started = {k: _gather_start(late_local[k], name=f"gather_start_{k}") for k in late}
    started_token = sum(s[4][0, 0] for s in started.values())

    def late_weights(key, after):
        lands = _gather_wait(started[key], after, name=f"gather_wait_{key}")
        return [usable(b, got) for b, got in zip(late[key], lands, strict=True)]

    def dn_weights(after):
        w_in, w_out = late_weights("dn", after)
        return _pad_dn_w_in(jnp.concatenate([w_in[s].T for s in range(N_CHIPS)], axis=1)), w_out

    full = {}
    for b, got in zip(first, first_got, strict=True):
        full.setdefault(b[0], []).append(usable(b, got))
    conv_rows = _small_rows([(DN_CONV, CONV_COLS)])
    conv_mine = jnp.where(cc == 0, 1.0, 0.0) * w["dn_conv"][0]
    conv_placed = lax.dynamic_update_slice(jnp.zeros((DN_CONV, CONV_COLS), F32), conv_mine, (0, chip * CONV_SHARD))
    conv_full = _unpack_small(_all_reduce_small(_pack_small([conv_placed], conv_rows), name="gather_conv"),
                              [(DN_CONV, CONV_COLS)])[0]
    p = {
        "rel_bias": w["rel_bias"], "att_w_in": full["att_w_in"][0], "att_rest_weights": lambda after: late_weights("att_rest", after),
        "dn_conv": conv_full, "dn_a_log": w["dn_a_log"][0], "dn_dt_bias": w["dn_dt_bias"][0],
        "dn_out_norm": w["dn_out_norm"][0], "mem_norm": w["mem_norm"],
        "norm_mix_pre": w["norm_mix_pre"] + started_token,
        "norm_mix_post": w["norm_mix_post"], "norm_ffn_pre": w["norm_ffn_pre"], "norm_ffn_post": w["norm_ffn_post"],
        "ffn_weights": lambda layer, after: late_weights(f"ffn{layer}", after), "dn_weights": dn_weights,
    }

    def chip_blocks(n, a):
        if n == "dn_w_in":
            a = _unpad_dn_w_in(a)
        if a.ndim == 3:
            return a
        if COLUMN_SHARDED[n]:
            return a.reshape(a.shape[0], N_CHIPS, -1).transpose(1, 0, 2)
        return a.reshape(N_CHIPS, -1, a.shape[-1])

    begun, begun_blocks = {}, {}

    def grads_ready(key, layer_grads):
        begun_blocks[key] = list(layer_grads)
        begun[key] = _scatter_start([chip_blocks(n, a) for (n, _), a in layer_grads.items()], name=f"rs_scatter_start_{key}")
        return begun[key][4][0, 0]

    p["grads_ready"] = grads_ready
    loss_cols, grad_x, g = _local_step(x[0], mem[0], loss_target[0], p)
    finished = _reduce_finish(begun, {k: [f"{n}{layer}" for n, layer in blks] for k, blks in begun_blocks.items()}, grad_x)
    reduced = {b: r for k in begun for b, r in zip(begun_blocks[k], finished[k], strict=True)}
    grads = {n: jnp.concatenate([reduced[b] for b in BLOCKS if b[0] == n], axis=0).reshape(shape) for n, shape, _ in BIG}
    small_full_shapes = [(DN_CONV, CONV_COLS) if n == "dn_conv" else w[n].shape for n in SMALL_NAMES] + [(1,)]
    small_sum = _all_reduce_small(_pack_small([g[n] for n in SMALL_NAMES] + [jnp.sum(loss_cols).reshape(1)],
                                              _small_rows(small_full_shapes)), name="reduce_small")
    *small_grads, loss = _unpack_small(small_sum, small_full_shapes)
    loss = loss[0]
    for n, s in zip(SMALL_NAMES, small_grads, strict=True):
        grads[n] = lax.dynamic_slice(s, (0, chip * CONV_SHARD), (DN_CONV, CONV_SHARD))[None] if n == "dn_conv" else s

    delta, new_m, new_v = {}, {}, {}
    for n in BIG_NAMES:
        shape = w[n].shape
        two_d = (lambda a: a[0].T) if n == "dn_w_in" else (lambda a: a.reshape(-1, shape[-1]))
        back = (lambda a: a.T[None]) if n == "dn_w_in" else (lambda a: a.reshape(shape))
        g_2d = two_d(grads[n])
        res = _adamw(two_d(w[n]), g_2d, two_d(m[n]), two_d(v[n]), name=f"adamw_{n}")
        grads[n], delta[n], new_m[n], new_v[n] = (back(r) for r in (g_2d, *res))
    res = _adamw_many(*[[d[n].reshape(-1, d[n].shape[-1]) for n in SMALL_NAMES] for d in (w, grads, m, v)], name="adamw_small")
    for d, outs in zip((delta, new_m, new_v), res):
        for n, a in zip(SMALL_NAMES, outs):
            d[n] = a.reshape(w[n].shape)
    return (loss, grad_x[None], *[grads[n] for n in WEIGHTS], *[delta[n] for n in WEIGHTS],
            *[new_m[n] for n in WEIGHTS], *[new_v[n] for n in WEIGHTS])
```

```python
import functools
import math

import numpy as np
import jax
import jax.numpy as jnp
from jax import lax
from jax.experimental import pallas as pl
from jax.experimental.pallas import tpu as pltpu

F32 = jnp.float32
MXU_DTYPE = jnp.bfloat16
LINK_DTYPE = jnp.bfloat16
HI = lax.Precision.HIGHEST

EPS = 1e-6
NEG_INF = -1e30
LANES = 128
SUBLANES = 8
VMEM_LIMIT = 56 * 1024 * 1024
MM_WHOLE_K_BUDGET = 44 * 1024 * 1024

D_MODEL = 1024
TOK_WIDTH = 768
MEM_WIDTH = 256
MEM_LEN = 256
ATT_HEAD_DIM = 64
DILATIONS = (1, 4, 16)
HALF = 64
ATT_BQ = 128
ATT_W = ATT_BQ + 2 * HALF
REL_BUCKETS = 32
REL_MAX_DIST = 1024
DN_HEADS = 6
DN_HEAD_DIM = 128
DN_CONV = 5
DN_CHUNK = 128
D_FF = 2816
DN_IN = 3352
DN_IN_PAD = 3456
N_GATES = 4 * DN_HEADS

ADAM_LR = 0.001
ADAM_B1 = 0.9
ADAM_B2 = 0.999
ADAM_EPS = 1e-08
ADAM_WD = 0.01
ADAM_STEP = 10


def _tile(n, target, align):
    if n <= target:
        return n
    t = (target // align) * align
    while t >= align:
        if n % t == 0:
            return t
        t -= align
    raise ValueError(f"no tile for {n} (target {target}, align {align})")


def _params(*sem):
    return pltpu.CompilerParams(dimension_semantics=sem, vmem_limit_bytes=VMEM_LIMIT)


def _mm(a, b, *, name, ta=False, tb=False, out_shards=None, tm=1408, tn=1408, tk=1408, out_dtype=F32):
    if ta:
        K, M = a.shape
    else:
        M, K = a.shape
    sharded_b = b.ndim == 3
    if sharded_b:
        n_sh, b_rows, b_cols = b.shape
        N, K2 = (b_rows, n_sh * b_cols) if tb else (n_sh * b_cols, b_rows)
    else:
        N, K2 = b.shape if tb else b.shape[::-1]
    assert K == K2, (a.shape, b.shape, ta, tb)
    tm = _tile(M, tm, LANES if ta else SUBLANES)
    tn = N // out_shards if out_shards else (b_cols if sharded_b and not tb else _tile(N, tn, LANES))
    tk = b_cols if sharded_b and tb else _tile(K, tk, LANES)

    def vmem_bytes(tm_, tk_):
        return 2 * (tm_ * tk_ * a.dtype.itemsize + tk_ * tn * b.dtype.itemsize + tm_ * tn * jnp.dtype(out_dtype).itemsize)

    for rows in (tm, tm // 2):
        if not (sharded_b and tb) and M % rows == 0 and rows % LANES == 0 and vmem_bytes(rows, K) <= MM_WHOLE_K_BUDGET:
            tm, tk = rows, K
            break
    nk = K // tk
    a_spec = pl.BlockSpec((tk, tm), lambda i, j, k: (k, i)) if ta else pl.BlockSpec((tm, tk), lambda i, j, k: (i, k))
    if sharded_b:
        b_spec = (pl.BlockSpec((None, tn, tk), lambda i, j, k: (k, j, 0)) if tb
                  else pl.BlockSpec((None, tk, tn), lambda i, j, k: (j, k, 0)))
    else:
        b_spec = pl.BlockSpec((tn, tk), lambda i, j, k: (j, k)) if tb else pl.BlockSpec((tk, tn), lambda i, j, k: (k, j))
    if out_shards:
        out_spec = pl.BlockSpec((None, tm, tn), lambda i, j, k: (j, i, 0))
        out_shape = jax.ShapeDtypeStruct((out_shards, M, tn), out_dtype)
    else:
        out_spec = pl.BlockSpec((tm, tn), lambda i, j, k: (i, j))
        out_shape = jax.ShapeDtypeStruct((M, N), out_dtype)
    dims = (((0 if ta else 1,), (1 if tb else 0,)), ((), ()))

    def product(a_ref, b_ref):
        return lax.dot_general(a_ref[...].astype(MXU_DTYPE), b_ref[...].astype(MXU_DTYPE), dims, preferred_element_type=F32)

    def kern_whole(a_ref, b_ref, o_ref):
        o_ref[...] = product(a_ref, b_ref).astype(o_ref.dtype)

    def kern_steps(a_ref, b_ref, o_ref, acc_ref):
        k = pl.program_id(2)

        @pl.when(k == 0)
        def _():
            acc_ref[...] = jnp.zeros_like(acc_ref)

        acc_ref[...] += product(a_ref, b_ref)

        @pl.when(k == nk - 1)
        def _():
            o_ref[...] = acc_ref[...].astype(o_ref.dtype)

    return pl.pallas_call(
        kern_whole if nk == 1 else kern_steps, name=name, grid=(M // tm, N // tn, nk), in_specs=[a_spec, b_spec],
        out_specs=out_spec, out_shape=out_shape,
        scratch_shapes=[] if nk == 1 else [pltpu.VMEM((tm, tn), F32)],
        compiler_params=_params("parallel", "parallel", "arbitrary"),
    )(a, b)


def _shards_of(w):
    return w.shape[0] if w.ndim == 3 else None


def _col(arr, width, blk):
    return (arr, width, blk)


def _rowwise(body, rows, consts, out_rows, out_acc, *, tm, name):
    n_rows = (rows[0][0] if isinstance(rows[0], tuple) else rows[0]).shape[0]
    assert n_rows % tm == 0, (n_rows, tm)
    arrs, in_specs = [], []
    for r in rows:
        arr, width, blk = r if isinstance(r, tuple) else (r, r.shape[1], 0)
        assert arr.shape[0] == n_rows
        arrs.append(arr)
        in_specs.append(pl.BlockSpec((tm, width), functools.partial(lambda i, b: (i, b), b=blk)))
    for c in consts:
        arrs.append(c)
        in_specs.append(pl.BlockSpec(c.shape, functools.partial(lambda i, n: (0,) * n, n=c.ndim)))
    n_in, n_ro = len(arrs), len(out_rows)
    out_shape = [jax.ShapeDtypeStruct((n_rows, w), dt) for w, dt in out_rows]
    out_specs = [pl.BlockSpec((tm, w), lambda i: (i, 0)) for w, _ in out_rows]
    out_shape += [jax.ShapeDtypeStruct(s, F32) for s in out_acc]
    out_specs += [pl.BlockSpec(s, lambda i: (0, 0)) for s in out_acc]

    def kern(*refs):
        ro, ao = body(*[r[...] for r in refs[:n_in]])
        outs = refs[n_in:]
        for r, v in zip(outs[:n_ro], ro, strict=True):
            r[...] = v.astype(r.dtype)
        if out_acc:
            @pl.when(pl.program_id(0) == 0)
            def _():
                for r in outs[n_ro:]:
                    r[...] = jnp.zeros_like(r)

            for r, v in zip(outs[n_ro:], ao, strict=True):
                r[...] += v

    res = pl.pallas_call(
        kern, name=name, grid=(n_rows // tm,), in_specs=in_specs, out_specs=out_specs, out_shape=out_shape,
        compiler_params=_params("arbitrary" if out_acc else "parallel"),
    )(*arrs)
    return res


def _rms(x, gain):
    return x * lax.rsqrt(jnp.mean(x * x, axis=-1, keepdims=True) + EPS) * gain


def _silu(x):
    return x * jax.nn.sigmoid(x)


def _softplus(x):
    return jnp.maximum(x, 0.0) + jnp.log(1.0 + jnp.exp(-jnp.abs(x)))


def _dot_nt(a, b, precision=None):
    return lax.dot_general(a, b, (((1,), (1,)), ((), ())), preferred_element_type=F32, precision=precision)


def _dot_tn(a, b, precision=None):
    return lax.dot_general(a, b, (((0,), (0,)), ((), ())), preferred_element_type=F32, precision=precision)


def _dot(a, b, precision=None):
    return jnp.dot(a, b, preferred_element_type=F32, precision=precision)


def _pre_norm(x, gain, *, name):
    def body(x, g):
        return (_rms(x, g),), ()
    return _rowwise(body, [x], [gain], [(x.shape[1], MXU_DTYPE)], [], tm=_tile(x.shape[0], 512, 2 * SUBLANES), name=name)[0]


def _pre_norm_bwd(x, gain, dh, dx_other, *, name):
    def body(x, dh, dxo, g):
        _, vjp = jax.vjp(_rms, x, g)
        dx, dg = vjp(dh)
        return (dx + dxo,), (dg,)
    return _rowwise(body, [x, dh, dx_other], [gain], [(x.shape[1], F32)], [gain.shape], tm=512, name=name)


def _gain_bwd(x, gain, dh, *, name):
    def body(x, dh, g):
        _, vjp = jax.vjp(lambda g_: _rms(x, g_), g)
        return (), (vjp(dh)[0],)
    return _rowwise(body, [x, dh], [gain], [], [gain.shape], tm=_tile(x.shape[0], 512, SUBLANES), name=name)[0]


def _res_block(x_res, m, g_post, g_pre):
    x_new = x_res + _rms(m, g_post)
    return x_new, _rms(x_new, g_pre)


def _post_pre(x_res, m, g_post, g_pre, *, name):
    def body(x, m, gp, gq):
        return _res_block(x, m, gp, gq), ()
    d = x_res.shape[1]
    return _rowwise(body, [x_res, m], [g_post, g_pre], [(d, F32), (d, MXU_DTYPE)], [], tm=512, name=name)


def _post_pre_bwd(x_res, m, g_post, g_pre, dx_new, dh, *, name):
    def body(x, m, dxn, dh, gp, gq):
        _, vjp = jax.vjp(_res_block, x, m, gp, gq)
        dx, dm, dgp, dgq = vjp((dxn, dh))
        return (dx, dm), (dgp, dgq)
    d = x_res.shape[1]
    return _rowwise(body, [x_res, m, dx_new, dh], [g_post, g_pre], [(d, F32), (d, MXU_DTYPE)],
                    [g_post.shape, g_pre.shape], tm=256, name=name)


def _final_loss_bwd(x_res, m, g_post, target, *, name):
    d = x_res.shape[1]

    def loss_cols(x, m, g, t):
        err = x + _rms(m, g) - t
        return jnp.sum(err * err, axis=0, keepdims=True) * (0.5 / d)

    def body(x, m, t, g):
        cols, vjp = jax.vjp(lambda x_, m_, g_: loss_cols(x_, m_, g_, t), x, m, g)
        dx, dm, dg = vjp(jnp.ones_like(cols))
        return (dx, dm), (dg, cols)
    return _rowwise(body, [x_res, m, target], [g_post], [(d, F32), (d, MXU_DTYPE)], [g_post.shape, (1, d)], tm=256, name=name)


def _swiglu_act(gu, *, name):
    def body(gate, up):
        return (_silu(gate.astype(F32)) * up.astype(F32),), ()
    return _rowwise(body, [_col(gu, D_FF, 0), _col(gu, D_FF, 1)], [], [(D_FF, MXU_DTYPE)], [], tm=256, name=name)[0]


def _swiglu_act_bwd(gu, da, *, name):
    def body(gate, up, da):
        _, vjp = jax.vjp(lambda g, u: _silu(g) * u, gate.astype(F32), up.astype(F32))
        dg, du = vjp(da.astype(F32))
        return (jnp.concatenate([dg, du], axis=1),), ()
    return _rowwise(body, [_col(gu, D_FF, 0), _col(gu, D_FF, 1), da], [], [(2 * D_FF, MXU_DTYPE)], [], tm=256, name=name)[0]


def _lane_head_mask(width, head_dim, head):
    lane = lax.broadcasted_iota(jnp.int32, (1, width), 1)
    return (lane // head_dim) == head


def _mem_attn_heads(q4, k4, v4):
    logits = _bdot_nt(q4, k4)
    p = jnp.exp(logits - jnp.max(logits, axis=-1, keepdims=True))
    return _bdot(p / jnp.sum(p, axis=-1, keepdims=True), v4)


def _mem_heads(q_mem, kv):
    return _heads(q_mem * (ATT_HEAD_DIM ** -0.5), mask=True), _heads(kv[:, :MEM_WIDTH]), _heads(kv[:, MEM_WIDTH:])


def _mem_attn(q_mem, kv):
    return _join_heads(_mem_attn_heads(*_mem_heads(q_mem, kv)))


def _mem_attn_bwd(q_mem, kv, do):
    _, vjp = jax.vjp(_mem_attn_heads, *_mem_heads(q_mem, kv))
    dq4, dk4, dv4 = vjp(_heads(do, mask=True))
    return (_join_heads(dq4) * (ATT_HEAD_DIM ** -0.5),
            jnp.concatenate([dk4[0] + dk4[1], dk4[2] + dk4[3], dv4[0] + dv4[1], dv4[2] + dv4[3]], axis=1))


def _t5_bucket(rel):
    half = REL_BUCKETS // 2
    max_exact = half // 2
    n = np.abs(rel)
    large = max_exact + (np.log(np.maximum(n, 1) / max_exact) / math.log(REL_MAX_DIST / max_exact)
                         * (half - max_exact)).astype(np.int64)
    large = np.minimum(large, half - 1)
    return ((rel > 0) * half + np.where(n < max_exact, n, large)).astype(np.int32)


ATT_DIAGS = ATT_BQ + ATT_W - 1


def _bias_diag_onehot(dil):
    j = np.arange(ATT_DIAGS)
    tiles = []
    for off in (-HALF, 0, HALF):
        rel = j - (ATT_BQ - 1) - HALF - off
        hot = _t5_bucket(rel * dil)[:, None] == np.arange(REL_BUCKETS)[None, :]
        tiles.append(hot & (np.abs(rel) <= HALF)[:, None])
    return np.stack(tiles).astype(np.float32)


def _toeplitz(r):
    lead = r.shape[:-1]
    a = jnp.broadcast_to(r[..., None, :], lead + (ATT_BQ, ATT_DIAGS))
    a = jnp.pad(a, [(0, 0)] * len(lead) + [(0, 0), (0, 1)])
    a = a.reshape(lead + (ATT_BQ * (ATT_DIAGS + 1),))[..., : ATT_BQ * ATT_DIAGS].reshape(lead + (ATT_BQ, ATT_DIAGS))
    return a[..., ATT_BQ - 1: ATT_BQ - 1 + ATT_W]


def _bias_tiles(rel_bias, gi):
    heads = rel_bias[:, 4 * gi: 4 * gi + 4]
    diag = jnp.einsum('tnb,bh->thn', jnp.asarray(_bias_diag_onehot(DILATIONS[gi])), heads, precision=HI)
    return _toeplitz(diag)


def _bias_tiles_bwd(rel_bias, dtiles, gi):
    return jax.vjp(lambda rb: _bias_tiles(rb, gi), rel_bias)[1](dtiles)[0]


def _att_window(i, n_sub):
    start = jnp.clip(i * ATT_BQ - HALF, 0, n_sub - ATT_W)
    off = i * ATT_BQ - HALF - start
    return pl.multiple_of(start, HALF), off


def _att_valid(off):
    q = lax.broadcasted_iota(jnp.int32, (ATT_BQ, ATT_W), 0)
    kk = lax.broadcasted_iota(jnp.int32, (ATT_BQ, ATT_W), 1)
    return jnp.abs(kk - q - HALF - off) <= HALF


def _att_tile_id(i, nq):
    return jnp.where(i == 0, 0, jnp.where(i == nq - 1, 2, 1))


ATT_GROUP_HEADS = 4


def _heads(x, mask=False):
    out = []
    for p in range(2):
        pair = x[:, p * LANES: (p + 1) * LANES]
        for h in range(2):
            out.append(jnp.where(_lane_head_mask(LANES, ATT_HEAD_DIM, h), pair, 0.0) if mask else pair)
    return jnp.stack(out)


def _join_heads(x):
    first = _lane_head_mask(LANES, ATT_HEAD_DIM, 0)
    return jnp.concatenate([jnp.where(first, x[2 * p], x[2 * p + 1]) for p in range(2)], axis=1)


def _head_scalar(x):
    out = []
    for p in range(2):
        pair = x[:, p * LANES: (p + 1) * LANES]
        for h in range(2):
            out.append(jnp.max(jnp.where(_lane_head_mask(LANES, ATT_HEAD_DIM, h), pair, NEG_INF), axis=-1, keepdims=True))
    return jnp.stack(out)


def _bdot(a, b):
    return jnp.einsum('hqk,hkd->hqd', a, b, preferred_element_type=F32)


def _bdot_nt(a, b):
    return jnp.einsum('hqd,hkd->hqk', a, b, preferred_element_type=F32)


def _bdot_tn(a, b):
    return jnp.einsum('hqk,hqd->hkd', a, b, preferred_element_type=F32)


def _residue_view(arr, col_blocks, dil):
    if dil == 1:
        return arr, lambda r, j: col_blocks[j]
    width = 2 * LANES
    picked = jnp.concatenate([arr[:, b * width: (b + 1) * width] for b in col_blocks], axis=1)
    n = len(col_blocks)
    return picked.reshape(arr.shape[0] // dil, dil * n * width), lambda r, j: r * n + j


def _att_fwd(qkvm, bias, gi, *, name):
    dil = DILATIONS[gi]
    s_len = qkvm.shape[0]
    n_sub = s_len // dil
    nq = n_sub // ATT_BQ
    assert n_sub % ATT_BQ == 0 and n_sub >= ATT_W
    view, at = _residue_view(qkvm, [gi, 3 + gi, 6 + gi], dil)

    def kern(q_ref, k_ref, v_ref, b_ref, o_ref, lse_ref):
        start, off = _att_window(pl.program_id(1), n_sub)
        valid = _att_valid(off)
        q4 = _heads(q_ref[...].astype(F32) * (ATT_HEAD_DIM ** -0.5), mask=True)
        k4 = _heads(k_ref[pl.ds(start, ATT_W), :].astype(F32))
        v4 = _heads(v_ref[pl.ds(start, ATT_W), :].astype(F32))
        s = jnp.where(valid, _bdot_nt(q4, k4) + b_ref[...], NEG_INF)
        mx = jnp.max(s, axis=-1, keepdims=True)
        p = jnp.exp(s - mx)
        den = jnp.sum(p, axis=-1, keepdims=True)
        o_ref[...] = _join_heads(_bdot(p, v4) / den)
        lse_ref[...] = _join_heads(jnp.broadcast_to(mx + jnp.log(den), (ATT_GROUP_HEADS, ATT_BQ, LANES)))

    def qkv_spec(which, full):
        shape = (n_sub, 2 * LANES) if full else (ATT_BQ, 2 * LANES)
        return pl.BlockSpec(shape, lambda r, i: (0 if full else i, at(r, which)))

    out_spec = pl.BlockSpec((ATT_BQ, 2 * LANES), lambda r, i: (i, r))
    o, lse = pl.pallas_call(
        kern, name=name, grid=(dil, nq),
        in_specs=[qkv_spec(0, False), qkv_spec(1, True), qkv_spec(2, True),
                  pl.BlockSpec((None, ATT_GROUP_HEADS, ATT_BQ, ATT_W), lambda r, i: (_att_tile_id(i, nq), 0, 0, 0))],
        out_specs=[out_spec, out_spec],
        out_shape=[jax.ShapeDtypeStruct((n_sub, dil * 2 * LANES), F32)] * 2,
        compiler_params=_params("parallel", "arbitrary"),
    )(view, view, view, bias)
    return o.reshape(s_len, 2 * LANES), lse.reshape(s_len, 2 * LANES)


def _att_bwd(qkvm, bias, lse_tot, delta, dcat, gi, *, name):
    dil = DILATIONS[gi]
    s_len = qkvm.shape[0]
    n_sub = s_len // dil
    nq = n_sub // ATT_BQ
    view, at = _residue_view(qkvm, [gi, 3 + gi, 6 + gi], dil)
    lse_v = lse_tot.reshape(n_sub, dil * 2 * LANES)
    delta_v = delta.reshape(n_sub, dil * 2 * LANES)
    dcat_v, dcat_at = _residue_view(dcat, [gi], dil)

    def kern(q_ref, k_ref, v_ref, b_ref, lse_ref, dl_ref, dm_ref, dq_ref, dk_ref, dv_ref, db_ref):
        r, i = pl.program_id(0), pl.program_id(1)
        start, off = _att_window(i, n_sub)
        valid = _att_valid(off)
        tile = _att_tile_id(i, nq)

        @pl.when(i == 0)
        def _():
            dk_ref[...] = jnp.zeros_like(dk_ref)
            dv_ref[...] = jnp.zeros_like(dv_ref)

        @pl.when((i == 0) & (r == 0))
        def _():
            db_ref[...] = jnp.zeros_like(db_ref)

        q4 = _heads(q_ref[...].astype(F32) * (ATT_HEAD_DIM ** -0.5), mask=True)
        k4 = _heads(k_ref[pl.ds(start, ATT_W), :].astype(F32))
        v4 = _heads(v_ref[pl.ds(start, ATT_W), :].astype(F32))
        dm4 = _heads(dm_ref[...], mask=True)
        s = jnp.where(valid, _bdot_nt(q4, k4) + b_ref[tile], NEG_INF)
        p = jnp.exp(s - _head_scalar(lse_ref[...]))
        ds = p * (_bdot_nt(dm4, v4) - _head_scalar(dl_ref[...]))
        dq_ref[...] = _join_heads(_bdot(ds, k4)) * (ATT_HEAD_DIM ** -0.5)
        dk4 = _bdot_tn(ds, q4)
        dv4 = _bdot_tn(p, dm4)
        dk_ref[pl.ds(start, ATT_W), :] += jnp.concatenate([dk4[0] + dk4[1], dk4[2] + dk4[3]], axis=1)
        dv_ref[pl.ds(start, ATT_W), :] += jnp.concatenate([dv4[0] + dv4[1], dv4[2] + dv4[3]], axis=1)
        db_ref[tile] += ds

    def qkv_spec(which, full):
        shape = (n_sub, 2 * LANES) if full else (ATT_BQ, 2 * LANES)
        return pl.BlockSpec(shape, lambda r, i: (0 if full else i, at(r, which)))

    blk = pl.BlockSpec((ATT_BQ, 2 * LANES), lambda r, i: (i, r))
    full = pl.BlockSpec((n_sub, 2 * LANES), lambda r, i: (0, r))
    bias_spec = pl.BlockSpec(bias.shape, lambda r, i: (0, 0, 0, 0))
    sub = jax.ShapeDtypeStruct((n_sub, dil * 2 * LANES), F32)
    dq, dk, dv, db = pl.pallas_call(
        kern, name=name, grid=(dil, nq),
        in_specs=[qkv_spec(0, False), qkv_spec(1, True), qkv_spec(2, True), bias_spec, blk, blk,
                  pl.BlockSpec((ATT_BQ, 2 * LANES), lambda r, i: (i, dcat_at(r, 0)))],
        out_specs=[blk, full, full, bias_spec],
        out_shape=[sub, sub, sub, jax.ShapeDtypeStruct(bias.shape, F32)],
        compiler_params=_params("arbitrary", "arbitrary"),
    )(view, view, view, bias, lse_v, delta_v, dcat_v)
    return dq.reshape(s_len, -1), dk.reshape(s_len, -1), dv.reshape(s_len, -1), db


def _att_combine(o_g, lse_g, qkvm, kv_mem, *, name):
    def body(o0, o1, o2, l0, l1, l2, qm, kv):
        mx = jnp.maximum(jnp.maximum(l0, l1), l2)
        tot = mx + jnp.log(jnp.exp(l0 - mx) + jnp.exp(l1 - mx) + jnp.exp(l2 - mx))
        mixed = [o * jnp.exp(l - tot) for o, l in ((o0, l0), (o1, l1), (o2, l2))]
        return (jnp.concatenate(mixed + [_mem_attn(qm.astype(F32), kv)], axis=1), tot), ()
    return _rowwise(body, list(o_g) + list(lse_g) + [_col(qkvm, MEM_WIDTH, (3 * TOK_WIDTH) // MEM_WIDTH)], [kv_mem],
                    [(D_MODEL, F32), (MEM_WIDTH, F32)], [], tm=256, name=name)


def _head_sum_matrix():
    a = np.arange(MEM_WIDTH)
    return jnp.asarray((a[:, None] // ATT_HEAD_DIM == a[None, :] // ATT_HEAD_DIM).astype(np.float32))


def _att_bwd_prep(cat, dcat, qkvm, kv_mem, *, name):
    def body(cat, dcat, qm, kv, hs):
        prod = cat * dcat
        summed = prod[:, 0:256] + prod[:, 256:512] + prod[:, 512:768]
        delta = _dot(summed, hs, precision=HI)
        dqm, dkv = _mem_attn_bwd(qm.astype(F32), kv, dcat[:, TOK_WIDTH:])
        return (delta, dqm), (dkv,)
    return _rowwise(body, [cat, dcat, _col(qkvm, MEM_WIDTH, (3 * TOK_WIDTH) // MEM_WIDTH)], [kv_mem, _head_sum_matrix()],
                    [(MEM_WIDTH, F32), (MEM_WIDTH, F32)], [kv_mem.shape], tm=256, name=name)


def _dn_conv_post(s, j):
    scale = jnp.where(j < DN_HEADS, DN_HEAD_DIM ** -0.5, 1.0)
    normed = s * lax.rsqrt(jnp.sum(s * s, axis=-1, keepdims=True) + EPS) * scale
    return jnp.where(j >= 2 * DN_HEADS, s, normed)


def _shift_rows(x, sh):
    n = x.shape[0]
    row = lax.broadcasted_iota(jnp.int32, (n, 1), 0)
    rolled = pltpu.roll(x, (-sh) % n, 0)
    return jnp.where((row + sh >= 0) & (row + sh < n), rolled, 0.0)


def _dn_conv_taps(x, w_ref):
    c = x * w_ref[pl.ds(DN_CONV // 2, 1), :]
    for jj in range(DN_CONV):
        if jj != DN_CONV // 2:
            c = c + _shift_rows(x, jj - DN_CONV // 2) * w_ref[pl.ds(jj, 1), :]
    return c


def _dn_conv_fwd(proj, conv_w, *, name):
    s_len = proj.shape[0]
    width = 3 * TOK_WIDTH

    def kern(x_ref, w_ref, o_ref):
        j = pl.program_id(0)
        o_ref[...] = _dn_conv_post(_silu(_dn_conv_taps(x_ref[...], w_ref)), j)

    return pl.pallas_call(
        kern, name=name, grid=(width // LANES,),
        in_specs=[pl.BlockSpec((s_len, LANES), lambda j: (0, j)), pl.BlockSpec((DN_CONV, LANES), lambda j: (0, j))],
        out_specs=pl.BlockSpec((s_len, LANES), lambda j: (0, j)),
        out_shape=jax.ShapeDtypeStruct((s_len, width), F32),
        compiler_params=_params("parallel"),
    )(proj, conv_w)


def _dn_conv_bwd(proj, conv_w, d_fwd, d_bwd, which, *, name):
    s_len = proj.shape[0]

    def kern(x_ref, w_ref, df_ref, db_ref, dx_ref, dw_ref):
        j = pl.program_id(0) + which * DN_HEADS
        x = x_ref[...]
        c = _dn_conv_taps(x, w_ref)
        _, vjp = jax.vjp(lambda c_: _dn_conv_post(_silu(c_), j), c)
        dc = vjp(df_ref[...] + db_ref[...])[0]
        dx = dc * w_ref[pl.ds(DN_CONV // 2, 1), :]
        for jj in range(DN_CONV):
            sh = jj - DN_CONV // 2
            if sh != 0:
                dx = dx + _shift_rows(dc, -sh) * w_ref[pl.ds(jj, 1), :]
            dw_ref[pl.ds(jj, 1), :] = jnp.sum(dc * _shift_rows(x, sh), axis=0, keepdims=True)
        dx_ref[...] = dx

    return pl.pallas_call(
        kern, name=name, grid=(DN_HEADS,),
        in_specs=[pl.BlockSpec((s_len, LANES), lambda j: (0, j + which * DN_HEADS)),
                  pl.BlockSpec((DN_CONV, LANES), lambda j: (0, j + which * DN_HEADS)),
                  pl.BlockSpec((s_len, LANES), lambda j: (0, j)),
                  pl.BlockSpec((s_len, LANES), lambda j: (0, j))],
        out_specs=[pl.BlockSpec((s_len, LANES), lambda j: (0, j)), pl.BlockSpec((DN_CONV, LANES), lambda j: (0, j))],
        out_shape=[jax.ShapeDtypeStruct((s_len, TOK_WIDTH), F32), jax.ShapeDtypeStruct((DN_CONV, TOK_WIDTH), F32)],
        compiler_params=_params("parallel"),
    )(proj, conv_w, d_fwd, d_bwd)


GATE_TM = 2 * DN_CHUNK


FWD_GATE_LANES = 2 * DN_HEADS


def _gate_constants():
    i = np.arange(GATE_TM)
    same = (i[:, None] // DN_CHUNK) == (i[None, :] // DN_CHUNK)
    cum_f = same & (i[None, :] <= i[:, None])
    cum_r = same & (i[None, :] >= i[:, None])
    return tuple(jnp.asarray(np.asarray(a, np.float32)) for a in (cum_f, cum_r, same))


def _gate_params(p):
    z = jnp.zeros((DN_HEADS,), F32)
    return jnp.concatenate([p[0], z, p[1], z, jnp.zeros((LANES - N_GATES,), F32)]).reshape(1, LANES)


def _gate_params_bwd(dp):
    return jnp.stack([dp[0, 0:DN_HEADS], dp[0, 2 * DN_HEADS: 3 * DN_HEADS]])


def _dn_gates(gate_in, a_cols, dt_cols, cum_f, cum_r, tot):
    g = -jnp.exp(a_cols) * _softplus(gate_in + dt_cols)
    fwd_lane = lax.broadcasted_iota(jnp.int32, (1, LANES), 1) < FWD_GATE_LANES
    gc = jnp.where(fwd_lane, _dot(cum_f, g, precision=HI), _dot(cum_r, g, precision=HI))
    return gc, _dot(tot, g, precision=HI), jax.nn.sigmoid(gate_in)


def _dn_gates_fwd(proj, a_cols, dt_cols, *, name):
    def body(gi, *consts):
        return _dn_gates(gi, *consts), ()
    return _rowwise(body, [_col(proj, LANES, DN_IN_PAD // LANES - 1)], [a_cols, dt_cols, *_gate_constants()],
                    [(LANES, F32)] * 3, [], tm=GATE_TM, name=name)


def _dn_gates_bwd(proj, a_cols, dt_cols, d_gates, *, name):
    def body(gi, gcf, gtf, bf, gcr, gtr, br, a, dt, *consts):
        _, vjp = jax.vjp(lambda gi_, a_, dt_: _dn_gates(gi_, a_, dt_, *consts), gi, a, dt)
        dgi, da, ddt = vjp((gcf + gcr, gtf + gtr, bf + br))
        return (dgi,), (da, ddt)
    return _rowwise(body, [_col(proj, LANES, DN_IN_PAD // LANES - 1), *d_gates[0], *d_gates[1]],
                    [a_cols, dt_cols, *_gate_constants()], [(LANES, F32)], [a_cols.shape, dt_cols.shape],
                    tm=GATE_TM, name=name)


INV_BASE = 8


def _block_id_equal(c, size):
    i = lax.broadcasted_iota(jnp.int32, (c, c), 0) // size
    j = lax.broadcasted_iota(jnp.int32, (c, c), 1) // size
    return (i == j).astype(F32)


def _unit_tri_inverse_impl(lmat):
    c = lmat.shape[0]
    eye = _block_id_equal(c, 1)
    same = _block_id_equal(c, INV_BASE)
    neg = -lmat * same
    inv = eye + neg
    power = neg
    for _ in range(int(math.log2(INV_BASE)) - 1):
        power = _dot(power, power)
        inv = inv + _dot(inv, power)
    size = INV_BASE
    while size < c:
        bigger = _block_id_equal(c, 2 * size)
        inv = inv - _dot(_dot(inv, lmat * (bigger - same)), inv)
        same, size = bigger, 2 * size
    resid = eye - _dot(eye + lmat, inv, precision=HI)
    return inv + _dot(inv, resid)


@jax.custom_vjp
def _unit_tri_inverse(lmat):
    return _unit_tri_inverse_impl(lmat)


def _unit_tri_inverse_fwd(lmat):
    inv = _unit_tri_inverse_impl(lmat)
    return inv, inv


def _unit_tri_inverse_bwd(inv, d_inv):
    return (-_dot_tn(inv, _dot_nt(d_inv, inv)),)


_unit_tri_inverse.defvjp(_unit_tri_inverse_fwd, _unit_tri_inverse_bwd)


def _dn_chunk(q, k, v, gates_t, gc_row, tot_row, beta_row, state, tri, inverse):
    c = q.shape[0]
    assert c == DN_HEAD_DIM
    eye = _block_id_equal(c, 1)

    def along_rows(x, pick):
        return jnp.broadcast_to(jnp.sum(x * pick, axis=0, keepdims=True), (c, c))

    gc_j = along_rows(gates_t[0], gc_row)
    gc = gc_j.T
    g_tot = along_rows(gates_t[1], tot_row)
    beta = along_rows(gates_t[2], beta_row).T
    decay = jnp.exp(jnp.where(tri > 0, gc - gc_j, NEG_INF))
    k_beta = k * beta
    inv = inverse((tri - eye) * (_dot_nt(k_beta, k) * decay))
    e_gc = jnp.exp(gc)
    u = _dot(inv, v * beta)
    w = _dot(inv, k_beta * e_gc)
    intra = tri * (_dot_nt(q, k) * decay)
    v_new = u - _dot(w, state)
    out = _dot(q * e_gc, state) + _dot(intra, v_new)
    state = state * jnp.exp(g_tot) + _dot_tn(k * jnp.exp(g_tot - gc), v_new)
    return out, state


def _dn_tri():
    i = np.arange(DN_CHUNK)
    tri = np.stack([(i[None, :] <= i[:, None]), (i[None, :] >= i[:, None])]).astype(np.float32)
    return jnp.asarray(np.repeat(tri, DN_HEADS, axis=0))


def _dn_gate_picks():
    picks = np.zeros((3, 2 * DN_HEADS, 2 * DN_CHUNK, 1), np.float32)
    for d in range(2):
        for h in range(DN_HEADS):
            alpha = d * DN_CHUNK + d * 2 * DN_HEADS + h
            picks[0, d * DN_HEADS + h, alpha] = 1.0
            picks[1, d * DN_HEADS + h, alpha] = 1.0
            picks[2, d * DN_HEADS + h, alpha + DN_HEADS] = 1.0
    return jnp.asarray(picks)


def _stack_chains(fwd_ref, rev_ref):
    return jnp.stack([r[:, _head_cols(h)] for r in (fwd_ref, rev_ref) for h in range(DN_HEADS)])


def _unstack_chains(val, fwd_ref, rev_ref):
    for d, r in enumerate((fwd_ref, rev_ref)):
        for h in range(DN_HEADS):
            r[:, _head_cols(h)] = val[d * DN_HEADS + h]


def _gates_transposed(fwd_refs, rev_refs):
    return jnp.stack([jnp.concatenate([f[...].T, r[...].T], axis=0) for f, r in zip(fwd_refs, rev_refs, strict=True)])


def _dn_row_spec(nc, col, reverse, width=TOK_WIDTH):
    return pl.BlockSpec((DN_CHUNK, width), lambda t: ((nc - 1 - t) if reverse else t, col))


def _dn_state_spec(nc, reverse):
    return pl.BlockSpec((None, DN_HEADS, DN_HEAD_DIM, DN_HEAD_DIM), lambda t: ((nc - 1 - t) if reverse else t, 0, 0, 0))


def _head_cols(h):
    return pl.ds(h * DN_HEAD_DIM, DN_HEAD_DIM)


def _const_spec(arr):
    return pl.BlockSpec(arr.shape, functools.partial(lambda t, n: (0,) * n, n=arr.ndim))


def _dn_chains(inverse):
    return jax.vmap(lambda q, k, v, gates_t, *rest: _dn_chunk(q, k, v, gates_t, *rest, inverse),
                    in_axes=(0, 0, 0, None, 0, 0, 0, 0, 0))


def _dn_scan_fwd(qkv, gates, *, name):
    s_len = qkv.shape[0]
    nc = s_len // DN_CHUNK
    tri, picks = _dn_tri(), _dn_gate_picks()

    def kern(*refs):
        ins, (tri_ref, pick_ref, of_ref, or_ref, sf_ref, sr_ref, state) = refs[:12], refs[12:]

        @pl.when(pl.program_id(0) == 0)
        def _():
            state[...] = jnp.zeros_like(state)

        entry = state[...]
        qkv_c = [_stack_chains(ins[i], ins[6 + i]) for i in range(3)]
        gates_t = _gates_transposed(ins[3:6], ins[9:12])
        out, new = _dn_chains(_unit_tri_inverse_impl)(*qkv_c, gates_t, pick_ref[0], pick_ref[1], pick_ref[2], entry, tri_ref[...])
        sf_ref[...] = entry[:DN_HEADS]
        sr_ref[...] = entry[DN_HEADS:]
        _unstack_chains(out, of_ref, or_ref)
        state[...] = new

    in_specs = []
    for rev in (False, True):
        in_specs += [_dn_row_spec(nc, col, rev) for col in (0, 1, 2)] + [_dn_row_spec(nc, 0, rev, LANES)] * 3
    in_specs += [_const_spec(tri), _const_spec(picks)]
    return pl.pallas_call(
        kern, name=name, grid=(nc,), in_specs=in_specs,
        out_specs=[_dn_row_spec(nc, 0, False), _dn_row_spec(nc, 0, True), _dn_state_spec(nc, False), _dn_state_spec(nc, True)],
        out_shape=[jax.ShapeDtypeStruct((s_len, TOK_WIDTH), F32)] * 2
        + [jax.ShapeDtypeStruct((nc, DN_HEADS, DN_HEAD_DIM, DN_HEAD_DIM), F32)] * 2,
        scratch_shapes=[pltpu.VMEM((2 * DN_HEADS, DN_HEAD_DIM, DN_HEAD_DIM), F32)],
        compiler_params=_params("arbitrary"),
    )(*([qkv, qkv, qkv, *gates] * 2), tri, picks)


def _dn_scan_bwd(qkv, gates, states, d_o, *, name):
    s_len = qkv.shape[0]
    nc = s_len // DN_CHUNK
    tri, picks = _dn_tri(), _dn_gate_picks()

    def kern(*refs):
        ins, tri_ref, pick_ref, outs, d_state = refs[:16], refs[16], refs[17], refs[18:30], refs[30]

        @pl.when(pl.program_id(0) == 0)
        def _():
            d_state[...] = jnp.zeros_like(d_state)

        qkv_c = [_stack_chains(ins[i], ins[8 + i]) for i in range(3)]
        gates_t = _gates_transposed(ins[3:6], ins[11:14])
        entry = jnp.concatenate([ins[6][...], ins[14][...]], axis=0)
        d_out = _stack_chains(ins[7], ins[15])
        tri_v, picks_v = tri_ref[...], pick_ref[...]
        _, vjp = jax.vjp(lambda q, k, v, g, s: _dn_chains(_unit_tri_inverse)(q, k, v, g, picks_v[0], picks_v[1], picks_v[2], s, tri_v),
                         *qkv_c, gates_t, entry)
        dq, dk, dv, d_gates_t, d_entry = vjp((d_out, d_state[...]))
        for i, val in enumerate((dq, dk, dv)):
            _unstack_chains(val, outs[i], outs[6 + i])
        for i in range(3):
            outs[3 + i][...] = d_gates_t[i, :DN_CHUNK].T
            outs[9 + i][...] = d_gates_t[i, DN_CHUNK:].T
        d_state[...] = d_entry

    in_specs, out_specs, out_shape = [], [], []
    for rev in (True, False):
        in_specs += [_dn_row_spec(nc, col, rev) for col in (0, 1, 2)] + [_dn_row_spec(nc, 0, rev, LANES)] * 3
        in_specs += [_dn_state_spec(nc, rev), _dn_row_spec(nc, 0, rev)]
        out_specs += [_dn_row_spec(nc, 0, rev)] * 3 + [_dn_row_spec(nc, 0, rev, LANES)] * 3
        out_shape += [jax.ShapeDtypeStruct((s_len, TOK_WIDTH), F32)] * 3 + [jax.ShapeDtypeStruct((s_len, LANES), F32)] * 3
    in_specs += [_const_spec(tri), _const_spec(picks)]
    res = pl.pallas_call(
        kern, name=name, grid=(nc,), in_specs=in_specs, out_specs=out_specs, out_shape=out_shape,
        scratch_shapes=[pltpu.VMEM((2 * DN_HEADS, DN_HEAD_DIM, DN_HEAD_DIM), F32)],
        compiler_params=_params("arbitrary"),
    )(*[a for d in range(2) for a in (qkv, qkv, qkv, *gates, states[d], d_o)], tri, picks)
    return (res[0:3], res[3:6]), (res[6:9], res[9:12])


def _dn_out_head(o_f, o_b, z, gain):
    o = o_f + o_b
    return o * lax.rsqrt(jnp.mean(o * o, axis=-1, keepdims=True) + EPS) * gain * _silu(z)


def _dn_out(o_fwd, o_rev, proj, gain, qkv_kv_mem, *, name):
    def body(of, ob, z, qm, g, kv):
        heads = []
        for h in range(DN_HEADS):
            sl = slice(h * DN_HEAD_DIM, (h + 1) * DN_HEAD_DIM)
            heads.append(_dn_out_head(of[:, sl], ob[:, sl], z[:, sl], g))
        return (jnp.concatenate(heads + [_mem_attn(qm, kv)], axis=1),), ()
    return _rowwise(body, [o_fwd, o_rev, _col(proj, TOK_WIDTH, 3),
                           _col(proj, MEM_WIDTH, (4 * TOK_WIDTH) // MEM_WIDTH)], [gain, qkv_kv_mem],
                    [(D_MODEL, MXU_DTYPE)], [], tm=256, name=name)[0]


def _dn_out_bwd(o_fwd, o_rev, proj, gain, kv_mem, dcat, *, name):
    def body(of, ob, z, qm, dcat, g, kv):
        dos, dzs = [], []
        dgain = jnp.zeros_like(g)
        for h in range(DN_HEADS):
            sl = slice(h * DN_HEAD_DIM, (h + 1) * DN_HEAD_DIM)
            _, vjp = jax.vjp(_dn_out_head, of[:, sl], ob[:, sl], z[:, sl], g)
            d_of, _, dz, dg = vjp(dcat[:, sl])
            dos.append(d_of)
            dzs.append(dz)
            dgain = dgain + dg
        dqm, dkv = _mem_attn_bwd(qm, kv, dcat[:, TOK_WIDTH:])
        return (jnp.concatenate(dos, axis=1), jnp.concatenate(dzs, axis=1), dqm), (dgain, dkv)
    return _rowwise(body, [o_fwd, o_rev, _col(proj, TOK_WIDTH, 3),
                           _col(proj, MEM_WIDTH, (4 * TOK_WIDTH) // MEM_WIDTH), dcat], [gain, kv_mem],
                    [(TOK_WIDTH, F32), (TOK_WIDTH, F32), (MEM_WIDTH, F32)], [gain.shape, kv_mem.shape], tm=256, name=name)


def _pad_dn_w_in(w):
    gates = w[:, 4 * TOK_WIDTH: 4 * TOK_WIDTH + N_GATES]
    zeros = jnp.zeros((w.shape[0], DN_IN_PAD - DN_IN), w.dtype)
    return jnp.concatenate([w[:, :4 * TOK_WIDTH], w[:, 4 * TOK_WIDTH + N_GATES:], gates, zeros], axis=1)


def _unpad_dn_w_in(w):
    q_mem = w[:, 4 * TOK_WIDTH: 4 * TOK_WIDTH + MEM_WIDTH]
    gates = w[:, 4 * TOK_WIDTH + MEM_WIDTH: 4 * TOK_WIDTH + MEM_WIDTH + N_GATES]
    return jnp.concatenate([w[:, :4 * TOK_WIDTH], gates, q_mem], axis=1)


def _ffn_fwd(h, w_gu, w_d, tag):
    gu = _mm(h, w_gu, out_dtype=MXU_DTYPE, name=f"ffn_gu_{tag}")
    act = _swiglu_act(gu, name=f"ffn_act_{tag}")
    return gu, act, _mm(act, w_d, name=f"ffn_down_{tag}")


def _ffn_bwd(h, gu, act, w_gu, w_d, df, tag):
    d_act = _mm(df, w_d, tb=True, out_dtype=MXU_DTYPE, name=f"ffn_dact_{tag}")
    d_wd = _mm(act, df, ta=True, out_dtype=LINK_DTYPE, name=f"ffn_dwd_{tag}")
    d_gu = _swiglu_act_bwd(gu, d_act, name=f"ffn_dgu_{tag}")
    dh = _mm(d_gu, w_gu, tb=True, name=f"ffn_dh_{tag}")
    d_wgu = _mm(h, d_gu, ta=True, out_shards=_shards_of(w_gu), out_dtype=LINK_DTYPE, name=f"ffn_dwgu_{tag}")
    return dh, d_wgu, d_wd


def _local_step(x, mem, target, p):
    g = {}
    row = lambda v: v.reshape(1, -1)
    gains = {k: [row(p[k][i]) for i in range(2)] for k in
             ("mem_norm", "norm_mix_pre", "norm_mix_post", "norm_ffn_pre", "norm_ffn_post")}
    out_gain = row(p["dn_out_norm"])
    a_cols, dt_cols = _gate_params(p["dn_a_log"]), _gate_params(p["dn_dt_bias"])

    h0 = _pre_norm(x, gains["norm_mix_pre"][0], name="pre0")
    mem_n = [_pre_norm(mem, gains["mem_norm"][i], name=f"mem_norm{i}") for i in range(2)]
    qkvm = _mm(h0, p["att_w_in"], out_dtype=MXU_DTYPE, name="att_in")
    bias = [_bias_tiles(p["rel_bias"], gi) for gi in range(3)]
    att = [_att_fwd(qkvm, bias[gi], gi, name=f"att_fwd{gi}") for gi in range(3)]
    att_w_out, *mem_w_kv = p["att_rest_weights"](att[2][0])
    kv_mem = [_mm(mem_n[i], mem_w_kv[i], name=f"mem_kv{i}") for i in range(2)]
    cat0, lse_tot = _att_combine([a[0] for a in att], [a[1] for a in att], qkvm, kv_mem[0], name="att_combine")
    mo0 = _mm(cat0, att_w_out, name="att_out")
    x1, h1 = _post_pre(x, mo0, gains["norm_mix_post"][0], gains["norm_ffn_pre"][0], name="post_mix0")
    w_gu0, w_d0 = p["ffn_weights"](0, h1)
    gu0, act0, f0 = _ffn_fwd(h1, w_gu0, w_d0, 0)
    x2, h2 = _post_pre(x1, f0, gains["norm_ffn_post"][0], gains["norm_mix_pre"][1], name="post_ffn0")

    dn_w_in, dn_w_out, dn_conv_w = p["dn_weights"](h2)
    proj = _mm(h2, dn_w_in, name="dn_in")
    qkv = _dn_conv_fwd(proj, dn_conv_w, name="dn_conv")
    gates = _dn_gates_fwd(proj, a_cols, dt_cols, name="dn_gates")
    o_fwd, o_rev, st_fwd, st_rev = _dn_scan_fwd(qkv, gates, name="dn_scan")
    cat1 = _dn_out(o_fwd, o_rev, proj, out_gain, kv_mem[1], name="dn_outnorm")
    mo1 = _mm(cat1, dn_w_out, name="dn_out")
    x3, h3 = _post_pre(x2, mo1, gains["norm_mix_post"][1], gains["norm_ffn_pre"][1], name="post_mix1")
    w_gu1, w_d1 = p["ffn_weights"](1, h3)
    gu1, act1, f1 = _ffn_fwd(h3, w_gu1, w_d1, 1)

    dx3, df1, dg_ffn_post1, loss_cols = _final_loss_bwd(x3, f1, gains["norm_ffn_post"][1], target, name="loss_bwd")
    dh3, d_wgu1, d_wd1 = _ffn_bwd(h3, gu1, act1, w_gu1, w_d1, df1, 1)
    sent = p["grads_ready"]("ffn1", {("ffn_w_gate_up", 1): d_wgu1, ("ffn_w_down", 1): d_wd1})
    dx2, dmo1, dg_mix_post1, dg_ffn_pre1 = _post_pre_bwd(x2, mo1, gains["norm_mix_post"][1] + sent, gains["norm_ffn_pre"][1],
                                                         dx3, dh3, name="post_mix1_bwd")
    dcat1 = _mm(dmo1, dn_w_out, tb=True, name="dn_out_dx")
    g["dn_w_out"] = _mm(cat1, dmo1, ta=True, out_dtype=LINK_DTYPE, name="dn_out_dw")
    d_o, dz, dqm1, d_out_gain, dkv1 = _dn_out_bwd(o_fwd, o_rev, proj, out_gain, kv_mem[1], dcat1, name="dn_outnorm_bwd")
    (d_f, dg_f), (d_r, dg_r) = _dn_scan_bwd(qkv, gates, (st_fwd, st_rev), d_o, name="dn_scan_bwd")
    d_gate_cols, d_a_cols, d_dt_cols = _dn_gates_bwd(proj, a_cols, dt_cols, (dg_f, dg_r), name="dn_gates_bwd")
    d_pre, d_conv = zip(*[_dn_conv_bwd(proj, dn_conv_w, d_f[which], d_r[which], which, name=f"dn_conv_bwd{which}")
                          for which in range(3)])
    dproj = jnp.concatenate(list(d_pre) + [dz, dqm1, d_gate_cols], axis=1).astype(MXU_DTYPE)
    dh2 = _mm(dproj, dn_w_in, tb=True, name="dn_in_dx")
    g["dn_w_in"] = _mm(h2, dproj, ta=True, out_dtype=LINK_DTYPE, name="dn_in_dw")
    g["dn_conv"] = jnp.concatenate(d_conv, axis=1)
    g["dn_a_log"] = _gate_params_bwd(d_a_cols)
    g["dn_dt_bias"] = _gate_params_bwd(d_dt_cols)
    g["dn_out_norm"] = d_out_gain

    d_mem_kv1 = _mm(mem_n[1], dkv1, ta=True, out_dtype=LINK_DTYPE, name="mem_kv_dw1")
    sent = p["grads_ready"]("dn", {("dn_w_in", 0): g["dn_w_in"], ("dn_w_out", 0): g["dn_w_out"], ("mem_w_kv", 1): d_mem_kv1})
    dx1, df0, dg_ffn_post0, dg_mix_pre1 = _post_pre_bwd(x1, f0, gains["norm_ffn_post"][0] + sent, gains["norm_mix_pre"][1],
                                                        dx2, dh2, name="post_ffn0_bwd")
    dh1, d_wgu0, d_wd0 = _ffn_bwd(h1, gu0, act0, w_gu0, w_d0, df0, 0)
    sent = p["grads_ready"]("ffn0", {("ffn_w_gate_up", 0): d_wgu0, ("ffn_w_down", 0): d_wd0})
    dx0, dmo0, dg_mix_post0, dg_ffn_pre0 = _post_pre_bwd(x, mo0, gains["norm_mix_post"][0] + sent, gains["norm_ffn_pre"][0],
                                                         dx1, dh1, name="post_mix0_bwd")
    dcat0 = _mm(dmo0, att_w_out, tb=True, name="att_out_dx")
    g["att_w_out"] = _mm(cat0, dmo0, ta=True, out_dtype=LINK_DTYPE, name="att_out_dw")
    delta, dqm0, dkv0 = _att_bwd_prep(cat0, dcat0, qkvm, kv_mem[0], name="att_bwd_prep")
    d_mem_kv0 = _mm(mem_n[0], dkv0, ta=True, out_dtype=LINK_DTYPE, name="mem_kv_dw0")
    sent = p["grads_ready"]("att_out", {("att_w_out", 0): g["att_w_out"], ("mem_w_kv", 0): d_mem_kv0})
    att_b = [_att_bwd(qkvm, bias[gi] + sent, lse_tot, delta, dcat0, gi, name=f"att_bwd{gi}") for gi in range(3)]
    dqkvm = jnp.concatenate([a[w] for w in range(3) for a in att_b] + [dqm0], axis=1).astype(MXU_DTYPE)
    g["rel_bias"] = sum(_bias_tiles_bwd(p["rel_bias"], att_b[gi][3], gi) for gi in range(3))
    g["att_w_in"] = _mm(h0, dqkvm, ta=True, out_shards=_shards_of(p["att_w_in"]), out_dtype=LINK_DTYPE, name="att_in_dw")
    sent = p["grads_ready"]("att_in", {("att_w_in", 0): g["att_w_in"]})
    dh0 = _mm(dqkvm, p["att_w_in"], tb=True, name="att_in_dx")
    grad_x, dg_mix_pre0 = _pre_norm_bwd(x, gains["norm_mix_pre"][0] + sent, dh0, dx0, name="pre0_bwd")

    d_mem_norm = []
    for i, dkv in enumerate((dkv0, dkv1)):
        d_mem_n = _mm(dkv, mem_w_kv[i], tb=True, name=f"mem_kv_dx{i}")
        d_mem_norm.append(_gain_bwd(mem, gains["mem_norm"][i], d_mem_n, name=f"mem_norm_bwd{i}"))
    g["mem_w_kv"] = [d_mem_kv0, d_mem_kv1]
    g["mem_norm"] = jnp.concatenate(d_mem_norm, axis=0)
    g["norm_mix_pre"] = jnp.concatenate([dg_mix_pre0, dg_mix_pre1], axis=0)
    g["norm_mix_post"] = jnp.concatenate([dg_mix_post0, dg_mix_post1], axis=0)
    g["norm_ffn_pre"] = jnp.concatenate([dg_ffn_pre0, dg_ffn_pre1], axis=0)
    g["norm_ffn_post"] = jnp.concatenate([dg_ffn_post0, dg_ffn_post1], axis=0)
    g["ffn_w_gate_up"] = [d_wgu0, d_wgu1]
    g["ffn_w_down"] = [d_wd0, d_wd1]
    return loss_cols, grad_x, g


N_CHIPS = 4
N_DEV = 8
MESH = pl.DeviceIdType.MESH
BIG = (("att_w_in", (1, 1024, 640), 2), ("att_w_out", (1, 256, 1024), 1), ("dn_w_in", (1, 1024, 838), 2),
       ("dn_w_out", (1, 256, 1024), 1), ("mem_w_kv", (2, 256, 512), 1), ("ffn_w_gate_up", (2, 1024, 1408), 2),
       ("ffn_w_down", (2, 704, 1024), 1))


def _mesh_pos():
    return lax.axis_index("x"), lax.axis_index("y"), lax.axis_index("c")


def _other_chips(x, y):
    return [(1 - x, y), (x, 1 - y), (1 - x, 1 - y)]


ANY = pl.BlockSpec(memory_space=pl.ANY)


def _all_reduce_small(v, *, name):
    rows, cols = v.shape
    flips = [(dx, dy, dc) for dx in (0, 1) for dy in (0, 1) for dc in (0, 1)][1:]

    def body(v_ref, o_ref, buf, send_sems, recv_sems):
        x, y, c = _mesh_pos()

        def peer(f):
            return tuple(1 - p if fl else p for p, fl in zip((x, y, c), f))

        def index(p):
            return 4 * p[0] + 2 * p[1] + p[2]

        buf[index((x, y, c))] = v_ref[...]
        sends = []
        for k, f in enumerate(flips):
            cp = pltpu.make_async_remote_copy(src_ref=v_ref, dst_ref=buf.at[index((x, y, c))], send_sem=send_sems.at[k],
                                              recv_sem=recv_sems.at[k], device_id=peer(f), device_id_type=MESH)
            cp.start()
            sends.append(cp)
        for k, f in enumerate(flips):
            pltpu.make_async_remote_copy(src_ref=v_ref, dst_ref=buf.at[index(peer(f))], send_sem=send_sems.at[k],
                                         recv_sem=recv_sems.at[k], device_id=peer(f), device_id_type=MESH).wait_recv()
        for cp in sends:
            cp.wait_send()
        acc = buf[0]
        for d in range(1, N_DEV):
            acc = acc + buf[d]
        o_ref[...] = acc

    vmem = pl.BlockSpec(memory_space=pltpu.VMEM)
    return pl.pallas_call(
        body, name=name, in_specs=[vmem], out_specs=vmem, out_shape=jax.ShapeDtypeStruct((rows, cols), F32),
        scratch_shapes=[pltpu.VMEM((N_DEV, rows, cols), F32), pltpu.SemaphoreType.DMA((N_DEV - 1,)),
                        pltpu.SemaphoreType.DMA((N_DEV - 1,))],
    )(v)


def _adamw_update(w, g, m, v):
    m = ADAM_B1 * m + (1.0 - ADAM_B1) * g
    v = ADAM_B2 * v + (1.0 - ADAM_B2) * (g * g)
    m_hat = m / (1.0 - ADAM_B1 ** ADAM_STEP)
    v_hat = v / (1.0 - ADAM_B2 ** ADAM_STEP)
    return -ADAM_LR * (m_hat / (jnp.sqrt(v_hat) + ADAM_EPS) + ADAM_WD * w), m, v


def _adamw_many(ws, gs, ms, vs, *, name):
    n = len(ws)

    def kern(*refs):
        for i in range(n):
            outs = _adamw_update(*[refs[k * n + i][...] for k in range(4)])
            for k, val in enumerate(outs):
                refs[(4 + k) * n + i][...] = val

    vmem = pl.BlockSpec(memory_space=pltpu.VMEM)
    res = pl.pallas_call(
        kern, name=name, in_specs=[vmem] * (4 * n), out_specs=[vmem] * (3 * n),
        out_shape=[jax.ShapeDtypeStruct(a.shape, F32) for _ in range(3) for a in ws],
    )(*ws, *gs, *ms, *vs)
    return res[:n], res[n: 2 * n], res[2 * n:]


def _adamw(w, g, m, v, *, name):
    def body(w, g, m, v):
        return _adamw_update(w, g, m, v), ()
    rows, cols = w.shape
    if rows % SUBLANES == 0:
        return _rowwise(body, [w, g, m, v], [], [(cols, F32)] * 3, [], tm=_tile(rows, 256, SUBLANES), name=name)

    def kern(*refs):
        outs, _ = body(*[r[...] for r in refs[:4]])
        for r, val in zip(refs[4:], outs, strict=True):
            r[...] = val

    spec = pl.BlockSpec((rows, _tile(cols, 256, LANES)), lambda j: (0, j))
    return pl.pallas_call(
        kern, name=name, grid=(cols // spec.block_shape[1],), in_specs=[spec] * 4, out_specs=[spec] * 3,
        out_shape=[jax.ShapeDtypeStruct((rows, cols), F32)] * 3, compiler_params=_params("parallel"),
    )(w, g, m, v)


def _pack_small(arrs, rows):
    flat = jnp.concatenate([a.reshape(-1) for a in arrs])
    return jnp.pad(flat, (0, rows * LANES - flat.shape[0])).reshape(rows, LANES)


def _unpack_small(packed, shapes):
    flat = packed.reshape(-1)
    out, off = [], 0
    for s in shapes:
        size = math.prod(s)
        out.append(flat[off: off + size].reshape(s))
        off += size
    return out


def _small_rows(shapes):
    return -(-sum(math.prod(s) for s in shapes) // (SUBLANES * LANES)) * SUBLANES


def _sem_pairs(n):
    return [pltpu.SemaphoreType.DMA((n,)), pltpu.SemaphoreType.DMA((n,))]


def _gather_blocks(blocks, *, name):
    n = len(blocks)

    def body(*refs):
        x_refs, out_refs, (send_sems, recv_sems) = refs[:n], refs[n: 2 * n], refs[2 * n:]
        x, y, c = _mesh_pos()
        sibling = (x, y, 1 - c)
        chips = _other_chips(x, y)

        def copy(k, src, dst, to):
            return pltpu.make_async_remote_copy(src_ref=src, dst_ref=dst, send_sem=send_sems.at[k],
                                                recv_sem=recv_sems.at[k], device_id=to, device_id_type=MESH)

        def part(b, chip, h):
            half = blocks[b].shape[0] // 2
            return out_refs[b].at[2 * chip[0] + chip[1], pl.ds(h * half, half), :]

        def my_half(b):
            half = blocks[b].shape[0] // 2
            return x_refs[b].at[pl.ds(c * half, half), :]

        first = [copy(6 * b + j, my_half(b), part(b, (x, y), c), (*chip, c)) for b in range(n) for j, chip in enumerate(chips)]
        for cp in first:
            cp.start()
        passed = []
        for b in range(n):
            for j, chip in enumerate(chips):
                copy(6 * b + j, my_half(b), part(b, chip, c), (*chip, c)).wait_recv()
                cp = copy(6 * b + 3 + j, part(b, chip, c), part(b, chip, c), sibling)
                cp.start()
                passed.append(cp)
        for b in range(n):
            for j, chip in enumerate(chips):
                copy(6 * b + 3 + j, part(b, chip, 1 - c), part(b, chip, 1 - c), sibling).wait_recv()
        for cp in first + passed:
            cp.wait_send()

    return pl.pallas_call(
        body, name=name, in_specs=[ANY] * n, out_specs=[ANY] * n,
        out_shape=[jax.ShapeDtypeStruct((N_CHIPS, *a.shape), a.dtype) for a in blocks],
        scratch_shapes=_sem_pairs(6 * n),
    )(*blocks)


HBM = pl.BlockSpec(memory_space=pltpu.HBM)
SEM = pl.BlockSpec(memory_space=pltpu.SEMAPHORE)
DATAFLOW = pltpu.SideEffectType.DATAFLOW_SIDE_EFFECTING


def _gather_start(blocks, *, name):
    n = len(blocks)
    lands = [lax.empty((N_CHIPS, *a.shape), a.dtype) for a in blocks]

    def body(*refs):
        x_refs, land_refs, send_sems, recv_sems, token = refs[:n], refs[n: 2 * n], refs[2 * n], refs[2 * n + 1], refs[-1]
        x, y, c = _mesh_pos()
        for b in range(n):
            for j, chip in enumerate(_other_chips(x, y)):
                pltpu.make_async_remote_copy(src_ref=x_refs[b], dst_ref=land_refs[b].at[2 * x + y], send_sem=send_sems.at[3 * b + j],
                                             recv_sem=recv_sems.at[3 * b + j], device_id=(*chip, c), device_id_type=MESH).start()
        token[...] = jnp.zeros_like(token)

    operands = [pltpu.with_memory_space_constraint(a, pltpu.HBM) for a in blocks + lands]
    res = pl.pallas_call(
        body, name=name, in_specs=[HBM] * (2 * n),
        out_shape=(pltpu.SemaphoreType.DMA((3 * n,)), pltpu.SemaphoreType.DMA((3 * n,)),
                   *[pltpu.HBM(a.shape, a.dtype) for a in operands], jax.ShapeDtypeStruct((SUBLANES, LANES), F32)),
        out_specs=(SEM, SEM, *[HBM] * (2 * n), pl.BlockSpec(memory_space=pltpu.VMEM)),
        input_output_aliases={i: 2 + i for i in range(2 * n)},
        compiler_params=pltpu.CompilerParams(has_side_effects=DATAFLOW),
    )(*operands)
    return res[0], res[1], list(res[2: 2 + n]), list(res[2 + n: 2 + 2 * n]), res[-1]


def _gather_wait(started, after, *, name):
    send_sems, recv_sems, blocks, lands, _ = started
    n = len(blocks)

    def body(*refs):
        x_refs, land_refs, send_sems, recv_sems = refs[:n], refs[n: 2 * n], refs[2 * n], refs[2 * n + 1]
        x, y, c = _mesh_pos()
        for b in range(n):
            for j, chip in enumerate(_other_chips(x, y)):
                cp = pltpu.make_async_remote_copy(src_ref=x_refs[b], dst_ref=land_refs[b].at[2 * chip[0] + chip[1]],
                                                  send_sem=send_sems.at[3 * b + j], recv_sem=recv_sems.at[3 * b + j],
                                                  device_id=(*chip, c), device_id_type=MESH)
                cp.wait_send()
                cp.wait_recv()

    res = pl.pallas_call(
        body, name=name, in_specs=(*[HBM] * (2 * n), SEM, SEM, ANY),
        out_shape=tuple(pltpu.HBM(a.shape, a.dtype) for a in blocks + lands), out_specs=tuple([HBM] * (2 * n)),
        input_output_aliases={i: i for i in range(2 * n)},
        compiler_params=pltpu.CompilerParams(has_side_effects=DATAFLOW),
    )(*blocks, *lands, send_sems, recv_sems, after)
    return list(res[n:])


def _swap_with_sibling(blocks, *, name):
    n = len(blocks)

    def body(*refs):
        in_refs, out_refs, (send_sems, recv_sems) = refs[:n], refs[n: 2 * n], refs[2 * n:]
        x, y, c = _mesh_pos()
        copies = [pltpu.make_async_remote_copy(src_ref=in_refs[b], dst_ref=out_refs[b], send_sem=send_sems.at[b],
                                               recv_sem=recv_sems.at[b], device_id=(x, y, 1 - c), device_id_type=MESH)
                  for b in range(n)]
        for cp in copies:
            cp.start()
        for cp in copies:
            cp.wait()

    return pl.pallas_call(
        body, name=name, in_specs=[ANY] * n, out_specs=[ANY] * n,
        out_shape=[jax.ShapeDtypeStruct(a.shape, a.dtype) for a in blocks], scratch_shapes=_sem_pairs(n),
    )(*blocks)


def _sum_chips_block(parts, *, name):
    n, half, cols = parts.shape
    tm = _tile(half, 512, 2 * SUBLANES)

    def kern(p_ref, o_ref):
        acc = p_ref[0].astype(F32)
        for s in range(1, n):
            acc = acc + p_ref[s].astype(F32)
        o_ref[...] = acc

    return pl.pallas_call(
        kern, name=name, grid=(half // tm,),
        in_specs=[pl.BlockSpec((n, tm, cols), lambda i: (0, i, 0))],
        out_specs=pl.BlockSpec((tm, cols), lambda i: (i, 0)),
        out_shape=jax.ShapeDtypeStruct((half, cols), F32),
        compiler_params=_params("parallel"),
    )(parts)


PEER_FLIPS = [(dx, dy, dc) for dx in (0, 1) for dy in (0, 1) for dc in (0, 1)][1:]


def _flipped(pos, flip):
    return tuple(1 - p if f else p for p, f in zip(pos, flip))


def _device_index(pos):
    return 4 * pos[0] + 2 * pos[1] + pos[2]


def _scatter_start(blocks, *, name):
    n = len(blocks)
    lands = [lax.empty((N_DEV, a.shape[1] // 2, a.shape[2]), a.dtype) for a in blocks]

    def body(*refs):
        g_refs, land_refs, send_sems, recv_sems, token = refs[:n], refs[n: 2 * n], refs[2 * n], refs[2 * n + 1], refs[-1]
        pos = _mesh_pos()
        for b in range(n):
            half = blocks[b].shape[1] // 2
            for k, flip in enumerate(PEER_FLIPS):
                peer = _flipped(pos, flip)
                pltpu.make_async_remote_copy(src_ref=g_refs[b].at[2 * peer[0] + peer[1], pl.ds(peer[2] * half, half), :],
                                             dst_ref=land_refs[b].at[_device_index(pos)],
                                             send_sem=send_sems.at[7 * b + k], recv_sem=recv_sems.at[7 * b + k],
                                             device_id=peer, device_id_type=MESH).start()
        token[...] = jnp.zeros_like(token)

    operands = [pltpu.with_memory_space_constraint(a, pltpu.HBM) for a in blocks + lands]
    res = pl.pallas_call(
        body, name=name, in_specs=[HBM] * (2 * n),
        out_shape=(pltpu.SemaphoreType.DMA((7 * n,)), pltpu.SemaphoreType.DMA((7 * n,)),
                   *[pltpu.HBM(a.shape, a.dtype) for a in operands], jax.ShapeDtypeStruct((SUBLANES, LANES), F32)),
        out_specs=(SEM, SEM, *[HBM] * (2 * n), pl.BlockSpec(memory_space=pltpu.VMEM)),
        input_output_aliases={i: 2 + i for i in range(2 * n)},
        compiler_params=pltpu.CompilerParams(has_side_effects=DATAFLOW),
    )(*operands)
    return res[0], res[1], list(res[2: 2 + n]), list(res[2 + n: 2 + 2 * n]), res[-1]


def _scatter_wait(started, after, *, name):
    send_sems, recv_sems, blocks, lands, _ = started
    n = len(blocks)

    def body(*refs):
        g_refs, land_refs, send_sems, recv_sems = refs[:n], refs[n: 2 * n], refs[2 * n], refs[2 * n + 1]
        pos = _mesh_pos()
        for b in range(n):
            half = blocks[b].shape[1] // 2
            for k, flip in enumerate(PEER_FLIPS):
                peer = _flipped(pos, flip)
                cp = pltpu.make_async_remote_copy(src_ref=g_refs[b].at[0, pl.ds(0, half), :],
                                                  dst_ref=land_refs[b].at[_device_index(peer)],
                                                  send_sem=send_sems.at[7 * b + k], recv_sem=recv_sems.at[7 * b + k],
                                                  device_id=peer, device_id_type=MESH)
                cp.wait_send()
                cp.wait_recv()

    res = pl.pallas_call(
        body, name=name, in_specs=(*[HBM] * (2 * n), SEM, SEM, ANY),
        out_shape=tuple(pltpu.HBM(a.shape, a.dtype) for a in blocks + lands), out_specs=tuple([HBM] * (2 * n)),
        input_output_aliases={i: i for i in range(2 * n)},
        compiler_params=pltpu.CompilerParams(has_side_effects=DATAFLOW),
    )(*blocks, *lands, send_sems, recv_sems, after)
    return list(res[:n]), list(res[n:])


def _reduce_finish(begun, names, after):
    x, y, c = _mesh_pos()
    mine = {}
    for key, started in begun.items():
        blocks, lands = _scatter_wait(started, after, name=f"rs_scatter_wait_{key}")
        parts = []
        for blk, land in zip(blocks, lands, strict=True):
            half = blk.shape[1] // 2
            own = lax.dynamic_slice(blk, (2 * x + y, c * half, 0), (1, half, blk.shape[2]))
            parts.append(lax.dynamic_update_slice(land, own, (_device_index((x, y, c)), 0, 0)))
        mine[key] = [_sum_chips_block(p, name=f"rs_sum_{nm}") for p, nm in zip(parts, names[key], strict=True)]
    flat = [a for key in begun for a in mine[key]]
    other = iter(_swap_with_sibling(flat, name="rs_join"))
    return {key: [jnp.concatenate([jnp.where(c == 0, a, b), jnp.where(c == 0, b, a)], axis=0)
                  for a, b in ((a, next(other)) for a in mine[key])] for key in begun}


WEIGHTS = ("rel_bias", "att_w_in", "att_w_out", "dn_w_in", "dn_conv", "dn_a_log", "dn_dt_bias", "dn_out_norm", "dn_w_out",
           "mem_norm", "mem_w_kv", "norm_mix_pre", "norm_mix_post", "norm_ffn_pre", "norm_ffn_post", "ffn_w_gate_up",
           "ffn_w_down")
BIG_NAMES = tuple(n for n, _, _ in BIG)
SMALL_NAMES = tuple(n for n in WEIGHTS if n not in BIG_NAMES)
CONV_COLS = 3 * TOK_WIDTH
CONV_SHARD = CONV_COLS // N_CHIPS
BLOCKS = tuple((n, layer) for n, shape, _ in BIG for layer in range(shape[0]))
COLUMN_SHARDED = {n: axis == 2 for n, _, axis in BIG}


def kernel(x, mem, rel_bias, att_w_in, att_w_out, dn_w_in, dn_conv, dn_a_log, dn_dt_bias, dn_out_norm, dn_w_out, mem_norm, mem_w_kv, norm_mix_pre, norm_mix_post, norm_ffn_pre, norm_ffn_post, ffn_w_gate_up, ffn_w_down, loss_target, m_rel_bias, m_att_w_in, m_att_w_out, m_dn_w_in, m_dn_conv, m_dn_a_log, m_dn_dt_bias, m_dn_out_norm, m_dn_w_out, m_mem_norm, m_mem_w_kv, m_norm_mix_pre, m_norm_mix_post, m_norm_ffn_pre, m_norm_ffn_post, m_ffn_w_gate_up, m_ffn_w_down, v_rel_bias, v_att_w_in, v_att_w_out, v_dn_w_in, v_dn_conv, v_dn_a_log, v_dn_dt_bias, v_dn_out_norm, v_dn_w_out, v_mem_norm, v_mem_w_kv, v_norm_mix_pre, v_norm_mix_post, v_norm_ffn_pre, v_norm_ffn_post, v_ffn_w_gate_up, v_ffn_w_down):
    w = dict(zip(WEIGHTS, (rel_bias, att_w_in, att_w_out, dn_w_in, dn_conv, dn_a_log, dn_dt_bias, dn_out_norm, dn_w_out,
                           mem_norm, mem_w_kv, norm_mix_pre, norm_mix_post, norm_ffn_pre, norm_ffn_post, ffn_w_gate_up,
                           ffn_w_down)))
    m = dict(zip(WEIGHTS, (m_rel_bias, m_att_w_in, m_att_w_out, m_dn_w_in, m_dn_conv, m_dn_a_log, m_dn_dt_bias,
                           m_dn_out_norm, m_dn_w_out, m_mem_norm, m_mem_w_kv, m_norm_mix_pre, m_norm_mix_post,
                           m_norm_ffn_pre, m_norm_ffn_post, m_ffn_w_gate_up, m_ffn_w_down)))
    v = dict(zip(WEIGHTS, (v_rel_bias, v_att_w_in, v_att_w_out, v_dn_w_in, v_dn_conv, v_dn_a_log, v_dn_dt_bias,
                           v_dn_out_norm, v_dn_w_out, v_mem_norm, v_mem_w_kv, v_norm_mix_pre, v_norm_mix_post,
                           v_norm_ffn_pre, v_norm_ffn_post, v_ffn_w_gate_up, v_ffn_w_down)))
    cx, cy, cc = _mesh_pos()
    chip = 2 * cx + cy

    local = dict(zip(BLOCKS, lax.optimization_barrier(
        [(w[n][layer].T if n == "dn_w_in" else w[n][layer]).astype(MXU_DTYPE) for n, layer in BLOCKS]), strict=True))

    def usable(block, got):
        got = lax.dynamic_update_slice(got, local[block][None], (chip, 0, 0))
        if block[0] == "dn_conv":
            return jnp.concatenate([got[s] for s in range(N_CHIPS)], axis=1)
        return got if COLUMN_SHARDED[block[0]] else got.reshape(-1, got.shape[-1])

    late = {"att_rest": [("att_w_out", 0), ("mem_w_kv", 0), ("mem_w_kv", 1)],
            "ffn0": [("ffn_w_gate_up", 0), ("ffn_w_down", 0)], "dn": [("dn_w_in", 0), ("dn_w_out", 0), ("dn_conv", 0)],
            "ffn1": [("ffn_w_gate_up", 1), ("ffn_w_down", 1)]}
    local[("dn_conv", 0)] = w["dn_conv"][0]
    first = [b for b in BLOCKS if all(b not in blks for blks in late.values())]
    first_got = _gather_blocks([local[b] for b in first], name="gather_weights")
    late_local, _ = lax.optimization_barrier(({k: [local[b] for b in blks] for k, blks in late.items()}, first_got[0]))
    started = {k: _gather_start(late_local[k], name=f"gather_start_{k}") for k in late}
    started_token = sum(s[4][0, 0] for s in started.values())

    def late_weights(key, after):
        lands = _gather_wait(started[key], after, name=f"gather_wait_{key}")
        return [usable(b, got) for b, got in zip(late[key], lands, strict=True)]

    def dn_weights(after):
        w_in, w_out, conv = late_weights("dn", after)
        return _pad_dn_w_in(jnp.concatenate([w_in[s].T for s in range(N_CHIPS)], axis=1)), w_out, conv

    full = {}
    for b, got in zip(first, first_got, strict=True):
        full.setdefault(b[0], []).append(usable(b, got))
    p = {
        "rel_bias": w["rel_bias"], "att_w_in": full["att_w_in"][0], "att_rest_weights": lambda after: late_weights("att_rest", after),
        "dn_a_log": w["dn_a_log"][0], "dn_dt_bias": w["dn_dt_bias"][0],
        "dn_out_norm": w["dn_out_norm"][0], "mem_norm": w["mem_norm"],
        "norm_mix_pre": w["norm_mix_pre"] + started_token,
        "norm_mix_post": w["norm_mix_post"], "norm_ffn_pre": w["norm_ffn_pre"], "norm_ffn_post": w["norm_ffn_post"],
        "ffn_weights": lambda layer, after: late_weights(f"ffn{layer}", after), "dn_weights": dn_weights,
    }

    def chip_blocks(n, a):
        if n == "dn_w_in":
            a = _unpad_dn_w_in(a)
        if a.ndim == 3:
            return a
        if COLUMN_SHARDED[n]:
            return a.reshape(a.shape[0], N_CHIPS, -1).transpose(1, 0, 2)
        return a.reshape(N_CHIPS, -1, a.shape[-1])

    begun, begun_blocks = {}, {}

    def grads_ready(key, layer_grads):
        begun_blocks[key] = list(layer_grads)
        begun[key] = _scatter_start([chip_blocks(n, a) for (n, _), a in layer_grads.items()], name=f"rs_scatter_start_{key}")
        return begun[key][4][0, 0]

    p["grads_ready"] = grads_ready
    loss_cols, grad_x, g = _local_step(x[0], mem[0], loss_target[0], p)
    finished = _reduce_finish(begun, {k: [f"{n}{layer}" for n, layer in blks] for k, blks in begun_blocks.items()}, grad_x)
    reduced = {b: r for k in begun for b, r in zip(begun_blocks[k], finished[k], strict=True)}
    grads = {n: jnp.concatenate([reduced[b] for b in BLOCKS if b[0] == n], axis=0).reshape(shape) for n, shape, _ in BIG}
    small_full_shapes = [(DN_CONV, CONV_COLS) if n == "dn_conv" else w[n].shape for n in SMALL_NAMES] + [(1,)]
    small_sum = _all_reduce_small(_pack_small([g[n] for n in SMALL_NAMES] + [jnp.sum(loss_cols).reshape(1)],
                                              _small_rows(small_full_shapes)), name="reduce_small")
    *small_grads, loss = _unpack_small(small_sum, small_full_shapes)
    loss = loss[0]
    for n, s in zip(SMALL_NAMES, small_grads, strict=True):
        grads[n] = lax.dynamic_slice(s, (0, chip * CONV_SHARD), (DN_CONV, CONV_SHARD))[None] if n == "dn_conv" else s

    delta, new_m, new_v = {}, {}, {}
    for n in BIG_NAMES:
        shape = w[n].shape
        two_d = (lambda a: a[0].T) if n == "dn_w_in" else (lambda a: a.reshape(-1, shape[-1]))
        back = (lambda a: a.T[None]) if n == "dn_w_in" else (lambda a: a.reshape(shape))
        g_2d = two_d(grads[n])
        res = _adamw(two_d(w[n]), g_2d, two_d(m[n]), two_d(v[n]), name=f"adamw_{n}")
        grads[n], delta[n], new_m[n], new_v[n] = (back(r) for r in (g_2d, *res))
    res = _adamw_many(*[[d[n].reshape(-1, d[n].shape[-1]) for n in SMALL_NAMES] for d in (w, grads, m, v)], name="adamw_small")
    for d, outs in zip((delta, new_m, new_v), res):
        for n, a in zip(SMALL_NAMES, outs):
            d[n] = a.reshape(w[n].shape)
    return (loss, grad_x[None], *[grads[n] for n in WEIGHTS], *[delta[n] for n in WEIGHTS],
            *[new_m[n] for n in WEIGHTS], *[new_v[n] for n in WEIGHTS])
```

```python
import functools
import math

import numpy as np
import jax
import jax.numpy as jnp
from jax import lax
from jax.experimental import pallas as pl
from jax.experimental.pallas import tpu as pltpu

F32 = jnp.float32
MXU_DTYPE = jnp.bfloat16
LINK_DTYPE = jnp.bfloat16
HI = lax.Precision.HIGHEST

EPS = 1e-6
NEG_INF = -1e30
LANES = 128
SUBLANES = 8
VMEM_LIMIT = 56 * 1024 * 1024
MM_WHOLE_K_BUDGET = 44 * 1024 * 1024

D_MODEL = 1024
TOK_WIDTH = 768
MEM_WIDTH = 256
MEM_LEN = 256
ATT_HEAD_DIM = 64
DILATIONS = (1, 4, 16)
HALF = 64
ATT_BQ = 128
ATT_W = ATT_BQ + 2 * HALF
REL_BUCKETS = 32
REL_MAX_DIST = 1024
DN_HEADS = 6
DN_HEAD_DIM = 128
DN_CONV = 5
DN_CHUNK = 128
D_FF = 2816
DN_IN = 3352
DN_IN_PAD = 3456
N_GATES = 4 * DN_HEADS

ADAM_LR = 0.001
ADAM_B1 = 0.9
ADAM_B2 = 0.999
ADAM_EPS = 1e-08
ADAM_WD = 0.01
ADAM_STEP = 10


def _tile(n, target, align):
    if n <= target:
        return n
    t = (target // align) * align
    while t >= align:
        if n % t == 0:
            return t
        t -= align
    raise ValueError(f"no tile for {n} (target {target}, align {align})")


def _params(*sem):
    return pltpu.CompilerParams(dimension_semantics=sem, vmem_limit_bytes=VMEM_LIMIT)


def _mm(a, b, *, name, ta=False, tb=False, out_shards=None, tm=1408, tn=1408, tk=1408, out_dtype=F32):
    if ta:
        K, M = a.shape
    else:
        M, K = a.shape
    sharded_b = b.ndim == 3
    if sharded_b:
        n_sh, b_rows, b_cols = b.shape
        N, K2 = (b_rows, n_sh * b_cols) if tb else (n_sh * b_cols, b_rows)
    else:
        N, K2 = b.shape if tb else b.shape[::-1]
    assert K == K2, (a.shape, b.shape, ta, tb)
    tm = _tile(M, tm, LANES if ta else SUBLANES)
    tn = N // out_shards if out_shards else (b_cols if sharded_b and not tb else _tile(N, tn, LANES))
    tk = b_cols if sharded_b and tb else _tile(K, tk, LANES)

    def vmem_bytes(tm_, tk_):
        return 2 * (tm_ * tk_ * a.dtype.itemsize + tk_ * tn * b.dtype.itemsize + tm_ * tn * jnp.dtype(out_dtype).itemsize)

    for rows in (tm, tm // 2):
        if not (sharded_b and tb) and M % rows == 0 and rows % LANES == 0 and vmem_bytes(rows, K) <= MM_WHOLE_K_BUDGET:
            tm, tk = rows, K
            break
    nk = K // tk
    a_spec = pl.BlockSpec((tk, tm), lambda i, j, k: (k, i)) if ta else pl.BlockSpec((tm, tk), lambda i, j, k: (i, k))
    if sharded_b:
        b_spec = (pl.BlockSpec((None, tn, tk), lambda i, j, k: (k, j, 0)) if tb
                  else pl.BlockSpec((None, tk, tn), lambda i, j, k: (j, k, 0)))
    else:
        b_spec = pl.BlockSpec((tn, tk), lambda i, j, k: (j, k)) if tb else pl.BlockSpec((tk, tn), lambda i, j, k: (k, j))
    if out_shards:
        out_spec = pl.BlockSpec((None, tm, tn), lambda i, j, k: (j, i, 0))
        out_shape = jax.ShapeDtypeStruct((out_shards, M, tn), out_dtype)
    else:
        out_spec = pl.BlockSpec((tm, tn), lambda i, j, k: (i, j))
        out_shape = jax.ShapeDtypeStruct((M, N), out_dtype)
    dims = (((0 if ta else 1,), (1 if tb else 0,)), ((), ()))

    def product(a_ref, b_ref):
        return lax.dot_general(a_ref[...].astype(MXU_DTYPE), b_ref[...].astype(MXU_DTYPE), dims, preferred_element_type=F32)

    def kern_whole(a_ref, b_ref, o_ref):
        o_ref[...] = product(a_ref, b_ref).astype(o_ref.dtype)

    def kern_steps(a_ref, b_ref, o_ref, acc_ref):
        k = pl.program_id(2)

        @pl.when(k == 0)
        def _():
            acc_ref[...] = jnp.zeros_like(acc_ref)

        acc_ref[...] += product(a_ref, b_ref)

        @pl.when(k == nk - 1)
        def _():
            o_ref[...] = acc_ref[...].astype(o_ref.dtype)

    return pl.pallas_call(
        kern_whole if nk == 1 else kern_steps, name=name, grid=(M // tm, N // tn, nk), in_specs=[a_spec, b_spec],
        out_specs=out_spec, out_shape=out_shape,
        scratch_shapes=[] if nk == 1 else [pltpu.VMEM((tm, tn), F32)],
        compiler_params=_params("parallel", "parallel", "arbitrary"),
    )(a, b)


def _shards_of(w):
    return w.shape[0] if w.ndim == 3 else None


def _col(arr, width, blk):
    return (arr, width, blk)


def _rowwise(body, rows, consts, out_rows, out_acc, *, tm, name):
    n_rows = (rows[0][0] if isinstance(rows[0], tuple) else rows[0]).shape[0]
    assert n_rows % tm == 0, (n_rows, tm)
    arrs, in_specs = [], []
    for r in rows:
        arr, width, blk = r if isinstance(r, tuple) else (r, r.shape[1], 0)
        assert arr.shape[0] == n_rows
        arrs.append(arr)
        in_specs.append(pl.BlockSpec((tm, width), functools.partial(lambda i, b: (i, b), b=blk)))
    for c in consts:
        arrs.append(c)
        in_specs.append(pl.BlockSpec(c.shape, functools.partial(lambda i, n: (0,) * n, n=c.ndim)))
    n_in, n_ro = len(arrs), len(out_rows)
    out_shape = [jax.ShapeDtypeStruct((n_rows, w), dt) for w, dt in out_rows]
    out_specs = [pl.BlockSpec((tm, w), lambda i: (i, 0)) for w, _ in out_rows]
    out_shape += [jax.ShapeDtypeStruct(s, F32) for s in out_acc]
    out_specs += [pl.BlockSpec(s, lambda i: (0, 0)) for s in out_acc]

    def kern(*refs):
        ro, ao = body(*[r[...] for r in refs[:n_in]])
        outs = refs[n_in:]
        for r, v in zip(outs[:n_ro], ro, strict=True):
            r[...] = v.astype(r.dtype)
        if out_acc:
            @pl.when(pl.program_id(0) == 0)
            def _():
                for r in outs[n_ro:]:
                    r[...] = jnp.zeros_like(r)

            for r, v in zip(outs[n_ro:], ao, strict=True):
                r[...] += v

    res = pl.pallas_call(
        kern, name=name, grid=(n_rows // tm,), in_specs=in_specs, out_specs=out_specs, out_shape=out_shape,
        compiler_params=_params("arbitrary" if out_acc else "parallel"),
    )(*arrs)
    return res


def _rms(x, gain):
    return x * lax.rsqrt(jnp.mean(x * x, axis=-1, keepdims=True) + EPS) * gain


def _silu(x):
    return x * jax.nn.sigmoid(x)


def _softplus(x):
    return jnp.maximum(x, 0.0) + jnp.log(1.0 + jnp.exp(-jnp.abs(x)))


def _dot_nt(a, b, precision=None):
    return lax.dot_general(a, b, (((1,), (1,)), ((), ())), preferred_element_type=F32, precision=precision)


def _dot_tn(a, b, precision=None):
    return lax.dot_general(a, b, (((0,), (0,)), ((), ())), preferred_element_type=F32, precision=precision)


def _dot(a, b, precision=None):
    return jnp.dot(a, b, preferred_element_type=F32, precision=precision)


def _pre_norm(x, gain, *, name):
    def body(x, g):
        return (_rms(x, g),), ()
    return _rowwise(body, [x], [gain], [(x.shape[1], MXU_DTYPE)], [], tm=_tile(x.shape[0], 512, 2 * SUBLANES), name=name)[0]


def _pre_norm_bwd(x, gain, dh, dx_other, *, name):
    def body(x, dh, dxo, g):
        _, vjp = jax.vjp(_rms, x, g)
        dx, dg = vjp(dh)
        return (dx + dxo,), (dg,)
    return _rowwise(body, [x, dh, dx_other], [gain], [(x.shape[1], F32)], [gain.shape], tm=512, name=name)


def _gain_bwd(x, gain, dh, *, name):
    def body(x, dh, g):
        _, vjp = jax.vjp(lambda g_: _rms(x, g_), g)
        return (), (vjp(dh)[0],)
    return _rowwise(body, [x, dh], [gain], [], [gain.shape], tm=_tile(x.shape[0], 512, SUBLANES), name=name)[0]


def _res_block(x_res, m, g_post, g_pre):
    x_new = x_res + _rms(m, g_post)
    return x_new, _rms(x_new, g_pre)


def _post_pre(x_res, m, g_post, g_pre, *, name):
    def body(x, m, gp, gq):
        return _res_block(x, m, gp, gq), ()
    d = x_res.shape[1]
    return _rowwise(body, [x_res, m], [g_post, g_pre], [(d, F32), (d, MXU_DTYPE)], [], tm=512, name=name)


def _post_pre_bwd(x_res, m, g_post, g_pre, dx_new, dh, *, name):
    def body(x, m, dxn, dh, gp, gq):
        _, vjp = jax.vjp(_res_block, x, m, gp, gq)
        dx, dm, dgp, dgq = vjp((dxn, dh))
        return (dx, dm), (dgp, dgq)
    d = x_res.shape[1]
    return _rowwise(body, [x_res, m, dx_new, dh], [g_post, g_pre], [(d, F32), (d, MXU_DTYPE)],
                    [g_post.shape, g_pre.shape], tm=256, name=name)


def _final_loss_bwd(x_res, m, g_post, target, *, name):
    d = x_res.shape[1]

    def loss_cols(x, m, g, t):
        err = x + _rms(m, g) - t
        return jnp.sum(err * err, axis=0, keepdims=True) * (0.5 / d)

    def body(x, m, t, g):
        cols, vjp = jax.vjp(lambda x_, m_, g_: loss_cols(x_, m_, g_, t), x, m, g)
        dx, dm, dg = vjp(jnp.ones_like(cols))
        return (dx, dm), (dg, cols)
    return _rowwise(body, [x_res, m, target], [g_post], [(d, F32), (d, MXU_DTYPE)], [g_post.shape, (1, d)], tm=256, name=name)


def _swiglu_act(gu, *, name):
    def body(gate, up):
        return (_silu(gate.astype(F32)) * up.astype(F32),), ()
    return _rowwise(body, [_col(gu, D_FF, 0), _col(gu, D_FF, 1)], [], [(D_FF, MXU_DTYPE)], [], tm=256, name=name)[0]


def _swiglu_act_bwd(gu, da, *, name):
    def body(gate, up, da):
        _, vjp = jax.vjp(lambda g, u: _silu(g) * u, gate.astype(F32), up.astype(F32))
        dg, du = vjp(da.astype(F32))
        return (jnp.concatenate([dg, du], axis=1),), ()
    return _rowwise(body, [_col(gu, D_FF, 0), _col(gu, D_FF, 1), da], [], [(2 * D_FF, MXU_DTYPE)], [], tm=256, name=name)[0]


def _lane_head_mask(width, head_dim, head):
    lane = lax.broadcasted_iota(jnp.int32, (1, width), 1)
    return (lane // head_dim) == head


def _mem_attn_heads(q4, k4, v4):
    logits = _bdot_nt(q4, k4)
    p = jnp.exp(logits - jnp.max(logits, axis=-1, keepdims=True))
    return _bdot(p / jnp.sum(p, axis=-1, keepdims=True), v4)


def _mem_heads(q_mem, kv):
    return _heads(q_mem * (ATT_HEAD_DIM ** -0.5), mask=True), _heads(kv[:, :MEM_WIDTH]), _heads(kv[:, MEM_WIDTH:])


def _mem_attn(q_mem, kv):
    return _join_heads(_mem_attn_heads(*_mem_heads(q_mem, kv)))


def _mem_attn_bwd(q_mem, kv, do):
    _, vjp = jax.vjp(_mem_attn_heads, *_mem_heads(q_mem, kv))
    dq4, dk4, dv4 = vjp(_heads(do, mask=True))
    return (_join_heads(dq4) * (ATT_HEAD_DIM ** -0.5),
            jnp.concatenate([dk4[0] + dk4[1], dk4[2] + dk4[3], dv4[0] + dv4[1], dv4[2] + dv4[3]], axis=1))


def _t5_bucket(rel):
    half = REL_BUCKETS // 2
    max_exact = half // 2
    n = np.abs(rel)
    large = max_exact + (np.log(np.maximum(n, 1) / max_exact) / math.log(REL_MAX_DIST / max_exact)
                         * (half - max_exact)).astype(np.int64)
    large = np.minimum(large, half - 1)
    return ((rel > 0) * half + np.where(n < max_exact, n, large)).astype(np.int32)


ATT_DIAGS = ATT_BQ + ATT_W - 1


def _bias_diag_onehot(dil):
    j = np.arange(ATT_DIAGS)
    tiles = []
    for off in (-HALF, 0, HALF):
        rel = j - (ATT_BQ - 1) - HALF - off
        hot = _t5_bucket(rel * dil)[:, None] == np.arange(REL_BUCKETS)[None, :]
        tiles.append(hot & (np.abs(rel) <= HALF)[:, None])
    return np.stack(tiles).astype(np.float32)


def _toeplitz(r):
    lead = r.shape[:-1]
    a = jnp.broadcast_to(r[..., None, :], lead + (ATT_BQ, ATT_DIAGS))
    a = jnp.pad(a, [(0, 0)] * len(lead) + [(0, 0), (0, 1)])
    a = a.reshape(lead + (ATT_BQ * (ATT_DIAGS + 1),))[..., : ATT_BQ * ATT_DIAGS].reshape(lead + (ATT_BQ, ATT_DIAGS))
    return a[..., ATT_BQ - 1: ATT_BQ - 1 + ATT_W]


def _bias_tiles(rel_bias, gi):
    heads = rel_bias[:, 4 * gi: 4 * gi + 4]
    diag = jnp.einsum('tnb,bh->thn', jnp.asarray(_bias_diag_onehot(DILATIONS[gi])), heads, precision=HI)
    return _toeplitz(diag)


def _bias_tiles_bwd(rel_bias, dtiles, gi):
    return jax.vjp(lambda rb: _bias_tiles(rb, gi), rel_bias)[1](dtiles)[0]


def _att_window(i, n_sub):
    start = jnp.clip(i * ATT_BQ - HALF, 0, n_sub - ATT_W)
    off = i * ATT_BQ - HALF - start
    return pl.multiple_of(start, HALF), off


def _att_valid(off):
    q = lax.broadcasted_iota(jnp.int32, (ATT_BQ, ATT_W), 0)
    kk = lax.broadcasted_iota(jnp.int32, (ATT_BQ, ATT_W), 1)
    return jnp.abs(kk - q - HALF - off) <= HALF


def _att_tile_id(i, nq):
    return jnp.where(i == 0, 0, jnp.where(i == nq - 1, 2, 1))


ATT_GROUP_HEADS = 4


def _heads(x, mask=False):
    out = []
    for p in range(2):
        pair = x[:, p * LANES: (p + 1) * LANES]
        for h in range(2):
            out.append(jnp.where(_lane_head_mask(LANES, ATT_HEAD_DIM, h), pair, 0.0) if mask else pair)
    return jnp.stack(out)


def _join_heads(x):
    first = _lane_head_mask(LANES, ATT_HEAD_DIM, 0)
    return jnp.concatenate([jnp.where(first, x[2 * p], x[2 * p + 1]) for p in range(2)], axis=1)


def _head_scalar(x):
    out = []
    for p in range(2):
        pair = x[:, p * LANES: (p + 1) * LANES]
        for h in range(2):
            out.append(jnp.max(jnp.where(_lane_head_mask(LANES, ATT_HEAD_DIM, h), pair, NEG_INF), axis=-1, keepdims=True))
    return jnp.stack(out)


def _bdot(a, b):
    return jnp.einsum('hqk,hkd->hqd', a, b, preferred_element_type=F32)


def _bdot_nt(a, b):
    return jnp.einsum('hqd,hkd->hqk', a, b, preferred_element_type=F32)


def _bdot_tn(a, b):
    return jnp.einsum('hqk,hqd->hkd', a, b, preferred_element_type=F32)


def _residue_view(arr, col_blocks, dil):
    if dil == 1:
        return arr, lambda r, j: col_blocks[j]
    width = 2 * LANES
    picked = jnp.concatenate([arr[:, b * width: (b + 1) * width] for b in col_blocks], axis=1)
    n = len(col_blocks)
    return picked.reshape(arr.shape[0] // dil, dil * n * width), lambda r, j: r * n + j


def _att_fwd(qkvm, bias, gi, *, name):
    dil = DILATIONS[gi]
    s_len = qkvm.shape[0]
    n_sub = s_len // dil
    nq = n_sub // ATT_BQ
    assert n_sub % ATT_BQ == 0 and n_sub >= ATT_W
    view, at = _residue_view(qkvm, [gi, 3 + gi, 6 + gi], dil)

    def kern(q_ref, k_ref, v_ref, b_ref, o_ref, lse_ref):
        start, off = _att_window(pl.program_id(1), n_sub)
        valid = _att_valid(off)
        q4 = _heads(q_ref[...] * (ATT_HEAD_DIM ** -0.5), mask=True)
        k4 = _heads(k_ref[pl.ds(start, ATT_W), :])
        v4 = _heads(v_ref[pl.ds(start, ATT_W), :])
        s = jnp.where(valid, _bdot_nt(q4, k4) + b_ref[...], NEG_INF)
        mx = jnp.max(s, axis=-1, keepdims=True)
        p = jnp.exp(s - mx)
        den = jnp.sum(p, axis=-1, keepdims=True)
        o_ref[...] = _join_heads(_bdot(p.astype(MXU_DTYPE), v4) / den)
        lse_ref[...] = _join_heads(jnp.broadcast_to(mx + jnp.log(den), (ATT_GROUP_HEADS, ATT_BQ, LANES)))

    def qkv_spec(which, full):
        shape = (n_sub, 2 * LANES) if full else (ATT_BQ, 2 * LANES)
        return pl.BlockSpec(shape, lambda r, i: (0 if full else i, at(r, which)))

    out_spec = pl.BlockSpec((ATT_BQ, 2 * LANES), lambda r, i: (i, r))
    o, lse = pl.pallas_call(
        kern, name=name, grid=(dil, nq),
        in_specs=[qkv_spec(0, False), qkv_spec(1, True), qkv_spec(2, True),
                  pl.BlockSpec((None, ATT_GROUP_HEADS, ATT_BQ, ATT_W), lambda r, i: (_att_tile_id(i, nq), 0, 0, 0))],
        out_specs=[out_spec, out_spec],
        out_shape=[jax.ShapeDtypeStruct((n_sub, dil * 2 * LANES), F32)] * 2,
        compiler_params=_params("parallel", "arbitrary"),
    )(view, view, view, bias)
    return o.reshape(s_len, 2 * LANES), lse.reshape(s_len, 2 * LANES)


def _att_bwd(qkvm, bias, lse_tot, delta, dcat, gi, *, name):
    dil = DILATIONS[gi]
    s_len = qkvm.shape[0]
    n_sub = s_len // dil
    nq = n_sub // ATT_BQ
    view, at = _residue_view(qkvm, [gi, 3 + gi, 6 + gi], dil)
    lse_v = lse_tot.reshape(n_sub, dil * 2 * LANES)
    delta_v = delta.reshape(n_sub, dil * 2 * LANES)
    dcat_v, dcat_at = _residue_view(dcat, [gi], dil)

    def kern(q_ref, k_ref, v_ref, b_ref, lse_ref, dl_ref, dm_ref, dq_ref, dk_ref, dv_ref, db_ref):
        r, i = pl.program_id(0), pl.program_id(1)
        start, off = _att_window(i, n_sub)
        valid = _att_valid(off)
        tile = _att_tile_id(i, nq)

        @pl.when(i == 0)
        def _():
            dk_ref[...] = jnp.zeros_like(dk_ref)
            dv_ref[...] = jnp.zeros_like(dv_ref)

        @pl.when((i == 0) & (r == 0))
        def _():
            db_ref[...] = jnp.zeros_like(db_ref)

        q4 = _heads(q_ref[...] * (ATT_HEAD_DIM ** -0.5), mask=True)
        k4 = _heads(k_ref[pl.ds(start, ATT_W), :])
        v4 = _heads(v_ref[pl.ds(start, ATT_W), :])
        dm4 = _heads(dm_ref[...].astype(MXU_DTYPE), mask=True)
        s = jnp.where(valid, _bdot_nt(q4, k4) + b_ref[tile], NEG_INF)
        p = jnp.exp(s - _head_scalar(lse_ref[...]))
        ds = p * (_bdot_nt(dm4, v4) - _head_scalar(dl_ref[...]))
        ds_mxu = ds.astype(MXU_DTYPE)
        dq_ref[...] = _join_heads(_bdot(ds_mxu, k4)) * (ATT_HEAD_DIM ** -0.5)
        dk4 = _bdot_tn(ds_mxu, q4)
        dv4 = _bdot_tn(p.astype(MXU_DTYPE), dm4)
        dk_ref[pl.ds(start, ATT_W), :] += jnp.concatenate([dk4[0] + dk4[1], dk4[2] + dk4[3]], axis=1)
        dv_ref[pl.ds(start, ATT_W), :] += jnp.concatenate([dv4[0] + dv4[1], dv4[2] + dv4[3]], axis=1)
        db_ref[tile] += ds

    def qkv_spec(which, full):
        shape = (n_sub, 2 * LANES) if full else (ATT_BQ, 2 * LANES)
        return pl.BlockSpec(shape, lambda r, i: (0 if full else i, at(r, which)))

    blk = pl.BlockSpec((ATT_BQ, 2 * LANES), lambda r, i: (i, r))
    full = pl.BlockSpec((n_sub, 2 * LANES), lambda r, i: (0, r))
    bias_spec = pl.BlockSpec(bias.shape, lambda r, i: (0, 0, 0, 0))
    sub = jax.ShapeDtypeStruct((n_sub, dil * 2 * LANES), F32)
    dq, dk, dv, db = pl.pallas_call(
        kern, name=name, grid=(dil, nq),
        in_specs=[qkv_spec(0, False), qkv_spec(1, True), qkv_spec(2, True), bias_spec, blk, blk,
                  pl.BlockSpec((ATT_BQ, 2 * LANES), lambda r, i: (i, dcat_at(r, 0)))],
        out_specs=[blk, full, full, bias_spec],
        out_shape=[sub, sub, sub, jax.ShapeDtypeStruct(bias.shape, F32)],
        compiler_params=_params("arbitrary", "arbitrary"),
    )(view, view, view, bias, lse_v, delta_v, dcat_v)
    return dq.reshape(s_len, -1), dk.reshape(s_len, -1), dv.reshape(s_len, -1), db


def _att_combine(o_g, lse_g, qkvm, kv_mem, *, name):
    def body(o0, o1, o2, l0, l1, l2, qm, kv):
        mx = jnp.maximum(jnp.maximum(l0, l1), l2)
        tot = mx + jnp.log(jnp.exp(l0 - mx) + jnp.exp(l1 - mx) + jnp.exp(l2 - mx))
        mixed = [o * jnp.exp(l - tot) for o, l in ((o0, l0), (o1, l1), (o2, l2))]
        return (jnp.concatenate(mixed + [_mem_attn(qm.astype(F32), kv)], axis=1), tot), ()
    return _rowwise(body, list(o_g) + list(lse_g) + [_col(qkvm, MEM_WIDTH, (3 * TOK_WIDTH) // MEM_WIDTH)], [kv_mem],
                    [(D_MODEL, F32), (MEM_WIDTH, F32)], [], tm=256, name=name)


def _head_sum_matrix():
    a = np.arange(MEM_WIDTH)
    return jnp.asarray((a[:, None] // ATT_HEAD_DIM == a[None, :] // ATT_HEAD_DIM).astype(np.float32))


def _att_bwd_prep(cat, dcat, qkvm, kv_mem, *, name):
    def body(cat, dcat, qm, kv, hs):
        prod = cat * dcat
        summed = prod[:, 0:256] + prod[:, 256:512] + prod[:, 512:768]
        delta = _dot(summed, hs, precision=HI)
        dqm, dkv = _mem_attn_bwd(qm.astype(F32), kv, dcat[:, TOK_WIDTH:])
        return (delta, dqm), (dkv,)
    return _rowwise(body, [cat, dcat, _col(qkvm, MEM_WIDTH, (3 * TOK_WIDTH) // MEM_WIDTH)], [kv_mem, _head_sum_matrix()],
                    [(MEM_WIDTH, F32), (MEM_WIDTH, F32)], [kv_mem.shape], tm=256, name=name)


def _dn_conv_post(s, j):
    scale = jnp.where(j < DN_HEADS, DN_HEAD_DIM ** -0.5, 1.0)
    normed = s * lax.rsqrt(jnp.sum(s * s, axis=-1, keepdims=True) + EPS) * scale
    return jnp.where(j >= 2 * DN_HEADS, s, normed)


def _shift_rows(x, sh):
    n = x.shape[0]
    row = lax.broadcasted_iota(jnp.int32, (n, 1), 0)
    rolled = pltpu.roll(x, (-sh) % n, 0)
    return jnp.where((row + sh >= 0) & (row + sh < n), rolled, 0.0)


def _dn_conv_taps(x, w_ref):
    c = x * w_ref[pl.ds(DN_CONV // 2, 1), :]
    for jj in range(DN_CONV):
        if jj != DN_CONV // 2:
            c = c + _shift_rows(x, jj - DN_CONV // 2) * w_ref[pl.ds(jj, 1), :]
    return c


def _dn_conv_fwd(proj, conv_w, *, name):
    s_len = proj.shape[0]
    width = 3 * TOK_WIDTH

    def kern(x_ref, w_ref, o_ref):
        j = pl.program_id(0)
        o_ref[...] = _dn_conv_post(_silu(_dn_conv_taps(x_ref[...], w_ref)), j)

    return pl.pallas_call(
        kern, name=name, grid=(width // LANES,),
        in_specs=[pl.BlockSpec((s_len, LANES), lambda j: (0, j)), pl.BlockSpec((DN_CONV, LANES), lambda j: (0, j))],
        out_specs=pl.BlockSpec((s_len, LANES), lambda j: (0, j)),
        out_shape=jax.ShapeDtypeStruct((s_len, width), F32),
        compiler_params=_params("parallel"),
    )(proj, conv_w)


def _dn_conv_bwd(proj, conv_w, d_fwd, d_bwd, which, *, name):
    s_len = proj.shape[0]

    def kern(x_ref, w_ref, df_ref, db_ref, dx_ref, dw_ref):
        j = pl.program_id(0) + which * DN_HEADS
        x = x_ref[...]
        c = _dn_conv_taps(x, w_ref)
        _, vjp = jax.vjp(lambda c_: _dn_conv_post(_silu(c_), j), c)
        dc = vjp(df_ref[...] + db_ref[...])[0]
        dx = dc * w_ref[pl.ds(DN_CONV // 2, 1), :]
        for jj in range(DN_CONV):
            sh = jj - DN_CONV // 2
            if sh != 0:
                dx = dx + _shift_rows(dc, -sh) * w_ref[pl.ds(jj, 1), :]
            dw_ref[pl.ds(jj, 1), :] = jnp.sum(dc * _shift_rows(x, sh), axis=0, keepdims=True)
        dx_ref[...] = dx

    return pl.pallas_call(
        kern, name=name, grid=(DN_HEADS,),
        in_specs=[pl.BlockSpec((s_len, LANES), lambda j: (0, j + which * DN_HEADS)),
                  pl.BlockSpec((DN_CONV, LANES), lambda j: (0, j + which * DN_HEADS)),
                  pl.BlockSpec((s_len, LANES), lambda j: (0, j)),
                  pl.BlockSpec((s_len, LANES), lambda j: (0, j))],
        out_specs=[pl.BlockSpec((s_len, LANES), lambda j: (0, j)), pl.BlockSpec((DN_CONV, LANES), lambda j: (0, j))],
        out_shape=[jax.ShapeDtypeStruct((s_len, TOK_WIDTH), F32), jax.ShapeDtypeStruct((DN_CONV, TOK_WIDTH), F32)],
        compiler_params=_params("parallel"),
    )(proj, conv_w, d_fwd, d_bwd)


GATE_TM = 2 * DN_CHUNK


FWD_GATE_LANES = 2 * DN_HEADS


def _gate_constants():
    i = np.arange(GATE_TM)
    same = (i[:, None] // DN_CHUNK) == (i[None, :] // DN_CHUNK)
    cum_f = same & (i[None, :] <= i[:, None])
    cum_r = same & (i[None, :] >= i[:, None])
    return tuple(jnp.asarray(np.asarray(a, np.float32)) for a in (cum_f, cum_r, same))


def _gate_params(p):
    z = jnp.zeros((DN_HEADS,), F32)
    return jnp.concatenate([p[0], z, p[1], z, jnp.zeros((LANES - N_GATES,), F32)]).reshape(1, LANES)


def _gate_params_bwd(dp):
    return jnp.stack([dp[0, 0:DN_HEADS], dp[0, 2 * DN_HEADS: 3 * DN_HEADS]])


def _dn_gates(gate_in, a_cols, dt_cols, cum_f, cum_r, tot):
    g = -jnp.exp(a_cols) * _softplus(gate_in + dt_cols)
    fwd_lane = lax.broadcasted_iota(jnp.int32, (1, LANES), 1) < FWD_GATE_LANES
    gc = jnp.where(fwd_lane, _dot(cum_f, g, precision=HI), _dot(cum_r, g, precision=HI))
    return gc, _dot(tot, g, precision=HI), jax.nn.sigmoid(gate_in)


def _dn_gates_fwd(proj, a_cols, dt_cols, *, name):
    def body(gi, *consts):
        return _dn_gates(gi, *consts), ()
    return _rowwise(body, [_col(proj, LANES, DN_IN_PAD // LANES - 1)], [a_cols, dt_cols, *_gate_constants()],
                    [(LANES, F32)] * 3, [], tm=GATE_TM, name=name)


def _dn_gates_bwd(proj, a_cols, dt_cols, d_gates, *, name):
    def body(gi, gcf, gtf, bf, gcr, gtr, br, a, dt, *consts):
        _, vjp = jax.vjp(lambda gi_, a_, dt_: _dn_gates(gi_, a_, dt_, *consts), gi, a, dt)
        dgi, da, ddt = vjp((gcf + gcr, gtf + gtr, bf + br))
        return (dgi,), (da, ddt)
    return _rowwise(body, [_col(proj, LANES, DN_IN_PAD // LANES - 1), *d_gates[0], *d_gates[1]],
                    [a_cols, dt_cols, *_gate_constants()], [(LANES, F32)], [a_cols.shape, dt_cols.shape],
                    tm=GATE_TM, name=name)


INV_BASE = 8


def _block_id_equal(c, size):
    i = lax.broadcasted_iota(jnp.int32, (c, c), 0) // size
    j = lax.broadcasted_iota(jnp.int32, (c, c), 1) // size
    return (i == j).astype(F32)


def _unit_tri_inverse_impl(lmat):
    c = lmat.shape[0]
    eye = _block_id_equal(c, 1)
    same = _block_id_equal(c, INV_BASE)
    neg = -lmat * same
    inv = eye + neg
    power = neg
    for _ in range(int(math.log2(INV_BASE)) - 1):
        power = _dot(power, power)
        inv = inv + _dot(inv, power)
    size = INV_BASE
    while size < c:
        bigger = _block_id_equal(c, 2 * size)
        inv = inv - _dot(_dot(inv, lmat * (bigger - same)), inv)
        same, size = bigger, 2 * size
    resid = eye - _dot(eye + lmat, inv, precision=HI)
    return inv + _dot(inv, resid)


@jax.custom_vjp
def _unit_tri_inverse(lmat):
    return _unit_tri_inverse_impl(lmat)


def _unit_tri_inverse_fwd(lmat):
    inv = _unit_tri_inverse_impl(lmat)
    return inv, inv


def _unit_tri_inverse_bwd(inv, d_inv):
    return (-_dot_tn(inv, _dot_nt(d_inv, inv)),)


_unit_tri_inverse.defvjp(_unit_tri_inverse_fwd, _unit_tri_inverse_bwd)


def _dn_chunk(q, k, v, gates_t, gc_row, tot_row, beta_row, state, tri, inverse):
    c = q.shape[0]
    assert c == DN_HEAD_DIM
    eye = _block_id_equal(c, 1)

    def along_rows(x, pick):
        return jnp.broadcast_to(jnp.sum(x * pick, axis=0, keepdims=True), (c, c))

    gc_j = along_rows(gates_t[0], gc_row)
    gc = gc_j.T
    g_tot = along_rows(gates_t[1], tot_row)
    beta = along_rows(gates_t[2], beta_row).T
    decay = jnp.exp(jnp.where(tri > 0, gc - gc_j, NEG_INF))
    k_beta = k * beta
    inv = inverse((tri - eye) * (_dot_nt(k_beta, k) * decay))
    e_gc = jnp.exp(gc)
    u = _dot(inv, v * beta)
    w = _dot(inv, k_beta * e_gc)
    intra = tri * (_dot_nt(q, k) * decay)
    v_new = u - _dot(w, state)
    out = _dot(q * e_gc, state) + _dot(intra, v_new)
    state = state * jnp.exp(g_tot) + _dot_tn(k * jnp.exp(g_tot - gc), v_new)
    return out, state


def _dn_tri():
    i = np.arange(DN_CHUNK)
    tri = np.stack([(i[None, :] <= i[:, None]), (i[None, :] >= i[:, None])]).astype(np.float32)
    return jnp.asarray(np.repeat(tri, DN_HEADS, axis=0))


def _dn_gate_picks():
    picks = np.zeros((3, 2 * DN_HEADS, 2 * DN_CHUNK, 1), np.float32)
    for d in range(2):
        for h in range(DN_HEADS):
            alpha = d * DN_CHUNK + d * 2 * DN_HEADS + h
            picks[0, d * DN_HEADS + h, alpha] = 1.0
            picks[1, d * DN_HEADS + h, alpha] = 1.0
            picks[2, d * DN_HEADS + h, alpha + DN_HEADS] = 1.0
    return jnp.asarray(picks)


def _stack_chains(fwd_ref, rev_ref):
    return jnp.stack([r[:, _head_cols(h)] for r in (fwd_ref, rev_ref) for h in range(DN_HEADS)])


def _unstack_chains(val, fwd_ref, rev_ref):
    for d, r in enumerate((fwd_ref, rev_ref)):
        for h in range(DN_HEADS):
            r[:, _head_cols(h)] = val[d * DN_HEADS + h]


def _gates_transposed(fwd_refs, rev_refs):
    return jnp.stack([jnp.concatenate([f[...].T, r[...].T], axis=0) for f, r in zip(fwd_refs, rev_refs, strict=True)])


def _dn_row_spec(nc, col, reverse, width=TOK_WIDTH):
    return pl.BlockSpec((DN_CHUNK, width), lambda t: ((nc - 1 - t) if reverse else t, col))


def _dn_state_spec(nc, reverse):
    return pl.BlockSpec((None, DN_HEADS, DN_HEAD_DIM, DN_HEAD_DIM), lambda t: ((nc - 1 - t) if reverse else t, 0, 0, 0))


def _head_cols(h):
    return pl.ds(h * DN_HEAD_DIM, DN_HEAD_DIM)


def _const_spec(arr):
    return pl.BlockSpec(arr.shape, functools.partial(lambda t, n: (0,) * n, n=arr.ndim))


def _dn_chains(inverse):
    return jax.vmap(lambda q, k, v, gates_t, *rest: _dn_chunk(q, k, v, gates_t, *rest, inverse),
                    in_axes=(0, 0, 0, None, 0, 0, 0, 0, 0))


def _dn_scan_fwd(qkv, gates, *, name):
    s_len = qkv.shape[0]
    nc = s_len // DN_CHUNK
    tri, picks = _dn_tri(), _dn_gate_picks()

    def kern(*refs):
        ins, (tri_ref, pick_ref, of_ref, or_ref, sf_ref, sr_ref, state) = refs[:12], refs[12:]

        @pl.when(pl.program_id(0) == 0)
        def _():
            state[...] = jnp.zeros_like(state)

        entry = state[...]
        qkv_c = [_stack_chains(ins[i], ins[6 + i]) for i in range(3)]
        gates_t = _gates_transposed(ins[3:6], ins[9:12])
        out, new = _dn_chains(_unit_tri_inverse_impl)(*qkv_c, gates_t, pick_ref[0], pick_ref[1], pick_ref[2], entry, tri_ref[...])
        sf_ref[...] = entry[:DN_HEADS]
        sr_ref[...] = entry[DN_HEADS:]
        _unstack_chains(out, of_ref, or_ref)
        state[...] = new

    in_specs = []
    for rev in (False, True):
        in_specs += [_dn_row_spec(nc, col, rev) for col in (0, 1, 2)] + [_dn_row_spec(nc, 0, rev, LANES)] * 3
    in_specs += [_const_spec(tri), _const_spec(picks)]
    return pl.pallas_call(
        kern, name=name, grid=(nc,), in_specs=in_specs,
        out_specs=[_dn_row_spec(nc, 0, False), _dn_row_spec(nc, 0, True), _dn_state_spec(nc, False), _dn_state_spec(nc, True)],
        out_shape=[jax.ShapeDtypeStruct((s_len, TOK_WIDTH), F32)] * 2
        + [jax.ShapeDtypeStruct((nc, DN_HEADS, DN_HEAD_DIM, DN_HEAD_DIM), F32)] * 2,
        scratch_shapes=[pltpu.VMEM((2 * DN_HEADS, DN_HEAD_DIM, DN_HEAD_DIM), F32)],
        compiler_params=_params("arbitrary"),
    )(*([qkv, qkv, qkv, *gates] * 2), tri, picks)


def _dn_scan_bwd(qkv, gates, states, d_o, *, name):
    s_len = qkv.shape[0]
    nc = s_len // DN_CHUNK
    tri, picks = _dn_tri(), _dn_gate_picks()

    def kern(*refs):
        ins, tri_ref, pick_ref, outs, d_state = refs[:16], refs[16], refs[17], refs[18:30], refs[30]

        @pl.when(pl.program_id(0) == 0)
        def _():
            d_state[...] = jnp.zeros_like(d_state)

        qkv_c = [_stack_chains(ins[i], ins[8 + i]) for i in range(3)]
        gates_t = _gates_transposed(ins[3:6], ins[11:14])
        entry = jnp.concatenate([ins[6][...], ins[14][...]], axis=0)
        d_out = _stack_chains(ins[7], ins[15])
        tri_v, picks_v = tri_ref[...], pick_ref[...]
        _, vjp = jax.vjp(lambda q, k, v, g, s: _dn_chains(_unit_tri_inverse)(q, k, v, g, picks_v[0], picks_v[1], picks_v[2], s, tri_v),
                         *qkv_c, gates_t, entry)
        dq, dk, dv, d_gates_t, d_entry = vjp((d_out, d_state[...]))
        for i, val in enumerate((dq, dk, dv)):
            _unstack_chains(val, outs[i], outs[6 + i])
        for i in range(3):
            outs[3 + i][...] = d_gates_t[i, :DN_CHUNK].T
            outs[9 + i][...] = d_gates_t[i, DN_CHUNK:].T
        d_state[...] = d_entry

    in_specs, out_specs, out_shape = [], [], []
    for rev in (True, False):
        in_specs += [_dn_row_spec(nc, col, rev) for col in (0, 1, 2)] + [_dn_row_spec(nc, 0, rev, LANES)] * 3
        in_specs += [_dn_state_spec(nc, rev), _dn_row_spec(nc, 0, rev)]
        out_specs += [_dn_row_spec(nc, 0, rev)] * 3 + [_dn_row_spec(nc, 0, rev, LANES)] * 3
        out_shape += [jax.ShapeDtypeStruct((s_len, TOK_WIDTH), F32)] * 3 + [jax.ShapeDtypeStruct((s_len, LANES), F32)] * 3
    in_specs += [_const_spec(tri), _const_spec(picks)]
    res = pl.pallas_call(
        kern, name=name, grid=(nc,), in_specs=in_specs, out_specs=out_specs, out_shape=out_shape,
        scratch_shapes=[pltpu.VMEM((2 * DN_HEADS, DN_HEAD_DIM, DN_HEAD_DIM), F32)],
        compiler_params=_params("arbitrary"),
    )(*[a for d in range(2) for a in (qkv, qkv, qkv, *gates, states[d], d_o)], tri, picks)
    return (res[0:3], res[3:6]), (res[6:9], res[9:12])


def _dn_out_head(o_f, o_b, z, gain):
    o = o_f + o_b
    return o * lax.rsqrt(jnp.mean(o * o, axis=-1, keepdims=True) + EPS) * gain * _silu(z)


def _dn_out(o_fwd, o_rev, proj, gain, qkv_kv_mem, *, name):
    def body(of, ob, z, qm, g, kv):
        heads = []
        for h in range(DN_HEADS):
            sl = slice(h * DN_HEAD_DIM, (h + 1) * DN_HEAD_DIM)
            heads.append(_dn_out_head(of[:, sl], ob[:, sl], z[:, sl], g))
        return (jnp.concatenate(heads + [_mem_attn(qm, kv)], axis=1),), ()
    return _rowwise(body, [o_fwd, o_rev, _col(proj, TOK_WIDTH, 3),
                           _col(proj, MEM_WIDTH, (4 * TOK_WIDTH) // MEM_WIDTH)], [gain, qkv_kv_mem],
                    [(D_MODEL, MXU_DTYPE)], [], tm=256, name=name)[0]


def _dn_out_bwd(o_fwd, o_rev, proj, gain, kv_mem, dcat, *, name):
    def body(of, ob, z, qm, dcat, g, kv):
        dos, dzs = [], []
        dgain = jnp.zeros_like(g)
        for h in range(DN_HEADS):
            sl = slice(h * DN_HEAD_DIM, (h + 1) * DN_HEAD_DIM)
            _, vjp = jax.vjp(_dn_out_head, of[:, sl], ob[:, sl], z[:, sl], g)
            d_of, _, dz, dg = vjp(dcat[:, sl])
            dos.append(d_of)
            dzs.append(dz)
            dgain = dgain + dg
        dqm, dkv = _mem_attn_bwd(qm, kv, dcat[:, TOK_WIDTH:])
        return (jnp.concatenate(dos, axis=1), jnp.concatenate(dzs, axis=1), dqm), (dgain, dkv)
    return _rowwise(body, [o_fwd, o_rev, _col(proj, TOK_WIDTH, 3),
                           _col(proj, MEM_WIDTH, (4 * TOK_WIDTH) // MEM_WIDTH), dcat], [gain, kv_mem],
                    [(TOK_WIDTH, F32), (TOK_WIDTH, F32), (MEM_WIDTH, F32)], [gain.shape, kv_mem.shape], tm=256, name=name)


def _pad_dn_w_in(w):
    gates = w[:, 4 * TOK_WIDTH: 4 * TOK_WIDTH + N_GATES]
    zeros = jnp.zeros((w.shape[0], DN_IN_PAD - DN_IN), w.dtype)
    return jnp.concatenate([w[:, :4 * TOK_WIDTH], w[:, 4 * TOK_WIDTH + N_GATES:], gates, zeros], axis=1)


def _unpad_dn_w_in(w):
    q_mem = w[:, 4 * TOK_WIDTH: 4 * TOK_WIDTH + MEM_WIDTH]
    gates = w[:, 4 * TOK_WIDTH + MEM_WIDTH: 4 * TOK_WIDTH + MEM_WIDTH + N_GATES]
    return jnp.concatenate([w[:, :4 * TOK_WIDTH], gates, q_mem], axis=1)


def _ffn_fwd(h, w_gu, w_d, tag):
    gu = _mm(h, w_gu, out_dtype=MXU_DTYPE, name=f"ffn_gu_{tag}")
    act = _swiglu_act(gu, name=f"ffn_act_{tag}")
    return gu, act, _mm(act, w_d, name=f"ffn_down_{tag}")


def _ffn_bwd(h, gu, act, w_gu, w_d, df, tag):
    d_act = _mm(df, w_d, tb=True, out_dtype=MXU_DTYPE, name=f"ffn_dact_{tag}")
    d_wd = _mm(act, df, ta=True, out_dtype=LINK_DTYPE, name=f"ffn_dwd_{tag}")
    d_gu = _swiglu_act_bwd(gu, d_act, name=f"ffn_dgu_{tag}")
    dh = _mm(d_gu, w_gu, tb=True, name=f"ffn_dh_{tag}")
    d_wgu = _mm(h, d_gu, ta=True, out_shards=_shards_of(w_gu), out_dtype=LINK_DTYPE, name=f"ffn_dwgu_{tag}")
    return dh, d_wgu, d_wd


def _local_step(x, mem, target, p):
    g = {}
    row = lambda v: v.reshape(1, -1)
    gains = {k: [row(p[k][i]) for i in range(2)] for k in
             ("mem_norm", "norm_mix_pre", "norm_mix_post", "norm_ffn_pre", "norm_ffn_post")}
    out_gain = row(p["dn_out_norm"])
    a_cols, dt_cols = _gate_params(p["dn_a_log"]), _gate_params(p["dn_dt_bias"])

    h0 = _pre_norm(x, gains["norm_mix_pre"][0], name="pre0")
    mem_n = [_pre_norm(mem, gains["mem_norm"][i], name=f"mem_norm{i}") for i in range(2)]
    qkvm = _mm(h0, p["att_w_in"], out_dtype=MXU_DTYPE, name="att_in")
    bias = [_bias_tiles(p["rel_bias"], gi) for gi in range(3)]
    att = [_att_fwd(qkvm, bias[gi], gi, name=f"att_fwd{gi}") for gi in range(3)]
    att_w_out, *mem_w_kv = p["att_rest_weights"](att[2][0])
    kv_mem = [_mm(mem_n[i], mem_w_kv[i], name=f"mem_kv{i}") for i in range(2)]
    cat0, lse_tot = _att_combine([a[0] for a in att], [a[1] for a in att], qkvm, kv_mem[0], name="att_combine")
    mo0 = _mm(cat0, att_w_out, name="att_out")
    x1, h1 = _post_pre(x, mo0, gains["norm_mix_post"][0], gains["norm_ffn_pre"][0], name="post_mix0")
    w_gu0, w_d0 = p["ffn_weights"](0, h1)
    gu0, act0, f0 = _ffn_fwd(h1, w_gu0, w_d0, 0)
    x2, h2 = _post_pre(x1, f0, gains["norm_ffn_post"][0], gains["norm_mix_pre"][1], name="post_ffn0")

    dn_w_in, dn_w_out, dn_conv_w = p["dn_weights"](h2)
    proj = _mm(h2, dn_w_in, name="dn_in")
    qkv = _dn_conv_fwd(proj, dn_conv_w, name="dn_conv")
    gates = _dn_gates_fwd(proj, a_cols, dt_cols, name="dn_gates")
    o_fwd, o_rev, st_fwd, st_rev = _dn_scan_fwd(qkv, gates, name="dn_scan")
    cat1 = _dn_out(o_fwd, o_rev, proj, out_gain, kv_mem[1], name="dn_outnorm")
    mo1 = _mm(cat1, dn_w_out, name="dn_out")
    x3, h3 = _post_pre(x2, mo1, gains["norm_mix_post"][1], gains["norm_ffn_pre"][1], name="post_mix1")
    w_gu1, w_d1 = p["ffn_weights"](1, h3)
    gu1, act1, f1 = _ffn_fwd(h3, w_gu1, w_d1, 1)

    dx3, df1, dg_ffn_post1, loss_cols = _final_loss_bwd(x3, f1, gains["norm_ffn_post"][1], target, name="loss_bwd")
    dh3, d_wgu1, d_wd1 = _ffn_bwd(h3, gu1, act1, w_gu1, w_d1, df1, 1)
    sent = p["grads_ready"]("ffn1", {("ffn_w_gate_up", 1): d_wgu1, ("ffn_w_down", 1): d_wd1})
    dx2, dmo1, dg_mix_post1, dg_ffn_pre1 = _post_pre_bwd(x2, mo1, gains["norm_mix_post"][1] + sent, gains["norm_ffn_pre"][1],
                                                         dx3, dh3, name="post_mix1_bwd")
    dcat1 = _mm(dmo1, dn_w_out, tb=True, name="dn_out_dx")
    g["dn_w_out"] = _mm(cat1, dmo1, ta=True, out_dtype=LINK_DTYPE, name="dn_out_dw")
    d_o, dz, dqm1, d_out_gain, dkv1 = _dn_out_bwd(o_fwd, o_rev, proj, out_gain, kv_mem[1], dcat1, name="dn_outnorm_bwd")
    (d_f, dg_f), (d_r, dg_r) = _dn_scan_bwd(qkv, gates, (st_fwd, st_rev), d_o, name="dn_scan_bwd")
    d_gate_cols, d_a_cols, d_dt_cols = _dn_gates_bwd(proj, a_cols, dt_cols, (dg_f, dg_r), name="dn_gates_bwd")
    d_pre, d_conv = zip(*[_dn_conv_bwd(proj, dn_conv_w, d_f[which], d_r[which], which, name=f"dn_conv_bwd{which}")
                          for which in range(3)])
    dproj = jnp.concatenate(list(d_pre) + [dz, dqm1, d_gate_cols], axis=1).astype(MXU_DTYPE)
    dh2 = _mm(dproj, dn_w_in, tb=True, name="dn_in_dx")
    g["dn_w_in"] = _mm(h2, dproj, ta=True, out_dtype=LINK_DTYPE, name="dn_in_dw")
    g["dn_conv"] = jnp.concatenate(d_conv, axis=1)
    g["dn_a_log"] = _gate_params_bwd(d_a_cols)
    g["dn_dt_bias"] = _gate_params_bwd(d_dt_cols)
    g["dn_out_norm"] = d_out_gain

    d_mem_kv1 = _mm(mem_n[1], dkv1, ta=True, out_dtype=LINK_DTYPE, name="mem_kv_dw1")
    sent = p["grads_ready"]("dn", {("dn_w_in", 0): g["dn_w_in"], ("dn_w_out", 0): g["dn_w_out"], ("mem_w_kv", 1): d_mem_kv1})
    dx1, df0, dg_ffn_post0, dg_mix_pre1 = _post_pre_bwd(x1, f0, gains["norm_ffn_post"][0] + sent, gains["norm_mix_pre"][1],
                                                        dx2, dh2, name="post_ffn0_bwd")
    dh1, d_wgu0, d_wd0 = _ffn_bwd(h1, gu0, act0, w_gu0, w_d0, df0, 0)
    sent = p["grads_ready"]("ffn0", {("ffn_w_gate_up", 0): d_wgu0, ("ffn_w_down", 0): d_wd0})
    dx0, dmo0, dg_mix_post0, dg_ffn_pre0 = _post_pre_bwd(x, mo0, gains["norm_mix_post"][0] + sent, gains["norm_ffn_pre"][0],
                                                         dx1, dh1, name="post_mix0_bwd")
    dcat0 = _mm(dmo0, att_w_out, tb=True, name="att_out_dx")
    g["att_w_out"] = _mm(cat0, dmo0, ta=True, out_dtype=LINK_DTYPE, name="att_out_dw")
    delta, dqm0, dkv0 = _att_bwd_prep(cat0, dcat0, qkvm, kv_mem[0], name="att_bwd_prep")
    d_mem_kv0 = _mm(mem_n[0], dkv0, ta=True, out_dtype=LINK_DTYPE, name="mem_kv_dw0")
    sent = p["grads_ready"]("att_out", {("att_w_out", 0): g["att_w_out"], ("mem_w_kv", 0): d_mem_kv0})
    att_b = [_att_bwd(qkvm, bias[gi] + sent, lse_tot, delta, dcat0, gi, name=f"att_bwd{gi}") for gi in range(3)]
    dqkvm = jnp.concatenate([a[w] for w in range(3) for a in att_b] + [dqm0], axis=1).astype(MXU_DTYPE)
    g["rel_bias"] = sum(_bias_tiles_bwd(p["rel_bias"], att_b[gi][3], gi) for gi in range(3))
    g["att_w_in"] = _mm(h0, dqkvm, ta=True, out_shards=_shards_of(p["att_w_in"]), out_dtype=LINK_DTYPE, name="att_in_dw")
    sent = p["grads_ready"]("att_in", {("att_w_in", 0): g["att_w_in"]})
    dh0 = _mm(dqkvm, p["att_w_in"], tb=True, name="att_in_dx")
    grad_x, dg_mix_pre0 = _pre_norm_bwd(x, gains["norm_mix_pre"][0] + sent, dh0, dx0, name="pre0_bwd")

    d_mem_norm = []
    for i, dkv in enumerate((dkv0, dkv1)):
        d_mem_n = _mm(dkv, mem_w_kv[i], tb=True, name=f"mem_kv_dx{i}")
        d_mem_norm.append(_gain_bwd(mem, gains["mem_norm"][i], d_mem_n, name=f"mem_norm_bwd{i}"))
    g["mem_w_kv"] = [d_mem_kv0, d_mem_kv1]
    g["mem_norm"] = jnp.concatenate(d_mem_norm, axis=0)
    g["norm_mix_pre"] = jnp.concatenate([dg_mix_pre0, dg_mix_pre1], axis=0)
    g["norm_mix_post"] = jnp.concatenate([dg_mix_post0, dg_mix_post1], axis=0)
    g["norm_ffn_pre"] = jnp.concatenate([dg_ffn_pre0, dg_ffn_pre1], axis=0)
    g["norm_ffn_post"] = jnp.concatenate([dg_ffn_post0, dg_ffn_post1], axis=0)
    g["ffn_w_gate_up"] = [d_wgu0, d_wgu1]
    g["ffn_w_down"] = [d_wd0, d_wd1]
    return loss_cols, grad_x, g


N_CHIPS = 4
N_DEV = 8
MESH = pl.DeviceIdType.MESH
BIG = (("att_w_in", (1, 1024, 640), 2), ("att_w_out", (1, 256, 1024), 1), ("dn_w_in", (1, 1024, 838), 2),
       ("dn_w_out", (1, 256, 1024), 1), ("mem_w_kv", (2, 256, 512), 1), ("ffn_w_gate_up", (2, 1024, 1408), 2),
       ("ffn_w_down", (2, 704, 1024), 1))


def _mesh_pos():
    return lax.axis_index("x"), lax.axis_index("y"), lax.axis_index("c")


def _other_chips(x, y):
    return [(1 - x, y), (x, 1 - y), (1 - x, 1 - y)]


ANY = pl.BlockSpec(memory_space=pl.ANY)


def _all_reduce_small(v, *, name):
    rows, cols = v.shape
    flips = [(dx, dy, dc) for dx in (0, 1) for dy in (0, 1) for dc in (0, 1)][1:]

    def body(v_ref, o_ref, buf, send_sems, recv_sems):
        x, y, c = _mesh_pos()

        def peer(f):
            return tuple(1 - p if fl else p for p, fl in zip((x, y, c), f))

        def index(p):
            return 4 * p[0] + 2 * p[1] + p[2]

        buf[index((x, y, c))] = v_ref[...]
        sends = []
        for k, f in enumerate(flips):
            cp = pltpu.make_async_remote_copy(src_ref=v_ref, dst_ref=buf.at[index((x, y, c))], send_sem=send_sems.at[k],
                                              recv_sem=recv_sems.at[k], device_id=peer(f), device_id_type=MESH)
            cp.start()
            sends.append(cp)
        for k, f in enumerate(flips):
            pltpu.make_async_remote_copy(src_ref=v_ref, dst_ref=buf.at[index(peer(f))], send_sem=send_sems.at[k],
                                         recv_sem=recv_sems.at[k], device_id=peer(f), device_id_type=MESH).wait_recv()
        for cp in sends:
            cp.wait_send()
        acc = buf[0]
        for d in range(1, N_DEV):
            acc = acc + buf[d]
        o_ref[...] = acc

    vmem = pl.BlockSpec(memory_space=pltpu.VMEM)
    return pl.pallas_call(
        body, name=name, in_specs=[vmem], out_specs=vmem, out_shape=jax.ShapeDtypeStruct((rows, cols), F32),
        scratch_shapes=[pltpu.VMEM((N_DEV, rows, cols), F32), pltpu.SemaphoreType.DMA((N_DEV - 1,)),
                        pltpu.SemaphoreType.DMA((N_DEV - 1,))],
    )(v)


def _adamw_update(w, g, m, v):
    m = ADAM_B1 * m + (1.0 - ADAM_B1) * g
    v = ADAM_B2 * v + (1.0 - ADAM_B2) * (g * g)
    m_hat = m / (1.0 - ADAM_B1 ** ADAM_STEP)
    v_hat = v / (1.0 - ADAM_B2 ** ADAM_STEP)
    return -ADAM_LR * (m_hat / (jnp.sqrt(v_hat) + ADAM_EPS) + ADAM_WD * w), m, v


def _adamw_many(ws, gs, ms, vs, *, name):
    n = len(ws)

    def kern(*refs):
        for i in range(n):
            outs = _adamw_update(*[refs[k * n + i][...] for k in range(4)])
            for k, val in enumerate(outs):
                refs[(4 + k) * n + i][...] = val

    vmem = pl.BlockSpec(memory_space=pltpu.VMEM)
    res = pl.pallas_call(
        kern, name=name, in_specs=[vmem] * (4 * n), out_specs=[vmem] * (3 * n),
        out_shape=[jax.ShapeDtypeStruct(a.shape, F32) for _ in range(3) for a in ws],
    )(*ws, *gs, *ms, *vs)
    return res[:n], res[n: 2 * n], res[2 * n:]


def _adamw(w, g, m, v, *, name):
    def body(w, g, m, v):
        return _adamw_update(w, g, m, v), ()
    rows, cols = w.shape
    if rows % SUBLANES == 0:
        return _rowwise(body, [w, g, m, v], [], [(cols, F32)] * 3, [], tm=_tile(rows, 256, SUBLANES), name=name)

    def kern(*refs):
        outs, _ = body(*[r[...] for r in refs[:4]])
        for r, val in zip(refs[4:], outs, strict=True):
            r[...] = val

    spec = pl.BlockSpec((rows, _tile(cols, 256, LANES)), lambda j: (0, j))
    return pl.pallas_call(
        kern, name=name, grid=(cols // spec.block_shape[1],), in_specs=[spec] * 4, out_specs=[spec] * 3,
        out_shape=[jax.ShapeDtypeStruct((rows, cols), F32)] * 3, compiler_params=_params("parallel"),
    )(w, g, m, v)


def _pack_small(arrs, rows):
    flat = jnp.concatenate([a.reshape(-1) for a in arrs])
    return jnp.pad(flat, (0, rows * LANES - flat.shape[0])).reshape(rows, LANES)


def _unpack_small(packed, shapes):
    flat = packed.reshape(-1)
    out, off = [], 0
    for s in shapes:
        size = math.prod(s)
        out.append(flat[off: off + size].reshape(s))
        off += size
    return out


def _small_rows(shapes):
    return -(-sum(math.prod(s) for s in shapes) // (SUBLANES * LANES)) * SUBLANES


def _sem_pairs(n):
    return [pltpu.SemaphoreType.DMA((n,)), pltpu.SemaphoreType.DMA((n,))]


def _gather_blocks(blocks, *, name):
    n = len(blocks)

    def body(*refs):
        x_refs, out_refs, (send_sems, recv_sems) = refs[:n], refs[n: 2 * n], refs[2 * n:]
        x, y, c = _mesh_pos()
        sibling = (x, y, 1 - c)
        chips = _other_chips(x, y)

        def copy(k, src, dst, to):
            return pltpu.make_async_remote_copy(src_ref=src, dst_ref=dst, send_sem=send_sems.at[k],
                                                recv_sem=recv_sems.at[k], device_id=to, device_id_type=MESH)

        def part(b, chip, h):
            half = blocks[b].shape[0] // 2
            return out_refs[b].at[2 * chip[0] + chip[1], pl.ds(h * half, half), :]

        def my_half(b):
            half = blocks[b].shape[0] // 2
            return x_refs[b].at[pl.ds(c * half, half), :]

        first = [copy(6 * b + j, my_half(b), part(b, (x, y), c), (*chip, c)) for b in range(n) for j, chip in enumerate(chips)]
        for cp in first:
            cp.start()
        passed = []
        for b in range(n):
            for j, chip in enumerate(chips):
                copy(6 * b + j, my_half(b), part(b, chip, c), (*chip, c)).wait_recv()
                cp = copy(6 * b + 3 + j, part(b, chip, c), part(b, chip, c), sibling)
                cp.start()
                passed.append(cp)
        for b in range(n):
            for j, chip in enumerate(chips):
                copy(6 * b + 3 + j, part(b, chip, 1 - c), part(b, chip, 1 - c), sibling).wait_recv()
        for cp in first + passed:
            cp.wait_send()

    return pl.pallas_call(
        body, name=name, in_specs=[ANY] * n, out_specs=[ANY] * n,
        out_shape=[jax.ShapeDtypeStruct((N_CHIPS, *a.shape), a.dtype) for a in blocks],
        scratch_shapes=_sem_pairs(6 * n),
    )(*blocks)


HBM = pl.BlockSpec(memory_space=pltpu.HBM)
SEM = pl.BlockSpec(memory_space=pltpu.SEMAPHORE)
DATAFLOW = pltpu.SideEffectType.DATAFLOW_SIDE_EFFECTING


def _gather_start(blocks, *, name):
    n = len(blocks)
    lands = [lax.empty((N_CHIPS, *a.shape), a.dtype) for a in blocks]

    def body(*refs):
        x_refs, land_refs, send_sems, recv_sems, token = refs[:n], refs[n: 2 * n], refs[2 * n], refs[2 * n + 1], refs[-1]
        x, y, c = _mesh_pos()
        for b in range(n):
            for j, chip in enumerate(_other_chips(x, y)):
                pltpu.make_async_remote_copy(src_ref=x_refs[b], dst_ref=land_refs[b].at[2 * x + y], send_sem=send_sems.at[3 * b + j],
                                             recv_sem=recv_sems.at[3 * b + j], device_id=(*chip, c), device_id_type=MESH).start()
        token[...] = jnp.zeros_like(token)

    operands = [pltpu.with_memory_space_constraint(a, pltpu.HBM) for a in blocks + lands]
    res = pl.pallas_call(
        body, name=name, in_specs=[HBM] * (2 * n),
        out_shape=(pltpu.SemaphoreType.DMA((3 * n,)), pltpu.SemaphoreType.DMA((3 * n,)),
                   *[pltpu.HBM(a.shape, a.dtype) for a in operands], jax.ShapeDtypeStruct((SUBLANES, LANES), F32)),
        out_specs=(SEM, SEM, *[HBM] * (2 * n), pl.BlockSpec(memory_space=pltpu.VMEM)),
        input_output_aliases={i: 2 + i for i in range(2 * n)},
        compiler_params=pltpu.CompilerParams(has_side_effects=DATAFLOW),
    )(*operands)
    return res[0], res[1], list(res[2: 2 + n]), list(res[2 + n: 2 + 2 * n]), res[-1]


def _gather_wait(started, after, *, name):
    send_sems, recv_sems, blocks, lands, _ = started
    n = len(blocks)

    def body(*refs):
        x_refs, land_refs, send_sems, recv_sems = refs[:n], refs[n: 2 * n], refs[2 * n], refs[2 * n + 1]
        x, y, c = _mesh_pos()
        for b in range(n):
            for j, chip in enumerate(_other_chips(x, y)):
                cp = pltpu.make_async_remote_copy(src_ref=x_refs[b], dst_ref=land_refs[b].at[2 * chip[0] + chip[1]],
                                                  send_sem=send_sems.at[3 * b + j], recv_sem=recv_sems.at[3 * b + j],
                                                  device_id=(*chip, c), device_id_type=MESH)
                cp.wait_send()
                cp.wait_recv()

    res = pl.pallas_call(
        body, name=name, in_specs=(*[HBM] * (2 * n), SEM, SEM, ANY),
        out_shape=tuple(pltpu.HBM(a.shape, a.dtype) for a in blocks + lands), out_specs=tuple([HBM] * (2 * n)),
        input_output_aliases={i: i for i in range(2 * n)},
        compiler_params=pltpu.CompilerParams(has_side_effects=DATAFLOW),
    )(*blocks, *lands, send_sems, recv_sems, after)
    return list(res[n:])


def _swap_with_sibling(blocks, *, name):
    n = len(blocks)

    def body(*refs):
        in_refs, out_refs, (send_sems, recv_sems) = refs[:n], refs[n: 2 * n], refs[2 * n:]
        x, y, c = _mesh_pos()
        copies = [pltpu.make_async_remote_copy(src_ref=in_refs[b], dst_ref=out_refs[b], send_sem=send_sems.at[b],
                                               recv_sem=recv_sems.at[b], device_id=(x, y, 1 - c), device_id_type=MESH)
                  for b in range(n)]
        for cp in copies:
            cp.start()
        for cp in copies:
            cp.wait()

    return pl.pallas_call(
        body, name=name, in_specs=[ANY] * n, out_specs=[ANY] * n,
        out_shape=[jax.ShapeDtypeStruct(a.shape, a.dtype) for a in blocks], scratch_shapes=_sem_pairs(n),
    )(*blocks)


def _sum_chips_block(parts, *, name):
    n, half, cols = parts.shape
    tm = _tile(half, 512, 2 * SUBLANES)

    def kern(p_ref, o_ref):
        acc = p_ref[0].astype(F32)
        for s in range(1, n):
            acc = acc + p_ref[s].astype(F32)
        o_ref[...] = acc

    return pl.pallas_call(
        kern, name=name, grid=(half // tm,),
        in_specs=[pl.BlockSpec((n, tm, cols), lambda i: (0, i, 0))],
        out_specs=pl.BlockSpec((tm, cols), lambda i: (i, 0)),
        out_shape=jax.ShapeDtypeStruct((half, cols), F32),
        compiler_params=_params("parallel"),
    )(parts)


PEER_FLIPS = [(dx, dy, dc) for dx in (0, 1) for dy in (0, 1) for dc in (0, 1)][1:]


def _flipped(pos, flip):
    return tuple(1 - p if f else p for p, f in zip(pos, flip))


def _device_index(pos):
    return 4 * pos[0] + 2 * pos[1] + pos[2]


def _scatter_start(blocks, *, name):
    n = len(blocks)
    lands = [lax.empty((N_DEV, a.shape[1] // 2, a.shape[2]), a.dtype) for a in blocks]

    def body(*refs):
        g_refs, land_refs, send_sems, recv_sems, token = refs[:n], refs[n: 2 * n], refs[2 * n], refs[2 * n + 1], refs[-1]
        pos = _mesh_pos()
        for b in range(n):
            half = blocks[b].shape[1] // 2
            for k, flip in enumerate(PEER_FLIPS):
                peer = _flipped(pos, flip)
                pltpu.make_async_remote_copy(src_ref=g_refs[b].at[2 * peer[0] + peer[1], pl.ds(peer[2] * half, half), :],
                                             dst_ref=land_refs[b].at[_device_index(pos)],
                                             send_sem=send_sems.at[7 * b + k], recv_sem=recv_sems.at[7 * b + k],
                                             device_id=peer, device_id_type=MESH).start()
        token[...] = jnp.zeros_like(token)

    operands = [pltpu.with_memory_space_constraint(a, pltpu.HBM) for a in blocks + lands]
    res = pl.pallas_call(
        body, name=name, in_specs=[HBM] * (2 * n),
        out_shape=(pltpu.SemaphoreType.DMA((7 * n,)), pltpu.SemaphoreType.DMA((7 * n,)),
                   *[pltpu.HBM(a.shape, a.dtype) for a in operands], jax.ShapeDtypeStruct((SUBLANES, LANES), F32)),
        out_specs=(SEM, SEM, *[HBM] * (2 * n), pl.BlockSpec(memory_space=pltpu.VMEM)),
        input_output_aliases={i: 2 + i for i in range(2 * n)},
        compiler_params=pltpu.CompilerParams(has_side_effects=DATAFLOW),
    )(*operands)
    return res[0], res[1], list(res[2: 2 + n]), list(res[2 + n: 2 + 2 * n]), res[-1]


def _scatter_wait(started, after, *, name):
    send_sems, recv_sems, blocks, lands, _ = started
    n = len(blocks)

    def body(*refs):
        g_refs, land_refs, send_sems, recv_sems = refs[:n], refs[n: 2 * n], refs[2 * n], refs[2 * n + 1]
        pos = _mesh_pos()
        for b in range(n):
            half = blocks[b].shape[1] // 2
            for k, flip in enumerate(PEER_FLIPS):
                peer = _flipped(pos, flip)
                cp = pltpu.make_async_remote_copy(src_ref=g_refs[b].at[0, pl.ds(0, half), :],
                                                  dst_ref=land_refs[b].at[_device_index(peer)],
                                                  send_sem=send_sems.at[7 * b + k], recv_sem=recv_sems.at[7 * b + k],
                                                  device_id=peer, device_id_type=MESH)
                cp.wait_send()
                cp.wait_recv()

    res = pl.pallas_call(
        body, name=name, in_specs=(*[HBM] * (2 * n), SEM, SEM, ANY),
        out_shape=tuple(pltpu.HBM(a.shape, a.dtype) for a in blocks + lands), out_specs=tuple([HBM] * (2 * n)),
        input_output_aliases={i: i for i in range(2 * n)},
        compiler_params=pltpu.CompilerParams(has_side_effects=DATAFLOW),
    )(*blocks, *lands, send_sems, recv_sems, after)
    return list(res[:n]), list(res[n:])


def _reduce_finish(begun, names, after):
    x, y, c = _mesh_pos()
    mine = {}
    for key, started in begun.items():
        blocks, lands = _scatter_wait(started, after, name=f"rs_scatter_wait_{key}")
        parts = []
        for blk, land in zip(blocks, lands, strict=True):
            half = blk.shape[1] // 2
            own = lax.dynamic_slice(blk, (2 * x + y, c * half, 0), (1, half, blk.shape[2]))
            parts.append(lax.dynamic_update_slice(land, own, (_device_index((x, y, c)), 0, 0)))
        mine[key] = [_sum_chips_block(p, name=f"rs_sum_{nm}") for p, nm in zip(parts, names[key], strict=True)]
    flat = [a for key in begun for a in mine[key]]
    other = iter(_swap_with_sibling(flat, name="rs_join"))
    return {key: [jnp.concatenate([jnp.where(c == 0, a, b), jnp.where(c == 0, b, a)], axis=0)
                  for a, b in ((a, next(other)) for a in mine[key])] for key in begun}


WEIGHTS = ("rel_bias", "att_w_in", "att_w_out", "dn_w_in", "dn_conv", "dn_a_log", "dn_dt_bias", "dn_out_norm", "dn_w_out",
           "mem_norm", "mem_w_kv", "norm_mix_pre", "norm_mix_post", "norm_ffn_pre", "norm_ffn_post", "ffn_w_gate_up",
           "ffn_w_down")
BIG_NAMES = tuple(n for n, _, _ in BIG)
SMALL_NAMES = tuple(n for n in WEIGHTS if n not in BIG_NAMES)
CONV_COLS = 3 * TOK_WIDTH
CONV_SHARD = CONV_COLS // N_CHIPS
BLOCKS = tuple((n, layer) for n, shape, _ in BIG for layer in range(shape[0]))
COLUMN_SHARDED = {n: axis == 2 for n, _, axis in BIG}


def kernel(x, mem, rel_bias, att_w_in, att_w_out, dn_w_in, dn_conv, dn_a_log, dn_dt_bias, dn_out_norm, dn_w_out, mem_norm, mem_w_kv, norm_mix_pre, norm_mix_post, norm_ffn_pre, norm_ffn_post, ffn_w_gate_up, ffn_w_down, loss_target, m_rel_bias, m_att_w_in, m_att_w_out, m_dn_w_in, m_dn_conv, m_dn_a_log, m_dn_dt_bias, m_dn_out_norm, m_dn_w_out, m_mem_norm, m_mem_w_kv, m_norm_mix_pre, m_norm_mix_post, m_norm_ffn_pre, m_norm_ffn_post, m_ffn_w_gate_up, m_ffn_w_down, v_rel_bias, v_att_w_in, v_att_w_out, v_dn_w_in, v_dn_conv, v_dn_a_log, v_dn_dt_bias, v_dn_out_norm, v_dn_w_out, v_mem_norm, v_mem_w_kv, v_norm_mix_pre, v_norm_mix_post, v_norm_ffn_pre, v_norm_ffn_post, v_ffn_w_gate_up, v_ffn_w_down):
    w = dict(zip(WEIGHTS, (rel_bias, att_w_in, att_w_out, dn_w_in, dn_conv, dn_a_log, dn_dt_bias, dn_out_norm, dn_w_out,
                           mem_norm, mem_w_kv, norm_mix_pre, norm_mix_post, norm_ffn_pre, norm_ffn_post, ffn_w_gate_up,
                           ffn_w_down)))
    m = dict(zip(WEIGHTS, (m_rel_bias, m_att_w_in, m_att_w_out, m_dn_w_in, m_dn_conv, m_dn_a_log, m_dn_dt_bias,
                           m_dn_out_norm, m_dn_w_out, m_mem_norm, m_mem_w_kv, m_norm_mix_pre, m_norm_mix_post,
                           m_norm_ffn_pre, m_norm_ffn_post, m_ffn_w_gate_up, m_ffn_w_down)))
    v = dict(zip(WEIGHTS, (v_rel_bias, v_att_w_in, v_att_w_out, v_dn_w_in, v_dn_conv, v_dn_a_log, v_dn_dt_bias,
                           v_dn_out_norm, v_dn_w_out, v_mem_norm, v_mem_w_kv, v_norm_mix_pre, v_norm_mix_post,
                           v_norm_ffn_pre, v_norm_ffn_post, v_ffn_w_gate_up, v_ffn_w_down)))
    cx, cy, cc = _mesh_pos()
    chip = 2 * cx + cy

    local = dict(zip(BLOCKS, lax.optimization_barrier(
        [(w[n][layer].T if n == "dn_w_in" else w[n][layer]).astype(MXU_DTYPE) for n, layer in BLOCKS]), strict=True))

    def usable(block, got):
        got = lax.dynamic_update_slice(got, local[block][None], (chip, 0, 0))
        if block[0] == "dn_conv":
            return jnp.concatenate([got[s] for s in range(N_CHIPS)], axis=1)
        return got if COLUMN_SHARDED[block[0]] else got.reshape(-1, got.shape[-1])

    late = {"att_rest": [("att_w_out", 0), ("mem_w_kv", 0), ("mem_w_kv", 1)],
            "ffn0": [("ffn_w_gate_up", 0), ("ffn_w_down", 0)], "dn": [("dn_w_in", 0), ("dn_w_out", 0), ("dn_conv", 0)],
            "ffn1": [("ffn_w_gate_up", 1), ("ffn_w_down", 1)]}
    local[("dn_conv", 0)] = w["dn_conv"][0]
    first = [b for b in BLOCKS if all(b not in blks for blks in late.values())]
    first_got = _gather_blocks([local[b] for b in first], name="gather_weights")
    late_local, _ = lax.optimization_barrier(({k: [local[b] for b in blks] for k, blks in late.items()}, first_got[0]))
    started = {k: _gather_start(late_local[k], name=f"gather_start_{k}") for k in late}
    started_token = sum(s[4][0, 0] for s in started.values())

    def late_weights(key, after):
        lands = _gather_wait(started[key], after, name=f"gather_wait_{key}")
        return [usable(b, got) for b, got in zip(late[key], lands, strict=True)]

    def dn_weights(after):
        w_in, w_out, conv = late_weights("dn", after)
        return _pad_dn_w_in(jnp.concatenate([w_in[s].T for s in range(N_CHIPS)], axis=1)), w_out, conv

    full = {}
    for b, got in zip(first, first_got, strict=True):
        full.setdefault(b[0], []).append(usable(b, got))
    p = {
        "rel_bias": w["rel_bias"], "att_w_in": full["att_w_in"][0], "att_rest_weights": lambda after: late_weights("att_rest", after),
        "dn_a_log": w["dn_a_log"][0], "dn_dt_bias": w["dn_dt_bias"][0],
        "dn_out_norm": w["dn_out_norm"][0], "mem_norm": w["mem_norm"],
        "norm_mix_pre": w["norm_mix_pre"] + started_token,
        "norm_mix_post": w["norm_mix_post"], "norm_ffn_pre": w["norm_ffn_pre"], "norm_ffn_post": w["norm_ffn_post"],
        "ffn_weights": lambda layer, after: late_weights(f"ffn{layer}", after), "dn_weights": dn_weights,
    }

    def chip_blocks(n, a):
        if n == "dn_w_in":
            a = _unpad_dn_w_in(a)
        if a.ndim == 3:
            return a
        if COLUMN_SHARDED[n]:
            return a.reshape(a.shape[0], N_CHIPS, -1).transpose(1, 0, 2)
        return a.reshape(N_CHIPS, -1, a.shape[-1])

    begun, begun_blocks = {}, {}

    def grads_ready(key, layer_grads):
        begun_blocks[key] = list(layer_grads)
        begun[key] = _scatter_start([chip_blocks(n, a) for (n, _), a in layer_grads.items()], name=f"rs_scatter_start_{key}")
        return begun[key][4][0, 0]

    p["grads_ready"] = grads_ready
    loss_cols, grad_x, g = _local_step(x[0], mem[0], loss_target[0], p)
    finished = _reduce_finish(begun, {k: [f"{n}{layer}" for n, layer in blks] for k, blks in begun_blocks.items()}, grad_x)
    reduced = {b: r for k in begun for b, r in zip(begun_blocks[k], finished[k], strict=True)}
    grads = {n: jnp.concatenate([reduced[b] for b in BLOCKS if b[0] == n], axis=0).reshape(shape) for n, shape, _ in BIG}
    small_full_shapes = [(DN_CONV, CONV_COLS) if n == "dn_conv" else w[n].shape for n in SMALL_NAMES] + [(1,)]
    small_sum = _all_reduce_small(_pack_small([g[n] for n in SMALL_NAMES] + [jnp.sum(loss_cols).reshape(1)],
                                              _small_rows(small_full_shapes)), name="reduce_small")
    *small_grads, loss = _unpack_small(small_sum, small_full_shapes)
    loss = loss[0]
    for n, s in zip(SMALL_NAMES, small_grads, strict=True):
        grads[n] = lax.dynamic_slice(s, (0, chip * CONV_SHARD), (DN_CONV, CONV_SHARD))[None] if n == "dn_conv" else s

    delta, new_m, new_v = {}, {}, {}
    for n in BIG_NAMES:
        shape = w[n].shape
        two_d = (lambda a: a[0].T) if n == "dn_w_in" else (lambda a: a.reshape(-1, shape[-1]))
        back = (lambda a: a.T[None]) if n == "dn_w_in" else (lambda a: a.reshape(shape))
        g_2d = two_d(grads[n])
        res = _adamw(two_d(w[n]), g_2d, two_d(m[n]), two_d(v[n]), name=f"adamw_{n}")
        grads[n], delta[n], new_m[n], new_v[n] = (back(r) for r in (g_2d, *res))
    res = _adamw_many(*[[d[n].reshape(-1, d[n].shape[-1]) for n in SMALL_NAMES] for d in (w, grads, m, v)], name="adamw_small")
    for d, outs in zip((delta, new_m, new_v), res):
        for n, a in zip(SMALL_NAMES, outs):
            d[n] = a.reshape(w[n].shape)
    return (loss, grad_x[None], *[grads[n] for n in WEIGHTS], *[delta[n] for n in WEIGHTS],
            *[new_m[n] for n in WEIGHTS], *[new_v[n] for n in WEIGHTS])
```
